```python
import jax, jax.numpy as jnp
from jax import lax
import numpy as np

D_MODEL = 1024
BATCH = 8
SEQ = 2048
DEPTH = 1

CHUNK = 64
D_SHORT = D_MODEL
SHORT_CONV = 3
SSM_EXPAND = 2
D_INNER = SSM_EXPAND * D_MODEL
SSM_HEAD_DIM = 64
SSM_HEADS = D_INNER // SSM_HEAD_DIM
SSM_GROUPS = 8
D_STATE = 128
SSM_CONV = 4
D_XBC = D_INNER + 2 * SSM_GROUPS * D_STATE
D_FF = 2816
NORM_EPS = 1e-5

kernel_name = "hybrid_shortconv_ssd_macaron_block"


def rms_norm(x, w):
    xf = x.astype(jnp.float32)
    y = xf * lax.rsqrt(jnp.mean(xf * xf, axis=-1, keepdims=True) + NORM_EPS)
    return (y * w.astype(jnp.float32)).astype(x.dtype)


def gated_group_rms_norm(y, z, w, groups):
    yf = y.astype(jnp.float32) * jax.nn.silu(z.astype(jnp.float32))
    shp = yf.shape
    yg = yf.reshape(shp[:-1] + (groups, shp[-1] // groups))
    yg = yg * lax.rsqrt(jnp.mean(yg * yg, axis=-1, keepdims=True) + NORM_EPS)
    return (yg.reshape(shp) * w.astype(jnp.float32)).astype(y.dtype)


def swiglu(h, w_in, w_out):
    g, u = jnp.split(h @ w_in, 2, axis=-1)
    return (jax.nn.silu(g) * u) @ w_out


def causal_dwconv(x, w):
    k = w.shape[0]
    return lax.conv_general_dilated(
        x, w[:, None, :].astype(x.dtype), window_strides=(1,),
        padding=[(k - 1, 0)], dimension_numbers=("NWC", "WIO", "NWC"),
        feature_group_count=x.shape[-1])


def segsum(a):
    t = a.shape[-1]
    ae = jnp.broadcast_to(a[..., None], a.shape + (t,))
    ae = jnp.where(jnp.tril(jnp.ones((t, t), bool), -1), ae, 0.0)
    ss = jnp.cumsum(ae, axis=-2)
    return jnp.where(jnp.tril(jnp.ones((t, t), bool), 0), ss, -jnp.inf)


def ssd_scan(x, dt, a_coef, bm, cm):
    b, s, h, p = x.shape
    g, n = bm.shape[-2:]
    e = h // g
    c = s // CHUNK
    xdt = (x * dt[..., None]).reshape(b, c, CHUNK, g, e, p)
    bc = bm.reshape(b, c, CHUNK, g, n)
    cc = cm.reshape(b, c, CHUNK, g, n)
    a = (dt * a_coef).reshape(b, c, CHUNK, g, e).transpose(0, 3, 4, 1, 2)
    a_cs = jnp.cumsum(a, axis=-1)
    decay = jnp.exp(segsum(a))
    cb = jnp.einsum("bclgn,bcsgn->bgcls", cc, bc)
    y_diag = jnp.einsum("bgecls,bcsgep->bclgep", cb[:, :, None] * decay, xdt)
    decay_states = jnp.exp(a_cs[..., -1:] - a_cs)
    states = jnp.einsum("bclgn,bgecl,bclgep->bcgepn", bc, decay_states, xdt)
    states = jnp.concatenate([jnp.zeros_like(states[:, :1]), states], axis=1)
    chunk_decay = jnp.exp(segsum(jnp.pad(a_cs[..., -1], ((0, 0), (0, 0), (0, 0), (1, 0)))))
    states = jnp.einsum("bgezc,bcgepn->bzgepn", chunk_decay, states)[:, :-1]
    y_off = jnp.einsum("bclgn,bcgepn,bgecl->bclgep", cc, states, jnp.exp(a_cs))
    return (y_diag + y_off).reshape(b, s, h, p)


def _fwd_setup_inputs(seed: int = 0) -> dict:
    key = jax.random.key(seed)
    ks = jax.random.split(key, 24)
    L, D = DEPTH, D_MODEL
    n_in = 3 * D_SHORT + D_INNER + D_XBC + SSM_HEADS + 2 * D_MODEL

    def nrm(k, shape, fan_in):
        return jax.random.normal(k, shape, jnp.float32) * fan_in ** -0.5

    def gain(k, shape):
        return 1.0 + 0.02 * jax.random.normal(k, shape, jnp.float32)

    dt0 = jnp.exp(jax.random.uniform(ks[10], (L, SSM_HEADS), jnp.float32,
                                     np.log(1e-3), np.log(1e-1)))
    dt_bias = dt0 + jnp.log(-jnp.expm1(-dt0))
    a_log = jnp.log(jax.random.uniform(ks[11], (L, SSM_HEADS), jnp.float32, 1.0, 16.0))
    return {
        "x": jax.random.normal(ks[0], (BATCH, SEQ, D), jnp.float32),
        "ffn1_norm": gain(ks[1], (L, D)),
        "ffn1_w_in": nrm(ks[2], (L, D, 2 * D_FF), D),
        "ffn1_w_out": nrm(ks[3], (L, D_FF, D), D_FF),
        "mix_norm": gain(ks[4], (L, D)),
        "w_in": nrm(ks[5], (L, D, n_in), D),
        "short_conv_w": nrm(ks[6], (L, SHORT_CONV, D_SHORT), SHORT_CONV),
        "short_w_out": nrm(ks[7], (L, D_SHORT, D), D_SHORT),
        "ssm_conv_w": nrm(ks[8], (L, SSM_CONV, D_XBC), SSM_CONV),
        "ssm_conv_b": 0.02 * jax.random.normal(ks[9], (L, D_XBC), jnp.float32),
        "ssm_dt_bias": dt_bias,
        "ssm_A_log": a_log,
        "ssm_D": gain(ks[12], (L, SSM_HEADS)),
        "ssm_norm": gain(ks[13], (L, D_INNER)),
        "ssm_w_out": nrm(ks[14], (L, D_INNER, D), D_INNER),
        "w_out": nrm(ks[15], (L, D, D), D),
        "ffn2_norm": gain(ks[16], (L, D)),
        "ffn2_w_in": nrm(ks[17], (L, D, 2 * D_FF), D),
        "ffn2_w_out": nrm(ks[18], (L, D_FF, D), D_FF),
        "final_norm": gain(ks[19], (D,)),
    }


def _fwd_reference(x, ffn1_norm, ffn1_w_in, ffn1_w_out, mix_norm, w_in, short_conv_w,
              short_w_out, ssm_conv_w, ssm_conv_b, ssm_dt_bias, ssm_A_log, ssm_D,
              ssm_norm, ssm_w_out, w_out, ffn2_norm, ffn2_w_in, ffn2_w_out,
              final_norm):
    b, s, _ = x.shape
    sizes = [D_SHORT, D_SHORT, D_SHORT, D_INNER, D_XBC, SSM_HEADS, D_MODEL, D_MODEL]
    cuts = [int(v) for v in np.cumsum(sizes)[:-1]]
    for l in range(DEPTH):
        x = x + 0.5 * swiglu(rms_norm(x, ffn1_norm[l]), ffn1_w_in[l], ffn1_w_out[l])

        h = rms_norm(x, mix_norm[l])
        b_gate, c_gate, xa, z, xbc, dt_raw, ga, gb = jnp.split(h @ w_in[l], cuts, axis=-1)

        va = causal_dwconv(c_gate * xa, short_conv_w[l])
        y_a = (b_gate * va) @ short_w_out[l]

        xbc = jax.nn.silu(causal_dwconv(xbc, ssm_conv_w[l]) + ssm_conv_b[l])
        xs, bm, cm = jnp.split(xbc, [D_INNER, D_INNER + SSM_GROUPS * D_STATE], axis=-1)
        xs = xs.reshape(b, s, SSM_HEADS, SSM_HEAD_DIM).astype(jnp.float32)
        bm = bm.reshape(b, s, SSM_GROUPS, D_STATE).astype(jnp.float32)
        cm = cm.reshape(b, s, SSM_GROUPS, D_STATE).astype(jnp.float32)
        dt = jax.nn.softplus(dt_raw.astype(jnp.float32) + ssm_dt_bias[l].astype(jnp.float32))
        a_coef = -jnp.exp(ssm_A_log[l].astype(jnp.float32))
        y_ssm = ssd_scan(xs, dt, a_coef, bm, cm) + xs * ssm_D[l].astype(jnp.float32)[:, None]
        y_ssm = y_ssm.reshape(b, s, D_INNER).astype(x.dtype)
        y_b = gated_group_rms_norm(y_ssm, z, ssm_norm[l], SSM_GROUPS) @ ssm_w_out[l]

        merged = jax.nn.sigmoid(ga) * y_a + jax.nn.sigmoid(gb) * y_b
        x = x + merged @ w_out[l]

        x = x + 0.5 * swiglu(rms_norm(x, ffn2_norm[l]), ffn2_w_in[l], ffn2_w_out[l])
    return rms_norm(x, final_norm)


import jax as _jax
import jax.numpy as _jnp

TWIN_FORMAT = 'train_step'
FWD_PARAMS = ['x', 'ffn1_norm', 'ffn1_w_in', 'ffn1_w_out', 'mix_norm', 'w_in', 'short_conv_w', 'short_w_out', 'ssm_conv_w', 'ssm_conv_b', 'ssm_dt_bias', 'ssm_A_log', 'ssm_D', 'ssm_norm', 'ssm_w_out', 'w_out', 'ffn2_norm', 'ffn2_w_in', 'ffn2_w_out', 'final_norm']
TWIN_WEIGHTS = ['ffn1_norm', 'ffn1_w_in', 'ffn1_w_out', 'mix_norm', 'w_in', 'short_conv_w', 'short_w_out', 'ssm_conv_w', 'ssm_conv_b', 'ssm_dt_bias', 'ssm_A_log', 'ssm_D', 'ssm_norm', 'ssm_w_out', 'w_out', 'ffn2_norm', 'ffn2_w_in', 'ffn2_w_out', 'final_norm']
TWIN_DIFF_INPUT = 'x'
TWIN_INPUTS = ['x', 'ffn1_norm', 'ffn1_w_in', 'ffn1_w_out', 'mix_norm', 'w_in', 'short_conv_w', 'short_w_out', 'ssm_conv_w', 'ssm_conv_b', 'ssm_dt_bias', 'ssm_A_log', 'ssm_D', 'ssm_norm', 'ssm_w_out', 'w_out', 'ffn2_norm', 'ffn2_w_in', 'ffn2_w_out', 'final_norm', 'loss_target', 'm_ffn1_norm', 'm_ffn1_w_in', 'm_ffn1_w_out', 'm_mix_norm', 'm_w_in', 'm_short_conv_w', 'm_short_w_out', 'm_ssm_conv_w', 'm_ssm_conv_b', 'm_ssm_dt_bias', 'm_ssm_A_log', 'm_ssm_D', 'm_ssm_norm', 'm_ssm_w_out', 'm_w_out', 'm_ffn2_norm', 'm_ffn2_w_in', 'm_ffn2_w_out', 'm_final_norm', 'v_ffn1_norm', 'v_ffn1_w_in', 'v_ffn1_w_out', 'v_mix_norm', 'v_w_in', 'v_short_conv_w', 'v_short_w_out', 'v_ssm_conv_w', 'v_ssm_conv_b', 'v_ssm_dt_bias', 'v_ssm_A_log', 'v_ssm_D', 'v_ssm_norm', 'v_ssm_w_out', 'v_w_out', 'v_ffn2_norm', 'v_ffn2_w_in', 'v_ffn2_w_out', 'v_final_norm']
TWIN_OUTPUTS = ['loss', 'grad_x', 'grad_ffn1_norm', 'grad_ffn1_w_in', 'grad_ffn1_w_out', 'grad_mix_norm', 'grad_w_in', 'grad_short_conv_w', 'grad_short_w_out', 'grad_ssm_conv_w', 'grad_ssm_conv_b', 'grad_ssm_dt_bias', 'grad_ssm_A_log', 'grad_ssm_D', 'grad_ssm_norm', 'grad_ssm_w_out', 'grad_w_out', 'grad_ffn2_norm', 'grad_ffn2_w_in', 'grad_ffn2_w_out', 'grad_final_norm', 'delta_ffn1_norm', 'delta_ffn1_w_in', 'delta_ffn1_w_out', 'delta_mix_norm', 'delta_w_in', 'delta_short_conv_w', 'delta_short_w_out', 'delta_ssm_conv_w', 'delta_ssm_conv_b', 'delta_ssm_dt_bias', 'delta_ssm_A_log', 'delta_ssm_D', 'delta_ssm_norm', 'delta_ssm_w_out', 'delta_w_out', 'delta_ffn2_norm', 'delta_ffn2_w_in', 'delta_ffn2_w_out', 'delta_final_norm', 'new_m_ffn1_norm', 'new_m_ffn1_w_in', 'new_m_ffn1_w_out', 'new_m_mix_norm', 'new_m_w_in', 'new_m_short_conv_w', 'new_m_short_w_out', 'new_m_ssm_conv_w', 'new_m_ssm_conv_b', 'new_m_ssm_dt_bias', 'new_m_ssm_A_log', 'new_m_ssm_D', 'new_m_ssm_norm', 'new_m_ssm_w_out', 'new_m_w_out', 'new_m_ffn2_norm', 'new_m_ffn2_w_in', 'new_m_ffn2_w_out', 'new_m_final_norm', 'new_v_ffn1_norm', 'new_v_ffn1_w_in', 'new_v_ffn1_w_out', 'new_v_mix_norm', 'new_v_w_in', 'new_v_short_conv_w', 'new_v_short_w_out', 'new_v_ssm_conv_w', 'new_v_ssm_conv_b', 'new_v_ssm_dt_bias', 'new_v_ssm_A_log', 'new_v_ssm_D', 'new_v_ssm_norm', 'new_v_ssm_w_out', 'new_v_w_out', 'new_v_ffn2_norm', 'new_v_ffn2_w_in', 'new_v_ffn2_w_out', 'new_v_final_norm']
TWIN_LEAF_KINDS = {'loss': 'loss', 'grad_x': 'grad_x', 'grad_ffn1_norm': 'grad_w', 'grad_ffn1_w_in': 'grad_w', 'grad_ffn1_w_out': 'grad_w', 'grad_mix_norm': 'grad_w', 'grad_w_in': 'grad_w', 'grad_short_conv_w': 'grad_w', 'grad_short_w_out': 'grad_w', 'grad_ssm_conv_w': 'grad_w', 'grad_ssm_conv_b': 'grad_w', 'grad_ssm_dt_bias': 'grad_w', 'grad_ssm_A_log': 'grad_w', 'grad_ssm_D': 'grad_w', 'grad_ssm_norm': 'grad_w', 'grad_ssm_w_out': 'grad_w', 'grad_w_out': 'grad_w', 'grad_ffn2_norm': 'grad_w', 'grad_ffn2_w_in': 'grad_w', 'grad_ffn2_w_out': 'grad_w', 'grad_final_norm': 'grad_w', 'delta_ffn1_norm': 'delta_w', 'delta_ffn1_w_in': 'delta_w', 'delta_ffn1_w_out': 'delta_w', 'delta_mix_norm': 'delta_w', 'delta_w_in': 'delta_w', 'delta_short_conv_w': 'delta_w', 'delta_short_w_out': 'delta_w', 'delta_ssm_conv_w': 'delta_w', 'delta_ssm_conv_b': 'delta_w', 'delta_ssm_dt_bias': 'delta_w', 'delta_ssm_A_log': 'delta_w', 'delta_ssm_D': 'delta_w', 'delta_ssm_norm': 'delta_w', 'delta_ssm_w_out': 'delta_w', 'delta_w_out': 'delta_w', 'delta_ffn2_norm': 'delta_w', 'delta_ffn2_w_in': 'delta_w', 'delta_ffn2_w_out': 'delta_w', 'delta_final_norm': 'delta_w', 'new_m_ffn1_norm': 'new_m', 'new_m_ffn1_w_in': 'new_m', 'new_m_ffn1_w_out': 'new_m', 'new_m_mix_norm': 'new_m', 'new_m_w_in': 'new_m', 'new_m_short_conv_w': 'new_m', 'new_m_short_w_out': 'new_m', 'new_m_ssm_conv_w': 'new_m', 'new_m_ssm_conv_b': 'new_m', 'new_m_ssm_dt_bias': 'new_m', 'new_m_ssm_A_log': 'new_m', 'new_m_ssm_D': 'new_m', 'new_m_ssm_norm': 'new_m', 'new_m_ssm_w_out': 'new_m', 'new_m_w_out': 'new_m', 'new_m_ffn2_norm': 'new_m', 'new_m_ffn2_w_in': 'new_m', 'new_m_ffn2_w_out': 'new_m', 'new_m_final_norm': 'new_m', 'new_v_ffn1_norm': 'new_v', 'new_v_ffn1_w_in': 'new_v', 'new_v_ffn1_w_out': 'new_v', 'new_v_mix_norm': 'new_v', 'new_v_w_in': 'new_v', 'new_v_short_conv_w': 'new_v', 'new_v_short_w_out': 'new_v', 'new_v_ssm_conv_w': 'new_v', 'new_v_ssm_conv_b': 'new_v', 'new_v_ssm_dt_bias': 'new_v', 'new_v_ssm_A_log': 'new_v', 'new_v_ssm_D': 'new_v', 'new_v_ssm_norm': 'new_v', 'new_v_ssm_w_out': 'new_v', 'new_v_w_out': 'new_v', 'new_v_ffn2_norm': 'new_v', 'new_v_ffn2_w_in': 'new_v', 'new_v_ffn2_w_out': 'new_v', 'new_v_final_norm': 'new_v'}


def _forward(args):
    return _fwd_reference(*[args[k] for k in FWD_PARAMS])


def _output_shape():
    out = _jax.eval_shape(lambda: _forward(_fwd_setup_inputs(0)))
    return out.shape, out.dtype

N_MICROBATCH = 1
ADAM_LR = 0.001
ADAM_B1 = 0.9
ADAM_B2 = 0.999
ADAM_EPS = 1e-08
ADAM_WD = 0.01
ADAM_STEP = 10
PER_EXAMPLE_BATCH_AXIS = {'x': 0, 'loss_target': 0}
SHARED_INPUTS = []
_WEIGHT_DTYPES = {'ffn1_norm': _jnp.float32, 'ffn1_w_in': _jnp.float32, 'ffn1_w_out': _jnp.float32, 'mix_norm': _jnp.float32, 'w_in': _jnp.float32, 'short_conv_w': _jnp.float32, 'short_w_out': _jnp.float32, 'ssm_conv_w': _jnp.float32, 'ssm_conv_b': _jnp.float32, 'ssm_dt_bias': _jnp.float32, 'ssm_A_log': _jnp.float32, 'ssm_D': _jnp.float32, 'ssm_norm': _jnp.float32, 'ssm_w_out': _jnp.float32, 'w_out': _jnp.float32, 'ffn2_norm': _jnp.float32, 'ffn2_w_in': _jnp.float32, 'ffn2_w_out': _jnp.float32, 'final_norm': _jnp.float32}
MOMENT_SCALE = {'ffn1_norm': 6.824988e-02, 'ffn1_w_in': 2.869605e-02, 'ffn1_w_out': 4.683394e-02, 'mix_norm': 1.323411e-01, 'w_in': 3.816703e-02, 'short_conv_w': 5.496568e-02, 'short_w_out': 5.398638e-02, 'ssm_conv_w': 2.929445e-02, 'ssm_conv_b': 3.991612e-02, 'ssm_dt_bias': 7.706635e-02, 'ssm_A_log': 7.974507e-02, 'ssm_D': 1.645799e-01, 'ssm_norm': 3.830197e-02, 'ssm_w_out': 5.403953e-02, 'w_out': 7.613610e-02, 'ffn2_norm': 4.094109e-02, 'ffn2_w_in': 1.729029e-02, 'ffn2_w_out': 2.813046e-02, 'final_norm': 1.598979e+01}


def _to_microbatches(a, axis):
    t = _jnp.moveaxis(a, axis, 0)
    t = t.reshape((N_MICROBATCH, t.shape[0] // N_MICROBATCH) + t.shape[1:])
    return _jnp.moveaxis(t, 1, axis + 1)


def setup_inputs(seed: int = 0) -> dict:
    inp = _fwd_setup_inputs(seed)
    key = _jax.random.fold_in(_jax.random.key(seed), 7919)
    shape, _ = _output_shape()
    out = dict(inp)
    out["loss_target"] = _jax.random.normal(_jax.random.fold_in(key, 0), shape, _jnp.float32)
    for i, name in enumerate(TWIN_WEIGHTS):
        w = inp[name].astype(_jnp.float32)
        if MOMENT_SCALE is None:
            s = _jnp.sqrt(_jnp.mean(_jnp.square(w)) + 1e-30)
        else:
            s = MOMENT_SCALE[name]
        km, kv = _jax.random.split(_jax.random.fold_in(key, i + 1))
        out[name] = w
        out["m_" + name] = s * _jax.random.normal(km, w.shape, _jnp.float32)
        out["v_" + name] = (s * s) * _jax.random.uniform(kv, w.shape, _jnp.float32, 0.5, 1.5)
    if N_MICROBATCH > 1:
        for name, axis in PER_EXAMPLE_BATCH_AXIS.items():
            out[name] = _to_microbatches(out[name], axis)
    return {'x': out['x'], 'ffn1_norm': out['ffn1_norm'], 'ffn1_w_in': out['ffn1_w_in'], 'ffn1_w_out': out['ffn1_w_out'], 'mix_norm': out['mix_norm'], 'w_in': out['w_in'], 'short_conv_w': out['short_conv_w'], 'short_w_out': out['short_w_out'], 'ssm_conv_w': out['ssm_conv_w'], 'ssm_conv_b': out['ssm_conv_b'], 'ssm_dt_bias': out['ssm_dt_bias'], 'ssm_A_log': out['ssm_A_log'], 'ssm_D': out['ssm_D'], 'ssm_norm': out['ssm_norm'], 'ssm_w_out': out['ssm_w_out'], 'w_out': out['w_out'], 'ffn2_norm': out['ffn2_norm'], 'ffn2_w_in': out['ffn2_w_in'], 'ffn2_w_out': out['ffn2_w_out'], 'final_norm': out['final_norm'], 'loss_target': out['loss_target'], 'm_ffn1_norm': out['m_ffn1_norm'], 'm_ffn1_w_in': out['m_ffn1_w_in'], 'm_ffn1_w_out': out['m_ffn1_w_out'], 'm_mix_norm': out['m_mix_norm'], 'm_w_in': out['m_w_in'], 'm_short_conv_w': out['m_short_conv_w'], 'm_short_w_out': out['m_short_w_out'], 'm_ssm_conv_w': out['m_ssm_conv_w'], 'm_ssm_conv_b': out['m_ssm_conv_b'], 'm_ssm_dt_bias': out['m_ssm_dt_bias'], 'm_ssm_A_log': out['m_ssm_A_log'], 'm_ssm_D': out['m_ssm_D'], 'm_ssm_norm': out['m_ssm_norm'], 'm_ssm_w_out': out['m_ssm_w_out'], 'm_w_out': out['m_w_out'], 'm_ffn2_norm': out['m_ffn2_norm'], 'm_ffn2_w_in': out['m_ffn2_w_in'], 'm_ffn2_w_out': out['m_ffn2_w_out'], 'm_final_norm': out['m_final_norm'], 'v_ffn1_norm': out['v_ffn1_norm'], 'v_ffn1_w_in': out['v_ffn1_w_in'], 'v_ffn1_w_out': out['v_ffn1_w_out'], 'v_mix_norm': out['v_mix_norm'], 'v_w_in': out['v_w_in'], 'v_short_conv_w': out['v_short_conv_w'], 'v_short_w_out': out['v_short_w_out'], 'v_ssm_conv_w': out['v_ssm_conv_w'], 'v_ssm_conv_b': out['v_ssm_conv_b'], 'v_ssm_dt_bias': out['v_ssm_dt_bias'], 'v_ssm_A_log': out['v_ssm_A_log'], 'v_ssm_D': out['v_ssm_D'], 'v_ssm_norm': out['v_ssm_norm'], 'v_ssm_w_out': out['v_ssm_w_out'], 'v_w_out': out['v_w_out'], 'v_ffn2_norm': out['v_ffn2_norm'], 'v_ffn2_w_in': out['v_ffn2_w_in'], 'v_ffn2_w_out': out['v_ffn2_w_out'], 'v_final_norm': out['v_final_norm']}


def _loss(weights, diff, rest, loss_target):
    with _jax.named_scope("forward"):
        args = {**rest, TWIN_DIFF_INPUT: diff, **{k: w.astype(_WEIGHT_DTYPES[k]) for k, w in weights.items()}}
        y = _forward(args)
    with _jax.named_scope("loss_head"):
        err = _jnp.square(y.astype(_jnp.float32) - loss_target)
        return 0.5 * _jnp.sum(_jnp.mean(err, axis=-1)) if err.ndim else 0.5 * err


def _adamw(w, g, m, v):
    m = ADAM_B1 * m + (1.0 - ADAM_B1) * g
    v = ADAM_B2 * v + (1.0 - ADAM_B2) * _jnp.square(g)
    m_hat = m / (1.0 - ADAM_B1 ** ADAM_STEP)
    v_hat = v / (1.0 - ADAM_B2 ** ADAM_STEP)
    delta = -ADAM_LR * (m_hat / (_jnp.sqrt(v_hat) + ADAM_EPS) + ADAM_WD * w)
    return delta, m, v


def reference(x, ffn1_norm, ffn1_w_in, ffn1_w_out, mix_norm, w_in, short_conv_w, short_w_out, ssm_conv_w, ssm_conv_b, ssm_dt_bias, ssm_A_log, ssm_D, ssm_norm, ssm_w_out, w_out, ffn2_norm, ffn2_w_in, ffn2_w_out, final_norm, loss_target, m_ffn1_norm, m_ffn1_w_in, m_ffn1_w_out, m_mix_norm, m_w_in, m_short_conv_w, m_short_w_out, m_ssm_conv_w, m_ssm_conv_b, m_ssm_dt_bias, m_ssm_A_log, m_ssm_D, m_ssm_norm, m_ssm_w_out, m_w_out, m_ffn2_norm, m_ffn2_w_in, m_ffn2_w_out, m_final_norm, v_ffn1_norm, v_ffn1_w_in, v_ffn1_w_out, v_mix_norm, v_w_in, v_short_conv_w, v_short_w_out, v_ssm_conv_w, v_ssm_conv_b, v_ssm_dt_bias, v_ssm_A_log, v_ssm_D, v_ssm_norm, v_ssm_w_out, v_w_out, v_ffn2_norm, v_ffn2_w_in, v_ffn2_w_out, v_final_norm):
    given = dict(x=x, ffn1_norm=ffn1_norm, ffn1_w_in=ffn1_w_in, ffn1_w_out=ffn1_w_out, mix_norm=mix_norm, w_in=w_in, short_conv_w=short_conv_w, short_w_out=short_w_out, ssm_conv_w=ssm_conv_w, ssm_conv_b=ssm_conv_b, ssm_dt_bias=ssm_dt_bias, ssm_A_log=ssm_A_log, ssm_D=ssm_D, ssm_norm=ssm_norm, ssm_w_out=ssm_w_out, w_out=w_out, ffn2_norm=ffn2_norm, ffn2_w_in=ffn2_w_in, ffn2_w_out=ffn2_w_out, final_norm=final_norm, loss_target=loss_target, m_ffn1_norm=m_ffn1_norm, m_ffn1_w_in=m_ffn1_w_in, m_ffn1_w_out=m_ffn1_w_out, m_mix_norm=m_mix_norm, m_w_in=m_w_in, m_short_conv_w=m_short_conv_w, m_short_w_out=m_short_w_out, m_ssm_conv_w=m_ssm_conv_w, m_ssm_conv_b=m_ssm_conv_b, m_ssm_dt_bias=m_ssm_dt_bias, m_ssm_A_log=m_ssm_A_log, m_ssm_D=m_ssm_D, m_ssm_norm=m_ssm_norm, m_ssm_w_out=m_ssm_w_out, m_w_out=m_w_out, m_ffn2_norm=m_ffn2_norm, m_ffn2_w_in=m_ffn2_w_in, m_ffn2_w_out=m_ffn2_w_out, m_final_norm=m_final_norm, v_ffn1_norm=v_ffn1_norm, v_ffn1_w_in=v_ffn1_w_in, v_ffn1_w_out=v_ffn1_w_out, v_mix_norm=v_mix_norm, v_w_in=v_w_in, v_short_conv_w=v_short_conv_w, v_short_w_out=v_short_w_out, v_ssm_conv_w=v_ssm_conv_w, v_ssm_conv_b=v_ssm_conv_b, v_ssm_dt_bias=v_ssm_dt_bias, v_ssm_A_log=v_ssm_A_log, v_ssm_D=v_ssm_D, v_ssm_norm=v_ssm_norm, v_ssm_w_out=v_ssm_w_out, v_w_out=v_w_out, v_ffn2_norm=v_ffn2_norm, v_ffn2_w_in=v_ffn2_w_in, v_ffn2_w_out=v_ffn2_w_out, v_final_norm=v_final_norm)
    weights = {n: given[n] for n in TWIN_WEIGHTS}
    shared = {n: given[n] for n in SHARED_INPUTS}
    per_example = {n: given[n] for n in ['x']}
    grad_fn = _jax.value_and_grad(_loss, argnums=(0, 1))

    def one_microbatch(ex, loss_target):
        ex = dict(ex)
        diff = ex.pop(TWIN_DIFF_INPUT)
        return grad_fn(weights, diff, {**shared, **ex}, loss_target)

    if N_MICROBATCH == 1:
        loss, (grad_w, grad_x) = one_microbatch(per_example, given["loss_target"])
    else:
        def body(carry, xs):
            loss_sum, grad_sum = carry
            l_k, (gw_k, gx_k) = one_microbatch(xs[0], xs[1])
            with _jax.named_scope("update"):
                return (loss_sum + l_k, _jax.tree.map(_jnp.add, grad_sum, gw_k)), gx_k

        init = (_jnp.zeros((), _jnp.float32), _jax.tree.map(_jnp.zeros_like, weights))
        (loss, grad_w), grad_x = _jax.lax.scan(body, init, (per_example, given["loss_target"]))
    with _jax.named_scope("update"):
        delta_w, new_m, new_v = {}, {}, {}
        for n in TWIN_WEIGHTS:
            delta_w[n], new_m[n], new_v[n] = _adamw(weights[n], grad_w[n], given["m_" + n], given["v_" + n])
    return (loss, grad_x, *[grad_w[n] for n in TWIN_WEIGHTS], *[delta_w[n] for n in TWIN_WEIGHTS],
            *[new_m[n] for n in TWIN_WEIGHTS], *[new_v[n] for n in TWIN_WEIGHTS])
```

```python
import functools

import jax
import jax.numpy as jnp
from jax import lax
from jax.experimental import pallas as pl
from jax.experimental.pallas import tpu as pltpu

F32 = jnp.float32
BF = jnp.bfloat16

N_DEV = 8
D_MODEL = 1024
D_FF = 2816
D_INNER = 2048
D_XBC = 4096
N_HEADS = 32
HEAD_DIM = 64
N_GROUPS = 8
D_STATE = 128
CHUNK = 64
GROUP_W = D_INNER // N_GROUPS
HEADS_PER_GROUP = N_HEADS // N_GROUPS
NORM_EPS = 1e-5
N_IN = 11296
FF_SHARD = 2 * D_FF // N_DEV
IN_SHARD = N_IN // N_DEV

OFF_B, OFF_C, OFF_XA, OFF_Z, OFF_XBC, OFF_GA, OFF_GB, OFF_DT = 0, 1024, 2048, 3072, 5120, 9216, 10240, 11264
NP_COLS = 11520
P_BLOCK = 1280
DT_W = 128

ADAM_LR, ADAM_B1, ADAM_B2, ADAM_EPS, ADAM_WD, ADAM_STEP = 0.001, 0.9, 0.999, 1e-08, 0.01, 10

VMEM_LIMIT_V7X = 56 * 1024 * 1024
TM = 512
TE = 256


def _params(*sem):
    return pltpu.CompilerParams(dimension_semantics=sem, vmem_limit_bytes=VMEM_LIMIT_V7X)


_DIMS = {
    "nn": (((1,), (0,)), ((), ())),
    "nt": (((1,), (1,)), ((), ())),
    "tn": (((0,), (0,)), ((), ())),
}


def _dot(a, b, mode="nn"):
    return lax.dot_general(a, b, _DIMS[mode], preferred_element_type=F32)


def _sigmoid(x):
    return 1.0 / (1.0 + jnp.exp(-x))


def _matmul(name, mode, a, b, grid, a_spec, b_spec, o_spec, out_shape, acc_shape,
            res=None, res_spec=None, alpha=1.0):
    nk = grid[-1]
    has_res = res is not None

    def body(*refs):
        if has_res:
            a_ref, b_ref, r_ref, o_ref = refs[:4]
        else:
            a_ref, b_ref, o_ref = refs[:3]
            r_ref = None
        part = _dot(a_ref[...], b_ref[...], mode)

        def finish(v):
            if alpha != 1.0:
                v = v * alpha
            if has_res:
                v = r_ref[...] + v
            o_ref[...] = v.astype(o_ref.dtype)

        if nk == 1:
            finish(part)
        else:
            acc = refs[-1]
            k = pl.program_id(len(grid) - 1)

            @pl.when(k == 0)
            def _():
                acc[...] = part

            @pl.when(k > 0)
            def _():
                acc[...] += part

            @pl.when(k == nk - 1)
            def _():
                finish(acc[...])

    in_specs = [a_spec, b_spec] + ([res_spec] if has_res else [])
    args = (a, b) + ((res,) if has_res else ())
    scratch = [] if nk == 1 else [pltpu.VMEM(acc_shape, F32)]
    sem = ("parallel",) * (len(grid) - 1) + ("arbitrary",)
    return pl.pallas_call(
        body, name=name, grid=grid, in_specs=in_specs, out_specs=o_spec, out_shape=out_shape,
        scratch_shapes=scratch, compiler_params=_params(*sem))(*args)


def _mm_nn(name, a, b, out_dtype=F32, res=None, alpha=1.0, tk=None):
    t, kk = a.shape
    n = b.shape[1]
    tk = kk if tk is None else tk
    grid = (t // TM, 1, kk // tk)
    return _matmul(
        name, "nn", a, b, grid,
        pl.BlockSpec((TM, tk), lambda i, j, k: (i, k)),
        pl.BlockSpec((tk, n), lambda i, j, k: (k, 0)),
        pl.BlockSpec((TM, n), lambda i, j, k: (i, 0)),
        jax.ShapeDtypeStruct((t, n), out_dtype), (TM, n),
        res=res, res_spec=pl.BlockSpec((TM, n), lambda i, j, k: (i, 0)), alpha=alpha)


def _mm_nt(name, a, b, out_dtype=F32, tk=None):
    t, kk = a.shape
    n = b.shape[0]
    tk = kk if tk is None else tk
    grid = (t // TM, 1, kk // tk)
    return _matmul(
        name, "nt", a, b, grid,
        pl.BlockSpec((TM, tk), lambda i, j, k: (i, k)),
        pl.BlockSpec((n, tk), lambda i, j, k: (0, k)),
        pl.BlockSpec((TM, n), lambda i, j, k: (i, 0)),
        jax.ShapeDtypeStruct((t, n), out_dtype), (TM, n))


def _mm_tn(name, a, b, out_dtype, tn=None):
    t, m = a.shape
    n = b.shape[1]
    tn = n if tn is None else tn
    grid = (n // tn, 1, t // TM)
    return _matmul(
        name, "tn", a, b, grid,
        pl.BlockSpec((TM, m), lambda j, i, k: (k, 0)),
        pl.BlockSpec((TM, tn), lambda j, i, k: (k, j)),
        pl.BlockSpec((m, tn), lambda j, i, k: (0, j)),
        jax.ShapeDtypeStruct((m, n), out_dtype), (m, tn))


def _ffn_in(name, h, w_in):
    t = h.shape[0]
    grid = (N_DEV, t // TM, 1)
    return _matmul(
        name, "nn", h, w_in, grid,
        pl.BlockSpec((TM, D_MODEL), lambda d, i, k: (i, 0)),
        pl.BlockSpec((None, D_MODEL, FF_SHARD), lambda d, i, k: (d, 0, 0)),
        pl.BlockSpec((None, TM, FF_SHARD), lambda d, i, k: (d, i, 0)),
        jax.ShapeDtypeStruct((N_DEV, t, FF_SHARD), F32), None)


def _ffn_out(name, act, w_out, x):
    t = x.shape[0]
    grid = (t // TM, 1, 4)
    return _matmul(
        name, "nn", act, w_out, grid,
        pl.BlockSpec((None, TM, FF_SHARD), lambda i, j, k: (k, i, 0)),
        pl.BlockSpec((None, FF_SHARD, D_MODEL), lambda i, j, k: (k, 0, 0)),
        pl.BlockSpec((TM, D_MODEL), lambda i, j, k: (i, 0)),
        jax.ShapeDtypeStruct((t, D_MODEL), F32), (TM, D_MODEL),
        res=x, res_spec=pl.BlockSpec((TM, D_MODEL), lambda i, j, k: (i, 0)), alpha=0.5)


def _ffn_out_bwd_act(name, dy, w_out):
    t = dy.shape[0]
    grid = (4, t // TM, 1)
    return _matmul(
        name, "nt", dy, w_out, grid,
        pl.BlockSpec((TM, D_MODEL), lambda d, i, k: (i, 0)),
        pl.BlockSpec((None, FF_SHARD, D_MODEL), lambda d, i, k: (d, 0, 0)),
        pl.BlockSpec((None, TM, FF_SHARD), lambda d, i, k: (d, i, 0)),
        jax.ShapeDtypeStruct((4, t, FF_SHARD), F32), None)


def _ffn_out_bwd_w(name, act, dy, out_dtype):
    t = dy.shape[0]
    grid = (4, 1, t // TM)
    return _matmul(
        name, "tn", act, dy, grid,
        pl.BlockSpec((None, TM, FF_SHARD), lambda d, j, k: (d, k, 0)),
        pl.BlockSpec((TM, D_MODEL), lambda d, j, k: (k, 0)),
        pl.BlockSpec((None, FF_SHARD, D_MODEL), lambda d, j, k: (d, 0, 0)),
        jax.ShapeDtypeStruct((4, FF_SHARD, D_MODEL), out_dtype), (FF_SHARD, D_MODEL))


def _ffn_in_bwd_h(name, dgu, w_in):
    t = dgu.shape[1]
    grid = (t // TM, 1, N_DEV)
    return _matmul(
        name, "nt", dgu, w_in, grid,
        pl.BlockSpec((None, TM, FF_SHARD), lambda i, j, k: (k, i, 0)),
        pl.BlockSpec((None, D_MODEL, FF_SHARD), lambda i, j, k: (k, 0, 0)),
        pl.BlockSpec((TM, D_MODEL), lambda i, j, k: (i, 0)),
        jax.ShapeDtypeStruct((t, D_MODEL), F32), (TM, D_MODEL))


def _ffn_in_bwd_w(name, h, dgu, out_dtype):
    t = h.shape[0]
    grid = (N_DEV, 1, t // TM)
    return _matmul(
        name, "tn", h, dgu, grid,
        pl.BlockSpec((TM, D_MODEL), lambda d, j, k: (k, 0)),
        pl.BlockSpec((None, TM, FF_SHARD), lambda d, j, k: (d, k, 0)),
        pl.BlockSpec((None, D_MODEL, FF_SHARD), lambda d, j, k: (d, 0, 0)),
        jax.ShapeDtypeStruct((N_DEV, D_MODEL, FF_SHARD), out_dtype), (D_MODEL, FF_SHARD))


def _proj_in(name, h, wp):
    t = h.shape[0]
    grid = (NP_COLS // P_BLOCK, t // TM, 1)
    return _matmul(
        name, "nn", h, wp, grid,
        pl.BlockSpec((TM, D_MODEL), lambda j, i, k: (i, 0)),
        pl.BlockSpec((D_MODEL, P_BLOCK), lambda j, i, k: (0, j)),
        pl.BlockSpec((TM, P_BLOCK), lambda j, i, k: (i, j)),
        jax.ShapeDtypeStruct((t, NP_COLS), F32), None)


def _rms_fwd(name, x, w):
    t, d = x.shape

    def body(x_ref, w_ref, h_ref):
        xv = x_ref[...]
        rstd = lax.rsqrt(jnp.mean(xv * xv, axis=-1, keepdims=True) + NORM_EPS)
        h_ref[...] = (xv * rstd * w_ref[...]).astype(h_ref.dtype)

    return pl.pallas_call(
        body, name=name, grid=(t // TE,),
        in_specs=[pl.BlockSpec((TE, d), lambda i: (i, 0)), pl.BlockSpec((1, d), lambda i: (0, 0))],
        out_specs=pl.BlockSpec((TE, d), lambda i: (i, 0)),
        out_shape=jax.ShapeDtypeStruct((t, d), BF), compiler_params=_params("parallel"))(x, w)


def _rms_bwd(name, x, w, dh, dres, out_scale):
    t, d = x.shape

    def body(x_ref, w_ref, dh_ref, dres_ref, dx_ref, dxb_ref, dw_ref):
        i = pl.program_id(0)
        xv = x_ref[...]
        rstd = lax.rsqrt(jnp.mean(xv * xv, axis=-1, keepdims=True) + NORM_EPS)
        xhat = xv * rstd
        dhv = dh_ref[...]
        wd = dhv * w_ref[...]
        proj = jnp.mean(wd * xhat, axis=-1, keepdims=True)
        dx = dres_ref[...] + rstd * (wd - xhat * proj)
        dx_ref[...] = dx
        dxb_ref[...] = (dx * out_scale).astype(BF)
        part = jnp.sum(dhv * xhat, axis=0, keepdims=True)

        @pl.when(i == 0)
        def _():
            dw_ref[...] = part

        @pl.when(i > 0)
        def _():
            dw_ref[...] += part

    row = pl.BlockSpec((TE, d), lambda i: (i, 0))
    vec = pl.BlockSpec((1, d), lambda i: (0, 0))
    return pl.pallas_call(
        body, name=name, grid=(t // TE,), in_specs=[row, vec, row, row], out_specs=[row, row, vec],
        out_shape=[jax.ShapeDtypeStruct((t, d), F32), jax.ShapeDtypeStruct((t, d), BF),
                   jax.ShapeDtypeStruct((1, d), F32)],
        compiler_params=_params("arbitrary"))(x, w, dh, dres)


def _final_loss(x, w, target):
    t, d = x.shape

    def body(x_ref, w_ref, t_ref, loss_ref, dx_ref, dxb_ref, dw_ref):
        i = pl.program_id(0)
        xv = x_ref[...]
        rstd = lax.rsqrt(jnp.mean(xv * xv, axis=-1, keepdims=True) + NORM_EPS)
        xhat = xv * rstd
        err = xhat * w_ref[...] - t_ref[...]
        lpart = 0.5 * jnp.sum(jnp.mean(err * err, axis=-1, keepdims=True), axis=0, keepdims=True)
        dy = err * (1.0 / d)
        wd = dy * w_ref[...]
        proj = jnp.mean(wd * xhat, axis=-1, keepdims=True)
        dx = rstd * (wd - xhat * proj)
        dx_ref[...] = dx
        dxb_ref[...] = (0.5 * dx).astype(BF)
        part = jnp.sum(dy * xhat, axis=0, keepdims=True)
        lfull = jnp.broadcast_to(lpart, (1, 128))

        @pl.when(i == 0)
        def _():
            dw_ref[...] = part
            loss_ref[...] = lfull

        @pl.when(i > 0)
        def _():
            dw_ref[...] += part
            loss_ref[...] += lfull

    row = pl.BlockSpec((TE, d), lambda i: (i, 0))
    vec = pl.BlockSpec((1, d), lambda i: (0, 0))
    return pl.pallas_call(
        body, name="final_loss", grid=(t // TE,), in_specs=[row, vec, row],
        out_specs=[pl.BlockSpec((1, 128), lambda i: (0, 0)), row, row, vec],
        out_shape=[jax.ShapeDtypeStruct((1, 128), F32), jax.ShapeDtypeStruct((t, d), F32),
                   jax.ShapeDtypeStruct((t, d), BF), jax.ShapeDtypeStruct((1, d), F32)],
        compiler_params=_params("arbitrary"))(x, w, target)


def _swiglu_fwd(name, gu):
    t = gu.shape[1]

    def body(g_ref, u_ref, a_ref):
        g = g_ref[...]
        a_ref[...] = (g * _sigmoid(g) * u_ref[...]).astype(BF)

    blk = (None, TE, FF_SHARD)
    return pl.pallas_call(
        body, name=name, grid=(4, t // TE),
        in_specs=[pl.BlockSpec(blk, lambda d, i: (d, i, 0)), pl.BlockSpec(blk, lambda d, i: (d + 4, i, 0))],
        out_specs=pl.BlockSpec(blk, lambda d, i: (d, i, 0)),
        out_shape=jax.ShapeDtypeStruct((4, t, FF_SHARD), BF),
        compiler_params=_params("parallel", "parallel"))(gu, gu)


def _swiglu_bwd(name, gu, dact):
    t = gu.shape[1]

    def body(g_ref, u_ref, da_ref, o_ref):
        g = g_ref[...]
        da = da_ref[...]
        s = _sigmoid(g)
        o_ref[0] = (da * u_ref[...] * (s * (1.0 + g * (1.0 - s)))).astype(BF)
        o_ref[1] = (da * g * s).astype(BF)

    blk = (None, TE, FF_SHARD)
    out = pl.pallas_call(
        body, name=name, grid=(4, t // TE),
        in_specs=[pl.BlockSpec(blk, lambda d, i: (d, i, 0)), pl.BlockSpec(blk, lambda d, i: (d + 4, i, 0)),
                  pl.BlockSpec(blk, lambda d, i: (d, i, 0))],
        out_specs=pl.BlockSpec((2, None, TE, FF_SHARD), lambda d, i: (0, d, i, 0)),
        out_shape=jax.ShapeDtypeStruct((2, 4, t, FF_SHARD), BF),
        compiler_params=_params("parallel", "parallel"))(gu, gu, dact)
    return out.reshape(N_DEV, t, FF_SHARD)


CONV_CB = 256


def _shift_down(v, s):
    if s == 0:
        return v
    row = lax.broadcasted_iota(jnp.int32, v.shape, 0)
    return jnp.where(row >= s, pltpu.roll(v, s, 0), 0.0)


def _shift_up(v, s):
    if s == 0:
        return v
    n = v.shape[0]
    row = lax.broadcasted_iota(jnp.int32, v.shape, 0)
    return jnp.where(row < n - s, pltpu.roll(v, n - s, 0), 0.0)


def _conv_fwd_val(q, w_ref, k):
    out = q * w_ref[k - 1:k, :]
    for j in range(k - 1):
        out = out + _shift_down(q, k - 1 - j) * w_ref[j:j + 1, :]
    return out


def _conv_bwd_val(q, dv, w_ref, k):
    dq = dv * w_ref[k - 1:k, :]
    dws = []
    for j in range(k - 1):
        dq = dq + _shift_up(dv, k - 1 - j) * w_ref[j:j + 1, :]
        dws.append(jnp.sum(dv * _shift_down(q, k - 1 - j), axis=0, keepdims=True))
    dws.append(jnp.sum(dv * q, axis=0, keepdims=True))
    return dq, dws


def _pspec(t, off):
    base = off // CONV_CB
    return pl.BlockSpec((t, CONV_CB), lambda j: (0, base + j))


def _mix_a_fwd(p, conv_w):
    t = p.shape[0]

    def body(b_ref, c_ref, xa_ref, w_ref, o_ref):
        q = c_ref[...] * xa_ref[...]
        o_ref[...] = (b_ref[...] * _conv_fwd_val(q, w_ref, 3)).astype(BF)

    return pl.pallas_call(
        body, name="mix_a_fwd", grid=(D_MODEL // CONV_CB,),
        in_specs=[_pspec(t, OFF_B), _pspec(t, OFF_C), _pspec(t, OFF_XA),
                  pl.BlockSpec((3, CONV_CB), lambda j: (0, j))],
        out_specs=pl.BlockSpec((t, CONV_CB), lambda j: (0, j)),
        out_shape=jax.ShapeDtypeStruct((t, D_MODEL), BF), compiler_params=_params("parallel"))(p, p, p, conv_w)


def _mix_a_bwd(p, conv_w, dya):
    t = p.shape[0]

    def body(b_ref, c_ref, xa_ref, w_ref, dy_ref, db_ref, dc_ref, dxa_ref, dw_ref):
        cv = c_ref[...]
        xav = xa_ref[...]
        q = cv * xav
        va = _conv_fwd_val(q, w_ref, 3)
        dyv = dy_ref[...]
        db_ref[...] = (dyv * va).astype(BF)
        dq, dws = _conv_bwd_val(q, dyv * b_ref[...], w_ref, 3)
        dc_ref[...] = (dq * xav).astype(BF)
        dxa_ref[...] = (dq * cv).astype(BF)
        for j in range(3):
            dw_ref[j:j + 1, :] = dws[j]

    col = pl.BlockSpec((t, CONV_CB), lambda j: (0, j))
    wsp = pl.BlockSpec((3, CONV_CB), lambda j: (0, j))
    return pl.pallas_call(
        body, name="mix_a_bwd", grid=(D_MODEL // CONV_CB,),
        in_specs=[_pspec(t, OFF_B), _pspec(t, OFF_C), _pspec(t, OFF_XA), wsp, col],
        out_specs=[col, col, col, wsp],
        out_shape=[jax.ShapeDtypeStruct((t, D_MODEL), BF)] * 3 + [jax.ShapeDtypeStruct((3, D_MODEL), F32)],
        compiler_params=_params("parallel"))(p, p, p, conv_w, dya)


def _ssm_conv_fwd(p, conv_w, conv_b):
    t = p.shape[0]

    def body(x_ref, w_ref, b_ref, o_ref):
        pre = _conv_fwd_val(x_ref[...], w_ref, 4) + b_ref[...]
        o_ref[...] = pre * _sigmoid(pre)

    return pl.pallas_call(
        body, name="ssm_conv_fwd", grid=(D_XBC // CONV_CB,),
        in_specs=[_pspec(t, OFF_XBC), pl.BlockSpec((4, CONV_CB), lambda j: (0, j)),
                  pl.BlockSpec((1, CONV_CB), lambda j: (0, j))],
        out_specs=pl.BlockSpec((t, CONV_CB), lambda j: (0, j)),
        out_shape=jax.ShapeDtypeStruct((t, D_XBC), F32), compiler_params=_params("parallel"))(p, conv_w, conv_b)


def _ssm_conv_bwd(p, conv_w, conv_b, dxc):
    t = p.shape[0]

    def body(x_ref, w_ref, b_ref, d_ref, dx_ref, dw_ref, db_ref):
        xv = x_ref[...]
        pre = _conv_fwd_val(xv, w_ref, 4) + b_ref[...]
        s = _sigmoid(pre)
        dpre = d_ref[...] * (s * (1.0 + pre * (1.0 - s)))
        dq, dws = _conv_bwd_val(xv, dpre, w_ref, 4)
        dx_ref[...] = dq.astype(BF)
        for j in range(4):
            dw_ref[j:j + 1, :] = dws[j]
        db_ref[...] = jnp.sum(dpre, axis=0, keepdims=True)

    col = pl.BlockSpec((t, CONV_CB), lambda j: (0, j))
    wsp = pl.BlockSpec((4, CONV_CB), lambda j: (0, j))
    bsp = pl.BlockSpec((1, CONV_CB), lambda j: (0, j))
    return pl.pallas_call(
        body, name="ssm_conv_bwd", grid=(D_XBC // CONV_CB,),
        in_specs=[_pspec(t, OFF_XBC), wsp, bsp, col], out_specs=[col, wsp, bsp],
        out_shape=[jax.ShapeDtypeStruct((t, D_XBC), BF), jax.ShapeDtypeStruct((4, D_XBC), F32),
                   jax.ShapeDtypeStruct((1, D_XBC), F32)],
        compiler_params=_params("parallel"))(p, conv_w, conv_b, dxc)


DT_ROWS = 512


def _tri(lower):
    r = lax.broadcasted_iota(jnp.int32, (CHUNK, CHUNK), 0)
    c = lax.broadcasted_iota(jnp.int32, (CHUNK, CHUNK), 1)
    return jnp.where((r >= c) if lower else (r <= c), 1.0, 0.0).astype(F32)


def _dot_exact(a, b):
    return lax.dot_general(a, b, _DIMS["nn"], preferred_element_type=F32, precision=lax.Precision.HIGHEST)


def _dt_fwd(p, bias_pad, alog_pad):
    t = p.shape[0]

    def body(raw_ref, b_ref, al_ref, dt_ref, acs_ref):
        z = raw_ref[...] + b_ref[...]
        dt = jnp.maximum(z, 0.0) + jnp.log(1.0 + jnp.exp(-jnp.abs(z)))
        dt_ref[...] = dt
        a = dt * (-jnp.exp(al_ref[...]))
        tri = _tri(True)
        for k in range(DT_ROWS // CHUNK):
            acs_ref[k * CHUNK:(k + 1) * CHUNK, :] = _dot_exact(tri, a[k * CHUNK:(k + 1) * CHUNK, :])

    blk = pl.BlockSpec((DT_ROWS, DT_W), lambda i: (i, 0))
    vec = pl.BlockSpec((1, DT_W), lambda i: (0, 0))
    return pl.pallas_call(
        body, name="dt_fwd", grid=(t // DT_ROWS,),
        in_specs=[pl.BlockSpec((DT_ROWS, DT_W), lambda i: (i, OFF_DT // DT_W)), vec, vec],
        out_specs=[blk, blk], out_shape=[jax.ShapeDtypeStruct((t, DT_W), F32)] * 2,
        compiler_params=_params("parallel"))(p, bias_pad, alog_pad)


def _dt_bwd(p, bias_pad, alog_pad, dt, ddt, dacs):
    t = p.shape[0]

    def body(raw_ref, b_ref, al_ref, dt_ref, ddt_ref, dacs_ref, draw_ref, db_ref, dal_ref):
        i = pl.program_id(0)
        acoef = -jnp.exp(al_ref[...])
        triu = _tri(False)
        das = []
        for k in range(DT_ROWS // CHUNK):
            das.append(_dot_exact(triu, dacs_ref[k * CHUNK:(k + 1) * CHUNK, :]))
        da = jnp.concatenate(das, axis=0)
        dtv = dt_ref[...]
        ddt_tot = ddt_ref[...] + da * acoef
        lane = lax.broadcasted_iota(jnp.int32, (DT_ROWS, DT_W), 1)
        draw = jnp.where(lane < N_HEADS, ddt_tot * _sigmoid(raw_ref[...] + b_ref[...]), 0.0)
        draw_ref[...] = draw.astype(BF)
        pb = jnp.sum(draw, axis=0, keepdims=True)
        pa = jnp.sum(da * dtv * acoef, axis=0, keepdims=True)

        @pl.when(i == 0)
        def _():
            db_ref[...] = pb
            dal_ref[...] = pa

        @pl.when(i > 0)
        def _():
            db_ref[...] += pb
            dal_ref[...] += pa

    blk = pl.BlockSpec((DT_ROWS, DT_W), lambda i: (i, 0))
    vec = pl.BlockSpec((1, DT_W), lambda i: (0, 0))
    return pl.pallas_call(
        body, name="dt_bwd", grid=(t // DT_ROWS,),
        in_specs=[pl.BlockSpec((DT_ROWS, DT_W), lambda i: (i, OFF_DT // DT_W)), vec, vec, blk, blk, blk],
        out_specs=[blk, vec, vec],
        out_shape=[jax.ShapeDtypeStruct((t, DT_W), BF), jax.ShapeDtypeStruct((1, DT_W), F32),
                   jax.ShapeDtypeStruct((1, DT_W), F32)],
        compiler_params=_params("arbitrary"))(p, bias_pad, alog_pad, dt, ddt, dacs)


def _expand(v, g, lane_head):
    h0 = HEADS_PER_GROUP * g
    out = jnp.broadcast_to(v[:, h0 + 3:h0 + 4], (CHUNK, GROUP_W))
    for j in (2, 1, 0):
        out = jnp.where(lane_head == j, jnp.broadcast_to(v[:, h0 + j:h0 + j + 1], (CHUNK, GROUP_W)), out)
    return out


def _expand_rows(v, g, row_head):
    h0 = HEADS_PER_GROUP * g
    out = jnp.broadcast_to(v[:, h0 + 3:h0 + 4], (GROUP_W, 1))
    for j in (2, 1, 0):
        out = jnp.where(row_head == j, jnp.broadcast_to(v[:, h0 + j:h0 + j + 1], (GROUP_W, 1)), out)
    return out


def _ssd_consts():
    r = lax.broadcasted_iota(jnp.int32, (CHUNK, CHUNK), 0)
    c = lax.broadcasted_iota(jnp.int32, (CHUNK, CHUNK), 1)
    lane_head = lax.broadcasted_iota(jnp.int32, (CHUNK, GROUP_W), 1) // HEAD_DIM
    row_head = lax.broadcasted_iota(jnp.int32, (GROUP_W, 1), 0) // HEAD_DIM
    return r >= c, r == c, lane_head, row_head


def _decay_mat(acsv, h, tri, eye):
    col = acsv[:, h:h + 1]
    row = jnp.sum(jnp.where(eye, col, 0.0), axis=0, keepdims=True)
    return jnp.where(tri, jnp.exp(jnp.minimum(col - row, 0.0)), 0.0)


def _ssd_fwd(xconv, dt, acs, d_exp):
    t = xconv.shape[0]
    nc = t // CHUNK

    def body(xc_ref, dt_ref, acs_ref, d_ref, y_ref, hs_ref, state):
        c = pl.program_id(0)

        @pl.when(c == 0)
        def _():
            state[...] = jnp.zeros_like(state)

        hs_ref[...] = state[...]
        tri, eye, lane_head, row_head = _ssd_consts()
        dtv = dt_ref[...]
        acsv = acs_ref[...]
        atot = acsv[CHUNK - 1:CHUNK, :]
        e_all = jnp.exp(acsv)
        dec_all = jnp.exp(atot - acsv)
        eat = jnp.exp(atot)
        for g in range(N_GROUPS):
            gs = slice(GROUP_W * g, GROUP_W * (g + 1))
            xs_g = xc_ref[:, gs]
            b_g = xc_ref[:, D_INNER + D_STATE * g:D_INNER + D_STATE * (g + 1)].astype(BF)
            c_g = xc_ref[:, D_INNER + 1024 + D_STATE * g:D_INNER + 1024 + D_STATE * (g + 1)].astype(BF)
            gmat = _dot(c_g, b_g, "nt")
            x_g = xs_g * _expand(dtv, g, lane_head)
            h_g = state[gs, :]
            yoff = _dot(c_g, h_g.astype(BF), "nt") * _expand(e_all, g, lane_head)
            y_ref[:, gs] = yoff + d_ref[:, gs] * xs_g
            s_g = _dot((x_g * _expand(dec_all, g, lane_head)).astype(BF), b_g, "tn")
            state[gs, :] = _expand_rows(eat, g, row_head) * h_g + s_g
            for j in range(HEADS_PER_GROUP):
                h = HEADS_PER_GROUP * g + j
                m = gmat * _decay_mat(acsv, h, tri, eye)
                hs_ = slice(HEAD_DIM * h, HEAD_DIM * (h + 1))
                y_ref[:, hs_] += _dot(m.astype(BF), x_g[:, HEAD_DIM * j:HEAD_DIM * (j + 1)].astype(BF))

    blk = lambda w: pl.BlockSpec((CHUNK, w), lambda c: (c, 0))
    return pl.pallas_call(
        body, name="ssd_fwd", grid=(nc,),
        in_specs=[blk(D_XBC), blk(DT_W), blk(DT_W), pl.BlockSpec((1, D_INNER), lambda c: (0, 0))],
        out_specs=[blk(D_INNER), pl.BlockSpec((None, D_INNER, D_STATE), lambda c: (c, 0, 0))],
        out_shape=[jax.ShapeDtypeStruct((t, D_INNER), F32), jax.ShapeDtypeStruct((nc, D_INNER, D_STATE), F32)],
        scratch_shapes=[pltpu.VMEM((D_INNER, D_STATE), F32)],
        compiler_params=_params("arbitrary"))(xconv, dt, acs, d_exp)


def _ssd_bwd(xconv, dt, acs, d_exp, hsave, dy):
    t = xconv.shape[0]
    nc = t // CHUNK

    def body(xc_ref, dt_ref, acs_ref, d_ref, hs_ref, dy_ref, dxc_ref, ddt_ref, dacs_ref, dd_ref, dstate, dx_scr):
        c = pl.program_id(0)

        @pl.when(c == 0)
        def _():
            dstate[...] = jnp.zeros_like(dstate)
            dd_ref[...] = jnp.zeros_like(dd_ref)

        tri, eye, lane_head, row_head = _ssd_consts()
        lane = lax.broadcasted_iota(jnp.int32, (CHUNK, DT_W), 1)
        lane1 = lax.broadcasted_iota(jnp.int32, (1, DT_W), 1)
        dtv = dt_ref[...]
        acsv = acs_ref[...]
        atot = acsv[CHUNK - 1:CHUNK, :]
        e_all = jnp.exp(acsv)
        dec_all = jnp.exp(atot - acsv)
        eat = jnp.exp(atot)
        ddt_acc = jnp.zeros((CHUNK, DT_W), F32)
        dacs_acc = jnp.zeros((CHUNK, DT_W), F32)
        datot_acc = jnp.zeros((1, DT_W), F32)

        def rsum(v):
            return jnp.sum(v, axis=1, keepdims=True)

        for g in range(N_GROUPS):
            gs = slice(GROUP_W * g, GROUP_W * (g + 1))
            bs = slice(D_INNER + D_STATE * g, D_INNER + D_STATE * (g + 1))
            cs = slice(D_INNER + 1024 + D_STATE * g, D_INNER + 1024 + D_STATE * (g + 1))
            xs_g = xc_ref[:, gs]
            b_g = xc_ref[:, bs].astype(BF)
            c_g = xc_ref[:, cs].astype(BF)
            gmat = _dot(c_g, b_g, "nt")
            dt_g = _expand(dtv, g, lane_head)
            dec_g = _expand(dec_all, g, lane_head)
            e_g = _expand(e_all, g, lane_head)
            x_g = xs_g * dt_g
            h_g = hs_ref[gs, :]
            h_b = h_g.astype(BF)
            dy_g = dy_ref[:, gs]
            ds_g = dstate[gs, :]
            ds_b = ds_g.astype(BF)

            yoff = _dot(c_g, h_b, "nt") * e_g
            edy = (e_g * dy_g).astype(BF)
            d_c = _dot(edy, h_b)
            d_hc = _dot(edy, c_g, "tn")
            bds = _dot(b_g, ds_b, "nt")
            xd = x_g * dec_g
            d_b = _dot(xd.astype(BF), ds_b)
            dx_scr[...] = dec_g * bds
            q_off = dy_g * yoff
            q_dec = xd * bds
            hh = ds_g * h_g
            d_g = jnp.zeros((CHUNK, CHUNK), F32)
            for j in range(HEADS_PER_GROUP):
                h = HEADS_PER_GROUP * g + j
                js = slice(HEAD_DIM * j, HEAD_DIM * (j + 1))
                lmat = _decay_mat(acsv, h, tri, eye)
                m = gmat * lmat
                dy_h = dy_g[:, js].astype(BF)
                dm = _dot(dy_h, x_g[:, js].astype(BF), "nt")
                dx_scr[:, js] += _dot(m.astype(BF), dy_h, "tn")
                d_g = d_g + dm * lmat
                w = dm * m
                cs_row = jnp.sum(w, axis=0, keepdims=True)
                cs_col = rsum(jnp.where(eye, cs_row, 0.0))
                t_h = rsum(q_dec[:, js])
                dacs_h = rsum(w) - cs_col + rsum(q_off[:, js]) - t_h
                hh_h = jnp.sum(rsum(hh[js, :]), axis=0, keepdims=True)
                datot_h = jnp.sum(t_h, axis=0, keepdims=True) + eat[:, h:h + 1] * hh_h
                dacs_acc = jnp.where(lane == h, dacs_h, dacs_acc)
                datot_acc = jnp.where(lane1 == h, datot_h, datot_acc)
            d_gb = d_g.astype(BF)
            dxc_ref[:, cs] = d_c + _dot(d_gb, b_g)
            dxc_ref[:, bs] = d_b + _dot(d_gb, c_g, "tn")
            dx_full = dx_scr[...]
            dxc_ref[:, gs] = dx_full * dt_g + d_ref[:, gs] * dy_g
            qx = dx_full * xs_g
            for j in range(HEADS_PER_GROUP):
                h = HEADS_PER_GROUP * g + j
                ddt_acc = jnp.where(lane == h, rsum(qx[:, HEAD_DIM * j:HEAD_DIM * (j + 1)]), ddt_acc)
            dd_ref[:, gs] += jnp.sum(dy_g * xs_g, axis=0, keepdims=True)
            dstate[gs, :] = _expand_rows(eat, g, row_head) * ds_g + d_hc

        rowi = lax.broadcasted_iota(jnp.int32, (CHUNK, DT_W), 0)
        ddt_ref[...] = ddt_acc
        dacs_ref[...] = dacs_acc + jnp.where(rowi == CHUNK - 1, datot_acc, 0.0)

    rev = lambda w: pl.BlockSpec((CHUNK, w), lambda c: (nc - 1 - c, 0))
    vec = pl.BlockSpec((1, D_INNER), lambda c: (0, 0))
    return pl.pallas_call(
        body, name="ssd_bwd", grid=(nc,),
        in_specs=[rev(D_XBC), rev(DT_W), rev(DT_W), vec,
                  pl.BlockSpec((None, D_INNER, D_STATE), lambda c: (nc - 1 - c, 0, 0)), rev(D_INNER)],
        out_specs=[rev(D_XBC), rev(DT_W), rev(DT_W), vec],
        out_shape=[jax.ShapeDtypeStruct((t, D_XBC), F32), jax.ShapeDtypeStruct((t, DT_W), F32),
                   jax.ShapeDtypeStruct((t, DT_W), F32), jax.ShapeDtypeStruct((1, D_INNER), F32)],
        scratch_shapes=[pltpu.VMEM((D_INNER, D_STATE), F32), pltpu.VMEM((CHUNK, GROUP_W), F32)],
        compiler_params=_params("arbitrary"))(xconv, dt, acs, d_exp, hsave, dy)


def _gnorm_fwd(y, p, w):
    t = y.shape[0]
    zoff = OFF_Z // GROUP_W

    def body(y_ref, z_ref, w_ref, o_ref):
        z = z_ref[...]
        yf = y_ref[...] * (z * _sigmoid(z))
        rstd = lax.rsqrt(jnp.mean(yf * yf, axis=-1, keepdims=True) + NORM_EPS)
        o_ref[...] = (yf * rstd * w_ref[...]).astype(BF)

    blk = pl.BlockSpec((TE, GROUP_W), lambda i, j: (i, j))
    return pl.pallas_call(
        body, name="gnorm_fwd", grid=(t // TE, N_GROUPS),
        in_specs=[blk, pl.BlockSpec((TE, GROUP_W), lambda i, j: (i, zoff + j)),
                  pl.BlockSpec((1, GROUP_W), lambda i, j: (0, j))],
        out_specs=blk, out_shape=jax.ShapeDtypeStruct((t, D_INNER), BF),
        compiler_params=_params("parallel", "parallel"))(y, p, w)


def _gnorm_bwd(y, p, w, dyn):
    t = y.shape[0]
    zoff = OFF_Z // GROUP_W

    def body(y_ref, z_ref, w_ref, dn_ref, dy_ref, dz_ref, dw_ref):
        i = pl.program_id(1)
        z = z_ref[...]
        yv = y_ref[...]
        s = _sigmoid(z)
        sil = z * s
        yf = yv * sil
        rstd = lax.rsqrt(jnp.mean(yf * yf, axis=-1, keepdims=True) + NORM_EPS)
        xhat = yf * rstd
        dn = dn_ref[...]
        wd = dn * w_ref[...]
        proj = jnp.mean(wd * xhat, axis=-1, keepdims=True)
        dyf = rstd * (wd - xhat * proj)
        dy_ref[...] = dyf * sil
        dz_ref[...] = (dyf * yv * (s * (1.0 + z * (1.0 - s)))).astype(BF)
        part = jnp.sum(dn * xhat, axis=0, keepdims=True)

        @pl.when(i == 0)
        def _():
            dw_ref[...] = part

        @pl.when(i > 0)
        def _():
            dw_ref[...] += part

    blk = pl.BlockSpec((TE, GROUP_W), lambda j, i: (i, j))
    vec = pl.BlockSpec((1, GROUP_W), lambda j, i: (0, j))
    return pl.pallas_call(
        body, name="gnorm_bwd", grid=(N_GROUPS, t // TE),
        in_specs=[blk, pl.BlockSpec((TE, GROUP_W), lambda j, i: (i, zoff + j)), vec, blk],
        out_specs=[blk, blk, vec],
        out_shape=[jax.ShapeDtypeStruct((t, D_INNER), F32), jax.ShapeDtypeStruct((t, D_INNER), BF),
                   jax.ShapeDtypeStruct((1, D_INNER), F32)],
        compiler_params=_params("parallel", "arbitrary"))(y, p, w, dyn)


MERGE_CB = 512


def _merge_fwd(p, ya, yb):
    t = ya.shape[0]

    def body(ga_ref, gb_ref, ya_ref, yb_ref, o_ref):
        o_ref[...] = (_sigmoid(ga_ref[...]) * ya_ref[...] + _sigmoid(gb_ref[...]) * yb_ref[...]).astype(BF)

    blk = pl.BlockSpec((TE, MERGE_CB), lambda i, j: (i, j))
    return pl.pallas_call(
        body, name="merge_fwd", grid=(t // TE, D_MODEL // MERGE_CB),
        in_specs=[pl.BlockSpec((TE, MERGE_CB), lambda i, j: (i, OFF_GA // MERGE_CB + j)),
                  pl.BlockSpec((TE, MERGE_CB), lambda i, j: (i, OFF_GB // MERGE_CB + j)), blk, blk],
        out_specs=blk, out_shape=jax.ShapeDtypeStruct((t, D_MODEL), BF),
        compiler_params=_params("parallel", "parallel"))(p, p, ya, yb)


def _merge_bwd(p, ya, yb, dm):
    t = ya.shape[0]

    def body(ga_ref, gb_ref, ya_ref, yb_ref, dm_ref, dga_ref, dgb_ref, dya_ref, dyb_ref):
        d = dm_ref[...]
        sa = _sigmoid(ga_ref[...])
        sb = _sigmoid(gb_ref[...])
        dga_ref[...] = (d * ya_ref[...] * sa * (1.0 - sa)).astype(BF)
        dgb_ref[...] = (d * yb_ref[...] * sb * (1.0 - sb)).astype(BF)
        dya_ref[...] = (d * sa).astype(BF)
        dyb_ref[...] = (d * sb).astype(BF)

    blk = pl.BlockSpec((TE, MERGE_CB), lambda i, j: (i, j))
    return pl.pallas_call(
        body, name="merge_bwd", grid=(t // TE, D_MODEL // MERGE_CB),
        in_specs=[pl.BlockSpec((TE, MERGE_CB), lambda i, j: (i, OFF_GA // MERGE_CB + j)),
                  pl.BlockSpec((TE, MERGE_CB), lambda i, j: (i, OFF_GB // MERGE_CB + j)), blk, blk, blk],
        out_specs=[blk] * 4, out_shape=[jax.ShapeDtypeStruct((t, D_MODEL), BF)] * 4,
        compiler_params=_params("parallel", "parallel"))(p, p, ya, yb, dm)


def _exchange(name, entries):
    n = len(entries)

    def body(*refs):
        srcs, outs = refs[:n], refs[n:2 * n]
        send_sems, recv_sems, local_sems = refs[2 * n:]
        x, y, c = lax.axis_index("x"), lax.axis_index("y"), lax.axis_index("c")
        me = 4 * x + 2 * y + c
        sends, recvs, locals_ = [], [], []
        for i, (_, scatter) in enumerate(entries):
            src, out = srcs[i], outs[i]
            lc = pltpu.make_async_copy(src.at[me] if scatter else src, out.at[me], local_sems.at[i])
            lc.start()
            locals_.append(lc)
            for m in range(1, N_DEV):
                px = 1 - x if m & 4 else x
                py = 1 - y if m & 2 else y
                pc = 1 - c if m & 1 else c
                peer = 4 * px + 2 * py + pc
                k = i * (N_DEV - 1) + m - 1
                piece = src.at[peer] if scatter else src
                cp = pltpu.make_async_remote_copy(
                    src_ref=piece, dst_ref=out.at[me], send_sem=send_sems.at[k], recv_sem=recv_sems.at[k],
                    device_id=(px, py, pc), device_id_type=pl.DeviceIdType.MESH)
                cp.start()
                sends.append(cp)
                recvs.append(pltpu.make_async_remote_copy(
                    src_ref=piece, dst_ref=out.at[peer], send_sem=send_sems.at[k], recv_sem=recv_sems.at[k],
                    device_id=(px, py, pc), device_id_type=pl.DeviceIdType.MESH))
        for cp in sends:
            cp.wait_send()
        for cp in recvs:
            cp.wait_recv()
        for lc in locals_:
            lc.wait()

    out_shape = []
    for a, scatter in entries:
        shard = a.shape[1:] if scatter else a.shape
        out_shape.append(jax.ShapeDtypeStruct((N_DEV,) + tuple(shard), a.dtype))
    any_spec = pl.BlockSpec(memory_space=pl.ANY)
    nsem = n * (N_DEV - 1)
    return pl.pallas_call(
        body, name=name, in_specs=[any_spec] * n, out_specs=[any_spec] * n, out_shape=out_shape,
        scratch_shapes=[pltpu.SemaphoreType.DMA((nsem,)), pltpu.SemaphoreType.DMA((nsem,)),
                        pltpu.SemaphoreType.DMA((n,))],
        compiler_params=pltpu.CompilerParams(has_side_effects=True))(*[a for a, _ in entries])


def _adamw(name, parts, w, m, v):
    r, c = w.shape
    tr = r
    for cand in (256, 128):
        if r > cand and r % cand == 0:
            tr = cand
            break
    bc1 = 1.0 - ADAM_B1 ** ADAM_STEP
    bc2 = 1.0 - ADAM_B2 ** ADAM_STEP

    def body(p_ref, w_ref, m_ref, v_ref, g_ref, d_ref, nm_ref, nv_ref):
        g = p_ref[0].astype(F32)
        for k in range(1, N_DEV):
            g = g + p_ref[k].astype(F32)
        nm = ADAM_B1 * m_ref[...] + (1.0 - ADAM_B1) * g
        nv = ADAM_B2 * v_ref[...] + (1.0 - ADAM_B2) * (g * g)
        g_ref[...] = g
        nm_ref[...] = nm
        nv_ref[...] = nv
        d_ref[...] = -ADAM_LR * ((nm / bc1) / (jnp.sqrt(nv / bc2) + ADAM_EPS) + ADAM_WD * w_ref[...])

    blk = pl.BlockSpec((tr, c), lambda i: (i, 0))
    return pl.pallas_call(
        body, name=name, grid=(r // tr,),
        in_specs=[pl.BlockSpec((N_DEV, tr, c), lambda i: (0, i, 0)), blk, blk, blk],
        out_specs=[blk] * 4, out_shape=[jax.ShapeDtypeStruct((r, c), F32)] * 4,
        compiler_params=_params("parallel"))(parts, w, m, v)


_SECTIONS = [(0, 1024), (1024, 2048), (2048, 3072), (3072, 5120), (5120, 9216), (9248, 10272), (10272, 11296),
             (9216, 9248)]


def _to_padded_cols(w_full):
    parts = [w_full[:, a:b] for a, b in _SECTIONS]
    parts.append(jnp.zeros((w_full.shape[0], NP_COLS - N_IN), w_full.dtype))
    return jnp.concatenate(parts, axis=1)


def _from_padded_cols(g):
    return jnp.concatenate(
        [g[:, OFF_B:OFF_Z + 2048], g[:, OFF_XBC:OFF_XBC + 4096], g[:, OFF_DT:OFF_DT + N_HEADS],
         g[:, OFF_GA:OFF_GA + 2048]], axis=1)


def _pad_lanes(v, width):
    return jnp.pad(v, ((0, 0), (0, width - v.shape[1])))


def _local_step(x, target, wts):
    gdt = BF
    t = x.shape[0]
    h1 = _rms_fwd("rms1_fwd", x, wts["ffn1_norm"])
    gu1 = _ffn_in("ffn1_in", h1, wts["ffn1_w_in"])
    act1 = _swiglu_fwd("swiglu1_fwd", gu1)
    x1 = _ffn_out("ffn1_out", act1, wts["ffn1_w_out"], x)

    h2 = _rms_fwd("rms2_fwd", x1, wts["mix_norm"])
    p = _proj_in("proj_in", h2, wts["w_in"])
    ya_in = _mix_a_fwd(p, wts["short_conv_w"])
    y_a = _mm_nn("short_out", ya_in, wts["short_w_out"])
    xconv = _ssm_conv_fwd(p, wts["ssm_conv_w"], wts["ssm_conv_b"])
    dt, acs = _dt_fwd(p, wts["dt_bias_pad"], wts["a_log_pad"])
    y_ssm, hsave = _ssd_fwd(xconv, dt, acs, wts["d_exp"])
    yn = _gnorm_fwd(y_ssm, p, wts["ssm_norm"])
    y_b = _mm_nn("ssm_out", yn, wts["ssm_w_out"], tk=1024)
    merged = _merge_fwd(p, y_a, y_b)
    x2 = _mm_nn("mix_out", merged, wts["w_out"], res=x1)

    h3 = _rms_fwd("rms3_fwd", x2, wts["ffn2_norm"])
    gu2 = _ffn_in("ffn2_in", h3, wts["ffn2_w_in"])
    act2 = _swiglu_fwd("swiglu2_fwd", gu2)
    x3 = _ffn_out("ffn2_out", act2, wts["ffn2_w_out"], x2)

    loss, dx3, dx3h, g_final = _final_loss(x3, wts["final_norm"], target)

    grads = {"final_norm": g_final}
    dact2 = _ffn_out_bwd_act("ffn2_out_bwd_act", dx3h, wts["ffn2_w_out"])
    grads["ffn2_w_out"] = _ffn_out_bwd_w("ffn2_out_bwd_w", act2, dx3h, gdt)
    dgu2 = _swiglu_bwd("swiglu2_bwd", gu2, dact2)
    grads["ffn2_w_in"] = _ffn_in_bwd_w("ffn2_in_bwd_w", h3, dgu2, gdt)
    dh3 = _ffn_in_bwd_h("ffn2_in_bwd_h", dgu2, wts["ffn2_w_in"])
    dx2, dx2b, grads["ffn2_norm"] = _rms_bwd("rms3_bwd", x2, wts["ffn2_norm"], dh3, dx3, 1.0)

    dmerged = _mm_nt("mix_out_bwd_x", dx2b, wts["w_out"])
    grads["w_out"] = _mm_tn("mix_out_bwd_w", merged, dx2b, gdt)
    dga, dgb, dya, dyb = _merge_bwd(p, y_a, y_b, dmerged)

    dya_in = _mm_nt("short_out_bwd_x", dya, wts["short_w_out"])
    grads["short_w_out"] = _mm_tn("short_out_bwd_w", ya_in, dya, gdt)
    db, dc, dxa, grads["short_conv_w"] = _mix_a_bwd(p, wts["short_conv_w"], dya_in)

    dyn = _mm_nt("ssm_out_bwd_x", dyb, wts["ssm_w_out"])
    grads["ssm_w_out"] = _mm_tn("ssm_out_bwd_w", yn, dyb, gdt)
    dy_ssm, dz, grads["ssm_norm"] = _gnorm_bwd(y_ssm, p, wts["ssm_norm"], dyn)
    dxconv, ddt, dacs, dd_lane = _ssd_bwd(xconv, dt, acs, wts["d_exp"], hsave, dy_ssm)
    grads["ssm_D"] = dd_lane.reshape(N_HEADS, HEAD_DIM).sum(axis=1)[None, :]
    dxbc, grads["ssm_conv_w"], grads["ssm_conv_b"] = _ssm_conv_bwd(p, wts["ssm_conv_w"], wts["ssm_conv_b"], dxconv)
    draw, dbias, dalog = _dt_bwd(p, wts["dt_bias_pad"], wts["a_log_pad"], dt, ddt, dacs)
    grads["ssm_dt_bias"] = dbias[:, :N_HEADS]
    grads["ssm_A_log"] = dalog[:, :N_HEADS]

    dp = jnp.concatenate(
        [db, dc, dxa, dz, dxbc, dga, dgb, draw, jnp.zeros((t, NP_COLS - OFF_DT - DT_W), BF)], axis=1)
    dh2 = _mm_nt("proj_in_bwd_x", dp, wts["w_in"], tk=P_BLOCK)
    grads["w_in"] = _mm_tn("proj_in_bwd_w", h2, dp, gdt, tn=P_BLOCK)
    dx1, dx1h, grads["mix_norm"] = _rms_bwd("rms2_bwd", x1, wts["mix_norm"], dh2, dx2, 0.5)

    dact1 = _ffn_out_bwd_act("ffn1_out_bwd_act", dx1h, wts["ffn1_w_out"])
    grads["ffn1_w_out"] = _ffn_out_bwd_w("ffn1_out_bwd_w", act1, dx1h, gdt)
    dgu1 = _swiglu_bwd("swiglu1_bwd", gu1, dact1)
    grads["ffn1_w_in"] = _ffn_in_bwd_w("ffn1_in_bwd_w", h1, dgu1, gdt)
    dh1 = _ffn_in_bwd_h("ffn1_in_bwd_h", dgu1, wts["ffn1_w_in"])
    dx0, _, grads["ffn1_norm"] = _rms_bwd("rms1_bwd", x, wts["ffn1_norm"], dh1, dx1, 1.0)
    return loss, dx0, grads


_SMALL = [("ffn1_norm", 1024), ("mix_norm", 1024), ("ssm_conv_b", 4096), ("ssm_dt_bias", 32), ("ssm_A_log", 32),
          ("ssm_D", 32), ("ssm_norm", 2048), ("ffn2_norm", 1024), ("final_norm", 1024)]
SMALL_W = 10368


def _pack_small(d, loss=None):
    parts = [d[n].reshape(1, -1).astype(F32) for n, _ in _SMALL]
    used = sum(sz for _, sz in _SMALL)
    tail = jnp.zeros((1, SMALL_W - used), F32)
    if loss is not None:
        tail = tail.at[:, 0:1].set(loss)
    return jnp.concatenate(parts + [tail], axis=1)


def _unpack_small(v, shapes):
    out, off = {}, 0
    for n, sz in _SMALL:
        out[n] = v[:, off:off + sz].reshape(shapes[n])
        off += sz
    return out, v[0, off]


_SHARDED = ["ffn1_w_in", "ffn1_w_out", "w_in", "short_conv_w", "short_w_out", "ssm_conv_w", "ssm_w_out", "w_out",
            "ffn2_w_in", "ffn2_w_out"]
_ORDER = ["ffn1_norm", "ffn1_w_in", "ffn1_w_out", "mix_norm", "w_in", "short_conv_w", "short_w_out", "ssm_conv_w",
          "ssm_conv_b", "ssm_dt_bias", "ssm_A_log", "ssm_D", "ssm_norm", "ssm_w_out", "w_out", "ffn2_norm",
          "ffn2_w_in", "ffn2_w_out", "final_norm"]


def kernel(x, ffn1_norm, ffn1_w_in, ffn1_w_out, mix_norm, w_in, short_conv_w, short_w_out, ssm_conv_w, ssm_conv_b, ssm_dt_bias, ssm_A_log, ssm_D, ssm_norm, ssm_w_out, w_out, ffn2_norm, ffn2_w_in, ffn2_w_out, final_norm, loss_target, m_ffn1_norm, m_ffn1_w_in, m_ffn1_w_out, m_mix_norm, m_w_in, m_short_conv_w, m_short_w_out, m_ssm_conv_w, m_ssm_conv_b, m_ssm_dt_bias, m_ssm_A_log, m_ssm_D, m_ssm_norm, m_ssm_w_out, m_w_out, m_ffn2_norm, m_ffn2_w_in, m_ffn2_w_out, m_final_norm, v_ffn1_norm, v_ffn1_w_in, v_ffn1_w_out, v_mix_norm, v_w_in, v_short_conv_w, v_short_w_out, v_ssm_conv_w, v_ssm_conv_b, v_ssm_dt_bias, v_ssm_A_log, v_ssm_D, v_ssm_norm, v_ssm_w_out, v_w_out, v_ffn2_norm, v_ffn2_w_in, v_ffn2_w_out, v_final_norm):
    w = dict(ffn1_norm=ffn1_norm, ffn1_w_in=ffn1_w_in, ffn1_w_out=ffn1_w_out, mix_norm=mix_norm, w_in=w_in,
             short_conv_w=short_conv_w, short_w_out=short_w_out, ssm_conv_w=ssm_conv_w, ssm_conv_b=ssm_conv_b,
             ssm_dt_bias=ssm_dt_bias, ssm_A_log=ssm_A_log, ssm_D=ssm_D, ssm_norm=ssm_norm, ssm_w_out=ssm_w_out,
             w_out=w_out, ffn2_norm=ffn2_norm, ffn2_w_in=ffn2_w_in, ffn2_w_out=ffn2_w_out, final_norm=final_norm)
    m = dict(ffn1_norm=m_ffn1_norm, ffn1_w_in=m_ffn1_w_in, ffn1_w_out=m_ffn1_w_out, mix_norm=m_mix_norm, w_in=m_w_in,
             short_conv_w=m_short_conv_w, short_w_out=m_short_w_out, ssm_conv_w=m_ssm_conv_w,
             ssm_conv_b=m_ssm_conv_b, ssm_dt_bias=m_ssm_dt_bias, ssm_A_log=m_ssm_A_log, ssm_D=m_ssm_D,
             ssm_norm=m_ssm_norm, ssm_w_out=m_ssm_w_out, w_out=m_w_out, ffn2_norm=m_ffn2_norm,
             ffn2_w_in=m_ffn2_w_in, ffn2_w_out=m_ffn2_w_out, final_norm=m_final_norm)
    v = dict(ffn1_norm=v_ffn1_norm, ffn1_w_in=v_ffn1_w_in, ffn1_w_out=v_ffn1_w_out, mix_norm=v_mix_norm, w_in=v_w_in,
             short_conv_w=v_short_conv_w, short_w_out=v_short_w_out, ssm_conv_w=v_ssm_conv_w,
             ssm_conv_b=v_ssm_conv_b, ssm_dt_bias=v_ssm_dt_bias, ssm_A_log=v_ssm_A_log, ssm_D=v_ssm_D,
             ssm_norm=v_ssm_norm, ssm_w_out=v_ssm_w_out, w_out=v_w_out, ffn2_norm=v_ffn2_norm,
             ffn2_w_in=v_ffn2_w_in, ffn2_w_out=v_ffn2_w_out, final_norm=v_final_norm)
    shapes = {n: w[n].shape for n in _ORDER}
    shard = {n: w[n][0] for n in _SHARDED}

    big = [n for n in _SHARDED if n not in ("short_conv_w", "ssm_conv_w")]
    send = [(shard[n].astype(BF) if n in big else shard[n], False) for n in _SHARDED]
    got = dict(zip(_SHARDED, _exchange("gather_weights", send)))

    full = {
        "ffn1_w_in": got["ffn1_w_in"],
        "ffn2_w_in": got["ffn2_w_in"],
        "ffn1_w_out": got["ffn1_w_out"].reshape(4, FF_SHARD, D_MODEL),
        "ffn2_w_out": got["ffn2_w_out"].reshape(4, FF_SHARD, D_MODEL),
        "w_in": _to_padded_cols(got["w_in"].transpose(1, 0, 2).reshape(D_MODEL, N_IN)),
        "short_conv_w": got["short_conv_w"].transpose(1, 0, 2).reshape(3, D_MODEL),
        "ssm_conv_w": got["ssm_conv_w"].transpose(1, 0, 2).reshape(4, D_XBC),
        "short_w_out": got["short_w_out"].reshape(D_MODEL, D_MODEL),
        "ssm_w_out": got["ssm_w_out"].reshape(D_INNER, D_MODEL),
        "w_out": got["w_out"].reshape(D_MODEL, D_MODEL),
        "ffn1_norm": ffn1_norm, "mix_norm": mix_norm, "ffn2_norm": ffn2_norm, "ssm_norm": ssm_norm,
        "ssm_conv_b": ssm_conv_b, "final_norm": final_norm.reshape(1, D_MODEL),
        "dt_bias_pad": _pad_lanes(ssm_dt_bias, DT_W), "a_log_pad": _pad_lanes(ssm_A_log, DT_W),
        "d_exp": jnp.repeat(ssm_D, HEAD_DIM, axis=1),
    }

    loss_part, grad_x, g = _local_step(x[0], loss_target[0], full)

    slots = {
        "ffn1_w_in": g["ffn1_w_in"], "ffn2_w_in": g["ffn2_w_in"],
        "ffn1_w_out": g["ffn1_w_out"].reshape(N_DEV, FF_SHARD // 2, D_MODEL),
        "ffn2_w_out": g["ffn2_w_out"].reshape(N_DEV, FF_SHARD // 2, D_MODEL),
        "w_in": _from_padded_cols(g["w_in"]).reshape(D_MODEL, N_DEV, IN_SHARD).transpose(1, 0, 2),
        "short_conv_w": g["short_conv_w"].reshape(3, N_DEV, -1).transpose(1, 0, 2),
        "ssm_conv_w": g["ssm_conv_w"].reshape(4, N_DEV, -1).transpose(1, 0, 2),
        "short_w_out": g["short_w_out"].reshape(N_DEV, -1, D_MODEL),
        "ssm_w_out": g["ssm_w_out"].reshape(N_DEV, -1, D_MODEL),
        "w_out": g["w_out"].reshape(N_DEV, -1, D_MODEL),
    }
    small = _pack_small(g, loss_part[:, 0:1])
    ex = _exchange("exchange_grads", [(slots[n], True) for n in _SHARDED] + [(small, False)])
    parts = dict(zip(_SHARDED, ex[:-1]))
    small_parts = ex[-1]

    out_g, out_d, out_m, out_v = {}, {}, {}, {}
    for n in _SHARDED:
        res = _adamw("adamw_" + n, parts[n], shard[n], m[n][0], v[n][0])
        out_g[n], out_d[n], out_m[n], out_v[n] = [r.reshape(shapes[n]) for r in res]
    sres = _adamw("adamw_small", small_parts, _pack_small(w), _pack_small(m), _pack_small(v))
    sg, loss = _unpack_small(sres[0], shapes)
    sd, _ = _unpack_small(sres[1], shapes)
    sm, _ = _unpack_small(sres[2], shapes)
    sv, _ = _unpack_small(sres[3], shapes)
    out_g.update(sg)
    out_d.update(sd)
    out_m.update(sm)
    out_v.update(sv)
    return (loss, grad_x[None], *[out_g[n] for n in _ORDER], *[out_d[n] for n in _ORDER],
            *[out_m[n] for n in _ORDER], *[out_v[n] for n in _ORDER])
```

```python
import functools

import jax
import jax.numpy as jnp
from jax import lax
from jax.experimental import pallas as pl
from jax.experimental.pallas import tpu as pltpu

F32 = jnp.float32
BF = jnp.bfloat16

N_DEV = 8
D_MODEL = 1024
D_FF = 2816
D_INNER = 2048
D_XBC = 4096
N_HEADS = 32
HEAD_DIM = 64
N_GROUPS = 8
D_STATE = 128
CHUNK = 64
GROUP_W = D_INNER // N_GROUPS
HEADS_PER_GROUP = N_HEADS // N_GROUPS
NORM_EPS = 1e-5
N_IN = 11296
FF_SHARD = 2 * D_FF // N_DEV
IN_SHARD = N_IN // N_DEV

OFF_B, OFF_C, OFF_XA, OFF_Z, OFF_XBC, OFF_GA, OFF_GB, OFF_DT = 0, 1024, 2048, 3072, 5120, 9216, 10240, 11264
NP_COLS = 11520
P_BLOCK = 1280
DT_W = 128

ADAM_LR, ADAM_B1, ADAM_B2, ADAM_EPS, ADAM_WD, ADAM_STEP = 0.001, 0.9, 0.999, 1e-08, 0.01, 10

VMEM_LIMIT_V7X = 56 * 1024 * 1024
TM = 512
TE = 256


def _params(*sem):
    return pltpu.CompilerParams(dimension_semantics=sem, vmem_limit_bytes=VMEM_LIMIT_V7X)


_DIMS = {
    "nn": (((1,), (0,)), ((), ())),
    "nt": (((1,), (1,)), ((), ())),
    "tn": (((0,), (0,)), ((), ())),
}


def _dot(a, b, mode="nn"):
    return lax.dot_general(a, b, _DIMS[mode], preferred_element_type=F32)


def _sigmoid(x):
    return 1.0 / (1.0 + jnp.exp(-x))


class _Comm:
    def __init__(self, inputs, out_shapes, sems, start, finish):
        self.inputs, self.out_shapes, self.sems, self.start, self.finish = inputs, out_shapes, sems, start, finish


def _pcall(name, body, grid, in_specs, out_specs, out_shape, args, scratch=(), sem=None, comm=None):
    single = not isinstance(out_shape, (list, tuple))
    out_shapes = [out_shape] if single else list(out_shape)
    out_specs = [out_specs] if single else list(out_specs)
    n_in, n_out, n_scr = len(args), len(out_shapes), len(scratch)
    if comm is None:
        res = pl.pallas_call(
            body, name=name, grid=grid, in_specs=list(in_specs), out_specs=out_specs, out_shape=out_shapes,
            scratch_shapes=list(scratch), compiler_params=_params(*sem))(*args)
        return (res[0] if single else res), []
    nci, nco = len(comm.inputs), len(comm.out_shapes)

    def wrapped(*refs):
        a = refs[:n_in]
        ci = refs[n_in:n_in + nci]
        o0 = n_in + nci
        o = refs[o0:o0 + n_out]
        co = refs[o0 + n_out:o0 + n_out + nco]
        s0 = o0 + n_out + nco
        s = refs[s0:s0 + n_scr]
        cs = refs[s0 + n_scr:]
        pids = [pl.program_id(i) for i in range(len(grid))]
        first = functools.reduce(jnp.logical_and, [p == 0 for p in pids])
        last = functools.reduce(jnp.logical_and, [p == g - 1 for p, g in zip(pids, grid)])

        @pl.when(first)
        def _():
            comm.start(ci, co, cs)

        body(*a, *o, *s)

        @pl.when(last)
        def _():
            comm.finish(ci, co, cs)

    any_spec = pl.BlockSpec(memory_space=pl.ANY)
    res = pl.pallas_call(
        wrapped, name=name, grid=grid, in_specs=list(in_specs) + [any_spec] * nci,
        out_specs=out_specs + [any_spec] * nco, out_shape=out_shapes + list(comm.out_shapes),
        scratch_shapes=list(scratch) + list(comm.sems),
        compiler_params=_params(*(("arbitrary",) * len(grid))))(*args, *comm.inputs)
    core = res[:n_out]
    return (core[0] if single else core), list(res[n_out:])


def _comm_call(name, comm):
    nci, nco = len(comm.inputs), len(comm.out_shapes)

    def body(*refs):
        ci, co, cs = refs[:nci], refs[nci:nci + nco], refs[nci + nco:]
        comm.start(ci, co, cs)
        comm.finish(ci, co, cs)

    any_spec = pl.BlockSpec(memory_space=pl.ANY)
    return pl.pallas_call(
        body, name=name, in_specs=[any_spec] * nci, out_specs=[any_spec] * nco, out_shape=list(comm.out_shapes),
        scratch_shapes=list(comm.sems), compiler_params=pltpu.CompilerParams(has_side_effects=True))(*comm.inputs)


def _remote(src, dst, ssem, rsem, dev):
    return pltpu.make_async_remote_copy(src_ref=src, dst_ref=dst, send_sem=ssem, recv_sem=rsem, device_id=dev,
                                        device_id_type=pl.DeviceIdType.MESH)


def _place():
    x, y, c = lax.axis_index("x"), lax.axis_index("y"), lax.axis_index("c")
    other_chips = [(1 - x, y), (x, 1 - y), (1 - x, 1 - y)]
    return x, y, c, other_chips


def _gather_comm(shards):
    n = len(shards)
    per = N_DEV - 1

    def start(ins, outs, sems):
        send, recv, loc = sems
        x, y, c, chips = _place()
        me = 4 * x + 2 * y + c
        for i in range(n):
            pltpu.make_async_copy(ins[i], outs[i].at[me], loc.at[i]).start()
            _remote(ins[i], outs[i].at[me], send.at[per * i], recv.at[per * i], (x, y, 1 - c)).start()
            for j, (qx, qy) in enumerate(chips):
                _remote(ins[i], outs[i].at[me], send.at[per * i + 1 + j], recv.at[per * i + 1 + j], (qx, qy, c)).start()

    def finish(ins, outs, sems):
        send, recv, loc = sems
        x, y, c, chips = _place()
        me = 4 * x + 2 * y + c
        sib = (x, y, 1 - c)
        for i in range(n):
            for j, (qx, qy) in enumerate(chips):
                blk = outs[i].at[4 * qx + 2 * qy + c]
                _remote(blk, blk, send.at[per * i + 1 + j], recv.at[per * i + 1 + j], (qx, qy, c)).wait_recv()
                _remote(blk, blk, send.at[per * i + 4 + j], recv.at[per * i + 4 + j], sib).start()
        for i in range(n):
            blk = outs[i].at[4 * x + 2 * y + 1 - c]
            _remote(blk, blk, send.at[per * i], recv.at[per * i], sib).wait_recv()
            for j, (qx, qy) in enumerate(chips):
                blk = outs[i].at[4 * qx + 2 * qy + 1 - c]
                _remote(blk, blk, send.at[per * i + 4 + j], recv.at[per * i + 4 + j], sib).wait_recv()
        for i in range(n):
            own = outs[i].at[me]
            for k in range(per):
                _remote(ins[i], own, send.at[per * i + k], recv.at[per * i + k], sib).wait_send()
            pltpu.make_async_copy(ins[i], own, loc.at[i]).wait()

    out_shapes = [jax.ShapeDtypeStruct((N_DEV,) + tuple(a.shape), a.dtype) for a in shards]
    sems = [pltpu.SemaphoreType.DMA((per * n,)), pltpu.SemaphoreType.DMA((per * n,)), pltpu.SemaphoreType.DMA((n,))]
    return _Comm(list(shards), out_shapes, sems, start, finish)


def _pair_comm(slots):
    n = len(slots)

    def copies(ins, outs, sems):
        send, recv = sems
        x, y, c, _ = _place()
        sib = (x, y, 1 - c)
        out = []
        for i in range(n):
            for q in range(4):
                out.append(_remote(ins[i].at[2 * q + 1 - c], outs[i].at[q], send.at[4 * i + q], recv.at[4 * i + q], sib))
        return out

    def start(ins, outs, sems):
        for cp in copies(ins, outs, sems):
            cp.start()

    def finish(ins, outs, sems):
        for cp in copies(ins, outs, sems):
            cp.wait_send()
            cp.wait_recv()

    out_shapes = [jax.ShapeDtypeStruct((4,) + tuple(a.shape[1:]), a.dtype) for a in slots]
    sems = [pltpu.SemaphoreType.DMA((4 * n,)), pltpu.SemaphoreType.DMA((4 * n,))]
    return _Comm(list(slots), out_shapes, sems, start, finish)


def _chip_comm(chip_sums):
    n = len(chip_sums)

    def start(ins, outs, sems):
        send, recv, loc = sems
        x, y, c, chips = _place()
        mine = 2 * x + y
        for i in range(n):
            pltpu.make_async_copy(ins[i].at[mine], outs[i].at[mine], loc.at[i]).start()
            for j, (qx, qy) in enumerate(chips):
                _remote(ins[i].at[2 * qx + qy], outs[i].at[mine], send.at[3 * i + j], recv.at[3 * i + j],
                        (qx, qy, c)).start()

    def finish(ins, outs, sems):
        send, recv, loc = sems
        x, y, c, chips = _place()
        mine = 2 * x + y
        for i in range(n):
            for j, (qx, qy) in enumerate(chips):
                cp = _remote(ins[i].at[2 * qx + qy], outs[i].at[2 * qx + qy], send.at[3 * i + j], recv.at[3 * i + j],
                             (qx, qy, c))
                cp.wait_send()
                cp.wait_recv()
            pltpu.make_async_copy(ins[i].at[mine], outs[i].at[mine], loc.at[i]).wait()

    out_shapes = [jax.ShapeDtypeStruct(a.shape, a.dtype) for a in chip_sums]
    sems = [pltpu.SemaphoreType.DMA((3 * n,)), pltpu.SemaphoreType.DMA((3 * n,)), pltpu.SemaphoreType.DMA((n,))]
    return _Comm(list(chip_sums), out_shapes, sems, start, finish)


def _join_comm(a, b):
    na_i, na_o, na_s = len(a.inputs), len(a.out_shapes), len(a.sems)

    def start(ins, outs, sems):
        a.start(ins[:na_i], outs[:na_o], sems[:na_s])
        b.start(ins[na_i:], outs[na_o:], sems[na_s:])

    def finish(ins, outs, sems):
        a.finish(ins[:na_i], outs[:na_o], sems[:na_s])
        b.finish(ins[na_i:], outs[na_o:], sems[na_s:])

    return _Comm(a.inputs + b.inputs, a.out_shapes + b.out_shapes, a.sems + b.sems, start, finish)


def _row_tile(r):
    for cand in (256, 128):
        if r > cand and r % cand == 0:
            return cand
    return r


def _add_pairs(name, slots, sib):
    r, c = slots.shape[1:]
    tr = _row_tile(r)

    def body(s_ref, b_ref, o_ref):
        core = lax.axis_index("c")
        o_ref[...] = (s_ref[core].astype(F32) + b_ref[...].astype(F32)).astype(o_ref.dtype)

    return pl.pallas_call(
        body, name=name, grid=(4, r // tr),
        in_specs=[pl.BlockSpec((None, 2, tr, c), lambda q, i: (q, 0, i, 0)),
                  pl.BlockSpec((None, tr, c), lambda q, i: (q, i, 0))],
        out_specs=pl.BlockSpec((None, tr, c), lambda q, i: (q, i, 0)),
        out_shape=jax.ShapeDtypeStruct((4, r, c), slots.dtype),
        compiler_params=_params("parallel", "parallel"))(slots.reshape(4, 2, r, c), sib)


def _matmul(name, mode, a, b, grid, a_spec, b_spec, o_spec, out_shape, acc_shape,
            res=None, res_spec=None, alpha=1.0, comm=None):
    nk = grid[-1]
    has_res = res is not None

    def body(*refs):
        if has_res:
            a_ref, b_ref, r_ref, o_ref = refs[:4]
        else:
            a_ref, b_ref, o_ref = refs[:3]
            r_ref = None
        part = _dot(a_ref[...], b_ref[...], mode)

        def finish(v):
            if alpha != 1.0:
                v = v * alpha
            if has_res:
                v = r_ref[...] + v
            o_ref[...] = v.astype(o_ref.dtype)

        if nk == 1:
            finish(part)
        else:
            acc = refs[-1]
            k = pl.program_id(len(grid) - 1)

            @pl.when(k == 0)
            def _():
                acc[...] = part

            @pl.when(k > 0)
            def _():
                acc[...] += part

            @pl.when(k == nk - 1)
            def _():
                finish(acc[...])

    in_specs = [a_spec, b_spec] + ([res_spec] if has_res else [])
    args = (a, b) + ((res,) if has_res else ())
    scratch = [] if nk == 1 else [pltpu.VMEM(acc_shape, F32)]
    sem = ("parallel",) * (len(grid) - 1) + ("arbitrary",)
    out, couts = _pcall(name, body, grid, in_specs, o_spec, out_shape, args, scratch, sem, comm)
    return out if comm is None else (out, couts)


def _mm_nn(name, a, b, out_dtype=F32, res=None, alpha=1.0, tk=None):
    t, kk = a.shape
    n = b.shape[1]
    tk = kk if tk is None else tk
    grid = (t // TM, 1, kk // tk)
    return _matmul(
        name, "nn", a, b, grid,
        pl.BlockSpec((TM, tk), lambda i, j, k: (i, k)),
        pl.BlockSpec((tk, n), lambda i, j, k: (k, 0)),
        pl.BlockSpec((TM, n), lambda i, j, k: (i, 0)),
        jax.ShapeDtypeStruct((t, n), out_dtype), (TM, n),
        res=res, res_spec=pl.BlockSpec((TM, n), lambda i, j, k: (i, 0)), alpha=alpha)


def _mm_nt(name, a, b, out_dtype=F32, tk=None):
    t, kk = a.shape
    n = b.shape[0]
    tk = kk if tk is None else tk
    grid = (t // TM, 1, kk // tk)
    return _matmul(
        name, "nt", a, b, grid,
        pl.BlockSpec((TM, tk), lambda i, j, k: (i, k)),
        pl.BlockSpec((n, tk), lambda i, j, k: (0, k)),
        pl.BlockSpec((TM, n), lambda i, j, k: (i, 0)),
        jax.ShapeDtypeStruct((t, n), out_dtype), (TM, n))


def _mm_tn(name, a, b, out_dtype, tn=None):
    t, m = a.shape
    n = b.shape[1]
    tn = n if tn is None else tn
    grid = (n // tn, 1, t // TM)
    return _matmul(
        name, "tn", a, b, grid,
        pl.BlockSpec((TM, m), lambda j, i, k: (k, 0)),
        pl.BlockSpec((TM, tn), lambda j, i, k: (k, j)),
        pl.BlockSpec((m, tn), lambda j, i, k: (0, j)),
        jax.ShapeDtypeStruct((m, n), out_dtype), (m, tn))


def _ffn_in(name, h, w_in, comm=None):
    t = h.shape[0]
    grid = (N_DEV, t // TM, 1)
    return _matmul(
        name, "nn", h, w_in, grid,
        pl.BlockSpec((TM, D_MODEL), lambda d, i, k: (i, 0)),
        pl.BlockSpec((None, D_MODEL, FF_SHARD), lambda d, i, k: (d, 0, 0)),
        pl.BlockSpec((None, TM, FF_SHARD), lambda d, i, k: (d, i, 0)),
        jax.ShapeDtypeStruct((N_DEV, t, FF_SHARD), F32), None, comm=comm)


def _ffn_out(name, act, w_out, x, comm=None):
    t = x.shape[0]
    grid = (t // TM, 1, 4)
    return _matmul(
        name, "nn", act, w_out, grid,
        pl.BlockSpec((None, TM, FF_SHARD), lambda i, j, k: (k, i, 0)),
        pl.BlockSpec((None, FF_SHARD, D_MODEL), lambda i, j, k: (k, 0, 0)),
        pl.BlockSpec((TM, D_MODEL), lambda i, j, k: (i, 0)),
        jax.ShapeDtypeStruct((t, D_MODEL), F32), (TM, D_MODEL),
        res=x, res_spec=pl.BlockSpec((TM, D_MODEL), lambda i, j, k: (i, 0)), alpha=0.5, comm=comm)


def _ffn_out_bwd_act(name, dy, w_out):
    t = dy.shape[0]
    grid = (4, t // TM, 1)
    return _matmul(
        name, "nt", dy, w_out, grid,
        pl.BlockSpec((TM, D_MODEL), lambda d, i, k: (i, 0)),
        pl.BlockSpec((None, FF_SHARD, D_MODEL), lambda d, i, k: (d, 0, 0)),
        pl.BlockSpec((None, TM, FF_SHARD), lambda d, i, k: (d, i, 0)),
        jax.ShapeDtypeStruct((4, t, FF_SHARD), F32), None)


def _ffn_out_bwd_w(name, act, dy, out_dtype):
    t = dy.shape[0]
    grid = (4, 1, t // TM)
    return _matmul(
        name, "tn", act, dy, grid,
        pl.BlockSpec((None, TM, FF_SHARD), lambda d, j, k: (d, k, 0)),
        pl.BlockSpec((TM, D_MODEL), lambda d, j, k: (k, 0)),
        pl.BlockSpec((None, FF_SHARD, D_MODEL), lambda d, j, k: (d, 0, 0)),
        jax.ShapeDtypeStruct((4, FF_SHARD, D_MODEL), out_dtype), (FF_SHARD, D_MODEL))


def _ffn_in_bwd_h(name, dgu, w_in, comm=None):
    t = dgu.shape[1]
    grid = (t // TM, 1, N_DEV)
    return _matmul(
        name, "nt", dgu, w_in, grid,
        pl.BlockSpec((None, TM, FF_SHARD), lambda i, j, k: (k, i, 0)),
        pl.BlockSpec((None, D_MODEL, FF_SHARD), lambda i, j, k: (k, 0, 0)),
        pl.BlockSpec((TM, D_MODEL), lambda i, j, k: (i, 0)),
        jax.ShapeDtypeStruct((t, D_MODEL), F32), (TM, D_MODEL), comm=comm)


def _ffn_in_bwd_w(name, h, dgu, out_dtype):
    t = h.shape[0]
    grid = (N_DEV, 1, t // TM)
    return _matmul(
        name, "tn", h, dgu, grid,
        pl.BlockSpec((TM, D_MODEL), lambda d, j, k: (k, 0)),
        pl.BlockSpec((None, TM, FF_SHARD), lambda d, j, k: (d, k, 0)),
        pl.BlockSpec((None, D_MODEL, FF_SHARD), lambda d, j, k: (d, 0, 0)),
        jax.ShapeDtypeStruct((N_DEV, D_MODEL, FF_SHARD), out_dtype), (D_MODEL, FF_SHARD))


def _proj_in(name, h, wp, comm=None):
    t = h.shape[0]
    grid = (NP_COLS // P_BLOCK, t // TM, 1)
    return _matmul(
        name, "nn", h, wp, grid,
        pl.BlockSpec((TM, D_MODEL), lambda j, i, k: (i, 0)),
        pl.BlockSpec((D_MODEL, P_BLOCK), lambda j, i, k: (0, j)),
        pl.BlockSpec((TM, P_BLOCK), lambda j, i, k: (i, j)),
        jax.ShapeDtypeStruct((t, NP_COLS), F32), None, comm=comm)


def _rms_fwd(name, x, w):
    t, d = x.shape

    def body(x_ref, w_ref, h_ref):
        xv = x_ref[...]
        rstd = lax.rsqrt(jnp.mean(xv * xv, axis=-1, keepdims=True) + NORM_EPS)
        h_ref[...] = (xv * rstd * w_ref[...]).astype(h_ref.dtype)

    return pl.pallas_call(
        body, name=name, grid=(t // TE,),
        in_specs=[pl.BlockSpec((TE, d), lambda i: (i, 0)), pl.BlockSpec((1, d), lambda i: (0, 0))],
        out_specs=pl.BlockSpec((TE, d), lambda i: (i, 0)),
        out_shape=jax.ShapeDtypeStruct((t, d), BF), compiler_params=_params("parallel"))(x, w)


def _rms_bwd(name, x, w, dh, dres, out_scale):
    t, d = x.shape

    def body(x_ref, w_ref, dh_ref, dres_ref, dx_ref, dxb_ref, dw_ref):
        i = pl.program_id(0)
        xv = x_ref[...]
        rstd = lax.rsqrt(jnp.mean(xv * xv, axis=-1, keepdims=True) + NORM_EPS)
        xhat = xv * rstd
        dhv = dh_ref[...]
        wd = dhv * w_ref[...]
        proj = jnp.mean(wd * xhat, axis=-1, keepdims=True)
        dx = dres_ref[...] + rstd * (wd - xhat * proj)
        dx_ref[...] = dx
        dxb_ref[...] = (dx * out_scale).astype(BF)
        part = jnp.sum(dhv * xhat, axis=0, keepdims=True)

        @pl.when(i == 0)
        def _():
            dw_ref[...] = part

        @pl.when(i > 0)
        def _():
            dw_ref[...] += part

    row = pl.BlockSpec((TE, d), lambda i: (i, 0))
    vec = pl.BlockSpec((1, d), lambda i: (0, 0))
    return pl.pallas_call(
        body, name=name, grid=(t // TE,), in_specs=[row, vec, row, row], out_specs=[row, row, vec],
        out_shape=[jax.ShapeDtypeStruct((t, d), F32), jax.ShapeDtypeStruct((t, d), BF),
                   jax.ShapeDtypeStruct((1, d), F32)],
        compiler_params=_params("arbitrary"))(x, w, dh, dres)


def _final_loss(x, w, target):
    t, d = x.shape

    def body(x_ref, w_ref, t_ref, loss_ref, dx_ref, dxb_ref, dw_ref):
        i = pl.program_id(0)
        xv = x_ref[...]
        rstd = lax.rsqrt(jnp.mean(xv * xv, axis=-1, keepdims=True) + NORM_EPS)
        xhat = xv * rstd
        err = xhat * w_ref[...] - t_ref[...]
        lpart = 0.5 * jnp.sum(jnp.mean(err * err, axis=-1, keepdims=True), axis=0, keepdims=True)
        dy = err * (1.0 / d)
        wd = dy * w_ref[...]
        proj = jnp.mean(wd * xhat, axis=-1, keepdims=True)
        dx = rstd * (wd - xhat * proj)
        dx_ref[...] = dx
        dxb_ref[...] = (0.5 * dx).astype(BF)
        part = jnp.sum(dy * xhat, axis=0, keepdims=True)
        lfull = jnp.broadcast_to(lpart, (1, 128))

        @pl.when(i == 0)
        def _():
            dw_ref[...] = part
            loss_ref[...] = lfull

        @pl.when(i > 0)
        def _():
            dw_ref[...] += part
            loss_ref[...] += lfull

    row = pl.BlockSpec((TE, d), lambda i: (i, 0))
    vec = pl.BlockSpec((1, d), lambda i: (0, 0))
    return pl.pallas_call(
        body, name="final_loss", grid=(t // TE,), in_specs=[row, vec, row],
        out_specs=[pl.BlockSpec((1, 128), lambda i: (0, 0)), row, row, vec],
        out_shape=[jax.ShapeDtypeStruct((1, 128), F32), jax.ShapeDtypeStruct((t, d), F32),
                   jax.ShapeDtypeStruct((t, d), BF), jax.ShapeDtypeStruct((1, d), F32)],
        compiler_params=_params("arbitrary"))(x, w, target)


def _swiglu_fwd(name, gu):
    t = gu.shape[1]

    def body(g_ref, u_ref, a_ref):
        g = g_ref[...]
        a_ref[...] = (g * _sigmoid(g) * u_ref[...]).astype(BF)

    blk = (None, TE, FF_SHARD)
    return pl.pallas_call(
        body, name=name, grid=(4, t // TE),
        in_specs=[pl.BlockSpec(blk, lambda d, i: (d, i, 0)), pl.BlockSpec(blk, lambda d, i: (d + 4, i, 0))],
        out_specs=pl.BlockSpec(blk, lambda d, i: (d, i, 0)),
        out_shape=jax.ShapeDtypeStruct((4, t, FF_SHARD), BF),
        compiler_params=_params("parallel", "parallel"))(gu, gu)


def _swiglu_bwd(name, gu, dact, comm=None):
    t = gu.shape[1]

    def body(g_ref, u_ref, da_ref, o_ref):
        g = g_ref[...]
        da = da_ref[...]
        s = _sigmoid(g)
        o_ref[0] = (da * u_ref[...] * (s * (1.0 + g * (1.0 - s)))).astype(BF)
        o_ref[1] = (da * g * s).astype(BF)

    blk = (None, TE, FF_SHARD)
    out, couts = _pcall(
        name, body, (4, t // TE),
        [pl.BlockSpec(blk, lambda d, i: (d, i, 0)), pl.BlockSpec(blk, lambda d, i: (d + 4, i, 0)),
         pl.BlockSpec(blk, lambda d, i: (d, i, 0))],
        pl.BlockSpec((2, None, TE, FF_SHARD), lambda d, i: (0, d, i, 0)),
        jax.ShapeDtypeStruct((2, 4, t, FF_SHARD), BF), (gu, gu, dact), (), ("parallel", "parallel"), comm)
    out = out.reshape(N_DEV, t, FF_SHARD)
    return out if comm is None else (out, couts)


CONV_CB = 256


def _shift_down(v, s):
    if s == 0:
        return v
    row = lax.broadcasted_iota(jnp.int32, v.shape, 0)
    return jnp.where(row >= s, pltpu.roll(v, s, 0), 0.0)


def _shift_up(v, s):
    if s == 0:
        return v
    n = v.shape[0]
    row = lax.broadcasted_iota(jnp.int32, v.shape, 0)
    return jnp.where(row < n - s, pltpu.roll(v, n - s, 0), 0.0)


def _conv_fwd_val(q, w_ref, k):
    out = q * w_ref[k - 1:k, :]
    for j in range(k - 1):
        out = out + _shift_down(q, k - 1 - j) * w_ref[j:j + 1, :]
    return out


def _conv_bwd_val(q, dv, w_ref, k):
    dq = dv * w_ref[k - 1:k, :]
    dws = []
    for j in range(k - 1):
        dq = dq + _shift_up(dv, k - 1 - j) * w_ref[j:j + 1, :]
        dws.append(jnp.sum(dv * _shift_down(q, k - 1 - j), axis=0, keepdims=True))
    dws.append(jnp.sum(dv * q, axis=0, keepdims=True))
    return dq, dws


def _pspec(t, off):
    base = off // CONV_CB
    return pl.BlockSpec((t, CONV_CB), lambda j: (0, base + j))


def _mix_a_fwd(p, conv_w):
    t = p.shape[0]

    def body(b_ref, c_ref, xa_ref, w_ref, o_ref):
        q = c_ref[...] * xa_ref[...]
        o_ref[...] = (b_ref[...] * _conv_fwd_val(q, w_ref, 3)).astype(BF)

    return pl.pallas_call(
        body, name="mix_a_fwd", grid=(D_MODEL // CONV_CB,),
        in_specs=[_pspec(t, OFF_B), _pspec(t, OFF_C), _pspec(t, OFF_XA),
                  pl.BlockSpec((3, CONV_CB), lambda j: (0, j))],
        out_specs=pl.BlockSpec((t, CONV_CB), lambda j: (0, j)),
        out_shape=jax.ShapeDtypeStruct((t, D_MODEL), BF), compiler_params=_params("parallel"))(p, p, p, conv_w)


def _mix_a_bwd(p, conv_w, dya):
    t = p.shape[0]

    def body(b_ref, c_ref, xa_ref, w_ref, dy_ref, db_ref, dc_ref, dxa_ref, dw_ref):
        cv = c_ref[...]
        xav = xa_ref[...]
        q = cv * xav
        va = _conv_fwd_val(q, w_ref, 3)
        dyv = dy_ref[...]
        db_ref[...] = (dyv * va).astype(BF)
        dq, dws = _conv_bwd_val(q, dyv * b_ref[...], w_ref, 3)
        dc_ref[...] = (dq * xav).astype(BF)
        dxa_ref[...] = (dq * cv).astype(BF)
        for j in range(3):
            dw_ref[j:j + 1, :] = dws[j]

    col = pl.BlockSpec((t, CONV_CB), lambda j: (0, j))
    wsp = pl.BlockSpec((3, CONV_CB), lambda j: (0, j))
    return pl.pallas_call(
        body, name="mix_a_bwd", grid=(D_MODEL // CONV_CB,),
        in_specs=[_pspec(t, OFF_B), _pspec(t, OFF_C), _pspec(t, OFF_XA), wsp, col],
        out_specs=[col, col, col, wsp],
        out_shape=[jax.ShapeDtypeStruct((t, D_MODEL), BF)] * 3 + [jax.ShapeDtypeStruct((3, D_MODEL), F32)],
        compiler_params=_params("parallel"))(p, p, p, conv_w, dya)


def _ssm_conv_fwd(p, conv_w, conv_b):
    t = p.shape[0]

    def body(x_ref, w_ref, b_ref, o_ref):
        pre = _conv_fwd_val(x_ref[...], w_ref, 4) + b_ref[...]
        o_ref[...] = pre * _sigmoid(pre)

    return pl.pallas_call(
        body, name="ssm_conv_fwd", grid=(D_XBC // CONV_CB,),
        in_specs=[_pspec(t, OFF_XBC), pl.BlockSpec((4, CONV_CB), lambda j: (0, j)),
                  pl.BlockSpec((1, CONV_CB), lambda j: (0, j))],
        out_specs=pl.BlockSpec((t, CONV_CB), lambda j: (0, j)),
        out_shape=jax.ShapeDtypeStruct((t, D_XBC), F32), compiler_params=_params("parallel"))(p, conv_w, conv_b)


def _ssm_conv_bwd(p, conv_w, conv_b, dxc):
    t = p.shape[0]

    def body(x_ref, w_ref, b_ref, d_ref, dx_ref, dw_ref, db_ref):
        xv = x_ref[...]
        pre = _conv_fwd_val(xv, w_ref, 4) + b_ref[...]
        s = _sigmoid(pre)
        dpre = d_ref[...] * (s * (1.0 + pre * (1.0 - s)))
        dq, dws = _conv_bwd_val(xv, dpre, w_ref, 4)
        dx_ref[...] = dq.astype(BF)
        for j in range(4):
            dw_ref[j:j + 1, :] = dws[j]
        db_ref[...] = jnp.sum(dpre, axis=0, keepdims=True)

    col = pl.BlockSpec((t, CONV_CB), lambda j: (0, j))
    wsp = pl.BlockSpec((4, CONV_CB), lambda j: (0, j))
    bsp = pl.BlockSpec((1, CONV_CB), lambda j: (0, j))
    return pl.pallas_call(
        body, name="ssm_conv_bwd", grid=(D_XBC // CONV_CB,),
        in_specs=[_pspec(t, OFF_XBC), wsp, bsp, col], out_specs=[col, wsp, bsp],
        out_shape=[jax.ShapeDtypeStruct((t, D_XBC), BF), jax.ShapeDtypeStruct((4, D_XBC), F32),
                   jax.ShapeDtypeStruct((1, D_XBC), F32)],
        compiler_params=_params("parallel"))(p, conv_w, conv_b, dxc)


DT_ROWS = 512


def _tri(lower):
    r = lax.broadcasted_iota(jnp.int32, (CHUNK, CHUNK), 0)
    c = lax.broadcasted_iota(jnp.int32, (CHUNK, CHUNK), 1)
    return jnp.where((r >= c) if lower else (r <= c), 1.0, 0.0).astype(F32)


def _dot_exact(a, b):
    return lax.dot_general(a, b, _DIMS["nn"], preferred_element_type=F32, precision=lax.Precision.HIGHEST)


def _dt_fwd(p, bias_pad, alog_pad):
    t = p.shape[0]

    def body(raw_ref, b_ref, al_ref, dt_ref, acs_ref):
        z = raw_ref[...] + b_ref[...]
        dt = jnp.maximum(z, 0.0) + jnp.log(1.0 + jnp.exp(-jnp.abs(z)))
        dt_ref[...] = dt
        a = dt * (-jnp.exp(al_ref[...]))
        tri = _tri(True)
        for k in range(DT_ROWS // CHUNK):
            acs_ref[k * CHUNK:(k + 1) * CHUNK, :] = _dot_exact(tri, a[k * CHUNK:(k + 1) * CHUNK, :])

    blk = pl.BlockSpec((DT_ROWS, DT_W), lambda i: (i, 0))
    vec = pl.BlockSpec((1, DT_W), lambda i: (0, 0))
    return pl.pallas_call(
        body, name="dt_fwd", grid=(t // DT_ROWS,),
        in_specs=[pl.BlockSpec((DT_ROWS, DT_W), lambda i: (i, OFF_DT // DT_W)), vec, vec],
        out_specs=[blk, blk], out_shape=[jax.ShapeDtypeStruct((t, DT_W), F32)] * 2,
        compiler_params=_params("parallel"))(p, bias_pad, alog_pad)


def _dt_bwd(p, bias_pad, alog_pad, dt, ddt, dacs):
    t = p.shape[0]

    def body(raw_ref, b_ref, al_ref, dt_ref, ddt_ref, dacs_ref, draw_ref, db_ref, dal_ref):
        i = pl.program_id(0)
        acoef = -jnp.exp(al_ref[...])
        triu = _tri(False)
        das = []
        for k in range(DT_ROWS // CHUNK):
            das.append(_dot_exact(triu, dacs_ref[k * CHUNK:(k + 1) * CHUNK, :]))
        da = jnp.concatenate(das, axis=0)
        dtv = dt_ref[...]
        ddt_tot = ddt_ref[...] + da * acoef
        lane = lax.broadcasted_iota(jnp.int32, (DT_ROWS, DT_W), 1)
        draw = jnp.where(lane < N_HEADS, ddt_tot * _sigmoid(raw_ref[...] + b_ref[...]), 0.0)
        draw_ref[...] = draw.astype(BF)
        pb = jnp.sum(draw, axis=0, keepdims=True)
        pa = jnp.sum(da * dtv * acoef, axis=0, keepdims=True)

        @pl.when(i == 0)
        def _():
            db_ref[...] = pb
            dal_ref[...] = pa

        @pl.when(i > 0)
        def _():
            db_ref[...] += pb
            dal_ref[...] += pa

    blk = pl.BlockSpec((DT_ROWS, DT_W), lambda i: (i, 0))
    vec = pl.BlockSpec((1, DT_W), lambda i: (0, 0))
    return pl.pallas_call(
        body, name="dt_bwd", grid=(t // DT_ROWS,),
        in_specs=[pl.BlockSpec((DT_ROWS, DT_W), lambda i: (i, OFF_DT // DT_W)), vec, vec, blk, blk, blk],
        out_specs=[blk, vec, vec],
        out_shape=[jax.ShapeDtypeStruct((t, DT_W), BF), jax.ShapeDtypeStruct((1, DT_W), F32),
                   jax.ShapeDtypeStruct((1, DT_W), F32)],
        compiler_params=_params("arbitrary"))(p, bias_pad, alog_pad, dt, ddt, dacs)


def _expand(v, g, lane_head):
    h0 = HEADS_PER_GROUP * g
    out = jnp.broadcast_to(v[:, h0 + 3:h0 + 4], (CHUNK, GROUP_W))
    for j in (2, 1, 0):
        out = jnp.where(lane_head == j, jnp.broadcast_to(v[:, h0 + j:h0 + j + 1], (CHUNK, GROUP_W)), out)
    return out


def _expand_rows(v, g, row_head):
    h0 = HEADS_PER_GROUP * g
    out = jnp.broadcast_to(v[:, h0 + 3:h0 + 4], (GROUP_W, 1))
    for j in (2, 1, 0):
        out = jnp.where(row_head == j, jnp.broadcast_to(v[:, h0 + j:h0 + j + 1], (GROUP_W, 1)), out)
    return out


def _ssd_consts():
    r = lax.broadcasted_iota(jnp.int32, (CHUNK, CHUNK), 0)
    c = lax.broadcasted_iota(jnp.int32, (CHUNK, CHUNK), 1)
    lane_head = lax.broadcasted_iota(jnp.int32, (CHUNK, GROUP_W), 1) // HEAD_DIM
    row_head = lax.broadcasted_iota(jnp.int32, (GROUP_W, 1), 0) // HEAD_DIM
    return r >= c, r == c, lane_head, row_head


def _decay_mat(acsv, h, tri, eye):
    col = acsv[:, h:h + 1]
    row = jnp.sum(jnp.where(eye, col, 0.0), axis=0, keepdims=True)
    return jnp.where(tri, jnp.exp(jnp.minimum(col - row, 0.0)), 0.0)


def _ssd_fwd(xconv, dt, acs, d_exp, comm=None):
    t = xconv.shape[0]
    nc = t // CHUNK

    def body(xc_ref, dt_ref, acs_ref, d_ref, y_ref, hs_ref, state):
        c = pl.program_id(0)

        @pl.when(c == 0)
        def _():
            state[...] = jnp.zeros_like(state)

        hs_ref[...] = state[...]
        tri, eye, lane_head, row_head = _ssd_consts()
        dtv = dt_ref[...]
        acsv = acs_ref[...]
        atot = acsv[CHUNK - 1:CHUNK, :]
        e_all = jnp.exp(acsv)
        dec_all = jnp.exp(atot - acsv)
        eat = jnp.exp(atot)
        for g in range(N_GROUPS):
            gs = slice(GROUP_W * g, GROUP_W * (g + 1))
            xs_g = xc_ref[:, gs]
            b_g = xc_ref[:, D_INNER + D_STATE * g:D_INNER + D_STATE * (g + 1)].astype(BF)
            c_g = xc_ref[:, D_INNER + 1024 + D_STATE * g:D_INNER + 1024 + D_STATE * (g + 1)].astype(BF)
            gmat = _dot(c_g, b_g, "nt")
            x_g = xs_g * _expand(dtv, g, lane_head)
            h_g = state[gs, :]
            yoff = _dot(c_g, h_g.astype(BF), "nt") * _expand(e_all, g, lane_head)
            y_ref[:, gs] = yoff + d_ref[:, gs] * xs_g
            s_g = _dot((x_g * _expand(dec_all, g, lane_head)).astype(BF), b_g, "tn")
            state[gs, :] = _expand_rows(eat, g, row_head) * h_g + s_g
            for j in range(HEADS_PER_GROUP):
                h = HEADS_PER_GROUP * g + j
                m = gmat * _decay_mat(acsv, h, tri, eye)
                hs_ = slice(HEAD_DIM * h, HEAD_DIM * (h + 1))
                y_ref[:, hs_] += _dot(m.astype(BF), x_g[:, HEAD_DIM * j:HEAD_DIM * (j + 1)].astype(BF))

    blk = lambda w: pl.BlockSpec((CHUNK, w), lambda c: (c, 0))
    outs, couts = _pcall(
        "ssd_fwd", body, (nc,),
        [blk(D_XBC), blk(DT_W), blk(DT_W), pl.BlockSpec((1, D_INNER), lambda c: (0, 0))],
        [blk(D_INNER), pl.BlockSpec((None, D_INNER, D_STATE), lambda c: (c, 0, 0))],
        [jax.ShapeDtypeStruct((t, D_INNER), F32), jax.ShapeDtypeStruct((nc, D_INNER, D_STATE), F32)],
        (xconv, dt, acs, d_exp), [pltpu.VMEM((D_INNER, D_STATE), F32)], ("arbitrary",), comm)
    return outs if comm is None else (outs, couts)


def _ssd_bwd(xconv, dt, acs, d_exp, hsave, dy, comm=None):
    t = xconv.shape[0]
    nc = t // CHUNK

    def body(xc_ref, dt_ref, acs_ref, d_ref, hs_ref, dy_ref, dxc_ref, ddt_ref, dacs_ref, dd_ref, dstate, dx_scr):
        c = pl.program_id(0)

        @pl.when(c == 0)
        def _():
            dstate[...] = jnp.zeros_like(dstate)
            dd_ref[...] = jnp.zeros_like(dd_ref)

        tri, eye, lane_head, row_head = _ssd_consts()
        lane = lax.broadcasted_iota(jnp.int32, (CHUNK, DT_W), 1)
        lane1 = lax.broadcasted_iota(jnp.int32, (1, DT_W), 1)
        dtv = dt_ref[...]
        acsv = acs_ref[...]
        atot = acsv[CHUNK - 1:CHUNK, :]
        e_all = jnp.exp(acsv)
        dec_all = jnp.exp(atot - acsv)
        eat = jnp.exp(atot)
        ddt_acc = jnp.zeros((CHUNK, DT_W), F32)
        dacs_acc = jnp.zeros((CHUNK, DT_W), F32)
        datot_acc = jnp.zeros((1, DT_W), F32)

        def rsum(v):
            return jnp.sum(v, axis=1, keepdims=True)

        for g in range(N_GROUPS):
            gs = slice(GROUP_W * g, GROUP_W * (g + 1))
            bs = slice(D_INNER + D_STATE * g, D_INNER + D_STATE * (g + 1))
            cs = slice(D_INNER + 1024 + D_STATE * g, D_INNER + 1024 + D_STATE * (g + 1))
            xs_g = xc_ref[:, gs]
            b_g = xc_ref[:, bs].astype(BF)
            c_g = xc_ref[:, cs].astype(BF)
            gmat = _dot(c_g, b_g, "nt")
            dt_g = _expand(dtv, g, lane_head)
            dec_g = _expand(dec_all, g, lane_head)
            e_g = _expand(e_all, g, lane_head)
            x_g = xs_g * dt_g
            h_g = hs_ref[gs, :]
            h_b = h_g.astype(BF)
            dy_g = dy_ref[:, gs]
            ds_g = dstate[gs, :]
            ds_b = ds_g.astype(BF)

            yoff = _dot(c_g, h_b, "nt") * e_g
            edy = (e_g * dy_g).astype(BF)
            d_c = _dot(edy, h_b)
            d_hc = _dot(edy, c_g, "tn")
            bds = _dot(b_g, ds_b, "nt")
            xd = x_g * dec_g
            d_b = _dot(xd.astype(BF), ds_b)
            dx_scr[...] = dec_g * bds
            q_off = dy_g * yoff
            q_dec = xd * bds
            hh = ds_g * h_g
            d_g = jnp.zeros((CHUNK, CHUNK), F32)
            for j in range(HEADS_PER_GROUP):
                h = HEADS_PER_GROUP * g + j
                js = slice(HEAD_DIM * j, HEAD_DIM * (j + 1))
                lmat = _decay_mat(acsv, h, tri, eye)
                m = gmat * lmat
                dy_h = dy_g[:, js].astype(BF)
                dm = _dot(dy_h, x_g[:, js].astype(BF), "nt")
                dx_scr[:, js] += _dot(m.astype(BF), dy_h, "tn")
                d_g = d_g + dm * lmat
                w = dm * m
                cs_row = jnp.sum(w, axis=0, keepdims=True)
                cs_col = rsum(jnp.where(eye, cs_row, 0.0))
                t_h = rsum(q_dec[:, js])
                dacs_h = rsum(w) - cs_col + rsum(q_off[:, js]) - t_h
                hh_h = jnp.sum(rsum(hh[js, :]), axis=0, keepdims=True)
                datot_h = jnp.sum(t_h, axis=0, keepdims=True) + eat[:, h:h + 1] * hh_h
                dacs_acc = jnp.where(lane == h, dacs_h, dacs_acc)
                datot_acc = jnp.where(lane1 == h, datot_h, datot_acc)
            d_gb = d_g.astype(BF)
            dxc_ref[:, cs] = d_c + _dot(d_gb, b_g)
            dxc_ref[:, bs] = d_b + _dot(d_gb, c_g, "tn")
            dx_full = dx_scr[...]
            dxc_ref[:, gs] = dx_full * dt_g + d_ref[:, gs] * dy_g
            qx = dx_full * xs_g
            for j in range(HEADS_PER_GROUP):
                h = HEADS_PER_GROUP * g + j
                ddt_acc = jnp.where(lane == h, rsum(qx[:, HEAD_DIM * j:HEAD_DIM * (j + 1)]), ddt_acc)
            dd_ref[:, gs] += jnp.sum(dy_g * xs_g, axis=0, keepdims=True)
            dstate[gs, :] = _expand_rows(eat, g, row_head) * ds_g + d_hc

        rowi = lax.broadcasted_iota(jnp.int32, (CHUNK, DT_W), 0)
        ddt_ref[...] = ddt_acc
        dacs_ref[...] = dacs_acc + jnp.where(rowi == CHUNK - 1, datot_acc, 0.0)

    rev = lambda w: pl.BlockSpec((CHUNK, w), lambda c: (nc - 1 - c, 0))
    vec = pl.BlockSpec((1, D_INNER), lambda c: (0, 0))
    outs, couts = _pcall(
        "ssd_bwd", body, (nc,),
        [rev(D_XBC), rev(DT_W), rev(DT_W), vec,
         pl.BlockSpec((None, D_INNER, D_STATE), lambda c: (nc - 1 - c, 0, 0)), rev(D_INNER)],
        [rev(D_XBC), rev(DT_W), rev(DT_W), vec],
        [jax.ShapeDtypeStruct((t, D_XBC), F32), jax.ShapeDtypeStruct((t, DT_W), F32),
         jax.ShapeDtypeStruct((t, DT_W), F32), jax.ShapeDtypeStruct((1, D_INNER), F32)],
        (xconv, dt, acs, d_exp, hsave, dy),
        [pltpu.VMEM((D_INNER, D_STATE), F32), pltpu.VMEM((CHUNK, GROUP_W), F32)], ("arbitrary",), comm)
    return outs if comm is None else (outs, couts)


def _gnorm_fwd(y, p, w):
    t = y.shape[0]
    zoff = OFF_Z // GROUP_W

    def body(y_ref, z_ref, w_ref, o_ref):
        z = z_ref[...]
        yf = y_ref[...] * (z * _sigmoid(z))
        rstd = lax.rsqrt(jnp.mean(yf * yf, axis=-1, keepdims=True) + NORM_EPS)
        o_ref[...] = (yf * rstd * w_ref[...]).astype(BF)

    blk = pl.BlockSpec((TE, GROUP_W), lambda i, j: (i, j))
    return pl.pallas_call(
        body, name="gnorm_fwd", grid=(t // TE, N_GROUPS),
        in_specs=[blk, pl.BlockSpec((TE, GROUP_W), lambda i, j: (i, zoff + j)),
                  pl.BlockSpec((1, GROUP_W), lambda i, j: (0, j))],
        out_specs=blk, out_shape=jax.ShapeDtypeStruct((t, D_INNER), BF),
        compiler_params=_params("parallel", "parallel"))(y, p, w)


def _gnorm_bwd(y, p, w, dyn, comm=None):
    t = y.shape[0]
    zoff = OFF_Z // GROUP_W

    def body(y_ref, z_ref, w_ref, dn_ref, dy_ref, dz_ref, dw_ref):
        i = pl.program_id(1)
        z = z_ref[...]
        yv = y_ref[...]
        s = _sigmoid(z)
        sil = z * s
        yf = yv * sil
        rstd = lax.rsqrt(jnp.mean(yf * yf, axis=-1, keepdims=True) + NORM_EPS)
        xhat = yf * rstd
        dn = dn_ref[...]
        wd = dn * w_ref[...]
        proj = jnp.mean(wd * xhat, axis=-1, keepdims=True)
        dyf = rstd * (wd - xhat * proj)
        dy_ref[...] = dyf * sil
        dz_ref[...] = (dyf * yv * (s * (1.0 + z * (1.0 - s)))).astype(BF)
        part = jnp.sum(dn * xhat, axis=0, keepdims=True)

        @pl.when(i == 0)
        def _():
            dw_ref[...] = part

        @pl.when(i > 0)
        def _():
            dw_ref[...] += part

    blk = pl.BlockSpec((TE, GROUP_W), lambda j, i: (i, j))
    vec = pl.BlockSpec((1, GROUP_W), lambda j, i: (0, j))
    outs, couts = _pcall(
        "gnorm_bwd", body, (N_GROUPS, t // TE),
        [blk, pl.BlockSpec((TE, GROUP_W), lambda j, i: (i, zoff + j)), vec, blk],
        [blk, blk, vec],
        [jax.ShapeDtypeStruct((t, D_INNER), F32), jax.ShapeDtypeStruct((t, D_INNER), BF),
         jax.ShapeDtypeStruct((1, D_INNER), F32)],
        (y, p, w, dyn), (), ("parallel", "arbitrary"), comm)
    return outs if comm is None else (outs, couts)


MERGE_CB = 512


def _merge_fwd(p, ya, yb):
    t = ya.shape[0]

    def body(ga_ref, gb_ref, ya_ref, yb_ref, o_ref):
        o_ref[...] = (_sigmoid(ga_ref[...]) * ya_ref[...] + _sigmoid(gb_ref[...]) * yb_ref[...]).astype(BF)

    blk = pl.BlockSpec((TE, MERGE_CB), lambda i, j: (i, j))
    return pl.pallas_call(
        body, name="merge_fwd", grid=(t // TE, D_MODEL // MERGE_CB),
        in_specs=[pl.BlockSpec((TE, MERGE_CB), lambda i, j: (i, OFF_GA // MERGE_CB + j)),
                  pl.BlockSpec((TE, MERGE_CB), lambda i, j: (i, OFF_GB // MERGE_CB + j)), blk, blk],
        out_specs=blk, out_shape=jax.ShapeDtypeStruct((t, D_MODEL), BF),
        compiler_params=_params("parallel", "parallel"))(p, p, ya, yb)


def _merge_bwd(p, ya, yb, dm):
    t = ya.shape[0]

    def body(ga_ref, gb_ref, ya_ref, yb_ref, dm_ref, dga_ref, dgb_ref, dya_ref, dyb_ref):
        d = dm_ref[...]
        sa = _sigmoid(ga_ref[...])
        sb = _sigmoid(gb_ref[...])
        dga_ref[...] = (d * ya_ref[...] * sa * (1.0 - sa)).astype(BF)
        dgb_ref[...] = (d * yb_ref[...] * sb * (1.0 - sb)).astype(BF)
        dya_ref[...] = (d * sa).astype(BF)
        dyb_ref[...] = (d * sb).astype(BF)

    blk = pl.BlockSpec((TE, MERGE_CB), lambda i, j: (i, j))
    return pl.pallas_call(
        body, name="merge_bwd", grid=(t // TE, D_MODEL // MERGE_CB),
        in_specs=[pl.BlockSpec((TE, MERGE_CB), lambda i, j: (i, OFF_GA // MERGE_CB + j)),
                  pl.BlockSpec((TE, MERGE_CB), lambda i, j: (i, OFF_GB // MERGE_CB + j)), blk, blk, blk],
        out_specs=[blk] * 4, out_shape=[jax.ShapeDtypeStruct((t, D_MODEL), BF)] * 4,
        compiler_params=_params("parallel", "parallel"))(p, p, ya, yb, dm)


def _adamw(name, parts, w, m, v):
    r, c = w.shape
    tr = _row_tile(r)
    n_parts = parts.shape[0]
    bc1 = 1.0 - ADAM_B1 ** ADAM_STEP
    bc2 = 1.0 - ADAM_B2 ** ADAM_STEP

    def body(p_ref, w_ref, m_ref, v_ref, g_ref, d_ref, nm_ref, nv_ref):
        g = p_ref[0].astype(F32)
        for k in range(1, n_parts):
            g = g + p_ref[k].astype(F32)
        nm = ADAM_B1 * m_ref[...] + (1.0 - ADAM_B1) * g
        nv = ADAM_B2 * v_ref[...] + (1.0 - ADAM_B2) * (g * g)
        g_ref[...] = g
        nm_ref[...] = nm
        nv_ref[...] = nv
        d_ref[...] = -ADAM_LR * ((nm / bc1) / (jnp.sqrt(nv / bc2) + ADAM_EPS) + ADAM_WD * w_ref[...])

    blk = pl.BlockSpec((tr, c), lambda i: (i, 0))
    return pl.pallas_call(
        body, name=name, grid=(r // tr,),
        in_specs=[pl.BlockSpec((n_parts, tr, c), lambda i: (0, i, 0)), blk, blk, blk],
        out_specs=[blk] * 4, out_shape=[jax.ShapeDtypeStruct((r, c), F32)] * 4,
        compiler_params=_params("parallel"))(parts, w, m, v)


_SECTIONS = [(0, 1024), (1024, 2048), (2048, 3072), (3072, 5120), (5120, 9216), (9248, 10272), (10272, 11296),
             (9216, 9248)]


def _to_padded_cols(w_full):
    parts = [w_full[:, a:b] for a, b in _SECTIONS]
    parts.append(jnp.zeros((w_full.shape[0], NP_COLS - N_IN), w_full.dtype))
    return jnp.concatenate(parts, axis=1)


def _from_padded_cols(g):
    return jnp.concatenate(
        [g[:, OFF_B:OFF_Z + 2048], g[:, OFF_XBC:OFF_XBC + 4096], g[:, OFF_DT:OFF_DT + N_HEADS],
         g[:, OFF_GA:OFF_GA + 2048]], axis=1)


def _pad_lanes(v, width):
    return jnp.pad(v, ((0, 0), (0, width - v.shape[1])))


def _reduce_start(slots, host):
    outs, sib = host(_pair_comm([a for _, a in slots]))
    sums = [(n, _add_pairs("pairsum_" + n, a, b)) for (n, a), b in zip(slots, sib)]
    return outs, sums


def _train_step(x, target, shard, rep):
    gdt = BF
    t = x.shape[0]
    recv = {}
    (w1_in,) = _comm_call("gather_ffn1_in", _gather_comm([shard["ffn1_w_in"]]))
    h1 = _rms_fwd("rms1_fwd", x, rep["ffn1_norm"])
    gu1, got = _ffn_in("ffn1_in", h1, w1_in, comm=_gather_comm(
        [shard["ffn1_w_out"], shard["w_in"], shard["short_conv_w"], shard["ssm_conv_w"]]))
    w1_out = got[0].reshape(4, FF_SHARD, D_MODEL)
    wp = _to_padded_cols(got[1].transpose(1, 0, 2).reshape(D_MODEL, N_IN))
    short_conv_w = got[2].transpose(1, 0, 2).reshape(3, D_MODEL)
    ssm_conv_w = got[3].transpose(1, 0, 2).reshape(4, D_XBC)
    act1 = _swiglu_fwd("swiglu1_fwd", gu1)
    x1, got = _ffn_out("ffn1_out", act1, w1_out, x, comm=_gather_comm(
        [shard["short_w_out"], shard["ssm_w_out"], shard["w_out"]]))
    short_w_out = got[0].reshape(D_MODEL, D_MODEL)
    ssm_w_out = got[1].reshape(D_INNER, D_MODEL)
    w_out = got[2].reshape(D_MODEL, D_MODEL)

    h2 = _rms_fwd("rms2_fwd", x1, rep["mix_norm"])
    p, (w2_in,) = _proj_in("proj_in", h2, wp, comm=_gather_comm([shard["ffn2_w_in"]]))
    ya_in = _mix_a_fwd(p, short_conv_w)
    y_a = _mm_nn("short_out", ya_in, short_w_out)
    xconv = _ssm_conv_fwd(p, ssm_conv_w, rep["ssm_conv_b"])
    dt, acs = _dt_fwd(p, rep["dt_bias_pad"], rep["a_log_pad"])
    (y_ssm, hsave), got = _ssd_fwd(xconv, dt, acs, rep["d_exp"], comm=_gather_comm([shard["ffn2_w_out"]]))
    w2_out = got[0].reshape(4, FF_SHARD, D_MODEL)
    yn = _gnorm_fwd(y_ssm, p, rep["ssm_norm"])
    y_b = _mm_nn("ssm_out", yn, ssm_w_out, tk=1024)
    merged = _merge_fwd(p, y_a, y_b)
    x2 = _mm_nn("mix_out", merged, w_out, res=x1)

    h3 = _rms_fwd("rms3_fwd", x2, rep["ffn2_norm"])
    gu2 = _ffn_in("ffn2_in", h3, w2_in)
    act2 = _swiglu_fwd("swiglu2_fwd", gu2)
    x3 = _ffn_out("ffn2_out", act2, w2_out, x2)

    loss, dx3, dx3h, g_final = _final_loss(x3, rep["final_norm"], target)

    small = {"final_norm": g_final}
    dact2 = _ffn_out_bwd_act("ffn2_out_bwd_act", dx3h, w2_out)
    g_w2_out = _ffn_out_bwd_w("ffn2_out_bwd_w", act2, dx3h, gdt)
    dgu2 = _swiglu_bwd("swiglu2_bwd", gu2, dact2)
    g_w2_in = _ffn_in_bwd_w("ffn2_in_bwd_w", h3, dgu2, gdt)
    dh3 = _ffn_in_bwd_h("ffn2_in_bwd_h", dgu2, w2_in)
    dx2, dx2b, small["ffn2_norm"] = _rms_bwd("rms3_bwd", x2, rep["ffn2_norm"], dh3, dx3, 1.0)

    dmerged = _mm_nt("mix_out_bwd_x", dx2b, w_out)
    g_w_out = _mm_tn("mix_out_bwd_w", merged, dx2b, gdt)
    dga, dgb, dya, dyb = _merge_bwd(p, y_a, y_b, dmerged)

    dya_in = _mm_nt("short_out_bwd_x", dya, short_w_out)
    g_short_w_out = _mm_tn("short_out_bwd_w", ya_in, dya, gdt)
    db, dc, dxa, g_short_conv = _mix_a_bwd(p, short_conv_w, dya_in)

    dyn = _mm_nt("ssm_out_bwd_x", dyb, ssm_w_out)
    g_ssm_w_out = _mm_tn("ssm_out_bwd_w", yn, dyb, gdt)
    late = [("ffn2_w_out", g_w2_out.reshape(N_DEV, FF_SHARD // 2, D_MODEL)), ("ffn2_w_in", g_w2_in),
            ("w_out", g_w_out.reshape(N_DEV, -1, D_MODEL)), ("short_w_out", g_short_w_out.reshape(N_DEV, -1, D_MODEL)),
            ("ssm_w_out", g_ssm_w_out.reshape(N_DEV, -1, D_MODEL))]
    (dy_ssm, dz, small["ssm_norm"]), sums = _reduce_start(
        late, lambda comm: _gnorm_bwd(y_ssm, p, rep["ssm_norm"], dyn, comm=comm))
    (dxconv, ddt, dacs, dd_lane), got = _ssd_bwd(xconv, dt, acs, rep["d_exp"], hsave, dy_ssm,
                                                  comm=_chip_comm([a for _, a in sums]))
    recv.update({n: a for (n, _), a in zip(sums, got)})
    small["ssm_D"] = dd_lane.reshape(N_HEADS, HEAD_DIM).sum(axis=1)[None, :]
    dxbc, g_ssm_conv, small["ssm_conv_b"] = _ssm_conv_bwd(p, ssm_conv_w, rep["ssm_conv_b"], dxconv)
    draw, dbias, dalog = _dt_bwd(p, rep["dt_bias_pad"], rep["a_log_pad"], dt, ddt, dacs)
    small["ssm_dt_bias"] = dbias[:, :N_HEADS]
    small["ssm_A_log"] = dalog[:, :N_HEADS]

    dp = jnp.concatenate(
        [db, dc, dxa, dz, dxbc, dga, dgb, draw, jnp.zeros((t, NP_COLS - OFF_DT - DT_W), BF)], axis=1)
    dh2 = _mm_nt("proj_in_bwd_x", dp, wp, tk=P_BLOCK)
    g_wp = _mm_tn("proj_in_bwd_w", h2, dp, gdt, tn=P_BLOCK)
    dx1, dx1h, small["mix_norm"] = _rms_bwd("rms2_bwd", x1, rep["mix_norm"], dh2, dx2, 0.5)

    g_w1_out = _ffn_out_bwd_w("ffn1_out_bwd_w", act1, dx1h, gdt)
    dact1 = _ffn_out_bwd_act("ffn1_out_bwd_act", dx1h, w1_out)
    mid = [("w_in", _from_padded_cols(g_wp).reshape(D_MODEL, N_DEV, IN_SHARD).transpose(1, 0, 2)),
           ("short_conv_w", g_short_conv.reshape(3, N_DEV, -1).transpose(1, 0, 2)),
           ("ssm_conv_w", g_ssm_conv.reshape(4, N_DEV, -1).transpose(1, 0, 2)),
           ("ffn1_w_out", g_w1_out.reshape(N_DEV, FF_SHARD // 2, D_MODEL))]
    dgu1, sums = _reduce_start(mid, lambda comm: _swiglu_bwd("swiglu1_bwd", gu1, dact1, comm=comm))
    g_w1_in = _ffn_in_bwd_w("ffn1_in_bwd_w", h1, dgu1, gdt)
    dh1, got = _ffn_in_bwd_h("ffn1_in_bwd_h", dgu1, w1_in, comm=_chip_comm([a for _, a in sums]))
    recv.update({n: a for (n, _), a in zip(sums, got)})
    dx0, _, small["ffn1_norm"] = _rms_bwd("rms1_bwd", x, rep["ffn1_norm"], dh1, dx1, 1.0)

    _, sums = _reduce_start([("ffn1_w_in", g_w1_in)], lambda comm: (None, _comm_call("pair_last", comm)))
    packed = _pack_small(small, loss[:, 0:1])
    got = _comm_call("exchange_last", _join_comm(_chip_comm([sums[0][1]]), _gather_comm([packed])))
    recv["ffn1_w_in"] = got[0]
    return dx0, recv, got[1]


_SMALL = [("ffn1_norm", 1024), ("mix_norm", 1024), ("ssm_conv_b", 4096), ("ssm_dt_bias", 32), ("ssm_A_log", 32),
          ("ssm_D", 32), ("ssm_norm", 2048), ("ffn2_norm", 1024), ("final_norm", 1024)]
SMALL_W = 10368


def _pack_small(d, loss=None):
    parts = [d[n].reshape(1, -1).astype(F32) for n, _ in _SMALL]
    used = sum(sz for _, sz in _SMALL)
    tail = jnp.zeros((1, SMALL_W - used), F32)
    if loss is not None:
        tail = tail.at[:, 0:1].set(loss)
    return jnp.concatenate(parts + [tail], axis=1)


def _unpack_small(v, shapes):
    out, off = {}, 0
    for n, sz in _SMALL:
        out[n] = v[:, off:off + sz].reshape(shapes[n])
        off += sz
    return out, v[0, off]


_SHARDED = ["ffn1_w_in", "ffn1_w_out", "w_in", "short_conv_w", "short_w_out", "ssm_conv_w", "ssm_w_out", "w_out",
            "ffn2_w_in", "ffn2_w_out"]
_ORDER = ["ffn1_norm", "ffn1_w_in", "ffn1_w_out", "mix_norm", "w_in", "short_conv_w", "short_w_out", "ssm_conv_w",
          "ssm_conv_b", "ssm_dt_bias", "ssm_A_log", "ssm_D", "ssm_norm", "ssm_w_out", "w_out", "ffn2_norm",
          "ffn2_w_in", "ffn2_w_out", "final_norm"]


def kernel(x, ffn1_norm, ffn1_w_in, ffn1_w_out, mix_norm, w_in, short_conv_w, short_w_out, ssm_conv_w, ssm_conv_b, ssm_dt_bias, ssm_A_log, ssm_D, ssm_norm, ssm_w_out, w_out, ffn2_norm, ffn2_w_in, ffn2_w_out, final_norm, loss_target, m_ffn1_norm, m_ffn1_w_in, m_ffn1_w_out, m_mix_norm, m_w_in, m_short_conv_w, m_short_w_out, m_ssm_conv_w, m_ssm_conv_b, m_ssm_dt_bias, m_ssm_A_log, m_ssm_D, m_ssm_norm, m_ssm_w_out, m_w_out, m_ffn2_norm, m_ffn2_w_in, m_ffn2_w_out, m_final_norm, v_ffn1_norm, v_ffn1_w_in, v_ffn1_w_out, v_mix_norm, v_w_in, v_short_conv_w, v_short_w_out, v_ssm_conv_w, v_ssm_conv_b, v_ssm_dt_bias, v_ssm_A_log, v_ssm_D, v_ssm_norm, v_ssm_w_out, v_w_out, v_ffn2_norm, v_ffn2_w_in, v_ffn2_w_out, v_final_norm):
    w = dict(ffn1_norm=ffn1_norm, ffn1_w_in=ffn1_w_in, ffn1_w_out=ffn1_w_out, mix_norm=mix_norm, w_in=w_in,
             short_conv_w=short_conv_w, short_w_out=short_w_out, ssm_conv_w=ssm_conv_w, ssm_conv_b=ssm_conv_b,
             ssm_dt_bias=ssm_dt_bias, ssm_A_log=ssm_A_log, ssm_D=ssm_D, ssm_norm=ssm_norm, ssm_w_out=ssm_w_out,
             w_out=w_out, ffn2_norm=ffn2_norm, ffn2_w_in=ffn2_w_in, ffn2_w_out=ffn2_w_out, final_norm=final_norm)
    m = dict(ffn1_norm=m_ffn1_norm, ffn1_w_in=m_ffn1_w_in, ffn1_w_out=m_ffn1_w_out, mix_norm=m_mix_norm, w_in=m_w_in,
             short_conv_w=m_short_conv_w, short_w_out=m_short_w_out, ssm_conv_w=m_ssm_conv_w,
             ssm_conv_b=m_ssm_conv_b, ssm_dt_bias=m_ssm_dt_bias, ssm_A_log=m_ssm_A_log, ssm_D=m_ssm_D,
             ssm_norm=m_ssm_norm, ssm_w_out=m_ssm_w_out, w_out=m_w_out, ffn2_norm=m_ffn2_norm,
             ffn2_w_in=m_ffn2_w_in, ffn2_w_out=m_ffn2_w_out, final_norm=m_final_norm)
    v = dict(ffn1_norm=v_ffn1_norm, ffn1_w_in=v_ffn1_w_in, ffn1_w_out=v_ffn1_w_out, mix_norm=v_mix_norm, w_in=v_w_in,
             short_conv_w=v_short_conv_w, short_w_out=v_short_w_out, ssm_conv_w=v_ssm_conv_w,
             ssm_conv_b=v_ssm_conv_b, ssm_dt_bias=v_ssm_dt_bias, ssm_A_log=v_ssm_A_log, ssm_D=v_ssm_D,
             ssm_norm=v_ssm_norm, ssm_w_out=v_ssm_w_out, w_out=v_w_out, ffn2_norm=v_ffn2_norm,
             ffn2_w_in=v_ffn2_w_in, ffn2_w_out=v_ffn2_w_out, final_norm=v_final_norm)
    shapes = {n: w[n].shape for n in _ORDER}
    shard = {n: w[n][0] for n in _SHARDED}

    wire = {n: (shard[n] if n in ("short_conv_w", "ssm_conv_w") else shard[n].astype(BF)) for n in _SHARDED}
    rep = {
        "ffn1_norm": ffn1_norm, "mix_norm": mix_norm, "ffn2_norm": ffn2_norm, "ssm_norm": ssm_norm,
        "ssm_conv_b": ssm_conv_b, "final_norm": final_norm.reshape(1, D_MODEL),
        "dt_bias_pad": _pad_lanes(ssm_dt_bias, DT_W), "a_log_pad": _pad_lanes(ssm_A_log, DT_W),
        "d_exp": jnp.repeat(ssm_D, HEAD_DIM, axis=1),
    }
    grad_x, parts, small_parts = _train_step(x[0], loss_target[0], wire, rep)

    out_g, out_d, out_m, out_v = {}, {}, {}, {}
    for n in _SHARDED:
        res = _adamw("adamw_" + n, parts[n], shard[n], m[n][0], v[n][0])
        out_g[n], out_d[n], out_m[n], out_v[n] = [r.reshape(shapes[n]) for r in res]
    sres = _adamw("adamw_small", small_parts, _pack_small(w), _pack_small(m), _pack_small(v))
    sg, loss = _unpack_small(sres[0], shapes)
    sd, _ = _unpack_small(sres[1], shapes)
    sm, _ = _unpack_small(sres[2], shapes)
    sv, _ = _unpack_small(sres[3], shapes)
    out_g.update(sg)
    out_d.update(sd)
    out_m.update(sm)
    out_v.update(sv)
    return (loss, grad_x[None], *[out_g[n] for n in _ORDER], *[out_d[n] for n in _ORDER],
            *[out_m[n] for n in _ORDER], *[out_v[n] for n in _ORDER])
```

```python
import functools

import jax
import jax.numpy as jnp
from jax import lax
from jax.experimental import pallas as pl
from jax.experimental.pallas import tpu as pltpu

F32 = jnp.float32
BF = jnp.bfloat16

N_DEV = 8
D_MODEL = 1024
D_FF = 2816
D_INNER = 2048
D_XBC = 4096
N_HEADS = 32
HEAD_DIM = 64
N_GROUPS = 8
D_STATE = 128
CHUNK = 64
GROUP_W = D_INNER // N_GROUPS
HEADS_PER_GROUP = N_HEADS // N_GROUPS
NORM_EPS = 1e-5
N_IN = 11296
FF_SHARD = 2 * D_FF // N_DEV
IN_SHARD = N_IN // N_DEV

OFF_B, OFF_C, OFF_XA, OFF_Z, OFF_XBC, OFF_GA, OFF_GB, OFF_DT = 0, 1024, 2048, 3072, 5120, 9216, 10240, 11264
NP_COLS = 11520
P_BLOCK = 1280
DT_W = 128
W_IN_ROW_CUTS = [(0, 256), (256, 640), (640, 1024)]

ADAM_LR, ADAM_B1, ADAM_B2, ADAM_EPS, ADAM_WD, ADAM_STEP = 0.001, 0.9, 0.999, 1e-08, 0.01, 10

VMEM_LIMIT_V7X = 56 * 1024 * 1024
TM = 512
TE = 256


def _params(*sem):
    return pltpu.CompilerParams(dimension_semantics=sem, vmem_limit_bytes=VMEM_LIMIT_V7X)


_DIMS = {
    "nn": (((1,), (0,)), ((), ())),
    "nt": (((1,), (1,)), ((), ())),
    "tn": (((0,), (0,)), ((), ())),
}


def _dot(a, b, mode="nn"):
    return lax.dot_general(a, b, _DIMS[mode], preferred_element_type=F32)


def _sigmoid(x):
    return 1.0 / (1.0 + jnp.exp(-x))


class _Comm:
    def __init__(self, inputs, out_shapes, sems, start, finish):
        self.inputs, self.out_shapes, self.sems, self.start, self.finish = inputs, out_shapes, sems, start, finish


def _pcall(name, body, grid, in_specs, out_specs, out_shape, args, scratch=(), sem=None, comm=None):
    single = not isinstance(out_shape, (list, tuple))
    out_shapes = [out_shape] if single else list(out_shape)
    out_specs = [out_specs] if single else list(out_specs)
    n_in, n_out, n_scr = len(args), len(out_shapes), len(scratch)
    if comm is None:
        res = pl.pallas_call(
            body, name=name, grid=grid, in_specs=list(in_specs), out_specs=out_specs, out_shape=out_shapes,
            scratch_shapes=list(scratch), compiler_params=_params(*sem))(*args)
        return (res[0] if single else res), []
    nci, nco = len(comm.inputs), len(comm.out_shapes)

    def wrapped(*refs):
        a = refs[:n_in]
        ci = refs[n_in:n_in + nci]
        o0 = n_in + nci
        o = refs[o0:o0 + n_out]
        co = refs[o0 + n_out:o0 + n_out + nco]
        s0 = o0 + n_out + nco
        s = refs[s0:s0 + n_scr]
        cs = refs[s0 + n_scr:]
        pids = [pl.program_id(i) for i in range(len(grid))]
        first = functools.reduce(jnp.logical_and, [p == 0 for p in pids])
        last = functools.reduce(jnp.logical_and, [p == g - 1 for p, g in zip(pids, grid)])

        @pl.when(first)
        def _():
            comm.start(ci, co, cs)

        body(*a, *o, *s)

        @pl.when(last)
        def _():
            comm.finish(ci, co, cs)

    any_spec = pl.BlockSpec(memory_space=pl.ANY)
    res = pl.pallas_call(
        wrapped, name=name, grid=grid, in_specs=list(in_specs) + [any_spec] * nci,
        out_specs=out_specs + [any_spec] * nco, out_shape=out_shapes + list(comm.out_shapes),
        scratch_shapes=list(scratch) + list(comm.sems),
        compiler_params=_params(*(("arbitrary",) * len(grid))))(*args, *comm.inputs)
    core = res[:n_out]
    return (core[0] if single else core), list(res[n_out:])


def _comm_call(name, comm):
    nci, nco = len(comm.inputs), len(comm.out_shapes)

    def body(*refs):
        ci, co, cs = refs[:nci], refs[nci:nci + nco], refs[nci + nco:]
        comm.start(ci, co, cs)
        comm.finish(ci, co, cs)

    any_spec = pl.BlockSpec(memory_space=pl.ANY)
    return pl.pallas_call(
        body, name=name, in_specs=[any_spec] * nci, out_specs=[any_spec] * nco, out_shape=list(comm.out_shapes),
        scratch_shapes=list(comm.sems), compiler_params=pltpu.CompilerParams(has_side_effects=True))(*comm.inputs)


def _remote(src, dst, ssem, rsem, dev):
    return pltpu.make_async_remote_copy(src_ref=src, dst_ref=dst, send_sem=ssem, recv_sem=rsem, device_id=dev,
                                        device_id_type=pl.DeviceIdType.MESH)


def _place():
    x, y, c = lax.axis_index("x"), lax.axis_index("y"), lax.axis_index("c")
    other_chips = [(1 - x, y), (x, 1 - y), (1 - x, 1 - y)]
    return x, y, c, other_chips


def _gather_comm(shards):
    n = len(shards)
    per = N_DEV - 1

    def start(ins, outs, sems):
        send, recv, loc = sems
        x, y, c, chips = _place()
        me = 4 * x + 2 * y + c
        for i in range(n):
            pltpu.make_async_copy(ins[i], outs[i].at[me], loc.at[i]).start()
            _remote(ins[i], outs[i].at[me], send.at[per * i], recv.at[per * i], (x, y, 1 - c)).start()
            for j, (qx, qy) in enumerate(chips):
                _remote(ins[i], outs[i].at[me], send.at[per * i + 1 + j], recv.at[per * i + 1 + j], (qx, qy, c)).start()

    def finish(ins, outs, sems):
        send, recv, loc = sems
        x, y, c, chips = _place()
        me = 4 * x + 2 * y + c
        sib = (x, y, 1 - c)
        for i in range(n):
            for j, (qx, qy) in enumerate(chips):
                blk = outs[i].at[4 * qx + 2 * qy + c]
                _remote(blk, blk, send.at[per * i + 1 + j], recv.at[per * i + 1 + j], (qx, qy, c)).wait_recv()
                _remote(blk, blk, send.at[per * i + 4 + j], recv.at[per * i + 4 + j], sib).start()
        for i in range(n):
            blk = outs[i].at[4 * x + 2 * y + 1 - c]
            _remote(blk, blk, send.at[per * i], recv.at[per * i], sib).wait_recv()
            for j, (qx, qy) in enumerate(chips):
                blk = outs[i].at[4 * qx + 2 * qy + 1 - c]
                _remote(blk, blk, send.at[per * i + 4 + j], recv.at[per * i + 4 + j], sib).wait_recv()
        for i in range(n):
            own = outs[i].at[me]
            for k in range(per):
                _remote(ins[i], own, send.at[per * i + k], recv.at[per * i + k], sib).wait_send()
            pltpu.make_async_copy(ins[i], own, loc.at[i]).wait()

    out_shapes = [jax.ShapeDtypeStruct((N_DEV,) + tuple(a.shape), a.dtype) for a in shards]
    sems = [pltpu.SemaphoreType.DMA((per * n,)), pltpu.SemaphoreType.DMA((per * n,)), pltpu.SemaphoreType.DMA((n,))]
    return _Comm(list(shards), out_shapes, sems, start, finish)


def _pair_comm(slots):
    n = len(slots)

    def copies(ins, outs, sems):
        send, recv = sems
        x, y, c, _ = _place()
        sib = (x, y, 1 - c)
        out = []
        for i in range(n):
            for q in range(4):
                out.append(_remote(ins[i].at[2 * q + 1 - c], outs[i].at[q], send.at[4 * i + q], recv.at[4 * i + q], sib))
        return out

    def start(ins, outs, sems):
        for cp in copies(ins, outs, sems):
            cp.start()

    def finish(ins, outs, sems):
        for cp in copies(ins, outs, sems):
            cp.wait_send()
            cp.wait_recv()

    out_shapes = [jax.ShapeDtypeStruct((4,) + tuple(a.shape[1:]), a.dtype) for a in slots]
    sems = [pltpu.SemaphoreType.DMA((4 * n,)), pltpu.SemaphoreType.DMA((4 * n,))]
    return _Comm(list(slots), out_shapes, sems, start, finish)


def _chip_comm(chip_sums):
    n = len(chip_sums)

    def start(ins, outs, sems):
        send, recv, loc = sems
        x, y, c, chips = _place()
        mine = 2 * x + y
        for i in range(n):
            pltpu.make_async_copy(ins[i].at[mine], outs[i].at[mine], loc.at[i]).start()
            for j, (qx, qy) in enumerate(chips):
                _remote(ins[i].at[2 * qx + qy], outs[i].at[mine], send.at[3 * i + j], recv.at[3 * i + j],
                        (qx, qy, c)).start()

    def finish(ins, outs, sems):
        send, recv, loc = sems
        x, y, c, chips = _place()
        mine = 2 * x + y
        for i in range(n):
            for j, (qx, qy) in enumerate(chips):
                cp = _remote(ins[i].at[2 * qx + qy], outs[i].at[2 * qx + qy], send.at[3 * i + j], recv.at[3 * i + j],
                             (qx, qy, c))
                cp.wait_send()
                cp.wait_recv()
            pltpu.make_async_copy(ins[i].at[mine], outs[i].at[mine], loc.at[i]).wait()

    out_shapes = [jax.ShapeDtypeStruct(a.shape, a.dtype) for a in chip_sums]
    sems = [pltpu.SemaphoreType.DMA((3 * n,)), pltpu.SemaphoreType.DMA((3 * n,)), pltpu.SemaphoreType.DMA((n,))]
    return _Comm(list(chip_sums), out_shapes, sems, start, finish)


def _join_comm(a, b):
    na_i, na_o, na_s = len(a.inputs), len(a.out_shapes), len(a.sems)

    def start(ins, outs, sems):
        a.start(ins[:na_i], outs[:na_o], sems[:na_s])
        b.start(ins[na_i:], outs[na_o:], sems[na_s:])

    def finish(ins, outs, sems):
        a.finish(ins[:na_i], outs[:na_o], sems[:na_s])
        b.finish(ins[na_i:], outs[na_o:], sems[na_s:])

    return _Comm(a.inputs + b.inputs, a.out_shapes + b.out_shapes, a.sems + b.sems, start, finish)


def _row_tile(r):
    for cand in (256, 128):
        if r > cand and r % cand == 0:
            return cand
    return r


def _add_pairs(name, slots, sib):
    r, c = slots.shape[1:]
    tr = _row_tile(r)

    def body(s_ref, b_ref, o_ref):
        core = lax.axis_index("c")
        o_ref[...] = (s_ref[core].astype(F32) + b_ref[...].astype(F32)).astype(o_ref.dtype)

    return pl.pallas_call(
        body, name=name, grid=(4, r // tr),
        in_specs=[pl.BlockSpec((None, 2, tr, c), lambda q, i: (q, 0, i, 0)),
                  pl.BlockSpec((None, tr, c), lambda q, i: (q, i, 0))],
        out_specs=pl.BlockSpec((None, tr, c), lambda q, i: (q, i, 0)),
        out_shape=jax.ShapeDtypeStruct((4, r, c), slots.dtype),
        compiler_params=_params("parallel", "parallel"))(slots.reshape(4, 2, r, c), sib)


def _matmul(name, mode, a, b, grid, a_spec, b_spec, o_spec, out_shape, acc_shape,
            res=None, res_spec=None, alpha=1.0, comm=None):
    nk = grid[-1]
    has_res = res is not None

    def body(*refs):
        if has_res:
            a_ref, b_ref, r_ref, o_ref = refs[:4]
        else:
            a_ref, b_ref, o_ref = refs[:3]
            r_ref = None
        part = _dot(a_ref[...], b_ref[...], mode)

        def finish(v):
            if alpha != 1.0:
                v = v * alpha
            if has_res:
                v = r_ref[...] + v
            o_ref[...] = v.astype(o_ref.dtype)

        if nk == 1:
            finish(part)
        else:
            acc = refs[-1]
            k = pl.program_id(len(grid) - 1)

            @pl.when(k == 0)
            def _():
                acc[...] = part

            @pl.when(k > 0)
            def _():
                acc[...] += part

            @pl.when(k == nk - 1)
            def _():
                finish(acc[...])

    in_specs = [a_spec, b_spec] + ([res_spec] if has_res else [])
    args = (a, b) + ((res,) if has_res else ())
    scratch = [] if nk == 1 else [pltpu.VMEM(acc_shape, F32)]
    sem = ("parallel",) * (len(grid) - 1) + ("arbitrary",)
    out, couts = _pcall(name, body, grid, in_specs, o_spec, out_shape, args, scratch, sem, comm)
    return out if comm is None else (out, couts)


def _mm_nn(name, a, b, out_dtype=F32, res=None, alpha=1.0, tk=None):
    t, kk = a.shape
    n = b.shape[1]
    tk = kk if tk is None else tk
    grid = (t // TM, 1, kk // tk)
    return _matmul(
        name, "nn", a, b, grid,
        pl.BlockSpec((TM, tk), lambda i, j, k: (i, k)),
        pl.BlockSpec((tk, n), lambda i, j, k: (k, 0)),
        pl.BlockSpec((TM, n), lambda i, j, k: (i, 0)),
        jax.ShapeDtypeStruct((t, n), out_dtype), (TM, n),
        res=res, res_spec=pl.BlockSpec((TM, n), lambda i, j, k: (i, 0)), alpha=alpha)


def _mm_nt(name, a, b, out_dtype=F32, tk=None):
    t, kk = a.shape
    n = b.shape[0]
    tk = kk if tk is None else tk
    grid = (t // TM, 1, kk // tk)
    return _matmul(
        name, "nt", a, b, grid,
        pl.BlockSpec((TM, tk), lambda i, j, k: (i, k)),
        pl.BlockSpec((n, tk), lambda i, j, k: (0, k)),
        pl.BlockSpec((TM, n), lambda i, j, k: (i, 0)),
        jax.ShapeDtypeStruct((t, n), out_dtype), (TM, n))


def _mm_tn(name, a, b, out_dtype, tn=None):
    t, m = a.shape
    n = b.shape[1]
    tn = n if tn is None else tn
    grid = (n // tn, 1, t // TM)
    return _matmul(
        name, "tn", a, b, grid,
        pl.BlockSpec((TM, m), lambda j, i, k: (k, 0)),
        pl.BlockSpec((TM, tn), lambda j, i, k: (k, j)),
        pl.BlockSpec((m, tn), lambda j, i, k: (0, j)),
        jax.ShapeDtypeStruct((m, n), out_dtype), (m, tn))


def _ffn_in(name, h, w_in, comm=None):
    t = h.shape[0]
    grid = (N_DEV, t // TM, 1)
    return _matmul(
        name, "nn", h, w_in, grid,
        pl.BlockSpec((TM, D_MODEL), lambda d, i, k: (i, 0)),
        pl.BlockSpec((None, D_MODEL, FF_SHARD), lambda d, i, k: (d, 0, 0)),
        pl.BlockSpec((None, TM, FF_SHARD), lambda d, i, k: (d, i, 0)),
        jax.ShapeDtypeStruct((N_DEV, t, FF_SHARD), F32), None, comm=comm)


def _ffn_out(name, act, w_out, x, comm=None):
    t = x.shape[0]
    grid = (t // TM, 1, 4)
    return _matmul(
        name, "nn", act, w_out, grid,
        pl.BlockSpec((None, TM, FF_SHARD), lambda i, j, k: (k, i, 0)),
        pl.BlockSpec((None, FF_SHARD, D_MODEL), lambda i, j, k: (k, 0, 0)),
        pl.BlockSpec((TM, D_MODEL), lambda i, j, k: (i, 0)),
        jax.ShapeDtypeStruct((t, D_MODEL), F32), (TM, D_MODEL),
        res=x, res_spec=pl.BlockSpec((TM, D_MODEL), lambda i, j, k: (i, 0)), alpha=0.5, comm=comm)


def _ffn_out_bwd_act(name, dy, w_out, comm=None):
    t = dy.shape[0]
    grid = (4, t // TM, 1)
    return _matmul(
        name, "nt", dy, w_out, grid,
        pl.BlockSpec((TM, D_MODEL), lambda d, i, k: (i, 0)),
        pl.BlockSpec((None, FF_SHARD, D_MODEL), lambda d, i, k: (d, 0, 0)),
        pl.BlockSpec((None, TM, FF_SHARD), lambda d, i, k: (d, i, 0)),
        jax.ShapeDtypeStruct((4, t, FF_SHARD), F32), None, comm=comm)


def _ffn_out_bwd_w(name, act, dy, out_dtype, comm=None):
    t = dy.shape[0]
    grid = (4, 1, t // TM)
    return _matmul(
        name, "tn", act, dy, grid,
        pl.BlockSpec((None, TM, FF_SHARD), lambda d, j, k: (d, k, 0)),
        pl.BlockSpec((TM, D_MODEL), lambda d, j, k: (k, 0)),
        pl.BlockSpec((None, FF_SHARD, D_MODEL), lambda d, j, k: (d, 0, 0)),
        jax.ShapeDtypeStruct((4, FF_SHARD, D_MODEL), out_dtype), (FF_SHARD, D_MODEL), comm=comm)


def _ffn_in_bwd_h(name, dgu, w_in, comm=None):
    t = dgu.shape[1]
    grid = (t // TM, 1, N_DEV)
    return _matmul(
        name, "nt", dgu, w_in, grid,
        pl.BlockSpec((None, TM, FF_SHARD), lambda i, j, k: (k, i, 0)),
        pl.BlockSpec((None, D_MODEL, FF_SHARD), lambda i, j, k: (k, 0, 0)),
        pl.BlockSpec((TM, D_MODEL), lambda i, j, k: (i, 0)),
        jax.ShapeDtypeStruct((t, D_MODEL), F32), (TM, D_MODEL), comm=comm)


def _ffn_in_bwd_w(name, h, dgu, out_dtype, comm=None):
    t = h.shape[0]
    grid = (N_DEV, 1, t // TM)
    return _matmul(
        name, "tn", h, dgu, grid,
        pl.BlockSpec((TM, D_MODEL), lambda d, j, k: (k, 0)),
        pl.BlockSpec((None, TM, FF_SHARD), lambda d, j, k: (d, k, 0)),
        pl.BlockSpec((None, D_MODEL, FF_SHARD), lambda d, j, k: (d, 0, 0)),
        jax.ShapeDtypeStruct((N_DEV, D_MODEL, FF_SHARD), out_dtype), (D_MODEL, FF_SHARD), comm=comm)


def _proj_in(name, h, wp, comm=None):
    t = h.shape[0]
    grid = (NP_COLS // P_BLOCK, t // TM, 1)
    return _matmul(
        name, "nn", h, wp, grid,
        pl.BlockSpec((TM, D_MODEL), lambda j, i, k: (i, 0)),
        pl.BlockSpec((D_MODEL, P_BLOCK), lambda j, i, k: (0, j)),
        pl.BlockSpec((TM, P_BLOCK), lambda j, i, k: (i, j)),
        jax.ShapeDtypeStruct((t, NP_COLS), F32), None, comm=comm)


def _rms_fwd(name, x, w):
    t, d = x.shape

    def body(x_ref, w_ref, h_ref):
        xv = x_ref[...]
        rstd = lax.rsqrt(jnp.mean(xv * xv, axis=-1, keepdims=True) + NORM_EPS)
        h_ref[...] = (xv * rstd * w_ref[...]).astype(h_ref.dtype)

    return pl.pallas_call(
        body, name=name, grid=(t // TE,),
        in_specs=[pl.BlockSpec((TE, d), lambda i: (i, 0)), pl.BlockSpec((1, d), lambda i: (0, 0))],
        out_specs=pl.BlockSpec((TE, d), lambda i: (i, 0)),
        out_shape=jax.ShapeDtypeStruct((t, d), BF), compiler_params=_params("parallel"))(x, w)


def _rms_bwd(name, x, w, dh, dres, out_scale, comm=None):
    t, d = x.shape

    def body(x_ref, w_ref, dh_ref, dres_ref, dx_ref, dxb_ref, dw_ref):
        i = pl.program_id(0)
        xv = x_ref[...]
        rstd = lax.rsqrt(jnp.mean(xv * xv, axis=-1, keepdims=True) + NORM_EPS)
        xhat = xv * rstd
        dhv = dh_ref[...]
        wd = dhv * w_ref[...]
        proj = jnp.mean(wd * xhat, axis=-1, keepdims=True)
        dx = dres_ref[...] + rstd * (wd - xhat * proj)
        dx_ref[...] = dx
        dxb_ref[...] = (dx * out_scale).astype(BF)
        part = jnp.sum(dhv * xhat, axis=0, keepdims=True)

        @pl.when(i == 0)
        def _():
            dw_ref[...] = part

        @pl.when(i > 0)
        def _():
            dw_ref[...] += part

    row = pl.BlockSpec((TE, d), lambda i: (i, 0))
    vec = pl.BlockSpec((1, d), lambda i: (0, 0))
    outs, couts = _pcall(
        name, body, (t // TE,), [row, vec, row, row], [row, row, vec],
        [jax.ShapeDtypeStruct((t, d), F32), jax.ShapeDtypeStruct((t, d), BF), jax.ShapeDtypeStruct((1, d), F32)],
        (x, w, dh, dres), (), ("arbitrary",), comm)
    return outs if comm is None else (outs, couts)


def _final_loss(x, w, target):
    t, d = x.shape

    def body(x_ref, w_ref, t_ref, loss_ref, dx_ref, dxb_ref, dw_ref):
        i = pl.program_id(0)
        xv = x_ref[...]
        rstd = lax.rsqrt(jnp.mean(xv * xv, axis=-1, keepdims=True) + NORM_EPS)
        xhat = xv * rstd
        err = xhat * w_ref[...] - t_ref[...]
        lpart = 0.5 * jnp.sum(jnp.mean(err * err, axis=-1, keepdims=True), axis=0, keepdims=True)
        dy = err * (1.0 / d)
        wd = dy * w_ref[...]
        proj = jnp.mean(wd * xhat, axis=-1, keepdims=True)
        dx = rstd * (wd - xhat * proj)
        dx_ref[...] = dx
        dxb_ref[...] = (0.5 * dx).astype(BF)
        part = jnp.sum(dy * xhat, axis=0, keepdims=True)
        lfull = jnp.broadcast_to(lpart, (1, 128))

        @pl.when(i == 0)
        def _():
            dw_ref[...] = part
            loss_ref[...] = lfull

        @pl.when(i > 0)
        def _():
            dw_ref[...] += part
            loss_ref[...] += lfull

    row = pl.BlockSpec((TE, d), lambda i: (i, 0))
    vec = pl.BlockSpec((1, d), lambda i: (0, 0))
    return pl.pallas_call(
        body, name="final_loss", grid=(t // TE,), in_specs=[row, vec, row],
        out_specs=[pl.BlockSpec((1, 128), lambda i: (0, 0)), row, row, vec],
        out_shape=[jax.ShapeDtypeStruct((1, 128), F32), jax.ShapeDtypeStruct((t, d), F32),
                   jax.ShapeDtypeStruct((t, d), BF), jax.ShapeDtypeStruct((1, d), F32)],
        compiler_params=_params("arbitrary"))(x, w, target)


def _swiglu_fwd(name, gu, comm=None):
    t = gu.shape[1]

    def body(g_ref, u_ref, a_ref):
        g = g_ref[...]
        a_ref[...] = (g * _sigmoid(g) * u_ref[...]).astype(BF)

    blk = (None, TE, FF_SHARD)
    out, couts = _pcall(
        name, body, (4, t // TE),
        [pl.BlockSpec(blk, lambda d, i: (d, i, 0)), pl.BlockSpec(blk, lambda d, i: (d + 4, i, 0))],
        pl.BlockSpec(blk, lambda d, i: (d, i, 0)), jax.ShapeDtypeStruct((4, t, FF_SHARD), BF),
        (gu, gu), (), ("parallel", "parallel"), comm)
    return out if comm is None else (out, couts)


def _swiglu_bwd(name, gu, dact, comm=None):
    t = gu.shape[1]

    def body(g_ref, u_ref, da_ref, o_ref):
        g = g_ref[...]
        da = da_ref[...]
        s = _sigmoid(g)
        o_ref[0] = (da * u_ref[...] * (s * (1.0 + g * (1.0 - s)))).astype(BF)
        o_ref[1] = (da * g * s).astype(BF)

    blk = (None, TE, FF_SHARD)
    out, couts = _pcall(
        name, body, (4, t // TE),
        [pl.BlockSpec(blk, lambda d, i: (d, i, 0)), pl.BlockSpec(blk, lambda d, i: (d + 4, i, 0)),
         pl.BlockSpec(blk, lambda d, i: (d, i, 0))],
        pl.BlockSpec((2, None, TE, FF_SHARD), lambda d, i: (0, d, i, 0)),
        jax.ShapeDtypeStruct((2, 4, t, FF_SHARD), BF), (gu, gu, dact), (), ("parallel", "parallel"), comm)
    out = out.reshape(N_DEV, t, FF_SHARD)
    return out if comm is None else (out, couts)


CONV_CB = 256


def _shift_down(v, s):
    if s == 0:
        return v
    row = lax.broadcasted_iota(jnp.int32, v.shape, 0)
    return jnp.where(row >= s, pltpu.roll(v, s, 0), 0.0)


def _shift_up(v, s):
    if s == 0:
        return v
    n = v.shape[0]
    row = lax.broadcasted_iota(jnp.int32, v.shape, 0)
    return jnp.where(row < n - s, pltpu.roll(v, n - s, 0), 0.0)


def _conv_fwd_val(q, w_ref, k):
    out = q * w_ref[k - 1:k, :]
    for j in range(k - 1):
        out = out + _shift_down(q, k - 1 - j) * w_ref[j:j + 1, :]
    return out


def _conv_bwd_val(q, dv, w_ref, k):
    dq = dv * w_ref[k - 1:k, :]
    dws = []
    for j in range(k - 1):
        dq = dq + _shift_up(dv, k - 1 - j) * w_ref[j:j + 1, :]
        dws.append(jnp.sum(dv * _shift_down(q, k - 1 - j), axis=0, keepdims=True))
    dws.append(jnp.sum(dv * q, axis=0, keepdims=True))
    return dq, dws


def _pspec(t, off):
    base = off // CONV_CB
    return pl.BlockSpec((t, CONV_CB), lambda j: (0, base + j))


def _mix_a_fwd(p, conv_w):
    t = p.shape[0]

    def body(b_ref, c_ref, xa_ref, w_ref, o_ref):
        q = c_ref[...] * xa_ref[...]
        o_ref[...] = (b_ref[...] * _conv_fwd_val(q, w_ref, 3)).astype(BF)

    return pl.pallas_call(
        body, name="mix_a_fwd", grid=(D_MODEL // CONV_CB,),
        in_specs=[_pspec(t, OFF_B), _pspec(t, OFF_C), _pspec(t, OFF_XA),
                  pl.BlockSpec((3, CONV_CB), lambda j: (0, j))],
        out_specs=pl.BlockSpec((t, CONV_CB), lambda j: (0, j)),
        out_shape=jax.ShapeDtypeStruct((t, D_MODEL), BF), compiler_params=_params("parallel"))(p, p, p, conv_w)


def _mix_a_bwd(p, conv_w, dya):
    t = p.shape[0]

    def body(b_ref, c_ref, xa_ref, w_ref, dy_ref, db_ref, dc_ref, dxa_ref, dw_ref):
        cv = c_ref[...]
        xav = xa_ref[...]
        q = cv * xav
        va = _conv_fwd_val(q, w_ref, 3)
        dyv = dy_ref[...]
        db_ref[...] = (dyv * va).astype(BF)
        dq, dws = _conv_bwd_val(q, dyv * b_ref[...], w_ref, 3)
        dc_ref[...] = (dq * xav).astype(BF)
        dxa_ref[...] = (dq * cv).astype(BF)
        for j in range(3):
            dw_ref[j:j + 1, :] = dws[j]

    col = pl.BlockSpec((t, CONV_CB), lambda j: (0, j))
    wsp = pl.BlockSpec((3, CONV_CB), lambda j: (0, j))
    return pl.pallas_call(
        body, name="mix_a_bwd", grid=(D_MODEL // CONV_CB,),
        in_specs=[_pspec(t, OFF_B), _pspec(t, OFF_C), _pspec(t, OFF_XA), wsp, col],
        out_specs=[col, col, col, wsp],
        out_shape=[jax.ShapeDtypeStruct((t, D_MODEL), BF)] * 3 + [jax.ShapeDtypeStruct((3, D_MODEL), F32)],
        compiler_params=_params("parallel"))(p, p, p, conv_w, dya)


def _ssm_conv_fwd(p, conv_w, conv_b):
    t = p.shape[0]

    def body(x_ref, w_ref, b_ref, o_ref):
        pre = _conv_fwd_val(x_ref[...], w_ref, 4) + b_ref[...]
        o_ref[...] = pre * _sigmoid(pre)

    return pl.pallas_call(
        body, name="ssm_conv_fwd", grid=(D_XBC // CONV_CB,),
        in_specs=[_pspec(t, OFF_XBC), pl.BlockSpec((4, CONV_CB), lambda j: (0, j)),
                  pl.BlockSpec((1, CONV_CB), lambda j: (0, j))],
        out_specs=pl.BlockSpec((t, CONV_CB), lambda j: (0, j)),
        out_shape=jax.ShapeDtypeStruct((t, D_XBC), F32), compiler_params=_params("parallel"))(p, conv_w, conv_b)


def _ssm_conv_bwd(p, conv_w, conv_b, dxc):
    t = p.shape[0]

    def body(x_ref, w_ref, b_ref, d_ref, dx_ref, dw_ref, db_ref):
        xv = x_ref[...]
        pre = _conv_fwd_val(xv, w_ref, 4) + b_ref[...]
        s = _sigmoid(pre)
        dpre = d_ref[...] * (s * (1.0 + pre * (1.0 - s)))
        dq, dws = _conv_bwd_val(xv, dpre, w_ref, 4)
        dx_ref[...] = dq.astype(BF)
        for j in range(4):
            dw_ref[j:j + 1, :] = dws[j]
        db_ref[...] = jnp.sum(dpre, axis=0, keepdims=True)

    col = pl.BlockSpec((t, CONV_CB), lambda j: (0, j))
    wsp = pl.BlockSpec((4, CONV_CB), lambda j: (0, j))
    bsp = pl.BlockSpec((1, CONV_CB), lambda j: (0, j))
    return pl.pallas_call(
        body, name="ssm_conv_bwd", grid=(D_XBC // CONV_CB,),
        in_specs=[_pspec(t, OFF_XBC), wsp, bsp, col], out_specs=[col, wsp, bsp],
        out_shape=[jax.ShapeDtypeStruct((t, D_XBC), BF), jax.ShapeDtypeStruct((4, D_XBC), F32),
                   jax.ShapeDtypeStruct((1, D_XBC), F32)],
        compiler_params=_params("parallel"))(p, conv_w, conv_b, dxc)


DT_ROWS = 512


def _tri(lower):
    r = lax.broadcasted_iota(jnp.int32, (CHUNK, CHUNK), 0)
    c = lax.broadcasted_iota(jnp.int32, (CHUNK, CHUNK), 1)
    return jnp.where((r >= c) if lower else (r <= c), 1.0, 0.0).astype(F32)


def _dot_exact(a, b):
    return lax.dot_general(a, b, _DIMS["nn"], preferred_element_type=F32, precision=lax.Precision.HIGHEST)


def _dt_fwd(p, bias_pad, alog_pad):
    t = p.shape[0]

    def body(raw_ref, b_ref, al_ref, dt_ref, acs_ref):
        z = raw_ref[...] + b_ref[...]
        dt = jnp.maximum(z, 0.0) + jnp.log(1.0 + jnp.exp(-jnp.abs(z)))
        dt_ref[...] = dt
        a = dt * (-jnp.exp(al_ref[...]))
        tri = _tri(True)
        for k in range(DT_ROWS // CHUNK):
            acs_ref[k * CHUNK:(k + 1) * CHUNK, :] = _dot_exact(tri, a[k * CHUNK:(k + 1) * CHUNK, :])

    blk = pl.BlockSpec((DT_ROWS, DT_W), lambda i: (i, 0))
    vec = pl.BlockSpec((1, DT_W), lambda i: (0, 0))
    return pl.pallas_call(
        body, name="dt_fwd", grid=(t // DT_ROWS,),
        in_specs=[pl.BlockSpec((DT_ROWS, DT_W), lambda i: (i, OFF_DT // DT_W)), vec, vec],
        out_specs=[blk, blk], out_shape=[jax.ShapeDtypeStruct((t, DT_W), F32)] * 2,
        compiler_params=_params("parallel"))(p, bias_pad, alog_pad)


def _dt_bwd(p, bias_pad, alog_pad, dt, ddt, dacs):
    t = p.shape[0]

    def body(raw_ref, b_ref, al_ref, dt_ref, ddt_ref, dacs_ref, draw_ref, db_ref, dal_ref):
        i = pl.program_id(0)
        acoef = -jnp.exp(al_ref[...])
        triu = _tri(False)
        das = []
        for k in range(DT_ROWS // CHUNK):
            das.append(_dot_exact(triu, dacs_ref[k * CHUNK:(k + 1) * CHUNK, :]))
        da = jnp.concatenate(das, axis=0)
        dtv = dt_ref[...]
        ddt_tot = ddt_ref[...] + da * acoef
        lane = lax.broadcasted_iota(jnp.int32, (DT_ROWS, DT_W), 1)
        draw = jnp.where(lane < N_HEADS, ddt_tot * _sigmoid(raw_ref[...] + b_ref[...]), 0.0)
        draw_ref[...] = draw.astype(BF)
        pb = jnp.sum(draw, axis=0, keepdims=True)
        pa = jnp.sum(da * dtv * acoef, axis=0, keepdims=True)

        @pl.when(i == 0)
        def _():
            db_ref[...] = pb
            dal_ref[...] = pa

        @pl.when(i > 0)
        def _():
            db_ref[...] += pb
            dal_ref[...] += pa

    blk = pl.BlockSpec((DT_ROWS, DT_W), lambda i: (i, 0))
    vec = pl.BlockSpec((1, DT_W), lambda i: (0, 0))
    return pl.pallas_call(
        body, name="dt_bwd", grid=(t // DT_ROWS,),
        in_specs=[pl.BlockSpec((DT_ROWS, DT_W), lambda i: (i, OFF_DT // DT_W)), vec, vec, blk, blk, blk],
        out_specs=[blk, vec, vec],
        out_shape=[jax.ShapeDtypeStruct((t, DT_W), BF), jax.ShapeDtypeStruct((1, DT_W), F32),
                   jax.ShapeDtypeStruct((1, DT_W), F32)],
        compiler_params=_params("arbitrary"))(p, bias_pad, alog_pad, dt, ddt, dacs)


def _split_dot(z, onehot, terms):
    out = None
    rest = z
    for _ in range(terms):
        piece = rest.astype(BF)
        part = _dot(piece, onehot)
        out = part if out is None else out + part
        rest = rest - piece.astype(F32)
    return out


def _spread_mat(g):
    row = lax.broadcasted_iota(jnp.int32, (DT_W, GROUP_W), 0)
    lane = lax.broadcasted_iota(jnp.int32, (DT_W, GROUP_W), 1)
    return jnp.where(row == HEADS_PER_GROUP * g + lane // HEAD_DIM, 1.0, 0.0).astype(BF)


def _gather_mat(g):
    row = lax.broadcasted_iota(jnp.int32, (GROUP_W, DT_W), 0)
    lane = lax.broadcasted_iota(jnp.int32, (GROUP_W, DT_W), 1)
    return jnp.where(lane == HEADS_PER_GROUP * g + row // HEAD_DIM, 1.0, 0.0).astype(BF)


def _ssd_masks():
    row = lax.broadcasted_iota(jnp.int32, (CHUNK, GROUP_W), 0)
    col = lax.broadcasted_iota(jnp.int32, (CHUNK, GROUP_W), 1) % HEAD_DIM
    brow = lax.broadcasted_iota(jnp.int32, (GROUP_W, GROUP_W), 0) // HEAD_DIM
    bcol = lax.broadcasted_iota(jnp.int32, (GROUP_W, GROUP_W), 1) // HEAD_DIM
    return row >= col, row == col, brow == bcol


def _stack4(v):
    return jnp.concatenate([v, v, v, v], axis=0)


def _fold4(v):
    return v[0:CHUNK] + v[CHUNK:2 * CHUNK] + v[2 * CHUNK:3 * CHUNK] + v[3 * CHUNK:4 * CHUNK]


def _ssd_group(xc_ref, stacked, g, tri, eye, blockdiag):
    gs = slice(GROUP_W * g, GROUP_W * (g + 1))
    xs_g = xc_ref[:, gs]
    b_g = xc_ref[:, D_INNER + D_STATE * g:D_INNER + D_STATE * (g + 1)].astype(BF)
    c_g = xc_ref[:, D_INNER + 1024 + D_STATE * g:D_INNER + 1024 + D_STATE * (g + 1)].astype(BF)
    wide = _split_dot(stacked, _spread_mat(g), 3)
    acs_e, dt_e = wide[0:CHUNK], wide[CHUNK:2 * CHUNK]
    atot_e = acs_e[CHUNK - 1:CHUNK, :]
    acs_j = jnp.sum(jnp.where(eye, acs_e, 0.0), axis=0, keepdims=True)
    lmat = jnp.where(tri, jnp.exp(jnp.minimum(acs_e - acs_j, 0.0)), 0.0)
    b_t = _stack4(b_g)
    m = _dot(c_g, b_t, "nt") * lmat
    x_g = xs_g * dt_e
    xbd = jnp.where(blockdiag, _stack4(x_g), 0.0).astype(BF)
    return dict(gs=gs, xs=xs_g, b=b_g, c=c_g, b_t=b_t, dt=dt_e, e=jnp.exp(acs_e), dec=jnp.exp(atot_e - acs_e),
                eat=jnp.exp(atot_e), lmat=lmat, m=m, x=x_g, xbd=xbd)


def _ssd_fwd(xconv, dt, acs, d_exp, comm=None):
    t = xconv.shape[0]
    nc = t // CHUNK

    def body(xc_ref, dt_ref, acs_ref, d_ref, y_ref, hs_ref, state):
        c = pl.program_id(0)

        @pl.when(c == 0)
        def _():
            state[...] = jnp.zeros_like(state)

        hs_ref[...] = state[...]
        tri, eye, blockdiag = _ssd_masks()
        stacked = jnp.concatenate([acs_ref[...], dt_ref[...]], axis=0)
        for g in range(N_GROUPS):
            q = _ssd_group(xc_ref, stacked, g, tri, eye, blockdiag)
            gs = q["gs"]
            h_t = state[:, gs]
            ydiag = _dot(q["m"].astype(BF), q["xbd"])
            yoff = _dot(q["c"], h_t.astype(BF)) * q["e"]
            y_ref[:, gs] = ydiag + yoff + d_ref[:, gs] * q["xs"]
            s_t = _dot(q["b"], (q["x"] * q["dec"]).astype(BF), "tn")
            state[:, gs] = q["eat"] * h_t + s_t

    blk = lambda w: pl.BlockSpec((CHUNK, w), lambda c: (c, 0))
    outs, couts = _pcall(
        "ssd_fwd", body, (nc,),
        [blk(D_XBC), blk(DT_W), blk(DT_W), pl.BlockSpec((1, D_INNER), lambda c: (0, 0))],
        [blk(D_INNER), pl.BlockSpec((None, D_STATE, D_INNER), lambda c: (c, 0, 0))],
        [jax.ShapeDtypeStruct((t, D_INNER), F32), jax.ShapeDtypeStruct((nc, D_STATE, D_INNER), F32)],
        (xconv, dt, acs, d_exp), [pltpu.VMEM((D_STATE, D_INNER), F32)], ("arbitrary",), comm)
    return outs if comm is None else (outs, couts)


def _ssd_bwd(xconv, dt, acs, d_exp, hsave, dy, comm=None):
    t = xconv.shape[0]
    nc = t // CHUNK

    def body(xc_ref, dt_ref, acs_ref, d_ref, hs_ref, dy_ref, dxc_ref, ddt_ref, dacs_ref, dd_ref, dstate):
        c = pl.program_id(0)

        @pl.when(c == 0)
        def _():
            dstate[...] = jnp.zeros_like(dstate)
            dd_ref[...] = jnp.zeros_like(dd_ref)

        tri, eye, blockdiag = _ssd_masks()
        acsv = acs_ref[...]
        stacked = jnp.concatenate([acsv, dt_ref[...]], axis=0)
        eat_heads = jnp.exp(acsv[CHUNK - 1:CHUNK, :])
        ddt_acc = jnp.zeros((CHUNK, DT_W), F32)
        dacs_acc = jnp.zeros((CHUNK, DT_W), F32)
        datot_acc = jnp.zeros((1, DT_W), F32)

        for g in range(N_GROUPS):
            q = _ssd_group(xc_ref, stacked, g, tri, eye, blockdiag)
            gs, xs_g, b_g, c_g, m = q["gs"], q["xs"], q["b"], q["c"], q["m"]
            bs = slice(D_INNER + D_STATE * g, D_INNER + D_STATE * (g + 1))
            cs = slice(D_INNER + 1024 + D_STATE * g, D_INNER + 1024 + D_STATE * (g + 1))
            h_t = hs_ref[:, gs]
            h_b = h_t.astype(BF)
            dy_g = dy_ref[:, gs]
            dy_b = dy_g.astype(BF)
            ds_t = dstate[:, gs]
            ds_b = ds_t.astype(BF)

            yoff = _dot(c_g, h_b) * q["e"]
            edy = (q["e"] * dy_g).astype(BF)
            d_c = _dot(edy, h_b, "nt")
            d_ht = _dot(c_g, edy, "tn")
            bds = _dot(b_g, ds_b)
            xd = q["x"] * q["dec"]
            d_b = _dot(xd.astype(BF), ds_b, "nt")
            dm = _dot(dy_b, q["xbd"], "nt")
            cross = _dot(m.astype(BF), dy_b, "tn")
            dx_full = q["dec"] * bds + _fold4(jnp.where(blockdiag, cross, 0.0))
            dml = (dm * q["lmat"]).astype(BF)
            d_c = d_c + _dot(dml, q["b_t"])
            d_b = d_b + _fold4(_dot(dml, c_g, "tn"))
            w = dm * m
            q_dec = xd * bds
            z = w - jnp.where(eye, jnp.sum(w, axis=0, keepdims=True), 0.0) + dy_g * yoff - q_dec
            rows = jnp.concatenate(
                [jnp.sum(q_dec, axis=0, keepdims=True), jnp.sum(ds_t * h_t, axis=0, keepdims=True),
                 jnp.zeros((6, GROUP_W), F32)], axis=0)
            seg = _split_dot(jnp.concatenate([z, dx_full * xs_g, rows], axis=0), _gather_mat(g), 2)
            dacs_acc = dacs_acc + seg[0:CHUNK]
            ddt_acc = ddt_acc + seg[CHUNK:2 * CHUNK]
            datot_acc = datot_acc + seg[2 * CHUNK:2 * CHUNK + 1] + eat_heads * seg[2 * CHUNK + 1:2 * CHUNK + 2]
            dxc_ref[:, cs] = d_c
            dxc_ref[:, bs] = d_b
            dxc_ref[:, gs] = dx_full * q["dt"] + d_ref[:, gs] * dy_g
            dd_ref[:, gs] += jnp.sum(dy_g * xs_g, axis=0, keepdims=True)
            dstate[:, gs] = q["eat"] * ds_t + d_ht

        rowi = lax.broadcasted_iota(jnp.int32, (CHUNK, DT_W), 0)
        ddt_ref[...] = ddt_acc
        dacs_ref[...] = dacs_acc + jnp.where(rowi == CHUNK - 1, datot_acc, 0.0)

    rev = lambda w: pl.BlockSpec((CHUNK, w), lambda c: (nc - 1 - c, 0))
    vec = pl.BlockSpec((1, D_INNER), lambda c: (0, 0))
    outs, couts = _pcall(
        "ssd_bwd", body, (nc,),
        [rev(D_XBC), rev(DT_W), rev(DT_W), vec,
         pl.BlockSpec((None, D_STATE, D_INNER), lambda c: (nc - 1 - c, 0, 0)), rev(D_INNER)],
        [rev(D_XBC), rev(DT_W), rev(DT_W), vec],
        [jax.ShapeDtypeStruct((t, D_XBC), F32), jax.ShapeDtypeStruct((t, DT_W), F32),
         jax.ShapeDtypeStruct((t, DT_W), F32), jax.ShapeDtypeStruct((1, D_INNER), F32)],
        (xconv, dt, acs, d_exp, hsave, dy),
        [pltpu.VMEM((D_STATE, D_INNER), F32)], ("arbitrary",), comm)
    return outs if comm is None else (outs, couts)


def _gnorm_fwd(y, p, w, comm=None):
    t = y.shape[0]
    zoff = OFF_Z // GROUP_W

    def body(y_ref, z_ref, w_ref, o_ref):
        z = z_ref[...]
        yf = y_ref[...] * (z * _sigmoid(z))
        rstd = lax.rsqrt(jnp.mean(yf * yf, axis=-1, keepdims=True) + NORM_EPS)
        o_ref[...] = (yf * rstd * w_ref[...]).astype(BF)

    blk = pl.BlockSpec((TE, GROUP_W), lambda i, j: (i, j))
    out, couts = _pcall(
        "gnorm_fwd", body, (t // TE, N_GROUPS),
        [blk, pl.BlockSpec((TE, GROUP_W), lambda i, j: (i, zoff + j)), pl.BlockSpec((1, GROUP_W), lambda i, j: (0, j))],
        blk, jax.ShapeDtypeStruct((t, D_INNER), BF), (y, p, w), (), ("parallel", "parallel"), comm)
    return out if comm is None else (out, couts)


def _gnorm_bwd(y, p, w, dyn, comm=None):
    t = y.shape[0]
    zoff = OFF_Z // GROUP_W

    def body(y_ref, z_ref, w_ref, dn_ref, dy_ref, dz_ref, dw_ref):
        i = pl.program_id(1)
        z = z_ref[...]
        yv = y_ref[...]
        s = _sigmoid(z)
        sil = z * s
        yf = yv * sil
        rstd = lax.rsqrt(jnp.mean(yf * yf, axis=-1, keepdims=True) + NORM_EPS)
        xhat = yf * rstd
        dn = dn_ref[...]
        wd = dn * w_ref[...]
        proj = jnp.mean(wd * xhat, axis=-1, keepdims=True)
        dyf = rstd * (wd - xhat * proj)
        dy_ref[...] = dyf * sil
        dz_ref[...] = (dyf * yv * (s * (1.0 + z * (1.0 - s)))).astype(BF)
        part = jnp.sum(dn * xhat, axis=0, keepdims=True)

        @pl.when(i == 0)
        def _():
            dw_ref[...] = part

        @pl.when(i > 0)
        def _():
            dw_ref[...] += part

    blk = pl.BlockSpec((TE, GROUP_W), lambda j, i: (i, j))
    vec = pl.BlockSpec((1, GROUP_W), lambda j, i: (0, j))
    outs, couts = _pcall(
        "gnorm_bwd", body, (N_GROUPS, t // TE),
        [blk, pl.BlockSpec((TE, GROUP_W), lambda j, i: (i, zoff + j)), vec, blk],
        [blk, blk, vec],
        [jax.ShapeDtypeStruct((t, D_INNER), F32), jax.ShapeDtypeStruct((t, D_INNER), BF),
         jax.ShapeDtypeStruct((1, D_INNER), F32)],
        (y, p, w, dyn), (), ("parallel", "arbitrary"), comm)
    return outs if comm is None else (outs, couts)


MERGE_CB = 512


def _merge_fwd(p, ya, yb):
    t = ya.shape[0]

    def body(ga_ref, gb_ref, ya_ref, yb_ref, o_ref):
        o_ref[...] = (_sigmoid(ga_ref[...]) * ya_ref[...] + _sigmoid(gb_ref[...]) * yb_ref[...]).astype(BF)

    blk = pl.BlockSpec((TE, MERGE_CB), lambda i, j: (i, j))
    return pl.pallas_call(
        body, name="merge_fwd", grid=(t // TE, D_MODEL // MERGE_CB),
        in_specs=[pl.BlockSpec((TE, MERGE_CB), lambda i, j: (i, OFF_GA // MERGE_CB + j)),
                  pl.BlockSpec((TE, MERGE_CB), lambda i, j: (i, OFF_GB // MERGE_CB + j)), blk, blk],
        out_specs=blk, out_shape=jax.ShapeDtypeStruct((t, D_MODEL), BF),
        compiler_params=_params("parallel", "parallel"))(p, p, ya, yb)


def _merge_bwd(p, ya, yb, dm):
    t = ya.shape[0]

    def body(ga_ref, gb_ref, ya_ref, yb_ref, dm_ref, dga_ref, dgb_ref, dya_ref, dyb_ref):
        d = dm_ref[...]
        sa = _sigmoid(ga_ref[...])
        sb = _sigmoid(gb_ref[...])
        dga_ref[...] = (d * ya_ref[...] * sa * (1.0 - sa)).astype(BF)
        dgb_ref[...] = (d * yb_ref[...] * sb * (1.0 - sb)).astype(BF)
        dya_ref[...] = (d * sa).astype(BF)
        dyb_ref[...] = (d * sb).astype(BF)

    blk = pl.BlockSpec((TE, MERGE_CB), lambda i, j: (i, j))
    return pl.pallas_call(
        body, name="merge_bwd", grid=(t // TE, D_MODEL // MERGE_CB),
        in_specs=[pl.BlockSpec((TE, MERGE_CB), lambda i, j: (i, OFF_GA // MERGE_CB + j)),
                  pl.BlockSpec((TE, MERGE_CB), lambda i, j: (i, OFF_GB // MERGE_CB + j)), blk, blk, blk],
        out_specs=[blk] * 4, out_shape=[jax.ShapeDtypeStruct((t, D_MODEL), BF)] * 4,
        compiler_params=_params("parallel", "parallel"))(p, p, ya, yb, dm)


def _adamw(name, parts, w, m, v):
    r, c = w.shape
    tr = _row_tile(r)
    n_parts = parts.shape[0]
    bc1 = 1.0 - ADAM_B1 ** ADAM_STEP
    bc2 = 1.0 - ADAM_B2 ** ADAM_STEP

    def body(p_ref, w_ref, m_ref, v_ref, g_ref, d_ref, nm_ref, nv_ref):
        g = p_ref[0].astype(F32)
        for k in range(1, n_parts):
            g = g + p_ref[k].astype(F32)
        nm = ADAM_B1 * m_ref[...] + (1.0 - ADAM_B1) * g
        nv = ADAM_B2 * v_ref[...] + (1.0 - ADAM_B2) * (g * g)
        g_ref[...] = g
        nm_ref[...] = nm
        nv_ref[...] = nv
        d_ref[...] = -ADAM_LR * ((nm / bc1) / (jnp.sqrt(nv / bc2) + ADAM_EPS) + ADAM_WD * w_ref[...])

    blk = pl.BlockSpec((tr, c), lambda i: (i, 0))
    return pl.pallas_call(
        body, name=name, grid=(r // tr,),
        in_specs=[pl.BlockSpec((n_parts, tr, c), lambda i: (0, i, 0)), blk, blk, blk],
        out_specs=[blk] * 4, out_shape=[jax.ShapeDtypeStruct((r, c), F32)] * 4,
        compiler_params=_params("parallel"))(parts, w, m, v)


_SECTIONS = [(0, 1024), (1024, 2048), (2048, 3072), (3072, 5120), (5120, 9216), (9248, 10272), (10272, 11296),
             (9216, 9248)]


def _to_padded_cols(w_full):
    parts = [w_full[:, a:b] for a, b in _SECTIONS]
    parts.append(jnp.zeros((w_full.shape[0], NP_COLS - N_IN), w_full.dtype))
    return jnp.concatenate(parts, axis=1)


def _from_padded_cols(g):
    return jnp.concatenate(
        [g[:, OFF_B:OFF_Z + 2048], g[:, OFF_XBC:OFF_XBC + 4096], g[:, OFF_DT:OFF_DT + N_HEADS],
         g[:, OFF_GA:OFF_GA + 2048]], axis=1)


def _pad_lanes(v, width):
    return jnp.pad(v, ((0, 0), (0, width - v.shape[1])))


def _reduce_start(slots, host):
    outs, sib = host(_pair_comm([a for _, a in slots]))
    sums = [(n, _add_pairs("pairsum_" + n, a, b)) for (n, a), b in zip(slots, sib)]
    return outs, sums


def _train_step(x, target, shard, rep):
    gdt = BF
    t = x.shape[0]
    recv = {}
    (w1_in,) = _comm_call("gather_ffn1_in", _gather_comm([shard["ffn1_w_in"]]))
    h1 = _rms_fwd("rms1_fwd", x, rep["ffn1_norm"])
    w_in_rows = [shard["w_in"][a:b] for a, b in W_IN_ROW_CUTS]
    gu1, got = _ffn_in("ffn1_in", h1, w1_in, comm=_gather_comm([shard["ffn1_w_out"], w_in_rows[0]]))
    w1_out = got[0].reshape(4, FF_SHARD, D_MODEL)
    w_in_got = [got[1]]
    act1, got = _swiglu_fwd("swiglu1_fwd", gu1, comm=_gather_comm([w_in_rows[1]]))
    w_in_got.append(got[0])
    x1, got = _ffn_out("ffn1_out", act1, w1_out, x, comm=_gather_comm(
        [w_in_rows[2], shard["short_conv_w"], shard["ssm_conv_w"]]))
    w_in_got.append(got[0])
    short_conv_w = got[1].transpose(1, 0, 2).reshape(3, D_MODEL)
    ssm_conv_w = got[2].transpose(1, 0, 2).reshape(4, D_XBC)
    wp = jnp.concatenate(
        [_to_padded_cols(a.transpose(1, 0, 2).reshape(a.shape[1], N_IN)) for a in w_in_got], axis=0)

    h2 = _rms_fwd("rms2_fwd", x1, rep["mix_norm"])
    p, got = _proj_in("proj_in", h2, wp, comm=_gather_comm([shard["short_w_out"], shard["ssm_w_out"], shard["w_out"]]))
    short_w_out = got[0].reshape(D_MODEL, D_MODEL)
    ssm_w_out = got[1].reshape(D_INNER, D_MODEL)
    w_out = got[2].reshape(D_MODEL, D_MODEL)
    ya_in = _mix_a_fwd(p, short_conv_w)
    y_a = _mm_nn("short_out", ya_in, short_w_out)
    xconv = _ssm_conv_fwd(p, ssm_conv_w, rep["ssm_conv_b"])
    dt, acs = _dt_fwd(p, rep["dt_bias_pad"], rep["a_log_pad"])
    (y_ssm, hsave), (w2_in,) = _ssd_fwd(xconv, dt, acs, rep["d_exp"], comm=_gather_comm([shard["ffn2_w_in"]]))
    yn, got = _gnorm_fwd(y_ssm, p, rep["ssm_norm"], comm=_gather_comm([shard["ffn2_w_out"]]))
    w2_out = got[0].reshape(4, FF_SHARD, D_MODEL)
    y_b = _mm_nn("ssm_out", yn, ssm_w_out, tk=1024)
    merged = _merge_fwd(p, y_a, y_b)
    x2 = _mm_nn("mix_out", merged, w_out, res=x1)

    h3 = _rms_fwd("rms3_fwd", x2, rep["ffn2_norm"])
    gu2 = _ffn_in("ffn2_in", h3, w2_in)
    act2 = _swiglu_fwd("swiglu2_fwd", gu2)
    x3 = _ffn_out("ffn2_out", act2, w2_out, x2)

    loss, dx3, dx3h, g_final = _final_loss(x3, rep["final_norm"], target)

    small = {"final_norm": g_final}
    dact2 = _ffn_out_bwd_act("ffn2_out_bwd_act", dx3h, w2_out)
    g_w2_out = _ffn_out_bwd_w("ffn2_out_bwd_w", act2, dx3h, gdt)
    dgu2 = _swiglu_bwd("swiglu2_bwd", gu2, dact2)
    g_w2_in = _ffn_in_bwd_w("ffn2_in_bwd_w", h3, dgu2, gdt)
    dh3 = _ffn_in_bwd_h("ffn2_in_bwd_h", dgu2, w2_in)
    dx2, dx2b, small["ffn2_norm"] = _rms_bwd("rms3_bwd", x2, rep["ffn2_norm"], dh3, dx3, 1.0)

    dmerged = _mm_nt("mix_out_bwd_x", dx2b, w_out)
    g_w_out = _mm_tn("mix_out_bwd_w", merged, dx2b, gdt)
    dga, dgb, dya, dyb = _merge_bwd(p, y_a, y_b, dmerged)

    dya_in = _mm_nt("short_out_bwd_x", dya, short_w_out)
    g_short_w_out = _mm_tn("short_out_bwd_w", ya_in, dya, gdt)
    db, dc, dxa, g_short_conv = _mix_a_bwd(p, short_conv_w, dya_in)

    dyn = _mm_nt("ssm_out_bwd_x", dyb, ssm_w_out)
    g_ssm_w_out = _mm_tn("ssm_out_bwd_w", yn, dyb, gdt)
    late = [("ffn2_w_out", g_w2_out.reshape(N_DEV, FF_SHARD // 2, D_MODEL)), ("ffn2_w_in", g_w2_in),
            ("w_out", g_w_out.reshape(N_DEV, -1, D_MODEL)), ("short_w_out", g_short_w_out.reshape(N_DEV, -1, D_MODEL)),
            ("ssm_w_out", g_ssm_w_out.reshape(N_DEV, -1, D_MODEL))]
    (dy_ssm, dz, small["ssm_norm"]), sums = _reduce_start(
        late, lambda comm: _gnorm_bwd(y_ssm, p, rep["ssm_norm"], dyn, comm=comm))
    (dxconv, ddt, dacs, dd_lane), got = _ssd_bwd(xconv, dt, acs, rep["d_exp"], hsave, dy_ssm,
                                                  comm=_chip_comm([a for _, a in sums]))
    recv.update({n: a for (n, _), a in zip(sums, got)})
    small["ssm_D"] = dd_lane.reshape(N_HEADS, HEAD_DIM).sum(axis=1)[None, :]
    dxbc, g_ssm_conv, small["ssm_conv_b"] = _ssm_conv_bwd(p, ssm_conv_w, rep["ssm_conv_b"], dxconv)
    draw, dbias, dalog = _dt_bwd(p, rep["dt_bias_pad"], rep["a_log_pad"], dt, ddt, dacs)
    small["ssm_dt_bias"] = dbias[:, :N_HEADS]
    small["ssm_A_log"] = dalog[:, :N_HEADS]

    dp = jnp.concatenate(
        [db, dc, dxa, dz, dxbc, dga, dgb, draw, jnp.zeros((t, NP_COLS - OFF_DT - DT_W), BF)], axis=1)
    dh2 = _mm_nt("proj_in_bwd_x", dp, wp, tk=P_BLOCK)
    g_wp = _mm_tn("proj_in_bwd_w", h2, dp, gdt, tn=P_BLOCK)
    w_rows = [("w_in%d" % i, _from_padded_cols(g_wp[a:b]).reshape(b - a, N_DEV, IN_SHARD).transpose(1, 0, 2))
              for i, (a, b) in enumerate(W_IN_ROW_CUTS)]
    (dx1, dx1h, small["mix_norm"]), w_sums = _reduce_start(
        w_rows, lambda comm: _rms_bwd("rms2_bwd", x1, rep["mix_norm"], dh2, dx2, 0.5, comm=comm))

    g_w1_out, got0 = _ffn_out_bwd_w("ffn1_out_bwd_w", act1, dx1h, gdt, comm=_chip_comm([w_sums[0][1]]))
    rest = [("ffn1_w_out", g_w1_out.reshape(N_DEV, FF_SHARD // 2, D_MODEL)),
            ("short_conv_w", g_short_conv.reshape(3, N_DEV, -1).transpose(1, 0, 2)),
            ("ssm_conv_w", g_ssm_conv.reshape(4, N_DEV, -1).transpose(1, 0, 2))]
    pair_rest = _pair_comm([a for _, a in rest])
    dact1, got = _ffn_out_bwd_act("ffn1_out_bwd_act", dx1h, w1_out,
                                  comm=_join_comm(_chip_comm([w_sums[1][1]]), pair_rest))
    got1, sib = got[0], got[1:]
    rest_sums = [(n, _add_pairs("pairsum_" + n, a, b)) for (n, a), b in zip(rest, sib)]
    dgu1, got2 = _swiglu_bwd("swiglu1_bwd", gu1, dact1, comm=_chip_comm([w_sums[2][1]]))
    recv["w_in"] = jnp.concatenate([got0[0], got1, got2[0]], axis=1)
    g_w1_in, got = _ffn_in_bwd_w("ffn1_in_bwd_w", h1, dgu1, gdt, comm=_chip_comm([a for _, a in rest_sums]))
    recv.update({n: a for (n, _), a in zip(rest_sums, got)})
    dh1, last_sums = _reduce_start(
        [("ffn1_w_in", g_w1_in)], lambda comm: _ffn_in_bwd_h("ffn1_in_bwd_h", dgu1, w1_in, comm=comm))
    (dx0, _, small["ffn1_norm"]), got = _rms_bwd("rms1_bwd", x, rep["ffn1_norm"], dh1, dx1, 1.0,
                                                  comm=_chip_comm([last_sums[0][1]]))
    recv["ffn1_w_in"] = got[0]
    packed = _pack_small(small, loss[:, 0:1])
    (small_parts,) = _comm_call("exchange_last", _gather_comm([packed]))
    return dx0, recv, small_parts


_SMALL = [("ffn1_norm", 1024), ("mix_norm", 1024), ("ssm_conv_b", 4096), ("ssm_dt_bias", 32), ("ssm_A_log", 32),
          ("ssm_D", 32), ("ssm_norm", 2048), ("ffn2_norm", 1024), ("final_norm", 1024)]
SMALL_W = 10368


def _pack_small(d, loss=None):
    parts = [d[n].reshape(1, -1).astype(F32) for n, _ in _SMALL]
    used = sum(sz for _, sz in _SMALL)
    tail = jnp.zeros((1, SMALL_W - used), F32)
    if loss is not None:
        tail = tail.at[:, 0:1].set(loss)
    return jnp.concatenate(parts + [tail], axis=1)


def _unpack_small(v, shapes):
    out, off = {}, 0
    for n, sz in _SMALL:
        out[n] = v[:, off:off + sz].reshape(shapes[n])
        off += sz
    return out, v[0, off]


_SHARDED = ["ffn1_w_in", "ffn1_w_out", "w_in", "short_conv_w", "short_w_out", "ssm_conv_w", "ssm_w_out", "w_out",
            "ffn2_w_in", "ffn2_w_out"]
_ORDER = ["ffn1_norm", "ffn1_w_in", "ffn1_w_out", "mix_norm", "w_in", "short_conv_w", "short_w_out", "ssm_conv_w",
          "ssm_conv_b", "ssm_dt_bias", "ssm_A_log", "ssm_D", "ssm_norm", "ssm_w_out", "w_out", "ffn2_norm",
          "ffn2_w_in", "ffn2_w_out", "final_norm"]


def kernel(x, ffn1_norm, ffn1_w_in, ffn1_w_out, mix_norm, w_in, short_conv_w, short_w_out, ssm_conv_w, ssm_conv_b, ssm_dt_bias, ssm_A_log, ssm_D, ssm_norm, ssm_w_out, w_out, ffn2_norm, ffn2_w_in, ffn2_w_out, final_norm, loss_target, m_ffn1_norm, m_ffn1_w_in, m_ffn1_w_out, m_mix_norm, m_w_in, m_short_conv_w, m_short_w_out, m_ssm_conv_w, m_ssm_conv_b, m_ssm_dt_bias, m_ssm_A_log, m_ssm_D, m_ssm_norm, m_ssm_w_out, m_w_out, m_ffn2_norm, m_ffn2_w_in, m_ffn2_w_out, m_final_norm, v_ffn1_norm, v_ffn1_w_in, v_ffn1_w_out, v_mix_norm, v_w_in, v_short_conv_w, v_short_w_out, v_ssm_conv_w, v_ssm_conv_b, v_ssm_dt_bias, v_ssm_A_log, v_ssm_D, v_ssm_norm, v_ssm_w_out, v_w_out, v_ffn2_norm, v_ffn2_w_in, v_ffn2_w_out, v_final_norm):
    w = dict(ffn1_norm=ffn1_norm, ffn1_w_in=ffn1_w_in, ffn1_w_out=ffn1_w_out, mix_norm=mix_norm, w_in=w_in,
             short_conv_w=short_conv_w, short_w_out=short_w_out, ssm_conv_w=ssm_conv_w, ssm_conv_b=ssm_conv_b,
             ssm_dt_bias=ssm_dt_bias, ssm_A_log=ssm_A_log, ssm_D=ssm_D, ssm_norm=ssm_norm, ssm_w_out=ssm_w_out,
             w_out=w_out, ffn2_norm=ffn2_norm, ffn2_w_in=ffn2_w_in, ffn2_w_out=ffn2_w_out, final_norm=final_norm)
    m = dict(ffn1_norm=m_ffn1_norm, ffn1_w_in=m_ffn1_w_in, ffn1_w_out=m_ffn1_w_out, mix_norm=m_mix_norm, w_in=m_w_in,
             short_conv_w=m_short_conv_w, short_w_out=m_short_w_out, ssm_conv_w=m_ssm_conv_w,
             ssm_conv_b=m_ssm_conv_b, ssm_dt_bias=m_ssm_dt_bias, ssm_A_log=m_ssm_A_log, ssm_D=m_ssm_D,
             ssm_norm=m_ssm_norm, ssm_w_out=m_ssm_w_out, w_out=m_w_out, ffn2_norm=m_ffn2_norm,
             ffn2_w_in=m_ffn2_w_in, ffn2_w_out=m_ffn2_w_out, final_norm=m_final_norm)
    v = dict(ffn1_norm=v_ffn1_norm, ffn1_w_in=v_ffn1_w_in, ffn1_w_out=v_ffn1_w_out, mix_norm=v_mix_norm, w_in=v_w_in,
             short_conv_w=v_short_conv_w, short_w_out=v_short_w_out, ssm_conv_w=v_ssm_conv_w,
             ssm_conv_b=v_ssm_conv_b, ssm_dt_bias=v_ssm_dt_bias, ssm_A_log=v_ssm_A_log, ssm_D=v_ssm_D,
             ssm_norm=v_ssm_norm, ssm_w_out=v_ssm_w_out, w_out=v_w_out, ffn2_norm=v_ffn2_norm,
             ffn2_w_in=v_ffn2_w_in, ffn2_w_out=v_ffn2_w_out, final_norm=v_final_norm)
    shapes = {n: w[n].shape for n in _ORDER}
    shard = {n: w[n][0] for n in _SHARDED}

    wire = {n: (shard[n] if n in ("short_conv_w", "ssm_conv_w") else shard[n].astype(BF)) for n in _SHARDED}
    rep = {
        "ffn1_norm": ffn1_norm, "mix_norm": mix_norm, "ffn2_norm": ffn2_norm, "ssm_norm": ssm_norm,
        "ssm_conv_b": ssm_conv_b, "final_norm": final_norm.reshape(1, D_MODEL),
        "dt_bias_pad": _pad_lanes(ssm_dt_bias, DT_W), "a_log_pad": _pad_lanes(ssm_A_log, DT_W),
        "d_exp": jnp.repeat(ssm_D, HEAD_DIM, axis=1),
    }
    grad_x, parts, small_parts = _train_step(x[0], loss_target[0], wire, rep)

    out_g, out_d, out_m, out_v = {}, {}, {}, {}
    for n in _SHARDED:
        res = _adamw("adamw_" + n, parts[n], shard[n], m[n][0], v[n][0])
        out_g[n], out_d[n], out_m[n], out_v[n] = [r.reshape(shapes[n]) for r in res]
    sres = _adamw("adamw_small", small_parts, _pack_small(w), _pack_small(m), _pack_small(v))
    sg, loss = _unpack_small(sres[0], shapes)
    sd, _ = _unpack_small(sres[1], shapes)
    sm, _ = _unpack_small(sres[2], shapes)
    sv, _ = _unpack_small(sres[3], shapes)
    out_g.update(sg)
    out_d.update(sd)
    out_m.update(sm)
    out_v.update(sv)
    return (loss, grad_x[None], *[out_g[n] for n in _ORDER], *[out_d[n] for n in _ORDER],
            *[out_m[n] for n in _ORDER], *[out_v[n] for n in _ORDER])
```

```python
import functools

import jax
import jax.numpy as jnp
from jax import lax
from jax.experimental import pallas as pl
from jax.experimental.pallas import tpu as pltpu

F32 = jnp.float32
BF = jnp.bfloat16

N_DEV = 8
D_MODEL = 1024
D_FF = 2816
D_INNER = 2048
D_XBC = 4096
N_HEADS = 32
HEAD_DIM = 64
N_GROUPS = 8
D_STATE = 128
CHUNK = 64
GROUP_W = D_INNER // N_GROUPS
HEADS_PER_GROUP = N_HEADS // N_GROUPS
NORM_EPS = 1e-5
N_IN = 11296
FF_SHARD = 2 * D_FF // N_DEV
FF_HALF = D_FF // 2
IN_SHARD = N_IN // N_DEV

OFF_B, OFF_C, OFF_XA, OFF_Z, OFF_XBC = 0, 1024, 2048, 3072, 5120
N_MAIN = 9216
OFF_GA, OFF_GB, OFF_DT = 0, 1024, 2048
DT_W = 128
N_GD = 2048 + DT_W
W_IN_ROW_CUTS = [(0, 480), (480, 944), (944, 1412)]

ADAM_LR, ADAM_B1, ADAM_B2, ADAM_EPS, ADAM_WD, ADAM_STEP = 0.001, 0.9, 0.999, 1e-08, 0.01, 10

VMEM_LIMIT_V7X = 56 * 1024 * 1024
TM = 512
TE = 256
ADAM_COL_TILE = 256


def _params(*sem):
    return pltpu.CompilerParams(dimension_semantics=sem, vmem_limit_bytes=VMEM_LIMIT_V7X)


_DIMS = {
    "nn": (((1,), (0,)), ((), ())),
    "nt": (((1,), (1,)), ((), ())),
    "tn": (((0,), (0,)), ((), ())),
}


def _dot(a, b, mode="nn"):
    return lax.dot_general(a, b, _DIMS[mode], preferred_element_type=F32)


def _sigmoid(x):
    return 1.0 / (1.0 + jnp.exp(-x))


class _Comm:
    def __init__(self, inputs, out_shapes, sems, start, finish):
        self.inputs, self.out_shapes, self.sems, self.start, self.finish = inputs, out_shapes, sems, start, finish


def _pcall(name, body, grid, in_specs, out_specs, out_shape, args, scratch=(), sem=None, comm=None):
    single = not isinstance(out_shape, (list, tuple))
    out_shapes = [out_shape] if single else list(out_shape)
    out_specs = [out_specs] if single else list(out_specs)
    n_in, n_out, n_scr = len(args), len(out_shapes), len(scratch)
    if comm is None:
        res = pl.pallas_call(
            body, name=name, grid=grid, in_specs=list(in_specs), out_specs=out_specs, out_shape=out_shapes,
            scratch_shapes=list(scratch), compiler_params=_params(*sem))(*args)
        return (res[0] if single else res), []
    nci, nco = len(comm.inputs), len(comm.out_shapes)

    def wrapped(*refs):
        a = refs[:n_in]
        ci = refs[n_in:n_in + nci]
        o0 = n_in + nci
        o = refs[o0:o0 + n_out]
        co = refs[o0 + n_out:o0 + n_out + nco]
        s0 = o0 + n_out + nco
        s = refs[s0:s0 + n_scr]
        cs = refs[s0 + n_scr:]
        pids = [pl.program_id(i) for i in range(len(grid))]
        first = functools.reduce(jnp.logical_and, [p == 0 for p in pids])
        last = functools.reduce(jnp.logical_and, [p == g - 1 for p, g in zip(pids, grid)])

        @pl.when(first)
        def _():
            comm.start(ci, co, cs)

        body(*a, *o, *s)

        @pl.when(last)
        def _():
            comm.finish(ci, co, cs)

    any_spec = pl.BlockSpec(memory_space=pl.ANY)
    res = pl.pallas_call(
        wrapped, name=name, grid=grid, in_specs=list(in_specs) + [any_spec] * nci,
        out_specs=out_specs + [any_spec] * nco, out_shape=out_shapes + list(comm.out_shapes),
        scratch_shapes=list(scratch) + list(comm.sems),
        compiler_params=_params(*(("arbitrary",) * len(grid))))(*args, *comm.inputs)
    core = res[:n_out]
    return (core[0] if single else core), list(res[n_out:])


def _comm_call(name, comm):
    nci, nco = len(comm.inputs), len(comm.out_shapes)

    def body(*refs):
        ci, co, cs = refs[:nci], refs[nci:nci + nco], refs[nci + nco:]
        comm.start(ci, co, cs)
        comm.finish(ci, co, cs)

    any_spec = pl.BlockSpec(memory_space=pl.ANY)
    return pl.pallas_call(
        body, name=name, in_specs=[any_spec] * nci, out_specs=[any_spec] * nco, out_shape=list(comm.out_shapes),
        scratch_shapes=list(comm.sems), compiler_params=pltpu.CompilerParams(has_side_effects=True))(*comm.inputs)


def _remote(src, dst, ssem, rsem, dev):
    return pltpu.make_async_remote_copy(src_ref=src, dst_ref=dst, send_sem=ssem, recv_sem=rsem, device_id=dev,
                                        device_id_type=pl.DeviceIdType.MESH)


def _place():
    x, y, c = lax.axis_index("x"), lax.axis_index("y"), lax.axis_index("c")
    other_chips = [(1 - x, y), (x, 1 - y), (1 - x, 1 - y)]
    return x, y, c, other_chips


def _slot(x, y, c, swap):
    return 4 * y + 2 * x + c if swap else 4 * x + 2 * y + c


def _chip_slot(x, y, swap):
    return 2 * y + x if swap else 2 * x + y


def _gather_comm(shards, swaps=None):
    n = len(shards)
    per = N_DEV - 1
    swaps = [False] * n if swaps is None else swaps

    def start(ins, outs, sems):
        send, recv, loc = sems
        x, y, c, chips = _place()
        for i in range(n):
            me = _slot(x, y, c, swaps[i])
            pltpu.make_async_copy(ins[i], outs[i].at[me], loc.at[i]).start()
            _remote(ins[i], outs[i].at[me], send.at[per * i], recv.at[per * i], (x, y, 1 - c)).start()
            for j, (qx, qy) in enumerate(chips):
                _remote(ins[i], outs[i].at[me], send.at[per * i + 1 + j], recv.at[per * i + 1 + j], (qx, qy, c)).start()

    def finish(ins, outs, sems):
        send, recv, loc = sems
        x, y, c, chips = _place()
        sib = (x, y, 1 - c)
        for i in range(n):
            for j, (qx, qy) in enumerate(chips):
                blk = outs[i].at[_slot(qx, qy, c, swaps[i])]
                _remote(blk, blk, send.at[per * i + 1 + j], recv.at[per * i + 1 + j], (qx, qy, c)).wait_recv()
                _remote(blk, blk, send.at[per * i + 4 + j], recv.at[per * i + 4 + j], sib).start()
        for i in range(n):
            blk = outs[i].at[_slot(x, y, 1 - c, swaps[i])]
            _remote(blk, blk, send.at[per * i], recv.at[per * i], sib).wait_recv()
            for j, (qx, qy) in enumerate(chips):
                blk = outs[i].at[_slot(qx, qy, 1 - c, swaps[i])]
                _remote(blk, blk, send.at[per * i + 4 + j], recv.at[per * i + 4 + j], sib).wait_recv()
        for i in range(n):
            own = outs[i].at[_slot(x, y, c, swaps[i])]
            for k in range(per):
                _remote(ins[i], own, send.at[per * i + k], recv.at[per * i + k], sib).wait_send()
            pltpu.make_async_copy(ins[i], own, loc.at[i]).wait()

    out_shapes = [jax.ShapeDtypeStruct((N_DEV,) + tuple(a.shape), a.dtype) for a in shards]
    sems = [pltpu.SemaphoreType.DMA((per * n,)), pltpu.SemaphoreType.DMA((per * n,)), pltpu.SemaphoreType.DMA((n,))]
    return _Comm(list(shards), out_shapes, sems, start, finish)


def _pair_comm(slots):
    n = len(slots)

    def copies(ins, outs, sems):
        send, recv = sems
        x, y, c, _ = _place()
        sib = (x, y, 1 - c)
        out = []
        for i in range(n):
            for q in range(4):
                out.append(_remote(ins[i].at[2 * q + 1 - c], outs[i].at[q], send.at[4 * i + q], recv.at[4 * i + q], sib))
        return out

    def start(ins, outs, sems):
        for cp in copies(ins, outs, sems):
            cp.start()

    def finish(ins, outs, sems):
        for cp in copies(ins, outs, sems):
            cp.wait_send()
            cp.wait_recv()

    out_shapes = [jax.ShapeDtypeStruct((4,) + tuple(a.shape[1:]), a.dtype) for a in slots]
    sems = [pltpu.SemaphoreType.DMA((4 * n,)), pltpu.SemaphoreType.DMA((4 * n,))]
    return _Comm(list(slots), out_shapes, sems, start, finish)


def _chip_comm(chip_sums, swaps=None):
    n = len(chip_sums)
    swaps = [False] * n if swaps is None else swaps

    def start(ins, outs, sems):
        send, recv, loc = sems
        x, y, c, chips = _place()
        for i in range(n):
            mine = _chip_slot(x, y, swaps[i])
            pltpu.make_async_copy(ins[i].at[mine], outs[i].at[mine], loc.at[i]).start()
            for j, (qx, qy) in enumerate(chips):
                _remote(ins[i].at[_chip_slot(qx, qy, swaps[i])], outs[i].at[mine], send.at[3 * i + j],
                        recv.at[3 * i + j], (qx, qy, c)).start()

    def finish(ins, outs, sems):
        send, recv, loc = sems
        x, y, c, chips = _place()
        for i in range(n):
            mine = _chip_slot(x, y, swaps[i])
            for j, (qx, qy) in enumerate(chips):
                theirs = _chip_slot(qx, qy, swaps[i])
                cp = _remote(ins[i].at[theirs], outs[i].at[theirs], send.at[3 * i + j], recv.at[3 * i + j], (qx, qy, c))
                cp.wait_send()
                cp.wait_recv()
            pltpu.make_async_copy(ins[i].at[mine], outs[i].at[mine], loc.at[i]).wait()

    out_shapes = [jax.ShapeDtypeStruct(a.shape, a.dtype) for a in chip_sums]
    sems = [pltpu.SemaphoreType.DMA((3 * n,)), pltpu.SemaphoreType.DMA((3 * n,)), pltpu.SemaphoreType.DMA((n,))]
    return _Comm(list(chip_sums), out_shapes, sems, start, finish)


def _join_comm(a, b):
    na_i, na_o, na_s = len(a.inputs), len(a.out_shapes), len(a.sems)

    def start(ins, outs, sems):
        a.start(ins[:na_i], outs[:na_o], sems[:na_s])
        b.start(ins[na_i:], outs[na_o:], sems[na_s:])

    def finish(ins, outs, sems):
        a.finish(ins[:na_i], outs[:na_o], sems[:na_s])
        b.finish(ins[na_i:], outs[na_o:], sems[na_s:])

    return _Comm(a.inputs + b.inputs, a.out_shapes + b.out_shapes, a.sems + b.sems, start, finish)


def _row_tile(r):
    for cand in (256, 128):
        if r > cand and r % cand == 0:
            return cand
    return r


def _add_pairs(name, slots, sib):
    r, c = slots.shape[1:]
    tr = _row_tile(r)

    def body(s_ref, b_ref, o_ref):
        core = lax.axis_index("c")
        o_ref[...] = (s_ref[core].astype(F32) + b_ref[...].astype(F32)).astype(o_ref.dtype)

    return pl.pallas_call(
        body, name=name, grid=(4, r // tr),
        in_specs=[pl.BlockSpec((None, 2, tr, c), lambda q, i: (q, 0, i, 0)),
                  pl.BlockSpec((None, tr, c), lambda q, i: (q, i, 0))],
        out_specs=pl.BlockSpec((None, tr, c), lambda q, i: (q, i, 0)),
        out_shape=jax.ShapeDtypeStruct((4, r, c), slots.dtype),
        compiler_params=_params("parallel", "parallel"))(slots.reshape(4, 2, r, c), sib)


def _matmul(name, mode, a, b, grid, a_spec, b_spec, o_spec, out_shape, acc_shape,
            res=None, res_spec=None, alpha=1.0, comm=None):
    nk = grid[-1]
    has_res = res is not None

    def body(*refs):
        if has_res:
            a_ref, b_ref, r_ref, o_ref = refs[:4]
        else:
            a_ref, b_ref, o_ref = refs[:3]
            r_ref = None
        part = _dot(a_ref[...], b_ref[...], mode)

        def finish(v):
            if alpha != 1.0:
                v = v * alpha
            if has_res:
                v = r_ref[...] + v
            o_ref[...] = v.astype(o_ref.dtype)

        if nk == 1:
            finish(part)
        else:
            acc = refs[-1]
            k = pl.program_id(len(grid) - 1)

            @pl.when(k == 0)
            def _():
                acc[...] = part

            @pl.when(k > 0)
            def _():
                acc[...] += part

            @pl.when(k == nk - 1)
            def _():
                finish(acc[...])

    in_specs = [a_spec, b_spec] + ([res_spec] if has_res else [])
    args = (a, b) + ((res,) if has_res else ())
    scratch = [] if nk == 1 else [pltpu.VMEM(acc_shape, F32)]
    sem = ("parallel",) * (len(grid) - 1) + ("arbitrary",)
    out, couts = _pcall(name, body, grid, in_specs, o_spec, out_shape, args, scratch, sem, comm)
    return out if comm is None else (out, couts)


def _mm_nn(name, a, b, out_dtype=F32, res=None, alpha=1.0, tk=None, comm=None):
    t, kk = a.shape
    n = b.shape[1]
    tk = kk if tk is None else tk
    grid = (t // TM, 1, kk // tk)
    return _matmul(
        name, "nn", a, b, grid,
        pl.BlockSpec((TM, tk), lambda i, j, k: (i, k)),
        pl.BlockSpec((tk, n), lambda i, j, k: (k, 0)),
        pl.BlockSpec((TM, n), lambda i, j, k: (i, 0)),
        jax.ShapeDtypeStruct((t, n), out_dtype), (TM, n),
        res=res, res_spec=pl.BlockSpec((TM, n), lambda i, j, k: (i, 0)), alpha=alpha, comm=comm)


def _mm_nt(name, a, b, n=None, tn=None, tk=None, out_dtype=F32, comm=None):
    t, kk = a.shape
    n = b.shape[0] if n is None else n
    tn = n if tn is None else tn
    tk = kk if tk is None else tk
    grid = (n // tn, t // TM, kk // tk)
    return _matmul(
        name, "nt", a, b, grid,
        pl.BlockSpec((TM, tk), lambda j, i, k: (i, k)),
        pl.BlockSpec((tn, tk), lambda j, i, k: (j, k)),
        pl.BlockSpec((TM, tn), lambda j, i, k: (i, j)),
        jax.ShapeDtypeStruct((t, n), out_dtype), (TM, tn), comm=comm)


def _mm_tn(name, a, b, out_dtype, tm=None, comm=None):
    t, m = a.shape
    n = b.shape[1]
    tm = m if tm is None else tm
    grid = (m // tm, 1, t // TM)
    return _matmul(
        name, "tn", a, b, grid,
        pl.BlockSpec((TM, tm), lambda j, i, k: (k, j)),
        pl.BlockSpec((TM, n), lambda j, i, k: (k, 0)),
        pl.BlockSpec((tm, n), lambda j, i, k: (j, 0)),
        jax.ShapeDtypeStruct((m, n), out_dtype), (tm, n), comm=comm)


def _rms_fwd(name, x, w):
    t, d = x.shape

    def body(x_ref, w_ref, h_ref):
        xv = x_ref[...]
        rstd = lax.rsqrt(jnp.mean(xv * xv, axis=-1, keepdims=True) + NORM_EPS)
        h_ref[...] = (xv * rstd * w_ref[...]).astype(h_ref.dtype)

    return pl.pallas_call(
        body, name=name, grid=(t // TE,),
        in_specs=[pl.BlockSpec((TE, d), lambda i: (i, 0)), pl.BlockSpec((1, d), lambda i: (0, 0))],
        out_specs=pl.BlockSpec((TE, d), lambda i: (i, 0)),
        out_shape=jax.ShapeDtypeStruct((t, d), BF), compiler_params=_params("parallel"))(x, w)


def _rms_bwd(name, x, w, dh, dres, out_scale, comm=None):
    t, d = x.shape

    def body(x_ref, w_ref, dh_ref, dres_ref, dx_ref, dxb_ref, dw_ref):
        i = pl.program_id(0)
        xv = x_ref[...]
        rstd = lax.rsqrt(jnp.mean(xv * xv, axis=-1, keepdims=True) + NORM_EPS)
        xhat = xv * rstd
        dhv = dh_ref[...]
        wd = dhv * w_ref[...]
        proj = jnp.mean(wd * xhat, axis=-1, keepdims=True)
        dx = dres_ref[...] + rstd * (wd - xhat * proj)
        dx_ref[...] = dx
        dxb_ref[...] = (dx * out_scale).astype(BF)
        part = jnp.sum(dhv * xhat, axis=0, keepdims=True)

        @pl.when(i == 0)
        def _():
            dw_ref[...] = part

        @pl.when(i > 0)
        def _():
            dw_ref[...] += part

    row = pl.BlockSpec((TE, d), lambda i: (i, 0))
    vec = pl.BlockSpec((1, d), lambda i: (0, 0))
    outs, couts = _pcall(
        name, body, (t // TE,), [row, vec, row, row], [row, row, vec],
        [jax.ShapeDtypeStruct((t, d), F32), jax.ShapeDtypeStruct((t, d), BF), jax.ShapeDtypeStruct((1, d), F32)],
        (x, w, dh, dres), (), ("arbitrary",), comm)
    return outs if comm is None else (outs, couts)


def _final_loss(x, w, target):
    t, d = x.shape

    def body(x_ref, w_ref, t_ref, loss_ref, dx_ref, dxb_ref, dw_ref):
        i = pl.program_id(0)
        xv = x_ref[...]
        rstd = lax.rsqrt(jnp.mean(xv * xv, axis=-1, keepdims=True) + NORM_EPS)
        xhat = xv * rstd
        err = xhat * w_ref[...] - t_ref[...]
        lpart = 0.5 * jnp.sum(jnp.mean(err * err, axis=-1, keepdims=True), axis=0, keepdims=True)
        dy = err * (1.0 / d)
        wd = dy * w_ref[...]
        proj = jnp.mean(wd * xhat, axis=-1, keepdims=True)
        dx = rstd * (wd - xhat * proj)
        dx_ref[...] = dx
        dxb_ref[...] = (0.5 * dx).astype(BF)
        part = jnp.sum(dy * xhat, axis=0, keepdims=True)
        lfull = jnp.broadcast_to(lpart, (1, 128))

        @pl.when(i == 0)
        def _():
            dw_ref[...] = part
            loss_ref[...] = lfull

        @pl.when(i > 0)
        def _():
            dw_ref[...] += part
            loss_ref[...] += lfull

    row = pl.BlockSpec((TE, d), lambda i: (i, 0))
    vec = pl.BlockSpec((1, d), lambda i: (0, 0))
    return pl.pallas_call(
        body, name="final_loss", grid=(t // TE,), in_specs=[row, vec, row],
        out_specs=[pl.BlockSpec((1, 128), lambda i: (0, 0)), row, row, vec],
        out_shape=[jax.ShapeDtypeStruct((1, 128), F32), jax.ShapeDtypeStruct((t, d), F32),
                   jax.ShapeDtypeStruct((t, d), BF), jax.ShapeDtypeStruct((1, d), F32)],
        compiler_params=_params("arbitrary"))(x, w, target)


def _swiglu_fwd(name, gu, comm=None):
    t = gu.shape[0]

    def body(g_ref, u_ref, a_ref):
        g = g_ref[...]
        a_ref[...] = (g * _sigmoid(g) * u_ref[...]).astype(BF)

    blk = (TE, FF_HALF)
    out, couts = _pcall(
        name, body, (t // TE, 2),
        [pl.BlockSpec(blk, lambda i, j: (i, 2 * j)), pl.BlockSpec(blk, lambda i, j: (i, 2 * j + 1))],
        pl.BlockSpec(blk, lambda i, j: (i, j)), jax.ShapeDtypeStruct((t, D_FF), BF),
        (gu, gu), (), ("parallel", "parallel"), comm)
    return out if comm is None else (out, couts)


def _swiglu_bwd(name, gu, dact, comm=None):
    t = gu.shape[0]

    def body(g_ref, u_ref, da_ref, o_ref):
        g = g_ref[...]
        da = da_ref[...]
        s = _sigmoid(g)
        o_ref[:, 0:FF_HALF] = (da * u_ref[...] * (s * (1.0 + g * (1.0 - s)))).astype(BF)
        o_ref[:, FF_HALF:2 * FF_HALF] = (da * g * s).astype(BF)

    blk = (TE, FF_HALF)
    out, couts = _pcall(
        name, body, (t // TE, 2),
        [pl.BlockSpec(blk, lambda i, j: (i, 2 * j)), pl.BlockSpec(blk, lambda i, j: (i, 2 * j + 1)),
         pl.BlockSpec(blk, lambda i, j: (i, j))],
        pl.BlockSpec((TE, 2 * FF_HALF), lambda i, j: (i, j)),
        jax.ShapeDtypeStruct((t, 2 * D_FF), BF), (gu, gu, dact), (), ("parallel", "parallel"), comm)
    return out if comm is None else (out, couts)


CONV_CB = 256


def _shift_down(v, s):
    if s == 0:
        return v
    row = lax.broadcasted_iota(jnp.int32, v.shape, 0)
    return jnp.where(row >= s, pltpu.roll(v, s, 0), 0.0)


def _shift_up(v, s):
    if s == 0:
        return v
    n = v.shape[0]
    row = lax.broadcasted_iota(jnp.int32, v.shape, 0)
    return jnp.where(row < n - s, pltpu.roll(v, n - s, 0), 0.0)


def _conv_fwd_val(q, w_ref, k):
    out = q * w_ref[k - 1:k, :]
    for j in range(k - 1):
        out = out + _shift_down(q, k - 1 - j) * w_ref[j:j + 1, :]
    return out


def _conv_bwd_val(q, dv, w_ref, k):
    dq = dv * w_ref[k - 1:k, :]
    dws = []
    for j in range(k - 1):
        dq = dq + _shift_up(dv, k - 1 - j) * w_ref[j:j + 1, :]
        dws.append(jnp.sum(dv * _shift_down(q, k - 1 - j), axis=0, keepdims=True))
    dws.append(jnp.sum(dv * q, axis=0, keepdims=True))
    return dq, dws


def _pspec(t, off):
    base = off // CONV_CB
    return pl.BlockSpec((t, CONV_CB), lambda j: (0, base + j))


def _mix_a_fwd(p, conv_w):
    t = p.shape[0]

    def body(b_ref, c_ref, xa_ref, w_ref, o_ref):
        q = c_ref[...] * xa_ref[...]
        o_ref[...] = (b_ref[...] * _conv_fwd_val(q, w_ref, 3)).astype(BF)

    return pl.pallas_call(
        body, name="mix_a_fwd", grid=(D_MODEL // CONV_CB,),
        in_specs=[_pspec(t, OFF_B), _pspec(t, OFF_C), _pspec(t, OFF_XA),
                  pl.BlockSpec((3, CONV_CB), lambda j: (0, j))],
        out_specs=pl.BlockSpec((t, CONV_CB), lambda j: (0, j)),
        out_shape=jax.ShapeDtypeStruct((t, D_MODEL), BF), compiler_params=_params("parallel"))(p, p, p, conv_w)


def _mix_a_bwd(p, conv_w, dya):
    t = p.shape[0]

    def body(b_ref, c_ref, xa_ref, w_ref, dy_ref, db_ref, dc_ref, dxa_ref, dw_ref):
        cv = c_ref[...]
        xav = xa_ref[...]
        q = cv * xav
        va = _conv_fwd_val(q, w_ref, 3)
        dyv = dy_ref[...]
        db_ref[...] = (dyv * va).astype(BF)
        dq, dws = _conv_bwd_val(q, dyv * b_ref[...], w_ref, 3)
        dc_ref[...] = (dq * xav).astype(BF)
        dxa_ref[...] = (dq * cv).astype(BF)
        for j in range(3):
            dw_ref[j:j + 1, :] = dws[j]

    col = pl.BlockSpec((t, CONV_CB), lambda j: (0, j))
    wsp = pl.BlockSpec((3, CONV_CB), lambda j: (0, j))
    return pl.pallas_call(
        body, name="mix_a_bwd", grid=(D_MODEL // CONV_CB,),
        in_specs=[_pspec(t, OFF_B), _pspec(t, OFF_C), _pspec(t, OFF_XA), wsp, col],
        out_specs=[col, col, col, wsp],
        out_shape=[jax.ShapeDtypeStruct((t, D_MODEL), BF)] * 3 + [jax.ShapeDtypeStruct((3, D_MODEL), F32)],
        compiler_params=_params("parallel"))(p, p, p, conv_w, dya)


def _ssm_conv_fwd(p, conv_w, conv_b):
    t = p.shape[0]

    def body(x_ref, w_ref, b_ref, o_ref):
        pre = _conv_fwd_val(x_ref[...], w_ref, 4) + b_ref[...]
        o_ref[...] = pre * _sigmoid(pre)

    return pl.pallas_call(
        body, name="ssm_conv_fwd", grid=(D_XBC // CONV_CB,),
        in_specs=[_pspec(t, OFF_XBC), pl.BlockSpec((4, CONV_CB), lambda j: (0, j)),
                  pl.BlockSpec((1, CONV_CB), lambda j: (0, j))],
        out_specs=pl.BlockSpec((t, CONV_CB), lambda j: (0, j)),
        out_shape=jax.ShapeDtypeStruct((t, D_XBC), F32), compiler_params=_params("parallel"))(p, conv_w, conv_b)


def _ssm_conv_bwd(p, conv_w, conv_b, dxc):
    t = p.shape[0]

    def body(x_ref, w_ref, b_ref, d_ref, dx_ref, dw_ref, db_ref):
        xv = x_ref[...]
        pre = _conv_fwd_val(xv, w_ref, 4) + b_ref[...]
        s = _sigmoid(pre)
        dpre = d_ref[...] * (s * (1.0 + pre * (1.0 - s)))
        dq, dws = _conv_bwd_val(xv, dpre, w_ref, 4)
        dx_ref[...] = dq.astype(BF)
        for j in range(4):
            dw_ref[j:j + 1, :] = dws[j]
        db_ref[...] = jnp.sum(dpre, axis=0, keepdims=True)

    col = pl.BlockSpec((t, CONV_CB), lambda j: (0, j))
    wsp = pl.BlockSpec((4, CONV_CB), lambda j: (0, j))
    bsp = pl.BlockSpec((1, CONV_CB), lambda j: (0, j))
    return pl.pallas_call(
        body, name="ssm_conv_bwd", grid=(D_XBC // CONV_CB,),
        in_specs=[_pspec(t, OFF_XBC), wsp, bsp, col], out_specs=[col, wsp, bsp],
        out_shape=[jax.ShapeDtypeStruct((t, D_XBC), BF), jax.ShapeDtypeStruct((4, D_XBC), F32),
                   jax.ShapeDtypeStruct((1, D_XBC), F32)],
        compiler_params=_params("parallel"))(p, conv_w, conv_b, dxc)


DT_ROWS = 512


def _tri(lower):
    r = lax.broadcasted_iota(jnp.int32, (CHUNK, CHUNK), 0)
    c = lax.broadcasted_iota(jnp.int32, (CHUNK, CHUNK), 1)
    return jnp.where((r >= c) if lower else (r <= c), 1.0, 0.0).astype(F32)


def _dot_exact(a, b):
    return lax.dot_general(a, b, _DIMS["nn"], preferred_element_type=F32, precision=lax.Precision.HIGHEST)


def _dt_fwd(p, bias_pad, alog_pad):
    t = p.shape[0]

    def body(raw_ref, b_ref, al_ref, dt_ref, acs_ref):
        z = raw_ref[...] + b_ref[...]
        dt = jnp.maximum(z, 0.0) + jnp.log(1.0 + jnp.exp(-jnp.abs(z)))
        dt_ref[...] = dt
        a = dt * (-jnp.exp(al_ref[...]))
        tri = _tri(True)
        for k in range(DT_ROWS // CHUNK):
            acs_ref[k * CHUNK:(k + 1) * CHUNK, :] = _dot_exact(tri, a[k * CHUNK:(k + 1) * CHUNK, :])

    blk = pl.BlockSpec((DT_ROWS, DT_W), lambda i: (i, 0))
    vec = pl.BlockSpec((1, DT_W), lambda i: (0, 0))
    return pl.pallas_call(
        body, name="dt_fwd", grid=(t // DT_ROWS,),
        in_specs=[pl.BlockSpec((DT_ROWS, DT_W), lambda i: (i, OFF_DT // DT_W)), vec, vec],
        out_specs=[blk, blk], out_shape=[jax.ShapeDtypeStruct((t, DT_W), F32)] * 2,
        compiler_params=_params("parallel"))(p, bias_pad, alog_pad)


def _dt_bwd(p, bias_pad, alog_pad, dt, ddt, dacs):
    t = p.shape[0]

    def body(raw_ref, b_ref, al_ref, dt_ref, ddt_ref, dacs_ref, draw_ref, db_ref, dal_ref):
        i = pl.program_id(0)
        acoef = -jnp.exp(al_ref[...])
        triu = _tri(False)
        das = []
        for k in range(DT_ROWS // CHUNK):
            das.append(_dot_exact(triu, dacs_ref[k * CHUNK:(k + 1) * CHUNK, :]))
        da = jnp.concatenate(das, axis=0)
        dtv = dt_ref[...]
        ddt_tot = ddt_ref[...] + da * acoef
        lane = lax.broadcasted_iota(jnp.int32, (DT_ROWS, DT_W), 1)
        draw = jnp.where(lane < N_HEADS, ddt_tot * _sigmoid(raw_ref[...] + b_ref[...]), 0.0)
        draw_ref[...] = draw.astype(BF)
        pb = jnp.sum(draw, axis=0, keepdims=True)
        pa = jnp.sum(da * dtv * acoef, axis=0, keepdims=True)

        @pl.when(i == 0)
        def _():
            db_ref[...] = pb
            dal_ref[...] = pa

        @pl.when(i > 0)
        def _():
            db_ref[...] += pb
            dal_ref[...] += pa

    blk = pl.BlockSpec((DT_ROWS, DT_W), lambda i: (i, 0))
    vec = pl.BlockSpec((1, DT_W), lambda i: (0, 0))
    return pl.pallas_call(
        body, name="dt_bwd", grid=(t // DT_ROWS,),
        in_specs=[pl.BlockSpec((DT_ROWS, DT_W), lambda i: (i, OFF_DT // DT_W)), vec, vec, blk, blk, blk],
        out_specs=[blk, vec, vec],
        out_shape=[jax.ShapeDtypeStruct((t, DT_W), BF), jax.ShapeDtypeStruct((1, DT_W), F32),
                   jax.ShapeDtypeStruct((1, DT_W), F32)],
        compiler_params=_params("arbitrary"))(p, bias_pad, alog_pad, dt, ddt, dacs)


def _split_dot(z, onehot, terms):
    out = None
    rest = z
    for _ in range(terms):
        piece = rest.astype(BF)
        part = _dot(piece, onehot)
        out = part if out is None else out + part
        rest = rest - piece.astype(F32)
    return out


def _spread_mat(g):
    row = lax.broadcasted_iota(jnp.int32, (DT_W, GROUP_W), 0)
    lane = lax.broadcasted_iota(jnp.int32, (DT_W, GROUP_W), 1)
    return jnp.where(row == HEADS_PER_GROUP * g + lane // HEAD_DIM, 1.0, 0.0).astype(BF)


def _gather_mat(g):
    row = lax.broadcasted_iota(jnp.int32, (GROUP_W, DT_W), 0)
    lane = lax.broadcasted_iota(jnp.int32, (GROUP_W, DT_W), 1)
    return jnp.where(lane == HEADS_PER_GROUP * g + row // HEAD_DIM, 1.0, 0.0).astype(BF)


def _ssd_masks():
    row = lax.broadcasted_iota(jnp.int32, (CHUNK, GROUP_W), 0)
    col = lax.broadcasted_iota(jnp.int32, (CHUNK, GROUP_W), 1) % HEAD_DIM
    brow = lax.broadcasted_iota(jnp.int32, (GROUP_W, GROUP_W), 0) // HEAD_DIM
    bcol = lax.broadcasted_iota(jnp.int32, (GROUP_W, GROUP_W), 1) // HEAD_DIM
    return row >= col, row == col, brow == bcol


def _stack4(v):
    return jnp.concatenate([v, v, v, v], axis=0)


def _fold4(v):
    return v[0:CHUNK] + v[CHUNK:2 * CHUNK] + v[2 * CHUNK:3 * CHUNK] + v[3 * CHUNK:4 * CHUNK]


def _ssd_group(xc_ref, stacked, g, tri, eye, blockdiag):
    gs = slice(GROUP_W * g, GROUP_W * (g + 1))
    xs_g = xc_ref[:, gs]
    b_g = xc_ref[:, D_INNER + D_STATE * g:D_INNER + D_STATE * (g + 1)].astype(BF)
    c_g = xc_ref[:, D_INNER + 1024 + D_STATE * g:D_INNER + 1024 + D_STATE * (g + 1)].astype(BF)
    wide = _split_dot(stacked, _spread_mat(g), 3)
    acs_e, dt_e = wide[0:CHUNK], wide[CHUNK:2 * CHUNK]
    atot_e = acs_e[CHUNK - 1:CHUNK, :]
    acs_j = jnp.sum(jnp.where(eye, acs_e, 0.0), axis=0, keepdims=True)
    lmat = jnp.where(tri, jnp.exp(jnp.minimum(acs_e - acs_j, 0.0)), 0.0)
    b_t = _stack4(b_g)
    m = _dot(c_g, b_t, "nt") * lmat
    x_g = xs_g * dt_e
    xbd = jnp.where(blockdiag, _stack4(x_g), 0.0).astype(BF)
    return dict(gs=gs, xs=xs_g, b=b_g, c=c_g, b_t=b_t, dt=dt_e, e=jnp.exp(acs_e), dec=jnp.exp(atot_e - acs_e),
                eat=jnp.exp(atot_e), lmat=lmat, m=m, x=x_g, xbd=xbd)


def _ssd_fwd(xconv, dt, acs, d_exp, comm=None):
    t = xconv.shape[0]
    nc = t // CHUNK

    def body(xc_ref, dt_ref, acs_ref, d_ref, y_ref, hs_ref, state):
        c = pl.program_id(0)

        @pl.when(c == 0)
        def _():
            state[...] = jnp.zeros_like(state)

        hs_ref[...] = state[...]
        tri, eye, blockdiag = _ssd_masks()
        stacked = jnp.concatenate([acs_ref[...], dt_ref[...]], axis=0)
        for g in range(N_GROUPS):
            q = _ssd_group(xc_ref, stacked, g, tri, eye, blockdiag)
            gs = q["gs"]
            h_t = state[:, gs]
            ydiag = _dot(q["m"].astype(BF), q["xbd"])
            yoff = _dot(q["c"], h_t.astype(BF)) * q["e"]
            y_ref[:, gs] = ydiag + yoff + d_ref[:, gs] * q["xs"]
            s_t = _dot(q["b"], (q["x"] * q["dec"]).astype(BF), "tn")
            state[:, gs] = q["eat"] * h_t + s_t

    blk = lambda w: pl.BlockSpec((CHUNK, w), lambda c: (c, 0))
    outs, couts = _pcall(
        "ssd_fwd", body, (nc,),
        [blk(D_XBC), blk(DT_W), blk(DT_W), pl.BlockSpec((1, D_INNER), lambda c: (0, 0))],
        [blk(D_INNER), pl.BlockSpec((None, D_STATE, D_INNER), lambda c: (c, 0, 0))],
        [jax.ShapeDtypeStruct((t, D_INNER), F32), jax.ShapeDtypeStruct((nc, D_STATE, D_INNER), F32)],
        (xconv, dt, acs, d_exp), [pltpu.VMEM((D_STATE, D_INNER), F32)], ("arbitrary",), comm)
    return outs if comm is None else (outs, couts)


def _ssd_bwd(xconv, dt, acs, d_exp, hsave, dy, comm=None):
    t = xconv.shape[0]
    nc = t // CHUNK

    def body(xc_ref, dt_ref, acs_ref, d_ref, hs_ref, dy_ref, dxc_ref, ddt_ref, dacs_ref, dd_ref, dstate):
        c = pl.program_id(0)

        @pl.when(c == 0)
        def _():
            dstate[...] = jnp.zeros_like(dstate)
            dd_ref[...] = jnp.zeros_like(dd_ref)

        tri, eye, blockdiag = _ssd_masks()
        acsv = acs_ref[...]
        stacked = jnp.concatenate([acsv, dt_ref[...]], axis=0)
        eat_heads = jnp.exp(acsv[CHUNK - 1:CHUNK, :])
        ddt_acc = jnp.zeros((CHUNK, DT_W), F32)
        dacs_acc = jnp.zeros((CHUNK, DT_W), F32)
        datot_acc = jnp.zeros((1, DT_W), F32)

        for g in range(N_GROUPS):
            q = _ssd_group(xc_ref, stacked, g, tri, eye, blockdiag)
            gs, xs_g, b_g, c_g, m = q["gs"], q["xs"], q["b"], q["c"], q["m"]
            bs = slice(D_INNER + D_STATE * g, D_INNER + D_STATE * (g + 1))
            cs = slice(D_INNER + 1024 + D_STATE * g, D_INNER + 1024 + D_STATE * (g + 1))
            h_t = hs_ref[:, gs]
            h_b = h_t.astype(BF)
            dy_g = dy_ref[:, gs]
            dy_b = dy_g.astype(BF)
            ds_t = dstate[:, gs]
            ds_b = ds_t.astype(BF)

            yoff = _dot(c_g, h_b) * q["e"]
            edy = (q["e"] * dy_g).astype(BF)
            d_c = _dot(edy, h_b, "nt")
            d_ht = _dot(c_g, edy, "tn")
            bds = _dot(b_g, ds_b)
            xd = q["x"] * q["dec"]
            d_b = _dot(xd.astype(BF), ds_b, "nt")
            dm = _dot(dy_b, q["xbd"], "nt")
            cross = _dot(m.astype(BF), dy_b, "tn")
            dx_full = q["dec"] * bds + _fold4(jnp.where(blockdiag, cross, 0.0))
            dml = (dm * q["lmat"]).astype(BF)
            d_c = d_c + _dot(dml, q["b_t"])
            d_b = d_b + _fold4(_dot(dml, c_g, "tn"))
            w = dm * m
            q_dec = xd * bds
            z = w - jnp.where(eye, jnp.sum(w, axis=0, keepdims=True), 0.0) + dy_g * yoff - q_dec
            rows = jnp.concatenate(
                [jnp.sum(q_dec, axis=0, keepdims=True), jnp.sum(ds_t * h_t, axis=0, keepdims=True),
                 jnp.zeros((6, GROUP_W), F32)], axis=0)
            seg = _split_dot(jnp.concatenate([z, dx_full * xs_g, rows], axis=0), _gather_mat(g), 2)
            dacs_acc = dacs_acc + seg[0:CHUNK]
            ddt_acc = ddt_acc + seg[CHUNK:2 * CHUNK]
            datot_acc = datot_acc + seg[2 * CHUNK:2 * CHUNK + 1] + eat_heads * seg[2 * CHUNK + 1:2 * CHUNK + 2]
            dxc_ref[:, cs] = d_c
            dxc_ref[:, bs] = d_b
            dxc_ref[:, gs] = dx_full * q["dt"] + d_ref[:, gs] * dy_g
            dd_ref[:, gs] += jnp.sum(dy_g * xs_g, axis=0, keepdims=True)
            dstate[:, gs] = q["eat"] * ds_t + d_ht

        rowi = lax.broadcasted_iota(jnp.int32, (CHUNK, DT_W), 0)
        ddt_ref[...] = ddt_acc
        dacs_ref[...] = dacs_acc + jnp.where(rowi == CHUNK - 1, datot_acc, 0.0)

    rev = lambda w: pl.BlockSpec((CHUNK, w), lambda c: (nc - 1 - c, 0))
    vec = pl.BlockSpec((1, D_INNER), lambda c: (0, 0))
    outs, couts = _pcall(
        "ssd_bwd", body, (nc,),
        [rev(D_XBC), rev(DT_W), rev(DT_W), vec,
         pl.BlockSpec((None, D_STATE, D_INNER), lambda c: (nc - 1 - c, 0, 0)), rev(D_INNER)],
        [rev(D_XBC), rev(DT_W), rev(DT_W), vec],
        [jax.ShapeDtypeStruct((t, D_XBC), F32), jax.ShapeDtypeStruct((t, DT_W), F32),
         jax.ShapeDtypeStruct((t, DT_W), F32), jax.ShapeDtypeStruct((1, D_INNER), F32)],
        (xconv, dt, acs, d_exp, hsave, dy),
        [pltpu.VMEM((D_STATE, D_INNER), F32)], ("arbitrary",), comm)
    return outs if comm is None else (outs, couts)


def _gnorm_fwd(y, p, w, comm=None):
    t = y.shape[0]
    zoff = OFF_Z // GROUP_W

    def body(y_ref, z_ref, w_ref, o_ref):
        z = z_ref[...]
        yf = y_ref[...] * (z * _sigmoid(z))
        rstd = lax.rsqrt(jnp.mean(yf * yf, axis=-1, keepdims=True) + NORM_EPS)
        o_ref[...] = (yf * rstd * w_ref[...]).astype(BF)

    blk = pl.BlockSpec((TE, GROUP_W), lambda i, j: (i, j))
    out, couts = _pcall(
        "gnorm_fwd", body, (t // TE, N_GROUPS),
        [blk, pl.BlockSpec((TE, GROUP_W), lambda i, j: (i, zoff + j)), pl.BlockSpec((1, GROUP_W), lambda i, j: (0, j))],
        blk, jax.ShapeDtypeStruct((t, D_INNER), BF), (y, p, w), (), ("parallel", "parallel"), comm)
    return out if comm is None else (out, couts)


def _gnorm_bwd(y, p, w, dyn, comm=None):
    t = y.shape[0]
    zoff = OFF_Z // GROUP_W

    def body(y_ref, z_ref, w_ref, dn_ref, dy_ref, dz_ref, dw_ref):
        i = pl.program_id(1)
        z = z_ref[...]
        yv = y_ref[...]
        s = _sigmoid(z)
        sil = z * s
        yf = yv * sil
        rstd = lax.rsqrt(jnp.mean(yf * yf, axis=-1, keepdims=True) + NORM_EPS)
        xhat = yf * rstd
        dn = dn_ref[...]
        wd = dn * w_ref[...]
        proj = jnp.mean(wd * xhat, axis=-1, keepdims=True)
        dyf = rstd * (wd - xhat * proj)
        dy_ref[...] = dyf * sil
        dz_ref[...] = (dyf * yv * (s * (1.0 + z * (1.0 - s)))).astype(BF)
        part = jnp.sum(dn * xhat, axis=0, keepdims=True)

        @pl.when(i == 0)
        def _():
            dw_ref[...] = part

        @pl.when(i > 0)
        def _():
            dw_ref[...] += part

    blk = pl.BlockSpec((TE, GROUP_W), lambda j, i: (i, j))
    vec = pl.BlockSpec((1, GROUP_W), lambda j, i: (0, j))
    outs, couts = _pcall(
        "gnorm_bwd", body, (N_GROUPS, t // TE),
        [blk, pl.BlockSpec((TE, GROUP_W), lambda j, i: (i, zoff + j)), vec, blk],
        [blk, blk, vec],
        [jax.ShapeDtypeStruct((t, D_INNER), F32), jax.ShapeDtypeStruct((t, D_INNER), BF),
         jax.ShapeDtypeStruct((1, D_INNER), F32)],
        (y, p, w, dyn), (), ("parallel", "arbitrary"), comm)
    return outs if comm is None else (outs, couts)


MERGE_CB = 512


def _merge_fwd(p, ya, yb):
    t = ya.shape[0]

    def body(ga_ref, gb_ref, ya_ref, yb_ref, o_ref):
        o_ref[...] = (_sigmoid(ga_ref[...]) * ya_ref[...] + _sigmoid(gb_ref[...]) * yb_ref[...]).astype(BF)

    blk = pl.BlockSpec((TE, MERGE_CB), lambda i, j: (i, j))
    return pl.pallas_call(
        body, name="merge_fwd", grid=(t // TE, D_MODEL // MERGE_CB),
        in_specs=[pl.BlockSpec((TE, MERGE_CB), lambda i, j: (i, OFF_GA // MERGE_CB + j)),
                  pl.BlockSpec((TE, MERGE_CB), lambda i, j: (i, OFF_GB // MERGE_CB + j)), blk, blk],
        out_specs=blk, out_shape=jax.ShapeDtypeStruct((t, D_MODEL), BF),
        compiler_params=_params("parallel", "parallel"))(p, p, ya, yb)


def _merge_bwd(p, ya, yb, dm):
    t = ya.shape[0]

    def body(ga_ref, gb_ref, ya_ref, yb_ref, dm_ref, dga_ref, dgb_ref, dya_ref, dyb_ref):
        d = dm_ref[...]
        sa = _sigmoid(ga_ref[...])
        sb = _sigmoid(gb_ref[...])
        dga_ref[...] = (d * ya_ref[...] * sa * (1.0 - sa)).astype(BF)
        dgb_ref[...] = (d * yb_ref[...] * sb * (1.0 - sb)).astype(BF)
        dya_ref[...] = (d * sa).astype(BF)
        dyb_ref[...] = (d * sb).astype(BF)

    blk = pl.BlockSpec((TE, MERGE_CB), lambda i, j: (i, j))
    return pl.pallas_call(
        body, name="merge_bwd", grid=(t // TE, D_MODEL // MERGE_CB),
        in_specs=[pl.BlockSpec((TE, MERGE_CB), lambda i, j: (i, OFF_GA // MERGE_CB + j)),
                  pl.BlockSpec((TE, MERGE_CB), lambda i, j: (i, OFF_GB // MERGE_CB + j)), blk, blk, blk],
        out_specs=[blk] * 4, out_shape=[jax.ShapeDtypeStruct((t, D_MODEL), BF)] * 4,
        compiler_params=_params("parallel", "parallel"))(p, p, ya, yb, dm)


def _adamw(name, parts, w, m, v):
    r, c = w.shape
    tr = _row_tile(r)
    tc = ADAM_COL_TILE if (tr == r and r > 512 and c % ADAM_COL_TILE == 0) else c
    n_parts = parts.shape[0]
    bc1 = 1.0 - ADAM_B1 ** ADAM_STEP
    bc2 = 1.0 - ADAM_B2 ** ADAM_STEP

    def body(p_ref, w_ref, m_ref, v_ref, g_ref, d_ref, nm_ref, nv_ref):
        g = p_ref[0].astype(F32)
        for k in range(1, n_parts):
            g = g + p_ref[k].astype(F32)
        nm = ADAM_B1 * m_ref[...] + (1.0 - ADAM_B1) * g
        nv = ADAM_B2 * v_ref[...] + (1.0 - ADAM_B2) * (g * g)
        g_ref[...] = g
        nm_ref[...] = nm
        nv_ref[...] = nv
        d_ref[...] = -ADAM_LR * ((nm / bc1) / (jnp.sqrt(nv / bc2) + ADAM_EPS) + ADAM_WD * w_ref[...])

    blk = pl.BlockSpec((tr, tc), lambda i, j: (i, j))
    return pl.pallas_call(
        body, name=name, grid=(r // tr, c // tc),
        in_specs=[pl.BlockSpec((n_parts, tr, tc), lambda i, j: (0, i, j)), blk, blk, blk],
        out_specs=[blk] * 4, out_shape=[jax.ShapeDtypeStruct((r, c), F32)] * 4,
        compiler_params=_params("parallel", "parallel"))(parts, w, m, v)


def _pad_lanes(v, width):
    return jnp.pad(v, ((0, 0), (0, width - v.shape[1])))


def _reduce_start(slots, host):
    outs, sib = host(_pair_comm([a for _, a in slots]))
    sums = [(n, _add_pairs("pairsum_" + n, a, b)) for (n, a), b in zip(slots, sib)]
    return outs, sums


def _train_step(x, target, shard, rep):
    gdt = BF
    t = x.shape[0]
    recv = {}
    (got,) = _comm_call("gather_ffn1_in", _gather_comm([shard["ffn1_w_in"]], [True]))
    w1_in = got.reshape(2 * D_FF, D_MODEL)
    h1 = _rms_fwd("rms1_fwd", x, rep["ffn1_norm"])
    w_in_rows = [shard["w_in"][a:b] for a, b in W_IN_ROW_CUTS]
    gu1, got = _mm_nt("ffn1_in", h1, w1_in, tn=FF_HALF, comm=_gather_comm([shard["ffn1_w_out"], w_in_rows[0]]))
    w1_out = got[0].reshape(D_FF, D_MODEL)
    w_in_got = [got[1]]
    act1, got = _swiglu_fwd("swiglu1_fwd", gu1, comm=_gather_comm([w_in_rows[1]]))
    w_in_got.append(got[0])
    x1, got = _mm_nn("ffn1_out", act1, w1_out, res=x, alpha=0.5, comm=_gather_comm(
        [w_in_rows[2], shard["short_conv_w"], shard["ssm_conv_w"]]))
    w_in_got.append(got[0])
    short_conv_w = got[1].transpose(1, 0, 2).reshape(3, D_MODEL)
    ssm_conv_w = got[2].transpose(1, 0, 2).reshape(4, D_XBC)
    w_in_t = jnp.concatenate(w_in_got, axis=1).reshape(N_IN, D_MODEL)
    w_gd = jnp.concatenate([w_in_t[N_MAIN + N_HEADS:], w_in_t[N_MAIN:N_MAIN + N_HEADS],
                            jnp.zeros((DT_W - N_HEADS, D_MODEL), BF)], axis=0)

    h2 = _rms_fwd("rms2_fwd", x1, rep["mix_norm"])
    p, got = _mm_nt("proj_main", h2, w_in_t, n=N_MAIN, tn=1024, comm=_gather_comm(
        [shard["short_w_out"], shard["ssm_w_out"], shard["w_out"]]))
    p_gd = _mm_nt("proj_gd", h2, w_gd)
    short_w_out = got[0].reshape(D_MODEL, D_MODEL)
    ssm_w_out = got[1].reshape(D_INNER, D_MODEL)
    w_out = got[2].reshape(D_MODEL, D_MODEL)
    ya_in = _mix_a_fwd(p, short_conv_w)
    y_a = _mm_nn("short_out", ya_in, short_w_out)
    xconv = _ssm_conv_fwd(p, ssm_conv_w, rep["ssm_conv_b"])
    dt, acs = _dt_fwd(p_gd, rep["dt_bias_pad"], rep["a_log_pad"])
    (y_ssm, hsave), (got,) = _ssd_fwd(xconv, dt, acs, rep["d_exp"], comm=_gather_comm([shard["ffn2_w_in"]], [True]))
    w2_in = got.reshape(2 * D_FF, D_MODEL)
    yn, got = _gnorm_fwd(y_ssm, p, rep["ssm_norm"], comm=_gather_comm([shard["ffn2_w_out"]]))
    w2_out = got[0].reshape(D_FF, D_MODEL)
    y_b = _mm_nn("ssm_out", yn, ssm_w_out, tk=1024)
    merged = _merge_fwd(p_gd, y_a, y_b)
    x2 = _mm_nn("mix_out", merged, w_out, res=x1)

    h3 = _rms_fwd("rms3_fwd", x2, rep["ffn2_norm"])
    gu2 = _mm_nt("ffn2_in", h3, w2_in, tn=FF_HALF)
    act2 = _swiglu_fwd("swiglu2_fwd", gu2)
    x3 = _mm_nn("ffn2_out", act2, w2_out, res=x2, alpha=0.5)

    loss, dx3, dx3h, g_final = _final_loss(x3, rep["final_norm"], target)

    small = {"final_norm": g_final}
    dact2 = _mm_nt("ffn2_out_bwd_act", dx3h, w2_out)
    g_w2_out = _mm_tn("ffn2_out_bwd_w", act2, dx3h, gdt, tm=FF_HALF)
    dgu2 = _swiglu_bwd("swiglu2_bwd", gu2, dact2)
    g_w2_in = _mm_tn("ffn2_in_bwd_w", dgu2, h3, gdt, tm=FF_HALF)
    dh3 = _mm_nn("ffn2_in_bwd_h", dgu2, w2_in, tk=FF_HALF)
    dx2, dx2b, small["ffn2_norm"] = _rms_bwd("rms3_bwd", x2, rep["ffn2_norm"], dh3, dx3, 1.0)

    dmerged = _mm_nt("mix_out_bwd_x", dx2b, w_out)
    g_w_out = _mm_tn("mix_out_bwd_w", merged, dx2b, gdt)
    dga, dgb, dya, dyb = _merge_bwd(p_gd, y_a, y_b, dmerged)

    dya_in = _mm_nt("short_out_bwd_x", dya, short_w_out)
    g_short_w_out = _mm_tn("short_out_bwd_w", ya_in, dya, gdt)
    db, dc, dxa, g_short_conv = _mix_a_bwd(p, short_conv_w, dya_in)

    dyn = _mm_nt("ssm_out_bwd_x", dyb, ssm_w_out)
    g_ssm_w_out = _mm_tn("ssm_out_bwd_w", yn, dyb, gdt)
    late = [("ffn2_w_out", g_w2_out.reshape(N_DEV, FF_SHARD // 2, D_MODEL)),
            ("ffn2_w_in", g_w2_in.reshape(N_DEV, FF_SHARD, D_MODEL)),
            ("w_out", g_w_out.reshape(N_DEV, -1, D_MODEL)), ("short_w_out", g_short_w_out.reshape(N_DEV, -1, D_MODEL)),
            ("ssm_w_out", g_ssm_w_out.reshape(N_DEV, -1, D_MODEL))]
    (dy_ssm, dz, small["ssm_norm"]), sums = _reduce_start(
        late, lambda comm: _gnorm_bwd(y_ssm, p, rep["ssm_norm"], dyn, comm=comm))
    (dxconv, ddt, dacs, dd_lane), got = _ssd_bwd(
        xconv, dt, acs, rep["d_exp"], hsave, dy_ssm,
        comm=_chip_comm([a for _, a in sums], [n == "ffn2_w_in" for n, _ in sums]))
    recv.update({n: a for (n, _), a in zip(sums, got)})
    small["ssm_D"] = dd_lane.reshape(N_HEADS, HEAD_DIM).sum(axis=1)[None, :]
    dxbc, g_ssm_conv, small["ssm_conv_b"] = _ssm_conv_bwd(p, ssm_conv_w, rep["ssm_conv_b"], dxconv)
    draw, dbias, dalog = _dt_bwd(p_gd, rep["dt_bias_pad"], rep["a_log_pad"], dt, ddt, dacs)
    small["ssm_dt_bias"] = dbias[:, :N_HEADS]
    small["ssm_A_log"] = dalog[:, :N_HEADS]

    dp = jnp.concatenate([db, dc, dxa, dz, dxbc], axis=1)
    dp_gd = jnp.concatenate([dga, dgb, draw], axis=1)
    dh2 = _mm_nn("proj_main_bwd_x", dp, w_in_t, tk=1024)
    dh2 = _mm_nn("proj_gd_bwd_x", dp_gd, w_gd, res=dh2)
    g_main = _mm_tn("proj_main_bwd_w", dp, h2, gdt, tm=1024)
    g_gd = _mm_tn("proj_gd_bwd_w", dp_gd, h2, gdt)
    g_in_t = jnp.concatenate([g_main, g_gd[2048:2048 + N_HEADS], g_gd[0:2048]], axis=0).reshape(
        N_DEV, IN_SHARD, D_MODEL)
    w_rows = [("w_in%d" % i, g_in_t[:, a:b]) for i, (a, b) in enumerate(W_IN_ROW_CUTS)]
    (dx1, dx1h, small["mix_norm"]), w_sums = _reduce_start(
        w_rows, lambda comm: _rms_bwd("rms2_bwd", x1, rep["mix_norm"], dh2, dx2, 0.5, comm=comm))

    g_w1_out, got0 = _mm_tn("ffn1_out_bwd_w", act1, dx1h, gdt, tm=FF_HALF, comm=_chip_comm([w_sums[0][1]]))
    rest = [("ffn1_w_out", g_w1_out.reshape(N_DEV, FF_SHARD // 2, D_MODEL)),
            ("short_conv_w", g_short_conv.reshape(3, N_DEV, -1).transpose(1, 0, 2)),
            ("ssm_conv_w", g_ssm_conv.reshape(4, N_DEV, -1).transpose(1, 0, 2))]
    pair_rest = _pair_comm([a for _, a in rest])
    dact1, got = _mm_nt("ffn1_out_bwd_act", dx1h, w1_out, comm=_join_comm(_chip_comm([w_sums[1][1]]), pair_rest))
    got1, sib = got[0], got[1:]
    rest_sums = [(n, _add_pairs("pairsum_" + n, a, b)) for (n, a), b in zip(rest, sib)]
    dgu1, got2 = _swiglu_bwd("swiglu1_bwd", gu1, dact1, comm=_chip_comm([w_sums[2][1]]))
    recv["w_in"] = jnp.concatenate([got0[0], got1, got2[0]], axis=1)
    g_w1_in, got = _mm_tn("ffn1_in_bwd_w", dgu1, h1, gdt, tm=FF_HALF, comm=_chip_comm([a for _, a in rest_sums]))
    recv.update({n: a for (n, _), a in zip(rest_sums, got)})
    dh1, last_sums = _reduce_start(
        [("ffn1_w_in", g_w1_in.reshape(N_DEV, FF_SHARD, D_MODEL))],
        lambda comm: _mm_nn("ffn1_in_bwd_h", dgu1, w1_in, tk=FF_HALF, comm=comm))
    (dx0, _, small["ffn1_norm"]), got = _rms_bwd("rms1_bwd", x, rep["ffn1_norm"], dh1, dx1, 1.0,
                                                  comm=_chip_comm([last_sums[0][1]], [True]))
    recv["ffn1_w_in"] = got[0]
    packed = _pack_small(small, loss[:, 0:1])
    (small_parts,) = _comm_call("exchange_last", _gather_comm([packed]))
    return dx0, recv, small_parts


_SMALL = [("ffn1_norm", 1024), ("mix_norm", 1024), ("ssm_conv_b", 4096), ("ssm_dt_bias", 32), ("ssm_A_log", 32),
          ("ssm_D", 32), ("ssm_norm", 2048), ("ffn2_norm", 1024), ("final_norm", 1024)]
SMALL_W = 10368


def _pack_small(d, loss=None):
    parts = [d[n].reshape(1, -1).astype(F32) for n, _ in _SMALL]
    used = sum(sz for _, sz in _SMALL)
    tail = jnp.zeros((1, SMALL_W - used), F32)
    if loss is not None:
        tail = tail.at[:, 0:1].set(loss)
    return jnp.concatenate(parts + [tail], axis=1)


def _unpack_small(v, shapes):
    out, off = {}, 0
    for n, sz in _SMALL:
        out[n] = v[:, off:off + sz].reshape(shapes[n])
        off += sz
    return out, v[0, off]


_SHARDED = ["ffn1_w_in", "ffn1_w_out", "w_in", "short_conv_w", "short_w_out", "ssm_conv_w", "ssm_w_out", "w_out",
            "ffn2_w_in", "ffn2_w_out"]
_TRANSPOSED = ("ffn1_w_in", "w_in", "ffn2_w_in")
_ORDER = ["ffn1_norm", "ffn1_w_in", "ffn1_w_out", "mix_norm", "w_in", "short_conv_w", "short_w_out", "ssm_conv_w",
          "ssm_conv_b", "ssm_dt_bias", "ssm_A_log", "ssm_D", "ssm_norm", "ssm_w_out", "w_out", "ffn2_norm",
          "ffn2_w_in", "ffn2_w_out", "final_norm"]


def kernel(x, ffn1_norm, ffn1_w_in, ffn1_w_out, mix_norm, w_in, short_conv_w, short_w_out, ssm_conv_w, ssm_conv_b, ssm_dt_bias, ssm_A_log, ssm_D, ssm_norm, ssm_w_out, w_out, ffn2_norm, ffn2_w_in, ffn2_w_out, final_norm, loss_target, m_ffn1_norm, m_ffn1_w_in, m_ffn1_w_out, m_mix_norm, m_w_in, m_short_conv_w, m_short_w_out, m_ssm_conv_w, m_ssm_conv_b, m_ssm_dt_bias, m_ssm_A_log, m_ssm_D, m_ssm_norm, m_ssm_w_out, m_w_out, m_ffn2_norm, m_ffn2_w_in, m_ffn2_w_out, m_final_norm, v_ffn1_norm, v_ffn1_w_in, v_ffn1_w_out, v_mix_norm, v_w_in, v_short_conv_w, v_short_w_out, v_ssm_conv_w, v_ssm_conv_b, v_ssm_dt_bias, v_ssm_A_log, v_ssm_D, v_ssm_norm, v_ssm_w_out, v_w_out, v_ffn2_norm, v_ffn2_w_in, v_ffn2_w_out, v_final_norm):
    w = dict(ffn1_norm=ffn1_norm, ffn1_w_in=ffn1_w_in, ffn1_w_out=ffn1_w_out, mix_norm=mix_norm, w_in=w_in,
             short_conv_w=short_conv_w, short_w_out=short_w_out, ssm_conv_w=ssm_conv_w, ssm_conv_b=ssm_conv_b,
             ssm_dt_bias=ssm_dt_bias, ssm_A_log=ssm_A_log, ssm_D=ssm_D, ssm_norm=ssm_norm, ssm_w_out=ssm_w_out,
             w_out=w_out, ffn2_norm=ffn2_norm, ffn2_w_in=ffn2_w_in, ffn2_w_out=ffn2_w_out, final_norm=final_norm)
    m = dict(ffn1_norm=m_ffn1_norm, ffn1_w_in=m_ffn1_w_in, ffn1_w_out=m_ffn1_w_out, mix_norm=m_mix_norm, w_in=m_w_in,
             short_conv_w=m_short_conv_w, short_w_out=m_short_w_out, ssm_conv_w=m_ssm_conv_w,
             ssm_conv_b=m_ssm_conv_b, ssm_dt_bias=m_ssm_dt_bias, ssm_A_log=m_ssm_A_log, ssm_D=m_ssm_D,
             ssm_norm=m_ssm_norm, ssm_w_out=m_ssm_w_out, w_out=m_w_out, ffn2_norm=m_ffn2_norm,
             ffn2_w_in=m_ffn2_w_in, ffn2_w_out=m_ffn2_w_out, final_norm=m_final_norm)
    v = dict(ffn1_norm=v_ffn1_norm, ffn1_w_in=v_ffn1_w_in, ffn1_w_out=v_ffn1_w_out, mix_norm=v_mix_norm, w_in=v_w_in,
             short_conv_w=v_short_conv_w, short_w_out=v_short_w_out, ssm_conv_w=v_ssm_conv_w,
             ssm_conv_b=v_ssm_conv_b, ssm_dt_bias=v_ssm_dt_bias, ssm_A_log=v_ssm_A_log, ssm_D=v_ssm_D,
             ssm_norm=v_ssm_norm, ssm_w_out=v_ssm_w_out, w_out=v_w_out, ffn2_norm=v_ffn2_norm,
             ffn2_w_in=v_ffn2_w_in, ffn2_w_out=v_ffn2_w_out, final_norm=v_final_norm)
    shapes = {n: w[n].shape for n in _ORDER}

    def local(d, n):
        return d[n][0].T if n in _TRANSPOSED else d[n][0]

    shard = {n: local(w, n) for n in _SHARDED}

    wire = {n: (shard[n] if n in ("short_conv_w", "ssm_conv_w") else shard[n].astype(BF)) for n in _SHARDED}
    rep = {
        "ffn1_norm": ffn1_norm, "mix_norm": mix_norm, "ffn2_norm": ffn2_norm, "ssm_norm": ssm_norm,
        "ssm_conv_b": ssm_conv_b, "final_norm": final_norm.reshape(1, D_MODEL),
        "dt_bias_pad": _pad_lanes(ssm_dt_bias, DT_W), "a_log_pad": _pad_lanes(ssm_A_log, DT_W),
        "d_exp": jnp.repeat(ssm_D, HEAD_DIM, axis=1),
    }
    grad_x, parts, small_parts = _train_step(x[0], loss_target[0], wire, rep)

    out_g, out_d, out_m, out_v = {}, {}, {}, {}
    for n in _SHARDED:
        res = _adamw("adamw_" + n, parts[n], shard[n], local(m, n), local(v, n))
        out_g[n], out_d[n], out_m[n], out_v[n] = [(r.T if n in _TRANSPOSED else r).reshape(shapes[n]) for r in res]
    sres = _adamw("adamw_small", small_parts, _pack_small(w), _pack_small(m), _pack_small(v))
    sg, loss = _unpack_small(sres[0], shapes)
    sd, _ = _unpack_small(sres[1], shapes)
    sm, _ = _unpack_small(sres[2], shapes)
    sv, _ = _unpack_small(sres[3], shapes)
    out_g.update(sg)
    out_d.update(sd)
    out_m.update(sm)
    out_v.update(sv)
    return (loss, grad_x[None], *[out_g[n] for n in _ORDER], *[out_d[n] for n in _ORDER],
            *[out_m[n] for n in _ORDER], *[out_v[n] for n in _ORDER])
```

```python
import functools

import jax
import jax.numpy as jnp
from jax import lax
from jax.experimental import pallas as pl
from jax.experimental.pallas import tpu as pltpu

F32 = jnp.float32
BF = jnp.bfloat16

N_DEV = 8
D_MODEL = 1024
D_FF = 2816
D_INNER = 2048
D_XBC = 4096
N_HEADS = 32
HEAD_DIM = 64
N_GROUPS = 8
D_STATE = 128
CHUNK = 64
GROUP_W = D_INNER // N_GROUPS
HEADS_PER_GROUP = N_HEADS // N_GROUPS
NORM_EPS = 1e-5
N_IN = 11296
FF_SHARD = 2 * D_FF // N_DEV
FF_HALF = D_FF // 2
IN_SHARD = N_IN // N_DEV

OFF_B, OFF_C, OFF_XA, OFF_Z, OFF_XBC = 0, 1024, 2048, 3072, 5120
N_MAIN = 9216
OFF_GA, OFF_GB, OFF_DT = 0, 1024, 2048
DT_W = 128
N_GD = 2048 + DT_W
W_IN_ROW_CUTS = [(0, 480), (480, 944), (944, 1412)]

ADAM_LR, ADAM_B1, ADAM_B2, ADAM_EPS, ADAM_WD, ADAM_STEP = 0.001, 0.9, 0.999, 1e-08, 0.01, 10

VMEM_LIMIT_V7X = 56 * 1024 * 1024
TM = 1024
TE = 256
ADAM_COL_TILE = 256


def _params(*sem):
    return pltpu.CompilerParams(dimension_semantics=sem, vmem_limit_bytes=VMEM_LIMIT_V7X)


_DIMS = {
    "nn": (((1,), (0,)), ((), ())),
    "nt": (((1,), (1,)), ((), ())),
    "tn": (((0,), (0,)), ((), ())),
}


def _dot(a, b, mode="nn"):
    return lax.dot_general(a, b, _DIMS[mode], preferred_element_type=F32)


def _sigmoid(x):
    return 1.0 / (1.0 + jnp.exp(-x))


class _Comm:
    def __init__(self, inputs, out_shapes, sems, start, finish):
        self.inputs, self.out_shapes, self.sems, self.start, self.finish = inputs, out_shapes, sems, start, finish


def _pcall(name, body, grid, in_specs, out_specs, out_shape, args, scratch=(), sem=None, comm=None):
    single = not isinstance(out_shape, (list, tuple))
    out_shapes = [out_shape] if single else list(out_shape)
    out_specs = [out_specs] if single else list(out_specs)
    n_in, n_out, n_scr = len(args), len(out_shapes), len(scratch)
    if comm is None:
        res = pl.pallas_call(
            body, name=name, grid=grid, in_specs=list(in_specs), out_specs=out_specs, out_shape=out_shapes,
            scratch_shapes=list(scratch), compiler_params=_params(*sem))(*args)
        return (res[0] if single else res), []
    nci, nco = len(comm.inputs), len(comm.out_shapes)

    def wrapped(*refs):
        a = refs[:n_in]
        ci = refs[n_in:n_in + nci]
        o0 = n_in + nci
        o = refs[o0:o0 + n_out]
        co = refs[o0 + n_out:o0 + n_out + nco]
        s0 = o0 + n_out + nco
        s = refs[s0:s0 + n_scr]
        cs = refs[s0 + n_scr:]
        pids = [pl.program_id(i) for i in range(len(grid))]
        first = functools.reduce(jnp.logical_and, [p == 0 for p in pids])
        last = functools.reduce(jnp.logical_and, [p == g - 1 for p, g in zip(pids, grid)])

        @pl.when(first)
        def _():
            comm.start(ci, co, cs)

        body(*a, *o, *s)

        @pl.when(last)
        def _():
            comm.finish(ci, co, cs)

    any_spec = pl.BlockSpec(memory_space=pl.ANY)
    res = pl.pallas_call(
        wrapped, name=name, grid=grid, in_specs=list(in_specs) + [any_spec] * nci,
        out_specs=out_specs + [any_spec] * nco, out_shape=out_shapes + list(comm.out_shapes),
        scratch_shapes=list(scratch) + list(comm.sems),
        compiler_params=_params(*(("arbitrary",) * len(grid))))(*args, *comm.inputs)
    core = res[:n_out]
    return (core[0] if single else core), list(res[n_out:])


def _comm_call(name, comm):
    nci, nco = len(comm.inputs), len(comm.out_shapes)

    def body(*refs):
        ci, co, cs = refs[:nci], refs[nci:nci + nco], refs[nci + nco:]
        comm.start(ci, co, cs)
        comm.finish(ci, co, cs)

    any_spec = pl.BlockSpec(memory_space=pl.ANY)
    return pl.pallas_call(
        body, name=name, in_specs=[any_spec] * nci, out_specs=[any_spec] * nco, out_shape=list(comm.out_shapes),
        scratch_shapes=list(comm.sems), compiler_params=pltpu.CompilerParams(has_side_effects=True))(*comm.inputs)


def _remote(src, dst, ssem, rsem, dev):
    return pltpu.make_async_remote_copy(src_ref=src, dst_ref=dst, send_sem=ssem, recv_sem=rsem, device_id=dev,
                                        device_id_type=pl.DeviceIdType.MESH)


def _place():
    x, y, c = lax.axis_index("x"), lax.axis_index("y"), lax.axis_index("c")
    other_chips = [(1 - x, y), (x, 1 - y), (1 - x, 1 - y)]
    return x, y, c, other_chips


def _slot(x, y, c, swap):
    return 4 * y + 2 * x + c if swap else 4 * x + 2 * y + c


def _chip_slot(x, y, swap):
    return 2 * y + x if swap else 2 * x + y


def _gather_comm(shards, swaps=None):
    n = len(shards)
    per = N_DEV - 1
    swaps = [False] * n if swaps is None else swaps

    def start(ins, outs, sems):
        send, recv, loc = sems
        x, y, c, chips = _place()
        for i in range(n):
            me = _slot(x, y, c, swaps[i])
            pltpu.make_async_copy(ins[i], outs[i].at[me], loc.at[i]).start()
            _remote(ins[i], outs[i].at[me], send.at[per * i], recv.at[per * i], (x, y, 1 - c)).start()
            for j, (qx, qy) in enumerate(chips):
                _remote(ins[i], outs[i].at[me], send.at[per * i + 1 + j], recv.at[per * i + 1 + j], (qx, qy, c)).start()

    def finish(ins, outs, sems):
        send, recv, loc = sems
        x, y, c, chips = _place()
        sib = (x, y, 1 - c)
        for i in range(n):
            for j, (qx, qy) in enumerate(chips):
                blk = outs[i].at[_slot(qx, qy, c, swaps[i])]
                _remote(blk, blk, send.at[per * i + 1 + j], recv.at[per * i + 1 + j], (qx, qy, c)).wait_recv()
                _remote(blk, blk, send.at[per * i + 4 + j], recv.at[per * i + 4 + j], sib).start()
        for i in range(n):
            blk = outs[i].at[_slot(x, y, 1 - c, swaps[i])]
            _remote(blk, blk, send.at[per * i], recv.at[per * i], sib).wait_recv()
            for j, (qx, qy) in enumerate(chips):
                blk = outs[i].at[_slot(qx, qy, 1 - c, swaps[i])]
                _remote(blk, blk, send.at[per * i + 4 + j], recv.at[per * i + 4 + j], sib).wait_recv()
        for i in range(n):
            own = outs[i].at[_slot(x, y, c, swaps[i])]
            for k in range(per):
                _remote(ins[i], own, send.at[per * i + k], recv.at[per * i + k], sib).wait_send()
            pltpu.make_async_copy(ins[i], own, loc.at[i]).wait()

    out_shapes = [jax.ShapeDtypeStruct((N_DEV,) + tuple(a.shape), a.dtype) for a in shards]
    sems = [pltpu.SemaphoreType.DMA((per * n,)), pltpu.SemaphoreType.DMA((per * n,)), pltpu.SemaphoreType.DMA((n,))]
    return _Comm(list(shards), out_shapes, sems, start, finish)


def _pair_comm(slots):
    n = len(slots)

    def copies(ins, outs, sems):
        send, recv = sems
        x, y, c, _ = _place()
        sib = (x, y, 1 - c)
        out = []
        for i in range(n):
            for q in range(4):
                out.append(_remote(ins[i].at[2 * q + 1 - c], outs[i].at[q], send.at[4 * i + q], recv.at[4 * i + q], sib))
        return out

    def start(ins, outs, sems):
        for cp in copies(ins, outs, sems):
            cp.start()

    def finish(ins, outs, sems):
        for cp in copies(ins, outs, sems):
            cp.wait_send()
            cp.wait_recv()

    out_shapes = [jax.ShapeDtypeStruct((4,) + tuple(a.shape[1:]), a.dtype) for a in slots]
    sems = [pltpu.SemaphoreType.DMA((4 * n,)), pltpu.SemaphoreType.DMA((4 * n,))]
    return _Comm(list(slots), out_shapes, sems, start, finish)


def _chip_comm(chip_sums, swaps=None):
    n = len(chip_sums)
    swaps = [False] * n if swaps is None else swaps

    def start(ins, outs, sems):
        send, recv, loc = sems
        x, y, c, chips = _place()
        for i in range(n):
            mine = _chip_slot(x, y, swaps[i])
            pltpu.make_async_copy(ins[i].at[mine], outs[i].at[mine], loc.at[i]).start()
            for j, (qx, qy) in enumerate(chips):
                _remote(ins[i].at[_chip_slot(qx, qy, swaps[i])], outs[i].at[mine], send.at[3 * i + j],
                        recv.at[3 * i + j], (qx, qy, c)).start()

    def finish(ins, outs, sems):
        send, recv, loc = sems
        x, y, c, chips = _place()
        for i in range(n):
            mine = _chip_slot(x, y, swaps[i])
            for j, (qx, qy) in enumerate(chips):
                theirs = _chip_slot(qx, qy, swaps[i])
                cp = _remote(ins[i].at[theirs], outs[i].at[theirs], send.at[3 * i + j], recv.at[3 * i + j], (qx, qy, c))
                cp.wait_send()
                cp.wait_recv()
            pltpu.make_async_copy(ins[i].at[mine], outs[i].at[mine], loc.at[i]).wait()

    out_shapes = [jax.ShapeDtypeStruct(a.shape, a.dtype) for a in chip_sums]
    sems = [pltpu.SemaphoreType.DMA((3 * n,)), pltpu.SemaphoreType.DMA((3 * n,)), pltpu.SemaphoreType.DMA((n,))]
    return _Comm(list(chip_sums), out_shapes, sems, start, finish)


def _join_comm(a, b):
    na_i, na_o, na_s = len(a.inputs), len(a.out_shapes), len(a.sems)

    def start(ins, outs, sems):
        a.start(ins[:na_i], outs[:na_o], sems[:na_s])
        b.start(ins[na_i:], outs[na_o:], sems[na_s:])

    def finish(ins, outs, sems):
        a.finish(ins[:na_i], outs[:na_o], sems[:na_s])
        b.finish(ins[na_i:], outs[na_o:], sems[na_s:])

    return _Comm(a.inputs + b.inputs, a.out_shapes + b.out_shapes, a.sems + b.sems, start, finish)


def _row_tile(r):
    for cand in (256, 128):
        if r > cand and r % cand == 0:
            return cand
    return r


def _add_pairs(name, slots, sib):
    r, c = slots.shape[1:]
    tr = _row_tile(r)

    def body(s_ref, b_ref, o_ref):
        core = lax.axis_index("c")
        o_ref[...] = (s_ref[core].astype(F32) + b_ref[...].astype(F32)).astype(o_ref.dtype)

    return pl.pallas_call(
        body, name=name, grid=(4, r // tr),
        in_specs=[pl.BlockSpec((None, 2, tr, c), lambda q, i: (q, 0, i, 0)),
                  pl.BlockSpec((None, tr, c), lambda q, i: (q, i, 0))],
        out_specs=pl.BlockSpec((None, tr, c), lambda q, i: (q, i, 0)),
        out_shape=jax.ShapeDtypeStruct((4, r, c), slots.dtype),
        compiler_params=_params("parallel", "parallel"))(slots.reshape(4, 2, r, c), sib)


def _matmul(name, mode, a, b, grid, a_spec, b_spec, o_spec, out_shape, acc_shape,
            res=None, res_spec=None, alpha=1.0, comm=None):
    nk = grid[-1]
    has_res = res is not None

    def body(*refs):
        if has_res:
            a_ref, b_ref, r_ref, o_ref = refs[:4]
        else:
            a_ref, b_ref, o_ref = refs[:3]
            r_ref = None
        part = _dot(a_ref[...], b_ref[...], mode)

        def finish(v):
            if alpha != 1.0:
                v = v * alpha
            if has_res:
                v = r_ref[...] + v
            o_ref[...] = v.astype(o_ref.dtype)

        if nk == 1:
            finish(part)
        else:
            acc = refs[-1]
            k = pl.program_id(len(grid) - 1)

            @pl.when(k == 0)
            def _():
                acc[...] = part

            @pl.when(k > 0)
            def _():
                acc[...] += part

            @pl.when(k == nk - 1)
            def _():
                finish(acc[...])

    in_specs = [a_spec, b_spec] + ([res_spec] if has_res else [])
    args = (a, b) + ((res,) if has_res else ())
    scratch = [] if nk == 1 else [pltpu.VMEM(acc_shape, F32)]
    sem = ("parallel",) * (len(grid) - 1) + ("arbitrary",)
    out, couts = _pcall(name, body, grid, in_specs, o_spec, out_shape, args, scratch, sem, comm)
    return out if comm is None else (out, couts)


def _mm_nn(name, a, b, out_dtype=F32, res=None, alpha=1.0, tk=None, comm=None):
    t, kk = a.shape
    n = b.shape[1]
    tk = kk if tk is None else tk
    grid = (t // TM, 1, kk // tk)
    return _matmul(
        name, "nn", a, b, grid,
        pl.BlockSpec((TM, tk), lambda i, j, k: (i, k)),
        pl.BlockSpec((tk, n), lambda i, j, k: (k, 0)),
        pl.BlockSpec((TM, n), lambda i, j, k: (i, 0)),
        jax.ShapeDtypeStruct((t, n), out_dtype), (TM, n),
        res=res, res_spec=pl.BlockSpec((TM, n), lambda i, j, k: (i, 0)), alpha=alpha, comm=comm)


def _mm_nt(name, a, b, n=None, tn=None, tk=None, out_dtype=F32, comm=None):
    t, kk = a.shape
    n = b.shape[0] if n is None else n
    tn = n if tn is None else tn
    tk = kk if tk is None else tk
    grid = (n // tn, t // TM, kk // tk)
    return _matmul(
        name, "nt", a, b, grid,
        pl.BlockSpec((TM, tk), lambda j, i, k: (i, k)),
        pl.BlockSpec((tn, tk), lambda j, i, k: (j, k)),
        pl.BlockSpec((TM, tn), lambda j, i, k: (i, j)),
        jax.ShapeDtypeStruct((t, n), out_dtype), (TM, tn), comm=comm)


def _mm_tn(name, a, b, out_dtype, tm=None, comm=None):
    t, m = a.shape
    n = b.shape[1]
    tm = m if tm is None else tm
    grid = (m // tm, 1, t // TM)
    return _matmul(
        name, "tn", a, b, grid,
        pl.BlockSpec((TM, tm), lambda j, i, k: (k, j)),
        pl.BlockSpec((TM, n), lambda j, i, k: (k, 0)),
        pl.BlockSpec((tm, n), lambda j, i, k: (j, 0)),
        jax.ShapeDtypeStruct((m, n), out_dtype), (tm, n), comm=comm)


def _rms_fwd(name, x, w):
    t, d = x.shape

    def body(x_ref, w_ref, h_ref):
        xv = x_ref[...]
        rstd = lax.rsqrt(jnp.mean(xv * xv, axis=-1, keepdims=True) + NORM_EPS)
        h_ref[...] = (xv * rstd * w_ref[...]).astype(h_ref.dtype)

    return pl.pallas_call(
        body, name=name, grid=(t // TE,),
        in_specs=[pl.BlockSpec((TE, d), lambda i: (i, 0)), pl.BlockSpec((1, d), lambda i: (0, 0))],
        out_specs=pl.BlockSpec((TE, d), lambda i: (i, 0)),
        out_shape=jax.ShapeDtypeStruct((t, d), BF), compiler_params=_params("parallel"))(x, w)


def _rms_bwd(name, x, w, dh, dres, out_scale, comm=None):
    t, d = x.shape

    def body(x_ref, w_ref, dh_ref, dres_ref, dx_ref, dxb_ref, dw_ref):
        i = pl.program_id(0)
        xv = x_ref[...]
        rstd = lax.rsqrt(jnp.mean(xv * xv, axis=-1, keepdims=True) + NORM_EPS)
        xhat = xv * rstd
        dhv = dh_ref[...]
        wd = dhv * w_ref[...]
        proj = jnp.mean(wd * xhat, axis=-1, keepdims=True)
        dx = dres_ref[...] + rstd * (wd - xhat * proj)
        dx_ref[...] = dx
        dxb_ref[...] = (dx * out_scale).astype(BF)
        part = jnp.sum(dhv * xhat, axis=0, keepdims=True)

        @pl.when(i == 0)
        def _():
            dw_ref[...] = part

        @pl.when(i > 0)
        def _():
            dw_ref[...] += part

    row = pl.BlockSpec((TE, d), lambda i: (i, 0))
    vec = pl.BlockSpec((1, d), lambda i: (0, 0))
    outs, couts = _pcall(
        name, body, (t // TE,), [row, vec, row, row], [row, row, vec],
        [jax.ShapeDtypeStruct((t, d), F32), jax.ShapeDtypeStruct((t, d), BF), jax.ShapeDtypeStruct((1, d), F32)],
        (x, w, dh, dres), (), ("arbitrary",), comm)
    return outs if comm is None else (outs, couts)


def _final_loss(x, w, target):
    t, d = x.shape

    def body(x_ref, w_ref, t_ref, loss_ref, dx_ref, dxb_ref, dw_ref):
        i = pl.program_id(0)
        xv = x_ref[...]
        rstd = lax.rsqrt(jnp.mean(xv * xv, axis=-1, keepdims=True) + NORM_EPS)
        xhat = xv * rstd
        err = xhat * w_ref[...] - t_ref[...]
        lpart = 0.5 * jnp.sum(jnp.mean(err * err, axis=-1, keepdims=True), axis=0, keepdims=True)
        dy = err * (1.0 / d)
        wd = dy * w_ref[...]
        proj = jnp.mean(wd * xhat, axis=-1, keepdims=True)
        dx = rstd * (wd - xhat * proj)
        dx_ref[...] = dx
        dxb_ref[...] = (0.5 * dx).astype(BF)
        part = jnp.sum(dy * xhat, axis=0, keepdims=True)
        lfull = jnp.broadcast_to(lpart, (1, 128))

        @pl.when(i == 0)
        def _():
            dw_ref[...] = part
            loss_ref[...] = lfull

        @pl.when(i > 0)
        def _():
            dw_ref[...] += part
            loss_ref[...] += lfull

    row = pl.BlockSpec((TE, d), lambda i: (i, 0))
    vec = pl.BlockSpec((1, d), lambda i: (0, 0))
    return pl.pallas_call(
        body, name="final_loss", grid=(t // TE,), in_specs=[row, vec, row],
        out_specs=[pl.BlockSpec((1, 128), lambda i: (0, 0)), row, row, vec],
        out_shape=[jax.ShapeDtypeStruct((1, 128), F32), jax.ShapeDtypeStruct((t, d), F32),
                   jax.ShapeDtypeStruct((t, d), BF), jax.ShapeDtypeStruct((1, d), F32)],
        compiler_params=_params("arbitrary"))(x, w, target)


def _swiglu_fwd(name, gu, comm=None):
    t = gu.shape[0]

    def body(g_ref, u_ref, a_ref):
        g = g_ref[...]
        a_ref[...] = (g * _sigmoid(g) * u_ref[...]).astype(BF)

    blk = (TE, FF_HALF)
    out, couts = _pcall(
        name, body, (t // TE, 2),
        [pl.BlockSpec(blk, lambda i, j: (i, 2 * j)), pl.BlockSpec(blk, lambda i, j: (i, 2 * j + 1))],
        pl.BlockSpec(blk, lambda i, j: (i, j)), jax.ShapeDtypeStruct((t, D_FF), BF),
        (gu, gu), (), ("parallel", "parallel"), comm)
    return out if comm is None else (out, couts)


def _swiglu_bwd(name, gu, dact, comm=None):
    t = gu.shape[0]

    def body(g_ref, u_ref, da_ref, o_ref):
        g = g_ref[...]
        da = da_ref[...]
        s = _sigmoid(g)
        o_ref[:, 0:FF_HALF] = (da * u_ref[...] * (s * (1.0 + g * (1.0 - s)))).astype(BF)
        o_ref[:, FF_HALF:2 * FF_HALF] = (da * g * s).astype(BF)

    blk = (TE, FF_HALF)
    out, couts = _pcall(
        name, body, (t // TE, 2),
        [pl.BlockSpec(blk, lambda i, j: (i, 2 * j)), pl.BlockSpec(blk, lambda i, j: (i, 2 * j + 1)),
         pl.BlockSpec(blk, lambda i, j: (i, j))],
        pl.BlockSpec((TE, 2 * FF_HALF), lambda i, j: (i, j)),
        jax.ShapeDtypeStruct((t, 2 * D_FF), BF), (gu, gu, dact), (), ("parallel", "parallel"), comm)
    return out if comm is None else (out, couts)


CONV_CB = 256


def _shift_down(v, s):
    if s == 0:
        return v
    row = lax.broadcasted_iota(jnp.int32, v.shape, 0)
    return jnp.where(row >= s, pltpu.roll(v, s, 0), 0.0)


def _shift_up(v, s):
    if s == 0:
        return v
    n = v.shape[0]
    row = lax.broadcasted_iota(jnp.int32, v.shape, 0)
    return jnp.where(row < n - s, pltpu.roll(v, n - s, 0), 0.0)


def _conv_fwd_val(q, w_ref, k):
    out = q * w_ref[k - 1:k, :]
    for j in range(k - 1):
        out = out + _shift_down(q, k - 1 - j) * w_ref[j:j + 1, :]
    return out


def _conv_bwd_val(q, dv, w_ref, k):
    dq = dv * w_ref[k - 1:k, :]
    dws = []
    for j in range(k - 1):
        dq = dq + _shift_up(dv, k - 1 - j) * w_ref[j:j + 1, :]
        dws.append(jnp.sum(dv * _shift_down(q, k - 1 - j), axis=0, keepdims=True))
    dws.append(jnp.sum(dv * q, axis=0, keepdims=True))
    return dq, dws


def _pspec(t, off):
    base = off // CONV_CB
    return pl.BlockSpec((t, CONV_CB), lambda j: (0, base + j))


def _mix_a_fwd(p, conv_w):
    t = p.shape[0]

    def body(b_ref, c_ref, xa_ref, w_ref, o_ref):
        q = c_ref[...] * xa_ref[...]
        o_ref[...] = (b_ref[...] * _conv_fwd_val(q, w_ref, 3)).astype(BF)

    return pl.pallas_call(
        body, name="mix_a_fwd", grid=(D_MODEL // CONV_CB,),
        in_specs=[_pspec(t, OFF_B), _pspec(t, OFF_C), _pspec(t, OFF_XA),
                  pl.BlockSpec((3, CONV_CB), lambda j: (0, j))],
        out_specs=pl.BlockSpec((t, CONV_CB), lambda j: (0, j)),
        out_shape=jax.ShapeDtypeStruct((t, D_MODEL), BF), compiler_params=_params("parallel"))(p, p, p, conv_w)


def _mix_a_bwd(p, conv_w, dya):
    t = p.shape[0]

    def body(b_ref, c_ref, xa_ref, w_ref, dy_ref, db_ref, dc_ref, dxa_ref, dw_ref):
        cv = c_ref[...]
        xav = xa_ref[...]
        q = cv * xav
        va = _conv_fwd_val(q, w_ref, 3)
        dyv = dy_ref[...]
        db_ref[...] = (dyv * va).astype(BF)
        dq, dws = _conv_bwd_val(q, dyv * b_ref[...], w_ref, 3)
        dc_ref[...] = (dq * xav).astype(BF)
        dxa_ref[...] = (dq * cv).astype(BF)
        for j in range(3):
            dw_ref[j:j + 1, :] = dws[j]

    col = pl.BlockSpec((t, CONV_CB), lambda j: (0, j))
    wsp = pl.BlockSpec((3, CONV_CB), lambda j: (0, j))
    return pl.pallas_call(
        body, name="mix_a_bwd", grid=(D_MODEL // CONV_CB,),
        in_specs=[_pspec(t, OFF_B), _pspec(t, OFF_C), _pspec(t, OFF_XA), wsp, col],
        out_specs=[col, col, col, wsp],
        out_shape=[jax.ShapeDtypeStruct((t, D_MODEL), BF)] * 3 + [jax.ShapeDtypeStruct((3, D_MODEL), F32)],
        compiler_params=_params("parallel"))(p, p, p, conv_w, dya)


def _ssm_conv_fwd(p, conv_w, conv_b):
    t = p.shape[0]

    def body(x_ref, w_ref, b_ref, o_ref):
        pre = _conv_fwd_val(x_ref[...], w_ref, 4) + b_ref[...]
        o_ref[...] = pre * _sigmoid(pre)

    return pl.pallas_call(
        body, name="ssm_conv_fwd", grid=(D_XBC // CONV_CB,),
        in_specs=[_pspec(t, OFF_XBC), pl.BlockSpec((4, CONV_CB), lambda j: (0, j)),
                  pl.BlockSpec((1, CONV_CB), lambda j: (0, j))],
        out_specs=pl.BlockSpec((t, CONV_CB), lambda j: (0, j)),
        out_shape=jax.ShapeDtypeStruct((t, D_XBC), F32), compiler_params=_params("parallel"))(p, conv_w, conv_b)


def _ssm_conv_bwd(p, conv_w, conv_b, dxc):
    t = p.shape[0]

    def body(x_ref, w_ref, b_ref, d_ref, dx_ref, dw_ref, db_ref):
        xv = x_ref[...]
        pre = _conv_fwd_val(xv, w_ref, 4) + b_ref[...]
        s = _sigmoid(pre)
        dpre = d_ref[...] * (s * (1.0 + pre * (1.0 - s)))
        dq, dws = _conv_bwd_val(xv, dpre, w_ref, 4)
        dx_ref[...] = dq.astype(BF)
        for j in range(4):
            dw_ref[j:j + 1, :] = dws[j]
        db_ref[...] = jnp.sum(dpre, axis=0, keepdims=True)

    col = pl.BlockSpec((t, CONV_CB), lambda j: (0, j))
    wsp = pl.BlockSpec((4, CONV_CB), lambda j: (0, j))
    bsp = pl.BlockSpec((1, CONV_CB), lambda j: (0, j))
    return pl.pallas_call(
        body, name="ssm_conv_bwd", grid=(D_XBC // CONV_CB,),
        in_specs=[_pspec(t, OFF_XBC), wsp, bsp, col], out_specs=[col, wsp, bsp],
        out_shape=[jax.ShapeDtypeStruct((t, D_XBC), BF), jax.ShapeDtypeStruct((4, D_XBC), F32),
                   jax.ShapeDtypeStruct((1, D_XBC), F32)],
        compiler_params=_params("parallel"))(p, conv_w, conv_b, dxc)


DT_ROWS = 512


def _tri(lower):
    r = lax.broadcasted_iota(jnp.int32, (CHUNK, CHUNK), 0)
    c = lax.broadcasted_iota(jnp.int32, (CHUNK, CHUNK), 1)
    return jnp.where((r >= c) if lower else (r <= c), 1.0, 0.0).astype(F32)


def _dot_exact(a, b):
    return lax.dot_general(a, b, _DIMS["nn"], preferred_element_type=F32, precision=lax.Precision.HIGHEST)


def _dt_fwd(p, bias_pad, alog_pad):
    t = p.shape[0]

    def body(raw_ref, b_ref, al_ref, dt_ref, acs_ref):
        z = raw_ref[...] + b_ref[...]
        dt = jnp.maximum(z, 0.0) + jnp.log(1.0 + jnp.exp(-jnp.abs(z)))
        dt_ref[...] = dt
        a = dt * (-jnp.exp(al_ref[...]))
        tri = _tri(True)
        for k in range(DT_ROWS // CHUNK):
            acs_ref[k * CHUNK:(k + 1) * CHUNK, :] = _dot_exact(tri, a[k * CHUNK:(k + 1) * CHUNK, :])

    blk = pl.BlockSpec((DT_ROWS, DT_W), lambda i: (i, 0))
    vec = pl.BlockSpec((1, DT_W), lambda i: (0, 0))
    return pl.pallas_call(
        body, name="dt_fwd", grid=(t // DT_ROWS,),
        in_specs=[pl.BlockSpec((DT_ROWS, DT_W), lambda i: (i, OFF_DT // DT_W)), vec, vec],
        out_specs=[blk, blk], out_shape=[jax.ShapeDtypeStruct((t, DT_W), F32)] * 2,
        compiler_params=_params("parallel"))(p, bias_pad, alog_pad)


def _dt_bwd(p, bias_pad, alog_pad, dt, ddt, dacs):
    t = p.shape[0]

    def body(raw_ref, b_ref, al_ref, dt_ref, ddt_ref, dacs_ref, draw_ref, db_ref, dal_ref):
        i = pl.program_id(0)
        acoef = -jnp.exp(al_ref[...])
        triu = _tri(False)
        das = []
        for k in range(DT_ROWS // CHUNK):
            das.append(_dot_exact(triu, dacs_ref[k * CHUNK:(k + 1) * CHUNK, :]))
        da = jnp.concatenate(das, axis=0)
        dtv = dt_ref[...]
        ddt_tot = ddt_ref[...] + da * acoef
        lane = lax.broadcasted_iota(jnp.int32, (DT_ROWS, DT_W), 1)
        draw = jnp.where(lane < N_HEADS, ddt_tot * _sigmoid(raw_ref[...] + b_ref[...]), 0.0)
        draw_ref[...] = draw.astype(BF)
        pb = jnp.sum(draw, axis=0, keepdims=True)
        pa = jnp.sum(da * dtv * acoef, axis=0, keepdims=True)

        @pl.when(i == 0)
        def _():
            db_ref[...] = pb
            dal_ref[...] = pa

        @pl.when(i > 0)
        def _():
            db_ref[...] += pb
            dal_ref[...] += pa

    blk = pl.BlockSpec((DT_ROWS, DT_W), lambda i: (i, 0))
    vec = pl.BlockSpec((1, DT_W), lambda i: (0, 0))
    return pl.pallas_call(
        body, name="dt_bwd", grid=(t // DT_ROWS,),
        in_specs=[pl.BlockSpec((DT_ROWS, DT_W), lambda i: (i, OFF_DT // DT_W)), vec, vec, blk, blk, blk],
        out_specs=[blk, vec, vec],
        out_shape=[jax.ShapeDtypeStruct((t, DT_W), BF), jax.ShapeDtypeStruct((1, DT_W), F32),
                   jax.ShapeDtypeStruct((1, DT_W), F32)],
        compiler_params=_params("arbitrary"))(p, bias_pad, alog_pad, dt, ddt, dacs)


def _split_dot(z, onehot, terms):
    out = None
    rest = z
    for _ in range(terms):
        piece = rest.astype(BF)
        part = _dot(piece, onehot)
        out = part if out is None else out + part
        rest = rest - piece.astype(F32)
    return out


def _spread_mat(g):
    row = lax.broadcasted_iota(jnp.int32, (DT_W, GROUP_W), 0)
    lane = lax.broadcasted_iota(jnp.int32, (DT_W, GROUP_W), 1)
    return jnp.where(row == HEADS_PER_GROUP * g + lane // HEAD_DIM, 1.0, 0.0).astype(BF)


def _gather_mat(g):
    row = lax.broadcasted_iota(jnp.int32, (GROUP_W, DT_W), 0)
    lane = lax.broadcasted_iota(jnp.int32, (GROUP_W, DT_W), 1)
    return jnp.where(lane == HEADS_PER_GROUP * g + row // HEAD_DIM, 1.0, 0.0).astype(BF)


def _ssd_masks():
    row = lax.broadcasted_iota(jnp.int32, (CHUNK, GROUP_W), 0)
    col = lax.broadcasted_iota(jnp.int32, (CHUNK, GROUP_W), 1) % HEAD_DIM
    brow = lax.broadcasted_iota(jnp.int32, (GROUP_W, GROUP_W), 0) // HEAD_DIM
    bcol = lax.broadcasted_iota(jnp.int32, (GROUP_W, GROUP_W), 1) // HEAD_DIM
    return row >= col, row == col, brow == bcol


def _stack4(v):
    return jnp.concatenate([v, v, v, v], axis=0)


def _fold4(v):
    return v[0:CHUNK] + v[CHUNK:2 * CHUNK] + v[2 * CHUNK:3 * CHUNK] + v[3 * CHUNK:4 * CHUNK]


def _ssd_group(xc_ref, stacked, g, tri, eye, blockdiag):
    gs = slice(GROUP_W * g, GROUP_W * (g + 1))
    xs_g = xc_ref[:, gs]
    b_g = xc_ref[:, D_INNER + D_STATE * g:D_INNER + D_STATE * (g + 1)].astype(BF)
    c_g = xc_ref[:, D_INNER + 1024 + D_STATE * g:D_INNER + 1024 + D_STATE * (g + 1)].astype(BF)
    wide = _split_dot(stacked, _spread_mat(g), 3)
    acs_e, dt_e = wide[0:CHUNK], wide[CHUNK:2 * CHUNK]
    atot_e = acs_e[CHUNK - 1:CHUNK, :]
    acs_j = jnp.sum(jnp.where(eye, acs_e, 0.0), axis=0, keepdims=True)
    lmat = jnp.where(tri, jnp.exp(jnp.minimum(acs_e - acs_j, 0.0)), 0.0)
    b_t = _stack4(b_g)
    m = _dot(c_g, b_t, "nt") * lmat
    x_g = xs_g * dt_e
    xbd = jnp.where(blockdiag, _stack4(x_g), 0.0).astype(BF)
    return dict(gs=gs, xs=xs_g, b=b_g, c=c_g, b_t=b_t, dt=dt_e, e=jnp.exp(acs_e), dec=jnp.exp(atot_e - acs_e),
                eat=jnp.exp(atot_e), lmat=lmat, m=m, x=x_g, xbd=xbd)


def _ssd_fwd(xconv, dt, acs, d_exp, comm=None):
    t = xconv.shape[0]
    nc = t // CHUNK

    def body(xc_ref, dt_ref, acs_ref, d_ref, y_ref, hs_ref, state):
        c = pl.program_id(0)

        @pl.when(c == 0)
        def _():
            state[...] = jnp.zeros_like(state)

        hs_ref[...] = state[...]
        tri, eye, blockdiag = _ssd_masks()
        stacked = jnp.concatenate([acs_ref[...], dt_ref[...]], axis=0)
        for g in range(N_GROUPS):
            q = _ssd_group(xc_ref, stacked, g, tri, eye, blockdiag)
            gs = q["gs"]
            h_t = state[:, gs]
            ydiag = _dot(q["m"].astype(BF), q["xbd"])
            yoff = _dot(q["c"], h_t.astype(BF)) * q["e"]
            y_ref[:, gs] = ydiag + yoff + d_ref[:, gs] * q["xs"]
            s_t = _dot(q["b"], (q["x"] * q["dec"]).astype(BF), "tn")
            state[:, gs] = q["eat"] * h_t + s_t

    blk = lambda w: pl.BlockSpec((CHUNK, w), lambda c: (c, 0))
    outs, couts = _pcall(
        "ssd_fwd", body, (nc,),
        [blk(D_XBC), blk(DT_W), blk(DT_W), pl.BlockSpec((1, D_INNER), lambda c: (0, 0))],
        [blk(D_INNER), pl.BlockSpec((None, D_STATE, D_INNER), lambda c: (c, 0, 0))],
        [jax.ShapeDtypeStruct((t, D_INNER), F32), jax.ShapeDtypeStruct((nc, D_STATE, D_INNER), F32)],
        (xconv, dt, acs, d_exp), [pltpu.VMEM((D_STATE, D_INNER), F32)], ("arbitrary",), comm)
    return outs if comm is None else (outs, couts)


def _ssd_bwd(xconv, dt, acs, d_exp, hsave, dy, comm=None):
    t = xconv.shape[0]
    nc = t // CHUNK

    def body(xc_ref, dt_ref, acs_ref, d_ref, hs_ref, dy_ref, dxc_ref, ddt_ref, dacs_ref, dd_ref, dstate):
        c = pl.program_id(0)

        @pl.when(c == 0)
        def _():
            dstate[...] = jnp.zeros_like(dstate)
            dd_ref[...] = jnp.zeros_like(dd_ref)

        tri, eye, blockdiag = _ssd_masks()
        acsv = acs_ref[...]
        stacked = jnp.concatenate([acsv, dt_ref[...]], axis=0)
        eat_heads = jnp.exp(acsv[CHUNK - 1:CHUNK, :])
        ddt_acc = jnp.zeros((CHUNK, DT_W), F32)
        dacs_acc = jnp.zeros((CHUNK, DT_W), F32)
        datot_acc = jnp.zeros((1, DT_W), F32)

        for g in range(N_GROUPS):
            q = _ssd_group(xc_ref, stacked, g, tri, eye, blockdiag)
            gs, xs_g, b_g, c_g, m = q["gs"], q["xs"], q["b"], q["c"], q["m"]
            bs = slice(D_INNER + D_STATE * g, D_INNER + D_STATE * (g + 1))
            cs = slice(D_INNER + 1024 + D_STATE * g, D_INNER + 1024 + D_STATE * (g + 1))
            h_t = hs_ref[:, gs]
            h_b = h_t.astype(BF)
            dy_g = dy_ref[:, gs]
            dy_b = dy_g.astype(BF)
            ds_t = dstate[:, gs]
            ds_b = ds_t.astype(BF)

            yoff = _dot(c_g, h_b) * q["e"]
            edy = (q["e"] * dy_g).astype(BF)
            d_c = _dot(edy, h_b, "nt")
            d_ht = _dot(c_g, edy, "tn")
            bds = _dot(b_g, ds_b)
            xd = q["x"] * q["dec"]
            d_b = _dot(xd.astype(BF), ds_b, "nt")
            dm = _dot(dy_b, q["xbd"], "nt")
            cross = _dot(m.astype(BF), dy_b, "tn")
            dx_full = q["dec"] * bds + _fold4(jnp.where(blockdiag, cross, 0.0))
            dml = (dm * q["lmat"]).astype(BF)
            d_c = d_c + _dot(dml, q["b_t"])
            d_b = d_b + _fold4(_dot(dml, c_g, "tn"))
            w = dm * m
            q_dec = xd * bds
            z = w - jnp.where(eye, jnp.sum(w, axis=0, keepdims=True), 0.0) + dy_g * yoff - q_dec
            rows = jnp.concatenate(
                [jnp.sum(q_dec, axis=0, keepdims=True), jnp.sum(ds_t * h_t, axis=0, keepdims=True),
                 jnp.zeros((6, GROUP_W), F32)], axis=0)
            seg = _split_dot(jnp.concatenate([z, dx_full * xs_g, rows], axis=0), _gather_mat(g), 2)
            dacs_acc = dacs_acc + seg[0:CHUNK]
            ddt_acc = ddt_acc + seg[CHUNK:2 * CHUNK]
            datot_acc = datot_acc + seg[2 * CHUNK:2 * CHUNK + 1] + eat_heads * seg[2 * CHUNK + 1:2 * CHUNK + 2]
            dxc_ref[:, cs] = d_c
            dxc_ref[:, bs] = d_b
            dxc_ref[:, gs] = dx_full * q["dt"] + d_ref[:, gs] * dy_g
            dd_ref[:, gs] += jnp.sum(dy_g * xs_g, axis=0, keepdims=True)
            dstate[:, gs] = q["eat"] * ds_t + d_ht

        rowi = lax.broadcasted_iota(jnp.int32, (CHUNK, DT_W), 0)
        ddt_ref[...] = ddt_acc
        dacs_ref[...] = dacs_acc + jnp.where(rowi == CHUNK - 1, datot_acc, 0.0)

    rev = lambda w: pl.BlockSpec((CHUNK, w), lambda c: (nc - 1 - c, 0))
    vec = pl.BlockSpec((1, D_INNER), lambda c: (0, 0))
    outs, couts = _pcall(
        "ssd_bwd", body, (nc,),
        [rev(D_XBC), rev(DT_W), rev(DT_W), vec,
         pl.BlockSpec((None, D_STATE, D_INNER), lambda c: (nc - 1 - c, 0, 0)), rev(D_INNER)],
        [rev(D_XBC), rev(DT_W), rev(DT_W), vec],
        [jax.ShapeDtypeStruct((t, D_XBC), F32), jax.ShapeDtypeStruct((t, DT_W), F32),
         jax.ShapeDtypeStruct((t, DT_W), F32), jax.ShapeDtypeStruct((1, D_INNER), F32)],
        (xconv, dt, acs, d_exp, hsave, dy),
        [pltpu.VMEM((D_STATE, D_INNER), F32)], ("arbitrary",), comm)
    return outs if comm is None else (outs, couts)


GN_CB = 1024
GN_GROUPS = GN_CB // GROUP_W


def _gnorm_fwd(y, p, w, comm=None):
    t = y.shape[0]
    zoff = OFF_Z // GN_CB

    def body(y_ref, z_ref, w_ref, o_ref):
        for g in range(GN_GROUPS):
            gs = slice(GROUP_W * g, GROUP_W * (g + 1))
            z = z_ref[:, gs]
            yf = y_ref[:, gs] * (z * _sigmoid(z))
            rstd = lax.rsqrt(jnp.mean(yf * yf, axis=-1, keepdims=True) + NORM_EPS)
            o_ref[:, gs] = (yf * rstd * w_ref[:, gs]).astype(BF)

    blk = pl.BlockSpec((TE, GN_CB), lambda i, j: (i, j))
    out, couts = _pcall(
        "gnorm_fwd", body, (t // TE, D_INNER // GN_CB),
        [blk, pl.BlockSpec((TE, GN_CB), lambda i, j: (i, zoff + j)), pl.BlockSpec((1, GN_CB), lambda i, j: (0, j))],
        blk, jax.ShapeDtypeStruct((t, D_INNER), BF), (y, p, w), (), ("parallel", "parallel"), comm)
    return out if comm is None else (out, couts)


def _gnorm_bwd(y, p, w, dyn, comm=None):
    t = y.shape[0]
    zoff = OFF_Z // GN_CB

    def body(y_ref, z_ref, w_ref, dn_ref, dy_ref, dz_ref, dw_ref):
        i = pl.program_id(1)
        for g in range(GN_GROUPS):
            gs = slice(GROUP_W * g, GROUP_W * (g + 1))
            z = z_ref[:, gs]
            yv = y_ref[:, gs]
            s = _sigmoid(z)
            sil = z * s
            yf = yv * sil
            rstd = lax.rsqrt(jnp.mean(yf * yf, axis=-1, keepdims=True) + NORM_EPS)
            xhat = yf * rstd
            dn = dn_ref[:, gs]
            wd = dn * w_ref[:, gs]
            proj = jnp.mean(wd * xhat, axis=-1, keepdims=True)
            dyf = rstd * (wd - xhat * proj)
            dy_ref[:, gs] = dyf * sil
            dz_ref[:, gs] = (dyf * yv * (s * (1.0 + z * (1.0 - s)))).astype(BF)
            part = jnp.sum(dn * xhat, axis=0, keepdims=True)

            @pl.when(i == 0)
            def _():
                dw_ref[:, gs] = part

            @pl.when(i > 0)
            def _():
                dw_ref[:, gs] += part

    blk = pl.BlockSpec((TE, GN_CB), lambda j, i: (i, j))
    vec = pl.BlockSpec((1, GN_CB), lambda j, i: (0, j))
    outs, couts = _pcall(
        "gnorm_bwd", body, (D_INNER // GN_CB, t // TE),
        [blk, pl.BlockSpec((TE, GN_CB), lambda j, i: (i, zoff + j)), vec, blk],
        [blk, blk, vec],
        [jax.ShapeDtypeStruct((t, D_INNER), F32), jax.ShapeDtypeStruct((t, D_INNER), BF),
         jax.ShapeDtypeStruct((1, D_INNER), F32)],
        (y, p, w, dyn), (), ("parallel", "arbitrary"), comm)
    return outs if comm is None else (outs, couts)


MERGE_CB = 512


def _merge_fwd(p, ya, yb):
    t = ya.shape[0]

    def body(ga_ref, gb_ref, ya_ref, yb_ref, o_ref):
        o_ref[...] = (_sigmoid(ga_ref[...]) * ya_ref[...] + _sigmoid(gb_ref[...]) * yb_ref[...]).astype(BF)

    blk = pl.BlockSpec((TE, MERGE_CB), lambda i, j: (i, j))
    return pl.pallas_call(
        body, name="merge_fwd", grid=(t // TE, D_MODEL // MERGE_CB),
        in_specs=[pl.BlockSpec((TE, MERGE_CB), lambda i, j: (i, OFF_GA // MERGE_CB + j)),
                  pl.BlockSpec((TE, MERGE_CB), lambda i, j: (i, OFF_GB // MERGE_CB + j)), blk, blk],
        out_specs=blk, out_shape=jax.ShapeDtypeStruct((t, D_MODEL), BF),
        compiler_params=_params("parallel", "parallel"))(p, p, ya, yb)


def _merge_bwd(p, ya, yb, dm):
    t = ya.shape[0]

    def body(ga_ref, gb_ref, ya_ref, yb_ref, dm_ref, dga_ref, dgb_ref, dya_ref, dyb_ref):
        d = dm_ref[...]
        sa = _sigmoid(ga_ref[...])
        sb = _sigmoid(gb_ref[...])
        dga_ref[...] = (d * ya_ref[...] * sa * (1.0 - sa)).astype(BF)
        dgb_ref[...] = (d * yb_ref[...] * sb * (1.0 - sb)).astype(BF)
        dya_ref[...] = (d * sa).astype(BF)
        dyb_ref[...] = (d * sb).astype(BF)

    blk = pl.BlockSpec((TE, MERGE_CB), lambda i, j: (i, j))
    return pl.pallas_call(
        body, name="merge_bwd", grid=(t // TE, D_MODEL // MERGE_CB),
        in_specs=[pl.BlockSpec((TE, MERGE_CB), lambda i, j: (i, OFF_GA // MERGE_CB + j)),
                  pl.BlockSpec((TE, MERGE_CB), lambda i, j: (i, OFF_GB // MERGE_CB + j)), blk, blk, blk],
        out_specs=[blk] * 4, out_shape=[jax.ShapeDtypeStruct((t, D_MODEL), BF)] * 4,
        compiler_params=_params("parallel", "parallel"))(p, p, ya, yb, dm)


def _adamw(name, parts, w, m, v):
    r, c = w.shape
    tr = _row_tile(r)
    tc = ADAM_COL_TILE if (tr == r and r > 512 and c % ADAM_COL_TILE == 0) else c
    n_parts = parts.shape[0]
    bc1 = 1.0 - ADAM_B1 ** ADAM_STEP
    bc2 = 1.0 - ADAM_B2 ** ADAM_STEP

    def body(p_ref, w_ref, m_ref, v_ref, g_ref, d_ref, nm_ref, nv_ref):
        g = p_ref[0].astype(F32)
        for k in range(1, n_parts):
            g = g + p_ref[k].astype(F32)
        nm = ADAM_B1 * m_ref[...] + (1.0 - ADAM_B1) * g
        nv = ADAM_B2 * v_ref[...] + (1.0 - ADAM_B2) * (g * g)
        g_ref[...] = g
        nm_ref[...] = nm
        nv_ref[...] = nv
        d_ref[...] = -ADAM_LR * ((nm / bc1) / (jnp.sqrt(nv / bc2) + ADAM_EPS) + ADAM_WD * w_ref[...])

    blk = pl.BlockSpec((tr, tc), lambda i, j: (i, j))
    return pl.pallas_call(
        body, name=name, grid=(r // tr, c // tc),
        in_specs=[pl.BlockSpec((n_parts, tr, tc), lambda i, j: (0, i, j)), blk, blk, blk],
        out_specs=[blk] * 4, out_shape=[jax.ShapeDtypeStruct((r, c), F32)] * 4,
        compiler_params=_params("parallel", "parallel"))(parts, w, m, v)


def _pad_lanes(v, width):
    return jnp.pad(v, ((0, 0), (0, width - v.shape[1])))


def _reduce_start(slots, host):
    outs, sib = host(_pair_comm([a for _, a in slots]))
    sums = [(n, _add_pairs("pairsum_" + n, a, b)) for (n, a), b in zip(slots, sib)]
    return outs, sums


def _train_step(x, target, shard, rep):
    gdt = BF
    t = x.shape[0]
    recv = {}
    (got,) = _comm_call("gather_ffn1_in", _gather_comm([shard["ffn1_w_in"]], [True]))
    w1_in = got.reshape(2 * D_FF, D_MODEL)
    h1 = _rms_fwd("rms1_fwd", x, rep["ffn1_norm"])
    w_in_rows = [shard["w_in"][a:b] for a, b in W_IN_ROW_CUTS]
    gu1, got = _mm_nt("ffn1_in", h1, w1_in, tn=FF_HALF, comm=_gather_comm([shard["ffn1_w_out"], w_in_rows[0]]))
    w1_out = got[0].reshape(D_FF, D_MODEL)
    w_in_got = [got[1]]
    act1, got = _swiglu_fwd("swiglu1_fwd", gu1, comm=_gather_comm([w_in_rows[1]]))
    w_in_got.append(got[0])
    x1, got = _mm_nn("ffn1_out", act1, w1_out, res=x, alpha=0.5, comm=_gather_comm(
        [w_in_rows[2], shard["short_conv_w"], shard["ssm_conv_w"]]))
    w_in_got.append(got[0])
    short_conv_w = got[1].transpose(1, 0, 2).reshape(3, D_MODEL)
    ssm_conv_w = got[2].transpose(1, 0, 2).reshape(4, D_XBC)
    w_in_t = jnp.concatenate(w_in_got, axis=1).reshape(N_IN, D_MODEL)
    w_gd = jnp.concatenate([w_in_t[N_MAIN + N_HEADS:], w_in_t[N_MAIN:N_MAIN + N_HEADS],
                            jnp.zeros((DT_W - N_HEADS, D_MODEL), BF)], axis=0)

    h2 = _rms_fwd("rms2_fwd", x1, rep["mix_norm"])
    p, got = _mm_nt("proj_main", h2, w_in_t, n=N_MAIN, tn=1024, comm=_gather_comm(
        [shard["short_w_out"], shard["ssm_w_out"], shard["w_out"]]))
    p_gd = _mm_nt("proj_gd", h2, w_gd)
    short_w_out = got[0].reshape(D_MODEL, D_MODEL)
    ssm_w_out = got[1].reshape(D_INNER, D_MODEL)
    w_out = got[2].reshape(D_MODEL, D_MODEL)
    ya_in = _mix_a_fwd(p, short_conv_w)
    y_a = _mm_nn("short_out", ya_in, short_w_out)
    xconv = _ssm_conv_fwd(p, ssm_conv_w, rep["ssm_conv_b"])
    dt, acs = _dt_fwd(p_gd, rep["dt_bias_pad"], rep["a_log_pad"])
    (y_ssm, hsave), (got,) = _ssd_fwd(xconv, dt, acs, rep["d_exp"], comm=_gather_comm([shard["ffn2_w_in"]], [True]))
    w2_in = got.reshape(2 * D_FF, D_MODEL)
    yn, got = _gnorm_fwd(y_ssm, p, rep["ssm_norm"], comm=_gather_comm([shard["ffn2_w_out"]]))
    w2_out = got[0].reshape(D_FF, D_MODEL)
    y_b = _mm_nn("ssm_out", yn, ssm_w_out, tk=1024)
    merged = _merge_fwd(p_gd, y_a, y_b)
    x2 = _mm_nn("mix_out", merged, w_out, res=x1)

    h3 = _rms_fwd("rms3_fwd", x2, rep["ffn2_norm"])
    gu2 = _mm_nt("ffn2_in", h3, w2_in, tn=FF_HALF)
    act2 = _swiglu_fwd("swiglu2_fwd", gu2)
    x3 = _mm_nn("ffn2_out", act2, w2_out, res=x2, alpha=0.5)

    loss, dx3, dx3h, g_final = _final_loss(x3, rep["final_norm"], target)

    small = {"final_norm": g_final}
    dact2 = _mm_nt("ffn2_out_bwd_act", dx3h, w2_out)
    g_w2_out = _mm_tn("ffn2_out_bwd_w", act2, dx3h, gdt, tm=FF_HALF)
    dgu2 = _swiglu_bwd("swiglu2_bwd", gu2, dact2)
    g_w2_in = _mm_tn("ffn2_in_bwd_w", dgu2, h3, gdt, tm=FF_HALF)
    dh3 = _mm_nn("ffn2_in_bwd_h", dgu2, w2_in, tk=FF_HALF)
    dx2, dx2b, small["ffn2_norm"] = _rms_bwd("rms3_bwd", x2, rep["ffn2_norm"], dh3, dx3, 1.0)

    dmerged = _mm_nt("mix_out_bwd_x", dx2b, w_out)
    g_w_out = _mm_tn("mix_out_bwd_w", merged, dx2b, gdt)
    dga, dgb, dya, dyb = _merge_bwd(p_gd, y_a, y_b, dmerged)

    dya_in = _mm_nt("short_out_bwd_x", dya, short_w_out)
    g_short_w_out = _mm_tn("short_out_bwd_w", ya_in, dya, gdt)
    db, dc, dxa, g_short_conv = _mix_a_bwd(p, short_conv_w, dya_in)

    dyn = _mm_nt("ssm_out_bwd_x", dyb, ssm_w_out)
    g_ssm_w_out = _mm_tn("ssm_out_bwd_w", yn, dyb, gdt)
    late = [("ffn2_w_out", g_w2_out.reshape(N_DEV, FF_SHARD // 2, D_MODEL)),
            ("ffn2_w_in", g_w2_in.reshape(N_DEV, FF_SHARD, D_MODEL)),
            ("w_out", g_w_out.reshape(N_DEV, -1, D_MODEL)), ("short_w_out", g_short_w_out.reshape(N_DEV, -1, D_MODEL)),
            ("ssm_w_out", g_ssm_w_out.reshape(N_DEV, -1, D_MODEL))]
    (dy_ssm, dz, small["ssm_norm"]), sums = _reduce_start(
        late, lambda comm: _gnorm_bwd(y_ssm, p, rep["ssm_norm"], dyn, comm=comm))
    (dxconv, ddt, dacs, dd_lane), got = _ssd_bwd(
        xconv, dt, acs, rep["d_exp"], hsave, dy_ssm,
        comm=_chip_comm([a for _, a in sums], [n == "ffn2_w_in" for n, _ in sums]))
    recv.update({n: a for (n, _), a in zip(sums, got)})
    small["ssm_D"] = dd_lane.reshape(N_HEADS, HEAD_DIM).sum(axis=1)[None, :]
    dxbc, g_ssm_conv, small["ssm_conv_b"] = _ssm_conv_bwd(p, ssm_conv_w, rep["ssm_conv_b"], dxconv)
    draw, dbias, dalog = _dt_bwd(p_gd, rep["dt_bias_pad"], rep["a_log_pad"], dt, ddt, dacs)
    small["ssm_dt_bias"] = dbias[:, :N_HEADS]
    small["ssm_A_log"] = dalog[:, :N_HEADS]

    dp = jnp.concatenate([db, dc, dxa, dz, dxbc], axis=1)
    dp_gd = jnp.concatenate([dga, dgb, draw], axis=1)
    g_main = _mm_tn("proj_main_bwd_w", dp, h2, gdt, tm=1024)
    g_gd = _mm_tn("proj_gd_bwd_w", dp_gd, h2, gdt)
    g_in_t = jnp.concatenate([g_main, g_gd[2048:2048 + N_HEADS], g_gd[0:2048]], axis=0).reshape(
        N_DEV, IN_SHARD, D_MODEL)
    w_rows = [("w_in%d" % i, g_in_t[:, a:b]) for i, (a, b) in enumerate(W_IN_ROW_CUTS)]
    dh2, w_sums = _reduce_start(w_rows, lambda comm: _mm_nn("proj_main_bwd_x", dp, w_in_t, tk=1024, comm=comm))
    dh2 = _mm_nn("proj_gd_bwd_x", dp_gd, w_gd, res=dh2)
    (dx1, dx1h, small["mix_norm"]), got0 = _rms_bwd("rms2_bwd", x1, rep["mix_norm"], dh2, dx2, 0.5,
                                                      comm=_chip_comm([w_sums[0][1]]))

    g_w1_out, got1 = _mm_tn("ffn1_out_bwd_w", act1, dx1h, gdt, tm=FF_HALF, comm=_chip_comm([w_sums[1][1]]))
    rest = [("ffn1_w_out", g_w1_out.reshape(N_DEV, FF_SHARD // 2, D_MODEL)),
            ("short_conv_w", g_short_conv.reshape(3, N_DEV, -1).transpose(1, 0, 2)),
            ("ssm_conv_w", g_ssm_conv.reshape(4, N_DEV, -1).transpose(1, 0, 2))]
    pair_rest = _pair_comm([a for _, a in rest])
    dact1, got = _mm_nt("ffn1_out_bwd_act", dx1h, w1_out, comm=_join_comm(_chip_comm([w_sums[2][1]]), pair_rest))
    got2, sib = got[0], got[1:]
    rest_sums = [(n, _add_pairs("pairsum_" + n, a, b)) for (n, a), b in zip(rest, sib)]
    recv["w_in"] = jnp.concatenate([got0[0], got1[0], got2], axis=1)
    dgu1, got = _swiglu_bwd("swiglu1_bwd", gu1, dact1, comm=_chip_comm([a for _, a in rest_sums]))
    recv.update({n: a for (n, _), a in zip(rest_sums, got)})
    g_w1_in = _mm_tn("ffn1_in_bwd_w", dgu1, h1, gdt, tm=FF_HALF)
    dh1, last_sums = _reduce_start(
        [("ffn1_w_in", g_w1_in.reshape(N_DEV, FF_SHARD, D_MODEL))],
        lambda comm: _mm_nn("ffn1_in_bwd_h", dgu1, w1_in, tk=FF_HALF, comm=comm))
    (dx0, _, small["ffn1_norm"]), got = _rms_bwd("rms1_bwd", x, rep["ffn1_norm"], dh1, dx1, 1.0,
                                                  comm=_chip_comm([last_sums[0][1]], [True]))
    recv["ffn1_w_in"] = got[0]
    packed = _pack_small(small, loss[:, 0:1])
    (small_parts,) = _comm_call("exchange_last", _gather_comm([packed]))
    return dx0, recv, small_parts


_SMALL = [("ffn1_norm", 1024), ("mix_norm", 1024), ("ssm_conv_b", 4096), ("ssm_dt_bias", 32), ("ssm_A_log", 32),
          ("ssm_D", 32), ("ssm_norm", 2048), ("ffn2_norm", 1024), ("final_norm", 1024)]
SMALL_W = 10368


def _pack_small(d, loss=None):
    parts = [d[n].reshape(1, -1).astype(F32) for n, _ in _SMALL]
    used = sum(sz for _, sz in _SMALL)
    tail = jnp.zeros((1, SMALL_W - used), F32)
    if loss is not None:
        tail = tail.at[:, 0:1].set(loss)
    return jnp.concatenate(parts + [tail], axis=1)


def _unpack_small(v, shapes):
    out, off = {}, 0
    for n, sz in _SMALL:
        out[n] = v[:, off:off + sz].reshape(shapes[n])
        off += sz
    return out, v[0, off]


_SHARDED = ["ffn1_w_in", "ffn1_w_out", "w_in", "short_conv_w", "short_w_out", "ssm_conv_w", "ssm_w_out", "w_out",
            "ffn2_w_in", "ffn2_w_out"]
_TRANSPOSED = ("ffn1_w_in", "w_in", "ffn2_w_in")
_ORDER = ["ffn1_norm", "ffn1_w_in", "ffn1_w_out", "mix_norm", "w_in", "short_conv_w", "short_w_out", "ssm_conv_w",
          "ssm_conv_b", "ssm_dt_bias", "ssm_A_log", "ssm_D", "ssm_norm", "ssm_w_out", "w_out", "ffn2_norm",
          "ffn2_w_in", "ffn2_w_out", "final_norm"]


def kernel(x, ffn1_norm, ffn1_w_in, ffn1_w_out, mix_norm, w_in, short_conv_w, short_w_out, ssm_conv_w, ssm_conv_b, ssm_dt_bias, ssm_A_log, ssm_D, ssm_norm, ssm_w_out, w_out, ffn2_norm, ffn2_w_in, ffn2_w_out, final_norm, loss_target, m_ffn1_norm, m_ffn1_w_in, m_ffn1_w_out, m_mix_norm, m_w_in, m_short_conv_w, m_short_w_out, m_ssm_conv_w, m_ssm_conv_b, m_ssm_dt_bias, m_ssm_A_log, m_ssm_D, m_ssm_norm, m_ssm_w_out, m_w_out, m_ffn2_norm, m_ffn2_w_in, m_ffn2_w_out, m_final_norm, v_ffn1_norm, v_ffn1_w_in, v_ffn1_w_out, v_mix_norm, v_w_in, v_short_conv_w, v_short_w_out, v_ssm_conv_w, v_ssm_conv_b, v_ssm_dt_bias, v_ssm_A_log, v_ssm_D, v_ssm_norm, v_ssm_w_out, v_w_out, v_ffn2_norm, v_ffn2_w_in, v_ffn2_w_out, v_final_norm):
    w = dict(ffn1_norm=ffn1_norm, ffn1_w_in=ffn1_w_in, ffn1_w_out=ffn1_w_out, mix_norm=mix_norm, w_in=w_in,
             short_conv_w=short_conv_w, short_w_out=short_w_out, ssm_conv_w=ssm_conv_w, ssm_conv_b=ssm_conv_b,
             ssm_dt_bias=ssm_dt_bias, ssm_A_log=ssm_A_log, ssm_D=ssm_D, ssm_norm=ssm_norm, ssm_w_out=ssm_w_out,
             w_out=w_out, ffn2_norm=ffn2_norm, ffn2_w_in=ffn2_w_in, ffn2_w_out=ffn2_w_out, final_norm=final_norm)
    m = dict(ffn1_norm=m_ffn1_norm, ffn1_w_in=m_ffn1_w_in, ffn1_w_out=m_ffn1_w_out, mix_norm=m_mix_norm, w_in=m_w_in,
             short_conv_w=m_short_conv_w, short_w_out=m_short_w_out, ssm_conv_w=m_ssm_conv_w,
             ssm_conv_b=m_ssm_conv_b, ssm_dt_bias=m_ssm_dt_bias, ssm_A_log=m_ssm_A_log, ssm_D=m_ssm_D,
             ssm_norm=m_ssm_norm, ssm_w_out=m_ssm_w_out, w_out=m_w_out, ffn2_norm=m_ffn2_norm,
             ffn2_w_in=m_ffn2_w_in, ffn2_w_out=m_ffn2_w_out, final_norm=m_final_norm)
    v = dict(ffn1_norm=v_ffn1_norm, ffn1_w_in=v_ffn1_w_in, ffn1_w_out=v_ffn1_w_out, mix_norm=v_mix_norm, w_in=v_w_in,
             short_conv_w=v_short_conv_w, short_w_out=v_short_w_out, ssm_conv_w=v_ssm_conv_w,
             ssm_conv_b=v_ssm_conv_b, ssm_dt_bias=v_ssm_dt_bias, ssm_A_log=v_ssm_A_log, ssm_D=v_ssm_D,
             ssm_norm=v_ssm_norm, ssm_w_out=v_ssm_w_out, w_out=v_w_out, ffn2_norm=v_ffn2_norm,
             ffn2_w_in=v_ffn2_w_in, ffn2_w_out=v_ffn2_w_out, final_norm=v_final_norm)
    shapes = {n: w[n].shape for n in _ORDER}

    def local(d, n):
        return d[n][0].T if n in _TRANSPOSED else d[n][0]

    shard = {n: local(w, n) for n in _SHARDED}

    wire = {n: (shard[n] if n in ("short_conv_w", "ssm_conv_w") else shard[n].astype(BF)) for n in _SHARDED}
    rep = {
        "ffn1_norm": ffn1_norm, "mix_norm": mix_norm, "ffn2_norm": ffn2_norm, "ssm_norm": ssm_norm,
        "ssm_conv_b": ssm_conv_b, "final_norm": final_norm.reshape(1, D_MODEL),
        "dt_bias_pad": _pad_lanes(ssm_dt_bias, DT_W), "a_log_pad": _pad_lanes(ssm_A_log, DT_W),
        "d_exp": jnp.repeat(ssm_D, HEAD_DIM, axis=1),
    }
    grad_x, parts, small_parts = _train_step(x[0], loss_target[0], wire, rep)

    out_g, out_d, out_m, out_v = {}, {}, {}, {}
    for n in _SHARDED:
        res = _adamw("adamw_" + n, parts[n], shard[n], local(m, n), local(v, n))
        out_g[n], out_d[n], out_m[n], out_v[n] = [(r.T if n in _TRANSPOSED else r).reshape(shapes[n]) for r in res]
    sres = _adamw("adamw_small", small_parts, _pack_small(w), _pack_small(m), _pack_small(v))
    sg, loss = _unpack_small(sres[0], shapes)
    sd, _ = _unpack_small(sres[1], shapes)
    sm, _ = _unpack_small(sres[2], shapes)
    sv, _ = _unpack_small(sres[3], shapes)
    out_g.update(sg)
    out_d.update(sd)
    out_m.update(sm)
    out_v.update(sv)
    return (loss, grad_x[None], *[out_g[n] for n in _ORDER], *[out_d[n] for n in _ORDER],
            *[out_m[n] for n in _ORDER], *[out_v[n] for n in _ORDER])
```

```python
import functools

import jax
import jax.numpy as jnp
from jax import lax
from jax.experimental import pallas as pl
from jax.experimental.pallas import tpu as pltpu

F32 = jnp.float32
BF = jnp.bfloat16

N_DEV = 8
D_MODEL = 1024
D_FF = 2816
D_INNER = 2048
D_XBC = 4096
N_HEADS = 32
HEAD_DIM = 64
N_GROUPS = 8
D_STATE = 128
CHUNK = 64
GROUP_W = D_INNER // N_GROUPS
HEADS_PER_GROUP = N_HEADS // N_GROUPS
NORM_EPS = 1e-5
N_IN = 11296
FF_SHARD = 2 * D_FF // N_DEV
FF_HALF = D_FF // 2
IN_SHARD = N_IN // N_DEV

OFF_B, OFF_C, OFF_XA, OFF_Z, OFF_XBC = 0, 1024, 2048, 3072, 5120
N_MAIN = 9216
OFF_GA, OFF_GB, OFF_DT = 0, 1024, 2048
DT_W = 128
N_GD = 2048 + DT_W
W_IN_ROW_CUTS = [(0, 480), (480, 944), (944, 1412)]

ADAM_LR, ADAM_B1, ADAM_B2, ADAM_EPS, ADAM_WD, ADAM_STEP = 0.001, 0.9, 0.999, 1e-08, 0.01, 10

VMEM_LIMIT_V7X = 56 * 1024 * 1024
TM = 1024
TE = 512
ADAM_COL_TILE = 256


def _params(*sem):
    return pltpu.CompilerParams(dimension_semantics=sem, vmem_limit_bytes=VMEM_LIMIT_V7X)


_DIMS = {
    "nn": (((1,), (0,)), ((), ())),
    "nt": (((1,), (1,)), ((), ())),
    "tn": (((0,), (0,)), ((), ())),
}


def _dot(a, b, mode="nn"):
    return lax.dot_general(a, b, _DIMS[mode], preferred_element_type=F32)


def _sigmoid(x):
    return 1.0 / (1.0 + jnp.exp(-x))


class _Comm:
    def __init__(self, inputs, out_shapes, sems, start, finish):
        self.inputs, self.out_shapes, self.sems, self.start, self.finish = inputs, out_shapes, sems, start, finish


def _pcall(name, body, grid, in_specs, out_specs, out_shape, args, scratch=(), sem=None, comm=None):
    single = not isinstance(out_shape, (list, tuple))
    out_shapes = [out_shape] if single else list(out_shape)
    out_specs = [out_specs] if single else list(out_specs)
    n_in, n_out, n_scr = len(args), len(out_shapes), len(scratch)
    if comm is None:
        res = pl.pallas_call(
            body, name=name, grid=grid, in_specs=list(in_specs), out_specs=out_specs, out_shape=out_shapes,
            scratch_shapes=list(scratch), compiler_params=_params(*sem))(*args)
        return (res[0] if single else res), []
    nci, nco = len(comm.inputs), len(comm.out_shapes)

    def wrapped(*refs):
        a = refs[:n_in]
        ci = refs[n_in:n_in + nci]
        o0 = n_in + nci
        o = refs[o0:o0 + n_out]
        co = refs[o0 + n_out:o0 + n_out + nco]
        s0 = o0 + n_out + nco
        s = refs[s0:s0 + n_scr]
        cs = refs[s0 + n_scr:]
        pids = [pl.program_id(i) for i in range(len(grid))]
        first = functools.reduce(jnp.logical_and, [p == 0 for p in pids])
        last = functools.reduce(jnp.logical_and, [p == g - 1 for p, g in zip(pids, grid)])

        @pl.when(first)
        def _():
            comm.start(ci, co, cs)

        body(*a, *o, *s)

        @pl.when(last)
        def _():
            comm.finish(ci, co, cs)

    any_spec = pl.BlockSpec(memory_space=pl.ANY)
    res = pl.pallas_call(
        wrapped, name=name, grid=grid, in_specs=list(in_specs) + [any_spec] * nci,
        out_specs=out_specs + [any_spec] * nco, out_shape=out_shapes + list(comm.out_shapes),
        scratch_shapes=list(scratch) + list(comm.sems),
        compiler_params=_params(*(("arbitrary",) * len(grid))))(*args, *comm.inputs)
    core = res[:n_out]
    return (core[0] if single else core), list(res[n_out:])


def _comm_call(name, comm):
    nci, nco = len(comm.inputs), len(comm.out_shapes)

    def body(*refs):
        ci, co, cs = refs[:nci], refs[nci:nci + nco], refs[nci + nco:]
        comm.start(ci, co, cs)
        comm.finish(ci, co, cs)

    any_spec = pl.BlockSpec(memory_space=pl.ANY)
    return pl.pallas_call(
        body, name=name, in_specs=[any_spec] * nci, out_specs=[any_spec] * nco, out_shape=list(comm.out_shapes),
        scratch_shapes=list(comm.sems), compiler_params=pltpu.CompilerParams(has_side_effects=True))(*comm.inputs)


def _remote(src, dst, ssem, rsem, dev):
    return pltpu.make_async_remote_copy(src_ref=src, dst_ref=dst, send_sem=ssem, recv_sem=rsem, device_id=dev,
                                        device_id_type=pl.DeviceIdType.MESH)


def _place():
    x, y, c = lax.axis_index("x"), lax.axis_index("y"), lax.axis_index("c")
    other_chips = [(1 - x, y), (x, 1 - y), (1 - x, 1 - y)]
    return x, y, c, other_chips


def _slot(x, y, c, swap):
    return 4 * y + 2 * x + c if swap else 4 * x + 2 * y + c


def _chip_slot(x, y, swap):
    return 2 * y + x if swap else 2 * x + y


def _gather_comm(shards, swaps=None):
    n = len(shards)
    per = N_DEV - 1
    swaps = [False] * n if swaps is None else swaps

    def start(ins, outs, sems):
        send, recv, loc = sems
        x, y, c, chips = _place()
        for i in range(n):
            me = _slot(x, y, c, swaps[i])
            pltpu.make_async_copy(ins[i], outs[i].at[me], loc.at[i]).start()
            _remote(ins[i], outs[i].at[me], send.at[per * i], recv.at[per * i], (x, y, 1 - c)).start()
            for j, (qx, qy) in enumerate(chips):
                _remote(ins[i], outs[i].at[me], send.at[per * i + 1 + j], recv.at[per * i + 1 + j], (qx, qy, c)).start()

    def finish(ins, outs, sems):
        send, recv, loc = sems
        x, y, c, chips = _place()
        sib = (x, y, 1 - c)
        for i in range(n):
            for j, (qx, qy) in enumerate(chips):
                blk = outs[i].at[_slot(qx, qy, c, swaps[i])]
                _remote(blk, blk, send.at[per * i + 1 + j], recv.at[per * i + 1 + j], (qx, qy, c)).wait_recv()
                _remote(blk, blk, send.at[per * i + 4 + j], recv.at[per * i + 4 + j], sib).start()
        for i in range(n):
            blk = outs[i].at[_slot(x, y, 1 - c, swaps[i])]
            _remote(blk, blk, send.at[per * i], recv.at[per * i], sib).wait_recv()
            for j, (qx, qy) in enumerate(chips):
                blk = outs[i].at[_slot(qx, qy, 1 - c, swaps[i])]
                _remote(blk, blk, send.at[per * i + 4 + j], recv.at[per * i + 4 + j], sib).wait_recv()
        for i in range(n):
            own = outs[i].at[_slot(x, y, c, swaps[i])]
            for k in range(per):
                _remote(ins[i], own, send.at[per * i + k], recv.at[per * i + k], sib).wait_send()
            pltpu.make_async_copy(ins[i], own, loc.at[i]).wait()

    out_shapes = [jax.ShapeDtypeStruct((N_DEV,) + tuple(a.shape), a.dtype) for a in shards]
    sems = [pltpu.SemaphoreType.DMA((per * n,)), pltpu.SemaphoreType.DMA((per * n,)), pltpu.SemaphoreType.DMA((n,))]
    return _Comm(list(shards), out_shapes, sems, start, finish)


def _pair_comm(slots):
    n = len(slots)

    def copies(ins, outs, sems):
        send, recv = sems
        x, y, c, _ = _place()
        sib = (x, y, 1 - c)
        out = []
        for i in range(n):
            for q in range(4):
                out.append(_remote(ins[i].at[2 * q + 1 - c], outs[i].at[q], send.at[4 * i + q], recv.at[4 * i + q], sib))
        return out

    def start(ins, outs, sems):
        for cp in copies(ins, outs, sems):
            cp.start()

    def finish(ins, outs, sems):
        for cp in copies(ins, outs, sems):
            cp.wait_send()
            cp.wait_recv()

    out_shapes = [jax.ShapeDtypeStruct((4,) + tuple(a.shape[1:]), a.dtype) for a in slots]
    sems = [pltpu.SemaphoreType.DMA((4 * n,)), pltpu.SemaphoreType.DMA((4 * n,))]
    return _Comm(list(slots), out_shapes, sems, start, finish)


def _chip_comm(chip_sums, swaps=None):
    n = len(chip_sums)
    swaps = [False] * n if swaps is None else swaps

    def start(ins, outs, sems):
        send, recv, loc = sems
        x, y, c, chips = _place()
        for i in range(n):
            mine = _chip_slot(x, y, swaps[i])
            pltpu.make_async_copy(ins[i].at[mine], outs[i].at[mine], loc.at[i]).start()
            for j, (qx, qy) in enumerate(chips):
                _remote(ins[i].at[_chip_slot(qx, qy, swaps[i])], outs[i].at[mine], send.at[3 * i + j],
                        recv.at[3 * i + j], (qx, qy, c)).start()

    def finish(ins, outs, sems):
        send, recv, loc = sems
        x, y, c, chips = _place()
        for i in range(n):
            mine = _chip_slot(x, y, swaps[i])
            for j, (qx, qy) in enumerate(chips):
                theirs = _chip_slot(qx, qy, swaps[i])
                cp = _remote(ins[i].at[theirs], outs[i].at[theirs], send.at[3 * i + j], recv.at[3 * i + j], (qx, qy, c))
                cp.wait_send()
                cp.wait_recv()
            pltpu.make_async_copy(ins[i].at[mine], outs[i].at[mine], loc.at[i]).wait()

    out_shapes = [jax.ShapeDtypeStruct(a.shape, a.dtype) for a in chip_sums]
    sems = [pltpu.SemaphoreType.DMA((3 * n,)), pltpu.SemaphoreType.DMA((3 * n,)), pltpu.SemaphoreType.DMA((n,))]
    return _Comm(list(chip_sums), out_shapes, sems, start, finish)


def _join_comm(a, b):
    na_i, na_o, na_s = len(a.inputs), len(a.out_shapes), len(a.sems)

    def start(ins, outs, sems):
        a.start(ins[:na_i], outs[:na_o], sems[:na_s])
        b.start(ins[na_i:], outs[na_o:], sems[na_s:])

    def finish(ins, outs, sems):
        a.finish(ins[:na_i], outs[:na_o], sems[:na_s])
        b.finish(ins[na_i:], outs[na_o:], sems[na_s:])

    return _Comm(a.inputs + b.inputs, a.out_shapes + b.out_shapes, a.sems + b.sems, start, finish)


def _row_tile(r):
    for cand in (256, 128):
        if r > cand and r % cand == 0:
            return cand
    return r


def _add_pairs(name, slots, sib):
    r, c = slots.shape[1:]
    tr = _row_tile(r)

    def body(s_ref, b_ref, o_ref):
        core = lax.axis_index("c")
        o_ref[...] = (s_ref[core].astype(F32) + b_ref[...].astype(F32)).astype(o_ref.dtype)

    return pl.pallas_call(
        body, name=name, grid=(4, r // tr),
        in_specs=[pl.BlockSpec((None, 2, tr, c), lambda q, i: (q, 0, i, 0)),
                  pl.BlockSpec((None, tr, c), lambda q, i: (q, i, 0))],
        out_specs=pl.BlockSpec((None, tr, c), lambda q, i: (q, i, 0)),
        out_shape=jax.ShapeDtypeStruct((4, r, c), slots.dtype),
        compiler_params=_params("parallel", "parallel"))(slots.reshape(4, 2, r, c), sib)


def _matmul(name, mode, a, b, grid, a_spec, b_spec, o_spec, out_shape, acc_shape,
            res=None, res_spec=None, alpha=1.0, comm=None):
    nk = grid[-1]
    has_res = res is not None

    def body(*refs):
        if has_res:
            a_ref, b_ref, r_ref, o_ref = refs[:4]
        else:
            a_ref, b_ref, o_ref = refs[:3]
            r_ref = None
        part = _dot(a_ref[...], b_ref[...], mode)

        def finish(v):
            if alpha != 1.0:
                v = v * alpha
            if has_res:
                v = r_ref[...] + v
            o_ref[...] = v.astype(o_ref.dtype)

        if nk == 1:
            finish(part)
        else:
            acc = refs[-1]
            k = pl.program_id(len(grid) - 1)

            @pl.when(k == 0)
            def _():
                acc[...] = part

            @pl.when(k > 0)
            def _():
                acc[...] += part

            @pl.when(k == nk - 1)
            def _():
                finish(acc[...])

    in_specs = [a_spec, b_spec] + ([res_spec] if has_res else [])
    args = (a, b) + ((res,) if has_res else ())
    scratch = [] if nk == 1 else [pltpu.VMEM(acc_shape, F32)]
    sem = ("parallel",) * (len(grid) - 1) + ("arbitrary",)
    out, couts = _pcall(name, body, grid, in_specs, o_spec, out_shape, args, scratch, sem, comm)
    return out if comm is None else (out, couts)


def _mm_nn(name, a, b, out_dtype=F32, res=None, alpha=1.0, tk=None, comm=None):
    t, kk = a.shape
    n = b.shape[1]
    tk = kk if tk is None else tk
    grid = (t // TM, 1, kk // tk)
    return _matmul(
        name, "nn", a, b, grid,
        pl.BlockSpec((TM, tk), lambda i, j, k: (i, k)),
        pl.BlockSpec((tk, n), lambda i, j, k: (k, 0)),
        pl.BlockSpec((TM, n), lambda i, j, k: (i, 0)),
        jax.ShapeDtypeStruct((t, n), out_dtype), (TM, n),
        res=res, res_spec=pl.BlockSpec((TM, n), lambda i, j, k: (i, 0)), alpha=alpha, comm=comm)


def _mm_nt(name, a, b, n=None, tn=None, tk=None, out_dtype=F32, comm=None):
    t, kk = a.shape
    n = b.shape[0] if n is None else n
    tn = n if tn is None else tn
    tk = kk if tk is None else tk
    grid = (n // tn, t // TM, kk // tk)
    return _matmul(
        name, "nt", a, b, grid,
        pl.BlockSpec((TM, tk), lambda j, i, k: (i, k)),
        pl.BlockSpec((tn, tk), lambda j, i, k: (j, k)),
        pl.BlockSpec((TM, tn), lambda j, i, k: (i, j)),
        jax.ShapeDtypeStruct((t, n), out_dtype), (TM, tn), comm=comm)


def _mm_tn(name, a, b, out_dtype, tm=None, comm=None):
    t, m = a.shape
    n = b.shape[1]
    tm = m if tm is None else tm
    grid = (m // tm, 1, t // TM)
    return _matmul(
        name, "tn", a, b, grid,
        pl.BlockSpec((TM, tm), lambda j, i, k: (k, j)),
        pl.BlockSpec((TM, n), lambda j, i, k: (k, 0)),
        pl.BlockSpec((tm, n), lambda j, i, k: (j, 0)),
        jax.ShapeDtypeStruct((m, n), out_dtype), (tm, n), comm=comm)


def _rms_fwd(name, x, w):
    t, d = x.shape

    def body(x_ref, w_ref, h_ref):
        xv = x_ref[...]
        rstd = lax.rsqrt(jnp.mean(xv * xv, axis=-1, keepdims=True) + NORM_EPS)
        h_ref[...] = (xv * rstd * w_ref[...]).astype(h_ref.dtype)

    return pl.pallas_call(
        body, name=name, grid=(t // TE,),
        in_specs=[pl.BlockSpec((TE, d), lambda i: (i, 0)), pl.BlockSpec((1, d), lambda i: (0, 0))],
        out_specs=pl.BlockSpec((TE, d), lambda i: (i, 0)),
        out_shape=jax.ShapeDtypeStruct((t, d), BF), compiler_params=_params("parallel"))(x, w)


def _rms_bwd(name, x, w, dh, dres, out_scale, comm=None):
    t, d = x.shape

    def body(x_ref, w_ref, dh_ref, dres_ref, dx_ref, dxb_ref, dw_ref):
        i = pl.program_id(0)
        xv = x_ref[...]
        rstd = lax.rsqrt(jnp.mean(xv * xv, axis=-1, keepdims=True) + NORM_EPS)
        xhat = xv * rstd
        dhv = dh_ref[...]
        wd = dhv * w_ref[...]
        proj = jnp.mean(wd * xhat, axis=-1, keepdims=True)
        dx = dres_ref[...] + rstd * (wd - xhat * proj)
        dx_ref[...] = dx
        dxb_ref[...] = (dx * out_scale).astype(BF)
        part = jnp.sum(dhv * xhat, axis=0, keepdims=True)

        @pl.when(i == 0)
        def _():
            dw_ref[...] = part

        @pl.when(i > 0)
        def _():
            dw_ref[...] += part

    row = pl.BlockSpec((TE, d), lambda i: (i, 0))
    vec = pl.BlockSpec((1, d), lambda i: (0, 0))
    outs, couts = _pcall(
        name, body, (t // TE,), [row, vec, row, row], [row, row, vec],
        [jax.ShapeDtypeStruct((t, d), F32), jax.ShapeDtypeStruct((t, d), BF), jax.ShapeDtypeStruct((1, d), F32)],
        (x, w, dh, dres), (), ("arbitrary",), comm)
    return outs if comm is None else (outs, couts)


def _final_loss(x, w, target):
    t, d = x.shape

    def body(x_ref, w_ref, t_ref, loss_ref, dx_ref, dxb_ref, dw_ref):
        i = pl.program_id(0)
        xv = x_ref[...]
        rstd = lax.rsqrt(jnp.mean(xv * xv, axis=-1, keepdims=True) + NORM_EPS)
        xhat = xv * rstd
        err = xhat * w_ref[...] - t_ref[...]
        lpart = 0.5 * jnp.sum(jnp.mean(err * err, axis=-1, keepdims=True), axis=0, keepdims=True)
        dy = err * (1.0 / d)
        wd = dy * w_ref[...]
        proj = jnp.mean(wd * xhat, axis=-1, keepdims=True)
        dx = rstd * (wd - xhat * proj)
        dx_ref[...] = dx
        dxb_ref[...] = (0.5 * dx).astype(BF)
        part = jnp.sum(dy * xhat, axis=0, keepdims=True)
        lfull = jnp.broadcast_to(lpart, (1, 128))

        @pl.when(i == 0)
        def _():
            dw_ref[...] = part
            loss_ref[...] = lfull

        @pl.when(i > 0)
        def _():
            dw_ref[...] += part
            loss_ref[...] += lfull

    row = pl.BlockSpec((TE, d), lambda i: (i, 0))
    vec = pl.BlockSpec((1, d), lambda i: (0, 0))
    return pl.pallas_call(
        body, name="final_loss", grid=(t // TE,), in_specs=[row, vec, row],
        out_specs=[pl.BlockSpec((1, 128), lambda i: (0, 0)), row, row, vec],
        out_shape=[jax.ShapeDtypeStruct((1, 128), F32), jax.ShapeDtypeStruct((t, d), F32),
                   jax.ShapeDtypeStruct((t, d), BF), jax.ShapeDtypeStruct((1, d), F32)],
        compiler_params=_params("arbitrary"))(x, w, target)


def _swiglu_fwd(name, gu, comm=None):
    t = gu.shape[0]

    def body(g_ref, u_ref, a_ref):
        g = g_ref[...].astype(F32)
        a_ref[...] = (g * _sigmoid(g) * u_ref[...].astype(F32)).astype(BF)

    blk = (TE, FF_HALF)
    out, couts = _pcall(
        name, body, (t // TE, 2),
        [pl.BlockSpec(blk, lambda i, j: (i, 2 * j)), pl.BlockSpec(blk, lambda i, j: (i, 2 * j + 1))],
        pl.BlockSpec(blk, lambda i, j: (i, j)), jax.ShapeDtypeStruct((t, D_FF), BF),
        (gu, gu), (), ("parallel", "parallel"), comm)
    return out if comm is None else (out, couts)


def _swiglu_bwd(name, gu, dact, comm=None):
    t = gu.shape[0]

    def body(g_ref, u_ref, da_ref, o_ref):
        g = g_ref[...].astype(F32)
        da = da_ref[...].astype(F32)
        s = _sigmoid(g)
        o_ref[:, 0:FF_HALF] = (da * u_ref[...].astype(F32) * (s * (1.0 + g * (1.0 - s)))).astype(BF)
        o_ref[:, FF_HALF:2 * FF_HALF] = (da * g * s).astype(BF)

    blk = (TE, FF_HALF)
    out, couts = _pcall(
        name, body, (t // TE, 2),
        [pl.BlockSpec(blk, lambda i, j: (i, 2 * j)), pl.BlockSpec(blk, lambda i, j: (i, 2 * j + 1)),
         pl.BlockSpec(blk, lambda i, j: (i, j))],
        pl.BlockSpec((TE, 2 * FF_HALF), lambda i, j: (i, j)),
        jax.ShapeDtypeStruct((t, 2 * D_FF), BF), (gu, gu, dact), (), ("parallel", "parallel"), comm)
    return out if comm is None else (out, couts)


CONV_CB = 256


def _shift_down(v, s):
    if s == 0:
        return v
    row = lax.broadcasted_iota(jnp.int32, v.shape, 0)
    return jnp.where(row >= s, pltpu.roll(v, s, 0), 0.0)


def _shift_up(v, s):
    if s == 0:
        return v
    n = v.shape[0]
    row = lax.broadcasted_iota(jnp.int32, v.shape, 0)
    return jnp.where(row < n - s, pltpu.roll(v, n - s, 0), 0.0)


def _conv_fwd_val(q, w_ref, k):
    out = q * w_ref[k - 1:k, :]
    for j in range(k - 1):
        out = out + _shift_down(q, k - 1 - j) * w_ref[j:j + 1, :]
    return out


def _conv_bwd_val(q, dv, w_ref, k):
    dq = dv * w_ref[k - 1:k, :]
    dws = []
    for j in range(k - 1):
        dq = dq + _shift_up(dv, k - 1 - j) * w_ref[j:j + 1, :]
        dws.append(jnp.sum(dv * _shift_down(q, k - 1 - j), axis=0, keepdims=True))
    dws.append(jnp.sum(dv * q, axis=0, keepdims=True))
    return dq, dws


def _pspec(t, off):
    base = off // CONV_CB
    return pl.BlockSpec((t, CONV_CB), lambda j: (0, base + j))


def _mix_a_fwd(p, conv_w):
    t = p.shape[0]

    def body(b_ref, c_ref, xa_ref, w_ref, o_ref):
        q = c_ref[...].astype(F32) * xa_ref[...].astype(F32)
        o_ref[...] = (b_ref[...].astype(F32) * _conv_fwd_val(q, w_ref, 3)).astype(BF)

    return pl.pallas_call(
        body, name="mix_a_fwd", grid=(D_MODEL // CONV_CB,),
        in_specs=[_pspec(t, OFF_B), _pspec(t, OFF_C), _pspec(t, OFF_XA),
                  pl.BlockSpec((3, CONV_CB), lambda j: (0, j))],
        out_specs=pl.BlockSpec((t, CONV_CB), lambda j: (0, j)),
        out_shape=jax.ShapeDtypeStruct((t, D_MODEL), BF), compiler_params=_params("parallel"))(p, p, p, conv_w)


def _mix_a_bwd(p, conv_w, dya):
    t = p.shape[0]

    def body(b_ref, c_ref, xa_ref, w_ref, dy_ref, db_ref, dc_ref, dxa_ref, dw_ref):
        cv = c_ref[...].astype(F32)
        xav = xa_ref[...].astype(F32)
        q = cv * xav
        va = _conv_fwd_val(q, w_ref, 3)
        dyv = dy_ref[...]
        db_ref[...] = (dyv * va).astype(BF)
        dq, dws = _conv_bwd_val(q, dyv * b_ref[...].astype(F32), w_ref, 3)
        dc_ref[...] = (dq * xav).astype(BF)
        dxa_ref[...] = (dq * cv).astype(BF)
        for j in range(3):
            dw_ref[j:j + 1, :] = dws[j]

    col = pl.BlockSpec((t, CONV_CB), lambda j: (0, j))
    wsp = pl.BlockSpec((3, CONV_CB), lambda j: (0, j))
    return pl.pallas_call(
        body, name="mix_a_bwd", grid=(D_MODEL // CONV_CB,),
        in_specs=[_pspec(t, OFF_B), _pspec(t, OFF_C), _pspec(t, OFF_XA), wsp, col],
        out_specs=[col, col, col, wsp],
        out_shape=[jax.ShapeDtypeStruct((t, D_MODEL), BF)] * 3 + [jax.ShapeDtypeStruct((3, D_MODEL), F32)],
        compiler_params=_params("parallel"))(p, p, p, conv_w, dya)


def _ssm_conv_fwd(p, conv_w, conv_b):
    t = p.shape[0]

    def body(x_ref, w_ref, b_ref, o_ref):
        pre = _conv_fwd_val(x_ref[...].astype(F32), w_ref, 4) + b_ref[...]
        o_ref[...] = pre * _sigmoid(pre)

    return pl.pallas_call(
        body, name="ssm_conv_fwd", grid=(D_XBC // CONV_CB,),
        in_specs=[_pspec(t, OFF_XBC), pl.BlockSpec((4, CONV_CB), lambda j: (0, j)),
                  pl.BlockSpec((1, CONV_CB), lambda j: (0, j))],
        out_specs=pl.BlockSpec((t, CONV_CB), lambda j: (0, j)),
        out_shape=jax.ShapeDtypeStruct((t, D_XBC), F32), compiler_params=_params("parallel"))(p, conv_w, conv_b)


def _ssm_conv_bwd(p, conv_w, conv_b, dxc):
    t = p.shape[0]

    def body(x_ref, w_ref, b_ref, d_ref, dx_ref, dw_ref, db_ref):
        xv = x_ref[...].astype(F32)
        pre = _conv_fwd_val(xv, w_ref, 4) + b_ref[...]
        s = _sigmoid(pre)
        dpre = d_ref[...] * (s * (1.0 + pre * (1.0 - s)))
        dq, dws = _conv_bwd_val(xv, dpre, w_ref, 4)
        dx_ref[...] = dq.astype(BF)
        for j in range(4):
            dw_ref[j:j + 1, :] = dws[j]
        db_ref[...] = jnp.sum(dpre, axis=0, keepdims=True)

    col = pl.BlockSpec((t, CONV_CB), lambda j: (0, j))
    wsp = pl.BlockSpec((4, CONV_CB), lambda j: (0, j))
    bsp = pl.BlockSpec((1, CONV_CB), lambda j: (0, j))
    return pl.pallas_call(
        body, name="ssm_conv_bwd", grid=(D_XBC // CONV_CB,),
        in_specs=[_pspec(t, OFF_XBC), wsp, bsp, col], out_specs=[col, wsp, bsp],
        out_shape=[jax.ShapeDtypeStruct((t, D_XBC), BF), jax.ShapeDtypeStruct((4, D_XBC), F32),
                   jax.ShapeDtypeStruct((1, D_XBC), F32)],
        compiler_params=_params("parallel"))(p, conv_w, conv_b, dxc)


DT_ROWS = 512


def _tri(lower):
    r = lax.broadcasted_iota(jnp.int32, (CHUNK, CHUNK), 0)
    c = lax.broadcasted_iota(jnp.int32, (CHUNK, CHUNK), 1)
    return jnp.where((r >= c) if lower else (r <= c), 1.0, 0.0).astype(F32)


def _dot_exact(a, b):
    return lax.dot_general(a, b, _DIMS["nn"], preferred_element_type=F32, precision=lax.Precision.HIGHEST)


def _dt_fwd(p, bias_pad, alog_pad):
    t = p.shape[0]

    def body(raw_ref, b_ref, al_ref, dt_ref, acs_ref):
        z = raw_ref[...] + b_ref[...]
        dt = jnp.maximum(z, 0.0) + jnp.log(1.0 + jnp.exp(-jnp.abs(z)))
        dt_ref[...] = dt
        a = dt * (-jnp.exp(al_ref[...]))
        tri = _tri(True)
        for k in range(DT_ROWS // CHUNK):
            acs_ref[k * CHUNK:(k + 1) * CHUNK, :] = _dot_exact(tri, a[k * CHUNK:(k + 1) * CHUNK, :])

    blk = pl.BlockSpec((DT_ROWS, DT_W), lambda i: (i, 0))
    vec = pl.BlockSpec((1, DT_W), lambda i: (0, 0))
    return pl.pallas_call(
        body, name="dt_fwd", grid=(t // DT_ROWS,),
        in_specs=[pl.BlockSpec((DT_ROWS, DT_W), lambda i: (i, OFF_DT // DT_W)), vec, vec],
        out_specs=[blk, blk], out_shape=[jax.ShapeDtypeStruct((t, DT_W), F32)] * 2,
        compiler_params=_params("parallel"))(p, bias_pad, alog_pad)


def _dt_bwd(p, bias_pad, alog_pad, dt, ddt, dacs):
    t = p.shape[0]

    def body(raw_ref, b_ref, al_ref, dt_ref, ddt_ref, dacs_ref, draw_ref, db_ref, dal_ref):
        i = pl.program_id(0)
        acoef = -jnp.exp(al_ref[...])
        triu = _tri(False)
        das = []
        for k in range(DT_ROWS // CHUNK):
            das.append(_dot_exact(triu, dacs_ref[k * CHUNK:(k + 1) * CHUNK, :]))
        da = jnp.concatenate(das, axis=0)
        dtv = dt_ref[...]
        ddt_tot = ddt_ref[...] + da * acoef
        lane = lax.broadcasted_iota(jnp.int32, (DT_ROWS, DT_W), 1)
        draw = jnp.where(lane < N_HEADS, ddt_tot * _sigmoid(raw_ref[...] + b_ref[...]), 0.0)
        draw_ref[...] = draw.astype(BF)
        pb = jnp.sum(draw, axis=0, keepdims=True)
        pa = jnp.sum(da * dtv * acoef, axis=0, keepdims=True)

        @pl.when(i == 0)
        def _():
            db_ref[...] = pb
            dal_ref[...] = pa

        @pl.when(i > 0)
        def _():
            db_ref[...] += pb
            dal_ref[...] += pa

    blk = pl.BlockSpec((DT_ROWS, DT_W), lambda i: (i, 0))
    vec = pl.BlockSpec((1, DT_W), lambda i: (0, 0))
    return pl.pallas_call(
        body, name="dt_bwd", grid=(t // DT_ROWS,),
        in_specs=[pl.BlockSpec((DT_ROWS, DT_W), lambda i: (i, OFF_DT // DT_W)), vec, vec, blk, blk, blk],
        out_specs=[blk, vec, vec],
        out_shape=[jax.ShapeDtypeStruct((t, DT_W), BF), jax.ShapeDtypeStruct((1, DT_W), F32),
                   jax.ShapeDtypeStruct((1, DT_W), F32)],
        compiler_params=_params("arbitrary"))(p, bias_pad, alog_pad, dt, ddt, dacs)


def _split_dot(z, onehot, terms):
    out = None
    rest = z
    for _ in range(terms):
        piece = rest.astype(BF)
        part = _dot(piece, onehot)
        out = part if out is None else out + part
        rest = rest - piece.astype(F32)
    return out


def _spread_mat(g):
    row = lax.broadcasted_iota(jnp.int32, (DT_W, GROUP_W), 0)
    lane = lax.broadcasted_iota(jnp.int32, (DT_W, GROUP_W), 1)
    return jnp.where(row == HEADS_PER_GROUP * g + lane // HEAD_DIM, 1.0, 0.0).astype(BF)


def _gather_mat(g):
    row = lax.broadcasted_iota(jnp.int32, (GROUP_W, DT_W), 0)
    lane = lax.broadcasted_iota(jnp.int32, (GROUP_W, DT_W), 1)
    return jnp.where(lane == HEADS_PER_GROUP * g + row // HEAD_DIM, 1.0, 0.0).astype(BF)


def _ssd_masks():
    row = lax.broadcasted_iota(jnp.int32, (CHUNK, GROUP_W), 0)
    col = lax.broadcasted_iota(jnp.int32, (CHUNK, GROUP_W), 1) % HEAD_DIM
    brow = lax.broadcasted_iota(jnp.int32, (GROUP_W, GROUP_W), 0) // HEAD_DIM
    bcol = lax.broadcasted_iota(jnp.int32, (GROUP_W, GROUP_W), 1) // HEAD_DIM
    return row >= col, row == col, brow == bcol


def _stack4(v):
    return jnp.concatenate([v, v, v, v], axis=0)


def _fold4(v):
    return v[0:CHUNK] + v[CHUNK:2 * CHUNK] + v[2 * CHUNK:3 * CHUNK] + v[3 * CHUNK:4 * CHUNK]


def _ssd_group(xc_ref, stacked, g, tri, eye, blockdiag):
    gs = slice(GROUP_W * g, GROUP_W * (g + 1))
    xs_g = xc_ref[:, gs]
    b_g = xc_ref[:, D_INNER + D_STATE * g:D_INNER + D_STATE * (g + 1)].astype(BF)
    c_g = xc_ref[:, D_INNER + 1024 + D_STATE * g:D_INNER + 1024 + D_STATE * (g + 1)].astype(BF)
    wide = _split_dot(stacked, _spread_mat(g), 3)
    acs_e, dt_e = wide[0:CHUNK], wide[CHUNK:2 * CHUNK]
    atot_e = acs_e[CHUNK - 1:CHUNK, :]
    acs_j = jnp.sum(jnp.where(eye, acs_e, 0.0), axis=0, keepdims=True)
    lmat = jnp.where(tri, jnp.exp(jnp.minimum(acs_e - acs_j, 0.0)), 0.0)
    b_t = _stack4(b_g)
    m = _dot(c_g, b_t, "nt") * lmat
    x_g = xs_g * dt_e
    xbd = jnp.where(blockdiag, _stack4(x_g), 0.0).astype(BF)
    return dict(gs=gs, xs=xs_g, b=b_g, c=c_g, b_t=b_t, dt=dt_e, e=jnp.exp(acs_e), dec=jnp.exp(atot_e - acs_e),
                eat=jnp.exp(atot_e), lmat=lmat, m=m, x=x_g, xbd=xbd)


def _ssd_fwd(xconv, dt, acs, d_exp, comm=None):
    t = xconv.shape[0]
    nc = t // CHUNK

    def body(xc_ref, dt_ref, acs_ref, d_ref, y_ref, hs_ref, state):
        c = pl.program_id(0)

        @pl.when(c == 0)
        def _():
            state[...] = jnp.zeros_like(state)

        hs_ref[...] = state[...]
        tri, eye, blockdiag = _ssd_masks()
        stacked = jnp.concatenate([acs_ref[...], dt_ref[...]], axis=0)
        for g in range(N_GROUPS):
            q = _ssd_group(xc_ref, stacked, g, tri, eye, blockdiag)
            gs = q["gs"]
            h_t = state[:, gs]
            ydiag = _dot(q["m"].astype(BF), q["xbd"])
            yoff = _dot(q["c"], h_t.astype(BF)) * q["e"]
            y_ref[:, gs] = ydiag + yoff + d_ref[:, gs] * q["xs"]
            s_t = _dot(q["b"], (q["x"] * q["dec"]).astype(BF), "tn")
            state[:, gs] = q["eat"] * h_t + s_t

    blk = lambda w: pl.BlockSpec((CHUNK, w), lambda c: (c, 0))
    outs, couts = _pcall(
        "ssd_fwd", body, (nc,),
        [blk(D_XBC), blk(DT_W), blk(DT_W), pl.BlockSpec((1, D_INNER), lambda c: (0, 0))],
        [blk(D_INNER), pl.BlockSpec((None, D_STATE, D_INNER), lambda c: (c, 0, 0))],
        [jax.ShapeDtypeStruct((t, D_INNER), F32), jax.ShapeDtypeStruct((nc, D_STATE, D_INNER), F32)],
        (xconv, dt, acs, d_exp), [pltpu.VMEM((D_STATE, D_INNER), F32)], ("arbitrary",), comm)
    return outs if comm is None else (outs, couts)


def _ssd_bwd(xconv, dt, acs, d_exp, hsave, dy, comm=None):
    t = xconv.shape[0]
    nc = t // CHUNK

    def body(xc_ref, dt_ref, acs_ref, d_ref, hs_ref, dy_ref, dxc_ref, ddt_ref, dacs_ref, dd_ref, dstate):
        c = pl.program_id(0)

        @pl.when(c == 0)
        def _():
            dstate[...] = jnp.zeros_like(dstate)
            dd_ref[...] = jnp.zeros_like(dd_ref)

        tri, eye, blockdiag = _ssd_masks()
        acsv = acs_ref[...]
        stacked = jnp.concatenate([acsv, dt_ref[...]], axis=0)
        eat_heads = jnp.exp(acsv[CHUNK - 1:CHUNK, :])
        ddt_acc = jnp.zeros((CHUNK, DT_W), F32)
        dacs_acc = jnp.zeros((CHUNK, DT_W), F32)
        datot_acc = jnp.zeros((1, DT_W), F32)

        for g in range(N_GROUPS):
            q = _ssd_group(xc_ref, stacked, g, tri, eye, blockdiag)
            gs, xs_g, b_g, c_g, m = q["gs"], q["xs"], q["b"], q["c"], q["m"]
            bs = slice(D_INNER + D_STATE * g, D_INNER + D_STATE * (g + 1))
            cs = slice(D_INNER + 1024 + D_STATE * g, D_INNER + 1024 + D_STATE * (g + 1))
            h_t = hs_ref[:, gs]
            h_b = h_t.astype(BF)
            dy_g = dy_ref[:, gs]
            dy_b = dy_g.astype(BF)
            ds_t = dstate[:, gs]
            ds_b = ds_t.astype(BF)

            yoff = _dot(c_g, h_b) * q["e"]
            edy = (q["e"] * dy_g).astype(BF)
            d_c = _dot(edy, h_b, "nt")
            d_ht = _dot(c_g, edy, "tn")
            bds = _dot(b_g, ds_b)
            xd = q["x"] * q["dec"]
            d_b = _dot(xd.astype(BF), ds_b, "nt")
            dm = _dot(dy_b, q["xbd"], "nt")
            cross = _dot(m.astype(BF), dy_b, "tn")
            dx_full = q["dec"] * bds + _fold4(jnp.where(blockdiag, cross, 0.0))
            dml = (dm * q["lmat"]).astype(BF)
            d_c = d_c + _dot(dml, q["b_t"])
            d_b = d_b + _fold4(_dot(dml, c_g, "tn"))
            w = dm * m
            q_dec = xd * bds
            z = w - jnp.where(eye, jnp.sum(w, axis=0, keepdims=True), 0.0) + dy_g * yoff - q_dec
            rows = jnp.concatenate(
                [jnp.sum(q_dec, axis=0, keepdims=True), jnp.sum(ds_t * h_t, axis=0, keepdims=True),
                 jnp.zeros((6, GROUP_W), F32)], axis=0)
            seg = _split_dot(jnp.concatenate([z, dx_full * xs_g, rows], axis=0), _gather_mat(g), 2)
            dacs_acc = dacs_acc + seg[0:CHUNK]
            ddt_acc = ddt_acc + seg[CHUNK:2 * CHUNK]
            datot_acc = datot_acc + seg[2 * CHUNK:2 * CHUNK + 1] + eat_heads * seg[2 * CHUNK + 1:2 * CHUNK + 2]
            dxc_ref[:, cs] = d_c
            dxc_ref[:, bs] = d_b
            dxc_ref[:, gs] = dx_full * q["dt"] + d_ref[:, gs] * dy_g
            dd_ref[:, gs] += jnp.sum(dy_g * xs_g, axis=0, keepdims=True)
            dstate[:, gs] = q["eat"] * ds_t + d_ht

        rowi = lax.broadcasted_iota(jnp.int32, (CHUNK, DT_W), 0)
        ddt_ref[...] = ddt_acc
        dacs_ref[...] = dacs_acc + jnp.where(rowi == CHUNK - 1, datot_acc, 0.0)

    rev = lambda w: pl.BlockSpec((CHUNK, w), lambda c: (nc - 1 - c, 0))
    vec = pl.BlockSpec((1, D_INNER), lambda c: (0, 0))
    outs, couts = _pcall(
        "ssd_bwd", body, (nc,),
        [rev(D_XBC), rev(DT_W), rev(DT_W), vec,
         pl.BlockSpec((None, D_STATE, D_INNER), lambda c: (nc - 1 - c, 0, 0)), rev(D_INNER)],
        [rev(D_XBC), rev(DT_W), rev(DT_W), vec],
        [jax.ShapeDtypeStruct((t, D_XBC), F32), jax.ShapeDtypeStruct((t, DT_W), F32),
         jax.ShapeDtypeStruct((t, DT_W), F32), jax.ShapeDtypeStruct((1, D_INNER), F32)],
        (xconv, dt, acs, d_exp, hsave, dy),
        [pltpu.VMEM((D_STATE, D_INNER), F32)], ("arbitrary",), comm)
    return outs if comm is None else (outs, couts)


GN_CB = 1024
GN_GROUPS = GN_CB // GROUP_W


def _gnorm_fwd(y, p, w, comm=None):
    t = y.shape[0]
    zoff = OFF_Z // GN_CB

    def body(y_ref, z_ref, w_ref, o_ref):
        for g in range(GN_GROUPS):
            gs = slice(GROUP_W * g, GROUP_W * (g + 1))
            z = z_ref[:, gs].astype(F32)
            yf = y_ref[:, gs] * (z * _sigmoid(z))
            rstd = lax.rsqrt(jnp.mean(yf * yf, axis=-1, keepdims=True) + NORM_EPS)
            o_ref[:, gs] = (yf * rstd * w_ref[:, gs]).astype(BF)

    blk = pl.BlockSpec((TE, GN_CB), lambda i, j: (i, j))
    out, couts = _pcall(
        "gnorm_fwd", body, (t // TE, D_INNER // GN_CB),
        [blk, pl.BlockSpec((TE, GN_CB), lambda i, j: (i, zoff + j)), pl.BlockSpec((1, GN_CB), lambda i, j: (0, j))],
        blk, jax.ShapeDtypeStruct((t, D_INNER), BF), (y, p, w), (), ("parallel", "parallel"), comm)
    return out if comm is None else (out, couts)


def _gnorm_bwd(y, p, w, dyn, comm=None):
    t = y.shape[0]
    zoff = OFF_Z // GN_CB

    def body(y_ref, z_ref, w_ref, dn_ref, dy_ref, dz_ref, dw_ref):
        i = pl.program_id(1)
        for g in range(GN_GROUPS):
            gs = slice(GROUP_W * g, GROUP_W * (g + 1))
            z = z_ref[:, gs].astype(F32)
            yv = y_ref[:, gs]
            s = _sigmoid(z)
            sil = z * s
            yf = yv * sil
            rstd = lax.rsqrt(jnp.mean(yf * yf, axis=-1, keepdims=True) + NORM_EPS)
            xhat = yf * rstd
            dn = dn_ref[:, gs]
            wd = dn * w_ref[:, gs]
            proj = jnp.mean(wd * xhat, axis=-1, keepdims=True)
            dyf = rstd * (wd - xhat * proj)
            dy_ref[:, gs] = dyf * sil
            dz_ref[:, gs] = (dyf * yv * (s * (1.0 + z * (1.0 - s)))).astype(BF)
            part = jnp.sum(dn * xhat, axis=0, keepdims=True)

            @pl.when(i == 0)
            def _():
                dw_ref[:, gs] = part

            @pl.when(i > 0)
            def _():
                dw_ref[:, gs] += part

    blk = pl.BlockSpec((TE, GN_CB), lambda j, i: (i, j))
    vec = pl.BlockSpec((1, GN_CB), lambda j, i: (0, j))
    outs, couts = _pcall(
        "gnorm_bwd", body, (D_INNER // GN_CB, t // TE),
        [blk, pl.BlockSpec((TE, GN_CB), lambda j, i: (i, zoff + j)), vec, blk],
        [blk, blk, vec],
        [jax.ShapeDtypeStruct((t, D_INNER), F32), jax.ShapeDtypeStruct((t, D_INNER), BF),
         jax.ShapeDtypeStruct((1, D_INNER), F32)],
        (y, p, w, dyn), (), ("parallel", "arbitrary"), comm)
    return outs if comm is None else (outs, couts)


MERGE_CB = 512


def _merge_fwd(p, ya, yb):
    t = ya.shape[0]

    def body(ga_ref, gb_ref, ya_ref, yb_ref, o_ref):
        o_ref[...] = (_sigmoid(ga_ref[...]) * ya_ref[...] + _sigmoid(gb_ref[...]) * yb_ref[...]).astype(BF)

    blk = pl.BlockSpec((TE, MERGE_CB), lambda i, j: (i, j))
    return pl.pallas_call(
        body, name="merge_fwd", grid=(t // TE, D_MODEL // MERGE_CB),
        in_specs=[pl.BlockSpec((TE, MERGE_CB), lambda i, j: (i, OFF_GA // MERGE_CB + j)),
                  pl.BlockSpec((TE, MERGE_CB), lambda i, j: (i, OFF_GB // MERGE_CB + j)), blk, blk],
        out_specs=blk, out_shape=jax.ShapeDtypeStruct((t, D_MODEL), BF),
        compiler_params=_params("parallel", "parallel"))(p, p, ya, yb)


def _merge_bwd(p, ya, yb, dm):
    t = ya.shape[0]

    def body(ga_ref, gb_ref, ya_ref, yb_ref, dm_ref, dga_ref, dgb_ref, dya_ref, dyb_ref):
        d = dm_ref[...]
        sa = _sigmoid(ga_ref[...])
        sb = _sigmoid(gb_ref[...])
        dga_ref[...] = (d * ya_ref[...] * sa * (1.0 - sa)).astype(BF)
        dgb_ref[...] = (d * yb_ref[...] * sb * (1.0 - sb)).astype(BF)
        dya_ref[...] = (d * sa).astype(BF)
        dyb_ref[...] = (d * sb).astype(BF)

    blk = pl.BlockSpec((TE, MERGE_CB), lambda i, j: (i, j))
    return pl.pallas_call(
        body, name="merge_bwd", grid=(t // TE, D_MODEL // MERGE_CB),
        in_specs=[pl.BlockSpec((TE, MERGE_CB), lambda i, j: (i, OFF_GA // MERGE_CB + j)),
                  pl.BlockSpec((TE, MERGE_CB), lambda i, j: (i, OFF_GB // MERGE_CB + j)), blk, blk, blk],
        out_specs=[blk] * 4, out_shape=[jax.ShapeDtypeStruct((t, D_MODEL), BF)] * 4,
        compiler_params=_params("parallel", "parallel"))(p, p, ya, yb, dm)


def _adamw(name, parts, w, m, v):
    r, c = w.shape
    tr = _row_tile(r)
    tc = ADAM_COL_TILE if (tr == r and r > 512 and c % ADAM_COL_TILE == 0) else c
    n_parts = parts.shape[0]
    bc1 = 1.0 - ADAM_B1 ** ADAM_STEP
    bc2 = 1.0 - ADAM_B2 ** ADAM_STEP

    def body(p_ref, w_ref, m_ref, v_ref, g_ref, d_ref, nm_ref, nv_ref):
        g = p_ref[0].astype(F32)
        for k in range(1, n_parts):
            g = g + p_ref[k].astype(F32)
        nm = ADAM_B1 * m_ref[...] + (1.0 - ADAM_B1) * g
        nv = ADAM_B2 * v_ref[...] + (1.0 - ADAM_B2) * (g * g)
        g_ref[...] = g
        nm_ref[...] = nm
        nv_ref[...] = nv
        d_ref[...] = -ADAM_LR * ((nm / bc1) / (jnp.sqrt(nv / bc2) + ADAM_EPS) + ADAM_WD * w_ref[...])

    blk = pl.BlockSpec((tr, tc), lambda i, j: (i, j))
    return pl.pallas_call(
        body, name=name, grid=(r // tr, c // tc),
        in_specs=[pl.BlockSpec((n_parts, tr, tc), lambda i, j: (0, i, j)), blk, blk, blk],
        out_specs=[blk] * 4, out_shape=[jax.ShapeDtypeStruct((r, c), F32)] * 4,
        compiler_params=_params("parallel", "parallel"))(parts, w, m, v)


def _pad_lanes(v, width):
    return jnp.pad(v, ((0, 0), (0, width - v.shape[1])))


def _reduce_start(slots, host):
    outs, sib = host(_pair_comm([a for _, a in slots]))
    sums = [(n, _add_pairs("pairsum_" + n, a, b)) for (n, a), b in zip(slots, sib)]
    return outs, sums


def _train_step(x, target, shard, rep):
    gdt = BF
    t = x.shape[0]
    recv = {}
    (got,) = _comm_call("gather_ffn1_in", _gather_comm([shard["ffn1_w_in"]], [True]))
    w1_in = got.reshape(2 * D_FF, D_MODEL)
    h1 = _rms_fwd("rms1_fwd", x, rep["ffn1_norm"])
    w_in_rows = [shard["w_in"][a:b] for a, b in W_IN_ROW_CUTS]
    gu1, got = _mm_nt("ffn1_in", h1, w1_in, tn=FF_HALF, out_dtype=BF, comm=_gather_comm([shard["ffn1_w_out"], w_in_rows[0]]))
    w1_out = got[0].reshape(D_FF, D_MODEL)
    w_in_got = [got[1]]
    act1, got = _swiglu_fwd("swiglu1_fwd", gu1, comm=_gather_comm([w_in_rows[1]]))
    w_in_got.append(got[0])
    x1, got = _mm_nn("ffn1_out", act1, w1_out, res=x, alpha=0.5, comm=_gather_comm(
        [w_in_rows[2], shard["short_conv_w"], shard["ssm_conv_w"]]))
    w_in_got.append(got[0])
    short_conv_w = got[1].transpose(1, 0, 2).reshape(3, D_MODEL)
    ssm_conv_w = got[2].transpose(1, 0, 2).reshape(4, D_XBC)
    w_in_t = jnp.concatenate(w_in_got, axis=1).reshape(N_IN, D_MODEL)
    w_gd = jnp.concatenate([w_in_t[N_MAIN + N_HEADS:], w_in_t[N_MAIN:N_MAIN + N_HEADS],
                            jnp.zeros((DT_W - N_HEADS, D_MODEL), BF)], axis=0)

    h2 = _rms_fwd("rms2_fwd", x1, rep["mix_norm"])
    p, got = _mm_nt("proj_main", h2, w_in_t, n=N_MAIN, tn=1024, out_dtype=BF, comm=_gather_comm(
        [shard["short_w_out"], shard["ssm_w_out"], shard["w_out"]]))
    p_gd = _mm_nt("proj_gd", h2, w_gd)
    short_w_out = got[0].reshape(D_MODEL, D_MODEL)
    ssm_w_out = got[1].reshape(D_INNER, D_MODEL)
    w_out = got[2].reshape(D_MODEL, D_MODEL)
    ya_in = _mix_a_fwd(p, short_conv_w)
    y_a = _mm_nn("short_out", ya_in, short_w_out)
    xconv = _ssm_conv_fwd(p, ssm_conv_w, rep["ssm_conv_b"])
    dt, acs = _dt_fwd(p_gd, rep["dt_bias_pad"], rep["a_log_pad"])
    (y_ssm, hsave), (got,) = _ssd_fwd(xconv, dt, acs, rep["d_exp"], comm=_gather_comm([shard["ffn2_w_in"]], [True]))
    w2_in = got.reshape(2 * D_FF, D_MODEL)
    yn, got = _gnorm_fwd(y_ssm, p, rep["ssm_norm"], comm=_gather_comm([shard["ffn2_w_out"]]))
    w2_out = got[0].reshape(D_FF, D_MODEL)
    y_b = _mm_nn("ssm_out", yn, ssm_w_out, tk=1024)
    merged = _merge_fwd(p_gd, y_a, y_b)
    x2 = _mm_nn("mix_out", merged, w_out, res=x1)

    h3 = _rms_fwd("rms3_fwd", x2, rep["ffn2_norm"])
    gu2 = _mm_nt("ffn2_in", h3, w2_in, tn=FF_HALF, out_dtype=BF)
    act2 = _swiglu_fwd("swiglu2_fwd", gu2)
    x3 = _mm_nn("ffn2_out", act2, w2_out, res=x2, alpha=0.5)

    loss, dx3, dx3h, g_final = _final_loss(x3, rep["final_norm"], target)

    small = {"final_norm": g_final}
    dact2 = _mm_nt("ffn2_out_bwd_act", dx3h, w2_out, out_dtype=BF)
    g_w2_out = _mm_tn("ffn2_out_bwd_w", act2, dx3h, gdt, tm=FF_HALF)
    dgu2 = _swiglu_bwd("swiglu2_bwd", gu2, dact2)
    g_w2_in = _mm_tn("ffn2_in_bwd_w", dgu2, h3, gdt, tm=FF_HALF)
    dh3 = _mm_nn("ffn2_in_bwd_h", dgu2, w2_in, tk=FF_HALF)
    dx2, dx2b, small["ffn2_norm"] = _rms_bwd("rms3_bwd", x2, rep["ffn2_norm"], dh3, dx3, 1.0)

    dmerged = _mm_nt("mix_out_bwd_x", dx2b, w_out)
    g_w_out = _mm_tn("mix_out_bwd_w", merged, dx2b, gdt)
    dga, dgb, dya, dyb = _merge_bwd(p_gd, y_a, y_b, dmerged)

    dya_in = _mm_nt("short_out_bwd_x", dya, short_w_out)
    g_short_w_out = _mm_tn("short_out_bwd_w", ya_in, dya, gdt)
    db, dc, dxa, g_short_conv = _mix_a_bwd(p, short_conv_w, dya_in)

    dyn = _mm_nt("ssm_out_bwd_x", dyb, ssm_w_out)
    g_ssm_w_out = _mm_tn("ssm_out_bwd_w", yn, dyb, gdt)
    late = [("ffn2_w_out", g_w2_out.reshape(N_DEV, FF_SHARD // 2, D_MODEL)),
            ("ffn2_w_in", g_w2_in.reshape(N_DEV, FF_SHARD, D_MODEL)),
            ("w_out", g_w_out.reshape(N_DEV, -1, D_MODEL)), ("short_w_out", g_short_w_out.reshape(N_DEV, -1, D_MODEL)),
            ("ssm_w_out", g_ssm_w_out.reshape(N_DEV, -1, D_MODEL))]
    (dy_ssm, dz, small["ssm_norm"]), sums = _reduce_start(
        late, lambda comm: _gnorm_bwd(y_ssm, p, rep["ssm_norm"], dyn, comm=comm))
    (dxconv, ddt, dacs, dd_lane), got = _ssd_bwd(
        xconv, dt, acs, rep["d_exp"], hsave, dy_ssm,
        comm=_chip_comm([a for _, a in sums], [n == "ffn2_w_in" for n, _ in sums]))
    recv.update({n: a for (n, _), a in zip(sums, got)})
    small["ssm_D"] = dd_lane.reshape(N_HEADS, HEAD_DIM).sum(axis=1)[None, :]
    dxbc, g_ssm_conv, small["ssm_conv_b"] = _ssm_conv_bwd(p, ssm_conv_w, rep["ssm_conv_b"], dxconv)
    draw, dbias, dalog = _dt_bwd(p_gd, rep["dt_bias_pad"], rep["a_log_pad"], dt, ddt, dacs)
    small["ssm_dt_bias"] = dbias[:, :N_HEADS]
    small["ssm_A_log"] = dalog[:, :N_HEADS]

    dp = jnp.concatenate([db, dc, dxa, dz, dxbc], axis=1)
    dp_gd = jnp.concatenate([dga, dgb, draw], axis=1)
    g_main = _mm_tn("proj_main_bwd_w", dp, h2, gdt, tm=1024)
    g_gd = _mm_tn("proj_gd_bwd_w", dp_gd, h2, gdt)
    g_in_t = jnp.concatenate([g_main, g_gd[2048:2048 + N_HEADS], g_gd[0:2048]], axis=0).reshape(
        N_DEV, IN_SHARD, D_MODEL)
    w_rows = [("w_in%d" % i, g_in_t[:, a:b]) for i, (a, b) in enumerate(W_IN_ROW_CUTS)]
    dh2, w_sums = _reduce_start(w_rows, lambda comm: _mm_nn("proj_main_bwd_x", dp, w_in_t, tk=1024, comm=comm))
    dh2 = _mm_nn("proj_gd_bwd_x", dp_gd, w_gd, res=dh2)
    (dx1, dx1h, small["mix_norm"]), got0 = _rms_bwd("rms2_bwd", x1, rep["mix_norm"], dh2, dx2, 0.5,
                                                      comm=_chip_comm([w_sums[0][1]]))

    g_w1_out, got1 = _mm_tn("ffn1_out_bwd_w", act1, dx1h, gdt, tm=FF_HALF, comm=_chip_comm([w_sums[1][1]]))
    rest = [("ffn1_w_out", g_w1_out.reshape(N_DEV, FF_SHARD // 2, D_MODEL)),
            ("short_conv_w", g_short_conv.reshape(3, N_DEV, -1).transpose(1, 0, 2)),
            ("ssm_conv_w", g_ssm_conv.reshape(4, N_DEV, -1).transpose(1, 0, 2))]
    pair_rest = _pair_comm([a for _, a in rest])
    dact1, got = _mm_nt("ffn1_out_bwd_act", dx1h, w1_out, out_dtype=BF, comm=_join_comm(_chip_comm([w_sums[2][1]]), pair_rest))
    got2, sib = got[0], got[1:]
    rest_sums = [(n, _add_pairs("pairsum_" + n, a, b)) for (n, a), b in zip(rest, sib)]
    recv["w_in"] = jnp.concatenate([got0[0], got1[0], got2], axis=1)
    dgu1, got = _swiglu_bwd("swiglu1_bwd", gu1, dact1, comm=_chip_comm([a for _, a in rest_sums]))
    recv.update({n: a for (n, _), a in zip(rest_sums, got)})
    g_w1_in = _mm_tn("ffn1_in_bwd_w", dgu1, h1, gdt, tm=FF_HALF)
    dh1, last_sums = _reduce_start(
        [("ffn1_w_in", g_w1_in.reshape(N_DEV, FF_SHARD, D_MODEL))],
        lambda comm: _mm_nn("ffn1_in_bwd_h", dgu1, w1_in, tk=FF_HALF, comm=comm))
    (dx0, _, small["ffn1_norm"]), got = _rms_bwd("rms1_bwd", x, rep["ffn1_norm"], dh1, dx1, 1.0,
                                                  comm=_chip_comm([last_sums[0][1]], [True]))
    recv["ffn1_w_in"] = got[0]
    packed = _pack_small(small, loss[:, 0:1])
    (small_parts,) = _comm_call("exchange_last", _gather_comm([packed]))
    return dx0, recv, small_parts


_SMALL = [("ffn1_norm", 1024), ("mix_norm", 1024), ("ssm_conv_b", 4096), ("ssm_dt_bias", 32), ("ssm_A_log", 32),
          ("ssm_D", 32), ("ssm_norm", 2048), ("ffn2_norm", 1024), ("final_norm", 1024)]
SMALL_W = 10368


def _pack_small(d, loss=None):
    parts = [d[n].reshape(1, -1).astype(F32) for n, _ in _SMALL]
    used = sum(sz for _, sz in _SMALL)
    tail = jnp.zeros((1, SMALL_W - used), F32)
    if loss is not None:
        tail = tail.at[:, 0:1].set(loss)
    return jnp.concatenate(parts + [tail], axis=1)


def _unpack_small(v, shapes):
    out, off = {}, 0
    for n, sz in _SMALL:
        out[n] = v[:, off:off + sz].reshape(shapes[n])
        off += sz
    return out, v[0, off]


_SHARDED = ["ffn1_w_in", "ffn1_w_out", "w_in", "short_conv_w", "short_w_out", "ssm_conv_w", "ssm_w_out", "w_out",
            "ffn2_w_in", "ffn2_w_out"]
_TRANSPOSED = ("ffn1_w_in", "w_in", "ffn2_w_in")
_ORDER = ["ffn1_norm", "ffn1_w_in", "ffn1_w_out", "mix_norm", "w_in", "short_conv_w", "short_w_out", "ssm_conv_w",
          "ssm_conv_b", "ssm_dt_bias", "ssm_A_log", "ssm_D", "ssm_norm", "ssm_w_out", "w_out", "ffn2_norm",
          "ffn2_w_in", "ffn2_w_out", "final_norm"]


def kernel(x, ffn1_norm, ffn1_w_in, ffn1_w_out, mix_norm, w_in, short_conv_w, short_w_out, ssm_conv_w, ssm_conv_b, ssm_dt_bias, ssm_A_log, ssm_D, ssm_norm, ssm_w_out, w_out, ffn2_norm, ffn2_w_in, ffn2_w_out, final_norm, loss_target, m_ffn1_norm, m_ffn1_w_in, m_ffn1_w_out, m_mix_norm, m_w_in, m_short_conv_w, m_short_w_out, m_ssm_conv_w, m_ssm_conv_b, m_ssm_dt_bias, m_ssm_A_log, m_ssm_D, m_ssm_norm, m_ssm_w_out, m_w_out, m_ffn2_norm, m_ffn2_w_in, m_ffn2_w_out, m_final_norm, v_ffn1_norm, v_ffn1_w_in, v_ffn1_w_out, v_mix_norm, v_w_in, v_short_conv_w, v_short_w_out, v_ssm_conv_w, v_ssm_conv_b, v_ssm_dt_bias, v_ssm_A_log, v_ssm_D, v_ssm_norm, v_ssm_w_out, v_w_out, v_ffn2_norm, v_ffn2_w_in, v_ffn2_w_out, v_final_norm):
    w = dict(ffn1_norm=ffn1_norm, ffn1_w_in=ffn1_w_in, ffn1_w_out=ffn1_w_out, mix_norm=mix_norm, w_in=w_in,
             short_conv_w=short_conv_w, short_w_out=short_w_out, ssm_conv_w=ssm_conv_w, ssm_conv_b=ssm_conv_b,
             ssm_dt_bias=ssm_dt_bias, ssm_A_log=ssm_A_log, ssm_D=ssm_D, ssm_norm=ssm_norm, ssm_w_out=ssm_w_out,
             w_out=w_out, ffn2_norm=ffn2_norm, ffn2_w_in=ffn2_w_in, ffn2_w_out=ffn2_w_out, final_norm=final_norm)
    m = dict(ffn1_norm=m_ffn1_norm, ffn1_w_in=m_ffn1_w_in, ffn1_w_out=m_ffn1_w_out, mix_norm=m_mix_norm, w_in=m_w_in,
             short_conv_w=m_short_conv_w, short_w_out=m_short_w_out, ssm_conv_w=m_ssm_conv_w,
             ssm_conv_b=m_ssm_conv_b, ssm_dt_bias=m_ssm_dt_bias, ssm_A_log=m_ssm_A_log, ssm_D=m_ssm_D,
             ssm_norm=m_ssm_norm, ssm_w_out=m_ssm_w_out, w_out=m_w_out, ffn2_norm=m_ffn2_norm,
             ffn2_w_in=m_ffn2_w_in, ffn2_w_out=m_ffn2_w_out, final_norm=m_final_norm)
    v = dict(ffn1_norm=v_ffn1_norm, ffn1_w_in=v_ffn1_w_in, ffn1_w_out=v_ffn1_w_out, mix_norm=v_mix_norm, w_in=v_w_in,
             short_conv_w=v_short_conv_w, short_w_out=v_short_w_out, ssm_conv_w=v_ssm_conv_w,
             ssm_conv_b=v_ssm_conv_b, ssm_dt_bias=v_ssm_dt_bias, ssm_A_log=v_ssm_A_log, ssm_D=v_ssm_D,
             ssm_norm=v_ssm_norm, ssm_w_out=v_ssm_w_out, w_out=v_w_out, ffn2_norm=v_ffn2_norm,
             ffn2_w_in=v_ffn2_w_in, ffn2_w_out=v_ffn2_w_out, final_norm=v_final_norm)
    shapes = {n: w[n].shape for n in _ORDER}

    def local(d, n):
        return d[n][0].T if n in _TRANSPOSED else d[n][0]

    shard = {n: local(w, n) for n in _SHARDED}

    wire = {n: (shard[n] if n in ("short_conv_w", "ssm_conv_w") else shard[n].astype(BF)) for n in _SHARDED}
    rep = {
        "ffn1_norm": ffn1_norm, "mix_norm": mix_norm, "ffn2_norm": ffn2_norm, "ssm_norm": ssm_norm,
        "ssm_conv_b": ssm_conv_b, "final_norm": final_norm.reshape(1, D_MODEL),
        "dt_bias_pad": _pad_lanes(ssm_dt_bias, DT_W), "a_log_pad": _pad_lanes(ssm_A_log, DT_W),
        "d_exp": jnp.repeat(ssm_D, HEAD_DIM, axis=1),
    }
    grad_x, parts, small_parts = _train_step(x[0], loss_target[0], wire, rep)

    out_g, out_d, out_m, out_v = {}, {}, {}, {}
    for n in _SHARDED:
        res = _adamw("adamw_" + n, parts[n], shard[n], local(m, n), local(v, n))
        out_g[n], out_d[n], out_m[n], out_v[n] = [(r.T if n in _TRANSPOSED else r).reshape(shapes[n]) for r in res]
    sres = _adamw("adamw_small", small_parts, _pack_small(w), _pack_small(m), _pack_small(v))
    sg, loss = _unpack_small(sres[0], shapes)
    sd, _ = _unpack_small(sres[1], shapes)
    sm, _ = _unpack_small(sres[2], shapes)
    sv, _ = _unpack_small(sres[3], shapes)
    out_g.update(sg)
    out_d.update(sd)
    out_m.update(sm)
    out_v.update(sv)
    return (loss, grad_x[None], *[out_g[n] for n in _ORDER], *[out_d[n] for n in _ORDER],
            *[out_m[n] for n in _ORDER], *[out_v[n] for n in _ORDER])
```

```python
import functools

import jax
import jax.numpy as jnp
from jax import lax
from jax.experimental import pallas as pl
from jax.experimental.pallas import tpu as pltpu

F32 = jnp.float32
BF = jnp.bfloat16

N_DEV = 8
D_MODEL = 1024
D_FF = 2816
D_INNER = 2048
D_XBC = 4096
N_HEADS = 32
HEAD_DIM = 64
N_GROUPS = 8
D_STATE = 128
CHUNK = 64
GROUP_W = D_INNER // N_GROUPS
HEADS_PER_GROUP = N_HEADS // N_GROUPS
NORM_EPS = 1e-5
N_IN = 11296
FF_SHARD = 2 * D_FF // N_DEV
FF_HALF = D_FF // 2
IN_SHARD = N_IN // N_DEV

OFF_B, OFF_C, OFF_XA, OFF_Z, OFF_XBC = 0, 1024, 2048, 3072, 5120
N_MAIN = 9216
OFF_GA, OFF_GB, OFF_DT = 0, 1024, 2048
DT_W = 128
N_GD = 2048 + DT_W
W_IN_ROW_CUTS = [(0, 480), (480, 944), (944, 1412)]

ADAM_LR, ADAM_B1, ADAM_B2, ADAM_EPS, ADAM_WD, ADAM_STEP = 0.001, 0.9, 0.999, 1e-08, 0.01, 10

VMEM_LIMIT_V7X = 56 * 1024 * 1024
TM = 1024
TE = 512
ADAM_COL_TILE = 256


def _params(*sem):
    return pltpu.CompilerParams(dimension_semantics=sem, vmem_limit_bytes=VMEM_LIMIT_V7X)


_DIMS = {
    "nn": (((1,), (0,)), ((), ())),
    "nt": (((1,), (1,)), ((), ())),
    "tn": (((0,), (0,)), ((), ())),
}


def _dot(a, b, mode="nn"):
    return lax.dot_general(a, b, _DIMS[mode], preferred_element_type=F32)


def _sigmoid(x):
    return 1.0 / (1.0 + jnp.exp(-x))


class _Comm:
    def __init__(self, inputs, out_shapes, sems, start, finish):
        self.inputs, self.out_shapes, self.sems, self.start, self.finish = inputs, out_shapes, sems, start, finish


def _pcall(name, body, grid, in_specs, out_specs, out_shape, args, scratch=(), sem=None, comm=None):
    single = not isinstance(out_shape, (list, tuple))
    out_shapes = [out_shape] if single else list(out_shape)
    out_specs = [out_specs] if single else list(out_specs)
    n_in, n_out, n_scr = len(args), len(out_shapes), len(scratch)
    if comm is None:
        res = pl.pallas_call(
            body, name=name, grid=grid, in_specs=list(in_specs), out_specs=out_specs, out_shape=out_shapes,
            scratch_shapes=list(scratch), compiler_params=_params(*sem))(*args)
        return (res[0] if single else res), []
    nci, nco = len(comm.inputs), len(comm.out_shapes)

    def wrapped(*refs):
        a = refs[:n_in]
        ci = refs[n_in:n_in + nci]
        o0 = n_in + nci
        o = refs[o0:o0 + n_out]
        co = refs[o0 + n_out:o0 + n_out + nco]
        s0 = o0 + n_out + nco
        s = refs[s0:s0 + n_scr]
        cs = refs[s0 + n_scr:]
        pids = [pl.program_id(i) for i in range(len(grid))]
        first = functools.reduce(jnp.logical_and, [p == 0 for p in pids])
        last = functools.reduce(jnp.logical_and, [p == g - 1 for p, g in zip(pids, grid)])

        @pl.when(first)
        def _():
            comm.start(ci, co, cs)

        body(*a, *o, *s)

        @pl.when(last)
        def _():
            comm.finish(ci, co, cs)

    any_spec = pl.BlockSpec(memory_space=pl.ANY)
    res = pl.pallas_call(
        wrapped, name=name, grid=grid, in_specs=list(in_specs) + [any_spec] * nci,
        out_specs=out_specs + [any_spec] * nco, out_shape=out_shapes + list(comm.out_shapes),
        scratch_shapes=list(scratch) + list(comm.sems),
        compiler_params=_params(*(("arbitrary",) * len(grid))))(*args, *comm.inputs)
    core = res[:n_out]
    return (core[0] if single else core), list(res[n_out:])


def _comm_call(name, comm):
    nci, nco = len(comm.inputs), len(comm.out_shapes)

    def body(*refs):
        ci, co, cs = refs[:nci], refs[nci:nci + nco], refs[nci + nco:]
        comm.start(ci, co, cs)
        comm.finish(ci, co, cs)

    any_spec = pl.BlockSpec(memory_space=pl.ANY)
    return pl.pallas_call(
        body, name=name, in_specs=[any_spec] * nci, out_specs=[any_spec] * nco, out_shape=list(comm.out_shapes),
        scratch_shapes=list(comm.sems), compiler_params=pltpu.CompilerParams(has_side_effects=True))(*comm.inputs)


def _remote(src, dst, ssem, rsem, dev):
    return pltpu.make_async_remote_copy(src_ref=src, dst_ref=dst, send_sem=ssem, recv_sem=rsem, device_id=dev,
                                        device_id_type=pl.DeviceIdType.MESH)


def _place():
    x, y, c = lax.axis_index("x"), lax.axis_index("y"), lax.axis_index("c")
    other_chips = [(1 - x, y), (x, 1 - y), (1 - x, 1 - y)]
    return x, y, c, other_chips


def _slot(x, y, c, swap):
    return 4 * y + 2 * x + c if swap else 4 * x + 2 * y + c


def _chip_slot(x, y, swap):
    return 2 * y + x if swap else 2 * x + y


def _gather_comm(shards, swaps=None):
    n = len(shards)
    per = N_DEV - 1
    swaps = [False] * n if swaps is None else swaps

    def start(ins, outs, sems):
        send, recv, loc = sems
        x, y, c, chips = _place()
        for i in range(n):
            me = _slot(x, y, c, swaps[i])
            pltpu.make_async_copy(ins[i], outs[i].at[me], loc.at[i]).start()
            _remote(ins[i], outs[i].at[me], send.at[per * i], recv.at[per * i], (x, y, 1 - c)).start()
            for j, (qx, qy) in enumerate(chips):
                _remote(ins[i], outs[i].at[me], send.at[per * i + 1 + j], recv.at[per * i + 1 + j], (qx, qy, c)).start()

    def finish(ins, outs, sems):
        send, recv, loc = sems
        x, y, c, chips = _place()
        sib = (x, y, 1 - c)
        for i in range(n):
            for j, (qx, qy) in enumerate(chips):
                blk = outs[i].at[_slot(qx, qy, c, swaps[i])]
                _remote(blk, blk, send.at[per * i + 1 + j], recv.at[per * i + 1 + j], (qx, qy, c)).wait_recv()
                _remote(blk, blk, send.at[per * i + 4 + j], recv.at[per * i + 4 + j], sib).start()
        for i in range(n):
            blk = outs[i].at[_slot(x, y, 1 - c, swaps[i])]
            _remote(blk, blk, send.at[per * i], recv.at[per * i], sib).wait_recv()
            for j, (qx, qy) in enumerate(chips):
                blk = outs[i].at[_slot(qx, qy, 1 - c, swaps[i])]
                _remote(blk, blk, send.at[per * i + 4 + j], recv.at[per * i + 4 + j], sib).wait_recv()
        for i in range(n):
            own = outs[i].at[_slot(x, y, c, swaps[i])]
            for k in range(per):
                _remote(ins[i], own, send.at[per * i + k], recv.at[per * i + k], sib).wait_send()
            pltpu.make_async_copy(ins[i], own, loc.at[i]).wait()

    out_shapes = [jax.ShapeDtypeStruct((N_DEV,) + tuple(a.shape), a.dtype) for a in shards]
    sems = [pltpu.SemaphoreType.DMA((per * n,)), pltpu.SemaphoreType.DMA((per * n,)), pltpu.SemaphoreType.DMA((n,))]
    return _Comm(list(shards), out_shapes, sems, start, finish)


def _pair_comm(slots):
    n = len(slots)

    def copies(ins, outs, sems):
        send, recv = sems
        x, y, c, _ = _place()
        sib = (x, y, 1 - c)
        out = []
        for i in range(n):
            for q in range(4):
                out.append(_remote(ins[i].at[2 * q + 1 - c], outs[i].at[q], send.at[4 * i + q], recv.at[4 * i + q], sib))
        return out

    def start(ins, outs, sems):
        for cp in copies(ins, outs, sems):
            cp.start()

    def finish(ins, outs, sems):
        for cp in copies(ins, outs, sems):
            cp.wait_send()
            cp.wait_recv()

    out_shapes = [jax.ShapeDtypeStruct((4,) + tuple(a.shape[1:]), a.dtype) for a in slots]
    sems = [pltpu.SemaphoreType.DMA((4 * n,)), pltpu.SemaphoreType.DMA((4 * n,))]
    return _Comm(list(slots), out_shapes, sems, start, finish)


def _chip_comm(chip_sums, swaps=None):
    n = len(chip_sums)
    swaps = [False] * n if swaps is None else swaps

    def start(ins, outs, sems):
        send, recv, loc = sems
        x, y, c, chips = _place()
        for i in range(n):
            mine = _chip_slot(x, y, swaps[i])
            pltpu.make_async_copy(ins[i].at[mine], outs[i].at[mine], loc.at[i]).start()
            for j, (qx, qy) in enumerate(chips):
                _remote(ins[i].at[_chip_slot(qx, qy, swaps[i])], outs[i].at[mine], send.at[3 * i + j],
                        recv.at[3 * i + j], (qx, qy, c)).start()

    def finish(ins, outs, sems):
        send, recv, loc = sems
        x, y, c, chips = _place()
        for i in range(n):
            mine = _chip_slot(x, y, swaps[i])
            for j, (qx, qy) in enumerate(chips):
                theirs = _chip_slot(qx, qy, swaps[i])
                cp = _remote(ins[i].at[theirs], outs[i].at[theirs], send.at[3 * i + j], recv.at[3 * i + j], (qx, qy, c))
                cp.wait_send()
                cp.wait_recv()
            pltpu.make_async_copy(ins[i].at[mine], outs[i].at[mine], loc.at[i]).wait()

    out_shapes = [jax.ShapeDtypeStruct(a.shape, a.dtype) for a in chip_sums]
    sems = [pltpu.SemaphoreType.DMA((3 * n,)), pltpu.SemaphoreType.DMA((3 * n,)), pltpu.SemaphoreType.DMA((n,))]
    return _Comm(list(chip_sums), out_shapes, sems, start, finish)


def _join_comm(a, b):
    na_i, na_o, na_s = len(a.inputs), len(a.out_shapes), len(a.sems)

    def start(ins, outs, sems):
        a.start(ins[:na_i], outs[:na_o], sems[:na_s])
        b.start(ins[na_i:], outs[na_o:], sems[na_s:])

    def finish(ins, outs, sems):
        a.finish(ins[:na_i], outs[:na_o], sems[:na_s])
        b.finish(ins[na_i:], outs[na_o:], sems[na_s:])

    return _Comm(a.inputs + b.inputs, a.out_shapes + b.out_shapes, a.sems + b.sems, start, finish)


def _row_tile(r):
    for cand in (256, 128):
        if r > cand and r % cand == 0:
            return cand
    return r


def _add_pairs(name, slots, sib):
    r, c = slots.shape[1:]
    tr = _row_tile(r)

    def body(core_ref, s_ref, b_ref, o_ref):
        o_ref[...] = (s_ref[...].astype(F32) + b_ref[...].astype(F32)).astype(o_ref.dtype)

    core = jnp.full((1,), lax.axis_index("c"), jnp.int32)
    return pl.pallas_call(
        body, name=name,
        grid_spec=pltpu.PrefetchScalarGridSpec(
            num_scalar_prefetch=1, grid=(4, r // tr),
            in_specs=[pl.BlockSpec((None, None, tr, c), lambda q, i, core_ref: (q, core_ref[0], i, 0)),
                      pl.BlockSpec((None, tr, c), lambda q, i, core_ref: (q, i, 0))],
            out_specs=pl.BlockSpec((None, tr, c), lambda q, i, core_ref: (q, i, 0))),
        out_shape=jax.ShapeDtypeStruct((4, r, c), slots.dtype),
        compiler_params=_params("parallel", "parallel"))(core, slots.reshape(4, 2, r, c), sib)


def _matmul(name, mode, a, b, grid, a_spec, b_spec, o_spec, out_shape, acc_shape,
            res=None, res_spec=None, alpha=1.0, comm=None):
    nk = grid[-1]
    has_res = res is not None

    def body(*refs):
        if has_res:
            a_ref, b_ref, r_ref, o_ref = refs[:4]
        else:
            a_ref, b_ref, o_ref = refs[:3]
            r_ref = None
        part = _dot(a_ref[...], b_ref[...], mode)

        def finish(v):
            if alpha != 1.0:
                v = v * alpha
            if has_res:
                v = r_ref[...] + v
            o_ref[...] = v.astype(o_ref.dtype)

        if nk == 1:
            finish(part)
        else:
            acc = refs[-1]
            k = pl.program_id(len(grid) - 1)

            @pl.when(k == 0)
            def _():
                acc[...] = part

            @pl.when(k > 0)
            def _():
                acc[...] += part

            @pl.when(k == nk - 1)
            def _():
                finish(acc[...])

    in_specs = [a_spec, b_spec] + ([res_spec] if has_res else [])
    args = (a, b) + ((res,) if has_res else ())
    scratch = [] if nk == 1 else [pltpu.VMEM(acc_shape, F32)]
    sem = ("parallel",) * (len(grid) - 1) + ("arbitrary",)
    out, couts = _pcall(name, body, grid, in_specs, o_spec, out_shape, args, scratch, sem, comm)
    return out if comm is None else (out, couts)


def _mm_nn(name, a, b, out_dtype=F32, res=None, alpha=1.0, tk=None, comm=None):
    t, kk = a.shape
    n = b.shape[1]
    tk = kk if tk is None else tk
    grid = (t // TM, 1, kk // tk)
    return _matmul(
        name, "nn", a, b, grid,
        pl.BlockSpec((TM, tk), lambda i, j, k: (i, k)),
        pl.BlockSpec((tk, n), lambda i, j, k: (k, 0)),
        pl.BlockSpec((TM, n), lambda i, j, k: (i, 0)),
        jax.ShapeDtypeStruct((t, n), out_dtype), (TM, n),
        res=res, res_spec=pl.BlockSpec((TM, n), lambda i, j, k: (i, 0)), alpha=alpha, comm=comm)


def _mm_nt(name, a, b, n=None, tn=None, tk=None, out_dtype=F32, comm=None):
    t, kk = a.shape
    n = b.shape[0] if n is None else n
    tn = n if tn is None else tn
    tk = kk if tk is None else tk
    grid = (n // tn, t // TM, kk // tk)
    return _matmul(
        name, "nt", a, b, grid,
        pl.BlockSpec((TM, tk), lambda j, i, k: (i, k)),
        pl.BlockSpec((tn, tk), lambda j, i, k: (j, k)),
        pl.BlockSpec((TM, tn), lambda j, i, k: (i, j)),
        jax.ShapeDtypeStruct((t, n), out_dtype), (TM, tn), comm=comm)


def _mm_tn(name, a, b, out_dtype, tm=None, comm=None):
    t, m = a.shape
    n = b.shape[1]
    tm = m if tm is None else tm
    grid = (m // tm, 1, t // TM)
    return _matmul(
        name, "tn", a, b, grid,
        pl.BlockSpec((TM, tm), lambda j, i, k: (k, j)),
        pl.BlockSpec((TM, n), lambda j, i, k: (k, 0)),
        pl.BlockSpec((tm, n), lambda j, i, k: (j, 0)),
        jax.ShapeDtypeStruct((m, n), out_dtype), (tm, n), comm=comm)


def _rms_fwd(name, x, w):
    t, d = x.shape

    def body(x_ref, w_ref, h_ref):
        xv = x_ref[...]
        rstd = lax.rsqrt(jnp.mean(xv * xv, axis=-1, keepdims=True) + NORM_EPS)
        h_ref[...] = (xv * rstd * w_ref[...]).astype(h_ref.dtype)

    return pl.pallas_call(
        body, name=name, grid=(t // TE,),
        in_specs=[pl.BlockSpec((TE, d), lambda i: (i, 0)), pl.BlockSpec((1, d), lambda i: (0, 0))],
        out_specs=pl.BlockSpec((TE, d), lambda i: (i, 0)),
        out_shape=jax.ShapeDtypeStruct((t, d), BF), compiler_params=_params("parallel"))(x, w)


def _rms_bwd(name, x, w, dh, dres, out_scale, comm=None):
    t, d = x.shape

    def body(x_ref, w_ref, dh_ref, dres_ref, dx_ref, dxb_ref, dw_ref):
        i = pl.program_id(0)
        xv = x_ref[...]
        rstd = lax.rsqrt(jnp.mean(xv * xv, axis=-1, keepdims=True) + NORM_EPS)
        xhat = xv * rstd
        dhv = dh_ref[...]
        wd = dhv * w_ref[...]
        proj = jnp.mean(wd * xhat, axis=-1, keepdims=True)
        dx = dres_ref[...] + rstd * (wd - xhat * proj)
        dx_ref[...] = dx
        dxb_ref[...] = (dx * out_scale).astype(BF)
        part = jnp.sum(dhv * xhat, axis=0, keepdims=True)

        @pl.when(i == 0)
        def _():
            dw_ref[...] = part

        @pl.when(i > 0)
        def _():
            dw_ref[...] += part

    row = pl.BlockSpec((TE, d), lambda i: (i, 0))
    vec = pl.BlockSpec((1, d), lambda i: (0, 0))
    outs, couts = _pcall(
        name, body, (t // TE,), [row, vec, row, row], [row, row, vec],
        [jax.ShapeDtypeStruct((t, d), F32), jax.ShapeDtypeStruct((t, d), BF), jax.ShapeDtypeStruct((1, d), F32)],
        (x, w, dh, dres), (), ("arbitrary",), comm)
    return outs if comm is None else (outs, couts)


def _final_loss(x, w, target):
    t, d = x.shape

    def body(x_ref, w_ref, t_ref, loss_ref, dx_ref, dxb_ref, dw_ref):
        i = pl.program_id(0)
        xv = x_ref[...]
        rstd = lax.rsqrt(jnp.mean(xv * xv, axis=-1, keepdims=True) + NORM_EPS)
        xhat = xv * rstd
        err = xhat * w_ref[...] - t_ref[...]
        lpart = 0.5 * jnp.sum(jnp.mean(err * err, axis=-1, keepdims=True), axis=0, keepdims=True)
        dy = err * (1.0 / d)
        wd = dy * w_ref[...]
        proj = jnp.mean(wd * xhat, axis=-1, keepdims=True)
        dx = rstd * (wd - xhat * proj)
        dx_ref[...] = dx
        dxb_ref[...] = (0.5 * dx).astype(BF)
        part = jnp.sum(dy * xhat, axis=0, keepdims=True)
        lfull = jnp.broadcast_to(lpart, (1, 128))

        @pl.when(i == 0)
        def _():
            dw_ref[...] = part
            loss_ref[...] = lfull

        @pl.when(i > 0)
        def _():
            dw_ref[...] += part
            loss_ref[...] += lfull

    row = pl.BlockSpec((TE, d), lambda i: (i, 0))
    vec = pl.BlockSpec((1, d), lambda i: (0, 0))
    return pl.pallas_call(
        body, name="final_loss", grid=(t // TE,), in_specs=[row, vec, row],
        out_specs=[pl.BlockSpec((1, 128), lambda i: (0, 0)), row, row, vec],
        out_shape=[jax.ShapeDtypeStruct((1, 128), F32), jax.ShapeDtypeStruct((t, d), F32),
                   jax.ShapeDtypeStruct((t, d), BF), jax.ShapeDtypeStruct((1, d), F32)],
        compiler_params=_params("arbitrary"))(x, w, target)


def _swiglu_fwd(name, gu, comm=None):
    t = gu.shape[0]

    def body(g_ref, u_ref, a_ref):
        g = g_ref[...].astype(F32)
        a_ref[...] = (g * _sigmoid(g) * u_ref[...].astype(F32)).astype(BF)

    blk = (TE, FF_HALF)
    out, couts = _pcall(
        name, body, (t // TE, 2),
        [pl.BlockSpec(blk, lambda i, j: (i, 2 * j)), pl.BlockSpec(blk, lambda i, j: (i, 2 * j + 1))],
        pl.BlockSpec(blk, lambda i, j: (i, j)), jax.ShapeDtypeStruct((t, D_FF), BF),
        (gu, gu), (), ("parallel", "parallel"), comm)
    return out if comm is None else (out, couts)


def _swiglu_bwd(name, gu, dact, comm=None):
    t = gu.shape[0]

    def body(g_ref, u_ref, da_ref, o_ref):
        g = g_ref[...].astype(F32)
        da = da_ref[...].astype(F32)
        s = _sigmoid(g)
        o_ref[:, 0:FF_HALF] = (da * u_ref[...].astype(F32) * (s * (1.0 + g * (1.0 - s)))).astype(BF)
        o_ref[:, FF_HALF:2 * FF_HALF] = (da * g * s).astype(BF)

    blk = (TE, FF_HALF)
    out, couts = _pcall(
        name, body, (t // TE, 2),
        [pl.BlockSpec(blk, lambda i, j: (i, 2 * j)), pl.BlockSpec(blk, lambda i, j: (i, 2 * j + 1)),
         pl.BlockSpec(blk, lambda i, j: (i, j))],
        pl.BlockSpec((TE, 2 * FF_HALF), lambda i, j: (i, j)),
        jax.ShapeDtypeStruct((t, 2 * D_FF), BF), (gu, gu, dact), (), ("parallel", "parallel"), comm)
    return out if comm is None else (out, couts)


CONV_CB = 256


CONV_ROWS = 64
CONV_HALO = 16


def _taps_down(ext, w, k):
    shifted = [pltpu.roll(ext, k - 1 - j, 0)[CONV_HALO:] for j in range(k - 1)] + [ext[CONV_HALO:]]
    out = shifted[k - 1] * w[k - 1:k, :]
    for j in range(k - 1):
        out = out + shifted[j] * w[j:j + 1, :]
    return out, shifted


def _taps_up(ext, w, k):
    rows = ext.shape[0]
    n = rows - CONV_HALO
    out = ext[:n] * w[k - 1:k, :]
    for j in range(k - 1):
        out = out + pltpu.roll(ext, rows - (k - 1 - j), 0)[:n] * w[j:j + 1, :]
    return out


def _rows_before(ref, i, r0):
    start = pl.multiple_of(jnp.maximum(r0 - CONV_HALO, 0), CONV_HALO)
    return jnp.where(i > 0, ref[pl.ds(start, CONV_HALO), :].astype(F32), 0.0)


def _rows_after(ref, r0, t):
    start = pl.multiple_of(jnp.minimum(r0 + CONV_ROWS, t - CONV_HALO), CONV_HALO)
    return ref[pl.ds(start, CONV_HALO), :].astype(F32)


def _fold8(v):
    return v.reshape(v.shape[0] // 8, 8, v.shape[1]).sum(axis=0)


def _silu_grad(pre):
    s = _sigmoid(pre)
    return s * (1.0 + pre * (1.0 - s))


def _pspec(t, off):
    base = off // CONV_CB
    return pl.BlockSpec((t, CONV_CB), lambda j: (0, base + j))


def _mix_a_fwd(p, conv_w):
    t = p.shape[0]

    def body(b_ref, c_ref, xa_ref, w_ref, o_ref):
        w = w_ref[...]

        def step(i, carry):
            r0 = pl.multiple_of(i * CONV_ROWS, CONV_ROWS)
            rows = pl.ds(r0, CONV_ROWS)
            q = c_ref[rows, :].astype(F32) * xa_ref[rows, :].astype(F32)
            q_before = _rows_before(c_ref, i, r0) * _rows_before(xa_ref, i, r0)
            va, _ = _taps_down(jnp.concatenate([q_before, q], axis=0), w, 3)
            o_ref[rows, :] = (b_ref[rows, :].astype(F32) * va).astype(BF)
            return carry

        lax.fori_loop(0, t // CONV_ROWS, step, 0)

    return pl.pallas_call(
        body, name="mix_a_fwd", grid=(D_MODEL // CONV_CB,),
        in_specs=[_pspec(t, OFF_B), _pspec(t, OFF_C), _pspec(t, OFF_XA),
                  pl.BlockSpec((3, CONV_CB), lambda j: (0, j))],
        out_specs=pl.BlockSpec((t, CONV_CB), lambda j: (0, j)),
        out_shape=jax.ShapeDtypeStruct((t, D_MODEL), BF), compiler_params=_params("parallel"))(p, p, p, conv_w)


def _mix_a_bwd(p, conv_w, dya):
    t = p.shape[0]

    def body(b_ref, c_ref, xa_ref, w_ref, dy_ref, db_ref, dc_ref, dxa_ref, dw_ref):
        w = w_ref[...]
        n = t // CONV_ROWS

        def step(i, acc):
            r0 = pl.multiple_of(i * CONV_ROWS, CONV_ROWS)
            rows = pl.ds(r0, CONV_ROWS)
            cv = c_ref[rows, :].astype(F32)
            xav = xa_ref[rows, :].astype(F32)
            q_before = _rows_before(c_ref, i, r0) * _rows_before(xa_ref, i, r0)
            va, shifted = _taps_down(jnp.concatenate([q_before, cv * xav], axis=0), w, 3)
            dyv = dy_ref[rows, :]
            db_ref[rows, :] = (dyv * va).astype(BF)
            dv = dyv * b_ref[rows, :].astype(F32)
            dv_after = jnp.where(i < n - 1, _rows_after(dy_ref, r0, t) * _rows_after(b_ref, r0, t), 0.0)
            dq = _taps_up(jnp.concatenate([dv, dv_after], axis=0), w, 3)
            dc_ref[rows, :] = (dq * xav).astype(BF)
            dxa_ref[rows, :] = (dq * cv).astype(BF)
            return tuple(a + _fold8(dv * s) for a, s in zip(acc, shifted))

        zero = jnp.zeros((8, CONV_CB), F32)
        acc = lax.fori_loop(0, n, step, (zero, zero, zero))
        for j in range(3):
            dw_ref[j:j + 1, :] = jnp.sum(acc[j], axis=0, keepdims=True)

    col = pl.BlockSpec((t, CONV_CB), lambda j: (0, j))
    wsp = pl.BlockSpec((3, CONV_CB), lambda j: (0, j))
    return pl.pallas_call(
        body, name="mix_a_bwd", grid=(D_MODEL // CONV_CB,),
        in_specs=[_pspec(t, OFF_B), _pspec(t, OFF_C), _pspec(t, OFF_XA), wsp, col],
        out_specs=[col, col, col, wsp],
        out_shape=[jax.ShapeDtypeStruct((t, D_MODEL), BF)] * 3 + [jax.ShapeDtypeStruct((3, D_MODEL), F32)],
        compiler_params=_params("parallel"))(p, p, p, conv_w, dya)


def _ssm_conv_fwd(p, conv_w, conv_b):
    t = p.shape[0]

    def body(x_ref, w_ref, b_ref, o_ref):
        w = w_ref[...]
        bias = b_ref[...]

        def step(i, carry):
            r0 = pl.multiple_of(i * CONV_ROWS, CONV_ROWS)
            rows = pl.ds(r0, CONV_ROWS)
            ext = jnp.concatenate([_rows_before(x_ref, i, r0), x_ref[rows, :].astype(F32)], axis=0)
            pre = _taps_down(ext, w, 4)[0] + bias
            o_ref[rows, :] = pre * _sigmoid(pre)
            return carry

        lax.fori_loop(0, t // CONV_ROWS, step, 0)

    return pl.pallas_call(
        body, name="ssm_conv_fwd", grid=(D_XBC // CONV_CB,),
        in_specs=[_pspec(t, OFF_XBC), pl.BlockSpec((4, CONV_CB), lambda j: (0, j)),
                  pl.BlockSpec((1, CONV_CB), lambda j: (0, j))],
        out_specs=pl.BlockSpec((t, CONV_CB), lambda j: (0, j)),
        out_shape=jax.ShapeDtypeStruct((t, D_XBC), F32), compiler_params=_params("parallel"))(p, conv_w, conv_b)


def _ssm_conv_bwd(p, conv_w, conv_b, dxc):
    t = p.shape[0]

    def body(x_ref, w_ref, b_ref, d_ref, dx_ref, dw_ref, db_ref):
        w = w_ref[...]
        bias = b_ref[...]
        n = t // CONV_ROWS

        def step(i, acc):
            r0 = pl.multiple_of(i * CONV_ROWS, CONV_ROWS)
            rows = pl.ds(r0, CONV_ROWS)
            x_cur = x_ref[rows, :].astype(F32)
            pre, shifted = _taps_down(jnp.concatenate([_rows_before(x_ref, i, r0), x_cur], axis=0), w, 4)
            pre = pre + bias
            dpre = d_ref[rows, :] * _silu_grad(pre)
            ext_after = jnp.concatenate([x_cur[CONV_ROWS - CONV_HALO:], _rows_after(x_ref, r0, t)], axis=0)
            pre_after = _taps_down(ext_after, w, 4)[0] + bias
            dpre_after = jnp.where(i < n - 1, _rows_after(d_ref, r0, t) * _silu_grad(pre_after), 0.0)
            dx_ref[rows, :] = _taps_up(jnp.concatenate([dpre, dpre_after], axis=0), w, 4).astype(BF)
            new = tuple(a + _fold8(dpre * s) for a, s in zip(acc[:4], shifted))
            return new + (acc[4] + _fold8(dpre),)

        zero = jnp.zeros((8, CONV_CB), F32)
        acc = lax.fori_loop(0, n, step, (zero,) * 5)
        for j in range(4):
            dw_ref[j:j + 1, :] = jnp.sum(acc[j], axis=0, keepdims=True)
        db_ref[...] = jnp.sum(acc[4], axis=0, keepdims=True)

    col = pl.BlockSpec((t, CONV_CB), lambda j: (0, j))
    wsp = pl.BlockSpec((4, CONV_CB), lambda j: (0, j))
    bsp = pl.BlockSpec((1, CONV_CB), lambda j: (0, j))
    return pl.pallas_call(
        body, name="ssm_conv_bwd", grid=(D_XBC // CONV_CB,),
        in_specs=[_pspec(t, OFF_XBC), wsp, bsp, col], out_specs=[col, wsp, bsp],
        out_shape=[jax.ShapeDtypeStruct((t, D_XBC), BF), jax.ShapeDtypeStruct((4, D_XBC), F32),
                   jax.ShapeDtypeStruct((1, D_XBC), F32)],
        compiler_params=_params("parallel"))(p, conv_w, conv_b, dxc)


DT_ROWS = 512


def _tri(lower):
    r = lax.broadcasted_iota(jnp.int32, (CHUNK, CHUNK), 0)
    c = lax.broadcasted_iota(jnp.int32, (CHUNK, CHUNK), 1)
    return jnp.where((r >= c) if lower else (r <= c), 1.0, 0.0).astype(F32)


def _dot_exact(a, b):
    return lax.dot_general(a, b, _DIMS["nn"], preferred_element_type=F32, precision=lax.Precision.HIGHEST)


def _dt_fwd(p, bias_pad, alog_pad):
    t = p.shape[0]

    def body(raw_ref, b_ref, al_ref, dt_ref, acs_ref):
        z = raw_ref[...] + b_ref[...]
        dt = jnp.maximum(z, 0.0) + jnp.log(1.0 + jnp.exp(-jnp.abs(z)))
        dt_ref[...] = dt
        a = dt * (-jnp.exp(al_ref[...]))
        tri = _tri(True)
        for k in range(DT_ROWS // CHUNK):
            acs_ref[k * CHUNK:(k + 1) * CHUNK, :] = _dot_exact(tri, a[k * CHUNK:(k + 1) * CHUNK, :])

    blk = pl.BlockSpec((DT_ROWS, DT_W), lambda i: (i, 0))
    vec = pl.BlockSpec((1, DT_W), lambda i: (0, 0))
    return pl.pallas_call(
        body, name="dt_fwd", grid=(t // DT_ROWS,),
        in_specs=[pl.BlockSpec((DT_ROWS, DT_W), lambda i: (i, OFF_DT // DT_W)), vec, vec],
        out_specs=[blk, blk], out_shape=[jax.ShapeDtypeStruct((t, DT_W), F32)] * 2,
        compiler_params=_params("parallel"))(p, bias_pad, alog_pad)


def _dt_bwd(p, bias_pad, alog_pad, dt, ddt, dacs):
    t = p.shape[0]

    def body(raw_ref, b_ref, al_ref, dt_ref, ddt_ref, dacs_ref, draw_ref, db_ref, dal_ref):
        i = pl.program_id(0)
        acoef = -jnp.exp(al_ref[...])
        triu = _tri(False)
        das = []
        for k in range(DT_ROWS // CHUNK):
            das.append(_dot_exact(triu, dacs_ref[k * CHUNK:(k + 1) * CHUNK, :]))
        da = jnp.concatenate(das, axis=0)
        dtv = dt_ref[...]
        ddt_tot = ddt_ref[...] + da * acoef
        lane = lax.broadcasted_iota(jnp.int32, (DT_ROWS, DT_W), 1)
        draw = jnp.where(lane < N_HEADS, ddt_tot * _sigmoid(raw_ref[...] + b_ref[...]), 0.0)
        draw_ref[...] = draw.astype(BF)
        pb = jnp.sum(draw, axis=0, keepdims=True)
        pa = jnp.sum(da * dtv * acoef, axis=0, keepdims=True)

        @pl.when(i == 0)
        def _():
            db_ref[...] = pb
            dal_ref[...] = pa

        @pl.when(i > 0)
        def _():
            db_ref[...] += pb
            dal_ref[...] += pa

    blk = pl.BlockSpec((DT_ROWS, DT_W), lambda i: (i, 0))
    vec = pl.BlockSpec((1, DT_W), lambda i: (0, 0))
    return pl.pallas_call(
        body, name="dt_bwd", grid=(t // DT_ROWS,),
        in_specs=[pl.BlockSpec((DT_ROWS, DT_W), lambda i: (i, OFF_DT // DT_W)), vec, vec, blk, blk, blk],
        out_specs=[blk, vec, vec],
        out_shape=[jax.ShapeDtypeStruct((t, DT_W), BF), jax.ShapeDtypeStruct((1, DT_W), F32),
                   jax.ShapeDtypeStruct((1, DT_W), F32)],
        compiler_params=_params("arbitrary"))(p, bias_pad, alog_pad, dt, ddt, dacs)


def _split_dot(z, onehot, terms):
    out = None
    rest = z
    for _ in range(terms):
        piece = rest.astype(BF)
        part = _dot(piece, onehot)
        out = part if out is None else out + part
        rest = rest - piece.astype(F32)
    return out


def _spread_mat(g):
    row = lax.broadcasted_iota(jnp.int32, (DT_W, GROUP_W), 0)
    lane = lax.broadcasted_iota(jnp.int32, (DT_W, GROUP_W), 1)
    return jnp.where(row == HEADS_PER_GROUP * g + lane // HEAD_DIM, 1.0, 0.0).astype(BF)


def _gather_mat(g):
    row = lax.broadcasted_iota(jnp.int32, (GROUP_W, DT_W), 0)
    lane = lax.broadcasted_iota(jnp.int32, (GROUP_W, DT_W), 1)
    return jnp.where(lane == HEADS_PER_GROUP * g + row // HEAD_DIM, 1.0, 0.0).astype(BF)


def _ssd_masks():
    row = lax.broadcasted_iota(jnp.int32, (CHUNK, GROUP_W), 0)
    col = lax.broadcasted_iota(jnp.int32, (CHUNK, GROUP_W), 1) % HEAD_DIM
    brow = lax.broadcasted_iota(jnp.int32, (GROUP_W, GROUP_W), 0) // HEAD_DIM
    bcol = lax.broadcasted_iota(jnp.int32, (GROUP_W, GROUP_W), 1) // HEAD_DIM
    return row >= col, row == col, brow == bcol


def _stack4(v):
    return jnp.concatenate([v, v, v, v], axis=0)


def _fold4(v):
    return v[0:CHUNK] + v[CHUNK:2 * CHUNK] + v[2 * CHUNK:3 * CHUNK] + v[3 * CHUNK:4 * CHUNK]


def _ssd_group(xc_ref, stacked, g, tri, eye, blockdiag):
    gs = slice(GROUP_W * g, GROUP_W * (g + 1))
    xs_g = xc_ref[:, gs]
    b_g = xc_ref[:, D_INNER + D_STATE * g:D_INNER + D_STATE * (g + 1)].astype(BF)
    c_g = xc_ref[:, D_INNER + 1024 + D_STATE * g:D_INNER + 1024 + D_STATE * (g + 1)].astype(BF)
    wide = _split_dot(stacked, _spread_mat(g), 3)
    acs_e, dt_e = wide[0:CHUNK], wide[CHUNK:2 * CHUNK]
    atot_e = acs_e[CHUNK - 1:CHUNK, :]
    acs_j = jnp.sum(jnp.where(eye, acs_e, 0.0), axis=0, keepdims=True)
    lmat = jnp.where(tri, jnp.exp(jnp.minimum(acs_e - acs_j, 0.0)), 0.0)
    b_t = _stack4(b_g)
    m = _dot(c_g, b_t, "nt") * lmat
    x_g = xs_g * dt_e
    xbd = jnp.where(blockdiag, _stack4(x_g), 0.0).astype(BF)
    return dict(gs=gs, xs=xs_g, b=b_g, c=c_g, b_t=b_t, dt=dt_e, e=jnp.exp(acs_e), dec=jnp.exp(atot_e - acs_e),
                eat=jnp.exp(atot_e), lmat=lmat, m=m, x=x_g, xbd=xbd)


def _ssd_fwd(xconv, dt, acs, d_exp, comm=None):
    t = xconv.shape[0]
    nc = t // CHUNK

    def body(xc_ref, dt_ref, acs_ref, d_ref, y_ref, hs_ref, state):
        c = pl.program_id(0)

        @pl.when(c == 0)
        def _():
            state[...] = jnp.zeros_like(state)

        hs_ref[...] = state[...]
        tri, eye, blockdiag = _ssd_masks()
        stacked = jnp.concatenate([acs_ref[...], dt_ref[...]], axis=0)
        for g in range(N_GROUPS):
            q = _ssd_group(xc_ref, stacked, g, tri, eye, blockdiag)
            gs = q["gs"]
            h_t = state[:, gs]
            ydiag = _dot(q["m"].astype(BF), q["xbd"])
            yoff = _dot(q["c"], h_t.astype(BF)) * q["e"]
            y_ref[:, gs] = ydiag + yoff + d_ref[:, gs] * q["xs"]
            s_t = _dot(q["b"], (q["x"] * q["dec"]).astype(BF), "tn")
            state[:, gs] = q["eat"] * h_t + s_t

    blk = lambda w: pl.BlockSpec((CHUNK, w), lambda c: (c, 0))
    outs, couts = _pcall(
        "ssd_fwd", body, (nc,),
        [blk(D_XBC), blk(DT_W), blk(DT_W), pl.BlockSpec((1, D_INNER), lambda c: (0, 0))],
        [blk(D_INNER), pl.BlockSpec((None, D_STATE, D_INNER), lambda c: (c, 0, 0))],
        [jax.ShapeDtypeStruct((t, D_INNER), F32), jax.ShapeDtypeStruct((nc, D_STATE, D_INNER), F32)],
        (xconv, dt, acs, d_exp), [pltpu.VMEM((D_STATE, D_INNER), F32)], ("arbitrary",), comm)
    return outs if comm is None else (outs, couts)


def _ssd_bwd(xconv, dt, acs, d_exp, hsave, dy, comm=None):
    t = xconv.shape[0]
    nc = t // CHUNK

    def body(xc_ref, dt_ref, acs_ref, d_ref, hs_ref, dy_ref, dxc_ref, ddt_ref, dacs_ref, dd_ref, dstate):
        c = pl.program_id(0)

        @pl.when(c == 0)
        def _():
            dstate[...] = jnp.zeros_like(dstate)
            dd_ref[...] = jnp.zeros_like(dd_ref)

        tri, eye, blockdiag = _ssd_masks()
        acsv = acs_ref[...]
        stacked = jnp.concatenate([acsv, dt_ref[...]], axis=0)
        eat_heads = jnp.exp(acsv[CHUNK - 1:CHUNK, :])
        ddt_acc = jnp.zeros((CHUNK, DT_W), F32)
        dacs_acc = jnp.zeros((CHUNK, DT_W), F32)
        datot_acc = jnp.zeros((1, DT_W), F32)

        for g in range(N_GROUPS):
            q = _ssd_group(xc_ref, stacked, g, tri, eye, blockdiag)
            gs, xs_g, b_g, c_g, m = q["gs"], q["xs"], q["b"], q["c"], q["m"]
            bs = slice(D_INNER + D_STATE * g, D_INNER + D_STATE * (g + 1))
            cs = slice(D_INNER + 1024 + D_STATE * g, D_INNER + 1024 + D_STATE * (g + 1))
            h_t = hs_ref[:, gs]
            h_b = h_t.astype(BF)
            dy_g = dy_ref[:, gs]
            dy_b = dy_g.astype(BF)
            ds_t = dstate[:, gs]
            ds_b = ds_t.astype(BF)

            yoff = _dot(c_g, h_b) * q["e"]
            edy = (q["e"] * dy_g).astype(BF)
            d_c = _dot(edy, h_b, "nt")
            d_ht = _dot(c_g, edy, "tn")
            bds = _dot(b_g, ds_b)
            xd = q["x"] * q["dec"]
            d_b = _dot(xd.astype(BF), ds_b, "nt")
            dm = _dot(dy_b, q["xbd"], "nt")
            cross = _dot(m.astype(BF), dy_b, "tn")
            dx_full = q["dec"] * bds + _fold4(jnp.where(blockdiag, cross, 0.0))
            dml = (dm * q["lmat"]).astype(BF)
            d_c = d_c + _dot(dml, q["b_t"])
            d_b = d_b + _fold4(_dot(dml, c_g, "tn"))
            w = dm * m
            q_dec = xd * bds
            z = w - jnp.where(eye, jnp.sum(w, axis=0, keepdims=True), 0.0) + dy_g * yoff - q_dec
            rows = jnp.concatenate(
                [jnp.sum(q_dec, axis=0, keepdims=True), jnp.sum(ds_t * h_t, axis=0, keepdims=True),
                 jnp.zeros((6, GROUP_W), F32)], axis=0)
            seg = _split_dot(jnp.concatenate([z, dx_full * xs_g, rows], axis=0), _gather_mat(g), 2)
            dacs_acc = dacs_acc + seg[0:CHUNK]
            ddt_acc = ddt_acc + seg[CHUNK:2 * CHUNK]
            datot_acc = datot_acc + seg[2 * CHUNK:2 * CHUNK + 1] + eat_heads * seg[2 * CHUNK + 1:2 * CHUNK + 2]
            dxc_ref[:, cs] = d_c
            dxc_ref[:, bs] = d_b
            dxc_ref[:, gs] = dx_full * q["dt"] + d_ref[:, gs] * dy_g
            dd_ref[:, gs] += jnp.sum(dy_g * xs_g, axis=0, keepdims=True)
            dstate[:, gs] = q["eat"] * ds_t + d_ht

        rowi = lax.broadcasted_iota(jnp.int32, (CHUNK, DT_W), 0)
        ddt_ref[...] = ddt_acc
        dacs_ref[...] = dacs_acc + jnp.where(rowi == CHUNK - 1, datot_acc, 0.0)

    rev = lambda w: pl.BlockSpec((CHUNK, w), lambda c: (nc - 1 - c, 0))
    vec = pl.BlockSpec((1, D_INNER), lambda c: (0, 0))
    outs, couts = _pcall(
        "ssd_bwd", body, (nc,),
        [rev(D_XBC), rev(DT_W), rev(DT_W), vec,
         pl.BlockSpec((None, D_STATE, D_INNER), lambda c: (nc - 1 - c, 0, 0)), rev(D_INNER)],
        [rev(D_XBC), rev(DT_W), rev(DT_W), vec],
        [jax.ShapeDtypeStruct((t, D_XBC), F32), jax.ShapeDtypeStruct((t, DT_W), F32),
         jax.ShapeDtypeStruct((t, DT_W), F32), jax.ShapeDtypeStruct((1, D_INNER), F32)],
        (xconv, dt, acs, d_exp, hsave, dy),
        [pltpu.VMEM((D_STATE, D_INNER), F32)], ("arbitrary",), comm)
    return outs if comm is None else (outs, couts)


GN_CB = 1024
GN_GROUPS = GN_CB // GROUP_W


def _gnorm_fwd(y, p, w, comm=None):
    t = y.shape[0]
    zoff = OFF_Z // GN_CB

    def body(y_ref, z_ref, w_ref, o_ref):
        for g in range(GN_GROUPS):
            gs = slice(GROUP_W * g, GROUP_W * (g + 1))
            z = z_ref[:, gs].astype(F32)
            yf = y_ref[:, gs] * (z * _sigmoid(z))
            rstd = lax.rsqrt(jnp.mean(yf * yf, axis=-1, keepdims=True) + NORM_EPS)
            o_ref[:, gs] = (yf * rstd * w_ref[:, gs]).astype(BF)

    blk = pl.BlockSpec((TE, GN_CB), lambda i, j: (i, j))
    out, couts = _pcall(
        "gnorm_fwd", body, (t // TE, D_INNER // GN_CB),
        [blk, pl.BlockSpec((TE, GN_CB), lambda i, j: (i, zoff + j)), pl.BlockSpec((1, GN_CB), lambda i, j: (0, j))],
        blk, jax.ShapeDtypeStruct((t, D_INNER), BF), (y, p, w), (), ("parallel", "parallel"), comm)
    return out if comm is None else (out, couts)


def _gnorm_bwd(y, p, w, dyn, comm=None):
    t = y.shape[0]
    zoff = OFF_Z // GN_CB

    def body(y_ref, z_ref, w_ref, dn_ref, dy_ref, dz_ref, dw_ref):
        i = pl.program_id(1)
        for g in range(GN_GROUPS):
            gs = slice(GROUP_W * g, GROUP_W * (g + 1))
            z = z_ref[:, gs].astype(F32)
            yv = y_ref[:, gs]
            s = _sigmoid(z)
            sil = z * s
            yf = yv * sil
            rstd = lax.rsqrt(jnp.mean(yf * yf, axis=-1, keepdims=True) + NORM_EPS)
            xhat = yf * rstd
            dn = dn_ref[:, gs]
            wd = dn * w_ref[:, gs]
            proj = jnp.mean(wd * xhat, axis=-1, keepdims=True)
            dyf = rstd * (wd - xhat * proj)
            dy_ref[:, gs] = dyf * sil
            dz_ref[:, gs] = (dyf * yv * (s * (1.0 + z * (1.0 - s)))).astype(BF)
            part = jnp.sum(dn * xhat, axis=0, keepdims=True)

            @pl.when(i == 0)
            def _():
                dw_ref[:, gs] = part

            @pl.when(i > 0)
            def _():
                dw_ref[:, gs] += part

    blk = pl.BlockSpec((TE, GN_CB), lambda j, i: (i, j))
    vec = pl.BlockSpec((1, GN_CB), lambda j, i: (0, j))
    outs, couts = _pcall(
        "gnorm_bwd", body, (D_INNER // GN_CB, t // TE),
        [blk, pl.BlockSpec((TE, GN_CB), lambda j, i: (i, zoff + j)), vec, blk],
        [blk, blk, vec],
        [jax.ShapeDtypeStruct((t, D_INNER), F32), jax.ShapeDtypeStruct((t, D_INNER), BF),
         jax.ShapeDtypeStruct((1, D_INNER), F32)],
        (y, p, w, dyn), (), ("parallel", "arbitrary"), comm)
    return outs if comm is None else (outs, couts)


MERGE_CB = 512


def _merge_fwd(p, ya, yb):
    t = ya.shape[0]

    def body(ga_ref, gb_ref, ya_ref, yb_ref, o_ref):
        o_ref[...] = (_sigmoid(ga_ref[...]) * ya_ref[...] + _sigmoid(gb_ref[...]) * yb_ref[...]).astype(BF)

    blk = pl.BlockSpec((TE, MERGE_CB), lambda i, j: (i, j))
    return pl.pallas_call(
        body, name="merge_fwd", grid=(t // TE, D_MODEL // MERGE_CB),
        in_specs=[pl.BlockSpec((TE, MERGE_CB), lambda i, j: (i, OFF_GA // MERGE_CB + j)),
                  pl.BlockSpec((TE, MERGE_CB), lambda i, j: (i, OFF_GB // MERGE_CB + j)), blk, blk],
        out_specs=blk, out_shape=jax.ShapeDtypeStruct((t, D_MODEL), BF),
        compiler_params=_params("parallel", "parallel"))(p, p, ya, yb)


def _merge_bwd(p, ya, yb, dm):
    t = ya.shape[0]

    def body(ga_ref, gb_ref, ya_ref, yb_ref, dm_ref, dga_ref, dgb_ref, dya_ref, dyb_ref):
        d = dm_ref[...]
        sa = _sigmoid(ga_ref[...])
        sb = _sigmoid(gb_ref[...])
        dga_ref[...] = (d * ya_ref[...] * sa * (1.0 - sa)).astype(BF)
        dgb_ref[...] = (d * yb_ref[...] * sb * (1.0 - sb)).astype(BF)
        dya_ref[...] = (d * sa).astype(BF)
        dyb_ref[...] = (d * sb).astype(BF)

    blk = pl.BlockSpec((TE, MERGE_CB), lambda i, j: (i, j))
    return pl.pallas_call(
        body, name="merge_bwd", grid=(t // TE, D_MODEL // MERGE_CB),
        in_specs=[pl.BlockSpec((TE, MERGE_CB), lambda i, j: (i, OFF_GA // MERGE_CB + j)),
                  pl.BlockSpec((TE, MERGE_CB), lambda i, j: (i, OFF_GB // MERGE_CB + j)), blk, blk, blk],
        out_specs=[blk] * 4, out_shape=[jax.ShapeDtypeStruct((t, D_MODEL), BF)] * 4,
        compiler_params=_params("parallel", "parallel"))(p, p, ya, yb, dm)


def _adamw(name, parts, w, m, v, comm=None):
    r, c = w.shape
    tr = _row_tile(r)
    tc = ADAM_COL_TILE if (tr == r and r > 512 and c % ADAM_COL_TILE == 0) else c
    n_parts = parts.shape[0]
    bc1 = 1.0 - ADAM_B1 ** ADAM_STEP
    bc2 = 1.0 - ADAM_B2 ** ADAM_STEP

    def body(p_ref, w_ref, m_ref, v_ref, g_ref, d_ref, nm_ref, nv_ref):
        g = p_ref[0].astype(F32)
        for k in range(1, n_parts):
            g = g + p_ref[k].astype(F32)
        nm = ADAM_B1 * m_ref[...] + (1.0 - ADAM_B1) * g
        nv = ADAM_B2 * v_ref[...] + (1.0 - ADAM_B2) * (g * g)
        g_ref[...] = g
        nm_ref[...] = nm
        nv_ref[...] = nv
        d_ref[...] = -ADAM_LR * ((nm / bc1) / (jnp.sqrt(nv / bc2) + ADAM_EPS) + ADAM_WD * w_ref[...])

    blk = pl.BlockSpec((tr, tc), lambda i, j: (i, j))
    outs, couts = _pcall(
        name, body, (r // tr, c // tc),
        [pl.BlockSpec((n_parts, tr, tc), lambda i, j: (0, i, j)), blk, blk, blk], [blk] * 4,
        [jax.ShapeDtypeStruct((r, c), F32)] * 4, (parts, w, m, v), (), ("parallel", "parallel"), comm)
    return outs if comm is None else (outs, couts)


def _pad_lanes(v, width):
    return jnp.pad(v, ((0, 0), (0, width - v.shape[1])))


def _reduce_start(slots, host):
    outs, sib = host(_pair_comm([a for _, a in slots]))
    sums = [(n, _add_pairs("pairsum_" + n, a, b)) for (n, a), b in zip(slots, sib)]
    return outs, sums


def _train_step(x, target, shard, rep):
    gdt = BF
    t = x.shape[0]
    recv = {}
    (got,) = _comm_call("gather_ffn1_in", _gather_comm([shard["ffn1_w_in"]], [True]))
    w1_in = got.reshape(2 * D_FF, D_MODEL)
    h1 = _rms_fwd("rms1_fwd", x, rep["ffn1_norm"])
    w_in_rows = [shard["w_in"][a:b] for a, b in W_IN_ROW_CUTS]
    gu1, got = _mm_nt("ffn1_in", h1, w1_in, tn=FF_HALF, out_dtype=BF, comm=_gather_comm([shard["ffn1_w_out"], w_in_rows[0]]))
    w1_out = got[0].reshape(D_FF, D_MODEL)
    w_in_got = [got[1]]
    act1, got = _swiglu_fwd("swiglu1_fwd", gu1, comm=_gather_comm([w_in_rows[1]]))
    w_in_got.append(got[0])
    x1, got = _mm_nn("ffn1_out", act1, w1_out, res=x, alpha=0.5, comm=_gather_comm(
        [w_in_rows[2], shard["short_conv_w"], shard["ssm_conv_w"]]))
    w_in_got.append(got[0])
    short_conv_w = got[1].transpose(1, 0, 2).reshape(3, D_MODEL)
    ssm_conv_w = got[2].transpose(1, 0, 2).reshape(4, D_XBC)
    w_in_t = jnp.concatenate(w_in_got, axis=1).reshape(N_IN, D_MODEL)
    w_gd = jnp.concatenate([w_in_t[N_MAIN + N_HEADS:], w_in_t[N_MAIN:N_MAIN + N_HEADS],
                            jnp.zeros((DT_W - N_HEADS, D_MODEL), BF)], axis=0)

    h2 = _rms_fwd("rms2_fwd", x1, rep["mix_norm"])
    p, got = _mm_nt("proj_main", h2, w_in_t, n=N_MAIN, tn=1024, out_dtype=BF, comm=_gather_comm(
        [shard["short_w_out"], shard["ssm_w_out"], shard["w_out"]]))
    p_gd = _mm_nt("proj_gd", h2, w_gd)
    short_w_out = got[0].reshape(D_MODEL, D_MODEL)
    ssm_w_out = got[1].reshape(D_INNER, D_MODEL)
    w_out = got[2].reshape(D_MODEL, D_MODEL)
    ya_in = _mix_a_fwd(p, short_conv_w)
    y_a = _mm_nn("short_out", ya_in, short_w_out)
    xconv = _ssm_conv_fwd(p, ssm_conv_w, rep["ssm_conv_b"])
    dt, acs = _dt_fwd(p_gd, rep["dt_bias_pad"], rep["a_log_pad"])
    (y_ssm, hsave), (got,) = _ssd_fwd(xconv, dt, acs, rep["d_exp"], comm=_gather_comm([shard["ffn2_w_in"]], [True]))
    w2_in = got.reshape(2 * D_FF, D_MODEL)
    yn, got = _gnorm_fwd(y_ssm, p, rep["ssm_norm"], comm=_gather_comm([shard["ffn2_w_out"]]))
    w2_out = got[0].reshape(D_FF, D_MODEL)
    y_b = _mm_nn("ssm_out", yn, ssm_w_out, tk=1024)
    merged = _merge_fwd(p_gd, y_a, y_b)
    x2 = _mm_nn("mix_out", merged, w_out, res=x1)

    h3 = _rms_fwd("rms3_fwd", x2, rep["ffn2_norm"])
    gu2 = _mm_nt("ffn2_in", h3, w2_in, tn=FF_HALF, out_dtype=BF)
    act2 = _swiglu_fwd("swiglu2_fwd", gu2)
    x3 = _mm_nn("ffn2_out", act2, w2_out, res=x2, alpha=0.5)

    loss, dx3, dx3h, g_final = _final_loss(x3, rep["final_norm"], target)

    small = {"final_norm": g_final}
    dact2 = _mm_nt("ffn2_out_bwd_act", dx3h, w2_out, out_dtype=BF)
    g_w2_out = _mm_tn("ffn2_out_bwd_w", act2, dx3h, gdt, tm=FF_HALF)
    dgu2 = _swiglu_bwd("swiglu2_bwd", gu2, dact2)
    g_w2_in = _mm_tn("ffn2_in_bwd_w", dgu2, h3, gdt, tm=FF_HALF)
    dh3 = _mm_nn("ffn2_in_bwd_h", dgu2, w2_in, tk=FF_HALF)
    dx2, dx2b, small["ffn2_norm"] = _rms_bwd("rms3_bwd", x2, rep["ffn2_norm"], dh3, dx3, 1.0)

    dmerged = _mm_nt("mix_out_bwd_x", dx2b, w_out)
    g_w_out = _mm_tn("mix_out_bwd_w", merged, dx2b, gdt)
    dga, dgb, dya, dyb = _merge_bwd(p_gd, y_a, y_b, dmerged)

    dya_in = _mm_nt("short_out_bwd_x", dya, short_w_out)
    g_short_w_out = _mm_tn("short_out_bwd_w", ya_in, dya, gdt)
    db, dc, dxa, g_short_conv = _mix_a_bwd(p, short_conv_w, dya_in)

    dyn = _mm_nt("ssm_out_bwd_x", dyb, ssm_w_out)
    g_ssm_w_out = _mm_tn("ssm_out_bwd_w", yn, dyb, gdt)
    late = [("ffn2_w_out", g_w2_out.reshape(N_DEV, FF_SHARD // 2, D_MODEL)),
            ("ffn2_w_in", g_w2_in.reshape(N_DEV, FF_SHARD, D_MODEL)),
            ("w_out", g_w_out.reshape(N_DEV, -1, D_MODEL)), ("short_w_out", g_short_w_out.reshape(N_DEV, -1, D_MODEL)),
            ("ssm_w_out", g_ssm_w_out.reshape(N_DEV, -1, D_MODEL))]
    (dy_ssm, dz, small["ssm_norm"]), sums = _reduce_start(
        late, lambda comm: _gnorm_bwd(y_ssm, p, rep["ssm_norm"], dyn, comm=comm))
    (dxconv, ddt, dacs, dd_lane), got = _ssd_bwd(
        xconv, dt, acs, rep["d_exp"], hsave, dy_ssm,
        comm=_chip_comm([a for _, a in sums], [n == "ffn2_w_in" for n, _ in sums]))
    recv.update({n: a for (n, _), a in zip(sums, got)})
    small["ssm_D"] = dd_lane.reshape(N_HEADS, HEAD_DIM).sum(axis=1)[None, :]
    dxbc, g_ssm_conv, small["ssm_conv_b"] = _ssm_conv_bwd(p, ssm_conv_w, rep["ssm_conv_b"], dxconv)
    draw, dbias, dalog = _dt_bwd(p_gd, rep["dt_bias_pad"], rep["a_log_pad"], dt, ddt, dacs)
    small["ssm_dt_bias"] = dbias[:, :N_HEADS]
    small["ssm_A_log"] = dalog[:, :N_HEADS]

    dp = jnp.concatenate([db, dc, dxa, dz, dxbc], axis=1)
    dp_gd = jnp.concatenate([dga, dgb, draw], axis=1)
    g_main = _mm_tn("proj_main_bwd_w", dp, h2, gdt, tm=1024)
    g_gd = _mm_tn("proj_gd_bwd_w", dp_gd, h2, gdt)
    g_in_t = jnp.concatenate([g_main, g_gd[2048:2048 + N_HEADS], g_gd[0:2048]], axis=0).reshape(
        N_DEV, IN_SHARD, D_MODEL)
    w_rows = [("w_in%d" % i, g_in_t[:, a:b]) for i, (a, b) in enumerate(W_IN_ROW_CUTS)]
    dh2, w_sums = _reduce_start(w_rows, lambda comm: _mm_nn("proj_main_bwd_x", dp, w_in_t, tk=1024, comm=comm))
    dh2 = _mm_nn("proj_gd_bwd_x", dp_gd, w_gd, res=dh2)
    (dx1, dx1h, small["mix_norm"]), got0 = _rms_bwd("rms2_bwd", x1, rep["mix_norm"], dh2, dx2, 0.5,
                                                      comm=_chip_comm([w_sums[0][1]]))

    g_w1_out, got1 = _mm_tn("ffn1_out_bwd_w", act1, dx1h, gdt, tm=FF_HALF, comm=_chip_comm([w_sums[1][1]]))
    rest = [("ffn1_w_out", g_w1_out.reshape(N_DEV, FF_SHARD // 2, D_MODEL)),
            ("short_conv_w", g_short_conv.reshape(3, N_DEV, -1).transpose(1, 0, 2)),
            ("ssm_conv_w", g_ssm_conv.reshape(4, N_DEV, -1).transpose(1, 0, 2))]
    pair_rest = _pair_comm([a for _, a in rest])
    dact1, got = _mm_nt("ffn1_out_bwd_act", dx1h, w1_out, out_dtype=BF, comm=_join_comm(_chip_comm([w_sums[2][1]]), pair_rest))
    got2, sib = got[0], got[1:]
    rest_sums = [(n, _add_pairs("pairsum_" + n, a, b)) for (n, a), b in zip(rest, sib)]
    recv["w_in"] = jnp.concatenate([got0[0], got1[0], got2], axis=1)
    dgu1, got = _swiglu_bwd("swiglu1_bwd", gu1, dact1, comm=_chip_comm([a for _, a in rest_sums]))
    recv.update({n: a for (n, _), a in zip(rest_sums, got)})
    g_w1_in = _mm_tn("ffn1_in_bwd_w", dgu1, h1, gdt, tm=FF_HALF)
    g_last = g_w1_in.reshape(N_DEV, FF_SHARD, D_MODEL)
    half = FF_SHARD // 2
    dh1, last_sums = _reduce_start(
        [("ffn1_w_in_a", g_last[:, :half]), ("ffn1_w_in_b", g_last[:, half:])],
        lambda comm: _mm_nn("ffn1_in_bwd_h", dgu1, w1_in, tk=FF_HALF, comm=comm))
    (dx0, _, small["ffn1_norm"]), got = _rms_bwd("rms1_bwd", x, rep["ffn1_norm"], dh1, dx1, 1.0,
                                                  comm=_chip_comm([last_sums[0][1]], [True]))
    pending = {"ffn1_w_in_a": got[0], "ffn1_w_in_b_sums": last_sums[1][1], "packed": _pack_small(small, loss[:, 0:1])}
    return dx0, recv, pending


_SMALL = [("ffn1_norm", 1024), ("mix_norm", 1024), ("ssm_conv_b", 4096), ("ssm_dt_bias", 32), ("ssm_A_log", 32),
          ("ssm_D", 32), ("ssm_norm", 2048), ("ffn2_norm", 1024), ("final_norm", 1024)]
SMALL_W = 10368


def _pack_small(d, loss=None):
    parts = [d[n].reshape(1, -1).astype(F32) for n, _ in _SMALL]
    used = sum(sz for _, sz in _SMALL)
    tail = jnp.zeros((1, SMALL_W - used), F32)
    if loss is not None:
        tail = tail.at[:, 0:1].set(loss)
    return jnp.concatenate(parts + [tail], axis=1)


def _unpack_small(v, shapes):
    out, off = {}, 0
    for n, sz in _SMALL:
        out[n] = v[:, off:off + sz].reshape(shapes[n])
        off += sz
    return out, v[0, off]


_SHARDED = ["ffn1_w_in", "ffn1_w_out", "w_in", "short_conv_w", "short_w_out", "ssm_conv_w", "ssm_w_out", "w_out",
            "ffn2_w_in", "ffn2_w_out"]
_TRANSPOSED = ("ffn1_w_in", "w_in", "ffn2_w_in")
_ORDER = ["ffn1_norm", "ffn1_w_in", "ffn1_w_out", "mix_norm", "w_in", "short_conv_w", "short_w_out", "ssm_conv_w",
          "ssm_conv_b", "ssm_dt_bias", "ssm_A_log", "ssm_D", "ssm_norm", "ssm_w_out", "w_out", "ffn2_norm",
          "ffn2_w_in", "ffn2_w_out", "final_norm"]


def kernel(x, ffn1_norm, ffn1_w_in, ffn1_w_out, mix_norm, w_in, short_conv_w, short_w_out, ssm_conv_w, ssm_conv_b, ssm_dt_bias, ssm_A_log, ssm_D, ssm_norm, ssm_w_out, w_out, ffn2_norm, ffn2_w_in, ffn2_w_out, final_norm, loss_target, m_ffn1_norm, m_ffn1_w_in, m_ffn1_w_out, m_mix_norm, m_w_in, m_short_conv_w, m_short_w_out, m_ssm_conv_w, m_ssm_conv_b, m_ssm_dt_bias, m_ssm_A_log, m_ssm_D, m_ssm_norm, m_ssm_w_out, m_w_out, m_ffn2_norm, m_ffn2_w_in, m_ffn2_w_out, m_final_norm, v_ffn1_norm, v_ffn1_w_in, v_ffn1_w_out, v_mix_norm, v_w_in, v_short_conv_w, v_short_w_out, v_ssm_conv_w, v_ssm_conv_b, v_ssm_dt_bias, v_ssm_A_log, v_ssm_D, v_ssm_norm, v_ssm_w_out, v_w_out, v_ffn2_norm, v_ffn2_w_in, v_ffn2_w_out, v_final_norm):
    w = dict(ffn1_norm=ffn1_norm, ffn1_w_in=ffn1_w_in, ffn1_w_out=ffn1_w_out, mix_norm=mix_norm, w_in=w_in,
             short_conv_w=short_conv_w, short_w_out=short_w_out, ssm_conv_w=ssm_conv_w, ssm_conv_b=ssm_conv_b,
             ssm_dt_bias=ssm_dt_bias, ssm_A_log=ssm_A_log, ssm_D=ssm_D, ssm_norm=ssm_norm, ssm_w_out=ssm_w_out,
             w_out=w_out, ffn2_norm=ffn2_norm, ffn2_w_in=ffn2_w_in, ffn2_w_out=ffn2_w_out, final_norm=final_norm)
    m = dict(ffn1_norm=m_ffn1_norm, ffn1_w_in=m_ffn1_w_in, ffn1_w_out=m_ffn1_w_out, mix_norm=m_mix_norm, w_in=m_w_in,
             short_conv_w=m_short_conv_w, short_w_out=m_short_w_out, ssm_conv_w=m_ssm_conv_w,
             ssm_conv_b=m_ssm_conv_b, ssm_dt_bias=m_ssm_dt_bias, ssm_A_log=m_ssm_A_log, ssm_D=m_ssm_D,
             ssm_norm=m_ssm_norm, ssm_w_out=m_ssm_w_out, w_out=m_w_out, ffn2_norm=m_ffn2_norm,
             ffn2_w_in=m_ffn2_w_in, ffn2_w_out=m_ffn2_w_out, final_norm=m_final_norm)
    v = dict(ffn1_norm=v_ffn1_norm, ffn1_w_in=v_ffn1_w_in, ffn1_w_out=v_ffn1_w_out, mix_norm=v_mix_norm, w_in=v_w_in,
             short_conv_w=v_short_conv_w, short_w_out=v_short_w_out, ssm_conv_w=v_ssm_conv_w,
             ssm_conv_b=v_ssm_conv_b, ssm_dt_bias=v_ssm_dt_bias, ssm_A_log=v_ssm_A_log, ssm_D=v_ssm_D,
             ssm_norm=v_ssm_norm, ssm_w_out=v_ssm_w_out, w_out=v_w_out, ffn2_norm=v_ffn2_norm,
             ffn2_w_in=v_ffn2_w_in, ffn2_w_out=v_ffn2_w_out, final_norm=v_final_norm)
    shapes = {n: w[n].shape for n in _ORDER}

    def local(d, n):
        return d[n][0].T if n in _TRANSPOSED else d[n][0]

    shard = {n: local(w, n) for n in _SHARDED}

    wire = {n: (shard[n] if n in ("short_conv_w", "ssm_conv_w") else shard[n].astype(BF)) for n in _SHARDED}
    rep = {
        "ffn1_norm": ffn1_norm, "mix_norm": mix_norm, "ffn2_norm": ffn2_norm, "ssm_norm": ssm_norm,
        "ssm_conv_b": ssm_conv_b, "final_norm": final_norm.reshape(1, D_MODEL),
        "dt_bias_pad": _pad_lanes(ssm_dt_bias, DT_W), "a_log_pad": _pad_lanes(ssm_A_log, DT_W),
        "d_exp": jnp.repeat(ssm_D, HEAD_DIM, axis=1),
    }
    grad_x, parts, pending = _train_step(x[0], loss_target[0], wire, rep)

    hosted = {"w_in": _chip_comm([pending["ffn1_w_in_b_sums"]], [True]), "ffn2_w_in": _gather_comm([pending["packed"]])}
    order = ["w_in", "ffn2_w_in"] + [n for n in _SHARDED if n not in ("w_in", "ffn2_w_in")]
    out_g, out_d, out_m, out_v = {}, {}, {}, {}
    for n in order:
        if n in hosted:
            res, got = _adamw("adamw_" + n, parts[n], shard[n], local(m, n), local(v, n), comm=hosted[n])
            if n == "w_in":
                parts["ffn1_w_in"] = jnp.concatenate([pending["ffn1_w_in_a"], got[0]], axis=1)
            else:
                small_parts = got[0]
        else:
            res = _adamw("adamw_" + n, parts[n], shard[n], local(m, n), local(v, n))
        out_g[n], out_d[n], out_m[n], out_v[n] = [(r.T if n in _TRANSPOSED else r).reshape(shapes[n]) for r in res]
    sres = _adamw("adamw_small", small_parts, _pack_small(w), _pack_small(m), _pack_small(v))
    sg, loss = _unpack_small(sres[0], shapes)
    sd, _ = _unpack_small(sres[1], shapes)
    sm, _ = _unpack_small(sres[2], shapes)
    sv, _ = _unpack_small(sres[3], shapes)
    out_g.update(sg)
    out_d.update(sd)
    out_m.update(sm)
    out_v.update(sv)
    return (loss, grad_x[None], *[out_g[n] for n in _ORDER], *[out_d[n] for n in _ORDER],
            *[out_m[n] for n in _ORDER], *[out_v[n] for n in _ORDER])
```

```python
import functools

import jax
import jax.numpy as jnp
from jax import lax
from jax.experimental import pallas as pl
from jax.experimental.pallas import tpu as pltpu

F32 = jnp.float32
BF = jnp.bfloat16

N_DEV = 8
D_MODEL = 1024
D_FF = 2816
D_INNER = 2048
D_XBC = 4096
N_HEADS = 32
HEAD_DIM = 64
N_GROUPS = 8
D_STATE = 128
CHUNK = 64
GROUP_W = D_INNER // N_GROUPS
HEADS_PER_GROUP = N_HEADS // N_GROUPS
NORM_EPS = 1e-5
N_IN = 11296
FF_SHARD = 2 * D_FF // N_DEV
FF_HALF = D_FF // 2
IN_SHARD = N_IN // N_DEV

OFF_B, OFF_C, OFF_XA, OFF_Z, OFF_XBC = 0, 1024, 2048, 3072, 5120
N_MAIN = 9216
OFF_GA, OFF_GB, OFF_DT = 0, 1024, 2048
DT_W = 128
N_GD = 2048 + DT_W
W_IN_ROW_CUTS = [(0, 480), (480, 944), (944, 1412)]

ADAM_LR, ADAM_B1, ADAM_B2, ADAM_EPS, ADAM_WD, ADAM_STEP = 0.001, 0.9, 0.999, 1e-08, 0.01, 10

VMEM_LIMIT_V7X = 56 * 1024 * 1024
TM = 1024
TE = 512
ADAM_COL_TILE = 256


def _params(*sem):
    return pltpu.CompilerParams(dimension_semantics=sem, vmem_limit_bytes=VMEM_LIMIT_V7X)


_DIMS = {
    "nn": (((1,), (0,)), ((), ())),
    "nt": (((1,), (1,)), ((), ())),
    "tn": (((0,), (0,)), ((), ())),
}


def _dot(a, b, mode="nn"):
    return lax.dot_general(a, b, _DIMS[mode], preferred_element_type=F32)


def _sigmoid(x):
    return 1.0 / (1.0 + jnp.exp(-x))


class _Comm:
    def __init__(self, inputs, out_shapes, sems, start, finish):
        self.inputs, self.out_shapes, self.sems, self.start, self.finish = inputs, out_shapes, sems, start, finish


def _pcall(name, body, grid, in_specs, out_specs, out_shape, args, scratch=(), sem=None, comm=None):
    single = not isinstance(out_shape, (list, tuple))
    out_shapes = [out_shape] if single else list(out_shape)
    out_specs = [out_specs] if single else list(out_specs)
    n_in, n_out, n_scr = len(args), len(out_shapes), len(scratch)
    if comm is None:
        res = pl.pallas_call(
            body, name=name, grid=grid, in_specs=list(in_specs), out_specs=out_specs, out_shape=out_shapes,
            scratch_shapes=list(scratch), compiler_params=_params(*sem))(*args)
        return (res[0] if single else res), []
    nci, nco = len(comm.inputs), len(comm.out_shapes)

    def wrapped(*refs):
        a = refs[:n_in]
        ci = refs[n_in:n_in + nci]
        o0 = n_in + nci
        o = refs[o0:o0 + n_out]
        co = refs[o0 + n_out:o0 + n_out + nco]
        s0 = o0 + n_out + nco
        s = refs[s0:s0 + n_scr]
        cs = refs[s0 + n_scr:]
        pids = [pl.program_id(i) for i in range(len(grid))]
        first = functools.reduce(jnp.logical_and, [p == 0 for p in pids])
        last = functools.reduce(jnp.logical_and, [p == g - 1 for p, g in zip(pids, grid)])

        @pl.when(first)
        def _():
            comm.start(ci, co, cs)

        body(*a, *o, *s)

        @pl.when(last)
        def _():
            comm.finish(ci, co, cs)

    any_spec = pl.BlockSpec(memory_space=pl.ANY)
    res = pl.pallas_call(
        wrapped, name=name, grid=grid, in_specs=list(in_specs) + [any_spec] * nci,
        out_specs=out_specs + [any_spec] * nco, out_shape=out_shapes + list(comm.out_shapes),
        scratch_shapes=list(scratch) + list(comm.sems),
        compiler_params=_params(*(("arbitrary",) * len(grid))))(*args, *comm.inputs)
    core = res[:n_out]
    return (core[0] if single else core), list(res[n_out:])


def _comm_call(name, comm):
    nci, nco = len(comm.inputs), len(comm.out_shapes)

    def body(*refs):
        ci, co, cs = refs[:nci], refs[nci:nci + nco], refs[nci + nco:]
        comm.start(ci, co, cs)
        comm.finish(ci, co, cs)

    any_spec = pl.BlockSpec(memory_space=pl.ANY)
    return pl.pallas_call(
        body, name=name, in_specs=[any_spec] * nci, out_specs=[any_spec] * nco, out_shape=list(comm.out_shapes),
        scratch_shapes=list(comm.sems), compiler_params=pltpu.CompilerParams(has_side_effects=True))(*comm.inputs)


def _remote(src, dst, ssem, rsem, dev):
    return pltpu.make_async_remote_copy(src_ref=src, dst_ref=dst, send_sem=ssem, recv_sem=rsem, device_id=dev,
                                        device_id_type=pl.DeviceIdType.MESH)


def _place():
    x, y, c = lax.axis_index("x"), lax.axis_index("y"), lax.axis_index("c")
    other_chips = [(1 - x, y), (x, 1 - y), (1 - x, 1 - y)]
    return x, y, c, other_chips


def _slot(x, y, c, swap):
    return 4 * y + 2 * x + c if swap else 4 * x + 2 * y + c


def _chip_slot(x, y, swap):
    return 2 * y + x if swap else 2 * x + y


def _gather_comm(shards, swaps=None):
    n = len(shards)
    per = N_DEV - 1
    swaps = [False] * n if swaps is None else swaps

    def start(ins, outs, sems):
        send, recv, loc = sems
        x, y, c, chips = _place()
        for i in range(n):
            me = _slot(x, y, c, swaps[i])
            pltpu.make_async_copy(ins[i], outs[i].at[me], loc.at[i]).start()
            _remote(ins[i], outs[i].at[me], send.at[per * i], recv.at[per * i], (x, y, 1 - c)).start()
            for j, (qx, qy) in enumerate(chips):
                _remote(ins[i], outs[i].at[me], send.at[per * i + 1 + j], recv.at[per * i + 1 + j], (qx, qy, c)).start()

    def finish(ins, outs, sems):
        send, recv, loc = sems
        x, y, c, chips = _place()
        sib = (x, y, 1 - c)
        for i in range(n):
            for j, (qx, qy) in enumerate(chips):
                blk = outs[i].at[_slot(qx, qy, c, swaps[i])]
                _remote(blk, blk, send.at[per * i + 1 + j], recv.at[per * i + 1 + j], (qx, qy, c)).wait_recv()
                _remote(blk, blk, send.at[per * i + 4 + j], recv.at[per * i + 4 + j], sib).start()
        for i in range(n):
            blk = outs[i].at[_slot(x, y, 1 - c, swaps[i])]
            _remote(blk, blk, send.at[per * i], recv.at[per * i], sib).wait_recv()
            for j, (qx, qy) in enumerate(chips):
                blk = outs[i].at[_slot(qx, qy, 1 - c, swaps[i])]
                _remote(blk, blk, send.at[per * i + 4 + j], recv.at[per * i + 4 + j], sib).wait_recv()
        for i in range(n):
            own = outs[i].at[_slot(x, y, c, swaps[i])]
            for k in range(per):
                _remote(ins[i], own, send.at[per * i + k], recv.at[per * i + k], sib).wait_send()
            pltpu.make_async_copy(ins[i], own, loc.at[i]).wait()

    out_shapes = [jax.ShapeDtypeStruct((N_DEV,) + tuple(a.shape), a.dtype) for a in shards]
    sems = [pltpu.SemaphoreType.DMA((per * n,)), pltpu.SemaphoreType.DMA((per * n,)), pltpu.SemaphoreType.DMA((n,))]
    return _Comm(list(shards), out_shapes, sems, start, finish)


def _pair_comm(slots):
    n = len(slots)

    def copies(ins, outs, sems):
        send, recv = sems
        x, y, c, _ = _place()
        sib = (x, y, 1 - c)
        out = []
        for i in range(n):
            for q in range(4):
                out.append(_remote(ins[i].at[2 * q + 1 - c], outs[i].at[q], send.at[4 * i + q], recv.at[4 * i + q], sib))
        return out

    def start(ins, outs, sems):
        for cp in copies(ins, outs, sems):
            cp.start()

    def finish(ins, outs, sems):
        for cp in copies(ins, outs, sems):
            cp.wait_send()
            cp.wait_recv()

    out_shapes = [jax.ShapeDtypeStruct((4,) + tuple(a.shape[1:]), a.dtype) for a in slots]
    sems = [pltpu.SemaphoreType.DMA((4 * n,)), pltpu.SemaphoreType.DMA((4 * n,))]
    return _Comm(list(slots), out_shapes, sems, start, finish)


def _chip_comm(chip_sums, swaps=None, rows=None):
    n = len(chip_sums)
    swaps = [False] * n if swaps is None else swaps
    rows = [None] * n if rows is None else rows

    def src(ins, i, q):
        return ins[i].at[q] if rows[i] is None else ins[i].at[q, pl.ds(rows[i][0], rows[i][1] - rows[i][0])]

    def start(ins, outs, sems):
        send, recv, loc = sems
        x, y, c, chips = _place()
        for i in range(n):
            mine = _chip_slot(x, y, swaps[i])
            pltpu.make_async_copy(src(ins, i, mine), outs[i].at[mine], loc.at[i]).start()
            for j, (qx, qy) in enumerate(chips):
                _remote(src(ins, i, _chip_slot(qx, qy, swaps[i])), outs[i].at[mine], send.at[3 * i + j],
                        recv.at[3 * i + j], (qx, qy, c)).start()

    def finish(ins, outs, sems):
        send, recv, loc = sems
        x, y, c, chips = _place()
        for i in range(n):
            mine = _chip_slot(x, y, swaps[i])
            for j, (qx, qy) in enumerate(chips):
                theirs = _chip_slot(qx, qy, swaps[i])
                cp = _remote(src(ins, i, theirs), outs[i].at[theirs], send.at[3 * i + j], recv.at[3 * i + j], (qx, qy, c))
                cp.wait_send()
                cp.wait_recv()
            pltpu.make_async_copy(src(ins, i, mine), outs[i].at[mine], loc.at[i]).wait()

    def out_shape(a, r):
        shape = a.shape if r is None else (a.shape[0], r[1] - r[0]) + tuple(a.shape[2:])
        return jax.ShapeDtypeStruct(shape, a.dtype)

    out_shapes = [out_shape(a, r) for a, r in zip(chip_sums, rows)]
    sems = [pltpu.SemaphoreType.DMA((3 * n,)), pltpu.SemaphoreType.DMA((3 * n,)), pltpu.SemaphoreType.DMA((n,))]
    return _Comm(list(chip_sums), out_shapes, sems, start, finish)


def _join_comm(a, b):
    na_i, na_o, na_s = len(a.inputs), len(a.out_shapes), len(a.sems)

    def start(ins, outs, sems):
        a.start(ins[:na_i], outs[:na_o], sems[:na_s])
        b.start(ins[na_i:], outs[na_o:], sems[na_s:])

    def finish(ins, outs, sems):
        a.finish(ins[:na_i], outs[:na_o], sems[:na_s])
        b.finish(ins[na_i:], outs[na_o:], sems[na_s:])

    return _Comm(a.inputs + b.inputs, a.out_shapes + b.out_shapes, a.sems + b.sems, start, finish)


def _row_tile(r):
    for cand in (256, 128):
        if r > cand and r % cand == 0:
            return cand
    return r


def _add_pairs(name, slots, sib):
    r, c = slots.shape[1:]
    tr = _row_tile(r)

    def body(core_ref, s_ref, b_ref, o_ref):
        o_ref[...] = (s_ref[...].astype(F32) + b_ref[...].astype(F32)).astype(o_ref.dtype)

    core = jnp.full((1,), lax.axis_index("c"), jnp.int32)
    return pl.pallas_call(
        body, name=name,
        grid_spec=pltpu.PrefetchScalarGridSpec(
            num_scalar_prefetch=1, grid=(4, r // tr),
            in_specs=[pl.BlockSpec((None, None, tr, c), lambda q, i, core_ref: (q, core_ref[0], i, 0)),
                      pl.BlockSpec((None, tr, c), lambda q, i, core_ref: (q, i, 0))],
            out_specs=pl.BlockSpec((None, tr, c), lambda q, i, core_ref: (q, i, 0))),
        out_shape=jax.ShapeDtypeStruct((4, r, c), slots.dtype),
        compiler_params=_params("parallel", "parallel"))(core, slots.reshape(4, 2, r, c), sib)


def _matmul(name, mode, a, b, grid, a_spec, b_spec, o_spec, out_shape, acc_shape,
            res=None, res_spec=None, alpha=1.0, comm=None):
    nk = grid[-1]
    has_res = res is not None

    def body(*refs):
        if has_res:
            a_ref, b_ref, r_ref, o_ref = refs[:4]
        else:
            a_ref, b_ref, o_ref = refs[:3]
            r_ref = None
        part = _dot(a_ref[...], b_ref[...], mode)

        def finish(v):
            if alpha != 1.0:
                v = v * alpha
            if has_res:
                v = r_ref[...] + v
            o_ref[...] = v.astype(o_ref.dtype)

        if nk == 1:
            finish(part)
        else:
            acc = refs[-1]
            k = pl.program_id(len(grid) - 1)

            @pl.when(k == 0)
            def _():
                acc[...] = part

            @pl.when(k > 0)
            def _():
                acc[...] += part

            @pl.when(k == nk - 1)
            def _():
                finish(acc[...])

    in_specs = [a_spec, b_spec] + ([res_spec] if has_res else [])
    args = (a, b) + ((res,) if has_res else ())
    scratch = [] if nk == 1 else [pltpu.VMEM(acc_shape, F32)]
    sem = ("parallel",) * (len(grid) - 1) + ("arbitrary",)
    out, couts = _pcall(name, body, grid, in_specs, o_spec, out_shape, args, scratch, sem, comm)
    return out if comm is None else (out, couts)


def _mm_nn(name, a, b, out_dtype=F32, res=None, alpha=1.0, tk=None, kk=None, a_off=0, b_off=0, comm=None):
    t = a.shape[0]
    kk = a.shape[1] if kk is None else kk
    n = b.shape[1]
    tk = kk if tk is None else tk
    grid = (t // TM, 1, kk // tk)
    return _matmul(
        name, "nn", a, b, grid,
        pl.BlockSpec((TM, tk), lambda i, j, k: (i, k + a_off)),
        pl.BlockSpec((tk, n), lambda i, j, k: (k + b_off, 0)),
        pl.BlockSpec((TM, n), lambda i, j, k: (i, 0)),
        jax.ShapeDtypeStruct((t, n), out_dtype), (TM, n),
        res=res, res_spec=pl.BlockSpec((TM, n), lambda i, j, k: (i, 0)), alpha=alpha, comm=comm)


def _mm_nt(name, a, b, n=None, tn=None, tk=None, out_dtype=F32, comm=None):
    t, kk = a.shape
    n = b.shape[0] if n is None else n
    tn = n if tn is None else tn
    tk = kk if tk is None else tk
    grid = (n // tn, t // TM, kk // tk)
    return _matmul(
        name, "nt", a, b, grid,
        pl.BlockSpec((TM, tk), lambda j, i, k: (i, k)),
        pl.BlockSpec((tn, tk), lambda j, i, k: (j, k)),
        pl.BlockSpec((TM, tn), lambda j, i, k: (i, j)),
        jax.ShapeDtypeStruct((t, n), out_dtype), (TM, tn), comm=comm)


def _mm_tn(name, a, b, out_dtype, tm=None, comm=None):
    t, m = a.shape
    n = b.shape[1]
    tm = m if tm is None else tm
    grid = (m // tm, 1, t // TM)
    return _matmul(
        name, "tn", a, b, grid,
        pl.BlockSpec((TM, tm), lambda j, i, k: (k, j)),
        pl.BlockSpec((TM, n), lambda j, i, k: (k, 0)),
        pl.BlockSpec((tm, n), lambda j, i, k: (j, 0)),
        jax.ShapeDtypeStruct((m, n), out_dtype), (tm, n), comm=comm)


def _rms_fwd(name, x, w):
    t, d = x.shape

    def body(x_ref, w_ref, h_ref):
        xv = x_ref[...]
        rstd = lax.rsqrt(jnp.mean(xv * xv, axis=-1, keepdims=True) + NORM_EPS)
        h_ref[...] = (xv * rstd * w_ref[...]).astype(h_ref.dtype)

    return pl.pallas_call(
        body, name=name, grid=(t // TE,),
        in_specs=[pl.BlockSpec((TE, d), lambda i: (i, 0)), pl.BlockSpec((1, d), lambda i: (0, 0))],
        out_specs=pl.BlockSpec((TE, d), lambda i: (i, 0)),
        out_shape=jax.ShapeDtypeStruct((t, d), BF), compiler_params=_params("parallel"))(x, w)


def _rms_bwd(name, x, w, dh, dres, out_scale, comm=None):
    t, d = x.shape

    def body(x_ref, w_ref, dh_ref, dres_ref, dx_ref, dxb_ref, dw_ref):
        i = pl.program_id(0)
        xv = x_ref[...]
        rstd = lax.rsqrt(jnp.mean(xv * xv, axis=-1, keepdims=True) + NORM_EPS)
        xhat = xv * rstd
        dhv = dh_ref[...]
        wd = dhv * w_ref[...]
        proj = jnp.mean(wd * xhat, axis=-1, keepdims=True)
        dx = dres_ref[...] + rstd * (wd - xhat * proj)
        dx_ref[...] = dx
        dxb_ref[...] = (dx * out_scale).astype(BF)
        part = jnp.sum(dhv * xhat, axis=0, keepdims=True)

        @pl.when(i == 0)
        def _():
            dw_ref[...] = part

        @pl.when(i > 0)
        def _():
            dw_ref[...] += part

    row = pl.BlockSpec((TE, d), lambda i: (i, 0))
    vec = pl.BlockSpec((1, d), lambda i: (0, 0))
    outs, couts = _pcall(
        name, body, (t // TE,), [row, vec, row, row], [row, row, vec],
        [jax.ShapeDtypeStruct((t, d), F32), jax.ShapeDtypeStruct((t, d), BF), jax.ShapeDtypeStruct((1, d), F32)],
        (x, w, dh, dres), (), ("arbitrary",), comm)
    return outs if comm is None else (outs, couts)


def _final_loss(x, w, target):
    t, d = x.shape

    def body(x_ref, w_ref, t_ref, loss_ref, dx_ref, dxb_ref, dw_ref):
        i = pl.program_id(0)
        xv = x_ref[...]
        rstd = lax.rsqrt(jnp.mean(xv * xv, axis=-1, keepdims=True) + NORM_EPS)
        xhat = xv * rstd
        err = xhat * w_ref[...] - t_ref[...]
        lpart = 0.5 * jnp.sum(jnp.mean(err * err, axis=-1, keepdims=True), axis=0, keepdims=True)
        dy = err * (1.0 / d)
        wd = dy * w_ref[...]
        proj = jnp.mean(wd * xhat, axis=-1, keepdims=True)
        dx = rstd * (wd - xhat * proj)
        dx_ref[...] = dx
        dxb_ref[...] = (0.5 * dx).astype(BF)
        part = jnp.sum(dy * xhat, axis=0, keepdims=True)
        lfull = jnp.broadcast_to(lpart, (1, 128))

        @pl.when(i == 0)
        def _():
            dw_ref[...] = part
            loss_ref[...] = lfull

        @pl.when(i > 0)
        def _():
            dw_ref[...] += part
            loss_ref[...] += lfull

    row = pl.BlockSpec((TE, d), lambda i: (i, 0))
    vec = pl.BlockSpec((1, d), lambda i: (0, 0))
    return pl.pallas_call(
        body, name="final_loss", grid=(t // TE,), in_specs=[row, vec, row],
        out_specs=[pl.BlockSpec((1, 128), lambda i: (0, 0)), row, row, vec],
        out_shape=[jax.ShapeDtypeStruct((1, 128), F32), jax.ShapeDtypeStruct((t, d), F32),
                   jax.ShapeDtypeStruct((t, d), BF), jax.ShapeDtypeStruct((1, d), F32)],
        compiler_params=_params("arbitrary"))(x, w, target)


def _swiglu_fwd(name, gu, comm=None):
    t = gu.shape[0]

    def body(g_ref, u_ref, a_ref):
        g = g_ref[...].astype(F32)
        a_ref[...] = (g * _sigmoid(g) * u_ref[...].astype(F32)).astype(BF)

    blk = (TE, FF_HALF)
    out, couts = _pcall(
        name, body, (t // TE, 2),
        [pl.BlockSpec(blk, lambda i, j: (i, 2 * j)), pl.BlockSpec(blk, lambda i, j: (i, 2 * j + 1))],
        pl.BlockSpec(blk, lambda i, j: (i, j)), jax.ShapeDtypeStruct((t, D_FF), BF),
        (gu, gu), (), ("parallel", "parallel"), comm)
    return out if comm is None else (out, couts)


def _swiglu_bwd(name, gu, dact, comm=None):
    t = gu.shape[0]

    def body(g_ref, u_ref, da_ref, o_ref):
        g = g_ref[...].astype(F32)
        da = da_ref[...].astype(F32)
        s = _sigmoid(g)
        o_ref[:, 0:FF_HALF] = (da * u_ref[...].astype(F32) * (s * (1.0 + g * (1.0 - s)))).astype(BF)
        o_ref[:, FF_HALF:2 * FF_HALF] = (da * g * s).astype(BF)

    blk = (TE, FF_HALF)
    out, couts = _pcall(
        name, body, (t // TE, 2),
        [pl.BlockSpec(blk, lambda i, j: (i, 2 * j)), pl.BlockSpec(blk, lambda i, j: (i, 2 * j + 1)),
         pl.BlockSpec(blk, lambda i, j: (i, j))],
        pl.BlockSpec((TE, 2 * FF_HALF), lambda i, j: (i, j)),
        jax.ShapeDtypeStruct((t, 2 * D_FF), BF), (gu, gu, dact), (), ("parallel", "parallel"), comm)
    return out if comm is None else (out, couts)


CONV_CB = 256


CONV_ROWS = 64
CONV_HALO = 16


def _taps_down(ext, w, k):
    shifted = [pltpu.roll(ext, k - 1 - j, 0)[CONV_HALO:] for j in range(k - 1)] + [ext[CONV_HALO:]]
    out = shifted[k - 1] * w[k - 1:k, :]
    for j in range(k - 1):
        out = out + shifted[j] * w[j:j + 1, :]
    return out, shifted


def _taps_up(ext, w, k):
    rows = ext.shape[0]
    n = rows - CONV_HALO
    out = ext[:n] * w[k - 1:k, :]
    for j in range(k - 1):
        out = out + pltpu.roll(ext, rows - (k - 1 - j), 0)[:n] * w[j:j + 1, :]
    return out


def _rows_before(ref, i, r0):
    start = pl.multiple_of(jnp.maximum(r0 - CONV_HALO, 0), CONV_HALO)
    return jnp.where(i > 0, ref[pl.ds(start, CONV_HALO), :].astype(F32), 0.0)


def _rows_after(ref, r0, t):
    start = pl.multiple_of(jnp.minimum(r0 + CONV_ROWS, t - CONV_HALO), CONV_HALO)
    return ref[pl.ds(start, CONV_HALO), :].astype(F32)


def _fold8(v):
    return v.reshape(v.shape[0] // 8, 8, v.shape[1]).sum(axis=0)


def _silu_grad(pre):
    s = _sigmoid(pre)
    return s * (1.0 + pre * (1.0 - s))


def _pspec(t, off):
    base = off // CONV_CB
    return pl.BlockSpec((t, CONV_CB), lambda j: (0, base + j))


def _mix_a_fwd(p, conv_w):
    t = p.shape[0]

    def body(b_ref, c_ref, xa_ref, w_ref, o_ref):
        w = w_ref[...]

        def step(i, carry):
            r0 = pl.multiple_of(i * CONV_ROWS, CONV_ROWS)
            rows = pl.ds(r0, CONV_ROWS)
            q = c_ref[rows, :].astype(F32) * xa_ref[rows, :].astype(F32)
            q_before = _rows_before(c_ref, i, r0) * _rows_before(xa_ref, i, r0)
            va, _ = _taps_down(jnp.concatenate([q_before, q], axis=0), w, 3)
            o_ref[rows, :] = (b_ref[rows, :].astype(F32) * va).astype(BF)
            return carry

        lax.fori_loop(0, t // CONV_ROWS, step, 0)

    return pl.pallas_call(
        body, name="mix_a_fwd", grid=(D_MODEL // CONV_CB,),
        in_specs=[_pspec(t, OFF_B), _pspec(t, OFF_C), _pspec(t, OFF_XA),
                  pl.BlockSpec((3, CONV_CB), lambda j: (0, j))],
        out_specs=pl.BlockSpec((t, CONV_CB), lambda j: (0, j)),
        out_shape=jax.ShapeDtypeStruct((t, D_MODEL), BF), compiler_params=_params("parallel"))(p, p, p, conv_w)


def _mix_a_bwd(p, conv_w, dya, dp):
    t = p.shape[0]

    def body(b_ref, c_ref, xa_ref, w_ref, dy_ref, dp_in, dp_ref, dw_ref):
        del dp_in
        w = w_ref[...]
        n = t // CONV_ROWS

        def step(i, acc):
            r0 = pl.multiple_of(i * CONV_ROWS, CONV_ROWS)
            rows = pl.ds(r0, CONV_ROWS)
            cv = c_ref[rows, :].astype(F32)
            xav = xa_ref[rows, :].astype(F32)
            q_before = _rows_before(c_ref, i, r0) * _rows_before(xa_ref, i, r0)
            va, shifted = _taps_down(jnp.concatenate([q_before, cv * xav], axis=0), w, 3)
            dyv = dy_ref[rows, :]
            dp_ref[rows, 0:CONV_CB] = (dyv * va).astype(BF)
            dv = dyv * b_ref[rows, :].astype(F32)
            dv_after = jnp.where(i < n - 1, _rows_after(dy_ref, r0, t) * _rows_after(b_ref, r0, t), 0.0)
            dq = _taps_up(jnp.concatenate([dv, dv_after], axis=0), w, 3)
            dp_ref[rows, CONV_CB:2 * CONV_CB] = (dq * xav).astype(BF)
            dp_ref[rows, 2 * CONV_CB:3 * CONV_CB] = (dq * cv).astype(BF)
            return tuple(a + _fold8(dv * s) for a, s in zip(acc, shifted))

        zero = jnp.zeros((8, CONV_CB), F32)
        acc = lax.fori_loop(0, n, step, (zero, zero, zero))
        for j in range(3):
            dw_ref[j:j + 1, :] = jnp.sum(acc[j], axis=0, keepdims=True)

    col = pl.BlockSpec((t, CONV_CB), lambda j: (0, j))
    wsp = pl.BlockSpec((3, CONV_CB), lambda j: (0, j))
    return pl.pallas_call(
        body, name="mix_a_bwd", grid=(D_MODEL // CONV_CB,),
        in_specs=[_pspec(t, OFF_B), _pspec(t, OFF_C), _pspec(t, OFF_XA), wsp, col, pl.BlockSpec(memory_space=pl.ANY)],
        out_specs=[pl.BlockSpec((t, 3 * CONV_CB), lambda j: (0, j)), wsp],
        out_shape=[jax.ShapeDtypeStruct(dp.shape, dp.dtype), jax.ShapeDtypeStruct((3, D_MODEL), F32)],
        input_output_aliases={5: 0},
        compiler_params=_params("parallel"))(p, p, p, conv_w, dya, dp)


def _ssm_conv_fwd(p, conv_w, conv_b):
    t = p.shape[0]

    def body(x_ref, w_ref, b_ref, o_ref):
        w = w_ref[...]
        bias = b_ref[...]

        def step(i, carry):
            r0 = pl.multiple_of(i * CONV_ROWS, CONV_ROWS)
            rows = pl.ds(r0, CONV_ROWS)
            ext = jnp.concatenate([_rows_before(x_ref, i, r0), x_ref[rows, :].astype(F32)], axis=0)
            pre = _taps_down(ext, w, 4)[0] + bias
            o_ref[rows, :] = pre * _sigmoid(pre)
            return carry

        lax.fori_loop(0, t // CONV_ROWS, step, 0)

    return pl.pallas_call(
        body, name="ssm_conv_fwd", grid=(D_XBC // CONV_CB,),
        in_specs=[_pspec(t, OFF_XBC), pl.BlockSpec((4, CONV_CB), lambda j: (0, j)),
                  pl.BlockSpec((1, CONV_CB), lambda j: (0, j))],
        out_specs=pl.BlockSpec((t, CONV_CB), lambda j: (0, j)),
        out_shape=jax.ShapeDtypeStruct((t, D_XBC), F32), compiler_params=_params("parallel"))(p, conv_w, conv_b)


def _ssm_conv_bwd(p, conv_w, conv_b, dxc, dp):
    t = p.shape[0]

    def body(x_ref, w_ref, b_ref, d_ref, dp_in, dx_ref, dw_ref, db_ref):
        del dp_in
        w = w_ref[...]
        bias = b_ref[...]
        n = t // CONV_ROWS

        def step(i, acc):
            r0 = pl.multiple_of(i * CONV_ROWS, CONV_ROWS)
            rows = pl.ds(r0, CONV_ROWS)
            x_cur = x_ref[rows, :].astype(F32)
            pre, shifted = _taps_down(jnp.concatenate([_rows_before(x_ref, i, r0), x_cur], axis=0), w, 4)
            pre = pre + bias
            dpre = d_ref[rows, :] * _silu_grad(pre)
            ext_after = jnp.concatenate([x_cur[CONV_ROWS - CONV_HALO:], _rows_after(x_ref, r0, t)], axis=0)
            pre_after = _taps_down(ext_after, w, 4)[0] + bias
            dpre_after = jnp.where(i < n - 1, _rows_after(d_ref, r0, t) * _silu_grad(pre_after), 0.0)
            dx_ref[rows, :] = _taps_up(jnp.concatenate([dpre, dpre_after], axis=0), w, 4).astype(BF)
            new = tuple(a + _fold8(dpre * s) for a, s in zip(acc[:4], shifted))
            return new + (acc[4] + _fold8(dpre),)

        zero = jnp.zeros((8, CONV_CB), F32)
        acc = lax.fori_loop(0, n, step, (zero,) * 5)
        for j in range(4):
            dw_ref[j:j + 1, :] = jnp.sum(acc[j], axis=0, keepdims=True)
        db_ref[...] = jnp.sum(acc[4], axis=0, keepdims=True)

    col = pl.BlockSpec((t, CONV_CB), lambda j: (0, j))
    wsp = pl.BlockSpec((4, CONV_CB), lambda j: (0, j))
    bsp = pl.BlockSpec((1, CONV_CB), lambda j: (0, j))
    return pl.pallas_call(
        body, name="ssm_conv_bwd", grid=(D_XBC // CONV_CB,),
        in_specs=[_pspec(t, OFF_XBC), wsp, bsp, col, pl.BlockSpec(memory_space=pl.ANY)],
        out_specs=[_pspec(t, OFF_XBC), wsp, bsp],
        out_shape=[jax.ShapeDtypeStruct(dp.shape, dp.dtype), jax.ShapeDtypeStruct((4, D_XBC), F32),
                   jax.ShapeDtypeStruct((1, D_XBC), F32)],
        input_output_aliases={4: 0},
        compiler_params=_params("parallel"))(p, conv_w, conv_b, dxc, dp)


DT_ROWS = 512


def _tri(lower):
    r = lax.broadcasted_iota(jnp.int32, (CHUNK, CHUNK), 0)
    c = lax.broadcasted_iota(jnp.int32, (CHUNK, CHUNK), 1)
    return jnp.where((r >= c) if lower else (r <= c), 1.0, 0.0).astype(F32)


def _dot_exact(a, b):
    return lax.dot_general(a, b, _DIMS["nn"], preferred_element_type=F32, precision=lax.Precision.HIGHEST)


def _dt_fwd(p, bias_pad, alog_pad):
    t = p.shape[0]

    def body(raw_ref, b_ref, al_ref, dt_ref, acs_ref):
        z = raw_ref[...] + b_ref[...]
        dt = jnp.maximum(z, 0.0) + jnp.log(1.0 + jnp.exp(-jnp.abs(z)))
        dt_ref[...] = dt
        a = dt * (-jnp.exp(al_ref[...]))
        tri = _tri(True)
        for k in range(DT_ROWS // CHUNK):
            acs_ref[k * CHUNK:(k + 1) * CHUNK, :] = _dot_exact(tri, a[k * CHUNK:(k + 1) * CHUNK, :])

    blk = pl.BlockSpec((DT_ROWS, DT_W), lambda i: (i, 0))
    vec = pl.BlockSpec((1, DT_W), lambda i: (0, 0))
    return pl.pallas_call(
        body, name="dt_fwd", grid=(t // DT_ROWS,),
        in_specs=[pl.BlockSpec((DT_ROWS, DT_W), lambda i: (i, OFF_DT // DT_W)), vec, vec],
        out_specs=[blk, blk], out_shape=[jax.ShapeDtypeStruct((t, DT_W), F32)] * 2,
        compiler_params=_params("parallel"))(p, bias_pad, alog_pad)


def _dt_bwd(p, bias_pad, alog_pad, dt, ddt, dacs, dp_gd):
    t = p.shape[0]

    def body(raw_ref, b_ref, al_ref, dt_ref, ddt_ref, dacs_ref, dp_in, draw_ref, db_ref, dal_ref):
        del dp_in
        i = pl.program_id(0)
        acoef = -jnp.exp(al_ref[...])
        triu = _tri(False)
        das = []
        for k in range(DT_ROWS // CHUNK):
            das.append(_dot_exact(triu, dacs_ref[k * CHUNK:(k + 1) * CHUNK, :]))
        da = jnp.concatenate(das, axis=0)
        dtv = dt_ref[...]
        ddt_tot = ddt_ref[...] + da * acoef
        lane = lax.broadcasted_iota(jnp.int32, (DT_ROWS, DT_W), 1)
        draw = jnp.where(lane < N_HEADS, ddt_tot * _sigmoid(raw_ref[...] + b_ref[...]), 0.0)
        draw_ref[...] = draw.astype(BF)
        pb = jnp.sum(draw, axis=0, keepdims=True)
        pa = jnp.sum(da * dtv * acoef, axis=0, keepdims=True)

        @pl.when(i == 0)
        def _():
            db_ref[...] = pb
            dal_ref[...] = pa

        @pl.when(i > 0)
        def _():
            db_ref[...] += pb
            dal_ref[...] += pa

    blk = pl.BlockSpec((DT_ROWS, DT_W), lambda i: (i, 0))
    vec = pl.BlockSpec((1, DT_W), lambda i: (0, 0))
    return pl.pallas_call(
        body, name="dt_bwd", grid=(t // DT_ROWS,),
        in_specs=[pl.BlockSpec((DT_ROWS, DT_W), lambda i: (i, OFF_DT // DT_W)), vec, vec, blk, blk, blk,
                  pl.BlockSpec(memory_space=pl.ANY)],
        out_specs=[pl.BlockSpec((DT_ROWS, DT_W), lambda i: (i, OFF_DT // DT_W)), vec, vec],
        out_shape=[jax.ShapeDtypeStruct(dp_gd.shape, dp_gd.dtype), jax.ShapeDtypeStruct((1, DT_W), F32),
                   jax.ShapeDtypeStruct((1, DT_W), F32)],
        input_output_aliases={6: 0},
        compiler_params=_params("arbitrary"))(p, bias_pad, alog_pad, dt, ddt, dacs, dp_gd)


def _split_dot(z, onehot, terms):
    out = None
    rest = z
    for _ in range(terms):
        piece = rest.astype(BF)
        part = _dot(piece, onehot)
        out = part if out is None else out + part
        rest = rest - piece.astype(F32)
    return out


def _spread_mat(g):
    row = lax.broadcasted_iota(jnp.int32, (DT_W, GROUP_W), 0)
    lane = lax.broadcasted_iota(jnp.int32, (DT_W, GROUP_W), 1)
    return jnp.where(row == HEADS_PER_GROUP * g + lane // HEAD_DIM, 1.0, 0.0).astype(BF)


def _gather_mat(g):
    row = lax.broadcasted_iota(jnp.int32, (GROUP_W, DT_W), 0)
    lane = lax.broadcasted_iota(jnp.int32, (GROUP_W, DT_W), 1)
    return jnp.where(lane == HEADS_PER_GROUP * g + row // HEAD_DIM, 1.0, 0.0).astype(BF)


def _ssd_masks():
    row = lax.broadcasted_iota(jnp.int32, (CHUNK, GROUP_W), 0)
    col = lax.broadcasted_iota(jnp.int32, (CHUNK, GROUP_W), 1) % HEAD_DIM
    brow = lax.broadcasted_iota(jnp.int32, (GROUP_W, GROUP_W), 0) // HEAD_DIM
    bcol = lax.broadcasted_iota(jnp.int32, (GROUP_W, GROUP_W), 1) // HEAD_DIM
    return row >= col, row == col, brow == bcol


def _stack4(v):
    return jnp.concatenate([v, v, v, v], axis=0)


def _fold4(v):
    return v[0:CHUNK] + v[CHUNK:2 * CHUNK] + v[2 * CHUNK:3 * CHUNK] + v[3 * CHUNK:4 * CHUNK]


def _ssd_group(xc_ref, stacked, g, tri, eye, blockdiag):
    gs = slice(GROUP_W * g, GROUP_W * (g + 1))
    xs_g = xc_ref[:, gs]
    b_g = xc_ref[:, D_INNER + D_STATE * g:D_INNER + D_STATE * (g + 1)].astype(BF)
    c_g = xc_ref[:, D_INNER + 1024 + D_STATE * g:D_INNER + 1024 + D_STATE * (g + 1)].astype(BF)
    wide = _split_dot(stacked, _spread_mat(g), 3)
    acs_e, dt_e = wide[0:CHUNK], wide[CHUNK:2 * CHUNK]
    atot_e = acs_e[CHUNK - 1:CHUNK, :]
    acs_j = jnp.sum(jnp.where(eye, acs_e, 0.0), axis=0, keepdims=True)
    lmat = jnp.where(tri, jnp.exp(jnp.minimum(acs_e - acs_j, 0.0)), 0.0)
    b_t = _stack4(b_g)
    m = _dot(c_g, b_t, "nt") * lmat
    x_g = xs_g * dt_e
    xbd = jnp.where(blockdiag, _stack4(x_g), 0.0).astype(BF)
    return dict(gs=gs, xs=xs_g, b=b_g, c=c_g, b_t=b_t, dt=dt_e, e=jnp.exp(acs_e), dec=jnp.exp(atot_e - acs_e),
                eat=jnp.exp(atot_e), lmat=lmat, m=m, x=x_g, xbd=xbd)


def _ssd_fwd(xconv, dt, acs, d_exp, comm=None):
    t = xconv.shape[0]
    nc = t // CHUNK

    def body(xc_ref, dt_ref, acs_ref, d_ref, y_ref, hs_ref, state):
        c = pl.program_id(0)

        @pl.when(c == 0)
        def _():
            state[...] = jnp.zeros_like(state)

        hs_ref[...] = state[...]
        tri, eye, blockdiag = _ssd_masks()
        stacked = jnp.concatenate([acs_ref[...], dt_ref[...]], axis=0)
        for g in range(N_GROUPS):
            q = _ssd_group(xc_ref, stacked, g, tri, eye, blockdiag)
            gs = q["gs"]
            h_t = state[:, gs]
            ydiag = _dot(q["m"].astype(BF), q["xbd"])
            yoff = _dot(q["c"], h_t.astype(BF)) * q["e"]
            y_ref[:, gs] = ydiag + yoff + d_ref[:, gs] * q["xs"]
            s_t = _dot(q["b"], (q["x"] * q["dec"]).astype(BF), "tn")
            state[:, gs] = q["eat"] * h_t + s_t

    blk = lambda w: pl.BlockSpec((CHUNK, w), lambda c: (c, 0))
    outs, couts = _pcall(
        "ssd_fwd", body, (nc,),
        [blk(D_XBC), blk(DT_W), blk(DT_W), pl.BlockSpec((1, D_INNER), lambda c: (0, 0))],
        [blk(D_INNER), pl.BlockSpec((None, D_STATE, D_INNER), lambda c: (c, 0, 0))],
        [jax.ShapeDtypeStruct((t, D_INNER), F32), jax.ShapeDtypeStruct((nc, D_STATE, D_INNER), F32)],
        (xconv, dt, acs, d_exp), [pltpu.VMEM((D_STATE, D_INNER), F32)], ("arbitrary",), comm)
    return outs if comm is None else (outs, couts)


def _ssd_bwd(xconv, dt, acs, d_exp, hsave, dy, comm=None):
    t = xconv.shape[0]
    nc = t // CHUNK

    def body(xc_ref, dt_ref, acs_ref, d_ref, hs_ref, dy_ref, dxc_ref, ddt_ref, dacs_ref, dd_ref, dstate):
        c = pl.program_id(0)

        @pl.when(c == 0)
        def _():
            dstate[...] = jnp.zeros_like(dstate)
            dd_ref[...] = jnp.zeros_like(dd_ref)

        tri, eye, blockdiag = _ssd_masks()
        acsv = acs_ref[...]
        stacked = jnp.concatenate([acsv, dt_ref[...]], axis=0)
        eat_heads = jnp.exp(acsv[CHUNK - 1:CHUNK, :])
        ddt_acc = jnp.zeros((CHUNK, DT_W), F32)
        dacs_acc = jnp.zeros((CHUNK, DT_W), F32)
        datot_acc = jnp.zeros((1, DT_W), F32)

        for g in range(N_GROUPS):
            q = _ssd_group(xc_ref, stacked, g, tri, eye, blockdiag)
            gs, xs_g, b_g, c_g, m = q["gs"], q["xs"], q["b"], q["c"], q["m"]
            bs = slice(D_INNER + D_STATE * g, D_INNER + D_STATE * (g + 1))
            cs = slice(D_INNER + 1024 + D_STATE * g, D_INNER + 1024 + D_STATE * (g + 1))
            h_t = hs_ref[:, gs]
            h_b = h_t.astype(BF)
            dy_g = dy_ref[:, gs]
            dy_b = dy_g.astype(BF)
            ds_t = dstate[:, gs]
            ds_b = ds_t.astype(BF)

            yoff = _dot(c_g, h_b) * q["e"]
            edy = (q["e"] * dy_g).astype(BF)
            d_c = _dot(edy, h_b, "nt")
            d_ht = _dot(c_g, edy, "tn")
            bds = _dot(b_g, ds_b)
            xd = q["x"] * q["dec"]
            d_b = _dot(xd.astype(BF), ds_b, "nt")
            dm = _dot(dy_b, q["xbd"], "nt")
            cross = _dot(m.astype(BF), dy_b, "tn")
            dx_full = q["dec"] * bds + _fold4(jnp.where(blockdiag, cross, 0.0))
            dml = (dm * q["lmat"]).astype(BF)
            d_c = d_c + _dot(dml, q["b_t"])
            d_b = d_b + _fold4(_dot(dml, c_g, "tn"))
            w = dm * m
            q_dec = xd * bds
            z = w - jnp.where(eye, jnp.sum(w, axis=0, keepdims=True), 0.0) + dy_g * yoff - q_dec
            rows = jnp.concatenate(
                [jnp.sum(q_dec, axis=0, keepdims=True), jnp.sum(ds_t * h_t, axis=0, keepdims=True),
                 jnp.zeros((6, GROUP_W), F32)], axis=0)
            seg = _split_dot(jnp.concatenate([z, dx_full * xs_g, rows], axis=0), _gather_mat(g), 2)
            dacs_acc = dacs_acc + seg[0:CHUNK]
            ddt_acc = ddt_acc + seg[CHUNK:2 * CHUNK]
            datot_acc = datot_acc + seg[2 * CHUNK:2 * CHUNK + 1] + eat_heads * seg[2 * CHUNK + 1:2 * CHUNK + 2]
            dxc_ref[:, cs] = d_c
            dxc_ref[:, bs] = d_b
            dxc_ref[:, gs] = dx_full * q["dt"] + d_ref[:, gs] * dy_g
            dd_ref[:, gs] += jnp.sum(dy_g * xs_g, axis=0, keepdims=True)
            dstate[:, gs] = q["eat"] * ds_t + d_ht

        rowi = lax.broadcasted_iota(jnp.int32, (CHUNK, DT_W), 0)
        ddt_ref[...] = ddt_acc
        dacs_ref[...] = dacs_acc + jnp.where(rowi == CHUNK - 1, datot_acc, 0.0)

    rev = lambda w: pl.BlockSpec((CHUNK, w), lambda c: (nc - 1 - c, 0))
    vec = pl.BlockSpec((1, D_INNER), lambda c: (0, 0))
    outs, couts = _pcall(
        "ssd_bwd", body, (nc,),
        [rev(D_XBC), rev(DT_W), rev(DT_W), vec,
         pl.BlockSpec((None, D_STATE, D_INNER), lambda c: (nc - 1 - c, 0, 0)), rev(D_INNER)],
        [rev(D_XBC), rev(DT_W), rev(DT_W), vec],
        [jax.ShapeDtypeStruct((t, D_XBC), F32), jax.ShapeDtypeStruct((t, DT_W), F32),
         jax.ShapeDtypeStruct((t, DT_W), F32), jax.ShapeDtypeStruct((1, D_INNER), F32)],
        (xconv, dt, acs, d_exp, hsave, dy),
        [pltpu.VMEM((D_STATE, D_INNER), F32)], ("arbitrary",), comm)
    return outs if comm is None else (outs, couts)


GN_CB = 1024
GN_GROUPS = GN_CB // GROUP_W


def _gnorm_fwd(y, p, w, comm=None):
    t = y.shape[0]
    zoff = OFF_Z // GN_CB

    def body(y_ref, z_ref, w_ref, o_ref):
        for g in range(GN_GROUPS):
            gs = slice(GROUP_W * g, GROUP_W * (g + 1))
            z = z_ref[:, gs].astype(F32)
            yf = y_ref[:, gs] * (z * _sigmoid(z))
            rstd = lax.rsqrt(jnp.mean(yf * yf, axis=-1, keepdims=True) + NORM_EPS)
            o_ref[:, gs] = (yf * rstd * w_ref[:, gs]).astype(BF)

    blk = pl.BlockSpec((TE, GN_CB), lambda i, j: (i, j))
    out, couts = _pcall(
        "gnorm_fwd", body, (t // TE, D_INNER // GN_CB),
        [blk, pl.BlockSpec((TE, GN_CB), lambda i, j: (i, zoff + j)), pl.BlockSpec((1, GN_CB), lambda i, j: (0, j))],
        blk, jax.ShapeDtypeStruct((t, D_INNER), BF), (y, p, w), (), ("parallel", "parallel"), comm)
    return out if comm is None else (out, couts)


def _gnorm_bwd(y, p, w, dyn, comm=None):
    t = y.shape[0]
    zoff = OFF_Z // GN_CB

    def body(y_ref, z_ref, w_ref, dn_ref, dy_ref, dz_ref, dw_ref):
        i = pl.program_id(1)
        for g in range(GN_GROUPS):
            gs = slice(GROUP_W * g, GROUP_W * (g + 1))
            z = z_ref[:, gs].astype(F32)
            yv = y_ref[:, gs]
            s = _sigmoid(z)
            sil = z * s
            yf = yv * sil
            rstd = lax.rsqrt(jnp.mean(yf * yf, axis=-1, keepdims=True) + NORM_EPS)
            xhat = yf * rstd
            dn = dn_ref[:, gs]
            wd = dn * w_ref[:, gs]
            proj = jnp.mean(wd * xhat, axis=-1, keepdims=True)
            dyf = rstd * (wd - xhat * proj)
            dy_ref[:, gs] = dyf * sil
            dz_ref[:, gs] = (dyf * yv * (s * (1.0 + z * (1.0 - s)))).astype(BF)
            part = jnp.sum(dn * xhat, axis=0, keepdims=True)

            @pl.when(i == 0)
            def _():
                dw_ref[:, gs] = part

            @pl.when(i > 0)
            def _():
                dw_ref[:, gs] += part

    blk = pl.BlockSpec((TE, GN_CB), lambda j, i: (i, j))
    vec = pl.BlockSpec((1, GN_CB), lambda j, i: (0, j))
    outs, couts = _pcall(
        "gnorm_bwd", body, (D_INNER // GN_CB, t // TE),
        [blk, pl.BlockSpec((TE, GN_CB), lambda j, i: (i, zoff + j)), vec, blk],
        [blk, pl.BlockSpec((TE, GN_CB), lambda j, i: (i, zoff + j)), vec],
        [jax.ShapeDtypeStruct((t, D_INNER), F32), jax.ShapeDtypeStruct((t, N_MAIN), BF),
         jax.ShapeDtypeStruct((1, D_INNER), F32)],
        (y, p, w, dyn), (), ("parallel", "arbitrary"), comm)
    return outs if comm is None else (outs, couts)


MERGE_CB = 512


def _merge_fwd(p, ya, yb):
    t = ya.shape[0]

    def body(ga_ref, gb_ref, ya_ref, yb_ref, o_ref):
        o_ref[...] = (_sigmoid(ga_ref[...]) * ya_ref[...] + _sigmoid(gb_ref[...]) * yb_ref[...]).astype(BF)

    blk = pl.BlockSpec((TE, MERGE_CB), lambda i, j: (i, j))
    return pl.pallas_call(
        body, name="merge_fwd", grid=(t // TE, D_MODEL // MERGE_CB),
        in_specs=[pl.BlockSpec((TE, MERGE_CB), lambda i, j: (i, 2 * j)),
                  pl.BlockSpec((TE, MERGE_CB), lambda i, j: (i, 2 * j + 1)), blk, blk],
        out_specs=blk, out_shape=jax.ShapeDtypeStruct((t, D_MODEL), BF),
        compiler_params=_params("parallel", "parallel"))(p, p, ya, yb)


def _merge_bwd(p, ya, yb, dm):
    t = ya.shape[0]

    def body(ga_ref, gb_ref, ya_ref, yb_ref, dm_ref, dg_ref, dya_ref, dyb_ref):
        d = dm_ref[...]
        sa = _sigmoid(ga_ref[...])
        sb = _sigmoid(gb_ref[...])
        dg_ref[:, 0:MERGE_CB] = (d * ya_ref[...] * sa * (1.0 - sa)).astype(BF)
        dg_ref[:, MERGE_CB:2 * MERGE_CB] = (d * yb_ref[...] * sb * (1.0 - sb)).astype(BF)
        dya_ref[...] = (d * sa).astype(BF)
        dyb_ref[...] = (d * sb).astype(BF)

    blk = pl.BlockSpec((TE, MERGE_CB), lambda i, j: (i, j))
    return pl.pallas_call(
        body, name="merge_bwd", grid=(t // TE, D_MODEL // MERGE_CB),
        in_specs=[pl.BlockSpec((TE, MERGE_CB), lambda i, j: (i, 2 * j)),
                  pl.BlockSpec((TE, MERGE_CB), lambda i, j: (i, 2 * j + 1)), blk, blk, blk],
        out_specs=[pl.BlockSpec((TE, 2 * MERGE_CB), lambda i, j: (i, j)), blk, blk],
        out_shape=[jax.ShapeDtypeStruct((t, N_GD), BF)] + [jax.ShapeDtypeStruct((t, D_MODEL), BF)] * 2,
        compiler_params=_params("parallel", "parallel"))(p, p, ya, yb, dm)


def _adamw(name, parts, w, m, v, comm=None):
    r, c = w.shape
    tr = _row_tile(r)
    tc = ADAM_COL_TILE if (tr == r and r > 512 and c % ADAM_COL_TILE == 0) else c
    n_parts = parts.shape[0]
    bc1 = 1.0 - ADAM_B1 ** ADAM_STEP
    bc2 = 1.0 - ADAM_B2 ** ADAM_STEP

    def body(p_ref, w_ref, m_ref, v_ref, g_ref, d_ref, nm_ref, nv_ref):
        g = p_ref[0].astype(F32)
        for k in range(1, n_parts):
            g = g + p_ref[k].astype(F32)
        nm = ADAM_B1 * m_ref[...] + (1.0 - ADAM_B1) * g
        nv = ADAM_B2 * v_ref[...] + (1.0 - ADAM_B2) * (g * g)
        g_ref[...] = g
        nm_ref[...] = nm
        nv_ref[...] = nv
        d_ref[...] = -ADAM_LR * ((nm / bc1) / (jnp.sqrt(nv / bc2) + ADAM_EPS) + ADAM_WD * w_ref[...])

    blk = pl.BlockSpec((tr, tc), lambda i, j: (i, j))
    outs, couts = _pcall(
        name, body, (r // tr, c // tc),
        [pl.BlockSpec((n_parts, tr, tc), lambda i, j: (0, i, j)), blk, blk, blk], [blk] * 4,
        [jax.ShapeDtypeStruct((r, c), F32)] * 4, (parts, w, m, v), (), ("parallel", "parallel"), comm)
    return outs if comm is None else (outs, couts)


def _pad_lanes(v, width):
    return jnp.pad(v, ((0, 0), (0, width - v.shape[1])))


def _reduce_start(slots, host):
    outs, sib = host(_pair_comm([a for _, a in slots]))
    sums = [(n, _add_pairs("pairsum_" + n, a, b)) for (n, a), b in zip(slots, sib)]
    return outs, sums


def _train_step(x, target, shard, rep):
    gdt = BF
    t = x.shape[0]
    recv = {}
    (got,) = _comm_call("gather_ffn1_in", _gather_comm([shard["ffn1_w_in"]], [True]))
    w1_in = got.reshape(2 * D_FF, D_MODEL)
    h1 = _rms_fwd("rms1_fwd", x, rep["ffn1_norm"])
    w_in_rows = [shard["w_in"][a:b] for a, b in W_IN_ROW_CUTS]
    gu1, got = _mm_nt("ffn1_in", h1, w1_in, tn=FF_HALF, out_dtype=BF, comm=_gather_comm([shard["ffn1_w_out"], w_in_rows[0]]))
    w1_out = got[0].reshape(D_FF, D_MODEL)
    w_in_got = [got[1]]
    act1, got = _swiglu_fwd("swiglu1_fwd", gu1, comm=_gather_comm([w_in_rows[1]]))
    w_in_got.append(got[0])
    x1, got = _mm_nn("ffn1_out", act1, w1_out, res=x, alpha=0.5, comm=_gather_comm(
        [w_in_rows[2], shard["short_conv_w"], shard["ssm_conv_w"]]))
    w_in_got.append(got[0])
    short_conv_w = got[1].transpose(1, 0, 2).reshape(3, D_MODEL)
    ssm_conv_w = got[2].transpose(1, 0, 2).reshape(4, D_XBC)
    w_in_t = jnp.concatenate(w_in_got, axis=1).reshape(N_IN, D_MODEL)
    ga0 = N_MAIN + N_HEADS
    gb0 = ga0 + D_MODEL
    half = D_MODEL // 2
    w_gd = jnp.concatenate(
        [w_in_t[ga0:ga0 + half], w_in_t[gb0:gb0 + half], w_in_t[ga0 + half:gb0], w_in_t[gb0 + half:],
         w_in_t[N_MAIN:N_MAIN + N_HEADS], jnp.zeros((DT_W - N_HEADS, D_MODEL), BF)], axis=0)
    w_mix_perm = w_in_t[0:3 * D_MODEL].reshape(3, 4, CONV_CB, D_MODEL).transpose(1, 0, 2, 3).reshape(3 * D_MODEL, D_MODEL)

    h2 = _rms_fwd("rms2_fwd", x1, rep["mix_norm"])
    p, got = _mm_nt("proj_main", h2, w_in_t, n=N_MAIN, tn=1024, out_dtype=BF, comm=_gather_comm(
        [shard["short_w_out"], shard["ssm_w_out"], shard["w_out"]]))
    p_gd = _mm_nt("proj_gd", h2, w_gd)
    short_w_out = got[0].reshape(D_MODEL, D_MODEL)
    ssm_w_out = got[1].reshape(D_INNER, D_MODEL)
    w_out = got[2].reshape(D_MODEL, D_MODEL)
    ya_in = _mix_a_fwd(p, short_conv_w)
    y_a = _mm_nn("short_out", ya_in, short_w_out)
    xconv = _ssm_conv_fwd(p, ssm_conv_w, rep["ssm_conv_b"])
    dt, acs = _dt_fwd(p_gd, rep["dt_bias_pad"], rep["a_log_pad"])
    (y_ssm, hsave), (got,) = _ssd_fwd(xconv, dt, acs, rep["d_exp"], comm=_gather_comm([shard["ffn2_w_in"]], [True]))
    w2_in = got.reshape(2 * D_FF, D_MODEL)
    yn, got = _gnorm_fwd(y_ssm, p, rep["ssm_norm"], comm=_gather_comm([shard["ffn2_w_out"]]))
    w2_out = got[0].reshape(D_FF, D_MODEL)
    y_b = _mm_nn("ssm_out", yn, ssm_w_out, tk=1024)
    merged = _merge_fwd(p_gd, y_a, y_b)
    x2 = _mm_nn("mix_out", merged, w_out, res=x1)

    h3 = _rms_fwd("rms3_fwd", x2, rep["ffn2_norm"])
    gu2 = _mm_nt("ffn2_in", h3, w2_in, tn=FF_HALF, out_dtype=BF)
    act2 = _swiglu_fwd("swiglu2_fwd", gu2)
    x3 = _mm_nn("ffn2_out", act2, w2_out, res=x2, alpha=0.5)

    loss, dx3, dx3h, g_final = _final_loss(x3, rep["final_norm"], target)

    small = {"final_norm": g_final}
    dact2 = _mm_nt("ffn2_out_bwd_act", dx3h, w2_out, out_dtype=BF)
    g_w2_out = _mm_tn("ffn2_out_bwd_w", act2, dx3h, gdt, tm=FF_HALF)
    dgu2 = _swiglu_bwd("swiglu2_bwd", gu2, dact2)
    g_w2_in = _mm_tn("ffn2_in_bwd_w", dgu2, h3, gdt, tm=FF_HALF)
    dh3 = _mm_nn("ffn2_in_bwd_h", dgu2, w2_in, tk=FF_HALF)
    dx2, dx2b, small["ffn2_norm"] = _rms_bwd("rms3_bwd", x2, rep["ffn2_norm"], dh3, dx3, 1.0)

    dmerged = _mm_nt("mix_out_bwd_x", dx2b, w_out)
    g_w_out = _mm_tn("mix_out_bwd_w", merged, dx2b, gdt)
    dp_gd, dya, dyb = _merge_bwd(p_gd, y_a, y_b, dmerged)

    dya_in = _mm_nt("short_out_bwd_x", dya, short_w_out)
    g_short_w_out = _mm_tn("short_out_bwd_w", ya_in, dya, gdt)

    dyn = _mm_nt("ssm_out_bwd_x", dyb, ssm_w_out)
    g_ssm_w_out = _mm_tn("ssm_out_bwd_w", yn, dyb, gdt)
    late = [("ffn2_w_out", g_w2_out.reshape(N_DEV, FF_SHARD // 2, D_MODEL)),
            ("ffn2_w_in", g_w2_in.reshape(N_DEV, FF_SHARD, D_MODEL)),
            ("w_out", g_w_out.reshape(N_DEV, -1, D_MODEL)), ("short_w_out", g_short_w_out.reshape(N_DEV, -1, D_MODEL)),
            ("ssm_w_out", g_ssm_w_out.reshape(N_DEV, -1, D_MODEL))]
    (dy_ssm, dp, small["ssm_norm"]), sums = _reduce_start(
        late, lambda comm: _gnorm_bwd(y_ssm, p, rep["ssm_norm"], dyn, comm=comm))
    dp, g_short_conv = _mix_a_bwd(p, short_conv_w, dya_in, dp)
    (dxconv, ddt, dacs, dd_lane), got = _ssd_bwd(
        xconv, dt, acs, rep["d_exp"], hsave, dy_ssm,
        comm=_chip_comm([a for _, a in sums], [n == "ffn2_w_in" for n, _ in sums]))
    recv.update({n: a for (n, _), a in zip(sums, got)})
    small["ssm_D"] = dd_lane.reshape(N_HEADS, HEAD_DIM).sum(axis=1)[None, :]
    dp, g_ssm_conv, small["ssm_conv_b"] = _ssm_conv_bwd(p, ssm_conv_w, rep["ssm_conv_b"], dxconv, dp)
    dp_gd, dbias, dalog = _dt_bwd(p_gd, rep["dt_bias_pad"], rep["a_log_pad"], dt, ddt, dacs, dp_gd)
    small["ssm_dt_bias"] = dbias[:, :N_HEADS]
    small["ssm_A_log"] = dalog[:, :N_HEADS]

    g_main = _mm_tn("proj_main_bwd_w", dp, h2, gdt, tm=1024)
    g_gd = _mm_tn("proj_gd_bwd_w", dp_gd, h2, gdt)
    g_mix = g_main[0:3 * D_MODEL].reshape(4, 3, CONV_CB, D_MODEL).transpose(1, 0, 2, 3).reshape(3 * D_MODEL, D_MODEL)
    g_in_t = jnp.concatenate(
        [g_mix, g_main[3 * D_MODEL:], g_gd[2 * D_MODEL:2 * D_MODEL + N_HEADS],
         g_gd[0:half], g_gd[2 * half:3 * half], g_gd[half:2 * half], g_gd[3 * half:4 * half]], axis=0).reshape(
        N_DEV, IN_SHARD, D_MODEL)
    dh2 = _mm_nn("proj_mix_bwd_x", dp, w_mix_perm, tk=1024, kk=3 * D_MODEL)
    dh2, w_sums = _reduce_start(
        [("w_in", g_in_t)],
        lambda comm: _mm_nn("proj_rest_bwd_x", dp, w_in_t, tk=1024, kk=N_MAIN - 3 * D_MODEL, a_off=3, b_off=3,
                            res=dh2, comm=comm))
    dh2 = _mm_nn("proj_gd_bwd_x", dp_gd, w_gd, res=dh2)
    w_sum = w_sums[0][1]
    (dx1, dx1h, small["mix_norm"]), got0 = _rms_bwd("rms2_bwd", x1, rep["mix_norm"], dh2, dx2, 0.5,
                                                      comm=_chip_comm([w_sum], rows=[W_IN_ROW_CUTS[0]]))

    g_w1_out, got1 = _mm_tn("ffn1_out_bwd_w", act1, dx1h, gdt, tm=FF_HALF,
                            comm=_chip_comm([w_sum], rows=[W_IN_ROW_CUTS[1]]))
    rest = [("ffn1_w_out", g_w1_out.reshape(N_DEV, FF_SHARD // 2, D_MODEL)),
            ("short_conv_w", g_short_conv.reshape(3, N_DEV, -1).transpose(1, 0, 2)),
            ("ssm_conv_w", g_ssm_conv.reshape(4, N_DEV, -1).transpose(1, 0, 2))]
    pair_rest = _pair_comm([a for _, a in rest])
    dact1, got = _mm_nt("ffn1_out_bwd_act", dx1h, w1_out, out_dtype=BF, comm=_join_comm(_chip_comm([w_sum], rows=[W_IN_ROW_CUTS[2]]), pair_rest))
    got2, sib = got[0], got[1:]
    rest_sums = [(n, _add_pairs("pairsum_" + n, a, b)) for (n, a), b in zip(rest, sib)]
    recv["w_in"] = jnp.concatenate([got0[0], got1[0], got2], axis=1)
    dgu1, got = _swiglu_bwd("swiglu1_bwd", gu1, dact1, comm=_chip_comm([a for _, a in rest_sums]))
    recv.update({n: a for (n, _), a in zip(rest_sums, got)})
    g_w1_in = _mm_tn("ffn1_in_bwd_w", dgu1, h1, gdt, tm=FF_HALF)
    g_last = g_w1_in.reshape(N_DEV, FF_SHARD, D_MODEL)
    half = FF_SHARD // 2
    dh1, last_sums = _reduce_start(
        [("ffn1_w_in_a", g_last[:, :half]), ("ffn1_w_in_b", g_last[:, half:])],
        lambda comm: _mm_nn("ffn1_in_bwd_h", dgu1, w1_in, tk=FF_HALF, comm=comm))
    (dx0, _, small["ffn1_norm"]), got = _rms_bwd("rms1_bwd", x, rep["ffn1_norm"], dh1, dx1, 1.0,
                                                  comm=_chip_comm([last_sums[0][1]], [True]))
    pending = {"ffn1_w_in_a": got[0], "ffn1_w_in_b_sums": last_sums[1][1], "packed": _pack_small(small, loss[:, 0:1])}
    return dx0, recv, pending


_SMALL = [("ffn1_norm", 1024), ("mix_norm", 1024), ("ssm_conv_b", 4096), ("ssm_dt_bias", 32), ("ssm_A_log", 32),
          ("ssm_D", 32), ("ssm_norm", 2048), ("ffn2_norm", 1024), ("final_norm", 1024)]
SMALL_W = 10368


def _pack_small(d, loss=None):
    parts = [d[n].reshape(1, -1).astype(F32) for n, _ in _SMALL]
    used = sum(sz for _, sz in _SMALL)
    tail = jnp.zeros((1, SMALL_W - used), F32)
    if loss is not None:
        tail = tail.at[:, 0:1].set(loss)
    return jnp.concatenate(parts + [tail], axis=1)


def _unpack_small(v, shapes):
    out, off = {}, 0
    for n, sz in _SMALL:
        out[n] = v[:, off:off + sz].reshape(shapes[n])
        off += sz
    return out, v[0, off]


_SHARDED = ["ffn1_w_in", "ffn1_w_out", "w_in", "short_conv_w", "short_w_out", "ssm_conv_w", "ssm_w_out", "w_out",
            "ffn2_w_in", "ffn2_w_out"]
_TRANSPOSED = ("ffn1_w_in", "w_in", "ffn2_w_in")
_ORDER = ["ffn1_norm", "ffn1_w_in", "ffn1_w_out", "mix_norm", "w_in", "short_conv_w", "short_w_out", "ssm_conv_w",
          "ssm_conv_b", "ssm_dt_bias", "ssm_A_log", "ssm_D", "ssm_norm", "ssm_w_out", "w_out", "ffn2_norm",
          "ffn2_w_in", "ffn2_w_out", "final_norm"]


def kernel(x, ffn1_norm, ffn1_w_in, ffn1_w_out, mix_norm, w_in, short_conv_w, short_w_out, ssm_conv_w, ssm_conv_b, ssm_dt_bias, ssm_A_log, ssm_D, ssm_norm, ssm_w_out, w_out, ffn2_norm, ffn2_w_in, ffn2_w_out, final_norm, loss_target, m_ffn1_norm, m_ffn1_w_in, m_ffn1_w_out, m_mix_norm, m_w_in, m_short_conv_w, m_short_w_out, m_ssm_conv_w, m_ssm_conv_b, m_ssm_dt_bias, m_ssm_A_log, m_ssm_D, m_ssm_norm, m_ssm_w_out, m_w_out, m_ffn2_norm, m_ffn2_w_in, m_ffn2_w_out, m_final_norm, v_ffn1_norm, v_ffn1_w_in, v_ffn1_w_out, v_mix_norm, v_w_in, v_short_conv_w, v_short_w_out, v_ssm_conv_w, v_ssm_conv_b, v_ssm_dt_bias, v_ssm_A_log, v_ssm_D, v_ssm_norm, v_ssm_w_out, v_w_out, v_ffn2_norm, v_ffn2_w_in, v_ffn2_w_out, v_final_norm):
    w = dict(ffn1_norm=ffn1_norm, ffn1_w_in=ffn1_w_in, ffn1_w_out=ffn1_w_out, mix_norm=mix_norm, w_in=w_in,
             short_conv_w=short_conv_w, short_w_out=short_w_out, ssm_conv_w=ssm_conv_w, ssm_conv_b=ssm_conv_b,
             ssm_dt_bias=ssm_dt_bias, ssm_A_log=ssm_A_log, ssm_D=ssm_D, ssm_norm=ssm_norm, ssm_w_out=ssm_w_out,
             w_out=w_out, ffn2_norm=ffn2_norm, ffn2_w_in=ffn2_w_in, ffn2_w_out=ffn2_w_out, final_norm=final_norm)
    m = dict(ffn1_norm=m_ffn1_norm, ffn1_w_in=m_ffn1_w_in, ffn1_w_out=m_ffn1_w_out, mix_norm=m_mix_norm, w_in=m_w_in,
             short_conv_w=m_short_conv_w, short_w_out=m_short_w_out, ssm_conv_w=m_ssm_conv_w,
             ssm_conv_b=m_ssm_conv_b, ssm_dt_bias=m_ssm_dt_bias, ssm_A_log=m_ssm_A_log, ssm_D=m_ssm_D,
             ssm_norm=m_ssm_norm, ssm_w_out=m_ssm_w_out, w_out=m_w_out, ffn2_norm=m_ffn2_norm,
             ffn2_w_in=m_ffn2_w_in, ffn2_w_out=m_ffn2_w_out, final_norm=m_final_norm)
    v = dict(ffn1_norm=v_ffn1_norm, ffn1_w_in=v_ffn1_w_in, ffn1_w_out=v_ffn1_w_out, mix_norm=v_mix_norm, w_in=v_w_in,
             short_conv_w=v_short_conv_w, short_w_out=v_short_w_out, ssm_conv_w=v_ssm_conv_w,
             ssm_conv_b=v_ssm_conv_b, ssm_dt_bias=v_ssm_dt_bias, ssm_A_log=v_ssm_A_log, ssm_D=v_ssm_D,
             ssm_norm=v_ssm_norm, ssm_w_out=v_ssm_w_out, w_out=v_w_out, ffn2_norm=v_ffn2_norm,
             ffn2_w_in=v_ffn2_w_in, ffn2_w_out=v_ffn2_w_out, final_norm=v_final_norm)
    shapes = {n: w[n].shape for n in _ORDER}

    def local(d, n):
        return d[n][0].T if n in _TRANSPOSED else d[n][0]

    shard = {n: local(w, n) for n in _SHARDED}

    wire = {n: (shard[n] if n in ("short_conv_w", "ssm_conv_w") else shard[n].astype(BF)) for n in _SHARDED}
    rep = {
        "ffn1_norm": ffn1_norm, "mix_norm": mix_norm, "ffn2_norm": ffn2_norm, "ssm_norm": ssm_norm,
        "ssm_conv_b": ssm_conv_b, "final_norm": final_norm.reshape(1, D_MODEL),
        "dt_bias_pad": _pad_lanes(ssm_dt_bias, DT_W), "a_log_pad": _pad_lanes(ssm_A_log, DT_W),
        "d_exp": jnp.repeat(ssm_D, HEAD_DIM, axis=1),
    }
    grad_x, parts, pending = _train_step(x[0], loss_target[0], wire, rep)

    hosted = {"w_in": _chip_comm([pending["ffn1_w_in_b_sums"]], [True]), "ffn2_w_in": _gather_comm([pending["packed"]])}
    order = ["w_in", "ffn2_w_in"] + [n for n in _SHARDED if n not in ("w_in", "ffn2_w_in")]
    out_g, out_d, out_m, out_v = {}, {}, {}, {}
    for n in order:
        if n in hosted:
            res, got = _adamw("adamw_" + n, parts[n], shard[n], local(m, n), local(v, n), comm=hosted[n])
            if n == "w_in":
                parts["ffn1_w_in"] = jnp.concatenate([pending["ffn1_w_in_a"], got[0]], axis=1)
            else:
                small_parts = got[0]
        else:
            res = _adamw("adamw_" + n, parts[n], shard[n], local(m, n), local(v, n))
        out_g[n], out_d[n], out_m[n], out_v[n] = [(r.T if n in _TRANSPOSED else r).reshape(shapes[n]) for r in res]
    sres = _adamw("adamw_small", small_parts, _pack_small(w), _pack_small(m), _pack_small(v))
    sg, loss = _unpack_small(sres[0], shapes)
    sd, _ = _unpack_small(sres[1], shapes)
    sm, _ = _unpack_small(sres[2], shapes)
    sv, _ = _unpack_small(sres[3], shapes)
    out_g.update(sg)
    out_d.update(sd)
    out_m.update(sm)
    out_v.update(sv)
    return (loss, grad_x[None], *[out_g[n] for n in _ORDER], *[out_d[n] for n in _ORDER],
            *[out_m[n] for n in _ORDER], *[out_v[n] for n in _ORDER])
```

```python
import functools

import jax
import jax.numpy as jnp
from jax import lax
from jax.experimental import pallas as pl
from jax.experimental.pallas import tpu as pltpu

F32 = jnp.float32
BF = jnp.bfloat16

N_DEV = 8
D_MODEL = 1024
D_FF = 2816
D_INNER = 2048
D_XBC = 4096
N_HEADS = 32
HEAD_DIM = 64
N_GROUPS = 8
D_STATE = 128
CHUNK = 64
GROUP_W = D_INNER // N_GROUPS
HEADS_PER_GROUP = N_HEADS // N_GROUPS
NORM_EPS = 1e-5
N_IN = 11296
FF_SHARD = 2 * D_FF // N_DEV
FF_HALF = D_FF // 2
IN_SHARD = N_IN // N_DEV

OFF_B, OFF_C, OFF_XA, OFF_Z, OFF_XBC = 0, 1024, 2048, 3072, 5120
N_MAIN = 9216
OFF_GA, OFF_GB, OFF_DT = 0, 1024, 2048
DT_W = 128
N_GD = 2048 + DT_W
W_IN_ROW_CUTS = [(0, 480), (480, 944), (944, 1412)]
W_GRAD_ROW_CUTS = [(0, 512), (512, 720), (720, 896), (896, 1152), (1152, 1412)]

ADAM_LR, ADAM_B1, ADAM_B2, ADAM_EPS, ADAM_WD, ADAM_STEP = 0.001, 0.9, 0.999, 1e-08, 0.01, 10

VMEM_LIMIT_V7X = 56 * 1024 * 1024
TM = 1024
TE = 512
ADAM_COL_TILE = 256


def _params(*sem):
    return pltpu.CompilerParams(dimension_semantics=sem, vmem_limit_bytes=VMEM_LIMIT_V7X)


_DIMS = {
    "nn": (((1,), (0,)), ((), ())),
    "nt": (((1,), (1,)), ((), ())),
    "tn": (((0,), (0,)), ((), ())),
}


def _dot(a, b, mode="nn"):
    return lax.dot_general(a, b, _DIMS[mode], preferred_element_type=F32)


def _sigmoid(x):
    return 1.0 / (1.0 + jnp.exp(-x))


class _Comm:
    def __init__(self, inputs, out_shapes, sems, start, finish):
        self.inputs, self.out_shapes, self.sems, self.start, self.finish = inputs, out_shapes, sems, start, finish


def _pcall(name, body, grid, in_specs, out_specs, out_shape, args, scratch=(), sem=None, comm=None):
    single = not isinstance(out_shape, (list, tuple))
    out_shapes = [out_shape] if single else list(out_shape)
    out_specs = [out_specs] if single else list(out_specs)
    n_in, n_out, n_scr = len(args), len(out_shapes), len(scratch)
    if comm is None:
        res = pl.pallas_call(
            body, name=name, grid=grid, in_specs=list(in_specs), out_specs=out_specs, out_shape=out_shapes,
            scratch_shapes=list(scratch), compiler_params=_params(*sem))(*args)
        return (res[0] if single else res), []
    nci, nco = len(comm.inputs), len(comm.out_shapes)

    def wrapped(*refs):
        a = refs[:n_in]
        ci = refs[n_in:n_in + nci]
        o0 = n_in + nci
        o = refs[o0:o0 + n_out]
        co = refs[o0 + n_out:o0 + n_out + nco]
        s0 = o0 + n_out + nco
        s = refs[s0:s0 + n_scr]
        cs = refs[s0 + n_scr:]
        pids = [pl.program_id(i) for i in range(len(grid))]
        first = functools.reduce(jnp.logical_and, [p == 0 for p in pids])
        last = functools.reduce(jnp.logical_and, [p == g - 1 for p, g in zip(pids, grid)])

        @pl.when(first)
        def _():
            comm.start(ci, co, cs)

        body(*a, *o, *s)

        @pl.when(last)
        def _():
            comm.finish(ci, co, cs)

    any_spec = pl.BlockSpec(memory_space=pl.ANY)
    res = pl.pallas_call(
        wrapped, name=name, grid=grid, in_specs=list(in_specs) + [any_spec] * nci,
        out_specs=out_specs + [any_spec] * nco, out_shape=out_shapes + list(comm.out_shapes),
        scratch_shapes=list(scratch) + list(comm.sems),
        compiler_params=_params(*(("arbitrary",) * len(grid))))(*args, *comm.inputs)
    core = res[:n_out]
    return (core[0] if single else core), list(res[n_out:])


def _comm_call(name, comm):
    nci, nco = len(comm.inputs), len(comm.out_shapes)

    def body(*refs):
        ci, co, cs = refs[:nci], refs[nci:nci + nco], refs[nci + nco:]
        comm.start(ci, co, cs)
        comm.finish(ci, co, cs)

    any_spec = pl.BlockSpec(memory_space=pl.ANY)
    return pl.pallas_call(
        body, name=name, in_specs=[any_spec] * nci, out_specs=[any_spec] * nco, out_shape=list(comm.out_shapes),
        scratch_shapes=list(comm.sems), compiler_params=pltpu.CompilerParams(has_side_effects=True))(*comm.inputs)


def _remote(src, dst, ssem, rsem, dev):
    return pltpu.make_async_remote_copy(src_ref=src, dst_ref=dst, send_sem=ssem, recv_sem=rsem, device_id=dev,
                                        device_id_type=pl.DeviceIdType.MESH)


def _place():
    x, y, c = lax.axis_index("x"), lax.axis_index("y"), lax.axis_index("c")
    other_chips = [(1 - x, y), (x, 1 - y), (1 - x, 1 - y)]
    return x, y, c, other_chips


def _slot(x, y, c, swap):
    return 4 * y + 2 * x + c if swap else 4 * x + 2 * y + c


def _chip_slot(x, y, swap):
    return 2 * y + x if swap else 2 * x + y


def _gather_comm(shards, swaps=None):
    n = len(shards)
    per = N_DEV - 1
    swaps = [False] * n if swaps is None else swaps

    def start(ins, outs, sems):
        send, recv, loc = sems
        x, y, c, chips = _place()
        for i in range(n):
            me = _slot(x, y, c, swaps[i])
            pltpu.make_async_copy(ins[i], outs[i].at[me], loc.at[i]).start()
            _remote(ins[i], outs[i].at[me], send.at[per * i], recv.at[per * i], (x, y, 1 - c)).start()
            for j, (qx, qy) in enumerate(chips):
                _remote(ins[i], outs[i].at[me], send.at[per * i + 1 + j], recv.at[per * i + 1 + j], (qx, qy, c)).start()

    def finish(ins, outs, sems):
        send, recv, loc = sems
        x, y, c, chips = _place()
        sib = (x, y, 1 - c)
        for i in range(n):
            for j, (qx, qy) in enumerate(chips):
                blk = outs[i].at[_slot(qx, qy, c, swaps[i])]
                _remote(blk, blk, send.at[per * i + 1 + j], recv.at[per * i + 1 + j], (qx, qy, c)).wait_recv()
                _remote(blk, blk, send.at[per * i + 4 + j], recv.at[per * i + 4 + j], sib).start()
        for i in range(n):
            blk = outs[i].at[_slot(x, y, 1 - c, swaps[i])]
            _remote(blk, blk, send.at[per * i], recv.at[per * i], sib).wait_recv()
            for j, (qx, qy) in enumerate(chips):
                blk = outs[i].at[_slot(qx, qy, 1 - c, swaps[i])]
                _remote(blk, blk, send.at[per * i + 4 + j], recv.at[per * i + 4 + j], sib).wait_recv()
        for i in range(n):
            own = outs[i].at[_slot(x, y, c, swaps[i])]
            for k in range(per):
                _remote(ins[i], own, send.at[per * i + k], recv.at[per * i + k], sib).wait_send()
            pltpu.make_async_copy(ins[i], own, loc.at[i]).wait()

    out_shapes = [jax.ShapeDtypeStruct((N_DEV,) + tuple(a.shape), a.dtype) for a in shards]
    sems = [pltpu.SemaphoreType.DMA((per * n,)), pltpu.SemaphoreType.DMA((per * n,)), pltpu.SemaphoreType.DMA((n,))]
    return _Comm(list(shards), out_shapes, sems, start, finish)


def _pair_comm(slots):
    n = len(slots)

    def copies(ins, outs, sems):
        send, recv = sems
        x, y, c, _ = _place()
        sib = (x, y, 1 - c)
        out = []
        for i in range(n):
            for q in range(4):
                out.append(_remote(ins[i].at[2 * q + 1 - c], outs[i].at[q], send.at[4 * i + q], recv.at[4 * i + q], sib))
        return out

    def start(ins, outs, sems):
        for cp in copies(ins, outs, sems):
            cp.start()

    def finish(ins, outs, sems):
        for cp in copies(ins, outs, sems):
            cp.wait_send()
            cp.wait_recv()

    out_shapes = [jax.ShapeDtypeStruct((4,) + tuple(a.shape[1:]), a.dtype) for a in slots]
    sems = [pltpu.SemaphoreType.DMA((4 * n,)), pltpu.SemaphoreType.DMA((4 * n,))]
    return _Comm(list(slots), out_shapes, sems, start, finish)


def _chip_comm(chip_sums, swaps=None, rows=None):
    n = len(chip_sums)
    swaps = [False] * n if swaps is None else swaps
    rows = [None] * n if rows is None else rows

    def src(ins, i, q):
        return ins[i].at[q] if rows[i] is None else ins[i].at[q, pl.ds(rows[i][0], rows[i][1] - rows[i][0])]

    def start(ins, outs, sems):
        send, recv, loc = sems
        x, y, c, chips = _place()
        for i in range(n):
            mine = _chip_slot(x, y, swaps[i])
            pltpu.make_async_copy(src(ins, i, mine), outs[i].at[mine], loc.at[i]).start()
            for j, (qx, qy) in enumerate(chips):
                _remote(src(ins, i, _chip_slot(qx, qy, swaps[i])), outs[i].at[mine], send.at[3 * i + j],
                        recv.at[3 * i + j], (qx, qy, c)).start()

    def finish(ins, outs, sems):
        send, recv, loc = sems
        x, y, c, chips = _place()
        for i in range(n):
            mine = _chip_slot(x, y, swaps[i])
            for j, (qx, qy) in enumerate(chips):
                theirs = _chip_slot(qx, qy, swaps[i])
                cp = _remote(src(ins, i, theirs), outs[i].at[theirs], send.at[3 * i + j], recv.at[3 * i + j], (qx, qy, c))
                cp.wait_send()
                cp.wait_recv()
            pltpu.make_async_copy(src(ins, i, mine), outs[i].at[mine], loc.at[i]).wait()

    def out_shape(a, r):
        shape = a.shape if r is None else (a.shape[0], r[1] - r[0]) + tuple(a.shape[2:])
        return jax.ShapeDtypeStruct(shape, a.dtype)

    out_shapes = [out_shape(a, r) for a, r in zip(chip_sums, rows)]
    sems = [pltpu.SemaphoreType.DMA((3 * n,)), pltpu.SemaphoreType.DMA((3 * n,)), pltpu.SemaphoreType.DMA((n,))]
    return _Comm(list(chip_sums), out_shapes, sems, start, finish)


def _join_comm(a, b):
    na_i, na_o, na_s = len(a.inputs), len(a.out_shapes), len(a.sems)

    def start(ins, outs, sems):
        a.start(ins[:na_i], outs[:na_o], sems[:na_s])
        b.start(ins[na_i:], outs[na_o:], sems[na_s:])

    def finish(ins, outs, sems):
        a.finish(ins[:na_i], outs[:na_o], sems[:na_s])
        b.finish(ins[na_i:], outs[na_o:], sems[na_s:])

    return _Comm(a.inputs + b.inputs, a.out_shapes + b.out_shapes, a.sems + b.sems, start, finish)


def _row_tile(r):
    for cand in (256, 128):
        if r > cand and r % cand == 0:
            return cand
    return r


def _add_pairs(name, slots, sib):
    r, c = slots.shape[1:]
    tr = _row_tile(r)

    def body(core_ref, s_ref, b_ref, o_ref):
        o_ref[...] = (s_ref[...].astype(F32) + b_ref[...].astype(F32)).astype(o_ref.dtype)

    core = jnp.full((1,), lax.axis_index("c"), jnp.int32)
    return pl.pallas_call(
        body, name=name,
        grid_spec=pltpu.PrefetchScalarGridSpec(
            num_scalar_prefetch=1, grid=(4, r // tr),
            in_specs=[pl.BlockSpec((None, None, tr, c), lambda q, i, core_ref: (q, core_ref[0], i, 0)),
                      pl.BlockSpec((None, tr, c), lambda q, i, core_ref: (q, i, 0))],
            out_specs=pl.BlockSpec((None, tr, c), lambda q, i, core_ref: (q, i, 0))),
        out_shape=jax.ShapeDtypeStruct((4, r, c), slots.dtype),
        compiler_params=_params("parallel", "parallel"))(core, slots.reshape(4, 2, r, c), sib)


def _matmul(name, mode, a, b, grid, a_spec, b_spec, o_spec, out_shape, acc_shape,
            res=None, res_spec=None, alpha=1.0, comm=None):
    nk = grid[-1]
    has_res = res is not None

    def body(*refs):
        if has_res:
            a_ref, b_ref, r_ref, o_ref = refs[:4]
        else:
            a_ref, b_ref, o_ref = refs[:3]
            r_ref = None
        part = _dot(a_ref[...], b_ref[...], mode)

        def finish(v):
            if alpha != 1.0:
                v = v * alpha
            if has_res:
                v = r_ref[...] + v
            o_ref[...] = v.astype(o_ref.dtype)

        if nk == 1:
            finish(part)
        else:
            acc = refs[-1]
            k = pl.program_id(len(grid) - 1)

            @pl.when(k == 0)
            def _():
                acc[...] = part

            @pl.when(k > 0)
            def _():
                acc[...] += part

            @pl.when(k == nk - 1)
            def _():
                finish(acc[...])

    in_specs = [a_spec, b_spec] + ([res_spec] if has_res else [])
    args = (a, b) + ((res,) if has_res else ())
    scratch = [] if nk == 1 else [pltpu.VMEM(acc_shape, F32)]
    sem = ("parallel",) * (len(grid) - 1) + ("arbitrary",)
    out, couts = _pcall(name, body, grid, in_specs, o_spec, out_shape, args, scratch, sem, comm)
    return out if comm is None else (out, couts)


def _mm_nn(name, a, b, out_dtype=F32, res=None, alpha=1.0, tk=None, kk=None, a_off=0, b_off=0, comm=None):
    t = a.shape[0]
    kk = a.shape[1] if kk is None else kk
    n = b.shape[1]
    tk = kk if tk is None else tk
    grid = (t // TM, 1, kk // tk)
    return _matmul(
        name, "nn", a, b, grid,
        pl.BlockSpec((TM, tk), lambda i, j, k: (i, k + a_off)),
        pl.BlockSpec((tk, n), lambda i, j, k: (k + b_off, 0)),
        pl.BlockSpec((TM, n), lambda i, j, k: (i, 0)),
        jax.ShapeDtypeStruct((t, n), out_dtype), (TM, n),
        res=res, res_spec=pl.BlockSpec((TM, n), lambda i, j, k: (i, 0)), alpha=alpha, comm=comm)


def _mm_nt(name, a, b, n=None, tn=None, tk=None, out_dtype=F32, comm=None):
    t, kk = a.shape
    n = b.shape[0] if n is None else n
    tn = n if tn is None else tn
    tk = kk if tk is None else tk
    grid = (n // tn, t // TM, kk // tk)
    return _matmul(
        name, "nt", a, b, grid,
        pl.BlockSpec((TM, tk), lambda j, i, k: (i, k)),
        pl.BlockSpec((tn, tk), lambda j, i, k: (j, k)),
        pl.BlockSpec((TM, tn), lambda j, i, k: (i, j)),
        jax.ShapeDtypeStruct((t, n), out_dtype), (TM, tn), comm=comm)


def _mm_tn(name, a, b, out_dtype, tm=None, n=None, col_off=0, comm=None):
    t, m = a.shape
    n = b.shape[1] if n is None else n
    tm = m if tm is None else tm
    grid = (m // tm, 1, t // TM)
    return _matmul(
        name, "tn", a, b, grid,
        pl.BlockSpec((TM, tm), lambda j, i, k: (k, j)),
        pl.BlockSpec((TM, n), lambda j, i, k: (k, col_off)),
        pl.BlockSpec((tm, n), lambda j, i, k: (j, 0)),
        jax.ShapeDtypeStruct((m, n), out_dtype), (tm, n), comm=comm)


def _rms_fwd(name, x, w):
    t, d = x.shape

    def body(x_ref, w_ref, h_ref):
        xv = x_ref[...]
        rstd = lax.rsqrt(jnp.mean(xv * xv, axis=-1, keepdims=True) + NORM_EPS)
        h_ref[...] = (xv * rstd * w_ref[...]).astype(h_ref.dtype)

    return pl.pallas_call(
        body, name=name, grid=(t // TE,),
        in_specs=[pl.BlockSpec((TE, d), lambda i: (i, 0)), pl.BlockSpec((1, d), lambda i: (0, 0))],
        out_specs=pl.BlockSpec((TE, d), lambda i: (i, 0)),
        out_shape=jax.ShapeDtypeStruct((t, d), BF), compiler_params=_params("parallel"))(x, w)


def _rms_bwd(name, x, w, dh, dres, out_scale, comm=None):
    t, d = x.shape

    def body(x_ref, w_ref, dh_ref, dres_ref, dx_ref, dxb_ref, dw_ref):
        i = pl.program_id(0)
        xv = x_ref[...]
        rstd = lax.rsqrt(jnp.mean(xv * xv, axis=-1, keepdims=True) + NORM_EPS)
        xhat = xv * rstd
        dhv = dh_ref[...]
        wd = dhv * w_ref[...]
        proj = jnp.mean(wd * xhat, axis=-1, keepdims=True)
        dx = dres_ref[...] + rstd * (wd - xhat * proj)
        dx_ref[...] = dx
        dxb_ref[...] = (dx * out_scale).astype(BF)
        part = jnp.sum(dhv * xhat, axis=0, keepdims=True)

        @pl.when(i == 0)
        def _():
            dw_ref[...] = part

        @pl.when(i > 0)
        def _():
            dw_ref[...] += part

    row = pl.BlockSpec((TE, d), lambda i: (i, 0))
    vec = pl.BlockSpec((1, d), lambda i: (0, 0))
    outs, couts = _pcall(
        name, body, (t // TE,), [row, vec, row, row], [row, row, vec],
        [jax.ShapeDtypeStruct((t, d), F32), jax.ShapeDtypeStruct((t, d), BF), jax.ShapeDtypeStruct((1, d), F32)],
        (x, w, dh, dres), (), ("arbitrary",), comm)
    return outs if comm is None else (outs, couts)


def _final_loss(x, w, target):
    t, d = x.shape

    def body(x_ref, w_ref, t_ref, loss_ref, dx_ref, dxb_ref, dw_ref):
        i = pl.program_id(0)
        xv = x_ref[...]
        rstd = lax.rsqrt(jnp.mean(xv * xv, axis=-1, keepdims=True) + NORM_EPS)
        xhat = xv * rstd
        err = xhat * w_ref[...] - t_ref[...]
        lpart = 0.5 * jnp.sum(jnp.mean(err * err, axis=-1, keepdims=True), axis=0, keepdims=True)
        dy = err * (1.0 / d)
        wd = dy * w_ref[...]
        proj = jnp.mean(wd * xhat, axis=-1, keepdims=True)
        dx = rstd * (wd - xhat * proj)
        dx_ref[...] = dx
        dxb_ref[...] = (0.5 * dx).astype(BF)
        part = jnp.sum(dy * xhat, axis=0, keepdims=True)
        lfull = jnp.broadcast_to(lpart, (1, 128))

        @pl.when(i == 0)
        def _():
            dw_ref[...] = part
            loss_ref[...] = lfull

        @pl.when(i > 0)
        def _():
            dw_ref[...] += part
            loss_ref[...] += lfull

    row = pl.BlockSpec((TE, d), lambda i: (i, 0))
    vec = pl.BlockSpec((1, d), lambda i: (0, 0))
    return pl.pallas_call(
        body, name="final_loss", grid=(t // TE,), in_specs=[row, vec, row],
        out_specs=[pl.BlockSpec((1, 128), lambda i: (0, 0)), row, row, vec],
        out_shape=[jax.ShapeDtypeStruct((1, 128), F32), jax.ShapeDtypeStruct((t, d), F32),
                   jax.ShapeDtypeStruct((t, d), BF), jax.ShapeDtypeStruct((1, d), F32)],
        compiler_params=_params("arbitrary"))(x, w, target)


def _swiglu_fwd(name, gu, comm=None):
    t = gu.shape[0]

    def body(g_ref, u_ref, a_ref):
        g = g_ref[...].astype(F32)
        a_ref[...] = (g * _sigmoid(g) * u_ref[...].astype(F32)).astype(BF)

    blk = (TE, FF_HALF)
    out, couts = _pcall(
        name, body, (t // TE, 2),
        [pl.BlockSpec(blk, lambda i, j: (i, 2 * j)), pl.BlockSpec(blk, lambda i, j: (i, 2 * j + 1))],
        pl.BlockSpec(blk, lambda i, j: (i, j)), jax.ShapeDtypeStruct((t, D_FF), BF),
        (gu, gu), (), ("parallel", "parallel"), comm)
    return out if comm is None else (out, couts)


def _swiglu_bwd(name, gu, dact, comm=None):
    t = gu.shape[0]

    def body(g_ref, u_ref, da_ref, o_ref):
        g = g_ref[...].astype(F32)
        da = da_ref[...].astype(F32)
        s = _sigmoid(g)
        o_ref[:, 0:FF_HALF] = (da * u_ref[...].astype(F32) * (s * (1.0 + g * (1.0 - s)))).astype(BF)
        o_ref[:, FF_HALF:2 * FF_HALF] = (da * g * s).astype(BF)

    blk = (TE, FF_HALF)
    out, couts = _pcall(
        name, body, (t // TE, 2),
        [pl.BlockSpec(blk, lambda i, j: (i, 2 * j)), pl.BlockSpec(blk, lambda i, j: (i, 2 * j + 1)),
         pl.BlockSpec(blk, lambda i, j: (i, j))],
        pl.BlockSpec((TE, 2 * FF_HALF), lambda i, j: (i, j)),
        jax.ShapeDtypeStruct((t, 2 * D_FF), BF), (gu, gu, dact), (), ("parallel", "parallel"), comm)
    return out if comm is None else (out, couts)


CONV_CB = 256


CONV_ROWS = 64
CONV_HALO = 16


def _taps_down(ext, w, k):
    shifted = [pltpu.roll(ext, k - 1 - j, 0)[CONV_HALO:] for j in range(k - 1)] + [ext[CONV_HALO:]]
    out = shifted[k - 1] * w[k - 1:k, :]
    for j in range(k - 1):
        out = out + shifted[j] * w[j:j + 1, :]
    return out, shifted


def _taps_up(ext, w, k):
    rows = ext.shape[0]
    n = rows - CONV_HALO
    out = ext[:n] * w[k - 1:k, :]
    for j in range(k - 1):
        out = out + pltpu.roll(ext, rows - (k - 1 - j), 0)[:n] * w[j:j + 1, :]
    return out


def _rows_before(ref, i, r0):
    start = pl.multiple_of(jnp.maximum(r0 - CONV_HALO, 0), CONV_HALO)
    return jnp.where(i > 0, ref[pl.ds(start, CONV_HALO), :].astype(F32), 0.0)


def _rows_after(ref, r0, t):
    start = pl.multiple_of(jnp.minimum(r0 + CONV_ROWS, t - CONV_HALO), CONV_HALO)
    return ref[pl.ds(start, CONV_HALO), :].astype(F32)


def _fold8(v):
    return v.reshape(v.shape[0] // 8, 8, v.shape[1]).sum(axis=0)


def _silu_grad(pre):
    s = _sigmoid(pre)
    return s * (1.0 + pre * (1.0 - s))


def _pspec(t, off):
    base = off // CONV_CB
    return pl.BlockSpec((t, CONV_CB), lambda j: (0, base + j))


def _mix_a_fwd(p, conv_w):
    t = p.shape[0]

    def body(b_ref, c_ref, xa_ref, w_ref, o_ref):
        w = w_ref[...]

        def step(i, carry):
            r0 = pl.multiple_of(i * CONV_ROWS, CONV_ROWS)
            rows = pl.ds(r0, CONV_ROWS)
            q = c_ref[rows, :].astype(F32) * xa_ref[rows, :].astype(F32)
            q_before = _rows_before(c_ref, i, r0) * _rows_before(xa_ref, i, r0)
            va, _ = _taps_down(jnp.concatenate([q_before, q], axis=0), w, 3)
            o_ref[rows, :] = (b_ref[rows, :].astype(F32) * va).astype(BF)
            return carry

        lax.fori_loop(0, t // CONV_ROWS, step, 0)

    return pl.pallas_call(
        body, name="mix_a_fwd", grid=(D_MODEL // CONV_CB,),
        in_specs=[_pspec(t, OFF_B), _pspec(t, OFF_C), _pspec(t, OFF_XA),
                  pl.BlockSpec((3, CONV_CB), lambda j: (0, j))],
        out_specs=pl.BlockSpec((t, CONV_CB), lambda j: (0, j)),
        out_shape=jax.ShapeDtypeStruct((t, D_MODEL), BF), compiler_params=_params("parallel"))(p, p, p, conv_w)


def _mix_a_bwd(p, conv_w, dya, dp):
    t = p.shape[0]

    def body(b_ref, c_ref, xa_ref, w_ref, dy_ref, dp_in, dp_ref, dw_ref):
        del dp_in
        w = w_ref[...]
        n = t // CONV_ROWS

        def step(i, acc):
            r0 = pl.multiple_of(i * CONV_ROWS, CONV_ROWS)
            rows = pl.ds(r0, CONV_ROWS)
            cv = c_ref[rows, :].astype(F32)
            xav = xa_ref[rows, :].astype(F32)
            q_before = _rows_before(c_ref, i, r0) * _rows_before(xa_ref, i, r0)
            va, shifted = _taps_down(jnp.concatenate([q_before, cv * xav], axis=0), w, 3)
            dyv = dy_ref[rows, :]
            dp_ref[rows, 0:CONV_CB] = (dyv * va).astype(BF)
            dv = dyv * b_ref[rows, :].astype(F32)
            dv_after = jnp.where(i < n - 1, _rows_after(dy_ref, r0, t) * _rows_after(b_ref, r0, t), 0.0)
            dq = _taps_up(jnp.concatenate([dv, dv_after], axis=0), w, 3)
            dp_ref[rows, CONV_CB:2 * CONV_CB] = (dq * xav).astype(BF)
            dp_ref[rows, 2 * CONV_CB:3 * CONV_CB] = (dq * cv).astype(BF)
            return tuple(a + _fold8(dv * s) for a, s in zip(acc, shifted))

        zero = jnp.zeros((8, CONV_CB), F32)
        acc = lax.fori_loop(0, n, step, (zero, zero, zero))
        for j in range(3):
            dw_ref[j:j + 1, :] = jnp.sum(acc[j], axis=0, keepdims=True)

    col = pl.BlockSpec((t, CONV_CB), lambda j: (0, j))
    wsp = pl.BlockSpec((3, CONV_CB), lambda j: (0, j))
    return pl.pallas_call(
        body, name="mix_a_bwd", grid=(D_MODEL // CONV_CB,),
        in_specs=[_pspec(t, OFF_B), _pspec(t, OFF_C), _pspec(t, OFF_XA), wsp, col, pl.BlockSpec(memory_space=pl.ANY)],
        out_specs=[pl.BlockSpec((t, 3 * CONV_CB), lambda j: (0, j)), wsp],
        out_shape=[jax.ShapeDtypeStruct(dp.shape, dp.dtype), jax.ShapeDtypeStruct((3, D_MODEL), F32)],
        input_output_aliases={5: 0},
        compiler_params=_params("parallel"))(p, p, p, conv_w, dya, dp)


def _ssm_conv_fwd(p, conv_w, conv_b):
    t = p.shape[0]

    def body(x_ref, w_ref, b_ref, o_ref):
        w = w_ref[...]
        bias = b_ref[...]

        def step(i, carry):
            r0 = pl.multiple_of(i * CONV_ROWS, CONV_ROWS)
            rows = pl.ds(r0, CONV_ROWS)
            ext = jnp.concatenate([_rows_before(x_ref, i, r0), x_ref[rows, :].astype(F32)], axis=0)
            pre = _taps_down(ext, w, 4)[0] + bias
            o_ref[rows, :] = pre * _sigmoid(pre)
            return carry

        lax.fori_loop(0, t // CONV_ROWS, step, 0)

    return pl.pallas_call(
        body, name="ssm_conv_fwd", grid=(D_XBC // CONV_CB,),
        in_specs=[_pspec(t, OFF_XBC), pl.BlockSpec((4, CONV_CB), lambda j: (0, j)),
                  pl.BlockSpec((1, CONV_CB), lambda j: (0, j))],
        out_specs=pl.BlockSpec((t, CONV_CB), lambda j: (0, j)),
        out_shape=jax.ShapeDtypeStruct((t, D_XBC), F32), compiler_params=_params("parallel"))(p, conv_w, conv_b)


def _ssm_conv_bwd(p, conv_w, conv_b, dxc, dp):
    t = p.shape[0]

    def body(x_ref, w_ref, b_ref, d_ref, dp_in, dx_ref, dw_ref, db_ref):
        del dp_in
        w = w_ref[...]
        bias = b_ref[...]
        n = t // CONV_ROWS

        def step(i, acc):
            r0 = pl.multiple_of(i * CONV_ROWS, CONV_ROWS)
            rows = pl.ds(r0, CONV_ROWS)
            x_cur = x_ref[rows, :].astype(F32)
            pre, shifted = _taps_down(jnp.concatenate([_rows_before(x_ref, i, r0), x_cur], axis=0), w, 4)
            pre = pre + bias
            dpre = d_ref[rows, :] * _silu_grad(pre)
            ext_after = jnp.concatenate([x_cur[CONV_ROWS - CONV_HALO:], _rows_after(x_ref, r0, t)], axis=0)
            pre_after = _taps_down(ext_after, w, 4)[0] + bias
            dpre_after = jnp.where(i < n - 1, _rows_after(d_ref, r0, t) * _silu_grad(pre_after), 0.0)
            dx_ref[rows, :] = _taps_up(jnp.concatenate([dpre, dpre_after], axis=0), w, 4).astype(BF)
            new = tuple(a + _fold8(dpre * s) for a, s in zip(acc[:4], shifted))
            return new + (acc[4] + _fold8(dpre),)

        zero = jnp.zeros((8, CONV_CB), F32)
        acc = lax.fori_loop(0, n, step, (zero,) * 5)
        for j in range(4):
            dw_ref[j:j + 1, :] = jnp.sum(acc[j], axis=0, keepdims=True)
        db_ref[...] = jnp.sum(acc[4], axis=0, keepdims=True)

    col = pl.BlockSpec((t, CONV_CB), lambda j: (0, j))
    wsp = pl.BlockSpec((4, CONV_CB), lambda j: (0, j))
    bsp = pl.BlockSpec((1, CONV_CB), lambda j: (0, j))
    return pl.pallas_call(
        body, name="ssm_conv_bwd", grid=(D_XBC // CONV_CB,),
        in_specs=[_pspec(t, OFF_XBC), wsp, bsp, col, pl.BlockSpec(memory_space=pl.ANY)],
        out_specs=[_pspec(t, OFF_XBC), wsp, bsp],
        out_shape=[jax.ShapeDtypeStruct(dp.shape, dp.dtype), jax.ShapeDtypeStruct((4, D_XBC), F32),
                   jax.ShapeDtypeStruct((1, D_XBC), F32)],
        input_output_aliases={4: 0},
        compiler_params=_params("parallel"))(p, conv_w, conv_b, dxc, dp)


DT_ROWS = 512


def _tri(lower):
    r = lax.broadcasted_iota(jnp.int32, (CHUNK, CHUNK), 0)
    c = lax.broadcasted_iota(jnp.int32, (CHUNK, CHUNK), 1)
    return jnp.where((r >= c) if lower else (r <= c), 1.0, 0.0).astype(F32)


def _dot_exact(a, b):
    return lax.dot_general(a, b, _DIMS["nn"], preferred_element_type=F32, precision=lax.Precision.HIGHEST)


def _dt_fwd(p, bias_pad, alog_pad):
    t = p.shape[0]

    def body(raw_ref, b_ref, al_ref, dt_ref, acs_ref):
        z = raw_ref[...] + b_ref[...]
        dt = jnp.maximum(z, 0.0) + jnp.log(1.0 + jnp.exp(-jnp.abs(z)))
        dt_ref[...] = dt
        a = dt * (-jnp.exp(al_ref[...]))
        tri = _tri(True)
        for k in range(DT_ROWS // CHUNK):
            acs_ref[k * CHUNK:(k + 1) * CHUNK, :] = _dot_exact(tri, a[k * CHUNK:(k + 1) * CHUNK, :])

    blk = pl.BlockSpec((DT_ROWS, DT_W), lambda i: (i, 0))
    vec = pl.BlockSpec((1, DT_W), lambda i: (0, 0))
    return pl.pallas_call(
        body, name="dt_fwd", grid=(t // DT_ROWS,),
        in_specs=[pl.BlockSpec((DT_ROWS, DT_W), lambda i: (i, OFF_DT // DT_W)), vec, vec],
        out_specs=[blk, blk], out_shape=[jax.ShapeDtypeStruct((t, DT_W), F32)] * 2,
        compiler_params=_params("parallel"))(p, bias_pad, alog_pad)


def _dt_bwd(p, bias_pad, alog_pad, dt, ddt, dacs, dp_gd):
    t = p.shape[0]

    def body(raw_ref, b_ref, al_ref, dt_ref, ddt_ref, dacs_ref, dp_in, draw_ref, db_ref, dal_ref):
        del dp_in
        i = pl.program_id(0)
        acoef = -jnp.exp(al_ref[...])
        triu = _tri(False)
        das = []
        for k in range(DT_ROWS // CHUNK):
            das.append(_dot_exact(triu, dacs_ref[k * CHUNK:(k + 1) * CHUNK, :]))
        da = jnp.concatenate(das, axis=0)
        dtv = dt_ref[...]
        ddt_tot = ddt_ref[...] + da * acoef
        lane = lax.broadcasted_iota(jnp.int32, (DT_ROWS, DT_W), 1)
        draw = jnp.where(lane < N_HEADS, ddt_tot * _sigmoid(raw_ref[...] + b_ref[...]), 0.0)
        draw_ref[...] = draw.astype(BF)
        pb = jnp.sum(draw, axis=0, keepdims=True)
        pa = jnp.sum(da * dtv * acoef, axis=0, keepdims=True)

        @pl.when(i == 0)
        def _():
            db_ref[...] = pb
            dal_ref[...] = pa

        @pl.when(i > 0)
        def _():
            db_ref[...] += pb
            dal_ref[...] += pa

    blk = pl.BlockSpec((DT_ROWS, DT_W), lambda i: (i, 0))
    vec = pl.BlockSpec((1, DT_W), lambda i: (0, 0))
    return pl.pallas_call(
        body, name="dt_bwd", grid=(t // DT_ROWS,),
        in_specs=[pl.BlockSpec((DT_ROWS, DT_W), lambda i: (i, OFF_DT // DT_W)), vec, vec, blk, blk, blk,
                  pl.BlockSpec(memory_space=pl.ANY)],
        out_specs=[pl.BlockSpec((DT_ROWS, DT_W), lambda i: (i, OFF_DT // DT_W)), vec, vec],
        out_shape=[jax.ShapeDtypeStruct(dp_gd.shape, dp_gd.dtype), jax.ShapeDtypeStruct((1, DT_W), F32),
                   jax.ShapeDtypeStruct((1, DT_W), F32)],
        input_output_aliases={6: 0},
        compiler_params=_params("arbitrary"))(p, bias_pad, alog_pad, dt, ddt, dacs, dp_gd)


def _split_dot(z, onehot, terms):
    out = None
    rest = z
    for _ in range(terms):
        piece = rest.astype(BF)
        part = _dot(piece, onehot)
        out = part if out is None else out + part
        rest = rest - piece.astype(F32)
    return out


def _spread_mat(g):
    row = lax.broadcasted_iota(jnp.int32, (DT_W, GROUP_W), 0)
    lane = lax.broadcasted_iota(jnp.int32, (DT_W, GROUP_W), 1)
    return jnp.where(row == HEADS_PER_GROUP * g + lane // HEAD_DIM, 1.0, 0.0).astype(BF)


def _gather_mat(g):
    row = lax.broadcasted_iota(jnp.int32, (GROUP_W, DT_W), 0)
    lane = lax.broadcasted_iota(jnp.int32, (GROUP_W, DT_W), 1)
    return jnp.where(lane == HEADS_PER_GROUP * g + row // HEAD_DIM, 1.0, 0.0).astype(BF)


def _ssd_masks():
    row = lax.broadcasted_iota(jnp.int32, (CHUNK, GROUP_W), 0)
    col = lax.broadcasted_iota(jnp.int32, (CHUNK, GROUP_W), 1) % HEAD_DIM
    brow = lax.broadcasted_iota(jnp.int32, (GROUP_W, GROUP_W), 0) // HEAD_DIM
    bcol = lax.broadcasted_iota(jnp.int32, (GROUP_W, GROUP_W), 1) // HEAD_DIM
    return row >= col, row == col, brow == bcol


def _stack4(v):
    return jnp.concatenate([v, v, v, v], axis=0)


def _fold4(v):
    return v[0:CHUNK] + v[CHUNK:2 * CHUNK] + v[2 * CHUNK:3 * CHUNK] + v[3 * CHUNK:4 * CHUNK]


def _ssd_group(xc_ref, stacked, g, tri, eye, blockdiag):
    gs = slice(GROUP_W * g, GROUP_W * (g + 1))
    xs_g = xc_ref[:, gs]
    b_g = xc_ref[:, D_INNER + D_STATE * g:D_INNER + D_STATE * (g + 1)].astype(BF)
    c_g = xc_ref[:, D_INNER + 1024 + D_STATE * g:D_INNER + 1024 + D_STATE * (g + 1)].astype(BF)
    wide = _split_dot(stacked, _spread_mat(g), 3)
    acs_e, dt_e = wide[0:CHUNK], wide[CHUNK:2 * CHUNK]
    atot_e = acs_e[CHUNK - 1:CHUNK, :]
    acs_j = jnp.sum(jnp.where(eye, acs_e, 0.0), axis=0, keepdims=True)
    lmat = jnp.where(tri, jnp.exp(jnp.minimum(acs_e - acs_j, 0.0)), 0.0)
    b_t = _stack4(b_g)
    m = _dot(c_g, b_t, "nt") * lmat
    x_g = xs_g * dt_e
    xbd = jnp.where(blockdiag, _stack4(x_g), 0.0).astype(BF)
    return dict(gs=gs, xs=xs_g, b=b_g, c=c_g, b_t=b_t, dt=dt_e, e=jnp.exp(acs_e), dec=jnp.exp(atot_e - acs_e),
                eat=jnp.exp(atot_e), lmat=lmat, m=m, x=x_g, xbd=xbd)


def _ssd_fwd(xconv, dt, acs, d_exp, comm=None):
    t = xconv.shape[0]
    nc = t // CHUNK

    def body(xc_ref, dt_ref, acs_ref, d_ref, y_ref, hs_ref, state):
        c = pl.program_id(0)

        @pl.when(c == 0)
        def _():
            state[...] = jnp.zeros_like(state)

        hs_ref[...] = state[...]
        tri, eye, blockdiag = _ssd_masks()
        stacked = jnp.concatenate([acs_ref[...], dt_ref[...]], axis=0)
        for g in range(N_GROUPS):
            q = _ssd_group(xc_ref, stacked, g, tri, eye, blockdiag)
            gs = q["gs"]
            h_t = state[:, gs]
            ydiag = _dot(q["m"].astype(BF), q["xbd"])
            yoff = _dot(q["c"], h_t.astype(BF)) * q["e"]
            y_ref[:, gs] = ydiag + yoff + d_ref[:, gs] * q["xs"]
            s_t = _dot(q["b"], (q["x"] * q["dec"]).astype(BF), "tn")
            state[:, gs] = q["eat"] * h_t + s_t

    blk = lambda w: pl.BlockSpec((CHUNK, w), lambda c: (c, 0))
    outs, couts = _pcall(
        "ssd_fwd", body, (nc,),
        [blk(D_XBC), blk(DT_W), blk(DT_W), pl.BlockSpec((1, D_INNER), lambda c: (0, 0))],
        [blk(D_INNER), pl.BlockSpec((None, D_STATE, D_INNER), lambda c: (c, 0, 0))],
        [jax.ShapeDtypeStruct((t, D_INNER), F32), jax.ShapeDtypeStruct((nc, D_STATE, D_INNER), F32)],
        (xconv, dt, acs, d_exp), [pltpu.VMEM((D_STATE, D_INNER), F32)], ("arbitrary",), comm)
    return outs if comm is None else (outs, couts)


def _ssd_bwd(xconv, dt, acs, d_exp, hsave, dy, comm=None):
    t = xconv.shape[0]
    nc = t // CHUNK

    def body(xc_ref, dt_ref, acs_ref, d_ref, hs_ref, dy_ref, dxc_ref, ddt_ref, dacs_ref, dd_ref, dstate):
        c = pl.program_id(0)

        @pl.when(c == 0)
        def _():
            dstate[...] = jnp.zeros_like(dstate)
            dd_ref[...] = jnp.zeros_like(dd_ref)

        tri, eye, blockdiag = _ssd_masks()
        acsv = acs_ref[...]
        stacked = jnp.concatenate([acsv, dt_ref[...]], axis=0)
        eat_heads = jnp.exp(acsv[CHUNK - 1:CHUNK, :])
        ddt_acc = jnp.zeros((CHUNK, DT_W), F32)
        dacs_acc = jnp.zeros((CHUNK, DT_W), F32)
        datot_acc = jnp.zeros((1, DT_W), F32)

        for g in range(N_GROUPS):
            q = _ssd_group(xc_ref, stacked, g, tri, eye, blockdiag)
            gs, xs_g, b_g, c_g, m = q["gs"], q["xs"], q["b"], q["c"], q["m"]
            bs = slice(D_INNER + D_STATE * g, D_INNER + D_STATE * (g + 1))
            cs = slice(D_INNER + 1024 + D_STATE * g, D_INNER + 1024 + D_STATE * (g + 1))
            h_t = hs_ref[:, gs]
            h_b = h_t.astype(BF)
            dy_g = dy_ref[:, gs]
            dy_b = dy_g.astype(BF)
            ds_t = dstate[:, gs]
            ds_b = ds_t.astype(BF)

            yoff = _dot(c_g, h_b) * q["e"]
            edy = (q["e"] * dy_g).astype(BF)
            d_c = _dot(edy, h_b, "nt")
            d_ht = _dot(c_g, edy, "tn")
            bds = _dot(b_g, ds_b)
            xd = q["x"] * q["dec"]
            d_b = _dot(xd.astype(BF), ds_b, "nt")
            dm = _dot(dy_b, q["xbd"], "nt")
            cross = _dot(m.astype(BF), dy_b, "tn")
            dx_full = q["dec"] * bds + _fold4(jnp.where(blockdiag, cross, 0.0))
            dml = (dm * q["lmat"]).astype(BF)
            d_c = d_c + _dot(dml, q["b_t"])
            d_b = d_b + _fold4(_dot(dml, c_g, "tn"))
            w = dm * m
            q_dec = xd * bds
            z = w - jnp.where(eye, jnp.sum(w, axis=0, keepdims=True), 0.0) + dy_g * yoff - q_dec
            rows = jnp.concatenate(
                [jnp.sum(q_dec, axis=0, keepdims=True), jnp.sum(ds_t * h_t, axis=0, keepdims=True),
                 jnp.zeros((6, GROUP_W), F32)], axis=0)
            seg = _split_dot(jnp.concatenate([z, dx_full * xs_g, rows], axis=0), _gather_mat(g), 2)
            dacs_acc = dacs_acc + seg[0:CHUNK]
            ddt_acc = ddt_acc + seg[CHUNK:2 * CHUNK]
            datot_acc = datot_acc + seg[2 * CHUNK:2 * CHUNK + 1] + eat_heads * seg[2 * CHUNK + 1:2 * CHUNK + 2]
            dxc_ref[:, cs] = d_c
            dxc_ref[:, bs] = d_b
            dxc_ref[:, gs] = dx_full * q["dt"] + d_ref[:, gs] * dy_g
            dd_ref[:, gs] += jnp.sum(dy_g * xs_g, axis=0, keepdims=True)
            dstate[:, gs] = q["eat"] * ds_t + d_ht

        rowi = lax.broadcasted_iota(jnp.int32, (CHUNK, DT_W), 0)
        ddt_ref[...] = ddt_acc
        dacs_ref[...] = dacs_acc + jnp.where(rowi == CHUNK - 1, datot_acc, 0.0)

    rev = lambda w: pl.BlockSpec((CHUNK, w), lambda c: (nc - 1 - c, 0))
    vec = pl.BlockSpec((1, D_INNER), lambda c: (0, 0))
    outs, couts = _pcall(
        "ssd_bwd", body, (nc,),
        [rev(D_XBC), rev(DT_W), rev(DT_W), vec,
         pl.BlockSpec((None, D_STATE, D_INNER), lambda c: (nc - 1 - c, 0, 0)), rev(D_INNER)],
        [rev(D_XBC), rev(DT_W), rev(DT_W), vec],
        [jax.ShapeDtypeStruct((t, D_XBC), F32), jax.ShapeDtypeStruct((t, DT_W), F32),
         jax.ShapeDtypeStruct((t, DT_W), F32), jax.ShapeDtypeStruct((1, D_INNER), F32)],
        (xconv, dt, acs, d_exp, hsave, dy),
        [pltpu.VMEM((D_STATE, D_INNER), F32)], ("arbitrary",), comm)
    return outs if comm is None else (outs, couts)


GN_CB = 1024
GN_GROUPS = GN_CB // GROUP_W


def _gnorm_fwd(y, p, w, comm=None):
    t = y.shape[0]
    zoff = OFF_Z // GN_CB

    def body(y_ref, z_ref, w_ref, o_ref):
        for g in range(GN_GROUPS):
            gs = slice(GROUP_W * g, GROUP_W * (g + 1))
            z = z_ref[:, gs].astype(F32)
            yf = y_ref[:, gs] * (z * _sigmoid(z))
            rstd = lax.rsqrt(jnp.mean(yf * yf, axis=-1, keepdims=True) + NORM_EPS)
            o_ref[:, gs] = (yf * rstd * w_ref[:, gs]).astype(BF)

    blk = pl.BlockSpec((TE, GN_CB), lambda i, j: (i, j))
    out, couts = _pcall(
        "gnorm_fwd", body, (t // TE, D_INNER // GN_CB),
        [blk, pl.BlockSpec((TE, GN_CB), lambda i, j: (i, zoff + j)), pl.BlockSpec((1, GN_CB), lambda i, j: (0, j))],
        blk, jax.ShapeDtypeStruct((t, D_INNER), BF), (y, p, w), (), ("parallel", "parallel"), comm)
    return out if comm is None else (out, couts)


def _gnorm_bwd(y, p, w, dyn, comm=None):
    t = y.shape[0]
    zoff = OFF_Z // GN_CB

    def body(y_ref, z_ref, w_ref, dn_ref, dy_ref, dz_ref, dw_ref):
        i = pl.program_id(1)
        for g in range(GN_GROUPS):
            gs = slice(GROUP_W * g, GROUP_W * (g + 1))
            z = z_ref[:, gs].astype(F32)
            yv = y_ref[:, gs]
            s = _sigmoid(z)
            sil = z * s
            yf = yv * sil
            rstd = lax.rsqrt(jnp.mean(yf * yf, axis=-1, keepdims=True) + NORM_EPS)
            xhat = yf * rstd
            dn = dn_ref[:, gs]
            wd = dn * w_ref[:, gs]
            proj = jnp.mean(wd * xhat, axis=-1, keepdims=True)
            dyf = rstd * (wd - xhat * proj)
            dy_ref[:, gs] = dyf * sil
            dz_ref[:, gs] = (dyf * yv * (s * (1.0 + z * (1.0 - s)))).astype(BF)
            part = jnp.sum(dn * xhat, axis=0, keepdims=True)

            @pl.when(i == 0)
            def _():
                dw_ref[:, gs] = part

            @pl.when(i > 0)
            def _():
                dw_ref[:, gs] += part

    blk = pl.BlockSpec((TE, GN_CB), lambda j, i: (i, j))
    vec = pl.BlockSpec((1, GN_CB), lambda j, i: (0, j))
    outs, couts = _pcall(
        "gnorm_bwd", body, (D_INNER // GN_CB, t // TE),
        [blk, pl.BlockSpec((TE, GN_CB), lambda j, i: (i, zoff + j)), vec, blk],
        [blk, pl.BlockSpec((TE, GN_CB), lambda j, i: (i, zoff + j)), vec],
        [jax.ShapeDtypeStruct((t, D_INNER), F32), jax.ShapeDtypeStruct((t, N_MAIN), BF),
         jax.ShapeDtypeStruct((1, D_INNER), F32)],
        (y, p, w, dyn), (), ("parallel", "arbitrary"), comm)
    return outs if comm is None else (outs, couts)


MERGE_CB = 512


def _merge_fwd(p, ya, yb):
    t = ya.shape[0]

    def body(ga_ref, gb_ref, ya_ref, yb_ref, o_ref):
        o_ref[...] = (_sigmoid(ga_ref[...]) * ya_ref[...] + _sigmoid(gb_ref[...]) * yb_ref[...]).astype(BF)

    blk = pl.BlockSpec((TE, MERGE_CB), lambda i, j: (i, j))
    return pl.pallas_call(
        body, name="merge_fwd", grid=(t // TE, D_MODEL // MERGE_CB),
        in_specs=[pl.BlockSpec((TE, MERGE_CB), lambda i, j: (i, 2 * j)),
                  pl.BlockSpec((TE, MERGE_CB), lambda i, j: (i, 2 * j + 1)), blk, blk],
        out_specs=blk, out_shape=jax.ShapeDtypeStruct((t, D_MODEL), BF),
        compiler_params=_params("parallel", "parallel"))(p, p, ya, yb)


def _merge_bwd(p, ya, yb, dm):
    t = ya.shape[0]

    def body(ga_ref, gb_ref, ya_ref, yb_ref, dm_ref, dg_ref, dya_ref, dyb_ref):
        d = dm_ref[...]
        sa = _sigmoid(ga_ref[...])
        sb = _sigmoid(gb_ref[...])
        dg_ref[:, 0:MERGE_CB] = (d * ya_ref[...] * sa * (1.0 - sa)).astype(BF)
        dg_ref[:, MERGE_CB:2 * MERGE_CB] = (d * yb_ref[...] * sb * (1.0 - sb)).astype(BF)
        dya_ref[...] = (d * sa).astype(BF)
        dyb_ref[...] = (d * sb).astype(BF)

    blk = pl.BlockSpec((TE, MERGE_CB), lambda i, j: (i, j))
    return pl.pallas_call(
        body, name="merge_bwd", grid=(t // TE, D_MODEL // MERGE_CB),
        in_specs=[pl.BlockSpec((TE, MERGE_CB), lambda i, j: (i, 2 * j)),
                  pl.BlockSpec((TE, MERGE_CB), lambda i, j: (i, 2 * j + 1)), blk, blk, blk],
        out_specs=[pl.BlockSpec((TE, 2 * MERGE_CB), lambda i, j: (i, j)), blk, blk],
        out_shape=[jax.ShapeDtypeStruct((t, N_GD), BF)] + [jax.ShapeDtypeStruct((t, D_MODEL), BF)] * 2,
        compiler_params=_params("parallel", "parallel"))(p, p, ya, yb, dm)


def _adamw(name, parts, w, m, v, comm=None):
    r, c = w.shape
    tr = _row_tile(r)
    tc = ADAM_COL_TILE if (tr == r and r > 512 and c % ADAM_COL_TILE == 0) else c
    n_parts = parts.shape[0]
    bc1 = 1.0 - ADAM_B1 ** ADAM_STEP
    bc2 = 1.0 - ADAM_B2 ** ADAM_STEP

    def body(p_ref, w_ref, m_ref, v_ref, g_ref, d_ref, nm_ref, nv_ref):
        g = p_ref[0].astype(F32)
        for k in range(1, n_parts):
            g = g + p_ref[k].astype(F32)
        nm = ADAM_B1 * m_ref[...] + (1.0 - ADAM_B1) * g
        nv = ADAM_B2 * v_ref[...] + (1.0 - ADAM_B2) * (g * g)
        g_ref[...] = g
        nm_ref[...] = nm
        nv_ref[...] = nv
        d_ref[...] = -ADAM_LR * ((nm / bc1) / (jnp.sqrt(nv / bc2) + ADAM_EPS) + ADAM_WD * w_ref[...])

    blk = pl.BlockSpec((tr, tc), lambda i, j: (i, j))
    outs, couts = _pcall(
        name, body, (r // tr, c // tc),
        [pl.BlockSpec((n_parts, tr, tc), lambda i, j: (0, i, j)), blk, blk, blk], [blk] * 4,
        [jax.ShapeDtypeStruct((r, c), F32)] * 4, (parts, w, m, v), (), ("parallel", "parallel"), comm)
    return outs if comm is None else (outs, couts)


def _pad_lanes(v, width):
    return jnp.pad(v, ((0, 0), (0, width - v.shape[1])))


def _reduce_start(slots, host):
    outs, sib = host(_pair_comm([a for _, a in slots]))
    sums = [(n, _add_pairs("pairsum_" + n, a, b)) for (n, a), b in zip(slots, sib)]
    return outs, sums


def _train_step(x, target, shard, rep):
    gdt = BF
    t = x.shape[0]
    recv = {}
    (got,) = _comm_call("gather_ffn1_in", _gather_comm([shard["ffn1_w_in"]], [True]))
    w1_in = got.reshape(2 * D_FF, D_MODEL)
    h1 = _rms_fwd("rms1_fwd", x, rep["ffn1_norm"])
    w_in_rows = [shard["w_in"][a:b] for a, b in W_IN_ROW_CUTS]
    gu1, got = _mm_nt("ffn1_in", h1, w1_in, tn=FF_HALF, out_dtype=BF, comm=_gather_comm([shard["ffn1_w_out"], w_in_rows[0]]))
    w1_out = got[0].reshape(D_FF, D_MODEL)
    w_in_got = [got[1]]
    act1, got = _swiglu_fwd("swiglu1_fwd", gu1, comm=_gather_comm([w_in_rows[1]]))
    w_in_got.append(got[0])
    x1, got = _mm_nn("ffn1_out", act1, w1_out, res=x, alpha=0.5, comm=_gather_comm(
        [w_in_rows[2], shard["short_conv_w"], shard["ssm_conv_w"]]))
    w_in_got.append(got[0])
    short_conv_w = got[1].transpose(1, 0, 2).reshape(3, D_MODEL)
    ssm_conv_w = got[2].transpose(1, 0, 2).reshape(4, D_XBC)
    w_in_t = jnp.concatenate(w_in_got, axis=1).reshape(N_IN, D_MODEL)
    ga0 = N_MAIN + N_HEADS
    gb0 = ga0 + D_MODEL
    half = D_MODEL // 2
    w_gd = jnp.concatenate(
        [w_in_t[ga0:ga0 + half], w_in_t[gb0:gb0 + half], w_in_t[ga0 + half:gb0], w_in_t[gb0 + half:],
         w_in_t[N_MAIN:N_MAIN + N_HEADS], jnp.zeros((DT_W - N_HEADS, D_MODEL), BF)], axis=0)
    w_mix_perm = w_in_t[0:3 * D_MODEL].reshape(3, 4, CONV_CB, D_MODEL).transpose(1, 0, 2, 3).reshape(3 * D_MODEL, D_MODEL)

    h2 = _rms_fwd("rms2_fwd", x1, rep["mix_norm"])
    p, got = _mm_nt("proj_main", h2, w_in_t, n=N_MAIN, tn=1024, out_dtype=BF, comm=_gather_comm(
        [shard["short_w_out"], shard["ssm_w_out"], shard["w_out"]]))
    p_gd = _mm_nt("proj_gd", h2, w_gd)
    short_w_out = got[0].reshape(D_MODEL, D_MODEL)
    ssm_w_out = got[1].reshape(D_INNER, D_MODEL)
    w_out = got[2].reshape(D_MODEL, D_MODEL)
    ya_in = _mix_a_fwd(p, short_conv_w)
    y_a = _mm_nn("short_out", ya_in, short_w_out)
    xconv = _ssm_conv_fwd(p, ssm_conv_w, rep["ssm_conv_b"])
    dt, acs = _dt_fwd(p_gd, rep["dt_bias_pad"], rep["a_log_pad"])
    (y_ssm, hsave), (got,) = _ssd_fwd(xconv, dt, acs, rep["d_exp"], comm=_gather_comm([shard["ffn2_w_in"]], [True]))
    w2_in = got.reshape(2 * D_FF, D_MODEL)
    yn, got = _gnorm_fwd(y_ssm, p, rep["ssm_norm"], comm=_gather_comm([shard["ffn2_w_out"]]))
    w2_out = got[0].reshape(D_FF, D_MODEL)
    y_b = _mm_nn("ssm_out", yn, ssm_w_out, tk=1024)
    merged = _merge_fwd(p_gd, y_a, y_b)
    x2 = _mm_nn("mix_out", merged, w_out, res=x1)

    h3 = _rms_fwd("rms3_fwd", x2, rep["ffn2_norm"])
    gu2 = _mm_nt("ffn2_in", h3, w2_in, tn=FF_HALF, out_dtype=BF)
    act2 = _swiglu_fwd("swiglu2_fwd", gu2)
    x3 = _mm_nn("ffn2_out", act2, w2_out, res=x2, alpha=0.5)

    loss, dx3, dx3h, g_final = _final_loss(x3, rep["final_norm"], target)

    small = {"final_norm": g_final}
    dact2 = _mm_nt("ffn2_out_bwd_act", dx3h, w2_out, out_dtype=BF)
    g_w2_out = _mm_tn("ffn2_out_bwd_w", act2, dx3h, gdt, tm=FF_HALF)
    dgu2 = _swiglu_bwd("swiglu2_bwd", gu2, dact2)
    g_w2_in = _mm_tn("ffn2_in_bwd_w", dgu2, h3, gdt, tm=FF_HALF)
    dh3 = _mm_nn("ffn2_in_bwd_h", dgu2, w2_in, tk=FF_HALF)
    dx2, dx2b, small["ffn2_norm"] = _rms_bwd("rms3_bwd", x2, rep["ffn2_norm"], dh3, dx3, 1.0)

    dmerged = _mm_nt("mix_out_bwd_x", dx2b, w_out)
    g_w_out = _mm_tn("mix_out_bwd_w", merged, dx2b, gdt)
    dp_gd, dya, dyb = _merge_bwd(p_gd, y_a, y_b, dmerged)

    dya_in = _mm_nt("short_out_bwd_x", dya, short_w_out)
    g_short_w_out = _mm_tn("short_out_bwd_w", ya_in, dya, gdt)

    dyn = _mm_nt("ssm_out_bwd_x", dyb, ssm_w_out)
    g_ssm_w_out = _mm_tn("ssm_out_bwd_w", yn, dyb, gdt)
    late = [("ffn2_w_out", g_w2_out.reshape(N_DEV, FF_SHARD // 2, D_MODEL)),
            ("ffn2_w_in", g_w2_in.reshape(N_DEV, FF_SHARD, D_MODEL)),
            ("w_out", g_w_out.reshape(N_DEV, -1, D_MODEL)), ("short_w_out", g_short_w_out.reshape(N_DEV, -1, D_MODEL)),
            ("ssm_w_out", g_ssm_w_out.reshape(N_DEV, -1, D_MODEL))]
    (dy_ssm, dp, small["ssm_norm"]), sums = _reduce_start(
        late, lambda comm: _gnorm_bwd(y_ssm, p, rep["ssm_norm"], dyn, comm=comm))
    dp, g_short_conv = _mix_a_bwd(p, short_conv_w, dya_in, dp)
    (dxconv, ddt, dacs, dd_lane), got = _ssd_bwd(
        xconv, dt, acs, rep["d_exp"], hsave, dy_ssm,
        comm=_chip_comm([a for _, a in sums], [n == "ffn2_w_in" for n, _ in sums]))
    recv.update({n: a for (n, _), a in zip(sums, got)})
    small["ssm_D"] = dd_lane.reshape(N_HEADS, HEAD_DIM).sum(axis=1)[None, :]
    dp, g_ssm_conv, small["ssm_conv_b"] = _ssm_conv_bwd(p, ssm_conv_w, rep["ssm_conv_b"], dxconv, dp)
    dp_gd, dbias, dalog = _dt_bwd(p_gd, rep["dt_bias_pad"], rep["a_log_pad"], dt, ddt, dacs, dp_gd)
    small["ssm_dt_bias"] = dbias[:, :N_HEADS]
    small["ssm_A_log"] = dalog[:, :N_HEADS]

    g_main = _mm_tn("proj_main_bwd_w", dp, h2, gdt, tm=1024)
    g_gd = _mm_tn("proj_gd_bwd_w", dp_gd, h2, gdt)
    g_mix = g_main[0:3 * D_MODEL].reshape(4, 3, CONV_CB, D_MODEL).transpose(1, 0, 2, 3).reshape(3 * D_MODEL, D_MODEL)
    g_in_t = jnp.concatenate(
        [g_mix, g_main[3 * D_MODEL:], g_gd[2 * D_MODEL:2 * D_MODEL + N_HEADS],
         g_gd[0:half], g_gd[2 * half:3 * half], g_gd[half:2 * half], g_gd[3 * half:4 * half]], axis=0).reshape(
        N_DEV, IN_SHARD, D_MODEL)
    dh2, w_sums = _reduce_start(
        [("w_in", g_in_t)], lambda comm: _mm_nn("proj_mix_bwd_x", dp, w_mix_perm, tk=1024, kk=3 * D_MODEL, comm=comm))
    w_sum = w_sums[0][1]

    def w_piece(i):
        return _chip_comm([w_sum], rows=[W_GRAD_ROW_CUTS[i]])

    dh2, got0 = _mm_nn("proj_rest_bwd_x", dp, w_in_t, tk=1024, kk=N_MAIN - 3 * D_MODEL, a_off=3, b_off=3, res=dh2,
                       comm=w_piece(0))
    dh2, got1 = _mm_nn("proj_gd_bwd_x", dp_gd, w_gd, res=dh2, comm=w_piece(1))
    (dx1, dx1h, small["mix_norm"]), got2 = _rms_bwd("rms2_bwd", x1, rep["mix_norm"], dh2, dx2, 0.5, comm=w_piece(2))
    g_w1_out, got3 = _mm_tn("ffn1_out_bwd_w", act1, dx1h, gdt, tm=FF_HALF, comm=w_piece(3))
    rest = [("ffn1_w_out", g_w1_out.reshape(N_DEV, FF_SHARD // 2, D_MODEL)),
            ("short_conv_w", g_short_conv.reshape(3, N_DEV, -1).transpose(1, 0, 2)),
            ("ssm_conv_w", g_ssm_conv.reshape(4, N_DEV, -1).transpose(1, 0, 2))]
    dact1, got = _mm_nt("ffn1_out_bwd_act", dx1h, w1_out, out_dtype=BF,
                        comm=_join_comm(w_piece(4), _pair_comm([a for _, a in rest])))
    got4, sib = got[0], got[1:]
    rest_sums = [(n, _add_pairs("pairsum_" + n, a, b)) for (n, a), b in zip(rest, sib)]
    recv["w_in"] = jnp.concatenate([got0[0], got1[0], got2[0], got3[0], got4], axis=1)
    dgu1, got = _swiglu_bwd("swiglu1_bwd", gu1, dact1, comm=_chip_comm([a for _, a in rest_sums]))
    recv.update({n: a for (n, _), a in zip(rest_sums, got)})

    cw = D_MODEL // 2
    g_a = _mm_tn("ffn1_in_bwd_w_a", dgu1, h1, gdt, tm=FF_HALF, n=cw, col_off=0).reshape(N_DEV, FF_SHARD, cw)
    g_b, sib_a = _mm_tn("ffn1_in_bwd_w_b", dgu1, h1, gdt, tm=FF_HALF, n=cw, col_off=1, comm=_pair_comm([g_a]))
    g_b = g_b.reshape(N_DEV, FF_SHARD, cw)
    sum_a = _add_pairs("pairsum_ffn1_w_in_a", g_a, sib_a[0])
    dh1, got = _mm_nn("ffn1_in_bwd_h", dgu1, w1_in, tk=FF_HALF,
                      comm=_join_comm(_chip_comm([sum_a], [True]), _pair_comm([g_b])))
    recv_a, sib_b = got
    sum_b = _add_pairs("pairsum_ffn1_w_in_b", g_b, sib_b)
    (dx0, _, small["ffn1_norm"]), got = _rms_bwd("rms1_bwd", x, rep["ffn1_norm"], dh1, dx1, 1.0,
                                                  comm=_chip_comm([sum_b], [True]))
    recv["ffn1_w_in"] = jnp.concatenate([recv_a, got[0]], axis=2)
    return dx0, recv, _pack_small(small, loss[:, 0:1])


_SMALL = [("ffn1_norm", 1024), ("mix_norm", 1024), ("ssm_conv_b", 4096), ("ssm_dt_bias", 32), ("ssm_A_log", 32),
          ("ssm_D", 32), ("ssm_norm", 2048), ("ffn2_norm", 1024), ("final_norm", 1024)]
SMALL_W = 10368


def _pack_small(d, loss=None):
    parts = [d[n].reshape(1, -1).astype(F32) for n, _ in _SMALL]
    used = sum(sz for _, sz in _SMALL)
    tail = jnp.zeros((1, SMALL_W - used), F32)
    if loss is not None:
        tail = tail.at[:, 0:1].set(loss)
    return jnp.concatenate(parts + [tail], axis=1)


def _unpack_small(v, shapes):
    out, off = {}, 0
    for n, sz in _SMALL:
        out[n] = v[:, off:off + sz].reshape(shapes[n])
        off += sz
    return out, v[0, off]


_SHARDED = ["ffn1_w_in", "ffn1_w_out", "w_in", "short_conv_w", "short_w_out", "ssm_conv_w", "ssm_w_out", "w_out",
            "ffn2_w_in", "ffn2_w_out"]
_TRANSPOSED = ("ffn1_w_in", "w_in", "ffn2_w_in")
_ORDER = ["ffn1_norm", "ffn1_w_in", "ffn1_w_out", "mix_norm", "w_in", "short_conv_w", "short_w_out", "ssm_conv_w",
          "ssm_conv_b", "ssm_dt_bias", "ssm_A_log", "ssm_D", "ssm_norm", "ssm_w_out", "w_out", "ffn2_norm",
          "ffn2_w_in", "ffn2_w_out", "final_norm"]


def kernel(x, ffn1_norm, ffn1_w_in, ffn1_w_out, mix_norm, w_in, short_conv_w, short_w_out, ssm_conv_w, ssm_conv_b, ssm_dt_bias, ssm_A_log, ssm_D, ssm_norm, ssm_w_out, w_out, ffn2_norm, ffn2_w_in, ffn2_w_out, final_norm, loss_target, m_ffn1_norm, m_ffn1_w_in, m_ffn1_w_out, m_mix_norm, m_w_in, m_short_conv_w, m_short_w_out, m_ssm_conv_w, m_ssm_conv_b, m_ssm_dt_bias, m_ssm_A_log, m_ssm_D, m_ssm_norm, m_ssm_w_out, m_w_out, m_ffn2_norm, m_ffn2_w_in, m_ffn2_w_out, m_final_norm, v_ffn1_norm, v_ffn1_w_in, v_ffn1_w_out, v_mix_norm, v_w_in, v_short_conv_w, v_short_w_out, v_ssm_conv_w, v_ssm_conv_b, v_ssm_dt_bias, v_ssm_A_log, v_ssm_D, v_ssm_norm, v_ssm_w_out, v_w_out, v_ffn2_norm, v_ffn2_w_in, v_ffn2_w_out, v_final_norm):
    w = dict(ffn1_norm=ffn1_norm, ffn1_w_in=ffn1_w_in, ffn1_w_out=ffn1_w_out, mix_norm=mix_norm, w_in=w_in,
             short_conv_w=short_conv_w, short_w_out=short_w_out, ssm_conv_w=ssm_conv_w, ssm_conv_b=ssm_conv_b,
             ssm_dt_bias=ssm_dt_bias, ssm_A_log=ssm_A_log, ssm_D=ssm_D, ssm_norm=ssm_norm, ssm_w_out=ssm_w_out,
             w_out=w_out, ffn2_norm=ffn2_norm, ffn2_w_in=ffn2_w_in, ffn2_w_out=ffn2_w_out, final_norm=final_norm)
    m = dict(ffn1_norm=m_ffn1_norm, ffn1_w_in=m_ffn1_w_in, ffn1_w_out=m_ffn1_w_out, mix_norm=m_mix_norm, w_in=m_w_in,
             short_conv_w=m_short_conv_w, short_w_out=m_short_w_out, ssm_conv_w=m_ssm_conv_w,
             ssm_conv_b=m_ssm_conv_b, ssm_dt_bias=m_ssm_dt_bias, ssm_A_log=m_ssm_A_log, ssm_D=m_ssm_D,
             ssm_norm=m_ssm_norm, ssm_w_out=m_ssm_w_out, w_out=m_w_out, ffn2_norm=m_ffn2_norm,
             ffn2_w_in=m_ffn2_w_in, ffn2_w_out=m_ffn2_w_out, final_norm=m_final_norm)
    v = dict(ffn1_norm=v_ffn1_norm, ffn1_w_in=v_ffn1_w_in, ffn1_w_out=v_ffn1_w_out, mix_norm=v_mix_norm, w_in=v_w_in,
             short_conv_w=v_short_conv_w, short_w_out=v_short_w_out, ssm_conv_w=v_ssm_conv_w,
             ssm_conv_b=v_ssm_conv_b, ssm_dt_bias=v_ssm_dt_bias, ssm_A_log=v_ssm_A_log, ssm_D=v_ssm_D,
             ssm_norm=v_ssm_norm, ssm_w_out=v_ssm_w_out, w_out=v_w_out, ffn2_norm=v_ffn2_norm,
             ffn2_w_in=v_ffn2_w_in, ffn2_w_out=v_ffn2_w_out, final_norm=v_final_norm)
    shapes = {n: w[n].shape for n in _ORDER}

    def local(d, n):
        return d[n][0].T if n in _TRANSPOSED else d[n][0]

    shard = {n: local(w, n) for n in _SHARDED}

    wire = {n: (shard[n] if n in ("short_conv_w", "ssm_conv_w") else shard[n].astype(BF)) for n in _SHARDED}
    rep = {
        "ffn1_norm": ffn1_norm, "mix_norm": mix_norm, "ffn2_norm": ffn2_norm, "ssm_norm": ssm_norm,
        "ssm_conv_b": ssm_conv_b, "final_norm": final_norm.reshape(1, D_MODEL),
        "dt_bias_pad": _pad_lanes(ssm_dt_bias, DT_W), "a_log_pad": _pad_lanes(ssm_A_log, DT_W),
        "d_exp": jnp.repeat(ssm_D, HEAD_DIM, axis=1),
    }
    grad_x, parts, packed = _train_step(x[0], loss_target[0], wire, rep)

    out_g, out_d, out_m, out_v = {}, {}, {}, {}
    for n in _SHARDED:
        if n == "ssm_w_out":
            res, (small_parts,) = _adamw("adamw_" + n, parts[n], shard[n], local(m, n), local(v, n),
                                         comm=_gather_comm([packed]))
        else:
            res = _adamw("adamw_" + n, parts[n], shard[n], local(m, n), local(v, n))
        out_g[n], out_d[n], out_m[n], out_v[n] = [(r.T if n in _TRANSPOSED else r).reshape(shapes[n]) for r in res]
    sres = _adamw("adamw_small", small_parts, _pack_small(w), _pack_small(m), _pack_small(v))
    sg, loss = _unpack_small(sres[0], shapes)
    sd, _ = _unpack_small(sres[1], shapes)
    sm, _ = _unpack_small(sres[2], shapes)
    sv, _ = _unpack_small(sres[3], shapes)
    out_g.update(sg)
    out_d.update(sd)
    out_m.update(sm)
    out_v.update(sv)
    return (loss, grad_x[None], *[out_g[n] for n in _ORDER], *[out_d[n] for n in _ORDER],
            *[out_m[n] for n in _ORDER], *[out_v[n] for n in _ORDER])
```

```python
import functools

import jax
import jax.numpy as jnp
from jax import lax
from jax.experimental import pallas as pl
from jax.experimental.pallas import tpu as pltpu

F32 = jnp.float32
BF = jnp.bfloat16

N_DEV = 8
D_MODEL = 1024
D_FF = 2816
D_INNER = 2048
D_XBC = 4096
N_HEADS = 32
HEAD_DIM = 64
N_GROUPS = 8
D_STATE = 128
CHUNK = 64
GROUP_W = D_INNER // N_GROUPS
HEADS_PER_GROUP = N_HEADS // N_GROUPS
NORM_EPS = 1e-5
N_IN = 11296
FF_SHARD = 2 * D_FF // N_DEV
FF_HALF = D_FF // 2
IN_SHARD = N_IN // N_DEV

OFF_B, OFF_C, OFF_XA, OFF_Z, OFF_XBC = 0, 1024, 2048, 3072, 5120
N_MAIN = 9216
OFF_GA, OFF_GB, OFF_DT = 0, 1024, 2048
DT_W = 128
N_GD = 2048 + DT_W
W_GRAD_ROW_CUTS = [(0, 512), (512, 720), (720, 896), (896, 1152), (1152, 1412)]

ADAM_LR, ADAM_B1, ADAM_B2, ADAM_EPS, ADAM_WD, ADAM_STEP = 0.001, 0.9, 0.999, 1e-08, 0.01, 10

VMEM_LIMIT_V7X = 56 * 1024 * 1024
TM = 1024
TE = 512
ADAM_COL_TILE = 256


def _params(*sem):
    return pltpu.CompilerParams(dimension_semantics=sem, vmem_limit_bytes=VMEM_LIMIT_V7X)


_DIMS = {
    "nn": (((1,), (0,)), ((), ())),
    "nt": (((1,), (1,)), ((), ())),
    "tn": (((0,), (0,)), ((), ())),
}


def _dot(a, b, mode="nn"):
    return lax.dot_general(a, b, _DIMS[mode], preferred_element_type=F32)


def _sigmoid(x):
    return 1.0 / (1.0 + jnp.exp(-x))


class _Comm:
    def __init__(self, inputs, out_shapes, sems, start, finish):
        self.inputs, self.out_shapes, self.sems, self.start, self.finish = inputs, out_shapes, sems, start, finish


def _pcall(name, body, grid, in_specs, out_specs, out_shape, args, scratch=(), sem=None, comm=None):
    single = not isinstance(out_shape, (list, tuple))
    out_shapes = [out_shape] if single else list(out_shape)
    out_specs = [out_specs] if single else list(out_specs)
    n_in, n_out, n_scr = len(args), len(out_shapes), len(scratch)
    if comm is None:
        res = pl.pallas_call(
            body, name=name, grid=grid, in_specs=list(in_specs), out_specs=out_specs, out_shape=out_shapes,
            scratch_shapes=list(scratch), compiler_params=_params(*sem))(*args)
        return (res[0] if single else res), []
    nci, nco = len(comm.inputs), len(comm.out_shapes)

    def wrapped(*refs):
        a = refs[:n_in]
        ci = refs[n_in:n_in + nci]
        o0 = n_in + nci
        o = refs[o0:o0 + n_out]
        co = refs[o0 + n_out:o0 + n_out + nco]
        s0 = o0 + n_out + nco
        s = refs[s0:s0 + n_scr]
        cs = refs[s0 + n_scr:]
        pids = [pl.program_id(i) for i in range(len(grid))]
        first = functools.reduce(jnp.logical_and, [p == 0 for p in pids])
        last = functools.reduce(jnp.logical_and, [p == g - 1 for p, g in zip(pids, grid)])

        @pl.when(first)
        def _():
            comm.start(ci, co, cs)

        body(*a, *o, *s)

        @pl.when(last)
        def _():
            comm.finish(ci, co, cs)

    any_spec = pl.BlockSpec(memory_space=pl.ANY)
    res = pl.pallas_call(
        wrapped, name=name, grid=grid, in_specs=list(in_specs) + [any_spec] * nci,
        out_specs=out_specs + [any_spec] * nco, out_shape=out_shapes + list(comm.out_shapes),
        scratch_shapes=list(scratch) + list(comm.sems),
        compiler_params=_params(*(("arbitrary",) * len(grid))))(*args, *comm.inputs)
    core = res[:n_out]
    return (core[0] if single else core), list(res[n_out:])


def _comm_call(name, comm):
    nci, nco = len(comm.inputs), len(comm.out_shapes)

    def body(*refs):
        ci, co, cs = refs[:nci], refs[nci:nci + nco], refs[nci + nco:]
        comm.start(ci, co, cs)
        comm.finish(ci, co, cs)

    any_spec = pl.BlockSpec(memory_space=pl.ANY)
    return pl.pallas_call(
        body, name=name, in_specs=[any_spec] * nci, out_specs=[any_spec] * nco, out_shape=list(comm.out_shapes),
        scratch_shapes=list(comm.sems), compiler_params=pltpu.CompilerParams(has_side_effects=True))(*comm.inputs)


def _remote(src, dst, ssem, rsem, dev):
    return pltpu.make_async_remote_copy(src_ref=src, dst_ref=dst, send_sem=ssem, recv_sem=rsem, device_id=dev,
                                        device_id_type=pl.DeviceIdType.MESH)


def _place():
    x, y, c = lax.axis_index("x"), lax.axis_index("y"), lax.axis_index("c")
    other_chips = [(1 - x, y), (x, 1 - y), (1 - x, 1 - y)]
    return x, y, c, other_chips


def _slot(x, y, c, swap):
    return 4 * y + 2 * x + c if swap else 4 * x + 2 * y + c


def _chip_slot(x, y, swap):
    return 2 * y + x if swap else 2 * x + y


def _gather_comm(shards, swaps=None):
    n = len(shards)
    per = N_DEV - 1
    swaps = [False] * n if swaps is None else swaps

    def start(ins, outs, sems):
        send, recv, loc = sems
        x, y, c, chips = _place()
        for i in range(n):
            me = _slot(x, y, c, swaps[i])
            pltpu.make_async_copy(ins[i], outs[i].at[me], loc.at[i]).start()
            _remote(ins[i], outs[i].at[me], send.at[per * i], recv.at[per * i], (x, y, 1 - c)).start()
            for j, (qx, qy) in enumerate(chips):
                _remote(ins[i], outs[i].at[me], send.at[per * i + 1 + j], recv.at[per * i + 1 + j], (qx, qy, c)).start()

    def finish(ins, outs, sems):
        send, recv, loc = sems
        x, y, c, chips = _place()
        sib = (x, y, 1 - c)
        for i in range(n):
            for j, (qx, qy) in enumerate(chips):
                blk = outs[i].at[_slot(qx, qy, c, swaps[i])]
                _remote(blk, blk, send.at[per * i + 1 + j], recv.at[per * i + 1 + j], (qx, qy, c)).wait_recv()
                _remote(blk, blk, send.at[per * i + 4 + j], recv.at[per * i + 4 + j], sib).start()
        for i in range(n):
            blk = outs[i].at[_slot(x, y, 1 - c, swaps[i])]
            _remote(blk, blk, send.at[per * i], recv.at[per * i], sib).wait_recv()
            for j, (qx, qy) in enumerate(chips):
                blk = outs[i].at[_slot(qx, qy, 1 - c, swaps[i])]
                _remote(blk, blk, send.at[per * i + 4 + j], recv.at[per * i + 4 + j], sib).wait_recv()
        for i in range(n):
            own = outs[i].at[_slot(x, y, c, swaps[i])]
            for k in range(per):
                _remote(ins[i], own, send.at[per * i + k], recv.at[per * i + k], sib).wait_send()
            pltpu.make_async_copy(ins[i], own, loc.at[i]).wait()

    out_shapes = [jax.ShapeDtypeStruct((N_DEV,) + tuple(a.shape), a.dtype) for a in shards]
    sems = [pltpu.SemaphoreType.DMA((per * n,)), pltpu.SemaphoreType.DMA((per * n,)), pltpu.SemaphoreType.DMA((n,))]
    return _Comm(list(shards), out_shapes, sems, start, finish)


def _pair_comm(slots):
    n = len(slots)

    def copies(ins, outs, sems):
        send, recv = sems
        x, y, c, _ = _place()
        sib = (x, y, 1 - c)
        out = []
        for i in range(n):
            for q in range(4):
                out.append(_remote(ins[i].at[2 * q + 1 - c], outs[i].at[q], send.at[4 * i + q], recv.at[4 * i + q], sib))
        return out

    def start(ins, outs, sems):
        for cp in copies(ins, outs, sems):
            cp.start()

    def finish(ins, outs, sems):
        for cp in copies(ins, outs, sems):
            cp.wait_send()
            cp.wait_recv()

    out_shapes = [jax.ShapeDtypeStruct((4,) + tuple(a.shape[1:]), a.dtype) for a in slots]
    sems = [pltpu.SemaphoreType.DMA((4 * n,)), pltpu.SemaphoreType.DMA((4 * n,))]
    return _Comm(list(slots), out_shapes, sems, start, finish)


def _chip_comm(chip_sums, swaps=None, rows=None):
    n = len(chip_sums)
    swaps = [False] * n if swaps is None else swaps
    rows = [None] * n if rows is None else rows

    def src(ins, i, q):
        return ins[i].at[q] if rows[i] is None else ins[i].at[q, pl.ds(rows[i][0], rows[i][1] - rows[i][0])]

    def start(ins, outs, sems):
        send, recv, loc = sems
        x, y, c, chips = _place()
        for i in range(n):
            mine = _chip_slot(x, y, swaps[i])
            pltpu.make_async_copy(src(ins, i, mine), outs[i].at[mine], loc.at[i]).start()
            for j, (qx, qy) in enumerate(chips):
                _remote(src(ins, i, _chip_slot(qx, qy, swaps[i])), outs[i].at[mine], send.at[3 * i + j],
                        recv.at[3 * i + j], (qx, qy, c)).start()

    def finish(ins, outs, sems):
        send, recv, loc = sems
        x, y, c, chips = _place()
        for i in range(n):
            mine = _chip_slot(x, y, swaps[i])
            for j, (qx, qy) in enumerate(chips):
                theirs = _chip_slot(qx, qy, swaps[i])
                cp = _remote(src(ins, i, theirs), outs[i].at[theirs], send.at[3 * i + j], recv.at[3 * i + j], (qx, qy, c))
                cp.wait_send()
                cp.wait_recv()
            pltpu.make_async_copy(src(ins, i, mine), outs[i].at[mine], loc.at[i]).wait()

    def out_shape(a, r):
        shape = a.shape if r is None else (a.shape[0], r[1] - r[0]) + tuple(a.shape[2:])
        return jax.ShapeDtypeStruct(shape, a.dtype)

    out_shapes = [out_shape(a, r) for a, r in zip(chip_sums, rows)]
    sems = [pltpu.SemaphoreType.DMA((3 * n,)), pltpu.SemaphoreType.DMA((3 * n,)), pltpu.SemaphoreType.DMA((n,))]
    return _Comm(list(chip_sums), out_shapes, sems, start, finish)


def _join_comm(a, b):
    na_i, na_o, na_s = len(a.inputs), len(a.out_shapes), len(a.sems)

    def start(ins, outs, sems):
        a.start(ins[:na_i], outs[:na_o], sems[:na_s])
        b.start(ins[na_i:], outs[na_o:], sems[na_s:])

    def finish(ins, outs, sems):
        a.finish(ins[:na_i], outs[:na_o], sems[:na_s])
        b.finish(ins[na_i:], outs[na_o:], sems[na_s:])

    return _Comm(a.inputs + b.inputs, a.out_shapes + b.out_shapes, a.sems + b.sems, start, finish)


def _row_tile(r):
    for cand in (256, 128):
        if r > cand and r % cand == 0:
            return cand
    return r


def _add_pairs(name, slots, sib):
    r, c = slots.shape[1:]
    tr = _row_tile(r)

    def body(core_ref, s_ref, b_ref, o_ref):
        o_ref[...] = (s_ref[...].astype(F32) + b_ref[...].astype(F32)).astype(o_ref.dtype)

    core = jnp.full((1,), lax.axis_index("c"), jnp.int32)
    return pl.pallas_call(
        body, name=name,
        grid_spec=pltpu.PrefetchScalarGridSpec(
            num_scalar_prefetch=1, grid=(4, r // tr),
            in_specs=[pl.BlockSpec((None, None, tr, c), lambda q, i, core_ref: (q, core_ref[0], i, 0)),
                      pl.BlockSpec((None, tr, c), lambda q, i, core_ref: (q, i, 0))],
            out_specs=pl.BlockSpec((None, tr, c), lambda q, i, core_ref: (q, i, 0))),
        out_shape=jax.ShapeDtypeStruct((4, r, c), slots.dtype),
        compiler_params=_params("parallel", "parallel"))(core, slots.reshape(4, 2, r, c), sib)


def _matmul(name, mode, a, b, grid, a_spec, b_spec, o_spec, out_shape, acc_shape,
            res=None, res_spec=None, alpha=1.0, comm=None):
    nk = grid[-1]
    has_res = res is not None

    def body(*refs):
        if has_res:
            a_ref, b_ref, r_ref, o_ref = refs[:4]
        else:
            a_ref, b_ref, o_ref = refs[:3]
            r_ref = None
        part = _dot(a_ref[...], b_ref[...], mode)

        def finish(v):
            if alpha != 1.0:
                v = v * alpha
            if has_res:
                v = r_ref[...] + v
            o_ref[...] = v.astype(o_ref.dtype)

        if nk == 1:
            finish(part)
        else:
            acc = refs[-1]
            k = pl.program_id(len(grid) - 1)

            @pl.when(k == 0)
            def _():
                acc[...] = part

            @pl.when(k > 0)
            def _():
                acc[...] += part

            @pl.when(k == nk - 1)
            def _():
                finish(acc[...])

    in_specs = [a_spec, b_spec] + ([res_spec] if has_res else [])
    args = (a, b) + ((res,) if has_res else ())
    scratch = [] if nk == 1 else [pltpu.VMEM(acc_shape, F32)]
    sem = ("parallel",) * (len(grid) - 1) + ("arbitrary",)
    out, couts = _pcall(name, body, grid, in_specs, o_spec, out_shape, args, scratch, sem, comm)
    return out if comm is None else (out, couts)


def _mm_nn(name, a, b, out_dtype=F32, res=None, alpha=1.0, tk=None, kk=None, a_off=0, b_off=0, comm=None):
    t = a.shape[0]
    kk = a.shape[1] if kk is None else kk
    n = b.shape[1]
    tk = kk if tk is None else tk
    grid = (t // TM, 1, kk // tk)
    return _matmul(
        name, "nn", a, b, grid,
        pl.BlockSpec((TM, tk), lambda i, j, k: (i, k + a_off)),
        pl.BlockSpec((tk, n), lambda i, j, k: (k + b_off, 0)),
        pl.BlockSpec((TM, n), lambda i, j, k: (i, 0)),
        jax.ShapeDtypeStruct((t, n), out_dtype), (TM, n),
        res=res, res_spec=pl.BlockSpec((TM, n), lambda i, j, k: (i, 0)), alpha=alpha, comm=comm)


def _mm_nt(name, a, b, n=None, tn=None, tk=None, out_dtype=F32, comm=None):
    t, kk = a.shape
    n = b.shape[0] if n is None else n
    tn = n if tn is None else tn
    tk = kk if tk is None else tk
    grid = (n // tn, t // TM, kk // tk)
    return _matmul(
        name, "nt", a, b, grid,
        pl.BlockSpec((TM, tk), lambda j, i, k: (i, k)),
        pl.BlockSpec((tn, tk), lambda j, i, k: (j, k)),
        pl.BlockSpec((TM, tn), lambda j, i, k: (i, j)),
        jax.ShapeDtypeStruct((t, n), out_dtype), (TM, tn), comm=comm)


def _mm_tn(name, a, b, out_dtype, tm=None, n=None, col_off=0, comm=None):
    t, m = a.shape
    n = b.shape[1] if n is None else n
    tm = m if tm is None else tm
    grid = (m // tm, 1, t // TM)
    return _matmul(
        name, "tn", a, b, grid,
        pl.BlockSpec((TM, tm), lambda j, i, k: (k, j)),
        pl.BlockSpec((TM, n), lambda j, i, k: (k, col_off)),
        pl.BlockSpec((tm, n), lambda j, i, k: (j, 0)),
        jax.ShapeDtypeStruct((m, n), out_dtype), (tm, n), comm=comm)


def _rms_fwd(name, x, w):
    t, d = x.shape

    def body(x_ref, w_ref, h_ref):
        xv = x_ref[...]
        rstd = lax.rsqrt(jnp.mean(xv * xv, axis=-1, keepdims=True) + NORM_EPS)
        h_ref[...] = (xv * rstd * w_ref[...]).astype(h_ref.dtype)

    return pl.pallas_call(
        body, name=name, grid=(t // TE,),
        in_specs=[pl.BlockSpec((TE, d), lambda i: (i, 0)), pl.BlockSpec((1, d), lambda i: (0, 0))],
        out_specs=pl.BlockSpec((TE, d), lambda i: (i, 0)),
        out_shape=jax.ShapeDtypeStruct((t, d), BF), compiler_params=_params("parallel"))(x, w)


def _rms_bwd(name, x, w, dh, dres, out_scale, comm=None):
    t, d = x.shape

    def body(x_ref, w_ref, dh_ref, dres_ref, dx_ref, dxb_ref, dw_ref):
        i = pl.program_id(0)
        xv = x_ref[...]
        rstd = lax.rsqrt(jnp.mean(xv * xv, axis=-1, keepdims=True) + NORM_EPS)
        xhat = xv * rstd
        dhv = dh_ref[...]
        wd = dhv * w_ref[...]
        proj = jnp.mean(wd * xhat, axis=-1, keepdims=True)
        dx = dres_ref[...] + rstd * (wd - xhat * proj)
        dx_ref[...] = dx
        dxb_ref[...] = (dx * out_scale).astype(BF)
        part = jnp.sum(dhv * xhat, axis=0, keepdims=True)

        @pl.when(i == 0)
        def _():
            dw_ref[...] = part

        @pl.when(i > 0)
        def _():
            dw_ref[...] += part

    row = pl.BlockSpec((TE, d), lambda i: (i, 0))
    vec = pl.BlockSpec((1, d), lambda i: (0, 0))
    outs, couts = _pcall(
        name, body, (t // TE,), [row, vec, row, row], [row, row, vec],
        [jax.ShapeDtypeStruct((t, d), F32), jax.ShapeDtypeStruct((t, d), BF), jax.ShapeDtypeStruct((1, d), F32)],
        (x, w, dh, dres), (), ("arbitrary",), comm)
    return outs if comm is None else (outs, couts)


def _final_loss(x, w, target):
    t, d = x.shape

    def body(x_ref, w_ref, t_ref, loss_ref, dx_ref, dxb_ref, dw_ref):
        i = pl.program_id(0)
        xv = x_ref[...]
        rstd = lax.rsqrt(jnp.mean(xv * xv, axis=-1, keepdims=True) + NORM_EPS)
        xhat = xv * rstd
        err = xhat * w_ref[...] - t_ref[...]
        lpart = 0.5 * jnp.sum(jnp.mean(err * err, axis=-1, keepdims=True), axis=0, keepdims=True)
        dy = err * (1.0 / d)
        wd = dy * w_ref[...]
        proj = jnp.mean(wd * xhat, axis=-1, keepdims=True)
        dx = rstd * (wd - xhat * proj)
        dx_ref[...] = dx
        dxb_ref[...] = (0.5 * dx).astype(BF)
        part = jnp.sum(dy * xhat, axis=0, keepdims=True)
        lfull = jnp.broadcast_to(lpart, (1, 128))

        @pl.when(i == 0)
        def _():
            dw_ref[...] = part
            loss_ref[...] = lfull

        @pl.when(i > 0)
        def _():
            dw_ref[...] += part
            loss_ref[...] += lfull

    row = pl.BlockSpec((TE, d), lambda i: (i, 0))
    vec = pl.BlockSpec((1, d), lambda i: (0, 0))
    return pl.pallas_call(
        body, name="final_loss", grid=(t // TE,), in_specs=[row, vec, row],
        out_specs=[pl.BlockSpec((1, 128), lambda i: (0, 0)), row, row, vec],
        out_shape=[jax.ShapeDtypeStruct((1, 128), F32), jax.ShapeDtypeStruct((t, d), F32),
                   jax.ShapeDtypeStruct((t, d), BF), jax.ShapeDtypeStruct((1, d), F32)],
        compiler_params=_params("arbitrary"))(x, w, target)


def _swiglu_fwd(name, gu, comm=None):
    t = gu.shape[0]

    def body(g_ref, u_ref, a_ref):
        g = g_ref[...].astype(F32)
        a_ref[...] = (g * _sigmoid(g) * u_ref[...].astype(F32)).astype(BF)

    blk = (TE, FF_HALF)
    out, couts = _pcall(
        name, body, (t // TE, 2),
        [pl.BlockSpec(blk, lambda i, j: (i, 2 * j)), pl.BlockSpec(blk, lambda i, j: (i, 2 * j + 1))],
        pl.BlockSpec(blk, lambda i, j: (i, j)), jax.ShapeDtypeStruct((t, D_FF), BF),
        (gu, gu), (), ("parallel", "parallel"), comm)
    return out if comm is None else (out, couts)


def _swiglu_bwd(name, gu, dact, comm=None):
    t = gu.shape[0]

    def body(g_ref, u_ref, da_ref, o_ref):
        g = g_ref[...].astype(F32)
        da = da_ref[...].astype(F32)
        s = _sigmoid(g)
        o_ref[:, 0:FF_HALF] = (da * u_ref[...].astype(F32) * (s * (1.0 + g * (1.0 - s)))).astype(BF)
        o_ref[:, FF_HALF:2 * FF_HALF] = (da * g * s).astype(BF)

    blk = (TE, FF_HALF)
    out, couts = _pcall(
        name, body, (t // TE, 2),
        [pl.BlockSpec(blk, lambda i, j: (i, 2 * j)), pl.BlockSpec(blk, lambda i, j: (i, 2 * j + 1)),
         pl.BlockSpec(blk, lambda i, j: (i, j))],
        pl.BlockSpec((TE, 2 * FF_HALF), lambda i, j: (i, j)),
        jax.ShapeDtypeStruct((t, 2 * D_FF), BF), (gu, gu, dact), (), ("parallel", "parallel"), comm)
    return out if comm is None else (out, couts)


CONV_CB = 256


CONV_ROWS = 64
CONV_HALO = 16


def _taps_down(ext, w, k):
    shifted = [pltpu.roll(ext, k - 1 - j, 0)[CONV_HALO:] for j in range(k - 1)] + [ext[CONV_HALO:]]
    out = shifted[k - 1] * w[k - 1:k, :]
    for j in range(k - 1):
        out = out + shifted[j] * w[j:j + 1, :]
    return out, shifted


def _taps_up(ext, w, k):
    rows = ext.shape[0]
    n = rows - CONV_HALO
    out = ext[:n] * w[k - 1:k, :]
    for j in range(k - 1):
        out = out + pltpu.roll(ext, rows - (k - 1 - j), 0)[:n] * w[j:j + 1, :]
    return out


def _rows_before(ref, i, r0):
    start = pl.multiple_of(jnp.maximum(r0 - CONV_HALO, 0), CONV_HALO)
    return jnp.where(i > 0, ref[pl.ds(start, CONV_HALO), :].astype(F32), 0.0)


def _rows_after(ref, r0, t):
    start = pl.multiple_of(jnp.minimum(r0 + CONV_ROWS, t - CONV_HALO), CONV_HALO)
    return ref[pl.ds(start, CONV_HALO), :].astype(F32)


def _fold8(v):
    return v.reshape(v.shape[0] // 8, 8, v.shape[1]).sum(axis=0)


def _silu_grad(pre):
    s = _sigmoid(pre)
    return s * (1.0 + pre * (1.0 - s))


def _pspec(t, off):
    base = off // CONV_CB
    return pl.BlockSpec((t, CONV_CB), lambda j: (0, base + j))


def _mix_a_fwd(p, conv_w):
    t = p.shape[0]

    def body(b_ref, c_ref, xa_ref, w_ref, o_ref):
        w = w_ref[...]

        def step(i, carry):
            r0 = pl.multiple_of(i * CONV_ROWS, CONV_ROWS)
            rows = pl.ds(r0, CONV_ROWS)
            q = c_ref[rows, :].astype(F32) * xa_ref[rows, :].astype(F32)
            q_before = _rows_before(c_ref, i, r0) * _rows_before(xa_ref, i, r0)
            va, _ = _taps_down(jnp.concatenate([q_before, q], axis=0), w, 3)
            o_ref[rows, :] = (b_ref[rows, :].astype(F32) * va).astype(BF)
            return carry

        lax.fori_loop(0, t // CONV_ROWS, step, 0)

    return pl.pallas_call(
        body, name="mix_a_fwd", grid=(D_MODEL // CONV_CB,),
        in_specs=[_pspec(t, OFF_B), _pspec(t, OFF_C), _pspec(t, OFF_XA),
                  pl.BlockSpec((3, CONV_CB), lambda j: (0, j))],
        out_specs=pl.BlockSpec((t, CONV_CB), lambda j: (0, j)),
        out_shape=jax.ShapeDtypeStruct((t, D_MODEL), BF), compiler_params=_params("parallel"))(p, p, p, conv_w)


def _mix_a_bwd(p, conv_w, dya, dp):
    t = p.shape[0]

    def body(b_ref, c_ref, xa_ref, w_ref, dy_ref, dp_in, dp_ref, dw_ref):
        del dp_in
        w = w_ref[...]
        n = t // CONV_ROWS

        def step(i, acc):
            r0 = pl.multiple_of(i * CONV_ROWS, CONV_ROWS)
            rows = pl.ds(r0, CONV_ROWS)
            cv = c_ref[rows, :].astype(F32)
            xav = xa_ref[rows, :].astype(F32)
            q_before = _rows_before(c_ref, i, r0) * _rows_before(xa_ref, i, r0)
            va, shifted = _taps_down(jnp.concatenate([q_before, cv * xav], axis=0), w, 3)
            dyv = dy_ref[rows, :]
            dp_ref[rows, 0:CONV_CB] = (dyv * va).astype(BF)
            dv = dyv * b_ref[rows, :].astype(F32)
            dv_after = jnp.where(i < n - 1, _rows_after(dy_ref, r0, t) * _rows_after(b_ref, r0, t), 0.0)
            dq = _taps_up(jnp.concatenate([dv, dv_after], axis=0), w, 3)
            dp_ref[rows, CONV_CB:2 * CONV_CB] = (dq * xav).astype(BF)
            dp_ref[rows, 2 * CONV_CB:3 * CONV_CB] = (dq * cv).astype(BF)
            return tuple(a + _fold8(dv * s) for a, s in zip(acc, shifted))

        zero = jnp.zeros((8, CONV_CB), F32)
        acc = lax.fori_loop(0, n, step, (zero, zero, zero))
        for j in range(3):
            dw_ref[j:j + 1, :] = jnp.sum(acc[j], axis=0, keepdims=True)

    col = pl.BlockSpec((t, CONV_CB), lambda j: (0, j))
    wsp = pl.BlockSpec((3, CONV_CB), lambda j: (0, j))
    return pl.pallas_call(
        body, name="mix_a_bwd", grid=(D_MODEL // CONV_CB,),
        in_specs=[_pspec(t, OFF_B), _pspec(t, OFF_C), _pspec(t, OFF_XA), wsp, col, pl.BlockSpec(memory_space=pl.ANY)],
        out_specs=[pl.BlockSpec((t, 3 * CONV_CB), lambda j: (0, j)), wsp],
        out_shape=[jax.ShapeDtypeStruct(dp.shape, dp.dtype), jax.ShapeDtypeStruct((3, D_MODEL), F32)],
        input_output_aliases={5: 0},
        compiler_params=_params("parallel"))(p, p, p, conv_w, dya, dp)


def _ssm_conv_fwd(p, conv_w, conv_b):
    t = p.shape[0]

    def body(x_ref, w_ref, b_ref, o_ref):
        w = w_ref[...]
        bias = b_ref[...]

        def step(i, carry):
            r0 = pl.multiple_of(i * CONV_ROWS, CONV_ROWS)
            rows = pl.ds(r0, CONV_ROWS)
            ext = jnp.concatenate([_rows_before(x_ref, i, r0), x_ref[rows, :].astype(F32)], axis=0)
            pre = _taps_down(ext, w, 4)[0] + bias
            o_ref[rows, :] = pre * _sigmoid(pre)
            return carry

        lax.fori_loop(0, t // CONV_ROWS, step, 0)

    return pl.pallas_call(
        body, name="ssm_conv_fwd", grid=(D_XBC // CONV_CB,),
        in_specs=[_pspec(t, OFF_XBC), pl.BlockSpec((4, CONV_CB), lambda j: (0, j)),
                  pl.BlockSpec((1, CONV_CB), lambda j: (0, j))],
        out_specs=pl.BlockSpec((t, CONV_CB), lambda j: (0, j)),
        out_shape=jax.ShapeDtypeStruct((t, D_XBC), F32), compiler_params=_params("parallel"))(p, conv_w, conv_b)


def _ssm_conv_bwd(p, conv_w, conv_b, dxc, dp):
    t = p.shape[0]

    def body(x_ref, w_ref, b_ref, d_ref, dp_in, dx_ref, dw_ref, db_ref):
        del dp_in
        w = w_ref[...]
        bias = b_ref[...]
        n = t // CONV_ROWS

        def step(i, acc):
            r0 = pl.multiple_of(i * CONV_ROWS, CONV_ROWS)
            rows = pl.ds(r0, CONV_ROWS)
            x_cur = x_ref[rows, :].astype(F32)
            pre, shifted = _taps_down(jnp.concatenate([_rows_before(x_ref, i, r0), x_cur], axis=0), w, 4)
            pre = pre + bias
            dpre = d_ref[rows, :] * _silu_grad(pre)
            ext_after = jnp.concatenate([x_cur[CONV_ROWS - CONV_HALO:], _rows_after(x_ref, r0, t)], axis=0)
            pre_after = _taps_down(ext_after, w, 4)[0] + bias
            dpre_after = jnp.where(i < n - 1, _rows_after(d_ref, r0, t) * _silu_grad(pre_after), 0.0)
            dx_ref[rows, :] = _taps_up(jnp.concatenate([dpre, dpre_after], axis=0), w, 4).astype(BF)
            new = tuple(a + _fold8(dpre * s) for a, s in zip(acc[:4], shifted))
            return new + (acc[4] + _fold8(dpre),)

        zero = jnp.zeros((8, CONV_CB), F32)
        acc = lax.fori_loop(0, n, step, (zero,) * 5)
        for j in range(4):
            dw_ref[j:j + 1, :] = jnp.sum(acc[j], axis=0, keepdims=True)
        db_ref[...] = jnp.sum(acc[4], axis=0, keepdims=True)

    col = pl.BlockSpec((t, CONV_CB), lambda j: (0, j))
    wsp = pl.BlockSpec((4, CONV_CB), lambda j: (0, j))
    bsp = pl.BlockSpec((1, CONV_CB), lambda j: (0, j))
    return pl.pallas_call(
        body, name="ssm_conv_bwd", grid=(D_XBC // CONV_CB,),
        in_specs=[_pspec(t, OFF_XBC), wsp, bsp, col, pl.BlockSpec(memory_space=pl.ANY)],
        out_specs=[_pspec(t, OFF_XBC), wsp, bsp],
        out_shape=[jax.ShapeDtypeStruct(dp.shape, dp.dtype), jax.ShapeDtypeStruct((4, D_XBC), F32),
                   jax.ShapeDtypeStruct((1, D_XBC), F32)],
        input_output_aliases={4: 0},
        compiler_params=_params("parallel"))(p, conv_w, conv_b, dxc, dp)


DT_ROWS = 512


def _tri(lower):
    r = lax.broadcasted_iota(jnp.int32, (CHUNK, CHUNK), 0)
    c = lax.broadcasted_iota(jnp.int32, (CHUNK, CHUNK), 1)
    return jnp.where((r >= c) if lower else (r <= c), 1.0, 0.0).astype(F32)


def _dot_exact(a, b):
    return lax.dot_general(a, b, _DIMS["nn"], preferred_element_type=F32, precision=lax.Precision.HIGHEST)


def _dt_fwd(p, bias_pad, alog_pad):
    t = p.shape[0]

    def body(raw_ref, b_ref, al_ref, dt_ref, acs_ref):
        z = raw_ref[...] + b_ref[...]
        dt = jnp.maximum(z, 0.0) + jnp.log(1.0 + jnp.exp(-jnp.abs(z)))
        dt_ref[...] = dt
        a = dt * (-jnp.exp(al_ref[...]))
        tri = _tri(True)
        for k in range(DT_ROWS // CHUNK):
            acs_ref[k * CHUNK:(k + 1) * CHUNK, :] = _dot_exact(tri, a[k * CHUNK:(k + 1) * CHUNK, :])

    blk = pl.BlockSpec((DT_ROWS, DT_W), lambda i: (i, 0))
    vec = pl.BlockSpec((1, DT_W), lambda i: (0, 0))
    return pl.pallas_call(
        body, name="dt_fwd", grid=(t // DT_ROWS,),
        in_specs=[pl.BlockSpec((DT_ROWS, DT_W), lambda i: (i, OFF_DT // DT_W)), vec, vec],
        out_specs=[blk, blk], out_shape=[jax.ShapeDtypeStruct((t, DT_W), F32)] * 2,
        compiler_params=_params("parallel"))(p, bias_pad, alog_pad)


def _dt_bwd(p, bias_pad, alog_pad, dt, ddt, dacs, dp_gd):
    t = p.shape[0]

    def body(raw_ref, b_ref, al_ref, dt_ref, ddt_ref, dacs_ref, dp_in, draw_ref, db_ref, dal_ref):
        del dp_in
        i = pl.program_id(0)
        acoef = -jnp.exp(al_ref[...])
        triu = _tri(False)
        das = []
        for k in range(DT_ROWS // CHUNK):
            das.append(_dot_exact(triu, dacs_ref[k * CHUNK:(k + 1) * CHUNK, :]))
        da = jnp.concatenate(das, axis=0)
        dtv = dt_ref[...]
        ddt_tot = ddt_ref[...] + da * acoef
        lane = lax.broadcasted_iota(jnp.int32, (DT_ROWS, DT_W), 1)
        draw = jnp.where(lane < N_HEADS, ddt_tot * _sigmoid(raw_ref[...] + b_ref[...]), 0.0)
        draw_ref[...] = draw.astype(BF)
        pb = jnp.sum(draw, axis=0, keepdims=True)
        pa = jnp.sum(da * dtv * acoef, axis=0, keepdims=True)

        @pl.when(i == 0)
        def _():
            db_ref[...] = pb
            dal_ref[...] = pa

        @pl.when(i > 0)
        def _():
            db_ref[...] += pb
            dal_ref[...] += pa

    blk = pl.BlockSpec((DT_ROWS, DT_W), lambda i: (i, 0))
    vec = pl.BlockSpec((1, DT_W), lambda i: (0, 0))
    return pl.pallas_call(
        body, name="dt_bwd", grid=(t // DT_ROWS,),
        in_specs=[pl.BlockSpec((DT_ROWS, DT_W), lambda i: (i, OFF_DT // DT_W)), vec, vec, blk, blk, blk,
                  pl.BlockSpec(memory_space=pl.ANY)],
        out_specs=[pl.BlockSpec((DT_ROWS, DT_W), lambda i: (i, OFF_DT // DT_W)), vec, vec],
        out_shape=[jax.ShapeDtypeStruct(dp_gd.shape, dp_gd.dtype), jax.ShapeDtypeStruct((1, DT_W), F32),
                   jax.ShapeDtypeStruct((1, DT_W), F32)],
        input_output_aliases={6: 0},
        compiler_params=_params("arbitrary"))(p, bias_pad, alog_pad, dt, ddt, dacs, dp_gd)


def _split_dot(z, onehot, terms):
    out = None
    rest = z
    for _ in range(terms):
        piece = rest.astype(BF)
        part = _dot(piece, onehot)
        out = part if out is None else out + part
        rest = rest - piece.astype(F32)
    return out


def _spread_mat(g):
    row = lax.broadcasted_iota(jnp.int32, (DT_W, GROUP_W), 0)
    lane = lax.broadcasted_iota(jnp.int32, (DT_W, GROUP_W), 1)
    return jnp.where(row == HEADS_PER_GROUP * g + lane // HEAD_DIM, 1.0, 0.0).astype(BF)


def _gather_mat(g):
    row = lax.broadcasted_iota(jnp.int32, (GROUP_W, DT_W), 0)
    lane = lax.broadcasted_iota(jnp.int32, (GROUP_W, DT_W), 1)
    return jnp.where(lane == HEADS_PER_GROUP * g + row // HEAD_DIM, 1.0, 0.0).astype(BF)


def _ssd_masks():
    row = lax.broadcasted_iota(jnp.int32, (CHUNK, GROUP_W), 0)
    col = lax.broadcasted_iota(jnp.int32, (CHUNK, GROUP_W), 1) % HEAD_DIM
    brow = lax.broadcasted_iota(jnp.int32, (GROUP_W, GROUP_W), 0) // HEAD_DIM
    bcol = lax.broadcasted_iota(jnp.int32, (GROUP_W, GROUP_W), 1) // HEAD_DIM
    return row >= col, row == col, brow == bcol


def _stack4(v):
    return jnp.concatenate([v, v, v, v], axis=0)


def _fold4(v):
    return v[0:CHUNK] + v[CHUNK:2 * CHUNK] + v[2 * CHUNK:3 * CHUNK] + v[3 * CHUNK:4 * CHUNK]


def _ssd_group(xc_ref, stacked, g, tri, eye, blockdiag):
    gs = slice(GROUP_W * g, GROUP_W * (g + 1))
    xs_g = xc_ref[:, gs]
    b_g = xc_ref[:, D_INNER + D_STATE * g:D_INNER + D_STATE * (g + 1)].astype(BF)
    c_g = xc_ref[:, D_INNER + 1024 + D_STATE * g:D_INNER + 1024 + D_STATE * (g + 1)].astype(BF)
    wide = _split_dot(stacked, _spread_mat(g), 3)
    acs_e, dt_e = wide[0:CHUNK], wide[CHUNK:2 * CHUNK]
    atot_e = acs_e[CHUNK - 1:CHUNK, :]
    acs_j = jnp.sum(jnp.where(eye, acs_e, 0.0), axis=0, keepdims=True)
    lmat = jnp.where(tri, jnp.exp(jnp.minimum(acs_e - acs_j, 0.0)), 0.0)
    b_t = _stack4(b_g)
    m = _dot(c_g, b_t, "nt") * lmat
    x_g = xs_g * dt_e
    xbd = jnp.where(blockdiag, _stack4(x_g), 0.0).astype(BF)
    return dict(gs=gs, xs=xs_g, b=b_g, c=c_g, b_t=b_t, dt=dt_e, e=jnp.exp(acs_e), dec=jnp.exp(atot_e - acs_e),
                eat=jnp.exp(atot_e), lmat=lmat, m=m, x=x_g, xbd=xbd)


def _ssd_fwd(xconv, dt, acs, d_exp, comm=None):
    t = xconv.shape[0]
    nc = t // CHUNK

    def body(xc_ref, dt_ref, acs_ref, d_ref, y_ref, hs_ref, state):
        c = pl.program_id(0)

        @pl.when(c == 0)
        def _():
            state[...] = jnp.zeros_like(state)

        hs_ref[...] = state[...]
        tri, eye, blockdiag = _ssd_masks()
        stacked = jnp.concatenate([acs_ref[...], dt_ref[...]], axis=0)
        for g in range(N_GROUPS):
            q = _ssd_group(xc_ref, stacked, g, tri, eye, blockdiag)
            gs = q["gs"]
            h_t = state[:, gs]
            ydiag = _dot(q["m"].astype(BF), q["xbd"])
            yoff = _dot(q["c"], h_t.astype(BF)) * q["e"]
            y_ref[:, gs] = ydiag + yoff + d_ref[:, gs] * q["xs"]
            s_t = _dot(q["b"], (q["x"] * q["dec"]).astype(BF), "tn")
            state[:, gs] = q["eat"] * h_t + s_t

    blk = lambda w: pl.BlockSpec((CHUNK, w), lambda c: (c, 0))
    outs, couts = _pcall(
        "ssd_fwd", body, (nc,),
        [blk(D_XBC), blk(DT_W), blk(DT_W), pl.BlockSpec((1, D_INNER), lambda c: (0, 0))],
        [blk(D_INNER), pl.BlockSpec((None, D_STATE, D_INNER), lambda c: (c, 0, 0))],
        [jax.ShapeDtypeStruct((t, D_INNER), F32), jax.ShapeDtypeStruct((nc, D_STATE, D_INNER), F32)],
        (xconv, dt, acs, d_exp), [pltpu.VMEM((D_STATE, D_INNER), F32)], ("arbitrary",), comm)
    return outs if comm is None else (outs, couts)


def _ssd_bwd(xconv, dt, acs, d_exp, hsave, dy, comm=None):
    t = xconv.shape[0]
    nc = t // CHUNK

    def body(xc_ref, dt_ref, acs_ref, d_ref, hs_ref, dy_ref, dxc_ref, ddt_ref, dacs_ref, dd_ref, dstate):
        c = pl.program_id(0)

        @pl.when(c == 0)
        def _():
            dstate[...] = jnp.zeros_like(dstate)
            dd_ref[...] = jnp.zeros_like(dd_ref)

        tri, eye, blockdiag = _ssd_masks()
        acsv = acs_ref[...]
        stacked = jnp.concatenate([acsv, dt_ref[...]], axis=0)
        eat_heads = jnp.exp(acsv[CHUNK - 1:CHUNK, :])
        ddt_acc = jnp.zeros((CHUNK, DT_W), F32)
        dacs_acc = jnp.zeros((CHUNK, DT_W), F32)
        datot_acc = jnp.zeros((1, DT_W), F32)

        for g in range(N_GROUPS):
            q = _ssd_group(xc_ref, stacked, g, tri, eye, blockdiag)
            gs, xs_g, b_g, c_g, m = q["gs"], q["xs"], q["b"], q["c"], q["m"]
            bs = slice(D_INNER + D_STATE * g, D_INNER + D_STATE * (g + 1))
            cs = slice(D_INNER + 1024 + D_STATE * g, D_INNER + 1024 + D_STATE * (g + 1))
            h_t = hs_ref[:, gs]
            h_b = h_t.astype(BF)
            dy_g = dy_ref[:, gs]
            dy_b = dy_g.astype(BF)
            ds_t = dstate[:, gs]
            ds_b = ds_t.astype(BF)

            yoff = _dot(c_g, h_b) * q["e"]
            edy = (q["e"] * dy_g).astype(BF)
            d_c = _dot(edy, h_b, "nt")
            d_ht = _dot(c_g, edy, "tn")
            bds = _dot(b_g, ds_b)
            xd = q["x"] * q["dec"]
            d_b = _dot(xd.astype(BF), ds_b, "nt")
            dm = _dot(dy_b, q["xbd"], "nt")
            cross = _dot(m.astype(BF), dy_b, "tn")
            dx_full = q["dec"] * bds + _fold4(jnp.where(blockdiag, cross, 0.0))
            dml = (dm * q["lmat"]).astype(BF)
            d_c = d_c + _dot(dml, q["b_t"])
            d_b = d_b + _fold4(_dot(dml, c_g, "tn"))
            w = dm * m
            q_dec = xd * bds
            z = w - jnp.where(eye, jnp.sum(w, axis=0, keepdims=True), 0.0) + dy_g * yoff - q_dec
            rows = jnp.concatenate(
                [jnp.sum(q_dec, axis=0, keepdims=True), jnp.sum(ds_t * h_t, axis=0, keepdims=True),
                 jnp.zeros((6, GROUP_W), F32)], axis=0)
            seg = _split_dot(jnp.concatenate([z, dx_full * xs_g, rows], axis=0), _gather_mat(g), 2)
            dacs_acc = dacs_acc + seg[0:CHUNK]
            ddt_acc = ddt_acc + seg[CHUNK:2 * CHUNK]
            datot_acc = datot_acc + seg[2 * CHUNK:2 * CHUNK + 1] + eat_heads * seg[2 * CHUNK + 1:2 * CHUNK + 2]
            dxc_ref[:, cs] = d_c
            dxc_ref[:, bs] = d_b
            dxc_ref[:, gs] = dx_full * q["dt"] + d_ref[:, gs] * dy_g
            dd_ref[:, gs] += jnp.sum(dy_g * xs_g, axis=0, keepdims=True)
            dstate[:, gs] = q["eat"] * ds_t + d_ht

        rowi = lax.broadcasted_iota(jnp.int32, (CHUNK, DT_W), 0)
        ddt_ref[...] = ddt_acc
        dacs_ref[...] = dacs_acc + jnp.where(rowi == CHUNK - 1, datot_acc, 0.0)

    rev = lambda w: pl.BlockSpec((CHUNK, w), lambda c: (nc - 1 - c, 0))
    vec = pl.BlockSpec((1, D_INNER), lambda c: (0, 0))
    outs, couts = _pcall(
        "ssd_bwd", body, (nc,),
        [rev(D_XBC), rev(DT_W), rev(DT_W), vec,
         pl.BlockSpec((None, D_STATE, D_INNER), lambda c: (nc - 1 - c, 0, 0)), rev(D_INNER)],
        [rev(D_XBC), rev(DT_W), rev(DT_W), vec],
        [jax.ShapeDtypeStruct((t, D_XBC), F32), jax.ShapeDtypeStruct((t, DT_W), F32),
         jax.ShapeDtypeStruct((t, DT_W), F32), jax.ShapeDtypeStruct((1, D_INNER), F32)],
        (xconv, dt, acs, d_exp, hsave, dy),
        [pltpu.VMEM((D_STATE, D_INNER), F32)], ("arbitrary",), comm)
    return outs if comm is None else (outs, couts)


GN_CB = 1024
GN_GROUPS = GN_CB // GROUP_W


def _gnorm_fwd(y, p, w, comm=None):
    t = y.shape[0]
    zoff = OFF_Z // GN_CB

    def body(y_ref, z_ref, w_ref, o_ref):
        for g in range(GN_GROUPS):
            gs = slice(GROUP_W * g, GROUP_W * (g + 1))
            z = z_ref[:, gs].astype(F32)
            yf = y_ref[:, gs] * (z * _sigmoid(z))
            rstd = lax.rsqrt(jnp.mean(yf * yf, axis=-1, keepdims=True) + NORM_EPS)
            o_ref[:, gs] = (yf * rstd * w_ref[:, gs]).astype(BF)

    blk = pl.BlockSpec((TE, GN_CB), lambda i, j: (i, j))
    out, couts = _pcall(
        "gnorm_fwd", body, (t // TE, D_INNER // GN_CB),
        [blk, pl.BlockSpec((TE, GN_CB), lambda i, j: (i, zoff + j)), pl.BlockSpec((1, GN_CB), lambda i, j: (0, j))],
        blk, jax.ShapeDtypeStruct((t, D_INNER), BF), (y, p, w), (), ("parallel", "parallel"), comm)
    return out if comm is None else (out, couts)


def _gnorm_bwd(y, p, w, dyn, comm=None):
    t = y.shape[0]
    zoff = OFF_Z // GN_CB

    def body(y_ref, z_ref, w_ref, dn_ref, dy_ref, dz_ref, dw_ref):
        i = pl.program_id(1)
        for g in range(GN_GROUPS):
            gs = slice(GROUP_W * g, GROUP_W * (g + 1))
            z = z_ref[:, gs].astype(F32)
            yv = y_ref[:, gs]
            s = _sigmoid(z)
            sil = z * s
            yf = yv * sil
            rstd = lax.rsqrt(jnp.mean(yf * yf, axis=-1, keepdims=True) + NORM_EPS)
            xhat = yf * rstd
            dn = dn_ref[:, gs]
            wd = dn * w_ref[:, gs]
            proj = jnp.mean(wd * xhat, axis=-1, keepdims=True)
            dyf = rstd * (wd - xhat * proj)
            dy_ref[:, gs] = dyf * sil
            dz_ref[:, gs] = (dyf * yv * (s * (1.0 + z * (1.0 - s)))).astype(BF)
            part = jnp.sum(dn * xhat, axis=0, keepdims=True)

            @pl.when(i == 0)
            def _():
                dw_ref[:, gs] = part

            @pl.when(i > 0)
            def _():
                dw_ref[:, gs] += part

    blk = pl.BlockSpec((TE, GN_CB), lambda j, i: (i, j))
    vec = pl.BlockSpec((1, GN_CB), lambda j, i: (0, j))
    outs, couts = _pcall(
        "gnorm_bwd", body, (D_INNER // GN_CB, t // TE),
        [blk, pl.BlockSpec((TE, GN_CB), lambda j, i: (i, zoff + j)), vec, blk],
        [blk, pl.BlockSpec((TE, GN_CB), lambda j, i: (i, zoff + j)), vec],
        [jax.ShapeDtypeStruct((t, D_INNER), F32), jax.ShapeDtypeStruct((t, N_MAIN), BF),
         jax.ShapeDtypeStruct((1, D_INNER), F32)],
        (y, p, w, dyn), (), ("parallel", "arbitrary"), comm)
    return outs if comm is None else (outs, couts)


MERGE_CB = 512


def _merge_fwd(p, ya, yb):
    t = ya.shape[0]

    def body(ga_ref, gb_ref, ya_ref, yb_ref, o_ref):
        o_ref[...] = (_sigmoid(ga_ref[...]) * ya_ref[...] + _sigmoid(gb_ref[...]) * yb_ref[...]).astype(BF)

    blk = pl.BlockSpec((TE, MERGE_CB), lambda i, j: (i, j))
    return pl.pallas_call(
        body, name="merge_fwd", grid=(t // TE, D_MODEL // MERGE_CB),
        in_specs=[pl.BlockSpec((TE, MERGE_CB), lambda i, j: (i, 2 * j)),
                  pl.BlockSpec((TE, MERGE_CB), lambda i, j: (i, 2 * j + 1)), blk, blk],
        out_specs=blk, out_shape=jax.ShapeDtypeStruct((t, D_MODEL), BF),
        compiler_params=_params("parallel", "parallel"))(p, p, ya, yb)


def _merge_bwd(p, ya, yb, dm):
    t = ya.shape[0]

    def body(ga_ref, gb_ref, ya_ref, yb_ref, dm_ref, dg_ref, dya_ref, dyb_ref):
        d = dm_ref[...]
        sa = _sigmoid(ga_ref[...])
        sb = _sigmoid(gb_ref[...])
        dg_ref[:, 0:MERGE_CB] = (d * ya_ref[...] * sa * (1.0 - sa)).astype(BF)
        dg_ref[:, MERGE_CB:2 * MERGE_CB] = (d * yb_ref[...] * sb * (1.0 - sb)).astype(BF)
        dya_ref[...] = (d * sa).astype(BF)
        dyb_ref[...] = (d * sb).astype(BF)

    blk = pl.BlockSpec((TE, MERGE_CB), lambda i, j: (i, j))
    return pl.pallas_call(
        body, name="merge_bwd", grid=(t // TE, D_MODEL // MERGE_CB),
        in_specs=[pl.BlockSpec((TE, MERGE_CB), lambda i, j: (i, 2 * j)),
                  pl.BlockSpec((TE, MERGE_CB), lambda i, j: (i, 2 * j + 1)), blk, blk, blk],
        out_specs=[pl.BlockSpec((TE, 2 * MERGE_CB), lambda i, j: (i, j)), blk, blk],
        out_shape=[jax.ShapeDtypeStruct((t, N_GD), BF)] + [jax.ShapeDtypeStruct((t, D_MODEL), BF)] * 2,
        compiler_params=_params("parallel", "parallel"))(p, p, ya, yb, dm)


def _adamw(name, parts, w, m, v, comm=None):
    r, c = w.shape
    tr = _row_tile(r)
    tc = ADAM_COL_TILE if (tr == r and r > 512 and c % ADAM_COL_TILE == 0) else c
    n_parts = parts.shape[0]
    bc1 = 1.0 - ADAM_B1 ** ADAM_STEP
    bc2 = 1.0 - ADAM_B2 ** ADAM_STEP

    def body(p_ref, w_ref, m_ref, v_ref, g_ref, d_ref, nm_ref, nv_ref):
        g = p_ref[0].astype(F32)
        for k in range(1, n_parts):
            g = g + p_ref[k].astype(F32)
        nm = ADAM_B1 * m_ref[...] + (1.0 - ADAM_B1) * g
        nv = ADAM_B2 * v_ref[...] + (1.0 - ADAM_B2) * (g * g)
        g_ref[...] = g
        nm_ref[...] = nm
        nv_ref[...] = nv
        d_ref[...] = -ADAM_LR * ((nm / bc1) / (jnp.sqrt(nv / bc2) + ADAM_EPS) + ADAM_WD * w_ref[...])

    blk = pl.BlockSpec((tr, tc), lambda i, j: (i, j))
    outs, couts = _pcall(
        name, body, (r // tr, c // tc),
        [pl.BlockSpec((n_parts, tr, tc), lambda i, j: (0, i, j)), blk, blk, blk], [blk] * 4,
        [jax.ShapeDtypeStruct((r, c), F32)] * 4, (parts, w, m, v), (), ("parallel", "parallel"), comm)
    return outs if comm is None else (outs, couts)


def _pad_lanes(v, width):
    return jnp.pad(v, ((0, 0), (0, width - v.shape[1])))


def _reduce_start(slots, host):
    outs, sib = host(_pair_comm([a for _, a in slots]))
    sums = [(n, _add_pairs("pairsum_" + n, a, b)) for (n, a), b in zip(slots, sib)]
    return outs, sums


def _train_step(x, target, shard, rep):
    gdt = BF
    t = x.shape[0]
    recv = {}
    (got,) = _comm_call("gather_ffn1_in", _gather_comm([shard["ffn1_w_in"]], [True]))
    w1_in = got.reshape(2 * D_FF, D_MODEL)
    h1 = _rms_fwd("rms1_fwd", x, rep["ffn1_norm"])
    gu1, got = _mm_nt("ffn1_in", h1, w1_in, tn=FF_HALF, out_dtype=BF, comm=_gather_comm(
        [shard["ffn1_w_out"], shard["w_in"], shard["short_conv_w"], shard["ssm_conv_w"]]))
    w1_out = got[0].reshape(D_FF, D_MODEL)
    w_in_t = got[1].reshape(N_IN, D_MODEL)
    short_conv_w = got[2].transpose(1, 0, 2).reshape(3, D_MODEL)
    ssm_conv_w = got[3].transpose(1, 0, 2).reshape(4, D_XBC)
    act1, (got,) = _swiglu_fwd("swiglu1_fwd", gu1, comm=_gather_comm([shard["short_w_out"]]))
    short_w_out = got.reshape(D_MODEL, D_MODEL)
    x1, (got,) = _mm_nn("ffn1_out", act1, w1_out, res=x, alpha=0.5, comm=_gather_comm([shard["ssm_w_out"]]))
    ssm_w_out = got.reshape(D_INNER, D_MODEL)
    ga0 = N_MAIN + N_HEADS
    gb0 = ga0 + D_MODEL
    half = D_MODEL // 2
    w_gd = jnp.concatenate(
        [w_in_t[ga0:ga0 + half], w_in_t[gb0:gb0 + half], w_in_t[ga0 + half:gb0], w_in_t[gb0 + half:],
         w_in_t[N_MAIN:N_MAIN + N_HEADS], jnp.zeros((DT_W - N_HEADS, D_MODEL), BF)], axis=0)
    w_mix_perm = w_in_t[0:3 * D_MODEL].reshape(3, 4, CONV_CB, D_MODEL).transpose(1, 0, 2, 3).reshape(3 * D_MODEL, D_MODEL)

    h2 = _rms_fwd("rms2_fwd", x1, rep["mix_norm"])
    p, got = _mm_nt("proj_main", h2, w_in_t, n=N_MAIN, tn=1024, out_dtype=BF, comm=_gather_comm(
        [shard["w_out"], shard["ffn2_w_out"]]))
    p_gd = _mm_nt("proj_gd", h2, w_gd)
    w_out = got[0].reshape(D_MODEL, D_MODEL)
    w2_out = got[1].reshape(D_FF, D_MODEL)
    ya_in = _mix_a_fwd(p, short_conv_w)
    y_a = _mm_nn("short_out", ya_in, short_w_out)
    xconv = _ssm_conv_fwd(p, ssm_conv_w, rep["ssm_conv_b"])
    dt, acs = _dt_fwd(p_gd, rep["dt_bias_pad"], rep["a_log_pad"])
    (y_ssm, hsave), (got,) = _ssd_fwd(xconv, dt, acs, rep["d_exp"], comm=_gather_comm([shard["ffn2_w_in"]], [True]))
    w2_in = got.reshape(2 * D_FF, D_MODEL)
    yn = _gnorm_fwd(y_ssm, p, rep["ssm_norm"])
    y_b = _mm_nn("ssm_out", yn, ssm_w_out, tk=1024)
    merged = _merge_fwd(p_gd, y_a, y_b)
    x2 = _mm_nn("mix_out", merged, w_out, res=x1)

    h3 = _rms_fwd("rms3_fwd", x2, rep["ffn2_norm"])
    gu2 = _mm_nt("ffn2_in", h3, w2_in, tn=FF_HALF, out_dtype=BF)
    act2 = _swiglu_fwd("swiglu2_fwd", gu2)
    x3 = _mm_nn("ffn2_out", act2, w2_out, res=x2, alpha=0.5)

    loss, dx3, dx3h, g_final = _final_loss(x3, rep["final_norm"], target)

    small = {"final_norm": g_final}
    dact2 = _mm_nt("ffn2_out_bwd_act", dx3h, w2_out, out_dtype=BF)
    g_w2_out = _mm_tn("ffn2_out_bwd_w", act2, dx3h, gdt, tm=FF_HALF)
    dgu2 = _swiglu_bwd("swiglu2_bwd", gu2, dact2)
    g_w2_in = _mm_tn("ffn2_in_bwd_w", dgu2, h3, gdt, tm=FF_HALF)
    dh3 = _mm_nn("ffn2_in_bwd_h", dgu2, w2_in, tk=FF_HALF)
    dx2, dx2b, small["ffn2_norm"] = _rms_bwd("rms3_bwd", x2, rep["ffn2_norm"], dh3, dx3, 1.0)

    dmerged = _mm_nt("mix_out_bwd_x", dx2b, w_out)
    g_w_out = _mm_tn("mix_out_bwd_w", merged, dx2b, gdt)
    dp_gd, dya, dyb = _merge_bwd(p_gd, y_a, y_b, dmerged)

    dya_in = _mm_nt("short_out_bwd_x", dya, short_w_out)
    g_short_w_out = _mm_tn("short_out_bwd_w", ya_in, dya, gdt)

    dyn = _mm_nt("ssm_out_bwd_x", dyb, ssm_w_out)
    g_ssm_w_out = _mm_tn("ssm_out_bwd_w", yn, dyb, gdt)
    late = [("ffn2_w_out", g_w2_out.reshape(N_DEV, FF_SHARD // 2, D_MODEL)),
            ("ffn2_w_in", g_w2_in.reshape(N_DEV, FF_SHARD, D_MODEL)),
            ("w_out", g_w_out.reshape(N_DEV, -1, D_MODEL)), ("short_w_out", g_short_w_out.reshape(N_DEV, -1, D_MODEL)),
            ("ssm_w_out", g_ssm_w_out.reshape(N_DEV, -1, D_MODEL))]
    (dy_ssm, dp, small["ssm_norm"]), sums = _reduce_start(
        late, lambda comm: _gnorm_bwd(y_ssm, p, rep["ssm_norm"], dyn, comm=comm))
    dp, g_short_conv = _mix_a_bwd(p, short_conv_w, dya_in, dp)
    (dxconv, ddt, dacs, dd_lane), got = _ssd_bwd(
        xconv, dt, acs, rep["d_exp"], hsave, dy_ssm,
        comm=_chip_comm([a for _, a in sums], [n == "ffn2_w_in" for n, _ in sums]))
    recv.update({n: a for (n, _), a in zip(sums, got)})
    small["ssm_D"] = dd_lane.reshape(N_HEADS, HEAD_DIM).sum(axis=1)[None, :]
    dp, g_ssm_conv, small["ssm_conv_b"] = _ssm_conv_bwd(p, ssm_conv_w, rep["ssm_conv_b"], dxconv, dp)
    dp_gd, dbias, dalog = _dt_bwd(p_gd, rep["dt_bias_pad"], rep["a_log_pad"], dt, ddt, dacs, dp_gd)
    small["ssm_dt_bias"] = dbias[:, :N_HEADS]
    small["ssm_A_log"] = dalog[:, :N_HEADS]

    g_main = _mm_tn("proj_main_bwd_w", dp, h2, gdt, tm=1024)
    g_gd = _mm_tn("proj_gd_bwd_w", dp_gd, h2, gdt)
    g_mix = g_main[0:3 * D_MODEL].reshape(4, 3, CONV_CB, D_MODEL).transpose(1, 0, 2, 3).reshape(3 * D_MODEL, D_MODEL)
    g_in_t = jnp.concatenate(
        [g_mix, g_main[3 * D_MODEL:], g_gd[2 * D_MODEL:2 * D_MODEL + N_HEADS],
         g_gd[0:half], g_gd[2 * half:3 * half], g_gd[half:2 * half], g_gd[3 * half:4 * half]], axis=0).reshape(
        N_DEV, IN_SHARD, D_MODEL)
    dh2, w_sums = _reduce_start(
        [("w_in", g_in_t)], lambda comm: _mm_nn("proj_mix_bwd_x", dp, w_mix_perm, tk=1024, kk=3 * D_MODEL, comm=comm))
    w_sum = w_sums[0][1]

    def w_piece(i):
        return _chip_comm([w_sum], rows=[W_GRAD_ROW_CUTS[i]])

    dh2, got0 = _mm_nn("proj_rest_bwd_x", dp, w_in_t, tk=1024, kk=N_MAIN - 3 * D_MODEL, a_off=3, b_off=3, res=dh2,
                       comm=w_piece(0))
    dh2, got1 = _mm_nn("proj_gd_bwd_x", dp_gd, w_gd, res=dh2, comm=w_piece(1))
    (dx1, dx1h, small["mix_norm"]), got2 = _rms_bwd("rms2_bwd", x1, rep["mix_norm"], dh2, dx2, 0.5, comm=w_piece(2))
    g_w1_out, got3 = _mm_tn("ffn1_out_bwd_w", act1, dx1h, gdt, tm=FF_HALF, comm=w_piece(3))
    rest = [("ffn1_w_out", g_w1_out.reshape(N_DEV, FF_SHARD // 2, D_MODEL)),
            ("short_conv_w", g_short_conv.reshape(3, N_DEV, -1).transpose(1, 0, 2)),
            ("ssm_conv_w", g_ssm_conv.reshape(4, N_DEV, -1).transpose(1, 0, 2))]
    dact1, got = _mm_nt("ffn1_out_bwd_act", dx1h, w1_out, out_dtype=BF,
                        comm=_join_comm(w_piece(4), _pair_comm([a for _, a in rest])))
    got4, sib = got[0], got[1:]
    rest_sums = [(n, _add_pairs("pairsum_" + n, a, b)) for (n, a), b in zip(rest, sib)]
    recv["w_in"] = jnp.concatenate([got0[0], got1[0], got2[0], got3[0], got4], axis=1)
    dgu1, got = _swiglu_bwd("swiglu1_bwd", gu1, dact1, comm=_chip_comm([a for _, a in rest_sums]))
    recv.update({n: a for (n, _), a in zip(rest_sums, got)})

    cw = D_MODEL // 2
    g_a = _mm_tn("ffn1_in_bwd_w_a", dgu1, h1, gdt, tm=FF_HALF, n=cw, col_off=0).reshape(N_DEV, FF_SHARD, cw)
    g_b, sib_a = _mm_tn("ffn1_in_bwd_w_b", dgu1, h1, gdt, tm=FF_HALF, n=cw, col_off=1, comm=_pair_comm([g_a]))
    g_b = g_b.reshape(N_DEV, FF_SHARD, cw)
    sum_a = _add_pairs("pairsum_ffn1_w_in_a", g_a, sib_a[0])
    dh1, got = _mm_nn("ffn1_in_bwd_h", dgu1, w1_in, tk=FF_HALF,
                      comm=_join_comm(_chip_comm([sum_a], [True]), _pair_comm([g_b])))
    recv_a, sib_b = got
    sum_b = _add_pairs("pairsum_ffn1_w_in_b", g_b, sib_b)
    (dx0, _, small["ffn1_norm"]), got = _rms_bwd("rms1_bwd", x, rep["ffn1_norm"], dh1, dx1, 1.0,
                                                  comm=_chip_comm([sum_b], [True]))
    recv["ffn1_w_in"] = jnp.concatenate([recv_a, got[0]], axis=2)
    return dx0, recv, _pack_small(small, loss[:, 0:1])


_SMALL = [("ffn1_norm", 1024), ("mix_norm", 1024), ("ssm_conv_b", 4096), ("ssm_dt_bias", 32), ("ssm_A_log", 32),
          ("ssm_D", 32), ("ssm_norm", 2048), ("ffn2_norm", 1024), ("final_norm", 1024)]
SMALL_W = 10368


def _pack_small(d, loss=None):
    parts = [d[n].reshape(1, -1).astype(F32) for n, _ in _SMALL]
    used = sum(sz for _, sz in _SMALL)
    tail = jnp.zeros((1, SMALL_W - used), F32)
    if loss is not None:
        tail = tail.at[:, 0:1].set(loss)
    return jnp.concatenate(parts + [tail], axis=1)


def _adamw_small(parts, w, m, v):
    n_par = len(_SMALL)
    bc1 = 1.0 - ADAM_B1 ** ADAM_STEP
    bc2 = 1.0 - ADAM_B2 ** ADAM_STEP
    used = sum(sz for _, sz in _SMALL)

    def body(*refs):
        p_ref = refs[0]
        ins = refs[1:1 + 3 * n_par]
        outs = refs[1 + 3 * n_par:]
        g_all = p_ref[0]
        for k in range(1, N_DEV):
            g_all = g_all + p_ref[k]
        off = 0
        for i, (_, sz) in enumerate(_SMALL):
            g = g_all[:, off:off + sz]
            w_ref, m_ref, v_ref = ins[3 * i:3 * i + 3]
            nm = ADAM_B1 * m_ref[...] + (1.0 - ADAM_B1) * g
            nv = ADAM_B2 * v_ref[...] + (1.0 - ADAM_B2) * (g * g)
            outs[4 * i][...] = g
            outs[4 * i + 1][...] = -ADAM_LR * ((nm / bc1) / (jnp.sqrt(nv / bc2) + ADAM_EPS) + ADAM_WD * w_ref[...])
            outs[4 * i + 2][...] = nm
            outs[4 * i + 3][...] = nv
            off += sz
        outs[4 * n_par][...] = g_all[:, used:SMALL_W]

    args = [parts]
    out_shape = []
    for name, sz in _SMALL:
        args += [w[name], m[name], v[name]]
        out_shape += [jax.ShapeDtypeStruct((1, sz), F32)] * 4
    out_shape.append(jax.ShapeDtypeStruct((1, SMALL_W - used), F32))
    res = pl.pallas_call(body, name="adamw_small", out_shape=out_shape,
                         compiler_params=pltpu.CompilerParams(vmem_limit_bytes=VMEM_LIMIT_V7X))(*args)
    return {name: tuple(res[4 * i:4 * i + 4]) for i, (name, _) in enumerate(_SMALL)}, res[-1]


_SHARDED = ["ffn1_w_in", "ffn1_w_out", "w_in", "short_conv_w", "short_w_out", "ssm_conv_w", "ssm_w_out", "w_out",
            "ffn2_w_in", "ffn2_w_out"]
_TRANSPOSED = ("ffn1_w_in", "w_in", "ffn2_w_in")
_ORDER = ["ffn1_norm", "ffn1_w_in", "ffn1_w_out", "mix_norm", "w_in", "short_conv_w", "short_w_out", "ssm_conv_w",
          "ssm_conv_b", "ssm_dt_bias", "ssm_A_log", "ssm_D", "ssm_norm", "ssm_w_out", "w_out", "ffn2_norm",
          "ffn2_w_in", "ffn2_w_out", "final_norm"]


def kernel(x, ffn1_norm, ffn1_w_in, ffn1_w_out, mix_norm, w_in, short_conv_w, short_w_out, ssm_conv_w, ssm_conv_b, ssm_dt_bias, ssm_A_log, ssm_D, ssm_norm, ssm_w_out, w_out, ffn2_norm, ffn2_w_in, ffn2_w_out, final_norm, loss_target, m_ffn1_norm, m_ffn1_w_in, m_ffn1_w_out, m_mix_norm, m_w_in, m_short_conv_w, m_short_w_out, m_ssm_conv_w, m_ssm_conv_b, m_ssm_dt_bias, m_ssm_A_log, m_ssm_D, m_ssm_norm, m_ssm_w_out, m_w_out, m_ffn2_norm, m_ffn2_w_in, m_ffn2_w_out, m_final_norm, v_ffn1_norm, v_ffn1_w_in, v_ffn1_w_out, v_mix_norm, v_w_in, v_short_conv_w, v_short_w_out, v_ssm_conv_w, v_ssm_conv_b, v_ssm_dt_bias, v_ssm_A_log, v_ssm_D, v_ssm_norm, v_ssm_w_out, v_w_out, v_ffn2_norm, v_ffn2_w_in, v_ffn2_w_out, v_final_norm):
    w = dict(ffn1_norm=ffn1_norm, ffn1_w_in=ffn1_w_in, ffn1_w_out=ffn1_w_out, mix_norm=mix_norm, w_in=w_in,
             short_conv_w=short_conv_w, short_w_out=short_w_out, ssm_conv_w=ssm_conv_w, ssm_conv_b=ssm_conv_b,
             ssm_dt_bias=ssm_dt_bias, ssm_A_log=ssm_A_log, ssm_D=ssm_D, ssm_norm=ssm_norm, ssm_w_out=ssm_w_out,
             w_out=w_out, ffn2_norm=ffn2_norm, ffn2_w_in=ffn2_w_in, ffn2_w_out=ffn2_w_out, final_norm=final_norm)
    m = dict(ffn1_norm=m_ffn1_norm, ffn1_w_in=m_ffn1_w_in, ffn1_w_out=m_ffn1_w_out, mix_norm=m_mix_norm, w_in=m_w_in,
             short_conv_w=m_short_conv_w, short_w_out=m_short_w_out, ssm_conv_w=m_ssm_conv_w,
             ssm_conv_b=m_ssm_conv_b, ssm_dt_bias=m_ssm_dt_bias, ssm_A_log=m_ssm_A_log, ssm_D=m_ssm_D,
             ssm_norm=m_ssm_norm, ssm_w_out=m_ssm_w_out, w_out=m_w_out, ffn2_norm=m_ffn2_norm,
             ffn2_w_in=m_ffn2_w_in, ffn2_w_out=m_ffn2_w_out, final_norm=m_final_norm)
    v = dict(ffn1_norm=v_ffn1_norm, ffn1_w_in=v_ffn1_w_in, ffn1_w_out=v_ffn1_w_out, mix_norm=v_mix_norm, w_in=v_w_in,
             short_conv_w=v_short_conv_w, short_w_out=v_short_w_out, ssm_conv_w=v_ssm_conv_w,
             ssm_conv_b=v_ssm_conv_b, ssm_dt_bias=v_ssm_dt_bias, ssm_A_log=v_ssm_A_log, ssm_D=v_ssm_D,
             ssm_norm=v_ssm_norm, ssm_w_out=v_ssm_w_out, w_out=v_w_out, ffn2_norm=v_ffn2_norm,
             ffn2_w_in=v_ffn2_w_in, ffn2_w_out=v_ffn2_w_out, final_norm=v_final_norm)
    shapes = {n: w[n].shape for n in _ORDER}

    def local(d, n):
        return d[n][0].T if n in _TRANSPOSED else d[n][0]

    shard = {n: local(w, n) for n in _SHARDED}

    wire = {n: (shard[n] if n in ("short_conv_w", "ssm_conv_w") else shard[n].astype(BF)) for n in _SHARDED}
    rep = {
        "ffn1_norm": ffn1_norm, "mix_norm": mix_norm, "ffn2_norm": ffn2_norm, "ssm_norm": ssm_norm,
        "ssm_conv_b": ssm_conv_b, "final_norm": final_norm.reshape(1, D_MODEL),
        "dt_bias_pad": _pad_lanes(ssm_dt_bias, DT_W), "a_log_pad": _pad_lanes(ssm_A_log, DT_W),
        "d_exp": jnp.repeat(ssm_D, HEAD_DIM, axis=1),
    }
    grad_x, parts, packed = _train_step(x[0], loss_target[0], wire, rep)

    out_g, out_d, out_m, out_v = {}, {}, {}, {}
    for n in _SHARDED:
        if n == "ssm_w_out":
            res, (small_parts,) = _adamw("adamw_" + n, parts[n], shard[n], local(m, n), local(v, n),
                                         comm=_gather_comm([packed]))
        else:
            res = _adamw("adamw_" + n, parts[n], shard[n], local(m, n), local(v, n))
        out_g[n], out_d[n], out_m[n], out_v[n] = [(r.T if n in _TRANSPOSED else r).reshape(shapes[n]) for r in res]
    row = lambda d: {n: d[n].reshape(1, -1) for n, _ in _SMALL}
    sres, loss_row = _adamw_small(small_parts, row(w), row(m), row(v))
    for n, _ in _SMALL:
        out_g[n], out_d[n], out_m[n], out_v[n] = [r.reshape(shapes[n]) for r in sres[n]]
    loss = loss_row[0, 0]
    return (loss, grad_x[None], *[out_g[n] for n in _ORDER], *[out_d[n] for n in _ORDER],
            *[out_m[n] for n in _ORDER], *[out_v[n] for n in _ORDER])
```

```python
import functools

import jax
import jax.numpy as jnp
from jax import lax
from jax.experimental import pallas as pl
from jax.experimental.pallas import tpu as pltpu

F32 = jnp.float32
BF = jnp.bfloat16

N_DEV = 8
D_MODEL = 1024
D_FF = 2816
D_INNER = 2048
D_XBC = 4096
N_HEADS = 32
HEAD_DIM = 64
N_GROUPS = 8
D_STATE = 128
CHUNK = 64
GROUP_W = D_INNER // N_GROUPS
HEADS_PER_GROUP = N_HEADS // N_GROUPS
NORM_EPS = 1e-5
N_IN = 11296
FF_SHARD = 2 * D_FF // N_DEV
FF_HALF = D_FF // 2
IN_SHARD = N_IN // N_DEV

OFF_B, OFF_C, OFF_XA, OFF_Z, OFF_XBC = 0, 1024, 2048, 3072, 5120
N_MAIN = 9216
OFF_GA, OFF_GB, OFF_DT = 0, 1024, 2048
DT_W = 128
N_GD = 2048 + DT_W
W_GRAD_ROW_CUTS = [(0, 512), (512, 720), (720, 896), (896, 1152), (1152, 1412)]

ADAM_LR, ADAM_B1, ADAM_B2, ADAM_EPS, ADAM_WD, ADAM_STEP = 0.001, 0.9, 0.999, 1e-08, 0.01, 10

VMEM_LIMIT_V7X = 56 * 1024 * 1024
TM = 1024
TE = 512
ADAM_COL_TILE = 256
GATHER_PIECES = 4
GATHER_PIECE_MIN_ROWS = 512


def _params(*sem):
    return pltpu.CompilerParams(dimension_semantics=sem, vmem_limit_bytes=VMEM_LIMIT_V7X)


_DIMS = {
    "nn": (((1,), (0,)), ((), ())),
    "nt": (((1,), (1,)), ((), ())),
    "tn": (((0,), (0,)), ((), ())),
}


def _dot(a, b, mode="nn"):
    return lax.dot_general(a, b, _DIMS[mode], preferred_element_type=F32)


def _sigmoid(x):
    return 1.0 / (1.0 + jnp.exp(-x))


class _Comm:
    def __init__(self, inputs, out_shapes, sems, start, finish):
        self.inputs, self.out_shapes, self.sems, self.start, self.finish = inputs, out_shapes, sems, start, finish


def _pcall(name, body, grid, in_specs, out_specs, out_shape, args, scratch=(), sem=None, comm=None):
    single = not isinstance(out_shape, (list, tuple))
    out_shapes = [out_shape] if single else list(out_shape)
    out_specs = [out_specs] if single else list(out_specs)
    n_in, n_out, n_scr = len(args), len(out_shapes), len(scratch)
    if comm is None:
        res = pl.pallas_call(
            body, name=name, grid=grid, in_specs=list(in_specs), out_specs=out_specs, out_shape=out_shapes,
            scratch_shapes=list(scratch), compiler_params=_params(*sem))(*args)
        return (res[0] if single else res), []
    nci, nco = len(comm.inputs), len(comm.out_shapes)

    def wrapped(*refs):
        a = refs[:n_in]
        ci = refs[n_in:n_in + nci]
        o0 = n_in + nci
        o = refs[o0:o0 + n_out]
        co = refs[o0 + n_out:o0 + n_out + nco]
        s0 = o0 + n_out + nco
        s = refs[s0:s0 + n_scr]
        cs = refs[s0 + n_scr:]
        pids = [pl.program_id(i) for i in range(len(grid))]
        first = functools.reduce(jnp.logical_and, [p == 0 for p in pids])
        last = functools.reduce(jnp.logical_and, [p == g - 1 for p, g in zip(pids, grid)])

        @pl.when(first)
        def _():
            comm.start(ci, co, cs)

        body(*a, *o, *s)

        @pl.when(last)
        def _():
            comm.finish(ci, co, cs)

    any_spec = pl.BlockSpec(memory_space=pl.ANY)
    res = pl.pallas_call(
        wrapped, name=name, grid=grid, in_specs=list(in_specs) + [any_spec] * nci,
        out_specs=out_specs + [any_spec] * nco, out_shape=out_shapes + list(comm.out_shapes),
        scratch_shapes=list(scratch) + list(comm.sems),
        compiler_params=_params(*(("arbitrary",) * len(grid))))(*args, *comm.inputs)
    core = res[:n_out]
    return (core[0] if single else core), list(res[n_out:])


def _comm_call(name, comm):
    nci, nco = len(comm.inputs), len(comm.out_shapes)

    def body(*refs):
        ci, co, cs = refs[:nci], refs[nci:nci + nco], refs[nci + nco:]
        comm.start(ci, co, cs)
        comm.finish(ci, co, cs)

    any_spec = pl.BlockSpec(memory_space=pl.ANY)
    return pl.pallas_call(
        body, name=name, in_specs=[any_spec] * nci, out_specs=[any_spec] * nco, out_shape=list(comm.out_shapes),
        scratch_shapes=list(comm.sems), compiler_params=pltpu.CompilerParams(has_side_effects=True))(*comm.inputs)


def _remote(src, dst, ssem, rsem, dev):
    return pltpu.make_async_remote_copy(src_ref=src, dst_ref=dst, send_sem=ssem, recv_sem=rsem, device_id=dev,
                                        device_id_type=pl.DeviceIdType.MESH)


def _place():
    x, y, c = lax.axis_index("x"), lax.axis_index("y"), lax.axis_index("c")
    other_chips = [(1 - x, y), (x, 1 - y), (1 - x, 1 - y)]
    return x, y, c, other_chips


def _slot(x, y, c, swap):
    return 4 * y + 2 * x + c if swap else 4 * x + 2 * y + c


def _chip_slot(x, y, swap):
    return 2 * y + x if swap else 2 * x + y


def _gather_comm(shards, swaps=None):
    n = len(shards)
    per = N_DEV - 1
    swaps = [False] * n if swaps is None else swaps
    pieces = []
    for i, a in enumerate(shards):
        rows = a.shape[0]
        k = GATHER_PIECES if (a.ndim == 2 and rows >= GATHER_PIECE_MIN_ROWS) else 1
        step = -(-rows // (k * 8)) * 8
        if k == 1:
            pieces.append((i, 0, None))
        else:
            pieces += [(i, r, min(step, rows - r)) for r in range(0, rows, step)]
    m = len(pieces)

    def src(ins, v):
        i, r, cnt = pieces[v]
        return ins[i] if cnt is None else ins[i].at[pl.ds(r, cnt)]

    def place(outs, v, x, y, c):
        i, r, cnt = pieces[v]
        blk = outs[i].at[_slot(x, y, c, swaps[i])]
        return blk if cnt is None else blk.at[pl.ds(r, cnt)]

    def start(ins, outs, sems):
        send, recv, loc = sems
        x, y, c, chips = _place()
        for v in range(m):
            me = place(outs, v, x, y, c)
            pltpu.make_async_copy(src(ins, v), me, loc.at[v]).start()
            _remote(src(ins, v), me, send.at[per * v], recv.at[per * v], (x, y, 1 - c)).start()
        for j, (qx, qy) in enumerate(chips):
            for v in range(m):
                _remote(src(ins, v), place(outs, v, x, y, c), send.at[per * v + 1 + j], recv.at[per * v + 1 + j],
                        (qx, qy, c)).start()

    def finish(ins, outs, sems):
        send, recv, loc = sems
        x, y, c, chips = _place()
        sib = (x, y, 1 - c)
        for v in range(m):
            for j, (qx, qy) in enumerate(chips):
                blk = place(outs, v, qx, qy, c)
                _remote(blk, blk, send.at[per * v + 1 + j], recv.at[per * v + 1 + j], (qx, qy, c)).wait_recv()
                _remote(blk, blk, send.at[per * v + 4 + j], recv.at[per * v + 4 + j], sib).start()
        for v in range(m):
            blk = place(outs, v, x, y, 1 - c)
            _remote(blk, blk, send.at[per * v], recv.at[per * v], sib).wait_recv()
            for j, (qx, qy) in enumerate(chips):
                blk = place(outs, v, qx, qy, 1 - c)
                _remote(blk, blk, send.at[per * v + 4 + j], recv.at[per * v + 4 + j], sib).wait_recv()
        for v in range(m):
            own = place(outs, v, x, y, c)
            for k in range(per):
                _remote(src(ins, v), own, send.at[per * v + k], recv.at[per * v + k], sib).wait_send()
            pltpu.make_async_copy(src(ins, v), own, loc.at[v]).wait()

    out_shapes = [jax.ShapeDtypeStruct((N_DEV,) + tuple(a.shape), a.dtype) for a in shards]
    sems = [pltpu.SemaphoreType.DMA((per * m,)), pltpu.SemaphoreType.DMA((per * m,)), pltpu.SemaphoreType.DMA((m,))]
    return _Comm(list(shards), out_shapes, sems, start, finish)


def _pair_comm(slots):
    n = len(slots)

    def copies(ins, outs, sems):
        send, recv = sems
        x, y, c, _ = _place()
        sib = (x, y, 1 - c)
        out = []
        for i in range(n):
            for q in range(4):
                out.append(_remote(ins[i].at[2 * q + 1 - c], outs[i].at[q], send.at[4 * i + q], recv.at[4 * i + q], sib))
        return out

    def start(ins, outs, sems):
        for cp in copies(ins, outs, sems):
            cp.start()

    def finish(ins, outs, sems):
        for cp in copies(ins, outs, sems):
            cp.wait_send()
            cp.wait_recv()

    out_shapes = [jax.ShapeDtypeStruct((4,) + tuple(a.shape[1:]), a.dtype) for a in slots]
    sems = [pltpu.SemaphoreType.DMA((4 * n,)), pltpu.SemaphoreType.DMA((4 * n,))]
    return _Comm(list(slots), out_shapes, sems, start, finish)


def _chip_comm(chip_sums, swaps=None, rows=None):
    n = len(chip_sums)
    swaps = [False] * n if swaps is None else swaps
    rows = [None] * n if rows is None else rows

    def src(ins, i, q):
        return ins[i].at[q] if rows[i] is None else ins[i].at[q, pl.ds(rows[i][0], rows[i][1] - rows[i][0])]

    def start(ins, outs, sems):
        send, recv, loc = sems
        x, y, c, chips = _place()
        for i in range(n):
            mine = _chip_slot(x, y, swaps[i])
            pltpu.make_async_copy(src(ins, i, mine), outs[i].at[mine], loc.at[i]).start()
            for j, (qx, qy) in enumerate(chips):
                _remote(src(ins, i, _chip_slot(qx, qy, swaps[i])), outs[i].at[mine], send.at[3 * i + j],
                        recv.at[3 * i + j], (qx, qy, c)).start()

    def finish(ins, outs, sems):
        send, recv, loc = sems
        x, y, c, chips = _place()
        for i in range(n):
            mine = _chip_slot(x, y, swaps[i])
            for j, (qx, qy) in enumerate(chips):
                theirs = _chip_slot(qx, qy, swaps[i])
                cp = _remote(src(ins, i, theirs), outs[i].at[theirs], send.at[3 * i + j], recv.at[3 * i + j], (qx, qy, c))
                cp.wait_send()
                cp.wait_recv()
            pltpu.make_async_copy(src(ins, i, mine), outs[i].at[mine], loc.at[i]).wait()

    def out_shape(a, r):
        shape = a.shape if r is None else (a.shape[0], r[1] - r[0]) + tuple(a.shape[2:])
        return jax.ShapeDtypeStruct(shape, a.dtype)

    out_shapes = [out_shape(a, r) for a, r in zip(chip_sums, rows)]
    sems = [pltpu.SemaphoreType.DMA((3 * n,)), pltpu.SemaphoreType.DMA((3 * n,)), pltpu.SemaphoreType.DMA((n,))]
    return _Comm(list(chip_sums), out_shapes, sems, start, finish)


def _join_comm(a, b):
    na_i, na_o, na_s = len(a.inputs), len(a.out_shapes), len(a.sems)

    def start(ins, outs, sems):
        a.start(ins[:na_i], outs[:na_o], sems[:na_s])
        b.start(ins[na_i:], outs[na_o:], sems[na_s:])

    def finish(ins, outs, sems):
        a.finish(ins[:na_i], outs[:na_o], sems[:na_s])
        b.finish(ins[na_i:], outs[na_o:], sems[na_s:])

    return _Comm(a.inputs + b.inputs, a.out_shapes + b.out_shapes, a.sems + b.sems, start, finish)


def _row_tile(r):
    for cand in (256, 128):
        if r > cand and r % cand == 0:
            return cand
    return r


def _add_pairs(name, slots, sib):
    r, c = slots.shape[1:]
    tr = _row_tile(r)

    def body(core_ref, s_ref, b_ref, o_ref):
        o_ref[...] = (s_ref[...].astype(F32) + b_ref[...].astype(F32)).astype(o_ref.dtype)

    core = jnp.full((1,), lax.axis_index("c"), jnp.int32)
    return pl.pallas_call(
        body, name=name,
        grid_spec=pltpu.PrefetchScalarGridSpec(
            num_scalar_prefetch=1, grid=(4, r // tr),
            in_specs=[pl.BlockSpec((None, None, tr, c), lambda q, i, core_ref: (q, core_ref[0], i, 0)),
                      pl.BlockSpec((None, tr, c), lambda q, i, core_ref: (q, i, 0))],
            out_specs=pl.BlockSpec((None, tr, c), lambda q, i, core_ref: (q, i, 0))),
        out_shape=jax.ShapeDtypeStruct((4, r, c), slots.dtype),
        compiler_params=_params("parallel", "parallel"))(core, slots.reshape(4, 2, r, c), sib)


def _matmul(name, mode, a, b, grid, a_spec, b_spec, o_spec, out_shape, acc_shape,
            res=None, res_spec=None, alpha=1.0, comm=None):
    nk = grid[-1]
    has_res = res is not None

    def body(*refs):
        if has_res:
            a_ref, b_ref, r_ref, o_ref = refs[:4]
        else:
            a_ref, b_ref, o_ref = refs[:3]
            r_ref = None
        part = _dot(a_ref[...], b_ref[...], mode)

        def finish(v):
            if alpha != 1.0:
                v = v * alpha
            if has_res:
                v = r_ref[...] + v
            o_ref[...] = v.astype(o_ref.dtype)

        if nk == 1:
            finish(part)
        else:
            acc = refs[-1]
            k = pl.program_id(len(grid) - 1)

            @pl.when(k == 0)
            def _():
                acc[...] = part

            @pl.when(k > 0)
            def _():
                acc[...] += part

            @pl.when(k == nk - 1)
            def _():
                finish(acc[...])

    in_specs = [a_spec, b_spec] + ([res_spec] if has_res else [])
    args = (a, b) + ((res,) if has_res else ())
    scratch = [] if nk == 1 else [pltpu.VMEM(acc_shape, F32)]
    sem = ("parallel",) * (len(grid) - 1) + ("arbitrary",)
    out, couts = _pcall(name, body, grid, in_specs, o_spec, out_shape, args, scratch, sem, comm)
    return out if comm is None else (out, couts)


def _mm_nn(name, a, b, out_dtype=F32, res=None, alpha=1.0, tk=None, kk=None, a_off=0, b_off=0, comm=None):
    t = a.shape[0]
    kk = a.shape[1] if kk is None else kk
    n = b.shape[1]
    tk = kk if tk is None else tk
    grid = (t // TM, 1, kk // tk)
    return _matmul(
        name, "nn", a, b, grid,
        pl.BlockSpec((TM, tk), lambda i, j, k: (i, k + a_off)),
        pl.BlockSpec((tk, n), lambda i, j, k: (k + b_off, 0)),
        pl.BlockSpec((TM, n), lambda i, j, k: (i, 0)),
        jax.ShapeDtypeStruct((t, n), out_dtype), (TM, n),
        res=res, res_spec=pl.BlockSpec((TM, n), lambda i, j, k: (i, 0)), alpha=alpha, comm=comm)


def _mm_nt(name, a, b, n=None, tn=None, tk=None, out_dtype=F32, comm=None):
    t, kk = a.shape
    n = b.shape[0] if n is None else n
    tn = n if tn is None else tn
    tk = kk if tk is None else tk
    grid = (n // tn, t // TM, kk // tk)
    return _matmul(
        name, "nt", a, b, grid,
        pl.BlockSpec((TM, tk), lambda j, i, k: (i, k)),
        pl.BlockSpec((tn, tk), lambda j, i, k: (j, k)),
        pl.BlockSpec((TM, tn), lambda j, i, k: (i, j)),
        jax.ShapeDtypeStruct((t, n), out_dtype), (TM, tn), comm=comm)


def _mm_tn(name, a, b, out_dtype, tm=None, n=None, col_off=0, comm=None):
    t, m = a.shape
    n = b.shape[1] if n is None else n
    tm = m if tm is None else tm
    grid = (m // tm, 1, t // TM)
    return _matmul(
        name, "tn", a, b, grid,
        pl.BlockSpec((TM, tm), lambda j, i, k: (k, j)),
        pl.BlockSpec((TM, n), lambda j, i, k: (k, col_off)),
        pl.BlockSpec((tm, n), lambda j, i, k: (j, 0)),
        jax.ShapeDtypeStruct((m, n), out_dtype), (tm, n), comm=comm)


def _rms_fwd(name, x, w):
    t, d = x.shape

    def body(x_ref, w_ref, h_ref):
        xv = x_ref[...]
        rstd = lax.rsqrt(jnp.mean(xv * xv, axis=-1, keepdims=True) + NORM_EPS)
        h_ref[...] = (xv * rstd * w_ref[...]).astype(h_ref.dtype)

    return pl.pallas_call(
        body, name=name, grid=(t // TE,),
        in_specs=[pl.BlockSpec((TE, d), lambda i: (i, 0)), pl.BlockSpec((1, d), lambda i: (0, 0))],
        out_specs=pl.BlockSpec((TE, d), lambda i: (i, 0)),
        out_shape=jax.ShapeDtypeStruct((t, d), BF), compiler_params=_params("parallel"))(x, w)


def _rms_bwd(name, x, w, dh, dres, out_scale, comm=None):
    t, d = x.shape

    def body(x_ref, w_ref, dh_ref, dres_ref, dx_ref, dxb_ref, dw_ref):
        i = pl.program_id(0)
        xv = x_ref[...]
        rstd = lax.rsqrt(jnp.mean(xv * xv, axis=-1, keepdims=True) + NORM_EPS)
        xhat = xv * rstd
        dhv = dh_ref[...]
        wd = dhv * w_ref[...]
        proj = jnp.mean(wd * xhat, axis=-1, keepdims=True)
        dx = dres_ref[...] + rstd * (wd - xhat * proj)
        dx_ref[...] = dx
        dxb_ref[...] = (dx * out_scale).astype(BF)
        part = jnp.sum(dhv * xhat, axis=0, keepdims=True)

        @pl.when(i == 0)
        def _():
            dw_ref[...] = part

        @pl.when(i > 0)
        def _():
            dw_ref[...] += part

    row = pl.BlockSpec((TE, d), lambda i: (i, 0))
    vec = pl.BlockSpec((1, d), lambda i: (0, 0))
    outs, couts = _pcall(
        name, body, (t // TE,), [row, vec, row, row], [row, row, vec],
        [jax.ShapeDtypeStruct((t, d), F32), jax.ShapeDtypeStruct((t, d), BF), jax.ShapeDtypeStruct((1, d), F32)],
        (x, w, dh, dres), (), ("arbitrary",), comm)
    return outs if comm is None else (outs, couts)


def _final_loss(x, w, target):
    t, d = x.shape

    def body(x_ref, w_ref, t_ref, loss_ref, dx_ref, dxb_ref, dw_ref):
        i = pl.program_id(0)
        xv = x_ref[...]
        rstd = lax.rsqrt(jnp.mean(xv * xv, axis=-1, keepdims=True) + NORM_EPS)
        xhat = xv * rstd
        err = xhat * w_ref[...] - t_ref[...]
        lpart = 0.5 * jnp.sum(jnp.mean(err * err, axis=-1, keepdims=True), axis=0, keepdims=True)
        dy = err * (1.0 / d)
        wd = dy * w_ref[...]
        proj = jnp.mean(wd * xhat, axis=-1, keepdims=True)
        dx = rstd * (wd - xhat * proj)
        dx_ref[...] = dx
        dxb_ref[...] = (0.5 * dx).astype(BF)
        part = jnp.sum(dy * xhat, axis=0, keepdims=True)
        lfull = jnp.broadcast_to(lpart, (1, 128))

        @pl.when(i == 0)
        def _():
            dw_ref[...] = part
            loss_ref[...] = lfull

        @pl.when(i > 0)
        def _():
            dw_ref[...] += part
            loss_ref[...] += lfull

    row = pl.BlockSpec((TE, d), lambda i: (i, 0))
    vec = pl.BlockSpec((1, d), lambda i: (0, 0))
    return pl.pallas_call(
        body, name="final_loss", grid=(t // TE,), in_specs=[row, vec, row],
        out_specs=[pl.BlockSpec((1, 128), lambda i: (0, 0)), row, row, vec],
        out_shape=[jax.ShapeDtypeStruct((1, 128), F32), jax.ShapeDtypeStruct((t, d), F32),
                   jax.ShapeDtypeStruct((t, d), BF), jax.ShapeDtypeStruct((1, d), F32)],
        compiler_params=_params("arbitrary"))(x, w, target)


def _swiglu_fwd(name, gu, comm=None):
    t = gu.shape[0]

    def body(g_ref, u_ref, a_ref):
        g = g_ref[...].astype(F32)
        a_ref[...] = (g * _sigmoid(g) * u_ref[...].astype(F32)).astype(BF)

    blk = (TE, FF_HALF)
    out, couts = _pcall(
        name, body, (t // TE, 2),
        [pl.BlockSpec(blk, lambda i, j: (i, 2 * j)), pl.BlockSpec(blk, lambda i, j: (i, 2 * j + 1))],
        pl.BlockSpec(blk, lambda i, j: (i, j)), jax.ShapeDtypeStruct((t, D_FF), BF),
        (gu, gu), (), ("parallel", "parallel"), comm)
    return out if comm is None else (out, couts)


def _swiglu_bwd(name, gu, dact, comm=None):
    t = gu.shape[0]

    def body(g_ref, u_ref, da_ref, o_ref):
        g = g_ref[...].astype(F32)
        da = da_ref[...].astype(F32)
        s = _sigmoid(g)
        o_ref[:, 0:FF_HALF] = (da * u_ref[...].astype(F32) * (s * (1.0 + g * (1.0 - s)))).astype(BF)
        o_ref[:, FF_HALF:2 * FF_HALF] = (da * g * s).astype(BF)

    blk = (TE, FF_HALF)
    out, couts = _pcall(
        name, body, (t // TE, 2),
        [pl.BlockSpec(blk, lambda i, j: (i, 2 * j)), pl.BlockSpec(blk, lambda i, j: (i, 2 * j + 1)),
         pl.BlockSpec(blk, lambda i, j: (i, j))],
        pl.BlockSpec((TE, 2 * FF_HALF), lambda i, j: (i, j)),
        jax.ShapeDtypeStruct((t, 2 * D_FF), BF), (gu, gu, dact), (), ("parallel", "parallel"), comm)
    return out if comm is None else (out, couts)


CONV_CB = 256


CONV_ROWS = 64
CONV_HALO = 16


def _taps_down(ext, w, k):
    shifted = [pltpu.roll(ext, k - 1 - j, 0)[CONV_HALO:] for j in range(k - 1)] + [ext[CONV_HALO:]]
    out = shifted[k - 1] * w[k - 1:k, :]
    for j in range(k - 1):
        out = out + shifted[j] * w[j:j + 1, :]
    return out, shifted


def _taps_up(ext, w, k):
    rows = ext.shape[0]
    n = rows - CONV_HALO
    out = ext[:n] * w[k - 1:k, :]
    for j in range(k - 1):
        out = out + pltpu.roll(ext, rows - (k - 1 - j), 0)[:n] * w[j:j + 1, :]
    return out


def _rows_before(ref, i, r0):
    start = pl.multiple_of(jnp.maximum(r0 - CONV_HALO, 0), CONV_HALO)
    return jnp.where(i > 0, ref[pl.ds(start, CONV_HALO), :].astype(F32), 0.0)


def _rows_after(ref, r0, t):
    start = pl.multiple_of(jnp.minimum(r0 + CONV_ROWS, t - CONV_HALO), CONV_HALO)
    return ref[pl.ds(start, CONV_HALO), :].astype(F32)


def _fold8(v):
    return v.reshape(v.shape[0] // 8, 8, v.shape[1]).sum(axis=0)


def _silu_grad(pre):
    s = _sigmoid(pre)
    return s * (1.0 + pre * (1.0 - s))


def _pspec(t, off):
    base = off // CONV_CB
    return pl.BlockSpec((t, CONV_CB), lambda j: (0, base + j))


def _mix_a_fwd(p, conv_w):
    t = p.shape[0]

    def body(b_ref, c_ref, xa_ref, w_ref, o_ref):
        w = w_ref[...]

        def step(i, carry):
            r0 = pl.multiple_of(i * CONV_ROWS, CONV_ROWS)
            rows = pl.ds(r0, CONV_ROWS)
            q = c_ref[rows, :].astype(F32) * xa_ref[rows, :].astype(F32)
            q_before = _rows_before(c_ref, i, r0) * _rows_before(xa_ref, i, r0)
            va, _ = _taps_down(jnp.concatenate([q_before, q], axis=0), w, 3)
            o_ref[rows, :] = (b_ref[rows, :].astype(F32) * va).astype(BF)
            return carry

        lax.fori_loop(0, t // CONV_ROWS, step, 0)

    return pl.pallas_call(
        body, name="mix_a_fwd", grid=(D_MODEL // CONV_CB,),
        in_specs=[_pspec(t, OFF_B), _pspec(t, OFF_C), _pspec(t, OFF_XA),
                  pl.BlockSpec((3, CONV_CB), lambda j: (0, j))],
        out_specs=pl.BlockSpec((t, CONV_CB), lambda j: (0, j)),
        out_shape=jax.ShapeDtypeStruct((t, D_MODEL), BF), compiler_params=_params("parallel"))(p, p, p, conv_w)


def _mix_a_bwd(p, conv_w, dya, dp):
    t = p.shape[0]

    def body(b_ref, c_ref, xa_ref, w_ref, dy_ref, dp_in, dp_ref, dw_ref):
        del dp_in
        w = w_ref[...]
        n = t // CONV_ROWS

        def step(i, acc):
            r0 = pl.multiple_of(i * CONV_ROWS, CONV_ROWS)
            rows = pl.ds(r0, CONV_ROWS)
            cv = c_ref[rows, :].astype(F32)
            xav = xa_ref[rows, :].astype(F32)
            q_before = _rows_before(c_ref, i, r0) * _rows_before(xa_ref, i, r0)
            va, shifted = _taps_down(jnp.concatenate([q_before, cv * xav], axis=0), w, 3)
            dyv = dy_ref[rows, :]
            dp_ref[rows, 0:CONV_CB] = (dyv * va).astype(BF)
            dv = dyv * b_ref[rows, :].astype(F32)
            dv_after = jnp.where(i < n - 1, _rows_after(dy_ref, r0, t) * _rows_after(b_ref, r0, t), 0.0)
            dq = _taps_up(jnp.concatenate([dv, dv_after], axis=0), w, 3)
            dp_ref[rows, CONV_CB:2 * CONV_CB] = (dq * xav).astype(BF)
            dp_ref[rows, 2 * CONV_CB:3 * CONV_CB] = (dq * cv).astype(BF)
            return tuple(a + _fold8(dv * s) for a, s in zip(acc, shifted))

        zero = jnp.zeros((8, CONV_CB), F32)
        acc = lax.fori_loop(0, n, step, (zero, zero, zero))
        for j in range(3):
            dw_ref[j:j + 1, :] = jnp.sum(acc[j], axis=0, keepdims=True)

    col = pl.BlockSpec((t, CONV_CB), lambda j: (0, j))
    wsp = pl.BlockSpec((3, CONV_CB), lambda j: (0, j))
    return pl.pallas_call(
        body, name="mix_a_bwd", grid=(D_MODEL // CONV_CB,),
        in_specs=[_pspec(t, OFF_B), _pspec(t, OFF_C), _pspec(t, OFF_XA), wsp, col, pl.BlockSpec(memory_space=pl.ANY)],
        out_specs=[pl.BlockSpec((t, 3 * CONV_CB), lambda j: (0, j)), wsp],
        out_shape=[jax.ShapeDtypeStruct(dp.shape, dp.dtype), jax.ShapeDtypeStruct((3, D_MODEL), F32)],
        input_output_aliases={5: 0},
        compiler_params=_params("parallel"))(p, p, p, conv_w, dya, dp)


def _ssm_conv_fwd(p, conv_w, conv_b, comm=None):
    t = p.shape[0]

    def body(x_ref, w_ref, b_ref, o_ref):
        w = w_ref[...]
        bias = b_ref[...]

        def step(i, carry):
            r0 = pl.multiple_of(i * CONV_ROWS, CONV_ROWS)
            rows = pl.ds(r0, CONV_ROWS)
            ext = jnp.concatenate([_rows_before(x_ref, i, r0), x_ref[rows, :].astype(F32)], axis=0)
            pre = _taps_down(ext, w, 4)[0] + bias
            o_ref[rows, :] = pre * _sigmoid(pre)
            return carry

        lax.fori_loop(0, t // CONV_ROWS, step, 0)

    out, couts = _pcall(
        "ssm_conv_fwd", body, (D_XBC // CONV_CB,),
        [_pspec(t, OFF_XBC), pl.BlockSpec((4, CONV_CB), lambda j: (0, j)), pl.BlockSpec((1, CONV_CB), lambda j: (0, j))],
        pl.BlockSpec((t, CONV_CB), lambda j: (0, j)), jax.ShapeDtypeStruct((t, D_XBC), F32),
        (p, conv_w, conv_b), (), ("parallel",), comm)
    return out if comm is None else (out, couts)


def _ssm_conv_bwd(p, conv_w, conv_b, dxc, dp):
    t = p.shape[0]

    def body(x_ref, w_ref, b_ref, d_ref, dp_in, dx_ref, dw_ref, db_ref):
        del dp_in
        w = w_ref[...]
        bias = b_ref[...]
        n = t // CONV_ROWS

        def step(i, acc):
            r0 = pl.multiple_of(i * CONV_ROWS, CONV_ROWS)
            rows = pl.ds(r0, CONV_ROWS)
            x_cur = x_ref[rows, :].astype(F32)
            pre, shifted = _taps_down(jnp.concatenate([_rows_before(x_ref, i, r0), x_cur], axis=0), w, 4)
            pre = pre + bias
            dpre = d_ref[rows, :] * _silu_grad(pre)
            ext_after = jnp.concatenate([x_cur[CONV_ROWS - CONV_HALO:], _rows_after(x_ref, r0, t)], axis=0)
            pre_after = _taps_down(ext_after, w, 4)[0] + bias
            dpre_after = jnp.where(i < n - 1, _rows_after(d_ref, r0, t) * _silu_grad(pre_after), 0.0)
            dx_ref[rows, :] = _taps_up(jnp.concatenate([dpre, dpre_after], axis=0), w, 4).astype(BF)
            new = tuple(a + _fold8(dpre * s) for a, s in zip(acc[:4], shifted))
            return new + (acc[4] + _fold8(dpre),)

        zero = jnp.zeros((8, CONV_CB), F32)
        acc = lax.fori_loop(0, n, step, (zero,) * 5)
        for j in range(4):
            dw_ref[j:j + 1, :] = jnp.sum(acc[j], axis=0, keepdims=True)
        db_ref[...] = jnp.sum(acc[4], axis=0, keepdims=True)

    col = pl.BlockSpec((t, CONV_CB), lambda j: (0, j))
    wsp = pl.BlockSpec((4, CONV_CB), lambda j: (0, j))
    bsp = pl.BlockSpec((1, CONV_CB), lambda j: (0, j))
    return pl.pallas_call(
        body, name="ssm_conv_bwd", grid=(D_XBC // CONV_CB,),
        in_specs=[_pspec(t, OFF_XBC), wsp, bsp, col, pl.BlockSpec(memory_space=pl.ANY)],
        out_specs=[_pspec(t, OFF_XBC), wsp, bsp],
        out_shape=[jax.ShapeDtypeStruct(dp.shape, dp.dtype), jax.ShapeDtypeStruct((4, D_XBC), F32),
                   jax.ShapeDtypeStruct((1, D_XBC), F32)],
        input_output_aliases={4: 0},
        compiler_params=_params("parallel"))(p, conv_w, conv_b, dxc, dp)


DT_ROWS = 512


def _tri(lower):
    r = lax.broadcasted_iota(jnp.int32, (CHUNK, CHUNK), 0)
    c = lax.broadcasted_iota(jnp.int32, (CHUNK, CHUNK), 1)
    return jnp.where((r >= c) if lower else (r <= c), 1.0, 0.0).astype(F32)


def _dot_exact(a, b):
    return lax.dot_general(a, b, _DIMS["nn"], preferred_element_type=F32, precision=lax.Precision.HIGHEST)


def _dt_fwd(p, bias_pad, alog_pad):
    t = p.shape[0]

    def body(raw_ref, b_ref, al_ref, dt_ref, acs_ref):
        z = raw_ref[...] + b_ref[...]
        dt = jnp.maximum(z, 0.0) + jnp.log(1.0 + jnp.exp(-jnp.abs(z)))
        dt_ref[...] = dt
        a = dt * (-jnp.exp(al_ref[...]))
        tri = _tri(True)
        for k in range(DT_ROWS // CHUNK):
            acs_ref[k * CHUNK:(k + 1) * CHUNK, :] = _dot_exact(tri, a[k * CHUNK:(k + 1) * CHUNK, :])

    blk = pl.BlockSpec((DT_ROWS, DT_W), lambda i: (i, 0))
    vec = pl.BlockSpec((1, DT_W), lambda i: (0, 0))
    return pl.pallas_call(
        body, name="dt_fwd", grid=(t // DT_ROWS,),
        in_specs=[pl.BlockSpec((DT_ROWS, DT_W), lambda i: (i, OFF_DT // DT_W)), vec, vec],
        out_specs=[blk, blk], out_shape=[jax.ShapeDtypeStruct((t, DT_W), F32)] * 2,
        compiler_params=_params("parallel"))(p, bias_pad, alog_pad)


def _dt_bwd(p, bias_pad, alog_pad, dt, ddt, dacs, dp_gd):
    t = p.shape[0]

    def body(raw_ref, b_ref, al_ref, dt_ref, ddt_ref, dacs_ref, dp_in, draw_ref, db_ref, dal_ref):
        del dp_in
        i = pl.program_id(0)
        acoef = -jnp.exp(al_ref[...])
        triu = _tri(False)
        das = []
        for k in range(DT_ROWS // CHUNK):
            das.append(_dot_exact(triu, dacs_ref[k * CHUNK:(k + 1) * CHUNK, :]))
        da = jnp.concatenate(das, axis=0)
        dtv = dt_ref[...]
        ddt_tot = ddt_ref[...] + da * acoef
        lane = lax.broadcasted_iota(jnp.int32, (DT_ROWS, DT_W), 1)
        draw = jnp.where(lane < N_HEADS, ddt_tot * _sigmoid(raw_ref[...] + b_ref[...]), 0.0)
        draw_ref[...] = draw.astype(BF)
        pb = jnp.sum(draw, axis=0, keepdims=True)
        pa = jnp.sum(da * dtv * acoef, axis=0, keepdims=True)

        @pl.when(i == 0)
        def _():
            db_ref[...] = pb
            dal_ref[...] = pa

        @pl.when(i > 0)
        def _():
            db_ref[...] += pb
            dal_ref[...] += pa

    blk = pl.BlockSpec((DT_ROWS, DT_W), lambda i: (i, 0))
    vec = pl.BlockSpec((1, DT_W), lambda i: (0, 0))
    return pl.pallas_call(
        body, name="dt_bwd", grid=(t // DT_ROWS,),
        in_specs=[pl.BlockSpec((DT_ROWS, DT_W), lambda i: (i, OFF_DT // DT_W)), vec, vec, blk, blk, blk,
                  pl.BlockSpec(memory_space=pl.ANY)],
        out_specs=[pl.BlockSpec((DT_ROWS, DT_W), lambda i: (i, OFF_DT // DT_W)), vec, vec],
        out_shape=[jax.ShapeDtypeStruct(dp_gd.shape, dp_gd.dtype), jax.ShapeDtypeStruct((1, DT_W), F32),
                   jax.ShapeDtypeStruct((1, DT_W), F32)],
        input_output_aliases={6: 0},
        compiler_params=_params("arbitrary"))(p, bias_pad, alog_pad, dt, ddt, dacs, dp_gd)


def _split_dot(z, onehot, terms):
    out = None
    rest = z
    for _ in range(terms):
        piece = rest.astype(BF)
        part = _dot(piece, onehot)
        out = part if out is None else out + part
        rest = rest - piece.astype(F32)
    return out


def _spread_mat(g):
    row = lax.broadcasted_iota(jnp.int32, (DT_W, GROUP_W), 0)
    lane = lax.broadcasted_iota(jnp.int32, (DT_W, GROUP_W), 1)
    return jnp.where(row == HEADS_PER_GROUP * g + lane // HEAD_DIM, 1.0, 0.0).astype(BF)


def _gather_mat(g):
    row = lax.broadcasted_iota(jnp.int32, (GROUP_W, DT_W), 0)
    lane = lax.broadcasted_iota(jnp.int32, (GROUP_W, DT_W), 1)
    return jnp.where(lane == HEADS_PER_GROUP * g + row // HEAD_DIM, 1.0, 0.0).astype(BF)


def _ssd_masks():
    row = lax.broadcasted_iota(jnp.int32, (CHUNK, GROUP_W), 0)
    col = lax.broadcasted_iota(jnp.int32, (CHUNK, GROUP_W), 1) % HEAD_DIM
    brow = lax.broadcasted_iota(jnp.int32, (GROUP_W, GROUP_W), 0) // HEAD_DIM
    bcol = lax.broadcasted_iota(jnp.int32, (GROUP_W, GROUP_W), 1) // HEAD_DIM
    return row >= col, row == col, brow == bcol


def _stack4(v):
    return jnp.concatenate([v, v, v, v], axis=0)


def _fold4(v):
    return v[0:CHUNK] + v[CHUNK:2 * CHUNK] + v[2 * CHUNK:3 * CHUNK] + v[3 * CHUNK:4 * CHUNK]


def _ssd_group(xc_ref, stacked, g, tri, eye, blockdiag):
    gs = slice(GROUP_W * g, GROUP_W * (g + 1))
    xs_g = xc_ref[:, gs]
    b_g = xc_ref[:, D_INNER + D_STATE * g:D_INNER + D_STATE * (g + 1)].astype(BF)
    c_g = xc_ref[:, D_INNER + 1024 + D_STATE * g:D_INNER + 1024 + D_STATE * (g + 1)].astype(BF)
    wide = _split_dot(stacked, _spread_mat(g), 3)
    acs_e, dt_e = wide[0:CHUNK], wide[CHUNK:2 * CHUNK]
    atot_e = acs_e[CHUNK - 1:CHUNK, :]
    acs_j = jnp.sum(jnp.where(eye, acs_e, 0.0), axis=0, keepdims=True)
    lmat = jnp.where(tri, jnp.exp(jnp.minimum(acs_e - acs_j, 0.0)), 0.0)
    b_t = _stack4(b_g)
    m = _dot(c_g, b_t, "nt") * lmat
    x_g = xs_g * dt_e
    xbd = jnp.where(blockdiag, _stack4(x_g), 0.0).astype(BF)
    return dict(gs=gs, xs=xs_g, b=b_g, c=c_g, b_t=b_t, dt=dt_e, e=jnp.exp(acs_e), dec=jnp.exp(atot_e - acs_e),
                eat=jnp.exp(atot_e), lmat=lmat, m=m, x=x_g, xbd=xbd)


def _ssd_fwd(xconv, dt, acs, d_exp, comm=None):
    t = xconv.shape[0]
    nc = t // CHUNK

    def body(xc_ref, dt_ref, acs_ref, d_ref, y_ref, hs_ref, state):
        c = pl.program_id(0)

        @pl.when(c == 0)
        def _():
            state[...] = jnp.zeros_like(state)

        hs_ref[...] = state[...]
        tri, eye, blockdiag = _ssd_masks()
        stacked = jnp.concatenate([acs_ref[...], dt_ref[...]], axis=0)
        for g in range(N_GROUPS):
            q = _ssd_group(xc_ref, stacked, g, tri, eye, blockdiag)
            gs = q["gs"]
            h_t = state[:, gs]
            ydiag = _dot(q["m"].astype(BF), q["xbd"])
            yoff = _dot(q["c"], h_t.astype(BF)) * q["e"]
            y_ref[:, gs] = ydiag + yoff + d_ref[:, gs] * q["xs"]
            s_t = _dot(q["b"], (q["x"] * q["dec"]).astype(BF), "tn")
            state[:, gs] = q["eat"] * h_t + s_t

    blk = lambda w: pl.BlockSpec((CHUNK, w), lambda c: (c, 0))
    outs, couts = _pcall(
        "ssd_fwd", body, (nc,),
        [blk(D_XBC), blk(DT_W), blk(DT_W), pl.BlockSpec((1, D_INNER), lambda c: (0, 0))],
        [blk(D_INNER), pl.BlockSpec((None, D_STATE, D_INNER), lambda c: (c, 0, 0))],
        [jax.ShapeDtypeStruct((t, D_INNER), F32), jax.ShapeDtypeStruct((nc, D_STATE, D_INNER), F32)],
        (xconv, dt, acs, d_exp), [pltpu.VMEM((D_STATE, D_INNER), F32)], ("arbitrary",), comm)
    return outs if comm is None else (outs, couts)


def _ssd_bwd(xconv, dt, acs, d_exp, hsave, dy, comm=None):
    t = xconv.shape[0]
    nc = t // CHUNK

    def body(xc_ref, dt_ref, acs_ref, d_ref, hs_ref, dy_ref, dxc_ref, ddt_ref, dacs_ref, dd_ref, dstate):
        c = pl.program_id(0)

        @pl.when(c == 0)
        def _():
            dstate[...] = jnp.zeros_like(dstate)
            dd_ref[...] = jnp.zeros_like(dd_ref)

        tri, eye, blockdiag = _ssd_masks()
        acsv = acs_ref[...]
        stacked = jnp.concatenate([acsv, dt_ref[...]], axis=0)
        eat_heads = jnp.exp(acsv[CHUNK - 1:CHUNK, :])
        ddt_acc = jnp.zeros((CHUNK, DT_W), F32)
        dacs_acc = jnp.zeros((CHUNK, DT_W), F32)
        datot_acc = jnp.zeros((1, DT_W), F32)

        for g in range(N_GROUPS):
            q = _ssd_group(xc_ref, stacked, g, tri, eye, blockdiag)
            gs, xs_g, b_g, c_g, m = q["gs"], q["xs"], q["b"], q["c"], q["m"]
            bs = slice(D_INNER + D_STATE * g, D_INNER + D_STATE * (g + 1))
            cs = slice(D_INNER + 1024 + D_STATE * g, D_INNER + 1024 + D_STATE * (g + 1))
            h_t = hs_ref[:, gs]
            h_b = h_t.astype(BF)
            dy_g = dy_ref[:, gs]
            dy_b = dy_g.astype(BF)
            ds_t = dstate[:, gs]
            ds_b = ds_t.astype(BF)

            yoff = _dot(c_g, h_b) * q["e"]
            edy = (q["e"] * dy_g).astype(BF)
            d_c = _dot(edy, h_b, "nt")
            d_ht = _dot(c_g, edy, "tn")
            bds = _dot(b_g, ds_b)
            xd = q["x"] * q["dec"]
            d_b = _dot(xd.astype(BF), ds_b, "nt")
            dm = _dot(dy_b, q["xbd"], "nt")
            cross = _dot(m.astype(BF), dy_b, "tn")
            dx_full = q["dec"] * bds + _fold4(jnp.where(blockdiag, cross, 0.0))
            dml = (dm * q["lmat"]).astype(BF)
            d_c = d_c + _dot(dml, q["b_t"])
            d_b = d_b + _fold4(_dot(dml, c_g, "tn"))
            w = dm * m
            q_dec = xd * bds
            z = w - jnp.where(eye, jnp.sum(w, axis=0, keepdims=True), 0.0) + dy_g * yoff - q_dec
            rows = jnp.concatenate(
                [jnp.sum(q_dec, axis=0, keepdims=True), jnp.sum(ds_t * h_t, axis=0, keepdims=True),
                 jnp.zeros((6, GROUP_W), F32)], axis=0)
            seg = _split_dot(jnp.concatenate([z, dx_full * xs_g, rows], axis=0), _gather_mat(g), 2)
            dacs_acc = dacs_acc + seg[0:CHUNK]
            ddt_acc = ddt_acc + seg[CHUNK:2 * CHUNK]
            datot_acc = datot_acc + seg[2 * CHUNK:2 * CHUNK + 1] + eat_heads * seg[2 * CHUNK + 1:2 * CHUNK + 2]
            dxc_ref[:, cs] = d_c
            dxc_ref[:, bs] = d_b
            dxc_ref[:, gs] = dx_full * q["dt"] + d_ref[:, gs] * dy_g
            dd_ref[:, gs] += jnp.sum(dy_g * xs_g, axis=0, keepdims=True)
            dstate[:, gs] = q["eat"] * ds_t + d_ht

        rowi = lax.broadcasted_iota(jnp.int32, (CHUNK, DT_W), 0)
        ddt_ref[...] = ddt_acc
        dacs_ref[...] = dacs_acc + jnp.where(rowi == CHUNK - 1, datot_acc, 0.0)

    rev = lambda w: pl.BlockSpec((CHUNK, w), lambda c: (nc - 1 - c, 0))
    vec = pl.BlockSpec((1, D_INNER), lambda c: (0, 0))
    outs, couts = _pcall(
        "ssd_bwd", body, (nc,),
        [rev(D_XBC), rev(DT_W), rev(DT_W), vec,
         pl.BlockSpec((None, D_STATE, D_INNER), lambda c: (nc - 1 - c, 0, 0)), rev(D_INNER)],
        [rev(D_XBC), rev(DT_W), rev(DT_W), vec],
        [jax.ShapeDtypeStruct((t, D_XBC), F32), jax.ShapeDtypeStruct((t, DT_W), F32),
         jax.ShapeDtypeStruct((t, DT_W), F32), jax.ShapeDtypeStruct((1, D_INNER), F32)],
        (xconv, dt, acs, d_exp, hsave, dy),
        [pltpu.VMEM((D_STATE, D_INNER), F32)], ("arbitrary",), comm)
    return outs if comm is None else (outs, couts)


GN_CB = 1024
GN_GROUPS = GN_CB // GROUP_W


def _gnorm_fwd(y, p, w, comm=None):
    t = y.shape[0]
    zoff = OFF_Z // GN_CB

    def body(y_ref, z_ref, w_ref, o_ref):
        for g in range(GN_GROUPS):
            gs = slice(GROUP_W * g, GROUP_W * (g + 1))
            z = z_ref[:, gs].astype(F32)
            yf = y_ref[:, gs] * (z * _sigmoid(z))
            rstd = lax.rsqrt(jnp.mean(yf * yf, axis=-1, keepdims=True) + NORM_EPS)
            o_ref[:, gs] = (yf * rstd * w_ref[:, gs]).astype(BF)

    blk = pl.BlockSpec((TE, GN_CB), lambda i, j: (i, j))
    out, couts = _pcall(
        "gnorm_fwd", body, (t // TE, D_INNER // GN_CB),
        [blk, pl.BlockSpec((TE, GN_CB), lambda i, j: (i, zoff + j)), pl.BlockSpec((1, GN_CB), lambda i, j: (0, j))],
        blk, jax.ShapeDtypeStruct((t, D_INNER), BF), (y, p, w), (), ("parallel", "parallel"), comm)
    return out if comm is None else (out, couts)


def _gnorm_bwd(y, p, w, dyn, comm=None):
    t = y.shape[0]
    zoff = OFF_Z // GN_CB

    def body(y_ref, z_ref, w_ref, dn_ref, dy_ref, dz_ref, dw_ref):
        i = pl.program_id(1)
        for g in range(GN_GROUPS):
            gs = slice(GROUP_W * g, GROUP_W * (g + 1))
            z = z_ref[:, gs].astype(F32)
            yv = y_ref[:, gs]
            s = _sigmoid(z)
            sil = z * s
            yf = yv * sil
            rstd = lax.rsqrt(jnp.mean(yf * yf, axis=-1, keepdims=True) + NORM_EPS)
            xhat = yf * rstd
            dn = dn_ref[:, gs]
            wd = dn * w_ref[:, gs]
            proj = jnp.mean(wd * xhat, axis=-1, keepdims=True)
            dyf = rstd * (wd - xhat * proj)
            dy_ref[:, gs] = dyf * sil
            dz_ref[:, gs] = (dyf * yv * (s * (1.0 + z * (1.0 - s)))).astype(BF)
            part = jnp.sum(dn * xhat, axis=0, keepdims=True)

            @pl.when(i == 0)
            def _():
                dw_ref[:, gs] = part

            @pl.when(i > 0)
            def _():
                dw_ref[:, gs] += part

    blk = pl.BlockSpec((TE, GN_CB), lambda j, i: (i, j))
    vec = pl.BlockSpec((1, GN_CB), lambda j, i: (0, j))
    outs, couts = _pcall(
        "gnorm_bwd", body, (D_INNER // GN_CB, t // TE),
        [blk, pl.BlockSpec((TE, GN_CB), lambda j, i: (i, zoff + j)), vec, blk],
        [blk, pl.BlockSpec((TE, GN_CB), lambda j, i: (i, zoff + j)), vec],
        [jax.ShapeDtypeStruct((t, D_INNER), F32), jax.ShapeDtypeStruct((t, N_MAIN), BF),
         jax.ShapeDtypeStruct((1, D_INNER), F32)],
        (y, p, w, dyn), (), ("parallel", "arbitrary"), comm)
    return outs if comm is None else (outs, couts)


MERGE_CB = 512


def _merge_fwd(p, ya, yb):
    t = ya.shape[0]

    def body(ga_ref, gb_ref, ya_ref, yb_ref, o_ref):
        o_ref[...] = (_sigmoid(ga_ref[...]) * ya_ref[...] + _sigmoid(gb_ref[...]) * yb_ref[...]).astype(BF)

    blk = pl.BlockSpec((TE, MERGE_CB), lambda i, j: (i, j))
    return pl.pallas_call(
        body, name="merge_fwd", grid=(t // TE, D_MODEL // MERGE_CB),
        in_specs=[pl.BlockSpec((TE, MERGE_CB), lambda i, j: (i, 2 * j)),
                  pl.BlockSpec((TE, MERGE_CB), lambda i, j: (i, 2 * j + 1)), blk, blk],
        out_specs=blk, out_shape=jax.ShapeDtypeStruct((t, D_MODEL), BF),
        compiler_params=_params("parallel", "parallel"))(p, p, ya, yb)


def _merge_bwd(p, ya, yb, dm):
    t = ya.shape[0]

    def body(ga_ref, gb_ref, ya_ref, yb_ref, dm_ref, dg_ref, dya_ref, dyb_ref):
        d = dm_ref[...]
        sa = _sigmoid(ga_ref[...])
        sb = _sigmoid(gb_ref[...])
        dg_ref[:, 0:MERGE_CB] = (d * ya_ref[...] * sa * (1.0 - sa)).astype(BF)
        dg_ref[:, MERGE_CB:2 * MERGE_CB] = (d * yb_ref[...] * sb * (1.0 - sb)).astype(BF)
        dya_ref[...] = (d * sa).astype(BF)
        dyb_ref[...] = (d * sb).astype(BF)

    blk = pl.BlockSpec((TE, MERGE_CB), lambda i, j: (i, j))
    return pl.pallas_call(
        body, name="merge_bwd", grid=(t // TE, D_MODEL // MERGE_CB),
        in_specs=[pl.BlockSpec((TE, MERGE_CB), lambda i, j: (i, 2 * j)),
                  pl.BlockSpec((TE, MERGE_CB), lambda i, j: (i, 2 * j + 1)), blk, blk, blk],
        out_specs=[pl.BlockSpec((TE, 2 * MERGE_CB), lambda i, j: (i, j)), blk, blk],
        out_shape=[jax.ShapeDtypeStruct((t, N_GD), BF)] + [jax.ShapeDtypeStruct((t, D_MODEL), BF)] * 2,
        compiler_params=_params("parallel", "parallel"))(p, p, ya, yb, dm)


def _adamw(name, parts, w, m, v, comm=None):
    r, c = w.shape
    tr = _row_tile(r)
    tc = ADAM_COL_TILE if (tr == r and r > 512 and c % ADAM_COL_TILE == 0) else c
    n_parts = parts.shape[0]
    bc1 = 1.0 - ADAM_B1 ** ADAM_STEP
    bc2 = 1.0 - ADAM_B2 ** ADAM_STEP

    def body(p_ref, w_ref, m_ref, v_ref, g_ref, d_ref, nm_ref, nv_ref):
        g = p_ref[0].astype(F32)
        for k in range(1, n_parts):
            g = g + p_ref[k].astype(F32)
        nm = ADAM_B1 * m_ref[...] + (1.0 - ADAM_B1) * g
        nv = ADAM_B2 * v_ref[...] + (1.0 - ADAM_B2) * (g * g)
        g_ref[...] = g
        nm_ref[...] = nm
        nv_ref[...] = nv
        d_ref[...] = -ADAM_LR * ((nm / bc1) / (jnp.sqrt(nv / bc2) + ADAM_EPS) + ADAM_WD * w_ref[...])

    blk = pl.BlockSpec((tr, tc), lambda i, j: (i, j))
    outs, couts = _pcall(
        name, body, (r // tr, c // tc),
        [pl.BlockSpec((n_parts, tr, tc), lambda i, j: (0, i, j)), blk, blk, blk], [blk] * 4,
        [jax.ShapeDtypeStruct((r, c), F32)] * 4, (parts, w, m, v), (), ("parallel", "parallel"), comm)
    return outs if comm is None else (outs, couts)


def _pad_lanes(v, width):
    return jnp.pad(v, ((0, 0), (0, width - v.shape[1])))


def _reduce_start(slots, host):
    outs, sib = host(_pair_comm([a for _, a in slots]))
    sums = [(n, _add_pairs("pairsum_" + n, a, b)) for (n, a), b in zip(slots, sib)]
    return outs, sums


def _train_step(x, target, shard, rep):
    gdt = BF
    t = x.shape[0]
    recv = {}
    (got,) = _comm_call("gather_ffn1_in", _gather_comm([shard["ffn1_w_in"]], [True]))
    w1_in = got.reshape(2 * D_FF, D_MODEL)
    h1 = _rms_fwd("rms1_fwd", x, rep["ffn1_norm"])
    gu1, got = _mm_nt("ffn1_in", h1, w1_in, tn=FF_HALF, out_dtype=BF, comm=_gather_comm(
        [shard["ffn1_w_out"], shard["w_in"], shard["short_conv_w"], shard["ssm_conv_w"]]))
    w1_out = got[0].reshape(D_FF, D_MODEL)
    w_in_t = got[1].reshape(N_IN, D_MODEL)
    short_conv_w = got[2].transpose(1, 0, 2).reshape(3, D_MODEL)
    ssm_conv_w = got[3].transpose(1, 0, 2).reshape(4, D_XBC)
    act1 = _swiglu_fwd("swiglu1_fwd", gu1)
    x1 = _mm_nn("ffn1_out", act1, w1_out, res=x, alpha=0.5)
    ga0 = N_MAIN + N_HEADS
    gb0 = ga0 + D_MODEL
    half = D_MODEL // 2
    w_gd = jnp.concatenate(
        [w_in_t[ga0:ga0 + half], w_in_t[gb0:gb0 + half], w_in_t[ga0 + half:gb0], w_in_t[gb0 + half:],
         w_in_t[N_MAIN:N_MAIN + N_HEADS], jnp.zeros((DT_W - N_HEADS, D_MODEL), BF)], axis=0)
    w_mix_perm = w_in_t[0:3 * D_MODEL].reshape(3, 4, CONV_CB, D_MODEL).transpose(1, 0, 2, 3).reshape(3 * D_MODEL, D_MODEL)

    h2 = _rms_fwd("rms2_fwd", x1, rep["mix_norm"])
    p, got = _mm_nt("proj_main", h2, w_in_t, n=N_MAIN, tn=1024, out_dtype=BF, comm=_gather_comm(
        [shard["short_w_out"], shard["ssm_w_out"], shard["w_out"]]))
    p_gd = _mm_nt("proj_gd", h2, w_gd)
    short_w_out = got[0].reshape(D_MODEL, D_MODEL)
    ssm_w_out = got[1].reshape(D_INNER, D_MODEL)
    w_out = got[2].reshape(D_MODEL, D_MODEL)
    ya_in = _mix_a_fwd(p, short_conv_w)
    y_a = _mm_nn("short_out", ya_in, short_w_out)
    xconv, (got,) = _ssm_conv_fwd(p, ssm_conv_w, rep["ssm_conv_b"], comm=_gather_comm([shard["ffn2_w_out"]]))
    w2_out = got.reshape(D_FF, D_MODEL)
    dt, acs = _dt_fwd(p_gd, rep["dt_bias_pad"], rep["a_log_pad"])
    (y_ssm, hsave), (got,) = _ssd_fwd(xconv, dt, acs, rep["d_exp"], comm=_gather_comm([shard["ffn2_w_in"]], [True]))
    w2_in = got.reshape(2 * D_FF, D_MODEL)
    yn = _gnorm_fwd(y_ssm, p, rep["ssm_norm"])
    y_b = _mm_nn("ssm_out", yn, ssm_w_out, tk=1024)
    merged = _merge_fwd(p_gd, y_a, y_b)
    x2 = _mm_nn("mix_out", merged, w_out, res=x1)

    h3 = _rms_fwd("rms3_fwd", x2, rep["ffn2_norm"])
    gu2 = _mm_nt("ffn2_in", h3, w2_in, tn=FF_HALF, out_dtype=BF)
    act2 = _swiglu_fwd("swiglu2_fwd", gu2)
    x3 = _mm_nn("ffn2_out", act2, w2_out, res=x2, alpha=0.5)

    loss, dx3, dx3h, g_final = _final_loss(x3, rep["final_norm"], target)

    small = {"final_norm": g_final}
    dact2 = _mm_nt("ffn2_out_bwd_act", dx3h, w2_out, out_dtype=BF)
    g_w2_out = _mm_tn("ffn2_out_bwd_w", act2, dx3h, gdt, tm=FF_HALF)
    dgu2 = _swiglu_bwd("swiglu2_bwd", gu2, dact2)
    g_w2_in = _mm_tn("ffn2_in_bwd_w", dgu2, h3, gdt, tm=FF_HALF)
    dh3 = _mm_nn("ffn2_in_bwd_h", dgu2, w2_in, tk=FF_HALF)
    dx2, dx2b, small["ffn2_norm"] = _rms_bwd("rms3_bwd", x2, rep["ffn2_norm"], dh3, dx3, 1.0)

    dmerged = _mm_nt("mix_out_bwd_x", dx2b, w_out)
    g_w_out = _mm_tn("mix_out_bwd_w", merged, dx2b, gdt)
    dp_gd, dya, dyb = _merge_bwd(p_gd, y_a, y_b, dmerged)

    dya_in = _mm_nt("short_out_bwd_x", dya, short_w_out)
    g_short_w_out = _mm_tn("short_out_bwd_w", ya_in, dya, gdt)

    dyn = _mm_nt("ssm_out_bwd_x", dyb, ssm_w_out)
    g_ssm_w_out = _mm_tn("ssm_out_bwd_w", yn, dyb, gdt)
    late = [("ffn2_w_out", g_w2_out.reshape(N_DEV, FF_SHARD // 2, D_MODEL)),
            ("ffn2_w_in", g_w2_in.reshape(N_DEV, FF_SHARD, D_MODEL)),
            ("w_out", g_w_out.reshape(N_DEV, -1, D_MODEL)), ("short_w_out", g_short_w_out.reshape(N_DEV, -1, D_MODEL)),
            ("ssm_w_out", g_ssm_w_out.reshape(N_DEV, -1, D_MODEL))]
    (dy_ssm, dp, small["ssm_norm"]), sums = _reduce_start(
        late, lambda comm: _gnorm_bwd(y_ssm, p, rep["ssm_norm"], dyn, comm=comm))
    dp, g_short_conv = _mix_a_bwd(p, short_conv_w, dya_in, dp)
    (dxconv, ddt, dacs, dd_lane), got = _ssd_bwd(
        xconv, dt, acs, rep["d_exp"], hsave, dy_ssm,
        comm=_chip_comm([a for _, a in sums], [n == "ffn2_w_in" for n, _ in sums]))
    recv.update({n: a for (n, _), a in zip(sums, got)})
    small["ssm_D"] = dd_lane.reshape(N_HEADS, HEAD_DIM).sum(axis=1)[None, :]
    dp, g_ssm_conv, small["ssm_conv_b"] = _ssm_conv_bwd(p, ssm_conv_w, rep["ssm_conv_b"], dxconv, dp)
    dp_gd, dbias, dalog = _dt_bwd(p_gd, rep["dt_bias_pad"], rep["a_log_pad"], dt, ddt, dacs, dp_gd)
    small["ssm_dt_bias"] = dbias[:, :N_HEADS]
    small["ssm_A_log"] = dalog[:, :N_HEADS]

    g_main = _mm_tn("proj_main_bwd_w", dp, h2, gdt, tm=1024)
    g_gd = _mm_tn("proj_gd_bwd_w", dp_gd, h2, gdt)
    g_mix = g_main[0:3 * D_MODEL].reshape(4, 3, CONV_CB, D_MODEL).transpose(1, 0, 2, 3).reshape(3 * D_MODEL, D_MODEL)
    g_in_t = jnp.concatenate(
        [g_mix, g_main[3 * D_MODEL:], g_gd[2 * D_MODEL:2 * D_MODEL + N_HEADS],
         g_gd[0:half], g_gd[2 * half:3 * half], g_gd[half:2 * half], g_gd[3 * half:4 * half]], axis=0).reshape(
        N_DEV, IN_SHARD, D_MODEL)
    dh2, w_sums = _reduce_start(
        [("w_in", g_in_t)], lambda comm: _mm_nn("proj_mix_bwd_x", dp, w_mix_perm, tk=1024, kk=3 * D_MODEL, comm=comm))
    w_sum = w_sums[0][1]

    def w_piece(i):
        return _chip_comm([w_sum], rows=[W_GRAD_ROW_CUTS[i]])

    dh2, got0 = _mm_nn("proj_rest_bwd_x", dp, w_in_t, tk=1024, kk=N_MAIN - 3 * D_MODEL, a_off=3, b_off=3, res=dh2,
                       comm=w_piece(0))
    dh2, got1 = _mm_nn("proj_gd_bwd_x", dp_gd, w_gd, res=dh2, comm=w_piece(1))
    (dx1, dx1h, small["mix_norm"]), got2 = _rms_bwd("rms2_bwd", x1, rep["mix_norm"], dh2, dx2, 0.5, comm=w_piece(2))
    g_w1_out, got3 = _mm_tn("ffn1_out_bwd_w", act1, dx1h, gdt, tm=FF_HALF, comm=w_piece(3))
    rest = [("ffn1_w_out", g_w1_out.reshape(N_DEV, FF_SHARD // 2, D_MODEL)),
            ("short_conv_w", g_short_conv.reshape(3, N_DEV, -1).transpose(1, 0, 2)),
            ("ssm_conv_w", g_ssm_conv.reshape(4, N_DEV, -1).transpose(1, 0, 2))]
    dact1, got = _mm_nt("ffn1_out_bwd_act", dx1h, w1_out, out_dtype=BF,
                        comm=_join_comm(w_piece(4), _pair_comm([a for _, a in rest])))
    got4, sib = got[0], got[1:]
    rest_sums = [(n, _add_pairs("pairsum_" + n, a, b)) for (n, a), b in zip(rest, sib)]
    recv["w_in"] = jnp.concatenate([got0[0], got1[0], got2[0], got3[0], got4], axis=1)
    dgu1, got = _swiglu_bwd("swiglu1_bwd", gu1, dact1, comm=_chip_comm([a for _, a in rest_sums]))
    recv.update({n: a for (n, _), a in zip(rest_sums, got)})

    def part(tag, width, off, comm=None):
        out = _mm_tn("ffn1_in_bwd_w_" + tag, dgu1, h1, gdt, tm=FF_HALF, n=width, col_off=off, comm=comm)
        g, couts = (out, None) if comm is None else out
        return g.reshape(N_DEV, FF_SHARD, width), couts

    g_a, _ = part("a", 384, 0)
    g_b, sib = part("b", 384, 1, _pair_comm([g_a]))
    sum_a = _add_pairs("pairsum_ffn1_w_in_a", g_a, sib[0])
    g_c, (recv_a, sib_b) = part("c", 256, 3, _join_comm(_chip_comm([sum_a], [True]), _pair_comm([g_b])))
    sum_b = _add_pairs("pairsum_ffn1_w_in_b", g_b, sib_b)
    dh1, (recv_b, sib_c) = _mm_nn("ffn1_in_bwd_h", dgu1, w1_in, tk=FF_HALF,
                                  comm=_join_comm(_chip_comm([sum_b], [True]), _pair_comm([g_c])))
    sum_c = _add_pairs("pairsum_ffn1_w_in_c", g_c, sib_c)
    (dx0, _, small["ffn1_norm"]), (recv_c,) = _rms_bwd("rms1_bwd", x, rep["ffn1_norm"], dh1, dx1, 1.0,
                                                        comm=_chip_comm([sum_c], [True]))
    recv["ffn1_w_in"] = jnp.concatenate([recv_a, recv_b, recv_c], axis=2)
    return dx0, recv, _pack_small(small, loss[:, 0:1])


_SMALL = [("ffn1_norm", 1024), ("mix_norm", 1024), ("ssm_conv_b", 4096), ("ssm_dt_bias", 32), ("ssm_A_log", 32),
          ("ssm_D", 32), ("ssm_norm", 2048), ("ffn2_norm", 1024), ("final_norm", 1024)]
SMALL_W = 10368


def _pack_small(d, loss=None):
    parts = [d[n].reshape(1, -1).astype(F32) for n, _ in _SMALL]
    used = sum(sz for _, sz in _SMALL)
    tail = jnp.zeros((1, SMALL_W - used), F32)
    if loss is not None:
        tail = tail.at[:, 0:1].set(loss)
    return jnp.concatenate(parts + [tail], axis=1)


def _adamw_small(parts, w, m, v):
    n_par = len(_SMALL)
    bc1 = 1.0 - ADAM_B1 ** ADAM_STEP
    bc2 = 1.0 - ADAM_B2 ** ADAM_STEP
    used = sum(sz for _, sz in _SMALL)

    def body(*refs):
        p_ref = refs[0]
        ins = refs[1:1 + 3 * n_par]
        outs = refs[1 + 3 * n_par:]
        g_all = p_ref[0]
        for k in range(1, N_DEV):
            g_all = g_all + p_ref[k]
        off = 0
        for i, (_, sz) in enumerate(_SMALL):
            g = g_all[:, off:off + sz]
            w_ref, m_ref, v_ref = ins[3 * i:3 * i + 3]
            nm = ADAM_B1 * m_ref[...] + (1.0 - ADAM_B1) * g
            nv = ADAM_B2 * v_ref[...] + (1.0 - ADAM_B2) * (g * g)
            outs[4 * i][...] = g
            outs[4 * i + 1][...] = -ADAM_LR * ((nm / bc1) / (jnp.sqrt(nv / bc2) + ADAM_EPS) + ADAM_WD * w_ref[...])
            outs[4 * i + 2][...] = nm
            outs[4 * i + 3][...] = nv
            off += sz
        outs[4 * n_par][...] = g_all[:, used:SMALL_W]

    args = [parts]
    out_shape = []
    for name, sz in _SMALL:
        args += [w[name], m[name], v[name]]
        out_shape += [jax.ShapeDtypeStruct((1, sz), F32)] * 4
    out_shape.append(jax.ShapeDtypeStruct((1, SMALL_W - used), F32))
    res = pl.pallas_call(body, name="adamw_small", out_shape=out_shape,
                         compiler_params=pltpu.CompilerParams(vmem_limit_bytes=VMEM_LIMIT_V7X))(*args)
    return {name: tuple(res[4 * i:4 * i + 4]) for i, (name, _) in enumerate(_SMALL)}, res[-1]


_SHARDED = ["ffn1_w_in", "ffn1_w_out", "w_in", "short_conv_w", "short_w_out", "ssm_conv_w", "ssm_w_out", "w_out",
            "ffn2_w_in", "ffn2_w_out"]
_TRANSPOSED = ("ffn1_w_in", "w_in", "ffn2_w_in")
_ORDER = ["ffn1_norm", "ffn1_w_in", "ffn1_w_out", "mix_norm", "w_in", "short_conv_w", "short_w_out", "ssm_conv_w",
          "ssm_conv_b", "ssm_dt_bias", "ssm_A_log", "ssm_D", "ssm_norm", "ssm_w_out", "w_out", "ffn2_norm",
          "ffn2_w_in", "ffn2_w_out", "final_norm"]


def kernel(x, ffn1_norm, ffn1_w_in, ffn1_w_out, mix_norm, w_in, short_conv_w, short_w_out, ssm_conv_w, ssm_conv_b, ssm_dt_bias, ssm_A_log, ssm_D, ssm_norm, ssm_w_out, w_out, ffn2_norm, ffn2_w_in, ffn2_w_out, final_norm, loss_target, m_ffn1_norm, m_ffn1_w_in, m_ffn1_w_out, m_mix_norm, m_w_in, m_short_conv_w, m_short_w_out, m_ssm_conv_w, m_ssm_conv_b, m_ssm_dt_bias, m_ssm_A_log, m_ssm_D, m_ssm_norm, m_ssm_w_out, m_w_out, m_ffn2_norm, m_ffn2_w_in, m_ffn2_w_out, m_final_norm, v_ffn1_norm, v_ffn1_w_in, v_ffn1_w_out, v_mix_norm, v_w_in, v_short_conv_w, v_short_w_out, v_ssm_conv_w, v_ssm_conv_b, v_ssm_dt_bias, v_ssm_A_log, v_ssm_D, v_ssm_norm, v_ssm_w_out, v_w_out, v_ffn2_norm, v_ffn2_w_in, v_ffn2_w_out, v_final_norm):
    w = dict(ffn1_norm=ffn1_norm, ffn1_w_in=ffn1_w_in, ffn1_w_out=ffn1_w_out, mix_norm=mix_norm, w_in=w_in,
             short_conv_w=short_conv_w, short_w_out=short_w_out, ssm_conv_w=ssm_conv_w, ssm_conv_b=ssm_conv_b,
             ssm_dt_bias=ssm_dt_bias, ssm_A_log=ssm_A_log, ssm_D=ssm_D, ssm_norm=ssm_norm, ssm_w_out=ssm_w_out,
             w_out=w_out, ffn2_norm=ffn2_norm, ffn2_w_in=ffn2_w_in, ffn2_w_out=ffn2_w_out, final_norm=final_norm)
    m = dict(ffn1_norm=m_ffn1_norm, ffn1_w_in=m_ffn1_w_in, ffn1_w_out=m_ffn1_w_out, mix_norm=m_mix_norm, w_in=m_w_in,
             short_conv_w=m_short_conv_w, short_w_out=m_short_w_out, ssm_conv_w=m_ssm_conv_w,
             ssm_conv_b=m_ssm_conv_b, ssm_dt_bias=m_ssm_dt_bias, ssm_A_log=m_ssm_A_log, ssm_D=m_ssm_D,
             ssm_norm=m_ssm_norm, ssm_w_out=m_ssm_w_out, w_out=m_w_out, ffn2_norm=m_ffn2_norm,
             ffn2_w_in=m_ffn2_w_in, ffn2_w_out=m_ffn2_w_out, final_norm=m_final_norm)
    v = dict(ffn1_norm=v_ffn1_norm, ffn1_w_in=v_ffn1_w_in, ffn1_w_out=v_ffn1_w_out, mix_norm=v_mix_norm, w_in=v_w_in,
             short_conv_w=v_short_conv_w, short_w_out=v_short_w_out, ssm_conv_w=v_ssm_conv_w,
             ssm_conv_b=v_ssm_conv_b, ssm_dt_bias=v_ssm_dt_bias, ssm_A_log=v_ssm_A_log, ssm_D=v_ssm_D,
             ssm_norm=v_ssm_norm, ssm_w_out=v_ssm_w_out, w_out=v_w_out, ffn2_norm=v_ffn2_norm,
             ffn2_w_in=v_ffn2_w_in, ffn2_w_out=v_ffn2_w_out, final_norm=v_final_norm)
    shapes = {n: w[n].shape for n in _ORDER}

    def local(d, n):
        return d[n][0].T if n in _TRANSPOSED else d[n][0]

    shard = {n: local(w, n) for n in _SHARDED}

    wire = {n: (shard[n] if n in ("short_conv_w", "ssm_conv_w") else shard[n].astype(BF)) for n in _SHARDED}
    rep = {
        "ffn1_norm": ffn1_norm, "mix_norm": mix_norm, "ffn2_norm": ffn2_norm, "ssm_norm": ssm_norm,
        "ssm_conv_b": ssm_conv_b, "final_norm": final_norm.reshape(1, D_MODEL),
        "dt_bias_pad": _pad_lanes(ssm_dt_bias, DT_W), "a_log_pad": _pad_lanes(ssm_A_log, DT_W),
        "d_exp": jnp.repeat(ssm_D, HEAD_DIM, axis=1),
    }
    grad_x, parts, packed = _train_step(x[0], loss_target[0], wire, rep)

    out_g, out_d, out_m, out_v = {}, {}, {}, {}
    for n in _SHARDED:
        if n == "ssm_w_out":
            res, (small_parts,) = _adamw("adamw_" + n, parts[n], shard[n], local(m, n), local(v, n),
                                         comm=_gather_comm([packed]))
        else:
            res = _adamw("adamw_" + n, parts[n], shard[n], local(m, n), local(v, n))
        out_g[n], out_d[n], out_m[n], out_v[n] = [(r.T if n in _TRANSPOSED else r).reshape(shapes[n]) for r in res]
    row = lambda d: {n: d[n].reshape(1, -1) for n, _ in _SMALL}
    sres, loss_row = _adamw_small(small_parts, row(w), row(m), row(v))
    for n, _ in _SMALL:
        out_g[n], out_d[n], out_m[n], out_v[n] = [r.reshape(shapes[n]) for r in sres[n]]
    loss = loss_row[0, 0]
    return (loss, grad_x[None], *[out_g[n] for n in _ORDER], *[out_d[n] for n in _ORDER],
            *[out_m[n] for n in _ORDER], *[out_v[n] for n in _ORDER])
```

```python
import functools

import jax
import jax.numpy as jnp
from jax import lax
from jax.experimental import pallas as pl
from jax.experimental.pallas import tpu as pltpu

F32 = jnp.float32
BF = jnp.bfloat16

N_DEV = 8
D_MODEL = 1024
D_FF = 2816
D_INNER = 2048
D_XBC = 4096
N_HEADS = 32
HEAD_DIM = 64
N_GROUPS = 8
D_STATE = 128
CHUNK = 64
GROUP_W = D_INNER // N_GROUPS
HEADS_PER_GROUP = N_HEADS // N_GROUPS
NORM_EPS = 1e-5
N_IN = 11296
FF_SHARD = 2 * D_FF // N_DEV
FF_HALF = D_FF // 2
IN_SHARD = N_IN // N_DEV

OFF_B, OFF_C, OFF_XA, OFF_Z, OFF_XBC = 0, 1024, 2048, 3072, 5120
N_MAIN = 9216
OFF_DT = 2048
DT_W = 128
N_GD = 2048 + DT_W
W_GRAD_ROW_CUTS = [(0, 512), (512, 720), (720, 896), (896, 1152), (1152, 1412)]

ADAM_LR, ADAM_B1, ADAM_B2, ADAM_EPS, ADAM_WD, ADAM_STEP = 0.001, 0.9, 0.999, 1e-08, 0.01, 10

VMEM_LIMIT_V7X = 56 * 1024 * 1024
TM = 1024
TN_MAX_TOKENS = 2048
TE = 512
ADAM_COL_TILE = 256
GATHER_PIECES = 4
GATHER_PIECE_MIN_ROWS = 512


def _params(*sem):
    return pltpu.CompilerParams(dimension_semantics=sem, vmem_limit_bytes=VMEM_LIMIT_V7X)


_DIMS = {
    "nn": (((1,), (0,)), ((), ())),
    "nt": (((1,), (1,)), ((), ())),
    "tn": (((0,), (0,)), ((), ())),
}


def _dot(a, b, mode="nn"):
    return lax.dot_general(a, b, _DIMS[mode], preferred_element_type=F32)


def _sigmoid(x):
    return 1.0 / (1.0 + jnp.exp(-x))


class _Comm:
    def __init__(self, inputs, out_shapes, sems, start, finish):
        self.inputs, self.out_shapes, self.sems, self.start, self.finish = inputs, out_shapes, sems, start, finish


def _pcall(name, body, grid, in_specs, out_specs, out_shape, args, scratch=(), sem=None, comm=None):
    single = not isinstance(out_shape, (list, tuple))
    out_shapes = [out_shape] if single else list(out_shape)
    out_specs = [out_specs] if single else list(out_specs)
    n_in, n_out, n_scr = len(args), len(out_shapes), len(scratch)
    if comm is None:
        res = pl.pallas_call(
            body, name=name, grid=grid, in_specs=list(in_specs), out_specs=out_specs, out_shape=out_shapes,
            scratch_shapes=list(scratch), compiler_params=_params(*sem))(*args)
        return (res[0] if single else res), []
    nci, nco = len(comm.inputs), len(comm.out_shapes)

    def wrapped(*refs):
        a = refs[:n_in]
        ci = refs[n_in:n_in + nci]
        o0 = n_in + nci
        o = refs[o0:o0 + n_out]
        co = refs[o0 + n_out:o0 + n_out + nco]
        s0 = o0 + n_out + nco
        s = refs[s0:s0 + n_scr]
        cs = refs[s0 + n_scr:]
        pids = [pl.program_id(i) for i in range(len(grid))]
        first = functools.reduce(jnp.logical_and, [p == 0 for p in pids])
        last = functools.reduce(jnp.logical_and, [p == g - 1 for p, g in zip(pids, grid)])

        @pl.when(first)
        def _():
            comm.start(ci, co, cs)

        body(*a, *o, *s)

        @pl.when(last)
        def _():
            comm.finish(ci, co, cs)

    any_spec = pl.BlockSpec(memory_space=pl.ANY)
    res = pl.pallas_call(
        wrapped, name=name, grid=grid, in_specs=list(in_specs) + [any_spec] * nci,
        out_specs=out_specs + [any_spec] * nco, out_shape=out_shapes + list(comm.out_shapes),
        scratch_shapes=list(scratch) + list(comm.sems),
        compiler_params=_params(*(("arbitrary",) * len(grid))))(*args, *comm.inputs)
    core = res[:n_out]
    return (core[0] if single else core), list(res[n_out:])


def _comm_call(name, comm):
    nci, nco = len(comm.inputs), len(comm.out_shapes)

    def body(*refs):
        ci, co, cs = refs[:nci], refs[nci:nci + nco], refs[nci + nco:]
        comm.start(ci, co, cs)
        comm.finish(ci, co, cs)

    any_spec = pl.BlockSpec(memory_space=pl.ANY)
    return pl.pallas_call(
        body, name=name, in_specs=[any_spec] * nci, out_specs=[any_spec] * nco, out_shape=list(comm.out_shapes),
        scratch_shapes=list(comm.sems), compiler_params=pltpu.CompilerParams(has_side_effects=True))(*comm.inputs)


def _remote(src, dst, ssem, rsem, dev):
    return pltpu.make_async_remote_copy(src_ref=src, dst_ref=dst, send_sem=ssem, recv_sem=rsem, device_id=dev,
                                        device_id_type=pl.DeviceIdType.MESH)


def _place():
    x, y, c = lax.axis_index("x"), lax.axis_index("y"), lax.axis_index("c")
    other_chips = [(1 - x, y), (x, 1 - y), (1 - x, 1 - y)]
    return x, y, c, other_chips


def _slot(x, y, c, swap):
    return 4 * y + 2 * x + c if swap else 4 * x + 2 * y + c


def _chip_slot(x, y, swap):
    return 2 * y + x if swap else 2 * x + y


def _gather_comm(shards, swaps=None):
    n = len(shards)
    per = N_DEV - 1
    swaps = [False] * n if swaps is None else swaps
    pieces = []
    for i, a in enumerate(shards):
        rows = a.shape[0]
        k = GATHER_PIECES if (a.ndim == 2 and rows >= GATHER_PIECE_MIN_ROWS) else 1
        step = -(-rows // (k * 8)) * 8
        if k == 1:
            pieces.append((i, 0, None))
        else:
            pieces += [(i, r, min(step, rows - r)) for r in range(0, rows, step)]
    m = len(pieces)

    def src(ins, v):
        i, r, cnt = pieces[v]
        return ins[i] if cnt is None else ins[i].at[pl.ds(r, cnt)]

    def place(outs, v, x, y, c):
        i, r, cnt = pieces[v]
        blk = outs[i].at[_slot(x, y, c, swaps[i])]
        return blk if cnt is None else blk.at[pl.ds(r, cnt)]

    def start(ins, outs, sems):
        send, recv, loc = sems
        x, y, c, chips = _place()
        for v in range(m):
            me = place(outs, v, x, y, c)
            pltpu.make_async_copy(src(ins, v), me, loc.at[v]).start()
            _remote(src(ins, v), me, send.at[per * v], recv.at[per * v], (x, y, 1 - c)).start()
        for j, (qx, qy) in enumerate(chips):
            for v in range(m):
                _remote(src(ins, v), place(outs, v, x, y, c), send.at[per * v + 1 + j], recv.at[per * v + 1 + j],
                        (qx, qy, c)).start()

    def finish(ins, outs, sems):
        send, recv, loc = sems
        x, y, c, chips = _place()
        sib = (x, y, 1 - c)
        for v in range(m):
            for j, (qx, qy) in enumerate(chips):
                blk = place(outs, v, qx, qy, c)
                _remote(blk, blk, send.at[per * v + 1 + j], recv.at[per * v + 1 + j], (qx, qy, c)).wait_recv()
                _remote(blk, blk, send.at[per * v + 4 + j], recv.at[per * v + 4 + j], sib).start()
        for v in range(m):
            blk = place(outs, v, x, y, 1 - c)
            _remote(blk, blk, send.at[per * v], recv.at[per * v], sib).wait_recv()
            for j, (qx, qy) in enumerate(chips):
                blk = place(outs, v, qx, qy, 1 - c)
                _remote(blk, blk, send.at[per * v + 4 + j], recv.at[per * v + 4 + j], sib).wait_recv()
        for v in range(m):
            own = place(outs, v, x, y, c)
            for k in range(per):
                _remote(src(ins, v), own, send.at[per * v + k], recv.at[per * v + k], sib).wait_send()
            pltpu.make_async_copy(src(ins, v), own, loc.at[v]).wait()

    out_shapes = [jax.ShapeDtypeStruct((N_DEV,) + tuple(a.shape), a.dtype) for a in shards]
    sems = [pltpu.SemaphoreType.DMA((per * m,)), pltpu.SemaphoreType.DMA((per * m,)), pltpu.SemaphoreType.DMA((m,))]
    return _Comm(list(shards), out_shapes, sems, start, finish)


def _pair_comm(slots):
    n = len(slots)

    def copies(ins, outs, sems):
        send, recv = sems
        x, y, c, _ = _place()
        sib = (x, y, 1 - c)
        out = []
        for i in range(n):
            for q in range(4):
                out.append(_remote(ins[i].at[2 * q + 1 - c], outs[i].at[q], send.at[4 * i + q], recv.at[4 * i + q], sib))
        return out

    def start(ins, outs, sems):
        for cp in copies(ins, outs, sems):
            cp.start()

    def finish(ins, outs, sems):
        for cp in copies(ins, outs, sems):
            cp.wait_send()
            cp.wait_recv()

    out_shapes = [jax.ShapeDtypeStruct((4,) + tuple(a.shape[1:]), a.dtype) for a in slots]
    sems = [pltpu.SemaphoreType.DMA((4 * n,)), pltpu.SemaphoreType.DMA((4 * n,))]
    return _Comm(list(slots), out_shapes, sems, start, finish)


def _chip_comm(chip_sums, swaps=None, rows=None):
    n = len(chip_sums)
    swaps = [False] * n if swaps is None else swaps
    rows = [None] * n if rows is None else rows

    def src(ins, i, q):
        return ins[i].at[q] if rows[i] is None else ins[i].at[q, pl.ds(rows[i][0], rows[i][1] - rows[i][0])]

    def start(ins, outs, sems):
        send, recv, loc = sems
        x, y, c, chips = _place()
        for i in range(n):
            mine = _chip_slot(x, y, swaps[i])
            pltpu.make_async_copy(src(ins, i, mine), outs[i].at[mine], loc.at[i]).start()
            for j, (qx, qy) in enumerate(chips):
                _remote(src(ins, i, _chip_slot(qx, qy, swaps[i])), outs[i].at[mine], send.at[3 * i + j],
                        recv.at[3 * i + j], (qx, qy, c)).start()

    def finish(ins, outs, sems):
        send, recv, loc = sems
        x, y, c, chips = _place()
        for i in range(n):
            mine = _chip_slot(x, y, swaps[i])
            for j, (qx, qy) in enumerate(chips):
                theirs = _chip_slot(qx, qy, swaps[i])
                cp = _remote(src(ins, i, theirs), outs[i].at[theirs], send.at[3 * i + j], recv.at[3 * i + j], (qx, qy, c))
                cp.wait_send()
                cp.wait_recv()
            pltpu.make_async_copy(src(ins, i, mine), outs[i].at[mine], loc.at[i]).wait()

    def out_shape(a, r):
        shape = a.shape if r is None else (a.shape[0], r[1] - r[0]) + tuple(a.shape[2:])
        return jax.ShapeDtypeStruct(shape, a.dtype)

    out_shapes = [out_shape(a, r) for a, r in zip(chip_sums, rows)]
    sems = [pltpu.SemaphoreType.DMA((3 * n,)), pltpu.SemaphoreType.DMA((3 * n,)), pltpu.SemaphoreType.DMA((n,))]
    return _Comm(list(chip_sums), out_shapes, sems, start, finish)


def _join_comm(a, b):
    na_i, na_o, na_s = len(a.inputs), len(a.out_shapes), len(a.sems)

    def start(ins, outs, sems):
        a.start(ins[:na_i], outs[:na_o], sems[:na_s])
        b.start(ins[na_i:], outs[na_o:], sems[na_s:])

    def finish(ins, outs, sems):
        a.finish(ins[:na_i], outs[:na_o], sems[:na_s])
        b.finish(ins[na_i:], outs[na_o:], sems[na_s:])

    return _Comm(a.inputs + b.inputs, a.out_shapes + b.out_shapes, a.sems + b.sems, start, finish)


def _row_tile(r):
    for cand in (256, 128):
        if r > cand and r % cand == 0:
            return cand
    return r


def _add_pairs(name, slots, sib):
    r, c = slots.shape[1:]
    tr = _row_tile(r)

    def body(core_ref, s_ref, b_ref, o_ref):
        o_ref[...] = (s_ref[...].astype(F32) + b_ref[...].astype(F32)).astype(o_ref.dtype)

    core = jnp.full((1,), lax.axis_index("c"), jnp.int32)
    return pl.pallas_call(
        body, name=name,
        grid_spec=pltpu.PrefetchScalarGridSpec(
            num_scalar_prefetch=1, grid=(4, r // tr),
            in_specs=[pl.BlockSpec((None, None, tr, c), lambda q, i, core_ref: (q, core_ref[0], i, 0)),
                      pl.BlockSpec((None, tr, c), lambda q, i, core_ref: (q, i, 0))],
            out_specs=pl.BlockSpec((None, tr, c), lambda q, i, core_ref: (q, i, 0))),
        out_shape=jax.ShapeDtypeStruct((4, r, c), slots.dtype),
        compiler_params=_params("parallel", "parallel"))(core, slots.reshape(4, 2, r, c), sib)


def _matmul(name, mode, a, b, grid, a_spec, b_spec, o_spec, out_shape, acc_shape,
            res=None, res_spec=None, alpha=1.0, comm=None):
    nk = grid[-1]
    has_res = res is not None

    def body(*refs):
        if has_res:
            a_ref, b_ref, r_ref, o_ref = refs[:4]
        else:
            a_ref, b_ref, o_ref = refs[:3]
            r_ref = None
        part = _dot(a_ref[...], b_ref[...], mode)

        def finish(v):
            if alpha != 1.0:
                v = v * alpha
            if has_res:
                v = r_ref[...] + v
            o_ref[...] = v.astype(o_ref.dtype)

        if nk == 1:
            finish(part)
        else:
            acc = refs[-1]
            k = pl.program_id(len(grid) - 1)

            @pl.when(k == 0)
            def _():
                acc[...] = part

            @pl.when(k > 0)
            def _():
                acc[...] += part

            @pl.when(k == nk - 1)
            def _():
                finish(acc[...])

    in_specs = [a_spec, b_spec] + ([res_spec] if has_res else [])
    args = (a, b) + ((res,) if has_res else ())
    scratch = [] if nk == 1 else [pltpu.VMEM(acc_shape, F32)]
    sem = ("parallel",) * (len(grid) - 1) + ("arbitrary",)
    out, couts = _pcall(name, body, grid, in_specs, o_spec, out_shape, args, scratch, sem, comm)
    return out if comm is None else (out, couts)


def _mm_nn(name, a, b, out_dtype=F32, res=None, alpha=1.0, tk=None, kk=None, a_off=0, b_off=0, comm=None):
    t = a.shape[0]
    kk = a.shape[1] if kk is None else kk
    n = b.shape[1]
    tk = kk if tk is None else tk
    grid = (t // TM, 1, kk // tk)
    return _matmul(
        name, "nn", a, b, grid,
        pl.BlockSpec((TM, tk), lambda i, j, k: (i, k + a_off)),
        pl.BlockSpec((tk, n), lambda i, j, k: (k + b_off, 0)),
        pl.BlockSpec((TM, n), lambda i, j, k: (i, 0)),
        jax.ShapeDtypeStruct((t, n), out_dtype), (TM, n),
        res=res, res_spec=pl.BlockSpec((TM, n), lambda i, j, k: (i, 0)), alpha=alpha, comm=comm)


def _mm_nt(name, a, b, n=None, tn=None, tk=None, out_dtype=F32, comm=None):
    t, kk = a.shape
    n = b.shape[0] if n is None else n
    tn = n if tn is None else tn
    tk = kk if tk is None else tk
    grid = (n // tn, t // TM, kk // tk)
    return _matmul(
        name, "nt", a, b, grid,
        pl.BlockSpec((TM, tk), lambda j, i, k: (i, k)),
        pl.BlockSpec((tn, tk), lambda j, i, k: (j, k)),
        pl.BlockSpec((TM, tn), lambda j, i, k: (i, j)),
        jax.ShapeDtypeStruct((t, n), out_dtype), (TM, tn), comm=comm)


def _mm_tn(name, a, b, out_dtype, tm=None, n=None, col_off=0, comm=None):
    t, m = a.shape
    n = b.shape[1] if n is None else n
    tm = m if tm is None else tm
    tk = t if t <= TN_MAX_TOKENS else TM
    grid = (m // tm, 1, t // tk)
    return _matmul(
        name, "tn", a, b, grid,
        pl.BlockSpec((tk, tm), lambda j, i, k: (k, j)),
        pl.BlockSpec((tk, n), lambda j, i, k: (k, col_off)),
        pl.BlockSpec((tm, n), lambda j, i, k: (j, 0)),
        jax.ShapeDtypeStruct((m, n), out_dtype), (tm, n), comm=comm)


def _rms_fwd(name, x, w):
    t, d = x.shape

    def body(x_ref, w_ref, h_ref):
        xv = x_ref[...]
        rstd = lax.rsqrt(jnp.mean(xv * xv, axis=-1, keepdims=True) + NORM_EPS)
        h_ref[...] = (xv * rstd * w_ref[...]).astype(h_ref.dtype)

    return pl.pallas_call(
        body, name=name, grid=(t // TE,),
        in_specs=[pl.BlockSpec((TE, d), lambda i: (i, 0)), pl.BlockSpec((1, d), lambda i: (0, 0))],
        out_specs=pl.BlockSpec((TE, d), lambda i: (i, 0)),
        out_shape=jax.ShapeDtypeStruct((t, d), BF), compiler_params=_params("parallel"))(x, w)


def _rms_bwd(name, x, w, dh, dres, out_scale, comm=None):
    t, d = x.shape

    def body(x_ref, w_ref, dh_ref, dres_ref, dx_ref, dxb_ref, dw_ref):
        i = pl.program_id(0)
        xv = x_ref[...]
        rstd = lax.rsqrt(jnp.mean(xv * xv, axis=-1, keepdims=True) + NORM_EPS)
        xhat = xv * rstd
        dhv = dh_ref[...]
        wd = dhv * w_ref[...]
        proj = jnp.mean(wd * xhat, axis=-1, keepdims=True)
        dx = dres_ref[...] + rstd * (wd - xhat * proj)
        dx_ref[...] = dx
        dxb_ref[...] = (dx * out_scale).astype(BF)
        part = jnp.sum(dhv * xhat, axis=0, keepdims=True)

        @pl.when(i == 0)
        def _():
            dw_ref[...] = part

        @pl.when(i > 0)
        def _():
            dw_ref[...] += part

    row = pl.BlockSpec((TE, d), lambda i: (i, 0))
    vec = pl.BlockSpec((1, d), lambda i: (0, 0))
    outs, couts = _pcall(
        name, body, (t // TE,), [row, vec, row, row], [row, row, vec],
        [jax.ShapeDtypeStruct((t, d), F32), jax.ShapeDtypeStruct((t, d), BF), jax.ShapeDtypeStruct((1, d), F32)],
        (x, w, dh, dres), (), ("arbitrary",), comm)
    return outs if comm is None else (outs, couts)


def _final_loss(x, w, target):
    t, d = x.shape

    def body(x_ref, w_ref, t_ref, loss_ref, dx_ref, dxb_ref, dw_ref):
        i = pl.program_id(0)
        xv = x_ref[...]
        rstd = lax.rsqrt(jnp.mean(xv * xv, axis=-1, keepdims=True) + NORM_EPS)
        xhat = xv * rstd
        err = xhat * w_ref[...] - t_ref[...]
        lpart = 0.5 * jnp.sum(jnp.mean(err * err, axis=-1, keepdims=True), axis=0, keepdims=True)
        dy = err * (1.0 / d)
        wd = dy * w_ref[...]
        proj = jnp.mean(wd * xhat, axis=-1, keepdims=True)
        dx = rstd * (wd - xhat * proj)
        dx_ref[...] = dx
        dxb_ref[...] = (0.5 * dx).astype(BF)
        part = jnp.sum(dy * xhat, axis=0, keepdims=True)
        lfull = jnp.broadcast_to(lpart, (1, 128))

        @pl.when(i == 0)
        def _():
            dw_ref[...] = part
            loss_ref[...] = lfull

        @pl.when(i > 0)
        def _():
            dw_ref[...] += part
            loss_ref[...] += lfull

    row = pl.BlockSpec((TE, d), lambda i: (i, 0))
    vec = pl.BlockSpec((1, d), lambda i: (0, 0))
    return pl.pallas_call(
        body, name="final_loss", grid=(t // TE,), in_specs=[row, vec, row],
        out_specs=[pl.BlockSpec((1, 128), lambda i: (0, 0)), row, row, vec],
        out_shape=[jax.ShapeDtypeStruct((1, 128), F32), jax.ShapeDtypeStruct((t, d), F32),
                   jax.ShapeDtypeStruct((t, d), BF), jax.ShapeDtypeStruct((1, d), F32)],
        compiler_params=_params("arbitrary"))(x, w, target)


def _swiglu_fwd(name, gu, comm=None):
    t = gu.shape[0]

    def body(g_ref, u_ref, a_ref):
        g = g_ref[...].astype(F32)
        a_ref[...] = (g * _sigmoid(g) * u_ref[...].astype(F32)).astype(BF)

    blk = (TE, FF_HALF)
    out, couts = _pcall(
        name, body, (t // TE, 2),
        [pl.BlockSpec(blk, lambda i, j: (i, 2 * j)), pl.BlockSpec(blk, lambda i, j: (i, 2 * j + 1))],
        pl.BlockSpec(blk, lambda i, j: (i, j)), jax.ShapeDtypeStruct((t, D_FF), BF),
        (gu, gu), (), ("parallel", "parallel"), comm)
    return out if comm is None else (out, couts)


def _swiglu_bwd(name, gu, dact, comm=None):
    t = gu.shape[0]

    def body(g_ref, u_ref, da_ref, o_ref):
        g = g_ref[...].astype(F32)
        da = da_ref[...].astype(F32)
        s = _sigmoid(g)
        o_ref[:, 0:FF_HALF] = (da * u_ref[...].astype(F32) * (s * (1.0 + g * (1.0 - s)))).astype(BF)
        o_ref[:, FF_HALF:2 * FF_HALF] = (da * g * s).astype(BF)

    blk = (TE, FF_HALF)
    out, couts = _pcall(
        name, body, (t // TE, 2),
        [pl.BlockSpec(blk, lambda i, j: (i, 2 * j)), pl.BlockSpec(blk, lambda i, j: (i, 2 * j + 1)),
         pl.BlockSpec(blk, lambda i, j: (i, j))],
        pl.BlockSpec((TE, 2 * FF_HALF), lambda i, j: (i, j)),
        jax.ShapeDtypeStruct((t, 2 * D_FF), BF), (gu, gu, dact), (), ("parallel", "parallel"), comm)
    return out if comm is None else (out, couts)


CONV_CB = 256


CONV_ROWS = 64
CONV_HALO = 16


def _taps_down(ext, w, k):
    shifted = [pltpu.roll(ext, k - 1 - j, 0)[CONV_HALO:] for j in range(k - 1)] + [ext[CONV_HALO:]]
    out = shifted[k - 1] * w[k - 1:k, :]
    for j in range(k - 1):
        out = out + shifted[j] * w[j:j + 1, :]
    return out, shifted


def _taps_up(ext, w, k):
    rows = ext.shape[0]
    n = rows - CONV_HALO
    out = ext[:n] * w[k - 1:k, :]
    for j in range(k - 1):
        out = out + pltpu.roll(ext, rows - (k - 1 - j), 0)[:n] * w[j:j + 1, :]
    return out


def _rows_before(ref, i, r0):
    start = pl.multiple_of(jnp.maximum(r0 - CONV_HALO, 0), CONV_HALO)
    return jnp.where(i > 0, ref[pl.ds(start, CONV_HALO), :].astype(F32), 0.0)


def _rows_after(ref, r0, t):
    start = pl.multiple_of(jnp.minimum(r0 + CONV_ROWS, t - CONV_HALO), CONV_HALO)
    return ref[pl.ds(start, CONV_HALO), :].astype(F32)


def _fold8(v):
    return v.reshape(v.shape[0] // 8, 8, v.shape[1]).sum(axis=0)


def _silu_grad(pre):
    s = _sigmoid(pre)
    return s * (1.0 + pre * (1.0 - s))


def _pspec(t, off):
    base = off // CONV_CB
    return pl.BlockSpec((t, CONV_CB), lambda j: (0, base + j))


def _mix_a_fwd(p, conv_w):
    t = p.shape[0]

    def body(b_ref, c_ref, xa_ref, w_ref, o_ref):
        w = w_ref[...]

        def step(i, carry):
            r0 = pl.multiple_of(i * CONV_ROWS, CONV_ROWS)
            rows = pl.ds(r0, CONV_ROWS)
            q = c_ref[rows, :].astype(F32) * xa_ref[rows, :].astype(F32)
            q_before = _rows_before(c_ref, i, r0) * _rows_before(xa_ref, i, r0)
            va, _ = _taps_down(jnp.concatenate([q_before, q], axis=0), w, 3)
            o_ref[rows, :] = (b_ref[rows, :].astype(F32) * va).astype(BF)
            return carry

        lax.fori_loop(0, t // CONV_ROWS, step, 0)

    return pl.pallas_call(
        body, name="mix_a_fwd", grid=(D_MODEL // CONV_CB,),
        in_specs=[_pspec(t, OFF_B), _pspec(t, OFF_C), _pspec(t, OFF_XA),
                  pl.BlockSpec((3, CONV_CB), lambda j: (0, j))],
        out_specs=pl.BlockSpec((t, CONV_CB), lambda j: (0, j)),
        out_shape=jax.ShapeDtypeStruct((t, D_MODEL), BF), compiler_params=_params("parallel"))(p, p, p, conv_w)


def _mix_a_bwd(p, conv_w, dya, dp):
    t = p.shape[0]

    def body(b_ref, c_ref, xa_ref, w_ref, dy_ref, dp_in, dp_ref, dw_ref):
        del dp_in
        w = w_ref[...]
        n = t // CONV_ROWS

        def step(i, acc):
            r0 = pl.multiple_of(i * CONV_ROWS, CONV_ROWS)
            rows = pl.ds(r0, CONV_ROWS)
            cv = c_ref[rows, :].astype(F32)
            xav = xa_ref[rows, :].astype(F32)
            q_before = _rows_before(c_ref, i, r0) * _rows_before(xa_ref, i, r0)
            va, shifted = _taps_down(jnp.concatenate([q_before, cv * xav], axis=0), w, 3)
            dyv = dy_ref[rows, :]
            dp_ref[rows, 0:CONV_CB] = (dyv * va).astype(BF)
            dv = dyv * b_ref[rows, :].astype(F32)
            dv_after = jnp.where(i < n - 1, _rows_after(dy_ref, r0, t) * _rows_after(b_ref, r0, t), 0.0)
            dq = _taps_up(jnp.concatenate([dv, dv_after], axis=0), w, 3)
            dp_ref[rows, CONV_CB:2 * CONV_CB] = (dq * xav).astype(BF)
            dp_ref[rows, 2 * CONV_CB:3 * CONV_CB] = (dq * cv).astype(BF)
            return tuple(a + _fold8(dv * s) for a, s in zip(acc, shifted))

        zero = jnp.zeros((8, CONV_CB), F32)
        acc = lax.fori_loop(0, n, step, (zero, zero, zero))
        for j in range(3):
            dw_ref[j:j + 1, :] = jnp.sum(acc[j], axis=0, keepdims=True)

    col = pl.BlockSpec((t, CONV_CB), lambda j: (0, j))
    wsp = pl.BlockSpec((3, CONV_CB), lambda j: (0, j))
    return pl.pallas_call(
        body, name="mix_a_bwd", grid=(D_MODEL // CONV_CB,),
        in_specs=[_pspec(t, OFF_B), _pspec(t, OFF_C), _pspec(t, OFF_XA), wsp, col, pl.BlockSpec(memory_space=pl.ANY)],
        out_specs=[pl.BlockSpec((t, 3 * CONV_CB), lambda j: (0, j)), wsp],
        out_shape=[jax.ShapeDtypeStruct(dp.shape, dp.dtype), jax.ShapeDtypeStruct((3, D_MODEL), F32)],
        input_output_aliases={5: 0},
        compiler_params=_params("parallel"))(p, p, p, conv_w, dya, dp)


def _ssm_conv_fwd(p, conv_w, conv_b, comm=None):
    t = p.shape[0]

    def body(x_ref, w_ref, b_ref, o_ref):
        w = w_ref[...]
        bias = b_ref[...]

        def step(i, carry):
            r0 = pl.multiple_of(i * CONV_ROWS, CONV_ROWS)
            rows = pl.ds(r0, CONV_ROWS)
            ext = jnp.concatenate([_rows_before(x_ref, i, r0), x_ref[rows, :].astype(F32)], axis=0)
            pre = _taps_down(ext, w, 4)[0] + bias
            o_ref[rows, :] = pre * _sigmoid(pre)
            return carry

        lax.fori_loop(0, t // CONV_ROWS, step, 0)

    out, couts = _pcall(
        "ssm_conv_fwd", body, (D_XBC // CONV_CB,),
        [_pspec(t, OFF_XBC), pl.BlockSpec((4, CONV_CB), lambda j: (0, j)), pl.BlockSpec((1, CONV_CB), lambda j: (0, j))],
        pl.BlockSpec((t, CONV_CB), lambda j: (0, j)), jax.ShapeDtypeStruct((t, D_XBC), F32),
        (p, conv_w, conv_b), (), ("parallel",), comm)
    return out if comm is None else (out, couts)


def _ssm_conv_bwd(p, conv_w, conv_b, dxc, dp):
    t = p.shape[0]

    def body(x_ref, w_ref, b_ref, d_ref, dp_in, dx_ref, dw_ref, db_ref):
        del dp_in
        w = w_ref[...]
        bias = b_ref[...]
        n = t // CONV_ROWS

        def step(i, acc):
            r0 = pl.multiple_of(i * CONV_ROWS, CONV_ROWS)
            rows = pl.ds(r0, CONV_ROWS)
            x_cur = x_ref[rows, :].astype(F32)
            pre, shifted = _taps_down(jnp.concatenate([_rows_before(x_ref, i, r0), x_cur], axis=0), w, 4)
            pre = pre + bias
            dpre = d_ref[rows, :] * _silu_grad(pre)
            ext_after = jnp.concatenate([x_cur[CONV_ROWS - CONV_HALO:], _rows_after(x_ref, r0, t)], axis=0)
            pre_after = _taps_down(ext_after, w, 4)[0] + bias
            dpre_after = jnp.where(i < n - 1, _rows_after(d_ref, r0, t) * _silu_grad(pre_after), 0.0)
            dx_ref[rows, :] = _taps_up(jnp.concatenate([dpre, dpre_after], axis=0), w, 4).astype(BF)
            new = tuple(a + _fold8(dpre * s) for a, s in zip(acc[:4], shifted))
            return new + (acc[4] + _fold8(dpre),)

        zero = jnp.zeros((8, CONV_CB), F32)
        acc = lax.fori_loop(0, n, step, (zero,) * 5)
        for j in range(4):
            dw_ref[j:j + 1, :] = jnp.sum(acc[j], axis=0, keepdims=True)
        db_ref[...] = jnp.sum(acc[4], axis=0, keepdims=True)

    col = pl.BlockSpec((t, CONV_CB), lambda j: (0, j))
    wsp = pl.BlockSpec((4, CONV_CB), lambda j: (0, j))
    bsp = pl.BlockSpec((1, CONV_CB), lambda j: (0, j))
    return pl.pallas_call(
        body, name="ssm_conv_bwd", grid=(D_XBC // CONV_CB,),
        in_specs=[_pspec(t, OFF_XBC), wsp, bsp, col, pl.BlockSpec(memory_space=pl.ANY)],
        out_specs=[_pspec(t, OFF_XBC), wsp, bsp],
        out_shape=[jax.ShapeDtypeStruct(dp.shape, dp.dtype), jax.ShapeDtypeStruct((4, D_XBC), F32),
                   jax.ShapeDtypeStruct((1, D_XBC), F32)],
        input_output_aliases={4: 0},
        compiler_params=_params("parallel"))(p, conv_w, conv_b, dxc, dp)


DT_ROWS = 512


def _tri(lower):
    r = lax.broadcasted_iota(jnp.int32, (CHUNK, CHUNK), 0)
    c = lax.broadcasted_iota(jnp.int32, (CHUNK, CHUNK), 1)
    return jnp.where((r >= c) if lower else (r <= c), 1.0, 0.0).astype(F32)


def _dot_exact(a, b):
    return lax.dot_general(a, b, _DIMS["nn"], preferred_element_type=F32, precision=lax.Precision.HIGHEST)


def _dt_fwd(p, bias_pad, alog_pad):
    t = p.shape[0]

    def body(raw_ref, b_ref, al_ref, dt_ref, acs_ref):
        z = raw_ref[...] + b_ref[...]
        dt = jnp.maximum(z, 0.0) + jnp.log(1.0 + jnp.exp(-jnp.abs(z)))
        dt_ref[...] = dt
        a = dt * (-jnp.exp(al_ref[...]))
        tri = _tri(True)
        for k in range(DT_ROWS // CHUNK):
            acs_ref[k * CHUNK:(k + 1) * CHUNK, :] = _dot_exact(tri, a[k * CHUNK:(k + 1) * CHUNK, :])

    blk = pl.BlockSpec((DT_ROWS, DT_W), lambda i: (i, 0))
    vec = pl.BlockSpec((1, DT_W), lambda i: (0, 0))
    return pl.pallas_call(
        body, name="dt_fwd", grid=(t // DT_ROWS,),
        in_specs=[pl.BlockSpec((DT_ROWS, DT_W), lambda i: (i, OFF_DT // DT_W)), vec, vec],
        out_specs=[blk, blk], out_shape=[jax.ShapeDtypeStruct((t, DT_W), F32)] * 2,
        compiler_params=_params("parallel"))(p, bias_pad, alog_pad)


def _dt_bwd(p, bias_pad, alog_pad, dt, ddt, dacs, dp_gd):
    t = p.shape[0]

    def body(raw_ref, b_ref, al_ref, dt_ref, ddt_ref, dacs_ref, dp_in, draw_ref, db_ref, dal_ref):
        del dp_in
        i = pl.program_id(0)
        acoef = -jnp.exp(al_ref[...])
        triu = _tri(False)
        das = []
        for k in range(DT_ROWS // CHUNK):
            das.append(_dot_exact(triu, dacs_ref[k * CHUNK:(k + 1) * CHUNK, :]))
        da = jnp.concatenate(das, axis=0)
        dtv = dt_ref[...]
        ddt_tot = ddt_ref[...] + da * acoef
        lane = lax.broadcasted_iota(jnp.int32, (DT_ROWS, DT_W), 1)
        draw = jnp.where(lane < N_HEADS, ddt_tot * _sigmoid(raw_ref[...] + b_ref[...]), 0.0)
        draw_ref[...] = draw.astype(BF)
        pb = jnp.sum(draw, axis=0, keepdims=True)
        pa = jnp.sum(da * dtv * acoef, axis=0, keepdims=True)

        @pl.when(i == 0)
        def _():
            db_ref[...] = pb
            dal_ref[...] = pa

        @pl.when(i > 0)
        def _():
            db_ref[...] += pb
            dal_ref[...] += pa

    blk = pl.BlockSpec((DT_ROWS, DT_W), lambda i: (i, 0))
    vec = pl.BlockSpec((1, DT_W), lambda i: (0, 0))
    return pl.pallas_call(
        body, name="dt_bwd", grid=(t // DT_ROWS,),
        in_specs=[pl.BlockSpec((DT_ROWS, DT_W), lambda i: (i, OFF_DT // DT_W)), vec, vec, blk, blk, blk,
                  pl.BlockSpec(memory_space=pl.ANY)],
        out_specs=[pl.BlockSpec((DT_ROWS, DT_W), lambda i: (i, OFF_DT // DT_W)), vec, vec],
        out_shape=[jax.ShapeDtypeStruct(dp_gd.shape, dp_gd.dtype), jax.ShapeDtypeStruct((1, DT_W), F32),
                   jax.ShapeDtypeStruct((1, DT_W), F32)],
        input_output_aliases={6: 0},
        compiler_params=_params("arbitrary"))(p, bias_pad, alog_pad, dt, ddt, dacs, dp_gd)


def _split_dot(z, onehot, terms):
    out = None
    rest = z
    for _ in range(terms):
        piece = rest.astype(BF)
        part = _dot(piece, onehot)
        out = part if out is None else out + part
        rest = rest - piece.astype(F32)
    return out


def _spread_mat(g):
    row = lax.broadcasted_iota(jnp.int32, (DT_W, GROUP_W), 0)
    lane = lax.broadcasted_iota(jnp.int32, (DT_W, GROUP_W), 1)
    return jnp.where(row == HEADS_PER_GROUP * g + lane // HEAD_DIM, 1.0, 0.0).astype(BF)


def _gather_mat(g):
    row = lax.broadcasted_iota(jnp.int32, (GROUP_W, DT_W), 0)
    lane = lax.broadcasted_iota(jnp.int32, (GROUP_W, DT_W), 1)
    return jnp.where(lane == HEADS_PER_GROUP * g + row // HEAD_DIM, 1.0, 0.0).astype(BF)


def _ssd_masks():
    row = lax.broadcasted_iota(jnp.int32, (CHUNK, GROUP_W), 0)
    col = lax.broadcasted_iota(jnp.int32, (CHUNK, GROUP_W), 1) % HEAD_DIM
    brow = lax.broadcasted_iota(jnp.int32, (GROUP_W, GROUP_W), 0) // HEAD_DIM
    bcol = lax.broadcasted_iota(jnp.int32, (GROUP_W, GROUP_W), 1) // HEAD_DIM
    return row >= col, row == col, brow == bcol


def _stack4(v):
    return jnp.concatenate([v, v, v, v], axis=0)


def _fold4(v):
    return v[0:CHUNK] + v[CHUNK:2 * CHUNK] + v[2 * CHUNK:3 * CHUNK] + v[3 * CHUNK:4 * CHUNK]


def _ssd_group(xc_ref, stacked, g, tri, eye, blockdiag):
    gs = slice(GROUP_W * g, GROUP_W * (g + 1))
    xs_g = xc_ref[:, gs]
    b_g = xc_ref[:, D_INNER + D_STATE * g:D_INNER + D_STATE * (g + 1)].astype(BF)
    c_g = xc_ref[:, D_INNER + 1024 + D_STATE * g:D_INNER + 1024 + D_STATE * (g + 1)].astype(BF)
    wide = _split_dot(stacked, _spread_mat(g), 3)
    acs_e, dt_e = wide[0:CHUNK], wide[CHUNK:2 * CHUNK]
    atot_e = acs_e[CHUNK - 1:CHUNK, :]
    acs_j = jnp.sum(jnp.where(eye, acs_e, 0.0), axis=0, keepdims=True)
    lmat = jnp.where(tri, jnp.exp(jnp.minimum(acs_e - acs_j, 0.0)), 0.0)
    b_t = _stack4(b_g)
    m = _dot(c_g, b_t, "nt") * lmat
    x_g = xs_g * dt_e
    xbd = jnp.where(blockdiag, _stack4(x_g), 0.0).astype(BF)
    return dict(gs=gs, xs=xs_g, b=b_g, c=c_g, b_t=b_t, dt=dt_e, e=jnp.exp(acs_e), dec=jnp.exp(atot_e - acs_e),
                eat=jnp.exp(atot_e), lmat=lmat, m=m, x=x_g, xbd=xbd)


def _ssd_fwd(xconv, dt, acs, d_exp, comm=None):
    t = xconv.shape[0]
    nc = t // CHUNK

    def body(xc_ref, dt_ref, acs_ref, d_ref, y_ref, hs_ref, state):
        c = pl.program_id(0)

        @pl.when(c == 0)
        def _():
            state[...] = jnp.zeros_like(state)

        hs_ref[...] = state[...]
        tri, eye, blockdiag = _ssd_masks()
        stacked = jnp.concatenate([acs_ref[...], dt_ref[...]], axis=0)
        for g in range(N_GROUPS):
            q = _ssd_group(xc_ref, stacked, g, tri, eye, blockdiag)
            gs = q["gs"]
            h_t = state[:, gs]
            ydiag = _dot(q["m"].astype(BF), q["xbd"])
            yoff = _dot(q["c"], h_t.astype(BF)) * q["e"]
            y_ref[:, gs] = ydiag + yoff + d_ref[:, gs] * q["xs"]
            s_t = _dot(q["b"], (q["x"] * q["dec"]).astype(BF), "tn")
            state[:, gs] = q["eat"] * h_t + s_t

    blk = lambda w: pl.BlockSpec((CHUNK, w), lambda c: (c, 0))
    outs, couts = _pcall(
        "ssd_fwd", body, (nc,),
        [blk(D_XBC), blk(DT_W), blk(DT_W), pl.BlockSpec((1, D_INNER), lambda c: (0, 0))],
        [blk(D_INNER), pl.BlockSpec((None, D_STATE, D_INNER), lambda c: (c, 0, 0))],
        [jax.ShapeDtypeStruct((t, D_INNER), F32), jax.ShapeDtypeStruct((nc, D_STATE, D_INNER), F32)],
        (xconv, dt, acs, d_exp), [pltpu.VMEM((D_STATE, D_INNER), F32)], ("arbitrary",), comm)
    return outs if comm is None else (outs, couts)


def _ssd_bwd(xconv, dt, acs, d_exp, hsave, dy, comm=None):
    t = xconv.shape[0]
    nc = t // CHUNK

    def body(xc_ref, dt_ref, acs_ref, d_ref, hs_ref, dy_ref, dxc_ref, ddt_ref, dacs_ref, dd_ref, dstate):
        c = pl.program_id(0)

        @pl.when(c == 0)
        def _():
            dstate[...] = jnp.zeros_like(dstate)
            dd_ref[...] = jnp.zeros_like(dd_ref)

        tri, eye, blockdiag = _ssd_masks()
        acsv = acs_ref[...]
        stacked = jnp.concatenate([acsv, dt_ref[...]], axis=0)
        eat_heads = jnp.exp(acsv[CHUNK - 1:CHUNK, :])
        ddt_acc = jnp.zeros((CHUNK, DT_W), F32)
        dacs_acc = jnp.zeros((CHUNK, DT_W), F32)
        datot_acc = jnp.zeros((1, DT_W), F32)

        for g in range(N_GROUPS):
            q = _ssd_group(xc_ref, stacked, g, tri, eye, blockdiag)
            gs, xs_g, b_g, c_g, m = q["gs"], q["xs"], q["b"], q["c"], q["m"]
            bs = slice(D_INNER + D_STATE * g, D_INNER + D_STATE * (g + 1))
            cs = slice(D_INNER + 1024 + D_STATE * g, D_INNER + 1024 + D_STATE * (g + 1))
            h_t = hs_ref[:, gs]
            h_b = h_t.astype(BF)
            dy_g = dy_ref[:, gs]
            dy_b = dy_g.astype(BF)
            ds_t = dstate[:, gs]
            ds_b = ds_t.astype(BF)

            yoff = _dot(c_g, h_b) * q["e"]
            edy = (q["e"] * dy_g).astype(BF)
            d_c = _dot(edy, h_b, "nt")
            d_ht = _dot(c_g, edy, "tn")
            bds = _dot(b_g, ds_b)
            xd = q["x"] * q["dec"]
            d_b = _dot(xd.astype(BF), ds_b, "nt")
            dm = _dot(dy_b, q["xbd"], "nt")
            cross = _dot(m.astype(BF), dy_b, "tn")
            dx_full = q["dec"] * bds + _fold4(jnp.where(blockdiag, cross, 0.0))
            dml = (dm * q["lmat"]).astype(BF)
            d_c = d_c + _dot(dml, q["b_t"])
            d_b = d_b + _fold4(_dot(dml, c_g, "tn"))
            w = dm * m
            q_dec = xd * bds
            z = w - jnp.where(eye, jnp.sum(w, axis=0, keepdims=True), 0.0) + dy_g * yoff - q_dec
            rows = jnp.concatenate(
                [jnp.sum(q_dec, axis=0, keepdims=True), jnp.sum(ds_t * h_t, axis=0, keepdims=True),
                 jnp.zeros((6, GROUP_W), F32)], axis=0)
            seg = _split_dot(jnp.concatenate([z, dx_full * xs_g, rows], axis=0), _gather_mat(g), 2)
            dacs_acc = dacs_acc + seg[0:CHUNK]
            ddt_acc = ddt_acc + seg[CHUNK:2 * CHUNK]
            datot_acc = datot_acc + seg[2 * CHUNK:2 * CHUNK + 1] + eat_heads * seg[2 * CHUNK + 1:2 * CHUNK + 2]
            dxc_ref[:, cs] = d_c
            dxc_ref[:, bs] = d_b
            dxc_ref[:, gs] = dx_full * q["dt"] + d_ref[:, gs] * dy_g
            dd_ref[:, gs] += jnp.sum(dy_g * xs_g, axis=0, keepdims=True)
            dstate[:, gs] = q["eat"] * ds_t + d_ht

        rowi = lax.broadcasted_iota(jnp.int32, (CHUNK, DT_W), 0)
        ddt_ref[...] = ddt_acc
        dacs_ref[...] = dacs_acc + jnp.where(rowi == CHUNK - 1, datot_acc, 0.0)

    rev = lambda w: pl.BlockSpec((CHUNK, w), lambda c: (nc - 1 - c, 0))
    vec = pl.BlockSpec((1, D_INNER), lambda c: (0, 0))
    outs, couts = _pcall(
        "ssd_bwd", body, (nc,),
        [rev(D_XBC), rev(DT_W), rev(DT_W), vec,
         pl.BlockSpec((None, D_STATE, D_INNER), lambda c: (nc - 1 - c, 0, 0)), rev(D_INNER)],
        [rev(D_XBC), rev(DT_W), rev(DT_W), vec],
        [jax.ShapeDtypeStruct((t, D_XBC), F32), jax.ShapeDtypeStruct((t, DT_W), F32),
         jax.ShapeDtypeStruct((t, DT_W), F32), jax.ShapeDtypeStruct((1, D_INNER), F32)],
        (xconv, dt, acs, d_exp, hsave, dy),
        [pltpu.VMEM((D_STATE, D_INNER), F32)], ("arbitrary",), comm)
    return outs if comm is None else (outs, couts)


GN_CB = 1024
GN_GROUPS = GN_CB // GROUP_W


def _gnorm_fwd(y, p, w, comm=None):
    t = y.shape[0]
    zoff = OFF_Z // GN_CB

    def body(y_ref, z_ref, w_ref, o_ref):
        for g in range(GN_GROUPS):
            gs = slice(GROUP_W * g, GROUP_W * (g + 1))
            z = z_ref[:, gs].astype(F32)
            yf = y_ref[:, gs] * (z * _sigmoid(z))
            rstd = lax.rsqrt(jnp.mean(yf * yf, axis=-1, keepdims=True) + NORM_EPS)
            o_ref[:, gs] = (yf * rstd * w_ref[:, gs]).astype(BF)

    blk = pl.BlockSpec((TE, GN_CB), lambda i, j: (i, j))
    out, couts = _pcall(
        "gnorm_fwd", body, (t // TE, D_INNER // GN_CB),
        [blk, pl.BlockSpec((TE, GN_CB), lambda i, j: (i, zoff + j)), pl.BlockSpec((1, GN_CB), lambda i, j: (0, j))],
        blk, jax.ShapeDtypeStruct((t, D_INNER), BF), (y, p, w), (), ("parallel", "parallel"), comm)
    return out if comm is None else (out, couts)


def _gnorm_bwd(y, p, w, dyn, comm=None):
    t = y.shape[0]
    zoff = OFF_Z // GN_CB

    def body(y_ref, z_ref, w_ref, dn_ref, dy_ref, dz_ref, dw_ref):
        i = pl.program_id(1)
        for g in range(GN_GROUPS):
            gs = slice(GROUP_W * g, GROUP_W * (g + 1))
            z = z_ref[:, gs].astype(F32)
            yv = y_ref[:, gs]
            s = _sigmoid(z)
            sil = z * s
            yf = yv * sil
            rstd = lax.rsqrt(jnp.mean(yf * yf, axis=-1, keepdims=True) + NORM_EPS)
            xhat = yf * rstd
            dn = dn_ref[:, gs]
            wd = dn * w_ref[:, gs]
            proj = jnp.mean(wd * xhat, axis=-1, keepdims=True)
            dyf = rstd * (wd - xhat * proj)
            dy_ref[:, gs] = dyf * sil
            dz_ref[:, gs] = (dyf * yv * (s * (1.0 + z * (1.0 - s)))).astype(BF)
            part = jnp.sum(dn * xhat, axis=0, keepdims=True)

            @pl.when(i == 0)
            def _():
                dw_ref[:, gs] = part

            @pl.when(i > 0)
            def _():
                dw_ref[:, gs] += part

    blk = pl.BlockSpec((TE, GN_CB), lambda j, i: (i, j))
    vec = pl.BlockSpec((1, GN_CB), lambda j, i: (0, j))
    outs, couts = _pcall(
        "gnorm_bwd", body, (D_INNER // GN_CB, t // TE),
        [blk, pl.BlockSpec((TE, GN_CB), lambda j, i: (i, zoff + j)), vec, blk],
        [blk, pl.BlockSpec((TE, GN_CB), lambda j, i: (i, zoff + j)), vec],
        [jax.ShapeDtypeStruct((t, D_INNER), F32), jax.ShapeDtypeStruct((t, N_MAIN), BF),
         jax.ShapeDtypeStruct((1, D_INNER), F32)],
        (y, p, w, dyn), (), ("parallel", "arbitrary"), comm)
    return outs if comm is None else (outs, couts)


MERGE_CB = 512


def _merge_fwd(p, ya, yb):
    t = ya.shape[0]

    def body(ga_ref, gb_ref, ya_ref, yb_ref, o_ref):
        o_ref[...] = (_sigmoid(ga_ref[...]) * ya_ref[...] + _sigmoid(gb_ref[...]) * yb_ref[...]).astype(BF)

    blk = pl.BlockSpec((TE, MERGE_CB), lambda i, j: (i, j))
    return pl.pallas_call(
        body, name="merge_fwd", grid=(t // TE, D_MODEL // MERGE_CB),
        in_specs=[pl.BlockSpec((TE, MERGE_CB), lambda i, j: (i, 2 * j)),
                  pl.BlockSpec((TE, MERGE_CB), lambda i, j: (i, 2 * j + 1)), blk, blk],
        out_specs=blk, out_shape=jax.ShapeDtypeStruct((t, D_MODEL), BF),
        compiler_params=_params("parallel", "parallel"))(p, p, ya, yb)


def _merge_bwd(p, ya, yb, dm):
    t = ya.shape[0]

    def body(ga_ref, gb_ref, ya_ref, yb_ref, dm_ref, dg_ref, dya_ref, dyb_ref):
        d = dm_ref[...]
        sa = _sigmoid(ga_ref[...])
        sb = _sigmoid(gb_ref[...])
        dg_ref[:, 0:MERGE_CB] = (d * ya_ref[...] * sa * (1.0 - sa)).astype(BF)
        dg_ref[:, MERGE_CB:2 * MERGE_CB] = (d * yb_ref[...] * sb * (1.0 - sb)).astype(BF)
        dya_ref[...] = (d * sa).astype(BF)
        dyb_ref[...] = (d * sb).astype(BF)

    blk = pl.BlockSpec((TE, MERGE_CB), lambda i, j: (i, j))
    return pl.pallas_call(
        body, name="merge_bwd", grid=(t // TE, D_MODEL // MERGE_CB),
        in_specs=[pl.BlockSpec((TE, MERGE_CB), lambda i, j: (i, 2 * j)),
                  pl.BlockSpec((TE, MERGE_CB), lambda i, j: (i, 2 * j + 1)), blk, blk, blk],
        out_specs=[pl.BlockSpec((TE, 2 * MERGE_CB), lambda i, j: (i, j)), blk, blk],
        out_shape=[jax.ShapeDtypeStruct((t, N_GD), BF)] + [jax.ShapeDtypeStruct((t, D_MODEL), BF)] * 2,
        compiler_params=_params("parallel", "parallel"))(p, p, ya, yb, dm)


def _adamw(name, parts, w, m, v, comm=None):
    r, c = w.shape
    tr = _row_tile(r)
    tc = ADAM_COL_TILE if (tr == r and r > 512 and c % ADAM_COL_TILE == 0) else c
    n_parts = parts.shape[0]
    bc1 = 1.0 - ADAM_B1 ** ADAM_STEP
    bc2 = 1.0 - ADAM_B2 ** ADAM_STEP

    def body(p_ref, w_ref, m_ref, v_ref, g_ref, d_ref, nm_ref, nv_ref):
        g = p_ref[0].astype(F32)
        for k in range(1, n_parts):
            g = g + p_ref[k].astype(F32)
        nm = ADAM_B1 * m_ref[...] + (1.0 - ADAM_B1) * g
        nv = ADAM_B2 * v_ref[...] + (1.0 - ADAM_B2) * (g * g)
        g_ref[...] = g
        nm_ref[...] = nm
        nv_ref[...] = nv
        d_ref[...] = -ADAM_LR * ((nm / bc1) / (jnp.sqrt(nv / bc2) + ADAM_EPS) + ADAM_WD * w_ref[...])

    blk = pl.BlockSpec((tr, tc), lambda i, j: (i, j))
    outs, couts = _pcall(
        name, body, (r // tr, c // tc),
        [pl.BlockSpec((n_parts, tr, tc), lambda i, j: (0, i, j)), blk, blk, blk], [blk] * 4,
        [jax.ShapeDtypeStruct((r, c), F32)] * 4, (parts, w, m, v), (), ("parallel", "parallel"), comm)
    return outs if comm is None else (outs, couts)


def _pad_lanes(v, width):
    return jnp.pad(v, ((0, 0), (0, width - v.shape[1])))


def _reduce_start(slots, host):
    outs, sib = host(_pair_comm([a for _, a in slots]))
    sums = [(n, _add_pairs("pairsum_" + n, a, b)) for (n, a), b in zip(slots, sib)]
    return outs, sums


def _train_step(x, target, shard, rep):
    gdt = BF
    recv = {}
    (got,) = _comm_call("gather_ffn1_in", _gather_comm([shard["ffn1_w_in"]], [True]))
    w1_in = got.reshape(2 * D_FF, D_MODEL)
    h1 = _rms_fwd("rms1_fwd", x, rep["ffn1_norm"])
    gu1, got = _mm_nt("ffn1_in", h1, w1_in, tn=FF_HALF, out_dtype=BF, comm=_gather_comm(
        [shard["ffn1_w_out"], shard["w_in"], shard["short_conv_w"], shard["ssm_conv_w"]]))
    w1_out = got[0].reshape(D_FF, D_MODEL)
    w_in_t = got[1].reshape(N_IN, D_MODEL)
    short_conv_w = got[2].transpose(1, 0, 2).reshape(3, D_MODEL)
    ssm_conv_w = got[3].transpose(1, 0, 2).reshape(4, D_XBC)
    act1 = _swiglu_fwd("swiglu1_fwd", gu1)
    x1 = _mm_nn("ffn1_out", act1, w1_out, res=x, alpha=0.5)
    ga0 = N_MAIN + N_HEADS
    gb0 = ga0 + D_MODEL
    half = D_MODEL // 2
    w_gd = jnp.concatenate(
        [w_in_t[ga0:ga0 + half], w_in_t[gb0:gb0 + half], w_in_t[ga0 + half:gb0], w_in_t[gb0 + half:],
         w_in_t[N_MAIN:N_MAIN + N_HEADS], jnp.zeros((DT_W - N_HEADS, D_MODEL), BF)], axis=0)
    w_mix_perm = w_in_t[0:3 * D_MODEL].reshape(3, 4, CONV_CB, D_MODEL).transpose(1, 0, 2, 3).reshape(3 * D_MODEL, D_MODEL)

    h2 = _rms_fwd("rms2_fwd", x1, rep["mix_norm"])
    p, got = _mm_nt("proj_main", h2, w_in_t, n=N_MAIN, tn=1024, out_dtype=BF, comm=_gather_comm(
        [shard["short_w_out"], shard["ssm_w_out"], shard["w_out"]]))
    p_gd = _mm_nt("proj_gd", h2, w_gd)
    short_w_out = got[0].reshape(D_MODEL, D_MODEL)
    ssm_w_out = got[1].reshape(D_INNER, D_MODEL)
    w_out = got[2].reshape(D_MODEL, D_MODEL)
    ya_in = _mix_a_fwd(p, short_conv_w)
    y_a = _mm_nn("short_out", ya_in, short_w_out)
    xconv, (got,) = _ssm_conv_fwd(p, ssm_conv_w, rep["ssm_conv_b"], comm=_gather_comm([shard["ffn2_w_out"]]))
    w2_out = got.reshape(D_FF, D_MODEL)
    dt, acs = _dt_fwd(p_gd, rep["dt_bias_pad"], rep["a_log_pad"])
    (y_ssm, hsave), (got,) = _ssd_fwd(xconv, dt, acs, rep["d_exp"], comm=_gather_comm([shard["ffn2_w_in"]], [True]))
    w2_in = got.reshape(2 * D_FF, D_MODEL)
    yn = _gnorm_fwd(y_ssm, p, rep["ssm_norm"])
    y_b = _mm_nn("ssm_out", yn, ssm_w_out, tk=1024)
    merged = _merge_fwd(p_gd, y_a, y_b)
    x2 = _mm_nn("mix_out", merged, w_out, res=x1)

    h3 = _rms_fwd("rms3_fwd", x2, rep["ffn2_norm"])
    gu2 = _mm_nt("ffn2_in", h3, w2_in, tn=FF_HALF, out_dtype=BF)
    act2 = _swiglu_fwd("swiglu2_fwd", gu2)
    x3 = _mm_nn("ffn2_out", act2, w2_out, res=x2, alpha=0.5)

    loss, dx3, dx3h, g_final = _final_loss(x3, rep["final_norm"], target)

    small = {"final_norm": g_final}
    dact2 = _mm_nt("ffn2_out_bwd_act", dx3h, w2_out, out_dtype=BF)
    g_w2_out = _mm_tn("ffn2_out_bwd_w", act2, dx3h, gdt, tm=FF_HALF)
    dgu2 = _swiglu_bwd("swiglu2_bwd", gu2, dact2)
    g_w2_in = _mm_tn("ffn2_in_bwd_w", dgu2, h3, gdt, tm=FF_HALF)
    dh3 = _mm_nn("ffn2_in_bwd_h", dgu2, w2_in, tk=FF_HALF)
    dx2, dx2b, small["ffn2_norm"] = _rms_bwd("rms3_bwd", x2, rep["ffn2_norm"], dh3, dx3, 1.0)

    dmerged = _mm_nt("mix_out_bwd_x", dx2b, w_out)
    g_w_out = _mm_tn("mix_out_bwd_w", merged, dx2b, gdt)
    dp_gd, dya, dyb = _merge_bwd(p_gd, y_a, y_b, dmerged)

    dya_in = _mm_nt("short_out_bwd_x", dya, short_w_out)
    g_short_w_out = _mm_tn("short_out_bwd_w", ya_in, dya, gdt)

    dyn = _mm_nt("ssm_out_bwd_x", dyb, ssm_w_out)
    g_ssm_w_out = _mm_tn("ssm_out_bwd_w", yn, dyb, gdt)
    late = [("ffn2_w_out", g_w2_out.reshape(N_DEV, FF_SHARD // 2, D_MODEL)),
            ("ffn2_w_in", g_w2_in.reshape(N_DEV, FF_SHARD, D_MODEL)),
            ("w_out", g_w_out.reshape(N_DEV, -1, D_MODEL)), ("short_w_out", g_short_w_out.reshape(N_DEV, -1, D_MODEL)),
            ("ssm_w_out", g_ssm_w_out.reshape(N_DEV, -1, D_MODEL))]
    (dy_ssm, dp, small["ssm_norm"]), sums = _reduce_start(
        late, lambda comm: _gnorm_bwd(y_ssm, p, rep["ssm_norm"], dyn, comm=comm))
    dp, g_short_conv = _mix_a_bwd(p, short_conv_w, dya_in, dp)
    (dxconv, ddt, dacs, dd_lane), got = _ssd_bwd(
        xconv, dt, acs, rep["d_exp"], hsave, dy_ssm,
        comm=_chip_comm([a for _, a in sums], [n == "ffn2_w_in" for n, _ in sums]))
    recv.update({n: a for (n, _), a in zip(sums, got)})
    small["ssm_D"] = dd_lane.reshape(N_HEADS, HEAD_DIM).sum(axis=1)[None, :]
    dp, g_ssm_conv, small["ssm_conv_b"] = _ssm_conv_bwd(p, ssm_conv_w, rep["ssm_conv_b"], dxconv, dp)
    dp_gd, dbias, dalog = _dt_bwd(p_gd, rep["dt_bias_pad"], rep["a_log_pad"], dt, ddt, dacs, dp_gd)
    small["ssm_dt_bias"] = dbias[:, :N_HEADS]
    small["ssm_A_log"] = dalog[:, :N_HEADS]

    g_main = _mm_tn("proj_main_bwd_w", dp, h2, gdt, tm=1024)
    g_gd = _mm_tn("proj_gd_bwd_w", dp_gd, h2, gdt)
    g_mix = g_main[0:3 * D_MODEL].reshape(4, 3, CONV_CB, D_MODEL).transpose(1, 0, 2, 3).reshape(3 * D_MODEL, D_MODEL)
    g_in_t = jnp.concatenate(
        [g_mix, g_main[3 * D_MODEL:], g_gd[2 * D_MODEL:2 * D_MODEL + N_HEADS],
         g_gd[0:half], g_gd[2 * half:3 * half], g_gd[half:2 * half], g_gd[3 * half:4 * half]], axis=0).reshape(
        N_DEV, IN_SHARD, D_MODEL)
    dh2, w_sums = _reduce_start(
        [("w_in", g_in_t)], lambda comm: _mm_nn("proj_mix_bwd_x", dp, w_mix_perm, tk=1024, kk=3 * D_MODEL, comm=comm))
    w_sum = w_sums[0][1]

    def w_piece(i):
        return _chip_comm([w_sum], rows=[W_GRAD_ROW_CUTS[i]])

    dh2, got0 = _mm_nn("proj_rest_bwd_x", dp, w_in_t, tk=1024, kk=N_MAIN - 3 * D_MODEL, a_off=3, b_off=3, res=dh2,
                       comm=w_piece(0))
    dh2, got1 = _mm_nn("proj_gd_bwd_x", dp_gd, w_gd, res=dh2, comm=w_piece(1))
    (dx1, dx1h, small["mix_norm"]), got2 = _rms_bwd("rms2_bwd", x1, rep["mix_norm"], dh2, dx2, 0.5, comm=w_piece(2))
    g_w1_out, got3 = _mm_tn("ffn1_out_bwd_w", act1, dx1h, gdt, tm=FF_HALF, comm=w_piece(3))
    rest = [("ffn1_w_out", g_w1_out.reshape(N_DEV, FF_SHARD // 2, D_MODEL)),
            ("short_conv_w", g_short_conv.reshape(3, N_DEV, -1).transpose(1, 0, 2)),
            ("ssm_conv_w", g_ssm_conv.reshape(4, N_DEV, -1).transpose(1, 0, 2))]
    dact1, got = _mm_nt("ffn1_out_bwd_act", dx1h, w1_out, out_dtype=BF,
                        comm=_join_comm(w_piece(4), _pair_comm([a for _, a in rest])))
    got4, sib = got[0], got[1:]
    rest_sums = [(n, _add_pairs("pairsum_" + n, a, b)) for (n, a), b in zip(rest, sib)]
    recv["w_in"] = jnp.concatenate([got0[0], got1[0], got2[0], got3[0], got4], axis=1)
    dgu1, got = _swiglu_bwd("swiglu1_bwd", gu1, dact1, comm=_chip_comm([a for _, a in rest_sums]))
    recv.update({n: a for (n, _), a in zip(rest_sums, got)})

    def part(tag, width, off, comm=None):
        out = _mm_tn("ffn1_in_bwd_w_" + tag, dgu1, h1, gdt, tm=FF_HALF, n=width, col_off=off, comm=comm)
        g, couts = (out, None) if comm is None else out
        return g.reshape(N_DEV, FF_SHARD, width), couts

    g_a, _ = part("a", 384, 0)
    g_b, sib = part("b", 384, 1, _pair_comm([g_a]))
    sum_a = _add_pairs("pairsum_ffn1_w_in_a", g_a, sib[0])
    g_c, (recv_a, sib_b) = part("c", 256, 3, _join_comm(_chip_comm([sum_a], [True]), _pair_comm([g_b])))
    sum_b = _add_pairs("pairsum_ffn1_w_in_b", g_b, sib_b)
    dh1, (recv_b, sib_c) = _mm_nn("ffn1_in_bwd_h", dgu1, w1_in, tk=FF_HALF,
                                  comm=_join_comm(_chip_comm([sum_b], [True]), _pair_comm([g_c])))
    sum_c = _add_pairs("pairsum_ffn1_w_in_c", g_c, sib_c)
    (dx0, _, small["ffn1_norm"]), (recv_c,) = _rms_bwd("rms1_bwd", x, rep["ffn1_norm"], dh1, dx1, 1.0,
                                                        comm=_chip_comm([sum_c], [True]))
    recv["ffn1_w_in"] = jnp.concatenate([recv_a, recv_b, recv_c], axis=2)
    return dx0, recv, _pack_small(small, loss[:, 0:1])


_SMALL = [("ffn1_norm", 1024), ("mix_norm", 1024), ("ssm_conv_b", 4096), ("ssm_dt_bias", 32), ("ssm_A_log", 32),
          ("ssm_D", 32), ("ssm_norm", 2048), ("ffn2_norm", 1024), ("final_norm", 1024)]
SMALL_W = 10368


def _pack_small(d, loss=None):
    parts = [d[n].reshape(1, -1).astype(F32) for n, _ in _SMALL]
    used = sum(sz for _, sz in _SMALL)
    tail = jnp.zeros((1, SMALL_W - used), F32)
    if loss is not None:
        tail = tail.at[:, 0:1].set(loss)
    return jnp.concatenate(parts + [tail], axis=1)


def _adamw_small(parts, w, m, v):
    n_par = len(_SMALL)
    bc1 = 1.0 - ADAM_B1 ** ADAM_STEP
    bc2 = 1.0 - ADAM_B2 ** ADAM_STEP
    used = sum(sz for _, sz in _SMALL)

    def body(*refs):
        p_ref = refs[0]
        ins = refs[1:1 + 3 * n_par]
        outs = refs[1 + 3 * n_par:]
        g_all = p_ref[0]
        for k in range(1, N_DEV):
            g_all = g_all + p_ref[k]
        off = 0
        for i, (_, sz) in enumerate(_SMALL):
            g = g_all[:, off:off + sz]
            w_ref, m_ref, v_ref = ins[3 * i:3 * i + 3]
            nm = ADAM_B1 * m_ref[...] + (1.0 - ADAM_B1) * g
            nv = ADAM_B2 * v_ref[...] + (1.0 - ADAM_B2) * (g * g)
            outs[4 * i][...] = g
            outs[4 * i + 1][...] = -ADAM_LR * ((nm / bc1) / (jnp.sqrt(nv / bc2) + ADAM_EPS) + ADAM_WD * w_ref[...])
            outs[4 * i + 2][...] = nm
            outs[4 * i + 3][...] = nv
            off += sz
        outs[4 * n_par][...] = g_all[:, used:SMALL_W]

    args = [parts]
    out_shape = []
    for name, sz in _SMALL:
        args += [w[name], m[name], v[name]]
        out_shape += [jax.ShapeDtypeStruct((1, sz), F32)] * 4
    out_shape.append(jax.ShapeDtypeStruct((1, SMALL_W - used), F32))
    res = pl.pallas_call(body, name="adamw_small", out_shape=out_shape,
                         compiler_params=pltpu.CompilerParams(vmem_limit_bytes=VMEM_LIMIT_V7X))(*args)
    return {name: tuple(res[4 * i:4 * i + 4]) for i, (name, _) in enumerate(_SMALL)}, res[-1]


_SHARDED = ["ffn1_w_in", "ffn1_w_out", "w_in", "short_conv_w", "short_w_out", "ssm_conv_w", "ssm_w_out", "w_out",
            "ffn2_w_in", "ffn2_w_out"]
_TRANSPOSED = ("ffn1_w_in", "w_in", "ffn2_w_in")
_ORDER = ["ffn1_norm", "ffn1_w_in", "ffn1_w_out", "mix_norm", "w_in", "short_conv_w", "short_w_out", "ssm_conv_w",
          "ssm_conv_b", "ssm_dt_bias", "ssm_A_log", "ssm_D", "ssm_norm", "ssm_w_out", "w_out", "ffn2_norm",
          "ffn2_w_in", "ffn2_w_out", "final_norm"]


def kernel(x, ffn1_norm, ffn1_w_in, ffn1_w_out, mix_norm, w_in, short_conv_w, short_w_out, ssm_conv_w, ssm_conv_b, ssm_dt_bias, ssm_A_log, ssm_D, ssm_norm, ssm_w_out, w_out, ffn2_norm, ffn2_w_in, ffn2_w_out, final_norm, loss_target, m_ffn1_norm, m_ffn1_w_in, m_ffn1_w_out, m_mix_norm, m_w_in, m_short_conv_w, m_short_w_out, m_ssm_conv_w, m_ssm_conv_b, m_ssm_dt_bias, m_ssm_A_log, m_ssm_D, m_ssm_norm, m_ssm_w_out, m_w_out, m_ffn2_norm, m_ffn2_w_in, m_ffn2_w_out, m_final_norm, v_ffn1_norm, v_ffn1_w_in, v_ffn1_w_out, v_mix_norm, v_w_in, v_short_conv_w, v_short_w_out, v_ssm_conv_w, v_ssm_conv_b, v_ssm_dt_bias, v_ssm_A_log, v_ssm_D, v_ssm_norm, v_ssm_w_out, v_w_out, v_ffn2_norm, v_ffn2_w_in, v_ffn2_w_out, v_final_norm):
    w = dict(ffn1_norm=ffn1_norm, ffn1_w_in=ffn1_w_in, ffn1_w_out=ffn1_w_out, mix_norm=mix_norm, w_in=w_in,
             short_conv_w=short_conv_w, short_w_out=short_w_out, ssm_conv_w=ssm_conv_w, ssm_conv_b=ssm_conv_b,
             ssm_dt_bias=ssm_dt_bias, ssm_A_log=ssm_A_log, ssm_D=ssm_D, ssm_norm=ssm_norm, ssm_w_out=ssm_w_out,
             w_out=w_out, ffn2_norm=ffn2_norm, ffn2_w_in=ffn2_w_in, ffn2_w_out=ffn2_w_out, final_norm=final_norm)
    m = dict(ffn1_norm=m_ffn1_norm, ffn1_w_in=m_ffn1_w_in, ffn1_w_out=m_ffn1_w_out, mix_norm=m_mix_norm, w_in=m_w_in,
             short_conv_w=m_short_conv_w, short_w_out=m_short_w_out, ssm_conv_w=m_ssm_conv_w,
             ssm_conv_b=m_ssm_conv_b, ssm_dt_bias=m_ssm_dt_bias, ssm_A_log=m_ssm_A_log, ssm_D=m_ssm_D,
             ssm_norm=m_ssm_norm, ssm_w_out=m_ssm_w_out, w_out=m_w_out, ffn2_norm=m_ffn2_norm,
             ffn2_w_in=m_ffn2_w_in, ffn2_w_out=m_ffn2_w_out, final_norm=m_final_norm)
    v = dict(ffn1_norm=v_ffn1_norm, ffn1_w_in=v_ffn1_w_in, ffn1_w_out=v_ffn1_w_out, mix_norm=v_mix_norm, w_in=v_w_in,
             short_conv_w=v_short_conv_w, short_w_out=v_short_w_out, ssm_conv_w=v_ssm_conv_w,
             ssm_conv_b=v_ssm_conv_b, ssm_dt_bias=v_ssm_dt_bias, ssm_A_log=v_ssm_A_log, ssm_D=v_ssm_D,
             ssm_norm=v_ssm_norm, ssm_w_out=v_ssm_w_out, w_out=v_w_out, ffn2_norm=v_ffn2_norm,
             ffn2_w_in=v_ffn2_w_in, ffn2_w_out=v_ffn2_w_out, final_norm=v_final_norm)
    shapes = {n: w[n].shape for n in _ORDER}

    def local(d, n):
        return d[n][0].T if n in _TRANSPOSED else d[n][0]

    shard = {n: local(w, n) for n in _SHARDED}

    wire = {n: (shard[n] if n in ("short_conv_w", "ssm_conv_w") else shard[n].astype(BF)) for n in _SHARDED}
    rep = {
        "ffn1_norm": ffn1_norm, "mix_norm": mix_norm, "ffn2_norm": ffn2_norm, "ssm_norm": ssm_norm,
        "ssm_conv_b": ssm_conv_b, "final_norm": final_norm.reshape(1, D_MODEL),
        "dt_bias_pad": _pad_lanes(ssm_dt_bias, DT_W), "a_log_pad": _pad_lanes(ssm_A_log, DT_W),
        "d_exp": jnp.repeat(ssm_D, HEAD_DIM, axis=1),
    }
    grad_x, parts, packed = _train_step(x[0], loss_target[0], wire, rep)

    out_g, out_d, out_m, out_v = {}, {}, {}, {}
    for n in _SHARDED:
        if n == "ssm_w_out":
            res, (small_parts,) = _adamw("adamw_" + n, parts[n], shard[n], local(m, n), local(v, n),
                                         comm=_gather_comm([packed]))
        else:
            res = _adamw("adamw_" + n, parts[n], shard[n], local(m, n), local(v, n))
        out_g[n], out_d[n], out_m[n], out_v[n] = [(r.T if n in _TRANSPOSED else r).reshape(shapes[n]) for r in res]
    row = lambda d: {n: d[n].reshape(1, -1) for n, _ in _SMALL}
    sres, loss_row = _adamw_small(small_parts, row(w), row(m), row(v))
    for n, _ in _SMALL:
        out_g[n], out_d[n], out_m[n], out_v[n] = [r.reshape(shapes[n]) for r in sres[n]]
    loss = loss_row[0, 0]
    return (loss, grad_x[None], *[out_g[n] for n in _ORDER], *[out_d[n] for n in _ORDER],
            *[out_m[n] for n in _ORDER], *[out_v[n] for n in _ORDER])
```

```python
import functools

import jax
import jax.numpy as jnp
from jax import lax
from jax.experimental import pallas as pl
from jax.experimental.pallas import tpu as pltpu

F32 = jnp.float32
BF = jnp.bfloat16

N_DEV = 8
D_MODEL = 1024
D_FF = 2816
D_INNER = 2048
D_XBC = 4096
N_HEADS = 32
HEAD_DIM = 64
N_GROUPS = 8
D_STATE = 128
CHUNK = 64
GROUP_W = D_INNER // N_GROUPS
HEADS_PER_GROUP = N_HEADS // N_GROUPS
NORM_EPS = 1e-5
N_IN = 11296
FF_SHARD = 2 * D_FF // N_DEV
FF_HALF = D_FF // 2
IN_SHARD = N_IN // N_DEV

OFF_B, OFF_C, OFF_XA, OFF_Z, OFF_XBC = 0, 1024, 2048, 3072, 5120
N_MAIN = 9216
OFF_DT = 2048
DT_W = 128
N_GD = 2048 + DT_W
W_GRAD_ROW_CUTS = [(0, 512), (512, 720), (720, 896), (896, 1152), (1152, 1412)]

ADAM_LR, ADAM_B1, ADAM_B2, ADAM_EPS, ADAM_WD, ADAM_STEP = 0.001, 0.9, 0.999, 1e-08, 0.01, 10

VMEM_LIMIT_V7X = 56 * 1024 * 1024
TM = 1024
TN_MAX_TOKENS = 2048
NT_MAX_OUT_TILE_BYTES = 12 * 1024 * 1024
TE = 512
ADAM_COL_TILE = 256
GATHER_PIECES = 4
GATHER_PIECE_MIN_ROWS = 512


def _params(*sem):
    return pltpu.CompilerParams(dimension_semantics=sem, vmem_limit_bytes=VMEM_LIMIT_V7X)


_DIMS = {
    "nn": (((1,), (0,)), ((), ())),
    "nt": (((1,), (1,)), ((), ())),
    "tn": (((0,), (0,)), ((), ())),
}


def _dot(a, b, mode="nn"):
    return lax.dot_general(a, b, _DIMS[mode], preferred_element_type=F32)


def _sigmoid(x):
    return 1.0 / (1.0 + jnp.exp(-x))


class _Comm:
    def __init__(self, inputs, out_shapes, sems, start, finish):
        self.inputs, self.out_shapes, self.sems, self.start, self.finish = inputs, out_shapes, sems, start, finish


def _pcall(name, body, grid, in_specs, out_specs, out_shape, args, scratch=(), sem=None, comm=None):
    single = not isinstance(out_shape, (list, tuple))
    out_shapes = [out_shape] if single else list(out_shape)
    out_specs = [out_specs] if single else list(out_specs)
    n_in, n_out, n_scr = len(args), len(out_shapes), len(scratch)
    if comm is None:
        res = pl.pallas_call(
            body, name=name, grid=grid, in_specs=list(in_specs), out_specs=out_specs, out_shape=out_shapes,
            scratch_shapes=list(scratch), compiler_params=_params(*sem))(*args)
        return (res[0] if single else res), []
    nci, nco = len(comm.inputs), len(comm.out_shapes)

    def wrapped(*refs):
        a = refs[:n_in]
        ci = refs[n_in:n_in + nci]
        o0 = n_in + nci
        o = refs[o0:o0 + n_out]
        co = refs[o0 + n_out:o0 + n_out + nco]
        s0 = o0 + n_out + nco
        s = refs[s0:s0 + n_scr]
        cs = refs[s0 + n_scr:]
        pids = [pl.program_id(i) for i in range(len(grid))]
        first = functools.reduce(jnp.logical_and, [p == 0 for p in pids])
        last = functools.reduce(jnp.logical_and, [p == g - 1 for p, g in zip(pids, grid)])

        @pl.when(first)
        def _():
            comm.start(ci, co, cs)

        body(*a, *o, *s)

        @pl.when(last)
        def _():
            comm.finish(ci, co, cs)

    any_spec = pl.BlockSpec(memory_space=pl.ANY)
    res = pl.pallas_call(
        wrapped, name=name, grid=grid, in_specs=list(in_specs) + [any_spec] * nci,
        out_specs=out_specs + [any_spec] * nco, out_shape=out_shapes + list(comm.out_shapes),
        scratch_shapes=list(scratch) + list(comm.sems),
        compiler_params=_params(*(("arbitrary",) * len(grid))))(*args, *comm.inputs)
    core = res[:n_out]
    return (core[0] if single else core), list(res[n_out:])


def _comm_call(name, comm):
    nci, nco = len(comm.inputs), len(comm.out_shapes)

    def body(*refs):
        ci, co, cs = refs[:nci], refs[nci:nci + nco], refs[nci + nco:]
        comm.start(ci, co, cs)
        comm.finish(ci, co, cs)

    any_spec = pl.BlockSpec(memory_space=pl.ANY)
    return pl.pallas_call(
        body, name=name, in_specs=[any_spec] * nci, out_specs=[any_spec] * nco, out_shape=list(comm.out_shapes),
        scratch_shapes=list(comm.sems), compiler_params=pltpu.CompilerParams(has_side_effects=True))(*comm.inputs)


def _remote(src, dst, ssem, rsem, dev):
    return pltpu.make_async_remote_copy(src_ref=src, dst_ref=dst, send_sem=ssem, recv_sem=rsem, device_id=dev,
                                        device_id_type=pl.DeviceIdType.MESH)


def _place():
    x, y, c = lax.axis_index("x"), lax.axis_index("y"), lax.axis_index("c")
    other_chips = [(1 - x, y), (x, 1 - y), (1 - x, 1 - y)]
    return x, y, c, other_chips


def _slot(x, y, c, swap):
    return 4 * y + 2 * x + c if swap else 4 * x + 2 * y + c


def _chip_slot(x, y, swap):
    return 2 * y + x if swap else 2 * x + y


def _gather_comm(shards, swaps=None):
    n = len(shards)
    per = N_DEV - 1
    swaps = [False] * n if swaps is None else swaps
    pieces = []
    for i, a in enumerate(shards):
        rows = a.shape[0]
        k = GATHER_PIECES if (a.ndim == 2 and rows >= GATHER_PIECE_MIN_ROWS) else 1
        step = -(-rows // (k * 8)) * 8
        if k == 1:
            pieces.append((i, 0, None))
        else:
            pieces += [(i, r, min(step, rows - r)) for r in range(0, rows, step)]
    m = len(pieces)

    def src(ins, v):
        i, r, cnt = pieces[v]
        return ins[i] if cnt is None else ins[i].at[pl.ds(r, cnt)]

    def place(outs, v, x, y, c):
        i, r, cnt = pieces[v]
        blk = outs[i].at[_slot(x, y, c, swaps[i])]
        return blk if cnt is None else blk.at[pl.ds(r, cnt)]

    def start(ins, outs, sems):
        send, recv, loc = sems
        x, y, c, chips = _place()
        for v in range(m):
            me = place(outs, v, x, y, c)
            pltpu.make_async_copy(src(ins, v), me, loc.at[v]).start()
            _remote(src(ins, v), me, send.at[per * v], recv.at[per * v], (x, y, 1 - c)).start()
        for j, (qx, qy) in enumerate(chips):
            for v in range(m):
                _remote(src(ins, v), place(outs, v, x, y, c), send.at[per * v + 1 + j], recv.at[per * v + 1 + j],
                        (qx, qy, c)).start()

    def finish(ins, outs, sems):
        send, recv, loc = sems
        x, y, c, chips = _place()
        sib = (x, y, 1 - c)
        for v in range(m):
            for j, (qx, qy) in enumerate(chips):
                blk = place(outs, v, qx, qy, c)
                _remote(blk, blk, send.at[per * v + 1 + j], recv.at[per * v + 1 + j], (qx, qy, c)).wait_recv()
                _remote(blk, blk, send.at[per * v + 4 + j], recv.at[per * v + 4 + j], sib).start()
        for v in range(m):
            blk = place(outs, v, x, y, 1 - c)
            _remote(blk, blk, send.at[per * v], recv.at[per * v], sib).wait_recv()
            for j, (qx, qy) in enumerate(chips):
                blk = place(outs, v, qx, qy, 1 - c)
                _remote(blk, blk, send.at[per * v + 4 + j], recv.at[per * v + 4 + j], sib).wait_recv()
        for v in range(m):
            own = place(outs, v, x, y, c)
            for k in range(per):
                _remote(src(ins, v), own, send.at[per * v + k], recv.at[per * v + k], sib).wait_send()
            pltpu.make_async_copy(src(ins, v), own, loc.at[v]).wait()

    out_shapes = [jax.ShapeDtypeStruct((N_DEV,) + tuple(a.shape), a.dtype) for a in shards]
    sems = [pltpu.SemaphoreType.DMA((per * m,)), pltpu.SemaphoreType.DMA((per * m,)), pltpu.SemaphoreType.DMA((m,))]
    return _Comm(list(shards), out_shapes, sems, start, finish)


def _pair_comm(slots):
    n = len(slots)

    def copies(ins, outs, sems):
        send, recv = sems
        x, y, c, _ = _place()
        sib = (x, y, 1 - c)
        out = []
        for i in range(n):
            for q in range(4):
                out.append(_remote(ins[i].at[2 * q + 1 - c], outs[i].at[q], send.at[4 * i + q], recv.at[4 * i + q], sib))
        return out

    def start(ins, outs, sems):
        for cp in copies(ins, outs, sems):
            cp.start()

    def finish(ins, outs, sems):
        for cp in copies(ins, outs, sems):
            cp.wait_send()
            cp.wait_recv()

    out_shapes = [jax.ShapeDtypeStruct((4,) + tuple(a.shape[1:]), a.dtype) for a in slots]
    sems = [pltpu.SemaphoreType.DMA((4 * n,)), pltpu.SemaphoreType.DMA((4 * n,))]
    return _Comm(list(slots), out_shapes, sems, start, finish)


def _chip_comm(chip_sums, swaps=None, rows=None):
    n = len(chip_sums)
    swaps = [False] * n if swaps is None else swaps
    rows = [None] * n if rows is None else rows

    def src(ins, i, q):
        return ins[i].at[q] if rows[i] is None else ins[i].at[q, pl.ds(rows[i][0], rows[i][1] - rows[i][0])]

    def start(ins, outs, sems):
        send, recv, loc = sems
        x, y, c, chips = _place()
        for i in range(n):
            mine = _chip_slot(x, y, swaps[i])
            pltpu.make_async_copy(src(ins, i, mine), outs[i].at[mine], loc.at[i]).start()
            for j, (qx, qy) in enumerate(chips):
                _remote(src(ins, i, _chip_slot(qx, qy, swaps[i])), outs[i].at[mine], send.at[3 * i + j],
                        recv.at[3 * i + j], (qx, qy, c)).start()

    def finish(ins, outs, sems):
        send, recv, loc = sems
        x, y, c, chips = _place()
        for i in range(n):
            mine = _chip_slot(x, y, swaps[i])
            for j, (qx, qy) in enumerate(chips):
                theirs = _chip_slot(qx, qy, swaps[i])
                cp = _remote(src(ins, i, theirs), outs[i].at[theirs], send.at[3 * i + j], recv.at[3 * i + j], (qx, qy, c))
                cp.wait_send()
                cp.wait_recv()
            pltpu.make_async_copy(src(ins, i, mine), outs[i].at[mine], loc.at[i]).wait()

    def out_shape(a, r):
        shape = a.shape if r is None else (a.shape[0], r[1] - r[0]) + tuple(a.shape[2:])
        return jax.ShapeDtypeStruct(shape, a.dtype)

    out_shapes = [out_shape(a, r) for a, r in zip(chip_sums, rows)]
    sems = [pltpu.SemaphoreType.DMA((3 * n,)), pltpu.SemaphoreType.DMA((3 * n,)), pltpu.SemaphoreType.DMA((n,))]
    return _Comm(list(chip_sums), out_shapes, sems, start, finish)


def _join_comm(a, b):
    na_i, na_o, na_s = len(a.inputs), len(a.out_shapes), len(a.sems)

    def start(ins, outs, sems):
        a.start(ins[:na_i], outs[:na_o], sems[:na_s])
        b.start(ins[na_i:], outs[na_o:], sems[na_s:])

    def finish(ins, outs, sems):
        a.finish(ins[:na_i], outs[:na_o], sems[:na_s])
        b.finish(ins[na_i:], outs[na_o:], sems[na_s:])

    return _Comm(a.inputs + b.inputs, a.out_shapes + b.out_shapes, a.sems + b.sems, start, finish)


def _row_tile(r):
    for cand in (256, 128):
        if r > cand and r % cand == 0:
            return cand
    return r


def _add_pairs(name, slots, sib):
    r, c = slots.shape[1:]
    tr = _row_tile(r)

    def body(core_ref, s_ref, b_ref, o_ref):
        o_ref[...] = (s_ref[...].astype(F32) + b_ref[...].astype(F32)).astype(o_ref.dtype)

    core = jnp.full((1,), lax.axis_index("c"), jnp.int32)
    return pl.pallas_call(
        body, name=name,
        grid_spec=pltpu.PrefetchScalarGridSpec(
            num_scalar_prefetch=1, grid=(4, r // tr),
            in_specs=[pl.BlockSpec((None, None, tr, c), lambda q, i, core_ref: (q, core_ref[0], i, 0)),
                      pl.BlockSpec((None, tr, c), lambda q, i, core_ref: (q, i, 0))],
            out_specs=pl.BlockSpec((None, tr, c), lambda q, i, core_ref: (q, i, 0))),
        out_shape=jax.ShapeDtypeStruct((4, r, c), slots.dtype),
        compiler_params=_params("parallel", "parallel"))(core, slots.reshape(4, 2, r, c), sib)


def _matmul(name, mode, a, b, grid, a_spec, b_spec, o_spec, out_shape, acc_shape,
            res=None, res_spec=None, alpha=1.0, comm=None):
    nk = grid[-1]
    has_res = res is not None

    def body(*refs):
        if has_res:
            a_ref, b_ref, r_ref, o_ref = refs[:4]
        else:
            a_ref, b_ref, o_ref = refs[:3]
            r_ref = None
        part = _dot(a_ref[...], b_ref[...], mode)

        def finish(v):
            if alpha != 1.0:
                v = v * alpha
            if has_res:
                v = r_ref[...] + v
            o_ref[...] = v.astype(o_ref.dtype)

        if nk == 1:
            finish(part)
        else:
            acc = refs[-1]
            k = pl.program_id(len(grid) - 1)

            @pl.when(k == 0)
            def _():
                acc[...] = part

            @pl.when(k > 0)
            def _():
                acc[...] += part

            @pl.when(k == nk - 1)
            def _():
                finish(acc[...])

    in_specs = [a_spec, b_spec] + ([res_spec] if has_res else [])
    args = (a, b) + ((res,) if has_res else ())
    scratch = [] if nk == 1 else [pltpu.VMEM(acc_shape, F32)]
    sem = ("parallel",) * (len(grid) - 1) + ("arbitrary",)
    out, couts = _pcall(name, body, grid, in_specs, o_spec, out_shape, args, scratch, sem, comm)
    return out if comm is None else (out, couts)


def _mm_nn(name, a, b, out_dtype=F32, res=None, alpha=1.0, tk=None, kk=None, a_off=0, b_off=0, comm=None):
    t = a.shape[0]
    kk = a.shape[1] if kk is None else kk
    n = b.shape[1]
    tk = kk if tk is None else tk
    grid = (t // TM, 1, kk // tk)
    return _matmul(
        name, "nn", a, b, grid,
        pl.BlockSpec((TM, tk), lambda i, j, k: (i, k + a_off)),
        pl.BlockSpec((tk, n), lambda i, j, k: (k + b_off, 0)),
        pl.BlockSpec((TM, n), lambda i, j, k: (i, 0)),
        jax.ShapeDtypeStruct((t, n), out_dtype), (TM, n),
        res=res, res_spec=pl.BlockSpec((TM, n), lambda i, j, k: (i, 0)), alpha=alpha, comm=comm)


def _mm_nt(name, a, b, n=None, tn=None, tk=None, out_dtype=F32, comm=None):
    t, kk = a.shape
    n = b.shape[0] if n is None else n
    tn = n if tn is None else tn
    tk = kk if tk is None else tk
    tt = t if t * tn * jnp.dtype(out_dtype).itemsize <= NT_MAX_OUT_TILE_BYTES else TM
    grid = (n // tn, t // tt, kk // tk)
    return _matmul(
        name, "nt", a, b, grid,
        pl.BlockSpec((tt, tk), lambda j, i, k: (i, k)),
        pl.BlockSpec((tn, tk), lambda j, i, k: (j, k)),
        pl.BlockSpec((tt, tn), lambda j, i, k: (i, j)),
        jax.ShapeDtypeStruct((t, n), out_dtype), (tt, tn), comm=comm)


def _mm_tn(name, a, b, out_dtype, tm=None, n=None, col_off=0, comm=None):
    t, m = a.shape
    n = b.shape[1] if n is None else n
    tm = m if tm is None else tm
    tk = t if t <= TN_MAX_TOKENS else TM
    grid = (m // tm, 1, t // tk)
    return _matmul(
        name, "tn", a, b, grid,
        pl.BlockSpec((tk, tm), lambda j, i, k: (k, j)),
        pl.BlockSpec((tk, n), lambda j, i, k: (k, col_off)),
        pl.BlockSpec((tm, n), lambda j, i, k: (j, 0)),
        jax.ShapeDtypeStruct((m, n), out_dtype), (tm, n), comm=comm)


def _rms_fwd(name, x, w):
    t, d = x.shape

    def body(x_ref, w_ref, h_ref):
        xv = x_ref[...]
        rstd = lax.rsqrt(jnp.mean(xv * xv, axis=-1, keepdims=True) + NORM_EPS)
        h_ref[...] = (xv * rstd * w_ref[...]).astype(h_ref.dtype)

    return pl.pallas_call(
        body, name=name, grid=(t // TE,),
        in_specs=[pl.BlockSpec((TE, d), lambda i: (i, 0)), pl.BlockSpec((1, d), lambda i: (0, 0))],
        out_specs=pl.BlockSpec((TE, d), lambda i: (i, 0)),
        out_shape=jax.ShapeDtypeStruct((t, d), BF), compiler_params=_params("parallel"))(x, w)


def _rms_bwd(name, x, w, dh, dres, out_scale, comm=None):
    t, d = x.shape

    def body(x_ref, w_ref, dh_ref, dres_ref, dx_ref, dxb_ref, dw_ref):
        i = pl.program_id(0)
        xv = x_ref[...]
        rstd = lax.rsqrt(jnp.mean(xv * xv, axis=-1, keepdims=True) + NORM_EPS)
        xhat = xv * rstd
        dhv = dh_ref[...]
        wd = dhv * w_ref[...]
        proj = jnp.mean(wd * xhat, axis=-1, keepdims=True)
        dx = dres_ref[...] + rstd * (wd - xhat * proj)
        dx_ref[...] = dx
        dxb_ref[...] = (dx * out_scale).astype(BF)
        part = jnp.sum(dhv * xhat, axis=0, keepdims=True)

        @pl.when(i == 0)
        def _():
            dw_ref[...] = part

        @pl.when(i > 0)
        def _():
            dw_ref[...] += part

    row = pl.BlockSpec((TE, d), lambda i: (i, 0))
    vec = pl.BlockSpec((1, d), lambda i: (0, 0))
    outs, couts = _pcall(
        name, body, (t // TE,), [row, vec, row, row], [row, row, vec],
        [jax.ShapeDtypeStruct((t, d), F32), jax.ShapeDtypeStruct((t, d), BF), jax.ShapeDtypeStruct((1, d), F32)],
        (x, w, dh, dres), (), ("arbitrary",), comm)
    return outs if comm is None else (outs, couts)


def _final_loss(x, w, target):
    t, d = x.shape

    def body(x_ref, w_ref, t_ref, loss_ref, dx_ref, dxb_ref, dw_ref):
        i = pl.program_id(0)
        xv = x_ref[...]
        rstd = lax.rsqrt(jnp.mean(xv * xv, axis=-1, keepdims=True) + NORM_EPS)
        xhat = xv * rstd
        err = xhat * w_ref[...] - t_ref[...]
        lpart = 0.5 * jnp.sum(jnp.mean(err * err, axis=-1, keepdims=True), axis=0, keepdims=True)
        dy = err * (1.0 / d)
        wd = dy * w_ref[...]
        proj = jnp.mean(wd * xhat, axis=-1, keepdims=True)
        dx = rstd * (wd - xhat * proj)
        dx_ref[...] = dx
        dxb_ref[...] = (0.5 * dx).astype(BF)
        part = jnp.sum(dy * xhat, axis=0, keepdims=True)
        lfull = jnp.broadcast_to(lpart, (1, 128))

        @pl.when(i == 0)
        def _():
            dw_ref[...] = part
            loss_ref[...] = lfull

        @pl.when(i > 0)
        def _():
            dw_ref[...] += part
            loss_ref[...] += lfull

    row = pl.BlockSpec((TE, d), lambda i: (i, 0))
    vec = pl.BlockSpec((1, d), lambda i: (0, 0))
    return pl.pallas_call(
        body, name="final_loss", grid=(t // TE,), in_specs=[row, vec, row],
        out_specs=[pl.BlockSpec((1, 128), lambda i: (0, 0)), row, row, vec],
        out_shape=[jax.ShapeDtypeStruct((1, 128), F32), jax.ShapeDtypeStruct((t, d), F32),
                   jax.ShapeDtypeStruct((t, d), BF), jax.ShapeDtypeStruct((1, d), F32)],
        compiler_params=_params("arbitrary"))(x, w, target)


def _swiglu_fwd(name, gu, comm=None):
    t = gu.shape[0]

    def body(g_ref, u_ref, a_ref):
        g = g_ref[...].astype(F32)
        a_ref[...] = (g * _sigmoid(g) * u_ref[...].astype(F32)).astype(BF)

    blk = (TE, FF_HALF)
    out, couts = _pcall(
        name, body, (t // TE, 2),
        [pl.BlockSpec(blk, lambda i, j: (i, 2 * j)), pl.BlockSpec(blk, lambda i, j: (i, 2 * j + 1))],
        pl.BlockSpec(blk, lambda i, j: (i, j)), jax.ShapeDtypeStruct((t, D_FF), BF),
        (gu, gu), (), ("parallel", "parallel"), comm)
    return out if comm is None else (out, couts)


def _swiglu_bwd(name, gu, dact, comm=None):
    t = gu.shape[0]

    def body(g_ref, u_ref, da_ref, o_ref):
        g = g_ref[...].astype(F32)
        da = da_ref[...].astype(F32)
        s = _sigmoid(g)
        o_ref[:, 0:FF_HALF] = (da * u_ref[...].astype(F32) * (s * (1.0 + g * (1.0 - s)))).astype(BF)
        o_ref[:, FF_HALF:2 * FF_HALF] = (da * g * s).astype(BF)

    blk = (TE, FF_HALF)
    out, couts = _pcall(
        name, body, (t // TE, 2),
        [pl.BlockSpec(blk, lambda i, j: (i, 2 * j)), pl.BlockSpec(blk, lambda i, j: (i, 2 * j + 1)),
         pl.BlockSpec(blk, lambda i, j: (i, j))],
        pl.BlockSpec((TE, 2 * FF_HALF), lambda i, j: (i, j)),
        jax.ShapeDtypeStruct((t, 2 * D_FF), BF), (gu, gu, dact), (), ("parallel", "parallel"), comm)
    return out if comm is None else (out, couts)


CONV_CB = 256


CONV_ROWS = 64
CONV_HALO = 16


def _taps_down(ext, w, k):
    shifted = [pltpu.roll(ext, k - 1 - j, 0)[CONV_HALO:] for j in range(k - 1)] + [ext[CONV_HALO:]]
    out = shifted[k - 1] * w[k - 1:k, :]
    for j in range(k - 1):
        out = out + shifted[j] * w[j:j + 1, :]
    return out, shifted


def _taps_up(ext, w, k):
    rows = ext.shape[0]
    n = rows - CONV_HALO
    out = ext[:n] * w[k - 1:k, :]
    for j in range(k - 1):
        out = out + pltpu.roll(ext, rows - (k - 1 - j), 0)[:n] * w[j:j + 1, :]
    return out


def _rows_before(ref, i, r0):
    start = pl.multiple_of(jnp.maximum(r0 - CONV_HALO, 0), CONV_HALO)
    return jnp.where(i > 0, ref[pl.ds(start, CONV_HALO), :].astype(F32), 0.0)


def _rows_after(ref, r0, t):
    start = pl.multiple_of(jnp.minimum(r0 + CONV_ROWS, t - CONV_HALO), CONV_HALO)
    return ref[pl.ds(start, CONV_HALO), :].astype(F32)


def _fold8(v):
    return v.reshape(v.shape[0] // 8, 8, v.shape[1]).sum(axis=0)


def _silu_grad(pre):
    s = _sigmoid(pre)
    return s * (1.0 + pre * (1.0 - s))


def _pspec(t, off):
    base = off // CONV_CB
    return pl.BlockSpec((t, CONV_CB), lambda j: (0, base + j))


def _mix_a_fwd(p, conv_w):
    t = p.shape[0]

    def body(b_ref, c_ref, xa_ref, w_ref, o_ref):
        w = w_ref[...]

        def step(i, carry):
            r0 = pl.multiple_of(i * CONV_ROWS, CONV_ROWS)
            rows = pl.ds(r0, CONV_ROWS)
            q = c_ref[rows, :].astype(F32) * xa_ref[rows, :].astype(F32)
            q_before = _rows_before(c_ref, i, r0) * _rows_before(xa_ref, i, r0)
            va, _ = _taps_down(jnp.concatenate([q_before, q], axis=0), w, 3)
            o_ref[rows, :] = (b_ref[rows, :].astype(F32) * va).astype(BF)
            return carry

        lax.fori_loop(0, t // CONV_ROWS, step, 0)

    return pl.pallas_call(
        body, name="mix_a_fwd", grid=(D_MODEL // CONV_CB,),
        in_specs=[_pspec(t, OFF_B), _pspec(t, OFF_C), _pspec(t, OFF_XA),
                  pl.BlockSpec((3, CONV_CB), lambda j: (0, j))],
        out_specs=pl.BlockSpec((t, CONV_CB), lambda j: (0, j)),
        out_shape=jax.ShapeDtypeStruct((t, D_MODEL), BF), compiler_params=_params("parallel"))(p, p, p, conv_w)


def _mix_a_bwd(p, conv_w, dya, dp):
    t = p.shape[0]

    def body(b_ref, c_ref, xa_ref, w_ref, dy_ref, dp_in, dp_ref, dw_ref):
        del dp_in
        w = w_ref[...]
        n = t // CONV_ROWS

        def step(i, acc):
            r0 = pl.multiple_of(i * CONV_ROWS, CONV_ROWS)
            rows = pl.ds(r0, CONV_ROWS)
            cv = c_ref[rows, :].astype(F32)
            xav = xa_ref[rows, :].astype(F32)
            q_before = _rows_before(c_ref, i, r0) * _rows_before(xa_ref, i, r0)
            va, shifted = _taps_down(jnp.concatenate([q_before, cv * xav], axis=0), w, 3)
            dyv = dy_ref[rows, :]
            dp_ref[rows, 0:CONV_CB] = (dyv * va).astype(BF)
            dv = dyv * b_ref[rows, :].astype(F32)
            dv_after = jnp.where(i < n - 1, _rows_after(dy_ref, r0, t) * _rows_after(b_ref, r0, t), 0.0)
            dq = _taps_up(jnp.concatenate([dv, dv_after], axis=0), w, 3)
            dp_ref[rows, CONV_CB:2 * CONV_CB] = (dq * xav).astype(BF)
            dp_ref[rows, 2 * CONV_CB:3 * CONV_CB] = (dq * cv).astype(BF)
            return tuple(a + _fold8(dv * s) for a, s in zip(acc, shifted))

        zero = jnp.zeros((8, CONV_CB), F32)
        acc = lax.fori_loop(0, n, step, (zero, zero, zero))
        for j in range(3):
            dw_ref[j:j + 1, :] = jnp.sum(acc[j], axis=0, keepdims=True)

    col = pl.BlockSpec((t, CONV_CB), lambda j: (0, j))
    wsp = pl.BlockSpec((3, CONV_CB), lambda j: (0, j))
    return pl.pallas_call(
        body, name="mix_a_bwd", grid=(D_MODEL // CONV_CB,),
        in_specs=[_pspec(t, OFF_B), _pspec(t, OFF_C), _pspec(t, OFF_XA), wsp, col, pl.BlockSpec(memory_space=pl.ANY)],
        out_specs=[pl.BlockSpec((t, 3 * CONV_CB), lambda j: (0, j)), wsp],
        out_shape=[jax.ShapeDtypeStruct(dp.shape, dp.dtype), jax.ShapeDtypeStruct((3, D_MODEL), F32)],
        input_output_aliases={5: 0},
        compiler_params=_params("parallel"))(p, p, p, conv_w, dya, dp)


def _ssm_conv_fwd(p, conv_w, conv_b, comm=None):
    t = p.shape[0]

    def body(x_ref, w_ref, b_ref, o_ref):
        w = w_ref[...]
        bias = b_ref[...]

        def step(i, carry):
            r0 = pl.multiple_of(i * CONV_ROWS, CONV_ROWS)
            rows = pl.ds(r0, CONV_ROWS)
            ext = jnp.concatenate([_rows_before(x_ref, i, r0), x_ref[rows, :].astype(F32)], axis=0)
            pre = _taps_down(ext, w, 4)[0] + bias
            o_ref[rows, :] = pre * _sigmoid(pre)
            return carry

        lax.fori_loop(0, t // CONV_ROWS, step, 0)

    out, couts = _pcall(
        "ssm_conv_fwd", body, (D_XBC // CONV_CB,),
        [_pspec(t, OFF_XBC), pl.BlockSpec((4, CONV_CB), lambda j: (0, j)), pl.BlockSpec((1, CONV_CB), lambda j: (0, j))],
        pl.BlockSpec((t, CONV_CB), lambda j: (0, j)), jax.ShapeDtypeStruct((t, D_XBC), F32),
        (p, conv_w, conv_b), (), ("parallel",), comm)
    return out if comm is None else (out, couts)


def _ssm_conv_bwd(p, conv_w, conv_b, dxc, dp):
    t = p.shape[0]

    def body(x_ref, w_ref, b_ref, d_ref, dp_in, dx_ref, dw_ref, db_ref):
        del dp_in
        w = w_ref[...]
        bias = b_ref[...]
        n = t // CONV_ROWS

        def step(i, acc):
            r0 = pl.multiple_of(i * CONV_ROWS, CONV_ROWS)
            rows = pl.ds(r0, CONV_ROWS)
            x_cur = x_ref[rows, :].astype(F32)
            pre, shifted = _taps_down(jnp.concatenate([_rows_before(x_ref, i, r0), x_cur], axis=0), w, 4)
            pre = pre + bias
            dpre = d_ref[rows, :] * _silu_grad(pre)
            ext_after = jnp.concatenate([x_cur[CONV_ROWS - CONV_HALO:], _rows_after(x_ref, r0, t)], axis=0)
            pre_after = _taps_down(ext_after, w, 4)[0] + bias
            dpre_after = jnp.where(i < n - 1, _rows_after(d_ref, r0, t) * _silu_grad(pre_after), 0.0)
            dx_ref[rows, :] = _taps_up(jnp.concatenate([dpre, dpre_after], axis=0), w, 4).astype(BF)
            new = tuple(a + _fold8(dpre * s) for a, s in zip(acc[:4], shifted))
            return new + (acc[4] + _fold8(dpre),)

        zero = jnp.zeros((8, CONV_CB), F32)
        acc = lax.fori_loop(0, n, step, (zero,) * 5)
        for j in range(4):
            dw_ref[j:j + 1, :] = jnp.sum(acc[j], axis=0, keepdims=True)
        db_ref[...] = jnp.sum(acc[4], axis=0, keepdims=True)

    col = pl.BlockSpec((t, CONV_CB), lambda j: (0, j))
    wsp = pl.BlockSpec((4, CONV_CB), lambda j: (0, j))
    bsp = pl.BlockSpec((1, CONV_CB), lambda j: (0, j))
    return pl.pallas_call(
        body, name="ssm_conv_bwd", grid=(D_XBC // CONV_CB,),
        in_specs=[_pspec(t, OFF_XBC), wsp, bsp, col, pl.BlockSpec(memory_space=pl.ANY)],
        out_specs=[_pspec(t, OFF_XBC), wsp, bsp],
        out_shape=[jax.ShapeDtypeStruct(dp.shape, dp.dtype), jax.ShapeDtypeStruct((4, D_XBC), F32),
                   jax.ShapeDtypeStruct((1, D_XBC), F32)],
        input_output_aliases={4: 0},
        compiler_params=_params("parallel"))(p, conv_w, conv_b, dxc, dp)


DT_ROWS = 512


def _tri(lower):
    r = lax.broadcasted_iota(jnp.int32, (CHUNK, CHUNK), 0)
    c = lax.broadcasted_iota(jnp.int32, (CHUNK, CHUNK), 1)
    return jnp.where((r >= c) if lower else (r <= c), 1.0, 0.0).astype(F32)


def _dot_exact(a, b):
    return lax.dot_general(a, b, _DIMS["nn"], preferred_element_type=F32, precision=lax.Precision.HIGHEST)


def _dt_fwd(p, bias_pad, alog_pad):
    t = p.shape[0]

    def body(raw_ref, b_ref, al_ref, dt_ref, acs_ref):
        z = raw_ref[...] + b_ref[...]
        dt = jnp.maximum(z, 0.0) + jnp.log(1.0 + jnp.exp(-jnp.abs(z)))
        dt_ref[...] = dt
        a = dt * (-jnp.exp(al_ref[...]))
        tri = _tri(True)
        for k in range(DT_ROWS // CHUNK):
            acs_ref[k * CHUNK:(k + 1) * CHUNK, :] = _dot_exact(tri, a[k * CHUNK:(k + 1) * CHUNK, :])

    blk = pl.BlockSpec((DT_ROWS, DT_W), lambda i: (i, 0))
    vec = pl.BlockSpec((1, DT_W), lambda i: (0, 0))
    return pl.pallas_call(
        body, name="dt_fwd", grid=(t // DT_ROWS,),
        in_specs=[pl.BlockSpec((DT_ROWS, DT_W), lambda i: (i, OFF_DT // DT_W)), vec, vec],
        out_specs=[blk, blk], out_shape=[jax.ShapeDtypeStruct((t, DT_W), F32)] * 2,
        compiler_params=_params("parallel"))(p, bias_pad, alog_pad)


def _dt_bwd(p, bias_pad, alog_pad, dt, ddt, dacs, dp_gd):
    t = p.shape[0]

    def body(raw_ref, b_ref, al_ref, dt_ref, ddt_ref, dacs_ref, dp_in, draw_ref, db_ref, dal_ref):
        del dp_in
        i = pl.program_id(0)
        acoef = -jnp.exp(al_ref[...])
        triu = _tri(False)
        das = []
        for k in range(DT_ROWS // CHUNK):
            das.append(_dot_exact(triu, dacs_ref[k * CHUNK:(k + 1) * CHUNK, :]))
        da = jnp.concatenate(das, axis=0)
        dtv = dt_ref[...]
        ddt_tot = ddt_ref[...] + da * acoef
        lane = lax.broadcasted_iota(jnp.int32, (DT_ROWS, DT_W), 1)
        draw = jnp.where(lane < N_HEADS, ddt_tot * _sigmoid(raw_ref[...] + b_ref[...]), 0.0)
        draw_ref[...] = draw.astype(BF)
        pb = jnp.sum(draw, axis=0, keepdims=True)
        pa = jnp.sum(da * dtv * acoef, axis=0, keepdims=True)

        @pl.when(i == 0)
        def _():
            db_ref[...] = pb
            dal_ref[...] = pa

        @pl.when(i > 0)
        def _():
            db_ref[...] += pb
            dal_ref[...] += pa

    blk = pl.BlockSpec((DT_ROWS, DT_W), lambda i: (i, 0))
    vec = pl.BlockSpec((1, DT_W), lambda i: (0, 0))
    return pl.pallas_call(
        body, name="dt_bwd", grid=(t // DT_ROWS,),
        in_specs=[pl.BlockSpec((DT_ROWS, DT_W), lambda i: (i, OFF_DT // DT_W)), vec, vec, blk, blk, blk,
                  pl.BlockSpec(memory_space=pl.ANY)],
        out_specs=[pl.BlockSpec((DT_ROWS, DT_W), lambda i: (i, OFF_DT // DT_W)), vec, vec],
        out_shape=[jax.ShapeDtypeStruct(dp_gd.shape, dp_gd.dtype), jax.ShapeDtypeStruct((1, DT_W), F32),
                   jax.ShapeDtypeStruct((1, DT_W), F32)],
        input_output_aliases={6: 0},
        compiler_params=_params("arbitrary"))(p, bias_pad, alog_pad, dt, ddt, dacs, dp_gd)


def _split_dot(z, onehot, terms):
    out = None
    rest = z
    for _ in range(terms):
        piece = rest.astype(BF)
        part = _dot(piece, onehot)
        out = part if out is None else out + part
        rest = rest - piece.astype(F32)
    return out


def _spread_mat(g):
    row = lax.broadcasted_iota(jnp.int32, (DT_W, GROUP_W), 0)
    lane = lax.broadcasted_iota(jnp.int32, (DT_W, GROUP_W), 1)
    return jnp.where(row == HEADS_PER_GROUP * g + lane // HEAD_DIM, 1.0, 0.0).astype(BF)


def _gather_mat(g):
    row = lax.broadcasted_iota(jnp.int32, (GROUP_W, DT_W), 0)
    lane = lax.broadcasted_iota(jnp.int32, (GROUP_W, DT_W), 1)
    return jnp.where(lane == HEADS_PER_GROUP * g + row // HEAD_DIM, 1.0, 0.0).astype(BF)


def _ssd_masks():
    row = lax.broadcasted_iota(jnp.int32, (CHUNK, GROUP_W), 0)
    col = lax.broadcasted_iota(jnp.int32, (CHUNK, GROUP_W), 1) % HEAD_DIM
    brow = lax.broadcasted_iota(jnp.int32, (GROUP_W, GROUP_W), 0) // HEAD_DIM
    bcol = lax.broadcasted_iota(jnp.int32, (GROUP_W, GROUP_W), 1) // HEAD_DIM
    return row >= col, row == col, brow == bcol


def _stack4(v):
    return jnp.concatenate([v, v, v, v], axis=0)


def _fold4(v):
    return v[0:CHUNK] + v[CHUNK:2 * CHUNK] + v[2 * CHUNK:3 * CHUNK] + v[3 * CHUNK:4 * CHUNK]


def _ssd_group(xc_ref, stacked, g, tri, eye, blockdiag):
    gs = slice(GROUP_W * g, GROUP_W * (g + 1))
    xs_g = xc_ref[:, gs]
    b_g = xc_ref[:, D_INNER + D_STATE * g:D_INNER + D_STATE * (g + 1)].astype(BF)
    c_g = xc_ref[:, D_INNER + 1024 + D_STATE * g:D_INNER + 1024 + D_STATE * (g + 1)].astype(BF)
    wide = _split_dot(stacked, _spread_mat(g), 3)
    acs_e, dt_e = wide[0:CHUNK], wide[CHUNK:2 * CHUNK]
    atot_e = acs_e[CHUNK - 1:CHUNK, :]
    acs_j = jnp.sum(jnp.where(eye, acs_e, 0.0), axis=0, keepdims=True)
    lmat = jnp.where(tri, jnp.exp(jnp.minimum(acs_e - acs_j, 0.0)), 0.0)
    b_t = _stack4(b_g)
    m = _dot(c_g, b_t, "nt") * lmat
    x_g = xs_g * dt_e
    xbd = jnp.where(blockdiag, _stack4(x_g), 0.0).astype(BF)
    return dict(gs=gs, xs=xs_g, b=b_g, c=c_g, b_t=b_t, dt=dt_e, e=jnp.exp(acs_e), dec=jnp.exp(atot_e - acs_e),
                eat=jnp.exp(atot_e), lmat=lmat, m=m, x=x_g, xbd=xbd)


def _ssd_fwd(xconv, dt, acs, d_exp, comm=None):
    t = xconv.shape[0]
    nc = t // CHUNK

    def body(xc_ref, dt_ref, acs_ref, d_ref, y_ref, hs_ref, state):
        c = pl.program_id(0)

        @pl.when(c == 0)
        def _():
            state[...] = jnp.zeros_like(state)

        hs_ref[...] = state[...]
        tri, eye, blockdiag = _ssd_masks()
        stacked = jnp.concatenate([acs_ref[...], dt_ref[...]], axis=0)
        for g in range(N_GROUPS):
            q = _ssd_group(xc_ref, stacked, g, tri, eye, blockdiag)
            gs = q["gs"]
            h_t = state[:, gs]
            ydiag = _dot(q["m"].astype(BF), q["xbd"])
            yoff = _dot(q["c"], h_t.astype(BF)) * q["e"]
            y_ref[:, gs] = ydiag + yoff + d_ref[:, gs] * q["xs"]
            s_t = _dot(q["b"], (q["x"] * q["dec"]).astype(BF), "tn")
            state[:, gs] = q["eat"] * h_t + s_t

    blk = lambda w: pl.BlockSpec((CHUNK, w), lambda c: (c, 0))
    outs, couts = _pcall(
        "ssd_fwd", body, (nc,),
        [blk(D_XBC), blk(DT_W), blk(DT_W), pl.BlockSpec((1, D_INNER), lambda c: (0, 0))],
        [blk(D_INNER), pl.BlockSpec((None, D_STATE, D_INNER), lambda c: (c, 0, 0))],
        [jax.ShapeDtypeStruct((t, D_INNER), F32), jax.ShapeDtypeStruct((nc, D_STATE, D_INNER), F32)],
        (xconv, dt, acs, d_exp), [pltpu.VMEM((D_STATE, D_INNER), F32)], ("arbitrary",), comm)
    return outs if comm is None else (outs, couts)


def _ssd_bwd(xconv, dt, acs, d_exp, hsave, dy, comm=None):
    t = xconv.shape[0]
    nc = t // CHUNK

    def body(xc_ref, dt_ref, acs_ref, d_ref, hs_ref, dy_ref, dxc_ref, ddt_ref, dacs_ref, dd_ref, dstate):
        c = pl.program_id(0)

        @pl.when(c == 0)
        def _():
            dstate[...] = jnp.zeros_like(dstate)
            dd_ref[...] = jnp.zeros_like(dd_ref)

        tri, eye, blockdiag = _ssd_masks()
        acsv = acs_ref[...]
        stacked = jnp.concatenate([acsv, dt_ref[...]], axis=0)
        eat_heads = jnp.exp(acsv[CHUNK - 1:CHUNK, :])
        ddt_acc = jnp.zeros((CHUNK, DT_W), F32)
        dacs_acc = jnp.zeros((CHUNK, DT_W), F32)
        datot_acc = jnp.zeros((1, DT_W), F32)

        for g in range(N_GROUPS):
            q = _ssd_group(xc_ref, stacked, g, tri, eye, blockdiag)
            gs, xs_g, b_g, c_g, m = q["gs"], q["xs"], q["b"], q["c"], q["m"]
            bs = slice(D_INNER + D_STATE * g, D_INNER + D_STATE * (g + 1))
            cs = slice(D_INNER + 1024 + D_STATE * g, D_INNER + 1024 + D_STATE * (g + 1))
            h_t = hs_ref[:, gs]
            h_b = h_t.astype(BF)
            dy_g = dy_ref[:, gs]
            dy_b = dy_g.astype(BF)
            ds_t = dstate[:, gs]
            ds_b = ds_t.astype(BF)

            yoff = _dot(c_g, h_b) * q["e"]
            edy = (q["e"] * dy_g).astype(BF)
            d_c = _dot(edy, h_b, "nt")
            d_ht = _dot(c_g, edy, "tn")
            bds = _dot(b_g, ds_b)
            xd = q["x"] * q["dec"]
            d_b = _dot(xd.astype(BF), ds_b, "nt")
            dm = _dot(dy_b, q["xbd"], "nt")
            cross = _dot(m.astype(BF), dy_b, "tn")
            dx_full = q["dec"] * bds + _fold4(jnp.where(blockdiag, cross, 0.0))
            dml = (dm * q["lmat"]).astype(BF)
            d_c = d_c + _dot(dml, q["b_t"])
            d_b = d_b + _fold4(_dot(dml, c_g, "tn"))
            w = dm * m
            q_dec = xd * bds
            z = w - jnp.where(eye, jnp.sum(w, axis=0, keepdims=True), 0.0) + dy_g * yoff - q_dec
            rows = jnp.concatenate(
                [jnp.sum(q_dec, axis=0, keepdims=True), jnp.sum(ds_t * h_t, axis=0, keepdims=True),
                 jnp.zeros((6, GROUP_W), F32)], axis=0)
            seg = _split_dot(jnp.concatenate([z, dx_full * xs_g, rows], axis=0), _gather_mat(g), 2)
            dacs_acc = dacs_acc + seg[0:CHUNK]
            ddt_acc = ddt_acc + seg[CHUNK:2 * CHUNK]
            datot_acc = datot_acc + seg[2 * CHUNK:2 * CHUNK + 1] + eat_heads * seg[2 * CHUNK + 1:2 * CHUNK + 2]
            dxc_ref[:, cs] = d_c
            dxc_ref[:, bs] = d_b
            dxc_ref[:, gs] = dx_full * q["dt"] + d_ref[:, gs] * dy_g
            dd_ref[:, gs] += jnp.sum(dy_g * xs_g, axis=0, keepdims=True)
            dstate[:, gs] = q["eat"] * ds_t + d_ht

        rowi = lax.broadcasted_iota(jnp.int32, (CHUNK, DT_W), 0)
        ddt_ref[...] = ddt_acc
        dacs_ref[...] = dacs_acc + jnp.where(rowi == CHUNK - 1, datot_acc, 0.0)

    rev = lambda w: pl.BlockSpec((CHUNK, w), lambda c: (nc - 1 - c, 0))
    vec = pl.BlockSpec((1, D_INNER), lambda c: (0, 0))
    outs, couts = _pcall(
        "ssd_bwd", body, (nc,),
        [rev(D_XBC), rev(DT_W), rev(DT_W), vec,
         pl.BlockSpec((None, D_STATE, D_INNER), lambda c: (nc - 1 - c, 0, 0)), rev(D_INNER)],
        [rev(D_XBC), rev(DT_W), rev(DT_W), vec],
        [jax.ShapeDtypeStruct((t, D_XBC), F32), jax.ShapeDtypeStruct((t, DT_W), F32),
         jax.ShapeDtypeStruct((t, DT_W), F32), jax.ShapeDtypeStruct((1, D_INNER), F32)],
        (xconv, dt, acs, d_exp, hsave, dy),
        [pltpu.VMEM((D_STATE, D_INNER), F32)], ("arbitrary",), comm)
    return outs if comm is None else (outs, couts)


GN_CB = 1024
GN_GROUPS = GN_CB // GROUP_W


def _gnorm_fwd(y, p, w, comm=None):
    t = y.shape[0]
    zoff = OFF_Z // GN_CB

    def body(y_ref, z_ref, w_ref, o_ref):
        for g in range(GN_GROUPS):
            gs = slice(GROUP_W * g, GROUP_W * (g + 1))
            z = z_ref[:, gs].astype(F32)
            yf = y_ref[:, gs] * (z * _sigmoid(z))
            rstd = lax.rsqrt(jnp.mean(yf * yf, axis=-1, keepdims=True) + NORM_EPS)
            o_ref[:, gs] = (yf * rstd * w_ref[:, gs]).astype(BF)

    blk = pl.BlockSpec((TE, GN_CB), lambda i, j: (i, j))
    out, couts = _pcall(
        "gnorm_fwd", body, (t // TE, D_INNER // GN_CB),
        [blk, pl.BlockSpec((TE, GN_CB), lambda i, j: (i, zoff + j)), pl.BlockSpec((1, GN_CB), lambda i, j: (0, j))],
        blk, jax.ShapeDtypeStruct((t, D_INNER), BF), (y, p, w), (), ("parallel", "parallel"), comm)
    return out if comm is None else (out, couts)


def _gnorm_bwd(y, p, w, dyn, comm=None):
    t = y.shape[0]
    zoff = OFF_Z // GN_CB

    def body(y_ref, z_ref, w_ref, dn_ref, dy_ref, dz_ref, dw_ref):
        i = pl.program_id(1)
        for g in range(GN_GROUPS):
            gs = slice(GROUP_W * g, GROUP_W * (g + 1))
            z = z_ref[:, gs].astype(F32)
            yv = y_ref[:, gs]
            s = _sigmoid(z)
            sil = z * s
            yf = yv * sil
            rstd = lax.rsqrt(jnp.mean(yf * yf, axis=-1, keepdims=True) + NORM_EPS)
            xhat = yf * rstd
            dn = dn_ref[:, gs]
            wd = dn * w_ref[:, gs]
            proj = jnp.mean(wd * xhat, axis=-1, keepdims=True)
            dyf = rstd * (wd - xhat * proj)
            dy_ref[:, gs] = dyf * sil
            dz_ref[:, gs] = (dyf * yv * (s * (1.0 + z * (1.0 - s)))).astype(BF)
            part = jnp.sum(dn * xhat, axis=0, keepdims=True)

            @pl.when(i == 0)
            def _():
                dw_ref[:, gs] = part

            @pl.when(i > 0)
            def _():
                dw_ref[:, gs] += part

    blk = pl.BlockSpec((TE, GN_CB), lambda j, i: (i, j))
    vec = pl.BlockSpec((1, GN_CB), lambda j, i: (0, j))
    outs, couts = _pcall(
        "gnorm_bwd", body, (D_INNER // GN_CB, t // TE),
        [blk, pl.BlockSpec((TE, GN_CB), lambda j, i: (i, zoff + j)), vec, blk],
        [blk, pl.BlockSpec((TE, GN_CB), lambda j, i: (i, zoff + j)), vec],
        [jax.ShapeDtypeStruct((t, D_INNER), F32), jax.ShapeDtypeStruct((t, N_MAIN), BF),
         jax.ShapeDtypeStruct((1, D_INNER), F32)],
        (y, p, w, dyn), (), ("parallel", "arbitrary"), comm)
    return outs if comm is None else (outs, couts)


MERGE_CB = 512


def _merge_fwd(p, ya, yb):
    t = ya.shape[0]

    def body(ga_ref, gb_ref, ya_ref, yb_ref, o_ref):
        o_ref[...] = (_sigmoid(ga_ref[...]) * ya_ref[...] + _sigmoid(gb_ref[...]) * yb_ref[...]).astype(BF)

    blk = pl.BlockSpec((TE, MERGE_CB), lambda i, j: (i, j))
    return pl.pallas_call(
        body, name="merge_fwd", grid=(t // TE, D_MODEL // MERGE_CB),
        in_specs=[pl.BlockSpec((TE, MERGE_CB), lambda i, j: (i, 2 * j)),
                  pl.BlockSpec((TE, MERGE_CB), lambda i, j: (i, 2 * j + 1)), blk, blk],
        out_specs=blk, out_shape=jax.ShapeDtypeStruct((t, D_MODEL), BF),
        compiler_params=_params("parallel", "parallel"))(p, p, ya, yb)


def _merge_bwd(p, ya, yb, dm):
    t = ya.shape[0]

    def body(ga_ref, gb_ref, ya_ref, yb_ref, dm_ref, dg_ref, dya_ref, dyb_ref):
        d = dm_ref[...]
        sa = _sigmoid(ga_ref[...])
        sb = _sigmoid(gb_ref[...])
        dg_ref[:, 0:MERGE_CB] = (d * ya_ref[...] * sa * (1.0 - sa)).astype(BF)
        dg_ref[:, MERGE_CB:2 * MERGE_CB] = (d * yb_ref[...] * sb * (1.0 - sb)).astype(BF)
        dya_ref[...] = (d * sa).astype(BF)
        dyb_ref[...] = (d * sb).astype(BF)

    blk = pl.BlockSpec((TE, MERGE_CB), lambda i, j: (i, j))
    return pl.pallas_call(
        body, name="merge_bwd", grid=(t // TE, D_MODEL // MERGE_CB),
        in_specs=[pl.BlockSpec((TE, MERGE_CB), lambda i, j: (i, 2 * j)),
                  pl.BlockSpec((TE, MERGE_CB), lambda i, j: (i, 2 * j + 1)), blk, blk, blk],
        out_specs=[pl.BlockSpec((TE, 2 * MERGE_CB), lambda i, j: (i, j)), blk, blk],
        out_shape=[jax.ShapeDtypeStruct((t, N_GD), BF)] + [jax.ShapeDtypeStruct((t, D_MODEL), BF)] * 2,
        compiler_params=_params("parallel", "parallel"))(p, p, ya, yb, dm)


def _adamw(name, parts, w, m, v, comm=None):
    r, c = w.shape
    tr = _row_tile(r)
    tc = ADAM_COL_TILE if (tr == r and r > 512 and c % ADAM_COL_TILE == 0) else c
    n_parts = parts.shape[0]
    bc1 = 1.0 - ADAM_B1 ** ADAM_STEP
    bc2 = 1.0 - ADAM_B2 ** ADAM_STEP

    def body(p_ref, w_ref, m_ref, v_ref, g_ref, d_ref, nm_ref, nv_ref):
        g = p_ref[0].astype(F32)
        for k in range(1, n_parts):
            g = g + p_ref[k].astype(F32)
        nm = ADAM_B1 * m_ref[...] + (1.0 - ADAM_B1) * g
        nv = ADAM_B2 * v_ref[...] + (1.0 - ADAM_B2) * (g * g)
        g_ref[...] = g
        nm_ref[...] = nm
        nv_ref[...] = nv
        d_ref[...] = -ADAM_LR * ((nm / bc1) / (jnp.sqrt(nv / bc2) + ADAM_EPS) + ADAM_WD * w_ref[...])

    blk = pl.BlockSpec((tr, tc), lambda i, j: (i, j))
    outs, couts = _pcall(
        name, body, (r // tr, c // tc),
        [pl.BlockSpec((n_parts, tr, tc), lambda i, j: (0, i, j)), blk, blk, blk], [blk] * 4,
        [jax.ShapeDtypeStruct((r, c), F32)] * 4, (parts, w, m, v), (), ("parallel", "parallel"), comm)
    return outs if comm is None else (outs, couts)


def _pad_lanes(v, width):
    return jnp.pad(v, ((0, 0), (0, width - v.shape[1])))


def _reduce_start(slots, host):
    outs, sib = host(_pair_comm([a for _, a in slots]))
    sums = [(n, _add_pairs("pairsum_" + n, a, b)) for (n, a), b in zip(slots, sib)]
    return outs, sums


def _train_step(x, target, shard, rep):
    gdt = BF
    recv = {}
    (got,) = _comm_call("gather_ffn1_in", _gather_comm([shard["ffn1_w_in"]], [True]))
    w1_in = got.reshape(2 * D_FF, D_MODEL)
    h1 = _rms_fwd("rms1_fwd", x, rep["ffn1_norm"])
    gu1, got = _mm_nt("ffn1_in", h1, w1_in, tn=FF_HALF, out_dtype=BF, comm=_gather_comm(
        [shard["ffn1_w_out"], shard["w_in"], shard["short_conv_w"], shard["ssm_conv_w"]]))
    w1_out = got[0].reshape(D_FF, D_MODEL)
    w_in_t = got[1].reshape(N_IN, D_MODEL)
    short_conv_w = got[2].transpose(1, 0, 2).reshape(3, D_MODEL)
    ssm_conv_w = got[3].transpose(1, 0, 2).reshape(4, D_XBC)
    act1 = _swiglu_fwd("swiglu1_fwd", gu1)
    x1 = _mm_nn("ffn1_out", act1, w1_out, res=x, alpha=0.5)
    ga0 = N_MAIN + N_HEADS
    gb0 = ga0 + D_MODEL
    half = D_MODEL // 2
    w_gd = jnp.concatenate(
        [w_in_t[ga0:ga0 + half], w_in_t[gb0:gb0 + half], w_in_t[ga0 + half:gb0], w_in_t[gb0 + half:],
         w_in_t[N_MAIN:N_MAIN + N_HEADS], jnp.zeros((DT_W - N_HEADS, D_MODEL), BF)], axis=0)
    w_mix_perm = w_in_t[0:3 * D_MODEL].reshape(3, 4, CONV_CB, D_MODEL).transpose(1, 0, 2, 3).reshape(3 * D_MODEL, D_MODEL)

    h2 = _rms_fwd("rms2_fwd", x1, rep["mix_norm"])
    p, got = _mm_nt("proj_main", h2, w_in_t, n=N_MAIN, tn=1024, out_dtype=BF, comm=_gather_comm(
        [shard["short_w_out"], shard["ssm_w_out"], shard["w_out"]]))
    p_gd = _mm_nt("proj_gd", h2, w_gd)
    short_w_out = got[0].reshape(D_MODEL, D_MODEL)
    ssm_w_out = got[1].reshape(D_INNER, D_MODEL)
    w_out = got[2].reshape(D_MODEL, D_MODEL)
    ya_in = _mix_a_fwd(p, short_conv_w)
    y_a = _mm_nn("short_out", ya_in, short_w_out)
    xconv, (got,) = _ssm_conv_fwd(p, ssm_conv_w, rep["ssm_conv_b"], comm=_gather_comm([shard["ffn2_w_out"]]))
    w2_out = got.reshape(D_FF, D_MODEL)
    dt, acs = _dt_fwd(p_gd, rep["dt_bias_pad"], rep["a_log_pad"])
    (y_ssm, hsave), (got,) = _ssd_fwd(xconv, dt, acs, rep["d_exp"], comm=_gather_comm([shard["ffn2_w_in"]], [True]))
    w2_in = got.reshape(2 * D_FF, D_MODEL)
    yn = _gnorm_fwd(y_ssm, p, rep["ssm_norm"])
    y_b = _mm_nn("ssm_out", yn, ssm_w_out, tk=1024)
    merged = _merge_fwd(p_gd, y_a, y_b)
    x2 = _mm_nn("mix_out", merged, w_out, res=x1)

    h3 = _rms_fwd("rms3_fwd", x2, rep["ffn2_norm"])
    gu2 = _mm_nt("ffn2_in", h3, w2_in, tn=FF_HALF, out_dtype=BF)
    act2 = _swiglu_fwd("swiglu2_fwd", gu2)
    x3 = _mm_nn("ffn2_out", act2, w2_out, res=x2, alpha=0.5)

    loss, dx3, dx3h, g_final = _final_loss(x3, rep["final_norm"], target)

    small = {"final_norm": g_final}
    dact2 = _mm_nt("ffn2_out_bwd_act", dx3h, w2_out, out_dtype=BF)
    g_w2_out = _mm_tn("ffn2_out_bwd_w", act2, dx3h, gdt, tm=FF_HALF)
    dgu2 = _swiglu_bwd("swiglu2_bwd", gu2, dact2)
    g_w2_in = _mm_tn("ffn2_in_bwd_w", dgu2, h3, gdt, tm=FF_HALF)
    dh3 = _mm_nn("ffn2_in_bwd_h", dgu2, w2_in, tk=FF_HALF)
    dx2, dx2b, small["ffn2_norm"] = _rms_bwd("rms3_bwd", x2, rep["ffn2_norm"], dh3, dx3, 1.0)

    dmerged = _mm_nt("mix_out_bwd_x", dx2b, w_out)
    g_w_out = _mm_tn("mix_out_bwd_w", merged, dx2b, gdt)
    dp_gd, dya, dyb = _merge_bwd(p_gd, y_a, y_b, dmerged)

    dya_in = _mm_nt("short_out_bwd_x", dya, short_w_out)
    g_short_w_out = _mm_tn("short_out_bwd_w", ya_in, dya, gdt)

    dyn = _mm_nt("ssm_out_bwd_x", dyb, ssm_w_out)
    g_ssm_w_out = _mm_tn("ssm_out_bwd_w", yn, dyb, gdt)
    late = [("ffn2_w_out", g_w2_out.reshape(N_DEV, FF_SHARD // 2, D_MODEL)),
            ("ffn2_w_in", g_w2_in.reshape(N_DEV, FF_SHARD, D_MODEL)),
            ("w_out", g_w_out.reshape(N_DEV, -1, D_MODEL)), ("short_w_out", g_short_w_out.reshape(N_DEV, -1, D_MODEL)),
            ("ssm_w_out", g_ssm_w_out.reshape(N_DEV, -1, D_MODEL))]
    (dy_ssm, dp, small["ssm_norm"]), sums = _reduce_start(
        late, lambda comm: _gnorm_bwd(y_ssm, p, rep["ssm_norm"], dyn, comm=comm))
    dp, g_short_conv = _mix_a_bwd(p, short_conv_w, dya_in, dp)
    (dxconv, ddt, dacs, dd_lane), got = _ssd_bwd(
        xconv, dt, acs, rep["d_exp"], hsave, dy_ssm,
        comm=_chip_comm([a for _, a in sums], [n == "ffn2_w_in" for n, _ in sums]))
    recv.update({n: a for (n, _), a in zip(sums, got)})
    small["ssm_D"] = dd_lane.reshape(N_HEADS, HEAD_DIM).sum(axis=1)[None, :]
    dp, g_ssm_conv, small["ssm_conv_b"] = _ssm_conv_bwd(p, ssm_conv_w, rep["ssm_conv_b"], dxconv, dp)
    dp_gd, dbias, dalog = _dt_bwd(p_gd, rep["dt_bias_pad"], rep["a_log_pad"], dt, ddt, dacs, dp_gd)
    small["ssm_dt_bias"] = dbias[:, :N_HEADS]
    small["ssm_A_log"] = dalog[:, :N_HEADS]

    g_main = _mm_tn("proj_main_bwd_w", dp, h2, gdt, tm=1024)
    g_gd = _mm_tn("proj_gd_bwd_w", dp_gd, h2, gdt)
    g_mix = g_main[0:3 * D_MODEL].reshape(4, 3, CONV_CB, D_MODEL).transpose(1, 0, 2, 3).reshape(3 * D_MODEL, D_MODEL)
    g_in_t = jnp.concatenate(
        [g_mix, g_main[3 * D_MODEL:], g_gd[2 * D_MODEL:2 * D_MODEL + N_HEADS],
         g_gd[0:half], g_gd[2 * half:3 * half], g_gd[half:2 * half], g_gd[3 * half:4 * half]], axis=0).reshape(
        N_DEV, IN_SHARD, D_MODEL)
    dh2, w_sums = _reduce_start(
        [("w_in", g_in_t)], lambda comm: _mm_nn("proj_mix_bwd_x", dp, w_mix_perm, tk=1024, kk=3 * D_MODEL, comm=comm))
    w_sum = w_sums[0][1]

    def w_piece(i):
        return _chip_comm([w_sum], rows=[W_GRAD_ROW_CUTS[i]])

    dh2, got0 = _mm_nn("proj_rest_bwd_x", dp, w_in_t, tk=1024, kk=N_MAIN - 3 * D_MODEL, a_off=3, b_off=3, res=dh2,
                       comm=w_piece(0))
    dh2, got1 = _mm_nn("proj_gd_bwd_x", dp_gd, w_gd, res=dh2, comm=w_piece(1))
    (dx1, dx1h, small["mix_norm"]), got2 = _rms_bwd("rms2_bwd", x1, rep["mix_norm"], dh2, dx2, 0.5, comm=w_piece(2))
    g_w1_out, got3 = _mm_tn("ffn1_out_bwd_w", act1, dx1h, gdt, tm=FF_HALF, comm=w_piece(3))
    rest = [("ffn1_w_out", g_w1_out.reshape(N_DEV, FF_SHARD // 2, D_MODEL)),
            ("short_conv_w", g_short_conv.reshape(3, N_DEV, -1).transpose(1, 0, 2)),
            ("ssm_conv_w", g_ssm_conv.reshape(4, N_DEV, -1).transpose(1, 0, 2))]
    dact1, got = _mm_nt("ffn1_out_bwd_act", dx1h, w1_out, out_dtype=BF,
                        comm=_join_comm(w_piece(4), _pair_comm([a for _, a in rest])))
    got4, sib = got[0], got[1:]
    rest_sums = [(n, _add_pairs("pairsum_" + n, a, b)) for (n, a), b in zip(rest, sib)]
    recv["w_in"] = jnp.concatenate([got0[0], got1[0], got2[0], got3[0], got4], axis=1)
    dgu1, got = _swiglu_bwd("swiglu1_bwd", gu1, dact1, comm=_chip_comm([a for _, a in rest_sums]))
    recv.update({n: a for (n, _), a in zip(rest_sums, got)})

    def part(tag, width, off, comm=None):
        out = _mm_tn("ffn1_in_bwd_w_" + tag, dgu1, h1, gdt, tm=FF_HALF, n=width, col_off=off, comm=comm)
        g, couts = (out, None) if comm is None else out
        return g.reshape(N_DEV, FF_SHARD, width), couts

    g_a, _ = part("a", 384, 0)
    g_b, sib = part("b", 384, 1, _pair_comm([g_a]))
    sum_a = _add_pairs("pairsum_ffn1_w_in_a", g_a, sib[0])
    g_c, (recv_a, sib_b) = part("c", 256, 3, _join_comm(_chip_comm([sum_a], [True]), _pair_comm([g_b])))
    sum_b = _add_pairs("pairsum_ffn1_w_in_b", g_b, sib_b)
    dh1, (recv_b, sib_c) = _mm_nn("ffn1_in_bwd_h", dgu1, w1_in, tk=FF_HALF,
                                  comm=_join_comm(_chip_comm([sum_b], [True]), _pair_comm([g_c])))
    sum_c = _add_pairs("pairsum_ffn1_w_in_c", g_c, sib_c)
    (dx0, _, small["ffn1_norm"]), (recv_c,) = _rms_bwd("rms1_bwd", x, rep["ffn1_norm"], dh1, dx1, 1.0,
                                                        comm=_chip_comm([sum_c], [True]))
    recv["ffn1_w_in"] = jnp.concatenate([recv_a, recv_b, recv_c], axis=2)
    return dx0, recv, _pack_small(small, loss[:, 0:1])


_SMALL = [("ffn1_norm", 1024), ("mix_norm", 1024), ("ssm_conv_b", 4096), ("ssm_dt_bias", 32), ("ssm_A_log", 32),
          ("ssm_D", 32), ("ssm_norm", 2048), ("ffn2_norm", 1024), ("final_norm", 1024)]
SMALL_W = 10368


def _pack_small(d, loss=None):
    parts = [d[n].reshape(1, -1).astype(F32) for n, _ in _SMALL]
    used = sum(sz for _, sz in _SMALL)
    tail = jnp.zeros((1, SMALL_W - used), F32)
    if loss is not None:
        tail = tail.at[:, 0:1].set(loss)
    return jnp.concatenate(parts + [tail], axis=1)


def _adamw_small(parts, w, m, v):
    n_par = len(_SMALL)
    bc1 = 1.0 - ADAM_B1 ** ADAM_STEP
    bc2 = 1.0 - ADAM_B2 ** ADAM_STEP
    used = sum(sz for _, sz in _SMALL)

    def body(*refs):
        p_ref = refs[0]
        ins = refs[1:1 + 3 * n_par]
        outs = refs[1 + 3 * n_par:]
        g_all = p_ref[0]
        for k in range(1, N_DEV):
            g_all = g_all + p_ref[k]
        off = 0
        for i, (_, sz) in enumerate(_SMALL):
            g = g_all[:, off:off + sz]
            w_ref, m_ref, v_ref = ins[3 * i:3 * i + 3]
            nm = ADAM_B1 * m_ref[...] + (1.0 - ADAM_B1) * g
            nv = ADAM_B2 * v_ref[...] + (1.0 - ADAM_B2) * (g * g)
            outs[4 * i][...] = g
            outs[4 * i + 1][...] = -ADAM_LR * ((nm / bc1) / (jnp.sqrt(nv / bc2) + ADAM_EPS) + ADAM_WD * w_ref[...])
            outs[4 * i + 2][...] = nm
            outs[4 * i + 3][...] = nv
            off += sz
        outs[4 * n_par][...] = g_all[:, used:SMALL_W]

    args = [parts]
    out_shape = []
    for name, sz in _SMALL:
        args += [w[name], m[name], v[name]]
        out_shape += [jax.ShapeDtypeStruct((1, sz), F32)] * 4
    out_shape.append(jax.ShapeDtypeStruct((1, SMALL_W - used), F32))
    res = pl.pallas_call(body, name="adamw_small", out_shape=out_shape,
                         compiler_params=pltpu.CompilerParams(vmem_limit_bytes=VMEM_LIMIT_V7X))(*args)
    return {name: tuple(res[4 * i:4 * i + 4]) for i, (name, _) in enumerate(_SMALL)}, res[-1]


_SHARDED = ["ffn1_w_in", "ffn1_w_out", "w_in", "short_conv_w", "short_w_out", "ssm_conv_w", "ssm_w_out", "w_out",
            "ffn2_w_in", "ffn2_w_out"]
_TRANSPOSED = ("ffn1_w_in", "w_in", "ffn2_w_in")
_ORDER = ["ffn1_norm", "ffn1_w_in", "ffn1_w_out", "mix_norm", "w_in", "short_conv_w", "short_w_out", "ssm_conv_w",
          "ssm_conv_b", "ssm_dt_bias", "ssm_A_log", "ssm_D", "ssm_norm", "ssm_w_out", "w_out", "ffn2_norm",
          "ffn2_w_in", "ffn2_w_out", "final_norm"]


def kernel(x, ffn1_norm, ffn1_w_in, ffn1_w_out, mix_norm, w_in, short_conv_w, short_w_out, ssm_conv_w, ssm_conv_b, ssm_dt_bias, ssm_A_log, ssm_D, ssm_norm, ssm_w_out, w_out, ffn2_norm, ffn2_w_in, ffn2_w_out, final_norm, loss_target, m_ffn1_norm, m_ffn1_w_in, m_ffn1_w_out, m_mix_norm, m_w_in, m_short_conv_w, m_short_w_out, m_ssm_conv_w, m_ssm_conv_b, m_ssm_dt_bias, m_ssm_A_log, m_ssm_D, m_ssm_norm, m_ssm_w_out, m_w_out, m_ffn2_norm, m_ffn2_w_in, m_ffn2_w_out, m_final_norm, v_ffn1_norm, v_ffn1_w_in, v_ffn1_w_out, v_mix_norm, v_w_in, v_short_conv_w, v_short_w_out, v_ssm_conv_w, v_ssm_conv_b, v_ssm_dt_bias, v_ssm_A_log, v_ssm_D, v_ssm_norm, v_ssm_w_out, v_w_out, v_ffn2_norm, v_ffn2_w_in, v_ffn2_w_out, v_final_norm):
    w = dict(ffn1_norm=ffn1_norm, ffn1_w_in=ffn1_w_in, ffn1_w_out=ffn1_w_out, mix_norm=mix_norm, w_in=w_in,
             short_conv_w=short_conv_w, short_w_out=short_w_out, ssm_conv_w=ssm_conv_w, ssm_conv_b=ssm_conv_b,
             ssm_dt_bias=ssm_dt_bias, ssm_A_log=ssm_A_log, ssm_D=ssm_D, ssm_norm=ssm_norm, ssm_w_out=ssm_w_out,
             w_out=w_out, ffn2_norm=ffn2_norm, ffn2_w_in=ffn2_w_in, ffn2_w_out=ffn2_w_out, final_norm=final_norm)
    m = dict(ffn1_norm=m_ffn1_norm, ffn1_w_in=m_ffn1_w_in, ffn1_w_out=m_ffn1_w_out, mix_norm=m_mix_norm, w_in=m_w_in,
             short_conv_w=m_short_conv_w, short_w_out=m_short_w_out, ssm_conv_w=m_ssm_conv_w,
             ssm_conv_b=m_ssm_conv_b, ssm_dt_bias=m_ssm_dt_bias, ssm_A_log=m_ssm_A_log, ssm_D=m_ssm_D,
             ssm_norm=m_ssm_norm, ssm_w_out=m_ssm_w_out, w_out=m_w_out, ffn2_norm=m_ffn2_norm,
             ffn2_w_in=m_ffn2_w_in, ffn2_w_out=m_ffn2_w_out, final_norm=m_final_norm)
    v = dict(ffn1_norm=v_ffn1_norm, ffn1_w_in=v_ffn1_w_in, ffn1_w_out=v_ffn1_w_out, mix_norm=v_mix_norm, w_in=v_w_in,
             short_conv_w=v_short_conv_w, short_w_out=v_short_w_out, ssm_conv_w=v_ssm_conv_w,
             ssm_conv_b=v_ssm_conv_b, ssm_dt_bias=v_ssm_dt_bias, ssm_A_log=v_ssm_A_log, ssm_D=v_ssm_D,
             ssm_norm=v_ssm_norm, ssm_w_out=v_ssm_w_out, w_out=v_w_out, ffn2_norm=v_ffn2_norm,
             ffn2_w_in=v_ffn2_w_in, ffn2_w_out=v_ffn2_w_out, final_norm=v_final_norm)
    shapes = {n: w[n].shape for n in _ORDER}

    def local(d, n):
        return d[n][0].T if n in _TRANSPOSED else d[n][0]

    shard = {n: local(w, n) for n in _SHARDED}

    wire = {n: (shard[n] if n in ("short_conv_w", "ssm_conv_w") else shard[n].astype(BF)) for n in _SHARDED}
    rep = {
        "ffn1_norm": ffn1_norm, "mix_norm": mix_norm, "ffn2_norm": ffn2_norm, "ssm_norm": ssm_norm,
        "ssm_conv_b": ssm_conv_b, "final_norm": final_norm.reshape(1, D_MODEL),
        "dt_bias_pad": _pad_lanes(ssm_dt_bias, DT_W), "a_log_pad": _pad_lanes(ssm_A_log, DT_W),
        "d_exp": jnp.repeat(ssm_D, HEAD_DIM, axis=1),
    }
    grad_x, parts, packed = _train_step(x[0], loss_target[0], wire, rep)

    out_g, out_d, out_m, out_v = {}, {}, {}, {}
    for n in _SHARDED:
        if n == "ssm_w_out":
            res, (small_parts,) = _adamw("adamw_" + n, parts[n], shard[n], local(m, n), local(v, n),
                                         comm=_gather_comm([packed]))
        else:
            res = _adamw("adamw_" + n, parts[n], shard[n], local(m, n), local(v, n))
        out_g[n], out_d[n], out_m[n], out_v[n] = [(r.T if n in _TRANSPOSED else r).reshape(shapes[n]) for r in res]
    row = lambda d: {n: d[n].reshape(1, -1) for n, _ in _SMALL}
    sres, loss_row = _adamw_small(small_parts, row(w), row(m), row(v))
    for n, _ in _SMALL:
        out_g[n], out_d[n], out_m[n], out_v[n] = [r.reshape(shapes[n]) for r in sres[n]]
    loss = loss_row[0, 0]
    return (loss, grad_x[None], *[out_g[n] for n in _ORDER], *[out_d[n] for n in _ORDER],
            *[out_m[n] for n in _ORDER], *[out_v[n] for n in _ORDER])
```

```python
import functools

import jax
import jax.numpy as jnp
from jax import lax
from jax.experimental import pallas as pl
from jax.experimental.pallas import tpu as pltpu

F32 = jnp.float32
BF = jnp.bfloat16

N_DEV = 8
D_MODEL = 1024
D_FF = 2816
D_INNER = 2048
D_XBC = 4096
N_HEADS = 32
HEAD_DIM = 64
N_GROUPS = 8
D_STATE = 128
CHUNK = 64
GROUP_W = D_INNER // N_GROUPS
HEADS_PER_GROUP = N_HEADS // N_GROUPS
NORM_EPS = 1e-5
N_IN = 11296
FF_SHARD = 2 * D_FF // N_DEV
FF_HALF = D_FF // 2
IN_SHARD = N_IN // N_DEV

OFF_B, OFF_C, OFF_XA, OFF_Z, OFF_XBC = 0, 1024, 2048, 3072, 5120
N_MAIN = 9216
OFF_DT = 2048
DT_W = 128
N_GD = 2048 + DT_W
W_GRAD_ROW_CUTS = [(0, 512), (512, 720), (720, 896), (896, 1152), (1152, 1412)]

ADAM_LR, ADAM_B1, ADAM_B2, ADAM_EPS, ADAM_WD, ADAM_STEP = 0.001, 0.9, 0.999, 1e-08, 0.01, 10

VMEM_LIMIT_V7X = 56 * 1024 * 1024
TM = 1024
TN_MAX_TOKENS = 2048
TE = 512
ADAM_COL_TILE = 256
GATHER_PIECES = 4
GATHER_PIECE_MIN_ROWS = 512


def _params(*sem):
    return pltpu.CompilerParams(dimension_semantics=sem, vmem_limit_bytes=VMEM_LIMIT_V7X)


_DIMS = {
    "nn": (((1,), (0,)), ((), ())),
    "nt": (((1,), (1,)), ((), ())),
    "tn": (((0,), (0,)), ((), ())),
}


def _dot(a, b, mode="nn"):
    return lax.dot_general(a, b, _DIMS[mode], preferred_element_type=F32)


def _sigmoid(x):
    return 1.0 / (1.0 + jnp.exp(-x))


class _Comm:
    def __init__(self, inputs, out_shapes, sems, start, finish):
        self.inputs, self.out_shapes, self.sems, self.start, self.finish = inputs, out_shapes, sems, start, finish


def _pcall(name, body, grid, in_specs, out_specs, out_shape, args, scratch=(), sem=None, comm=None):
    single = not isinstance(out_shape, (list, tuple))
    out_shapes = [out_shape] if single else list(out_shape)
    out_specs = [out_specs] if single else list(out_specs)
    n_in, n_out, n_scr = len(args), len(out_shapes), len(scratch)
    if comm is None:
        res = pl.pallas_call(
            body, name=name, grid=grid, in_specs=list(in_specs), out_specs=out_specs, out_shape=out_shapes,
            scratch_shapes=list(scratch), compiler_params=_params(*sem))(*args)
        return (res[0] if single else res), []
    nci, nco = len(comm.inputs), len(comm.out_shapes)

    def wrapped(*refs):
        a = refs[:n_in]
        ci = refs[n_in:n_in + nci]
        o0 = n_in + nci
        o = refs[o0:o0 + n_out]
        co = refs[o0 + n_out:o0 + n_out + nco]
        s0 = o0 + n_out + nco
        s = refs[s0:s0 + n_scr]
        cs = refs[s0 + n_scr:]
        pids = [pl.program_id(i) for i in range(len(grid))]
        first = functools.reduce(jnp.logical_and, [p == 0 for p in pids])
        last = functools.reduce(jnp.logical_and, [p == g - 1 for p, g in zip(pids, grid)])

        @pl.when(first)
        def _():
            comm.start(ci, co, cs)

        body(*a, *o, *s)

        @pl.when(last)
        def _():
            comm.finish(ci, co, cs)

    any_spec = pl.BlockSpec(memory_space=pl.ANY)
    res = pl.pallas_call(
        wrapped, name=name, grid=grid, in_specs=list(in_specs) + [any_spec] * nci,
        out_specs=out_specs + [any_spec] * nco, out_shape=out_shapes + list(comm.out_shapes),
        scratch_shapes=list(scratch) + list(comm.sems),
        compiler_params=_params(*(("arbitrary",) * len(grid))))(*args, *comm.inputs)
    core = res[:n_out]
    return (core[0] if single else core), list(res[n_out:])


def _comm_call(name, comm):
    nci, nco = len(comm.inputs), len(comm.out_shapes)

    def body(*refs):
        ci, co, cs = refs[:nci], refs[nci:nci + nco], refs[nci + nco:]
        comm.start(ci, co, cs)
        comm.finish(ci, co, cs)

    any_spec = pl.BlockSpec(memory_space=pl.ANY)
    return pl.pallas_call(
        body, name=name, in_specs=[any_spec] * nci, out_specs=[any_spec] * nco, out_shape=list(comm.out_shapes),
        scratch_shapes=list(comm.sems), compiler_params=pltpu.CompilerParams(has_side_effects=True))(*comm.inputs)


def _remote(src, dst, ssem, rsem, dev):
    return pltpu.make_async_remote_copy(src_ref=src, dst_ref=dst, send_sem=ssem, recv_sem=rsem, device_id=dev,
                                        device_id_type=pl.DeviceIdType.MESH)


def _place():
    x, y, c = lax.axis_index("x"), lax.axis_index("y"), lax.axis_index("c")
    other_chips = [(1 - x, y), (x, 1 - y), (1 - x, 1 - y)]
    return x, y, c, other_chips


def _slot(x, y, c, swap):
    return 4 * y + 2 * x + c if swap else 4 * x + 2 * y + c


def _chip_slot(x, y, swap):
    return 2 * y + x if swap else 2 * x + y


def _gather_comm(shards, swaps=None):
    n = len(shards)
    per = N_DEV - 1
    swaps = [False] * n if swaps is None else swaps
    pieces = []
    for i, a in enumerate(shards):
        rows = a.shape[0]
        k = GATHER_PIECES if (a.ndim == 2 and rows >= GATHER_PIECE_MIN_ROWS) else 1
        step = -(-rows // (k * 8)) * 8
        if k == 1:
            pieces.append((i, 0, None))
        else:
            pieces += [(i, r, min(step, rows - r)) for r in range(0, rows, step)]
    m = len(pieces)

    def src(ins, v):
        i, r, cnt = pieces[v]
        return ins[i] if cnt is None else ins[i].at[pl.ds(r, cnt)]

    def place(outs, v, x, y, c):
        i, r, cnt = pieces[v]
        blk = outs[i].at[_slot(x, y, c, swaps[i])]
        return blk if cnt is None else blk.at[pl.ds(r, cnt)]

    def start(ins, outs, sems):
        send, recv, loc = sems
        x, y, c, chips = _place()
        for v in range(m):
            me = place(outs, v, x, y, c)
            pltpu.make_async_copy(src(ins, v), me, loc.at[v]).start()
            _remote(src(ins, v), me, send.at[per * v], recv.at[per * v], (x, y, 1 - c)).start()
        for j, (qx, qy) in enumerate(chips):
            for v in range(m):
                _remote(src(ins, v), place(outs, v, x, y, c), send.at[per * v + 1 + j], recv.at[per * v + 1 + j],
                        (qx, qy, c)).start()

    def finish(ins, outs, sems):
        send, recv, loc = sems
        x, y, c, chips = _place()
        sib = (x, y, 1 - c)
        for v in range(m):
            for j, (qx, qy) in enumerate(chips):
                blk = place(outs, v, qx, qy, c)
                _remote(blk, blk, send.at[per * v + 1 + j], recv.at[per * v + 1 + j], (qx, qy, c)).wait_recv()
                _remote(blk, blk, send.at[per * v + 4 + j], recv.at[per * v + 4 + j], sib).start()
        for v in range(m):
            blk = place(outs, v, x, y, 1 - c)
            _remote(blk, blk, send.at[per * v], recv.at[per * v], sib).wait_recv()
            for j, (qx, qy) in enumerate(chips):
                blk = place(outs, v, qx, qy, 1 - c)
                _remote(blk, blk, send.at[per * v + 4 + j], recv.at[per * v + 4 + j], sib).wait_recv()
        for v in range(m):
            own = place(outs, v, x, y, c)
            for k in range(per):
                _remote(src(ins, v), own, send.at[per * v + k], recv.at[per * v + k], sib).wait_send()
            pltpu.make_async_copy(src(ins, v), own, loc.at[v]).wait()

    out_shapes = [jax.ShapeDtypeStruct((N_DEV,) + tuple(a.shape), a.dtype) for a in shards]
    sems = [pltpu.SemaphoreType.DMA((per * m,)), pltpu.SemaphoreType.DMA((per * m,)), pltpu.SemaphoreType.DMA((m,))]
    return _Comm(list(shards), out_shapes, sems, start, finish)


def _pair_comm(slots):
    n = len(slots)

    def copies(ins, outs, sems):
        send, recv = sems
        x, y, c, _ = _place()
        sib = (x, y, 1 - c)
        out = []
        for i in range(n):
            for q in range(4):
                out.append(_remote(ins[i].at[2 * q + 1 - c], outs[i].at[q], send.at[4 * i + q], recv.at[4 * i + q], sib))
        return out

    def start(ins, outs, sems):
        for cp in copies(ins, outs, sems):
            cp.start()

    def finish(ins, outs, sems):
        for cp in copies(ins, outs, sems):
            cp.wait_send()
            cp.wait_recv()

    out_shapes = [jax.ShapeDtypeStruct((4,) + tuple(a.shape[1:]), a.dtype) for a in slots]
    sems = [pltpu.SemaphoreType.DMA((4 * n,)), pltpu.SemaphoreType.DMA((4 * n,))]
    return _Comm(list(slots), out_shapes, sems, start, finish)


def _chip_comm(chip_sums, swaps=None, rows=None):
    n = len(chip_sums)
    swaps = [False] * n if swaps is None else swaps
    rows = [None] * n if rows is None else rows

    def src(ins, i, q):
        return ins[i].at[q] if rows[i] is None else ins[i].at[q, pl.ds(rows[i][0], rows[i][1] - rows[i][0])]

    def start(ins, outs, sems):
        send, recv, loc = sems
        x, y, c, chips = _place()
        for i in range(n):
            mine = _chip_slot(x, y, swaps[i])
            pltpu.make_async_copy(src(ins, i, mine), outs[i].at[mine], loc.at[i]).start()
            for j, (qx, qy) in enumerate(chips):
                _remote(src(ins, i, _chip_slot(qx, qy, swaps[i])), outs[i].at[mine], send.at[3 * i + j],
                        recv.at[3 * i + j], (qx, qy, c)).start()

    def finish(ins, outs, sems):
        send, recv, loc = sems
        x, y, c, chips = _place()
        for i in range(n):
            mine = _chip_slot(x, y, swaps[i])
            for j, (qx, qy) in enumerate(chips):
                theirs = _chip_slot(qx, qy, swaps[i])
                cp = _remote(src(ins, i, theirs), outs[i].at[theirs], send.at[3 * i + j], recv.at[3 * i + j], (qx, qy, c))
                cp.wait_send()
                cp.wait_recv()
            pltpu.make_async_copy(src(ins, i, mine), outs[i].at[mine], loc.at[i]).wait()

    def out_shape(a, r):
        shape = a.shape if r is None else (a.shape[0], r[1] - r[0]) + tuple(a.shape[2:])
        return jax.ShapeDtypeStruct(shape, a.dtype)

    out_shapes = [out_shape(a, r) for a, r in zip(chip_sums, rows)]
    sems = [pltpu.SemaphoreType.DMA((3 * n,)), pltpu.SemaphoreType.DMA((3 * n,)), pltpu.SemaphoreType.DMA((n,))]
    return _Comm(list(chip_sums), out_shapes, sems, start, finish)


def _join_comm(a, b):
    na_i, na_o, na_s = len(a.inputs), len(a.out_shapes), len(a.sems)

    def start(ins, outs, sems):
        a.start(ins[:na_i], outs[:na_o], sems[:na_s])
        b.start(ins[na_i:], outs[na_o:], sems[na_s:])

    def finish(ins, outs, sems):
        a.finish(ins[:na_i], outs[:na_o], sems[:na_s])
        b.finish(ins[na_i:], outs[na_o:], sems[na_s:])

    return _Comm(a.inputs + b.inputs, a.out_shapes + b.out_shapes, a.sems + b.sems, start, finish)


def _row_tile(r):
    for cand in (256, 128):
        if r > cand and r % cand == 0:
            return cand
    return r


def _add_pairs(name, slots, sib):
    r, c = slots.shape[1:]
    tr = _row_tile(r)

    def body(core_ref, s_ref, b_ref, o_ref):
        o_ref[...] = (s_ref[...].astype(F32) + b_ref[...].astype(F32)).astype(o_ref.dtype)

    core = jnp.full((1,), lax.axis_index("c"), jnp.int32)
    return pl.pallas_call(
        body, name=name,
        grid_spec=pltpu.PrefetchScalarGridSpec(
            num_scalar_prefetch=1, grid=(4, r // tr),
            in_specs=[pl.BlockSpec((None, None, tr, c), lambda q, i, core_ref: (q, core_ref[0], i, 0)),
                      pl.BlockSpec((None, tr, c), lambda q, i, core_ref: (q, i, 0))],
            out_specs=pl.BlockSpec((None, tr, c), lambda q, i, core_ref: (q, i, 0))),
        out_shape=jax.ShapeDtypeStruct((4, r, c), slots.dtype),
        compiler_params=_params("parallel", "parallel"))(core, slots.reshape(4, 2, r, c), sib)


def _matmul(name, mode, a, b, grid, a_spec, b_spec, o_spec, out_shape, acc_shape,
            res=None, res_spec=None, alpha=1.0, comm=None):
    nk = grid[-1]
    has_res = res is not None

    def body(*refs):
        if has_res:
            a_ref, b_ref, r_ref, o_ref = refs[:4]
        else:
            a_ref, b_ref, o_ref = refs[:3]
            r_ref = None
        part = _dot(a_ref[...], b_ref[...], mode)

        def finish(v):
            if alpha != 1.0:
                v = v * alpha
            if has_res:
                v = r_ref[...] + v
            o_ref[...] = v.astype(o_ref.dtype)

        if nk == 1:
            finish(part)
        else:
            acc = refs[-1]
            k = pl.program_id(len(grid) - 1)

            @pl.when(k == 0)
            def _():
                acc[...] = part

            @pl.when(k > 0)
            def _():
                acc[...] += part

            @pl.when(k == nk - 1)
            def _():
                finish(acc[...])

    in_specs = [a_spec, b_spec] + ([res_spec] if has_res else [])
    args = (a, b) + ((res,) if has_res else ())
    scratch = [] if nk == 1 else [pltpu.VMEM(acc_shape, F32)]
    sem = ("parallel",) * (len(grid) - 1) + ("arbitrary",)
    out, couts = _pcall(name, body, grid, in_specs, o_spec, out_shape, args, scratch, sem, comm)
    return out if comm is None else (out, couts)


def _mm_nn(name, a, b, out_dtype=F32, res=None, alpha=1.0, tk=None, kk=None, a_off=0, b_off=0, comm=None):
    t = a.shape[0]
    kk = a.shape[1] if kk is None else kk
    n = b.shape[1]
    tk = kk if tk is None else tk
    grid = (t // TM, 1, kk // tk)
    return _matmul(
        name, "nn", a, b, grid,
        pl.BlockSpec((TM, tk), lambda i, j, k: (i, k + a_off)),
        pl.BlockSpec((tk, n), lambda i, j, k: (k + b_off, 0)),
        pl.BlockSpec((TM, n), lambda i, j, k: (i, 0)),
        jax.ShapeDtypeStruct((t, n), out_dtype), (TM, n),
        res=res, res_spec=pl.BlockSpec((TM, n), lambda i, j, k: (i, 0)), alpha=alpha, comm=comm)


def _mm_nt(name, a, b, n=None, tn=None, tk=None, out_dtype=F32, comm=None):
    t, kk = a.shape
    n = b.shape[0] if n is None else n
    tn = n if tn is None else tn
    tk = kk if tk is None else tk
    grid = (n // tn, t // TM, kk // tk)
    return _matmul(
        name, "nt", a, b, grid,
        pl.BlockSpec((TM, tk), lambda j, i, k: (i, k)),
        pl.BlockSpec((tn, tk), lambda j, i, k: (j, k)),
        pl.BlockSpec((TM, tn), lambda j, i, k: (i, j)),
        jax.ShapeDtypeStruct((t, n), out_dtype), (TM, tn), comm=comm)


def _mm_tn(name, a, b, out_dtype, tm=None, n=None, col_off=0, comm=None):
    t, m = a.shape
    n = b.shape[1] if n is None else n
    tm = m if tm is None else tm
    tk = t if t <= TN_MAX_TOKENS else TM
    grid = (m // tm, 1, t // tk)
    return _matmul(
        name, "tn", a, b, grid,
        pl.BlockSpec((tk, tm), lambda j, i, k: (k, j)),
        pl.BlockSpec((tk, n), lambda j, i, k: (k, col_off)),
        pl.BlockSpec((tm, n), lambda j, i, k: (j, 0)),
        jax.ShapeDtypeStruct((m, n), out_dtype), (tm, n), comm=comm)


def _rms_fwd(name, x, w):
    t, d = x.shape

    def body(x_ref, w_ref, h_ref):
        xv = x_ref[...]
        rstd = lax.rsqrt(jnp.mean(xv * xv, axis=-1, keepdims=True) + NORM_EPS)
        h_ref[...] = (xv * rstd * w_ref[...]).astype(h_ref.dtype)

    return pl.pallas_call(
        body, name=name, grid=(t // TE,),
        in_specs=[pl.BlockSpec((TE, d), lambda i: (i, 0)), pl.BlockSpec((1, d), lambda i: (0, 0))],
        out_specs=pl.BlockSpec((TE, d), lambda i: (i, 0)),
        out_shape=jax.ShapeDtypeStruct((t, d), BF), compiler_params=_params("parallel"))(x, w)


def _rms_bwd(name, x, w, dh, dres, out_scale, comm=None):
    t, d = x.shape

    def body(x_ref, w_ref, dh_ref, dres_ref, dx_ref, dxb_ref, dw_ref):
        i = pl.program_id(0)
        xv = x_ref[...]
        rstd = lax.rsqrt(jnp.mean(xv * xv, axis=-1, keepdims=True) + NORM_EPS)
        xhat = xv * rstd
        dhv = dh_ref[...]
        wd = dhv * w_ref[...]
        proj = jnp.mean(wd * xhat, axis=-1, keepdims=True)
        dx = dres_ref[...] + rstd * (wd - xhat * proj)
        dx_ref[...] = dx
        dxb_ref[...] = (dx * out_scale).astype(BF)
        part = jnp.sum(dhv * xhat, axis=0, keepdims=True)

        @pl.when(i == 0)
        def _():
            dw_ref[...] = part

        @pl.when(i > 0)
        def _():
            dw_ref[...] += part

    row = pl.BlockSpec((TE, d), lambda i: (i, 0))
    vec = pl.BlockSpec((1, d), lambda i: (0, 0))
    outs, couts = _pcall(
        name, body, (t // TE,), [row, vec, row, row], [row, row, vec],
        [jax.ShapeDtypeStruct((t, d), F32), jax.ShapeDtypeStruct((t, d), BF), jax.ShapeDtypeStruct((1, d), F32)],
        (x, w, dh, dres), (), ("arbitrary",), comm)
    return outs if comm is None else (outs, couts)


def _final_loss(x, w, target):
    t, d = x.shape

    def body(x_ref, w_ref, t_ref, loss_ref, dx_ref, dxb_ref, dw_ref):
        i = pl.program_id(0)
        xv = x_ref[...]
        rstd = lax.rsqrt(jnp.mean(xv * xv, axis=-1, keepdims=True) + NORM_EPS)
        xhat = xv * rstd
        err = xhat * w_ref[...] - t_ref[...]
        lpart = 0.5 * jnp.sum(jnp.mean(err * err, axis=-1, keepdims=True), axis=0, keepdims=True)
        dy = err * (1.0 / d)
        wd = dy * w_ref[...]
        proj = jnp.mean(wd * xhat, axis=-1, keepdims=True)
        dx = rstd * (wd - xhat * proj)
        dx_ref[...] = dx
        dxb_ref[...] = (0.5 * dx).astype(BF)
        part = jnp.sum(dy * xhat, axis=0, keepdims=True)
        lfull = jnp.broadcast_to(lpart, (1, 128))

        @pl.when(i == 0)
        def _():
            dw_ref[...] = part
            loss_ref[...] = lfull

        @pl.when(i > 0)
        def _():
            dw_ref[...] += part
            loss_ref[...] += lfull

    row = pl.BlockSpec((TE, d), lambda i: (i, 0))
    vec = pl.BlockSpec((1, d), lambda i: (0, 0))
    return pl.pallas_call(
        body, name="final_loss", grid=(t // TE,), in_specs=[row, vec, row],
        out_specs=[pl.BlockSpec((1, 128), lambda i: (0, 0)), row, row, vec],
        out_shape=[jax.ShapeDtypeStruct((1, 128), F32), jax.ShapeDtypeStruct((t, d), F32),
                   jax.ShapeDtypeStruct((t, d), BF), jax.ShapeDtypeStruct((1, d), F32)],
        compiler_params=_params("arbitrary"))(x, w, target)


def _swiglu_fwd(name, gu, comm=None):
    t = gu.shape[0]

    def body(g_ref, u_ref, a_ref):
        g = g_ref[...].astype(F32)
        a_ref[...] = (g * _sigmoid(g) * u_ref[...].astype(F32)).astype(BF)

    blk = (TE, FF_HALF)
    out, couts = _pcall(
        name, body, (t // TE, 2),
        [pl.BlockSpec(blk, lambda i, j: (i, 2 * j)), pl.BlockSpec(blk, lambda i, j: (i, 2 * j + 1))],
        pl.BlockSpec(blk, lambda i, j: (i, j)), jax.ShapeDtypeStruct((t, D_FF), BF),
        (gu, gu), (), ("parallel", "parallel"), comm)
    return out if comm is None else (out, couts)


def _swiglu_bwd(name, gu, dact, comm=None):
    t = gu.shape[0]

    def body(g_ref, u_ref, da_ref, o_ref):
        g = g_ref[...].astype(F32)
        da = da_ref[...].astype(F32)
        s = _sigmoid(g)
        o_ref[:, 0:FF_HALF] = (da * u_ref[...].astype(F32) * (s * (1.0 + g * (1.0 - s)))).astype(BF)
        o_ref[:, FF_HALF:2 * FF_HALF] = (da * g * s).astype(BF)

    blk = (TE, FF_HALF)
    out, couts = _pcall(
        name, body, (t // TE, 2),
        [pl.BlockSpec(blk, lambda i, j: (i, 2 * j)), pl.BlockSpec(blk, lambda i, j: (i, 2 * j + 1)),
         pl.BlockSpec(blk, lambda i, j: (i, j))],
        pl.BlockSpec((TE, 2 * FF_HALF), lambda i, j: (i, j)),
        jax.ShapeDtypeStruct((t, 2 * D_FF), BF), (gu, gu, dact), (), ("parallel", "parallel"), comm)
    return out if comm is None else (out, couts)


CONV_CB = 256


CONV_ROWS = 64
CONV_HALO = 16


def _taps_down(ext, w, k):
    shifted = [pltpu.roll(ext, k - 1 - j, 0)[CONV_HALO:] for j in range(k - 1)] + [ext[CONV_HALO:]]
    out = shifted[k - 1] * w[k - 1:k, :]
    for j in range(k - 1):
        out = out + shifted[j] * w[j:j + 1, :]
    return out, shifted


def _taps_up(ext, w, k):
    rows = ext.shape[0]
    n = rows - CONV_HALO
    out = ext[:n] * w[k - 1:k, :]
    for j in range(k - 1):
        out = out + pltpu.roll(ext, rows - (k - 1 - j), 0)[:n] * w[j:j + 1, :]
    return out


def _rows_before(ref, i, r0):
    start = pl.multiple_of(jnp.maximum(r0 - CONV_HALO, 0), CONV_HALO)
    return jnp.where(i > 0, ref[pl.ds(start, CONV_HALO), :].astype(F32), 0.0)


def _rows_after(ref, r0, t):
    start = pl.multiple_of(jnp.minimum(r0 + CONV_ROWS, t - CONV_HALO), CONV_HALO)
    return ref[pl.ds(start, CONV_HALO), :].astype(F32)


def _fold8(v):
    return v.reshape(v.shape[0] // 8, 8, v.shape[1]).sum(axis=0)


def _silu_grad(pre):
    s = _sigmoid(pre)
    return s * (1.0 + pre * (1.0 - s))


def _pspec(t, off):
    base = off // CONV_CB
    return pl.BlockSpec((t, CONV_CB), lambda j: (0, base + j))


def _mix_a_fwd(p, conv_w):
    t = p.shape[0]

    def body(b_ref, c_ref, xa_ref, w_ref, o_ref):
        w = w_ref[...]

        def step(i, carry):
            r0 = pl.multiple_of(i * CONV_ROWS, CONV_ROWS)
            rows = pl.ds(r0, CONV_ROWS)
            q = c_ref[rows, :].astype(F32) * xa_ref[rows, :].astype(F32)
            q_before = _rows_before(c_ref, i, r0) * _rows_before(xa_ref, i, r0)
            va, _ = _taps_down(jnp.concatenate([q_before, q], axis=0), w, 3)
            o_ref[rows, :] = (b_ref[rows, :].astype(F32) * va).astype(BF)
            return carry

        lax.fori_loop(0, t // CONV_ROWS, step, 0)

    return pl.pallas_call(
        body, name="mix_a_fwd", grid=(D_MODEL // CONV_CB,),
        in_specs=[_pspec(t, OFF_B), _pspec(t, OFF_C), _pspec(t, OFF_XA),
                  pl.BlockSpec((3, CONV_CB), lambda j: (0, j))],
        out_specs=pl.BlockSpec((t, CONV_CB), lambda j: (0, j)),
        out_shape=jax.ShapeDtypeStruct((t, D_MODEL), BF), compiler_params=_params("parallel"))(p, p, p, conv_w)


def _mix_a_bwd(p, conv_w, dya, dp):
    t = p.shape[0]

    def body(b_ref, c_ref, xa_ref, w_ref, dy_ref, dp_in, dp_ref, dw_ref):
        del dp_in
        w = w_ref[...]
        n = t // CONV_ROWS

        def step(i, acc):
            r0 = pl.multiple_of(i * CONV_ROWS, CONV_ROWS)
            rows = pl.ds(r0, CONV_ROWS)
            cv = c_ref[rows, :].astype(F32)
            xav = xa_ref[rows, :].astype(F32)
            q_before = _rows_before(c_ref, i, r0) * _rows_before(xa_ref, i, r0)
            va, shifted = _taps_down(jnp.concatenate([q_before, cv * xav], axis=0), w, 3)
            dyv = dy_ref[rows, :]
            dp_ref[rows, 0:CONV_CB] = (dyv * va).astype(BF)
            dv = dyv * b_ref[rows, :].astype(F32)
            dv_after = jnp.where(i < n - 1, _rows_after(dy_ref, r0, t) * _rows_after(b_ref, r0, t), 0.0)
            dq = _taps_up(jnp.concatenate([dv, dv_after], axis=0), w, 3)
            dp_ref[rows, CONV_CB:2 * CONV_CB] = (dq * xav).astype(BF)
            dp_ref[rows, 2 * CONV_CB:3 * CONV_CB] = (dq * cv).astype(BF)
            return tuple(a + _fold8(dv * s) for a, s in zip(acc, shifted))

        zero = jnp.zeros((8, CONV_CB), F32)
        acc = lax.fori_loop(0, n, step, (zero, zero, zero))
        for j in range(3):
            dw_ref[j:j + 1, :] = jnp.sum(acc[j], axis=0, keepdims=True)

    col = pl.BlockSpec((t, CONV_CB), lambda j: (0, j))
    wsp = pl.BlockSpec((3, CONV_CB), lambda j: (0, j))
    return pl.pallas_call(
        body, name="mix_a_bwd", grid=(D_MODEL // CONV_CB,),
        in_specs=[_pspec(t, OFF_B), _pspec(t, OFF_C), _pspec(t, OFF_XA), wsp, col, pl.BlockSpec(memory_space=pl.ANY)],
        out_specs=[pl.BlockSpec((t, 3 * CONV_CB), lambda j: (0, j)), wsp],
        out_shape=[jax.ShapeDtypeStruct(dp.shape, dp.dtype), jax.ShapeDtypeStruct((3, D_MODEL), F32)],
        input_output_aliases={5: 0},
        compiler_params=_params("parallel"))(p, p, p, conv_w, dya, dp)


def _ssm_conv_fwd(p, conv_w, conv_b, comm=None):
    t = p.shape[0]

    def body(x_ref, w_ref, b_ref, o_ref):
        w = w_ref[...]
        bias = b_ref[...]

        def step(i, carry):
            r0 = pl.multiple_of(i * CONV_ROWS, CONV_ROWS)
            rows = pl.ds(r0, CONV_ROWS)
            ext = jnp.concatenate([_rows_before(x_ref, i, r0), x_ref[rows, :].astype(F32)], axis=0)
            pre = _taps_down(ext, w, 4)[0] + bias
            o_ref[rows, :] = pre * _sigmoid(pre)
            return carry

        lax.fori_loop(0, t // CONV_ROWS, step, 0)

    out, couts = _pcall(
        "ssm_conv_fwd", body, (D_XBC // CONV_CB,),
        [_pspec(t, OFF_XBC), pl.BlockSpec((4, CONV_CB), lambda j: (0, j)), pl.BlockSpec((1, CONV_CB), lambda j: (0, j))],
        pl.BlockSpec((t, CONV_CB), lambda j: (0, j)), jax.ShapeDtypeStruct((t, D_XBC), F32),
        (p, conv_w, conv_b), (), ("parallel",), comm)
    return out if comm is None else (out, couts)


def _ssm_conv_bwd(p, conv_w, conv_b, dxc, dp):
    t = p.shape[0]

    def body(x_ref, w_ref, b_ref, d_ref, dp_in, dx_ref, dw_ref, db_ref):
        del dp_in
        w = w_ref[...]
        bias = b_ref[...]
        n = t // CONV_ROWS

        def step(i, acc):
            r0 = pl.multiple_of(i * CONV_ROWS, CONV_ROWS)
            rows = pl.ds(r0, CONV_ROWS)
            x_cur = x_ref[rows, :].astype(F32)
            pre, shifted = _taps_down(jnp.concatenate([_rows_before(x_ref, i, r0), x_cur], axis=0), w, 4)
            pre = pre + bias
            dpre = d_ref[rows, :] * _silu_grad(pre)
            ext_after = jnp.concatenate([x_cur[CONV_ROWS - CONV_HALO:], _rows_after(x_ref, r0, t)], axis=0)
            pre_after = _taps_down(ext_after, w, 4)[0] + bias
            dpre_after = jnp.where(i < n - 1, _rows_after(d_ref, r0, t) * _silu_grad(pre_after), 0.0)
            dx_ref[rows, :] = _taps_up(jnp.concatenate([dpre, dpre_after], axis=0), w, 4).astype(BF)
            new = tuple(a + _fold8(dpre * s) for a, s in zip(acc[:4], shifted))
            return new + (acc[4] + _fold8(dpre),)

        zero = jnp.zeros((8, CONV_CB), F32)
        acc = lax.fori_loop(0, n, step, (zero,) * 5)
        for j in range(4):
            dw_ref[j:j + 1, :] = jnp.sum(acc[j], axis=0, keepdims=True)
        db_ref[...] = jnp.sum(acc[4], axis=0, keepdims=True)

    col = pl.BlockSpec((t, CONV_CB), lambda j: (0, j))
    wsp = pl.BlockSpec((4, CONV_CB), lambda j: (0, j))
    bsp = pl.BlockSpec((1, CONV_CB), lambda j: (0, j))
    return pl.pallas_call(
        body, name="ssm_conv_bwd", grid=(D_XBC // CONV_CB,),
        in_specs=[_pspec(t, OFF_XBC), wsp, bsp, col, pl.BlockSpec(memory_space=pl.ANY)],
        out_specs=[_pspec(t, OFF_XBC), wsp, bsp],
        out_shape=[jax.ShapeDtypeStruct(dp.shape, dp.dtype), jax.ShapeDtypeStruct((4, D_XBC), F32),
                   jax.ShapeDtypeStruct((1, D_XBC), F32)],
        input_output_aliases={4: 0},
        compiler_params=_params("parallel"))(p, conv_w, conv_b, dxc, dp)


DT_ROWS = 512


def _tri(lower):
    r = lax.broadcasted_iota(jnp.int32, (CHUNK, CHUNK), 0)
    c = lax.broadcasted_iota(jnp.int32, (CHUNK, CHUNK), 1)
    return jnp.where((r >= c) if lower else (r <= c), 1.0, 0.0).astype(F32)


def _dot_exact(a, b):
    return lax.dot_general(a, b, _DIMS["nn"], preferred_element_type=F32, precision=lax.Precision.HIGHEST)


def _dt_fwd(p, bias_pad, alog_pad):
    t = p.shape[0]

    def body(raw_ref, b_ref, al_ref, dt_ref, acs_ref):
        z = raw_ref[...] + b_ref[...]
        dt = jnp.maximum(z, 0.0) + jnp.log(1.0 + jnp.exp(-jnp.abs(z)))
        dt_ref[...] = dt
        a = dt * (-jnp.exp(al_ref[...]))
        tri = _tri(True)
        for k in range(DT_ROWS // CHUNK):
            acs_ref[k * CHUNK:(k + 1) * CHUNK, :] = _dot_exact(tri, a[k * CHUNK:(k + 1) * CHUNK, :])

    blk = pl.BlockSpec((DT_ROWS, DT_W), lambda i: (i, 0))
    vec = pl.BlockSpec((1, DT_W), lambda i: (0, 0))
    return pl.pallas_call(
        body, name="dt_fwd", grid=(t // DT_ROWS,),
        in_specs=[pl.BlockSpec((DT_ROWS, DT_W), lambda i: (i, OFF_DT // DT_W)), vec, vec],
        out_specs=[blk, blk], out_shape=[jax.ShapeDtypeStruct((t, DT_W), F32)] * 2,
        compiler_params=_params("parallel"))(p, bias_pad, alog_pad)


def _dt_bwd(p, bias_pad, alog_pad, dt, ddt, dacs, dp_gd):
    t = p.shape[0]

    def body(raw_ref, b_ref, al_ref, dt_ref, ddt_ref, dacs_ref, dp_in, draw_ref, db_ref, dal_ref):
        del dp_in
        i = pl.program_id(0)
        acoef = -jnp.exp(al_ref[...])
        triu = _tri(False)
        das = []
        for k in range(DT_ROWS // CHUNK):
            das.append(_dot_exact(triu, dacs_ref[k * CHUNK:(k + 1) * CHUNK, :]))
        da = jnp.concatenate(das, axis=0)
        dtv = dt_ref[...]
        ddt_tot = ddt_ref[...] + da * acoef
        lane = lax.broadcasted_iota(jnp.int32, (DT_ROWS, DT_W), 1)
        draw = jnp.where(lane < N_HEADS, ddt_tot * _sigmoid(raw_ref[...] + b_ref[...]), 0.0)
        draw_ref[...] = draw.astype(BF)
        pb = jnp.sum(draw, axis=0, keepdims=True)
        pa = jnp.sum(da * dtv * acoef, axis=0, keepdims=True)

        @pl.when(i == 0)
        def _():
            db_ref[...] = pb
            dal_ref[...] = pa

        @pl.when(i > 0)
        def _():
            db_ref[...] += pb
            dal_ref[...] += pa

    blk = pl.BlockSpec((DT_ROWS, DT_W), lambda i: (i, 0))
    vec = pl.BlockSpec((1, DT_W), lambda i: (0, 0))
    return pl.pallas_call(
        body, name="dt_bwd", grid=(t // DT_ROWS,),
        in_specs=[pl.BlockSpec((DT_ROWS, DT_W), lambda i: (i, OFF_DT // DT_W)), vec, vec, blk, blk, blk,
                  pl.BlockSpec(memory_space=pl.ANY)],
        out_specs=[pl.BlockSpec((DT_ROWS, DT_W), lambda i: (i, OFF_DT // DT_W)), vec, vec],
        out_shape=[jax.ShapeDtypeStruct(dp_gd.shape, dp_gd.dtype), jax.ShapeDtypeStruct((1, DT_W), F32),
                   jax.ShapeDtypeStruct((1, DT_W), F32)],
        input_output_aliases={6: 0},
        compiler_params=_params("arbitrary"))(p, bias_pad, alog_pad, dt, ddt, dacs, dp_gd)


def _split_dot(z, onehot, terms):
    out = None
    rest = z
    for _ in range(terms):
        piece = rest.astype(BF)
        part = _dot(piece, onehot)
        out = part if out is None else out + part
        rest = rest - piece.astype(F32)
    return out


def _spread_mat():
    row = lax.broadcasted_iota(jnp.int32, (DT_W, D_INNER), 0)
    lane = lax.broadcasted_iota(jnp.int32, (DT_W, D_INNER), 1)
    return jnp.where(row == lane // HEAD_DIM, 1.0, 0.0).astype(BF)


def _gather_mat():
    row = lax.broadcasted_iota(jnp.int32, (D_INNER, DT_W), 0)
    lane = lax.broadcasted_iota(jnp.int32, (D_INNER, DT_W), 1)
    return jnp.where(lane == row // HEAD_DIM, 1.0, 0.0).astype(BF)


def _ssd_masks():
    row = lax.broadcasted_iota(jnp.int32, (CHUNK, GROUP_W), 0)
    col = lax.broadcasted_iota(jnp.int32, (CHUNK, GROUP_W), 1) % HEAD_DIM
    brow = lax.broadcasted_iota(jnp.int32, (GROUP_W, GROUP_W), 0) // HEAD_DIM
    bcol = lax.broadcasted_iota(jnp.int32, (GROUP_W, GROUP_W), 1) // HEAD_DIM
    return row >= col, row == col, brow == bcol


def _stack4(v):
    return jnp.concatenate([v, v, v, v], axis=0)


def _fold4(v):
    return v[0:CHUNK] + v[CHUNK:2 * CHUNK] + v[2 * CHUNK:3 * CHUNK] + v[3 * CHUNK:4 * CHUNK]


def _ssd_group(xc_ref, wide_ref, g, tri, eye, blockdiag):
    gs = slice(GROUP_W * g, GROUP_W * (g + 1))
    xs_g = xc_ref[:, gs]
    b_g = xc_ref[:, D_INNER + D_STATE * g:D_INNER + D_STATE * (g + 1)].astype(BF)
    c_g = xc_ref[:, D_INNER + 1024 + D_STATE * g:D_INNER + 1024 + D_STATE * (g + 1)].astype(BF)
    acs_e, dt_e = wide_ref[0:CHUNK, gs], wide_ref[CHUNK:2 * CHUNK, gs]
    atot_e = acs_e[CHUNK - 1:CHUNK, :]
    acs_j = jnp.sum(jnp.where(eye, acs_e, 0.0), axis=0, keepdims=True)
    lmat = jnp.where(tri, jnp.exp(jnp.minimum(acs_e - acs_j, 0.0)), 0.0)
    b_t = _stack4(b_g)
    m = _dot(c_g, b_t, "nt") * lmat
    x_g = xs_g * dt_e
    xbd = jnp.where(blockdiag, _stack4(x_g), 0.0).astype(BF)
    return dict(gs=gs, xs=xs_g, b=b_g, c=c_g, b_t=b_t, dt=dt_e, e=jnp.exp(acs_e), dec=jnp.exp(atot_e - acs_e),
                eat=jnp.exp(atot_e), lmat=lmat, m=m, x=x_g, xbd=xbd)


def _ssd_fwd(xconv, dt, acs, d_exp, comm=None):
    t = xconv.shape[0]
    nc = t // CHUNK

    def body(xc_ref, dt_ref, acs_ref, d_ref, y_ref, hs_ref, state, wide):
        c = pl.program_id(0)

        @pl.when(c == 0)
        def _():
            state[...] = jnp.zeros_like(state)

        hs_ref[...] = state[...]
        tri, eye, blockdiag = _ssd_masks()
        wide[...] = _split_dot(jnp.concatenate([acs_ref[...], dt_ref[...]], axis=0), _spread_mat(), 3)
        for g in range(N_GROUPS):
            q = _ssd_group(xc_ref, wide, g, tri, eye, blockdiag)
            gs = q["gs"]
            h_t = state[:, gs]
            ydiag = _dot(q["m"].astype(BF), q["xbd"])
            yoff = _dot(q["c"], h_t.astype(BF)) * q["e"]
            y_ref[:, gs] = ydiag + yoff + d_ref[:, gs] * q["xs"]
            s_t = _dot(q["b"], (q["x"] * q["dec"]).astype(BF), "tn")
            state[:, gs] = q["eat"] * h_t + s_t

    blk = lambda w: pl.BlockSpec((CHUNK, w), lambda c: (c, 0))
    outs, couts = _pcall(
        "ssd_fwd", body, (nc,),
        [blk(D_XBC), blk(DT_W), blk(DT_W), pl.BlockSpec((1, D_INNER), lambda c: (0, 0))],
        [blk(D_INNER), pl.BlockSpec((None, D_STATE, D_INNER), lambda c: (c, 0, 0))],
        [jax.ShapeDtypeStruct((t, D_INNER), F32), jax.ShapeDtypeStruct((nc, D_STATE, D_INNER), F32)],
        (xconv, dt, acs, d_exp), [pltpu.VMEM((D_STATE, D_INNER), F32), pltpu.VMEM((2 * CHUNK, D_INNER), F32)],
        ("arbitrary",), comm)
    return outs if comm is None else (outs, couts)


def _ssd_bwd(xconv, dt, acs, d_exp, hsave, dy, comm=None):
    t = xconv.shape[0]
    nc = t // CHUNK

    def body(xc_ref, dt_ref, acs_ref, d_ref, hs_ref, dy_ref, dxc_ref, ddt_ref, dacs_ref, dd_ref, dstate, wide, per_head):
        c = pl.program_id(0)

        @pl.when(c == 0)
        def _():
            dstate[...] = jnp.zeros_like(dstate)
            dd_ref[...] = jnp.zeros_like(dd_ref)

        tri, eye, blockdiag = _ssd_masks()
        acsv = acs_ref[...]
        wide[...] = _split_dot(jnp.concatenate([acsv, dt_ref[...]], axis=0), _spread_mat(), 3)
        eat_heads = jnp.exp(acsv[CHUNK - 1:CHUNK, :])

        for g in range(N_GROUPS):
            q = _ssd_group(xc_ref, wide, g, tri, eye, blockdiag)
            gs, xs_g, b_g, c_g, m = q["gs"], q["xs"], q["b"], q["c"], q["m"]
            bs = slice(D_INNER + D_STATE * g, D_INNER + D_STATE * (g + 1))
            cs = slice(D_INNER + 1024 + D_STATE * g, D_INNER + 1024 + D_STATE * (g + 1))
            h_t = hs_ref[:, gs]
            h_b = h_t.astype(BF)
            dy_g = dy_ref[:, gs]
            dy_b = dy_g.astype(BF)
            ds_t = dstate[:, gs]
            ds_b = ds_t.astype(BF)

            yoff = _dot(c_g, h_b) * q["e"]
            edy = (q["e"] * dy_g).astype(BF)
            d_c = _dot(edy, h_b, "nt")
            d_ht = _dot(c_g, edy, "tn")
            bds = _dot(b_g, ds_b)
            xd = q["x"] * q["dec"]
            d_b = _dot(xd.astype(BF), ds_b, "nt")
            dm = _dot(dy_b, q["xbd"], "nt")
            cross = _dot(m.astype(BF), dy_b, "tn")
            dx_full = q["dec"] * bds + _fold4(jnp.where(blockdiag, cross, 0.0))
            dml = (dm * q["lmat"]).astype(BF)
            d_c = d_c + _dot(dml, q["b_t"])
            d_b = d_b + _fold4(_dot(dml, c_g, "tn"))
            w = dm * m
            q_dec = xd * bds
            z = w - jnp.where(eye, jnp.sum(w, axis=0, keepdims=True), 0.0) + dy_g * yoff - q_dec
            rows = jnp.concatenate(
                [jnp.sum(q_dec, axis=0, keepdims=True), jnp.sum(ds_t * h_t, axis=0, keepdims=True),
                 jnp.zeros((6, GROUP_W), F32)], axis=0)
            per_head[:, gs] = jnp.concatenate([z, dx_full * xs_g, rows], axis=0)
            dxc_ref[:, cs] = d_c
            dxc_ref[:, bs] = d_b
            dxc_ref[:, gs] = dx_full * q["dt"] + d_ref[:, gs] * dy_g
            dd_ref[:, gs] += jnp.sum(dy_g * xs_g, axis=0, keepdims=True)
            dstate[:, gs] = q["eat"] * ds_t + d_ht

        seg = _split_dot(per_head[...], _gather_mat(), 2)
        datot = seg[2 * CHUNK:2 * CHUNK + 1] + eat_heads * seg[2 * CHUNK + 1:2 * CHUNK + 2]
        rowi = lax.broadcasted_iota(jnp.int32, (CHUNK, DT_W), 0)
        ddt_ref[...] = seg[CHUNK:2 * CHUNK]
        dacs_ref[...] = seg[0:CHUNK] + jnp.where(rowi == CHUNK - 1, datot, 0.0)

    rev = lambda w: pl.BlockSpec((CHUNK, w), lambda c: (nc - 1 - c, 0))
    vec = pl.BlockSpec((1, D_INNER), lambda c: (0, 0))
    outs, couts = _pcall(
        "ssd_bwd", body, (nc,),
        [rev(D_XBC), rev(DT_W), rev(DT_W), vec,
         pl.BlockSpec((None, D_STATE, D_INNER), lambda c: (nc - 1 - c, 0, 0)), rev(D_INNER)],
        [rev(D_XBC), rev(DT_W), rev(DT_W), vec],
        [jax.ShapeDtypeStruct((t, D_XBC), F32), jax.ShapeDtypeStruct((t, DT_W), F32),
         jax.ShapeDtypeStruct((t, DT_W), F32), jax.ShapeDtypeStruct((1, D_INNER), F32)],
        (xconv, dt, acs, d_exp, hsave, dy),
        [pltpu.VMEM((D_STATE, D_INNER), F32), pltpu.VMEM((2 * CHUNK, D_INNER), F32),
         pltpu.VMEM((2 * CHUNK + 8, D_INNER), F32)], ("arbitrary",), comm)
    return outs if comm is None else (outs, couts)


GN_CB = 1024
GN_GROUPS = GN_CB // GROUP_W


def _gnorm_fwd(y, p, w, comm=None):
    t = y.shape[0]
    zoff = OFF_Z // GN_CB

    def body(y_ref, z_ref, w_ref, o_ref):
        for g in range(GN_GROUPS):
            gs = slice(GROUP_W * g, GROUP_W * (g + 1))
            z = z_ref[:, gs].astype(F32)
            yf = y_ref[:, gs] * (z * _sigmoid(z))
            rstd = lax.rsqrt(jnp.mean(yf * yf, axis=-1, keepdims=True) + NORM_EPS)
            o_ref[:, gs] = (yf * rstd * w_ref[:, gs]).astype(BF)

    blk = pl.BlockSpec((TE, GN_CB), lambda i, j: (i, j))
    out, couts = _pcall(
        "gnorm_fwd", body, (t // TE, D_INNER // GN_CB),
        [blk, pl.BlockSpec((TE, GN_CB), lambda i, j: (i, zoff + j)), pl.BlockSpec((1, GN_CB), lambda i, j: (0, j))],
        blk, jax.ShapeDtypeStruct((t, D_INNER), BF), (y, p, w), (), ("parallel", "parallel"), comm)
    return out if comm is None else (out, couts)


def _gnorm_bwd(y, p, w, dyn, comm=None):
    t = y.shape[0]
    zoff = OFF_Z // GN_CB

    def body(y_ref, z_ref, w_ref, dn_ref, dy_ref, dz_ref, dw_ref):
        i = pl.program_id(1)
        for g in range(GN_GROUPS):
            gs = slice(GROUP_W * g, GROUP_W * (g + 1))
            z = z_ref[:, gs].astype(F32)
            yv = y_ref[:, gs]
            s = _sigmoid(z)
            sil = z * s
            yf = yv * sil
            rstd = lax.rsqrt(jnp.mean(yf * yf, axis=-1, keepdims=True) + NORM_EPS)
            xhat = yf * rstd
            dn = dn_ref[:, gs]
            wd = dn * w_ref[:, gs]
            proj = jnp.mean(wd * xhat, axis=-1, keepdims=True)
            dyf = rstd * (wd - xhat * proj)
            dy_ref[:, gs] = dyf * sil
            dz_ref[:, gs] = (dyf * yv * (s * (1.0 + z * (1.0 - s)))).astype(BF)
            part = jnp.sum(dn * xhat, axis=0, keepdims=True)

            @pl.when(i == 0)
            def _():
                dw_ref[:, gs] = part

            @pl.when(i > 0)
            def _():
                dw_ref[:, gs] += part

    blk = pl.BlockSpec((TE, GN_CB), lambda j, i: (i, j))
    vec = pl.BlockSpec((1, GN_CB), lambda j, i: (0, j))
    outs, couts = _pcall(
        "gnorm_bwd", body, (D_INNER // GN_CB, t // TE),
        [blk, pl.BlockSpec((TE, GN_CB), lambda j, i: (i, zoff + j)), vec, blk],
        [blk, pl.BlockSpec((TE, GN_CB), lambda j, i: (i, zoff + j)), vec],
        [jax.ShapeDtypeStruct((t, D_INNER), F32), jax.ShapeDtypeStruct((t, N_MAIN), BF),
         jax.ShapeDtypeStruct((1, D_INNER), F32)],
        (y, p, w, dyn), (), ("parallel", "arbitrary"), comm)
    return outs if comm is None else (outs, couts)


MERGE_CB = 512


def _merge_fwd(p, ya, yb):
    t = ya.shape[0]

    def body(ga_ref, gb_ref, ya_ref, yb_ref, o_ref):
        o_ref[...] = (_sigmoid(ga_ref[...]) * ya_ref[...] + _sigmoid(gb_ref[...]) * yb_ref[...]).astype(BF)

    blk = pl.BlockSpec((TE, MERGE_CB), lambda i, j: (i, j))
    return pl.pallas_call(
        body, name="merge_fwd", grid=(t // TE, D_MODEL // MERGE_CB),
        in_specs=[pl.BlockSpec((TE, MERGE_CB), lambda i, j: (i, 2 * j)),
                  pl.BlockSpec((TE, MERGE_CB), lambda i, j: (i, 2 * j + 1)), blk, blk],
        out_specs=blk, out_shape=jax.ShapeDtypeStruct((t, D_MODEL), BF),
        compiler_params=_params("parallel", "parallel"))(p, p, ya, yb)


def _merge_bwd(p, ya, yb, dm):
    t = ya.shape[0]

    def body(ga_ref, gb_ref, ya_ref, yb_ref, dm_ref, dg_ref, dya_ref, dyb_ref):
        d = dm_ref[...]
        sa = _sigmoid(ga_ref[...])
        sb = _sigmoid(gb_ref[...])
        dg_ref[:, 0:MERGE_CB] = (d * ya_ref[...] * sa * (1.0 - sa)).astype(BF)
        dg_ref[:, MERGE_CB:2 * MERGE_CB] = (d * yb_ref[...] * sb * (1.0 - sb)).astype(BF)
        dya_ref[...] = (d * sa).astype(BF)
        dyb_ref[...] = (d * sb).astype(BF)

    blk = pl.BlockSpec((TE, MERGE_CB), lambda i, j: (i, j))
    return pl.pallas_call(
        body, name="merge_bwd", grid=(t // TE, D_MODEL // MERGE_CB),
        in_specs=[pl.BlockSpec((TE, MERGE_CB), lambda i, j: (i, 2 * j)),
                  pl.BlockSpec((TE, MERGE_CB), lambda i, j: (i, 2 * j + 1)), blk, blk, blk],
        out_specs=[pl.BlockSpec((TE, 2 * MERGE_CB), lambda i, j: (i, j)), blk, blk],
        out_shape=[jax.ShapeDtypeStruct((t, N_GD), BF)] + [jax.ShapeDtypeStruct((t, D_MODEL), BF)] * 2,
        compiler_params=_params("parallel", "parallel"))(p, p, ya, yb, dm)


def _adamw(name, parts, w, m, v, comm=None):
    r, c = w.shape
    tr = _row_tile(r)
    tc = ADAM_COL_TILE if (tr == r and r > 512 and c % ADAM_COL_TILE == 0) else c
    n_parts = parts.shape[0]
    bc1 = 1.0 - ADAM_B1 ** ADAM_STEP
    bc2 = 1.0 - ADAM_B2 ** ADAM_STEP

    def body(p_ref, w_ref, m_ref, v_ref, g_ref, d_ref, nm_ref, nv_ref):
        g = p_ref[0].astype(F32)
        for k in range(1, n_parts):
            g = g + p_ref[k].astype(F32)
        nm = ADAM_B1 * m_ref[...] + (1.0 - ADAM_B1) * g
        nv = ADAM_B2 * v_ref[...] + (1.0 - ADAM_B2) * (g * g)
        g_ref[...] = g
        nm_ref[...] = nm
        nv_ref[...] = nv
        d_ref[...] = -ADAM_LR * ((nm / bc1) / (jnp.sqrt(nv / bc2) + ADAM_EPS) + ADAM_WD * w_ref[...])

    blk = pl.BlockSpec((tr, tc), lambda i, j: (i, j))
    outs, couts = _pcall(
        name, body, (r // tr, c // tc),
        [pl.BlockSpec((n_parts, tr, tc), lambda i, j: (0, i, j)), blk, blk, blk], [blk] * 4,
        [jax.ShapeDtypeStruct((r, c), F32)] * 4, (parts, w, m, v), (), ("parallel", "parallel"), comm)
    return outs if comm is None else (outs, couts)


def _pad_lanes(v, width):
    return jnp.pad(v, ((0, 0), (0, width - v.shape[1])))


def _reduce_start(slots, host):
    outs, sib = host(_pair_comm([a for _, a in slots]))
    sums = [(n, _add_pairs("pairsum_" + n, a, b)) for (n, a), b in zip(slots, sib)]
    return outs, sums


def _train_step(x, target, shard, rep):
    gdt = BF
    recv = {}
    (got,) = _comm_call("gather_ffn1_in", _gather_comm([shard["ffn1_w_in"]], [True]))
    w1_in = got.reshape(2 * D_FF, D_MODEL)
    h1 = _rms_fwd("rms1_fwd", x, rep["ffn1_norm"])
    gu1, got = _mm_nt("ffn1_in", h1, w1_in, tn=FF_HALF, out_dtype=BF, comm=_gather_comm(
        [shard["ffn1_w_out"], shard["w_in"], shard["short_conv_w"], shard["ssm_conv_w"]]))
    w1_out = got[0].reshape(D_FF, D_MODEL)
    w_in_t = got[1].reshape(N_IN, D_MODEL)
    short_conv_w = got[2].transpose(1, 0, 2).reshape(3, D_MODEL)
    ssm_conv_w = got[3].transpose(1, 0, 2).reshape(4, D_XBC)
    act1 = _swiglu_fwd("swiglu1_fwd", gu1)
    x1 = _mm_nn("ffn1_out", act1, w1_out, res=x, alpha=0.5)
    ga0 = N_MAIN + N_HEADS
    gb0 = ga0 + D_MODEL
    half = D_MODEL // 2
    w_gd = jnp.concatenate(
        [w_in_t[ga0:ga0 + half], w_in_t[gb0:gb0 + half], w_in_t[ga0 + half:gb0], w_in_t[gb0 + half:],
         w_in_t[N_MAIN:N_MAIN + N_HEADS], jnp.zeros((DT_W - N_HEADS, D_MODEL), BF)], axis=0)
    w_mix_perm = w_in_t[0:3 * D_MODEL].reshape(3, 4, CONV_CB, D_MODEL).transpose(1, 0, 2, 3).reshape(3 * D_MODEL, D_MODEL)

    h2 = _rms_fwd("rms2_fwd", x1, rep["mix_norm"])
    p, got = _mm_nt("proj_main", h2, w_in_t, n=N_MAIN, tn=1024, out_dtype=BF, comm=_gather_comm(
        [shard["short_w_out"], shard["ssm_w_out"], shard["w_out"]]))
    p_gd = _mm_nt("proj_gd", h2, w_gd)
    short_w_out = got[0].reshape(D_MODEL, D_MODEL)
    ssm_w_out = got[1].reshape(D_INNER, D_MODEL)
    w_out = got[2].reshape(D_MODEL, D_MODEL)
    ya_in = _mix_a_fwd(p, short_conv_w)
    y_a = _mm_nn("short_out", ya_in, short_w_out)
    xconv, (got,) = _ssm_conv_fwd(p, ssm_conv_w, rep["ssm_conv_b"], comm=_gather_comm([shard["ffn2_w_out"]]))
    w2_out = got.reshape(D_FF, D_MODEL)
    dt, acs = _dt_fwd(p_gd, rep["dt_bias_pad"], rep["a_log_pad"])
    (y_ssm, hsave), (got,) = _ssd_fwd(xconv, dt, acs, rep["d_exp"], comm=_gather_comm([shard["ffn2_w_in"]], [True]))
    w2_in = got.reshape(2 * D_FF, D_MODEL)
    yn = _gnorm_fwd(y_ssm, p, rep["ssm_norm"])
    y_b = _mm_nn("ssm_out", yn, ssm_w_out, tk=1024)
    merged = _merge_fwd(p_gd, y_a, y_b)
    x2 = _mm_nn("mix_out", merged, w_out, res=x1)

    h3 = _rms_fwd("rms3_fwd", x2, rep["ffn2_norm"])
    gu2 = _mm_nt("ffn2_in", h3, w2_in, tn=FF_HALF, out_dtype=BF)
    act2 = _swiglu_fwd("swiglu2_fwd", gu2)
    x3 = _mm_nn("ffn2_out", act2, w2_out, res=x2, alpha=0.5)

    loss, dx3, dx3h, g_final = _final_loss(x3, rep["final_norm"], target)

    small = {"final_norm": g_final}
    dact2 = _mm_nt("ffn2_out_bwd_act", dx3h, w2_out, out_dtype=BF)
    g_w2_out = _mm_tn("ffn2_out_bwd_w", act2, dx3h, gdt, tm=FF_HALF)
    dgu2 = _swiglu_bwd("swiglu2_bwd", gu2, dact2)
    g_w2_in = _mm_tn("ffn2_in_bwd_w", dgu2, h3, gdt, tm=FF_HALF)
    dh3 = _mm_nn("ffn2_in_bwd_h", dgu2, w2_in, tk=FF_HALF)
    dx2, dx2b, small["ffn2_norm"] = _rms_bwd("rms3_bwd", x2, rep["ffn2_norm"], dh3, dx3, 1.0)

    dmerged = _mm_nt("mix_out_bwd_x", dx2b, w_out)
    g_w_out = _mm_tn("mix_out_bwd_w", merged, dx2b, gdt)
    dp_gd, dya, dyb = _merge_bwd(p_gd, y_a, y_b, dmerged)

    dya_in = _mm_nt("short_out_bwd_x", dya, short_w_out)
    g_short_w_out = _mm_tn("short_out_bwd_w", ya_in, dya, gdt)

    dyn = _mm_nt("ssm_out_bwd_x", dyb, ssm_w_out)
    g_ssm_w_out = _mm_tn("ssm_out_bwd_w", yn, dyb, gdt)
    late = [("ffn2_w_out", g_w2_out.reshape(N_DEV, FF_SHARD // 2, D_MODEL)),
            ("ffn2_w_in", g_w2_in.reshape(N_DEV, FF_SHARD, D_MODEL)),
            ("w_out", g_w_out.reshape(N_DEV, -1, D_MODEL)), ("short_w_out", g_short_w_out.reshape(N_DEV, -1, D_MODEL)),
            ("ssm_w_out", g_ssm_w_out.reshape(N_DEV, -1, D_MODEL))]
    (dy_ssm, dp, small["ssm_norm"]), sums = _reduce_start(
        late, lambda comm: _gnorm_bwd(y_ssm, p, rep["ssm_norm"], dyn, comm=comm))
    dp, g_short_conv = _mix_a_bwd(p, short_conv_w, dya_in, dp)
    (dxconv, ddt, dacs, dd_lane), got = _ssd_bwd(
        xconv, dt, acs, rep["d_exp"], hsave, dy_ssm,
        comm=_chip_comm([a for _, a in sums], [n == "ffn2_w_in" for n, _ in sums]))
    recv.update({n: a for (n, _), a in zip(sums, got)})
    small["ssm_D"] = dd_lane.reshape(N_HEADS, HEAD_DIM).sum(axis=1)[None, :]
    dp, g_ssm_conv, small["ssm_conv_b"] = _ssm_conv_bwd(p, ssm_conv_w, rep["ssm_conv_b"], dxconv, dp)
    dp_gd, dbias, dalog = _dt_bwd(p_gd, rep["dt_bias_pad"], rep["a_log_pad"], dt, ddt, dacs, dp_gd)
    small["ssm_dt_bias"] = dbias[:, :N_HEADS]
    small["ssm_A_log"] = dalog[:, :N_HEADS]

    g_main = _mm_tn("proj_main_bwd_w", dp, h2, gdt, tm=1024)
    g_gd = _mm_tn("proj_gd_bwd_w", dp_gd, h2, gdt)
    g_mix = g_main[0:3 * D_MODEL].reshape(4, 3, CONV_CB, D_MODEL).transpose(1, 0, 2, 3).reshape(3 * D_MODEL, D_MODEL)
    g_in_t = jnp.concatenate(
        [g_mix, g_main[3 * D_MODEL:], g_gd[2 * D_MODEL:2 * D_MODEL + N_HEADS],
         g_gd[0:half], g_gd[2 * half:3 * half], g_gd[half:2 * half], g_gd[3 * half:4 * half]], axis=0).reshape(
        N_DEV, IN_SHARD, D_MODEL)
    dh2, w_sums = _reduce_start(
        [("w_in", g_in_t)], lambda comm: _mm_nn("proj_mix_bwd_x", dp, w_mix_perm, tk=1024, kk=3 * D_MODEL, comm=comm))
    w_sum = w_sums[0][1]

    def w_piece(i):
        return _chip_comm([w_sum], rows=[W_GRAD_ROW_CUTS[i]])

    dh2, got0 = _mm_nn("proj_rest_bwd_x", dp, w_in_t, tk=1024, kk=N_MAIN - 3 * D_MODEL, a_off=3, b_off=3, res=dh2,
                       comm=w_piece(0))
    dh2, got1 = _mm_nn("proj_gd_bwd_x", dp_gd, w_gd, res=dh2, comm=w_piece(1))
    (dx1, dx1h, small["mix_norm"]), got2 = _rms_bwd("rms2_bwd", x1, rep["mix_norm"], dh2, dx2, 0.5, comm=w_piece(2))
    g_w1_out, got3 = _mm_tn("ffn1_out_bwd_w", act1, dx1h, gdt, tm=FF_HALF, comm=w_piece(3))
    rest = [("ffn1_w_out", g_w1_out.reshape(N_DEV, FF_SHARD // 2, D_MODEL)),
            ("short_conv_w", g_short_conv.reshape(3, N_DEV, -1).transpose(1, 0, 2)),
            ("ssm_conv_w", g_ssm_conv.reshape(4, N_DEV, -1).transpose(1, 0, 2))]
    dact1, got = _mm_nt("ffn1_out_bwd_act", dx1h, w1_out, out_dtype=BF,
                        comm=_join_comm(w_piece(4), _pair_comm([a for _, a in rest])))
    got4, sib = got[0], got[1:]
    rest_sums = [(n, _add_pairs("pairsum_" + n, a, b)) for (n, a), b in zip(rest, sib)]
    recv["w_in"] = jnp.concatenate([got0[0], got1[0], got2[0], got3[0], got4], axis=1)
    dgu1, got = _swiglu_bwd("swiglu1_bwd", gu1, dact1, comm=_chip_comm([a for _, a in rest_sums]))
    recv.update({n: a for (n, _), a in zip(rest_sums, got)})

    def part(tag, width, off, comm=None):
        out = _mm_tn("ffn1_in_bwd_w_" + tag, dgu1, h1, gdt, tm=FF_HALF, n=width, col_off=off, comm=comm)
        g, couts = (out, None) if comm is None else out
        return g.reshape(N_DEV, FF_SHARD, width), couts

    g_a, _ = part("a", 384, 0)
    g_b, sib = part("b", 384, 1, _pair_comm([g_a]))
    sum_a = _add_pairs("pairsum_ffn1_w_in_a", g_a, sib[0])
    g_c, (recv_a, sib_b) = part("c", 256, 3, _join_comm(_chip_comm([sum_a], [True]), _pair_comm([g_b])))
    sum_b = _add_pairs("pairsum_ffn1_w_in_b", g_b, sib_b)
    dh1, (recv_b, sib_c) = _mm_nn("ffn1_in_bwd_h", dgu1, w1_in, tk=FF_HALF,
                                  comm=_join_comm(_chip_comm([sum_b], [True]), _pair_comm([g_c])))
    sum_c = _add_pairs("pairsum_ffn1_w_in_c", g_c, sib_c)
    (dx0, _, small["ffn1_norm"]), (recv_c,) = _rms_bwd("rms1_bwd", x, rep["ffn1_norm"], dh1, dx1, 1.0,
                                                        comm=_chip_comm([sum_c], [True]))
    recv["ffn1_w_in"] = jnp.concatenate([recv_a, recv_b, recv_c], axis=2)
    return dx0, recv, _pack_small(small, loss[:, 0:1])


_SMALL = [("ffn1_norm", 1024), ("mix_norm", 1024), ("ssm_conv_b", 4096), ("ssm_dt_bias", 32), ("ssm_A_log", 32),
          ("ssm_D", 32), ("ssm_norm", 2048), ("ffn2_norm", 1024), ("final_norm", 1024)]
SMALL_W = 10368


def _pack_small(d, loss=None):
    parts = [d[n].reshape(1, -1).astype(F32) for n, _ in _SMALL]
    used = sum(sz for _, sz in _SMALL)
    tail = jnp.zeros((1, SMALL_W - used), F32)
    if loss is not None:
        tail = tail.at[:, 0:1].set(loss)
    return jnp.concatenate(parts + [tail], axis=1)


def _adamw_small(parts, w, m, v):
    n_par = len(_SMALL)
    bc1 = 1.0 - ADAM_B1 ** ADAM_STEP
    bc2 = 1.0 - ADAM_B2 ** ADAM_STEP
    used = sum(sz for _, sz in _SMALL)

    def body(*refs):
        p_ref = refs[0]
        ins = refs[1:1 + 3 * n_par]
        outs = refs[1 + 3 * n_par:]
        g_all = p_ref[0]
        for k in range(1, N_DEV):
            g_all = g_all + p_ref[k]
        off = 0
        for i, (_, sz) in enumerate(_SMALL):
            g = g_all[:, off:off + sz]
            w_ref, m_ref, v_ref = ins[3 * i:3 * i + 3]
            nm = ADAM_B1 * m_ref[...] + (1.0 - ADAM_B1) * g
            nv = ADAM_B2 * v_ref[...] + (1.0 - ADAM_B2) * (g * g)
            outs[4 * i][...] = g
            outs[4 * i + 1][...] = -ADAM_LR * ((nm / bc1) / (jnp.sqrt(nv / bc2) + ADAM_EPS) + ADAM_WD * w_ref[...])
            outs[4 * i + 2][...] = nm
            outs[4 * i + 3][...] = nv
            off += sz
        outs[4 * n_par][...] = g_all[:, used:SMALL_W]

    args = [parts]
    out_shape = []
    for name, sz in _SMALL:
        args += [w[name], m[name], v[name]]
        out_shape += [jax.ShapeDtypeStruct((1, sz), F32)] * 4
    out_shape.append(jax.ShapeDtypeStruct((1, SMALL_W - used), F32))
    res = pl.pallas_call(body, name="adamw_small", out_shape=out_shape,
                         compiler_params=pltpu.CompilerParams(vmem_limit_bytes=VMEM_LIMIT_V7X))(*args)
    return {name: tuple(res[4 * i:4 * i + 4]) for i, (name, _) in enumerate(_SMALL)}, res[-1]


_SHARDED = ["ffn1_w_in", "ffn1_w_out", "w_in", "short_conv_w", "short_w_out", "ssm_conv_w", "ssm_w_out", "w_out",
            "ffn2_w_in", "ffn2_w_out"]
_TRANSPOSED = ("ffn1_w_in", "w_in", "ffn2_w_in")
_ORDER = ["ffn1_norm", "ffn1_w_in", "ffn1_w_out", "mix_norm", "w_in", "short_conv_w", "short_w_out", "ssm_conv_w",
          "ssm_conv_b", "ssm_dt_bias", "ssm_A_log", "ssm_D", "ssm_norm", "ssm_w_out", "w_out", "ffn2_norm",
          "ffn2_w_in", "ffn2_w_out", "final_norm"]


def kernel(x, ffn1_norm, ffn1_w_in, ffn1_w_out, mix_norm, w_in, short_conv_w, short_w_out, ssm_conv_w, ssm_conv_b, ssm_dt_bias, ssm_A_log, ssm_D, ssm_norm, ssm_w_out, w_out, ffn2_norm, ffn2_w_in, ffn2_w_out, final_norm, loss_target, m_ffn1_norm, m_ffn1_w_in, m_ffn1_w_out, m_mix_norm, m_w_in, m_short_conv_w, m_short_w_out, m_ssm_conv_w, m_ssm_conv_b, m_ssm_dt_bias, m_ssm_A_log, m_ssm_D, m_ssm_norm, m_ssm_w_out, m_w_out, m_ffn2_norm, m_ffn2_w_in, m_ffn2_w_out, m_final_norm, v_ffn1_norm, v_ffn1_w_in, v_ffn1_w_out, v_mix_norm, v_w_in, v_short_conv_w, v_short_w_out, v_ssm_conv_w, v_ssm_conv_b, v_ssm_dt_bias, v_ssm_A_log, v_ssm_D, v_ssm_norm, v_ssm_w_out, v_w_out, v_ffn2_norm, v_ffn2_w_in, v_ffn2_w_out, v_final_norm):
    w = dict(ffn1_norm=ffn1_norm, ffn1_w_in=ffn1_w_in, ffn1_w_out=ffn1_w_out, mix_norm=mix_norm, w_in=w_in,
             short_conv_w=short_conv_w, short_w_out=short_w_out, ssm_conv_w=ssm_conv_w, ssm_conv_b=ssm_conv_b,
             ssm_dt_bias=ssm_dt_bias, ssm_A_log=ssm_A_log, ssm_D=ssm_D, ssm_norm=ssm_norm, ssm_w_out=ssm_w_out,
             w_out=w_out, ffn2_norm=ffn2_norm, ffn2_w_in=ffn2_w_in, ffn2_w_out=ffn2_w_out, final_norm=final_norm)
    m = dict(ffn1_norm=m_ffn1_norm, ffn1_w_in=m_ffn1_w_in, ffn1_w_out=m_ffn1_w_out, mix_norm=m_mix_norm, w_in=m_w_in,
             short_conv_w=m_short_conv_w, short_w_out=m_short_w_out, ssm_conv_w=m_ssm_conv_w,
             ssm_conv_b=m_ssm_conv_b, ssm_dt_bias=m_ssm_dt_bias, ssm_A_log=m_ssm_A_log, ssm_D=m_ssm_D,
             ssm_norm=m_ssm_norm, ssm_w_out=m_ssm_w_out, w_out=m_w_out, ffn2_norm=m_ffn2_norm,
             ffn2_w_in=m_ffn2_w_in, ffn2_w_out=m_ffn2_w_out, final_norm=m_final_norm)
    v = dict(ffn1_norm=v_ffn1_norm, ffn1_w_in=v_ffn1_w_in, ffn1_w_out=v_ffn1_w_out, mix_norm=v_mix_norm, w_in=v_w_in,
             short_conv_w=v_short_conv_w, short_w_out=v_short_w_out, ssm_conv_w=v_ssm_conv_w,
             ssm_conv_b=v_ssm_conv_b, ssm_dt_bias=v_ssm_dt_bias, ssm_A_log=v_ssm_A_log, ssm_D=v_ssm_D,
             ssm_norm=v_ssm_norm, ssm_w_out=v_ssm_w_out, w_out=v_w_out, ffn2_norm=v_ffn2_norm,
             ffn2_w_in=v_ffn2_w_in, ffn2_w_out=v_ffn2_w_out, final_norm=v_final_norm)
    shapes = {n: w[n].shape for n in _ORDER}

    def local(d, n):
        return d[n][0].T if n in _TRANSPOSED else d[n][0]

    shard = {n: local(w, n) for n in _SHARDED}

    wire = {n: (shard[n] if n in ("short_conv_w", "ssm_conv_w") else shard[n].astype(BF)) for n in _SHARDED}
    rep = {
        "ffn1_norm": ffn1_norm, "mix_norm": mix_norm, "ffn2_norm": ffn2_norm, "ssm_norm": ssm_norm,
        "ssm_conv_b": ssm_conv_b, "final_norm": final_norm.reshape(1, D_MODEL),
        "dt_bias_pad": _pad_lanes(ssm_dt_bias, DT_W), "a_log_pad": _pad_lanes(ssm_A_log, DT_W),
        "d_exp": jnp.repeat(ssm_D, HEAD_DIM, axis=1),
    }
    grad_x, parts, packed = _train_step(x[0], loss_target[0], wire, rep)

    out_g, out_d, out_m, out_v = {}, {}, {}, {}
    for n in _SHARDED:
        if n == "ssm_w_out":
            res, (small_parts,) = _adamw("adamw_" + n, parts[n], shard[n], local(m, n), local(v, n),
                                         comm=_gather_comm([packed]))
        else:
            res = _adamw("adamw_" + n, parts[n], shard[n], local(m, n), local(v, n))
        out_g[n], out_d[n], out_m[n], out_v[n] = [(r.T if n in _TRANSPOSED else r).reshape(shapes[n]) for r in res]
    row = lambda d: {n: d[n].reshape(1, -1) for n, _ in _SMALL}
    sres, loss_row = _adamw_small(small_parts, row(w), row(m), row(v))
    for n, _ in _SMALL:
        out_g[n], out_d[n], out_m[n], out_v[n] = [r.reshape(shapes[n]) for r in sres[n]]
    loss = loss_row[0, 0]
    return (loss, grad_x[None], *[out_g[n] for n in _ORDER], *[out_d[n] for n in _ORDER],
            *[out_m[n] for n in _ORDER], *[out_v[n] for n in _ORDER])
```

```python
import functools

import jax
import jax.numpy as jnp
from jax import lax
from jax.experimental import pallas as pl
from jax.experimental.pallas import tpu as pltpu

F32 = jnp.float32
BF = jnp.bfloat16

N_DEV = 8
D_MODEL = 1024
D_FF = 2816
D_INNER = 2048
D_XBC = 4096
N_HEADS = 32
HEAD_DIM = 64
N_GROUPS = 8
D_STATE = 128
CHUNK = 64
GROUP_W = D_INNER // N_GROUPS
HEADS_PER_GROUP = N_HEADS // N_GROUPS
NORM_EPS = 1e-5
N_IN = 11296
FF_SHARD = 2 * D_FF // N_DEV
FF_HALF = D_FF // 2
IN_SHARD = N_IN // N_DEV

OFF_B, OFF_C, OFF_XA, OFF_Z, OFF_XBC = 0, 1024, 2048, 3072, 5120
N_MAIN = 9216
OFF_DT = 2048
DT_W = 128
N_GD = 2048 + DT_W
W_GRAD_ROW_CUTS = [(0, 512), (512, 720), (720, 896), (896, 1152), (1152, 1412)]

ADAM_LR, ADAM_B1, ADAM_B2, ADAM_EPS, ADAM_WD, ADAM_STEP = 0.001, 0.9, 0.999, 1e-08, 0.01, 10

VMEM_LIMIT_V7X = 56 * 1024 * 1024
TM = 1024
TN_MAX_TOKENS = 2048
TE = 512
ADAM_COL_TILE = 256
GATHER_PIECES = 4
GATHER_PIECE_MIN_ROWS = 512


def _params(*sem):
    return pltpu.CompilerParams(dimension_semantics=sem, vmem_limit_bytes=VMEM_LIMIT_V7X)


_DIMS = {
    "nn": (((1,), (0,)), ((), ())),
    "nt": (((1,), (1,)), ((), ())),
    "tn": (((0,), (0,)), ((), ())),
}


def _dot(a, b, mode="nn"):
    return lax.dot_general(a, b, _DIMS[mode], preferred_element_type=F32)


def _sigmoid(x):
    return 1.0 / (1.0 + jnp.exp(-x))


class _Comm:
    def __init__(self, inputs, out_shapes, sems, start, finish):
        self.inputs, self.out_shapes, self.sems, self.start, self.finish = inputs, out_shapes, sems, start, finish


def _pcall(name, body, grid, in_specs, out_specs, out_shape, args, scratch=(), sem=None, comm=None, aliases=None):
    single = not isinstance(out_shape, (list, tuple))
    out_shapes = [out_shape] if single else list(out_shape)
    out_specs = [out_specs] if single else list(out_specs)
    n_in, n_out, n_scr = len(args), len(out_shapes), len(scratch)
    aliases = {} if aliases is None else aliases
    if comm is None:
        res = pl.pallas_call(
            body, name=name, grid=grid, in_specs=list(in_specs), out_specs=out_specs, out_shape=out_shapes,
            scratch_shapes=list(scratch), input_output_aliases=aliases, compiler_params=_params(*sem))(*args)
        return (res[0] if single else res), []
    nci, nco = len(comm.inputs), len(comm.out_shapes)

    def wrapped(*refs):
        a = refs[:n_in]
        ci = refs[n_in:n_in + nci]
        o0 = n_in + nci
        o = refs[o0:o0 + n_out]
        co = refs[o0 + n_out:o0 + n_out + nco]
        s0 = o0 + n_out + nco
        s = refs[s0:s0 + n_scr]
        cs = refs[s0 + n_scr:]
        pids = [pl.program_id(i) for i in range(len(grid))]
        first = functools.reduce(jnp.logical_and, [p == 0 for p in pids])
        last = functools.reduce(jnp.logical_and, [p == g - 1 for p, g in zip(pids, grid)])

        @pl.when(first)
        def _():
            comm.start(ci, co, cs)

        body(*a, *o, *s)

        @pl.when(last)
        def _():
            comm.finish(ci, co, cs)

    any_spec = pl.BlockSpec(memory_space=pl.ANY)
    res = pl.pallas_call(
        wrapped, name=name, grid=grid, in_specs=list(in_specs) + [any_spec] * nci,
        out_specs=out_specs + [any_spec] * nco, out_shape=out_shapes + list(comm.out_shapes),
        scratch_shapes=list(scratch) + list(comm.sems), input_output_aliases=aliases,
        compiler_params=_params(*(("arbitrary",) * len(grid))))(*args, *comm.inputs)
    core = res[:n_out]
    return (core[0] if single else core), list(res[n_out:])


def _comm_call(name, comm):
    nci, nco = len(comm.inputs), len(comm.out_shapes)

    def body(*refs):
        ci, co, cs = refs[:nci], refs[nci:nci + nco], refs[nci + nco:]
        comm.start(ci, co, cs)
        comm.finish(ci, co, cs)

    any_spec = pl.BlockSpec(memory_space=pl.ANY)
    return pl.pallas_call(
        body, name=name, in_specs=[any_spec] * nci, out_specs=[any_spec] * nco, out_shape=list(comm.out_shapes),
        scratch_shapes=list(comm.sems), compiler_params=pltpu.CompilerParams(has_side_effects=True))(*comm.inputs)


def _remote(src, dst, ssem, rsem, dev):
    return pltpu.make_async_remote_copy(src_ref=src, dst_ref=dst, send_sem=ssem, recv_sem=rsem, device_id=dev,
                                        device_id_type=pl.DeviceIdType.MESH)


def _place():
    x, y, c = lax.axis_index("x"), lax.axis_index("y"), lax.axis_index("c")
    other_chips = [(1 - x, y), (x, 1 - y), (1 - x, 1 - y)]
    return x, y, c, other_chips


def _slot(x, y, c, swap):
    return 4 * y + 2 * x + c if swap else 4 * x + 2 * y + c


def _chip_slot(x, y, swap):
    return 2 * y + x if swap else 2 * x + y


def _gather_comm(shards, swaps=None):
    n = len(shards)
    per = N_DEV - 1
    swaps = [False] * n if swaps is None else swaps
    pieces = []
    for i, a in enumerate(shards):
        rows = a.shape[0]
        k = GATHER_PIECES if (a.ndim == 2 and rows >= GATHER_PIECE_MIN_ROWS) else 1
        step = -(-rows // (k * 8)) * 8
        if k == 1:
            pieces.append((i, 0, None))
        else:
            pieces += [(i, r, min(step, rows - r)) for r in range(0, rows, step)]
    m = len(pieces)

    def src(ins, v):
        i, r, cnt = pieces[v]
        return ins[i] if cnt is None else ins[i].at[pl.ds(r, cnt)]

    def place(outs, v, x, y, c):
        i, r, cnt = pieces[v]
        blk = outs[i].at[_slot(x, y, c, swaps[i])]
        return blk if cnt is None else blk.at[pl.ds(r, cnt)]

    def start(ins, outs, sems):
        send, recv, loc = sems
        x, y, c, chips = _place()
        for v in range(m):
            me = place(outs, v, x, y, c)
            pltpu.make_async_copy(src(ins, v), me, loc.at[v]).start()
            _remote(src(ins, v), me, send.at[per * v], recv.at[per * v], (x, y, 1 - c)).start()
        for j, (qx, qy) in enumerate(chips):
            for v in range(m):
                _remote(src(ins, v), place(outs, v, x, y, c), send.at[per * v + 1 + j], recv.at[per * v + 1 + j],
                        (qx, qy, c)).start()

    def finish(ins, outs, sems):
        send, recv, loc = sems
        x, y, c, chips = _place()
        sib = (x, y, 1 - c)
        for v in range(m):
            for j, (qx, qy) in enumerate(chips):
                blk = place(outs, v, qx, qy, c)
                _remote(blk, blk, send.at[per * v + 1 + j], recv.at[per * v + 1 + j], (qx, qy, c)).wait_recv()
                _remote(blk, blk, send.at[per * v + 4 + j], recv.at[per * v + 4 + j], sib).start()
        for v in range(m):
            blk = place(outs, v, x, y, 1 - c)
            _remote(blk, blk, send.at[per * v], recv.at[per * v], sib).wait_recv()
            for j, (qx, qy) in enumerate(chips):
                blk = place(outs, v, qx, qy, 1 - c)
                _remote(blk, blk, send.at[per * v + 4 + j], recv.at[per * v + 4 + j], sib).wait_recv()
        for v in range(m):
            own = place(outs, v, x, y, c)
            for k in range(per):
                _remote(src(ins, v), own, send.at[per * v + k], recv.at[per * v + k], sib).wait_send()
            pltpu.make_async_copy(src(ins, v), own, loc.at[v]).wait()

    out_shapes = [jax.ShapeDtypeStruct((N_DEV,) + tuple(a.shape), a.dtype) for a in shards]
    sems = [pltpu.SemaphoreType.DMA((per * m,)), pltpu.SemaphoreType.DMA((per * m,)), pltpu.SemaphoreType.DMA((m,))]
    return _Comm(list(shards), out_shapes, sems, start, finish)


def _pair_comm(slots):
    n = len(slots)

    def copies(ins, outs, sems):
        send, recv = sems
        x, y, c, _ = _place()
        sib = (x, y, 1 - c)
        out = []
        for i in range(n):
            for q in range(4):
                out.append(_remote(ins[i].at[2 * q + 1 - c], outs[i].at[q], send.at[4 * i + q], recv.at[4 * i + q], sib))
        return out

    def start(ins, outs, sems):
        for cp in copies(ins, outs, sems):
            cp.start()

    def finish(ins, outs, sems):
        for cp in copies(ins, outs, sems):
            cp.wait_send()
            cp.wait_recv()

    out_shapes = [jax.ShapeDtypeStruct((4,) + tuple(a.shape[1:]), a.dtype) for a in slots]
    sems = [pltpu.SemaphoreType.DMA((4 * n,)), pltpu.SemaphoreType.DMA((4 * n,))]
    return _Comm(list(slots), out_shapes, sems, start, finish)


def _chip_comm(chip_sums, swaps=None, rows=None):
    n = len(chip_sums)
    swaps = [False] * n if swaps is None else swaps
    rows = [None] * n if rows is None else rows

    def src(ins, i, q):
        return ins[i].at[q] if rows[i] is None else ins[i].at[q, pl.ds(rows[i][0], rows[i][1] - rows[i][0])]

    def start(ins, outs, sems):
        send, recv, loc = sems
        x, y, c, chips = _place()
        for i in range(n):
            mine = _chip_slot(x, y, swaps[i])
            pltpu.make_async_copy(src(ins, i, mine), outs[i].at[mine], loc.at[i]).start()
            for j, (qx, qy) in enumerate(chips):
                _remote(src(ins, i, _chip_slot(qx, qy, swaps[i])), outs[i].at[mine], send.at[3 * i + j],
                        recv.at[3 * i + j], (qx, qy, c)).start()

    def finish(ins, outs, sems):
        send, recv, loc = sems
        x, y, c, chips = _place()
        for i in range(n):
            mine = _chip_slot(x, y, swaps[i])
            for j, (qx, qy) in enumerate(chips):
                theirs = _chip_slot(qx, qy, swaps[i])
                cp = _remote(src(ins, i, theirs), outs[i].at[theirs], send.at[3 * i + j], recv.at[3 * i + j], (qx, qy, c))
                cp.wait_send()
                cp.wait_recv()
            pltpu.make_async_copy(src(ins, i, mine), outs[i].at[mine], loc.at[i]).wait()

    def out_shape(a, r):
        shape = a.shape if r is None else (a.shape[0], r[1] - r[0]) + tuple(a.shape[2:])
        return jax.ShapeDtypeStruct(shape, a.dtype)

    out_shapes = [out_shape(a, r) for a, r in zip(chip_sums, rows)]
    sems = [pltpu.SemaphoreType.DMA((3 * n,)), pltpu.SemaphoreType.DMA((3 * n,)), pltpu.SemaphoreType.DMA((n,))]
    return _Comm(list(chip_sums), out_shapes, sems, start, finish)


def _join_comm(a, b):
    na_i, na_o, na_s = len(a.inputs), len(a.out_shapes), len(a.sems)

    def start(ins, outs, sems):
        a.start(ins[:na_i], outs[:na_o], sems[:na_s])
        b.start(ins[na_i:], outs[na_o:], sems[na_s:])

    def finish(ins, outs, sems):
        a.finish(ins[:na_i], outs[:na_o], sems[:na_s])
        b.finish(ins[na_i:], outs[na_o:], sems[na_s:])

    return _Comm(a.inputs + b.inputs, a.out_shapes + b.out_shapes, a.sems + b.sems, start, finish)


def _row_tile(r):
    for cand in (256, 128):
        if r > cand and r % cand == 0:
            return cand
    return r


def _add_pairs(name, slots, sib):
    r, c = slots.shape[1:]
    tr = _row_tile(r)

    def body(core_ref, s_ref, b_ref, o_ref):
        o_ref[...] = (s_ref[...].astype(F32) + b_ref[...].astype(F32)).astype(o_ref.dtype)

    core = jnp.full((1,), lax.axis_index("c"), jnp.int32)
    return pl.pallas_call(
        body, name=name,
        grid_spec=pltpu.PrefetchScalarGridSpec(
            num_scalar_prefetch=1, grid=(4, r // tr),
            in_specs=[pl.BlockSpec((None, None, tr, c), lambda q, i, core_ref: (q, core_ref[0], i, 0)),
                      pl.BlockSpec((None, tr, c), lambda q, i, core_ref: (q, i, 0))],
            out_specs=pl.BlockSpec((None, tr, c), lambda q, i, core_ref: (q, i, 0))),
        out_shape=jax.ShapeDtypeStruct((4, r, c), slots.dtype),
        compiler_params=_params("parallel", "parallel"))(core, slots.reshape(4, 2, r, c), sib)


def _matmul(name, mode, a, b, grid, a_spec, b_spec, o_spec, out_shape, acc_shape,
            res=None, res_spec=None, alpha=1.0, comm=None):
    nk = grid[-1]
    has_res = res is not None

    def body(*refs):
        if has_res:
            a_ref, b_ref, r_ref, o_ref = refs[:4]
        else:
            a_ref, b_ref, o_ref = refs[:3]
            r_ref = None
        part = _dot(a_ref[...], b_ref[...], mode)

        def finish(v):
            if alpha != 1.0:
                v = v * alpha
            if has_res:
                v = r_ref[...] + v
            o_ref[...] = v.astype(o_ref.dtype)

        if nk == 1:
            finish(part)
        else:
            acc = refs[-1]
            k = pl.program_id(len(grid) - 1)

            @pl.when(k == 0)
            def _():
                acc[...] = part

            @pl.when(k > 0)
            def _():
                acc[...] += part

            @pl.when(k == nk - 1)
            def _():
                finish(acc[...])

    in_specs = [a_spec, b_spec] + ([res_spec] if has_res else [])
    args = (a, b) + ((res,) if has_res else ())
    scratch = [] if nk == 1 else [pltpu.VMEM(acc_shape, F32)]
    sem = ("parallel",) * (len(grid) - 1) + ("arbitrary",)
    out, couts = _pcall(name, body, grid, in_specs, o_spec, out_shape, args, scratch, sem, comm)
    return out if comm is None else (out, couts)


def _mm_nn(name, a, b, out_dtype=F32, res=None, alpha=1.0, tk=None, kk=None, a_off=0, b_off=0, comm=None):
    t = a.shape[0]
    kk = a.shape[1] if kk is None else kk
    n = b.shape[1]
    tk = kk if tk is None else tk
    grid = (t // TM, 1, kk // tk)
    return _matmul(
        name, "nn", a, b, grid,
        pl.BlockSpec((TM, tk), lambda i, j, k: (i, k + a_off)),
        pl.BlockSpec((tk, n), lambda i, j, k: (k + b_off, 0)),
        pl.BlockSpec((TM, n), lambda i, j, k: (i, 0)),
        jax.ShapeDtypeStruct((t, n), out_dtype), (TM, n),
        res=res, res_spec=pl.BlockSpec((TM, n), lambda i, j, k: (i, 0)), alpha=alpha, comm=comm)


def _mm_nt(name, a, b, n=None, tn=None, tk=None, out_dtype=F32, comm=None):
    t, kk = a.shape
    n = b.shape[0] if n is None else n
    tn = n if tn is None else tn
    tk = kk if tk is None else tk
    grid = (n // tn, t // TM, kk // tk)
    return _matmul(
        name, "nt", a, b, grid,
        pl.BlockSpec((TM, tk), lambda j, i, k: (i, k)),
        pl.BlockSpec((tn, tk), lambda j, i, k: (j, k)),
        pl.BlockSpec((TM, tn), lambda j, i, k: (i, j)),
        jax.ShapeDtypeStruct((t, n), out_dtype), (TM, tn), comm=comm)


def _mm_tn(name, a, b, out_dtype, tm=None, n=None, col_off=0, comm=None):
    t, m = a.shape
    n = b.shape[1] if n is None else n
    tm = m if tm is None else tm
    tk = t if t <= TN_MAX_TOKENS else TM
    grid = (m // tm, 1, t // tk)
    return _matmul(
        name, "tn", a, b, grid,
        pl.BlockSpec((tk, tm), lambda j, i, k: (k, j)),
        pl.BlockSpec((tk, n), lambda j, i, k: (k, col_off)),
        pl.BlockSpec((tm, n), lambda j, i, k: (j, 0)),
        jax.ShapeDtypeStruct((m, n), out_dtype), (tm, n), comm=comm)


def _rms_fwd(name, x, w):
    t, d = x.shape

    def body(x_ref, w_ref, h_ref):
        xv = x_ref[...]
        rstd = lax.rsqrt(jnp.mean(xv * xv, axis=-1, keepdims=True) + NORM_EPS)
        h_ref[...] = (xv * rstd * w_ref[...]).astype(h_ref.dtype)

    return pl.pallas_call(
        body, name=name, grid=(t // TE,),
        in_specs=[pl.BlockSpec((TE, d), lambda i: (i, 0)), pl.BlockSpec((1, d), lambda i: (0, 0))],
        out_specs=pl.BlockSpec((TE, d), lambda i: (i, 0)),
        out_shape=jax.ShapeDtypeStruct((t, d), BF), compiler_params=_params("parallel"))(x, w)


def _rms_bwd(name, x, w, dh, dres, out_scale, comm=None):
    t, d = x.shape

    def body(x_ref, w_ref, dh_ref, dres_ref, dx_ref, dxb_ref, dw_ref):
        i = pl.program_id(0)
        xv = x_ref[...]
        rstd = lax.rsqrt(jnp.mean(xv * xv, axis=-1, keepdims=True) + NORM_EPS)
        xhat = xv * rstd
        dhv = dh_ref[...]
        wd = dhv * w_ref[...]
        proj = jnp.mean(wd * xhat, axis=-1, keepdims=True)
        dx = dres_ref[...] + rstd * (wd - xhat * proj)
        dx_ref[...] = dx
        dxb_ref[...] = (dx * out_scale).astype(BF)
        part = jnp.sum(dhv * xhat, axis=0, keepdims=True)

        @pl.when(i == 0)
        def _():
            dw_ref[...] = part

        @pl.when(i > 0)
        def _():
            dw_ref[...] += part

    row = pl.BlockSpec((TE, d), lambda i: (i, 0))
    vec = pl.BlockSpec((1, d), lambda i: (0, 0))
    outs, couts = _pcall(
        name, body, (t // TE,), [row, vec, row, row], [row, row, vec],
        [jax.ShapeDtypeStruct((t, d), F32), jax.ShapeDtypeStruct((t, d), BF), jax.ShapeDtypeStruct((1, d), F32)],
        (x, w, dh, dres), (), ("arbitrary",), comm)
    return outs if comm is None else (outs, couts)


def _final_loss(x, w, target):
    t, d = x.shape

    def body(x_ref, w_ref, t_ref, loss_ref, dx_ref, dxb_ref, dw_ref):
        i = pl.program_id(0)
        xv = x_ref[...]
        rstd = lax.rsqrt(jnp.mean(xv * xv, axis=-1, keepdims=True) + NORM_EPS)
        xhat = xv * rstd
        err = xhat * w_ref[...] - t_ref[...]
        lpart = 0.5 * jnp.sum(jnp.mean(err * err, axis=-1, keepdims=True), axis=0, keepdims=True)
        dy = err * (1.0 / d)
        wd = dy * w_ref[...]
        proj = jnp.mean(wd * xhat, axis=-1, keepdims=True)
        dx = rstd * (wd - xhat * proj)
        dx_ref[...] = dx
        dxb_ref[...] = (0.5 * dx).astype(BF)
        part = jnp.sum(dy * xhat, axis=0, keepdims=True)
        lfull = jnp.broadcast_to(lpart, (1, 128))

        @pl.when(i == 0)
        def _():
            dw_ref[...] = part
            loss_ref[...] = lfull

        @pl.when(i > 0)
        def _():
            dw_ref[...] += part
            loss_ref[...] += lfull

    row = pl.BlockSpec((TE, d), lambda i: (i, 0))
    vec = pl.BlockSpec((1, d), lambda i: (0, 0))
    return pl.pallas_call(
        body, name="final_loss", grid=(t // TE,), in_specs=[row, vec, row],
        out_specs=[pl.BlockSpec((1, 128), lambda i: (0, 0)), row, row, vec],
        out_shape=[jax.ShapeDtypeStruct((1, 128), F32), jax.ShapeDtypeStruct((t, d), F32),
                   jax.ShapeDtypeStruct((t, d), BF), jax.ShapeDtypeStruct((1, d), F32)],
        compiler_params=_params("arbitrary"))(x, w, target)


def _swiglu_fwd(name, gu, comm=None):
    t = gu.shape[0]

    def body(g_ref, u_ref, a_ref):
        g = g_ref[...].astype(F32)
        a_ref[...] = (g * _sigmoid(g) * u_ref[...].astype(F32)).astype(BF)

    blk = (TE, FF_HALF)
    out, couts = _pcall(
        name, body, (t // TE, 2),
        [pl.BlockSpec(blk, lambda i, j: (i, 2 * j)), pl.BlockSpec(blk, lambda i, j: (i, 2 * j + 1))],
        pl.BlockSpec(blk, lambda i, j: (i, j)), jax.ShapeDtypeStruct((t, D_FF), BF),
        (gu, gu), (), ("parallel", "parallel"), comm)
    return out if comm is None else (out, couts)


def _swiglu_bwd(name, gu, dact, comm=None):
    t = gu.shape[0]

    def body(g_ref, u_ref, da_ref, o_ref):
        g = g_ref[...].astype(F32)
        da = da_ref[...].astype(F32)
        s = _sigmoid(g)
        o_ref[:, 0:FF_HALF] = (da * u_ref[...].astype(F32) * (s * (1.0 + g * (1.0 - s)))).astype(BF)
        o_ref[:, FF_HALF:2 * FF_HALF] = (da * g * s).astype(BF)

    blk = (TE, FF_HALF)
    out, couts = _pcall(
        name, body, (t // TE, 2),
        [pl.BlockSpec(blk, lambda i, j: (i, 2 * j)), pl.BlockSpec(blk, lambda i, j: (i, 2 * j + 1)),
         pl.BlockSpec(blk, lambda i, j: (i, j))],
        pl.BlockSpec((TE, 2 * FF_HALF), lambda i, j: (i, j)),
        jax.ShapeDtypeStruct((t, 2 * D_FF), BF), (gu, gu, dact), (), ("parallel", "parallel"), comm)
    return out if comm is None else (out, couts)


CONV_CB = 256


CONV_ROWS = 64
CONV_HALO = 16


def _taps_down(ext, w, k):
    shifted = [pltpu.roll(ext, k - 1 - j, 0)[CONV_HALO:] for j in range(k - 1)] + [ext[CONV_HALO:]]
    out = shifted[k - 1] * w[k - 1:k, :]
    for j in range(k - 1):
        out = out + shifted[j] * w[j:j + 1, :]
    return out, shifted


def _taps_up(ext, w, k):
    rows = ext.shape[0]
    n = rows - CONV_HALO
    out = ext[:n] * w[k - 1:k, :]
    for j in range(k - 1):
        out = out + pltpu.roll(ext, rows - (k - 1 - j), 0)[:n] * w[j:j + 1, :]
    return out


def _rows_before(ref, i, r0):
    start = pl.multiple_of(jnp.maximum(r0 - CONV_HALO, 0), CONV_HALO)
    return jnp.where(i > 0, ref[pl.ds(start, CONV_HALO), :].astype(F32), 0.0)


def _rows_after(ref, r0, t):
    start = pl.multiple_of(jnp.minimum(r0 + CONV_ROWS, t - CONV_HALO), CONV_HALO)
    return ref[pl.ds(start, CONV_HALO), :].astype(F32)


def _fold8(v):
    return v.reshape(v.shape[0] // 8, 8, v.shape[1]).sum(axis=0)


def _silu_grad(pre):
    s = _sigmoid(pre)
    return s * (1.0 + pre * (1.0 - s))


def _pspec(t, off):
    base = off // CONV_CB
    return pl.BlockSpec((t, CONV_CB), lambda j: (0, base + j))


def _mix_a_fwd(p, conv_w):
    t = p.shape[0]

    def body(b_ref, c_ref, xa_ref, w_ref, o_ref):
        w = w_ref[...]

        def step(i, carry):
            r0 = pl.multiple_of(i * CONV_ROWS, CONV_ROWS)
            rows = pl.ds(r0, CONV_ROWS)
            q = c_ref[rows, :].astype(F32) * xa_ref[rows, :].astype(F32)
            q_before = _rows_before(c_ref, i, r0) * _rows_before(xa_ref, i, r0)
            va, _ = _taps_down(jnp.concatenate([q_before, q], axis=0), w, 3)
            o_ref[rows, :] = (b_ref[rows, :].astype(F32) * va).astype(BF)
            return carry

        lax.fori_loop(0, t // CONV_ROWS, step, 0)

    return pl.pallas_call(
        body, name="mix_a_fwd", grid=(D_MODEL // CONV_CB,),
        in_specs=[_pspec(t, OFF_B), _pspec(t, OFF_C), _pspec(t, OFF_XA),
                  pl.BlockSpec((3, CONV_CB), lambda j: (0, j))],
        out_specs=pl.BlockSpec((t, CONV_CB), lambda j: (0, j)),
        out_shape=jax.ShapeDtypeStruct((t, D_MODEL), BF), compiler_params=_params("parallel"))(p, p, p, conv_w)


def _mix_a_bwd(p, conv_w, dya, dp):
    t = p.shape[0]

    def body(b_ref, c_ref, xa_ref, w_ref, dy_ref, dp_in, dp_ref, dw_ref):
        del dp_in
        w = w_ref[...]
        n = t // CONV_ROWS

        def step(i, acc):
            r0 = pl.multiple_of(i * CONV_ROWS, CONV_ROWS)
            rows = pl.ds(r0, CONV_ROWS)
            cv = c_ref[rows, :].astype(F32)
            xav = xa_ref[rows, :].astype(F32)
            q_before = _rows_before(c_ref, i, r0) * _rows_before(xa_ref, i, r0)
            va, shifted = _taps_down(jnp.concatenate([q_before, cv * xav], axis=0), w, 3)
            dyv = dy_ref[rows, :]
            dp_ref[rows, 0:CONV_CB] = (dyv * va).astype(BF)
            dv = dyv * b_ref[rows, :].astype(F32)
            dv_after = jnp.where(i < n - 1, _rows_after(dy_ref, r0, t) * _rows_after(b_ref, r0, t), 0.0)
            dq = _taps_up(jnp.concatenate([dv, dv_after], axis=0), w, 3)
            dp_ref[rows, CONV_CB:2 * CONV_CB] = (dq * xav).astype(BF)
            dp_ref[rows, 2 * CONV_CB:3 * CONV_CB] = (dq * cv).astype(BF)
            return tuple(a + _fold8(dv * s) for a, s in zip(acc, shifted))

        zero = jnp.zeros((8, CONV_CB), F32)
        acc = lax.fori_loop(0, n, step, (zero, zero, zero))
        for j in range(3):
            dw_ref[j:j + 1, :] = jnp.sum(acc[j], axis=0, keepdims=True)

    col = pl.BlockSpec((t, CONV_CB), lambda j: (0, j))
    wsp = pl.BlockSpec((3, CONV_CB), lambda j: (0, j))
    return pl.pallas_call(
        body, name="mix_a_bwd", grid=(D_MODEL // CONV_CB,),
        in_specs=[_pspec(t, OFF_B), _pspec(t, OFF_C), _pspec(t, OFF_XA), wsp, col, pl.BlockSpec(memory_space=pl.ANY)],
        out_specs=[pl.BlockSpec((t, 3 * CONV_CB), lambda j: (0, j)), wsp],
        out_shape=[jax.ShapeDtypeStruct(dp.shape, dp.dtype), jax.ShapeDtypeStruct((3, D_MODEL), F32)],
        input_output_aliases={5: 0},
        compiler_params=_params("parallel"))(p, p, p, conv_w, dya, dp)


def _ssm_conv_fwd(p, conv_w, conv_b, comm=None):
    t = p.shape[0]

    def body(x_ref, w_ref, b_ref, o_ref):
        w = w_ref[...]
        bias = b_ref[...]

        def step(i, carry):
            r0 = pl.multiple_of(i * CONV_ROWS, CONV_ROWS)
            rows = pl.ds(r0, CONV_ROWS)
            ext = jnp.concatenate([_rows_before(x_ref, i, r0), x_ref[rows, :].astype(F32)], axis=0)
            pre = _taps_down(ext, w, 4)[0] + bias
            o_ref[rows, :] = pre * _sigmoid(pre)
            return carry

        lax.fori_loop(0, t // CONV_ROWS, step, 0)

    out, couts = _pcall(
        "ssm_conv_fwd", body, (D_XBC // CONV_CB,),
        [_pspec(t, OFF_XBC), pl.BlockSpec((4, CONV_CB), lambda j: (0, j)), pl.BlockSpec((1, CONV_CB), lambda j: (0, j))],
        pl.BlockSpec((t, CONV_CB), lambda j: (0, j)), jax.ShapeDtypeStruct((t, D_XBC), F32),
        (p, conv_w, conv_b), (), ("parallel",), comm)
    return out if comm is None else (out, couts)


def _ssm_conv_bwd(p, conv_w, conv_b, dxc, dp, comm=None):
    t = p.shape[0]

    def body(x_ref, w_ref, b_ref, d_ref, dp_in, dx_ref, dw_ref, db_ref):
        del dp_in
        w = w_ref[...]
        bias = b_ref[...]
        n = t // CONV_ROWS

        def step(i, acc):
            r0 = pl.multiple_of(i * CONV_ROWS, CONV_ROWS)
            rows = pl.ds(r0, CONV_ROWS)
            x_cur = x_ref[rows, :].astype(F32)
            pre, shifted = _taps_down(jnp.concatenate([_rows_before(x_ref, i, r0), x_cur], axis=0), w, 4)
            pre = pre + bias
            dpre = d_ref[rows, :] * _silu_grad(pre)
            ext_after = jnp.concatenate([x_cur[CONV_ROWS - CONV_HALO:], _rows_after(x_ref, r0, t)], axis=0)
            pre_after = _taps_down(ext_after, w, 4)[0] + bias
            dpre_after = jnp.where(i < n - 1, _rows_after(d_ref, r0, t) * _silu_grad(pre_after), 0.0)
            dx_ref[rows, :] = _taps_up(jnp.concatenate([dpre, dpre_after], axis=0), w, 4).astype(BF)
            new = tuple(a + _fold8(dpre * s) for a, s in zip(acc[:4], shifted))
            return new + (acc[4] + _fold8(dpre),)

        zero = jnp.zeros((8, CONV_CB), F32)
        acc = lax.fori_loop(0, n, step, (zero,) * 5)
        for j in range(4):
            dw_ref[j:j + 1, :] = jnp.sum(acc[j], axis=0, keepdims=True)
        db_ref[...] = jnp.sum(acc[4], axis=0, keepdims=True)

    col = pl.BlockSpec((t, CONV_CB), lambda j: (0, j))
    wsp = pl.BlockSpec((4, CONV_CB), lambda j: (0, j))
    bsp = pl.BlockSpec((1, CONV_CB), lambda j: (0, j))
    outs, couts = _pcall(
        "ssm_conv_bwd", body, (D_XBC // CONV_CB,),
        [_pspec(t, OFF_XBC), wsp, bsp, col, pl.BlockSpec(memory_space=pl.ANY)], [_pspec(t, OFF_XBC), wsp, bsp],
        [jax.ShapeDtypeStruct(dp.shape, dp.dtype), jax.ShapeDtypeStruct((4, D_XBC), F32),
         jax.ShapeDtypeStruct((1, D_XBC), F32)],
        (p, conv_w, conv_b, dxc, dp), (), ("parallel",), comm, aliases={4: 0})
    return outs if comm is None else (outs, couts)


DT_ROWS = 512


def _tri(lower):
    r = lax.broadcasted_iota(jnp.int32, (CHUNK, CHUNK), 0)
    c = lax.broadcasted_iota(jnp.int32, (CHUNK, CHUNK), 1)
    return jnp.where((r >= c) if lower else (r <= c), 1.0, 0.0).astype(F32)


def _dot_exact(a, b):
    return lax.dot_general(a, b, _DIMS["nn"], preferred_element_type=F32, precision=lax.Precision.HIGHEST)


def _dt_fwd(p, bias_pad, alog_pad):
    t = p.shape[0]

    def body(raw_ref, b_ref, al_ref, dt_ref, acs_ref):
        z = raw_ref[...] + b_ref[...]
        dt = jnp.maximum(z, 0.0) + jnp.log(1.0 + jnp.exp(-jnp.abs(z)))
        dt_ref[...] = dt
        a = dt * (-jnp.exp(al_ref[...]))
        tri = _tri(True)
        for k in range(DT_ROWS // CHUNK):
            acs_ref[k * CHUNK:(k + 1) * CHUNK, :] = _dot_exact(tri, a[k * CHUNK:(k + 1) * CHUNK, :])

    blk = pl.BlockSpec((DT_ROWS, DT_W), lambda i: (i, 0))
    vec = pl.BlockSpec((1, DT_W), lambda i: (0, 0))
    return pl.pallas_call(
        body, name="dt_fwd", grid=(t // DT_ROWS,),
        in_specs=[pl.BlockSpec((DT_ROWS, DT_W), lambda i: (i, OFF_DT // DT_W)), vec, vec],
        out_specs=[blk, blk], out_shape=[jax.ShapeDtypeStruct((t, DT_W), F32)] * 2,
        compiler_params=_params("parallel"))(p, bias_pad, alog_pad)


def _dt_bwd(p, bias_pad, alog_pad, dt, ddt, dacs, dp_gd):
    t = p.shape[0]

    def body(raw_ref, b_ref, al_ref, dt_ref, ddt_ref, dacs_ref, dp_in, draw_ref, db_ref, dal_ref):
        del dp_in
        i = pl.program_id(0)
        acoef = -jnp.exp(al_ref[...])
        triu = _tri(False)
        das = []
        for k in range(DT_ROWS // CHUNK):
            das.append(_dot_exact(triu, dacs_ref[k * CHUNK:(k + 1) * CHUNK, :]))
        da = jnp.concatenate(das, axis=0)
        dtv = dt_ref[...]
        ddt_tot = ddt_ref[...] + da * acoef
        lane = lax.broadcasted_iota(jnp.int32, (DT_ROWS, DT_W), 1)
        draw = jnp.where(lane < N_HEADS, ddt_tot * _sigmoid(raw_ref[...] + b_ref[...]), 0.0)
        draw_ref[...] = draw.astype(BF)
        pb = jnp.sum(draw, axis=0, keepdims=True)
        pa = jnp.sum(da * dtv * acoef, axis=0, keepdims=True)

        @pl.when(i == 0)
        def _():
            db_ref[...] = pb
            dal_ref[...] = pa

        @pl.when(i > 0)
        def _():
            db_ref[...] += pb
            dal_ref[...] += pa

    blk = pl.BlockSpec((DT_ROWS, DT_W), lambda i: (i, 0))
    vec = pl.BlockSpec((1, DT_W), lambda i: (0, 0))
    return pl.pallas_call(
        body, name="dt_bwd", grid=(t // DT_ROWS,),
        in_specs=[pl.BlockSpec((DT_ROWS, DT_W), lambda i: (i, OFF_DT // DT_W)), vec, vec, blk, blk, blk,
                  pl.BlockSpec(memory_space=pl.ANY)],
        out_specs=[pl.BlockSpec((DT_ROWS, DT_W), lambda i: (i, OFF_DT // DT_W)), vec, vec],
        out_shape=[jax.ShapeDtypeStruct(dp_gd.shape, dp_gd.dtype), jax.ShapeDtypeStruct((1, DT_W), F32),
                   jax.ShapeDtypeStruct((1, DT_W), F32)],
        input_output_aliases={6: 0},
        compiler_params=_params("arbitrary"))(p, bias_pad, alog_pad, dt, ddt, dacs, dp_gd)


def _split_dot(z, onehot, terms):
    out = None
    rest = z
    for _ in range(terms):
        piece = rest.astype(BF)
        part = _dot(piece, onehot)
        out = part if out is None else out + part
        rest = rest - piece.astype(F32)
    return out


def _spread_mat():
    row = lax.broadcasted_iota(jnp.int32, (DT_W, D_INNER), 0)
    lane = lax.broadcasted_iota(jnp.int32, (DT_W, D_INNER), 1)
    return jnp.where(row == lane // HEAD_DIM, 1.0, 0.0).astype(BF)


def _gather_mat():
    row = lax.broadcasted_iota(jnp.int32, (D_INNER, DT_W), 0)
    lane = lax.broadcasted_iota(jnp.int32, (D_INNER, DT_W), 1)
    return jnp.where(lane == row // HEAD_DIM, 1.0, 0.0).astype(BF)


def _ssd_masks():
    row = lax.broadcasted_iota(jnp.int32, (CHUNK, GROUP_W), 0)
    col = lax.broadcasted_iota(jnp.int32, (CHUNK, GROUP_W), 1) % HEAD_DIM
    brow = lax.broadcasted_iota(jnp.int32, (GROUP_W, GROUP_W), 0) // HEAD_DIM
    bcol = lax.broadcasted_iota(jnp.int32, (GROUP_W, GROUP_W), 1) // HEAD_DIM
    return row >= col, row == col, brow == bcol


def _stack4(v):
    return jnp.concatenate([v, v, v, v], axis=0)


def _fold4(v):
    return v[0:CHUNK] + v[CHUNK:2 * CHUNK] + v[2 * CHUNK:3 * CHUNK] + v[3 * CHUNK:4 * CHUNK]


def _ssd_group(xc_ref, wide_ref, g, tri, eye, blockdiag):
    gs = slice(GROUP_W * g, GROUP_W * (g + 1))
    xs_g = xc_ref[:, gs]
    b_g = xc_ref[:, D_INNER + D_STATE * g:D_INNER + D_STATE * (g + 1)].astype(BF)
    c_g = xc_ref[:, D_INNER + 1024 + D_STATE * g:D_INNER + 1024 + D_STATE * (g + 1)].astype(BF)
    acs_e, dt_e = wide_ref[0:CHUNK, gs], wide_ref[CHUNK:2 * CHUNK, gs]
    atot_e = acs_e[CHUNK - 1:CHUNK, :]
    acs_j = jnp.sum(jnp.where(eye, acs_e, 0.0), axis=0, keepdims=True)
    lmat = jnp.where(tri, jnp.exp(jnp.minimum(acs_e - acs_j, 0.0)), 0.0)
    b_t = _stack4(b_g)
    m = _dot(c_g, b_t, "nt") * lmat
    x_g = xs_g * dt_e
    xbd = jnp.where(blockdiag, _stack4(x_g), 0.0).astype(BF)
    return dict(gs=gs, xs=xs_g, b=b_g, c=c_g, b_t=b_t, dt=dt_e, e=jnp.exp(acs_e), dec=jnp.exp(atot_e - acs_e),
                eat=jnp.exp(atot_e), lmat=lmat, m=m, x=x_g, xbd=xbd)


def _ssd_fwd(xconv, dt, acs, d_exp, comm=None):
    t = xconv.shape[0]
    nc = t // CHUNK

    def body(xc_ref, dt_ref, acs_ref, d_ref, y_ref, hs_ref, state, wide):
        c = pl.program_id(0)

        @pl.when(c == 0)
        def _():
            state[...] = jnp.zeros_like(state)

        hs_ref[...] = state[...]
        tri, eye, blockdiag = _ssd_masks()
        wide[...] = _split_dot(jnp.concatenate([acs_ref[...], dt_ref[...]], axis=0), _spread_mat(), 3)
        for g in range(N_GROUPS):
            q = _ssd_group(xc_ref, wide, g, tri, eye, blockdiag)
            gs = q["gs"]
            h_t = state[:, gs]
            ydiag = _dot(q["m"].astype(BF), q["xbd"])
            yoff = _dot(q["c"], h_t.astype(BF)) * q["e"]
            y_ref[:, gs] = ydiag + yoff + d_ref[:, gs] * q["xs"]
            s_t = _dot(q["b"], (q["x"] * q["dec"]).astype(BF), "tn")
            state[:, gs] = q["eat"] * h_t + s_t

    blk = lambda w: pl.BlockSpec((CHUNK, w), lambda c: (c, 0))
    outs, couts = _pcall(
        "ssd_fwd", body, (nc,),
        [blk(D_XBC), blk(DT_W), blk(DT_W), pl.BlockSpec((1, D_INNER), lambda c: (0, 0))],
        [blk(D_INNER), pl.BlockSpec((None, D_STATE, D_INNER), lambda c: (c, 0, 0))],
        [jax.ShapeDtypeStruct((t, D_INNER), F32), jax.ShapeDtypeStruct((nc, D_STATE, D_INNER), F32)],
        (xconv, dt, acs, d_exp), [pltpu.VMEM((D_STATE, D_INNER), F32), pltpu.VMEM((2 * CHUNK, D_INNER), F32)],
        ("arbitrary",), comm)
    return outs if comm is None else (outs, couts)


def _ssd_bwd(xconv, dt, acs, d_exp, hsave, dy, comm=None):
    t = xconv.shape[0]
    nc = t // CHUNK

    def body(xc_ref, dt_ref, acs_ref, d_ref, hs_ref, dy_ref, dxc_ref, ddt_ref, dacs_ref, dd_ref, dstate, wide, per_head):
        c = pl.program_id(0)

        @pl.when(c == 0)
        def _():
            dstate[...] = jnp.zeros_like(dstate)
            dd_ref[...] = jnp.zeros_like(dd_ref)

        tri, eye, blockdiag = _ssd_masks()
        acsv = acs_ref[...]
        wide[...] = _split_dot(jnp.concatenate([acsv, dt_ref[...]], axis=0), _spread_mat(), 3)
        eat_heads = jnp.exp(acsv[CHUNK - 1:CHUNK, :])

        for g in range(N_GROUPS):
            q = _ssd_group(xc_ref, wide, g, tri, eye, blockdiag)
            gs, xs_g, b_g, c_g, m = q["gs"], q["xs"], q["b"], q["c"], q["m"]
            bs = slice(D_INNER + D_STATE * g, D_INNER + D_STATE * (g + 1))
            cs = slice(D_INNER + 1024 + D_STATE * g, D_INNER + 1024 + D_STATE * (g + 1))
            h_t = hs_ref[:, gs]
            h_b = h_t.astype(BF)
            dy_g = dy_ref[:, gs]
            dy_b = dy_g.astype(BF)
            ds_t = dstate[:, gs]
            ds_b = ds_t.astype(BF)

            yoff = _dot(c_g, h_b) * q["e"]
            edy = (q["e"] * dy_g).astype(BF)
            d_c = _dot(edy, h_b, "nt")
            d_ht = _dot(c_g, edy, "tn")
            bds = _dot(b_g, ds_b)
            xd = q["x"] * q["dec"]
            d_b = _dot(xd.astype(BF), ds_b, "nt")
            dm = _dot(dy_b, q["xbd"], "nt")
            cross = _dot(m.astype(BF), dy_b, "tn")
            dx_full = q["dec"] * bds + _fold4(jnp.where(blockdiag, cross, 0.0))
            dml = (dm * q["lmat"]).astype(BF)
            d_c = d_c + _dot(dml, q["b_t"])
            d_b = d_b + _fold4(_dot(dml, c_g, "tn"))
            w = dm * m
            q_dec = xd * bds
            z = w - jnp.where(eye, jnp.sum(w, axis=0, keepdims=True), 0.0) + dy_g * yoff - q_dec
            rows = jnp.concatenate(
                [jnp.sum(q_dec, axis=0, keepdims=True), jnp.sum(ds_t * h_t, axis=0, keepdims=True),
                 jnp.zeros((6, GROUP_W), F32)], axis=0)
            per_head[:, gs] = jnp.concatenate([z, dx_full * xs_g, rows], axis=0)
            dxc_ref[:, cs] = d_c
            dxc_ref[:, bs] = d_b
            dxc_ref[:, gs] = dx_full * q["dt"] + d_ref[:, gs] * dy_g
            dd_ref[:, gs] += jnp.sum(dy_g * xs_g, axis=0, keepdims=True)
            dstate[:, gs] = q["eat"] * ds_t + d_ht

        seg = _split_dot(per_head[...], _gather_mat(), 2)
        datot = seg[2 * CHUNK:2 * CHUNK + 1] + eat_heads * seg[2 * CHUNK + 1:2 * CHUNK + 2]
        rowi = lax.broadcasted_iota(jnp.int32, (CHUNK, DT_W), 0)
        ddt_ref[...] = seg[CHUNK:2 * CHUNK]
        dacs_ref[...] = seg[0:CHUNK] + jnp.where(rowi == CHUNK - 1, datot, 0.0)

    rev = lambda w: pl.BlockSpec((CHUNK, w), lambda c: (nc - 1 - c, 0))
    vec = pl.BlockSpec((1, D_INNER), lambda c: (0, 0))
    outs, couts = _pcall(
        "ssd_bwd", body, (nc,),
        [rev(D_XBC), rev(DT_W), rev(DT_W), vec,
         pl.BlockSpec((None, D_STATE, D_INNER), lambda c: (nc - 1 - c, 0, 0)), rev(D_INNER)],
        [rev(D_XBC), rev(DT_W), rev(DT_W), vec],
        [jax.ShapeDtypeStruct((t, D_XBC), F32), jax.ShapeDtypeStruct((t, DT_W), F32),
         jax.ShapeDtypeStruct((t, DT_W), F32), jax.ShapeDtypeStruct((1, D_INNER), F32)],
        (xconv, dt, acs, d_exp, hsave, dy),
        [pltpu.VMEM((D_STATE, D_INNER), F32), pltpu.VMEM((2 * CHUNK, D_INNER), F32),
         pltpu.VMEM((2 * CHUNK + 8, D_INNER), F32)], ("arbitrary",), comm)
    return outs if comm is None else (outs, couts)


GN_CB = 1024
GN_GROUPS = GN_CB // GROUP_W


def _gnorm_fwd(y, p, w, comm=None):
    t = y.shape[0]
    zoff = OFF_Z // GN_CB

    def body(y_ref, z_ref, w_ref, o_ref):
        for g in range(GN_GROUPS):
            gs = slice(GROUP_W * g, GROUP_W * (g + 1))
            z = z_ref[:, gs].astype(F32)
            yf = y_ref[:, gs] * (z * _sigmoid(z))
            rstd = lax.rsqrt(jnp.mean(yf * yf, axis=-1, keepdims=True) + NORM_EPS)
            o_ref[:, gs] = (yf * rstd * w_ref[:, gs]).astype(BF)

    blk = pl.BlockSpec((TE, GN_CB), lambda i, j: (i, j))
    out, couts = _pcall(
        "gnorm_fwd", body, (t // TE, D_INNER // GN_CB),
        [blk, pl.BlockSpec((TE, GN_CB), lambda i, j: (i, zoff + j)), pl.BlockSpec((1, GN_CB), lambda i, j: (0, j))],
        blk, jax.ShapeDtypeStruct((t, D_INNER), BF), (y, p, w), (), ("parallel", "parallel"), comm)
    return out if comm is None else (out, couts)


def _gnorm_bwd(y, p, w, dyn, comm=None):
    t = y.shape[0]
    zoff = OFF_Z // GN_CB

    def body(y_ref, z_ref, w_ref, dn_ref, dy_ref, dz_ref, dw_ref):
        i = pl.program_id(1)
        for g in range(GN_GROUPS):
            gs = slice(GROUP_W * g, GROUP_W * (g + 1))
            z = z_ref[:, gs].astype(F32)
            yv = y_ref[:, gs]
            s = _sigmoid(z)
            sil = z * s
            yf = yv * sil
            rstd = lax.rsqrt(jnp.mean(yf * yf, axis=-1, keepdims=True) + NORM_EPS)
            xhat = yf * rstd
            dn = dn_ref[:, gs]
            wd = dn * w_ref[:, gs]
            proj = jnp.mean(wd * xhat, axis=-1, keepdims=True)
            dyf = rstd * (wd - xhat * proj)
            dy_ref[:, gs] = dyf * sil
            dz_ref[:, gs] = (dyf * yv * (s * (1.0 + z * (1.0 - s)))).astype(BF)
            part = jnp.sum(dn * xhat, axis=0, keepdims=True)

            @pl.when(i == 0)
            def _():
                dw_ref[:, gs] = part

            @pl.when(i > 0)
            def _():
                dw_ref[:, gs] += part

    blk = pl.BlockSpec((TE, GN_CB), lambda j, i: (i, j))
    vec = pl.BlockSpec((1, GN_CB), lambda j, i: (0, j))
    outs, couts = _pcall(
        "gnorm_bwd", body, (D_INNER // GN_CB, t // TE),
        [blk, pl.BlockSpec((TE, GN_CB), lambda j, i: (i, zoff + j)), vec, blk],
        [blk, pl.BlockSpec((TE, GN_CB), lambda j, i: (i, zoff + j)), vec],
        [jax.ShapeDtypeStruct((t, D_INNER), F32), jax.ShapeDtypeStruct((t, N_MAIN), BF),
         jax.ShapeDtypeStruct((1, D_INNER), F32)],
        (y, p, w, dyn), (), ("parallel", "arbitrary"), comm)
    return outs if comm is None else (outs, couts)


MERGE_CB = 512


def _merge_fwd(p, ya, yb):
    t = ya.shape[0]

    def body(ga_ref, gb_ref, ya_ref, yb_ref, o_ref):
        o_ref[...] = (_sigmoid(ga_ref[...]) * ya_ref[...] + _sigmoid(gb_ref[...]) * yb_ref[...]).astype(BF)

    blk = pl.BlockSpec((TE, MERGE_CB), lambda i, j: (i, j))
    return pl.pallas_call(
        body, name="merge_fwd", grid=(t // TE, D_MODEL // MERGE_CB),
        in_specs=[pl.BlockSpec((TE, MERGE_CB), lambda i, j: (i, 2 * j)),
                  pl.BlockSpec((TE, MERGE_CB), lambda i, j: (i, 2 * j + 1)), blk, blk],
        out_specs=blk, out_shape=jax.ShapeDtypeStruct((t, D_MODEL), BF),
        compiler_params=_params("parallel", "parallel"))(p, p, ya, yb)


def _merge_bwd(p, ya, yb, dm):
    t = ya.shape[0]

    def body(ga_ref, gb_ref, ya_ref, yb_ref, dm_ref, dg_ref, dya_ref, dyb_ref):
        d = dm_ref[...]
        sa = _sigmoid(ga_ref[...])
        sb = _sigmoid(gb_ref[...])
        dg_ref[:, 0:MERGE_CB] = (d * ya_ref[...] * sa * (1.0 - sa)).astype(BF)
        dg_ref[:, MERGE_CB:2 * MERGE_CB] = (d * yb_ref[...] * sb * (1.0 - sb)).astype(BF)
        dya_ref[...] = (d * sa).astype(BF)
        dyb_ref[...] = (d * sb).astype(BF)

    blk = pl.BlockSpec((TE, MERGE_CB), lambda i, j: (i, j))
    return pl.pallas_call(
        body, name="merge_bwd", grid=(t // TE, D_MODEL // MERGE_CB),
        in_specs=[pl.BlockSpec((TE, MERGE_CB), lambda i, j: (i, 2 * j)),
                  pl.BlockSpec((TE, MERGE_CB), lambda i, j: (i, 2 * j + 1)), blk, blk, blk],
        out_specs=[pl.BlockSpec((TE, 2 * MERGE_CB), lambda i, j: (i, j)), blk, blk],
        out_shape=[jax.ShapeDtypeStruct((t, N_GD), BF)] + [jax.ShapeDtypeStruct((t, D_MODEL), BF)] * 2,
        compiler_params=_params("parallel", "parallel"))(p, p, ya, yb, dm)


def _adamw(name, parts, w, m, v, comm=None):
    r, c = w.shape
    tr = _row_tile(r)
    tc = ADAM_COL_TILE if (tr == r and r > 512 and c % ADAM_COL_TILE == 0) else c
    n_parts = parts.shape[0]
    bc1 = 1.0 - ADAM_B1 ** ADAM_STEP
    bc2 = 1.0 - ADAM_B2 ** ADAM_STEP

    def body(p_ref, w_ref, m_ref, v_ref, g_ref, d_ref, nm_ref, nv_ref):
        g = p_ref[0].astype(F32)
        for k in range(1, n_parts):
            g = g + p_ref[k].astype(F32)
        nm = ADAM_B1 * m_ref[...] + (1.0 - ADAM_B1) * g
        nv = ADAM_B2 * v_ref[...] + (1.0 - ADAM_B2) * (g * g)
        g_ref[...] = g
        nm_ref[...] = nm
        nv_ref[...] = nv
        d_ref[...] = -ADAM_LR * ((nm / bc1) / (jnp.sqrt(nv / bc2) + ADAM_EPS) + ADAM_WD * w_ref[...])

    blk = pl.BlockSpec((tr, tc), lambda i, j: (i, j))
    outs, couts = _pcall(
        name, body, (r // tr, c // tc),
        [pl.BlockSpec((n_parts, tr, tc), lambda i, j: (0, i, j)), blk, blk, blk], [blk] * 4,
        [jax.ShapeDtypeStruct((r, c), F32)] * 4, (parts, w, m, v), (), ("parallel", "parallel"), comm)
    return outs if comm is None else (outs, couts)


def _pad_lanes(v, width):
    return jnp.pad(v, ((0, 0), (0, width - v.shape[1])))


def _reduce_start(slots, host):
    outs, sib = host(_pair_comm([a for _, a in slots]))
    sums = [(n, _add_pairs("pairsum_" + n, a, b)) for (n, a), b in zip(slots, sib)]
    return outs, sums


def _train_step(x, target, shard, rep):
    gdt = BF
    recv = {}
    (got,) = _comm_call("gather_ffn1_in", _gather_comm([shard["ffn1_w_in"]], [True]))
    w1_in = got.reshape(2 * D_FF, D_MODEL)
    h1 = _rms_fwd("rms1_fwd", x, rep["ffn1_norm"])
    gu1, got = _mm_nt("ffn1_in", h1, w1_in, tn=FF_HALF, out_dtype=BF, comm=_gather_comm(
        [shard["ffn1_w_out"], shard["w_in"], shard["short_conv_w"], shard["ssm_conv_w"]]))
    w1_out = got[0].reshape(D_FF, D_MODEL)
    w_in_t = got[1].reshape(N_IN, D_MODEL)
    short_conv_w = got[2].transpose(1, 0, 2).reshape(3, D_MODEL)
    ssm_conv_w = got[3].transpose(1, 0, 2).reshape(4, D_XBC)
    act1 = _swiglu_fwd("swiglu1_fwd", gu1)
    x1 = _mm_nn("ffn1_out", act1, w1_out, res=x, alpha=0.5)
    ga0 = N_MAIN + N_HEADS
    gb0 = ga0 + D_MODEL
    half = D_MODEL // 2
    w_gd = jnp.concatenate(
        [w_in_t[ga0:ga0 + half], w_in_t[gb0:gb0 + half], w_in_t[ga0 + half:gb0], w_in_t[gb0 + half:],
         w_in_t[N_MAIN:N_MAIN + N_HEADS], jnp.zeros((DT_W - N_HEADS, D_MODEL), BF)], axis=0)
    w_mix_perm = w_in_t[0:3 * D_MODEL].reshape(3, 4, CONV_CB, D_MODEL).transpose(1, 0, 2, 3).reshape(3 * D_MODEL, D_MODEL)

    h2 = _rms_fwd("rms2_fwd", x1, rep["mix_norm"])
    p, got = _mm_nt("proj_main", h2, w_in_t, n=N_MAIN, tn=1024, out_dtype=BF, comm=_gather_comm(
        [shard["short_w_out"], shard["ssm_w_out"], shard["w_out"]]))
    p_gd = _mm_nt("proj_gd", h2, w_gd)
    short_w_out = got[0].reshape(D_MODEL, D_MODEL)
    ssm_w_out = got[1].reshape(D_INNER, D_MODEL)
    w_out = got[2].reshape(D_MODEL, D_MODEL)
    ya_in = _mix_a_fwd(p, short_conv_w)
    y_a = _mm_nn("short_out", ya_in, short_w_out)
    xconv, (got,) = _ssm_conv_fwd(p, ssm_conv_w, rep["ssm_conv_b"], comm=_gather_comm([shard["ffn2_w_out"]]))
    w2_out = got.reshape(D_FF, D_MODEL)
    dt, acs = _dt_fwd(p_gd, rep["dt_bias_pad"], rep["a_log_pad"])
    (y_ssm, hsave), (got,) = _ssd_fwd(xconv, dt, acs, rep["d_exp"], comm=_gather_comm([shard["ffn2_w_in"]], [True]))
    w2_in = got.reshape(2 * D_FF, D_MODEL)
    yn = _gnorm_fwd(y_ssm, p, rep["ssm_norm"])
    y_b = _mm_nn("ssm_out", yn, ssm_w_out, tk=1024)
    merged = _merge_fwd(p_gd, y_a, y_b)
    x2 = _mm_nn("mix_out", merged, w_out, res=x1)

    h3 = _rms_fwd("rms3_fwd", x2, rep["ffn2_norm"])
    gu2 = _mm_nt("ffn2_in", h3, w2_in, tn=FF_HALF, out_dtype=BF)
    act2 = _swiglu_fwd("swiglu2_fwd", gu2)
    x3 = _mm_nn("ffn2_out", act2, w2_out, res=x2, alpha=0.5)

    loss, dx3, dx3h, g_final = _final_loss(x3, rep["final_norm"], target)

    small = {"final_norm": g_final}
    dact2 = _mm_nt("ffn2_out_bwd_act", dx3h, w2_out, out_dtype=BF)
    g_w2_out = _mm_tn("ffn2_out_bwd_w", act2, dx3h, gdt, tm=FF_HALF)
    dgu2 = _swiglu_bwd("swiglu2_bwd", gu2, dact2)
    g_w2_in = _mm_tn("ffn2_in_bwd_w", dgu2, h3, gdt, tm=FF_HALF)
    dh3 = _mm_nn("ffn2_in_bwd_h", dgu2, w2_in, tk=FF_HALF)
    dx2, dx2b, small["ffn2_norm"] = _rms_bwd("rms3_bwd", x2, rep["ffn2_norm"], dh3, dx3, 1.0)

    dmerged = _mm_nt("mix_out_bwd_x", dx2b, w_out)
    g_w_out = _mm_tn("mix_out_bwd_w", merged, dx2b, gdt)
    dp_gd, dya, dyb = _merge_bwd(p_gd, y_a, y_b, dmerged)

    dya_in = _mm_nt("short_out_bwd_x", dya, short_w_out)
    g_short_w_out = _mm_tn("short_out_bwd_w", ya_in, dya, gdt)

    dyn = _mm_nt("ssm_out_bwd_x", dyb, ssm_w_out)
    g_ssm_w_out = _mm_tn("ssm_out_bwd_w", yn, dyb, gdt)
    late = [("ffn2_w_out", g_w2_out.reshape(N_DEV, FF_SHARD // 2, D_MODEL)),
            ("ffn2_w_in", g_w2_in.reshape(N_DEV, FF_SHARD, D_MODEL)),
            ("w_out", g_w_out.reshape(N_DEV, -1, D_MODEL)), ("short_w_out", g_short_w_out.reshape(N_DEV, -1, D_MODEL)),
            ("ssm_w_out", g_ssm_w_out.reshape(N_DEV, -1, D_MODEL))]
    (dy_ssm, dp, small["ssm_norm"]), sums = _reduce_start(
        late, lambda comm: _gnorm_bwd(y_ssm, p, rep["ssm_norm"], dyn, comm=comm))
    dp, g_short_conv = _mix_a_bwd(p, short_conv_w, dya_in, dp)
    first = [(n, a) for n, a in sums if n.startswith("ffn2")]
    second = [(n, a) for n, a in sums if not n.startswith("ffn2")]
    (dxconv, ddt, dacs, dd_lane), got = _ssd_bwd(
        xconv, dt, acs, rep["d_exp"], hsave, dy_ssm,
        comm=_chip_comm([a for _, a in first], [n == "ffn2_w_in" for n, _ in first]))
    recv.update({n: a for (n, _), a in zip(first, got)})
    small["ssm_D"] = dd_lane.reshape(N_HEADS, HEAD_DIM).sum(axis=1)[None, :]
    (dp, g_ssm_conv, small["ssm_conv_b"]), got = _ssm_conv_bwd(
        p, ssm_conv_w, rep["ssm_conv_b"], dxconv, dp, comm=_chip_comm([a for _, a in second]))
    recv.update({n: a for (n, _), a in zip(second, got)})
    dp_gd, dbias, dalog = _dt_bwd(p_gd, rep["dt_bias_pad"], rep["a_log_pad"], dt, ddt, dacs, dp_gd)
    small["ssm_dt_bias"] = dbias[:, :N_HEADS]
    small["ssm_A_log"] = dalog[:, :N_HEADS]

    g_main = _mm_tn("proj_main_bwd_w", dp, h2, gdt, tm=1024)
    g_gd = _mm_tn("proj_gd_bwd_w", dp_gd, h2, gdt)
    g_mix = g_main[0:3 * D_MODEL].reshape(4, 3, CONV_CB, D_MODEL).transpose(1, 0, 2, 3).reshape(3 * D_MODEL, D_MODEL)
    g_in_t = jnp.concatenate(
        [g_mix, g_main[3 * D_MODEL:], g_gd[2 * D_MODEL:2 * D_MODEL + N_HEADS],
         g_gd[0:half], g_gd[2 * half:3 * half], g_gd[half:2 * half], g_gd[3 * half:4 * half]], axis=0).reshape(
        N_DEV, IN_SHARD, D_MODEL)
    dh2, w_sums = _reduce_start(
        [("w_in", g_in_t)], lambda comm: _mm_nn("proj_mix_bwd_x", dp, w_mix_perm, tk=1024, kk=3 * D_MODEL, comm=comm))
    w_sum = w_sums[0][1]

    def w_piece(i):
        return _chip_comm([w_sum], rows=[W_GRAD_ROW_CUTS[i]])

    dh2, got0 = _mm_nn("proj_rest_bwd_x", dp, w_in_t, tk=1024, kk=N_MAIN - 3 * D_MODEL, a_off=3, b_off=3, res=dh2,
                       comm=w_piece(0))
    dh2, got1 = _mm_nn("proj_gd_bwd_x", dp_gd, w_gd, res=dh2, comm=w_piece(1))
    (dx1, dx1h, small["mix_norm"]), got2 = _rms_bwd("rms2_bwd", x1, rep["mix_norm"], dh2, dx2, 0.5, comm=w_piece(2))
    g_w1_out, got3 = _mm_tn("ffn1_out_bwd_w", act1, dx1h, gdt, tm=FF_HALF, comm=w_piece(3))
    rest = [("ffn1_w_out", g_w1_out.reshape(N_DEV, FF_SHARD // 2, D_MODEL)),
            ("short_conv_w", g_short_conv.reshape(3, N_DEV, -1).transpose(1, 0, 2)),
            ("ssm_conv_w", g_ssm_conv.reshape(4, N_DEV, -1).transpose(1, 0, 2))]
    dact1, got = _mm_nt("ffn1_out_bwd_act", dx1h, w1_out, out_dtype=BF,
                        comm=_join_comm(w_piece(4), _pair_comm([a for _, a in rest])))
    got4, sib = got[0], got[1:]
    rest_sums = [(n, _add_pairs("pairsum_" + n, a, b)) for (n, a), b in zip(rest, sib)]
    recv["w_in"] = jnp.concatenate([got0[0], got1[0], got2[0], got3[0], got4], axis=1)
    dgu1, got = _swiglu_bwd("swiglu1_bwd", gu1, dact1, comm=_chip_comm([a for _, a in rest_sums]))
    recv.update({n: a for (n, _), a in zip(rest_sums, got)})

    def part(tag, width, off, comm=None):
        out = _mm_tn("ffn1_in_bwd_w_" + tag, dgu1, h1, gdt, tm=FF_HALF, n=width, col_off=off, comm=comm)
        g, couts = (out, None) if comm is None else out
        return g.reshape(N_DEV, FF_SHARD, width), couts

    g_a, _ = part("a", 384, 0)
    g_b, sib = part("b", 384, 1, _pair_comm([g_a]))
    sum_a = _add_pairs("pairsum_ffn1_w_in_a", g_a, sib[0])
    g_c, (recv_a, sib_b) = part("c", 256, 3, _join_comm(_chip_comm([sum_a], [True]), _pair_comm([g_b])))
    sum_b = _add_pairs("pairsum_ffn1_w_in_b", g_b, sib_b)
    dh1, (recv_b, sib_c) = _mm_nn("ffn1_in_bwd_h", dgu1, w1_in, tk=FF_HALF,
                                  comm=_join_comm(_chip_comm([sum_b], [True]), _pair_comm([g_c])))
    sum_c = _add_pairs("pairsum_ffn1_w_in_c", g_c, sib_c)
    (dx0, _, small["ffn1_norm"]), (recv_c,) = _rms_bwd("rms1_bwd", x, rep["ffn1_norm"], dh1, dx1, 1.0,
                                                        comm=_chip_comm([sum_c], [True]))
    recv["ffn1_w_in"] = jnp.concatenate([recv_a, recv_b, recv_c], axis=2)
    return dx0, recv, _pack_small(small, loss[:, 0:1])


_SMALL = [("ffn1_norm", 1024), ("mix_norm", 1024), ("ssm_conv_b", 4096), ("ssm_dt_bias", 32), ("ssm_A_log", 32),
          ("ssm_D", 32), ("ssm_norm", 2048), ("ffn2_norm", 1024), ("final_norm", 1024)]
SMALL_W = 10368


def _pack_small(d, loss=None):
    parts = [d[n].reshape(1, -1).astype(F32) for n, _ in _SMALL]
    used = sum(sz for _, sz in _SMALL)
    tail = jnp.zeros((1, SMALL_W - used), F32)
    if loss is not None:
        tail = tail.at[:, 0:1].set(loss)
    return jnp.concatenate(parts + [tail], axis=1)


def _adamw_small(parts, w, m, v):
    n_par = len(_SMALL)
    bc1 = 1.0 - ADAM_B1 ** ADAM_STEP
    bc2 = 1.0 - ADAM_B2 ** ADAM_STEP
    used = sum(sz for _, sz in _SMALL)

    def body(*refs):
        p_ref = refs[0]
        ins = refs[1:1 + 3 * n_par]
        outs = refs[1 + 3 * n_par:]
        g_all = p_ref[0]
        for k in range(1, N_DEV):
            g_all = g_all + p_ref[k]
        off = 0
        for i, (_, sz) in enumerate(_SMALL):
            g = g_all[:, off:off + sz]
            w_ref, m_ref, v_ref = ins[3 * i:3 * i + 3]
            nm = ADAM_B1 * m_ref[...] + (1.0 - ADAM_B1) * g
            nv = ADAM_B2 * v_ref[...] + (1.0 - ADAM_B2) * (g * g)
            outs[4 * i][...] = g
            outs[4 * i + 1][...] = -ADAM_LR * ((nm / bc1) / (jnp.sqrt(nv / bc2) + ADAM_EPS) + ADAM_WD * w_ref[...])
            outs[4 * i + 2][...] = nm
            outs[4 * i + 3][...] = nv
            off += sz
        outs[4 * n_par][...] = g_all[:, used:SMALL_W]

    args = [parts]
    out_shape = []
    for name, sz in _SMALL:
        args += [w[name], m[name], v[name]]
        out_shape += [jax.ShapeDtypeStruct((1, sz), F32)] * 4
    out_shape.append(jax.ShapeDtypeStruct((1, SMALL_W - used), F32))
    res = pl.pallas_call(body, name="adamw_small", out_shape=out_shape,
                         compiler_params=pltpu.CompilerParams(vmem_limit_bytes=VMEM_LIMIT_V7X))(*args)
    return {name: tuple(res[4 * i:4 * i + 4]) for i, (name, _) in enumerate(_SMALL)}, res[-1]


_SHARDED = ["ffn1_w_in", "ffn1_w_out", "w_in", "short_conv_w", "short_w_out", "ssm_conv_w", "ssm_w_out", "w_out",
            "ffn2_w_in", "ffn2_w_out"]
_TRANSPOSED = ("ffn1_w_in", "w_in", "ffn2_w_in")
_ORDER = ["ffn1_norm", "ffn1_w_in", "ffn1_w_out", "mix_norm", "w_in", "short_conv_w", "short_w_out", "ssm_conv_w",
          "ssm_conv_b", "ssm_dt_bias", "ssm_A_log", "ssm_D", "ssm_norm", "ssm_w_out", "w_out", "ffn2_norm",
          "ffn2_w_in", "ffn2_w_out", "final_norm"]


def kernel(x, ffn1_norm, ffn1_w_in, ffn1_w_out, mix_norm, w_in, short_conv_w, short_w_out, ssm_conv_w, ssm_conv_b, ssm_dt_bias, ssm_A_log, ssm_D, ssm_norm, ssm_w_out, w_out, ffn2_norm, ffn2_w_in, ffn2_w_out, final_norm, loss_target, m_ffn1_norm, m_ffn1_w_in, m_ffn1_w_out, m_mix_norm, m_w_in, m_short_conv_w, m_short_w_out, m_ssm_conv_w, m_ssm_conv_b, m_ssm_dt_bias, m_ssm_A_log, m_ssm_D, m_ssm_norm, m_ssm_w_out, m_w_out, m_ffn2_norm, m_ffn2_w_in, m_ffn2_w_out, m_final_norm, v_ffn1_norm, v_ffn1_w_in, v_ffn1_w_out, v_mix_norm, v_w_in, v_short_conv_w, v_short_w_out, v_ssm_conv_w, v_ssm_conv_b, v_ssm_dt_bias, v_ssm_A_log, v_ssm_D, v_ssm_norm, v_ssm_w_out, v_w_out, v_ffn2_norm, v_ffn2_w_in, v_ffn2_w_out, v_final_norm):
    w = dict(ffn1_norm=ffn1_norm, ffn1_w_in=ffn1_w_in, ffn1_w_out=ffn1_w_out, mix_norm=mix_norm, w_in=w_in,
             short_conv_w=short_conv_w, short_w_out=short_w_out, ssm_conv_w=ssm_conv_w, ssm_conv_b=ssm_conv_b,
             ssm_dt_bias=ssm_dt_bias, ssm_A_log=ssm_A_log, ssm_D=ssm_D, ssm_norm=ssm_norm, ssm_w_out=ssm_w_out,
             w_out=w_out, ffn2_norm=ffn2_norm, ffn2_w_in=ffn2_w_in, ffn2_w_out=ffn2_w_out, final_norm=final_norm)
    m = dict(ffn1_norm=m_ffn1_norm, ffn1_w_in=m_ffn1_w_in, ffn1_w_out=m_ffn1_w_out, mix_norm=m_mix_norm, w_in=m_w_in,
             short_conv_w=m_short_conv_w, short_w_out=m_short_w_out, ssm_conv_w=m_ssm_conv_w,
             ssm_conv_b=m_ssm_conv_b, ssm_dt_bias=m_ssm_dt_bias, ssm_A_log=m_ssm_A_log, ssm_D=m_ssm_D,
             ssm_norm=m_ssm_norm, ssm_w_out=m_ssm_w_out, w_out=m_w_out, ffn2_norm=m_ffn2_norm,
             ffn2_w_in=m_ffn2_w_in, ffn2_w_out=m_ffn2_w_out, final_norm=m_final_norm)
    v = dict(ffn1_norm=v_ffn1_norm, ffn1_w_in=v_ffn1_w_in, ffn1_w_out=v_ffn1_w_out, mix_norm=v_mix_norm, w_in=v_w_in,
             short_conv_w=v_short_conv_w, short_w_out=v_short_w_out, ssm_conv_w=v_ssm_conv_w,
             ssm_conv_b=v_ssm_conv_b, ssm_dt_bias=v_ssm_dt_bias, ssm_A_log=v_ssm_A_log, ssm_D=v_ssm_D,
             ssm_norm=v_ssm_norm, ssm_w_out=v_ssm_w_out, w_out=v_w_out, ffn2_norm=v_ffn2_norm,
             ffn2_w_in=v_ffn2_w_in, ffn2_w_out=v_ffn2_w_out, final_norm=v_final_norm)
    shapes = {n: w[n].shape for n in _ORDER}

    def local(d, n):
        return d[n][0].T if n in _TRANSPOSED else d[n][0]

    shard = {n: local(w, n) for n in _SHARDED}

    wire = {n: (shard[n] if n in ("short_conv_w", "ssm_conv_w") else shard[n].astype(BF)) for n in _SHARDED}
    rep = {
        "ffn1_norm": ffn1_norm, "mix_norm": mix_norm, "ffn2_norm": ffn2_norm, "ssm_norm": ssm_norm,
        "ssm_conv_b": ssm_conv_b, "final_norm": final_norm.reshape(1, D_MODEL),
        "dt_bias_pad": _pad_lanes(ssm_dt_bias, DT_W), "a_log_pad": _pad_lanes(ssm_A_log, DT_W),
        "d_exp": jnp.repeat(ssm_D, HEAD_DIM, axis=1),
    }
    grad_x, parts, packed = _train_step(x[0], loss_target[0], wire, rep)

    out_g, out_d, out_m, out_v = {}, {}, {}, {}
    for n in _SHARDED:
        if n == "ssm_w_out":
            res, (small_parts,) = _adamw("adamw_" + n, parts[n], shard[n], local(m, n), local(v, n),
                                         comm=_gather_comm([packed]))
        else:
            res = _adamw("adamw_" + n, parts[n], shard[n], local(m, n), local(v, n))
        out_g[n], out_d[n], out_m[n], out_v[n] = [(r.T if n in _TRANSPOSED else r).reshape(shapes[n]) for r in res]
    row = lambda d: {n: d[n].reshape(1, -1) for n, _ in _SMALL}
    sres, loss_row = _adamw_small(small_parts, row(w), row(m), row(v))
    for n, _ in _SMALL:
        out_g[n], out_d[n], out_m[n], out_v[n] = [r.reshape(shapes[n]) for r in sres[n]]
    loss = loss_row[0, 0]
    return (loss, grad_x[None], *[out_g[n] for n in _ORDER], *[out_d[n] for n in _ORDER],
            *[out_m[n] for n in _ORDER], *[out_v[n] for n in _ORDER])
```

```python
import functools

import jax
import jax.numpy as jnp
from jax import lax
from jax.experimental import pallas as pl
from jax.experimental.pallas import tpu as pltpu

F32 = jnp.float32
BF = jnp.bfloat16

N_DEV = 8
D_MODEL = 1024
D_FF = 2816
D_INNER = 2048
D_XBC = 4096
N_HEADS = 32
HEAD_DIM = 64
N_GROUPS = 8
D_STATE = 128
CHUNK = 64
GROUP_W = D_INNER // N_GROUPS
NORM_EPS = 1e-5
N_IN = 11296
FF_SHARD = 2 * D_FF // N_DEV
FF_HALF = D_FF // 2
IN_SHARD = N_IN // N_DEV

OFF_B, OFF_C, OFF_XA, OFF_Z, OFF_XBC = 0, 1024, 2048, 3072, 5120
N_MAIN = 9216
OFF_DT = 2048
DT_W = 128
N_GD = 2048 + DT_W
W_GRAD_ROW_CUTS = [(0, 512), (512, 720), (720, 896), (896, 1152), (1152, 1412)]

ADAM_LR, ADAM_B1, ADAM_B2, ADAM_EPS, ADAM_WD, ADAM_STEP = 0.001, 0.9, 0.999, 1e-08, 0.01, 10

VMEM_LIMIT_V7X = 56 * 1024 * 1024
TM = 1024
TN_MAX_TOKENS = 2048
TE = 512
ADAM_COL_TILE = 256
GATHER_PIECES = 4
GATHER_PIECE_MIN_ROWS = 512


def _params(*sem):
    return pltpu.CompilerParams(dimension_semantics=sem, vmem_limit_bytes=VMEM_LIMIT_V7X)


_DIMS = {
    "nn": (((1,), (0,)), ((), ())),
    "nt": (((1,), (1,)), ((), ())),
    "tn": (((0,), (0,)), ((), ())),
}


def _dot(a, b, mode="nn"):
    return lax.dot_general(a, b, _DIMS[mode], preferred_element_type=F32)


def _sigmoid(x):
    return 1.0 / (1.0 + jnp.exp(-x))


class _Comm:
    def __init__(self, inputs, out_shapes, sems, start, finish):
        self.inputs, self.out_shapes, self.sems, self.start, self.finish = inputs, out_shapes, sems, start, finish


def _pcall(name, body, grid, in_specs, out_specs, out_shape, args, scratch=(), sem=None, comm=None, aliases=None):
    single = not isinstance(out_shape, (list, tuple))
    out_shapes = [out_shape] if single else list(out_shape)
    out_specs = [out_specs] if single else list(out_specs)
    n_in, n_out, n_scr = len(args), len(out_shapes), len(scratch)
    aliases = {} if aliases is None else aliases
    if comm is None:
        res = pl.pallas_call(
            body, name=name, grid=grid, in_specs=list(in_specs), out_specs=out_specs, out_shape=out_shapes,
            scratch_shapes=list(scratch), input_output_aliases=aliases, compiler_params=_params(*sem))(*args)
        return (res[0] if single else res), []
    nci, nco = len(comm.inputs), len(comm.out_shapes)

    def wrapped(*refs):
        a = refs[:n_in]
        ci = refs[n_in:n_in + nci]
        o0 = n_in + nci
        o = refs[o0:o0 + n_out]
        co = refs[o0 + n_out:o0 + n_out + nco]
        s0 = o0 + n_out + nco
        s = refs[s0:s0 + n_scr]
        cs = refs[s0 + n_scr:]
        pids = [pl.program_id(i) for i in range(len(grid))]
        first = functools.reduce(jnp.logical_and, [p == 0 for p in pids])
        last = functools.reduce(jnp.logical_and, [p == g - 1 for p, g in zip(pids, grid)])

        @pl.when(first)
        def _():
            comm.start(ci, co, cs)

        body(*a, *o, *s)

        @pl.when(last)
        def _():
            comm.finish(ci, co, cs)

    any_spec = pl.BlockSpec(memory_space=pl.ANY)
    res = pl.pallas_call(
        wrapped, name=name, grid=grid, in_specs=list(in_specs) + [any_spec] * nci,
        out_specs=out_specs + [any_spec] * nco, out_shape=out_shapes + list(comm.out_shapes),
        scratch_shapes=list(scratch) + list(comm.sems), input_output_aliases=aliases,
        compiler_params=_params(*(("arbitrary",) * len(grid))))(*args, *comm.inputs)
    core = res[:n_out]
    return (core[0] if single else core), list(res[n_out:])


def _comm_call(name, comm):
    nci, nco = len(comm.inputs), len(comm.out_shapes)

    def body(*refs):
        ci, co, cs = refs[:nci], refs[nci:nci + nco], refs[nci + nco:]
        comm.start(ci, co, cs)
        comm.finish(ci, co, cs)

    any_spec = pl.BlockSpec(memory_space=pl.ANY)
    return pl.pallas_call(
        body, name=name, in_specs=[any_spec] * nci, out_specs=[any_spec] * nco, out_shape=list(comm.out_shapes),
        scratch_shapes=list(comm.sems), compiler_params=pltpu.CompilerParams(has_side_effects=True))(*comm.inputs)


def _remote(src, dst, ssem, rsem, dev):
    return pltpu.make_async_remote_copy(src_ref=src, dst_ref=dst, send_sem=ssem, recv_sem=rsem, device_id=dev,
                                        device_id_type=pl.DeviceIdType.MESH)


def _place():
    x, y, c = lax.axis_index("x"), lax.axis_index("y"), lax.axis_index("c")
    other_chips = [(1 - x, y), (x, 1 - y), (1 - x, 1 - y)]
    return x, y, c, other_chips


def _slot(x, y, c, swap):
    return 4 * y + 2 * x + c if swap else 4 * x + 2 * y + c


def _chip_slot(x, y, swap):
    return 2 * y + x if swap else 2 * x + y


def _gather_comm(shards, swaps=None):
    n = len(shards)
    per = N_DEV - 1
    swaps = [False] * n if swaps is None else swaps
    pieces = []
    for i, a in enumerate(shards):
        rows = a.shape[0]
        k = GATHER_PIECES if (a.ndim == 2 and rows >= GATHER_PIECE_MIN_ROWS) else 1
        step = -(-rows // (k * 8)) * 8
        if k == 1:
            pieces.append((i, 0, None))
        else:
            pieces += [(i, r, min(step, rows - r)) for r in range(0, rows, step)]
    m = len(pieces)

    def src(ins, v):
        i, r, cnt = pieces[v]
        return ins[i] if cnt is None else ins[i].at[pl.ds(r, cnt)]

    def place(outs, v, x, y, c):
        i, r, cnt = pieces[v]
        blk = outs[i].at[_slot(x, y, c, swaps[i])]
        return blk if cnt is None else blk.at[pl.ds(r, cnt)]

    def start(ins, outs, sems):
        send, recv, loc = sems
        x, y, c, chips = _place()
        for v in range(m):
            me = place(outs, v, x, y, c)
            pltpu.make_async_copy(src(ins, v), me, loc.at[v]).start()
            _remote(src(ins, v), me, send.at[per * v], recv.at[per * v], (x, y, 1 - c)).start()
        for j, (qx, qy) in enumerate(chips):
            for v in range(m):
                _remote(src(ins, v), place(outs, v, x, y, c), send.at[per * v + 1 + j], recv.at[per * v + 1 + j],
                        (qx, qy, c)).start()

    def finish(ins, outs, sems):
        send, recv, loc = sems
        x, y, c, chips = _place()
        sib = (x, y, 1 - c)
        for v in range(m):
            for j, (qx, qy) in enumerate(chips):
                blk = place(outs, v, qx, qy, c)
                _remote(blk, blk, send.at[per * v + 1 + j], recv.at[per * v + 1 + j], (qx, qy, c)).wait_recv()
                _remote(blk, blk, send.at[per * v + 4 + j], recv.at[per * v + 4 + j], sib).start()
        for v in range(m):
            blk = place(outs, v, x, y, 1 - c)
            _remote(blk, blk, send.at[per * v], recv.at[per * v], sib).wait_recv()
            for j, (qx, qy) in enumerate(chips):
                blk = place(outs, v, qx, qy, 1 - c)
                _remote(blk, blk, send.at[per * v + 4 + j], recv.at[per * v + 4 + j], sib).wait_recv()
        for v in range(m):
            own = place(outs, v, x, y, c)
            for k in range(per):
                _remote(src(ins, v), own, send.at[per * v + k], recv.at[per * v + k], sib).wait_send()
            pltpu.make_async_copy(src(ins, v), own, loc.at[v]).wait()

    out_shapes = [jax.ShapeDtypeStruct((N_DEV,) + tuple(a.shape), a.dtype) for a in shards]
    sems = [pltpu.SemaphoreType.DMA((per * m,)), pltpu.SemaphoreType.DMA((per * m,)), pltpu.SemaphoreType.DMA((m,))]
    return _Comm(list(shards), out_shapes, sems, start, finish)


def _pair_comm(slots):
    n = len(slots)

    def copies(ins, outs, sems):
        send, recv = sems
        x, y, c, _ = _place()
        sib = (x, y, 1 - c)
        out = []
        for i in range(n):
            for q in range(4):
                out.append(_remote(ins[i].at[2 * q + 1 - c], outs[i].at[q], send.at[4 * i + q], recv.at[4 * i + q], sib))
        return out

    def start(ins, outs, sems):
        for cp in copies(ins, outs, sems):
            cp.start()

    def finish(ins, outs, sems):
        for cp in copies(ins, outs, sems):
            cp.wait_send()
            cp.wait_recv()

    out_shapes = [jax.ShapeDtypeStruct((4,) + tuple(a.shape[1:]), a.dtype) for a in slots]
    sems = [pltpu.SemaphoreType.DMA((4 * n,)), pltpu.SemaphoreType.DMA((4 * n,))]
    return _Comm(list(slots), out_shapes, sems, start, finish)


def _chip_comm(chip_sums, swaps=None, rows=None):
    n = len(chip_sums)
    swaps = [False] * n if swaps is None else swaps
    rows = [None] * n if rows is None else rows

    def src(ins, i, q):
        return ins[i].at[q] if rows[i] is None else ins[i].at[q, pl.ds(rows[i][0], rows[i][1] - rows[i][0])]

    def start(ins, outs, sems):
        send, recv, loc = sems
        x, y, c, chips = _place()
        for i in range(n):
            mine = _chip_slot(x, y, swaps[i])
            pltpu.make_async_copy(src(ins, i, mine), outs[i].at[mine], loc.at[i]).start()
            for j, (qx, qy) in enumerate(chips):
                _remote(src(ins, i, _chip_slot(qx, qy, swaps[i])), outs[i].at[mine], send.at[3 * i + j],
                        recv.at[3 * i + j], (qx, qy, c)).start()

    def finish(ins, outs, sems):
        send, recv, loc = sems
        x, y, c, chips = _place()
        for i in range(n):
            mine = _chip_slot(x, y, swaps[i])
            for j, (qx, qy) in enumerate(chips):
                theirs = _chip_slot(qx, qy, swaps[i])
                cp = _remote(src(ins, i, theirs), outs[i].at[theirs], send.at[3 * i + j], recv.at[3 * i + j], (qx, qy, c))
                cp.wait_send()
                cp.wait_recv()
            pltpu.make_async_copy(src(ins, i, mine), outs[i].at[mine], loc.at[i]).wait()

    def out_shape(a, r):
        shape = a.shape if r is None else (a.shape[0], r[1] - r[0]) + tuple(a.shape[2:])
        return jax.ShapeDtypeStruct(shape, a.dtype)

    out_shapes = [out_shape(a, r) for a, r in zip(chip_sums, rows)]
    sems = [pltpu.SemaphoreType.DMA((3 * n,)), pltpu.SemaphoreType.DMA((3 * n,)), pltpu.SemaphoreType.DMA((n,))]
    return _Comm(list(chip_sums), out_shapes, sems, start, finish)


def _join_comm(a, b):
    na_i, na_o, na_s = len(a.inputs), len(a.out_shapes), len(a.sems)

    def start(ins, outs, sems):
        a.start(ins[:na_i], outs[:na_o], sems[:na_s])
        b.start(ins[na_i:], outs[na_o:], sems[na_s:])

    def finish(ins, outs, sems):
        a.finish(ins[:na_i], outs[:na_o], sems[:na_s])
        b.finish(ins[na_i:], outs[na_o:], sems[na_s:])

    return _Comm(a.inputs + b.inputs, a.out_shapes + b.out_shapes, a.sems + b.sems, start, finish)


def _row_tile(r):
    for cand in (256, 128):
        if r > cand and r % cand == 0:
            return cand
    return r


def _add_pairs(name, slots, sib):
    r, c = slots.shape[1:]
    tr = _row_tile(r)

    def body(core_ref, s_ref, b_ref, o_ref):
        o_ref[...] = (s_ref[...].astype(F32) + b_ref[...].astype(F32)).astype(o_ref.dtype)

    core = jnp.full((1,), lax.axis_index("c"), jnp.int32)
    return pl.pallas_call(
        body, name=name,
        grid_spec=pltpu.PrefetchScalarGridSpec(
            num_scalar_prefetch=1, grid=(4, r // tr),
            in_specs=[pl.BlockSpec((None, tr, c), lambda q, i, core_ref: (2 * q + core_ref[0], i, 0)),
                      pl.BlockSpec((None, tr, c), lambda q, i, core_ref: (q, i, 0))],
            out_specs=pl.BlockSpec((None, tr, c), lambda q, i, core_ref: (q, i, 0))),
        out_shape=jax.ShapeDtypeStruct((4, r, c), slots.dtype),
        compiler_params=_params("parallel", "parallel"))(core, slots, sib)


def _matmul(name, mode, a, b, grid, a_spec, b_spec, o_spec, out_shape, acc_shape,
            res=None, res_spec=None, alpha=1.0, comm=None):
    nk = grid[-1]
    has_res = res is not None

    def body(*refs):
        if has_res:
            a_ref, b_ref, r_ref, o_ref = refs[:4]
        else:
            a_ref, b_ref, o_ref = refs[:3]
            r_ref = None
        part = _dot(a_ref[...], b_ref[...], mode)

        def finish(v):
            if alpha != 1.0:
                v = v * alpha
            if has_res:
                v = r_ref[...] + v
            o_ref[...] = v.astype(o_ref.dtype)

        if nk == 1:
            finish(part)
        else:
            acc = refs[-1]
            k = pl.program_id(len(grid) - 1)

            @pl.when(k == 0)
            def _():
                acc[...] = part

            @pl.when(k > 0)
            def _():
                acc[...] += part

            @pl.when(k == nk - 1)
            def _():
                finish(acc[...])

    in_specs = [a_spec, b_spec] + ([res_spec] if has_res else [])
    args = (a, b) + ((res,) if has_res else ())
    scratch = [] if nk == 1 else [pltpu.VMEM(acc_shape, F32)]
    sem = ("parallel",) * (len(grid) - 1) + ("arbitrary",)
    out, couts = _pcall(name, body, grid, in_specs, o_spec, out_shape, args, scratch, sem, comm)
    return out if comm is None else (out, couts)


def _mm_nn(name, a, b, out_dtype=F32, res=None, alpha=1.0, tk=None, kk=None, a_off=0, b_off=0, comm=None):
    t = a.shape[0]
    kk = a.shape[1] if kk is None else kk
    n = b.shape[1]
    tk = kk if tk is None else tk
    grid = (t // TM, 1, kk // tk)
    return _matmul(
        name, "nn", a, b, grid,
        pl.BlockSpec((TM, tk), lambda i, j, k: (i, k + a_off)),
        pl.BlockSpec((tk, n), lambda i, j, k: (k + b_off, 0)),
        pl.BlockSpec((TM, n), lambda i, j, k: (i, 0)),
        jax.ShapeDtypeStruct((t, n), out_dtype), (TM, n),
        res=res, res_spec=pl.BlockSpec((TM, n), lambda i, j, k: (i, 0)), alpha=alpha, comm=comm)


def _mm_nt(name, a, b, n=None, tn=None, tk=None, out_dtype=F32, comm=None):
    t, kk = a.shape
    n = b.shape[0] if n is None else n
    tn = n if tn is None else tn
    tk = kk if tk is None else tk
    grid = (n // tn, t // TM, kk // tk)
    return _matmul(
        name, "nt", a, b, grid,
        pl.BlockSpec((TM, tk), lambda j, i, k: (i, k)),
        pl.BlockSpec((tn, tk), lambda j, i, k: (j, k)),
        pl.BlockSpec((TM, tn), lambda j, i, k: (i, j)),
        jax.ShapeDtypeStruct((t, n), out_dtype), (TM, tn), comm=comm)


def _mm_tn(name, a, b, out_dtype, tm=None, n=None, col_off=0, comm=None):
    t, m = a.shape
    n = b.shape[1] if n is None else n
    tm = m if tm is None else tm
    tk = t if t <= TN_MAX_TOKENS else TM
    grid = (m // tm, 1, t // tk)
    return _matmul(
        name, "tn", a, b, grid,
        pl.BlockSpec((tk, tm), lambda j, i, k: (k, j)),
        pl.BlockSpec((tk, n), lambda j, i, k: (k, col_off)),
        pl.BlockSpec((tm, n), lambda j, i, k: (j, 0)),
        jax.ShapeDtypeStruct((m, n), out_dtype), (tm, n), comm=comm)


def _rms_fwd(name, x, w):
    t, d = x.shape

    def body(x_ref, w_ref, h_ref):
        xv = x_ref[...]
        rstd = lax.rsqrt(jnp.mean(xv * xv, axis=-1, keepdims=True) + NORM_EPS)
        h_ref[...] = (xv * rstd * w_ref[...]).astype(h_ref.dtype)

    return pl.pallas_call(
        body, name=name, grid=(t // TE,),
        in_specs=[pl.BlockSpec((TE, d), lambda i: (i, 0)), pl.BlockSpec((1, d), lambda i: (0, 0))],
        out_specs=pl.BlockSpec((TE, d), lambda i: (i, 0)),
        out_shape=jax.ShapeDtypeStruct((t, d), BF), compiler_params=_params("parallel"))(x, w)


def _rms_bwd(name, x, w, dh, dres, out_scale, comm=None):
    t, d = x.shape

    def body(x_ref, w_ref, dh_ref, dres_ref, dx_ref, dxb_ref, dw_ref):
        i = pl.program_id(0)
        xv = x_ref[...]
        rstd = lax.rsqrt(jnp.mean(xv * xv, axis=-1, keepdims=True) + NORM_EPS)
        xhat = xv * rstd
        dhv = dh_ref[...]
        wd = dhv * w_ref[...]
        proj = jnp.mean(wd * xhat, axis=-1, keepdims=True)
        dx = dres_ref[...] + rstd * (wd - xhat * proj)
        dx_ref[...] = dx
        dxb_ref[...] = (dx * out_scale).astype(BF)
        part = jnp.sum(dhv * xhat, axis=0, keepdims=True)

        @pl.when(i == 0)
        def _():
            dw_ref[...] = part

        @pl.when(i > 0)
        def _():
            dw_ref[...] += part

    row = pl.BlockSpec((TE, d), lambda i: (i, 0))
    vec = pl.BlockSpec((1, d), lambda i: (0, 0))
    outs, couts = _pcall(
        name, body, (t // TE,), [row, vec, row, row], [row, row, vec],
        [jax.ShapeDtypeStruct((t, d), F32), jax.ShapeDtypeStruct((t, d), BF), jax.ShapeDtypeStruct((1, d), F32)],
        (x, w, dh, dres), (), ("arbitrary",), comm)
    return outs if comm is None else (outs, couts)


def _final_loss(x, w, target):
    t, d = x.shape

    def body(x_ref, w_ref, t_ref, loss_ref, dx_ref, dxb_ref, dw_ref):
        i = pl.program_id(0)
        xv = x_ref[...]
        rstd = lax.rsqrt(jnp.mean(xv * xv, axis=-1, keepdims=True) + NORM_EPS)
        xhat = xv * rstd
        err = xhat * w_ref[...] - t_ref[...]
        lpart = 0.5 * jnp.sum(jnp.mean(err * err, axis=-1, keepdims=True), axis=0, keepdims=True)
        dy = err * (1.0 / d)
        wd = dy * w_ref[...]
        proj = jnp.mean(wd * xhat, axis=-1, keepdims=True)
        dx = rstd * (wd - xhat * proj)
        dx_ref[...] = dx
        dxb_ref[...] = (0.5 * dx).astype(BF)
        part = jnp.sum(dy * xhat, axis=0, keepdims=True)
        lfull = jnp.broadcast_to(lpart, (1, 128))

        @pl.when(i == 0)
        def _():
            dw_ref[...] = part
            loss_ref[...] = lfull

        @pl.when(i > 0)
        def _():
            dw_ref[...] += part
            loss_ref[...] += lfull

    row = pl.BlockSpec((TE, d), lambda i: (i, 0))
    vec = pl.BlockSpec((1, d), lambda i: (0, 0))
    return pl.pallas_call(
        body, name="final_loss", grid=(t // TE,), in_specs=[row, vec, row],
        out_specs=[pl.BlockSpec((1, 128), lambda i: (0, 0)), row, row, vec],
        out_shape=[jax.ShapeDtypeStruct((1, 128), F32), jax.ShapeDtypeStruct((t, d), F32),
                   jax.ShapeDtypeStruct((t, d), BF), jax.ShapeDtypeStruct((1, d), F32)],
        compiler_params=_params("arbitrary"))(x, w, target)


def _swiglu_fwd(name, gu, comm=None):
    t = gu.shape[0]

    def body(g_ref, u_ref, a_ref):
        g = g_ref[...].astype(F32)
        a_ref[...] = (g * _sigmoid(g) * u_ref[...].astype(F32)).astype(BF)

    blk = (TE, FF_HALF)
    out, couts = _pcall(
        name, body, (t // TE, 2),
        [pl.BlockSpec(blk, lambda i, j: (i, 2 * j)), pl.BlockSpec(blk, lambda i, j: (i, 2 * j + 1))],
        pl.BlockSpec(blk, lambda i, j: (i, j)), jax.ShapeDtypeStruct((t, D_FF), BF),
        (gu, gu), (), ("parallel", "parallel"), comm)
    return out if comm is None else (out, couts)


def _swiglu_bwd(name, gu, dact, comm=None):
    t = gu.shape[0]

    def body(g_ref, u_ref, da_ref, o_ref):
        g = g_ref[...].astype(F32)
        da = da_ref[...].astype(F32)
        s = _sigmoid(g)
        o_ref[:, 0:FF_HALF] = (da * u_ref[...].astype(F32) * (s * (1.0 + g * (1.0 - s)))).astype(BF)
        o_ref[:, FF_HALF:2 * FF_HALF] = (da * g * s).astype(BF)

    blk = (TE, FF_HALF)
    out, couts = _pcall(
        name, body, (t // TE, 2),
        [pl.BlockSpec(blk, lambda i, j: (i, 2 * j)), pl.BlockSpec(blk, lambda i, j: (i, 2 * j + 1)),
         pl.BlockSpec(blk, lambda i, j: (i, j))],
        pl.BlockSpec((TE, 2 * FF_HALF), lambda i, j: (i, j)),
        jax.ShapeDtypeStruct((t, 2 * D_FF), BF), (gu, gu, dact), (), ("parallel", "parallel"), comm)
    return out if comm is None else (out, couts)


CONV_CB = 256


CONV_ROWS = 64
CONV_HALO = 16


def _taps_down(ext, w, k):
    shifted = [pltpu.roll(ext, k - 1 - j, 0)[CONV_HALO:] for j in range(k - 1)] + [ext[CONV_HALO:]]
    out = shifted[k - 1] * w[k - 1:k, :]
    for j in range(k - 1):
        out = out + shifted[j] * w[j:j + 1, :]
    return out, shifted


def _taps_up(ext, w, k):
    rows = ext.shape[0]
    n = rows - CONV_HALO
    out = ext[:n] * w[k - 1:k, :]
    for j in range(k - 1):
        out = out + pltpu.roll(ext, rows - (k - 1 - j), 0)[:n] * w[j:j + 1, :]
    return out


def _rows_before(ref, i, r0):
    start = pl.multiple_of(jnp.maximum(r0 - CONV_HALO, 0), CONV_HALO)
    return jnp.where(i > 0, ref[pl.ds(start, CONV_HALO), :].astype(F32), 0.0)


def _rows_after(ref, r0, t):
    start = pl.multiple_of(jnp.minimum(r0 + CONV_ROWS, t - CONV_HALO), CONV_HALO)
    return ref[pl.ds(start, CONV_HALO), :].astype(F32)


def _fold8(v):
    return v.reshape(v.shape[0] // 8, 8, v.shape[1]).sum(axis=0)


def _silu_grad(pre):
    s = _sigmoid(pre)
    return s * (1.0 + pre * (1.0 - s))


def _pspec(t, off):
    base = off // CONV_CB
    return pl.BlockSpec((t, CONV_CB), lambda j: (0, base + j))


def _mix_a_fwd(p, conv_w):
    t = p.shape[0]

    def body(b_ref, c_ref, xa_ref, w_ref, o_ref):
        w = w_ref[...]

        def step(i, carry):
            r0 = pl.multiple_of(i * CONV_ROWS, CONV_ROWS)
            rows = pl.ds(r0, CONV_ROWS)
            q = c_ref[rows, :].astype(F32) * xa_ref[rows, :].astype(F32)
            q_before = _rows_before(c_ref, i, r0) * _rows_before(xa_ref, i, r0)
            va, _ = _taps_down(jnp.concatenate([q_before, q], axis=0), w, 3)
            o_ref[rows, :] = (b_ref[rows, :].astype(F32) * va).astype(BF)
            return carry

        lax.fori_loop(0, t // CONV_ROWS, step, 0)

    return pl.pallas_call(
        body, name="mix_a_fwd", grid=(D_MODEL // CONV_CB,),
        in_specs=[_pspec(t, OFF_B), _pspec(t, OFF_C), _pspec(t, OFF_XA),
                  pl.BlockSpec((3, CONV_CB), lambda j: (0, j))],
        out_specs=pl.BlockSpec((t, CONV_CB), lambda j: (0, j)),
        out_shape=jax.ShapeDtypeStruct((t, D_MODEL), BF), compiler_params=_params("parallel"))(p, p, p, conv_w)


def _mix_a_bwd(p, conv_w, dya, dp):
    t = p.shape[0]

    def body(b_ref, c_ref, xa_ref, w_ref, dy_ref, dp_in, dp_ref, dw_ref):
        del dp_in
        w = w_ref[...]
        n = t // CONV_ROWS

        def step(i, acc):
            r0 = pl.multiple_of(i * CONV_ROWS, CONV_ROWS)
            rows = pl.ds(r0, CONV_ROWS)
            cv = c_ref[rows, :].astype(F32)
            xav = xa_ref[rows, :].astype(F32)
            q_before = _rows_before(c_ref, i, r0) * _rows_before(xa_ref, i, r0)
            va, shifted = _taps_down(jnp.concatenate([q_before, cv * xav], axis=0), w, 3)
            dyv = dy_ref[rows, :]
            dp_ref[rows, 0:CONV_CB] = (dyv * va).astype(BF)
            dv = dyv * b_ref[rows, :].astype(F32)
            dv_after = jnp.where(i < n - 1, _rows_after(dy_ref, r0, t) * _rows_after(b_ref, r0, t), 0.0)
            dq = _taps_up(jnp.concatenate([dv, dv_after], axis=0), w, 3)
            dp_ref[rows, CONV_CB:2 * CONV_CB] = (dq * xav).astype(BF)
            dp_ref[rows, 2 * CONV_CB:3 * CONV_CB] = (dq * cv).astype(BF)
            return tuple(a + _fold8(dv * s) for a, s in zip(acc, shifted))

        zero = jnp.zeros((8, CONV_CB), F32)
        acc = lax.fori_loop(0, n, step, (zero, zero, zero))
        for j in range(3):
            dw_ref[j:j + 1, :] = jnp.sum(acc[j], axis=0, keepdims=True)

    col = pl.BlockSpec((t, CONV_CB), lambda j: (0, j))
    wsp = pl.BlockSpec((3, CONV_CB), lambda j: (0, j))
    return pl.pallas_call(
        body, name="mix_a_bwd", grid=(D_MODEL // CONV_CB,),
        in_specs=[_pspec(t, OFF_B), _pspec(t, OFF_C), _pspec(t, OFF_XA), wsp, col, pl.BlockSpec(memory_space=pl.ANY)],
        out_specs=[pl.BlockSpec((t, 3 * CONV_CB), lambda j: (0, j)), wsp],
        out_shape=[jax.ShapeDtypeStruct(dp.shape, dp.dtype), jax.ShapeDtypeStruct((3, D_MODEL), F32)],
        input_output_aliases={5: 0},
        compiler_params=_params("parallel"))(p, p, p, conv_w, dya, dp)


def _ssm_conv_fwd(p, conv_w, conv_b, comm=None):
    t = p.shape[0]

    def body(x_ref, w_ref, b_ref, o_ref):
        w = w_ref[...]
        bias = b_ref[...]

        def step(i, carry):
            r0 = pl.multiple_of(i * CONV_ROWS, CONV_ROWS)
            rows = pl.ds(r0, CONV_ROWS)
            ext = jnp.concatenate([_rows_before(x_ref, i, r0), x_ref[rows, :].astype(F32)], axis=0)
            pre = _taps_down(ext, w, 4)[0] + bias
            o_ref[rows, :] = pre * _sigmoid(pre)
            return carry

        lax.fori_loop(0, t // CONV_ROWS, step, 0)

    out, couts = _pcall(
        "ssm_conv_fwd", body, (D_XBC // CONV_CB,),
        [_pspec(t, OFF_XBC), pl.BlockSpec((4, CONV_CB), lambda j: (0, j)), pl.BlockSpec((1, CONV_CB), lambda j: (0, j))],
        pl.BlockSpec((t, CONV_CB), lambda j: (0, j)), jax.ShapeDtypeStruct((t, D_XBC), F32),
        (p, conv_w, conv_b), (), ("parallel",), comm)
    return out if comm is None else (out, couts)


def _ssm_conv_bwd(p, conv_w, conv_b, dxc, dp, comm=None):
    t = p.shape[0]

    def body(x_ref, w_ref, b_ref, d_ref, dp_in, dx_ref, dw_ref, db_ref):
        del dp_in
        w = w_ref[...]
        bias = b_ref[...]
        n = t // CONV_ROWS

        def step(i, acc):
            r0 = pl.multiple_of(i * CONV_ROWS, CONV_ROWS)
            rows = pl.ds(r0, CONV_ROWS)
            x_cur = x_ref[rows, :].astype(F32)
            pre, shifted = _taps_down(jnp.concatenate([_rows_before(x_ref, i, r0), x_cur], axis=0), w, 4)
            pre = pre + bias
            dpre = d_ref[rows, :] * _silu_grad(pre)
            ext_after = jnp.concatenate([x_cur[CONV_ROWS - CONV_HALO:], _rows_after(x_ref, r0, t)], axis=0)
            pre_after = _taps_down(ext_after, w, 4)[0] + bias
            dpre_after = jnp.where(i < n - 1, _rows_after(d_ref, r0, t) * _silu_grad(pre_after), 0.0)
            dx_ref[rows, :] = _taps_up(jnp.concatenate([dpre, dpre_after], axis=0), w, 4).astype(BF)
            new = tuple(a + _fold8(dpre * s) for a, s in zip(acc[:4], shifted))
            return new + (acc[4] + _fold8(dpre),)

        zero = jnp.zeros((8, CONV_CB), F32)
        acc = lax.fori_loop(0, n, step, (zero,) * 5)
        for j in range(4):
            dw_ref[j:j + 1, :] = jnp.sum(acc[j], axis=0, keepdims=True)
        db_ref[...] = jnp.sum(acc[4], axis=0, keepdims=True)

    col = pl.BlockSpec((t, CONV_CB), lambda j: (0, j))
    wsp = pl.BlockSpec((4, CONV_CB), lambda j: (0, j))
    bsp = pl.BlockSpec((1, CONV_CB), lambda j: (0, j))
    outs, couts = _pcall(
        "ssm_conv_bwd", body, (D_XBC // CONV_CB,),
        [_pspec(t, OFF_XBC), wsp, bsp, col, pl.BlockSpec(memory_space=pl.ANY)], [_pspec(t, OFF_XBC), wsp, bsp],
        [jax.ShapeDtypeStruct(dp.shape, dp.dtype), jax.ShapeDtypeStruct((4, D_XBC), F32),
         jax.ShapeDtypeStruct((1, D_XBC), F32)],
        (p, conv_w, conv_b, dxc, dp), (), ("parallel",), comm, aliases={4: 0})
    return outs if comm is None else (outs, couts)


DT_ROWS = 512


def _tri(lower):
    r = lax.broadcasted_iota(jnp.int32, (CHUNK, CHUNK), 0)
    c = lax.broadcasted_iota(jnp.int32, (CHUNK, CHUNK), 1)
    return jnp.where((r >= c) if lower else (r <= c), 1.0, 0.0).astype(F32)


def _dot_exact(a, b):
    return lax.dot_general(a, b, _DIMS["nn"], preferred_element_type=F32, precision=lax.Precision.HIGHEST)


def _dt_fwd(p, bias_pad, alog_pad):
    t = p.shape[0]

    def body(raw_ref, b_ref, al_ref, dt_ref, acs_ref):
        z = raw_ref[...] + b_ref[...]
        dt = jnp.maximum(z, 0.0) + jnp.log(1.0 + jnp.exp(-jnp.abs(z)))
        dt_ref[...] = dt
        a = dt * (-jnp.exp(al_ref[...]))
        tri = _tri(True)
        for k in range(DT_ROWS // CHUNK):
            acs_ref[k * CHUNK:(k + 1) * CHUNK, :] = _dot_exact(tri, a[k * CHUNK:(k + 1) * CHUNK, :])

    blk = pl.BlockSpec((DT_ROWS, DT_W), lambda i: (i, 0))
    vec = pl.BlockSpec((1, DT_W), lambda i: (0, 0))
    return pl.pallas_call(
        body, name="dt_fwd", grid=(t // DT_ROWS,),
        in_specs=[pl.BlockSpec((DT_ROWS, DT_W), lambda i: (i, OFF_DT // DT_W)), vec, vec],
        out_specs=[blk, blk], out_shape=[jax.ShapeDtypeStruct((t, DT_W), F32)] * 2,
        compiler_params=_params("parallel"))(p, bias_pad, alog_pad)


def _dt_bwd(p, bias_pad, alog_pad, dt, ddt, dacs, dp_gd):
    t = p.shape[0]

    def body(raw_ref, b_ref, al_ref, dt_ref, ddt_ref, dacs_ref, dp_in, draw_ref, db_ref, dal_ref):
        del dp_in
        i = pl.program_id(0)
        acoef = -jnp.exp(al_ref[...])
        triu = _tri(False)
        das = []
        for k in range(DT_ROWS // CHUNK):
            das.append(_dot_exact(triu, dacs_ref[k * CHUNK:(k + 1) * CHUNK, :]))
        da = jnp.concatenate(das, axis=0)
        dtv = dt_ref[...]
        ddt_tot = ddt_ref[...] + da * acoef
        lane = lax.broadcasted_iota(jnp.int32, (DT_ROWS, DT_W), 1)
        draw = jnp.where(lane < N_HEADS, ddt_tot * _sigmoid(raw_ref[...] + b_ref[...]), 0.0)
        draw_ref[...] = draw.astype(BF)
        pb = jnp.sum(draw, axis=0, keepdims=True)
        pa = jnp.sum(da * dtv * acoef, axis=0, keepdims=True)

        @pl.when(i == 0)
        def _():
            db_ref[...] = pb
            dal_ref[...] = pa

        @pl.when(i > 0)
        def _():
            db_ref[...] += pb
            dal_ref[...] += pa

    blk = pl.BlockSpec((DT_ROWS, DT_W), lambda i: (i, 0))
    vec = pl.BlockSpec((1, DT_W), lambda i: (0, 0))
    return pl.pallas_call(
        body, name="dt_bwd", grid=(t // DT_ROWS,),
        in_specs=[pl.BlockSpec((DT_ROWS, DT_W), lambda i: (i, OFF_DT // DT_W)), vec, vec, blk, blk, blk,
                  pl.BlockSpec(memory_space=pl.ANY)],
        out_specs=[pl.BlockSpec((DT_ROWS, DT_W), lambda i: (i, OFF_DT // DT_W)), vec, vec],
        out_shape=[jax.ShapeDtypeStruct(dp_gd.shape, dp_gd.dtype), jax.ShapeDtypeStruct((1, DT_W), F32),
                   jax.ShapeDtypeStruct((1, DT_W), F32)],
        input_output_aliases={6: 0},
        compiler_params=_params("arbitrary"))(p, bias_pad, alog_pad, dt, ddt, dacs, dp_gd)


def _split_dot(z, onehot, terms):
    out = None
    rest = z
    for _ in range(terms):
        piece = rest.astype(BF)
        part = _dot(piece, onehot)
        out = part if out is None else out + part
        rest = rest - piece.astype(F32)
    return out


def _spread_mat():
    row = lax.broadcasted_iota(jnp.int32, (DT_W, D_INNER), 0)
    lane = lax.broadcasted_iota(jnp.int32, (DT_W, D_INNER), 1)
    return jnp.where(row == lane // HEAD_DIM, 1.0, 0.0).astype(BF)


def _gather_mat():
    row = lax.broadcasted_iota(jnp.int32, (D_INNER, DT_W), 0)
    lane = lax.broadcasted_iota(jnp.int32, (D_INNER, DT_W), 1)
    return jnp.where(lane == row // HEAD_DIM, 1.0, 0.0).astype(BF)


def _ssd_masks():
    row = lax.broadcasted_iota(jnp.int32, (CHUNK, GROUP_W), 0)
    col = lax.broadcasted_iota(jnp.int32, (CHUNK, GROUP_W), 1) % HEAD_DIM
    brow = lax.broadcasted_iota(jnp.int32, (GROUP_W, GROUP_W), 0) // HEAD_DIM
    bcol = lax.broadcasted_iota(jnp.int32, (GROUP_W, GROUP_W), 1) // HEAD_DIM
    return row >= col, row == col, brow == bcol


def _stack4(v):
    return jnp.concatenate([v, v, v, v], axis=0)


def _fold4(v):
    return v[0:CHUNK] + v[CHUNK:2 * CHUNK] + v[2 * CHUNK:3 * CHUNK] + v[3 * CHUNK:4 * CHUNK]


def _ssd_group(xc_ref, wide_ref, g, tri, eye, blockdiag):
    gs = slice(GROUP_W * g, GROUP_W * (g + 1))
    xs_g = xc_ref[:, gs]
    b_g = xc_ref[:, D_INNER + D_STATE * g:D_INNER + D_STATE * (g + 1)].astype(BF)
    c_g = xc_ref[:, D_INNER + 1024 + D_STATE * g:D_INNER + 1024 + D_STATE * (g + 1)].astype(BF)
    acs_e, dt_e = wide_ref[0:CHUNK, gs], wide_ref[CHUNK:2 * CHUNK, gs]
    atot_e = acs_e[CHUNK - 1:CHUNK, :]
    acs_j = jnp.sum(jnp.where(eye, acs_e, 0.0), axis=0, keepdims=True)
    lmat = jnp.where(tri, jnp.exp(jnp.minimum(acs_e - acs_j, 0.0)), 0.0)
    b_t = _stack4(b_g)
    m = _dot(c_g, b_t, "nt") * lmat
    x_g = xs_g * dt_e
    xbd = jnp.where(blockdiag, _stack4(x_g), 0.0).astype(BF)
    return dict(gs=gs, xs=xs_g, b=b_g, c=c_g, b_t=b_t, dt=dt_e, e=jnp.exp(acs_e), dec=jnp.exp(atot_e - acs_e),
                eat=jnp.exp(atot_e), lmat=lmat, m=m, x=x_g, xbd=xbd)


def _ssd_fwd(xconv, dt, acs, d_exp, comm=None):
    t = xconv.shape[0]
    nc = t // CHUNK

    def body(xc_ref, dt_ref, acs_ref, d_ref, y_ref, hs_ref, state, wide):
        c = pl.program_id(0)

        @pl.when(c == 0)
        def _():
            state[...] = jnp.zeros_like(state)

        hs_ref[...] = state[...]
        tri, eye, blockdiag = _ssd_masks()
        wide[...] = _split_dot(jnp.concatenate([acs_ref[...], dt_ref[...]], axis=0), _spread_mat(), 3)
        for g in range(N_GROUPS):
            q = _ssd_group(xc_ref, wide, g, tri, eye, blockdiag)
            gs = q["gs"]
            h_t = state[:, gs]
            ydiag = _dot(q["m"].astype(BF), q["xbd"])
            yoff = _dot(q["c"], h_t.astype(BF)) * q["e"]
            y_ref[:, gs] = ydiag + yoff + d_ref[:, gs] * q["xs"]
            s_t = _dot(q["b"], (q["x"] * q["dec"]).astype(BF), "tn")
            state[:, gs] = q["eat"] * h_t + s_t

    blk = lambda w: pl.BlockSpec((CHUNK, w), lambda c: (c, 0))
    outs, couts = _pcall(
        "ssd_fwd", body, (nc,),
        [blk(D_XBC), blk(DT_W), blk(DT_W), pl.BlockSpec((1, D_INNER), lambda c: (0, 0))],
        [blk(D_INNER), pl.BlockSpec((None, D_STATE, D_INNER), lambda c: (c, 0, 0))],
        [jax.ShapeDtypeStruct((t, D_INNER), F32), jax.ShapeDtypeStruct((nc, D_STATE, D_INNER), F32)],
        (xconv, dt, acs, d_exp), [pltpu.VMEM((D_STATE, D_INNER), F32), pltpu.VMEM((2 * CHUNK, D_INNER), F32)],
        ("arbitrary",), comm)
    return outs if comm is None else (outs, couts)


def _ssd_bwd(xconv, dt, acs, d_exp, hsave, dy, comm=None):
    t = xconv.shape[0]
    nc = t // CHUNK

    def body(xc_ref, dt_ref, acs_ref, d_ref, hs_ref, dy_ref, dxc_ref, ddt_ref, dacs_ref, dd_ref, dstate, wide, per_head):
        c = pl.program_id(0)

        @pl.when(c == 0)
        def _():
            dstate[...] = jnp.zeros_like(dstate)
            dd_ref[...] = jnp.zeros_like(dd_ref)

        tri, eye, blockdiag = _ssd_masks()
        acsv = acs_ref[...]
        wide[...] = _split_dot(jnp.concatenate([acsv, dt_ref[...]], axis=0), _spread_mat(), 3)
        eat_heads = jnp.exp(acsv[CHUNK - 1:CHUNK, :])

        for g in range(N_GROUPS):
            q = _ssd_group(xc_ref, wide, g, tri, eye, blockdiag)
            gs, xs_g, b_g, c_g, m = q["gs"], q["xs"], q["b"], q["c"], q["m"]
            bs = slice(D_INNER + D_STATE * g, D_INNER + D_STATE * (g + 1))
            cs = slice(D_INNER + 1024 + D_STATE * g, D_INNER + 1024 + D_STATE * (g + 1))
            h_t = hs_ref[:, gs]
            h_b = h_t.astype(BF)
            dy_g = dy_ref[:, gs]
            dy_b = dy_g.astype(BF)
            ds_t = dstate[:, gs]
            ds_b = ds_t.astype(BF)

            yoff = _dot(c_g, h_b) * q["e"]
            edy = (q["e"] * dy_g).astype(BF)
            d_c = _dot(edy, h_b, "nt")
            d_ht = _dot(c_g, edy, "tn")
            bds = _dot(b_g, ds_b)
            xd = q["x"] * q["dec"]
            d_b = _dot(xd.astype(BF), ds_b, "nt")
            dm = _dot(dy_b, q["xbd"], "nt")
            cross = _dot(m.astype(BF), dy_b, "tn")
            dx_full = q["dec"] * bds + _fold4(jnp.where(blockdiag, cross, 0.0))
            dml = (dm * q["lmat"]).astype(BF)
            d_c = d_c + _dot(dml, q["b_t"])
            d_b = d_b + _fold4(_dot(dml, c_g, "tn"))
            w = dm * m
            q_dec = xd * bds
            z = w - jnp.where(eye, jnp.sum(w, axis=0, keepdims=True), 0.0) + dy_g * yoff - q_dec
            rows = jnp.concatenate(
                [jnp.sum(q_dec, axis=0, keepdims=True), jnp.sum(ds_t * h_t, axis=0, keepdims=True),
                 jnp.zeros((6, GROUP_W), F32)], axis=0)
            per_head[:, gs] = jnp.concatenate([z, dx_full * xs_g, rows], axis=0)
            dxc_ref[:, cs] = d_c
            dxc_ref[:, bs] = d_b
            dxc_ref[:, gs] = dx_full * q["dt"] + d_ref[:, gs] * dy_g
            dd_ref[:, gs] += jnp.sum(dy_g * xs_g, axis=0, keepdims=True)
            dstate[:, gs] = q["eat"] * ds_t + d_ht

        seg = _split_dot(per_head[...], _gather_mat(), 2)
        datot = seg[2 * CHUNK:2 * CHUNK + 1] + eat_heads * seg[2 * CHUNK + 1:2 * CHUNK + 2]
        rowi = lax.broadcasted_iota(jnp.int32, (CHUNK, DT_W), 0)
        ddt_ref[...] = seg[CHUNK:2 * CHUNK]
        dacs_ref[...] = seg[0:CHUNK] + jnp.where(rowi == CHUNK - 1, datot, 0.0)

    rev = lambda w: pl.BlockSpec((CHUNK, w), lambda c: (nc - 1 - c, 0))
    vec = pl.BlockSpec((1, D_INNER), lambda c: (0, 0))
    outs, couts = _pcall(
        "ssd_bwd", body, (nc,),
        [rev(D_XBC), rev(DT_W), rev(DT_W), vec,
         pl.BlockSpec((None, D_STATE, D_INNER), lambda c: (nc - 1 - c, 0, 0)), rev(D_INNER)],
        [rev(D_XBC), rev(DT_W), rev(DT_W), vec],
        [jax.ShapeDtypeStruct((t, D_XBC), F32), jax.ShapeDtypeStruct((t, DT_W), F32),
         jax.ShapeDtypeStruct((t, DT_W), F32), jax.ShapeDtypeStruct((1, D_INNER), F32)],
        (xconv, dt, acs, d_exp, hsave, dy),
        [pltpu.VMEM((D_STATE, D_INNER), F32), pltpu.VMEM((2 * CHUNK, D_INNER), F32),
         pltpu.VMEM((2 * CHUNK + 8, D_INNER), F32)], ("arbitrary",), comm)
    return outs if comm is None else (outs, couts)


GN_CB = 1024
GN_GROUPS = GN_CB // GROUP_W


def _gnorm_fwd(y, p, w, comm=None):
    t = y.shape[0]
    zoff = OFF_Z // GN_CB

    def body(y_ref, z_ref, w_ref, o_ref):
        for g in range(GN_GROUPS):
            gs = slice(GROUP_W * g, GROUP_W * (g + 1))
            z = z_ref[:, gs].astype(F32)
            yf = y_ref[:, gs] * (z * _sigmoid(z))
            rstd = lax.rsqrt(jnp.mean(yf * yf, axis=-1, keepdims=True) + NORM_EPS)
            o_ref[:, gs] = (yf * rstd * w_ref[:, gs]).astype(BF)

    blk = pl.BlockSpec((TE, GN_CB), lambda i, j: (i, j))
    out, couts = _pcall(
        "gnorm_fwd", body, (t // TE, D_INNER // GN_CB),
        [blk, pl.BlockSpec((TE, GN_CB), lambda i, j: (i, zoff + j)), pl.BlockSpec((1, GN_CB), lambda i, j: (0, j))],
        blk, jax.ShapeDtypeStruct((t, D_INNER), BF), (y, p, w), (), ("parallel", "parallel"), comm)
    return out if comm is None else (out, couts)


def _gnorm_bwd(y, p, w, dyn, comm=None):
    t = y.shape[0]
    zoff = OFF_Z // GN_CB

    def body(y_ref, z_ref, w_ref, dn_ref, dy_ref, dz_ref, dw_ref):
        i = pl.program_id(1)
        for g in range(GN_GROUPS):
            gs = slice(GROUP_W * g, GROUP_W * (g + 1))
            z = z_ref[:, gs].astype(F32)
            yv = y_ref[:, gs]
            s = _sigmoid(z)
            sil = z * s
            yf = yv * sil
            rstd = lax.rsqrt(jnp.mean(yf * yf, axis=-1, keepdims=True) + NORM_EPS)
            xhat = yf * rstd
            dn = dn_ref[:, gs]
            wd = dn * w_ref[:, gs]
            proj = jnp.mean(wd * xhat, axis=-1, keepdims=True)
            dyf = rstd * (wd - xhat * proj)
            dy_ref[:, gs] = dyf * sil
            dz_ref[:, gs] = (dyf * yv * (s * (1.0 + z * (1.0 - s)))).astype(BF)
            part = jnp.sum(dn * xhat, axis=0, keepdims=True)

            @pl.when(i == 0)
            def _():
                dw_ref[:, gs] = part

            @pl.when(i > 0)
            def _():
                dw_ref[:, gs] += part

    blk = pl.BlockSpec((TE, GN_CB), lambda j, i: (i, j))
    vec = pl.BlockSpec((1, GN_CB), lambda j, i: (0, j))
    outs, couts = _pcall(
        "gnorm_bwd", body, (D_INNER // GN_CB, t // TE),
        [blk, pl.BlockSpec((TE, GN_CB), lambda j, i: (i, zoff + j)), vec, blk],
        [blk, pl.BlockSpec((TE, GN_CB), lambda j, i: (i, zoff + j)), vec],
        [jax.ShapeDtypeStruct((t, D_INNER), F32), jax.ShapeDtypeStruct((t, N_MAIN), BF),
         jax.ShapeDtypeStruct((1, D_INNER), F32)],
        (y, p, w, dyn), (), ("parallel", "arbitrary"), comm)
    return outs if comm is None else (outs, couts)


MERGE_CB = 512


def _merge_fwd(p, ya, yb):
    t = ya.shape[0]

    def body(ga_ref, gb_ref, ya_ref, yb_ref, o_ref):
        o_ref[...] = (_sigmoid(ga_ref[...]) * ya_ref[...] + _sigmoid(gb_ref[...]) * yb_ref[...]).astype(BF)

    blk = pl.BlockSpec((TE, MERGE_CB), lambda i, j: (i, j))
    return pl.pallas_call(
        body, name="merge_fwd", grid=(t // TE, D_MODEL // MERGE_CB),
        in_specs=[pl.BlockSpec((TE, MERGE_CB), lambda i, j: (i, 2 * j)),
                  pl.BlockSpec((TE, MERGE_CB), lambda i, j: (i, 2 * j + 1)), blk, blk],
        out_specs=blk, out_shape=jax.ShapeDtypeStruct((t, D_MODEL), BF),
        compiler_params=_params("parallel", "parallel"))(p, p, ya, yb)


def _merge_bwd(p, ya, yb, dm):
    t = ya.shape[0]

    def body(ga_ref, gb_ref, ya_ref, yb_ref, dm_ref, dg_ref, dya_ref, dyb_ref):
        d = dm_ref[...]
        sa = _sigmoid(ga_ref[...])
        sb = _sigmoid(gb_ref[...])
        dg_ref[:, 0:MERGE_CB] = (d * ya_ref[...] * sa * (1.0 - sa)).astype(BF)
        dg_ref[:, MERGE_CB:2 * MERGE_CB] = (d * yb_ref[...] * sb * (1.0 - sb)).astype(BF)
        dya_ref[...] = (d * sa).astype(BF)
        dyb_ref[...] = (d * sb).astype(BF)

    blk = pl.BlockSpec((TE, MERGE_CB), lambda i, j: (i, j))
    return pl.pallas_call(
        body, name="merge_bwd", grid=(t // TE, D_MODEL // MERGE_CB),
        in_specs=[pl.BlockSpec((TE, MERGE_CB), lambda i, j: (i, 2 * j)),
                  pl.BlockSpec((TE, MERGE_CB), lambda i, j: (i, 2 * j + 1)), blk, blk, blk],
        out_specs=[pl.BlockSpec((TE, 2 * MERGE_CB), lambda i, j: (i, j)), blk, blk],
        out_shape=[jax.ShapeDtypeStruct((t, N_GD), BF)] + [jax.ShapeDtypeStruct((t, D_MODEL), BF)] * 2,
        compiler_params=_params("parallel", "parallel"))(p, p, ya, yb, dm)


def _adamw(name, parts, w, m, v, comm=None):
    r, c = w.shape
    tr = _row_tile(r)
    tc = ADAM_COL_TILE if (tr == r and r > 512 and c % ADAM_COL_TILE == 0) else c
    n_parts = parts.shape[0]
    bc1 = 1.0 - ADAM_B1 ** ADAM_STEP
    bc2 = 1.0 - ADAM_B2 ** ADAM_STEP

    def body(p_ref, w_ref, m_ref, v_ref, g_ref, d_ref, nm_ref, nv_ref):
        g = p_ref[0].astype(F32)
        for k in range(1, n_parts):
            g = g + p_ref[k].astype(F32)
        nm = ADAM_B1 * m_ref[...] + (1.0 - ADAM_B1) * g
        nv = ADAM_B2 * v_ref[...] + (1.0 - ADAM_B2) * (g * g)
        g_ref[...] = g
        nm_ref[...] = nm
        nv_ref[...] = nv
        d_ref[...] = -ADAM_LR * ((nm / bc1) / (jnp.sqrt(nv / bc2) + ADAM_EPS) + ADAM_WD * w_ref[...])

    blk = pl.BlockSpec((tr, tc), lambda i, j: (i, j))
    outs, couts = _pcall(
        name, body, (r // tr, c // tc),
        [pl.BlockSpec((n_parts, tr, tc), lambda i, j: (0, i, j)), blk, blk, blk], [blk] * 4,
        [jax.ShapeDtypeStruct((r, c), F32)] * 4, (parts, w, m, v), (), ("parallel", "parallel"), comm)
    return outs if comm is None else (outs, couts)


def _pad_lanes(v, width):
    return jnp.pad(v, ((0, 0), (0, width - v.shape[1])))


def _reduce_start(slots, host):
    outs, sib = host(_pair_comm([a for _, a in slots]))
    sums = [(n, _add_pairs("pairsum_" + n, a, b)) for (n, a), b in zip(slots, sib)]
    return outs, sums


def _train_step(x, target, shard, rep):
    gdt = BF
    recv = {}
    (got,) = _comm_call("gather_ffn1_in", _gather_comm([shard["ffn1_w_in"]], [True]))
    w1_in = got.reshape(2 * D_FF, D_MODEL)
    h1 = _rms_fwd("rms1_fwd", x, rep["ffn1_norm"])
    gu1, got = _mm_nt("ffn1_in", h1, w1_in, tn=FF_HALF, out_dtype=BF, comm=_gather_comm(
        [shard["ffn1_w_out"], shard["w_in"], shard["short_conv_w"], shard["ssm_conv_w"]]))
    w1_out = got[0].reshape(D_FF, D_MODEL)
    w_in_t = got[1].reshape(N_IN, D_MODEL)
    short_conv_w = got[2].transpose(1, 0, 2).reshape(3, D_MODEL)
    ssm_conv_w = got[3].transpose(1, 0, 2).reshape(4, D_XBC)
    act1 = _swiglu_fwd("swiglu1_fwd", gu1)
    x1 = _mm_nn("ffn1_out", act1, w1_out, res=x, alpha=0.5)
    ga0 = N_MAIN + N_HEADS
    gb0 = ga0 + D_MODEL
    half = D_MODEL // 2
    w_gd = jnp.concatenate(
        [w_in_t[ga0:ga0 + half], w_in_t[gb0:gb0 + half], w_in_t[ga0 + half:gb0], w_in_t[gb0 + half:],
         w_in_t[N_MAIN:N_MAIN + N_HEADS], jnp.zeros((DT_W - N_HEADS, D_MODEL), BF)], axis=0)
    w_mix_perm = w_in_t[0:3 * D_MODEL].reshape(3, 4, CONV_CB, D_MODEL).transpose(1, 0, 2, 3).reshape(3 * D_MODEL, D_MODEL)

    h2 = _rms_fwd("rms2_fwd", x1, rep["mix_norm"])
    p, got = _mm_nt("proj_main", h2, w_in_t, n=N_MAIN, tn=1024, out_dtype=BF, comm=_gather_comm(
        [shard["short_w_out"], shard["ssm_w_out"], shard["w_out"]]))
    p_gd = _mm_nt("proj_gd", h2, w_gd)
    short_w_out = got[0].reshape(D_MODEL, D_MODEL)
    ssm_w_out = got[1].reshape(D_INNER, D_MODEL)
    w_out = got[2].reshape(D_MODEL, D_MODEL)
    ya_in = _mix_a_fwd(p, short_conv_w)
    y_a = _mm_nn("short_out", ya_in, short_w_out)
    xconv, (got,) = _ssm_conv_fwd(p, ssm_conv_w, rep["ssm_conv_b"], comm=_gather_comm([shard["ffn2_w_out"]]))
    w2_out = got.reshape(D_FF, D_MODEL)
    dt, acs = _dt_fwd(p_gd, rep["dt_bias_pad"], rep["a_log_pad"])
    (y_ssm, hsave), (got,) = _ssd_fwd(xconv, dt, acs, rep["d_exp"], comm=_gather_comm([shard["ffn2_w_in"]], [True]))
    w2_in = got.reshape(2 * D_FF, D_MODEL)
    yn = _gnorm_fwd(y_ssm, p, rep["ssm_norm"])
    y_b = _mm_nn("ssm_out", yn, ssm_w_out, tk=1024)
    merged = _merge_fwd(p_gd, y_a, y_b)
    x2 = _mm_nn("mix_out", merged, w_out, res=x1)

    h3 = _rms_fwd("rms3_fwd", x2, rep["ffn2_norm"])
    gu2 = _mm_nt("ffn2_in", h3, w2_in, tn=FF_HALF, out_dtype=BF)
    act2 = _swiglu_fwd("swiglu2_fwd", gu2)
    x3 = _mm_nn("ffn2_out", act2, w2_out, res=x2, alpha=0.5)

    loss, dx3, dx3h, g_final = _final_loss(x3, rep["final_norm"], target)

    small = {"final_norm": g_final}
    dact2 = _mm_nt("ffn2_out_bwd_act", dx3h, w2_out, out_dtype=BF)
    g_w2_out = _mm_tn("ffn2_out_bwd_w", act2, dx3h, gdt, tm=FF_HALF)
    dgu2 = _swiglu_bwd("swiglu2_bwd", gu2, dact2)
    g_w2_in = _mm_tn("ffn2_in_bwd_w", dgu2, h3, gdt, tm=FF_HALF)
    dh3 = _mm_nn("ffn2_in_bwd_h", dgu2, w2_in, tk=FF_HALF)
    dx2, dx2b, small["ffn2_norm"] = _rms_bwd("rms3_bwd", x2, rep["ffn2_norm"], dh3, dx3, 1.0)

    dmerged = _mm_nt("mix_out_bwd_x", dx2b, w_out)
    g_w_out = _mm_tn("mix_out_bwd_w", merged, dx2b, gdt)
    dp_gd, dya, dyb = _merge_bwd(p_gd, y_a, y_b, dmerged)

    dya_in = _mm_nt("short_out_bwd_x", dya, short_w_out)
    g_short_w_out = _mm_tn("short_out_bwd_w", ya_in, dya, gdt)

    dyn = _mm_nt("ssm_out_bwd_x", dyb, ssm_w_out)
    g_ssm_w_out = _mm_tn("ssm_out_bwd_w", yn, dyb, gdt)
    late = [("ffn2_w_out", g_w2_out.reshape(N_DEV, FF_SHARD // 2, D_MODEL)),
            ("ffn2_w_in", g_w2_in.reshape(N_DEV, FF_SHARD, D_MODEL)),
            ("w_out", g_w_out.reshape(N_DEV, -1, D_MODEL)), ("short_w_out", g_short_w_out.reshape(N_DEV, -1, D_MODEL)),
            ("ssm_w_out", g_ssm_w_out.reshape(N_DEV, -1, D_MODEL))]
    (dy_ssm, dp, small["ssm_norm"]), sums = _reduce_start(
        late, lambda comm: _gnorm_bwd(y_ssm, p, rep["ssm_norm"], dyn, comm=comm))
    dp, g_short_conv = _mix_a_bwd(p, short_conv_w, dya_in, dp)
    first = [(n, a) for n, a in sums if n.startswith("ffn2")]
    second = [(n, a) for n, a in sums if not n.startswith("ffn2")]
    (dxconv, ddt, dacs, dd_lane), got = _ssd_bwd(
        xconv, dt, acs, rep["d_exp"], hsave, dy_ssm,
        comm=_chip_comm([a for _, a in first], [n == "ffn2_w_in" for n, _ in first]))
    recv.update({n: a for (n, _), a in zip(first, got)})
    small["ssm_D"] = dd_lane.reshape(N_HEADS, HEAD_DIM).sum(axis=1)[None, :]
    (dp, g_ssm_conv, small["ssm_conv_b"]), got = _ssm_conv_bwd(
        p, ssm_conv_w, rep["ssm_conv_b"], dxconv, dp, comm=_chip_comm([a for _, a in second]))
    recv.update({n: a for (n, _), a in zip(second, got)})
    dp_gd, dbias, dalog = _dt_bwd(p_gd, rep["dt_bias_pad"], rep["a_log_pad"], dt, ddt, dacs, dp_gd)
    small["ssm_dt_bias"] = dbias[:, :N_HEADS]
    small["ssm_A_log"] = dalog[:, :N_HEADS]

    g_main = _mm_tn("proj_main_bwd_w", dp, h2, gdt, tm=1024)
    g_gd = _mm_tn("proj_gd_bwd_w", dp_gd, h2, gdt)
    g_mix = g_main[0:3 * D_MODEL].reshape(4, 3, CONV_CB, D_MODEL).transpose(1, 0, 2, 3).reshape(3 * D_MODEL, D_MODEL)
    g_in_t = jnp.concatenate(
        [g_mix, g_main[3 * D_MODEL:], g_gd[2 * D_MODEL:2 * D_MODEL + N_HEADS],
         g_gd[0:half], g_gd[2 * half:3 * half], g_gd[half:2 * half], g_gd[3 * half:4 * half]], axis=0).reshape(
        N_DEV, IN_SHARD, D_MODEL)
    dh2, w_sums = _reduce_start(
        [("w_in", g_in_t)], lambda comm: _mm_nn("proj_mix_bwd_x", dp, w_mix_perm, tk=1024, kk=3 * D_MODEL, comm=comm))
    w_sum = w_sums[0][1]

    def w_piece(i):
        return _chip_comm([w_sum], rows=[W_GRAD_ROW_CUTS[i]])

    dh2, got0 = _mm_nn("proj_rest_bwd_x", dp, w_in_t, tk=1024, kk=N_MAIN - 3 * D_MODEL, a_off=3, b_off=3, res=dh2,
                       comm=w_piece(0))
    dh2, got1 = _mm_nn("proj_gd_bwd_x", dp_gd, w_gd, res=dh2, comm=w_piece(1))
    (dx1, dx1h, small["mix_norm"]), got2 = _rms_bwd("rms2_bwd", x1, rep["mix_norm"], dh2, dx2, 0.5, comm=w_piece(2))
    g_w1_out, got3 = _mm_tn("ffn1_out_bwd_w", act1, dx1h, gdt, tm=FF_HALF, comm=w_piece(3))
    rest = [("ffn1_w_out", g_w1_out.reshape(N_DEV, FF_SHARD // 2, D_MODEL)),
            ("short_conv_w", g_short_conv.reshape(3, N_DEV, -1).transpose(1, 0, 2)),
            ("ssm_conv_w", g_ssm_conv.reshape(4, N_DEV, -1).transpose(1, 0, 2))]
    dact1, got = _mm_nt("ffn1_out_bwd_act", dx1h, w1_out, out_dtype=BF,
                        comm=_join_comm(w_piece(4), _pair_comm([a for _, a in rest])))
    got4, sib = got[0], got[1:]
    rest_sums = [(n, _add_pairs("pairsum_" + n, a, b)) for (n, a), b in zip(rest, sib)]
    recv["w_in"] = jnp.concatenate([got0[0], got1[0], got2[0], got3[0], got4], axis=1)
    dgu1, got = _swiglu_bwd("swiglu1_bwd", gu1, dact1, comm=_chip_comm([a for _, a in rest_sums]))
    recv.update({n: a for (n, _), a in zip(rest_sums, got)})

    def part(tag, width, off, comm=None):
        out = _mm_tn("ffn1_in_bwd_w_" + tag, dgu1, h1, gdt, tm=FF_HALF, n=width, col_off=off, comm=comm)
        g, couts = (out, None) if comm is None else out
        return g.reshape(N_DEV, FF_SHARD, width), couts

    g_a, _ = part("a", 384, 0)
    g_b, sib = part("b", 384, 1, _pair_comm([g_a]))
    sum_a = _add_pairs("pairsum_ffn1_w_in_a", g_a, sib[0])
    g_c, (recv_a, sib_b) = part("c", 256, 3, _join_comm(_chip_comm([sum_a], [True]), _pair_comm([g_b])))
    sum_b = _add_pairs("pairsum_ffn1_w_in_b", g_b, sib_b)
    dh1, (recv_b, sib_c) = _mm_nn("ffn1_in_bwd_h", dgu1, w1_in, tk=FF_HALF,
                                  comm=_join_comm(_chip_comm([sum_b], [True]), _pair_comm([g_c])))
    sum_c = _add_pairs("pairsum_ffn1_w_in_c", g_c, sib_c)
    (dx0, _, small["ffn1_norm"]), (recv_c,) = _rms_bwd("rms1_bwd", x, rep["ffn1_norm"], dh1, dx1, 1.0,
                                                        comm=_chip_comm([sum_c], [True]))
    recv["ffn1_w_in"] = jnp.concatenate([recv_a, recv_b, recv_c], axis=2)
    return dx0, recv, _pack_small(small, loss[:, 0:1])


_SMALL = [("ffn1_norm", 1024), ("mix_norm", 1024), ("ssm_conv_b", 4096), ("ssm_dt_bias", 32), ("ssm_A_log", 32),
          ("ssm_D", 32), ("ssm_norm", 2048), ("ffn2_norm", 1024), ("final_norm", 1024)]
SMALL_W = 10368


def _pack_small(d, loss=None):
    parts = [d[n].reshape(1, -1).astype(F32) for n, _ in _SMALL]
    used = sum(sz for _, sz in _SMALL)
    tail = jnp.zeros((1, SMALL_W - used), F32)
    if loss is not None:
        tail = tail.at[:, 0:1].set(loss)
    return jnp.concatenate(parts + [tail], axis=1)


def _adamw_small(parts, w, m, v):
    n_par = len(_SMALL)
    bc1 = 1.0 - ADAM_B1 ** ADAM_STEP
    bc2 = 1.0 - ADAM_B2 ** ADAM_STEP
    used = sum(sz for _, sz in _SMALL)

    def body(*refs):
        p_ref = refs[0]
        ins = refs[1:1 + 3 * n_par]
        outs = refs[1 + 3 * n_par:]
        g_all = p_ref[0]
        for k in range(1, N_DEV):
            g_all = g_all + p_ref[k]
        off = 0
        for i, (_, sz) in enumerate(_SMALL):
            g = g_all[:, off:off + sz]
            w_ref, m_ref, v_ref = ins[3 * i:3 * i + 3]
            nm = ADAM_B1 * m_ref[...] + (1.0 - ADAM_B1) * g
            nv = ADAM_B2 * v_ref[...] + (1.0 - ADAM_B2) * (g * g)
            outs[4 * i][...] = g
            outs[4 * i + 1][...] = -ADAM_LR * ((nm / bc1) / (jnp.sqrt(nv / bc2) + ADAM_EPS) + ADAM_WD * w_ref[...])
            outs[4 * i + 2][...] = nm
            outs[4 * i + 3][...] = nv
            off += sz
        outs[4 * n_par][...] = g_all[:, used:SMALL_W]

    args = [parts]
    out_shape = []
    for name, sz in _SMALL:
        args += [w[name], m[name], v[name]]
        out_shape += [jax.ShapeDtypeStruct((1, sz), F32)] * 4
    out_shape.append(jax.ShapeDtypeStruct((1, SMALL_W - used), F32))
    res = pl.pallas_call(body, name="adamw_small", out_shape=out_shape,
                         compiler_params=pltpu.CompilerParams(vmem_limit_bytes=VMEM_LIMIT_V7X))(*args)
    return {name: tuple(res[4 * i:4 * i + 4]) for i, (name, _) in enumerate(_SMALL)}, res[-1]


_SHARDED = ["ffn1_w_in", "ffn1_w_out", "w_in", "short_conv_w", "short_w_out", "ssm_conv_w", "ssm_w_out", "w_out",
            "ffn2_w_in", "ffn2_w_out"]
_TRANSPOSED = ("ffn1_w_in", "w_in", "ffn2_w_in")
_ORDER = ["ffn1_norm", "ffn1_w_in", "ffn1_w_out", "mix_norm", "w_in", "short_conv_w", "short_w_out", "ssm_conv_w",
          "ssm_conv_b", "ssm_dt_bias", "ssm_A_log", "ssm_D", "ssm_norm", "ssm_w_out", "w_out", "ffn2_norm",
          "ffn2_w_in", "ffn2_w_out", "final_norm"]


def kernel(x, ffn1_norm, ffn1_w_in, ffn1_w_out, mix_norm, w_in, short_conv_w, short_w_out, ssm_conv_w, ssm_conv_b, ssm_dt_bias, ssm_A_log, ssm_D, ssm_norm, ssm_w_out, w_out, ffn2_norm, ffn2_w_in, ffn2_w_out, final_norm, loss_target, m_ffn1_norm, m_ffn1_w_in, m_ffn1_w_out, m_mix_norm, m_w_in, m_short_conv_w, m_short_w_out, m_ssm_conv_w, m_ssm_conv_b, m_ssm_dt_bias, m_ssm_A_log, m_ssm_D, m_ssm_norm, m_ssm_w_out, m_w_out, m_ffn2_norm, m_ffn2_w_in, m_ffn2_w_out, m_final_norm, v_ffn1_norm, v_ffn1_w_in, v_ffn1_w_out, v_mix_norm, v_w_in, v_short_conv_w, v_short_w_out, v_ssm_conv_w, v_ssm_conv_b, v_ssm_dt_bias, v_ssm_A_log, v_ssm_D, v_ssm_norm, v_ssm_w_out, v_w_out, v_ffn2_norm, v_ffn2_w_in, v_ffn2_w_out, v_final_norm):
    w = dict(ffn1_norm=ffn1_norm, ffn1_w_in=ffn1_w_in, ffn1_w_out=ffn1_w_out, mix_norm=mix_norm, w_in=w_in,
             short_conv_w=short_conv_w, short_w_out=short_w_out, ssm_conv_w=ssm_conv_w, ssm_conv_b=ssm_conv_b,
             ssm_dt_bias=ssm_dt_bias, ssm_A_log=ssm_A_log, ssm_D=ssm_D, ssm_norm=ssm_norm, ssm_w_out=ssm_w_out,
             w_out=w_out, ffn2_norm=ffn2_norm, ffn2_w_in=ffn2_w_in, ffn2_w_out=ffn2_w_out, final_norm=final_norm)
    m = dict(ffn1_norm=m_ffn1_norm, ffn1_w_in=m_ffn1_w_in, ffn1_w_out=m_ffn1_w_out, mix_norm=m_mix_norm, w_in=m_w_in,
             short_conv_w=m_short_conv_w, short_w_out=m_short_w_out, ssm_conv_w=m_ssm_conv_w,
             ssm_conv_b=m_ssm_conv_b, ssm_dt_bias=m_ssm_dt_bias, ssm_A_log=m_ssm_A_log, ssm_D=m_ssm_D,
             ssm_norm=m_ssm_norm, ssm_w_out=m_ssm_w_out, w_out=m_w_out, ffn2_norm=m_ffn2_norm,
             ffn2_w_in=m_ffn2_w_in, ffn2_w_out=m_ffn2_w_out, final_norm=m_final_norm)
    v = dict(ffn1_norm=v_ffn1_norm, ffn1_w_in=v_ffn1_w_in, ffn1_w_out=v_ffn1_w_out, mix_norm=v_mix_norm, w_in=v_w_in,
             short_conv_w=v_short_conv_w, short_w_out=v_short_w_out, ssm_conv_w=v_ssm_conv_w,
             ssm_conv_b=v_ssm_conv_b, ssm_dt_bias=v_ssm_dt_bias, ssm_A_log=v_ssm_A_log, ssm_D=v_ssm_D,
             ssm_norm=v_ssm_norm, ssm_w_out=v_ssm_w_out, w_out=v_w_out, ffn2_norm=v_ffn2_norm,
             ffn2_w_in=v_ffn2_w_in, ffn2_w_out=v_ffn2_w_out, final_norm=v_final_norm)
    shapes = {n: w[n].shape for n in _ORDER}

    def local(d, n):
        return d[n][0].T if n in _TRANSPOSED else d[n][0]

    shard = {n: local(w, n) for n in _SHARDED}

    wire = {n: (shard[n] if n in ("short_conv_w", "ssm_conv_w") else shard[n].astype(BF)) for n in _SHARDED}
    rep = {
        "ffn1_norm": ffn1_norm, "mix_norm": mix_norm, "ffn2_norm": ffn2_norm, "ssm_norm": ssm_norm,
        "ssm_conv_b": ssm_conv_b, "final_norm": final_norm.reshape(1, D_MODEL),
        "dt_bias_pad": _pad_lanes(ssm_dt_bias, DT_W), "a_log_pad": _pad_lanes(ssm_A_log, DT_W),
        "d_exp": jnp.repeat(ssm_D, HEAD_DIM, axis=1),
    }
    grad_x, parts, packed = _train_step(x[0], loss_target[0], wire, rep)

    out_g, out_d, out_m, out_v = {}, {}, {}, {}
    for n in _SHARDED:
        if n == "ssm_w_out":
            res, (small_parts,) = _adamw("adamw_" + n, parts[n], shard[n], local(m, n), local(v, n),
                                         comm=_gather_comm([packed]))
        else:
            res = _adamw("adamw_" + n, parts[n], shard[n], local(m, n), local(v, n))
        out_g[n], out_d[n], out_m[n], out_v[n] = [(r.T if n in _TRANSPOSED else r).reshape(shapes[n]) for r in res]
    row = lambda d: {n: d[n].reshape(1, -1) for n, _ in _SMALL}
    sres, loss_row = _adamw_small(small_parts, row(w), row(m), row(v))
    for n, _ in _SMALL:
        out_g[n], out_d[n], out_m[n], out_v[n] = [r.reshape(shapes[n]) for r in sres[n]]
    loss = loss_row[0, 0]
    return (loss, grad_x[None], *[out_g[n] for n in _ORDER], *[out_d[n] for n in _ORDER],
            *[out_m[n] for n in _ORDER], *[out_v[n] for n in _ORDER])
```

```python
import functools

import jax
import jax.numpy as jnp
from jax import lax
from jax.experimental import pallas as pl
from jax.experimental.pallas import tpu as pltpu

F32 = jnp.float32
BF = jnp.bfloat16

N_DEV = 8
D_MODEL = 1024
D_FF = 2816
D_INNER = 2048
D_XBC = 4096
N_HEADS = 32
HEAD_DIM = 64
N_GROUPS = 8
D_STATE = 128
CHUNK = 64
GROUP_W = D_INNER // N_GROUPS
NORM_EPS = 1e-5
N_IN = 11296
FF_SHARD = 2 * D_FF // N_DEV
FF_HALF = D_FF // 2
IN_SHARD = N_IN // N_DEV

OFF_B, OFF_C, OFF_XA, OFF_Z, OFF_XBC = 0, 1024, 2048, 3072, 5120
N_MAIN = 9216
OFF_DT = 2048
DT_W = 128
N_GD = 2048 + DT_W
W_GRAD_ROW_CUTS = [(0, 512), (512, 720), (720, 896), (896, 1152), (1152, 1412)]

ADAM_LR, ADAM_B1, ADAM_B2, ADAM_EPS, ADAM_WD, ADAM_STEP = 0.001, 0.9, 0.999, 1e-08, 0.01, 10

VMEM_LIMIT_V7X = 56 * 1024 * 1024
TM = 1024
TN_MAX_TOKENS = 2048
TE = 512
ADAM_COL_TILE = 256
GATHER_PIECES = 4
GATHER_PIECE_MIN_ROWS = 512


def _params(*sem):
    return pltpu.CompilerParams(dimension_semantics=sem, vmem_limit_bytes=VMEM_LIMIT_V7X)


_DIMS = {
    "nn": (((1,), (0,)), ((), ())),
    "nt": (((1,), (1,)), ((), ())),
    "tn": (((0,), (0,)), ((), ())),
}


def _dot(a, b, mode="nn"):
    return lax.dot_general(a, b, _DIMS[mode], preferred_element_type=F32)


def _sigmoid(x):
    return 1.0 / (1.0 + jnp.exp(-x))


class _Comm:
    def __init__(self, inputs, out_shapes, sems, start, finish):
        self.inputs, self.out_shapes, self.sems, self.start, self.finish = inputs, out_shapes, sems, start, finish


def _pcall(name, body, grid, in_specs, out_specs, out_shape, args, scratch=(), sem=None, comm=None, aliases=None):
    single = not isinstance(out_shape, (list, tuple))
    out_shapes = [out_shape] if single else list(out_shape)
    out_specs = [out_specs] if single else list(out_specs)
    n_in, n_out, n_scr = len(args), len(out_shapes), len(scratch)
    aliases = {} if aliases is None else aliases
    if comm is None:
        res = pl.pallas_call(
            body, name=name, grid=grid, in_specs=list(in_specs), out_specs=out_specs, out_shape=out_shapes,
            scratch_shapes=list(scratch), input_output_aliases=aliases, compiler_params=_params(*sem))(*args)
        return (res[0] if single else res), []
    nci, nco = len(comm.inputs), len(comm.out_shapes)

    def wrapped(*refs):
        a = refs[:n_in]
        ci = refs[n_in:n_in + nci]
        o0 = n_in + nci
        o = refs[o0:o0 + n_out]
        co = refs[o0 + n_out:o0 + n_out + nco]
        s0 = o0 + n_out + nco
        s = refs[s0:s0 + n_scr]
        cs = refs[s0 + n_scr:]
        pids = [pl.program_id(i) for i in range(len(grid))]
        first = functools.reduce(jnp.logical_and, [p == 0 for p in pids])
        last = functools.reduce(jnp.logical_and, [p == g - 1 for p, g in zip(pids, grid)])

        @pl.when(first)
        def _():
            comm.start(ci, co, cs)

        body(*a, *o, *s)

        @pl.when(last)
        def _():
            comm.finish(ci, co, cs)

    any_spec = pl.BlockSpec(memory_space=pl.ANY)
    res = pl.pallas_call(
        wrapped, name=name, grid=grid, in_specs=list(in_specs) + [any_spec] * nci,
        out_specs=out_specs + [any_spec] * nco, out_shape=out_shapes + list(comm.out_shapes),
        scratch_shapes=list(scratch) + list(comm.sems), input_output_aliases=aliases,
        compiler_params=_params(*(("arbitrary",) * len(grid))))(*args, *comm.inputs)
    core = res[:n_out]
    return (core[0] if single else core), list(res[n_out:])


def _comm_call(name, comm):
    nci, nco = len(comm.inputs), len(comm.out_shapes)

    def body(*refs):
        ci, co, cs = refs[:nci], refs[nci:nci + nco], refs[nci + nco:]
        comm.start(ci, co, cs)
        comm.finish(ci, co, cs)

    any_spec = pl.BlockSpec(memory_space=pl.ANY)
    return pl.pallas_call(
        body, name=name, in_specs=[any_spec] * nci, out_specs=[any_spec] * nco, out_shape=list(comm.out_shapes),
        scratch_shapes=list(comm.sems), compiler_params=pltpu.CompilerParams(has_side_effects=True))(*comm.inputs)


def _remote(src, dst, ssem, rsem, dev):
    return pltpu.make_async_remote_copy(src_ref=src, dst_ref=dst, send_sem=ssem, recv_sem=rsem, device_id=dev,
                                        device_id_type=pl.DeviceIdType.MESH)


def _place():
    x, y, c = lax.axis_index("x"), lax.axis_index("y"), lax.axis_index("c")
    other_chips = [(1 - x, y), (x, 1 - y), (1 - x, 1 - y)]
    return x, y, c, other_chips


def _slot(x, y, c, swap):
    return 4 * y + 2 * x + c if swap else 4 * x + 2 * y + c


def _chip_slot(x, y, swap):
    return 2 * y + x if swap else 2 * x + y


def _gather_comm(shards, swaps=None):
    n = len(shards)
    per = N_DEV - 1
    swaps = [False] * n if swaps is None else swaps
    pieces = []
    for i, a in enumerate(shards):
        rows = a.shape[0]
        k = GATHER_PIECES if (a.ndim == 2 and rows >= GATHER_PIECE_MIN_ROWS) else 1
        step = -(-rows // (k * 8)) * 8
        if k == 1:
            pieces.append((i, 0, None))
        else:
            pieces += [(i, r, min(step, rows - r)) for r in range(0, rows, step)]
    m = len(pieces)

    def src(ins, v):
        i, r, cnt = pieces[v]
        return ins[i] if cnt is None else ins[i].at[pl.ds(r, cnt)]

    def place(outs, v, x, y, c):
        i, r, cnt = pieces[v]
        blk = outs[i].at[_slot(x, y, c, swaps[i])]
        return blk if cnt is None else blk.at[pl.ds(r, cnt)]

    def start(ins, outs, sems):
        send, recv, loc = sems
        x, y, c, chips = _place()
        for v in range(m):
            me = place(outs, v, x, y, c)
            pltpu.make_async_copy(src(ins, v), me, loc.at[v]).start()
            _remote(src(ins, v), me, send.at[per * v], recv.at[per * v], (x, y, 1 - c)).start()
        for j, (qx, qy) in enumerate(chips):
            for v in range(m):
                _remote(src(ins, v), place(outs, v, x, y, c), send.at[per * v + 1 + j], recv.at[per * v + 1 + j],
                        (qx, qy, c)).start()

    def finish(ins, outs, sems):
        send, recv, loc = sems
        x, y, c, chips = _place()
        sib = (x, y, 1 - c)
        for v in range(m):
            for j, (qx, qy) in enumerate(chips):
                blk = place(outs, v, qx, qy, c)
                _remote(blk, blk, send.at[per * v + 1 + j], recv.at[per * v + 1 + j], (qx, qy, c)).wait_recv()
                _remote(blk, blk, send.at[per * v + 4 + j], recv.at[per * v + 4 + j], sib).start()
        for v in range(m):
            blk = place(outs, v, x, y, 1 - c)
            _remote(blk, blk, send.at[per * v], recv.at[per * v], sib).wait_recv()
            for j, (qx, qy) in enumerate(chips):
                blk = place(outs, v, qx, qy, 1 - c)
                _remote(blk, blk, send.at[per * v + 4 + j], recv.at[per * v + 4 + j], sib).wait_recv()
        for v in range(m):
            own = place(outs, v, x, y, c)
            for k in range(per):
                _remote(src(ins, v), own, send.at[per * v + k], recv.at[per * v + k], sib).wait_send()
            pltpu.make_async_copy(src(ins, v), own, loc.at[v]).wait()

    out_shapes = [jax.ShapeDtypeStruct((N_DEV,) + tuple(a.shape), a.dtype) for a in shards]
    sems = [pltpu.SemaphoreType.DMA((per * m,)), pltpu.SemaphoreType.DMA((per * m,)), pltpu.SemaphoreType.DMA((m,))]
    return _Comm(list(shards), out_shapes, sems, start, finish)


def _pair_comm(slots):
    n = len(slots)

    def copies(ins, outs, sems):
        send, recv = sems
        x, y, c, _ = _place()
        sib = (x, y, 1 - c)
        out = []
        for i in range(n):
            for q in range(4):
                out.append(_remote(ins[i].at[2 * q + 1 - c], outs[i].at[q], send.at[4 * i + q], recv.at[4 * i + q], sib))
        return out

    def start(ins, outs, sems):
        for cp in copies(ins, outs, sems):
            cp.start()

    def finish(ins, outs, sems):
        for cp in copies(ins, outs, sems):
            cp.wait_send()
            cp.wait_recv()

    out_shapes = [jax.ShapeDtypeStruct((4,) + tuple(a.shape[1:]), a.dtype) for a in slots]
    sems = [pltpu.SemaphoreType.DMA((4 * n,)), pltpu.SemaphoreType.DMA((4 * n,))]
    return _Comm(list(slots), out_shapes, sems, start, finish)


def _chip_comm(chip_sums, swaps=None, rows=None):
    n = len(chip_sums)
    swaps = [False] * n if swaps is None else swaps
    rows = [None] * n if rows is None else rows

    def src(ins, i, q):
        return ins[i].at[q] if rows[i] is None else ins[i].at[q, pl.ds(rows[i][0], rows[i][1] - rows[i][0])]

    def start(ins, outs, sems):
        send, recv, loc = sems
        x, y, c, chips = _place()
        for i in range(n):
            mine = _chip_slot(x, y, swaps[i])
            pltpu.make_async_copy(src(ins, i, mine), outs[i].at[mine], loc.at[i]).start()
            for j, (qx, qy) in enumerate(chips):
                _remote(src(ins, i, _chip_slot(qx, qy, swaps[i])), outs[i].at[mine], send.at[3 * i + j],
                        recv.at[3 * i + j], (qx, qy, c)).start()

    def finish(ins, outs, sems):
        send, recv, loc = sems
        x, y, c, chips = _place()
        for i in range(n):
            mine = _chip_slot(x, y, swaps[i])
            for j, (qx, qy) in enumerate(chips):
                theirs = _chip_slot(qx, qy, swaps[i])
                cp = _remote(src(ins, i, theirs), outs[i].at[theirs], send.at[3 * i + j], recv.at[3 * i + j], (qx, qy, c))
                cp.wait_send()
                cp.wait_recv()
            pltpu.make_async_copy(src(ins, i, mine), outs[i].at[mine], loc.at[i]).wait()

    def out_shape(a, r):
        shape = a.shape if r is None else (a.shape[0], r[1] - r[0]) + tuple(a.shape[2:])
        return jax.ShapeDtypeStruct(shape, a.dtype)

    out_shapes = [out_shape(a, r) for a, r in zip(chip_sums, rows)]
    sems = [pltpu.SemaphoreType.DMA((3 * n,)), pltpu.SemaphoreType.DMA((3 * n,)), pltpu.SemaphoreType.DMA((n,))]
    return _Comm(list(chip_sums), out_shapes, sems, start, finish)


def _join_comm(a, b):
    na_i, na_o, na_s = len(a.inputs), len(a.out_shapes), len(a.sems)

    def start(ins, outs, sems):
        a.start(ins[:na_i], outs[:na_o], sems[:na_s])
        b.start(ins[na_i:], outs[na_o:], sems[na_s:])

    def finish(ins, outs, sems):
        a.finish(ins[:na_i], outs[:na_o], sems[:na_s])
        b.finish(ins[na_i:], outs[na_o:], sems[na_s:])

    return _Comm(a.inputs + b.inputs, a.out_shapes + b.out_shapes, a.sems + b.sems, start, finish)


def _row_tile(r):
    for cand in (256, 128):
        if r > cand and r % cand == 0:
            return cand
    return r


def _add_pairs(name, slots, sib):
    r, c = slots.shape[1:]
    tr = _row_tile(r)

    def body(core_ref, s_ref, b_ref, o_ref):
        o_ref[...] = (s_ref[...].astype(F32) + b_ref[...].astype(F32)).astype(o_ref.dtype)

    core = jnp.full((1,), lax.axis_index("c"), jnp.int32)
    return pl.pallas_call(
        body, name=name,
        grid_spec=pltpu.PrefetchScalarGridSpec(
            num_scalar_prefetch=1, grid=(4, r // tr),
            in_specs=[pl.BlockSpec((None, tr, c), lambda q, i, core_ref: (2 * q + core_ref[0], i, 0)),
                      pl.BlockSpec((None, tr, c), lambda q, i, core_ref: (q, i, 0))],
            out_specs=pl.BlockSpec((None, tr, c), lambda q, i, core_ref: (q, i, 0))),
        out_shape=jax.ShapeDtypeStruct((4, r, c), slots.dtype),
        compiler_params=_params("parallel", "parallel"))(core, slots, sib)


def _matmul(name, mode, a, b, grid, a_spec, b_spec, o_spec, out_shape, acc_shape,
            res=None, res_spec=None, alpha=1.0, comm=None):
    nk = grid[-1]
    has_res = res is not None

    def body(*refs):
        if has_res:
            a_ref, b_ref, r_ref, o_ref = refs[:4]
        else:
            a_ref, b_ref, o_ref = refs[:3]
            r_ref = None
        part = _dot(a_ref[...], b_ref[...], mode)

        def finish(v):
            if alpha != 1.0:
                v = v * alpha
            if has_res:
                v = r_ref[...] + v
            o_ref[...] = v.astype(o_ref.dtype)

        if nk == 1:
            finish(part)
        else:
            acc = refs[-1]
            k = pl.program_id(len(grid) - 1)

            @pl.when(k == 0)
            def _():
                acc[...] = part

            @pl.when(k > 0)
            def _():
                acc[...] += part

            @pl.when(k == nk - 1)
            def _():
                finish(acc[...])

    in_specs = [a_spec, b_spec] + ([res_spec] if has_res else [])
    args = (a, b) + ((res,) if has_res else ())
    scratch = [] if nk == 1 else [pltpu.VMEM(acc_shape, F32)]
    sem = ("parallel",) * (len(grid) - 1) + ("arbitrary",)
    out, couts = _pcall(name, body, grid, in_specs, o_spec, out_shape, args, scratch, sem, comm)
    return out if comm is None else (out, couts)


def _mm_nn(name, a, b, out_dtype=F32, res=None, alpha=1.0, tk=None, kk=None, a_off=0, b_off=0, comm=None):
    t = a.shape[0]
    kk = a.shape[1] if kk is None else kk
    n = b.shape[1]
    tk = kk if tk is None else tk
    grid = (t // TM, 1, kk // tk)
    return _matmul(
        name, "nn", a, b, grid,
        pl.BlockSpec((TM, tk), lambda i, j, k: (i, k + a_off)),
        pl.BlockSpec((tk, n), lambda i, j, k: (k + b_off, 0)),
        pl.BlockSpec((TM, n), lambda i, j, k: (i, 0)),
        jax.ShapeDtypeStruct((t, n), out_dtype), (TM, n),
        res=res, res_spec=pl.BlockSpec((TM, n), lambda i, j, k: (i, 0)), alpha=alpha, comm=comm)


def _mm_nt(name, a, b, n=None, tn=None, tk=None, out_dtype=F32, comm=None):
    t, kk = a.shape
    n = b.shape[0] if n is None else n
    tn = n if tn is None else tn
    tk = kk if tk is None else tk
    grid = (n // tn, t // TM, kk // tk)
    return _matmul(
        name, "nt", a, b, grid,
        pl.BlockSpec((TM, tk), lambda j, i, k: (i, k)),
        pl.BlockSpec((tn, tk), lambda j, i, k: (j, k)),
        pl.BlockSpec((TM, tn), lambda j, i, k: (i, j)),
        jax.ShapeDtypeStruct((t, n), out_dtype), (TM, tn), comm=comm)


def _mm_tn(name, a, b, out_dtype, tm=None, n=None, col_off=0, comm=None):
    t, m = a.shape
    n = b.shape[1] if n is None else n
    tm = m if tm is None else tm
    tk = t if t <= TN_MAX_TOKENS else TM
    grid = (m // tm, 1, t // tk)
    return _matmul(
        name, "tn", a, b, grid,
        pl.BlockSpec((tk, tm), lambda j, i, k: (k, j)),
        pl.BlockSpec((tk, n), lambda j, i, k: (k, col_off)),
        pl.BlockSpec((tm, n), lambda j, i, k: (j, 0)),
        jax.ShapeDtypeStruct((m, n), out_dtype), (tm, n), comm=comm)


def _rms_fwd(name, x, w):
    t, d = x.shape

    def body(x_ref, w_ref, h_ref):
        xv = x_ref[...]
        rstd = lax.rsqrt(jnp.mean(xv * xv, axis=-1, keepdims=True) + NORM_EPS)
        h_ref[...] = (xv * rstd * w_ref[...]).astype(h_ref.dtype)

    return pl.pallas_call(
        body, name=name, grid=(t // TE,),
        in_specs=[pl.BlockSpec((TE, d), lambda i: (i, 0)), pl.BlockSpec((1, d), lambda i: (0, 0))],
        out_specs=pl.BlockSpec((TE, d), lambda i: (i, 0)),
        out_shape=jax.ShapeDtypeStruct((t, d), BF), compiler_params=_params("parallel"))(x, w)


def _rms_bwd(name, x, w, dh, dres, out_scale, comm=None):
    t, d = x.shape

    def body(x_ref, w_ref, dh_ref, dres_ref, dx_ref, dxb_ref, dw_ref):
        i = pl.program_id(0)
        xv = x_ref[...]
        rstd = lax.rsqrt(jnp.mean(xv * xv, axis=-1, keepdims=True) + NORM_EPS)
        xhat = xv * rstd
        dhv = dh_ref[...]
        wd = dhv * w_ref[...]
        proj = jnp.mean(wd * xhat, axis=-1, keepdims=True)
        dx = dres_ref[...] + rstd * (wd - xhat * proj)
        dx_ref[...] = dx
        dxb_ref[...] = (dx * out_scale).astype(BF)
        part = jnp.sum(dhv * xhat, axis=0, keepdims=True)

        @pl.when(i == 0)
        def _():
            dw_ref[...] = part

        @pl.when(i > 0)
        def _():
            dw_ref[...] += part

    row = pl.BlockSpec((TE, d), lambda i: (i, 0))
    vec = pl.BlockSpec((1, d), lambda i: (0, 0))
    outs, couts = _pcall(
        name, body, (t // TE,), [row, vec, row, row], [row, row, vec],
        [jax.ShapeDtypeStruct((t, d), F32), jax.ShapeDtypeStruct((t, d), BF), jax.ShapeDtypeStruct((1, d), F32)],
        (x, w, dh, dres), (), ("arbitrary",), comm)
    return outs if comm is None else (outs, couts)


def _final_loss(x, w, target):
    t, d = x.shape

    def body(x_ref, w_ref, t_ref, loss_ref, dx_ref, dxb_ref, dw_ref):
        i = pl.program_id(0)
        xv = x_ref[...]
        rstd = lax.rsqrt(jnp.mean(xv * xv, axis=-1, keepdims=True) + NORM_EPS)
        xhat = xv * rstd
        err = xhat * w_ref[...] - t_ref[...]
        lpart = 0.5 * jnp.sum(jnp.mean(err * err, axis=-1, keepdims=True), axis=0, keepdims=True)
        dy = err * (1.0 / d)
        wd = dy * w_ref[...]
        proj = jnp.mean(wd * xhat, axis=-1, keepdims=True)
        dx = rstd * (wd - xhat * proj)
        dx_ref[...] = dx
        dxb_ref[...] = (0.5 * dx).astype(BF)
        part = jnp.sum(dy * xhat, axis=0, keepdims=True)
        lfull = jnp.broadcast_to(lpart, (1, 128))

        @pl.when(i == 0)
        def _():
            dw_ref[...] = part
            loss_ref[...] = lfull

        @pl.when(i > 0)
        def _():
            dw_ref[...] += part
            loss_ref[...] += lfull

    row = pl.BlockSpec((TE, d), lambda i: (i, 0))
    vec = pl.BlockSpec((1, d), lambda i: (0, 0))
    return pl.pallas_call(
        body, name="final_loss", grid=(t // TE,), in_specs=[row, vec, row],
        out_specs=[pl.BlockSpec((1, 128), lambda i: (0, 0)), row, row, vec],
        out_shape=[jax.ShapeDtypeStruct((1, 128), F32), jax.ShapeDtypeStruct((t, d), F32),
                   jax.ShapeDtypeStruct((t, d), BF), jax.ShapeDtypeStruct((1, d), F32)],
        compiler_params=_params("arbitrary"))(x, w, target)


def _swiglu_fwd(name, gu, comm=None):
    t = gu.shape[0]

    def body(g_ref, u_ref, a_ref):
        g = g_ref[...].astype(F32)
        a_ref[...] = (g * _sigmoid(g) * u_ref[...].astype(F32)).astype(BF)

    blk = (TE, FF_HALF)
    out, couts = _pcall(
        name, body, (t // TE, 2),
        [pl.BlockSpec(blk, lambda i, j: (i, 2 * j)), pl.BlockSpec(blk, lambda i, j: (i, 2 * j + 1))],
        pl.BlockSpec(blk, lambda i, j: (i, j)), jax.ShapeDtypeStruct((t, D_FF), BF),
        (gu, gu), (), ("parallel", "parallel"), comm)
    return out if comm is None else (out, couts)


def _swiglu_bwd(name, gu, dact, comm=None):
    t = gu.shape[0]

    def body(g_ref, u_ref, da_ref, o_ref):
        g = g_ref[...].astype(F32)
        da = da_ref[...].astype(F32)
        s = _sigmoid(g)
        o_ref[:, 0:FF_HALF] = (da * u_ref[...].astype(F32) * (s * (1.0 + g * (1.0 - s)))).astype(BF)
        o_ref[:, FF_HALF:2 * FF_HALF] = (da * g * s).astype(BF)

    blk = (TE, FF_HALF)
    out, couts = _pcall(
        name, body, (t // TE, 2),
        [pl.BlockSpec(blk, lambda i, j: (i, 2 * j)), pl.BlockSpec(blk, lambda i, j: (i, 2 * j + 1)),
         pl.BlockSpec(blk, lambda i, j: (i, j))],
        pl.BlockSpec((TE, 2 * FF_HALF), lambda i, j: (i, j)),
        jax.ShapeDtypeStruct((t, 2 * D_FF), BF), (gu, gu, dact), (), ("parallel", "parallel"), comm)
    return out if comm is None else (out, couts)


CONV_CB = 256


CONV_ROWS = 64
CONV_HALO = 16


def _taps_down(ext, w, k):
    shifted = [pltpu.roll(ext, k - 1 - j, 0)[CONV_HALO:] for j in range(k - 1)] + [ext[CONV_HALO:]]
    out = shifted[k - 1] * w[k - 1:k, :]
    for j in range(k - 1):
        out = out + shifted[j] * w[j:j + 1, :]
    return out, shifted


def _taps_up(ext, w, k):
    rows = ext.shape[0]
    n = rows - CONV_HALO
    out = ext[:n] * w[k - 1:k, :]
    for j in range(k - 1):
        out = out + pltpu.roll(ext, rows - (k - 1 - j), 0)[:n] * w[j:j + 1, :]
    return out


def _rows_before(ref, i, r0):
    start = pl.multiple_of(jnp.maximum(r0 - CONV_HALO, 0), CONV_HALO)
    return jnp.where(i > 0, ref[pl.ds(start, CONV_HALO), :].astype(F32), 0.0)


def _rows_after(ref, r0, t):
    start = pl.multiple_of(jnp.minimum(r0 + CONV_ROWS, t - CONV_HALO), CONV_HALO)
    return ref[pl.ds(start, CONV_HALO), :].astype(F32)


def _fold8(v):
    return v.reshape(v.shape[0] // 8, 8, v.shape[1]).sum(axis=0)


def _silu_grad(pre):
    s = _sigmoid(pre)
    return s * (1.0 + pre * (1.0 - s))


def _pspec(t, off):
    base = off // CONV_CB
    return pl.BlockSpec((t, CONV_CB), lambda j: (0, base + j))


def _mix_a_fwd(p, conv_w):
    t = p.shape[0]

    def body(b_ref, c_ref, xa_ref, w_ref, o_ref):
        w = w_ref[...]

        def step(i, carry):
            r0 = pl.multiple_of(i * CONV_ROWS, CONV_ROWS)
            rows = pl.ds(r0, CONV_ROWS)
            q = c_ref[rows, :].astype(F32) * xa_ref[rows, :].astype(F32)
            q_before = _rows_before(c_ref, i, r0) * _rows_before(xa_ref, i, r0)
            va, _ = _taps_down(jnp.concatenate([q_before, q], axis=0), w, 3)
            o_ref[rows, :] = (b_ref[rows, :].astype(F32) * va).astype(BF)
            return carry

        lax.fori_loop(0, t // CONV_ROWS, step, 0)

    return pl.pallas_call(
        body, name="mix_a_fwd", grid=(D_MODEL // CONV_CB,),
        in_specs=[_pspec(t, OFF_B), _pspec(t, OFF_C), _pspec(t, OFF_XA),
                  pl.BlockSpec((3, CONV_CB), lambda j: (0, j))],
        out_specs=pl.BlockSpec((t, CONV_CB), lambda j: (0, j)),
        out_shape=jax.ShapeDtypeStruct((t, D_MODEL), BF), compiler_params=_params("parallel"))(p, p, p, conv_w)


def _mix_a_bwd(p, conv_w, dya, dp):
    t = p.shape[0]

    def body(b_ref, c_ref, xa_ref, w_ref, dy_ref, dp_in, dp_ref, dw_ref):
        del dp_in
        w = w_ref[...]
        n = t // CONV_ROWS

        def step(i, acc):
            r0 = pl.multiple_of(i * CONV_ROWS, CONV_ROWS)
            rows = pl.ds(r0, CONV_ROWS)
            cv = c_ref[rows, :].astype(F32)
            xav = xa_ref[rows, :].astype(F32)
            q_before = _rows_before(c_ref, i, r0) * _rows_before(xa_ref, i, r0)
            va, shifted = _taps_down(jnp.concatenate([q_before, cv * xav], axis=0), w, 3)
            dyv = dy_ref[rows, :]
            dp_ref[rows, 0:CONV_CB] = (dyv * va).astype(BF)
            dv = dyv * b_ref[rows, :].astype(F32)
            dv_after = jnp.where(i < n - 1, _rows_after(dy_ref, r0, t) * _rows_after(b_ref, r0, t), 0.0)
            dq = _taps_up(jnp.concatenate([dv, dv_after], axis=0), w, 3)
            dp_ref[rows, CONV_CB:2 * CONV_CB] = (dq * xav).astype(BF)
            dp_ref[rows, 2 * CONV_CB:3 * CONV_CB] = (dq * cv).astype(BF)
            return tuple(a + _fold8(dv * s) for a, s in zip(acc, shifted))

        zero = jnp.zeros((8, CONV_CB), F32)
        acc = lax.fori_loop(0, n, step, (zero, zero, zero))
        for j in range(3):
            dw_ref[j:j + 1, :] = jnp.sum(acc[j], axis=0, keepdims=True)

    col = pl.BlockSpec((t, CONV_CB), lambda j: (0, j))
    wsp = pl.BlockSpec((3, CONV_CB), lambda j: (0, j))
    return pl.pallas_call(
        body, name="mix_a_bwd", grid=(D_MODEL // CONV_CB,),
        in_specs=[_pspec(t, OFF_B), _pspec(t, OFF_C), _pspec(t, OFF_XA), wsp, col, pl.BlockSpec(memory_space=pl.ANY)],
        out_specs=[pl.BlockSpec((t, 3 * CONV_CB), lambda j: (0, j)), wsp],
        out_shape=[jax.ShapeDtypeStruct(dp.shape, dp.dtype), jax.ShapeDtypeStruct((3, D_MODEL), F32)],
        input_output_aliases={5: 0},
        compiler_params=_params("parallel"))(p, p, p, conv_w, dya, dp)


def _ssm_conv_fwd(p, conv_w, conv_b, comm=None):
    t = p.shape[0]

    def body(x_ref, w_ref, b_ref, o_ref):
        w = w_ref[...]
        bias = b_ref[...]

        def step(i, carry):
            r0 = pl.multiple_of(i * CONV_ROWS, CONV_ROWS)
            rows = pl.ds(r0, CONV_ROWS)
            ext = jnp.concatenate([_rows_before(x_ref, i, r0), x_ref[rows, :].astype(F32)], axis=0)
            pre = _taps_down(ext, w, 4)[0] + bias
            o_ref[rows, :] = pre * _sigmoid(pre)
            return carry

        lax.fori_loop(0, t // CONV_ROWS, step, 0)

    out, couts = _pcall(
        "ssm_conv_fwd", body, (D_XBC // CONV_CB,),
        [_pspec(t, OFF_XBC), pl.BlockSpec((4, CONV_CB), lambda j: (0, j)), pl.BlockSpec((1, CONV_CB), lambda j: (0, j))],
        pl.BlockSpec((t, CONV_CB), lambda j: (0, j)), jax.ShapeDtypeStruct((t, D_XBC), F32),
        (p, conv_w, conv_b), (), ("parallel",), comm)
    return out if comm is None else (out, couts)


def _ssm_conv_bwd(p, conv_w, conv_b, dxc, dp, comm=None):
    t = p.shape[0]

    def body(x_ref, w_ref, b_ref, d_ref, dp_in, dx_ref, dw_ref, db_ref):
        del dp_in
        w = w_ref[...]
        bias = b_ref[...]
        n = t // CONV_ROWS

        def step(i, acc):
            r0 = pl.multiple_of(i * CONV_ROWS, CONV_ROWS)
            rows = pl.ds(r0, CONV_ROWS)
            x_cur = x_ref[rows, :].astype(F32)
            pre, shifted = _taps_down(jnp.concatenate([_rows_before(x_ref, i, r0), x_cur], axis=0), w, 4)
            pre = pre + bias
            dpre = d_ref[rows, :] * _silu_grad(pre)
            ext_after = jnp.concatenate([x_cur[CONV_ROWS - CONV_HALO:], _rows_after(x_ref, r0, t)], axis=0)
            pre_after = _taps_down(ext_after, w, 4)[0] + bias
            dpre_after = jnp.where(i < n - 1, _rows_after(d_ref, r0, t) * _silu_grad(pre_after), 0.0)
            dx_ref[rows, :] = _taps_up(jnp.concatenate([dpre, dpre_after], axis=0), w, 4).astype(BF)
            new = tuple(a + _fold8(dpre * s) for a, s in zip(acc[:4], shifted))
            return new + (acc[4] + _fold8(dpre),)

        zero = jnp.zeros((8, CONV_CB), F32)
        acc = lax.fori_loop(0, n, step, (zero,) * 5)
        for j in range(4):
            dw_ref[j:j + 1, :] = jnp.sum(acc[j], axis=0, keepdims=True)
        db_ref[...] = jnp.sum(acc[4], axis=0, keepdims=True)

    col = pl.BlockSpec((t, CONV_CB), lambda j: (0, j))
    wsp = pl.BlockSpec((4, CONV_CB), lambda j: (0, j))
    bsp = pl.BlockSpec((1, CONV_CB), lambda j: (0, j))
    outs, couts = _pcall(
        "ssm_conv_bwd", body, (D_XBC // CONV_CB,),
        [_pspec(t, OFF_XBC), wsp, bsp, col, pl.BlockSpec(memory_space=pl.ANY)], [_pspec(t, OFF_XBC), wsp, bsp],
        [jax.ShapeDtypeStruct(dp.shape, dp.dtype), jax.ShapeDtypeStruct((4, D_XBC), F32),
         jax.ShapeDtypeStruct((1, D_XBC), F32)],
        (p, conv_w, conv_b, dxc, dp), (), ("parallel",), comm, aliases={4: 0})
    return outs if comm is None else (outs, couts)


DT_ROWS = 512


def _tri(lower):
    r = lax.broadcasted_iota(jnp.int32, (CHUNK, CHUNK), 0)
    c = lax.broadcasted_iota(jnp.int32, (CHUNK, CHUNK), 1)
    return jnp.where((r >= c) if lower else (r <= c), 1.0, 0.0).astype(F32)


def _dot_exact(a, b):
    return lax.dot_general(a, b, _DIMS["nn"], preferred_element_type=F32, precision=lax.Precision.HIGHEST)


def _dt_fwd(p, bias_pad, alog_pad):
    t = p.shape[0]

    def body(raw_ref, b_ref, al_ref, dt_ref, acs_ref):
        z = raw_ref[...] + b_ref[...]
        dt = jnp.maximum(z, 0.0) + jnp.log(1.0 + jnp.exp(-jnp.abs(z)))
        dt_ref[...] = dt
        a = dt * (-jnp.exp(al_ref[...]))
        tri = _tri(True)
        for k in range(DT_ROWS // CHUNK):
            acs_ref[k * CHUNK:(k + 1) * CHUNK, :] = _dot_exact(tri, a[k * CHUNK:(k + 1) * CHUNK, :])

    blk = pl.BlockSpec((DT_ROWS, DT_W), lambda i: (i, 0))
    vec = pl.BlockSpec((1, DT_W), lambda i: (0, 0))
    return pl.pallas_call(
        body, name="dt_fwd", grid=(t // DT_ROWS,),
        in_specs=[pl.BlockSpec((DT_ROWS, DT_W), lambda i: (i, OFF_DT // DT_W)), vec, vec],
        out_specs=[blk, blk], out_shape=[jax.ShapeDtypeStruct((t, DT_W), F32)] * 2,
        compiler_params=_params("parallel"))(p, bias_pad, alog_pad)


def _dt_bwd(p, bias_pad, alog_pad, dt, ddt, dacs, dp_gd):
    t = p.shape[0]

    def body(raw_ref, b_ref, al_ref, dt_ref, ddt_ref, dacs_ref, dp_in, draw_ref, db_ref, dal_ref):
        del dp_in
        i = pl.program_id(0)
        acoef = -jnp.exp(al_ref[...])
        triu = _tri(False)
        das = []
        for k in range(DT_ROWS // CHUNK):
            das.append(_dot_exact(triu, dacs_ref[k * CHUNK:(k + 1) * CHUNK, :]))
        da = jnp.concatenate(das, axis=0)
        dtv = dt_ref[...]
        ddt_tot = ddt_ref[...] + da * acoef
        lane = lax.broadcasted_iota(jnp.int32, (DT_ROWS, DT_W), 1)
        draw = jnp.where(lane < N_HEADS, ddt_tot * _sigmoid(raw_ref[...] + b_ref[...]), 0.0)
        draw_ref[...] = draw.astype(BF)
        pb = jnp.sum(draw, axis=0, keepdims=True)
        pa = jnp.sum(da * dtv * acoef, axis=0, keepdims=True)

        @pl.when(i == 0)
        def _():
            db_ref[...] = pb
            dal_ref[...] = pa

        @pl.when(i > 0)
        def _():
            db_ref[...] += pb
            dal_ref[...] += pa

    blk = pl.BlockSpec((DT_ROWS, DT_W), lambda i: (i, 0))
    vec = pl.BlockSpec((1, DT_W), lambda i: (0, 0))
    return pl.pallas_call(
        body, name="dt_bwd", grid=(t // DT_ROWS,),
        in_specs=[pl.BlockSpec((DT_ROWS, DT_W), lambda i: (i, OFF_DT // DT_W)), vec, vec, blk, blk, blk,
                  pl.BlockSpec(memory_space=pl.ANY)],
        out_specs=[pl.BlockSpec((DT_ROWS, DT_W), lambda i: (i, OFF_DT // DT_W)), vec, vec],
        out_shape=[jax.ShapeDtypeStruct(dp_gd.shape, dp_gd.dtype), jax.ShapeDtypeStruct((1, DT_W), F32),
                   jax.ShapeDtypeStruct((1, DT_W), F32)],
        input_output_aliases={6: 0},
        compiler_params=_params("arbitrary"))(p, bias_pad, alog_pad, dt, ddt, dacs, dp_gd)


def _split_dot(z, onehot, terms):
    out = None
    rest = z
    for _ in range(terms):
        piece = rest.astype(BF)
        part = _dot(piece, onehot)
        out = part if out is None else out + part
        rest = rest - piece.astype(F32)
    return out


def _spread_mat():
    row = lax.broadcasted_iota(jnp.int32, (DT_W, D_INNER), 0)
    lane = lax.broadcasted_iota(jnp.int32, (DT_W, D_INNER), 1)
    return jnp.where(row == lane // HEAD_DIM, 1.0, 0.0).astype(BF)


def _gather_mat():
    row = lax.broadcasted_iota(jnp.int32, (D_INNER, DT_W), 0)
    lane = lax.broadcasted_iota(jnp.int32, (D_INNER, DT_W), 1)
    return jnp.where(lane == row // HEAD_DIM, 1.0, 0.0).astype(BF)


def _ssd_masks():
    row = lax.broadcasted_iota(jnp.int32, (CHUNK, GROUP_W), 0)
    col = lax.broadcasted_iota(jnp.int32, (CHUNK, GROUP_W), 1) % HEAD_DIM
    brow = lax.broadcasted_iota(jnp.int32, (GROUP_W, GROUP_W), 0) // HEAD_DIM
    bcol = lax.broadcasted_iota(jnp.int32, (GROUP_W, GROUP_W), 1) // HEAD_DIM
    return row >= col, row == col, brow == bcol


def _stack4(v):
    return jnp.concatenate([v, v, v, v], axis=0)


def _fold4(v):
    return v[0:CHUNK] + v[CHUNK:2 * CHUNK] + v[2 * CHUNK:3 * CHUNK] + v[3 * CHUNK:4 * CHUNK]


def _ssd_group(xc_ref, wide_ref, g, tri, eye, blockdiag):
    gs = slice(GROUP_W * g, GROUP_W * (g + 1))
    xs_g = xc_ref[:, gs]
    b_g = xc_ref[:, D_INNER + D_STATE * g:D_INNER + D_STATE * (g + 1)].astype(BF)
    c_g = xc_ref[:, D_INNER + 1024 + D_STATE * g:D_INNER + 1024 + D_STATE * (g + 1)].astype(BF)
    acs_e, dt_e = wide_ref[0:CHUNK, gs], wide_ref[CHUNK:2 * CHUNK, gs]
    atot_e = acs_e[CHUNK - 1:CHUNK, :]
    acs_j = jnp.sum(jnp.where(eye, acs_e, 0.0), axis=0, keepdims=True)
    lmat = jnp.where(tri, jnp.exp(jnp.minimum(acs_e - acs_j, 0.0)), 0.0)
    b_t = _stack4(b_g)
    m = _dot(c_g, b_t, "nt") * lmat
    x_g = xs_g * dt_e
    xbd = jnp.where(blockdiag, _stack4(x_g), 0.0).astype(BF)
    return dict(gs=gs, xs=xs_g, b=b_g, c=c_g, b_t=b_t, dt=dt_e, e=jnp.exp(acs_e), dec=jnp.exp(atot_e - acs_e),
                eat=jnp.exp(atot_e), lmat=lmat, m=m, x=x_g, xbd=xbd)


def _ssd_fwd(xconv, dt, acs, d_exp, comm=None):
    t = xconv.shape[0]
    nc = t // CHUNK

    def body(xc_ref, dt_ref, acs_ref, d_ref, y_ref, hs_ref, state, wide):
        c = pl.program_id(0)

        @pl.when(c == 0)
        def _():
            state[...] = jnp.zeros_like(state)

        hs_ref[...] = state[...]
        tri, eye, blockdiag = _ssd_masks()
        wide[...] = _split_dot(jnp.concatenate([acs_ref[...], dt_ref[...]], axis=0), _spread_mat(), 3)
        for g in range(N_GROUPS):
            q = _ssd_group(xc_ref, wide, g, tri, eye, blockdiag)
            gs = q["gs"]
            h_t = state[:, gs]
            ydiag = _dot(q["m"].astype(BF), q["xbd"])
            yoff = _dot(q["c"], h_t.astype(BF)) * q["e"]
            y_ref[:, gs] = ydiag + yoff + d_ref[:, gs] * q["xs"]
            s_t = _dot(q["b"], (q["x"] * q["dec"]).astype(BF), "tn")
            state[:, gs] = q["eat"] * h_t + s_t

    blk = lambda w: pl.BlockSpec((CHUNK, w), lambda c: (c, 0))
    outs, couts = _pcall(
        "ssd_fwd", body, (nc,),
        [blk(D_XBC), blk(DT_W), blk(DT_W), pl.BlockSpec((1, D_INNER), lambda c: (0, 0))],
        [blk(D_INNER), pl.BlockSpec((None, D_STATE, D_INNER), lambda c: (c, 0, 0))],
        [jax.ShapeDtypeStruct((t, D_INNER), F32), jax.ShapeDtypeStruct((nc, D_STATE, D_INNER), F32)],
        (xconv, dt, acs, d_exp), [pltpu.VMEM((D_STATE, D_INNER), F32), pltpu.VMEM((2 * CHUNK, D_INNER), F32)],
        ("arbitrary",), comm)
    return outs if comm is None else (outs, couts)


def _ssd_bwd(xconv, dt, acs, d_exp, hsave, dy, comm=None):
    t = xconv.shape[0]
    nc = t // CHUNK

    def body(xc_ref, dt_ref, acs_ref, d_ref, hs_ref, dy_ref, dxc_ref, ddt_ref, dacs_ref, dd_ref, dstate, wide, per_head):
        c = pl.program_id(0)

        @pl.when(c == 0)
        def _():
            dstate[...] = jnp.zeros_like(dstate)
            dd_ref[...] = jnp.zeros_like(dd_ref)

        tri, eye, blockdiag = _ssd_masks()
        acsv = acs_ref[...]
        wide[...] = _split_dot(jnp.concatenate([acsv, dt_ref[...]], axis=0), _spread_mat(), 3)
        eat_heads = jnp.exp(acsv[CHUNK - 1:CHUNK, :])

        for g in range(N_GROUPS):
            q = _ssd_group(xc_ref, wide, g, tri, eye, blockdiag)
            gs, xs_g, b_g, c_g, m = q["gs"], q["xs"], q["b"], q["c"], q["m"]
            bs = slice(D_INNER + D_STATE * g, D_INNER + D_STATE * (g + 1))
            cs = slice(D_INNER + 1024 + D_STATE * g, D_INNER + 1024 + D_STATE * (g + 1))
            h_t = hs_ref[:, gs]
            h_b = h_t.astype(BF)
            dy_g = dy_ref[:, gs]
            dy_b = dy_g.astype(BF)
            ds_t = dstate[:, gs]
            ds_b = ds_t.astype(BF)

            yoff = _dot(c_g, h_b) * q["e"]
            edy = (q["e"] * dy_g).astype(BF)
            d_c = _dot(edy, h_b, "nt")
            d_ht = _dot(c_g, edy, "tn")
            bds = _dot(b_g, ds_b)
            xd = q["x"] * q["dec"]
            d_b = _dot(xd.astype(BF), ds_b, "nt")
            dm = _dot(dy_b, q["xbd"], "nt")
            cross = _dot(m.astype(BF), dy_b, "tn")
            dx_full = q["dec"] * bds + _fold4(jnp.where(blockdiag, cross, 0.0))
            dml = (dm * q["lmat"]).astype(BF)
            d_c = d_c + _dot(dml, q["b_t"])
            d_b = d_b + _fold4(_dot(dml, c_g, "tn"))
            w = dm * m
            q_dec = xd * bds
            z = w - jnp.where(eye, jnp.sum(w, axis=0, keepdims=True), 0.0) + dy_g * yoff - q_dec
            rows = jnp.concatenate(
                [jnp.sum(q_dec, axis=0, keepdims=True), jnp.sum(ds_t * h_t, axis=0, keepdims=True),
                 jnp.zeros((6, GROUP_W), F32)], axis=0)
            per_head[:, gs] = jnp.concatenate([z, dx_full * xs_g, rows], axis=0)
            dxc_ref[:, cs] = d_c
            dxc_ref[:, bs] = d_b
            dxc_ref[:, gs] = dx_full * q["dt"] + d_ref[:, gs] * dy_g
            dd_ref[:, gs] += jnp.sum(dy_g * xs_g, axis=0, keepdims=True)
            dstate[:, gs] = q["eat"] * ds_t + d_ht

        seg = _split_dot(per_head[...], _gather_mat(), 2)
        datot = seg[2 * CHUNK:2 * CHUNK + 1] + eat_heads * seg[2 * CHUNK + 1:2 * CHUNK + 2]
        rowi = lax.broadcasted_iota(jnp.int32, (CHUNK, DT_W), 0)
        ddt_ref[...] = seg[CHUNK:2 * CHUNK]
        dacs_ref[...] = seg[0:CHUNK] + jnp.where(rowi == CHUNK - 1, datot, 0.0)

    rev = lambda w: pl.BlockSpec((CHUNK, w), lambda c: (nc - 1 - c, 0))
    vec = pl.BlockSpec((1, D_INNER), lambda c: (0, 0))
    outs, couts = _pcall(
        "ssd_bwd", body, (nc,),
        [rev(D_XBC), rev(DT_W), rev(DT_W), vec,
         pl.BlockSpec((None, D_STATE, D_INNER), lambda c: (nc - 1 - c, 0, 0)), rev(D_INNER)],
        [rev(D_XBC), rev(DT_W), rev(DT_W), vec],
        [jax.ShapeDtypeStruct((t, D_XBC), F32), jax.ShapeDtypeStruct((t, DT_W), F32),
         jax.ShapeDtypeStruct((t, DT_W), F32), jax.ShapeDtypeStruct((1, D_INNER), F32)],
        (xconv, dt, acs, d_exp, hsave, dy),
        [pltpu.VMEM((D_STATE, D_INNER), F32), pltpu.VMEM((2 * CHUNK, D_INNER), F32),
         pltpu.VMEM((2 * CHUNK + 8, D_INNER), F32)], ("arbitrary",), comm)
    return outs if comm is None else (outs, couts)


GN_CB = 1024
GN_GROUPS = GN_CB // GROUP_W


def _gnorm_fwd(y, p, w, comm=None):
    t = y.shape[0]
    zoff = OFF_Z // GN_CB

    def body(y_ref, z_ref, w_ref, o_ref):
        for g in range(GN_GROUPS):
            gs = slice(GROUP_W * g, GROUP_W * (g + 1))
            z = z_ref[:, gs].astype(F32)
            yf = y_ref[:, gs] * (z * _sigmoid(z))
            rstd = lax.rsqrt(jnp.mean(yf * yf, axis=-1, keepdims=True) + NORM_EPS)
            o_ref[:, gs] = (yf * rstd * w_ref[:, gs]).astype(BF)

    blk = pl.BlockSpec((TE, GN_CB), lambda i, j: (i, j))
    out, couts = _pcall(
        "gnorm_fwd", body, (t // TE, D_INNER // GN_CB),
        [blk, pl.BlockSpec((TE, GN_CB), lambda i, j: (i, zoff + j)), pl.BlockSpec((1, GN_CB), lambda i, j: (0, j))],
        blk, jax.ShapeDtypeStruct((t, D_INNER), BF), (y, p, w), (), ("parallel", "parallel"), comm)
    return out if comm is None else (out, couts)


def _gnorm_bwd(y, p, w, dyn, comm=None):
    t = y.shape[0]
    zoff = OFF_Z // GN_CB

    def body(y_ref, z_ref, w_ref, dn_ref, dy_ref, dz_ref, dw_ref):
        i = pl.program_id(1)
        for g in range(GN_GROUPS):
            gs = slice(GROUP_W * g, GROUP_W * (g + 1))
            z = z_ref[:, gs].astype(F32)
            yv = y_ref[:, gs]
            s = _sigmoid(z)
            sil = z * s
            yf = yv * sil
            rstd = lax.rsqrt(jnp.mean(yf * yf, axis=-1, keepdims=True) + NORM_EPS)
            xhat = yf * rstd
            dn = dn_ref[:, gs]
            wd = dn * w_ref[:, gs]
            proj = jnp.mean(wd * xhat, axis=-1, keepdims=True)
            dyf = rstd * (wd - xhat * proj)
            dy_ref[:, gs] = dyf * sil
            dz_ref[:, gs] = (dyf * yv * (s * (1.0 + z * (1.0 - s)))).astype(BF)
            part = jnp.sum(dn * xhat, axis=0, keepdims=True)

            @pl.when(i == 0)
            def _():
                dw_ref[:, gs] = part

            @pl.when(i > 0)
            def _():
                dw_ref[:, gs] += part

    blk = pl.BlockSpec((TE, GN_CB), lambda j, i: (i, j))
    vec = pl.BlockSpec((1, GN_CB), lambda j, i: (0, j))
    outs, couts = _pcall(
        "gnorm_bwd", body, (D_INNER // GN_CB, t // TE),
        [blk, pl.BlockSpec((TE, GN_CB), lambda j, i: (i, zoff + j)), vec, blk],
        [blk, pl.BlockSpec((TE, GN_CB), lambda j, i: (i, zoff + j)), vec],
        [jax.ShapeDtypeStruct((t, D_INNER), F32), jax.ShapeDtypeStruct((t, N_MAIN), BF),
         jax.ShapeDtypeStruct((1, D_INNER), F32)],
        (y, p, w, dyn), (), ("parallel", "arbitrary"), comm)
    return outs if comm is None else (outs, couts)


MERGE_CB = 512


def _merge_fwd(p, ya, yb):
    t = ya.shape[0]

    def body(ga_ref, gb_ref, ya_ref, yb_ref, o_ref):
        o_ref[...] = (_sigmoid(ga_ref[...]) * ya_ref[...] + _sigmoid(gb_ref[...]) * yb_ref[...]).astype(BF)

    blk = pl.BlockSpec((TE, MERGE_CB), lambda i, j: (i, j))
    return pl.pallas_call(
        body, name="merge_fwd", grid=(t // TE, D_MODEL // MERGE_CB),
        in_specs=[pl.BlockSpec((TE, MERGE_CB), lambda i, j: (i, 2 * j)),
                  pl.BlockSpec((TE, MERGE_CB), lambda i, j: (i, 2 * j + 1)), blk, blk],
        out_specs=blk, out_shape=jax.ShapeDtypeStruct((t, D_MODEL), BF),
        compiler_params=_params("parallel", "parallel"))(p, p, ya, yb)


def _merge_bwd(p, ya, yb, dm):
    t = ya.shape[0]

    def body(ga_ref, gb_ref, ya_ref, yb_ref, dm_ref, dg_ref, dya_ref, dyb_ref):
        d = dm_ref[...]
        sa = _sigmoid(ga_ref[...])
        sb = _sigmoid(gb_ref[...])
        dg_ref[:, 0:MERGE_CB] = (d * ya_ref[...] * sa * (1.0 - sa)).astype(BF)
        dg_ref[:, MERGE_CB:2 * MERGE_CB] = (d * yb_ref[...] * sb * (1.0 - sb)).astype(BF)
        dya_ref[...] = (d * sa).astype(BF)
        dyb_ref[...] = (d * sb).astype(BF)

    blk = pl.BlockSpec((TE, MERGE_CB), lambda i, j: (i, j))
    return pl.pallas_call(
        body, name="merge_bwd", grid=(t // TE, D_MODEL // MERGE_CB),
        in_specs=[pl.BlockSpec((TE, MERGE_CB), lambda i, j: (i, 2 * j)),
                  pl.BlockSpec((TE, MERGE_CB), lambda i, j: (i, 2 * j + 1)), blk, blk, blk],
        out_specs=[pl.BlockSpec((TE, 2 * MERGE_CB), lambda i, j: (i, j)), blk, blk],
        out_shape=[jax.ShapeDtypeStruct((t, N_GD), BF)] + [jax.ShapeDtypeStruct((t, D_MODEL), BF)] * 2,
        compiler_params=_params("parallel", "parallel"))(p, p, ya, yb, dm)


def _adamw(name, parts, w, m, v, comm=None):
    r, c = w.shape
    tr = _row_tile(r)
    tc = ADAM_COL_TILE if (tr == r and r > 512 and c % ADAM_COL_TILE == 0) else c
    n_parts = parts.shape[0]
    bc1 = 1.0 - ADAM_B1 ** ADAM_STEP
    bc2 = 1.0 - ADAM_B2 ** ADAM_STEP

    def body(p_ref, w_ref, m_ref, v_ref, g_ref, d_ref, nm_ref, nv_ref):
        g = p_ref[0].astype(F32)
        for k in range(1, n_parts):
            g = g + p_ref[k].astype(F32)
        nm = ADAM_B1 * m_ref[...] + (1.0 - ADAM_B1) * g
        nv = ADAM_B2 * v_ref[...] + (1.0 - ADAM_B2) * (g * g)
        g_ref[...] = g
        nm_ref[...] = nm
        nv_ref[...] = nv
        d_ref[...] = -ADAM_LR * ((nm / bc1) / (jnp.sqrt(nv / bc2) + ADAM_EPS) + ADAM_WD * w_ref[...])

    blk = pl.BlockSpec((tr, tc), lambda i, j: (i, j))
    outs, couts = _pcall(
        name, body, (r // tr, c // tc),
        [pl.BlockSpec((n_parts, tr, tc), lambda i, j: (0, i, j)), blk, blk, blk], [blk] * 4,
        [jax.ShapeDtypeStruct((r, c), F32)] * 4, (parts, w, m, v), (), ("parallel", "parallel"), comm)
    return outs if comm is None else (outs, couts)


def _device_blocks(parts, rows):
    starts = [0]
    for a in parts:
        starts.append(starts[-1] + a.shape[0])
    blocks = []
    for d in range(N_DEV):
        lo, hi = d * rows, (d + 1) * rows
        pieces = [a[max(lo, s) - s:min(hi, s + a.shape[0]) - s]
                  for a, s in zip(parts, starts) if s < hi and s + a.shape[0] > lo]
        blocks.append(jnp.concatenate(pieces, axis=0))
    return jnp.stack(blocks, axis=0)


def _pad_lanes(v, width):
    return jnp.pad(v, ((0, 0), (0, width - v.shape[1])))


def _reduce_start(slots, host):
    outs, sib = host(_pair_comm([a for _, a in slots]))
    sums = [(n, _add_pairs("pairsum_" + n, a, b)) for (n, a), b in zip(slots, sib)]
    return outs, sums


def _train_step(x, target, shard, rep):
    gdt = BF
    recv = {}
    (got,) = _comm_call("gather_ffn1_in", _gather_comm([shard["ffn1_w_in"]], [True]))
    w1_in = got.reshape(2 * D_FF, D_MODEL)
    h1 = _rms_fwd("rms1_fwd", x, rep["ffn1_norm"])
    gu1, got = _mm_nt("ffn1_in", h1, w1_in, tn=FF_HALF, out_dtype=BF, comm=_gather_comm(
        [shard["ffn1_w_out"], shard["w_in"], shard["short_conv_w"], shard["ssm_conv_w"]]))
    w1_out = got[0].reshape(D_FF, D_MODEL)
    w_in_t = jnp.concatenate([got[1][d] for d in range(N_DEV)], axis=0)
    short_conv_w = got[2].transpose(1, 0, 2).reshape(3, D_MODEL)
    ssm_conv_w = got[3].transpose(1, 0, 2).reshape(4, D_XBC)
    act1 = _swiglu_fwd("swiglu1_fwd", gu1)
    x1 = _mm_nn("ffn1_out", act1, w1_out, res=x, alpha=0.5)
    ga0 = N_MAIN + N_HEADS
    gb0 = ga0 + D_MODEL
    half = D_MODEL // 2
    w_gd = jnp.concatenate(
        [w_in_t[ga0:ga0 + half], w_in_t[gb0:gb0 + half], w_in_t[ga0 + half:gb0], w_in_t[gb0 + half:],
         w_in_t[N_MAIN:N_MAIN + N_HEADS], jnp.zeros((DT_W - N_HEADS, D_MODEL), BF)], axis=0)
    w_mix_perm = w_in_t[0:3 * D_MODEL].reshape(3, 4, CONV_CB, D_MODEL).transpose(1, 0, 2, 3).reshape(3 * D_MODEL, D_MODEL)

    h2 = _rms_fwd("rms2_fwd", x1, rep["mix_norm"])
    p, got = _mm_nt("proj_main", h2, w_in_t, n=N_MAIN, tn=1024, out_dtype=BF, comm=_gather_comm(
        [shard["short_w_out"], shard["ssm_w_out"], shard["w_out"]]))
    p_gd = _mm_nt("proj_gd", h2, w_gd)
    short_w_out = got[0].reshape(D_MODEL, D_MODEL)
    ssm_w_out = got[1].reshape(D_INNER, D_MODEL)
    w_out = got[2].reshape(D_MODEL, D_MODEL)
    ya_in = _mix_a_fwd(p, short_conv_w)
    y_a = _mm_nn("short_out", ya_in, short_w_out)
    xconv, (got,) = _ssm_conv_fwd(p, ssm_conv_w, rep["ssm_conv_b"], comm=_gather_comm([shard["ffn2_w_out"]]))
    w2_out = got.reshape(D_FF, D_MODEL)
    dt, acs = _dt_fwd(p_gd, rep["dt_bias_pad"], rep["a_log_pad"])
    (y_ssm, hsave), (got,) = _ssd_fwd(xconv, dt, acs, rep["d_exp"], comm=_gather_comm([shard["ffn2_w_in"]], [True]))
    w2_in = got.reshape(2 * D_FF, D_MODEL)
    yn = _gnorm_fwd(y_ssm, p, rep["ssm_norm"])
    y_b = _mm_nn("ssm_out", yn, ssm_w_out, tk=1024)
    merged = _merge_fwd(p_gd, y_a, y_b)
    x2 = _mm_nn("mix_out", merged, w_out, res=x1)

    h3 = _rms_fwd("rms3_fwd", x2, rep["ffn2_norm"])
    gu2 = _mm_nt("ffn2_in", h3, w2_in, tn=FF_HALF, out_dtype=BF)
    act2 = _swiglu_fwd("swiglu2_fwd", gu2)
    x3 = _mm_nn("ffn2_out", act2, w2_out, res=x2, alpha=0.5)

    loss, dx3, dx3h, g_final = _final_loss(x3, rep["final_norm"], target)

    small = {"final_norm": g_final}
    dact2 = _mm_nt("ffn2_out_bwd_act", dx3h, w2_out, out_dtype=BF)
    g_w2_out = _mm_tn("ffn2_out_bwd_w", act2, dx3h, gdt, tm=FF_HALF)
    dgu2 = _swiglu_bwd("swiglu2_bwd", gu2, dact2)
    g_w2_in = _mm_tn("ffn2_in_bwd_w", dgu2, h3, gdt, tm=FF_HALF)
    dh3 = _mm_nn("ffn2_in_bwd_h", dgu2, w2_in, tk=FF_HALF)
    dx2, dx2b, small["ffn2_norm"] = _rms_bwd("rms3_bwd", x2, rep["ffn2_norm"], dh3, dx3, 1.0)

    dmerged = _mm_nt("mix_out_bwd_x", dx2b, w_out)
    g_w_out = _mm_tn("mix_out_bwd_w", merged, dx2b, gdt)
    dp_gd, dya, dyb = _merge_bwd(p_gd, y_a, y_b, dmerged)

    dya_in = _mm_nt("short_out_bwd_x", dya, short_w_out)
    g_short_w_out = _mm_tn("short_out_bwd_w", ya_in, dya, gdt)

    dyn = _mm_nt("ssm_out_bwd_x", dyb, ssm_w_out)
    g_ssm_w_out = _mm_tn("ssm_out_bwd_w", yn, dyb, gdt)
    late = [("ffn2_w_out", g_w2_out.reshape(N_DEV, FF_SHARD // 2, D_MODEL)),
            ("ffn2_w_in", g_w2_in.reshape(N_DEV, FF_SHARD, D_MODEL)),
            ("w_out", g_w_out.reshape(N_DEV, -1, D_MODEL)), ("short_w_out", g_short_w_out.reshape(N_DEV, -1, D_MODEL)),
            ("ssm_w_out", g_ssm_w_out.reshape(N_DEV, -1, D_MODEL))]
    (dy_ssm, dp, small["ssm_norm"]), sums = _reduce_start(
        late, lambda comm: _gnorm_bwd(y_ssm, p, rep["ssm_norm"], dyn, comm=comm))
    dp, g_short_conv = _mix_a_bwd(p, short_conv_w, dya_in, dp)
    first = [(n, a) for n, a in sums if n.startswith("ffn2")]
    second = [(n, a) for n, a in sums if not n.startswith("ffn2")]
    (dxconv, ddt, dacs, dd_lane), got = _ssd_bwd(
        xconv, dt, acs, rep["d_exp"], hsave, dy_ssm,
        comm=_chip_comm([a for _, a in first], [n == "ffn2_w_in" for n, _ in first]))
    recv.update({n: a for (n, _), a in zip(first, got)})
    small["ssm_D"] = dd_lane.reshape(N_HEADS, HEAD_DIM).sum(axis=1)[None, :]
    (dp, g_ssm_conv, small["ssm_conv_b"]), got = _ssm_conv_bwd(
        p, ssm_conv_w, rep["ssm_conv_b"], dxconv, dp, comm=_chip_comm([a for _, a in second]))
    recv.update({n: a for (n, _), a in zip(second, got)})
    dp_gd, dbias, dalog = _dt_bwd(p_gd, rep["dt_bias_pad"], rep["a_log_pad"], dt, ddt, dacs, dp_gd)
    small["ssm_dt_bias"] = dbias[:, :N_HEADS]
    small["ssm_A_log"] = dalog[:, :N_HEADS]

    g_main = _mm_tn("proj_main_bwd_w", dp, h2, gdt, tm=1024)
    g_gd = _mm_tn("proj_gd_bwd_w", dp_gd, h2, gdt)
    g_mix = g_main[0:3 * D_MODEL].reshape(4, 3, CONV_CB, D_MODEL).transpose(1, 0, 2, 3).reshape(3 * D_MODEL, D_MODEL)
    g_in_t = _device_blocks(
        [g_mix, g_main[3 * D_MODEL:], g_gd[2 * D_MODEL:2 * D_MODEL + N_HEADS],
         g_gd[0:half], g_gd[2 * half:3 * half], g_gd[half:2 * half], g_gd[3 * half:4 * half]], IN_SHARD)
    dh2, w_sums = _reduce_start(
        [("w_in", g_in_t)], lambda comm: _mm_nn("proj_mix_bwd_x", dp, w_mix_perm, tk=1024, kk=3 * D_MODEL, comm=comm))
    w_sum = w_sums[0][1]

    def w_piece(i):
        return _chip_comm([w_sum], rows=[W_GRAD_ROW_CUTS[i]])

    dh2, got0 = _mm_nn("proj_rest_bwd_x", dp, w_in_t, tk=1024, kk=N_MAIN - 3 * D_MODEL, a_off=3, b_off=3, res=dh2,
                       comm=w_piece(0))
    dh2, got1 = _mm_nn("proj_gd_bwd_x", dp_gd, w_gd, res=dh2, comm=w_piece(1))
    (dx1, dx1h, small["mix_norm"]), got2 = _rms_bwd("rms2_bwd", x1, rep["mix_norm"], dh2, dx2, 0.5, comm=w_piece(2))
    g_w1_out, got3 = _mm_tn("ffn1_out_bwd_w", act1, dx1h, gdt, tm=FF_HALF, comm=w_piece(3))
    rest = [("ffn1_w_out", g_w1_out.reshape(N_DEV, FF_SHARD // 2, D_MODEL)),
            ("short_conv_w", g_short_conv.reshape(3, N_DEV, -1).transpose(1, 0, 2)),
            ("ssm_conv_w", g_ssm_conv.reshape(4, N_DEV, -1).transpose(1, 0, 2))]
    dact1, got = _mm_nt("ffn1_out_bwd_act", dx1h, w1_out, out_dtype=BF,
                        comm=_join_comm(w_piece(4), _pair_comm([a for _, a in rest])))
    got4, sib = got[0], got[1:]
    rest_sums = [(n, _add_pairs("pairsum_" + n, a, b)) for (n, a), b in zip(rest, sib)]
    recv["w_in"] = jnp.concatenate([got0[0], got1[0], got2[0], got3[0], got4], axis=1)
    dgu1, got = _swiglu_bwd("swiglu1_bwd", gu1, dact1, comm=_chip_comm([a for _, a in rest_sums]))
    recv.update({n: a for (n, _), a in zip(rest_sums, got)})

    def part(tag, width, off, comm=None):
        out = _mm_tn("ffn1_in_bwd_w_" + tag, dgu1, h1, gdt, tm=FF_HALF, n=width, col_off=off, comm=comm)
        g, couts = (out, None) if comm is None else out
        return g.reshape(N_DEV, FF_SHARD, width), couts

    g_a, _ = part("a", 384, 0)
    g_b, sib = part("b", 384, 1, _pair_comm([g_a]))
    sum_a = _add_pairs("pairsum_ffn1_w_in_a", g_a, sib[0])
    g_c, (recv_a, sib_b) = part("c", 256, 3, _join_comm(_chip_comm([sum_a], [True]), _pair_comm([g_b])))
    sum_b = _add_pairs("pairsum_ffn1_w_in_b", g_b, sib_b)
    dh1, (recv_b, sib_c) = _mm_nn("ffn1_in_bwd_h", dgu1, w1_in, tk=FF_HALF,
                                  comm=_join_comm(_chip_comm([sum_b], [True]), _pair_comm([g_c])))
    sum_c = _add_pairs("pairsum_ffn1_w_in_c", g_c, sib_c)
    (dx0, _, small["ffn1_norm"]), (recv_c,) = _rms_bwd("rms1_bwd", x, rep["ffn1_norm"], dh1, dx1, 1.0,
                                                        comm=_chip_comm([sum_c], [True]))
    recv["ffn1_w_in"] = jnp.concatenate([recv_a, recv_b, recv_c], axis=2)
    return dx0, recv, _pack_small(small, loss[:, 0:1])


_SMALL = [("ffn1_norm", 1024), ("mix_norm", 1024), ("ssm_conv_b", 4096), ("ssm_dt_bias", 32), ("ssm_A_log", 32),
          ("ssm_D", 32), ("ssm_norm", 2048), ("ffn2_norm", 1024), ("final_norm", 1024)]
SMALL_W = 10368


def _pack_small(d, loss=None):
    parts = [d[n].reshape(1, -1).astype(F32) for n, _ in _SMALL]
    used = sum(sz for _, sz in _SMALL)
    tail = jnp.zeros((1, SMALL_W - used), F32)
    if loss is not None:
        tail = tail.at[:, 0:1].set(loss)
    return jnp.concatenate(parts + [tail], axis=1)


def _adamw_small(parts, w, m, v):
    n_par = len(_SMALL)
    bc1 = 1.0 - ADAM_B1 ** ADAM_STEP
    bc2 = 1.0 - ADAM_B2 ** ADAM_STEP
    used = sum(sz for _, sz in _SMALL)

    def body(*refs):
        p_ref = refs[0]
        ins = refs[1:1 + 3 * n_par]
        outs = refs[1 + 3 * n_par:]
        g_all = p_ref[0]
        for k in range(1, N_DEV):
            g_all = g_all + p_ref[k]
        off = 0
        for i, (_, sz) in enumerate(_SMALL):
            g = g_all[:, off:off + sz]
            w_ref, m_ref, v_ref = ins[3 * i:3 * i + 3]
            nm = ADAM_B1 * m_ref[...] + (1.0 - ADAM_B1) * g
            nv = ADAM_B2 * v_ref[...] + (1.0 - ADAM_B2) * (g * g)
            outs[4 * i][...] = g
            outs[4 * i + 1][...] = -ADAM_LR * ((nm / bc1) / (jnp.sqrt(nv / bc2) + ADAM_EPS) + ADAM_WD * w_ref[...])
            outs[4 * i + 2][...] = nm
            outs[4 * i + 3][...] = nv
            off += sz
        outs[4 * n_par][...] = g_all[:, used:SMALL_W]

    args = [parts]
    out_shape = []
    for name, sz in _SMALL:
        args += [w[name], m[name], v[name]]
        out_shape += [jax.ShapeDtypeStruct((1, sz), F32)] * 4
    out_shape.append(jax.ShapeDtypeStruct((1, SMALL_W - used), F32))
    res = pl.pallas_call(body, name="adamw_small", out_shape=out_shape,
                         compiler_params=pltpu.CompilerParams(vmem_limit_bytes=VMEM_LIMIT_V7X))(*args)
    return {name: tuple(res[4 * i:4 * i + 4]) for i, (name, _) in enumerate(_SMALL)}, res[-1]


_SHARDED = ["ffn1_w_in", "ffn1_w_out", "w_in", "short_conv_w", "short_w_out", "ssm_conv_w", "ssm_w_out", "w_out",
            "ffn2_w_in", "ffn2_w_out"]
_TRANSPOSED = ("ffn1_w_in", "w_in", "ffn2_w_in")
_ORDER = ["ffn1_norm", "ffn1_w_in", "ffn1_w_out", "mix_norm", "w_in", "short_conv_w", "short_w_out", "ssm_conv_w",
          "ssm_conv_b", "ssm_dt_bias", "ssm_A_log", "ssm_D", "ssm_norm", "ssm_w_out", "w_out", "ffn2_norm",
          "ffn2_w_in", "ffn2_w_out", "final_norm"]


def kernel(x, ffn1_norm, ffn1_w_in, ffn1_w_out, mix_norm, w_in, short_conv_w, short_w_out, ssm_conv_w, ssm_conv_b, ssm_dt_bias, ssm_A_log, ssm_D, ssm_norm, ssm_w_out, w_out, ffn2_norm, ffn2_w_in, ffn2_w_out, final_norm, loss_target, m_ffn1_norm, m_ffn1_w_in, m_ffn1_w_out, m_mix_norm, m_w_in, m_short_conv_w, m_short_w_out, m_ssm_conv_w, m_ssm_conv_b, m_ssm_dt_bias, m_ssm_A_log, m_ssm_D, m_ssm_norm, m_ssm_w_out, m_w_out, m_ffn2_norm, m_ffn2_w_in, m_ffn2_w_out, m_final_norm, v_ffn1_norm, v_ffn1_w_in, v_ffn1_w_out, v_mix_norm, v_w_in, v_short_conv_w, v_short_w_out, v_ssm_conv_w, v_ssm_conv_b, v_ssm_dt_bias, v_ssm_A_log, v_ssm_D, v_ssm_norm, v_ssm_w_out, v_w_out, v_ffn2_norm, v_ffn2_w_in, v_ffn2_w_out, v_final_norm):
    w = dict(ffn1_norm=ffn1_norm, ffn1_w_in=ffn1_w_in, ffn1_w_out=ffn1_w_out, mix_norm=mix_norm, w_in=w_in,
             short_conv_w=short_conv_w, short_w_out=short_w_out, ssm_conv_w=ssm_conv_w, ssm_conv_b=ssm_conv_b,
             ssm_dt_bias=ssm_dt_bias, ssm_A_log=ssm_A_log, ssm_D=ssm_D, ssm_norm=ssm_norm, ssm_w_out=ssm_w_out,
             w_out=w_out, ffn2_norm=ffn2_norm, ffn2_w_in=ffn2_w_in, ffn2_w_out=ffn2_w_out, final_norm=final_norm)
    m = dict(ffn1_norm=m_ffn1_norm, ffn1_w_in=m_ffn1_w_in, ffn1_w_out=m_ffn1_w_out, mix_norm=m_mix_norm, w_in=m_w_in,
             short_conv_w=m_short_conv_w, short_w_out=m_short_w_out, ssm_conv_w=m_ssm_conv_w,
             ssm_conv_b=m_ssm_conv_b, ssm_dt_bias=m_ssm_dt_bias, ssm_A_log=m_ssm_A_log, ssm_D=m_ssm_D,
             ssm_norm=m_ssm_norm, ssm_w_out=m_ssm_w_out, w_out=m_w_out, ffn2_norm=m_ffn2_norm,
             ffn2_w_in=m_ffn2_w_in, ffn2_w_out=m_ffn2_w_out, final_norm=m_final_norm)
    v = dict(ffn1_norm=v_ffn1_norm, ffn1_w_in=v_ffn1_w_in, ffn1_w_out=v_ffn1_w_out, mix_norm=v_mix_norm, w_in=v_w_in,
             short_conv_w=v_short_conv_w, short_w_out=v_short_w_out, ssm_conv_w=v_ssm_conv_w,
             ssm_conv_b=v_ssm_conv_b, ssm_dt_bias=v_ssm_dt_bias, ssm_A_log=v_ssm_A_log, ssm_D=v_ssm_D,
             ssm_norm=v_ssm_norm, ssm_w_out=v_ssm_w_out, w_out=v_w_out, ffn2_norm=v_ffn2_norm,
             ffn2_w_in=v_ffn2_w_in, ffn2_w_out=v_ffn2_w_out, final_norm=v_final_norm)
    shapes = {n: w[n].shape for n in _ORDER}

    def local(d, n):
        return d[n][0].T if n in _TRANSPOSED else d[n][0]

    shard = {n: local(w, n) for n in _SHARDED}

    wire = {n: (shard[n] if n in ("short_conv_w", "ssm_conv_w") else shard[n].astype(BF)) for n in _SHARDED}
    rep = {
        "ffn1_norm": ffn1_norm, "mix_norm": mix_norm, "ffn2_norm": ffn2_norm, "ssm_norm": ssm_norm,
        "ssm_conv_b": ssm_conv_b, "final_norm": final_norm.reshape(1, D_MODEL),
        "dt_bias_pad": _pad_lanes(ssm_dt_bias, DT_W), "a_log_pad": _pad_lanes(ssm_A_log, DT_W),
        "d_exp": jnp.repeat(ssm_D, HEAD_DIM, axis=1),
    }
    grad_x, parts, packed = _train_step(x[0], loss_target[0], wire, rep)

    out_g, out_d, out_m, out_v = {}, {}, {}, {}
    for n in _SHARDED:
        if n == "ssm_w_out":
            res, (small_parts,) = _adamw("adamw_" + n, parts[n], shard[n], local(m, n), local(v, n),
                                         comm=_gather_comm([packed]))
        else:
            res = _adamw("adamw_" + n, parts[n], shard[n], local(m, n), local(v, n))
        out_g[n], out_d[n], out_m[n], out_v[n] = [(r.T if n in _TRANSPOSED else r).reshape(shapes[n]) for r in res]
    row = lambda d: {n: d[n].reshape(1, -1) for n, _ in _SMALL}
    sres, loss_row = _adamw_small(small_parts, row(w), row(m), row(v))
    for n, _ in _SMALL:
        out_g[n], out_d[n], out_m[n], out_v[n] = [r.reshape(shapes[n]) for r in sres[n]]
    loss = loss_row[0, 0]
    return (loss, grad_x[None], *[out_g[n] for n in _ORDER], *[out_d[n] for n in _ORDER],
            *[out_m[n] for n in _ORDER], *[out_v[n] for n in _ORDER])
```

```python
import functools

import jax
import jax.numpy as jnp
from jax import lax
from jax.experimental import pallas as pl
from jax.experimental.pallas import tpu as pltpu

F32 = jnp.float32
BF = jnp.bfloat16

N_DEV = 8
D_MODEL = 1024
D_FF = 2816
D_INNER = 2048
D_XBC = 4096
N_HEADS = 32
HEAD_DIM = 64
N_GROUPS = 8
D_STATE = 128
CHUNK = 64
GROUP_W = D_INNER // N_GROUPS
NORM_EPS = 1e-5
N_IN = 11296
FF_SHARD = 2 * D_FF // N_DEV
FF_HALF = D_FF // 2
IN_SHARD = N_IN // N_DEV

OFF_B, OFF_C, OFF_XA, OFF_Z, OFF_XBC = 0, 1024, 2048, 3072, 5120
N_MAIN = 9216
OFF_DT = 2048
DT_W = 128
N_GD = 2048 + DT_W
W_GRAD_ROW_CUTS = [(0, 512), (512, 720), (720, 896), (896, 1152), (1152, 1412)]

ADAM_LR, ADAM_B1, ADAM_B2, ADAM_EPS, ADAM_WD, ADAM_STEP = 0.001, 0.9, 0.999, 1e-08, 0.01, 10

VMEM_LIMIT_V7X = 56 * 1024 * 1024
TM = 1024
TN_MAX_TOKENS = 2048
TE = 1024
ADAM_COL_TILE = 256
GATHER_PIECES = 4
GATHER_PIECE_MIN_ROWS = 512


def _params(*sem):
    return pltpu.CompilerParams(dimension_semantics=sem, vmem_limit_bytes=VMEM_LIMIT_V7X)


_DIMS = {
    "nn": (((1,), (0,)), ((), ())),
    "nt": (((1,), (1,)), ((), ())),
    "tn": (((0,), (0,)), ((), ())),
}


def _dot(a, b, mode="nn"):
    return lax.dot_general(a, b, _DIMS[mode], preferred_element_type=F32)


def _sigmoid(x):
    return 1.0 / (1.0 + jnp.exp(-x))


class _Comm:
    def __init__(self, inputs, out_shapes, sems, start, finish):
        self.inputs, self.out_shapes, self.sems, self.start, self.finish = inputs, out_shapes, sems, start, finish


def _pcall(name, body, grid, in_specs, out_specs, out_shape, args, scratch=(), sem=None, comm=None, aliases=None):
    single = not isinstance(out_shape, (list, tuple))
    out_shapes = [out_shape] if single else list(out_shape)
    out_specs = [out_specs] if single else list(out_specs)
    n_in, n_out, n_scr = len(args), len(out_shapes), len(scratch)
    aliases = {} if aliases is None else aliases
    if comm is None:
        res = pl.pallas_call(
            body, name=name, grid=grid, in_specs=list(in_specs), out_specs=out_specs, out_shape=out_shapes,
            scratch_shapes=list(scratch), input_output_aliases=aliases, compiler_params=_params(*sem))(*args)
        return (res[0] if single else res), []
    nci, nco = len(comm.inputs), len(comm.out_shapes)

    def wrapped(*refs):
        a = refs[:n_in]
        ci = refs[n_in:n_in + nci]
        o0 = n_in + nci
        o = refs[o0:o0 + n_out]
        co = refs[o0 + n_out:o0 + n_out + nco]
        s0 = o0 + n_out + nco
        s = refs[s0:s0 + n_scr]
        cs = refs[s0 + n_scr:]
        pids = [pl.program_id(i) for i in range(len(grid))]
        first = functools.reduce(jnp.logical_and, [p == 0 for p in pids])
        last = functools.reduce(jnp.logical_and, [p == g - 1 for p, g in zip(pids, grid)])

        @pl.when(first)
        def _():
            comm.start(ci, co, cs)

        body(*a, *o, *s)

        @pl.when(last)
        def _():
            comm.finish(ci, co, cs)

    any_spec = pl.BlockSpec(memory_space=pl.ANY)
    res = pl.pallas_call(
        wrapped, name=name, grid=grid, in_specs=list(in_specs) + [any_spec] * nci,
        out_specs=out_specs + [any_spec] * nco, out_shape=out_shapes + list(comm.out_shapes),
        scratch_shapes=list(scratch) + list(comm.sems), input_output_aliases=aliases,
        compiler_params=_params(*(("arbitrary",) * len(grid))))(*args, *comm.inputs)
    core = res[:n_out]
    return (core[0] if single else core), list(res[n_out:])


def _comm_call(name, comm):
    nci, nco = len(comm.inputs), len(comm.out_shapes)

    def body(*refs):
        ci, co, cs = refs[:nci], refs[nci:nci + nco], refs[nci + nco:]
        comm.start(ci, co, cs)
        comm.finish(ci, co, cs)

    any_spec = pl.BlockSpec(memory_space=pl.ANY)
    return pl.pallas_call(
        body, name=name, in_specs=[any_spec] * nci, out_specs=[any_spec] * nco, out_shape=list(comm.out_shapes),
        scratch_shapes=list(comm.sems), compiler_params=pltpu.CompilerParams(has_side_effects=True))(*comm.inputs)


def _remote(src, dst, ssem, rsem, dev):
    return pltpu.make_async_remote_copy(src_ref=src, dst_ref=dst, send_sem=ssem, recv_sem=rsem, device_id=dev,
                                        device_id_type=pl.DeviceIdType.MESH)


def _place():
    x, y, c = lax.axis_index("x"), lax.axis_index("y"), lax.axis_index("c")
    other_chips = [(1 - x, y), (x, 1 - y), (1 - x, 1 - y)]
    return x, y, c, other_chips


def _slot(x, y, c, swap):
    return 4 * y + 2 * x + c if swap else 4 * x + 2 * y + c


def _chip_slot(x, y, swap):
    return 2 * y + x if swap else 2 * x + y


def _gather_comm(shards, swaps=None):
    n = len(shards)
    per = N_DEV - 1
    swaps = [False] * n if swaps is None else swaps
    pieces = []
    for i, a in enumerate(shards):
        rows = a.shape[0]
        k = GATHER_PIECES if (a.ndim == 2 and rows >= GATHER_PIECE_MIN_ROWS) else 1
        step = -(-rows // (k * 8)) * 8
        if k == 1:
            pieces.append((i, 0, None))
        else:
            pieces += [(i, r, min(step, rows - r)) for r in range(0, rows, step)]
    m = len(pieces)

    def src(ins, v):
        i, r, cnt = pieces[v]
        return ins[i] if cnt is None else ins[i].at[pl.ds(r, cnt)]

    def place(outs, v, x, y, c):
        i, r, cnt = pieces[v]
        blk = outs[i].at[_slot(x, y, c, swaps[i])]
        return blk if cnt is None else blk.at[pl.ds(r, cnt)]

    def start(ins, outs, sems):
        send, recv, loc = sems
        x, y, c, chips = _place()
        for v in range(m):
            me = place(outs, v, x, y, c)
            pltpu.make_async_copy(src(ins, v), me, loc.at[v]).start()
            _remote(src(ins, v), me, send.at[per * v], recv.at[per * v], (x, y, 1 - c)).start()
        for j, (qx, qy) in enumerate(chips):
            for v in range(m):
                _remote(src(ins, v), place(outs, v, x, y, c), send.at[per * v + 1 + j], recv.at[per * v + 1 + j],
                        (qx, qy, c)).start()

    def finish(ins, outs, sems):
        send, recv, loc = sems
        x, y, c, chips = _place()
        sib = (x, y, 1 - c)
        for v in range(m):
            for j, (qx, qy) in enumerate(chips):
                blk = place(outs, v, qx, qy, c)
                _remote(blk, blk, send.at[per * v + 1 + j], recv.at[per * v + 1 + j], (qx, qy, c)).wait_recv()
                _remote(blk, blk, send.at[per * v + 4 + j], recv.at[per * v + 4 + j], sib).start()
        for v in range(m):
            blk = place(outs, v, x, y, 1 - c)
            _remote(blk, blk, send.at[per * v], recv.at[per * v], sib).wait_recv()
            for j, (qx, qy) in enumerate(chips):
                blk = place(outs, v, qx, qy, 1 - c)
                _remote(blk, blk, send.at[per * v + 4 + j], recv.at[per * v + 4 + j], sib).wait_recv()
        for v in range(m):
            own = place(outs, v, x, y, c)
            for k in range(per):
                _remote(src(ins, v), own, send.at[per * v + k], recv.at[per * v + k], sib).wait_send()
            pltpu.make_async_copy(src(ins, v), own, loc.at[v]).wait()

    out_shapes = [jax.ShapeDtypeStruct((N_DEV,) + tuple(a.shape), a.dtype) for a in shards]
    sems = [pltpu.SemaphoreType.DMA((per * m,)), pltpu.SemaphoreType.DMA((per * m,)), pltpu.SemaphoreType.DMA((m,))]
    return _Comm(list(shards), out_shapes, sems, start, finish)


def _pair_comm(slots):
    n = len(slots)

    def copies(ins, outs, sems):
        send, recv = sems
        x, y, c, _ = _place()
        sib = (x, y, 1 - c)
        out = []
        for i in range(n):
            for q in range(4):
                out.append(_remote(ins[i].at[2 * q + 1 - c], outs[i].at[q], send.at[4 * i + q], recv.at[4 * i + q], sib))
        return out

    def start(ins, outs, sems):
        for cp in copies(ins, outs, sems):
            cp.start()

    def finish(ins, outs, sems):
        for cp in copies(ins, outs, sems):
            cp.wait_send()
            cp.wait_recv()

    out_shapes = [jax.ShapeDtypeStruct((4,) + tuple(a.shape[1:]), a.dtype) for a in slots]
    sems = [pltpu.SemaphoreType.DMA((4 * n,)), pltpu.SemaphoreType.DMA((4 * n,))]
    return _Comm(list(slots), out_shapes, sems, start, finish)


def _chip_comm(chip_sums, swaps=None, rows=None):
    n = len(chip_sums)
    swaps = [False] * n if swaps is None else swaps
    rows = [None] * n if rows is None else rows

    def src(ins, i, q):
        return ins[i].at[q] if rows[i] is None else ins[i].at[q, pl.ds(rows[i][0], rows[i][1] - rows[i][0])]

    def start(ins, outs, sems):
        send, recv, loc = sems
        x, y, c, chips = _place()
        for i in range(n):
            mine = _chip_slot(x, y, swaps[i])
            pltpu.make_async_copy(src(ins, i, mine), outs[i].at[mine], loc.at[i]).start()
            for j, (qx, qy) in enumerate(chips):
                _remote(src(ins, i, _chip_slot(qx, qy, swaps[i])), outs[i].at[mine], send.at[3 * i + j],
                        recv.at[3 * i + j], (qx, qy, c)).start()

    def finish(ins, outs, sems):
        send, recv, loc = sems
        x, y, c, chips = _place()
        for i in range(n):
            mine = _chip_slot(x, y, swaps[i])
            for j, (qx, qy) in enumerate(chips):
                theirs = _chip_slot(qx, qy, swaps[i])
                cp = _remote(src(ins, i, theirs), outs[i].at[theirs], send.at[3 * i + j], recv.at[3 * i + j], (qx, qy, c))
                cp.wait_send()
                cp.wait_recv()
            pltpu.make_async_copy(src(ins, i, mine), outs[i].at[mine], loc.at[i]).wait()

    def out_shape(a, r):
        shape = a.shape if r is None else (a.shape[0], r[1] - r[0]) + tuple(a.shape[2:])
        return jax.ShapeDtypeStruct(shape, a.dtype)

    out_shapes = [out_shape(a, r) for a, r in zip(chip_sums, rows)]
    sems = [pltpu.SemaphoreType.DMA((3 * n,)), pltpu.SemaphoreType.DMA((3 * n,)), pltpu.SemaphoreType.DMA((n,))]
    return _Comm(list(chip_sums), out_shapes, sems, start, finish)


def _join_comm(a, b):
    na_i, na_o, na_s = len(a.inputs), len(a.out_shapes), len(a.sems)

    def start(ins, outs, sems):
        a.start(ins[:na_i], outs[:na_o], sems[:na_s])
        b.start(ins[na_i:], outs[na_o:], sems[na_s:])

    def finish(ins, outs, sems):
        a.finish(ins[:na_i], outs[:na_o], sems[:na_s])
        b.finish(ins[na_i:], outs[na_o:], sems[na_s:])

    return _Comm(a.inputs + b.inputs, a.out_shapes + b.out_shapes, a.sems + b.sems, start, finish)


def _row_tile(r):
    for cand in (256, 128):
        if r > cand and r % cand == 0:
            return cand
    return r


def _add_pairs(name, slots, sib):
    r, c = slots.shape[1:]
    tr = _row_tile(r)

    def body(core_ref, s_ref, b_ref, o_ref):
        o_ref[...] = (s_ref[...].astype(F32) + b_ref[...].astype(F32)).astype(o_ref.dtype)

    core = jnp.full((1,), lax.axis_index("c"), jnp.int32)
    return pl.pallas_call(
        body, name=name,
        grid_spec=pltpu.PrefetchScalarGridSpec(
            num_scalar_prefetch=1, grid=(4, r // tr),
            in_specs=[pl.BlockSpec((None, tr, c), lambda q, i, core_ref: (2 * q + core_ref[0], i, 0)),
                      pl.BlockSpec((None, tr, c), lambda q, i, core_ref: (q, i, 0))],
            out_specs=pl.BlockSpec((None, tr, c), lambda q, i, core_ref: (q, i, 0))),
        out_shape=jax.ShapeDtypeStruct((4, r, c), slots.dtype),
        compiler_params=_params("parallel", "parallel"))(core, slots, sib)


def _matmul(name, mode, a, b, grid, a_spec, b_spec, o_spec, out_shape, acc_shape,
            res=None, res_spec=None, alpha=1.0, comm=None):
    nk = grid[-1]
    has_res = res is not None

    def body(*refs):
        if has_res:
            a_ref, b_ref, r_ref, o_ref = refs[:4]
        else:
            a_ref, b_ref, o_ref = refs[:3]
            r_ref = None
        part = _dot(a_ref[...], b_ref[...], mode)

        def finish(v):
            if alpha != 1.0:
                v = v * alpha
            if has_res:
                v = r_ref[...] + v
            o_ref[...] = v.astype(o_ref.dtype)

        if nk == 1:
            finish(part)
        else:
            acc = refs[-1]
            k = pl.program_id(len(grid) - 1)

            @pl.when(k == 0)
            def _():
                acc[...] = part

            @pl.when(k > 0)
            def _():
                acc[...] += part

            @pl.when(k == nk - 1)
            def _():
                finish(acc[...])

    in_specs = [a_spec, b_spec] + ([res_spec] if has_res else [])
    args = (a, b) + ((res,) if has_res else ())
    scratch = [] if nk == 1 else [pltpu.VMEM(acc_shape, F32)]
    sem = ("parallel",) * (len(grid) - 1) + ("arbitrary",)
    out, couts = _pcall(name, body, grid, in_specs, o_spec, out_shape, args, scratch, sem, comm)
    return out if comm is None else (out, couts)


def _mm_nn(name, a, b, out_dtype=F32, res=None, alpha=1.0, tk=None, kk=None, a_off=0, b_off=0, comm=None):
    t = a.shape[0]
    kk = a.shape[1] if kk is None else kk
    n = b.shape[1]
    tk = kk if tk is None else tk
    grid = (t // TM, 1, kk // tk)
    return _matmul(
        name, "nn", a, b, grid,
        pl.BlockSpec((TM, tk), lambda i, j, k: (i, k + a_off)),
        pl.BlockSpec((tk, n), lambda i, j, k: (k + b_off, 0)),
        pl.BlockSpec((TM, n), lambda i, j, k: (i, 0)),
        jax.ShapeDtypeStruct((t, n), out_dtype), (TM, n),
        res=res, res_spec=pl.BlockSpec((TM, n), lambda i, j, k: (i, 0)), alpha=alpha, comm=comm)


def _mm_nt(name, a, b, n=None, tn=None, tk=None, out_dtype=F32, comm=None):
    t, kk = a.shape
    n = b.shape[0] if n is None else n
    tn = n if tn is None else tn
    tk = kk if tk is None else tk
    grid = (n // tn, t // TM, kk // tk)
    return _matmul(
        name, "nt", a, b, grid,
        pl.BlockSpec((TM, tk), lambda j, i, k: (i, k)),
        pl.BlockSpec((tn, tk), lambda j, i, k: (j, k)),
        pl.BlockSpec((TM, tn), lambda j, i, k: (i, j)),
        jax.ShapeDtypeStruct((t, n), out_dtype), (TM, tn), comm=comm)


def _mm_tn(name, a, b, out_dtype, tm=None, n=None, col_off=0, comm=None):
    t, m = a.shape
    n = b.shape[1] if n is None else n
    tm = m if tm is None else tm
    tk = t if t <= TN_MAX_TOKENS else TM
    grid = (m // tm, 1, t // tk)
    return _matmul(
        name, "tn", a, b, grid,
        pl.BlockSpec((tk, tm), lambda j, i, k: (k, j)),
        pl.BlockSpec((tk, n), lambda j, i, k: (k, col_off)),
        pl.BlockSpec((tm, n), lambda j, i, k: (j, 0)),
        jax.ShapeDtypeStruct((m, n), out_dtype), (tm, n), comm=comm)


def _rms_fwd(name, x, w):
    t, d = x.shape

    def body(x_ref, w_ref, h_ref):
        xv = x_ref[...]
        rstd = lax.rsqrt(jnp.mean(xv * xv, axis=-1, keepdims=True) + NORM_EPS)
        h_ref[...] = (xv * rstd * w_ref[...]).astype(h_ref.dtype)

    return pl.pallas_call(
        body, name=name, grid=(t // TE,),
        in_specs=[pl.BlockSpec((TE, d), lambda i: (i, 0)), pl.BlockSpec((1, d), lambda i: (0, 0))],
        out_specs=pl.BlockSpec((TE, d), lambda i: (i, 0)),
        out_shape=jax.ShapeDtypeStruct((t, d), BF), compiler_params=_params("parallel"))(x, w)


def _rms_bwd(name, x, w, dh, dres, out_scale, comm=None):
    t, d = x.shape

    def body(x_ref, w_ref, dh_ref, dres_ref, dx_ref, dxb_ref, dw_ref):
        i = pl.program_id(0)
        xv = x_ref[...]
        rstd = lax.rsqrt(jnp.mean(xv * xv, axis=-1, keepdims=True) + NORM_EPS)
        xhat = xv * rstd
        dhv = dh_ref[...]
        wd = dhv * w_ref[...]
        proj = jnp.mean(wd * xhat, axis=-1, keepdims=True)
        dx = dres_ref[...] + rstd * (wd - xhat * proj)
        dx_ref[...] = dx
        dxb_ref[...] = (dx * out_scale).astype(BF)
        part = jnp.sum(dhv * xhat, axis=0, keepdims=True)

        @pl.when(i == 0)
        def _():
            dw_ref[...] = part

        @pl.when(i > 0)
        def _():
            dw_ref[...] += part

    row = pl.BlockSpec((TE, d), lambda i: (i, 0))
    vec = pl.BlockSpec((1, d), lambda i: (0, 0))
    outs, couts = _pcall(
        name, body, (t // TE,), [row, vec, row, row], [row, row, vec],
        [jax.ShapeDtypeStruct((t, d), F32), jax.ShapeDtypeStruct((t, d), BF), jax.ShapeDtypeStruct((1, d), F32)],
        (x, w, dh, dres), (), ("arbitrary",), comm)
    return outs if comm is None else (outs, couts)


def _final_loss(x, w, target):
    t, d = x.shape

    def body(x_ref, w_ref, t_ref, loss_ref, dx_ref, dxb_ref, dw_ref):
        i = pl.program_id(0)
        xv = x_ref[...]
        rstd = lax.rsqrt(jnp.mean(xv * xv, axis=-1, keepdims=True) + NORM_EPS)
        xhat = xv * rstd
        err = xhat * w_ref[...] - t_ref[...]
        lpart = 0.5 * jnp.sum(jnp.mean(err * err, axis=-1, keepdims=True), axis=0, keepdims=True)
        dy = err * (1.0 / d)
        wd = dy * w_ref[...]
        proj = jnp.mean(wd * xhat, axis=-1, keepdims=True)
        dx = rstd * (wd - xhat * proj)
        dx_ref[...] = dx
        dxb_ref[...] = (0.5 * dx).astype(BF)
        part = jnp.sum(dy * xhat, axis=0, keepdims=True)
        lfull = jnp.broadcast_to(lpart, (1, 128))

        @pl.when(i == 0)
        def _():
            dw_ref[...] = part
            loss_ref[...] = lfull

        @pl.when(i > 0)
        def _():
            dw_ref[...] += part
            loss_ref[...] += lfull

    row = pl.BlockSpec((TE, d), lambda i: (i, 0))
    vec = pl.BlockSpec((1, d), lambda i: (0, 0))
    return pl.pallas_call(
        body, name="final_loss", grid=(t // TE,), in_specs=[row, vec, row],
        out_specs=[pl.BlockSpec((1, 128), lambda i: (0, 0)), row, row, vec],
        out_shape=[jax.ShapeDtypeStruct((1, 128), F32), jax.ShapeDtypeStruct((t, d), F32),
                   jax.ShapeDtypeStruct((t, d), BF), jax.ShapeDtypeStruct((1, d), F32)],
        compiler_params=_params("arbitrary"))(x, w, target)


def _swiglu_fwd(name, gu, comm=None):
    t = gu.shape[0]

    def body(g_ref, u_ref, a_ref):
        g = g_ref[...].astype(F32)
        a_ref[...] = (g * _sigmoid(g) * u_ref[...].astype(F32)).astype(BF)

    blk = (TE, FF_HALF)
    out, couts = _pcall(
        name, body, (t // TE, 2),
        [pl.BlockSpec(blk, lambda i, j: (i, 2 * j)), pl.BlockSpec(blk, lambda i, j: (i, 2 * j + 1))],
        pl.BlockSpec(blk, lambda i, j: (i, j)), jax.ShapeDtypeStruct((t, D_FF), BF),
        (gu, gu), (), ("parallel", "parallel"), comm)
    return out if comm is None else (out, couts)


def _swiglu_bwd(name, gu, dact, comm=None):
    t = gu.shape[0]

    def body(g_ref, u_ref, da_ref, o_ref):
        g = g_ref[...].astype(F32)
        da = da_ref[...].astype(F32)
        s = _sigmoid(g)
        o_ref[:, 0:FF_HALF] = (da * u_ref[...].astype(F32) * (s * (1.0 + g * (1.0 - s)))).astype(BF)
        o_ref[:, FF_HALF:2 * FF_HALF] = (da * g * s).astype(BF)

    blk = (TE, FF_HALF)
    out, couts = _pcall(
        name, body, (t // TE, 2),
        [pl.BlockSpec(blk, lambda i, j: (i, 2 * j)), pl.BlockSpec(blk, lambda i, j: (i, 2 * j + 1)),
         pl.BlockSpec(blk, lambda i, j: (i, j))],
        pl.BlockSpec((TE, 2 * FF_HALF), lambda i, j: (i, j)),
        jax.ShapeDtypeStruct((t, 2 * D_FF), BF), (gu, gu, dact), (), ("parallel", "parallel"), comm)
    return out if comm is None else (out, couts)


CONV_CB = 256


CONV_ROWS = 64
CONV_HALO = 16


def _taps_down(ext, w, k):
    shifted = [pltpu.roll(ext, k - 1 - j, 0)[CONV_HALO:] for j in range(k - 1)] + [ext[CONV_HALO:]]
    out = shifted[k - 1] * w[k - 1:k, :]
    for j in range(k - 1):
        out = out + shifted[j] * w[j:j + 1, :]
    return out, shifted


def _taps_up(ext, w, k):
    rows = ext.shape[0]
    n = rows - CONV_HALO
    out = ext[:n] * w[k - 1:k, :]
    for j in range(k - 1):
        out = out + pltpu.roll(ext, rows - (k - 1 - j), 0)[:n] * w[j:j + 1, :]
    return out


def _rows_before(ref, i, r0):
    start = pl.multiple_of(jnp.maximum(r0 - CONV_HALO, 0), CONV_HALO)
    return jnp.where(i > 0, ref[pl.ds(start, CONV_HALO), :].astype(F32), 0.0)


def _rows_after(ref, r0, t):
    start = pl.multiple_of(jnp.minimum(r0 + CONV_ROWS, t - CONV_HALO), CONV_HALO)
    return ref[pl.ds(start, CONV_HALO), :].astype(F32)


def _fold8(v):
    return v.reshape(v.shape[0] // 8, 8, v.shape[1]).sum(axis=0)


def _silu_grad(pre):
    s = _sigmoid(pre)
    return s * (1.0 + pre * (1.0 - s))


def _pspec(t, off):
    base = off // CONV_CB
    return pl.BlockSpec((t, CONV_CB), lambda j: (0, base + j))


def _mix_a_fwd(p, conv_w):
    t = p.shape[0]

    def body(b_ref, c_ref, xa_ref, w_ref, o_ref):
        w = w_ref[...]

        def step(i, carry):
            r0 = pl.multiple_of(i * CONV_ROWS, CONV_ROWS)
            rows = pl.ds(r0, CONV_ROWS)
            q = c_ref[rows, :].astype(F32) * xa_ref[rows, :].astype(F32)
            q_before = _rows_before(c_ref, i, r0) * _rows_before(xa_ref, i, r0)
            va, _ = _taps_down(jnp.concatenate([q_before, q], axis=0), w, 3)
            o_ref[rows, :] = (b_ref[rows, :].astype(F32) * va).astype(BF)
            return carry

        lax.fori_loop(0, t // CONV_ROWS, step, 0)

    return pl.pallas_call(
        body, name="mix_a_fwd", grid=(D_MODEL // CONV_CB,),
        in_specs=[_pspec(t, OFF_B), _pspec(t, OFF_C), _pspec(t, OFF_XA),
                  pl.BlockSpec((3, CONV_CB), lambda j: (0, j))],
        out_specs=pl.BlockSpec((t, CONV_CB), lambda j: (0, j)),
        out_shape=jax.ShapeDtypeStruct((t, D_MODEL), BF), compiler_params=_params("parallel"))(p, p, p, conv_w)


def _mix_a_bwd(p, conv_w, dya, dp):
    t = p.shape[0]

    def body(b_ref, c_ref, xa_ref, w_ref, dy_ref, dp_in, dp_ref, dw_ref):
        del dp_in
        w = w_ref[...]
        n = t // CONV_ROWS

        def step(i, acc):
            r0 = pl.multiple_of(i * CONV_ROWS, CONV_ROWS)
            rows = pl.ds(r0, CONV_ROWS)
            cv = c_ref[rows, :].astype(F32)
            xav = xa_ref[rows, :].astype(F32)
            q_before = _rows_before(c_ref, i, r0) * _rows_before(xa_ref, i, r0)
            va, shifted = _taps_down(jnp.concatenate([q_before, cv * xav], axis=0), w, 3)
            dyv = dy_ref[rows, :]
            dp_ref[rows, 0:CONV_CB] = (dyv * va).astype(BF)
            dv = dyv * b_ref[rows, :].astype(F32)
            dv_after = jnp.where(i < n - 1, _rows_after(dy_ref, r0, t) * _rows_after(b_ref, r0, t), 0.0)
            dq = _taps_up(jnp.concatenate([dv, dv_after], axis=0), w, 3)
            dp_ref[rows, CONV_CB:2 * CONV_CB] = (dq * xav).astype(BF)
            dp_ref[rows, 2 * CONV_CB:3 * CONV_CB] = (dq * cv).astype(BF)
            return tuple(a + _fold8(dv * s) for a, s in zip(acc, shifted))

        zero = jnp.zeros((8, CONV_CB), F32)
        acc = lax.fori_loop(0, n, step, (zero, zero, zero))
        for j in range(3):
            dw_ref[j:j + 1, :] = jnp.sum(acc[j], axis=0, keepdims=True)

    col = pl.BlockSpec((t, CONV_CB), lambda j: (0, j))
    wsp = pl.BlockSpec((3, CONV_CB), lambda j: (0, j))
    return pl.pallas_call(
        body, name="mix_a_bwd", grid=(D_MODEL // CONV_CB,),
        in_specs=[_pspec(t, OFF_B), _pspec(t, OFF_C), _pspec(t, OFF_XA), wsp, col, pl.BlockSpec(memory_space=pl.ANY)],
        out_specs=[pl.BlockSpec((t, 3 * CONV_CB), lambda j: (0, j)), wsp],
        out_shape=[jax.ShapeDtypeStruct(dp.shape, dp.dtype), jax.ShapeDtypeStruct((3, D_MODEL), F32)],
        input_output_aliases={5: 0},
        compiler_params=_params("parallel"))(p, p, p, conv_w, dya, dp)


def _ssm_conv_fwd(p, conv_w, conv_b, comm=None):
    t = p.shape[0]

    def body(x_ref, w_ref, b_ref, o_ref):
        w = w_ref[...]
        bias = b_ref[...]

        def step(i, carry):
            r0 = pl.multiple_of(i * CONV_ROWS, CONV_ROWS)
            rows = pl.ds(r0, CONV_ROWS)
            ext = jnp.concatenate([_rows_before(x_ref, i, r0), x_ref[rows, :].astype(F32)], axis=0)
            pre = _taps_down(ext, w, 4)[0] + bias
            o_ref[rows, :] = pre * _sigmoid(pre)
            return carry

        lax.fori_loop(0, t // CONV_ROWS, step, 0)

    out, couts = _pcall(
        "ssm_conv_fwd", body, (D_XBC // CONV_CB,),
        [_pspec(t, OFF_XBC), pl.BlockSpec((4, CONV_CB), lambda j: (0, j)), pl.BlockSpec((1, CONV_CB), lambda j: (0, j))],
        pl.BlockSpec((t, CONV_CB), lambda j: (0, j)), jax.ShapeDtypeStruct((t, D_XBC), F32),
        (p, conv_w, conv_b), (), ("parallel",), comm)
    return out if comm is None else (out, couts)


def _ssm_conv_bwd(p, conv_w, conv_b, dxc, dp, comm=None):
    t = p.shape[0]

    def body(x_ref, w_ref, b_ref, d_ref, dp_in, dx_ref, dw_ref, db_ref):
        del dp_in
        w = w_ref[...]
        bias = b_ref[...]
        n = t // CONV_ROWS

        def step(i, acc):
            r0 = pl.multiple_of(i * CONV_ROWS, CONV_ROWS)
            rows = pl.ds(r0, CONV_ROWS)
            x_cur = x_ref[rows, :].astype(F32)
            pre, shifted = _taps_down(jnp.concatenate([_rows_before(x_ref, i, r0), x_cur], axis=0), w, 4)
            pre = pre + bias
            dpre = d_ref[rows, :] * _silu_grad(pre)
            ext_after = jnp.concatenate([x_cur[CONV_ROWS - CONV_HALO:], _rows_after(x_ref, r0, t)], axis=0)
            pre_after = _taps_down(ext_after, w, 4)[0] + bias
            dpre_after = jnp.where(i < n - 1, _rows_after(d_ref, r0, t) * _silu_grad(pre_after), 0.0)
            dx_ref[rows, :] = _taps_up(jnp.concatenate([dpre, dpre_after], axis=0), w, 4).astype(BF)
            new = tuple(a + _fold8(dpre * s) for a, s in zip(acc[:4], shifted))
            return new + (acc[4] + _fold8(dpre),)

        zero = jnp.zeros((8, CONV_CB), F32)
        acc = lax.fori_loop(0, n, step, (zero,) * 5)
        for j in range(4):
            dw_ref[j:j + 1, :] = jnp.sum(acc[j], axis=0, keepdims=True)
        db_ref[...] = jnp.sum(acc[4], axis=0, keepdims=True)

    col = pl.BlockSpec((t, CONV_CB), lambda j: (0, j))
    wsp = pl.BlockSpec((4, CONV_CB), lambda j: (0, j))
    bsp = pl.BlockSpec((1, CONV_CB), lambda j: (0, j))
    outs, couts = _pcall(
        "ssm_conv_bwd", body, (D_XBC // CONV_CB,),
        [_pspec(t, OFF_XBC), wsp, bsp, col, pl.BlockSpec(memory_space=pl.ANY)], [_pspec(t, OFF_XBC), wsp, bsp],
        [jax.ShapeDtypeStruct(dp.shape, dp.dtype), jax.ShapeDtypeStruct((4, D_XBC), F32),
         jax.ShapeDtypeStruct((1, D_XBC), F32)],
        (p, conv_w, conv_b, dxc, dp), (), ("parallel",), comm, aliases={4: 0})
    return outs if comm is None else (outs, couts)


DT_ROWS = 512


def _tri(lower):
    r = lax.broadcasted_iota(jnp.int32, (CHUNK, CHUNK), 0)
    c = lax.broadcasted_iota(jnp.int32, (CHUNK, CHUNK), 1)
    return jnp.where((r >= c) if lower else (r <= c), 1.0, 0.0).astype(F32)


def _dot_exact(a, b):
    return lax.dot_general(a, b, _DIMS["nn"], preferred_element_type=F32, precision=lax.Precision.HIGHEST)


def _dt_fwd(p, bias_pad, alog_pad):
    t = p.shape[0]

    def body(raw_ref, b_ref, al_ref, dt_ref, acs_ref):
        z = raw_ref[...] + b_ref[...]
        dt = jnp.maximum(z, 0.0) + jnp.log(1.0 + jnp.exp(-jnp.abs(z)))
        dt_ref[...] = dt
        a = dt * (-jnp.exp(al_ref[...]))
        tri = _tri(True)
        for k in range(DT_ROWS // CHUNK):
            acs_ref[k * CHUNK:(k + 1) * CHUNK, :] = _dot_exact(tri, a[k * CHUNK:(k + 1) * CHUNK, :])

    blk = pl.BlockSpec((DT_ROWS, DT_W), lambda i: (i, 0))
    vec = pl.BlockSpec((1, DT_W), lambda i: (0, 0))
    return pl.pallas_call(
        body, name="dt_fwd", grid=(t // DT_ROWS,),
        in_specs=[pl.BlockSpec((DT_ROWS, DT_W), lambda i: (i, OFF_DT // DT_W)), vec, vec],
        out_specs=[blk, blk], out_shape=[jax.ShapeDtypeStruct((t, DT_W), F32)] * 2,
        compiler_params=_params("parallel"))(p, bias_pad, alog_pad)


def _dt_bwd(p, bias_pad, alog_pad, dt, ddt, dacs, dp_gd):
    t = p.shape[0]

    def body(raw_ref, b_ref, al_ref, dt_ref, ddt_ref, dacs_ref, dp_in, draw_ref, db_ref, dal_ref):
        del dp_in
        i = pl.program_id(0)
        acoef = -jnp.exp(al_ref[...])
        triu = _tri(False)
        das = []
        for k in range(DT_ROWS // CHUNK):
            das.append(_dot_exact(triu, dacs_ref[k * CHUNK:(k + 1) * CHUNK, :]))
        da = jnp.concatenate(das, axis=0)
        dtv = dt_ref[...]
        ddt_tot = ddt_ref[...] + da * acoef
        lane = lax.broadcasted_iota(jnp.int32, (DT_ROWS, DT_W), 1)
        draw = jnp.where(lane < N_HEADS, ddt_tot * _sigmoid(raw_ref[...] + b_ref[...]), 0.0)
        draw_ref[...] = draw.astype(BF)
        pb = jnp.sum(draw, axis=0, keepdims=True)
        pa = jnp.sum(da * dtv * acoef, axis=0, keepdims=True)

        @pl.when(i == 0)
        def _():
            db_ref[...] = pb
            dal_ref[...] = pa

        @pl.when(i > 0)
        def _():
            db_ref[...] += pb
            dal_ref[...] += pa

    blk = pl.BlockSpec((DT_ROWS, DT_W), lambda i: (i, 0))
    vec = pl.BlockSpec((1, DT_W), lambda i: (0, 0))
    return pl.pallas_call(
        body, name="dt_bwd", grid=(t // DT_ROWS,),
        in_specs=[pl.BlockSpec((DT_ROWS, DT_W), lambda i: (i, OFF_DT // DT_W)), vec, vec, blk, blk, blk,
                  pl.BlockSpec(memory_space=pl.ANY)],
        out_specs=[pl.BlockSpec((DT_ROWS, DT_W), lambda i: (i, OFF_DT // DT_W)), vec, vec],
        out_shape=[jax.ShapeDtypeStruct(dp_gd.shape, dp_gd.dtype), jax.ShapeDtypeStruct((1, DT_W), F32),
                   jax.ShapeDtypeStruct((1, DT_W), F32)],
        input_output_aliases={6: 0},
        compiler_params=_params("arbitrary"))(p, bias_pad, alog_pad, dt, ddt, dacs, dp_gd)


def _split_dot(z, onehot, terms):
    out = None
    rest = z
    for _ in range(terms):
        piece = rest.astype(BF)
        part = _dot(piece, onehot)
        out = part if out is None else out + part
        rest = rest - piece.astype(F32)
    return out


def _spread_mat():
    row = lax.broadcasted_iota(jnp.int32, (DT_W, D_INNER), 0)
    lane = lax.broadcasted_iota(jnp.int32, (DT_W, D_INNER), 1)
    return jnp.where(row == lane // HEAD_DIM, 1.0, 0.0).astype(BF)


def _gather_mat():
    row = lax.broadcasted_iota(jnp.int32, (D_INNER, DT_W), 0)
    lane = lax.broadcasted_iota(jnp.int32, (D_INNER, DT_W), 1)
    return jnp.where(lane == row // HEAD_DIM, 1.0, 0.0).astype(BF)


def _ssd_masks():
    row = lax.broadcasted_iota(jnp.int32, (CHUNK, GROUP_W), 0)
    col = lax.broadcasted_iota(jnp.int32, (CHUNK, GROUP_W), 1) % HEAD_DIM
    brow = lax.broadcasted_iota(jnp.int32, (GROUP_W, GROUP_W), 0) // HEAD_DIM
    bcol = lax.broadcasted_iota(jnp.int32, (GROUP_W, GROUP_W), 1) // HEAD_DIM
    return row >= col, row == col, brow == bcol


def _stack4(v):
    return jnp.concatenate([v, v, v, v], axis=0)


def _fold4(v):
    return v[0:CHUNK] + v[CHUNK:2 * CHUNK] + v[2 * CHUNK:3 * CHUNK] + v[3 * CHUNK:4 * CHUNK]


def _ssd_group(xc_ref, wide_ref, g, tri, eye, blockdiag):
    gs = slice(GROUP_W * g, GROUP_W * (g + 1))
    xs_g = xc_ref[:, gs]
    b_g = xc_ref[:, D_INNER + D_STATE * g:D_INNER + D_STATE * (g + 1)].astype(BF)
    c_g = xc_ref[:, D_INNER + 1024 + D_STATE * g:D_INNER + 1024 + D_STATE * (g + 1)].astype(BF)
    acs_e, dt_e = wide_ref[0:CHUNK, gs], wide_ref[CHUNK:2 * CHUNK, gs]
    atot_e = acs_e[CHUNK - 1:CHUNK, :]
    acs_j = jnp.sum(jnp.where(eye, acs_e, 0.0), axis=0, keepdims=True)
    lmat = jnp.where(tri, jnp.exp(jnp.minimum(acs_e - acs_j, 0.0)), 0.0)
    b_t = _stack4(b_g)
    m = _dot(c_g, b_t, "nt") * lmat
    x_g = xs_g * dt_e
    xbd = jnp.where(blockdiag, _stack4(x_g), 0.0).astype(BF)
    return dict(gs=gs, xs=xs_g, b=b_g, c=c_g, b_t=b_t, dt=dt_e, e=jnp.exp(acs_e), dec=jnp.exp(atot_e - acs_e),
                eat=jnp.exp(atot_e), lmat=lmat, m=m, x=x_g, xbd=xbd)


def _ssd_fwd(xconv, dt, acs, d_exp, comm=None):
    t = xconv.shape[0]
    nc = t // CHUNK

    def body(xc_ref, dt_ref, acs_ref, d_ref, y_ref, hs_ref, state, wide):
        c = pl.program_id(0)

        @pl.when(c == 0)
        def _():
            state[...] = jnp.zeros_like(state)

        hs_ref[...] = state[...]
        tri, eye, blockdiag = _ssd_masks()
        wide[...] = _split_dot(jnp.concatenate([acs_ref[...], dt_ref[...]], axis=0), _spread_mat(), 3)
        for g in range(N_GROUPS):
            q = _ssd_group(xc_ref, wide, g, tri, eye, blockdiag)
            gs = q["gs"]
            h_t = state[:, gs]
            ydiag = _dot(q["m"].astype(BF), q["xbd"])
            yoff = _dot(q["c"], h_t.astype(BF)) * q["e"]
            y_ref[:, gs] = ydiag + yoff + d_ref[:, gs] * q["xs"]
            s_t = _dot(q["b"], (q["x"] * q["dec"]).astype(BF), "tn")
            state[:, gs] = q["eat"] * h_t + s_t

    blk = lambda w: pl.BlockSpec((CHUNK, w), lambda c: (c, 0))
    outs, couts = _pcall(
        "ssd_fwd", body, (nc,),
        [blk(D_XBC), blk(DT_W), blk(DT_W), pl.BlockSpec((1, D_INNER), lambda c: (0, 0))],
        [blk(D_INNER), pl.BlockSpec((None, D_STATE, D_INNER), lambda c: (c, 0, 0))],
        [jax.ShapeDtypeStruct((t, D_INNER), F32), jax.ShapeDtypeStruct((nc, D_STATE, D_INNER), F32)],
        (xconv, dt, acs, d_exp), [pltpu.VMEM((D_STATE, D_INNER), F32), pltpu.VMEM((2 * CHUNK, D_INNER), F32)],
        ("arbitrary",), comm)
    return outs if comm is None else (outs, couts)


def _ssd_bwd(xconv, dt, acs, d_exp, hsave, dy, comm=None):
    t = xconv.shape[0]
    nc = t // CHUNK

    def body(xc_ref, dt_ref, acs_ref, d_ref, hs_ref, dy_ref, dxc_ref, ddt_ref, dacs_ref, dd_ref, dstate, wide, per_head):
        c = pl.program_id(0)

        @pl.when(c == 0)
        def _():
            dstate[...] = jnp.zeros_like(dstate)
            dd_ref[...] = jnp.zeros_like(dd_ref)

        tri, eye, blockdiag = _ssd_masks()
        acsv = acs_ref[...]
        wide[...] = _split_dot(jnp.concatenate([acsv, dt_ref[...]], axis=0), _spread_mat(), 3)
        eat_heads = jnp.exp(acsv[CHUNK - 1:CHUNK, :])

        for g in range(N_GROUPS):
            q = _ssd_group(xc_ref, wide, g, tri, eye, blockdiag)
            gs, xs_g, b_g, c_g, m = q["gs"], q["xs"], q["b"], q["c"], q["m"]
            bs = slice(D_INNER + D_STATE * g, D_INNER + D_STATE * (g + 1))
            cs = slice(D_INNER + 1024 + D_STATE * g, D_INNER + 1024 + D_STATE * (g + 1))
            h_t = hs_ref[:, gs]
            h_b = h_t.astype(BF)
            dy_g = dy_ref[:, gs]
            dy_b = dy_g.astype(BF)
            ds_t = dstate[:, gs]
            ds_b = ds_t.astype(BF)

            yoff = _dot(c_g, h_b) * q["e"]
            edy = (q["e"] * dy_g).astype(BF)
            d_c = _dot(edy, h_b, "nt")
            d_ht = _dot(c_g, edy, "tn")
            bds = _dot(b_g, ds_b)
            xd = q["x"] * q["dec"]
            d_b = _dot(xd.astype(BF), ds_b, "nt")
            dm = _dot(dy_b, q["xbd"], "nt")
            cross = _dot(m.astype(BF), dy_b, "tn")
            dx_full = q["dec"] * bds + _fold4(jnp.where(blockdiag, cross, 0.0))
            dml = (dm * q["lmat"]).astype(BF)
            d_c = d_c + _dot(dml, q["b_t"])
            d_b = d_b + _fold4(_dot(dml, c_g, "tn"))
            w = dm * m
            q_dec = xd * bds
            z = w - jnp.where(eye, jnp.sum(w, axis=0, keepdims=True), 0.0) + dy_g * yoff - q_dec
            rows = jnp.concatenate(
                [jnp.sum(q_dec, axis=0, keepdims=True), jnp.sum(ds_t * h_t, axis=0, keepdims=True),
                 jnp.zeros((6, GROUP_W), F32)], axis=0)
            per_head[:, gs] = jnp.concatenate([z, dx_full * xs_g, rows], axis=0)
            dxc_ref[:, cs] = d_c
            dxc_ref[:, bs] = d_b
            dxc_ref[:, gs] = dx_full * q["dt"] + d_ref[:, gs] * dy_g
            dd_ref[:, gs] += jnp.sum(dy_g * xs_g, axis=0, keepdims=True)
            dstate[:, gs] = q["eat"] * ds_t + d_ht

        seg = _split_dot(per_head[...], _gather_mat(), 2)
        datot = seg[2 * CHUNK:2 * CHUNK + 1] + eat_heads * seg[2 * CHUNK + 1:2 * CHUNK + 2]
        rowi = lax.broadcasted_iota(jnp.int32, (CHUNK, DT_W), 0)
        ddt_ref[...] = seg[CHUNK:2 * CHUNK]
        dacs_ref[...] = seg[0:CHUNK] + jnp.where(rowi == CHUNK - 1, datot, 0.0)

    rev = lambda w: pl.BlockSpec((CHUNK, w), lambda c: (nc - 1 - c, 0))
    vec = pl.BlockSpec((1, D_INNER), lambda c: (0, 0))
    outs, couts = _pcall(
        "ssd_bwd", body, (nc,),
        [rev(D_XBC), rev(DT_W), rev(DT_W), vec,
         pl.BlockSpec((None, D_STATE, D_INNER), lambda c: (nc - 1 - c, 0, 0)), rev(D_INNER)],
        [rev(D_XBC), rev(DT_W), rev(DT_W), vec],
        [jax.ShapeDtypeStruct((t, D_XBC), F32), jax.ShapeDtypeStruct((t, DT_W), F32),
         jax.ShapeDtypeStruct((t, DT_W), F32), jax.ShapeDtypeStruct((1, D_INNER), F32)],
        (xconv, dt, acs, d_exp, hsave, dy),
        [pltpu.VMEM((D_STATE, D_INNER), F32), pltpu.VMEM((2 * CHUNK, D_INNER), F32),
         pltpu.VMEM((2 * CHUNK + 8, D_INNER), F32)], ("arbitrary",), comm)
    return outs if comm is None else (outs, couts)


GN_CB = 1024
GN_GROUPS = GN_CB // GROUP_W


def _gnorm_fwd(y, p, w, comm=None):
    t = y.shape[0]
    zoff = OFF_Z // GN_CB

    def body(y_ref, z_ref, w_ref, o_ref):
        for g in range(GN_GROUPS):
            gs = slice(GROUP_W * g, GROUP_W * (g + 1))
            z = z_ref[:, gs].astype(F32)
            yf = y_ref[:, gs] * (z * _sigmoid(z))
            rstd = lax.rsqrt(jnp.mean(yf * yf, axis=-1, keepdims=True) + NORM_EPS)
            o_ref[:, gs] = (yf * rstd * w_ref[:, gs]).astype(BF)

    blk = pl.BlockSpec((TE, GN_CB), lambda i, j: (i, j))
    out, couts = _pcall(
        "gnorm_fwd", body, (t // TE, D_INNER // GN_CB),
        [blk, pl.BlockSpec((TE, GN_CB), lambda i, j: (i, zoff + j)), pl.BlockSpec((1, GN_CB), lambda i, j: (0, j))],
        blk, jax.ShapeDtypeStruct((t, D_INNER), BF), (y, p, w), (), ("parallel", "parallel"), comm)
    return out if comm is None else (out, couts)


def _gnorm_bwd(y, p, w, dyn, comm=None):
    t = y.shape[0]
    zoff = OFF_Z // GN_CB

    def body(y_ref, z_ref, w_ref, dn_ref, dy_ref, dz_ref, dw_ref):
        i = pl.program_id(1)
        for g in range(GN_GROUPS):
            gs = slice(GROUP_W * g, GROUP_W * (g + 1))
            z = z_ref[:, gs].astype(F32)
            yv = y_ref[:, gs]
            s = _sigmoid(z)
            sil = z * s
            yf = yv * sil
            rstd = lax.rsqrt(jnp.mean(yf * yf, axis=-1, keepdims=True) + NORM_EPS)
            xhat = yf * rstd
            dn = dn_ref[:, gs]
            wd = dn * w_ref[:, gs]
            proj = jnp.mean(wd * xhat, axis=-1, keepdims=True)
            dyf = rstd * (wd - xhat * proj)
            dy_ref[:, gs] = dyf * sil
            dz_ref[:, gs] = (dyf * yv * (s * (1.0 + z * (1.0 - s)))).astype(BF)
            part = jnp.sum(dn * xhat, axis=0, keepdims=True)

            @pl.when(i == 0)
            def _():
                dw_ref[:, gs] = part

            @pl.when(i > 0)
            def _():
                dw_ref[:, gs] += part

    blk = pl.BlockSpec((TE, GN_CB), lambda j, i: (i, j))
    vec = pl.BlockSpec((1, GN_CB), lambda j, i: (0, j))
    outs, couts = _pcall(
        "gnorm_bwd", body, (D_INNER // GN_CB, t // TE),
        [blk, pl.BlockSpec((TE, GN_CB), lambda j, i: (i, zoff + j)), vec, blk],
        [blk, pl.BlockSpec((TE, GN_CB), lambda j, i: (i, zoff + j)), vec],
        [jax.ShapeDtypeStruct((t, D_INNER), F32), jax.ShapeDtypeStruct((t, N_MAIN), BF),
         jax.ShapeDtypeStruct((1, D_INNER), F32)],
        (y, p, w, dyn), (), ("parallel", "arbitrary"), comm)
    return outs if comm is None else (outs, couts)


MERGE_CB = 512


def _merge_fwd(p, ya, yb):
    t = ya.shape[0]

    def body(ga_ref, gb_ref, ya_ref, yb_ref, o_ref):
        o_ref[...] = (_sigmoid(ga_ref[...]) * ya_ref[...] + _sigmoid(gb_ref[...]) * yb_ref[...]).astype(BF)

    blk = pl.BlockSpec((TE, MERGE_CB), lambda i, j: (i, j))
    return pl.pallas_call(
        body, name="merge_fwd", grid=(t // TE, D_MODEL // MERGE_CB),
        in_specs=[pl.BlockSpec((TE, MERGE_CB), lambda i, j: (i, 2 * j)),
                  pl.BlockSpec((TE, MERGE_CB), lambda i, j: (i, 2 * j + 1)), blk, blk],
        out_specs=blk, out_shape=jax.ShapeDtypeStruct((t, D_MODEL), BF),
        compiler_params=_params("parallel", "parallel"))(p, p, ya, yb)


def _merge_bwd(p, ya, yb, dm):
    t = ya.shape[0]

    def body(ga_ref, gb_ref, ya_ref, yb_ref, dm_ref, dg_ref, dya_ref, dyb_ref):
        d = dm_ref[...]
        sa = _sigmoid(ga_ref[...])
        sb = _sigmoid(gb_ref[...])
        dg_ref[:, 0:MERGE_CB] = (d * ya_ref[...] * sa * (1.0 - sa)).astype(BF)
        dg_ref[:, MERGE_CB:2 * MERGE_CB] = (d * yb_ref[...] * sb * (1.0 - sb)).astype(BF)
        dya_ref[...] = (d * sa).astype(BF)
        dyb_ref[...] = (d * sb).astype(BF)

    blk = pl.BlockSpec((TE, MERGE_CB), lambda i, j: (i, j))
    return pl.pallas_call(
        body, name="merge_bwd", grid=(t // TE, D_MODEL // MERGE_CB),
        in_specs=[pl.BlockSpec((TE, MERGE_CB), lambda i, j: (i, 2 * j)),
                  pl.BlockSpec((TE, MERGE_CB), lambda i, j: (i, 2 * j + 1)), blk, blk, blk],
        out_specs=[pl.BlockSpec((TE, 2 * MERGE_CB), lambda i, j: (i, j)), blk, blk],
        out_shape=[jax.ShapeDtypeStruct((t, N_GD), BF)] + [jax.ShapeDtypeStruct((t, D_MODEL), BF)] * 2,
        compiler_params=_params("parallel", "parallel"))(p, p, ya, yb, dm)


def _adamw(name, parts, w, m, v, comm=None):
    r, c = w.shape
    tr = _row_tile(r)
    tc = ADAM_COL_TILE if (tr == r and r > 512 and c % ADAM_COL_TILE == 0) else c
    n_parts = parts.shape[0]
    bc1 = 1.0 - ADAM_B1 ** ADAM_STEP
    bc2 = 1.0 - ADAM_B2 ** ADAM_STEP

    def body(p_ref, w_ref, m_ref, v_ref, g_ref, d_ref, nm_ref, nv_ref):
        g = p_ref[0].astype(F32)
        for k in range(1, n_parts):
            g = g + p_ref[k].astype(F32)
        nm = ADAM_B1 * m_ref[...] + (1.0 - ADAM_B1) * g
        nv = ADAM_B2 * v_ref[...] + (1.0 - ADAM_B2) * (g * g)
        g_ref[...] = g
        nm_ref[...] = nm
        nv_ref[...] = nv
        d_ref[...] = -ADAM_LR * ((nm / bc1) / (jnp.sqrt(nv / bc2) + ADAM_EPS) + ADAM_WD * w_ref[...])

    blk = pl.BlockSpec((tr, tc), lambda i, j: (i, j))
    outs, couts = _pcall(
        name, body, (r // tr, c // tc),
        [pl.BlockSpec((n_parts, tr, tc), lambda i, j: (0, i, j)), blk, blk, blk], [blk] * 4,
        [jax.ShapeDtypeStruct((r, c), F32)] * 4, (parts, w, m, v), (), ("parallel", "parallel"), comm)
    return outs if comm is None else (outs, couts)


def _pad_lanes(v, width):
    return jnp.pad(v, ((0, 0), (0, width - v.shape[1])))


def _reduce_start(slots, host):
    outs, sib = host(_pair_comm([a for _, a in slots]))
    sums = [(n, _add_pairs("pairsum_" + n, a, b)) for (n, a), b in zip(slots, sib)]
    return outs, sums


def _train_step(x, target, shard, rep):
    gdt = BF
    recv = {}
    (got,) = _comm_call("gather_ffn1_in", _gather_comm([shard["ffn1_w_in"]], [True]))
    w1_in = got.reshape(2 * D_FF, D_MODEL)
    h1 = _rms_fwd("rms1_fwd", x, rep["ffn1_norm"])
    gu1, got = _mm_nt("ffn1_in", h1, w1_in, tn=FF_HALF, out_dtype=BF, comm=_gather_comm(
        [shard["ffn1_w_out"], shard["w_in"], shard["short_conv_w"], shard["ssm_conv_w"]]))
    w1_out = got[0].reshape(D_FF, D_MODEL)
    w_in_t = got[1].reshape(N_IN, D_MODEL)
    short_conv_w = got[2].transpose(1, 0, 2).reshape(3, D_MODEL)
    ssm_conv_w = got[3].transpose(1, 0, 2).reshape(4, D_XBC)
    act1 = _swiglu_fwd("swiglu1_fwd", gu1)
    x1 = _mm_nn("ffn1_out", act1, w1_out, res=x, alpha=0.5)
    ga0 = N_MAIN + N_HEADS
    gb0 = ga0 + D_MODEL
    half = D_MODEL // 2
    w_gd = jnp.concatenate(
        [w_in_t[ga0:ga0 + half], w_in_t[gb0:gb0 + half], w_in_t[ga0 + half:gb0], w_in_t[gb0 + half:],
         w_in_t[N_MAIN:N_MAIN + N_HEADS], jnp.zeros((DT_W - N_HEADS, D_MODEL), BF)], axis=0)
    w_mix_perm = w_in_t[0:3 * D_MODEL].reshape(3, 4, CONV_CB, D_MODEL).transpose(1, 0, 2, 3).reshape(3 * D_MODEL, D_MODEL)

    h2 = _rms_fwd("rms2_fwd", x1, rep["mix_norm"])
    p, got = _mm_nt("proj_main", h2, w_in_t, n=N_MAIN, tn=1024, out_dtype=BF, comm=_gather_comm(
        [shard["short_w_out"], shard["ssm_w_out"], shard["w_out"]]))
    p_gd = _mm_nt("proj_gd", h2, w_gd)
    short_w_out = got[0].reshape(D_MODEL, D_MODEL)
    ssm_w_out = got[1].reshape(D_INNER, D_MODEL)
    w_out = got[2].reshape(D_MODEL, D_MODEL)
    ya_in = _mix_a_fwd(p, short_conv_w)
    y_a = _mm_nn("short_out", ya_in, short_w_out)
    xconv, (got,) = _ssm_conv_fwd(p, ssm_conv_w, rep["ssm_conv_b"], comm=_gather_comm([shard["ffn2_w_out"]]))
    w2_out = got.reshape(D_FF, D_MODEL)
    dt, acs = _dt_fwd(p_gd, rep["dt_bias_pad"], rep["a_log_pad"])
    (y_ssm, hsave), (got,) = _ssd_fwd(xconv, dt, acs, rep["d_exp"], comm=_gather_comm([shard["ffn2_w_in"]], [True]))
    w2_in = got.reshape(2 * D_FF, D_MODEL)
    yn = _gnorm_fwd(y_ssm, p, rep["ssm_norm"])
    y_b = _mm_nn("ssm_out", yn, ssm_w_out, tk=1024)
    merged = _merge_fwd(p_gd, y_a, y_b)
    x2 = _mm_nn("mix_out", merged, w_out, res=x1)

    h3 = _rms_fwd("rms3_fwd", x2, rep["ffn2_norm"])
    gu2 = _mm_nt("ffn2_in", h3, w2_in, tn=FF_HALF, out_dtype=BF)
    act2 = _swiglu_fwd("swiglu2_fwd", gu2)
    x3 = _mm_nn("ffn2_out", act2, w2_out, res=x2, alpha=0.5)

    loss, dx3, dx3h, g_final = _final_loss(x3, rep["final_norm"], target)

    small = {"final_norm": g_final}
    dact2 = _mm_nt("ffn2_out_bwd_act", dx3h, w2_out, out_dtype=BF)
    g_w2_out = _mm_tn("ffn2_out_bwd_w", act2, dx3h, gdt, tm=FF_HALF)
    dgu2 = _swiglu_bwd("swiglu2_bwd", gu2, dact2)
    g_w2_in = _mm_tn("ffn2_in_bwd_w", dgu2, h3, gdt, tm=FF_HALF)
    dh3 = _mm_nn("ffn2_in_bwd_h", dgu2, w2_in, tk=FF_HALF)
    dx2, dx2b, small["ffn2_norm"] = _rms_bwd("rms3_bwd", x2, rep["ffn2_norm"], dh3, dx3, 1.0)

    dmerged = _mm_nt("mix_out_bwd_x", dx2b, w_out)
    g_w_out = _mm_tn("mix_out_bwd_w", merged, dx2b, gdt)
    dp_gd, dya, dyb = _merge_bwd(p_gd, y_a, y_b, dmerged)

    dya_in = _mm_nt("short_out_bwd_x", dya, short_w_out)
    g_short_w_out = _mm_tn("short_out_bwd_w", ya_in, dya, gdt)

    dyn = _mm_nt("ssm_out_bwd_x", dyb, ssm_w_out)
    g_ssm_w_out = _mm_tn("ssm_out_bwd_w", yn, dyb, gdt)
    late = [("ffn2_w_out", g_w2_out.reshape(N_DEV, FF_SHARD // 2, D_MODEL)),
            ("ffn2_w_in", g_w2_in.reshape(N_DEV, FF_SHARD, D_MODEL)),
            ("w_out", g_w_out.reshape(N_DEV, -1, D_MODEL)), ("short_w_out", g_short_w_out.reshape(N_DEV, -1, D_MODEL)),
            ("ssm_w_out", g_ssm_w_out.reshape(N_DEV, -1, D_MODEL))]
    (dy_ssm, dp, small["ssm_norm"]), sums = _reduce_start(
        late, lambda comm: _gnorm_bwd(y_ssm, p, rep["ssm_norm"], dyn, comm=comm))
    dp, g_short_conv = _mix_a_bwd(p, short_conv_w, dya_in, dp)
    first = [(n, a) for n, a in sums if n.startswith("ffn2")]
    second = [(n, a) for n, a in sums if not n.startswith("ffn2")]
    (dxconv, ddt, dacs, dd_lane), got = _ssd_bwd(
        xconv, dt, acs, rep["d_exp"], hsave, dy_ssm,
        comm=_chip_comm([a for _, a in first], [n == "ffn2_w_in" for n, _ in first]))
    recv.update({n: a for (n, _), a in zip(first, got)})
    small["ssm_D"] = dd_lane.reshape(N_HEADS, HEAD_DIM).sum(axis=1)[None, :]
    (dp, g_ssm_conv, small["ssm_conv_b"]), got = _ssm_conv_bwd(
        p, ssm_conv_w, rep["ssm_conv_b"], dxconv, dp, comm=_chip_comm([a for _, a in second]))
    recv.update({n: a for (n, _), a in zip(second, got)})
    dp_gd, dbias, dalog = _dt_bwd(p_gd, rep["dt_bias_pad"], rep["a_log_pad"], dt, ddt, dacs, dp_gd)
    small["ssm_dt_bias"] = dbias[:, :N_HEADS]
    small["ssm_A_log"] = dalog[:, :N_HEADS]

    g_main = _mm_tn("proj_main_bwd_w", dp, h2, gdt, tm=1024)
    g_gd = _mm_tn("proj_gd_bwd_w", dp_gd, h2, gdt)
    g_mix = g_main[0:3 * D_MODEL].reshape(4, 3, CONV_CB, D_MODEL).transpose(1, 0, 2, 3).reshape(3 * D_MODEL, D_MODEL)
    g_in_t = jnp.concatenate(
        [g_mix, g_main[3 * D_MODEL:], g_gd[2 * D_MODEL:2 * D_MODEL + N_HEADS],
         g_gd[0:half], g_gd[2 * half:3 * half], g_gd[half:2 * half], g_gd[3 * half:4 * half]], axis=0).reshape(
        N_DEV, IN_SHARD, D_MODEL)
    dh2, w_sums = _reduce_start(
        [("w_in", g_in_t)], lambda comm: _mm_nn("proj_mix_bwd_x", dp, w_mix_perm, tk=1024, kk=3 * D_MODEL, comm=comm))
    w_sum = w_sums[0][1]

    def w_piece(i):
        return _chip_comm([w_sum], rows=[W_GRAD_ROW_CUTS[i]])

    dh2, got0 = _mm_nn("proj_rest_bwd_x", dp, w_in_t, tk=1024, kk=N_MAIN - 3 * D_MODEL, a_off=3, b_off=3, res=dh2,
                       comm=w_piece(0))
    dh2, got1 = _mm_nn("proj_gd_bwd_x", dp_gd, w_gd, res=dh2, comm=w_piece(1))
    (dx1, dx1h, small["mix_norm"]), got2 = _rms_bwd("rms2_bwd", x1, rep["mix_norm"], dh2, dx2, 0.5, comm=w_piece(2))
    g_w1_out, got3 = _mm_tn("ffn1_out_bwd_w", act1, dx1h, gdt, tm=FF_HALF, comm=w_piece(3))
    rest = [("ffn1_w_out", g_w1_out.reshape(N_DEV, FF_SHARD // 2, D_MODEL)),
            ("short_conv_w", g_short_conv.reshape(3, N_DEV, -1).transpose(1, 0, 2)),
            ("ssm_conv_w", g_ssm_conv.reshape(4, N_DEV, -1).transpose(1, 0, 2))]
    dact1, got = _mm_nt("ffn1_out_bwd_act", dx1h, w1_out, out_dtype=BF,
                        comm=_join_comm(w_piece(4), _pair_comm([a for _, a in rest])))
    got4, sib = got[0], got[1:]
    rest_sums = [(n, _add_pairs("pairsum_" + n, a, b)) for (n, a), b in zip(rest, sib)]
    recv["w_in"] = jnp.concatenate([got0[0], got1[0], got2[0], got3[0], got4], axis=1)
    dgu1, got = _swiglu_bwd("swiglu1_bwd", gu1, dact1, comm=_chip_comm([a for _, a in rest_sums]))
    recv.update({n: a for (n, _), a in zip(rest_sums, got)})

    def part(tag, width, off, comm=None):
        out = _mm_tn("ffn1_in_bwd_w_" + tag, dgu1, h1, gdt, tm=FF_HALF, n=width, col_off=off, comm=comm)
        g, couts = (out, None) if comm is None else out
        return g.reshape(N_DEV, FF_SHARD, width), couts

    g_a, _ = part("a", 384, 0)
    g_b, sib = part("b", 384, 1, _pair_comm([g_a]))
    sum_a = _add_pairs("pairsum_ffn1_w_in_a", g_a, sib[0])
    g_c, (recv_a, sib_b) = part("c", 256, 3, _join_comm(_chip_comm([sum_a], [True]), _pair_comm([g_b])))
    sum_b = _add_pairs("pairsum_ffn1_w_in_b", g_b, sib_b)
    dh1, (recv_b, sib_c) = _mm_nn("ffn1_in_bwd_h", dgu1, w1_in, tk=FF_HALF,
                                  comm=_join_comm(_chip_comm([sum_b], [True]), _pair_comm([g_c])))
    sum_c = _add_pairs("pairsum_ffn1_w_in_c", g_c, sib_c)
    (dx0, _, small["ffn1_norm"]), (recv_c,) = _rms_bwd("rms1_bwd", x, rep["ffn1_norm"], dh1, dx1, 1.0,
                                                        comm=_chip_comm([sum_c], [True]))
    recv["ffn1_w_in"] = jnp.concatenate([recv_a, recv_b, recv_c], axis=2)
    return dx0, recv, _pack_small(small, loss[:, 0:1])


_SMALL = [("ffn1_norm", 1024), ("mix_norm", 1024), ("ssm_conv_b", 4096), ("ssm_dt_bias", 32), ("ssm_A_log", 32),
          ("ssm_D", 32), ("ssm_norm", 2048), ("ffn2_norm", 1024), ("final_norm", 1024)]
SMALL_W = 10368


def _pack_small(d, loss=None):
    parts = [d[n].reshape(1, -1).astype(F32) for n, _ in _SMALL]
    used = sum(sz for _, sz in _SMALL)
    tail = jnp.zeros((1, SMALL_W - used), F32)
    if loss is not None:
        tail = tail.at[:, 0:1].set(loss)
    return jnp.concatenate(parts + [tail], axis=1)


def _adamw_small(parts, w, m, v):
    n_par = len(_SMALL)
    bc1 = 1.0 - ADAM_B1 ** ADAM_STEP
    bc2 = 1.0 - ADAM_B2 ** ADAM_STEP
    used = sum(sz for _, sz in _SMALL)

    def body(*refs):
        p_ref = refs[0]
        ins = refs[1:1 + 3 * n_par]
        outs = refs[1 + 3 * n_par:]
        g_all = p_ref[0]
        for k in range(1, N_DEV):
            g_all = g_all + p_ref[k]
        off = 0
        for i, (_, sz) in enumerate(_SMALL):
            g = g_all[:, off:off + sz]
            w_ref, m_ref, v_ref = ins[3 * i:3 * i + 3]
            nm = ADAM_B1 * m_ref[...] + (1.0 - ADAM_B1) * g
            nv = ADAM_B2 * v_ref[...] + (1.0 - ADAM_B2) * (g * g)
            outs[4 * i][...] = g
            outs[4 * i + 1][...] = -ADAM_LR * ((nm / bc1) / (jnp.sqrt(nv / bc2) + ADAM_EPS) + ADAM_WD * w_ref[...])
            outs[4 * i + 2][...] = nm
            outs[4 * i + 3][...] = nv
            off += sz
        outs[4 * n_par][...] = g_all[:, used:SMALL_W]

    args = [parts]
    out_shape = []
    for name, sz in _SMALL:
        args += [w[name], m[name], v[name]]
        out_shape += [jax.ShapeDtypeStruct((1, sz), F32)] * 4
    out_shape.append(jax.ShapeDtypeStruct((1, SMALL_W - used), F32))
    res = pl.pallas_call(body, name="adamw_small", out_shape=out_shape,
                         compiler_params=pltpu.CompilerParams(vmem_limit_bytes=VMEM_LIMIT_V7X))(*args)
    return {name: tuple(res[4 * i:4 * i + 4]) for i, (name, _) in enumerate(_SMALL)}, res[-1]


_SHARDED = ["ffn1_w_in", "ffn1_w_out", "w_in", "short_conv_w", "short_w_out", "ssm_conv_w", "ssm_w_out", "w_out",
            "ffn2_w_in", "ffn2_w_out"]
_TRANSPOSED = ("ffn1_w_in", "w_in", "ffn2_w_in")
_ORDER = ["ffn1_norm", "ffn1_w_in", "ffn1_w_out", "mix_norm", "w_in", "short_conv_w", "short_w_out", "ssm_conv_w",
          "ssm_conv_b", "ssm_dt_bias", "ssm_A_log", "ssm_D", "ssm_norm", "ssm_w_out", "w_out", "ffn2_norm",
          "ffn2_w_in", "ffn2_w_out", "final_norm"]


def kernel(x, ffn1_norm, ffn1_w_in, ffn1_w_out, mix_norm, w_in, short_conv_w, short_w_out, ssm_conv_w, ssm_conv_b, ssm_dt_bias, ssm_A_log, ssm_D, ssm_norm, ssm_w_out, w_out, ffn2_norm, ffn2_w_in, ffn2_w_out, final_norm, loss_target, m_ffn1_norm, m_ffn1_w_in, m_ffn1_w_out, m_mix_norm, m_w_in, m_short_conv_w, m_short_w_out, m_ssm_conv_w, m_ssm_conv_b, m_ssm_dt_bias, m_ssm_A_log, m_ssm_D, m_ssm_norm, m_ssm_w_out, m_w_out, m_ffn2_norm, m_ffn2_w_in, m_ffn2_w_out, m_final_norm, v_ffn1_norm, v_ffn1_w_in, v_ffn1_w_out, v_mix_norm, v_w_in, v_short_conv_w, v_short_w_out, v_ssm_conv_w, v_ssm_conv_b, v_ssm_dt_bias, v_ssm_A_log, v_ssm_D, v_ssm_norm, v_ssm_w_out, v_w_out, v_ffn2_norm, v_ffn2_w_in, v_ffn2_w_out, v_final_norm):
    w = dict(ffn1_norm=ffn1_norm, ffn1_w_in=ffn1_w_in, ffn1_w_out=ffn1_w_out, mix_norm=mix_norm, w_in=w_in,
             short_conv_w=short_conv_w, short_w_out=short_w_out, ssm_conv_w=ssm_conv_w, ssm_conv_b=ssm_conv_b,
             ssm_dt_bias=ssm_dt_bias, ssm_A_log=ssm_A_log, ssm_D=ssm_D, ssm_norm=ssm_norm, ssm_w_out=ssm_w_out,
             w_out=w_out, ffn2_norm=ffn2_norm, ffn2_w_in=ffn2_w_in, ffn2_w_out=ffn2_w_out, final_norm=final_norm)
    m = dict(ffn1_norm=m_ffn1_norm, ffn1_w_in=m_ffn1_w_in, ffn1_w_out=m_ffn1_w_out, mix_norm=m_mix_norm, w_in=m_w_in,
             short_conv_w=m_short_conv_w, short_w_out=m_short_w_out, ssm_conv_w=m_ssm_conv_w,
             ssm_conv_b=m_ssm_conv_b, ssm_dt_bias=m_ssm_dt_bias, ssm_A_log=m_ssm_A_log, ssm_D=m_ssm_D,
             ssm_norm=m_ssm_norm, ssm_w_out=m_ssm_w_out, w_out=m_w_out, ffn2_norm=m_ffn2_norm,
             ffn2_w_in=m_ffn2_w_in, ffn2_w_out=m_ffn2_w_out, final_norm=m_final_norm)
    v = dict(ffn1_norm=v_ffn1_norm, ffn1_w_in=v_ffn1_w_in, ffn1_w_out=v_ffn1_w_out, mix_norm=v_mix_norm, w_in=v_w_in,
             short_conv_w=v_short_conv_w, short_w_out=v_short_w_out, ssm_conv_w=v_ssm_conv_w,
             ssm_conv_b=v_ssm_conv_b, ssm_dt_bias=v_ssm_dt_bias, ssm_A_log=v_ssm_A_log, ssm_D=v_ssm_D,
             ssm_norm=v_ssm_norm, ssm_w_out=v_ssm_w_out, w_out=v_w_out, ffn2_norm=v_ffn2_norm,
             ffn2_w_in=v_ffn2_w_in, ffn2_w_out=v_ffn2_w_out, final_norm=v_final_norm)
    shapes = {n: w[n].shape for n in _ORDER}

    def local(d, n):
        return d[n][0].T if n in _TRANSPOSED else d[n][0]

    shard = {n: local(w, n) for n in _SHARDED}

    wire = {n: (shard[n] if n in ("short_conv_w", "ssm_conv_w") else shard[n].astype(BF)) for n in _SHARDED}
    rep = {
        "ffn1_norm": ffn1_norm, "mix_norm": mix_norm, "ffn2_norm": ffn2_norm, "ssm_norm": ssm_norm,
        "ssm_conv_b": ssm_conv_b, "final_norm": final_norm.reshape(1, D_MODEL),
        "dt_bias_pad": _pad_lanes(ssm_dt_bias, DT_W), "a_log_pad": _pad_lanes(ssm_A_log, DT_W),
        "d_exp": jnp.repeat(ssm_D, HEAD_DIM, axis=1),
    }
    grad_x, parts, packed = _train_step(x[0], loss_target[0], wire, rep)

    out_g, out_d, out_m, out_v = {}, {}, {}, {}
    for n in _SHARDED:
        if n == "ssm_w_out":
            res, (small_parts,) = _adamw("adamw_" + n, parts[n], shard[n], local(m, n), local(v, n),
                                         comm=_gather_comm([packed]))
        else:
            res = _adamw("adamw_" + n, parts[n], shard[n], local(m, n), local(v, n))
        out_g[n], out_d[n], out_m[n], out_v[n] = [(r.T if n in _TRANSPOSED else r).reshape(shapes[n]) for r in res]
    row = lambda d: {n: d[n].reshape(1, -1) for n, _ in _SMALL}
    sres, loss_row = _adamw_small(small_parts, row(w), row(m), row(v))
    for n, _ in _SMALL:
        out_g[n], out_d[n], out_m[n], out_v[n] = [r.reshape(shapes[n]) for r in sres[n]]
    loss = loss_row[0, 0]
    return (loss, grad_x[None], *[out_g[n] for n in _ORDER], *[out_d[n] for n in _ORDER],
            *[out_m[n] for n in _ORDER], *[out_v[n] for n in _ORDER])
```

```python
import functools

import jax
import jax.numpy as jnp
from jax import lax
from jax.experimental import pallas as pl
from jax.experimental.pallas import tpu as pltpu

F32 = jnp.float32
BF = jnp.bfloat16

N_DEV = 8
D_MODEL = 1024
D_FF = 2816
D_INNER = 2048
D_XBC = 4096
N_HEADS = 32
HEAD_DIM = 64
N_GROUPS = 8
D_STATE = 128
CHUNK = 64
GROUP_W = D_INNER // N_GROUPS
NORM_EPS = 1e-5
N_IN = 11296
FF_SHARD = 2 * D_FF // N_DEV
FF_HALF = D_FF // 2
IN_SHARD = N_IN // N_DEV

OFF_B, OFF_C, OFF_XA, OFF_Z, OFF_XBC = 0, 1024, 2048, 3072, 5120
N_MAIN = 9216
OFF_DT = 2048
DT_W = 128
N_GD = 2048 + DT_W
W_GRAD_ROW_CUTS = [(0, 400), (400, 568), (568, 704), (704, 880), (880, 1040), (1040, 1240), (1240, 1412)]

ADAM_LR, ADAM_B1, ADAM_B2, ADAM_EPS, ADAM_WD, ADAM_STEP = 0.001, 0.9, 0.999, 1e-08, 0.01, 10

VMEM_LIMIT_V7X = 56 * 1024 * 1024
TM = 1024
TN_MAX_TOKENS = 2048
TE = 512
ADAM_COL_TILE = 256
GATHER_PIECES = 4
GATHER_PIECE_MIN_ROWS = 512


def _params(*sem):
    return pltpu.CompilerParams(dimension_semantics=sem, vmem_limit_bytes=VMEM_LIMIT_V7X)


_DIMS = {
    "nn": (((1,), (0,)), ((), ())),
    "nt": (((1,), (1,)), ((), ())),
    "tn": (((0,), (0,)), ((), ())),
}


def _dot(a, b, mode="nn"):
    return lax.dot_general(a, b, _DIMS[mode], preferred_element_type=F32)


def _sigmoid(x):
    return 1.0 / (1.0 + jnp.exp(-x))


class _Comm:
    def __init__(self, inputs, out_shapes, sems, start, finish):
        self.inputs, self.out_shapes, self.sems, self.start, self.finish = inputs, out_shapes, sems, start, finish


def _pcall(name, body, grid, in_specs, out_specs, out_shape, args, scratch=(), sem=None, comm=None, aliases=None):
    single = not isinstance(out_shape, (list, tuple))
    out_shapes = [out_shape] if single else list(out_shape)
    out_specs = [out_specs] if single else list(out_specs)
    n_in, n_out, n_scr = len(args), len(out_shapes), len(scratch)
    aliases = {} if aliases is None else aliases
    if comm is None:
        res = pl.pallas_call(
            body, name=name, grid=grid, in_specs=list(in_specs), out_specs=out_specs, out_shape=out_shapes,
            scratch_shapes=list(scratch), input_output_aliases=aliases, compiler_params=_params(*sem))(*args)
        return (res[0] if single else res), []
    nci, nco = len(comm.inputs), len(comm.out_shapes)

    def wrapped(*refs):
        a = refs[:n_in]
        ci = refs[n_in:n_in + nci]
        o0 = n_in + nci
        o = refs[o0:o0 + n_out]
        co = refs[o0 + n_out:o0 + n_out + nco]
        s0 = o0 + n_out + nco
        s = refs[s0:s0 + n_scr]
        cs = refs[s0 + n_scr:]
        pids = [pl.program_id(i) for i in range(len(grid))]
        first = functools.reduce(jnp.logical_and, [p == 0 for p in pids])
        last = functools.reduce(jnp.logical_and, [p == g - 1 for p, g in zip(pids, grid)])

        @pl.when(first)
        def _():
            comm.start(ci, co, cs)

        body(*a, *o, *s)

        @pl.when(last)
        def _():
            comm.finish(ci, co, cs)

    any_spec = pl.BlockSpec(memory_space=pl.ANY)
    res = pl.pallas_call(
        wrapped, name=name, grid=grid, in_specs=list(in_specs) + [any_spec] * nci,
        out_specs=out_specs + [any_spec] * nco, out_shape=out_shapes + list(comm.out_shapes),
        scratch_shapes=list(scratch) + list(comm.sems), input_output_aliases=aliases,
        compiler_params=_params(*(("arbitrary",) * len(grid))))(*args, *comm.inputs)
    core = res[:n_out]
    return (core[0] if single else core), list(res[n_out:])


def _comm_call(name, comm):
    nci, nco = len(comm.inputs), len(comm.out_shapes)

    def body(*refs):
        ci, co, cs = refs[:nci], refs[nci:nci + nco], refs[nci + nco:]
        comm.start(ci, co, cs)
        comm.finish(ci, co, cs)

    any_spec = pl.BlockSpec(memory_space=pl.ANY)
    return pl.pallas_call(
        body, name=name, in_specs=[any_spec] * nci, out_specs=[any_spec] * nco, out_shape=list(comm.out_shapes),
        scratch_shapes=list(comm.sems), compiler_params=pltpu.CompilerParams(has_side_effects=True))(*comm.inputs)


def _remote(src, dst, ssem, rsem, dev):
    return pltpu.make_async_remote_copy(src_ref=src, dst_ref=dst, send_sem=ssem, recv_sem=rsem, device_id=dev,
                                        device_id_type=pl.DeviceIdType.MESH)


def _place():
    x, y, c = lax.axis_index("x"), lax.axis_index("y"), lax.axis_index("c")
    other_chips = [(1 - x, y), (x, 1 - y), (1 - x, 1 - y)]
    return x, y, c, other_chips


def _slot(x, y, c, swap):
    return 4 * y + 2 * x + c if swap else 4 * x + 2 * y + c


def _chip_slot(x, y, swap):
    return 2 * y + x if swap else 2 * x + y


def _gather_comm(shards, swaps=None):
    n = len(shards)
    per = N_DEV - 1
    swaps = [False] * n if swaps is None else swaps
    pieces = []
    for i, a in enumerate(shards):
        rows = a.shape[0]
        k = GATHER_PIECES if (a.ndim == 2 and rows >= GATHER_PIECE_MIN_ROWS) else 1
        step = -(-rows // (k * 8)) * 8
        if k == 1:
            pieces.append((i, 0, None))
        else:
            pieces += [(i, r, min(step, rows - r)) for r in range(0, rows, step)]
    m = len(pieces)

    def src(ins, v):
        i, r, cnt = pieces[v]
        return ins[i] if cnt is None else ins[i].at[pl.ds(r, cnt)]

    def place(outs, v, x, y, c):
        i, r, cnt = pieces[v]
        blk = outs[i].at[_slot(x, y, c, swaps[i])]
        return blk if cnt is None else blk.at[pl.ds(r, cnt)]

    def start(ins, outs, sems):
        send, recv, loc = sems
        x, y, c, chips = _place()
        for v in range(m):
            me = place(outs, v, x, y, c)
            pltpu.make_async_copy(src(ins, v), me, loc.at[v]).start()
            _remote(src(ins, v), me, send.at[per * v], recv.at[per * v], (x, y, 1 - c)).start()
        for j, (qx, qy) in enumerate(chips):
            for v in range(m):
                _remote(src(ins, v), place(outs, v, x, y, c), send.at[per * v + 1 + j], recv.at[per * v + 1 + j],
                        (qx, qy, c)).start()

    def finish(ins, outs, sems):
        send, recv, loc = sems
        x, y, c, chips = _place()
        sib = (x, y, 1 - c)
        for v in range(m):
            for j, (qx, qy) in enumerate(chips):
                blk = place(outs, v, qx, qy, c)
                _remote(blk, blk, send.at[per * v + 1 + j], recv.at[per * v + 1 + j], (qx, qy, c)).wait_recv()
                _remote(blk, blk, send.at[per * v + 4 + j], recv.at[per * v + 4 + j], sib).start()
        for v in range(m):
            blk = place(outs, v, x, y, 1 - c)
            _remote(blk, blk, send.at[per * v], recv.at[per * v], sib).wait_recv()
            for j, (qx, qy) in enumerate(chips):
                blk = place(outs, v, qx, qy, 1 - c)
                _remote(blk, blk, send.at[per * v + 4 + j], recv.at[per * v + 4 + j], sib).wait_recv()
        for v in range(m):
            own = place(outs, v, x, y, c)
            for k in range(per):
                _remote(src(ins, v), own, send.at[per * v + k], recv.at[per * v + k], sib).wait_send()
            pltpu.make_async_copy(src(ins, v), own, loc.at[v]).wait()

    out_shapes = [jax.ShapeDtypeStruct((N_DEV,) + tuple(a.shape), a.dtype) for a in shards]
    sems = [pltpu.SemaphoreType.DMA((per * m,)), pltpu.SemaphoreType.DMA((per * m,)), pltpu.SemaphoreType.DMA((m,))]
    return _Comm(list(shards), out_shapes, sems, start, finish)


def _pair_comm(slots):
    n = len(slots)

    def copies(ins, outs, sems):
        send, recv = sems
        x, y, c, _ = _place()
        sib = (x, y, 1 - c)
        out = []
        for i in range(n):
            for q in range(4):
                out.append(_remote(ins[i].at[2 * q + 1 - c], outs[i].at[q], send.at[4 * i + q], recv.at[4 * i + q], sib))
        return out

    def start(ins, outs, sems):
        for cp in copies(ins, outs, sems):
            cp.start()

    def finish(ins, outs, sems):
        for cp in copies(ins, outs, sems):
            cp.wait_send()
            cp.wait_recv()

    out_shapes = [jax.ShapeDtypeStruct((4,) + tuple(a.shape[1:]), a.dtype) for a in slots]
    sems = [pltpu.SemaphoreType.DMA((4 * n,)), pltpu.SemaphoreType.DMA((4 * n,))]
    return _Comm(list(slots), out_shapes, sems, start, finish)


def _chip_comm(chip_sums, swaps=None, rows=None):
    n = len(chip_sums)
    swaps = [False] * n if swaps is None else swaps
    rows = [None] * n if rows is None else rows

    def src(ins, i, q):
        return ins[i].at[q] if rows[i] is None else ins[i].at[q, pl.ds(rows[i][0], rows[i][1] - rows[i][0])]

    def start(ins, outs, sems):
        send, recv, loc = sems
        x, y, c, chips = _place()
        for i in range(n):
            mine = _chip_slot(x, y, swaps[i])
            pltpu.make_async_copy(src(ins, i, mine), outs[i].at[mine], loc.at[i]).start()
            for j, (qx, qy) in enumerate(chips):
                _remote(src(ins, i, _chip_slot(qx, qy, swaps[i])), outs[i].at[mine], send.at[3 * i + j],
                        recv.at[3 * i + j], (qx, qy, c)).start()

    def finish(ins, outs, sems):
        send, recv, loc = sems
        x, y, c, chips = _place()
        for i in range(n):
            mine = _chip_slot(x, y, swaps[i])
            for j, (qx, qy) in enumerate(chips):
                theirs = _chip_slot(qx, qy, swaps[i])
                cp = _remote(src(ins, i, theirs), outs[i].at[theirs], send.at[3 * i + j], recv.at[3 * i + j], (qx, qy, c))
                cp.wait_send()
                cp.wait_recv()
            pltpu.make_async_copy(src(ins, i, mine), outs[i].at[mine], loc.at[i]).wait()

    def out_shape(a, r):
        shape = a.shape if r is None else (a.shape[0], r[1] - r[0]) + tuple(a.shape[2:])
        return jax.ShapeDtypeStruct(shape, a.dtype)

    out_shapes = [out_shape(a, r) for a, r in zip(chip_sums, rows)]
    sems = [pltpu.SemaphoreType.DMA((3 * n,)), pltpu.SemaphoreType.DMA((3 * n,)), pltpu.SemaphoreType.DMA((n,))]
    return _Comm(list(chip_sums), out_shapes, sems, start, finish)


def _join_comm(a, b):
    na_i, na_o, na_s = len(a.inputs), len(a.out_shapes), len(a.sems)

    def start(ins, outs, sems):
        a.start(ins[:na_i], outs[:na_o], sems[:na_s])
        b.start(ins[na_i:], outs[na_o:], sems[na_s:])

    def finish(ins, outs, sems):
        a.finish(ins[:na_i], outs[:na_o], sems[:na_s])
        b.finish(ins[na_i:], outs[na_o:], sems[na_s:])

    return _Comm(a.inputs + b.inputs, a.out_shapes + b.out_shapes, a.sems + b.sems, start, finish)


def _row_tile(r):
    for cand in (256, 128):
        if r > cand and r % cand == 0:
            return cand
    return r


def _add_pairs(name, slots, sib):
    r, c = slots.shape[1:]
    tr = _row_tile(r)

    def body(core_ref, s_ref, b_ref, o_ref):
        o_ref[...] = (s_ref[...].astype(F32) + b_ref[...].astype(F32)).astype(o_ref.dtype)

    core = jnp.full((1,), lax.axis_index("c"), jnp.int32)
    return pl.pallas_call(
        body, name=name,
        grid_spec=pltpu.PrefetchScalarGridSpec(
            num_scalar_prefetch=1, grid=(4, r // tr),
            in_specs=[pl.BlockSpec((None, tr, c), lambda q, i, core_ref: (2 * q + core_ref[0], i, 0)),
                      pl.BlockSpec((None, tr, c), lambda q, i, core_ref: (q, i, 0))],
            out_specs=pl.BlockSpec((None, tr, c), lambda q, i, core_ref: (q, i, 0))),
        out_shape=jax.ShapeDtypeStruct((4, r, c), slots.dtype),
        compiler_params=_params("parallel", "parallel"))(core, slots, sib)


def _matmul(name, mode, a, b, grid, a_spec, b_spec, o_spec, out_shape, acc_shape,
            res=None, res_spec=None, alpha=1.0, comm=None):
    nk = grid[-1]
    has_res = res is not None

    def body(*refs):
        if has_res:
            a_ref, b_ref, r_ref, o_ref = refs[:4]
        else:
            a_ref, b_ref, o_ref = refs[:3]
            r_ref = None
        part = _dot(a_ref[...], b_ref[...], mode)

        def finish(v):
            if alpha != 1.0:
                v = v * alpha
            if has_res:
                v = r_ref[...] + v
            o_ref[...] = v.astype(o_ref.dtype)

        if nk == 1:
            finish(part)
        else:
            acc = refs[-1]
            k = pl.program_id(len(grid) - 1)

            @pl.when(k == 0)
            def _():
                acc[...] = part

            @pl.when(k > 0)
            def _():
                acc[...] += part

            @pl.when(k == nk - 1)
            def _():
                finish(acc[...])

    in_specs = [a_spec, b_spec] + ([res_spec] if has_res else [])
    args = (a, b) + ((res,) if has_res else ())
    scratch = [] if nk == 1 else [pltpu.VMEM(acc_shape, F32)]
    sem = ("parallel",) * (len(grid) - 1) + ("arbitrary",)
    out, couts = _pcall(name, body, grid, in_specs, o_spec, out_shape, args, scratch, sem, comm)
    return out if comm is None else (out, couts)


def _mm_nn(name, a, b, out_dtype=F32, res=None, alpha=1.0, tk=None, kk=None, a_off=0, b_off=0, comm=None):
    t = a.shape[0]
    kk = a.shape[1] if kk is None else kk
    n = b.shape[1]
    tk = kk if tk is None else tk
    grid = (t // TM, 1, kk // tk)
    return _matmul(
        name, "nn", a, b, grid,
        pl.BlockSpec((TM, tk), lambda i, j, k: (i, k + a_off)),
        pl.BlockSpec((tk, n), lambda i, j, k: (k + b_off, 0)),
        pl.BlockSpec((TM, n), lambda i, j, k: (i, 0)),
        jax.ShapeDtypeStruct((t, n), out_dtype), (TM, n),
        res=res, res_spec=pl.BlockSpec((TM, n), lambda i, j, k: (i, 0)), alpha=alpha, comm=comm)


def _mm_nt(name, a, b, n=None, tn=None, tk=None, out_dtype=F32, comm=None):
    t, kk = a.shape
    n = b.shape[0] if n is None else n
    tn = n if tn is None else tn
    tk = kk if tk is None else tk
    grid = (n // tn, t // TM, kk // tk)
    return _matmul(
        name, "nt", a, b, grid,
        pl.BlockSpec((TM, tk), lambda j, i, k: (i, k)),
        pl.BlockSpec((tn, tk), lambda j, i, k: (j, k)),
        pl.BlockSpec((TM, tn), lambda j, i, k: (i, j)),
        jax.ShapeDtypeStruct((t, n), out_dtype), (TM, tn), comm=comm)


def _mm_tn(name, a, b, out_dtype, tm=None, n=None, col_off=0, comm=None):
    t, m = a.shape
    n = b.shape[1] if n is None else n
    tm = m if tm is None else tm
    tk = t if t <= TN_MAX_TOKENS else TM
    grid = (m // tm, 1, t // tk)
    return _matmul(
        name, "tn", a, b, grid,
        pl.BlockSpec((tk, tm), lambda j, i, k: (k, j)),
        pl.BlockSpec((tk, n), lambda j, i, k: (k, col_off)),
        pl.BlockSpec((tm, n), lambda j, i, k: (j, 0)),
        jax.ShapeDtypeStruct((m, n), out_dtype), (tm, n), comm=comm)


def _rms_fwd(name, x, w):
    t, d = x.shape

    def body(x_ref, w_ref, h_ref):
        xv = x_ref[...]
        rstd = lax.rsqrt(jnp.mean(xv * xv, axis=-1, keepdims=True) + NORM_EPS)
        h_ref[...] = (xv * rstd * w_ref[...]).astype(h_ref.dtype)

    return pl.pallas_call(
        body, name=name, grid=(t // TE,),
        in_specs=[pl.BlockSpec((TE, d), lambda i: (i, 0)), pl.BlockSpec((1, d), lambda i: (0, 0))],
        out_specs=pl.BlockSpec((TE, d), lambda i: (i, 0)),
        out_shape=jax.ShapeDtypeStruct((t, d), BF), compiler_params=_params("parallel"))(x, w)


def _rms_bwd(name, x, w, dh, dres, out_scale, comm=None):
    t, d = x.shape

    def body(x_ref, w_ref, dh_ref, dres_ref, dx_ref, dxb_ref, dw_ref):
        i = pl.program_id(0)
        xv = x_ref[...]
        rstd = lax.rsqrt(jnp.mean(xv * xv, axis=-1, keepdims=True) + NORM_EPS)
        xhat = xv * rstd
        dhv = dh_ref[...]
        wd = dhv * w_ref[...]
        proj = jnp.mean(wd * xhat, axis=-1, keepdims=True)
        dx = dres_ref[...] + rstd * (wd - xhat * proj)
        dx_ref[...] = dx
        dxb_ref[...] = (dx * out_scale).astype(BF)
        part = jnp.sum(dhv * xhat, axis=0, keepdims=True)

        @pl.when(i == 0)
        def _():
            dw_ref[...] = part

        @pl.when(i > 0)
        def _():
            dw_ref[...] += part

    row = pl.BlockSpec((TE, d), lambda i: (i, 0))
    vec = pl.BlockSpec((1, d), lambda i: (0, 0))
    outs, couts = _pcall(
        name, body, (t // TE,), [row, vec, row, row], [row, row, vec],
        [jax.ShapeDtypeStruct((t, d), F32), jax.ShapeDtypeStruct((t, d), BF), jax.ShapeDtypeStruct((1, d), F32)],
        (x, w, dh, dres), (), ("arbitrary",), comm)
    return outs if comm is None else (outs, couts)


def _final_loss(x, w, target):
    t, d = x.shape

    def body(x_ref, w_ref, t_ref, loss_ref, dx_ref, dxb_ref, dw_ref):
        i = pl.program_id(0)
        xv = x_ref[...]
        rstd = lax.rsqrt(jnp.mean(xv * xv, axis=-1, keepdims=True) + NORM_EPS)
        xhat = xv * rstd
        err = xhat * w_ref[...] - t_ref[...]
        lpart = 0.5 * jnp.sum(jnp.mean(err * err, axis=-1, keepdims=True), axis=0, keepdims=True)
        dy = err * (1.0 / d)
        wd = dy * w_ref[...]
        proj = jnp.mean(wd * xhat, axis=-1, keepdims=True)
        dx = rstd * (wd - xhat * proj)
        dx_ref[...] = dx
        dxb_ref[...] = (0.5 * dx).astype(BF)
        part = jnp.sum(dy * xhat, axis=0, keepdims=True)
        lfull = jnp.broadcast_to(lpart, (1, 128))

        @pl.when(i == 0)
        def _():
            dw_ref[...] = part
            loss_ref[...] = lfull

        @pl.when(i > 0)
        def _():
            dw_ref[...] += part
            loss_ref[...] += lfull

    row = pl.BlockSpec((TE, d), lambda i: (i, 0))
    vec = pl.BlockSpec((1, d), lambda i: (0, 0))
    return pl.pallas_call(
        body, name="final_loss", grid=(t // TE,), in_specs=[row, vec, row],
        out_specs=[pl.BlockSpec((1, 128), lambda i: (0, 0)), row, row, vec],
        out_shape=[jax.ShapeDtypeStruct((1, 128), F32), jax.ShapeDtypeStruct((t, d), F32),
                   jax.ShapeDtypeStruct((t, d), BF), jax.ShapeDtypeStruct((1, d), F32)],
        compiler_params=_params("arbitrary"))(x, w, target)


def _swiglu_fwd(name, gu, comm=None):
    t = gu.shape[0]

    def body(g_ref, u_ref, a_ref):
        g = g_ref[...].astype(F32)
        a_ref[...] = (g * _sigmoid(g) * u_ref[...].astype(F32)).astype(BF)

    blk = (TE, FF_HALF)
    out, couts = _pcall(
        name, body, (t // TE, 2),
        [pl.BlockSpec(blk, lambda i, j: (i, 2 * j)), pl.BlockSpec(blk, lambda i, j: (i, 2 * j + 1))],
        pl.BlockSpec(blk, lambda i, j: (i, j)), jax.ShapeDtypeStruct((t, D_FF), BF),
        (gu, gu), (), ("parallel", "parallel"), comm)
    return out if comm is None else (out, couts)


def _swiglu_bwd(name, gu, dact, comm=None):
    t = gu.shape[0]

    def body(g_ref, u_ref, da_ref, o_ref):
        g = g_ref[...].astype(F32)
        da = da_ref[...].astype(F32)
        s = _sigmoid(g)
        o_ref[:, 0:FF_HALF] = (da * u_ref[...].astype(F32) * (s * (1.0 + g * (1.0 - s)))).astype(BF)
        o_ref[:, FF_HALF:2 * FF_HALF] = (da * g * s).astype(BF)

    blk = (TE, FF_HALF)
    out, couts = _pcall(
        name, body, (t // TE, 2),
        [pl.BlockSpec(blk, lambda i, j: (i, 2 * j)), pl.BlockSpec(blk, lambda i, j: (i, 2 * j + 1)),
         pl.BlockSpec(blk, lambda i, j: (i, j))],
        pl.BlockSpec((TE, 2 * FF_HALF), lambda i, j: (i, j)),
        jax.ShapeDtypeStruct((t, 2 * D_FF), BF), (gu, gu, dact), (), ("parallel", "parallel"), comm)
    return out if comm is None else (out, couts)


CONV_CB = 256


CONV_ROWS = 64
CONV_HALO = 16


def _taps_down(ext, w, k):
    shifted = [pltpu.roll(ext, k - 1 - j, 0)[CONV_HALO:] for j in range(k - 1)] + [ext[CONV_HALO:]]
    out = shifted[k - 1] * w[k - 1:k, :]
    for j in range(k - 1):
        out = out + shifted[j] * w[j:j + 1, :]
    return out, shifted


def _taps_up(ext, w, k):
    rows = ext.shape[0]
    n = rows - CONV_HALO
    out = ext[:n] * w[k - 1:k, :]
    for j in range(k - 1):
        out = out + pltpu.roll(ext, rows - (k - 1 - j), 0)[:n] * w[j:j + 1, :]
    return out


def _rows_before(ref, i, r0):
    start = pl.multiple_of(jnp.maximum(r0 - CONV_HALO, 0), CONV_HALO)
    return jnp.where(i > 0, ref[pl.ds(start, CONV_HALO), :].astype(F32), 0.0)


def _rows_after(ref, r0, t):
    start = pl.multiple_of(jnp.minimum(r0 + CONV_ROWS, t - CONV_HALO), CONV_HALO)
    return ref[pl.ds(start, CONV_HALO), :].astype(F32)


def _fold8(v):
    return v.reshape(v.shape[0] // 8, 8, v.shape[1]).sum(axis=0)


def _silu_grad(pre):
    s = _sigmoid(pre)
    return s * (1.0 + pre * (1.0 - s))


def _pspec(t, off):
    base = off // CONV_CB
    return pl.BlockSpec((t, CONV_CB), lambda j: (0, base + j))


def _mix_a_fwd(p, conv_w):
    t = p.shape[0]

    def body(b_ref, c_ref, xa_ref, w_ref, o_ref):
        w = w_ref[...]

        def step(i, carry):
            r0 = pl.multiple_of(i * CONV_ROWS, CONV_ROWS)
            rows = pl.ds(r0, CONV_ROWS)
            q = c_ref[rows, :].astype(F32) * xa_ref[rows, :].astype(F32)
            q_before = _rows_before(c_ref, i, r0) * _rows_before(xa_ref, i, r0)
            va, _ = _taps_down(jnp.concatenate([q_before, q], axis=0), w, 3)
            o_ref[rows, :] = (b_ref[rows, :].astype(F32) * va).astype(BF)
            return carry

        lax.fori_loop(0, t // CONV_ROWS, step, 0)

    return pl.pallas_call(
        body, name="mix_a_fwd", grid=(D_MODEL // CONV_CB,),
        in_specs=[_pspec(t, OFF_B), _pspec(t, OFF_C), _pspec(t, OFF_XA),
                  pl.BlockSpec((3, CONV_CB), lambda j: (0, j))],
        out_specs=pl.BlockSpec((t, CONV_CB), lambda j: (0, j)),
        out_shape=jax.ShapeDtypeStruct((t, D_MODEL), BF), compiler_params=_params("parallel"))(p, p, p, conv_w)


def _mix_a_bwd(p, conv_w, dya, dp):
    t = p.shape[0]

    def body(b_ref, c_ref, xa_ref, w_ref, dy_ref, dp_in, dp_ref, dw_ref):
        del dp_in
        w = w_ref[...]
        n = t // CONV_ROWS

        def step(i, acc):
            r0 = pl.multiple_of(i * CONV_ROWS, CONV_ROWS)
            rows = pl.ds(r0, CONV_ROWS)
            cv = c_ref[rows, :].astype(F32)
            xav = xa_ref[rows, :].astype(F32)
            q_before = _rows_before(c_ref, i, r0) * _rows_before(xa_ref, i, r0)
            va, shifted = _taps_down(jnp.concatenate([q_before, cv * xav], axis=0), w, 3)
            dyv = dy_ref[rows, :]
            dp_ref[rows, 0:CONV_CB] = (dyv * va).astype(BF)
            dv = dyv * b_ref[rows, :].astype(F32)
            dv_after = jnp.where(i < n - 1, _rows_after(dy_ref, r0, t) * _rows_after(b_ref, r0, t), 0.0)
            dq = _taps_up(jnp.concatenate([dv, dv_after], axis=0), w, 3)
            dp_ref[rows, CONV_CB:2 * CONV_CB] = (dq * xav).astype(BF)
            dp_ref[rows, 2 * CONV_CB:3 * CONV_CB] = (dq * cv).astype(BF)
            return tuple(a + _fold8(dv * s) for a, s in zip(acc, shifted))

        zero = jnp.zeros((8, CONV_CB), F32)
        acc = lax.fori_loop(0, n, step, (zero, zero, zero))
        for j in range(3):
            dw_ref[j:j + 1, :] = jnp.sum(acc[j], axis=0, keepdims=True)

    col = pl.BlockSpec((t, CONV_CB), lambda j: (0, j))
    wsp = pl.BlockSpec((3, CONV_CB), lambda j: (0, j))
    return pl.pallas_call(
        body, name="mix_a_bwd", grid=(D_MODEL // CONV_CB,),
        in_specs=[_pspec(t, OFF_B), _pspec(t, OFF_C), _pspec(t, OFF_XA), wsp, col, pl.BlockSpec(memory_space=pl.ANY)],
        out_specs=[pl.BlockSpec((t, 3 * CONV_CB), lambda j: (0, j)), wsp],
        out_shape=[jax.ShapeDtypeStruct(dp.shape, dp.dtype), jax.ShapeDtypeStruct((3, D_MODEL), F32)],
        input_output_aliases={5: 0},
        compiler_params=_params("parallel"))(p, p, p, conv_w, dya, dp)


def _ssm_conv_fwd(p, conv_w, conv_b, comm=None):
    t = p.shape[0]

    def body(x_ref, w_ref, b_ref, o_ref):
        w = w_ref[...]
        bias = b_ref[...]

        def step(i, carry):
            r0 = pl.multiple_of(i * CONV_ROWS, CONV_ROWS)
            rows = pl.ds(r0, CONV_ROWS)
            ext = jnp.concatenate([_rows_before(x_ref, i, r0), x_ref[rows, :].astype(F32)], axis=0)
            pre = _taps_down(ext, w, 4)[0] + bias
            o_ref[rows, :] = pre * _sigmoid(pre)
            return carry

        lax.fori_loop(0, t // CONV_ROWS, step, 0)

    out, couts = _pcall(
        "ssm_conv_fwd", body, (D_XBC // CONV_CB,),
        [_pspec(t, OFF_XBC), pl.BlockSpec((4, CONV_CB), lambda j: (0, j)), pl.BlockSpec((1, CONV_CB), lambda j: (0, j))],
        pl.BlockSpec((t, CONV_CB), lambda j: (0, j)), jax.ShapeDtypeStruct((t, D_XBC), F32),
        (p, conv_w, conv_b), (), ("parallel",), comm)
    return out if comm is None else (out, couts)


def _ssm_conv_bwd(p, conv_w, conv_b, dxc, dp, comm=None):
    t = p.shape[0]

    def body(x_ref, w_ref, b_ref, d_ref, dp_in, dx_ref, dw_ref, db_ref):
        del dp_in
        w = w_ref[...]
        bias = b_ref[...]
        n = t // CONV_ROWS

        def step(i, acc):
            r0 = pl.multiple_of(i * CONV_ROWS, CONV_ROWS)
            rows = pl.ds(r0, CONV_ROWS)
            x_cur = x_ref[rows, :].astype(F32)
            pre, shifted = _taps_down(jnp.concatenate([_rows_before(x_ref, i, r0), x_cur], axis=0), w, 4)
            pre = pre + bias
            dpre = d_ref[rows, :] * _silu_grad(pre)
            ext_after = jnp.concatenate([x_cur[CONV_ROWS - CONV_HALO:], _rows_after(x_ref, r0, t)], axis=0)
            pre_after = _taps_down(ext_after, w, 4)[0] + bias
            dpre_after = jnp.where(i < n - 1, _rows_after(d_ref, r0, t) * _silu_grad(pre_after), 0.0)
            dx_ref[rows, :] = _taps_up(jnp.concatenate([dpre, dpre_after], axis=0), w, 4).astype(BF)
            new = tuple(a + _fold8(dpre * s) for a, s in zip(acc[:4], shifted))
            return new + (acc[4] + _fold8(dpre),)

        zero = jnp.zeros((8, CONV_CB), F32)
        acc = lax.fori_loop(0, n, step, (zero,) * 5)
        for j in range(4):
            dw_ref[j:j + 1, :] = jnp.sum(acc[j], axis=0, keepdims=True)
        db_ref[...] = jnp.sum(acc[4], axis=0, keepdims=True)

    col = pl.BlockSpec((t, CONV_CB), lambda j: (0, j))
    wsp = pl.BlockSpec((4, CONV_CB), lambda j: (0, j))
    bsp = pl.BlockSpec((1, CONV_CB), lambda j: (0, j))
    outs, couts = _pcall(
        "ssm_conv_bwd", body, (D_XBC // CONV_CB,),
        [_pspec(t, OFF_XBC), wsp, bsp, col, pl.BlockSpec(memory_space=pl.ANY)], [_pspec(t, OFF_XBC), wsp, bsp],
        [jax.ShapeDtypeStruct(dp.shape, dp.dtype), jax.ShapeDtypeStruct((4, D_XBC), F32),
         jax.ShapeDtypeStruct((1, D_XBC), F32)],
        (p, conv_w, conv_b, dxc, dp), (), ("parallel",), comm, aliases={4: 0})
    return outs if comm is None else (outs, couts)


DT_ROWS = 512


def _tri(lower):
    r = lax.broadcasted_iota(jnp.int32, (CHUNK, CHUNK), 0)
    c = lax.broadcasted_iota(jnp.int32, (CHUNK, CHUNK), 1)
    return jnp.where((r >= c) if lower else (r <= c), 1.0, 0.0).astype(F32)


def _dot_exact(a, b):
    return lax.dot_general(a, b, _DIMS["nn"], preferred_element_type=F32, precision=lax.Precision.HIGHEST)


def _dt_fwd(p, bias_pad, alog_pad):
    t = p.shape[0]

    def body(raw_ref, b_ref, al_ref, dt_ref, acs_ref):
        z = raw_ref[...] + b_ref[...]
        dt = jnp.maximum(z, 0.0) + jnp.log(1.0 + jnp.exp(-jnp.abs(z)))
        dt_ref[...] = dt
        a = dt * (-jnp.exp(al_ref[...]))
        tri = _tri(True)
        for k in range(DT_ROWS // CHUNK):
            acs_ref[k * CHUNK:(k + 1) * CHUNK, :] = _dot_exact(tri, a[k * CHUNK:(k + 1) * CHUNK, :])

    blk = pl.BlockSpec((DT_ROWS, DT_W), lambda i: (i, 0))
    vec = pl.BlockSpec((1, DT_W), lambda i: (0, 0))
    return pl.pallas_call(
        body, name="dt_fwd", grid=(t // DT_ROWS,),
        in_specs=[pl.BlockSpec((DT_ROWS, DT_W), lambda i: (i, OFF_DT // DT_W)), vec, vec],
        out_specs=[blk, blk], out_shape=[jax.ShapeDtypeStruct((t, DT_W), F32)] * 2,
        compiler_params=_params("parallel"))(p, bias_pad, alog_pad)


def _dt_bwd(p, bias_pad, alog_pad, dt, ddt, dacs, dp_gd):
    t = p.shape[0]

    def body(raw_ref, b_ref, al_ref, dt_ref, ddt_ref, dacs_ref, dp_in, draw_ref, db_ref, dal_ref):
        del dp_in
        i = pl.program_id(0)
        acoef = -jnp.exp(al_ref[...])
        triu = _tri(False)
        das = []
        for k in range(DT_ROWS // CHUNK):
            das.append(_dot_exact(triu, dacs_ref[k * CHUNK:(k + 1) * CHUNK, :]))
        da = jnp.concatenate(das, axis=0)
        dtv = dt_ref[...]
        ddt_tot = ddt_ref[...] + da * acoef
        lane = lax.broadcasted_iota(jnp.int32, (DT_ROWS, DT_W), 1)
        draw = jnp.where(lane < N_HEADS, ddt_tot * _sigmoid(raw_ref[...] + b_ref[...]), 0.0)
        draw_ref[...] = draw.astype(BF)
        pb = jnp.sum(draw, axis=0, keepdims=True)
        pa = jnp.sum(da * dtv * acoef, axis=0, keepdims=True)

        @pl.when(i == 0)
        def _():
            db_ref[...] = pb
            dal_ref[...] = pa

        @pl.when(i > 0)
        def _():
            db_ref[...] += pb
            dal_ref[...] += pa

    blk = pl.BlockSpec((DT_ROWS, DT_W), lambda i: (i, 0))
    vec = pl.BlockSpec((1, DT_W), lambda i: (0, 0))
    return pl.pallas_call(
        body, name="dt_bwd", grid=(t // DT_ROWS,),
        in_specs=[pl.BlockSpec((DT_ROWS, DT_W), lambda i: (i, OFF_DT // DT_W)), vec, vec, blk, blk, blk,
                  pl.BlockSpec(memory_space=pl.ANY)],
        out_specs=[pl.BlockSpec((DT_ROWS, DT_W), lambda i: (i, OFF_DT // DT_W)), vec, vec],
        out_shape=[jax.ShapeDtypeStruct(dp_gd.shape, dp_gd.dtype), jax.ShapeDtypeStruct((1, DT_W), F32),
                   jax.ShapeDtypeStruct((1, DT_W), F32)],
        input_output_aliases={6: 0},
        compiler_params=_params("arbitrary"))(p, bias_pad, alog_pad, dt, ddt, dacs, dp_gd)


def _split_dot(z, onehot, terms):
    out = None
    rest = z
    for _ in range(terms):
        piece = rest.astype(BF)
        part = _dot(piece, onehot)
        out = part if out is None else out + part
        rest = rest - piece.astype(F32)
    return out


def _spread_mat():
    row = lax.broadcasted_iota(jnp.int32, (DT_W, D_INNER), 0)
    lane = lax.broadcasted_iota(jnp.int32, (DT_W, D_INNER), 1)
    return jnp.where(row == lane // HEAD_DIM, 1.0, 0.0).astype(BF)


def _gather_mat():
    row = lax.broadcasted_iota(jnp.int32, (D_INNER, DT_W), 0)
    lane = lax.broadcasted_iota(jnp.int32, (D_INNER, DT_W), 1)
    return jnp.where(lane == row // HEAD_DIM, 1.0, 0.0).astype(BF)


def _ssd_masks():
    row = lax.broadcasted_iota(jnp.int32, (CHUNK, GROUP_W), 0)
    col = lax.broadcasted_iota(jnp.int32, (CHUNK, GROUP_W), 1) % HEAD_DIM
    brow = lax.broadcasted_iota(jnp.int32, (GROUP_W, GROUP_W), 0) // HEAD_DIM
    bcol = lax.broadcasted_iota(jnp.int32, (GROUP_W, GROUP_W), 1) // HEAD_DIM
    return row >= col, row == col, brow == bcol


def _stack4(v):
    return jnp.concatenate([v, v, v, v], axis=0)


def _fold4(v):
    return v[0:CHUNK] + v[CHUNK:2 * CHUNK] + v[2 * CHUNK:3 * CHUNK] + v[3 * CHUNK:4 * CHUNK]


def _ssd_group(xc_ref, wide_ref, g, tri, eye, blockdiag):
    gs = slice(GROUP_W * g, GROUP_W * (g + 1))
    xs_g = xc_ref[:, gs]
    b_g = xc_ref[:, D_INNER + D_STATE * g:D_INNER + D_STATE * (g + 1)].astype(BF)
    c_g = xc_ref[:, D_INNER + 1024 + D_STATE * g:D_INNER + 1024 + D_STATE * (g + 1)].astype(BF)
    acs_e, dt_e = wide_ref[0:CHUNK, gs], wide_ref[CHUNK:2 * CHUNK, gs]
    atot_e = acs_e[CHUNK - 1:CHUNK, :]
    acs_j = jnp.sum(jnp.where(eye, acs_e, 0.0), axis=0, keepdims=True)
    lmat = jnp.where(tri, jnp.exp(jnp.minimum(acs_e - acs_j, 0.0)), 0.0)
    b_t = _stack4(b_g)
    m = _dot(c_g, b_t, "nt") * lmat
    x_g = xs_g * dt_e
    xbd = jnp.where(blockdiag, _stack4(x_g), 0.0).astype(BF)
    return dict(gs=gs, xs=xs_g, b=b_g, c=c_g, b_t=b_t, dt=dt_e, e=jnp.exp(acs_e), dec=jnp.exp(atot_e - acs_e),
                eat=jnp.exp(atot_e), lmat=lmat, m=m, x=x_g, xbd=xbd)


def _ssd_fwd(xconv, dt, acs, d_exp, comm=None):
    t = xconv.shape[0]
    nc = t // CHUNK

    def body(xc_ref, dt_ref, acs_ref, d_ref, y_ref, hs_ref, state, wide):
        c = pl.program_id(0)

        @pl.when(c == 0)
        def _():
            state[...] = jnp.zeros_like(state)

        hs_ref[...] = state[...]
        tri, eye, blockdiag = _ssd_masks()
        wide[...] = _split_dot(jnp.concatenate([acs_ref[...], dt_ref[...]], axis=0), _spread_mat(), 3)
        for g in range(N_GROUPS):
            q = _ssd_group(xc_ref, wide, g, tri, eye, blockdiag)
            gs = q["gs"]
            h_t = state[:, gs]
            ydiag = _dot(q["m"].astype(BF), q["xbd"])
            yoff = _dot(q["c"], h_t.astype(BF)) * q["e"]
            y_ref[:, gs] = ydiag + yoff + d_ref[:, gs] * q["xs"]
            s_t = _dot(q["b"], (q["x"] * q["dec"]).astype(BF), "tn")
            state[:, gs] = q["eat"] * h_t + s_t

    blk = lambda w: pl.BlockSpec((CHUNK, w), lambda c: (c, 0))
    outs, couts = _pcall(
        "ssd_fwd", body, (nc,),
        [blk(D_XBC), blk(DT_W), blk(DT_W), pl.BlockSpec((1, D_INNER), lambda c: (0, 0))],
        [blk(D_INNER), pl.BlockSpec((None, D_STATE, D_INNER), lambda c: (c, 0, 0))],
        [jax.ShapeDtypeStruct((t, D_INNER), F32), jax.ShapeDtypeStruct((nc, D_STATE, D_INNER), F32)],
        (xconv, dt, acs, d_exp), [pltpu.VMEM((D_STATE, D_INNER), F32), pltpu.VMEM((2 * CHUNK, D_INNER), F32)],
        ("arbitrary",), comm)
    return outs if comm is None else (outs, couts)


def _ssd_bwd(xconv, dt, acs, d_exp, hsave, dy, comm=None):
    t = xconv.shape[0]
    nc = t // CHUNK

    def body(xc_ref, dt_ref, acs_ref, d_ref, hs_ref, dy_ref, dxc_ref, ddt_ref, dacs_ref, dd_ref, dstate, wide, per_head):
        c = pl.program_id(0)

        @pl.when(c == 0)
        def _():
            dstate[...] = jnp.zeros_like(dstate)
            dd_ref[...] = jnp.zeros_like(dd_ref)

        tri, eye, blockdiag = _ssd_masks()
        acsv = acs_ref[...]
        wide[...] = _split_dot(jnp.concatenate([acsv, dt_ref[...]], axis=0), _spread_mat(), 3)
        eat_heads = jnp.exp(acsv[CHUNK - 1:CHUNK, :])

        for g in range(N_GROUPS):
            q = _ssd_group(xc_ref, wide, g, tri, eye, blockdiag)
            gs, xs_g, b_g, c_g, m = q["gs"], q["xs"], q["b"], q["c"], q["m"]
            bs = slice(D_INNER + D_STATE * g, D_INNER + D_STATE * (g + 1))
            cs = slice(D_INNER + 1024 + D_STATE * g, D_INNER + 1024 + D_STATE * (g + 1))
            h_t = hs_ref[:, gs]
            h_b = h_t.astype(BF)
            dy_g = dy_ref[:, gs]
            dy_b = dy_g.astype(BF)
            ds_t = dstate[:, gs]
            ds_b = ds_t.astype(BF)

            yoff = _dot(c_g, h_b) * q["e"]
            edy = (q["e"] * dy_g).astype(BF)
            d_c = _dot(edy, h_b, "nt")
            d_ht = _dot(c_g, edy, "tn")
            bds = _dot(b_g, ds_b)
            xd = q["x"] * q["dec"]
            d_b = _dot(xd.astype(BF), ds_b, "nt")
            dm = _dot(dy_b, q["xbd"], "nt")
            cross = _dot(m.astype(BF), dy_b, "tn")
            dx_full = q["dec"] * bds + _fold4(jnp.where(blockdiag, cross, 0.0))
            dml = (dm * q["lmat"]).astype(BF)
            d_c = d_c + _dot(dml, q["b_t"])
            d_b = d_b + _fold4(_dot(dml, c_g, "tn"))
            w = dm * m
            q_dec = xd * bds
            z = w - jnp.where(eye, jnp.sum(w, axis=0, keepdims=True), 0.0) + dy_g * yoff - q_dec
            rows = jnp.concatenate(
                [jnp.sum(q_dec, axis=0, keepdims=True), jnp.sum(ds_t * h_t, axis=0, keepdims=True),
                 jnp.zeros((6, GROUP_W), F32)], axis=0)
            per_head[:, gs] = jnp.concatenate([z, dx_full * xs_g, rows], axis=0)
            dxc_ref[:, cs] = d_c
            dxc_ref[:, bs] = d_b
            dxc_ref[:, gs] = dx_full * q["dt"] + d_ref[:, gs] * dy_g
            dd_ref[:, gs] += jnp.sum(dy_g * xs_g, axis=0, keepdims=True)
            dstate[:, gs] = q["eat"] * ds_t + d_ht

        seg = _split_dot(per_head[...], _gather_mat(), 2)
        datot = seg[2 * CHUNK:2 * CHUNK + 1] + eat_heads * seg[2 * CHUNK + 1:2 * CHUNK + 2]
        rowi = lax.broadcasted_iota(jnp.int32, (CHUNK, DT_W), 0)
        ddt_ref[...] = seg[CHUNK:2 * CHUNK]
        dacs_ref[...] = seg[0:CHUNK] + jnp.where(rowi == CHUNK - 1, datot, 0.0)

    rev = lambda w: pl.BlockSpec((CHUNK, w), lambda c: (nc - 1 - c, 0))
    vec = pl.BlockSpec((1, D_INNER), lambda c: (0, 0))
    outs, couts = _pcall(
        "ssd_bwd", body, (nc,),
        [rev(D_XBC), rev(DT_W), rev(DT_W), vec,
         pl.BlockSpec((None, D_STATE, D_INNER), lambda c: (nc - 1 - c, 0, 0)), rev(D_INNER)],
        [rev(D_XBC), rev(DT_W), rev(DT_W), vec],
        [jax.ShapeDtypeStruct((t, D_XBC), F32), jax.ShapeDtypeStruct((t, DT_W), F32),
         jax.ShapeDtypeStruct((t, DT_W), F32), jax.ShapeDtypeStruct((1, D_INNER), F32)],
        (xconv, dt, acs, d_exp, hsave, dy),
        [pltpu.VMEM((D_STATE, D_INNER), F32), pltpu.VMEM((2 * CHUNK, D_INNER), F32),
         pltpu.VMEM((2 * CHUNK + 8, D_INNER), F32)], ("arbitrary",), comm)
    return outs if comm is None else (outs, couts)


GN_CB = 1024
GN_GROUPS = GN_CB // GROUP_W


def _gnorm_fwd(y, p, w, comm=None):
    t = y.shape[0]
    zoff = OFF_Z // GN_CB

    def body(y_ref, z_ref, w_ref, o_ref):
        for g in range(GN_GROUPS):
            gs = slice(GROUP_W * g, GROUP_W * (g + 1))
            z = z_ref[:, gs].astype(F32)
            yf = y_ref[:, gs] * (z * _sigmoid(z))
            rstd = lax.rsqrt(jnp.mean(yf * yf, axis=-1, keepdims=True) + NORM_EPS)
            o_ref[:, gs] = (yf * rstd * w_ref[:, gs]).astype(BF)

    blk = pl.BlockSpec((TE, GN_CB), lambda i, j: (i, j))
    out, couts = _pcall(
        "gnorm_fwd", body, (t // TE, D_INNER // GN_CB),
        [blk, pl.BlockSpec((TE, GN_CB), lambda i, j: (i, zoff + j)), pl.BlockSpec((1, GN_CB), lambda i, j: (0, j))],
        blk, jax.ShapeDtypeStruct((t, D_INNER), BF), (y, p, w), (), ("parallel", "parallel"), comm)
    return out if comm is None else (out, couts)


def _gnorm_bwd(y, p, w, dyn, comm=None):
    t = y.shape[0]
    zoff = OFF_Z // GN_CB

    def body(y_ref, z_ref, w_ref, dn_ref, dy_ref, dz_ref, dw_ref):
        i = pl.program_id(1)
        for g in range(GN_GROUPS):
            gs = slice(GROUP_W * g, GROUP_W * (g + 1))
            z = z_ref[:, gs].astype(F32)
            yv = y_ref[:, gs]
            s = _sigmoid(z)
            sil = z * s
            yf = yv * sil
            rstd = lax.rsqrt(jnp.mean(yf * yf, axis=-1, keepdims=True) + NORM_EPS)
            xhat = yf * rstd
            dn = dn_ref[:, gs]
            wd = dn * w_ref[:, gs]
            proj = jnp.mean(wd * xhat, axis=-1, keepdims=True)
            dyf = rstd * (wd - xhat * proj)
            dy_ref[:, gs] = dyf * sil
            dz_ref[:, gs] = (dyf * yv * (s * (1.0 + z * (1.0 - s)))).astype(BF)
            part = jnp.sum(dn * xhat, axis=0, keepdims=True)

            @pl.when(i == 0)
            def _():
                dw_ref[:, gs] = part

            @pl.when(i > 0)
            def _():
                dw_ref[:, gs] += part

    blk = pl.BlockSpec((TE, GN_CB), lambda j, i: (i, j))
    vec = pl.BlockSpec((1, GN_CB), lambda j, i: (0, j))
    outs, couts = _pcall(
        "gnorm_bwd", body, (D_INNER // GN_CB, t // TE),
        [blk, pl.BlockSpec((TE, GN_CB), lambda j, i: (i, zoff + j)), vec, blk],
        [blk, pl.BlockSpec((TE, GN_CB), lambda j, i: (i, zoff + j)), vec],
        [jax.ShapeDtypeStruct((t, D_INNER), F32), jax.ShapeDtypeStruct((t, N_MAIN), BF),
         jax.ShapeDtypeStruct((1, D_INNER), F32)],
        (y, p, w, dyn), (), ("parallel", "arbitrary"), comm)
    return outs if comm is None else (outs, couts)


MERGE_CB = 512


def _merge_fwd(p, ya, yb):
    t = ya.shape[0]

    def body(ga_ref, gb_ref, ya_ref, yb_ref, o_ref):
        o_ref[...] = (_sigmoid(ga_ref[...]) * ya_ref[...] + _sigmoid(gb_ref[...]) * yb_ref[...]).astype(BF)

    blk = pl.BlockSpec((TE, MERGE_CB), lambda i, j: (i, j))
    return pl.pallas_call(
        body, name="merge_fwd", grid=(t // TE, D_MODEL // MERGE_CB),
        in_specs=[pl.BlockSpec((TE, MERGE_CB), lambda i, j: (i, 2 * j)),
                  pl.BlockSpec((TE, MERGE_CB), lambda i, j: (i, 2 * j + 1)), blk, blk],
        out_specs=blk, out_shape=jax.ShapeDtypeStruct((t, D_MODEL), BF),
        compiler_params=_params("parallel", "parallel"))(p, p, ya, yb)


def _merge_bwd(p, ya, yb, dm):
    t = ya.shape[0]

    def body(ga_ref, gb_ref, ya_ref, yb_ref, dm_ref, dg_ref, dya_ref, dyb_ref):
        d = dm_ref[...]
        sa = _sigmoid(ga_ref[...])
        sb = _sigmoid(gb_ref[...])
        dg_ref[:, 0:MERGE_CB] = (d * ya_ref[...] * sa * (1.0 - sa)).astype(BF)
        dg_ref[:, MERGE_CB:2 * MERGE_CB] = (d * yb_ref[...] * sb * (1.0 - sb)).astype(BF)
        dya_ref[...] = (d * sa).astype(BF)
        dyb_ref[...] = (d * sb).astype(BF)

    blk = pl.BlockSpec((TE, MERGE_CB), lambda i, j: (i, j))
    return pl.pallas_call(
        body, name="merge_bwd", grid=(t // TE, D_MODEL // MERGE_CB),
        in_specs=[pl.BlockSpec((TE, MERGE_CB), lambda i, j: (i, 2 * j)),
                  pl.BlockSpec((TE, MERGE_CB), lambda i, j: (i, 2 * j + 1)), blk, blk, blk],
        out_specs=[pl.BlockSpec((TE, 2 * MERGE_CB), lambda i, j: (i, j)), blk, blk],
        out_shape=[jax.ShapeDtypeStruct((t, N_GD), BF)] + [jax.ShapeDtypeStruct((t, D_MODEL), BF)] * 2,
        compiler_params=_params("parallel", "parallel"))(p, p, ya, yb, dm)


def _adamw(name, parts, w, m, v, comm=None):
    r, c = w.shape
    tr = _row_tile(r)
    tc = ADAM_COL_TILE if (tr == r and r > 512 and c % ADAM_COL_TILE == 0) else c
    n_parts = parts.shape[0]
    bc1 = 1.0 - ADAM_B1 ** ADAM_STEP
    bc2 = 1.0 - ADAM_B2 ** ADAM_STEP

    def body(p_ref, w_ref, m_ref, v_ref, g_ref, d_ref, nm_ref, nv_ref):
        g = p_ref[0].astype(F32)
        for k in range(1, n_parts):
            g = g + p_ref[k].astype(F32)
        nm = ADAM_B1 * m_ref[...] + (1.0 - ADAM_B1) * g
        nv = ADAM_B2 * v_ref[...] + (1.0 - ADAM_B2) * (g * g)
        g_ref[...] = g
        nm_ref[...] = nm
        nv_ref[...] = nv
        d_ref[...] = -ADAM_LR * ((nm / bc1) / (jnp.sqrt(nv / bc2) + ADAM_EPS) + ADAM_WD * w_ref[...])

    blk = pl.BlockSpec((tr, tc), lambda i, j: (i, j))
    outs, couts = _pcall(
        name, body, (r // tr, c // tc),
        [pl.BlockSpec((n_parts, tr, tc), lambda i, j: (0, i, j)), blk, blk, blk], [blk] * 4,
        [jax.ShapeDtypeStruct((r, c), F32)] * 4, (parts, w, m, v), (), ("parallel", "parallel"), comm)
    return outs if comm is None else (outs, couts)


def _pad_lanes(v, width):
    return jnp.pad(v, ((0, 0), (0, width - v.shape[1])))


def _reduce_start(slots, host):
    outs, sib = host(_pair_comm([a for _, a in slots]))
    sums = [(n, _add_pairs("pairsum_" + n, a, b)) for (n, a), b in zip(slots, sib)]
    return outs, sums


def _train_step(x, target, shard, rep):
    gdt = BF
    recv = {}
    (got,) = _comm_call("gather_ffn1_in", _gather_comm([shard["ffn1_w_in"]], [True]))
    w1_in = got.reshape(2 * D_FF, D_MODEL)
    h1 = _rms_fwd("rms1_fwd", x, rep["ffn1_norm"])
    gu1, got = _mm_nt("ffn1_in", h1, w1_in, tn=FF_HALF, out_dtype=BF, comm=_gather_comm(
        [shard["ffn1_w_out"], shard["w_in"], shard["short_conv_w"], shard["ssm_conv_w"]]))
    w1_out = got[0].reshape(D_FF, D_MODEL)
    w_in_t = got[1].reshape(N_IN, D_MODEL)
    short_conv_w = got[2].transpose(1, 0, 2).reshape(3, D_MODEL)
    ssm_conv_w = got[3].transpose(1, 0, 2).reshape(4, D_XBC)
    act1 = _swiglu_fwd("swiglu1_fwd", gu1)
    x1 = _mm_nn("ffn1_out", act1, w1_out, res=x, alpha=0.5)
    ga0 = N_MAIN + N_HEADS
    gb0 = ga0 + D_MODEL
    half = D_MODEL // 2
    w_gd = jnp.concatenate(
        [w_in_t[ga0:ga0 + half], w_in_t[gb0:gb0 + half], w_in_t[ga0 + half:gb0], w_in_t[gb0 + half:],
         w_in_t[N_MAIN:N_MAIN + N_HEADS], jnp.zeros((DT_W - N_HEADS, D_MODEL), BF)], axis=0)
    w_mix_perm = w_in_t[0:3 * D_MODEL].reshape(3, 4, CONV_CB, D_MODEL).transpose(1, 0, 2, 3).reshape(3 * D_MODEL, D_MODEL)

    h2 = _rms_fwd("rms2_fwd", x1, rep["mix_norm"])
    p, got = _mm_nt("proj_main", h2, w_in_t, n=N_MAIN, tn=1024, out_dtype=BF, comm=_gather_comm(
        [shard["short_w_out"], shard["ssm_w_out"], shard["w_out"]]))
    p_gd = _mm_nt("proj_gd", h2, w_gd)
    short_w_out = got[0].reshape(D_MODEL, D_MODEL)
    ssm_w_out = got[1].reshape(D_INNER, D_MODEL)
    w_out = got[2].reshape(D_MODEL, D_MODEL)
    ya_in = _mix_a_fwd(p, short_conv_w)
    y_a = _mm_nn("short_out", ya_in, short_w_out)
    xconv, (got,) = _ssm_conv_fwd(p, ssm_conv_w, rep["ssm_conv_b"], comm=_gather_comm([shard["ffn2_w_out"]]))
    w2_out = got.reshape(D_FF, D_MODEL)
    dt, acs = _dt_fwd(p_gd, rep["dt_bias_pad"], rep["a_log_pad"])
    (y_ssm, hsave), (got,) = _ssd_fwd(xconv, dt, acs, rep["d_exp"], comm=_gather_comm([shard["ffn2_w_in"]], [True]))
    w2_in = got.reshape(2 * D_FF, D_MODEL)
    yn = _gnorm_fwd(y_ssm, p, rep["ssm_norm"])
    y_b = _mm_nn("ssm_out", yn, ssm_w_out, tk=1024)
    merged = _merge_fwd(p_gd, y_a, y_b)
    x2 = _mm_nn("mix_out", merged, w_out, res=x1)

    h3 = _rms_fwd("rms3_fwd", x2, rep["ffn2_norm"])
    gu2 = _mm_nt("ffn2_in", h3, w2_in, tn=FF_HALF, out_dtype=BF)
    act2 = _swiglu_fwd("swiglu2_fwd", gu2)
    x3 = _mm_nn("ffn2_out", act2, w2_out, res=x2, alpha=0.5)

    loss, dx3, dx3h, g_final = _final_loss(x3, rep["final_norm"], target)

    small = {"final_norm": g_final}
    dact2 = _mm_nt("ffn2_out_bwd_act", dx3h, w2_out, out_dtype=BF)
    g_w2_out = _mm_tn("ffn2_out_bwd_w", act2, dx3h, gdt, tm=FF_HALF)
    dgu2 = _swiglu_bwd("swiglu2_bwd", gu2, dact2)
    g_w2_in = _mm_tn("ffn2_in_bwd_w", dgu2, h3, gdt, tm=FF_HALF)
    dh3 = _mm_nn("ffn2_in_bwd_h", dgu2, w2_in, tk=FF_HALF)
    dx2, dx2b, small["ffn2_norm"] = _rms_bwd("rms3_bwd", x2, rep["ffn2_norm"], dh3, dx3, 1.0)

    dmerged = _mm_nt("mix_out_bwd_x", dx2b, w_out)
    g_w_out = _mm_tn("mix_out_bwd_w", merged, dx2b, gdt)
    dp_gd, dya, dyb = _merge_bwd(p_gd, y_a, y_b, dmerged)

    dya_in = _mm_nt("short_out_bwd_x", dya, short_w_out)
    g_short_w_out = _mm_tn("short_out_bwd_w", ya_in, dya, gdt)

    dyn = _mm_nt("ssm_out_bwd_x", dyb, ssm_w_out)
    g_ssm_w_out = _mm_tn("ssm_out_bwd_w", yn, dyb, gdt)
    late = [("ffn2_w_out", g_w2_out.reshape(N_DEV, FF_SHARD // 2, D_MODEL)),
            ("ffn2_w_in", g_w2_in.reshape(N_DEV, FF_SHARD, D_MODEL)),
            ("w_out", g_w_out.reshape(N_DEV, -1, D_MODEL)), ("short_w_out", g_short_w_out.reshape(N_DEV, -1, D_MODEL)),
            ("ssm_w_out", g_ssm_w_out.reshape(N_DEV, -1, D_MODEL))]
    (dy_ssm, dp, small["ssm_norm"]), sums = _reduce_start(
        late, lambda comm: _gnorm_bwd(y_ssm, p, rep["ssm_norm"], dyn, comm=comm))
    dp, g_short_conv = _mix_a_bwd(p, short_conv_w, dya_in, dp)
    first = [(n, a) for n, a in sums if n.startswith("ffn2")]
    second = [(n, a) for n, a in sums if not n.startswith("ffn2")]
    (dxconv, ddt, dacs, dd_lane), got = _ssd_bwd(
        xconv, dt, acs, rep["d_exp"], hsave, dy_ssm,
        comm=_chip_comm([a for _, a in first], [n == "ffn2_w_in" for n, _ in first]))
    recv.update({n: a for (n, _), a in zip(first, got)})
    small["ssm_D"] = dd_lane.reshape(N_HEADS, HEAD_DIM).sum(axis=1)[None, :]
    (dp, g_ssm_conv, small["ssm_conv_b"]), got = _ssm_conv_bwd(
        p, ssm_conv_w, rep["ssm_conv_b"], dxconv, dp, comm=_chip_comm([a for _, a in second]))
    recv.update({n: a for (n, _), a in zip(second, got)})
    dp_gd, dbias, dalog = _dt_bwd(p_gd, rep["dt_bias_pad"], rep["a_log_pad"], dt, ddt, dacs, dp_gd)
    small["ssm_dt_bias"] = dbias[:, :N_HEADS]
    small["ssm_A_log"] = dalog[:, :N_HEADS]

    g_main = _mm_tn("proj_main_bwd_w", dp, h2, gdt, tm=1024)
    g_gd = _mm_tn("proj_gd_bwd_w", dp_gd, h2, gdt)
    g_mix = g_main[0:3 * D_MODEL].reshape(4, 3, CONV_CB, D_MODEL).transpose(1, 0, 2, 3).reshape(3 * D_MODEL, D_MODEL)
    g_in_t = jnp.concatenate(
        [g_mix, g_main[3 * D_MODEL:], g_gd[2 * D_MODEL:2 * D_MODEL + N_HEADS],
         g_gd[0:half], g_gd[2 * half:3 * half], g_gd[half:2 * half], g_gd[3 * half:4 * half]], axis=0).reshape(
        N_DEV, IN_SHARD, D_MODEL)
    dh2, w_sums = _reduce_start(
        [("w_in", g_in_t)], lambda comm: _mm_nn("proj_mix_bwd_x", dp, w_mix_perm, tk=1024, kk=3 * D_MODEL, comm=comm))
    w_sum = w_sums[0][1]

    def w_piece(i):
        return _chip_comm([w_sum], rows=[W_GRAD_ROW_CUTS[i]])

    dh2, got0 = _mm_nn("proj_rest_bwd_x", dp, w_in_t, tk=1024, kk=N_MAIN - 3 * D_MODEL, a_off=3, b_off=3, res=dh2,
                       comm=w_piece(0))
    dh2, got1 = _mm_nn("proj_gd_bwd_x", dp_gd, w_gd, res=dh2, comm=w_piece(1))
    (dx1, dx1h, small["mix_norm"]), got2 = _rms_bwd("rms2_bwd", x1, rep["mix_norm"], dh2, dx2, 0.5, comm=w_piece(2))
    g_w1_out, got3 = _mm_tn("ffn1_out_bwd_w", act1, dx1h, gdt, tm=FF_HALF, comm=w_piece(3))
    rest = [("ffn1_w_out", g_w1_out.reshape(N_DEV, FF_SHARD // 2, D_MODEL)),
            ("short_conv_w", g_short_conv.reshape(3, N_DEV, -1).transpose(1, 0, 2)),
            ("ssm_conv_w", g_ssm_conv.reshape(4, N_DEV, -1).transpose(1, 0, 2))]
    dact1, got = _mm_nt("ffn1_out_bwd_act", dx1h, w1_out, out_dtype=BF,
                        comm=_join_comm(w_piece(4), _pair_comm([a for _, a in rest])))
    got4, sib = got[0], got[1:]
    rest_sums = [(n, _add_pairs("pairsum_" + n, a, b)) for (n, a), b in zip(rest, sib)]
    dgu1, got = _swiglu_bwd("swiglu1_bwd", gu1, dact1, comm=_chip_comm([a for _, a in rest_sums]))
    recv.update({n: a for (n, _), a in zip(rest_sums, got)})

    def part(tag, width, off, comm=None):
        out = _mm_tn("ffn1_in_bwd_w_" + tag, dgu1, h1, gdt, tm=FF_HALF, n=width, col_off=off, comm=comm)
        g, couts = (out, None) if comm is None else out
        return g.reshape(N_DEV, FF_SHARD, width), couts

    g_a, (got5,) = part("a", 384, 0, w_piece(5))
    g_b, (got6, sib) = part("b", 384, 1, _join_comm(w_piece(6), _pair_comm([g_a])))
    recv["w_in"] = jnp.concatenate([got0[0], got1[0], got2[0], got3[0], got4, got5, got6], axis=1)
    sum_a = _add_pairs("pairsum_ffn1_w_in_a", g_a, sib)
    g_c, (recv_a, sib_b) = part("c", 256, 3, _join_comm(_chip_comm([sum_a], [True]), _pair_comm([g_b])))
    sum_b = _add_pairs("pairsum_ffn1_w_in_b", g_b, sib_b)
    dh1, (recv_b, sib_c) = _mm_nn("ffn1_in_bwd_h", dgu1, w1_in, tk=FF_HALF,
                                  comm=_join_comm(_chip_comm([sum_b], [True]), _pair_comm([g_c])))
    sum_c = _add_pairs("pairsum_ffn1_w_in_c", g_c, sib_c)
    (dx0, _, small["ffn1_norm"]), (recv_c,) = _rms_bwd("rms1_bwd", x, rep["ffn1_norm"], dh1, dx1, 1.0,
                                                        comm=_chip_comm([sum_c], [True]))
    recv["ffn1_w_in"] = jnp.concatenate([recv_a, recv_b, recv_c], axis=2)
    return dx0, recv, _pack_small(small, loss[:, 0:1])


_SMALL = [("ffn1_norm", 1024), ("mix_norm", 1024), ("ssm_conv_b", 4096), ("ssm_dt_bias", 32), ("ssm_A_log", 32),
          ("ssm_D", 32), ("ssm_norm", 2048), ("ffn2_norm", 1024), ("final_norm", 1024)]
SMALL_W = 10368


def _pack_small(d, loss=None):
    parts = [d[n].reshape(1, -1).astype(F32) for n, _ in _SMALL]
    used = sum(sz for _, sz in _SMALL)
    tail = jnp.zeros((1, SMALL_W - used), F32)
    if loss is not None:
        tail = tail.at[:, 0:1].set(loss)
    return jnp.concatenate(parts + [tail], axis=1)


def _adamw_small(parts, w, m, v):
    n_par = len(_SMALL)
    bc1 = 1.0 - ADAM_B1 ** ADAM_STEP
    bc2 = 1.0 - ADAM_B2 ** ADAM_STEP
    used = sum(sz for _, sz in _SMALL)

    def body(*refs):
        p_ref = refs[0]
        ins = refs[1:1 + 3 * n_par]
        outs = refs[1 + 3 * n_par:]
        g_all = p_ref[0]
        for k in range(1, N_DEV):
            g_all = g_all + p_ref[k]
        off = 0
        for i, (_, sz) in enumerate(_SMALL):
            g = g_all[:, off:off + sz]
            w_ref, m_ref, v_ref = ins[3 * i:3 * i + 3]
            nm = ADAM_B1 * m_ref[...] + (1.0 - ADAM_B1) * g
            nv = ADAM_B2 * v_ref[...] + (1.0 - ADAM_B2) * (g * g)
            outs[4 * i][...] = g
            outs[4 * i + 1][...] = -ADAM_LR * ((nm / bc1) / (jnp.sqrt(nv / bc2) + ADAM_EPS) + ADAM_WD * w_ref[...])
            outs[4 * i + 2][...] = nm
            outs[4 * i + 3][...] = nv
            off += sz
        outs[4 * n_par][...] = g_all[:, used:SMALL_W]

    args = [parts]
    out_shape = []
    for name, sz in _SMALL:
        args += [w[name], m[name], v[name]]
        out_shape += [jax.ShapeDtypeStruct((1, sz), F32)] * 4
    out_shape.append(jax.ShapeDtypeStruct((1, SMALL_W - used), F32))
    res = pl.pallas_call(body, name="adamw_small", out_shape=out_shape,
                         compiler_params=pltpu.CompilerParams(vmem_limit_bytes=VMEM_LIMIT_V7X))(*args)
    return {name: tuple(res[4 * i:4 * i + 4]) for i, (name, _) in enumerate(_SMALL)}, res[-1]


_SHARDED = ["ffn1_w_in", "ffn1_w_out", "w_in", "short_conv_w", "short_w_out", "ssm_conv_w", "ssm_w_out", "w_out",
            "ffn2_w_in", "ffn2_w_out"]
_TRANSPOSED = ("ffn1_w_in", "w_in", "ffn2_w_in")
_ORDER = ["ffn1_norm", "ffn1_w_in", "ffn1_w_out", "mix_norm", "w_in", "short_conv_w", "short_w_out", "ssm_conv_w",
          "ssm_conv_b", "ssm_dt_bias", "ssm_A_log", "ssm_D", "ssm_norm", "ssm_w_out", "w_out", "ffn2_norm",
          "ffn2_w_in", "ffn2_w_out", "final_norm"]


def kernel(x, ffn1_norm, ffn1_w_in, ffn1_w_out, mix_norm, w_in, short_conv_w, short_w_out, ssm_conv_w, ssm_conv_b, ssm_dt_bias, ssm_A_log, ssm_D, ssm_norm, ssm_w_out, w_out, ffn2_norm, ffn2_w_in, ffn2_w_out, final_norm, loss_target, m_ffn1_norm, m_ffn1_w_in, m_ffn1_w_out, m_mix_norm, m_w_in, m_short_conv_w, m_short_w_out, m_ssm_conv_w, m_ssm_conv_b, m_ssm_dt_bias, m_ssm_A_log, m_ssm_D, m_ssm_norm, m_ssm_w_out, m_w_out, m_ffn2_norm, m_ffn2_w_in, m_ffn2_w_out, m_final_norm, v_ffn1_norm, v_ffn1_w_in, v_ffn1_w_out, v_mix_norm, v_w_in, v_short_conv_w, v_short_w_out, v_ssm_conv_w, v_ssm_conv_b, v_ssm_dt_bias, v_ssm_A_log, v_ssm_D, v_ssm_norm, v_ssm_w_out, v_w_out, v_ffn2_norm, v_ffn2_w_in, v_ffn2_w_out, v_final_norm):
    w = dict(ffn1_norm=ffn1_norm, ffn1_w_in=ffn1_w_in, ffn1_w_out=ffn1_w_out, mix_norm=mix_norm, w_in=w_in,
             short_conv_w=short_conv_w, short_w_out=short_w_out, ssm_conv_w=ssm_conv_w, ssm_conv_b=ssm_conv_b,
             ssm_dt_bias=ssm_dt_bias, ssm_A_log=ssm_A_log, ssm_D=ssm_D, ssm_norm=ssm_norm, ssm_w_out=ssm_w_out,
             w_out=w_out, ffn2_norm=ffn2_norm, ffn2_w_in=ffn2_w_in, ffn2_w_out=ffn2_w_out, final_norm=final_norm)
    m = dict(ffn1_norm=m_ffn1_norm, ffn1_w_in=m_ffn1_w_in, ffn1_w_out=m_ffn1_w_out, mix_norm=m_mix_norm, w_in=m_w_in,
             short_conv_w=m_short_conv_w, short_w_out=m_short_w_out, ssm_conv_w=m_ssm_conv_w,
             ssm_conv_b=m_ssm_conv_b, ssm_dt_bias=m_ssm_dt_bias, ssm_A_log=m_ssm_A_log, ssm_D=m_ssm_D,
             ssm_norm=m_ssm_norm, ssm_w_out=m_ssm_w_out, w_out=m_w_out, ffn2_norm=m_ffn2_norm,
             ffn2_w_in=m_ffn2_w_in, ffn2_w_out=m_ffn2_w_out, final_norm=m_final_norm)
    v = dict(ffn1_norm=v_ffn1_norm, ffn1_w_in=v_ffn1_w_in, ffn1_w_out=v_ffn1_w_out, mix_norm=v_mix_norm, w_in=v_w_in,
             short_conv_w=v_short_conv_w, short_w_out=v_short_w_out, ssm_conv_w=v_ssm_conv_w,
             ssm_conv_b=v_ssm_conv_b, ssm_dt_bias=v_ssm_dt_bias, ssm_A_log=v_ssm_A_log, ssm_D=v_ssm_D,
             ssm_norm=v_ssm_norm, ssm_w_out=v_ssm_w_out, w_out=v_w_out, ffn2_norm=v_ffn2_norm,
             ffn2_w_in=v_ffn2_w_in, ffn2_w_out=v_ffn2_w_out, final_norm=v_final_norm)
    shapes = {n: w[n].shape for n in _ORDER}

    def local(d, n):
        return d[n][0].T if n in _TRANSPOSED else d[n][0]

    shard = {n: local(w, n) for n in _SHARDED}

    wire = {n: (shard[n] if n in ("short_conv_w", "ssm_conv_w") else shard[n].astype(BF)) for n in _SHARDED}
    rep = {
        "ffn1_norm": ffn1_norm, "mix_norm": mix_norm, "ffn2_norm": ffn2_norm, "ssm_norm": ssm_norm,
        "ssm_conv_b": ssm_conv_b, "final_norm": final_norm.reshape(1, D_MODEL),
        "dt_bias_pad": _pad_lanes(ssm_dt_bias, DT_W), "a_log_pad": _pad_lanes(ssm_A_log, DT_W),
        "d_exp": jnp.repeat(ssm_D, HEAD_DIM, axis=1),
    }
    grad_x, parts, packed = _train_step(x[0], loss_target[0], wire, rep)

    out_g, out_d, out_m, out_v = {}, {}, {}, {}
    for n in _SHARDED:
        if n == "ssm_w_out":
            res, (small_parts,) = _adamw("adamw_" + n, parts[n], shard[n], local(m, n), local(v, n),
                                         comm=_gather_comm([packed]))
        else:
            res = _adamw("adamw_" + n, parts[n], shard[n], local(m, n), local(v, n))
        out_g[n], out_d[n], out_m[n], out_v[n] = [(r.T if n in _TRANSPOSED else r).reshape(shapes[n]) for r in res]
    row = lambda d: {n: d[n].reshape(1, -1) for n, _ in _SMALL}
    sres, loss_row = _adamw_small(small_parts, row(w), row(m), row(v))
    for n, _ in _SMALL:
        out_g[n], out_d[n], out_m[n], out_v[n] = [r.reshape(shapes[n]) for r in sres[n]]
    loss = loss_row[0, 0]
    return (loss, grad_x[None], *[out_g[n] for n in _ORDER], *[out_d[n] for n in _ORDER],
            *[out_m[n] for n in _ORDER], *[out_v[n] for n in _ORDER])
```

```python
import functools

import jax
import jax.numpy as jnp
from jax import lax
from jax.experimental import pallas as pl
from jax.experimental.pallas import tpu as pltpu

F32 = jnp.float32
BF = jnp.bfloat16

N_DEV = 8
D_MODEL = 1024
D_FF = 2816
D_INNER = 2048
D_XBC = 4096
N_HEADS = 32
HEAD_DIM = 64
N_GROUPS = 8
D_STATE = 128
CHUNK = 64
GROUP_W = D_INNER // N_GROUPS
NORM_EPS = 1e-5
N_IN = 11296
FF_SHARD = 2 * D_FF // N_DEV
FF_HALF = D_FF // 2
IN_SHARD = N_IN // N_DEV

OFF_B, OFF_C, OFF_XA, OFF_Z, OFF_XBC = 0, 1024, 2048, 3072, 5120
N_MAIN = 9216
OFF_DT = 2048
DT_W = 128
N_GD = 2048 + DT_W
W_GRAD_ROW_CUTS = [(0, 432), (432, 600), (600, 744), (744, 920), (920, 1080), (1080, 1208), (1208, 1412)]

ADAM_LR, ADAM_B1, ADAM_B2, ADAM_EPS, ADAM_WD, ADAM_STEP = 0.001, 0.9, 0.999, 1e-08, 0.01, 10

VMEM_LIMIT_V7X = 56 * 1024 * 1024
TM = 1024
TN_MAX_TOKENS = 2048
TE = 512
ADAM_COL_TILE = 256
GATHER_PIECES = 4
GATHER_PIECE_MIN_ROWS = 512


def _params(*sem):
    return pltpu.CompilerParams(dimension_semantics=sem, vmem_limit_bytes=VMEM_LIMIT_V7X)


_DIMS = {
    "nn": (((1,), (0,)), ((), ())),
    "nt": (((1,), (1,)), ((), ())),
    "tn": (((0,), (0,)), ((), ())),
}


def _dot(a, b, mode="nn"):
    return lax.dot_general(a, b, _DIMS[mode], preferred_element_type=F32)


def _sigmoid(x):
    return 1.0 / (1.0 + jnp.exp(-x))


class _Comm:
    def __init__(self, inputs, out_shapes, sems, start, finish):
        self.inputs, self.out_shapes, self.sems, self.start, self.finish = inputs, out_shapes, sems, start, finish


def _pcall(name, body, grid, in_specs, out_specs, out_shape, args, scratch=(), sem=None, comm=None, aliases=None):
    single = not isinstance(out_shape, (list, tuple))
    out_shapes = [out_shape] if single else list(out_shape)
    out_specs = [out_specs] if single else list(out_specs)
    n_in, n_out, n_scr = len(args), len(out_shapes), len(scratch)
    aliases = {} if aliases is None else aliases
    if comm is None:
        res = pl.pallas_call(
            body, name=name, grid=grid, in_specs=list(in_specs), out_specs=out_specs, out_shape=out_shapes,
            scratch_shapes=list(scratch), input_output_aliases=aliases, compiler_params=_params(*sem))(*args)
        return (res[0] if single else res), []
    nci, nco = len(comm.inputs), len(comm.out_shapes)

    def wrapped(*refs):
        a = refs[:n_in]
        ci = refs[n_in:n_in + nci]
        o0 = n_in + nci
        o = refs[o0:o0 + n_out]
        co = refs[o0 + n_out:o0 + n_out + nco]
        s0 = o0 + n_out + nco
        s = refs[s0:s0 + n_scr]
        cs = refs[s0 + n_scr:]
        pids = [pl.program_id(i) for i in range(len(grid))]
        first = functools.reduce(jnp.logical_and, [p == 0 for p in pids])
        last = functools.reduce(jnp.logical_and, [p == g - 1 for p, g in zip(pids, grid)])

        @pl.when(first)
        def _():
            comm.start(ci, co, cs)

        body(*a, *o, *s)

        @pl.when(last)
        def _():
            comm.finish(ci, co, cs)

    any_spec = pl.BlockSpec(memory_space=pl.ANY)
    res = pl.pallas_call(
        wrapped, name=name, grid=grid, in_specs=list(in_specs) + [any_spec] * nci,
        out_specs=out_specs + [any_spec] * nco, out_shape=out_shapes + list(comm.out_shapes),
        scratch_shapes=list(scratch) + list(comm.sems), input_output_aliases=aliases,
        compiler_params=_params(*(("arbitrary",) * len(grid))))(*args, *comm.inputs)
    core = res[:n_out]
    return (core[0] if single else core), list(res[n_out:])


def _comm_call(name, comm):
    nci, nco = len(comm.inputs), len(comm.out_shapes)

    def body(*refs):
        ci, co, cs = refs[:nci], refs[nci:nci + nco], refs[nci + nco:]
        comm.start(ci, co, cs)
        comm.finish(ci, co, cs)

    any_spec = pl.BlockSpec(memory_space=pl.ANY)
    return pl.pallas_call(
        body, name=name, in_specs=[any_spec] * nci, out_specs=[any_spec] * nco, out_shape=list(comm.out_shapes),
        scratch_shapes=list(comm.sems), compiler_params=pltpu.CompilerParams(has_side_effects=True))(*comm.inputs)


def _remote(src, dst, ssem, rsem, dev):
    return pltpu.make_async_remote_copy(src_ref=src, dst_ref=dst, send_sem=ssem, recv_sem=rsem, device_id=dev,
                                        device_id_type=pl.DeviceIdType.MESH)


def _place():
    x, y, c = lax.axis_index("x"), lax.axis_index("y"), lax.axis_index("c")
    other_chips = [(1 - x, y), (x, 1 - y), (1 - x, 1 - y)]
    return x, y, c, other_chips


def _slot(x, y, c, swap):
    return 4 * y + 2 * x + c if swap else 4 * x + 2 * y + c


def _chip_slot(x, y, swap):
    return 2 * y + x if swap else 2 * x + y


def _gather_comm(shards, swaps=None):
    n = len(shards)
    per = N_DEV - 1
    swaps = [False] * n if swaps is None else swaps
    pieces = []
    for i, a in enumerate(shards):
        rows = a.shape[0]
        k = GATHER_PIECES if (a.ndim == 2 and rows >= GATHER_PIECE_MIN_ROWS) else 1
        step = -(-rows // (k * 8)) * 8
        if k == 1:
            pieces.append((i, 0, None))
        else:
            pieces += [(i, r, min(step, rows - r)) for r in range(0, rows, step)]
    m = len(pieces)

    def src(ins, v):
        i, r, cnt = pieces[v]
        return ins[i] if cnt is None else ins[i].at[pl.ds(r, cnt)]

    def place(outs, v, x, y, c):
        i, r, cnt = pieces[v]
        blk = outs[i].at[_slot(x, y, c, swaps[i])]
        return blk if cnt is None else blk.at[pl.ds(r, cnt)]

    def start(ins, outs, sems):
        send, recv, loc = sems
        x, y, c, chips = _place()
        for v in range(m):
            me = place(outs, v, x, y, c)
            pltpu.make_async_copy(src(ins, v), me, loc.at[v]).start()
            _remote(src(ins, v), me, send.at[per * v], recv.at[per * v], (x, y, 1 - c)).start()
        for j, (qx, qy) in enumerate(chips):
            for v in range(m):
                _remote(src(ins, v), place(outs, v, x, y, c), send.at[per * v + 1 + j], recv.at[per * v + 1 + j],
                        (qx, qy, c)).start()

    def finish(ins, outs, sems):
        send, recv, loc = sems
        x, y, c, chips = _place()
        sib = (x, y, 1 - c)
        for v in range(m):
            for j, (qx, qy) in enumerate(chips):
                blk = place(outs, v, qx, qy, c)
                _remote(blk, blk, send.at[per * v + 1 + j], recv.at[per * v + 1 + j], (qx, qy, c)).wait_recv()
                _remote(blk, blk, send.at[per * v + 4 + j], recv.at[per * v + 4 + j], sib).start()
        for v in range(m):
            blk = place(outs, v, x, y, 1 - c)
            _remote(blk, blk, send.at[per * v], recv.at[per * v], sib).wait_recv()
            for j, (qx, qy) in enumerate(chips):
                blk = place(outs, v, qx, qy, 1 - c)
                _remote(blk, blk, send.at[per * v + 4 + j], recv.at[per * v + 4 + j], sib).wait_recv()
        for v in range(m):
            own = place(outs, v, x, y, c)
            for k in range(per):
                _remote(src(ins, v), own, send.at[per * v + k], recv.at[per * v + k], sib).wait_send()
            pltpu.make_async_copy(src(ins, v), own, loc.at[v]).wait()

    out_shapes = [jax.ShapeDtypeStruct((N_DEV,) + tuple(a.shape), a.dtype) for a in shards]
    sems = [pltpu.SemaphoreType.DMA((per * m,)), pltpu.SemaphoreType.DMA((per * m,)), pltpu.SemaphoreType.DMA((m,))]
    return _Comm(list(shards), out_shapes, sems, start, finish)


def _pair_comm(slots):
    n = len(slots)

    def copies(ins, outs, sems):
        send, recv = sems
        x, y, c, _ = _place()
        sib = (x, y, 1 - c)
        out = []
        for i in range(n):
            for q in range(4):
                out.append(_remote(ins[i].at[2 * q + 1 - c], outs[i].at[q], send.at[4 * i + q], recv.at[4 * i + q], sib))
        return out

    def start(ins, outs, sems):
        for cp in copies(ins, outs, sems):
            cp.start()

    def finish(ins, outs, sems):
        for cp in copies(ins, outs, sems):
            cp.wait_send()
            cp.wait_recv()

    out_shapes = [jax.ShapeDtypeStruct((4,) + tuple(a.shape[1:]), a.dtype) for a in slots]
    sems = [pltpu.SemaphoreType.DMA((4 * n,)), pltpu.SemaphoreType.DMA((4 * n,))]
    return _Comm(list(slots), out_shapes, sems, start, finish)


def _chip_comm(chip_sums, swaps=None, rows=None):
    n = len(chip_sums)
    swaps = [False] * n if swaps is None else swaps
    rows = [None] * n if rows is None else rows

    def src(ins, i, q):
        return ins[i].at[q] if rows[i] is None else ins[i].at[q, pl.ds(rows[i][0], rows[i][1] - rows[i][0])]

    def start(ins, outs, sems):
        send, recv, loc = sems
        x, y, c, chips = _place()
        for i in range(n):
            mine = _chip_slot(x, y, swaps[i])
            pltpu.make_async_copy(src(ins, i, mine), outs[i].at[mine], loc.at[i]).start()
            for j, (qx, qy) in enumerate(chips):
                _remote(src(ins, i, _chip_slot(qx, qy, swaps[i])), outs[i].at[mine], send.at[3 * i + j],
                        recv.at[3 * i + j], (qx, qy, c)).start()

    def finish(ins, outs, sems):
        send, recv, loc = sems
        x, y, c, chips = _place()
        for i in range(n):
            mine = _chip_slot(x, y, swaps[i])
            for j, (qx, qy) in enumerate(chips):
                theirs = _chip_slot(qx, qy, swaps[i])
                cp = _remote(src(ins, i, theirs), outs[i].at[theirs], send.at[3 * i + j], recv.at[3 * i + j], (qx, qy, c))
                cp.wait_send()
                cp.wait_recv()
            pltpu.make_async_copy(src(ins, i, mine), outs[i].at[mine], loc.at[i]).wait()

    def out_shape(a, r):
        shape = a.shape if r is None else (a.shape[0], r[1] - r[0]) + tuple(a.shape[2:])
        return jax.ShapeDtypeStruct(shape, a.dtype)

    out_shapes = [out_shape(a, r) for a, r in zip(chip_sums, rows)]
    sems = [pltpu.SemaphoreType.DMA((3 * n,)), pltpu.SemaphoreType.DMA((3 * n,)), pltpu.SemaphoreType.DMA((n,))]
    return _Comm(list(chip_sums), out_shapes, sems, start, finish)


def _join_comm(a, b):
    na_i, na_o, na_s = len(a.inputs), len(a.out_shapes), len(a.sems)

    def start(ins, outs, sems):
        a.start(ins[:na_i], outs[:na_o], sems[:na_s])
        b.start(ins[na_i:], outs[na_o:], sems[na_s:])

    def finish(ins, outs, sems):
        a.finish(ins[:na_i], outs[:na_o], sems[:na_s])
        b.finish(ins[na_i:], outs[na_o:], sems[na_s:])

    return _Comm(a.inputs + b.inputs, a.out_shapes + b.out_shapes, a.sems + b.sems, start, finish)


def _row_tile(r):
    for cand in (256, 128):
        if r > cand and r % cand == 0:
            return cand
    return r


def _add_pairs(name, slots, sib):
    r, c = slots.shape[1:]
    tr = _row_tile(r)

    def body(core_ref, s_ref, b_ref, o_ref):
        o_ref[...] = (s_ref[...].astype(F32) + b_ref[...].astype(F32)).astype(o_ref.dtype)

    core = jnp.full((1,), lax.axis_index("c"), jnp.int32)
    return pl.pallas_call(
        body, name=name,
        grid_spec=pltpu.PrefetchScalarGridSpec(
            num_scalar_prefetch=1, grid=(4, r // tr),
            in_specs=[pl.BlockSpec((None, tr, c), lambda q, i, core_ref: (2 * q + core_ref[0], i, 0)),
                      pl.BlockSpec((None, tr, c), lambda q, i, core_ref: (q, i, 0))],
            out_specs=pl.BlockSpec((None, tr, c), lambda q, i, core_ref: (q, i, 0))),
        out_shape=jax.ShapeDtypeStruct((4, r, c), slots.dtype),
        compiler_params=_params("parallel", "parallel"))(core, slots, sib)


def _matmul(name, mode, a, b, grid, a_spec, b_spec, o_spec, out_shape, acc_shape,
            res=None, res_spec=None, alpha=1.0, comm=None):
    nk = grid[-1]
    has_res = res is not None

    def body(*refs):
        if has_res:
            a_ref, b_ref, r_ref, o_ref = refs[:4]
        else:
            a_ref, b_ref, o_ref = refs[:3]
            r_ref = None
        part = _dot(a_ref[...], b_ref[...], mode)

        def finish(v):
            if alpha != 1.0:
                v = v * alpha
            if has_res:
                v = r_ref[...] + v
            o_ref[...] = v.astype(o_ref.dtype)

        if nk == 1:
            finish(part)
        else:
            acc = refs[-1]
            k = pl.program_id(len(grid) - 1)

            @pl.when(k == 0)
            def _():
                acc[...] = part

            @pl.when(k > 0)
            def _():
                acc[...] += part

            @pl.when(k == nk - 1)
            def _():
                finish(acc[...])

    in_specs = [a_spec, b_spec] + ([res_spec] if has_res else [])
    args = (a, b) + ((res,) if has_res else ())
    scratch = [] if nk == 1 else [pltpu.VMEM(acc_shape, F32)]
    sem = ("parallel",) * (len(grid) - 1) + ("arbitrary",)
    out, couts = _pcall(name, body, grid, in_specs, o_spec, out_shape, args, scratch, sem, comm)
    return out if comm is None else (out, couts)


def _mm_nn(name, a, b, out_dtype=F32, res=None, alpha=1.0, tk=None, kk=None, a_off=0, b_off=0, comm=None):
    t = a.shape[0]
    kk = a.shape[1] if kk is None else kk
    n = b.shape[1]
    tk = kk if tk is None else tk
    grid = (t // TM, 1, kk // tk)
    return _matmul(
        name, "nn", a, b, grid,
        pl.BlockSpec((TM, tk), lambda i, j, k: (i, k + a_off)),
        pl.BlockSpec((tk, n), lambda i, j, k: (k + b_off, 0)),
        pl.BlockSpec((TM, n), lambda i, j, k: (i, 0)),
        jax.ShapeDtypeStruct((t, n), out_dtype), (TM, n),
        res=res, res_spec=pl.BlockSpec((TM, n), lambda i, j, k: (i, 0)), alpha=alpha, comm=comm)


def _mm_nt(name, a, b, n=None, tn=None, tk=None, out_dtype=F32, comm=None):
    t, kk = a.shape
    n = b.shape[0] if n is None else n
    tn = n if tn is None else tn
    tk = kk if tk is None else tk
    grid = (n // tn, t // TM, kk // tk)
    return _matmul(
        name, "nt", a, b, grid,
        pl.BlockSpec((TM, tk), lambda j, i, k: (i, k)),
        pl.BlockSpec((tn, tk), lambda j, i, k: (j, k)),
        pl.BlockSpec((TM, tn), lambda j, i, k: (i, j)),
        jax.ShapeDtypeStruct((t, n), out_dtype), (TM, tn), comm=comm)


def _mm_tn(name, a, b, out_dtype, tm=None, n=None, col_off=0, comm=None):
    t, m = a.shape
    n = b.shape[1] if n is None else n
    tm = m if tm is None else tm
    tk = t if t <= TN_MAX_TOKENS else TM
    grid = (m // tm, 1, t // tk)
    return _matmul(
        name, "tn", a, b, grid,
        pl.BlockSpec((tk, tm), lambda j, i, k: (k, j)),
        pl.BlockSpec((tk, n), lambda j, i, k: (k, col_off)),
        pl.BlockSpec((tm, n), lambda j, i, k: (j, 0)),
        jax.ShapeDtypeStruct((m, n), out_dtype), (tm, n), comm=comm)


def _rms_fwd(name, x, w):
    t, d = x.shape

    def body(x_ref, w_ref, h_ref):
        xv = x_ref[...]
        rstd = lax.rsqrt(jnp.mean(xv * xv, axis=-1, keepdims=True) + NORM_EPS)
        h_ref[...] = (xv * rstd * w_ref[...]).astype(h_ref.dtype)

    return pl.pallas_call(
        body, name=name, grid=(t // TE,),
        in_specs=[pl.BlockSpec((TE, d), lambda i: (i, 0)), pl.BlockSpec((1, d), lambda i: (0, 0))],
        out_specs=pl.BlockSpec((TE, d), lambda i: (i, 0)),
        out_shape=jax.ShapeDtypeStruct((t, d), BF), compiler_params=_params("parallel"))(x, w)


def _rms_bwd(name, x, w, dh, dres, out_scale, comm=None):
    t, d = x.shape

    def body(x_ref, w_ref, dh_ref, dres_ref, dx_ref, dxb_ref, dw_ref):
        i = pl.program_id(0)
        xv = x_ref[...]
        rstd = lax.rsqrt(jnp.mean(xv * xv, axis=-1, keepdims=True) + NORM_EPS)
        xhat = xv * rstd
        dhv = dh_ref[...]
        wd = dhv * w_ref[...]
        proj = jnp.mean(wd * xhat, axis=-1, keepdims=True)
        dx = dres_ref[...] + rstd * (wd - xhat * proj)
        dx_ref[...] = dx
        dxb_ref[...] = (dx * out_scale).astype(BF)
        part = jnp.sum(dhv * xhat, axis=0, keepdims=True)

        @pl.when(i == 0)
        def _():
            dw_ref[...] = part

        @pl.when(i > 0)
        def _():
            dw_ref[...] += part

    row = pl.BlockSpec((TE, d), lambda i: (i, 0))
    vec = pl.BlockSpec((1, d), lambda i: (0, 0))
    outs, couts = _pcall(
        name, body, (t // TE,), [row, vec, row, row], [row, row, vec],
        [jax.ShapeDtypeStruct((t, d), F32), jax.ShapeDtypeStruct((t, d), BF), jax.ShapeDtypeStruct((1, d), F32)],
        (x, w, dh, dres), (), ("arbitrary",), comm)
    return outs if comm is None else (outs, couts)


def _final_loss(x, w, target):
    t, d = x.shape

    def body(x_ref, w_ref, t_ref, loss_ref, dx_ref, dxb_ref, dw_ref):
        i = pl.program_id(0)
        xv = x_ref[...]
        rstd = lax.rsqrt(jnp.mean(xv * xv, axis=-1, keepdims=True) + NORM_EPS)
        xhat = xv * rstd
        err = xhat * w_ref[...] - t_ref[...]
        lpart = 0.5 * jnp.sum(jnp.mean(err * err, axis=-1, keepdims=True), axis=0, keepdims=True)
        dy = err * (1.0 / d)
        wd = dy * w_ref[...]
        proj = jnp.mean(wd * xhat, axis=-1, keepdims=True)
        dx = rstd * (wd - xhat * proj)
        dx_ref[...] = dx
        dxb_ref[...] = (0.5 * dx).astype(BF)
        part = jnp.sum(dy * xhat, axis=0, keepdims=True)
        lfull = jnp.broadcast_to(lpart, (1, 128))

        @pl.when(i == 0)
        def _():
            dw_ref[...] = part
            loss_ref[...] = lfull

        @pl.when(i > 0)
        def _():
            dw_ref[...] += part
            loss_ref[...] += lfull

    row = pl.BlockSpec((TE, d), lambda i: (i, 0))
    vec = pl.BlockSpec((1, d), lambda i: (0, 0))
    return pl.pallas_call(
        body, name="final_loss", grid=(t // TE,), in_specs=[row, vec, row],
        out_specs=[pl.BlockSpec((1, 128), lambda i: (0, 0)), row, row, vec],
        out_shape=[jax.ShapeDtypeStruct((1, 128), F32), jax.ShapeDtypeStruct((t, d), F32),
                   jax.ShapeDtypeStruct((t, d), BF), jax.ShapeDtypeStruct((1, d), F32)],
        compiler_params=_params("arbitrary"))(x, w, target)


def _swiglu_fwd(name, gu, comm=None):
    t = gu.shape[0]

    def body(g_ref, u_ref, a_ref):
        g = g_ref[...].astype(F32)
        a_ref[...] = (g * _sigmoid(g) * u_ref[...].astype(F32)).astype(BF)

    blk = (TE, FF_HALF)
    out, couts = _pcall(
        name, body, (t // TE, 2),
        [pl.BlockSpec(blk, lambda i, j: (i, 2 * j)), pl.BlockSpec(blk, lambda i, j: (i, 2 * j + 1))],
        pl.BlockSpec(blk, lambda i, j: (i, j)), jax.ShapeDtypeStruct((t, D_FF), BF),
        (gu, gu), (), ("parallel", "parallel"), comm)
    return out if comm is None else (out, couts)


def _swiglu_bwd(name, gu, dact, comm=None):
    t = gu.shape[0]

    def body(g_ref, u_ref, da_ref, o_ref):
        g = g_ref[...].astype(F32)
        da = da_ref[...].astype(F32)
        s = _sigmoid(g)
        o_ref[:, 0:FF_HALF] = (da * u_ref[...].astype(F32) * (s * (1.0 + g * (1.0 - s)))).astype(BF)
        o_ref[:, FF_HALF:2 * FF_HALF] = (da * g * s).astype(BF)

    blk = (TE, FF_HALF)
    out, couts = _pcall(
        name, body, (t // TE, 2),
        [pl.BlockSpec(blk, lambda i, j: (i, 2 * j)), pl.BlockSpec(blk, lambda i, j: (i, 2 * j + 1)),
         pl.BlockSpec(blk, lambda i, j: (i, j))],
        pl.BlockSpec((TE, 2 * FF_HALF), lambda i, j: (i, j)),
        jax.ShapeDtypeStruct((t, 2 * D_FF), BF), (gu, gu, dact), (), ("parallel", "parallel"), comm)
    return out if comm is None else (out, couts)


CONV_CB = 256


CONV_ROWS = 64
CONV_HALO = 16


def _taps_down(ext, w, k):
    shifted = [pltpu.roll(ext, k - 1 - j, 0)[CONV_HALO:] for j in range(k - 1)] + [ext[CONV_HALO:]]
    out = shifted[k - 1] * w[k - 1:k, :]
    for j in range(k - 1):
        out = out + shifted[j] * w[j:j + 1, :]
    return out, shifted


def _taps_up(ext, w, k):
    rows = ext.shape[0]
    n = rows - CONV_HALO
    out = ext[:n] * w[k - 1:k, :]
    for j in range(k - 1):
        out = out + pltpu.roll(ext, rows - (k - 1 - j), 0)[:n] * w[j:j + 1, :]
    return out


def _rows_before(ref, i, r0):
    start = pl.multiple_of(jnp.maximum(r0 - CONV_HALO, 0), CONV_HALO)
    return jnp.where(i > 0, ref[pl.ds(start, CONV_HALO), :].astype(F32), 0.0)


def _rows_after(ref, r0, t):
    start = pl.multiple_of(jnp.minimum(r0 + CONV_ROWS, t - CONV_HALO), CONV_HALO)
    return ref[pl.ds(start, CONV_HALO), :].astype(F32)


def _fold8(v):
    return v.reshape(v.shape[0] // 8, 8, v.shape[1]).sum(axis=0)


def _silu_grad(pre):
    s = _sigmoid(pre)
    return s * (1.0 + pre * (1.0 - s))


def _pspec(t, off):
    base = off // CONV_CB
    return pl.BlockSpec((t, CONV_CB), lambda j: (0, base + j))


def _mix_a_fwd(p, conv_w):
    t = p.shape[0]

    def body(b_ref, c_ref, xa_ref, w_ref, o_ref):
        w = w_ref[...]

        def step(i, carry):
            r0 = pl.multiple_of(i * CONV_ROWS, CONV_ROWS)
            rows = pl.ds(r0, CONV_ROWS)
            q = c_ref[rows, :].astype(F32) * xa_ref[rows, :].astype(F32)
            q_before = _rows_before(c_ref, i, r0) * _rows_before(xa_ref, i, r0)
            va, _ = _taps_down(jnp.concatenate([q_before, q], axis=0), w, 3)
            o_ref[rows, :] = (b_ref[rows, :].astype(F32) * va).astype(BF)
            return carry

        lax.fori_loop(0, t // CONV_ROWS, step, 0)

    return pl.pallas_call(
        body, name="mix_a_fwd", grid=(D_MODEL // CONV_CB,),
        in_specs=[_pspec(t, OFF_B), _pspec(t, OFF_C), _pspec(t, OFF_XA),
                  pl.BlockSpec((3, CONV_CB), lambda j: (0, j))],
        out_specs=pl.BlockSpec((t, CONV_CB), lambda j: (0, j)),
        out_shape=jax.ShapeDtypeStruct((t, D_MODEL), BF), compiler_params=_params("parallel"))(p, p, p, conv_w)


def _mix_a_bwd(p, conv_w, dya, dp):
    t = p.shape[0]

    def body(b_ref, c_ref, xa_ref, w_ref, dy_ref, dp_in, dp_ref, dw_ref):
        del dp_in
        w = w_ref[...]
        n = t // CONV_ROWS

        def step(i, acc):
            r0 = pl.multiple_of(i * CONV_ROWS, CONV_ROWS)
            rows = pl.ds(r0, CONV_ROWS)
            cv = c_ref[rows, :].astype(F32)
            xav = xa_ref[rows, :].astype(F32)
            q_before = _rows_before(c_ref, i, r0) * _rows_before(xa_ref, i, r0)
            va, shifted = _taps_down(jnp.concatenate([q_before, cv * xav], axis=0), w, 3)
            dyv = dy_ref[rows, :]
            dp_ref[rows, 0:CONV_CB] = (dyv * va).astype(BF)
            dv = dyv * b_ref[rows, :].astype(F32)
            dv_after = jnp.where(i < n - 1, _rows_after(dy_ref, r0, t) * _rows_after(b_ref, r0, t), 0.0)
            dq = _taps_up(jnp.concatenate([dv, dv_after], axis=0), w, 3)
            dp_ref[rows, CONV_CB:2 * CONV_CB] = (dq * xav).astype(BF)
            dp_ref[rows, 2 * CONV_CB:3 * CONV_CB] = (dq * cv).astype(BF)
            return tuple(a + _fold8(dv * s) for a, s in zip(acc, shifted))

        zero = jnp.zeros((8, CONV_CB), F32)
        acc = lax.fori_loop(0, n, step, (zero, zero, zero))
        for j in range(3):
            dw_ref[j:j + 1, :] = jnp.sum(acc[j], axis=0, keepdims=True)

    col = pl.BlockSpec((t, CONV_CB), lambda j: (0, j))
    wsp = pl.BlockSpec((3, CONV_CB), lambda j: (0, j))
    return pl.pallas_call(
        body, name="mix_a_bwd", grid=(D_MODEL // CONV_CB,),
        in_specs=[_pspec(t, OFF_B), _pspec(t, OFF_C), _pspec(t, OFF_XA), wsp, col, pl.BlockSpec(memory_space=pl.ANY)],
        out_specs=[pl.BlockSpec((t, 3 * CONV_CB), lambda j: (0, j)), wsp],
        out_shape=[jax.ShapeDtypeStruct(dp.shape, dp.dtype), jax.ShapeDtypeStruct((3, D_MODEL), F32)],
        input_output_aliases={5: 0},
        compiler_params=_params("parallel"))(p, p, p, conv_w, dya, dp)


def _ssm_conv_fwd(p, conv_w, conv_b, comm=None):
    t = p.shape[0]

    def body(x_ref, w_ref, b_ref, o_ref):
        w = w_ref[...]
        bias = b_ref[...]

        def step(i, carry):
            r0 = pl.multiple_of(i * CONV_ROWS, CONV_ROWS)
            rows = pl.ds(r0, CONV_ROWS)
            ext = jnp.concatenate([_rows_before(x_ref, i, r0), x_ref[rows, :].astype(F32)], axis=0)
            pre = _taps_down(ext, w, 4)[0] + bias
            o_ref[rows, :] = pre * _sigmoid(pre)
            return carry

        lax.fori_loop(0, t // CONV_ROWS, step, 0)

    out, couts = _pcall(
        "ssm_conv_fwd", body, (D_XBC // CONV_CB,),
        [_pspec(t, OFF_XBC), pl.BlockSpec((4, CONV_CB), lambda j: (0, j)), pl.BlockSpec((1, CONV_CB), lambda j: (0, j))],
        pl.BlockSpec((t, CONV_CB), lambda j: (0, j)), jax.ShapeDtypeStruct((t, D_XBC), F32),
        (p, conv_w, conv_b), (), ("parallel",), comm)
    return out if comm is None else (out, couts)


def _ssm_conv_bwd(p, conv_w, conv_b, dxc, dp, comm=None):
    t = p.shape[0]

    def body(x_ref, w_ref, b_ref, d_ref, dp_in, dx_ref, dw_ref, db_ref):
        del dp_in
        w = w_ref[...]
        bias = b_ref[...]
        n = t // CONV_ROWS

        def step(i, acc):
            r0 = pl.multiple_of(i * CONV_ROWS, CONV_ROWS)
            rows = pl.ds(r0, CONV_ROWS)
            x_cur = x_ref[rows, :].astype(F32)
            pre, shifted = _taps_down(jnp.concatenate([_rows_before(x_ref, i, r0), x_cur], axis=0), w, 4)
            pre = pre + bias
            dpre = d_ref[rows, :] * _silu_grad(pre)
            ext_after = jnp.concatenate([x_cur[CONV_ROWS - CONV_HALO:], _rows_after(x_ref, r0, t)], axis=0)
            pre_after = _taps_down(ext_after, w, 4)[0] + bias
            dpre_after = jnp.where(i < n - 1, _rows_after(d_ref, r0, t) * _silu_grad(pre_after), 0.0)
            dx_ref[rows, :] = _taps_up(jnp.concatenate([dpre, dpre_after], axis=0), w, 4).astype(BF)
            new = tuple(a + _fold8(dpre * s) for a, s in zip(acc[:4], shifted))
            return new + (acc[4] + _fold8(dpre),)

        zero = jnp.zeros((8, CONV_CB), F32)
        acc = lax.fori_loop(0, n, step, (zero,) * 5)
        for j in range(4):
            dw_ref[j:j + 1, :] = jnp.sum(acc[j], axis=0, keepdims=True)
        db_ref[...] = jnp.sum(acc[4], axis=0, keepdims=True)

    col = pl.BlockSpec((t, CONV_CB), lambda j: (0, j))
    wsp = pl.BlockSpec((4, CONV_CB), lambda j: (0, j))
    bsp = pl.BlockSpec((1, CONV_CB), lambda j: (0, j))
    outs, couts = _pcall(
        "ssm_conv_bwd", body, (D_XBC // CONV_CB,),
        [_pspec(t, OFF_XBC), wsp, bsp, col, pl.BlockSpec(memory_space=pl.ANY)], [_pspec(t, OFF_XBC), wsp, bsp],
        [jax.ShapeDtypeStruct(dp.shape, dp.dtype), jax.ShapeDtypeStruct((4, D_XBC), F32),
         jax.ShapeDtypeStruct((1, D_XBC), F32)],
        (p, conv_w, conv_b, dxc, dp), (), ("parallel",), comm, aliases={4: 0})
    return outs if comm is None else (outs, couts)


DT_ROWS = 512


def _tri(lower):
    r = lax.broadcasted_iota(jnp.int32, (CHUNK, CHUNK), 0)
    c = lax.broadcasted_iota(jnp.int32, (CHUNK, CHUNK), 1)
    return jnp.where((r >= c) if lower else (r <= c), 1.0, 0.0).astype(F32)


def _dot_exact(a, b):
    return lax.dot_general(a, b, _DIMS["nn"], preferred_element_type=F32, precision=lax.Precision.HIGHEST)


def _dt_fwd(p, bias_pad, alog_pad):
    t = p.shape[0]

    def body(raw_ref, b_ref, al_ref, dt_ref, acs_ref):
        z = raw_ref[...] + b_ref[...]
        dt = jnp.maximum(z, 0.0) + jnp.log(1.0 + jnp.exp(-jnp.abs(z)))
        dt_ref[...] = dt
        a = dt * (-jnp.exp(al_ref[...]))
        tri = _tri(True)
        for k in range(DT_ROWS // CHUNK):
            acs_ref[k * CHUNK:(k + 1) * CHUNK, :] = _dot_exact(tri, a[k * CHUNK:(k + 1) * CHUNK, :])

    blk = pl.BlockSpec((DT_ROWS, DT_W), lambda i: (i, 0))
    vec = pl.BlockSpec((1, DT_W), lambda i: (0, 0))
    return pl.pallas_call(
        body, name="dt_fwd", grid=(t // DT_ROWS,),
        in_specs=[pl.BlockSpec((DT_ROWS, DT_W), lambda i: (i, OFF_DT // DT_W)), vec, vec],
        out_specs=[blk, blk], out_shape=[jax.ShapeDtypeStruct((t, DT_W), F32)] * 2,
        compiler_params=_params("parallel"))(p, bias_pad, alog_pad)


def _dt_bwd(p, bias_pad, alog_pad, dt, ddt, dacs, dp_gd):
    t = p.shape[0]

    def body(raw_ref, b_ref, al_ref, dt_ref, ddt_ref, dacs_ref, dp_in, draw_ref, db_ref, dal_ref):
        del dp_in
        i = pl.program_id(0)
        acoef = -jnp.exp(al_ref[...])
        triu = _tri(False)
        das = []
        for k in range(DT_ROWS // CHUNK):
            das.append(_dot_exact(triu, dacs_ref[k * CHUNK:(k + 1) * CHUNK, :]))
        da = jnp.concatenate(das, axis=0)
        dtv = dt_ref[...]
        ddt_tot = ddt_ref[...] + da * acoef
        lane = lax.broadcasted_iota(jnp.int32, (DT_ROWS, DT_W), 1)
        draw = jnp.where(lane < N_HEADS, ddt_tot * _sigmoid(raw_ref[...] + b_ref[...]), 0.0)
        draw_ref[...] = draw.astype(BF)
        pb = jnp.sum(draw, axis=0, keepdims=True)
        pa = jnp.sum(da * dtv * acoef, axis=0, keepdims=True)

        @pl.when(i == 0)
        def _():
            db_ref[...] = pb
            dal_ref[...] = pa

        @pl.when(i > 0)
        def _():
            db_ref[...] += pb
            dal_ref[...] += pa

    blk = pl.BlockSpec((DT_ROWS, DT_W), lambda i: (i, 0))
    vec = pl.BlockSpec((1, DT_W), lambda i: (0, 0))
    return pl.pallas_call(
        body, name="dt_bwd", grid=(t // DT_ROWS,),
        in_specs=[pl.BlockSpec((DT_ROWS, DT_W), lambda i: (i, OFF_DT // DT_W)), vec, vec, blk, blk, blk,
                  pl.BlockSpec(memory_space=pl.ANY)],
        out_specs=[pl.BlockSpec((DT_ROWS, DT_W), lambda i: (i, OFF_DT // DT_W)), vec, vec],
        out_shape=[jax.ShapeDtypeStruct(dp_gd.shape, dp_gd.dtype), jax.ShapeDtypeStruct((1, DT_W), F32),
                   jax.ShapeDtypeStruct((1, DT_W), F32)],
        input_output_aliases={6: 0},
        compiler_params=_params("arbitrary"))(p, bias_pad, alog_pad, dt, ddt, dacs, dp_gd)


def _split_dot(z, onehot, terms):
    out = None
    rest = z
    for _ in range(terms):
        piece = rest.astype(BF)
        part = _dot(piece, onehot)
        out = part if out is None else out + part
        rest = rest - piece.astype(F32)
    return out


def _spread_mat():
    row = lax.broadcasted_iota(jnp.int32, (DT_W, D_INNER), 0)
    lane = lax.broadcasted_iota(jnp.int32, (DT_W, D_INNER), 1)
    return jnp.where(row == lane // HEAD_DIM, 1.0, 0.0).astype(BF)


def _gather_mat():
    row = lax.broadcasted_iota(jnp.int32, (D_INNER, DT_W), 0)
    lane = lax.broadcasted_iota(jnp.int32, (D_INNER, DT_W), 1)
    return jnp.where(lane == row // HEAD_DIM, 1.0, 0.0).astype(BF)


def _ssd_masks():
    row = lax.broadcasted_iota(jnp.int32, (CHUNK, GROUP_W), 0)
    col = lax.broadcasted_iota(jnp.int32, (CHUNK, GROUP_W), 1) % HEAD_DIM
    brow = lax.broadcasted_iota(jnp.int32, (GROUP_W, GROUP_W), 0) // HEAD_DIM
    bcol = lax.broadcasted_iota(jnp.int32, (GROUP_W, GROUP_W), 1) // HEAD_DIM
    return row >= col, row == col, brow == bcol


def _stack4(v):
    return jnp.concatenate([v, v, v, v], axis=0)


def _fold4(v):
    return v[0:CHUNK] + v[CHUNK:2 * CHUNK] + v[2 * CHUNK:3 * CHUNK] + v[3 * CHUNK:4 * CHUNK]


def _ssd_group(xc_ref, wide_ref, g, tri, eye, blockdiag):
    gs = slice(GROUP_W * g, GROUP_W * (g + 1))
    xs_g = xc_ref[:, gs]
    b_g = xc_ref[:, D_INNER + D_STATE * g:D_INNER + D_STATE * (g + 1)].astype(BF)
    c_g = xc_ref[:, D_INNER + 1024 + D_STATE * g:D_INNER + 1024 + D_STATE * (g + 1)].astype(BF)
    acs_e, dt_e = wide_ref[0:CHUNK, gs], wide_ref[CHUNK:2 * CHUNK, gs]
    atot_e = acs_e[CHUNK - 1:CHUNK, :]
    acs_j = jnp.sum(jnp.where(eye, acs_e, 0.0), axis=0, keepdims=True)
    lmat = jnp.where(tri, jnp.exp(jnp.minimum(acs_e - acs_j, 0.0)), 0.0)
    b_t = _stack4(b_g)
    m = _dot(c_g, b_t, "nt") * lmat
    x_g = xs_g * dt_e
    xbd = jnp.where(blockdiag, _stack4(x_g), 0.0).astype(BF)
    return dict(gs=gs, xs=xs_g, b=b_g, c=c_g, b_t=b_t, dt=dt_e, e=jnp.exp(acs_e), dec=jnp.exp(atot_e - acs_e),
                eat=jnp.exp(atot_e), lmat=lmat, m=m, x=x_g, xbd=xbd)


def _ssd_fwd(xconv, dt, acs, d_exp, comm=None):
    t = xconv.shape[0]
    nc = t // CHUNK

    def body(xc_ref, dt_ref, acs_ref, d_ref, y_ref, hs_ref, state, wide):
        c = pl.program_id(0)

        @pl.when(c == 0)
        def _():
            state[...] = jnp.zeros_like(state)

        hs_ref[...] = state[...]
        tri, eye, blockdiag = _ssd_masks()
        wide[...] = _split_dot(jnp.concatenate([acs_ref[...], dt_ref[...]], axis=0), _spread_mat(), 3)
        for g in range(N_GROUPS):
            q = _ssd_group(xc_ref, wide, g, tri, eye, blockdiag)
            gs = q["gs"]
            h_t = state[:, gs]
            ydiag = _dot(q["m"].astype(BF), q["xbd"])
            yoff = _dot(q["c"], h_t.astype(BF)) * q["e"]
            y_ref[:, gs] = ydiag + yoff + d_ref[:, gs] * q["xs"]
            s_t = _dot(q["b"], (q["x"] * q["dec"]).astype(BF), "tn")
            state[:, gs] = q["eat"] * h_t + s_t

    blk = lambda w: pl.BlockSpec((CHUNK, w), lambda c: (c, 0))
    outs, couts = _pcall(
        "ssd_fwd", body, (nc,),
        [blk(D_XBC), blk(DT_W), blk(DT_W), pl.BlockSpec((1, D_INNER), lambda c: (0, 0))],
        [blk(D_INNER), pl.BlockSpec((None, D_STATE, D_INNER), lambda c: (c, 0, 0))],
        [jax.ShapeDtypeStruct((t, D_INNER), F32), jax.ShapeDtypeStruct((nc, D_STATE, D_INNER), F32)],
        (xconv, dt, acs, d_exp), [pltpu.VMEM((D_STATE, D_INNER), F32), pltpu.VMEM((2 * CHUNK, D_INNER), F32)],
        ("arbitrary",), comm)
    return outs if comm is None else (outs, couts)


def _ssd_bwd(xconv, dt, acs, d_exp, hsave, dy, comm=None):
    t = xconv.shape[0]
    nc = t // CHUNK

    def body(xc_ref, dt_ref, acs_ref, d_ref, hs_ref, dy_ref, dxc_ref, ddt_ref, dacs_ref, dd_ref, dstate, wide, per_head):
        c = pl.program_id(0)

        @pl.when(c == 0)
        def _():
            dstate[...] = jnp.zeros_like(dstate)
            dd_ref[...] = jnp.zeros_like(dd_ref)

        tri, eye, blockdiag = _ssd_masks()
        acsv = acs_ref[...]
        wide[...] = _split_dot(jnp.concatenate([acsv, dt_ref[...]], axis=0), _spread_mat(), 3)
        eat_heads = jnp.exp(acsv[CHUNK - 1:CHUNK, :])

        for g in range(N_GROUPS):
            q = _ssd_group(xc_ref, wide, g, tri, eye, blockdiag)
            gs, xs_g, b_g, c_g, m = q["gs"], q["xs"], q["b"], q["c"], q["m"]
            bs = slice(D_INNER + D_STATE * g, D_INNER + D_STATE * (g + 1))
            cs = slice(D_INNER + 1024 + D_STATE * g, D_INNER + 1024 + D_STATE * (g + 1))
            h_t = hs_ref[:, gs]
            h_b = h_t.astype(BF)
            dy_g = dy_ref[:, gs]
            dy_b = dy_g.astype(BF)
            ds_t = dstate[:, gs]
            ds_b = ds_t.astype(BF)

            yoff = _dot(c_g, h_b) * q["e"]
            edy = (q["e"] * dy_g).astype(BF)
            d_c = _dot(edy, h_b, "nt")
            d_ht = _dot(c_g, edy, "tn")
            bds = _dot(b_g, ds_b)
            xd = q["x"] * q["dec"]
            d_b = _dot(xd.astype(BF), ds_b, "nt")
            dm = _dot(dy_b, q["xbd"], "nt")
            cross = _dot(m.astype(BF), dy_b, "tn")
            dx_full = q["dec"] * bds + _fold4(jnp.where(blockdiag, cross, 0.0))
            dml = (dm * q["lmat"]).astype(BF)
            d_c = d_c + _dot(dml, q["b_t"])
            d_b = d_b + _fold4(_dot(dml, c_g, "tn"))
            w = dm * m
            q_dec = xd * bds
            z = w - jnp.where(eye, jnp.sum(w, axis=0, keepdims=True), 0.0) + dy_g * yoff - q_dec
            rows = jnp.concatenate(
                [jnp.sum(q_dec, axis=0, keepdims=True), jnp.sum(ds_t * h_t, axis=0, keepdims=True),
                 jnp.zeros((6, GROUP_W), F32)], axis=0)
            per_head[:, gs] = jnp.concatenate([z, dx_full * xs_g, rows], axis=0)
            dxc_ref[:, cs] = d_c
            dxc_ref[:, bs] = d_b
            dxc_ref[:, gs] = dx_full * q["dt"] + d_ref[:, gs] * dy_g
            dd_ref[:, gs] += jnp.sum(dy_g * xs_g, axis=0, keepdims=True)
            dstate[:, gs] = q["eat"] * ds_t + d_ht

        seg = _split_dot(per_head[...], _gather_mat(), 2)
        datot = seg[2 * CHUNK:2 * CHUNK + 1] + eat_heads * seg[2 * CHUNK + 1:2 * CHUNK + 2]
        rowi = lax.broadcasted_iota(jnp.int32, (CHUNK, DT_W), 0)
        ddt_ref[...] = seg[CHUNK:2 * CHUNK]
        dacs_ref[...] = seg[0:CHUNK] + jnp.where(rowi == CHUNK - 1, datot, 0.0)

    rev = lambda w: pl.BlockSpec((CHUNK, w), lambda c: (nc - 1 - c, 0))
    vec = pl.BlockSpec((1, D_INNER), lambda c: (0, 0))
    outs, couts = _pcall(
        "ssd_bwd", body, (nc,),
        [rev(D_XBC), rev(DT_W), rev(DT_W), vec,
         pl.BlockSpec((None, D_STATE, D_INNER), lambda c: (nc - 1 - c, 0, 0)), rev(D_INNER)],
        [rev(D_XBC), rev(DT_W), rev(DT_W), vec],
        [jax.ShapeDtypeStruct((t, D_XBC), F32), jax.ShapeDtypeStruct((t, DT_W), F32),
         jax.ShapeDtypeStruct((t, DT_W), F32), jax.ShapeDtypeStruct((1, D_INNER), F32)],
        (xconv, dt, acs, d_exp, hsave, dy),
        [pltpu.VMEM((D_STATE, D_INNER), F32), pltpu.VMEM((2 * CHUNK, D_INNER), F32),
         pltpu.VMEM((2 * CHUNK + 8, D_INNER), F32)], ("arbitrary",), comm)
    return outs if comm is None else (outs, couts)


GN_CB = 1024
GN_GROUPS = GN_CB // GROUP_W


def _gnorm_fwd(y, p, w, comm=None):
    t = y.shape[0]
    zoff = OFF_Z // GN_CB

    def body(y_ref, z_ref, w_ref, o_ref):
        for g in range(GN_GROUPS):
            gs = slice(GROUP_W * g, GROUP_W * (g + 1))
            z = z_ref[:, gs].astype(F32)
            yf = y_ref[:, gs] * (z * _sigmoid(z))
            rstd = lax.rsqrt(jnp.mean(yf * yf, axis=-1, keepdims=True) + NORM_EPS)
            o_ref[:, gs] = (yf * rstd * w_ref[:, gs]).astype(BF)

    blk = pl.BlockSpec((TE, GN_CB), lambda i, j: (i, j))
    out, couts = _pcall(
        "gnorm_fwd", body, (t // TE, D_INNER // GN_CB),
        [blk, pl.BlockSpec((TE, GN_CB), lambda i, j: (i, zoff + j)), pl.BlockSpec((1, GN_CB), lambda i, j: (0, j))],
        blk, jax.ShapeDtypeStruct((t, D_INNER), BF), (y, p, w), (), ("parallel", "parallel"), comm)
    return out if comm is None else (out, couts)


def _gnorm_bwd(y, p, w, dyn, comm=None):
    t = y.shape[0]
    zoff = OFF_Z // GN_CB

    def body(y_ref, z_ref, w_ref, dn_ref, dy_ref, dz_ref, dw_ref):
        i = pl.program_id(1)
        for g in range(GN_GROUPS):
            gs = slice(GROUP_W * g, GROUP_W * (g + 1))
            z = z_ref[:, gs].astype(F32)
            yv = y_ref[:, gs]
            s = _sigmoid(z)
            sil = z * s
            yf = yv * sil
            rstd = lax.rsqrt(jnp.mean(yf * yf, axis=-1, keepdims=True) + NORM_EPS)
            xhat = yf * rstd
            dn = dn_ref[:, gs]
            wd = dn * w_ref[:, gs]
            proj = jnp.mean(wd * xhat, axis=-1, keepdims=True)
            dyf = rstd * (wd - xhat * proj)
            dy_ref[:, gs] = dyf * sil
            dz_ref[:, gs] = (dyf * yv * (s * (1.0 + z * (1.0 - s)))).astype(BF)
            part = jnp.sum(dn * xhat, axis=0, keepdims=True)

            @pl.when(i == 0)
            def _():
                dw_ref[:, gs] = part

            @pl.when(i > 0)
            def _():
                dw_ref[:, gs] += part

    blk = pl.BlockSpec((TE, GN_CB), lambda j, i: (i, j))
    vec = pl.BlockSpec((1, GN_CB), lambda j, i: (0, j))
    outs, couts = _pcall(
        "gnorm_bwd", body, (D_INNER // GN_CB, t // TE),
        [blk, pl.BlockSpec((TE, GN_CB), lambda j, i: (i, zoff + j)), vec, blk],
        [blk, pl.BlockSpec((TE, GN_CB), lambda j, i: (i, zoff + j)), vec],
        [jax.ShapeDtypeStruct((t, D_INNER), F32), jax.ShapeDtypeStruct((t, N_MAIN), BF),
         jax.ShapeDtypeStruct((1, D_INNER), F32)],
        (y, p, w, dyn), (), ("parallel", "arbitrary"), comm)
    return outs if comm is None else (outs, couts)


MERGE_CB = 512


def _merge_fwd(p, ya, yb):
    t = ya.shape[0]

    def body(ga_ref, gb_ref, ya_ref, yb_ref, o_ref):
        o_ref[...] = (_sigmoid(ga_ref[...]) * ya_ref[...] + _sigmoid(gb_ref[...]) * yb_ref[...]).astype(BF)

    blk = pl.BlockSpec((TE, MERGE_CB), lambda i, j: (i, j))
    return pl.pallas_call(
        body, name="merge_fwd", grid=(t // TE, D_MODEL // MERGE_CB),
        in_specs=[pl.BlockSpec((TE, MERGE_CB), lambda i, j: (i, 2 * j)),
                  pl.BlockSpec((TE, MERGE_CB), lambda i, j: (i, 2 * j + 1)), blk, blk],
        out_specs=blk, out_shape=jax.ShapeDtypeStruct((t, D_MODEL), BF),
        compiler_params=_params("parallel", "parallel"))(p, p, ya, yb)


def _merge_bwd(p, ya, yb, dm):
    t = ya.shape[0]

    def body(ga_ref, gb_ref, ya_ref, yb_ref, dm_ref, dg_ref, dya_ref, dyb_ref):
        d = dm_ref[...]
        sa = _sigmoid(ga_ref[...])
        sb = _sigmoid(gb_ref[...])
        dg_ref[:, 0:MERGE_CB] = (d * ya_ref[...] * sa * (1.0 - sa)).astype(BF)
        dg_ref[:, MERGE_CB:2 * MERGE_CB] = (d * yb_ref[...] * sb * (1.0 - sb)).astype(BF)
        dya_ref[...] = (d * sa).astype(BF)
        dyb_ref[...] = (d * sb).astype(BF)

    blk = pl.BlockSpec((TE, MERGE_CB), lambda i, j: (i, j))
    return pl.pallas_call(
        body, name="merge_bwd", grid=(t // TE, D_MODEL // MERGE_CB),
        in_specs=[pl.BlockSpec((TE, MERGE_CB), lambda i, j: (i, 2 * j)),
                  pl.BlockSpec((TE, MERGE_CB), lambda i, j: (i, 2 * j + 1)), blk, blk, blk],
        out_specs=[pl.BlockSpec((TE, 2 * MERGE_CB), lambda i, j: (i, j)), blk, blk],
        out_shape=[jax.ShapeDtypeStruct((t, N_GD), BF)] + [jax.ShapeDtypeStruct((t, D_MODEL), BF)] * 2,
        compiler_params=_params("parallel", "parallel"))(p, p, ya, yb, dm)


def _adamw(name, parts, w, m, v, comm=None):
    r, c = w.shape
    tr = _row_tile(r)
    tc = ADAM_COL_TILE if (tr == r and r > 512 and c % ADAM_COL_TILE == 0) else c
    n_parts = parts.shape[0]
    bc1 = 1.0 - ADAM_B1 ** ADAM_STEP
    bc2 = 1.0 - ADAM_B2 ** ADAM_STEP

    def body(p_ref, w_ref, m_ref, v_ref, g_ref, d_ref, nm_ref, nv_ref):
        g = p_ref[0].astype(F32)
        for k in range(1, n_parts):
            g = g + p_ref[k].astype(F32)
        nm = ADAM_B1 * m_ref[...] + (1.0 - ADAM_B1) * g
        nv = ADAM_B2 * v_ref[...] + (1.0 - ADAM_B2) * (g * g)
        g_ref[...] = g
        nm_ref[...] = nm
        nv_ref[...] = nv
        d_ref[...] = -ADAM_LR * ((nm / bc1) / (jnp.sqrt(nv / bc2) + ADAM_EPS) + ADAM_WD * w_ref[...])

    blk = pl.BlockSpec((tr, tc), lambda i, j: (i, j))
    outs, couts = _pcall(
        name, body, (r // tr, c // tc),
        [pl.BlockSpec((n_parts, tr, tc), lambda i, j: (0, i, j)), blk, blk, blk], [blk] * 4,
        [jax.ShapeDtypeStruct((r, c), F32)] * 4, (parts, w, m, v), (), ("parallel", "parallel"), comm)
    return outs if comm is None else (outs, couts)


def _pad_lanes(v, width):
    return jnp.pad(v, ((0, 0), (0, width - v.shape[1])))


def _reduce_start(slots, host):
    outs, sib = host(_pair_comm([a for _, a in slots]))
    sums = [(n, _add_pairs("pairsum_" + n, a, b)) for (n, a), b in zip(slots, sib)]
    return outs, sums


def _train_step(x, target, shard, rep):
    gdt = BF
    recv = {}
    (got,) = _comm_call("gather_ffn1_in", _gather_comm([shard["ffn1_w_in"]], [True]))
    w1_in = got.reshape(2 * D_FF, D_MODEL)
    h1 = _rms_fwd("rms1_fwd", x, rep["ffn1_norm"])
    gu1, got = _mm_nt("ffn1_in", h1, w1_in, tn=FF_HALF, out_dtype=BF, comm=_gather_comm(
        [shard["ffn1_w_out"], shard["w_in"], shard["short_conv_w"], shard["ssm_conv_w"]]))
    w1_out = got[0].reshape(D_FF, D_MODEL)
    w_in_t = got[1].reshape(N_IN, D_MODEL)
    short_conv_w = got[2].transpose(1, 0, 2).reshape(3, D_MODEL)
    ssm_conv_w = got[3].transpose(1, 0, 2).reshape(4, D_XBC)
    act1 = _swiglu_fwd("swiglu1_fwd", gu1)
    x1 = _mm_nn("ffn1_out", act1, w1_out, res=x, alpha=0.5)
    ga0 = N_MAIN + N_HEADS
    gb0 = ga0 + D_MODEL
    half = D_MODEL // 2
    w_gd = jnp.concatenate(
        [w_in_t[ga0:ga0 + half], w_in_t[gb0:gb0 + half], w_in_t[ga0 + half:gb0], w_in_t[gb0 + half:],
         w_in_t[N_MAIN:N_MAIN + N_HEADS], jnp.zeros((DT_W - N_HEADS, D_MODEL), BF)], axis=0)
    w_mix_perm = w_in_t[0:3 * D_MODEL].reshape(3, 4, CONV_CB, D_MODEL).transpose(1, 0, 2, 3).reshape(3 * D_MODEL, D_MODEL)

    h2 = _rms_fwd("rms2_fwd", x1, rep["mix_norm"])
    p, got = _mm_nt("proj_main", h2, w_in_t, n=N_MAIN, tn=1024, out_dtype=BF, comm=_gather_comm(
        [shard["short_w_out"], shard["ssm_w_out"], shard["w_out"]]))
    p_gd = _mm_nt("proj_gd", h2, w_gd)
    short_w_out = got[0].reshape(D_MODEL, D_MODEL)
    ssm_w_out = got[1].reshape(D_INNER, D_MODEL)
    w_out = got[2].reshape(D_MODEL, D_MODEL)
    ya_in = _mix_a_fwd(p, short_conv_w)
    y_a = _mm_nn("short_out", ya_in, short_w_out)
    xconv, (got,) = _ssm_conv_fwd(p, ssm_conv_w, rep["ssm_conv_b"], comm=_gather_comm([shard["ffn2_w_out"]]))
    w2_out = got.reshape(D_FF, D_MODEL)
    dt, acs = _dt_fwd(p_gd, rep["dt_bias_pad"], rep["a_log_pad"])
    (y_ssm, hsave), (got,) = _ssd_fwd(xconv, dt, acs, rep["d_exp"], comm=_gather_comm([shard["ffn2_w_in"]], [True]))
    w2_in = got.reshape(2 * D_FF, D_MODEL)
    yn = _gnorm_fwd(y_ssm, p, rep["ssm_norm"])
    y_b = _mm_nn("ssm_out", yn, ssm_w_out, tk=1024)
    merged = _merge_fwd(p_gd, y_a, y_b)
    x2 = _mm_nn("mix_out", merged, w_out, res=x1)

    h3 = _rms_fwd("rms3_fwd", x2, rep["ffn2_norm"])
    gu2 = _mm_nt("ffn2_in", h3, w2_in, tn=FF_HALF, out_dtype=BF)
    act2 = _swiglu_fwd("swiglu2_fwd", gu2)
    x3 = _mm_nn("ffn2_out", act2, w2_out, res=x2, alpha=0.5)

    loss, dx3, dx3h, g_final = _final_loss(x3, rep["final_norm"], target)

    small = {"final_norm": g_final}
    dact2 = _mm_nt("ffn2_out_bwd_act", dx3h, w2_out, out_dtype=BF)
    g_w2_out = _mm_tn("ffn2_out_bwd_w", act2, dx3h, gdt, tm=FF_HALF)
    dgu2 = _swiglu_bwd("swiglu2_bwd", gu2, dact2)
    g_w2_in = _mm_tn("ffn2_in_bwd_w", dgu2, h3, gdt, tm=FF_HALF)
    dh3 = _mm_nn("ffn2_in_bwd_h", dgu2, w2_in, tk=FF_HALF)
    dx2, dx2b, small["ffn2_norm"] = _rms_bwd("rms3_bwd", x2, rep["ffn2_norm"], dh3, dx3, 1.0)

    dmerged = _mm_nt("mix_out_bwd_x", dx2b, w_out)
    g_w_out = _mm_tn("mix_out_bwd_w", merged, dx2b, gdt)
    dp_gd, dya, dyb = _merge_bwd(p_gd, y_a, y_b, dmerged)

    dya_in = _mm_nt("short_out_bwd_x", dya, short_w_out)
    g_short_w_out = _mm_tn("short_out_bwd_w", ya_in, dya, gdt)

    dyn = _mm_nt("ssm_out_bwd_x", dyb, ssm_w_out)
    g_ssm_w_out = _mm_tn("ssm_out_bwd_w", yn, dyb, gdt)
    late = [("ffn2_w_out", g_w2_out.reshape(N_DEV, FF_SHARD // 2, D_MODEL)),
            ("ffn2_w_in", g_w2_in.reshape(N_DEV, FF_SHARD, D_MODEL)),
            ("w_out", g_w_out.reshape(N_DEV, -1, D_MODEL)), ("short_w_out", g_short_w_out.reshape(N_DEV, -1, D_MODEL)),
            ("ssm_w_out", g_ssm_w_out.reshape(N_DEV, -1, D_MODEL))]
    (dy_ssm, dp, small["ssm_norm"]), sums = _reduce_start(
        late, lambda comm: _gnorm_bwd(y_ssm, p, rep["ssm_norm"], dyn, comm=comm))
    dp, g_short_conv = _mix_a_bwd(p, short_conv_w, dya_in, dp)
    first = [(n, a) for n, a in sums if n.startswith("ffn2")]
    second = [(n, a) for n, a in sums if not n.startswith("ffn2")]
    (dxconv, ddt, dacs, dd_lane), got = _ssd_bwd(
        xconv, dt, acs, rep["d_exp"], hsave, dy_ssm,
        comm=_chip_comm([a for _, a in first], [n == "ffn2_w_in" for n, _ in first]))
    recv.update({n: a for (n, _), a in zip(first, got)})
    small["ssm_D"] = dd_lane.reshape(N_HEADS, HEAD_DIM).sum(axis=1)[None, :]
    (dp, g_ssm_conv, small["ssm_conv_b"]), got = _ssm_conv_bwd(
        p, ssm_conv_w, rep["ssm_conv_b"], dxconv, dp, comm=_chip_comm([a for _, a in second]))
    recv.update({n: a for (n, _), a in zip(second, got)})
    dp_gd, dbias, dalog = _dt_bwd(p_gd, rep["dt_bias_pad"], rep["a_log_pad"], dt, ddt, dacs, dp_gd)
    small["ssm_dt_bias"] = dbias[:, :N_HEADS]
    small["ssm_A_log"] = dalog[:, :N_HEADS]

    g_main = _mm_tn("proj_main_bwd_w", dp, h2, gdt, tm=1024)
    g_gd = _mm_tn("proj_gd_bwd_w", dp_gd, h2, gdt)
    g_mix = g_main[0:3 * D_MODEL].reshape(4, 3, CONV_CB, D_MODEL).transpose(1, 0, 2, 3).reshape(3 * D_MODEL, D_MODEL)
    g_in_t = jnp.concatenate(
        [g_mix, g_main[3 * D_MODEL:], g_gd[2 * D_MODEL:2 * D_MODEL + N_HEADS],
         g_gd[0:half], g_gd[2 * half:3 * half], g_gd[half:2 * half], g_gd[3 * half:4 * half]], axis=0).reshape(
        N_DEV, IN_SHARD, D_MODEL)
    dh2, w_sums = _reduce_start(
        [("w_in", g_in_t)], lambda comm: _mm_nn("proj_mix_bwd_x", dp, w_mix_perm, tk=1024, kk=3 * D_MODEL, comm=comm))
    w_sum = w_sums[0][1]

    def w_piece(i):
        return _chip_comm([w_sum], rows=[W_GRAD_ROW_CUTS[i]])

    dh2, got0 = _mm_nn("proj_rest_bwd_x", dp, w_in_t, tk=1024, kk=N_MAIN - 3 * D_MODEL, a_off=3, b_off=3, res=dh2,
                       comm=w_piece(0))
    dh2, got1 = _mm_nn("proj_gd_bwd_x", dp_gd, w_gd, res=dh2, comm=w_piece(1))
    (dx1, dx1h, small["mix_norm"]), got2 = _rms_bwd("rms2_bwd", x1, rep["mix_norm"], dh2, dx2, 0.5, comm=w_piece(2))
    g_w1_out, got3 = _mm_tn("ffn1_out_bwd_w", act1, dx1h, gdt, tm=FF_HALF, comm=w_piece(3))
    rest = [("ffn1_w_out", g_w1_out.reshape(N_DEV, FF_SHARD // 2, D_MODEL)),
            ("short_conv_w", g_short_conv.reshape(3, N_DEV, -1).transpose(1, 0, 2)),
            ("ssm_conv_w", g_ssm_conv.reshape(4, N_DEV, -1).transpose(1, 0, 2))]
    dact1, got = _mm_nt("ffn1_out_bwd_act", dx1h, w1_out, out_dtype=BF,
                        comm=_join_comm(w_piece(4), _pair_comm([a for _, a in rest])))
    got4, sib = got[0], got[1:]
    rest_sums = [(n, _add_pairs("pairsum_" + n, a, b)) for (n, a), b in zip(rest, sib)]
    dgu1, got = _swiglu_bwd("swiglu1_bwd", gu1, dact1, comm=_chip_comm([a for _, a in rest_sums]))
    recv.update({n: a for (n, _), a in zip(rest_sums, got)})

    def part(tag, width, off, comm=None):
        out = _mm_tn("ffn1_in_bwd_w_" + tag, dgu1, h1, gdt, tm=FF_HALF, n=width, col_off=off, comm=comm)
        g, couts = (out, None) if comm is None else out
        return g.reshape(N_DEV, FF_SHARD, width), couts

    g_a, (got5,) = part("a", 256, 3, w_piece(5))
    g_b, (got6, sib) = part("b", 384, 0, _join_comm(w_piece(6), _pair_comm([g_a])))
    recv["w_in"] = jnp.concatenate([got0[0], got1[0], got2[0], got3[0], got4, got5, got6], axis=1)
    sum_a = _add_pairs("pairsum_ffn1_w_in_a", g_a, sib)
    g_c, (recv_a, sib_b) = part("c", 384, 1, _join_comm(_chip_comm([sum_a], [True]), _pair_comm([g_b])))
    sum_b = _add_pairs("pairsum_ffn1_w_in_b", g_b, sib_b)
    dh1, (recv_b, sib_c) = _mm_nn("ffn1_in_bwd_h", dgu1, w1_in, tk=FF_HALF,
                                  comm=_join_comm(_chip_comm([sum_b], [True]), _pair_comm([g_c])))
    sum_c = _add_pairs("pairsum_ffn1_w_in_c", g_c, sib_c)
    (dx0, _, small["ffn1_norm"]), (recv_c,) = _rms_bwd("rms1_bwd", x, rep["ffn1_norm"], dh1, dx1, 1.0,
                                                        comm=_chip_comm([sum_c], [True]))
    recv["ffn1_w_in"] = jnp.concatenate([recv_b, recv_c, recv_a], axis=2)
    return dx0, recv, _pack_small(small, loss[:, 0:1])


_SMALL = [("ffn1_norm", 1024), ("mix_norm", 1024), ("ssm_conv_b", 4096), ("ssm_dt_bias", 32), ("ssm_A_log", 32),
          ("ssm_D", 32), ("ssm_norm", 2048), ("ffn2_norm", 1024), ("final_norm", 1024)]
SMALL_W = 10368


def _pack_small(d, loss=None):
    parts = [d[n].reshape(1, -1).astype(F32) for n, _ in _SMALL]
    used = sum(sz for _, sz in _SMALL)
    tail = jnp.zeros((1, SMALL_W - used), F32)
    if loss is not None:
        tail = tail.at[:, 0:1].set(loss)
    return jnp.concatenate(parts + [tail], axis=1)


def _adamw_small(parts, w, m, v):
    n_par = len(_SMALL)
    bc1 = 1.0 - ADAM_B1 ** ADAM_STEP
    bc2 = 1.0 - ADAM_B2 ** ADAM_STEP
    used = sum(sz for _, sz in _SMALL)

    def body(*refs):
        p_ref = refs[0]
        ins = refs[1:1 + 3 * n_par]
        outs = refs[1 + 3 * n_par:]
        g_all = p_ref[0]
        for k in range(1, N_DEV):
            g_all = g_all + p_ref[k]
        off = 0
        for i, (_, sz) in enumerate(_SMALL):
            g = g_all[:, off:off + sz]
            w_ref, m_ref, v_ref = ins[3 * i:3 * i + 3]
            nm = ADAM_B1 * m_ref[...] + (1.0 - ADAM_B1) * g
            nv = ADAM_B2 * v_ref[...] + (1.0 - ADAM_B2) * (g * g)
            outs[4 * i][...] = g
            outs[4 * i + 1][...] = -ADAM_LR * ((nm / bc1) / (jnp.sqrt(nv / bc2) + ADAM_EPS) + ADAM_WD * w_ref[...])
            outs[4 * i + 2][...] = nm
            outs[4 * i + 3][...] = nv
            off += sz
        outs[4 * n_par][...] = g_all[:, used:SMALL_W]

    args = [parts]
    out_shape = []
    for name, sz in _SMALL:
        args += [w[name], m[name], v[name]]
        out_shape += [jax.ShapeDtypeStruct((1, sz), F32)] * 4
    out_shape.append(jax.ShapeDtypeStruct((1, SMALL_W - used), F32))
    res = pl.pallas_call(body, name="adamw_small", out_shape=out_shape,
                         compiler_params=pltpu.CompilerParams(vmem_limit_bytes=VMEM_LIMIT_V7X))(*args)
    return {name: tuple(res[4 * i:4 * i + 4]) for i, (name, _) in enumerate(_SMALL)}, res[-1]


_SHARDED = ["ffn1_w_in", "ffn1_w_out", "w_in", "short_conv_w", "short_w_out", "ssm_conv_w", "ssm_w_out", "w_out",
            "ffn2_w_in", "ffn2_w_out"]
_TRANSPOSED = ("ffn1_w_in", "w_in", "ffn2_w_in")
_ORDER = ["ffn1_norm", "ffn1_w_in", "ffn1_w_out", "mix_norm", "w_in", "short_conv_w", "short_w_out", "ssm_conv_w",
          "ssm_conv_b", "ssm_dt_bias", "ssm_A_log", "ssm_D", "ssm_norm", "ssm_w_out", "w_out", "ffn2_norm",
          "ffn2_w_in", "ffn2_w_out", "final_norm"]


def kernel(x, ffn1_norm, ffn1_w_in, ffn1_w_out, mix_norm, w_in, short_conv_w, short_w_out, ssm_conv_w, ssm_conv_b, ssm_dt_bias, ssm_A_log, ssm_D, ssm_norm, ssm_w_out, w_out, ffn2_norm, ffn2_w_in, ffn2_w_out, final_norm, loss_target, m_ffn1_norm, m_ffn1_w_in, m_ffn1_w_out, m_mix_norm, m_w_in, m_short_conv_w, m_short_w_out, m_ssm_conv_w, m_ssm_conv_b, m_ssm_dt_bias, m_ssm_A_log, m_ssm_D, m_ssm_norm, m_ssm_w_out, m_w_out, m_ffn2_norm, m_ffn2_w_in, m_ffn2_w_out, m_final_norm, v_ffn1_norm, v_ffn1_w_in, v_ffn1_w_out, v_mix_norm, v_w_in, v_short_conv_w, v_short_w_out, v_ssm_conv_w, v_ssm_conv_b, v_ssm_dt_bias, v_ssm_A_log, v_ssm_D, v_ssm_norm, v_ssm_w_out, v_w_out, v_ffn2_norm, v_ffn2_w_in, v_ffn2_w_out, v_final_norm):
    w = dict(ffn1_norm=ffn1_norm, ffn1_w_in=ffn1_w_in, ffn1_w_out=ffn1_w_out, mix_norm=mix_norm, w_in=w_in,
             short_conv_w=short_conv_w, short_w_out=short_w_out, ssm_conv_w=ssm_conv_w, ssm_conv_b=ssm_conv_b,
             ssm_dt_bias=ssm_dt_bias, ssm_A_log=ssm_A_log, ssm_D=ssm_D, ssm_norm=ssm_norm, ssm_w_out=ssm_w_out,
             w_out=w_out, ffn2_norm=ffn2_norm, ffn2_w_in=ffn2_w_in, ffn2_w_out=ffn2_w_out, final_norm=final_norm)
    m = dict(ffn1_norm=m_ffn1_norm, ffn1_w_in=m_ffn1_w_in, ffn1_w_out=m_ffn1_w_out, mix_norm=m_mix_norm, w_in=m_w_in,
             short_conv_w=m_short_conv_w, short_w_out=m_short_w_out, ssm_conv_w=m_ssm_conv_w,
             ssm_conv_b=m_ssm_conv_b, ssm_dt_bias=m_ssm_dt_bias, ssm_A_log=m_ssm_A_log, ssm_D=m_ssm_D,
             ssm_norm=m_ssm_norm, ssm_w_out=m_ssm_w_out, w_out=m_w_out, ffn2_norm=m_ffn2_norm,
             ffn2_w_in=m_ffn2_w_in, ffn2_w_out=m_ffn2_w_out, final_norm=m_final_norm)
    v = dict(ffn1_norm=v_ffn1_norm, ffn1_w_in=v_ffn1_w_in, ffn1_w_out=v_ffn1_w_out, mix_norm=v_mix_norm, w_in=v_w_in,
             short_conv_w=v_short_conv_w, short_w_out=v_short_w_out, ssm_conv_w=v_ssm_conv_w,
             ssm_conv_b=v_ssm_conv_b, ssm_dt_bias=v_ssm_dt_bias, ssm_A_log=v_ssm_A_log, ssm_D=v_ssm_D,
             ssm_norm=v_ssm_norm, ssm_w_out=v_ssm_w_out, w_out=v_w_out, ffn2_norm=v_ffn2_norm,
             ffn2_w_in=v_ffn2_w_in, ffn2_w_out=v_ffn2_w_out, final_norm=v_final_norm)
    shapes = {n: w[n].shape for n in _ORDER}

    def local(d, n):
        return d[n][0].T if n in _TRANSPOSED else d[n][0]

    shard = {n: local(w, n) for n in _SHARDED}

    wire = {n: (shard[n] if n in ("short_conv_w", "ssm_conv_w") else shard[n].astype(BF)) for n in _SHARDED}
    rep = {
        "ffn1_norm": ffn1_norm, "mix_norm": mix_norm, "ffn2_norm": ffn2_norm, "ssm_norm": ssm_norm,
        "ssm_conv_b": ssm_conv_b, "final_norm": final_norm.reshape(1, D_MODEL),
        "dt_bias_pad": _pad_lanes(ssm_dt_bias, DT_W), "a_log_pad": _pad_lanes(ssm_A_log, DT_W),
        "d_exp": jnp.repeat(ssm_D, HEAD_DIM, axis=1),
    }
    grad_x, parts, packed = _train_step(x[0], loss_target[0], wire, rep)

    out_g, out_d, out_m, out_v = {}, {}, {}, {}
    for n in _SHARDED:
        if n == "ssm_w_out":
            res, (small_parts,) = _adamw("adamw_" + n, parts[n], shard[n], local(m, n), local(v, n),
                                         comm=_gather_comm([packed]))
        else:
            res = _adamw("adamw_" + n, parts[n], shard[n], local(m, n), local(v, n))
        out_g[n], out_d[n], out_m[n], out_v[n] = [(r.T if n in _TRANSPOSED else r).reshape(shapes[n]) for r in res]
    row = lambda d: {n: d[n].reshape(1, -1) for n, _ in _SMALL}
    sres, loss_row = _adamw_small(small_parts, row(w), row(m), row(v))
    for n, _ in _SMALL:
        out_g[n], out_d[n], out_m[n], out_v[n] = [r.reshape(shapes[n]) for r in sres[n]]
    loss = loss_row[0, 0]
    return (loss, grad_x[None], *[out_g[n] for n in _ORDER], *[out_d[n] for n in _ORDER],
            *[out_m[n] for n in _ORDER], *[out_v[n] for n in _ORDER])
```

```python
import functools

import jax
import jax.numpy as jnp
from jax import lax
from jax.experimental import pallas as pl
from jax.experimental.pallas import tpu as pltpu

F32 = jnp.float32
BF = jnp.bfloat16

N_DEV = 8
D_MODEL = 1024
D_FF = 2816
D_INNER = 2048
D_XBC = 4096
N_HEADS = 32
HEAD_DIM = 64
N_GROUPS = 8
D_STATE = 128
CHUNK = 64
GROUP_W = D_INNER // N_GROUPS
NORM_EPS = 1e-5
N_IN = 11296
FF_SHARD = 2 * D_FF // N_DEV
FF_HALF = D_FF // 2
IN_SHARD = N_IN // N_DEV

OFF_B, OFF_C, OFF_XA, OFF_Z, OFF_XBC = 0, 1024, 2048, 3072, 5120
N_MAIN = 9216
OFF_DT = 2048
DT_W = 128
N_GD = 2048 + DT_W
W_GRAD_ROW_CUTS = [(0, 400), (400, 568), (568, 704), (704, 880), (880, 1040), (1040, 1240), (1240, 1412)]

ADAM_LR, ADAM_B1, ADAM_B2, ADAM_EPS, ADAM_WD, ADAM_STEP = 0.001, 0.9, 0.999, 1e-08, 0.01, 10

VMEM_LIMIT_V7X = 56 * 1024 * 1024
TM = 1024
TN_MAX_TOKENS = 2048
TE = 512
ADAM_COL_TILE = 256
GATHER_PIECES = 4
GATHER_PIECE_MIN_ROWS = 512


def _params(*sem):
    return pltpu.CompilerParams(dimension_semantics=sem, vmem_limit_bytes=VMEM_LIMIT_V7X)


_DIMS = {
    "nn": (((1,), (0,)), ((), ())),
    "nt": (((1,), (1,)), ((), ())),
    "tn": (((0,), (0,)), ((), ())),
}


def _dot(a, b, mode="nn"):
    return lax.dot_general(a, b, _DIMS[mode], preferred_element_type=F32)


def _sigmoid(x):
    return 1.0 / (1.0 + jnp.exp(-x))


class _Comm:
    def __init__(self, inputs, out_shapes, sems, start, finish):
        self.inputs, self.out_shapes, self.sems, self.start, self.finish = inputs, out_shapes, sems, start, finish


def _pcall(name, body, grid, in_specs, out_specs, out_shape, args, scratch=(), sem=None, comm=None, aliases=None):
    single = not isinstance(out_shape, (list, tuple))
    out_shapes = [out_shape] if single else list(out_shape)
    out_specs = [out_specs] if single else list(out_specs)
    n_in, n_out, n_scr = len(args), len(out_shapes), len(scratch)
    aliases = {} if aliases is None else aliases
    if comm is None:
        res = pl.pallas_call(
            body, name=name, grid=grid, in_specs=list(in_specs), out_specs=out_specs, out_shape=out_shapes,
            scratch_shapes=list(scratch), input_output_aliases=aliases, compiler_params=_params(*sem))(*args)
        return (res[0] if single else res), []
    nci, nco = len(comm.inputs), len(comm.out_shapes)

    def wrapped(*refs):
        a = refs[:n_in]
        ci = refs[n_in:n_in + nci]
        o0 = n_in + nci
        o = refs[o0:o0 + n_out]
        co = refs[o0 + n_out:o0 + n_out + nco]
        s0 = o0 + n_out + nco
        s = refs[s0:s0 + n_scr]
        cs = refs[s0 + n_scr:]
        pids = [pl.program_id(i) for i in range(len(grid))]
        first = functools.reduce(jnp.logical_and, [p == 0 for p in pids])
        last = functools.reduce(jnp.logical_and, [p == g - 1 for p, g in zip(pids, grid)])

        @pl.when(first)
        def _():
            comm.start(ci, co, cs)

        body(*a, *o, *s)

        @pl.when(last)
        def _():
            comm.finish(ci, co, cs)

    any_spec = pl.BlockSpec(memory_space=pl.ANY)
    res = pl.pallas_call(
        wrapped, name=name, grid=grid, in_specs=list(in_specs) + [any_spec] * nci,
        out_specs=out_specs + [any_spec] * nco, out_shape=out_shapes + list(comm.out_shapes),
        scratch_shapes=list(scratch) + list(comm.sems), input_output_aliases=aliases,
        compiler_params=_params(*(("arbitrary",) * len(grid))))(*args, *comm.inputs)
    core = res[:n_out]
    return (core[0] if single else core), list(res[n_out:])


def _comm_call(name, comm):
    nci, nco = len(comm.inputs), len(comm.out_shapes)

    def body(*refs):
        ci, co, cs = refs[:nci], refs[nci:nci + nco], refs[nci + nco:]
        comm.start(ci, co, cs)
        comm.finish(ci, co, cs)

    any_spec = pl.BlockSpec(memory_space=pl.ANY)
    return pl.pallas_call(
        body, name=name, in_specs=[any_spec] * nci, out_specs=[any_spec] * nco, out_shape=list(comm.out_shapes),
        scratch_shapes=list(comm.sems), compiler_params=pltpu.CompilerParams(has_side_effects=True))(*comm.inputs)


def _remote(src, dst, ssem, rsem, dev):
    return pltpu.make_async_remote_copy(src_ref=src, dst_ref=dst, send_sem=ssem, recv_sem=rsem, device_id=dev,
                                        device_id_type=pl.DeviceIdType.MESH)


def _place():
    x, y, c = lax.axis_index("x"), lax.axis_index("y"), lax.axis_index("c")
    other_chips = [(1 - x, y), (x, 1 - y), (1 - x, 1 - y)]
    return x, y, c, other_chips


def _slot(x, y, c, swap):
    return 4 * y + 2 * x + c if swap else 4 * x + 2 * y + c


def _chip_slot(x, y, swap):
    return 2 * y + x if swap else 2 * x + y


def _gather_comm(shards, swaps=None):
    n = len(shards)
    per = N_DEV - 1
    swaps = [False] * n if swaps is None else swaps
    pieces = []
    for i, a in enumerate(shards):
        rows = a.shape[0]
        k = GATHER_PIECES if (a.ndim == 2 and rows >= GATHER_PIECE_MIN_ROWS) else 1
        step = -(-rows // (k * 8)) * 8
        if k == 1:
            pieces.append((i, 0, None))
        else:
            pieces += [(i, r, min(step, rows - r)) for r in range(0, rows, step)]
    m = len(pieces)

    def src(ins, v):
        i, r, cnt = pieces[v]
        return ins[i] if cnt is None else ins[i].at[pl.ds(r, cnt)]

    def place(outs, v, x, y, c):
        i, r, cnt = pieces[v]
        blk = outs[i].at[_slot(x, y, c, swaps[i])]
        return blk if cnt is None else blk.at[pl.ds(r, cnt)]

    def start(ins, outs, sems):
        send, recv, loc = sems
        x, y, c, chips = _place()
        for v in range(m):
            me = place(outs, v, x, y, c)
            pltpu.make_async_copy(src(ins, v), me, loc.at[v]).start()
            _remote(src(ins, v), me, send.at[per * v], recv.at[per * v], (x, y, 1 - c)).start()
        for j, (qx, qy) in enumerate(chips):
            for v in range(m):
                _remote(src(ins, v), place(outs, v, x, y, c), send.at[per * v + 1 + j], recv.at[per * v + 1 + j],
                        (qx, qy, c)).start()

    def finish(ins, outs, sems):
        send, recv, loc = sems
        x, y, c, chips = _place()
        sib = (x, y, 1 - c)
        for v in range(m):
            for j, (qx, qy) in enumerate(chips):
                blk = place(outs, v, qx, qy, c)
                _remote(blk, blk, send.at[per * v + 1 + j], recv.at[per * v + 1 + j], (qx, qy, c)).wait_recv()
                _remote(blk, blk, send.at[per * v + 4 + j], recv.at[per * v + 4 + j], sib).start()
        for v in range(m):
            blk = place(outs, v, x, y, 1 - c)
            _remote(blk, blk, send.at[per * v], recv.at[per * v], sib).wait_recv()
            for j, (qx, qy) in enumerate(chips):
                blk = place(outs, v, qx, qy, 1 - c)
                _remote(blk, blk, send.at[per * v + 4 + j], recv.at[per * v + 4 + j], sib).wait_recv()
        for v in range(m):
            own = place(outs, v, x, y, c)
            for k in range(per):
                _remote(src(ins, v), own, send.at[per * v + k], recv.at[per * v + k], sib).wait_send()
            pltpu.make_async_copy(src(ins, v), own, loc.at[v]).wait()

    out_shapes = [jax.ShapeDtypeStruct((N_DEV,) + tuple(a.shape), a.dtype) for a in shards]
    sems = [pltpu.SemaphoreType.DMA((per * m,)), pltpu.SemaphoreType.DMA((per * m,)), pltpu.SemaphoreType.DMA((m,))]
    return _Comm(list(shards), out_shapes, sems, start, finish)


def _pair_comm(slots):
    n = len(slots)

    def copies(ins, outs, sems):
        send, recv = sems
        x, y, c, _ = _place()
        sib = (x, y, 1 - c)
        out = []
        for i in range(n):
            for q in range(4):
                out.append(_remote(ins[i].at[2 * q + 1 - c], outs[i].at[q], send.at[4 * i + q], recv.at[4 * i + q], sib))
        return out

    def start(ins, outs, sems):
        for cp in copies(ins, outs, sems):
            cp.start()

    def finish(ins, outs, sems):
        for cp in copies(ins, outs, sems):
            cp.wait_send()
            cp.wait_recv()

    out_shapes = [jax.ShapeDtypeStruct((4,) + tuple(a.shape[1:]), a.dtype) for a in slots]
    sems = [pltpu.SemaphoreType.DMA((4 * n,)), pltpu.SemaphoreType.DMA((4 * n,))]
    return _Comm(list(slots), out_shapes, sems, start, finish)


def _chip_comm(chip_sums, swaps=None, rows=None):
    n = len(chip_sums)
    swaps = [False] * n if swaps is None else swaps
    rows = [None] * n if rows is None else rows

    def src(ins, i, q):
        return ins[i].at[q] if rows[i] is None else ins[i].at[q, pl.ds(rows[i][0], rows[i][1] - rows[i][0])]

    def start(ins, outs, sems):
        send, recv, loc = sems
        x, y, c, chips = _place()
        for i in range(n):
            mine = _chip_slot(x, y, swaps[i])
            pltpu.make_async_copy(src(ins, i, mine), outs[i].at[mine], loc.at[i]).start()
            for j, (qx, qy) in enumerate(chips):
                _remote(src(ins, i, _chip_slot(qx, qy, swaps[i])), outs[i].at[mine], send.at[3 * i + j],
                        recv.at[3 * i + j], (qx, qy, c)).start()

    def finish(ins, outs, sems):
        send, recv, loc = sems
        x, y, c, chips = _place()
        for i in range(n):
            mine = _chip_slot(x, y, swaps[i])
            for j, (qx, qy) in enumerate(chips):
                theirs = _chip_slot(qx, qy, swaps[i])
                cp = _remote(src(ins, i, theirs), outs[i].at[theirs], send.at[3 * i + j], recv.at[3 * i + j], (qx, qy, c))
                cp.wait_send()
                cp.wait_recv()
            pltpu.make_async_copy(src(ins, i, mine), outs[i].at[mine], loc.at[i]).wait()

    def out_shape(a, r):
        shape = a.shape if r is None else (a.shape[0], r[1] - r[0]) + tuple(a.shape[2:])
        return jax.ShapeDtypeStruct(shape, a.dtype)

    out_shapes = [out_shape(a, r) for a, r in zip(chip_sums, rows)]
    sems = [pltpu.SemaphoreType.DMA((3 * n,)), pltpu.SemaphoreType.DMA((3 * n,)), pltpu.SemaphoreType.DMA((n,))]
    return _Comm(list(chip_sums), out_shapes, sems, start, finish)


def _join_comm(a, b):
    na_i, na_o, na_s = len(a.inputs), len(a.out_shapes), len(a.sems)

    def start(ins, outs, sems):
        a.start(ins[:na_i], outs[:na_o], sems[:na_s])
        b.start(ins[na_i:], outs[na_o:], sems[na_s:])

    def finish(ins, outs, sems):
        a.finish(ins[:na_i], outs[:na_o], sems[:na_s])
        b.finish(ins[na_i:], outs[na_o:], sems[na_s:])

    return _Comm(a.inputs + b.inputs, a.out_shapes + b.out_shapes, a.sems + b.sems, start, finish)


def _row_tile(r):
    for cand in (256, 128):
        if r > cand and r % cand == 0:
            return cand
    return r


def _add_pairs(name, slots, sib):
    r, c = slots.shape[1:]
    tr = _row_tile(r)

    def body(core_ref, s_ref, b_ref, o_ref):
        o_ref[...] = (s_ref[...].astype(F32) + b_ref[...].astype(F32)).astype(o_ref.dtype)

    core = jnp.full((1,), lax.axis_index("c"), jnp.int32)
    return pl.pallas_call(
        body, name=name,
        grid_spec=pltpu.PrefetchScalarGridSpec(
            num_scalar_prefetch=1, grid=(4, r // tr),
            in_specs=[pl.BlockSpec((None, tr, c), lambda q, i, core_ref: (2 * q + core_ref[0], i, 0)),
                      pl.BlockSpec((None, tr, c), lambda q, i, core_ref: (q, i, 0))],
            out_specs=pl.BlockSpec((None, tr, c), lambda q, i, core_ref: (q, i, 0))),
        out_shape=jax.ShapeDtypeStruct((4, r, c), slots.dtype),
        compiler_params=_params("parallel", "parallel"))(core, slots, sib)


def _matmul(name, mode, a, b, grid, a_spec, b_spec, o_spec, out_shape, acc_shape,
            res=None, res_spec=None, alpha=1.0, comm=None):
    nk = grid[-1]
    has_res = res is not None

    def body(*refs):
        if has_res:
            a_ref, b_ref, r_ref, o_ref = refs[:4]
        else:
            a_ref, b_ref, o_ref = refs[:3]
            r_ref = None
        part = _dot(a_ref[...], b_ref[...], mode)

        def finish(v):
            if alpha != 1.0:
                v = v * alpha
            if has_res:
                v = r_ref[...] + v
            o_ref[...] = v.astype(o_ref.dtype)

        if nk == 1:
            finish(part)
        else:
            acc = refs[-1]
            k = pl.program_id(len(grid) - 1)

            @pl.when(k == 0)
            def _():
                acc[...] = part

            @pl.when(k > 0)
            def _():
                acc[...] += part

            @pl.when(k == nk - 1)
            def _():
                finish(acc[...])

    in_specs = [a_spec, b_spec] + ([res_spec] if has_res else [])
    args = (a, b) + ((res,) if has_res else ())
    scratch = [] if nk == 1 else [pltpu.VMEM(acc_shape, F32)]
    sem = ("parallel",) * (len(grid) - 1) + ("arbitrary",)
    out, couts = _pcall(name, body, grid, in_specs, o_spec, out_shape, args, scratch, sem, comm)
    return out if comm is None else (out, couts)


def _mm_nn(name, a, b, out_dtype=F32, res=None, alpha=1.0, tk=None, kk=None, a_off=0, b_off=0, comm=None):
    t = a.shape[0]
    kk = a.shape[1] if kk is None else kk
    n = b.shape[1]
    tk = kk if tk is None else tk
    grid = (t // TM, 1, kk // tk)
    return _matmul(
        name, "nn", a, b, grid,
        pl.BlockSpec((TM, tk), lambda i, j, k: (i, k + a_off)),
        pl.BlockSpec((tk, n), lambda i, j, k: (k + b_off, 0)),
        pl.BlockSpec((TM, n), lambda i, j, k: (i, 0)),
        jax.ShapeDtypeStruct((t, n), out_dtype), (TM, n),
        res=res, res_spec=pl.BlockSpec((TM, n), lambda i, j, k: (i, 0)), alpha=alpha, comm=comm)


def _mm_nt(name, a, b, n=None, tn=None, tk=None, out_dtype=F32, comm=None):
    t, kk = a.shape
    n = b.shape[0] if n is None else n
    tn = n if tn is None else tn
    tk = kk if tk is None else tk
    grid = (n // tn, t // TM, kk // tk)
    return _matmul(
        name, "nt", a, b, grid,
        pl.BlockSpec((TM, tk), lambda j, i, k: (i, k)),
        pl.BlockSpec((tn, tk), lambda j, i, k: (j, k)),
        pl.BlockSpec((TM, tn), lambda j, i, k: (i, j)),
        jax.ShapeDtypeStruct((t, n), out_dtype), (TM, tn), comm=comm)


def _mm_tn(name, a, b, out_dtype, tm=None, n=None, col_off=0, comm=None):
    t, m = a.shape
    n = b.shape[1] if n is None else n
    tm = m if tm is None else tm
    tk = t if t <= TN_MAX_TOKENS else TM
    grid = (m // tm, 1, t // tk)
    return _matmul(
        name, "tn", a, b, grid,
        pl.BlockSpec((tk, tm), lambda j, i, k: (k, j)),
        pl.BlockSpec((tk, n), lambda j, i, k: (k, col_off)),
        pl.BlockSpec((tm, n), lambda j, i, k: (j, 0)),
        jax.ShapeDtypeStruct((m, n), out_dtype), (tm, n), comm=comm)


def _rms_fwd(name, x, w):
    t, d = x.shape

    def body(x_ref, w_ref, h_ref):
        xv = x_ref[...]
        rstd = lax.rsqrt(jnp.mean(xv * xv, axis=-1, keepdims=True) + NORM_EPS)
        h_ref[...] = (xv * rstd * w_ref[...]).astype(h_ref.dtype)

    return pl.pallas_call(
        body, name=name, grid=(t // TE,),
        in_specs=[pl.BlockSpec((TE, d), lambda i: (i, 0)), pl.BlockSpec((1, d), lambda i: (0, 0))],
        out_specs=pl.BlockSpec((TE, d), lambda i: (i, 0)),
        out_shape=jax.ShapeDtypeStruct((t, d), BF), compiler_params=_params("parallel"))(x, w)


def _rms_bwd(name, x, w, dh, dres, out_scale, comm=None):
    t, d = x.shape

    def body(x_ref, w_ref, dh_ref, dres_ref, dx_ref, dxb_ref, dw_ref):
        i = pl.program_id(0)
        xv = x_ref[...]
        rstd = lax.rsqrt(jnp.mean(xv * xv, axis=-1, keepdims=True) + NORM_EPS)
        xhat = xv * rstd
        dhv = dh_ref[...]
        wd = dhv * w_ref[...]
        proj = jnp.mean(wd * xhat, axis=-1, keepdims=True)
        dx = dres_ref[...] + rstd * (wd - xhat * proj)
        dx_ref[...] = dx
        dxb_ref[...] = (dx * out_scale).astype(BF)
        part = jnp.sum(dhv * xhat, axis=0, keepdims=True)

        @pl.when(i == 0)
        def _():
            dw_ref[...] = part

        @pl.when(i > 0)
        def _():
            dw_ref[...] += part

    row = pl.BlockSpec((TE, d), lambda i: (i, 0))
    vec = pl.BlockSpec((1, d), lambda i: (0, 0))
    outs, couts = _pcall(
        name, body, (t // TE,), [row, vec, row, row], [row, row, vec],
        [jax.ShapeDtypeStruct((t, d), F32), jax.ShapeDtypeStruct((t, d), BF), jax.ShapeDtypeStruct((1, d), F32)],
        (x, w, dh, dres), (), ("arbitrary",), comm)
    return outs if comm is None else (outs, couts)


def _final_loss(x, w, target):
    t, d = x.shape

    def body(x_ref, w_ref, t_ref, loss_ref, dx_ref, dxb_ref, dw_ref):
        i = pl.program_id(0)
        xv = x_ref[...]
        rstd = lax.rsqrt(jnp.mean(xv * xv, axis=-1, keepdims=True) + NORM_EPS)
        xhat = xv * rstd
        err = xhat * w_ref[...] - t_ref[...]
        lpart = 0.5 * jnp.sum(jnp.mean(err * err, axis=-1, keepdims=True), axis=0, keepdims=True)
        dy = err * (1.0 / d)
        wd = dy * w_ref[...]
        proj = jnp.mean(wd * xhat, axis=-1, keepdims=True)
        dx = rstd * (wd - xhat * proj)
        dx_ref[...] = dx
        dxb_ref[...] = (0.5 * dx).astype(BF)
        part = jnp.sum(dy * xhat, axis=0, keepdims=True)
        lfull = jnp.broadcast_to(lpart, (1, 128))

        @pl.when(i == 0)
        def _():
            dw_ref[...] = part
            loss_ref[...] = lfull

        @pl.when(i > 0)
        def _():
            dw_ref[...] += part
            loss_ref[...] += lfull

    row = pl.BlockSpec((TE, d), lambda i: (i, 0))
    vec = pl.BlockSpec((1, d), lambda i: (0, 0))
    return pl.pallas_call(
        body, name="final_loss", grid=(t // TE,), in_specs=[row, vec, row],
        out_specs=[pl.BlockSpec((1, 128), lambda i: (0, 0)), row, row, vec],
        out_shape=[jax.ShapeDtypeStruct((1, 128), F32), jax.ShapeDtypeStruct((t, d), F32),
                   jax.ShapeDtypeStruct((t, d), BF), jax.ShapeDtypeStruct((1, d), F32)],
        compiler_params=_params("arbitrary"))(x, w, target)


def _swiglu_fwd(name, gu, comm=None):
    t = gu.shape[0]

    def body(g_ref, u_ref, a_ref):
        g = g_ref[...].astype(F32)
        a_ref[...] = (g * _sigmoid(g) * u_ref[...].astype(F32)).astype(BF)

    blk = (TE, FF_HALF)
    out, couts = _pcall(
        name, body, (t // TE, 2),
        [pl.BlockSpec(blk, lambda i, j: (i, 2 * j)), pl.BlockSpec(blk, lambda i, j: (i, 2 * j + 1))],
        pl.BlockSpec(blk, lambda i, j: (i, j)), jax.ShapeDtypeStruct((t, D_FF), BF),
        (gu, gu), (), ("parallel", "parallel"), comm)
    return out if comm is None else (out, couts)


def _swiglu_bwd(name, gu, dact, comm=None):
    t = gu.shape[0]

    def body(g_ref, u_ref, da_ref, o_ref):
        g = g_ref[...].astype(F32)
        da = da_ref[...].astype(F32)
        s = _sigmoid(g)
        o_ref[:, 0:FF_HALF] = (da * u_ref[...].astype(F32) * (s * (1.0 + g * (1.0 - s)))).astype(BF)
        o_ref[:, FF_HALF:2 * FF_HALF] = (da * g * s).astype(BF)

    blk = (TE, FF_HALF)
    out, couts = _pcall(
        name, body, (t // TE, 2),
        [pl.BlockSpec(blk, lambda i, j: (i, 2 * j)), pl.BlockSpec(blk, lambda i, j: (i, 2 * j + 1)),
         pl.BlockSpec(blk, lambda i, j: (i, j))],
        pl.BlockSpec((TE, 2 * FF_HALF), lambda i, j: (i, j)),
        jax.ShapeDtypeStruct((t, 2 * D_FF), BF), (gu, gu, dact), (), ("parallel", "parallel"), comm)
    return out if comm is None else (out, couts)


CONV_CB = 256


CONV_ROWS = 64
CONV_HALO = 16


def _taps_down(ext, w, k):
    shifted = [pltpu.roll(ext, k - 1 - j, 0)[CONV_HALO:] for j in range(k - 1)] + [ext[CONV_HALO:]]
    out = shifted[k - 1] * w[k - 1:k, :]
    for j in range(k - 1):
        out = out + shifted[j] * w[j:j + 1, :]
    return out, shifted


def _taps_up(ext, w, k):
    rows = ext.shape[0]
    n = rows - CONV_HALO
    out = ext[:n] * w[k - 1:k, :]
    for j in range(k - 1):
        out = out + pltpu.roll(ext, rows - (k - 1 - j), 0)[:n] * w[j:j + 1, :]
    return out


def _rows_before(ref, i, r0):
    start = pl.multiple_of(jnp.maximum(r0 - CONV_HALO, 0), CONV_HALO)
    return jnp.where(i > 0, ref[pl.ds(start, CONV_HALO), :].astype(F32), 0.0)


def _rows_after(ref, r0, t):
    start = pl.multiple_of(jnp.minimum(r0 + CONV_ROWS, t - CONV_HALO), CONV_HALO)
    return ref[pl.ds(start, CONV_HALO), :].astype(F32)


def _fold8(v):
    return v.reshape(v.shape[0] // 8, 8, v.shape[1]).sum(axis=0)


def _silu_grad(pre):
    s = _sigmoid(pre)
    return s * (1.0 + pre * (1.0 - s))


def _pspec(t, off):
    base = off // CONV_CB
    return pl.BlockSpec((t, CONV_CB), lambda j: (0, base + j))


def _mix_a_fwd(p, conv_w):
    t = p.shape[0]

    def body(b_ref, c_ref, xa_ref, w_ref, o_ref):
        w = w_ref[...]

        def step(i, carry):
            r0 = pl.multiple_of(i * CONV_ROWS, CONV_ROWS)
            rows = pl.ds(r0, CONV_ROWS)
            q = c_ref[rows, :].astype(F32) * xa_ref[rows, :].astype(F32)
            q_before = _rows_before(c_ref, i, r0) * _rows_before(xa_ref, i, r0)
            va, _ = _taps_down(jnp.concatenate([q_before, q], axis=0), w, 3)
            o_ref[rows, :] = (b_ref[rows, :].astype(F32) * va).astype(BF)
            return carry

        lax.fori_loop(0, t // CONV_ROWS, step, 0)

    return pl.pallas_call(
        body, name="mix_a_fwd", grid=(D_MODEL // CONV_CB,),
        in_specs=[_pspec(t, OFF_B), _pspec(t, OFF_C), _pspec(t, OFF_XA),
                  pl.BlockSpec((3, CONV_CB), lambda j: (0, j))],
        out_specs=pl.BlockSpec((t, CONV_CB), lambda j: (0, j)),
        out_shape=jax.ShapeDtypeStruct((t, D_MODEL), BF), compiler_params=_params("parallel"))(p, p, p, conv_w)


def _mix_a_bwd(p, conv_w, dya, dp):
    t = p.shape[0]

    def body(b_ref, c_ref, xa_ref, w_ref, dy_ref, dp_in, dp_ref, dw_ref):
        del dp_in
        w = w_ref[...]
        n = t // CONV_ROWS

        def step(i, acc):
            r0 = pl.multiple_of(i * CONV_ROWS, CONV_ROWS)
            rows = pl.ds(r0, CONV_ROWS)
            cv = c_ref[rows, :].astype(F32)
            xav = xa_ref[rows, :].astype(F32)
            q_before = _rows_before(c_ref, i, r0) * _rows_before(xa_ref, i, r0)
            va, shifted = _taps_down(jnp.concatenate([q_before, cv * xav], axis=0), w, 3)
            dyv = dy_ref[rows, :]
            dp_ref[rows, 0:CONV_CB] = (dyv * va).astype(BF)
            dv = dyv * b_ref[rows, :].astype(F32)
            dv_after = jnp.where(i < n - 1, _rows_after(dy_ref, r0, t) * _rows_after(b_ref, r0, t), 0.0)
            dq = _taps_up(jnp.concatenate([dv, dv_after], axis=0), w, 3)
            dp_ref[rows, CONV_CB:2 * CONV_CB] = (dq * xav).astype(BF)
            dp_ref[rows, 2 * CONV_CB:3 * CONV_CB] = (dq * cv).astype(BF)
            return tuple(a + _fold8(dv * s) for a, s in zip(acc, shifted))

        zero = jnp.zeros((8, CONV_CB), F32)
        acc = lax.fori_loop(0, n, step, (zero, zero, zero))
        for j in range(3):
            dw_ref[j:j + 1, :] = jnp.sum(acc[j], axis=0, keepdims=True)

    col = pl.BlockSpec((t, CONV_CB), lambda j: (0, j))
    wsp = pl.BlockSpec((3, CONV_CB), lambda j: (0, j))
    return pl.pallas_call(
        body, name="mix_a_bwd", grid=(D_MODEL // CONV_CB,),
        in_specs=[_pspec(t, OFF_B), _pspec(t, OFF_C), _pspec(t, OFF_XA), wsp, col, pl.BlockSpec(memory_space=pl.ANY)],
        out_specs=[pl.BlockSpec((t, 3 * CONV_CB), lambda j: (0, j)), wsp],
        out_shape=[jax.ShapeDtypeStruct(dp.shape, dp.dtype), jax.ShapeDtypeStruct((3, D_MODEL), F32)],
        input_output_aliases={5: 0},
        compiler_params=_params("parallel"))(p, p, p, conv_w, dya, dp)


def _ssm_conv_fwd(p, conv_w, conv_b, comm=None):
    t = p.shape[0]

    def body(x_ref, w_ref, b_ref, o_ref):
        w = w_ref[...]
        bias = b_ref[...]

        def step(i, carry):
            r0 = pl.multiple_of(i * CONV_ROWS, CONV_ROWS)
            rows = pl.ds(r0, CONV_ROWS)
            ext = jnp.concatenate([_rows_before(x_ref, i, r0), x_ref[rows, :].astype(F32)], axis=0)
            pre = _taps_down(ext, w, 4)[0] + bias
            o_ref[rows, :] = pre * _sigmoid(pre)
            return carry

        lax.fori_loop(0, t // CONV_ROWS, step, 0)

    out, couts = _pcall(
        "ssm_conv_fwd", body, (D_XBC // CONV_CB,),
        [_pspec(t, OFF_XBC), pl.BlockSpec((4, CONV_CB), lambda j: (0, j)), pl.BlockSpec((1, CONV_CB), lambda j: (0, j))],
        pl.BlockSpec((t, CONV_CB), lambda j: (0, j)), jax.ShapeDtypeStruct((t, D_XBC), F32),
        (p, conv_w, conv_b), (), ("parallel",), comm)
    return out if comm is None else (out, couts)


def _ssm_conv_bwd(p, conv_w, conv_b, dxc, dp, comm=None):
    t = p.shape[0]

    def body(x_ref, w_ref, b_ref, d_ref, dp_in, dx_ref, dw_ref, db_ref):
        del dp_in
        w = w_ref[...]
        bias = b_ref[...]
        n = t // CONV_ROWS

        def step(i, acc):
            r0 = pl.multiple_of(i * CONV_ROWS, CONV_ROWS)
            rows = pl.ds(r0, CONV_ROWS)
            x_cur = x_ref[rows, :].astype(F32)
            pre, shifted = _taps_down(jnp.concatenate([_rows_before(x_ref, i, r0), x_cur], axis=0), w, 4)
            pre = pre + bias
            dpre = d_ref[rows, :] * _silu_grad(pre)
            ext_after = jnp.concatenate([x_cur[CONV_ROWS - CONV_HALO:], _rows_after(x_ref, r0, t)], axis=0)
            pre_after = _taps_down(ext_after, w, 4)[0] + bias
            dpre_after = jnp.where(i < n - 1, _rows_after(d_ref, r0, t) * _silu_grad(pre_after), 0.0)
            dx_ref[rows, :] = _taps_up(jnp.concatenate([dpre, dpre_after], axis=0), w, 4).astype(BF)
            new = tuple(a + _fold8(dpre * s) for a, s in zip(acc[:4], shifted))
            return new + (acc[4] + _fold8(dpre),)

        zero = jnp.zeros((8, CONV_CB), F32)
        acc = lax.fori_loop(0, n, step, (zero,) * 5)
        for j in range(4):
            dw_ref[j:j + 1, :] = jnp.sum(acc[j], axis=0, keepdims=True)
        db_ref[...] = jnp.sum(acc[4], axis=0, keepdims=True)

    col = pl.BlockSpec((t, CONV_CB), lambda j: (0, j))
    wsp = pl.BlockSpec((4, CONV_CB), lambda j: (0, j))
    bsp = pl.BlockSpec((1, CONV_CB), lambda j: (0, j))
    outs, couts = _pcall(
        "ssm_conv_bwd", body, (D_XBC // CONV_CB,),
        [_pspec(t, OFF_XBC), wsp, bsp, col, pl.BlockSpec(memory_space=pl.ANY)], [_pspec(t, OFF_XBC), wsp, bsp],
        [jax.ShapeDtypeStruct(dp.shape, dp.dtype), jax.ShapeDtypeStruct((4, D_XBC), F32),
         jax.ShapeDtypeStruct((1, D_XBC), F32)],
        (p, conv_w, conv_b, dxc, dp), (), ("parallel",), comm, aliases={4: 0})
    return outs if comm is None else (outs, couts)


DT_ROWS = 512


def _tri(lower):
    r = lax.broadcasted_iota(jnp.int32, (CHUNK, CHUNK), 0)
    c = lax.broadcasted_iota(jnp.int32, (CHUNK, CHUNK), 1)
    return jnp.where((r >= c) if lower else (r <= c), 1.0, 0.0).astype(F32)


def _dot_exact(a, b):
    return lax.dot_general(a, b, _DIMS["nn"], preferred_element_type=F32, precision=lax.Precision.HIGHEST)


def _dt_fwd(p, bias_pad, alog_pad):
    t = p.shape[0]

    def body(raw_ref, b_ref, al_ref, dt_ref, acs_ref):
        z = raw_ref[...] + b_ref[...]
        dt = jnp.maximum(z, 0.0) + jnp.log(1.0 + jnp.exp(-jnp.abs(z)))
        dt_ref[...] = dt
        a = dt * (-jnp.exp(al_ref[...]))
        tri = _tri(True)
        for k in range(DT_ROWS // CHUNK):
            acs_ref[k * CHUNK:(k + 1) * CHUNK, :] = _dot_exact(tri, a[k * CHUNK:(k + 1) * CHUNK, :])

    blk = pl.BlockSpec((DT_ROWS, DT_W), lambda i: (i, 0))
    vec = pl.BlockSpec((1, DT_W), lambda i: (0, 0))
    return pl.pallas_call(
        body, name="dt_fwd", grid=(t // DT_ROWS,),
        in_specs=[pl.BlockSpec((DT_ROWS, DT_W), lambda i: (i, OFF_DT // DT_W)), vec, vec],
        out_specs=[blk, blk], out_shape=[jax.ShapeDtypeStruct((t, DT_W), F32)] * 2,
        compiler_params=_params("parallel"))(p, bias_pad, alog_pad)


def _dt_bwd(p, bias_pad, alog_pad, dt, ddt, dacs, dp_gd):
    t = p.shape[0]

    def body(raw_ref, b_ref, al_ref, dt_ref, ddt_ref, dacs_ref, dp_in, draw_ref, db_ref, dal_ref):
        del dp_in
        i = pl.program_id(0)
        acoef = -jnp.exp(al_ref[...])
        triu = _tri(False)
        das = []
        for k in range(DT_ROWS // CHUNK):
            das.append(_dot_exact(triu, dacs_ref[k * CHUNK:(k + 1) * CHUNK, :]))
        da = jnp.concatenate(das, axis=0)
        dtv = dt_ref[...]
        ddt_tot = ddt_ref[...] + da * acoef
        lane = lax.broadcasted_iota(jnp.int32, (DT_ROWS, DT_W), 1)
        draw = jnp.where(lane < N_HEADS, ddt_tot * _sigmoid(raw_ref[...] + b_ref[...]), 0.0)
        draw_ref[...] = draw.astype(BF)
        pb = jnp.sum(draw, axis=0, keepdims=True)
        pa = jnp.sum(da * dtv * acoef, axis=0, keepdims=True)

        @pl.when(i == 0)
        def _():
            db_ref[...] = pb
            dal_ref[...] = pa

        @pl.when(i > 0)
        def _():
            db_ref[...] += pb
            dal_ref[...] += pa

    blk = pl.BlockSpec((DT_ROWS, DT_W), lambda i: (i, 0))
    vec = pl.BlockSpec((1, DT_W), lambda i: (0, 0))
    return pl.pallas_call(
        body, name="dt_bwd", grid=(t // DT_ROWS,),
        in_specs=[pl.BlockSpec((DT_ROWS, DT_W), lambda i: (i, OFF_DT // DT_W)), vec, vec, blk, blk, blk,
                  pl.BlockSpec(memory_space=pl.ANY)],
        out_specs=[pl.BlockSpec((DT_ROWS, DT_W), lambda i: (i, OFF_DT // DT_W)), vec, vec],
        out_shape=[jax.ShapeDtypeStruct(dp_gd.shape, dp_gd.dtype), jax.ShapeDtypeStruct((1, DT_W), F32),
                   jax.ShapeDtypeStruct((1, DT_W), F32)],
        input_output_aliases={6: 0},
        compiler_params=_params("arbitrary"))(p, bias_pad, alog_pad, dt, ddt, dacs, dp_gd)


def _split_dot(z, onehot, terms):
    out = None
    rest = z
    for _ in range(terms):
        piece = rest.astype(BF)
        part = _dot(piece, onehot)
        out = part if out is None else out + part
        rest = rest - piece.astype(F32)
    return out


def _spread_mat():
    row = lax.broadcasted_iota(jnp.int32, (DT_W, D_INNER), 0)
    lane = lax.broadcasted_iota(jnp.int32, (DT_W, D_INNER), 1)
    return jnp.where(row == lane // HEAD_DIM, 1.0, 0.0).astype(BF)


def _gather_mat():
    row = lax.broadcasted_iota(jnp.int32, (D_INNER, DT_W), 0)
    lane = lax.broadcasted_iota(jnp.int32, (D_INNER, DT_W), 1)
    return jnp.where(lane == row // HEAD_DIM, 1.0, 0.0).astype(BF)


def _ssd_masks():
    row = lax.broadcasted_iota(jnp.int32, (CHUNK, GROUP_W), 0)
    col = lax.broadcasted_iota(jnp.int32, (CHUNK, GROUP_W), 1) % HEAD_DIM
    brow = lax.broadcasted_iota(jnp.int32, (GROUP_W, GROUP_W), 0) // HEAD_DIM
    bcol = lax.broadcasted_iota(jnp.int32, (GROUP_W, GROUP_W), 1) // HEAD_DIM
    return row >= col, row == col, brow == bcol


def _stack4(v):
    return jnp.concatenate([v, v, v, v], axis=0)


def _fold4(v):
    return v[0:CHUNK] + v[CHUNK:2 * CHUNK] + v[2 * CHUNK:3 * CHUNK] + v[3 * CHUNK:4 * CHUNK]


def _ssd_group(xc_ref, wide_ref, g, tri, eye, blockdiag):
    gs = slice(GROUP_W * g, GROUP_W * (g + 1))
    xs_g = xc_ref[:, gs]
    b_g = xc_ref[:, D_INNER + D_STATE * g:D_INNER + D_STATE * (g + 1)].astype(BF)
    c_g = xc_ref[:, D_INNER + 1024 + D_STATE * g:D_INNER + 1024 + D_STATE * (g + 1)].astype(BF)
    acs_e, dt_e = wide_ref[0:CHUNK, gs], wide_ref[CHUNK:2 * CHUNK, gs]
    atot_e = acs_e[CHUNK - 1:CHUNK, :]
    acs_j = jnp.sum(jnp.where(eye, acs_e, 0.0), axis=0, keepdims=True)
    lmat = jnp.where(tri, jnp.exp(jnp.minimum(acs_e - acs_j, 0.0)), 0.0)
    b_t = _stack4(b_g)
    m = _dot(c_g, b_t, "nt") * lmat
    x_g = xs_g * dt_e
    xbd = jnp.where(blockdiag, _stack4(x_g), 0.0).astype(BF)
    return dict(gs=gs, xs=xs_g, b=b_g, c=c_g, b_t=b_t, dt=dt_e, e=jnp.exp(acs_e), dec=jnp.exp(atot_e - acs_e),
                eat=jnp.exp(atot_e), lmat=lmat, m=m, x=x_g, xbd=xbd)


def _ssd_fwd(xconv, dt, acs, d_exp, comm=None):
    t = xconv.shape[0]
    nc = t // CHUNK

    def body(xc_ref, dt_ref, acs_ref, d_ref, y_ref, hs_ref, state, wide):
        c = pl.program_id(0)

        @pl.when(c == 0)
        def _():
            state[...] = jnp.zeros_like(state)

        hs_ref[...] = state[...]
        tri, eye, blockdiag = _ssd_masks()
        wide[...] = _split_dot(jnp.concatenate([acs_ref[...], dt_ref[...]], axis=0), _spread_mat(), 3)
        for g in range(N_GROUPS):
            q = _ssd_group(xc_ref, wide, g, tri, eye, blockdiag)
            gs = q["gs"]
            h_t = state[:, gs]
            ydiag = _dot(q["m"].astype(BF), q["xbd"])
            yoff = _dot(q["c"], h_t.astype(BF)) * q["e"]
            y_ref[:, gs] = ydiag + yoff + d_ref[:, gs] * q["xs"]
            s_t = _dot(q["b"], (q["x"] * q["dec"]).astype(BF), "tn")
            state[:, gs] = q["eat"] * h_t + s_t

    blk = lambda w: pl.BlockSpec((CHUNK, w), lambda c: (c, 0))
    outs, couts = _pcall(
        "ssd_fwd", body, (nc,),
        [blk(D_XBC), blk(DT_W), blk(DT_W), pl.BlockSpec((1, D_INNER), lambda c: (0, 0))],
        [blk(D_INNER), pl.BlockSpec((None, D_STATE, D_INNER), lambda c: (c, 0, 0))],
        [jax.ShapeDtypeStruct((t, D_INNER), F32), jax.ShapeDtypeStruct((nc, D_STATE, D_INNER), F32)],
        (xconv, dt, acs, d_exp), [pltpu.VMEM((D_STATE, D_INNER), F32), pltpu.VMEM((2 * CHUNK, D_INNER), F32)],
        ("arbitrary",), comm)
    return outs if comm is None else (outs, couts)


def _ssd_bwd(xconv, dt, acs, d_exp, hsave, dy, comm=None):
    t = xconv.shape[0]
    nc = t // CHUNK

    def body(xc_ref, dt_ref, acs_ref, d_ref, hs_ref, dy_ref, dxc_ref, ddt_ref, dacs_ref, dd_ref, dstate, wide, per_head):
        c = pl.program_id(0)

        @pl.when(c == 0)
        def _():
            dstate[...] = jnp.zeros_like(dstate)
            dd_ref[...] = jnp.zeros_like(dd_ref)

        tri, eye, blockdiag = _ssd_masks()
        acsv = acs_ref[...]
        wide[...] = _split_dot(jnp.concatenate([acsv, dt_ref[...]], axis=0), _spread_mat(), 3)
        eat_heads = jnp.exp(acsv[CHUNK - 1:CHUNK, :])

        for g in range(N_GROUPS):
            q = _ssd_group(xc_ref, wide, g, tri, eye, blockdiag)
            gs, xs_g, b_g, c_g, m = q["gs"], q["xs"], q["b"], q["c"], q["m"]
            bs = slice(D_INNER + D_STATE * g, D_INNER + D_STATE * (g + 1))
            cs = slice(D_INNER + 1024 + D_STATE * g, D_INNER + 1024 + D_STATE * (g + 1))
            h_t = hs_ref[:, gs]
            h_b = h_t.astype(BF)
            dy_g = dy_ref[:, gs]
            dy_b = dy_g.astype(BF)
            ds_t = dstate[:, gs]
            ds_b = ds_t.astype(BF)

            yoff = _dot(c_g, h_b) * q["e"]
            edy = (q["e"] * dy_g).astype(BF)
            d_c = _dot(edy, h_b, "nt")
            d_ht = _dot(c_g, edy, "tn")
            bds = _dot(b_g, ds_b)
            xd = q["x"] * q["dec"]
            d_b = _dot(xd.astype(BF), ds_b, "nt")
            dm = _dot(dy_b, q["xbd"], "nt")
            cross = _dot(m.astype(BF), dy_b, "tn")
            dx_full = q["dec"] * bds + _fold4(jnp.where(blockdiag, cross, 0.0))
            dml = (dm * q["lmat"]).astype(BF)
            d_c = d_c + _dot(dml, q["b_t"])
            d_b = d_b + _fold4(_dot(dml, c_g, "tn"))
            w = dm * m
            q_dec = xd * bds
            z = w - jnp.where(eye, jnp.sum(w, axis=0, keepdims=True), 0.0) + dy_g * yoff - q_dec
            rows = jnp.concatenate(
                [jnp.sum(q_dec, axis=0, keepdims=True), jnp.sum(ds_t * h_t, axis=0, keepdims=True),
                 jnp.zeros((6, GROUP_W), F32)], axis=0)
            per_head[:, gs] = jnp.concatenate([z, dx_full * xs_g, rows], axis=0)
            dxc_ref[:, cs] = d_c
            dxc_ref[:, bs] = d_b
            dxc_ref[:, gs] = dx_full * q["dt"] + d_ref[:, gs] * dy_g
            dd_ref[:, gs] += jnp.sum(dy_g * xs_g, axis=0, keepdims=True)
            dstate[:, gs] = q["eat"] * ds_t + d_ht

        seg = _split_dot(per_head[...], _gather_mat(), 2)
        datot = seg[2 * CHUNK:2 * CHUNK + 1] + eat_heads * seg[2 * CHUNK + 1:2 * CHUNK + 2]
        rowi = lax.broadcasted_iota(jnp.int32, (CHUNK, DT_W), 0)
        ddt_ref[...] = seg[CHUNK:2 * CHUNK]
        dacs_ref[...] = seg[0:CHUNK] + jnp.where(rowi == CHUNK - 1, datot, 0.0)

    rev = lambda w: pl.BlockSpec((CHUNK, w), lambda c: (nc - 1 - c, 0))
    vec = pl.BlockSpec((1, D_INNER), lambda c: (0, 0))
    outs, couts = _pcall(
        "ssd_bwd", body, (nc,),
        [rev(D_XBC), rev(DT_W), rev(DT_W), vec,
         pl.BlockSpec((None, D_STATE, D_INNER), lambda c: (nc - 1 - c, 0, 0)), rev(D_INNER)],
        [rev(D_XBC), rev(DT_W), rev(DT_W), vec],
        [jax.ShapeDtypeStruct((t, D_XBC), F32), jax.ShapeDtypeStruct((t, DT_W), F32),
         jax.ShapeDtypeStruct((t, DT_W), F32), jax.ShapeDtypeStruct((1, D_INNER), F32)],
        (xconv, dt, acs, d_exp, hsave, dy),
        [pltpu.VMEM((D_STATE, D_INNER), F32), pltpu.VMEM((2 * CHUNK, D_INNER), F32),
         pltpu.VMEM((2 * CHUNK + 8, D_INNER), F32)], ("arbitrary",), comm)
    return outs if comm is None else (outs, couts)


GN_CB = 1024
GN_GROUPS = GN_CB // GROUP_W


def _gnorm_fwd(y, p, w, comm=None):
    t = y.shape[0]
    zoff = OFF_Z // GN_CB

    def body(y_ref, z_ref, w_ref, o_ref):
        for g in range(GN_GROUPS):
            gs = slice(GROUP_W * g, GROUP_W * (g + 1))
            z = z_ref[:, gs].astype(F32)
            yf = y_ref[:, gs] * (z * _sigmoid(z))
            rstd = lax.rsqrt(jnp.mean(yf * yf, axis=-1, keepdims=True) + NORM_EPS)
            o_ref[:, gs] = (yf * rstd * w_ref[:, gs]).astype(BF)

    blk = pl.BlockSpec((TE, GN_CB), lambda i, j: (i, j))
    out, couts = _pcall(
        "gnorm_fwd", body, (t // TE, D_INNER // GN_CB),
        [blk, pl.BlockSpec((TE, GN_CB), lambda i, j: (i, zoff + j)), pl.BlockSpec((1, GN_CB), lambda i, j: (0, j))],
        blk, jax.ShapeDtypeStruct((t, D_INNER), BF), (y, p, w), (), ("parallel", "parallel"), comm)
    return out if comm is None else (out, couts)


def _gnorm_bwd(y, p, w, dyn, comm=None):
    t = y.shape[0]
    zoff = OFF_Z // GN_CB

    def body(y_ref, z_ref, w_ref, dn_ref, dy_ref, dz_ref, dw_ref):
        i = pl.program_id(1)
        for g in range(GN_GROUPS):
            gs = slice(GROUP_W * g, GROUP_W * (g + 1))
            z = z_ref[:, gs].astype(F32)
            yv = y_ref[:, gs]
            s = _sigmoid(z)
            sil = z * s
            yf = yv * sil
            rstd = lax.rsqrt(jnp.mean(yf * yf, axis=-1, keepdims=True) + NORM_EPS)
            xhat = yf * rstd
            dn = dn_ref[:, gs]
            wd = dn * w_ref[:, gs]
            proj = jnp.mean(wd * xhat, axis=-1, keepdims=True)
            dyf = rstd * (wd - xhat * proj)
            dy_ref[:, gs] = dyf * sil
            dz_ref[:, gs] = (dyf * yv * (s * (1.0 + z * (1.0 - s)))).astype(BF)
            part = jnp.sum(dn * xhat, axis=0, keepdims=True)

            @pl.when(i == 0)
            def _():
                dw_ref[:, gs] = part

            @pl.when(i > 0)
            def _():
                dw_ref[:, gs] += part

    blk = pl.BlockSpec((TE, GN_CB), lambda j, i: (i, j))
    vec = pl.BlockSpec((1, GN_CB), lambda j, i: (0, j))
    outs, couts = _pcall(
        "gnorm_bwd", body, (D_INNER // GN_CB, t // TE),
        [blk, pl.BlockSpec((TE, GN_CB), lambda j, i: (i, zoff + j)), vec, blk],
        [blk, pl.BlockSpec((TE, GN_CB), lambda j, i: (i, zoff + j)), vec],
        [jax.ShapeDtypeStruct((t, D_INNER), F32), jax.ShapeDtypeStruct((t, N_MAIN), BF),
         jax.ShapeDtypeStruct((1, D_INNER), F32)],
        (y, p, w, dyn), (), ("parallel", "arbitrary"), comm)
    return outs if comm is None else (outs, couts)


MERGE_CB = 512


def _merge_fwd(p, ya, yb):
    t = ya.shape[0]

    def body(ga_ref, gb_ref, ya_ref, yb_ref, o_ref):
        o_ref[...] = (_sigmoid(ga_ref[...]) * ya_ref[...] + _sigmoid(gb_ref[...]) * yb_ref[...]).astype(BF)

    blk = pl.BlockSpec((TE, MERGE_CB), lambda i, j: (i, j))
    return pl.pallas_call(
        body, name="merge_fwd", grid=(t // TE, D_MODEL // MERGE_CB),
        in_specs=[pl.BlockSpec((TE, MERGE_CB), lambda i, j: (i, 2 * j)),
                  pl.BlockSpec((TE, MERGE_CB), lambda i, j: (i, 2 * j + 1)), blk, blk],
        out_specs=blk, out_shape=jax.ShapeDtypeStruct((t, D_MODEL), BF),
        compiler_params=_params("parallel", "parallel"))(p, p, ya, yb)


def _merge_bwd(p, ya, yb, dm):
    t = ya.shape[0]

    def body(ga_ref, gb_ref, ya_ref, yb_ref, dm_ref, dg_ref, dya_ref, dyb_ref):
        d = dm_ref[...]
        sa = _sigmoid(ga_ref[...])
        sb = _sigmoid(gb_ref[...])
        dg_ref[:, 0:MERGE_CB] = (d * ya_ref[...] * sa * (1.0 - sa)).astype(BF)
        dg_ref[:, MERGE_CB:2 * MERGE_CB] = (d * yb_ref[...] * sb * (1.0 - sb)).astype(BF)
        dya_ref[...] = (d * sa).astype(BF)
        dyb_ref[...] = (d * sb).astype(BF)

    blk = pl.BlockSpec((TE, MERGE_CB), lambda i, j: (i, j))
    return pl.pallas_call(
        body, name="merge_bwd", grid=(t // TE, D_MODEL // MERGE_CB),
        in_specs=[pl.BlockSpec((TE, MERGE_CB), lambda i, j: (i, 2 * j)),
                  pl.BlockSpec((TE, MERGE_CB), lambda i, j: (i, 2 * j + 1)), blk, blk, blk],
        out_specs=[pl.BlockSpec((TE, 2 * MERGE_CB), lambda i, j: (i, j)), blk, blk],
        out_shape=[jax.ShapeDtypeStruct((t, N_GD), BF)] + [jax.ShapeDtypeStruct((t, D_MODEL), BF)] * 2,
        compiler_params=_params("parallel", "parallel"))(p, p, ya, yb, dm)


def _adamw(name, parts, w, m, v, comm=None):
    r, c = w.shape
    tr = _row_tile(r)
    tc = ADAM_COL_TILE if (tr == r and r > 512 and c % ADAM_COL_TILE == 0) else c
    n_parts = parts.shape[0]
    bc1 = 1.0 - ADAM_B1 ** ADAM_STEP
    bc2 = 1.0 - ADAM_B2 ** ADAM_STEP

    def body(p_ref, w_ref, m_ref, v_ref, g_ref, d_ref, nm_ref, nv_ref):
        g = p_ref[0].astype(F32)
        for k in range(1, n_parts):
            g = g + p_ref[k].astype(F32)
        nm = ADAM_B1 * m_ref[...] + (1.0 - ADAM_B1) * g
        nv = ADAM_B2 * v_ref[...] + (1.0 - ADAM_B2) * (g * g)
        g_ref[...] = g
        nm_ref[...] = nm
        nv_ref[...] = nv
        d_ref[...] = -ADAM_LR * ((nm / bc1) / (jnp.sqrt(nv / bc2) + ADAM_EPS) + ADAM_WD * w_ref[...])

    blk = pl.BlockSpec((tr, tc), lambda i, j: (i, j))
    outs, couts = _pcall(
        name, body, (r // tr, c // tc),
        [pl.BlockSpec((n_parts, tr, tc), lambda i, j: (0, i, j)), blk, blk, blk], [blk] * 4,
        [jax.ShapeDtypeStruct((r, c), F32)] * 4, (parts, w, m, v), (), ("parallel", "parallel"), comm)
    return outs if comm is None else (outs, couts)


def _pad_lanes(v, width):
    return jnp.pad(v, ((0, 0), (0, width - v.shape[1])))


def _reduce_start(slots, host):
    outs, sib = host(_pair_comm([a for _, a in slots]))
    sums = [(n, _add_pairs("pairsum_" + n, a, b)) for (n, a), b in zip(slots, sib)]
    return outs, sums


def _train_step(x, target, shard, rep):
    gdt = BF
    recv = {}
    (got,) = _comm_call("gather_ffn1_in", _gather_comm([shard["ffn1_w_in"]], [True]))
    w1_in = got.reshape(2 * D_FF, D_MODEL)
    h1 = _rms_fwd("rms1_fwd", x, rep["ffn1_norm"])
    gu1, got = _mm_nt("ffn1_in", h1, w1_in, tn=FF_HALF, out_dtype=BF, comm=_gather_comm(
        [shard["ffn1_w_out"], shard["w_in"], shard["short_conv_w"], shard["ssm_conv_w"]]))
    w1_out = got[0].reshape(D_FF, D_MODEL)
    w_in_t = got[1].reshape(N_IN, D_MODEL)
    short_conv_w = got[2].transpose(1, 0, 2).reshape(3, D_MODEL)
    ssm_conv_w = got[3].transpose(1, 0, 2).reshape(4, D_XBC)
    act1 = _swiglu_fwd("swiglu1_fwd", gu1)
    x1 = _mm_nn("ffn1_out", act1, w1_out, res=x, alpha=0.5)
    ga0 = N_MAIN + N_HEADS
    gb0 = ga0 + D_MODEL
    half = D_MODEL // 2
    w_gd = jnp.concatenate(
        [w_in_t[ga0:ga0 + half], w_in_t[gb0:gb0 + half], w_in_t[ga0 + half:gb0], w_in_t[gb0 + half:],
         w_in_t[N_MAIN:N_MAIN + N_HEADS], jnp.zeros((DT_W - N_HEADS, D_MODEL), BF)], axis=0)
    w_mix_perm = w_in_t[0:3 * D_MODEL].reshape(3, 4, CONV_CB, D_MODEL).transpose(1, 0, 2, 3).reshape(3 * D_MODEL, D_MODEL)

    h2 = _rms_fwd("rms2_fwd", x1, rep["mix_norm"])
    p, got = _mm_nt("proj_main", h2, w_in_t, n=N_MAIN, tn=1024, out_dtype=BF, comm=_gather_comm(
        [shard["short_w_out"], shard["ssm_w_out"], shard["w_out"]]))
    p_gd = _mm_nt("proj_gd", h2, w_gd)
    short_w_out = got[0].reshape(D_MODEL, D_MODEL)
    ssm_w_out = got[1].reshape(D_INNER, D_MODEL)
    w_out = got[2].reshape(D_MODEL, D_MODEL)
    ya_in = _mix_a_fwd(p, short_conv_w)
    y_a = _mm_nn("short_out", ya_in, short_w_out)
    xconv, (got,) = _ssm_conv_fwd(p, ssm_conv_w, rep["ssm_conv_b"], comm=_gather_comm([shard["ffn2_w_out"]]))
    w2_out = got.reshape(D_FF, D_MODEL)
    dt, acs = _dt_fwd(p_gd, rep["dt_bias_pad"], rep["a_log_pad"])
    (y_ssm, hsave), (got,) = _ssd_fwd(xconv, dt, acs, rep["d_exp"], comm=_gather_comm([shard["ffn2_w_in"]], [True]))
    w2_in = got.reshape(2 * D_FF, D_MODEL)
    yn = _gnorm_fwd(y_ssm, p, rep["ssm_norm"])
    y_b = _mm_nn("ssm_out", yn, ssm_w_out, tk=1024)
    merged = _merge_fwd(p_gd, y_a, y_b)
    x2 = _mm_nn("mix_out", merged, w_out, res=x1)

    h3 = _rms_fwd("rms3_fwd", x2, rep["ffn2_norm"])
    gu2 = _mm_nt("ffn2_in", h3, w2_in, tn=FF_HALF, out_dtype=BF)
    act2 = _swiglu_fwd("swiglu2_fwd", gu2)
    x3 = _mm_nn("ffn2_out", act2, w2_out, res=x2, alpha=0.5)

    loss, dx3, dx3h, g_final = _final_loss(x3, rep["final_norm"], target)

    small = {"final_norm": g_final}
    dact2 = _mm_nt("ffn2_out_bwd_act", dx3h, w2_out, out_dtype=BF)
    g_w2_out = _mm_tn("ffn2_out_bwd_w", act2, dx3h, gdt, tm=FF_HALF)
    dgu2 = _swiglu_bwd("swiglu2_bwd", gu2, dact2)
    g_w2_in = _mm_tn("ffn2_in_bwd_w", dgu2, h3, gdt, tm=FF_HALF)
    dh3 = _mm_nn("ffn2_in_bwd_h", dgu2, w2_in, tk=FF_HALF)
    dx2, dx2b, small["ffn2_norm"] = _rms_bwd("rms3_bwd", x2, rep["ffn2_norm"], dh3, dx3, 1.0)

    dmerged = _mm_nt("mix_out_bwd_x", dx2b, w_out)
    g_w_out = _mm_tn("mix_out_bwd_w", merged, dx2b, gdt)
    dp_gd, dya, dyb = _merge_bwd(p_gd, y_a, y_b, dmerged)

    dya_in = _mm_nt("short_out_bwd_x", dya, short_w_out)
    g_short_w_out = _mm_tn("short_out_bwd_w", ya_in, dya, gdt)

    dyn = _mm_nt("ssm_out_bwd_x", dyb, ssm_w_out)
    g_ssm_w_out = _mm_tn("ssm_out_bwd_w", yn, dyb, gdt)
    late = [("ffn2_w_out", g_w2_out.reshape(N_DEV, FF_SHARD // 2, D_MODEL)),
            ("ffn2_w_in", g_w2_in.reshape(N_DEV, FF_SHARD, D_MODEL)),
            ("w_out", g_w_out.reshape(N_DEV, -1, D_MODEL)), ("short_w_out", g_short_w_out.reshape(N_DEV, -1, D_MODEL)),
            ("ssm_w_out", g_ssm_w_out.reshape(N_DEV, -1, D_MODEL))]
    (dy_ssm, dp, small["ssm_norm"]), sums = _reduce_start(
        late, lambda comm: _gnorm_bwd(y_ssm, p, rep["ssm_norm"], dyn, comm=comm))
    dp, g_short_conv = _mix_a_bwd(p, short_conv_w, dya_in, dp)
    first = [(n, a) for n, a in sums if n.startswith("ffn2")]
    second = [(n, a) for n, a in sums if not n.startswith("ffn2")]
    (dxconv, ddt, dacs, dd_lane), got = _ssd_bwd(
        xconv, dt, acs, rep["d_exp"], hsave, dy_ssm,
        comm=_chip_comm([a for _, a in first], [n == "ffn2_w_in" for n, _ in first]))
    recv.update({n: a for (n, _), a in zip(first, got)})
    small["ssm_D"] = dd_lane.reshape(N_HEADS, HEAD_DIM).sum(axis=1)[None, :]
    (dp, g_ssm_conv, small["ssm_conv_b"]), got = _ssm_conv_bwd(
        p, ssm_conv_w, rep["ssm_conv_b"], dxconv, dp, comm=_chip_comm([a for _, a in second]))
    recv.update({n: a for (n, _), a in zip(second, got)})
    dp_gd, dbias, dalog = _dt_bwd(p_gd, rep["dt_bias_pad"], rep["a_log_pad"], dt, ddt, dacs, dp_gd)
    small["ssm_dt_bias"] = dbias[:, :N_HEADS]
    small["ssm_A_log"] = dalog[:, :N_HEADS]

    g_main = _mm_tn("proj_main_bwd_w", dp, h2, gdt, tm=1024)
    g_gd = _mm_tn("proj_gd_bwd_w", dp_gd, h2, gdt)
    g_mix = g_main[0:3 * D_MODEL].reshape(4, 3, CONV_CB, D_MODEL).transpose(1, 0, 2, 3).reshape(3 * D_MODEL, D_MODEL)
    g_in_t = jnp.concatenate(
        [g_mix, g_main[3 * D_MODEL:], g_gd[2 * D_MODEL:2 * D_MODEL + N_HEADS],
         g_gd[0:half], g_gd[2 * half:3 * half], g_gd[half:2 * half], g_gd[3 * half:4 * half]], axis=0).reshape(
        N_DEV, IN_SHARD, D_MODEL)
    dh2, w_sums = _reduce_start(
        [("w_in", g_in_t)], lambda comm: _mm_nn("proj_mix_bwd_x", dp, w_mix_perm, tk=1024, kk=3 * D_MODEL, comm=comm))
    w_sum = w_sums[0][1]

    def w_piece(i):
        return _chip_comm([w_sum], rows=[W_GRAD_ROW_CUTS[i]])

    dh2, got0 = _mm_nn("proj_rest_bwd_x", dp, w_in_t, tk=1024, kk=N_MAIN - 3 * D_MODEL, a_off=3, b_off=3, res=dh2,
                       comm=w_piece(0))
    dh2, got1 = _mm_nn("proj_gd_bwd_x", dp_gd, w_gd, res=dh2, comm=w_piece(1))
    (dx1, dx1h, small["mix_norm"]), got2 = _rms_bwd("rms2_bwd", x1, rep["mix_norm"], dh2, dx2, 0.5, comm=w_piece(2))
    g_w1_out, got3 = _mm_tn("ffn1_out_bwd_w", act1, dx1h, gdt, tm=FF_HALF, comm=w_piece(3))
    rest = [("ffn1_w_out", g_w1_out.reshape(N_DEV, FF_SHARD // 2, D_MODEL)),
            ("short_conv_w", g_short_conv.reshape(3, N_DEV, -1).transpose(1, 0, 2)),
            ("ssm_conv_w", g_ssm_conv.reshape(4, N_DEV, -1).transpose(1, 0, 2))]
    dact1, got = _mm_nt("ffn1_out_bwd_act", dx1h, w1_out, out_dtype=BF,
                        comm=_join_comm(w_piece(4), _pair_comm([a for _, a in rest])))
    got4, sib = got[0], got[1:]
    rest_sums = [(n, _add_pairs("pairsum_" + n, a, b)) for (n, a), b in zip(rest, sib)]
    w1_out_sum = rest_sums[0][1]
    half_rows = FF_SHARD // 4
    dgu1, got = _swiglu_bwd("swiglu1_bwd", gu1, dact1, comm=_chip_comm(
        [a for _, a in rest_sums], rows=[(0, half_rows), None, None]))
    recv_w1_out_a = got[0]
    recv.update({n: a for (n, _), a in zip(rest_sums[1:], got[1:])})

    def part(tag, width, off, comm=None):
        out = _mm_tn("ffn1_in_bwd_w_" + tag, dgu1, h1, gdt, tm=FF_HALF, n=width, col_off=off, comm=comm)
        g, couts = (out, None) if comm is None else out
        return g.reshape(N_DEV, FF_SHARD, width), couts

    g_a, (got5,) = part("a", 384, 0, w_piece(5))
    g_b, (got6, sib) = part("b", 384, 1, _join_comm(w_piece(6), _pair_comm([g_a])))
    recv["w_in"] = jnp.concatenate([got0[0], got1[0], got2[0], got3[0], got4, got5, got6], axis=1)
    sum_a = _add_pairs("pairsum_ffn1_w_in_a", g_a, sib)
    g_c, (recv_a, sib_b) = part("c", 256, 3, _join_comm(_chip_comm([sum_a], [True]), _pair_comm([g_b])))
    sum_b = _add_pairs("pairsum_ffn1_w_in_b", g_b, sib_b)
    dh1, (recv_b, recv_w1_out_b, sib_c) = _mm_nn(
        "ffn1_in_bwd_h", dgu1, w1_in, tk=FF_HALF,
        comm=_join_comm(_chip_comm([sum_b, w1_out_sum], [True, False], rows=[None, (half_rows, 2 * half_rows)]),
                        _pair_comm([g_c])))
    recv["ffn1_w_out"] = jnp.concatenate([recv_w1_out_a, recv_w1_out_b], axis=1)
    sum_c = _add_pairs("pairsum_ffn1_w_in_c", g_c, sib_c)
    (dx0, _, small["ffn1_norm"]), (recv_c,) = _rms_bwd("rms1_bwd", x, rep["ffn1_norm"], dh1, dx1, 1.0,
                                                        comm=_chip_comm([sum_c], [True]))
    recv["ffn1_w_in"] = jnp.concatenate([recv_a, recv_b, recv_c], axis=2)
    return dx0, recv, _pack_small(small, loss[:, 0:1])


_SMALL = [("ffn1_norm", 1024), ("mix_norm", 1024), ("ssm_conv_b", 4096), ("ssm_dt_bias", 32), ("ssm_A_log", 32),
          ("ssm_D", 32), ("ssm_norm", 2048), ("ffn2_norm", 1024), ("final_norm", 1024)]
SMALL_W = 10368


def _pack_small(d, loss=None):
    parts = [d[n].reshape(1, -1).astype(F32) for n, _ in _SMALL]
    used = sum(sz for _, sz in _SMALL)
    tail = jnp.zeros((1, SMALL_W - used), F32)
    if loss is not None:
        tail = tail.at[:, 0:1].set(loss)
    return jnp.concatenate(parts + [tail], axis=1)


def _adamw_small(parts, w, m, v):
    n_par = len(_SMALL)
    bc1 = 1.0 - ADAM_B1 ** ADAM_STEP
    bc2 = 1.0 - ADAM_B2 ** ADAM_STEP
    used = sum(sz for _, sz in _SMALL)

    def body(*refs):
        p_ref = refs[0]
        ins = refs[1:1 + 3 * n_par]
        outs = refs[1 + 3 * n_par:]
        g_all = p_ref[0]
        for k in range(1, N_DEV):
            g_all = g_all + p_ref[k]
        off = 0
        for i, (_, sz) in enumerate(_SMALL):
            g = g_all[:, off:off + sz]
            w_ref, m_ref, v_ref = ins[3 * i:3 * i + 3]
            nm = ADAM_B1 * m_ref[...] + (1.0 - ADAM_B1) * g
            nv = ADAM_B2 * v_ref[...] + (1.0 - ADAM_B2) * (g * g)
            outs[4 * i][...] = g
            outs[4 * i + 1][...] = -ADAM_LR * ((nm / bc1) / (jnp.sqrt(nv / bc2) + ADAM_EPS) + ADAM_WD * w_ref[...])
            outs[4 * i + 2][...] = nm
            outs[4 * i + 3][...] = nv
            off += sz
        outs[4 * n_par][...] = g_all[:, used:SMALL_W]

    args = [parts]
    out_shape = []
    for name, sz in _SMALL:
        args += [w[name], m[name], v[name]]
        out_shape += [jax.ShapeDtypeStruct((1, sz), F32)] * 4
    out_shape.append(jax.ShapeDtypeStruct((1, SMALL_W - used), F32))
    res = pl.pallas_call(body, name="adamw_small", out_shape=out_shape,
                         compiler_params=pltpu.CompilerParams(vmem_limit_bytes=VMEM_LIMIT_V7X))(*args)
    return {name: tuple(res[4 * i:4 * i + 4]) for i, (name, _) in enumerate(_SMALL)}, res[-1]


_SHARDED = ["ffn1_w_in", "ffn1_w_out", "w_in", "short_conv_w", "short_w_out", "ssm_conv_w", "ssm_w_out", "w_out",
            "ffn2_w_in", "ffn2_w_out"]
_TRANSPOSED = ("ffn1_w_in", "w_in", "ffn2_w_in")
_ORDER = ["ffn1_norm", "ffn1_w_in", "ffn1_w_out", "mix_norm", "w_in", "short_conv_w", "short_w_out", "ssm_conv_w",
          "ssm_conv_b", "ssm_dt_bias", "ssm_A_log", "ssm_D", "ssm_norm", "ssm_w_out", "w_out", "ffn2_norm",
          "ffn2_w_in", "ffn2_w_out", "final_norm"]


def kernel(x, ffn1_norm, ffn1_w_in, ffn1_w_out, mix_norm, w_in, short_conv_w, short_w_out, ssm_conv_w, ssm_conv_b, ssm_dt_bias, ssm_A_log, ssm_D, ssm_norm, ssm_w_out, w_out, ffn2_norm, ffn2_w_in, ffn2_w_out, final_norm, loss_target, m_ffn1_norm, m_ffn1_w_in, m_ffn1_w_out, m_mix_norm, m_w_in, m_short_conv_w, m_short_w_out, m_ssm_conv_w, m_ssm_conv_b, m_ssm_dt_bias, m_ssm_A_log, m_ssm_D, m_ssm_norm, m_ssm_w_out, m_w_out, m_ffn2_norm, m_ffn2_w_in, m_ffn2_w_out, m_final_norm, v_ffn1_norm, v_ffn1_w_in, v_ffn1_w_out, v_mix_norm, v_w_in, v_short_conv_w, v_short_w_out, v_ssm_conv_w, v_ssm_conv_b, v_ssm_dt_bias, v_ssm_A_log, v_ssm_D, v_ssm_norm, v_ssm_w_out, v_w_out, v_ffn2_norm, v_ffn2_w_in, v_ffn2_w_out, v_final_norm):
    w = dict(ffn1_norm=ffn1_norm, ffn1_w_in=ffn1_w_in, ffn1_w_out=ffn1_w_out, mix_norm=mix_norm, w_in=w_in,
             short_conv_w=short_conv_w, short_w_out=short_w_out, ssm_conv_w=ssm_conv_w, ssm_conv_b=ssm_conv_b,
             ssm_dt_bias=ssm_dt_bias, ssm_A_log=ssm_A_log, ssm_D=ssm_D, ssm_norm=ssm_norm, ssm_w_out=ssm_w_out,
             w_out=w_out, ffn2_norm=ffn2_norm, ffn2_w_in=ffn2_w_in, ffn2_w_out=ffn2_w_out, final_norm=final_norm)
    m = dict(ffn1_norm=m_ffn1_norm, ffn1_w_in=m_ffn1_w_in, ffn1_w_out=m_ffn1_w_out, mix_norm=m_mix_norm, w_in=m_w_in,
             short_conv_w=m_short_conv_w, short_w_out=m_short_w_out, ssm_conv_w=m_ssm_conv_w,
             ssm_conv_b=m_ssm_conv_b, ssm_dt_bias=m_ssm_dt_bias, ssm_A_log=m_ssm_A_log, ssm_D=m_ssm_D,
             ssm_norm=m_ssm_norm, ssm_w_out=m_ssm_w_out, w_out=m_w_out, ffn2_norm=m_ffn2_norm,
             ffn2_w_in=m_ffn2_w_in, ffn2_w_out=m_ffn2_w_out, final_norm=m_final_norm)
    v = dict(ffn1_norm=v_ffn1_norm, ffn1_w_in=v_ffn1_w_in, ffn1_w_out=v_ffn1_w_out, mix_norm=v_mix_norm, w_in=v_w_in,
             short_conv_w=v_short_conv_w, short_w_out=v_short_w_out, ssm_conv_w=v_ssm_conv_w,
             ssm_conv_b=v_ssm_conv_b, ssm_dt_bias=v_ssm_dt_bias, ssm_A_log=v_ssm_A_log, ssm_D=v_ssm_D,
             ssm_norm=v_ssm_norm, ssm_w_out=v_ssm_w_out, w_out=v_w_out, ffn2_norm=v_ffn2_norm,
             ffn2_w_in=v_ffn2_w_in, ffn2_w_out=v_ffn2_w_out, final_norm=v_final_norm)
    shapes = {n: w[n].shape for n in _ORDER}

    def local(d, n):
        return d[n][0].T if n in _TRANSPOSED else d[n][0]

    shard = {n: local(w, n) for n in _SHARDED}

    wire = {n: (shard[n] if n in ("short_conv_w", "ssm_conv_w") else shard[n].astype(BF)) for n in _SHARDED}
    rep = {
        "ffn1_norm": ffn1_norm, "mix_norm": mix_norm, "ffn2_norm": ffn2_norm, "ssm_norm": ssm_norm,
        "ssm_conv_b": ssm_conv_b, "final_norm": final_norm.reshape(1, D_MODEL),
        "dt_bias_pad": _pad_lanes(ssm_dt_bias, DT_W), "a_log_pad": _pad_lanes(ssm_A_log, DT_W),
        "d_exp": jnp.repeat(ssm_D, HEAD_DIM, axis=1),
    }
    grad_x, parts, packed = _train_step(x[0], loss_target[0], wire, rep)

    out_g, out_d, out_m, out_v = {}, {}, {}, {}
    for n in _SHARDED:
        if n == "ssm_w_out":
            res, (small_parts,) = _adamw("adamw_" + n, parts[n], shard[n], local(m, n), local(v, n),
                                         comm=_gather_comm([packed]))
        else:
            res = _adamw("adamw_" + n, parts[n], shard[n], local(m, n), local(v, n))
        out_g[n], out_d[n], out_m[n], out_v[n] = [(r.T if n in _TRANSPOSED else r).reshape(shapes[n]) for r in res]
    row = lambda d: {n: d[n].reshape(1, -1) for n, _ in _SMALL}
    sres, loss_row = _adamw_small(small_parts, row(w), row(m), row(v))
    for n, _ in _SMALL:
        out_g[n], out_d[n], out_m[n], out_v[n] = [r.reshape(shapes[n]) for r in sres[n]]
    loss = loss_row[0, 0]
    return (loss, grad_x[None], *[out_g[n] for n in _ORDER], *[out_d[n] for n in _ORDER],
            *[out_m[n] for n in _ORDER], *[out_v[n] for n in _ORDER])
```

```python
import functools

import jax
import jax.numpy as jnp
from jax import lax
from jax.experimental import pallas as pl
from jax.experimental.pallas import tpu as pltpu

F32 = jnp.float32
BF = jnp.bfloat16

N_DEV = 8
D_MODEL = 1024
D_FF = 2816
D_INNER = 2048
D_XBC = 4096
N_HEADS = 32
HEAD_DIM = 64
N_GROUPS = 8
D_STATE = 128
CHUNK = 64
GROUP_W = D_INNER // N_GROUPS
NORM_EPS = 1e-5
N_IN = 11296
FF_SHARD = 2 * D_FF // N_DEV
FF_HALF = D_FF // 2
IN_SHARD = N_IN // N_DEV

OFF_B, OFF_C, OFF_XA, OFF_Z, OFF_XBC = 0, 1024, 2048, 3072, 5120
N_MAIN = 9216
OFF_DT = 2048
DT_W = 128
N_GD = 2048 + DT_W
W_GRAD_ROW_CUTS = [(0, 400), (400, 568), (568, 704), (704, 880), (880, 1040), (1040, 1240), (1240, 1412)]

ADAM_LR, ADAM_B1, ADAM_B2, ADAM_EPS, ADAM_WD, ADAM_STEP = 0.001, 0.9, 0.999, 1e-08, 0.01, 10

VMEM_LIMIT_V7X = 56 * 1024 * 1024
TM = 1024
TN_MAX_TOKENS = 2048
TE = 512
ADAM_COL_TILE = 256
GATHER_PIECES = 4
GATHER_PIECE_MIN_ROWS = 512


def _params(*sem):
    return pltpu.CompilerParams(dimension_semantics=sem, vmem_limit_bytes=VMEM_LIMIT_V7X)


_DIMS = {
    "nn": (((1,), (0,)), ((), ())),
    "nt": (((1,), (1,)), ((), ())),
    "tn": (((0,), (0,)), ((), ())),
}


def _dot(a, b, mode="nn"):
    return lax.dot_general(a, b, _DIMS[mode], preferred_element_type=F32)


def _sigmoid(x):
    return 1.0 / (1.0 + jnp.exp(-x))


class _Comm:
    def __init__(self, inputs, out_shapes, sems, start, finish):
        self.inputs, self.out_shapes, self.sems, self.start, self.finish = inputs, out_shapes, sems, start, finish


def _pcall(name, body, grid, in_specs, out_specs, out_shape, args, scratch=(), sem=None, comm=None, aliases=None):
    single = not isinstance(out_shape, (list, tuple))
    out_shapes = [out_shape] if single else list(out_shape)
    out_specs = [out_specs] if single else list(out_specs)
    n_in, n_out, n_scr = len(args), len(out_shapes), len(scratch)
    aliases = {} if aliases is None else aliases
    if comm is None:
        res = pl.pallas_call(
            body, name=name, grid=grid, in_specs=list(in_specs), out_specs=out_specs, out_shape=out_shapes,
            scratch_shapes=list(scratch), input_output_aliases=aliases, compiler_params=_params(*sem))(*args)
        return (res[0] if single else res), []
    nci, nco = len(comm.inputs), len(comm.out_shapes)

    def wrapped(*refs):
        a = refs[:n_in]
        ci = refs[n_in:n_in + nci]
        o0 = n_in + nci
        o = refs[o0:o0 + n_out]
        co = refs[o0 + n_out:o0 + n_out + nco]
        s0 = o0 + n_out + nco
        s = refs[s0:s0 + n_scr]
        cs = refs[s0 + n_scr:]
        pids = [pl.program_id(i) for i in range(len(grid))]
        first = functools.reduce(jnp.logical_and, [p == 0 for p in pids])
        last = functools.reduce(jnp.logical_and, [p == g - 1 for p, g in zip(pids, grid)])

        @pl.when(first)
        def _():
            comm.start(ci, co, cs)

        body(*a, *o, *s)

        @pl.when(last)
        def _():
            comm.finish(ci, co, cs)

    any_spec = pl.BlockSpec(memory_space=pl.ANY)
    res = pl.pallas_call(
        wrapped, name=name, grid=grid, in_specs=list(in_specs) + [any_spec] * nci,
        out_specs=out_specs + [any_spec] * nco, out_shape=out_shapes + list(comm.out_shapes),
        scratch_shapes=list(scratch) + list(comm.sems), input_output_aliases=aliases,
        compiler_params=_params(*(("arbitrary",) * len(grid))))(*args, *comm.inputs)
    core = res[:n_out]
    return (core[0] if single else core), list(res[n_out:])


def _comm_call(name, comm):
    nci, nco = len(comm.inputs), len(comm.out_shapes)

    def body(*refs):
        ci, co, cs = refs[:nci], refs[nci:nci + nco], refs[nci + nco:]
        comm.start(ci, co, cs)
        comm.finish(ci, co, cs)

    any_spec = pl.BlockSpec(memory_space=pl.ANY)
    return pl.pallas_call(
        body, name=name, in_specs=[any_spec] * nci, out_specs=[any_spec] * nco, out_shape=list(comm.out_shapes),
        scratch_shapes=list(comm.sems), compiler_params=pltpu.CompilerParams(has_side_effects=True))(*comm.inputs)


def _remote(src, dst, ssem, rsem, dev):
    return pltpu.make_async_remote_copy(src_ref=src, dst_ref=dst, send_sem=ssem, recv_sem=rsem, device_id=dev,
                                        device_id_type=pl.DeviceIdType.MESH)


def _place():
    x, y, c = lax.axis_index("x"), lax.axis_index("y"), lax.axis_index("c")
    other_chips = [(1 - x, y), (x, 1 - y), (1 - x, 1 - y)]
    return x, y, c, other_chips


def _slot(x, y, c, swap):
    return 4 * y + 2 * x + c if swap else 4 * x + 2 * y + c


def _chip_slot(x, y, swap):
    return 2 * y + x if swap else 2 * x + y


def _gather_comm(shards, swaps=None):
    n = len(shards)
    per = N_DEV - 1
    swaps = [False] * n if swaps is None else swaps
    pieces = []
    for i, a in enumerate(shards):
        rows = a.shape[0]
        k = GATHER_PIECES if (a.ndim == 2 and rows >= GATHER_PIECE_MIN_ROWS) else 1
        step = -(-rows // (k * 8)) * 8
        if k == 1:
            pieces.append((i, 0, None))
        else:
            pieces += [(i, r, min(step, rows - r)) for r in range(0, rows, step)]
    m = len(pieces)

    def src(ins, v):
        i, r, cnt = pieces[v]
        return ins[i] if cnt is None else ins[i].at[pl.ds(r, cnt)]

    def place(outs, v, x, y, c):
        i, r, cnt = pieces[v]
        blk = outs[i].at[_slot(x, y, c, swaps[i])]
        return blk if cnt is None else blk.at[pl.ds(r, cnt)]

    def start(ins, outs, sems):
        send, recv, loc = sems
        x, y, c, chips = _place()
        for v in range(m):
            me = place(outs, v, x, y, c)
            pltpu.make_async_copy(src(ins, v), me, loc.at[v]).start()
            _remote(src(ins, v), me, send.at[per * v], recv.at[per * v], (x, y, 1 - c)).start()
        for j, (qx, qy) in enumerate(chips):
            for v in range(m):
                _remote(src(ins, v), place(outs, v, x, y, c), send.at[per * v + 1 + j], recv.at[per * v + 1 + j],
                        (qx, qy, c)).start()

    def finish(ins, outs, sems):
        send, recv, loc = sems
        x, y, c, chips = _place()
        sib = (x, y, 1 - c)
        for v in range(m):
            for j, (qx, qy) in enumerate(chips):
                blk = place(outs, v, qx, qy, c)
                _remote(blk, blk, send.at[per * v + 1 + j], recv.at[per * v + 1 + j], (qx, qy, c)).wait_recv()
                _remote(blk, blk, send.at[per * v + 4 + j], recv.at[per * v + 4 + j], sib).start()
        for v in range(m):
            blk = place(outs, v, x, y, 1 - c)
            _remote(blk, blk, send.at[per * v], recv.at[per * v], sib).wait_recv()
            for j, (qx, qy) in enumerate(chips):
                blk = place(outs, v, qx, qy, 1 - c)
                _remote(blk, blk, send.at[per * v + 4 + j], recv.at[per * v + 4 + j], sib).wait_recv()
        for v in range(m):
            own = place(outs, v, x, y, c)
            for k in range(per):
                _remote(src(ins, v), own, send.at[per * v + k], recv.at[per * v + k], sib).wait_send()
            pltpu.make_async_copy(src(ins, v), own, loc.at[v]).wait()

    out_shapes = [jax.ShapeDtypeStruct((N_DEV,) + tuple(a.shape), a.dtype) for a in shards]
    sems = [pltpu.SemaphoreType.DMA((per * m,)), pltpu.SemaphoreType.DMA((per * m,)), pltpu.SemaphoreType.DMA((m,))]
    return _Comm(list(shards), out_shapes, sems, start, finish)


def _pair_comm(slots):
    n = len(slots)

    def copies(ins, outs, sems):
        send, recv = sems
        x, y, c, _ = _place()
        sib = (x, y, 1 - c)
        out = []
        for i in range(n):
            for q in range(4):
                out.append(_remote(ins[i].at[2 * q + 1 - c], outs[i].at[q], send.at[4 * i + q], recv.at[4 * i + q], sib))
        return out

    def start(ins, outs, sems):
        for cp in copies(ins, outs, sems):
            cp.start()

    def finish(ins, outs, sems):
        for cp in copies(ins, outs, sems):
            cp.wait_send()
            cp.wait_recv()

    out_shapes = [jax.ShapeDtypeStruct((4,) + tuple(a.shape[1:]), a.dtype) for a in slots]
    sems = [pltpu.SemaphoreType.DMA((4 * n,)), pltpu.SemaphoreType.DMA((4 * n,))]
    return _Comm(list(slots), out_shapes, sems, start, finish)


def _chip_comm(chip_sums, swaps=None, rows=None):
    n = len(chip_sums)
    swaps = [False] * n if swaps is None else swaps
    rows = [None] * n if rows is None else rows

    def src(ins, i, q):
        return ins[i].at[q] if rows[i] is None else ins[i].at[q, pl.ds(rows[i][0], rows[i][1] - rows[i][0])]

    def start(ins, outs, sems):
        send, recv, loc = sems
        x, y, c, chips = _place()
        for i in range(n):
            mine = _chip_slot(x, y, swaps[i])
            pltpu.make_async_copy(src(ins, i, mine), outs[i].at[mine], loc.at[i]).start()
            for j, (qx, qy) in enumerate(chips):
                _remote(src(ins, i, _chip_slot(qx, qy, swaps[i])), outs[i].at[mine], send.at[3 * i + j],
                        recv.at[3 * i + j], (qx, qy, c)).start()

    def finish(ins, outs, sems):
        send, recv, loc = sems
        x, y, c, chips = _place()
        for i in range(n):
            mine = _chip_slot(x, y, swaps[i])
            for j, (qx, qy) in enumerate(chips):
                theirs = _chip_slot(qx, qy, swaps[i])
                cp = _remote(src(ins, i, theirs), outs[i].at[theirs], send.at[3 * i + j], recv.at[3 * i + j], (qx, qy, c))
                cp.wait_send()
                cp.wait_recv()
            pltpu.make_async_copy(src(ins, i, mine), outs[i].at[mine], loc.at[i]).wait()

    def out_shape(a, r):
        shape = a.shape if r is None else (a.shape[0], r[1] - r[0]) + tuple(a.shape[2:])
        return jax.ShapeDtypeStruct(shape, a.dtype)

    out_shapes = [out_shape(a, r) for a, r in zip(chip_sums, rows)]
    sems = [pltpu.SemaphoreType.DMA((3 * n,)), pltpu.SemaphoreType.DMA((3 * n,)), pltpu.SemaphoreType.DMA((n,))]
    return _Comm(list(chip_sums), out_shapes, sems, start, finish)


def _join_comm(a, b):
    na_i, na_o, na_s = len(a.inputs), len(a.out_shapes), len(a.sems)

    def start(ins, outs, sems):
        a.start(ins[:na_i], outs[:na_o], sems[:na_s])
        b.start(ins[na_i:], outs[na_o:], sems[na_s:])

    def finish(ins, outs, sems):
        a.finish(ins[:na_i], outs[:na_o], sems[:na_s])
        b.finish(ins[na_i:], outs[na_o:], sems[na_s:])

    return _Comm(a.inputs + b.inputs, a.out_shapes + b.out_shapes, a.sems + b.sems, start, finish)


def _row_tile(r):
    for cand in (256, 128):
        if r > cand and r % cand == 0:
            return cand
    return r


def _add_pairs(name, slots, sib):
    r, c = slots.shape[1:]
    tr = _row_tile(r)

    def body(core_ref, s_ref, b_ref, o_ref):
        o_ref[...] = (s_ref[...].astype(F32) + b_ref[...].astype(F32)).astype(o_ref.dtype)

    core = jnp.full((1,), lax.axis_index("c"), jnp.int32)
    return pl.pallas_call(
        body, name=name,
        grid_spec=pltpu.PrefetchScalarGridSpec(
            num_scalar_prefetch=1, grid=(4, r // tr),
            in_specs=[pl.BlockSpec((None, tr, c), lambda q, i, core_ref: (2 * q + core_ref[0], i, 0)),
                      pl.BlockSpec((None, tr, c), lambda q, i, core_ref: (q, i, 0))],
            out_specs=pl.BlockSpec((None, tr, c), lambda q, i, core_ref: (q, i, 0))),
        out_shape=jax.ShapeDtypeStruct((4, r, c), slots.dtype),
        compiler_params=_params("parallel", "parallel"))(core, slots, sib)


def _matmul(name, mode, a, b, grid, a_spec, b_spec, o_spec, out_shape, acc_shape,
            res=None, res_spec=None, alpha=1.0, comm=None):
    nk = grid[-1]
    has_res = res is not None

    def body(*refs):
        if has_res:
            a_ref, b_ref, r_ref, o_ref = refs[:4]
        else:
            a_ref, b_ref, o_ref = refs[:3]
            r_ref = None
        part = _dot(a_ref[...], b_ref[...], mode)

        def finish(v):
            if alpha != 1.0:
                v = v * alpha
            if has_res:
                v = r_ref[...] + v
            o_ref[...] = v.astype(o_ref.dtype)

        if nk == 1:
            finish(part)
        else:
            acc = refs[-1]
            k = pl.program_id(len(grid) - 1)

            @pl.when(k == 0)
            def _():
                acc[...] = part

            @pl.when(k > 0)
            def _():
                acc[...] += part

            @pl.when(k == nk - 1)
            def _():
                finish(acc[...])

    in_specs = [a_spec, b_spec] + ([res_spec] if has_res else [])
    args = (a, b) + ((res,) if has_res else ())
    scratch = [] if nk == 1 else [pltpu.VMEM(acc_shape, F32)]
    sem = ("parallel",) * (len(grid) - 1) + ("arbitrary",)
    out, couts = _pcall(name, body, grid, in_specs, o_spec, out_shape, args, scratch, sem, comm)
    return out if comm is None else (out, couts)


def _mm_nn(name, a, b, out_dtype=F32, res=None, alpha=1.0, tk=None, kk=None, a_off=0, b_off=0, comm=None):
    t = a.shape[0]
    kk = a.shape[1] if kk is None else kk
    n = b.shape[1]
    tk = kk if tk is None else tk
    grid = (t // TM, 1, kk // tk)
    return _matmul(
        name, "nn", a, b, grid,
        pl.BlockSpec((TM, tk), lambda i, j, k: (i, k + a_off)),
        pl.BlockSpec((tk, n), lambda i, j, k: (k + b_off, 0)),
        pl.BlockSpec((TM, n), lambda i, j, k: (i, 0)),
        jax.ShapeDtypeStruct((t, n), out_dtype), (TM, n),
        res=res, res_spec=pl.BlockSpec((TM, n), lambda i, j, k: (i, 0)), alpha=alpha, comm=comm)


def _mm_nt(name, a, b, n=None, tn=None, tk=None, out_dtype=F32, comm=None):
    t, kk = a.shape
    n = b.shape[0] if n is None else n
    tn = n if tn is None else tn
    tk = kk if tk is None else tk
    grid = (n // tn, t // TM, kk // tk)
    return _matmul(
        name, "nt", a, b, grid,
        pl.BlockSpec((TM, tk), lambda j, i, k: (i, k)),
        pl.BlockSpec((tn, tk), lambda j, i, k: (j, k)),
        pl.BlockSpec((TM, tn), lambda j, i, k: (i, j)),
        jax.ShapeDtypeStruct((t, n), out_dtype), (TM, tn), comm=comm)


def _mm_tn(name, a, b, out_dtype, tm=None, n=None, col_off=0, comm=None):
    t, m = a.shape
    n = b.shape[1] if n is None else n
    tm = m if tm is None else tm
    tk = t if t <= TN_MAX_TOKENS else TM
    grid = (m // tm, 1, t // tk)
    return _matmul(
        name, "tn", a, b, grid,
        pl.BlockSpec((tk, tm), lambda j, i, k: (k, j)),
        pl.BlockSpec((tk, n), lambda j, i, k: (k, col_off)),
        pl.BlockSpec((tm, n), lambda j, i, k: (j, 0)),
        jax.ShapeDtypeStruct((m, n), out_dtype), (tm, n), comm=comm)


def _rms_fwd(name, x, w):
    t, d = x.shape

    def body(x_ref, w_ref, h_ref):
        xv = x_ref[...]
        rstd = lax.rsqrt(jnp.mean(xv * xv, axis=-1, keepdims=True) + NORM_EPS)
        h_ref[...] = (xv * rstd * w_ref[...]).astype(h_ref.dtype)

    return pl.pallas_call(
        body, name=name, grid=(t // TE,),
        in_specs=[pl.BlockSpec((TE, d), lambda i: (i, 0)), pl.BlockSpec((1, d), lambda i: (0, 0))],
        out_specs=pl.BlockSpec((TE, d), lambda i: (i, 0)),
        out_shape=jax.ShapeDtypeStruct((t, d), BF), compiler_params=_params("parallel"))(x, w)


def _rms_bwd(name, x, w, dh, dres, out_scale, comm=None):
    t, d = x.shape

    def body(x_ref, w_ref, dh_ref, dres_ref, dx_ref, dxb_ref, dw_ref):
        i = pl.program_id(0)
        xv = x_ref[...]
        rstd = lax.rsqrt(jnp.mean(xv * xv, axis=-1, keepdims=True) + NORM_EPS)
        xhat = xv * rstd
        dhv = dh_ref[...]
        wd = dhv * w_ref[...]
        proj = jnp.mean(wd * xhat, axis=-1, keepdims=True)
        dx = dres_ref[...] + rstd * (wd - xhat * proj)
        dx_ref[...] = dx
        dxb_ref[...] = (dx * out_scale).astype(BF)
        part = jnp.sum(dhv * xhat, axis=0, keepdims=True)

        @pl.when(i == 0)
        def _():
            dw_ref[...] = part

        @pl.when(i > 0)
        def _():
            dw_ref[...] += part

    row = pl.BlockSpec((TE, d), lambda i: (i, 0))
    vec = pl.BlockSpec((1, d), lambda i: (0, 0))
    outs, couts = _pcall(
        name, body, (t // TE,), [row, vec, row, row], [row, row, vec],
        [jax.ShapeDtypeStruct((t, d), F32), jax.ShapeDtypeStruct((t, d), BF), jax.ShapeDtypeStruct((1, d), F32)],
        (x, w, dh, dres), (), ("arbitrary",), comm)
    return outs if comm is None else (outs, couts)


def _final_loss(x, w, target):
    t, d = x.shape

    def body(x_ref, w_ref, t_ref, loss_ref, dx_ref, dxb_ref, dw_ref):
        i = pl.program_id(0)
        xv = x_ref[...]
        rstd = lax.rsqrt(jnp.mean(xv * xv, axis=-1, keepdims=True) + NORM_EPS)
        xhat = xv * rstd
        err = xhat * w_ref[...] - t_ref[...]
        lpart = 0.5 * jnp.sum(jnp.mean(err * err, axis=-1, keepdims=True), axis=0, keepdims=True)
        dy = err * (1.0 / d)
        wd = dy * w_ref[...]
        proj = jnp.mean(wd * xhat, axis=-1, keepdims=True)
        dx = rstd * (wd - xhat * proj)
        dx_ref[...] = dx
        dxb_ref[...] = (0.5 * dx).astype(BF)
        part = jnp.sum(dy * xhat, axis=0, keepdims=True)
        lfull = jnp.broadcast_to(lpart, (1, 128))

        @pl.when(i == 0)
        def _():
            dw_ref[...] = part
            loss_ref[...] = lfull

        @pl.when(i > 0)
        def _():
            dw_ref[...] += part
            loss_ref[...] += lfull

    row = pl.BlockSpec((TE, d), lambda i: (i, 0))
    vec = pl.BlockSpec((1, d), lambda i: (0, 0))
    return pl.pallas_call(
        body, name="final_loss", grid=(t // TE,), in_specs=[row, vec, row],
        out_specs=[pl.BlockSpec((1, 128), lambda i: (0, 0)), row, row, vec],
        out_shape=[jax.ShapeDtypeStruct((1, 128), F32), jax.ShapeDtypeStruct((t, d), F32),
                   jax.ShapeDtypeStruct((t, d), BF), jax.ShapeDtypeStruct((1, d), F32)],
        compiler_params=_params("arbitrary"))(x, w, target)


def _swiglu_fwd(name, gu, comm=None):
    t = gu.shape[0]

    def body(g_ref, u_ref, a_ref):
        g = g_ref[...].astype(F32)
        a_ref[...] = (g * _sigmoid(g) * u_ref[...].astype(F32)).astype(BF)

    blk = (TE, FF_HALF)
    out, couts = _pcall(
        name, body, (t // TE, 2),
        [pl.BlockSpec(blk, lambda i, j: (i, 2 * j)), pl.BlockSpec(blk, lambda i, j: (i, 2 * j + 1))],
        pl.BlockSpec(blk, lambda i, j: (i, j)), jax.ShapeDtypeStruct((t, D_FF), BF),
        (gu, gu), (), ("parallel", "parallel"), comm)
    return out if comm is None else (out, couts)


def _swiglu_bwd(name, gu, dact, comm=None):
    t = gu.shape[0]

    def body(g_ref, u_ref, da_ref, o_ref):
        g = g_ref[...].astype(F32)
        da = da_ref[...].astype(F32)
        s = _sigmoid(g)
        o_ref[:, 0:FF_HALF] = (da * u_ref[...].astype(F32) * (s * (1.0 + g * (1.0 - s)))).astype(BF)
        o_ref[:, FF_HALF:2 * FF_HALF] = (da * g * s).astype(BF)

    blk = (TE, FF_HALF)
    out, couts = _pcall(
        name, body, (t // TE, 2),
        [pl.BlockSpec(blk, lambda i, j: (i, 2 * j)), pl.BlockSpec(blk, lambda i, j: (i, 2 * j + 1)),
         pl.BlockSpec(blk, lambda i, j: (i, j))],
        pl.BlockSpec((TE, 2 * FF_HALF), lambda i, j: (i, j)),
        jax.ShapeDtypeStruct((t, 2 * D_FF), BF), (gu, gu, dact), (), ("parallel", "parallel"), comm)
    return out if comm is None else (out, couts)


CONV_CB = 256


CONV_ROWS = 128
CONV_HALO = 16


def _taps_down(ext, w, k):
    shifted = [pltpu.roll(ext, k - 1 - j, 0)[CONV_HALO:] for j in range(k - 1)] + [ext[CONV_HALO:]]
    out = shifted[k - 1] * w[k - 1:k, :]
    for j in range(k - 1):
        out = out + shifted[j] * w[j:j + 1, :]
    return out, shifted


def _taps_up(ext, w, k):
    rows = ext.shape[0]
    n = rows - CONV_HALO
    out = ext[:n] * w[k - 1:k, :]
    for j in range(k - 1):
        out = out + pltpu.roll(ext, rows - (k - 1 - j), 0)[:n] * w[j:j + 1, :]
    return out


def _rows_before(ref, i, r0):
    start = pl.multiple_of(jnp.maximum(r0 - CONV_HALO, 0), CONV_HALO)
    return jnp.where(i > 0, ref[pl.ds(start, CONV_HALO), :].astype(F32), 0.0)


def _rows_after(ref, r0, t):
    start = pl.multiple_of(jnp.minimum(r0 + CONV_ROWS, t - CONV_HALO), CONV_HALO)
    return ref[pl.ds(start, CONV_HALO), :].astype(F32)


def _fold8(v):
    return v.reshape(v.shape[0] // 8, 8, v.shape[1]).sum(axis=0)


def _silu_grad(pre):
    s = _sigmoid(pre)
    return s * (1.0 + pre * (1.0 - s))


def _pspec(t, off):
    base = off // CONV_CB
    return pl.BlockSpec((t, CONV_CB), lambda j: (0, base + j))


def _mix_a_fwd(p, conv_w):
    t = p.shape[0]

    def body(b_ref, c_ref, xa_ref, w_ref, o_ref):
        w = w_ref[...]

        def step(i, carry):
            r0 = pl.multiple_of(i * CONV_ROWS, CONV_ROWS)
            rows = pl.ds(r0, CONV_ROWS)
            q = c_ref[rows, :].astype(F32) * xa_ref[rows, :].astype(F32)
            q_before = _rows_before(c_ref, i, r0) * _rows_before(xa_ref, i, r0)
            va, _ = _taps_down(jnp.concatenate([q_before, q], axis=0), w, 3)
            o_ref[rows, :] = (b_ref[rows, :].astype(F32) * va).astype(BF)
            return carry

        lax.fori_loop(0, t // CONV_ROWS, step, 0)

    return pl.pallas_call(
        body, name="mix_a_fwd", grid=(D_MODEL // CONV_CB,),
        in_specs=[_pspec(t, OFF_B), _pspec(t, OFF_C), _pspec(t, OFF_XA),
                  pl.BlockSpec((3, CONV_CB), lambda j: (0, j))],
        out_specs=pl.BlockSpec((t, CONV_CB), lambda j: (0, j)),
        out_shape=jax.ShapeDtypeStruct((t, D_MODEL), BF), compiler_params=_params("parallel"))(p, p, p, conv_w)


def _mix_a_bwd(p, conv_w, dya, dp):
    t = p.shape[0]

    def body(b_ref, c_ref, xa_ref, w_ref, dy_ref, dp_in, dp_ref, dw_ref):
        del dp_in
        w = w_ref[...]
        n = t // CONV_ROWS

        def step(i, acc):
            r0 = pl.multiple_of(i * CONV_ROWS, CONV_ROWS)
            rows = pl.ds(r0, CONV_ROWS)
            cv = c_ref[rows, :].astype(F32)
            xav = xa_ref[rows, :].astype(F32)
            q_before = _rows_before(c_ref, i, r0) * _rows_before(xa_ref, i, r0)
            va, shifted = _taps_down(jnp.concatenate([q_before, cv * xav], axis=0), w, 3)
            dyv = dy_ref[rows, :]
            dp_ref[rows, 0:CONV_CB] = (dyv * va).astype(BF)
            dv = dyv * b_ref[rows, :].astype(F32)
            dv_after = jnp.where(i < n - 1, _rows_after(dy_ref, r0, t) * _rows_after(b_ref, r0, t), 0.0)
            dq = _taps_up(jnp.concatenate([dv, dv_after], axis=0), w, 3)
            dp_ref[rows, CONV_CB:2 * CONV_CB] = (dq * xav).astype(BF)
            dp_ref[rows, 2 * CONV_CB:3 * CONV_CB] = (dq * cv).astype(BF)
            return tuple(a + _fold8(dv * s) for a, s in zip(acc, shifted))

        zero = jnp.zeros((8, CONV_CB), F32)
        acc = lax.fori_loop(0, n, step, (zero, zero, zero))
        for j in range(3):
            dw_ref[j:j + 1, :] = jnp.sum(acc[j], axis=0, keepdims=True)

    col = pl.BlockSpec((t, CONV_CB), lambda j: (0, j))
    wsp = pl.BlockSpec((3, CONV_CB), lambda j: (0, j))
    return pl.pallas_call(
        body, name="mix_a_bwd", grid=(D_MODEL // CONV_CB,),
        in_specs=[_pspec(t, OFF_B), _pspec(t, OFF_C), _pspec(t, OFF_XA), wsp, col, pl.BlockSpec(memory_space=pl.ANY)],
        out_specs=[pl.BlockSpec((t, 3 * CONV_CB), lambda j: (0, j)), wsp],
        out_shape=[jax.ShapeDtypeStruct(dp.shape, dp.dtype), jax.ShapeDtypeStruct((3, D_MODEL), F32)],
        input_output_aliases={5: 0},
        compiler_params=_params("parallel"))(p, p, p, conv_w, dya, dp)


def _ssm_conv_fwd(p, conv_w, conv_b, comm=None):
    t = p.shape[0]

    def body(x_ref, w_ref, b_ref, o_ref):
        w = w_ref[...]
        bias = b_ref[...]

        def step(i, carry):
            r0 = pl.multiple_of(i * CONV_ROWS, CONV_ROWS)
            rows = pl.ds(r0, CONV_ROWS)
            ext = jnp.concatenate([_rows_before(x_ref, i, r0), x_ref[rows, :].astype(F32)], axis=0)
            pre = _taps_down(ext, w, 4)[0] + bias
            o_ref[rows, :] = pre * _sigmoid(pre)
            return carry

        lax.fori_loop(0, t // CONV_ROWS, step, 0)

    out, couts = _pcall(
        "ssm_conv_fwd", body, (D_XBC // CONV_CB,),
        [_pspec(t, OFF_XBC), pl.BlockSpec((4, CONV_CB), lambda j: (0, j)), pl.BlockSpec((1, CONV_CB), lambda j: (0, j))],
        pl.BlockSpec((t, CONV_CB), lambda j: (0, j)), jax.ShapeDtypeStruct((t, D_XBC), F32),
        (p, conv_w, conv_b), (), ("parallel",), comm)
    return out if comm is None else (out, couts)


def _ssm_conv_bwd(p, conv_w, conv_b, dxc, dp, comm=None):
    t = p.shape[0]

    def body(x_ref, w_ref, b_ref, d_ref, dp_in, dx_ref, dw_ref, db_ref):
        del dp_in
        w = w_ref[...]
        bias = b_ref[...]
        n = t // CONV_ROWS

        def step(i, acc):
            r0 = pl.multiple_of(i * CONV_ROWS, CONV_ROWS)
            rows = pl.ds(r0, CONV_ROWS)
            x_cur = x_ref[rows, :].astype(F32)
            pre, shifted = _taps_down(jnp.concatenate([_rows_before(x_ref, i, r0), x_cur], axis=0), w, 4)
            pre = pre + bias
            dpre = d_ref[rows, :] * _silu_grad(pre)
            ext_after = jnp.concatenate([x_cur[CONV_ROWS - CONV_HALO:], _rows_after(x_ref, r0, t)], axis=0)
            pre_after = _taps_down(ext_after, w, 4)[0] + bias
            dpre_after = jnp.where(i < n - 1, _rows_after(d_ref, r0, t) * _silu_grad(pre_after), 0.0)
            dx_ref[rows, :] = _taps_up(jnp.concatenate([dpre, dpre_after], axis=0), w, 4).astype(BF)
            new = tuple(a + _fold8(dpre * s) for a, s in zip(acc[:4], shifted))
            return new + (acc[4] + _fold8(dpre),)

        zero = jnp.zeros((8, CONV_CB), F32)
        acc = lax.fori_loop(0, n, step, (zero,) * 5)
        for j in range(4):
            dw_ref[j:j + 1, :] = jnp.sum(acc[j], axis=0, keepdims=True)
        db_ref[...] = jnp.sum(acc[4], axis=0, keepdims=True)

    col = pl.BlockSpec((t, CONV_CB), lambda j: (0, j))
    wsp = pl.BlockSpec((4, CONV_CB), lambda j: (0, j))
    bsp = pl.BlockSpec((1, CONV_CB), lambda j: (0, j))
    outs, couts = _pcall(
        "ssm_conv_bwd", body, (D_XBC // CONV_CB,),
        [_pspec(t, OFF_XBC), wsp, bsp, col, pl.BlockSpec(memory_space=pl.ANY)], [_pspec(t, OFF_XBC), wsp, bsp],
        [jax.ShapeDtypeStruct(dp.shape, dp.dtype), jax.ShapeDtypeStruct((4, D_XBC), F32),
         jax.ShapeDtypeStruct((1, D_XBC), F32)],
        (p, conv_w, conv_b, dxc, dp), (), ("parallel",), comm, aliases={4: 0})
    return outs if comm is None else (outs, couts)


DT_ROWS = 512


def _tri(lower):
    r = lax.broadcasted_iota(jnp.int32, (CHUNK, CHUNK), 0)
    c = lax.broadcasted_iota(jnp.int32, (CHUNK, CHUNK), 1)
    return jnp.where((r >= c) if lower else (r <= c), 1.0, 0.0).astype(F32)


def _dot_exact(a, b):
    return lax.dot_general(a, b, _DIMS["nn"], preferred_element_type=F32, precision=lax.Precision.HIGHEST)


def _dt_fwd(p, bias_pad, alog_pad):
    t = p.shape[0]

    def body(raw_ref, b_ref, al_ref, dt_ref, acs_ref):
        z = raw_ref[...] + b_ref[...]
        dt = jnp.maximum(z, 0.0) + jnp.log(1.0 + jnp.exp(-jnp.abs(z)))
        dt_ref[...] = dt
        a = dt * (-jnp.exp(al_ref[...]))
        tri = _tri(True)
        for k in range(DT_ROWS // CHUNK):
            acs_ref[k * CHUNK:(k + 1) * CHUNK, :] = _dot_exact(tri, a[k * CHUNK:(k + 1) * CHUNK, :])

    blk = pl.BlockSpec((DT_ROWS, DT_W), lambda i: (i, 0))
    vec = pl.BlockSpec((1, DT_W), lambda i: (0, 0))
    return pl.pallas_call(
        body, name="dt_fwd", grid=(t // DT_ROWS,),
        in_specs=[pl.BlockSpec((DT_ROWS, DT_W), lambda i: (i, OFF_DT // DT_W)), vec, vec],
        out_specs=[blk, blk], out_shape=[jax.ShapeDtypeStruct((t, DT_W), F32)] * 2,
        compiler_params=_params("parallel"))(p, bias_pad, alog_pad)


def _dt_bwd(p, bias_pad, alog_pad, dt, ddt, dacs, dp_gd):
    t = p.shape[0]

    def body(raw_ref, b_ref, al_ref, dt_ref, ddt_ref, dacs_ref, dp_in, draw_ref, db_ref, dal_ref):
        del dp_in
        i = pl.program_id(0)
        acoef = -jnp.exp(al_ref[...])
        triu = _tri(False)
        das = []
        for k in range(DT_ROWS // CHUNK):
            das.append(_dot_exact(triu, dacs_ref[k * CHUNK:(k + 1) * CHUNK, :]))
        da = jnp.concatenate(das, axis=0)
        dtv = dt_ref[...]
        ddt_tot = ddt_ref[...] + da * acoef
        lane = lax.broadcasted_iota(jnp.int32, (DT_ROWS, DT_W), 1)
        draw = jnp.where(lane < N_HEADS, ddt_tot * _sigmoid(raw_ref[...] + b_ref[...]), 0.0)
        draw_ref[...] = draw.astype(BF)
        pb = jnp.sum(draw, axis=0, keepdims=True)
        pa = jnp.sum(da * dtv * acoef, axis=0, keepdims=True)

        @pl.when(i == 0)
        def _():
            db_ref[...] = pb
            dal_ref[...] = pa

        @pl.when(i > 0)
        def _():
            db_ref[...] += pb
            dal_ref[...] += pa

    blk = pl.BlockSpec((DT_ROWS, DT_W), lambda i: (i, 0))
    vec = pl.BlockSpec((1, DT_W), lambda i: (0, 0))
    return pl.pallas_call(
        body, name="dt_bwd", grid=(t // DT_ROWS,),
        in_specs=[pl.BlockSpec((DT_ROWS, DT_W), lambda i: (i, OFF_DT // DT_W)), vec, vec, blk, blk, blk,
                  pl.BlockSpec(memory_space=pl.ANY)],
        out_specs=[pl.BlockSpec((DT_ROWS, DT_W), lambda i: (i, OFF_DT // DT_W)), vec, vec],
        out_shape=[jax.ShapeDtypeStruct(dp_gd.shape, dp_gd.dtype), jax.ShapeDtypeStruct((1, DT_W), F32),
                   jax.ShapeDtypeStruct((1, DT_W), F32)],
        input_output_aliases={6: 0},
        compiler_params=_params("arbitrary"))(p, bias_pad, alog_pad, dt, ddt, dacs, dp_gd)


def _split_dot(z, onehot, terms):
    out = None
    rest = z
    for _ in range(terms):
        piece = rest.astype(BF)
        part = _dot(piece, onehot)
        out = part if out is None else out + part
        rest = rest - piece.astype(F32)
    return out


def _spread_mat():
    row = lax.broadcasted_iota(jnp.int32, (DT_W, D_INNER), 0)
    lane = lax.broadcasted_iota(jnp.int32, (DT_W, D_INNER), 1)
    return jnp.where(row == lane // HEAD_DIM, 1.0, 0.0).astype(BF)


def _gather_mat():
    row = lax.broadcasted_iota(jnp.int32, (D_INNER, DT_W), 0)
    lane = lax.broadcasted_iota(jnp.int32, (D_INNER, DT_W), 1)
    return jnp.where(lane == row // HEAD_DIM, 1.0, 0.0).astype(BF)


def _ssd_masks():
    row = lax.broadcasted_iota(jnp.int32, (CHUNK, GROUP_W), 0)
    col = lax.broadcasted_iota(jnp.int32, (CHUNK, GROUP_W), 1) % HEAD_DIM
    brow = lax.broadcasted_iota(jnp.int32, (GROUP_W, GROUP_W), 0) // HEAD_DIM
    bcol = lax.broadcasted_iota(jnp.int32, (GROUP_W, GROUP_W), 1) // HEAD_DIM
    return row >= col, row == col, brow == bcol


def _stack4(v):
    return jnp.concatenate([v, v, v, v], axis=0)


def _fold4(v):
    return v[0:CHUNK] + v[CHUNK:2 * CHUNK] + v[2 * CHUNK:3 * CHUNK] + v[3 * CHUNK:4 * CHUNK]


def _ssd_group(xc_ref, wide_ref, g, tri, eye, blockdiag):
    gs = slice(GROUP_W * g, GROUP_W * (g + 1))
    xs_g = xc_ref[:, gs]
    b_g = xc_ref[:, D_INNER + D_STATE * g:D_INNER + D_STATE * (g + 1)].astype(BF)
    c_g = xc_ref[:, D_INNER + 1024 + D_STATE * g:D_INNER + 1024 + D_STATE * (g + 1)].astype(BF)
    acs_e, dt_e = wide_ref[0:CHUNK, gs], wide_ref[CHUNK:2 * CHUNK, gs]
    atot_e = acs_e[CHUNK - 1:CHUNK, :]
    acs_j = jnp.sum(jnp.where(eye, acs_e, 0.0), axis=0, keepdims=True)
    lmat = jnp.where(tri, jnp.exp(jnp.minimum(acs_e - acs_j, 0.0)), 0.0)
    b_t = _stack4(b_g)
    m = _dot(c_g, b_t, "nt") * lmat
    x_g = xs_g * dt_e
    xbd = jnp.where(blockdiag, _stack4(x_g), 0.0).astype(BF)
    return dict(gs=gs, xs=xs_g, b=b_g, c=c_g, b_t=b_t, dt=dt_e, e=jnp.exp(acs_e), dec=jnp.exp(atot_e - acs_e),
                eat=jnp.exp(atot_e), lmat=lmat, m=m, x=x_g, xbd=xbd)


def _ssd_fwd(xconv, dt, acs, d_exp, comm=None):
    t = xconv.shape[0]
    nc = t // CHUNK

    def body(xc_ref, dt_ref, acs_ref, d_ref, y_ref, hs_ref, state, wide):
        c = pl.program_id(0)

        @pl.when(c == 0)
        def _():
            state[...] = jnp.zeros_like(state)

        hs_ref[...] = state[...]
        tri, eye, blockdiag = _ssd_masks()
        wide[...] = _split_dot(jnp.concatenate([acs_ref[...], dt_ref[...]], axis=0), _spread_mat(), 3)
        for g in range(N_GROUPS):
            q = _ssd_group(xc_ref, wide, g, tri, eye, blockdiag)
            gs = q["gs"]
            h_t = state[:, gs]
            ydiag = _dot(q["m"].astype(BF), q["xbd"])
            yoff = _dot(q["c"], h_t.astype(BF)) * q["e"]
            y_ref[:, gs] = ydiag + yoff + d_ref[:, gs] * q["xs"]
            s_t = _dot(q["b"], (q["x"] * q["dec"]).astype(BF), "tn")
            state[:, gs] = q["eat"] * h_t + s_t

    blk = lambda w: pl.BlockSpec((CHUNK, w), lambda c: (c, 0))
    outs, couts = _pcall(
        "ssd_fwd", body, (nc,),
        [blk(D_XBC), blk(DT_W), blk(DT_W), pl.BlockSpec((1, D_INNER), lambda c: (0, 0))],
        [blk(D_INNER), pl.BlockSpec((None, D_STATE, D_INNER), lambda c: (c, 0, 0))],
        [jax.ShapeDtypeStruct((t, D_INNER), F32), jax.ShapeDtypeStruct((nc, D_STATE, D_INNER), F32)],
        (xconv, dt, acs, d_exp), [pltpu.VMEM((D_STATE, D_INNER), F32), pltpu.VMEM((2 * CHUNK, D_INNER), F32)],
        ("arbitrary",), comm)
    return outs if comm is None else (outs, couts)


def _ssd_bwd(xconv, dt, acs, d_exp, hsave, dy, comm=None):
    t = xconv.shape[0]
    nc = t // CHUNK

    def body(xc_ref, dt_ref, acs_ref, d_ref, hs_ref, dy_ref, dxc_ref, ddt_ref, dacs_ref, dd_ref, dstate, wide, per_head):
        c = pl.program_id(0)

        @pl.when(c == 0)
        def _():
            dstate[...] = jnp.zeros_like(dstate)
            dd_ref[...] = jnp.zeros_like(dd_ref)

        tri, eye, blockdiag = _ssd_masks()
        acsv = acs_ref[...]
        wide[...] = _split_dot(jnp.concatenate([acsv, dt_ref[...]], axis=0), _spread_mat(), 3)
        eat_heads = jnp.exp(acsv[CHUNK - 1:CHUNK, :])

        for g in range(N_GROUPS):
            q = _ssd_group(xc_ref, wide, g, tri, eye, blockdiag)
            gs, xs_g, b_g, c_g, m = q["gs"], q["xs"], q["b"], q["c"], q["m"]
            bs = slice(D_INNER + D_STATE * g, D_INNER + D_STATE * (g + 1))
            cs = slice(D_INNER + 1024 + D_STATE * g, D_INNER + 1024 + D_STATE * (g + 1))
            h_t = hs_ref[:, gs]
            h_b = h_t.astype(BF)
            dy_g = dy_ref[:, gs]
            dy_b = dy_g.astype(BF)
            ds_t = dstate[:, gs]
            ds_b = ds_t.astype(BF)

            yoff = _dot(c_g, h_b) * q["e"]
            edy = (q["e"] * dy_g).astype(BF)
            d_c = _dot(edy, h_b, "nt")
            d_ht = _dot(c_g, edy, "tn")
            bds = _dot(b_g, ds_b)
            xd = q["x"] * q["dec"]
            d_b = _dot(xd.astype(BF), ds_b, "nt")
            dm = _dot(dy_b, q["xbd"], "nt")
            cross = _dot(m.astype(BF), dy_b, "tn")
            dx_full = q["dec"] * bds + _fold4(jnp.where(blockdiag, cross, 0.0))
            dml = (dm * q["lmat"]).astype(BF)
            d_c = d_c + _dot(dml, q["b_t"])
            d_b = d_b + _fold4(_dot(dml, c_g, "tn"))
            w = dm * m
            q_dec = xd * bds
            z = w - jnp.where(eye, jnp.sum(w, axis=0, keepdims=True), 0.0) + dy_g * yoff - q_dec
            rows = jnp.concatenate(
                [jnp.sum(q_dec, axis=0, keepdims=True), jnp.sum(ds_t * h_t, axis=0, keepdims=True),
                 jnp.zeros((6, GROUP_W), F32)], axis=0)
            per_head[:, gs] = jnp.concatenate([z, dx_full * xs_g, rows], axis=0)
            dxc_ref[:, cs] = d_c
            dxc_ref[:, bs] = d_b
            dxc_ref[:, gs] = dx_full * q["dt"] + d_ref[:, gs] * dy_g
            dd_ref[:, gs] += jnp.sum(dy_g * xs_g, axis=0, keepdims=True)
            dstate[:, gs] = q["eat"] * ds_t + d_ht

        seg = _split_dot(per_head[...], _gather_mat(), 2)
        datot = seg[2 * CHUNK:2 * CHUNK + 1] + eat_heads * seg[2 * CHUNK + 1:2 * CHUNK + 2]
        rowi = lax.broadcasted_iota(jnp.int32, (CHUNK, DT_W), 0)
        ddt_ref[...] = seg[CHUNK:2 * CHUNK]
        dacs_ref[...] = seg[0:CHUNK] + jnp.where(rowi == CHUNK - 1, datot, 0.0)

    rev = lambda w: pl.BlockSpec((CHUNK, w), lambda c: (nc - 1 - c, 0))
    vec = pl.BlockSpec((1, D_INNER), lambda c: (0, 0))
    outs, couts = _pcall(
        "ssd_bwd", body, (nc,),
        [rev(D_XBC), rev(DT_W), rev(DT_W), vec,
         pl.BlockSpec((None, D_STATE, D_INNER), lambda c: (nc - 1 - c, 0, 0)), rev(D_INNER)],
        [rev(D_XBC), rev(DT_W), rev(DT_W), vec],
        [jax.ShapeDtypeStruct((t, D_XBC), F32), jax.ShapeDtypeStruct((t, DT_W), F32),
         jax.ShapeDtypeStruct((t, DT_W), F32), jax.ShapeDtypeStruct((1, D_INNER), F32)],
        (xconv, dt, acs, d_exp, hsave, dy),
        [pltpu.VMEM((D_STATE, D_INNER), F32), pltpu.VMEM((2 * CHUNK, D_INNER), F32),
         pltpu.VMEM((2 * CHUNK + 8, D_INNER), F32)], ("arbitrary",), comm)
    return outs if comm is None else (outs, couts)


GN_CB = 1024
GN_GROUPS = GN_CB // GROUP_W


def _gnorm_fwd(y, p, w, comm=None):
    t = y.shape[0]
    zoff = OFF_Z // GN_CB

    def body(y_ref, z_ref, w_ref, o_ref):
        for g in range(GN_GROUPS):
            gs = slice(GROUP_W * g, GROUP_W * (g + 1))
            z = z_ref[:, gs].astype(F32)
            yf = y_ref[:, gs] * (z * _sigmoid(z))
            rstd = lax.rsqrt(jnp.mean(yf * yf, axis=-1, keepdims=True) + NORM_EPS)
            o_ref[:, gs] = (yf * rstd * w_ref[:, gs]).astype(BF)

    blk = pl.BlockSpec((TE, GN_CB), lambda i, j: (i, j))
    out, couts = _pcall(
        "gnorm_fwd", body, (t // TE, D_INNER // GN_CB),
        [blk, pl.BlockSpec((TE, GN_CB), lambda i, j: (i, zoff + j)), pl.BlockSpec((1, GN_CB), lambda i, j: (0, j))],
        blk, jax.ShapeDtypeStruct((t, D_INNER), BF), (y, p, w), (), ("parallel", "parallel"), comm)
    return out if comm is None else (out, couts)


def _gnorm_bwd(y, p, w, dyn, comm=None):
    t = y.shape[0]
    zoff = OFF_Z // GN_CB

    def body(y_ref, z_ref, w_ref, dn_ref, dy_ref, dz_ref, dw_ref):
        i = pl.program_id(1)
        for g in range(GN_GROUPS):
            gs = slice(GROUP_W * g, GROUP_W * (g + 1))
            z = z_ref[:, gs].astype(F32)
            yv = y_ref[:, gs]
            s = _sigmoid(z)
            sil = z * s
            yf = yv * sil
            rstd = lax.rsqrt(jnp.mean(yf * yf, axis=-1, keepdims=True) + NORM_EPS)
            xhat = yf * rstd
            dn = dn_ref[:, gs]
            wd = dn * w_ref[:, gs]
            proj = jnp.mean(wd * xhat, axis=-1, keepdims=True)
            dyf = rstd * (wd - xhat * proj)
            dy_ref[:, gs] = dyf * sil
            dz_ref[:, gs] = (dyf * yv * (s * (1.0 + z * (1.0 - s)))).astype(BF)
            part = jnp.sum(dn * xhat, axis=0, keepdims=True)

            @pl.when(i == 0)
            def _():
                dw_ref[:, gs] = part

            @pl.when(i > 0)
            def _():
                dw_ref[:, gs] += part

    blk = pl.BlockSpec((TE, GN_CB), lambda j, i: (i, j))
    vec = pl.BlockSpec((1, GN_CB), lambda j, i: (0, j))
    outs, couts = _pcall(
        "gnorm_bwd", body, (D_INNER // GN_CB, t // TE),
        [blk, pl.BlockSpec((TE, GN_CB), lambda j, i: (i, zoff + j)), vec, blk],
        [blk, pl.BlockSpec((TE, GN_CB), lambda j, i: (i, zoff + j)), vec],
        [jax.ShapeDtypeStruct((t, D_INNER), F32), jax.ShapeDtypeStruct((t, N_MAIN), BF),
         jax.ShapeDtypeStruct((1, D_INNER), F32)],
        (y, p, w, dyn), (), ("parallel", "arbitrary"), comm)
    return outs if comm is None else (outs, couts)


MERGE_CB = 512


def _merge_fwd(p, ya, yb):
    t = ya.shape[0]

    def body(ga_ref, gb_ref, ya_ref, yb_ref, o_ref):
        o_ref[...] = (_sigmoid(ga_ref[...]) * ya_ref[...] + _sigmoid(gb_ref[...]) * yb_ref[...]).astype(BF)

    blk = pl.BlockSpec((TE, MERGE_CB), lambda i, j: (i, j))
    return pl.pallas_call(
        body, name="merge_fwd", grid=(t // TE, D_MODEL // MERGE_CB),
        in_specs=[pl.BlockSpec((TE, MERGE_CB), lambda i, j: (i, 2 * j)),
                  pl.BlockSpec((TE, MERGE_CB), lambda i, j: (i, 2 * j + 1)), blk, blk],
        out_specs=blk, out_shape=jax.ShapeDtypeStruct((t, D_MODEL), BF),
        compiler_params=_params("parallel", "parallel"))(p, p, ya, yb)


def _merge_bwd(p, ya, yb, dm):
    t = ya.shape[0]

    def body(ga_ref, gb_ref, ya_ref, yb_ref, dm_ref, dg_ref, dya_ref, dyb_ref):
        d = dm_ref[...]
        sa = _sigmoid(ga_ref[...])
        sb = _sigmoid(gb_ref[...])
        dg_ref[:, 0:MERGE_CB] = (d * ya_ref[...] * sa * (1.0 - sa)).astype(BF)
        dg_ref[:, MERGE_CB:2 * MERGE_CB] = (d * yb_ref[...] * sb * (1.0 - sb)).astype(BF)
        dya_ref[...] = (d * sa).astype(BF)
        dyb_ref[...] = (d * sb).astype(BF)

    blk = pl.BlockSpec((TE, MERGE_CB), lambda i, j: (i, j))
    return pl.pallas_call(
        body, name="merge_bwd", grid=(t // TE, D_MODEL // MERGE_CB),
        in_specs=[pl.BlockSpec((TE, MERGE_CB), lambda i, j: (i, 2 * j)),
                  pl.BlockSpec((TE, MERGE_CB), lambda i, j: (i, 2 * j + 1)), blk, blk, blk],
        out_specs=[pl.BlockSpec((TE, 2 * MERGE_CB), lambda i, j: (i, j)), blk, blk],
        out_shape=[jax.ShapeDtypeStruct((t, N_GD), BF)] + [jax.ShapeDtypeStruct((t, D_MODEL), BF)] * 2,
        compiler_params=_params("parallel", "parallel"))(p, p, ya, yb, dm)


def _adamw(name, parts, w, m, v, comm=None):
    r, c = w.shape
    tr = _row_tile(r)
    tc = ADAM_COL_TILE if (tr == r and r > 512 and c % ADAM_COL_TILE == 0) else c
    n_parts = parts.shape[0]
    bc1 = 1.0 - ADAM_B1 ** ADAM_STEP
    bc2 = 1.0 - ADAM_B2 ** ADAM_STEP

    def body(p_ref, w_ref, m_ref, v_ref, g_ref, d_ref, nm_ref, nv_ref):
        g = p_ref[0].astype(F32)
        for k in range(1, n_parts):
            g = g + p_ref[k].astype(F32)
        nm = ADAM_B1 * m_ref[...] + (1.0 - ADAM_B1) * g
        nv = ADAM_B2 * v_ref[...] + (1.0 - ADAM_B2) * (g * g)
        g_ref[...] = g
        nm_ref[...] = nm
        nv_ref[...] = nv
        d_ref[...] = -ADAM_LR * ((nm / bc1) / (jnp.sqrt(nv / bc2) + ADAM_EPS) + ADAM_WD * w_ref[...])

    blk = pl.BlockSpec((tr, tc), lambda i, j: (i, j))
    outs, couts = _pcall(
        name, body, (r // tr, c // tc),
        [pl.BlockSpec((n_parts, tr, tc), lambda i, j: (0, i, j)), blk, blk, blk], [blk] * 4,
        [jax.ShapeDtypeStruct((r, c), F32)] * 4, (parts, w, m, v), (), ("parallel", "parallel"), comm)
    return outs if comm is None else (outs, couts)


def _pad_lanes(v, width):
    return jnp.pad(v, ((0, 0), (0, width - v.shape[1])))


def _reduce_start(slots, host):
    outs, sib = host(_pair_comm([a for _, a in slots]))
    sums = [(n, _add_pairs("pairsum_" + n, a, b)) for (n, a), b in zip(slots, sib)]
    return outs, sums


def _train_step(x, target, shard, rep):
    gdt = BF
    recv = {}
    (got,) = _comm_call("gather_ffn1_in", _gather_comm([shard["ffn1_w_in"]], [True]))
    w1_in = got.reshape(2 * D_FF, D_MODEL)
    h1 = _rms_fwd("rms1_fwd", x, rep["ffn1_norm"])
    gu1, got = _mm_nt("ffn1_in", h1, w1_in, tn=FF_HALF, out_dtype=BF, comm=_gather_comm(
        [shard["ffn1_w_out"], shard["w_in"], shard["short_conv_w"], shard["ssm_conv_w"]]))
    w1_out = got[0].reshape(D_FF, D_MODEL)
    w_in_t = got[1].reshape(N_IN, D_MODEL)
    short_conv_w = got[2].transpose(1, 0, 2).reshape(3, D_MODEL)
    ssm_conv_w = got[3].transpose(1, 0, 2).reshape(4, D_XBC)
    act1 = _swiglu_fwd("swiglu1_fwd", gu1)
    x1 = _mm_nn("ffn1_out", act1, w1_out, res=x, alpha=0.5)
    ga0 = N_MAIN + N_HEADS
    gb0 = ga0 + D_MODEL
    half = D_MODEL // 2
    w_gd = jnp.concatenate(
        [w_in_t[ga0:ga0 + half], w_in_t[gb0:gb0 + half], w_in_t[ga0 + half:gb0], w_in_t[gb0 + half:],
         w_in_t[N_MAIN:N_MAIN + N_HEADS], jnp.zeros((DT_W - N_HEADS, D_MODEL), BF)], axis=0)
    w_mix_perm = w_in_t[0:3 * D_MODEL].reshape(3, 4, CONV_CB, D_MODEL).transpose(1, 0, 2, 3).reshape(3 * D_MODEL, D_MODEL)

    h2 = _rms_fwd("rms2_fwd", x1, rep["mix_norm"])
    p, got = _mm_nt("proj_main", h2, w_in_t, n=N_MAIN, tn=1024, out_dtype=BF, comm=_gather_comm(
        [shard["short_w_out"], shard["ssm_w_out"], shard["w_out"]]))
    p_gd = _mm_nt("proj_gd", h2, w_gd)
    short_w_out = got[0].reshape(D_MODEL, D_MODEL)
    ssm_w_out = got[1].reshape(D_INNER, D_MODEL)
    w_out = got[2].reshape(D_MODEL, D_MODEL)
    ya_in = _mix_a_fwd(p, short_conv_w)
    y_a = _mm_nn("short_out", ya_in, short_w_out)
    xconv, (got,) = _ssm_conv_fwd(p, ssm_conv_w, rep["ssm_conv_b"], comm=_gather_comm([shard["ffn2_w_out"]]))
    w2_out = got.reshape(D_FF, D_MODEL)
    dt, acs = _dt_fwd(p_gd, rep["dt_bias_pad"], rep["a_log_pad"])
    (y_ssm, hsave), (got,) = _ssd_fwd(xconv, dt, acs, rep["d_exp"], comm=_gather_comm([shard["ffn2_w_in"]], [True]))
    w2_in = got.reshape(2 * D_FF, D_MODEL)
    yn = _gnorm_fwd(y_ssm, p, rep["ssm_norm"])
    y_b = _mm_nn("ssm_out", yn, ssm_w_out, tk=1024)
    merged = _merge_fwd(p_gd, y_a, y_b)
    x2 = _mm_nn("mix_out", merged, w_out, res=x1)

    h3 = _rms_fwd("rms3_fwd", x2, rep["ffn2_norm"])
    gu2 = _mm_nt("ffn2_in", h3, w2_in, tn=FF_HALF, out_dtype=BF)
    act2 = _swiglu_fwd("swiglu2_fwd", gu2)
    x3 = _mm_nn("ffn2_out", act2, w2_out, res=x2, alpha=0.5)

    loss, dx3, dx3h, g_final = _final_loss(x3, rep["final_norm"], target)

    small = {"final_norm": g_final}
    dact2 = _mm_nt("ffn2_out_bwd_act", dx3h, w2_out, out_dtype=BF)
    g_w2_out = _mm_tn("ffn2_out_bwd_w", act2, dx3h, gdt, tm=FF_HALF)
    dgu2 = _swiglu_bwd("swiglu2_bwd", gu2, dact2)
    g_w2_in = _mm_tn("ffn2_in_bwd_w", dgu2, h3, gdt, tm=FF_HALF)
    dh3 = _mm_nn("ffn2_in_bwd_h", dgu2, w2_in, tk=FF_HALF)
    dx2, dx2b, small["ffn2_norm"] = _rms_bwd("rms3_bwd", x2, rep["ffn2_norm"], dh3, dx3, 1.0)

    dmerged = _mm_nt("mix_out_bwd_x", dx2b, w_out)
    g_w_out = _mm_tn("mix_out_bwd_w", merged, dx2b, gdt)
    dp_gd, dya, dyb = _merge_bwd(p_gd, y_a, y_b, dmerged)

    dya_in = _mm_nt("short_out_bwd_x", dya, short_w_out)
    g_short_w_out = _mm_tn("short_out_bwd_w", ya_in, dya, gdt)

    dyn = _mm_nt("ssm_out_bwd_x", dyb, ssm_w_out)
    g_ssm_w_out = _mm_tn("ssm_out_bwd_w", yn, dyb, gdt)
    late = [("ffn2_w_out", g_w2_out.reshape(N_DEV, FF_SHARD // 2, D_MODEL)),
            ("ffn2_w_in", g_w2_in.reshape(N_DEV, FF_SHARD, D_MODEL)),
            ("w_out", g_w_out.reshape(N_DEV, -1, D_MODEL)), ("short_w_out", g_short_w_out.reshape(N_DEV, -1, D_MODEL)),
            ("ssm_w_out", g_ssm_w_out.reshape(N_DEV, -1, D_MODEL))]
    (dy_ssm, dp, small["ssm_norm"]), sums = _reduce_start(
        late, lambda comm: _gnorm_bwd(y_ssm, p, rep["ssm_norm"], dyn, comm=comm))
    dp, g_short_conv = _mix_a_bwd(p, short_conv_w, dya_in, dp)
    first = [(n, a) for n, a in sums if n.startswith("ffn2")]
    second = [(n, a) for n, a in sums if not n.startswith("ffn2")]
    (dxconv, ddt, dacs, dd_lane), got = _ssd_bwd(
        xconv, dt, acs, rep["d_exp"], hsave, dy_ssm,
        comm=_chip_comm([a for _, a in first], [n == "ffn2_w_in" for n, _ in first]))
    recv.update({n: a for (n, _), a in zip(first, got)})
    small["ssm_D"] = dd_lane.reshape(N_HEADS, HEAD_DIM).sum(axis=1)[None, :]
    (dp, g_ssm_conv, small["ssm_conv_b"]), got = _ssm_conv_bwd(
        p, ssm_conv_w, rep["ssm_conv_b"], dxconv, dp, comm=_chip_comm([a for _, a in second]))
    recv.update({n: a for (n, _), a in zip(second, got)})
    dp_gd, dbias, dalog = _dt_bwd(p_gd, rep["dt_bias_pad"], rep["a_log_pad"], dt, ddt, dacs, dp_gd)
    small["ssm_dt_bias"] = dbias[:, :N_HEADS]
    small["ssm_A_log"] = dalog[:, :N_HEADS]

    g_main = _mm_tn("proj_main_bwd_w", dp, h2, gdt, tm=1024)
    g_gd = _mm_tn("proj_gd_bwd_w", dp_gd, h2, gdt)
    g_mix = g_main[0:3 * D_MODEL].reshape(4, 3, CONV_CB, D_MODEL).transpose(1, 0, 2, 3).reshape(3 * D_MODEL, D_MODEL)
    g_in_t = jnp.concatenate(
        [g_mix, g_main[3 * D_MODEL:], g_gd[2 * D_MODEL:2 * D_MODEL + N_HEADS],
         g_gd[0:half], g_gd[2 * half:3 * half], g_gd[half:2 * half], g_gd[3 * half:4 * half]], axis=0).reshape(
        N_DEV, IN_SHARD, D_MODEL)
    dh2, w_sums = _reduce_start(
        [("w_in", g_in_t)], lambda comm: _mm_nn("proj_mix_bwd_x", dp, w_mix_perm, tk=1024, kk=3 * D_MODEL, comm=comm))
    w_sum = w_sums[0][1]

    def w_piece(i):
        return _chip_comm([w_sum], rows=[W_GRAD_ROW_CUTS[i]])

    dh2, got0 = _mm_nn("proj_rest_bwd_x", dp, w_in_t, tk=1024, kk=N_MAIN - 3 * D_MODEL, a_off=3, b_off=3, res=dh2,
                       comm=w_piece(0))
    dh2, got1 = _mm_nn("proj_gd_bwd_x", dp_gd, w_gd, res=dh2, comm=w_piece(1))
    (dx1, dx1h, small["mix_norm"]), got2 = _rms_bwd("rms2_bwd", x1, rep["mix_norm"], dh2, dx2, 0.5, comm=w_piece(2))
    g_w1_out, got3 = _mm_tn("ffn1_out_bwd_w", act1, dx1h, gdt, tm=FF_HALF, comm=w_piece(3))
    rest = [("ffn1_w_out", g_w1_out.reshape(N_DEV, FF_SHARD // 2, D_MODEL)),
            ("short_conv_w", g_short_conv.reshape(3, N_DEV, -1).transpose(1, 0, 2)),
            ("ssm_conv_w", g_ssm_conv.reshape(4, N_DEV, -1).transpose(1, 0, 2))]
    dact1, got = _mm_nt("ffn1_out_bwd_act", dx1h, w1_out, out_dtype=BF,
                        comm=_join_comm(w_piece(4), _pair_comm([a for _, a in rest])))
    got4, sib = got[0], got[1:]
    rest_sums = [(n, _add_pairs("pairsum_" + n, a, b)) for (n, a), b in zip(rest, sib)]
    w1_out_sum = rest_sums[0][1]
    half_rows = FF_SHARD // 4
    dgu1, got = _swiglu_bwd("swiglu1_bwd", gu1, dact1, comm=_chip_comm(
        [a for _, a in rest_sums], rows=[(0, half_rows), None, None]))
    recv_w1_out_a = got[0]
    recv.update({n: a for (n, _), a in zip(rest_sums[1:], got[1:])})

    def part(tag, width, off, comm=None):
        out = _mm_tn("ffn1_in_bwd_w_" + tag, dgu1, h1, gdt, tm=FF_HALF, n=width, col_off=off, comm=comm)
        g, couts = (out, None) if comm is None else out
        return g.reshape(N_DEV, FF_SHARD, width), couts

    g_a, (got5,) = part("a", 384, 0, w_piece(5))
    g_b, (got6, sib) = part("b", 384, 1, _join_comm(w_piece(6), _pair_comm([g_a])))
    recv["w_in"] = jnp.concatenate([got0[0], got1[0], got2[0], got3[0], got4, got5, got6], axis=1)
    sum_a = _add_pairs("pairsum_ffn1_w_in_a", g_a, sib)
    g_c, (recv_a, sib_b) = part("c", 256, 3, _join_comm(_chip_comm([sum_a], [True]), _pair_comm([g_b])))
    sum_b = _add_pairs("pairsum_ffn1_w_in_b", g_b, sib_b)
    dh1, (recv_b, recv_w1_out_b, sib_c) = _mm_nn(
        "ffn1_in_bwd_h", dgu1, w1_in, tk=FF_HALF,
        comm=_join_comm(_chip_comm([sum_b, w1_out_sum], [True, False], rows=[None, (half_rows, 2 * half_rows)]),
                        _pair_comm([g_c])))
    recv["ffn1_w_out"] = jnp.concatenate([recv_w1_out_a, recv_w1_out_b], axis=1)
    sum_c = _add_pairs("pairsum_ffn1_w_in_c", g_c, sib_c)
    (dx0, _, small["ffn1_norm"]), (recv_c,) = _rms_bwd("rms1_bwd", x, rep["ffn1_norm"], dh1, dx1, 1.0,
                                                        comm=_chip_comm([sum_c], [True]))
    recv["ffn1_w_in"] = jnp.concatenate([recv_a, recv_b, recv_c], axis=2)
    return dx0, recv, _pack_small(small, loss[:, 0:1])


_SMALL = [("ffn1_norm", 1024), ("mix_norm", 1024), ("ssm_conv_b", 4096), ("ssm_dt_bias", 32), ("ssm_A_log", 32),
          ("ssm_D", 32), ("ssm_norm", 2048), ("ffn2_norm", 1024), ("final_norm", 1024)]
SMALL_W = 10368


def _pack_small(d, loss=None):
    parts = [d[n].reshape(1, -1).astype(F32) for n, _ in _SMALL]
    used = sum(sz for _, sz in _SMALL)
    tail = jnp.zeros((1, SMALL_W - used), F32)
    if loss is not None:
        tail = tail.at[:, 0:1].set(loss)
    return jnp.concatenate(parts + [tail], axis=1)


def _adamw_small(parts, w, m, v):
    n_par = len(_SMALL)
    bc1 = 1.0 - ADAM_B1 ** ADAM_STEP
    bc2 = 1.0 - ADAM_B2 ** ADAM_STEP
    used = sum(sz for _, sz in _SMALL)

    def body(*refs):
        p_ref = refs[0]
        ins = refs[1:1 + 3 * n_par]
        outs = refs[1 + 3 * n_par:]
        g_all = p_ref[0]
        for k in range(1, N_DEV):
            g_all = g_all + p_ref[k]
        off = 0
        for i, (_, sz) in enumerate(_SMALL):
            g = g_all[:, off:off + sz]
            w_ref, m_ref, v_ref = ins[3 * i:3 * i + 3]
            nm = ADAM_B1 * m_ref[...] + (1.0 - ADAM_B1) * g
            nv = ADAM_B2 * v_ref[...] + (1.0 - ADAM_B2) * (g * g)
            outs[4 * i][...] = g
            outs[4 * i + 1][...] = -ADAM_LR * ((nm / bc1) / (jnp.sqrt(nv / bc2) + ADAM_EPS) + ADAM_WD * w_ref[...])
            outs[4 * i + 2][...] = nm
            outs[4 * i + 3][...] = nv
            off += sz
        outs[4 * n_par][...] = g_all[:, used:SMALL_W]

    args = [parts]
    out_shape = []
    for name, sz in _SMALL:
        args += [w[name], m[name], v[name]]
        out_shape += [jax.ShapeDtypeStruct((1, sz), F32)] * 4
    out_shape.append(jax.ShapeDtypeStruct((1, SMALL_W - used), F32))
    res = pl.pallas_call(body, name="adamw_small", out_shape=out_shape,
                         compiler_params=pltpu.CompilerParams(vmem_limit_bytes=VMEM_LIMIT_V7X))(*args)
    return {name: tuple(res[4 * i:4 * i + 4]) for i, (name, _) in enumerate(_SMALL)}, res[-1]


_SHARDED = ["ffn1_w_in", "ffn1_w_out", "w_in", "short_conv_w", "short_w_out", "ssm_conv_w", "ssm_w_out", "w_out",
            "ffn2_w_in", "ffn2_w_out"]
_TRANSPOSED = ("ffn1_w_in", "w_in", "ffn2_w_in")
_ORDER = ["ffn1_norm", "ffn1_w_in", "ffn1_w_out", "mix_norm", "w_in", "short_conv_w", "short_w_out", "ssm_conv_w",
          "ssm_conv_b", "ssm_dt_bias", "ssm_A_log", "ssm_D", "ssm_norm", "ssm_w_out", "w_out", "ffn2_norm",
          "ffn2_w_in", "ffn2_w_out", "final_norm"]


def kernel(x, ffn1_norm, ffn1_w_in, ffn1_w_out, mix_norm, w_in, short_conv_w, short_w_out, ssm_conv_w, ssm_conv_b, ssm_dt_bias, ssm_A_log, ssm_D, ssm_norm, ssm_w_out, w_out, ffn2_norm, ffn2_w_in, ffn2_w_out, final_norm, loss_target, m_ffn1_norm, m_ffn1_w_in, m_ffn1_w_out, m_mix_norm, m_w_in, m_short_conv_w, m_short_w_out, m_ssm_conv_w, m_ssm_conv_b, m_ssm_dt_bias, m_ssm_A_log, m_ssm_D, m_ssm_norm, m_ssm_w_out, m_w_out, m_ffn2_norm, m_ffn2_w_in, m_ffn2_w_out, m_final_norm, v_ffn1_norm, v_ffn1_w_in, v_ffn1_w_out, v_mix_norm, v_w_in, v_short_conv_w, v_short_w_out, v_ssm_conv_w, v_ssm_conv_b, v_ssm_dt_bias, v_ssm_A_log, v_ssm_D, v_ssm_norm, v_ssm_w_out, v_w_out, v_ffn2_norm, v_ffn2_w_in, v_ffn2_w_out, v_final_norm):
    w = dict(ffn1_norm=ffn1_norm, ffn1_w_in=ffn1_w_in, ffn1_w_out=ffn1_w_out, mix_norm=mix_norm, w_in=w_in,
             short_conv_w=short_conv_w, short_w_out=short_w_out, ssm_conv_w=ssm_conv_w, ssm_conv_b=ssm_conv_b,
             ssm_dt_bias=ssm_dt_bias, ssm_A_log=ssm_A_log, ssm_D=ssm_D, ssm_norm=ssm_norm, ssm_w_out=ssm_w_out,
             w_out=w_out, ffn2_norm=ffn2_norm, ffn2_w_in=ffn2_w_in, ffn2_w_out=ffn2_w_out, final_norm=final_norm)
    m = dict(ffn1_norm=m_ffn1_norm, ffn1_w_in=m_ffn1_w_in, ffn1_w_out=m_ffn1_w_out, mix_norm=m_mix_norm, w_in=m_w_in,
             short_conv_w=m_short_conv_w, short_w_out=m_short_w_out, ssm_conv_w=m_ssm_conv_w,
             ssm_conv_b=m_ssm_conv_b, ssm_dt_bias=m_ssm_dt_bias, ssm_A_log=m_ssm_A_log, ssm_D=m_ssm_D,
             ssm_norm=m_ssm_norm, ssm_w_out=m_ssm_w_out, w_out=m_w_out, ffn2_norm=m_ffn2_norm,
             ffn2_w_in=m_ffn2_w_in, ffn2_w_out=m_ffn2_w_out, final_norm=m_final_norm)
    v = dict(ffn1_norm=v_ffn1_norm, ffn1_w_in=v_ffn1_w_in, ffn1_w_out=v_ffn1_w_out, mix_norm=v_mix_norm, w_in=v_w_in,
             short_conv_w=v_short_conv_w, short_w_out=v_short_w_out, ssm_conv_w=v_ssm_conv_w,
             ssm_conv_b=v_ssm_conv_b, ssm_dt_bias=v_ssm_dt_bias, ssm_A_log=v_ssm_A_log, ssm_D=v_ssm_D,
             ssm_norm=v_ssm_norm, ssm_w_out=v_ssm_w_out, w_out=v_w_out, ffn2_norm=v_ffn2_norm,
             ffn2_w_in=v_ffn2_w_in, ffn2_w_out=v_ffn2_w_out, final_norm=v_final_norm)
    shapes = {n: w[n].shape for n in _ORDER}

    def local(d, n):
        return d[n][0].T if n in _TRANSPOSED else d[n][0]

    shard = {n: local(w, n) for n in _SHARDED}

    wire = {n: (shard[n] if n in ("short_conv_w", "ssm_conv_w") else shard[n].astype(BF)) for n in _SHARDED}
    rep = {
        "ffn1_norm": ffn1_norm, "mix_norm": mix_norm, "ffn2_norm": ffn2_norm, "ssm_norm": ssm_norm,
        "ssm_conv_b": ssm_conv_b, "final_norm": final_norm.reshape(1, D_MODEL),
        "dt_bias_pad": _pad_lanes(ssm_dt_bias, DT_W), "a_log_pad": _pad_lanes(ssm_A_log, DT_W),
        "d_exp": jnp.repeat(ssm_D, HEAD_DIM, axis=1),
    }
    grad_x, parts, packed = _train_step(x[0], loss_target[0], wire, rep)

    out_g, out_d, out_m, out_v = {}, {}, {}, {}
    for n in _SHARDED:
        if n == "ssm_w_out":
            res, (small_parts,) = _adamw("adamw_" + n, parts[n], shard[n], local(m, n), local(v, n),
                                         comm=_gather_comm([packed]))
        else:
            res = _adamw("adamw_" + n, parts[n], shard[n], local(m, n), local(v, n))
        out_g[n], out_d[n], out_m[n], out_v[n] = [(r.T if n in _TRANSPOSED else r).reshape(shapes[n]) for r in res]
    row = lambda d: {n: d[n].reshape(1, -1) for n, _ in _SMALL}
    sres, loss_row = _adamw_small(small_parts, row(w), row(m), row(v))
    for n, _ in _SMALL:
        out_g[n], out_d[n], out_m[n], out_v[n] = [r.reshape(shapes[n]) for r in sres[n]]
    loss = loss_row[0, 0]
    return (loss, grad_x[None], *[out_g[n] for n in _ORDER], *[out_d[n] for n in _ORDER],
            *[out_m[n] for n in _ORDER], *[out_v[n] for n in _ORDER])
```

```python
import functools

import jax
import jax.numpy as jnp
from jax import lax
from jax.experimental import pallas as pl
from jax.experimental.pallas import tpu as pltpu

F32 = jnp.float32
BF = jnp.bfloat16

N_DEV = 8
D_MODEL = 1024
D_FF = 2816
D_INNER = 2048
D_XBC = 4096
N_HEADS = 32
HEAD_DIM = 64
N_GROUPS = 8
D_STATE = 128
CHUNK = 64
GROUP_W = D_INNER // N_GROUPS
NORM_EPS = 1e-5
N_IN = 11296
FF_SHARD = 2 * D_FF // N_DEV
FF_HALF = D_FF // 2
IN_SHARD = N_IN // N_DEV

OFF_B, OFF_C, OFF_XA, OFF_Z, OFF_XBC = 0, 1024, 2048, 3072, 5120
N_MAIN = 9216
OFF_DT = 2048
DT_W = 128
N_GD = 2048 + DT_W
W_GRAD_ROW_CUTS = [(0, 400), (400, 568), (568, 704), (704, 880), (880, 1040), (1040, 1240), (1240, 1412)]

ADAM_LR, ADAM_B1, ADAM_B2, ADAM_EPS, ADAM_WD, ADAM_STEP = 0.001, 0.9, 0.999, 1e-08, 0.01, 10

VMEM_LIMIT_V7X = 56 * 1024 * 1024
TM = 1024
TN_MAX_TOKENS = 2048
TE = 512
ADAM_COL_TILE = 256
GATHER_PIECES = 4
GATHER_PIECE_MIN_ROWS = 512


def _params(*sem):
    return pltpu.CompilerParams(dimension_semantics=sem, vmem_limit_bytes=VMEM_LIMIT_V7X)


_DIMS = {
    "nn": (((1,), (0,)), ((), ())),
    "nt": (((1,), (1,)), ((), ())),
    "tn": (((0,), (0,)), ((), ())),
}


def _dot(a, b, mode="nn"):
    return lax.dot_general(a, b, _DIMS[mode], preferred_element_type=F32)


def _sigmoid(x):
    return 1.0 / (1.0 + jnp.exp(-x))


class _Comm:
    def __init__(self, inputs, out_shapes, sems, start, finish):
        self.inputs, self.out_shapes, self.sems, self.start, self.finish = inputs, out_shapes, sems, start, finish


def _pcall(name, body, grid, in_specs, out_specs, out_shape, args, scratch=(), sem=None, comm=None, aliases=None):
    single = not isinstance(out_shape, (list, tuple))
    out_shapes = [out_shape] if single else list(out_shape)
    out_specs = [out_specs] if single else list(out_specs)
    n_in, n_out, n_scr = len(args), len(out_shapes), len(scratch)
    aliases = {} if aliases is None else aliases
    if comm is None:
        res = pl.pallas_call(
            body, name=name, grid=grid, in_specs=list(in_specs), out_specs=out_specs, out_shape=out_shapes,
            scratch_shapes=list(scratch), input_output_aliases=aliases, compiler_params=_params(*sem))(*args)
        return (res[0] if single else res), []
    nci, nco = len(comm.inputs), len(comm.out_shapes)

    def wrapped(*refs):
        a = refs[:n_in]
        ci = refs[n_in:n_in + nci]
        o0 = n_in + nci
        o = refs[o0:o0 + n_out]
        co = refs[o0 + n_out:o0 + n_out + nco]
        s0 = o0 + n_out + nco
        s = refs[s0:s0 + n_scr]
        cs = refs[s0 + n_scr:]
        pids = [pl.program_id(i) for i in range(len(grid))]
        first = functools.reduce(jnp.logical_and, [p == 0 for p in pids])
        last = functools.reduce(jnp.logical_and, [p == g - 1 for p, g in zip(pids, grid)])

        @pl.when(first)
        def _():
            comm.start(ci, co, cs)

        body(*a, *o, *s)

        @pl.when(last)
        def _():
            comm.finish(ci, co, cs)

    any_spec = pl.BlockSpec(memory_space=pl.ANY)
    res = pl.pallas_call(
        wrapped, name=name, grid=grid, in_specs=list(in_specs) + [any_spec] * nci,
        out_specs=out_specs + [any_spec] * nco, out_shape=out_shapes + list(comm.out_shapes),
        scratch_shapes=list(scratch) + list(comm.sems), input_output_aliases=aliases,
        compiler_params=_params(*(("arbitrary",) * len(grid))))(*args, *comm.inputs)
    core = res[:n_out]
    return (core[0] if single else core), list(res[n_out:])


def _comm_call(name, comm):
    nci, nco = len(comm.inputs), len(comm.out_shapes)

    def body(*refs):
        ci, co, cs = refs[:nci], refs[nci:nci + nco], refs[nci + nco:]
        comm.start(ci, co, cs)
        comm.finish(ci, co, cs)

    any_spec = pl.BlockSpec(memory_space=pl.ANY)
    return pl.pallas_call(
        body, name=name, in_specs=[any_spec] * nci, out_specs=[any_spec] * nco, out_shape=list(comm.out_shapes),
        scratch_shapes=list(comm.sems), compiler_params=pltpu.CompilerParams(has_side_effects=True))(*comm.inputs)


def _remote(src, dst, ssem, rsem, dev):
    return pltpu.make_async_remote_copy(src_ref=src, dst_ref=dst, send_sem=ssem, recv_sem=rsem, device_id=dev,
                                        device_id_type=pl.DeviceIdType.MESH)


def _place():
    x, y, c = lax.axis_index("x"), lax.axis_index("y"), lax.axis_index("c")
    other_chips = [(1 - x, y), (x, 1 - y), (1 - x, 1 - y)]
    return x, y, c, other_chips


def _slot(x, y, c, swap):
    return 4 * y + 2 * x + c if swap else 4 * x + 2 * y + c


def _chip_slot(x, y, swap):
    return 2 * y + x if swap else 2 * x + y


def _gather_comm(shards, swaps=None):
    n = len(shards)
    per = N_DEV - 1
    swaps = [False] * n if swaps is None else swaps
    pieces = []
    for i, a in enumerate(shards):
        rows = a.shape[0]
        k = GATHER_PIECES if (a.ndim == 2 and rows >= GATHER_PIECE_MIN_ROWS) else 1
        step = -(-rows // (k * 8)) * 8
        if k == 1:
            pieces.append((i, 0, None))
        else:
            pieces += [(i, r, min(step, rows - r)) for r in range(0, rows, step)]
    m = len(pieces)

    def src(ins, v):
        i, r, cnt = pieces[v]
        return ins[i] if cnt is None else ins[i].at[pl.ds(r, cnt)]

    def place(outs, v, x, y, c):
        i, r, cnt = pieces[v]
        blk = outs[i].at[_slot(x, y, c, swaps[i])]
        return blk if cnt is None else blk.at[pl.ds(r, cnt)]

    def start(ins, outs, sems):
        send, recv, loc = sems
        x, y, c, chips = _place()
        for v in range(m):
            me = place(outs, v, x, y, c)
            pltpu.make_async_copy(src(ins, v), me, loc.at[v]).start()
            _remote(src(ins, v), me, send.at[per * v], recv.at[per * v], (x, y, 1 - c)).start()
        for j, (qx, qy) in enumerate(chips):
            for v in range(m):
                _remote(src(ins, v), place(outs, v, x, y, c), send.at[per * v + 1 + j], recv.at[per * v + 1 + j],
                        (qx, qy, c)).start()

    def finish(ins, outs, sems):
        send, recv, loc = sems
        x, y, c, chips = _place()
        sib = (x, y, 1 - c)
        for v in range(m):
            for j, (qx, qy) in enumerate(chips):
                blk = place(outs, v, qx, qy, c)
                _remote(blk, blk, send.at[per * v + 1 + j], recv.at[per * v + 1 + j], (qx, qy, c)).wait_recv()
                _remote(blk, blk, send.at[per * v + 4 + j], recv.at[per * v + 4 + j], sib).start()
        for v in range(m):
            blk = place(outs, v, x, y, 1 - c)
            _remote(blk, blk, send.at[per * v], recv.at[per * v], sib).wait_recv()
            for j, (qx, qy) in enumerate(chips):
                blk = place(outs, v, qx, qy, 1 - c)
                _remote(blk, blk, send.at[per * v + 4 + j], recv.at[per * v + 4 + j], sib).wait_recv()
        for v in range(m):
            own = place(outs, v, x, y, c)
            for k in range(per):
                _remote(src(ins, v), own, send.at[per * v + k], recv.at[per * v + k], sib).wait_send()
            pltpu.make_async_copy(src(ins, v), own, loc.at[v]).wait()

    out_shapes = [jax.ShapeDtypeStruct((N_DEV,) + tuple(a.shape), a.dtype) for a in shards]
    sems = [pltpu.SemaphoreType.DMA((per * m,)), pltpu.SemaphoreType.DMA((per * m,)), pltpu.SemaphoreType.DMA((m,))]
    return _Comm(list(shards), out_shapes, sems, start, finish)


def _pair_comm(slots):
    n = len(slots)

    def copies(ins, outs, sems):
        send, recv = sems
        x, y, c, _ = _place()
        sib = (x, y, 1 - c)
        out = []
        for i in range(n):
            for q in range(4):
                out.append(_remote(ins[i].at[2 * q + 1 - c], outs[i].at[q], send.at[4 * i + q], recv.at[4 * i + q], sib))
        return out

    def start(ins, outs, sems):
        for cp in copies(ins, outs, sems):
            cp.start()

    def finish(ins, outs, sems):
        for cp in copies(ins, outs, sems):
            cp.wait_send()
            cp.wait_recv()

    out_shapes = [jax.ShapeDtypeStruct((4,) + tuple(a.shape[1:]), a.dtype) for a in slots]
    sems = [pltpu.SemaphoreType.DMA((4 * n,)), pltpu.SemaphoreType.DMA((4 * n,))]
    return _Comm(list(slots), out_shapes, sems, start, finish)


def _chip_comm(chip_sums, swaps=None, rows=None):
    n = len(chip_sums)
    swaps = [False] * n if swaps is None else swaps
    rows = [None] * n if rows is None else rows

    def src(ins, i, q):
        return ins[i].at[q] if rows[i] is None else ins[i].at[q, pl.ds(rows[i][0], rows[i][1] - rows[i][0])]

    def start(ins, outs, sems):
        send, recv, loc = sems
        x, y, c, chips = _place()
        for i in range(n):
            mine = _chip_slot(x, y, swaps[i])
            pltpu.make_async_copy(src(ins, i, mine), outs[i].at[mine], loc.at[i]).start()
            for j, (qx, qy) in enumerate(chips):
                _remote(src(ins, i, _chip_slot(qx, qy, swaps[i])), outs[i].at[mine], send.at[3 * i + j],
                        recv.at[3 * i + j], (qx, qy, c)).start()

    def finish(ins, outs, sems):
        send, recv, loc = sems
        x, y, c, chips = _place()
        for i in range(n):
            mine = _chip_slot(x, y, swaps[i])
            for j, (qx, qy) in enumerate(chips):
                theirs = _chip_slot(qx, qy, swaps[i])
                cp = _remote(src(ins, i, theirs), outs[i].at[theirs], send.at[3 * i + j], recv.at[3 * i + j], (qx, qy, c))
                cp.wait_send()
                cp.wait_recv()
            pltpu.make_async_copy(src(ins, i, mine), outs[i].at[mine], loc.at[i]).wait()

    def out_shape(a, r):
        shape = a.shape if r is None else (a.shape[0], r[1] - r[0]) + tuple(a.shape[2:])
        return jax.ShapeDtypeStruct(shape, a.dtype)

    out_shapes = [out_shape(a, r) for a, r in zip(chip_sums, rows)]
    sems = [pltpu.SemaphoreType.DMA((3 * n,)), pltpu.SemaphoreType.DMA((3 * n,)), pltpu.SemaphoreType.DMA((n,))]
    return _Comm(list(chip_sums), out_shapes, sems, start, finish)


def _join_comm(a, b):
    na_i, na_o, na_s = len(a.inputs), len(a.out_shapes), len(a.sems)

    def start(ins, outs, sems):
        a.start(ins[:na_i], outs[:na_o], sems[:na_s])
        b.start(ins[na_i:], outs[na_o:], sems[na_s:])

    def finish(ins, outs, sems):
        a.finish(ins[:na_i], outs[:na_o], sems[:na_s])
        b.finish(ins[na_i:], outs[na_o:], sems[na_s:])

    return _Comm(a.inputs + b.inputs, a.out_shapes + b.out_shapes, a.sems + b.sems, start, finish)


def _row_tile(r):
    for cand in (256, 128):
        if r > cand and r % cand == 0:
            return cand
    return r


def _add_pairs(name, slots, sib):
    r, c = slots.shape[1:]
    tr = _row_tile(r)

    def body(core_ref, s_ref, b_ref, o_ref):
        o_ref[...] = (s_ref[...].astype(F32) + b_ref[...].astype(F32)).astype(o_ref.dtype)

    core = jnp.full((1,), lax.axis_index("c"), jnp.int32)
    return pl.pallas_call(
        body, name=name,
        grid_spec=pltpu.PrefetchScalarGridSpec(
            num_scalar_prefetch=1, grid=(4, r // tr),
            in_specs=[pl.BlockSpec((None, tr, c), lambda q, i, core_ref: (2 * q + core_ref[0], i, 0)),
                      pl.BlockSpec((None, tr, c), lambda q, i, core_ref: (q, i, 0))],
            out_specs=pl.BlockSpec((None, tr, c), lambda q, i, core_ref: (q, i, 0))),
        out_shape=jax.ShapeDtypeStruct((4, r, c), slots.dtype),
        compiler_params=_params("parallel", "parallel"))(core, slots, sib)


def _matmul(name, mode, a, b, grid, a_spec, b_spec, o_spec, out_shape, acc_shape,
            res=None, res_spec=None, alpha=1.0, comm=None):
    nk = grid[-1]
    has_res = res is not None

    def body(*refs):
        if has_res:
            a_ref, b_ref, r_ref, o_ref = refs[:4]
        else:
            a_ref, b_ref, o_ref = refs[:3]
            r_ref = None
        part = _dot(a_ref[...], b_ref[...], mode)

        def finish(v):
            if alpha != 1.0:
                v = v * alpha
            if has_res:
                v = r_ref[...] + v
            o_ref[...] = v.astype(o_ref.dtype)

        if nk == 1:
            finish(part)
        else:
            acc = refs[-1]
            k = pl.program_id(len(grid) - 1)

            @pl.when(k == 0)
            def _():
                acc[...] = part

            @pl.when(k > 0)
            def _():
                acc[...] += part

            @pl.when(k == nk - 1)
            def _():
                finish(acc[...])

    in_specs = [a_spec, b_spec] + ([res_spec] if has_res else [])
    args = (a, b) + ((res,) if has_res else ())
    scratch = [] if nk == 1 else [pltpu.VMEM(acc_shape, F32)]
    sem = ("parallel",) * (len(grid) - 1) + ("arbitrary",)
    out, couts = _pcall(name, body, grid, in_specs, o_spec, out_shape, args, scratch, sem, comm)
    return out if comm is None else (out, couts)


def _mm_nn(name, a, b, out_dtype=F32, res=None, alpha=1.0, tk=None, kk=None, a_off=0, b_off=0, comm=None):
    t = a.shape[0]
    kk = a.shape[1] if kk is None else kk
    n = b.shape[1]
    tk = kk if tk is None else tk
    grid = (t // TM, 1, kk // tk)
    return _matmul(
        name, "nn", a, b, grid,
        pl.BlockSpec((TM, tk), lambda i, j, k: (i, k + a_off)),
        pl.BlockSpec((tk, n), lambda i, j, k: (k + b_off, 0)),
        pl.BlockSpec((TM, n), lambda i, j, k: (i, 0)),
        jax.ShapeDtypeStruct((t, n), out_dtype), (TM, n),
        res=res, res_spec=pl.BlockSpec((TM, n), lambda i, j, k: (i, 0)), alpha=alpha, comm=comm)


def _mm_nt(name, a, b, n=None, tn=None, tk=None, out_dtype=F32, comm=None):
    t, kk = a.shape
    n = b.shape[0] if n is None else n
    tn = n if tn is None else tn
    tk = kk if tk is None else tk
    grid = (n // tn, t // TM, kk // tk)
    return _matmul(
        name, "nt", a, b, grid,
        pl.BlockSpec((TM, tk), lambda j, i, k: (i, k)),
        pl.BlockSpec((tn, tk), lambda j, i, k: (j, k)),
        pl.BlockSpec((TM, tn), lambda j, i, k: (i, j)),
        jax.ShapeDtypeStruct((t, n), out_dtype), (TM, tn), comm=comm)


def _mm_tn(name, a, b, out_dtype, tm=None, n=None, col_off=0, comm=None):
    t, m = a.shape
    n = b.shape[1] if n is None else n
    tm = m if tm is None else tm
    tk = t if t <= TN_MAX_TOKENS else TM
    grid = (m // tm, 1, t // tk)
    return _matmul(
        name, "tn", a, b, grid,
        pl.BlockSpec((tk, tm), lambda j, i, k: (k, j)),
        pl.BlockSpec((tk, n), lambda j, i, k: (k, col_off)),
        pl.BlockSpec((tm, n), lambda j, i, k: (j, 0)),
        jax.ShapeDtypeStruct((m, n), out_dtype), (tm, n), comm=comm)


def _rms_fwd(name, x, w, comm=None):
    t, d = x.shape

    def body(x_ref, w_ref, h_ref):
        xv = x_ref[...]
        rstd = lax.rsqrt(jnp.mean(xv * xv, axis=-1, keepdims=True) + NORM_EPS)
        h_ref[...] = (xv * rstd * w_ref[...]).astype(h_ref.dtype)

    out, couts = _pcall(
        name, body, (t // TE,),
        [pl.BlockSpec((TE, d), lambda i: (i, 0)), pl.BlockSpec((1, d), lambda i: (0, 0))],
        pl.BlockSpec((TE, d), lambda i: (i, 0)), jax.ShapeDtypeStruct((t, d), BF), (x, w), (), ("parallel",), comm)
    return out if comm is None else (out, couts)


def _rms_bwd(name, x, w, dh, dres, out_scale, comm=None):
    t, d = x.shape

    def body(x_ref, w_ref, dh_ref, dres_ref, dx_ref, dxb_ref, dw_ref):
        i = pl.program_id(0)
        xv = x_ref[...]
        rstd = lax.rsqrt(jnp.mean(xv * xv, axis=-1, keepdims=True) + NORM_EPS)
        xhat = xv * rstd
        dhv = dh_ref[...]
        wd = dhv * w_ref[...]
        proj = jnp.mean(wd * xhat, axis=-1, keepdims=True)
        dx = dres_ref[...] + rstd * (wd - xhat * proj)
        dx_ref[...] = dx
        dxb_ref[...] = (dx * out_scale).astype(BF)
        part = jnp.sum(dhv * xhat, axis=0, keepdims=True)

        @pl.when(i == 0)
        def _():
            dw_ref[...] = part

        @pl.when(i > 0)
        def _():
            dw_ref[...] += part

    row = pl.BlockSpec((TE, d), lambda i: (i, 0))
    vec = pl.BlockSpec((1, d), lambda i: (0, 0))
    outs, couts = _pcall(
        name, body, (t // TE,), [row, vec, row, row], [row, row, vec],
        [jax.ShapeDtypeStruct((t, d), F32), jax.ShapeDtypeStruct((t, d), BF), jax.ShapeDtypeStruct((1, d), F32)],
        (x, w, dh, dres), (), ("arbitrary",), comm)
    return outs if comm is None else (outs, couts)


def _final_loss(x, w, target):
    t, d = x.shape

    def body(x_ref, w_ref, t_ref, loss_ref, dx_ref, dxb_ref, dw_ref):
        i = pl.program_id(0)
        xv = x_ref[...]
        rstd = lax.rsqrt(jnp.mean(xv * xv, axis=-1, keepdims=True) + NORM_EPS)
        xhat = xv * rstd
        err = xhat * w_ref[...] - t_ref[...]
        lpart = 0.5 * jnp.sum(jnp.mean(err * err, axis=-1, keepdims=True), axis=0, keepdims=True)
        dy = err * (1.0 / d)
        wd = dy * w_ref[...]
        proj = jnp.mean(wd * xhat, axis=-1, keepdims=True)
        dx = rstd * (wd - xhat * proj)
        dx_ref[...] = dx
        dxb_ref[...] = (0.5 * dx).astype(BF)
        part = jnp.sum(dy * xhat, axis=0, keepdims=True)
        lfull = jnp.broadcast_to(lpart, (1, 128))

        @pl.when(i == 0)
        def _():
            dw_ref[...] = part
            loss_ref[...] = lfull

        @pl.when(i > 0)
        def _():
            dw_ref[...] += part
            loss_ref[...] += lfull

    row = pl.BlockSpec((TE, d), lambda i: (i, 0))
    vec = pl.BlockSpec((1, d), lambda i: (0, 0))
    return pl.pallas_call(
        body, name="final_loss", grid=(t // TE,), in_specs=[row, vec, row],
        out_specs=[pl.BlockSpec((1, 128), lambda i: (0, 0)), row, row, vec],
        out_shape=[jax.ShapeDtypeStruct((1, 128), F32), jax.ShapeDtypeStruct((t, d), F32),
                   jax.ShapeDtypeStruct((t, d), BF), jax.ShapeDtypeStruct((1, d), F32)],
        compiler_params=_params("arbitrary"))(x, w, target)


def _swiglu_fwd(name, gu, comm=None):
    t = gu.shape[0]

    def body(g_ref, u_ref, a_ref):
        g = g_ref[...].astype(F32)
        a_ref[...] = (g * _sigmoid(g) * u_ref[...].astype(F32)).astype(BF)

    blk = (TE, FF_HALF)
    out, couts = _pcall(
        name, body, (t // TE, 2),
        [pl.BlockSpec(blk, lambda i, j: (i, 2 * j)), pl.BlockSpec(blk, lambda i, j: (i, 2 * j + 1))],
        pl.BlockSpec(blk, lambda i, j: (i, j)), jax.ShapeDtypeStruct((t, D_FF), BF),
        (gu, gu), (), ("parallel", "parallel"), comm)
    return out if comm is None else (out, couts)


def _swiglu_bwd(name, gu, dact, comm=None):
    t = gu.shape[0]

    def body(g_ref, u_ref, da_ref, o_ref):
        g = g_ref[...].astype(F32)
        da = da_ref[...].astype(F32)
        s = _sigmoid(g)
        o_ref[:, 0:FF_HALF] = (da * u_ref[...].astype(F32) * (s * (1.0 + g * (1.0 - s)))).astype(BF)
        o_ref[:, FF_HALF:2 * FF_HALF] = (da * g * s).astype(BF)

    blk = (TE, FF_HALF)
    out, couts = _pcall(
        name, body, (t // TE, 2),
        [pl.BlockSpec(blk, lambda i, j: (i, 2 * j)), pl.BlockSpec(blk, lambda i, j: (i, 2 * j + 1)),
         pl.BlockSpec(blk, lambda i, j: (i, j))],
        pl.BlockSpec((TE, 2 * FF_HALF), lambda i, j: (i, j)),
        jax.ShapeDtypeStruct((t, 2 * D_FF), BF), (gu, gu, dact), (), ("parallel", "parallel"), comm)
    return out if comm is None else (out, couts)


CONV_CB = 256


CONV_ROWS = 128
CONV_HALO = 16


def _taps_down(ext, w, k):
    shifted = [pltpu.roll(ext, k - 1 - j, 0)[CONV_HALO:] for j in range(k - 1)] + [ext[CONV_HALO:]]
    out = shifted[k - 1] * w[k - 1:k, :]
    for j in range(k - 1):
        out = out + shifted[j] * w[j:j + 1, :]
    return out, shifted


def _taps_up(ext, w, k):
    rows = ext.shape[0]
    n = rows - CONV_HALO
    out = ext[:n] * w[k - 1:k, :]
    for j in range(k - 1):
        out = out + pltpu.roll(ext, rows - (k - 1 - j), 0)[:n] * w[j:j + 1, :]
    return out


def _rows_before(ref, i, r0):
    start = pl.multiple_of(jnp.maximum(r0 - CONV_HALO, 0), CONV_HALO)
    return jnp.where(i > 0, ref[pl.ds(start, CONV_HALO), :].astype(F32), 0.0)


def _rows_after(ref, r0, t):
    start = pl.multiple_of(jnp.minimum(r0 + CONV_ROWS, t - CONV_HALO), CONV_HALO)
    return ref[pl.ds(start, CONV_HALO), :].astype(F32)


def _fold8(v):
    return v.reshape(v.shape[0] // 8, 8, v.shape[1]).sum(axis=0)


def _silu_grad(pre):
    s = _sigmoid(pre)
    return s * (1.0 + pre * (1.0 - s))


def _pspec(t, off):
    base = off // CONV_CB
    return pl.BlockSpec((t, CONV_CB), lambda j: (0, base + j))


def _mix_a_fwd(p, conv_w):
    t = p.shape[0]

    def body(b_ref, c_ref, xa_ref, w_ref, o_ref):
        w = w_ref[...]

        def step(i, carry):
            r0 = pl.multiple_of(i * CONV_ROWS, CONV_ROWS)
            rows = pl.ds(r0, CONV_ROWS)
            q = c_ref[rows, :].astype(F32) * xa_ref[rows, :].astype(F32)
            q_before = _rows_before(c_ref, i, r0) * _rows_before(xa_ref, i, r0)
            va, _ = _taps_down(jnp.concatenate([q_before, q], axis=0), w, 3)
            o_ref[rows, :] = (b_ref[rows, :].astype(F32) * va).astype(BF)
            return carry

        lax.fori_loop(0, t // CONV_ROWS, step, 0)

    return pl.pallas_call(
        body, name="mix_a_fwd", grid=(D_MODEL // CONV_CB,),
        in_specs=[_pspec(t, OFF_B), _pspec(t, OFF_C), _pspec(t, OFF_XA),
                  pl.BlockSpec((3, CONV_CB), lambda j: (0, j))],
        out_specs=pl.BlockSpec((t, CONV_CB), lambda j: (0, j)),
        out_shape=jax.ShapeDtypeStruct((t, D_MODEL), BF), compiler_params=_params("parallel"))(p, p, p, conv_w)


def _mix_a_bwd(p, conv_w, dya, dp):
    t = p.shape[0]

    def body(b_ref, c_ref, xa_ref, w_ref, dy_ref, dp_in, dp_ref, dw_ref):
        del dp_in
        w = w_ref[...]
        n = t // CONV_ROWS

        def step(i, acc):
            r0 = pl.multiple_of(i * CONV_ROWS, CONV_ROWS)
            rows = pl.ds(r0, CONV_ROWS)
            cv = c_ref[rows, :].astype(F32)
            xav = xa_ref[rows, :].astype(F32)
            q_before = _rows_before(c_ref, i, r0) * _rows_before(xa_ref, i, r0)
            va, shifted = _taps_down(jnp.concatenate([q_before, cv * xav], axis=0), w, 3)
            dyv = dy_ref[rows, :]
            dp_ref[rows, 0:CONV_CB] = (dyv * va).astype(BF)
            dv = dyv * b_ref[rows, :].astype(F32)
            dv_after = jnp.where(i < n - 1, _rows_after(dy_ref, r0, t) * _rows_after(b_ref, r0, t), 0.0)
            dq = _taps_up(jnp.concatenate([dv, dv_after], axis=0), w, 3)
            dp_ref[rows, CONV_CB:2 * CONV_CB] = (dq * xav).astype(BF)
            dp_ref[rows, 2 * CONV_CB:3 * CONV_CB] = (dq * cv).astype(BF)
            return tuple(a + _fold8(dv * s) for a, s in zip(acc, shifted))

        zero = jnp.zeros((8, CONV_CB), F32)
        acc = lax.fori_loop(0, n, step, (zero, zero, zero))
        for j in range(3):
            dw_ref[j:j + 1, :] = jnp.sum(acc[j], axis=0, keepdims=True)

    col = pl.BlockSpec((t, CONV_CB), lambda j: (0, j))
    wsp = pl.BlockSpec((3, CONV_CB), lambda j: (0, j))
    return pl.pallas_call(
        body, name="mix_a_bwd", grid=(D_MODEL // CONV_CB,),
        in_specs=[_pspec(t, OFF_B), _pspec(t, OFF_C), _pspec(t, OFF_XA), wsp, col, pl.BlockSpec(memory_space=pl.ANY)],
        out_specs=[pl.BlockSpec((t, 3 * CONV_CB), lambda j: (0, j)), wsp],
        out_shape=[jax.ShapeDtypeStruct(dp.shape, dp.dtype), jax.ShapeDtypeStruct((3, D_MODEL), F32)],
        input_output_aliases={5: 0},
        compiler_params=_params("parallel"))(p, p, p, conv_w, dya, dp)


def _ssm_conv_fwd(p, conv_w, conv_b, comm=None):
    t = p.shape[0]

    def body(x_ref, w_ref, b_ref, o_ref):
        w = w_ref[...]
        bias = b_ref[...]

        def step(i, carry):
            r0 = pl.multiple_of(i * CONV_ROWS, CONV_ROWS)
            rows = pl.ds(r0, CONV_ROWS)
            ext = jnp.concatenate([_rows_before(x_ref, i, r0), x_ref[rows, :].astype(F32)], axis=0)
            pre = _taps_down(ext, w, 4)[0] + bias
            o_ref[rows, :] = pre * _sigmoid(pre)
            return carry

        lax.fori_loop(0, t // CONV_ROWS, step, 0)

    out, couts = _pcall(
        "ssm_conv_fwd", body, (D_XBC // CONV_CB,),
        [_pspec(t, OFF_XBC), pl.BlockSpec((4, CONV_CB), lambda j: (0, j)), pl.BlockSpec((1, CONV_CB), lambda j: (0, j))],
        pl.BlockSpec((t, CONV_CB), lambda j: (0, j)), jax.ShapeDtypeStruct((t, D_XBC), F32),
        (p, conv_w, conv_b), (), ("parallel",), comm)
    return out if comm is None else (out, couts)


def _ssm_conv_bwd(p, conv_w, conv_b, dxc, dp, comm=None):
    t = p.shape[0]

    def body(x_ref, w_ref, b_ref, d_ref, dp_in, dx_ref, dw_ref, db_ref):
        del dp_in
        w = w_ref[...]
        bias = b_ref[...]
        n = t // CONV_ROWS

        def step(i, acc):
            r0 = pl.multiple_of(i * CONV_ROWS, CONV_ROWS)
            rows = pl.ds(r0, CONV_ROWS)
            x_cur = x_ref[rows, :].astype(F32)
            pre, shifted = _taps_down(jnp.concatenate([_rows_before(x_ref, i, r0), x_cur], axis=0), w, 4)
            pre = pre + bias
            dpre = d_ref[rows, :] * _silu_grad(pre)
            ext_after = jnp.concatenate([x_cur[CONV_ROWS - CONV_HALO:], _rows_after(x_ref, r0, t)], axis=0)
            pre_after = _taps_down(ext_after, w, 4)[0] + bias
            dpre_after = jnp.where(i < n - 1, _rows_after(d_ref, r0, t) * _silu_grad(pre_after), 0.0)
            dx_ref[rows, :] = _taps_up(jnp.concatenate([dpre, dpre_after], axis=0), w, 4).astype(BF)
            new = tuple(a + _fold8(dpre * s) for a, s in zip(acc[:4], shifted))
            return new + (acc[4] + _fold8(dpre),)

        zero = jnp.zeros((8, CONV_CB), F32)
        acc = lax.fori_loop(0, n, step, (zero,) * 5)
        for j in range(4):
            dw_ref[j:j + 1, :] = jnp.sum(acc[j], axis=0, keepdims=True)
        db_ref[...] = jnp.sum(acc[4], axis=0, keepdims=True)

    col = pl.BlockSpec((t, CONV_CB), lambda j: (0, j))
    wsp = pl.BlockSpec((4, CONV_CB), lambda j: (0, j))
    bsp = pl.BlockSpec((1, CONV_CB), lambda j: (0, j))
    outs, couts = _pcall(
        "ssm_conv_bwd", body, (D_XBC // CONV_CB,),
        [_pspec(t, OFF_XBC), wsp, bsp, col, pl.BlockSpec(memory_space=pl.ANY)], [_pspec(t, OFF_XBC), wsp, bsp],
        [jax.ShapeDtypeStruct(dp.shape, dp.dtype), jax.ShapeDtypeStruct((4, D_XBC), F32),
         jax.ShapeDtypeStruct((1, D_XBC), F32)],
        (p, conv_w, conv_b, dxc, dp), (), ("parallel",), comm, aliases={4: 0})
    return outs if comm is None else (outs, couts)


DT_ROWS = 512


def _tri(lower):
    r = lax.broadcasted_iota(jnp.int32, (CHUNK, CHUNK), 0)
    c = lax.broadcasted_iota(jnp.int32, (CHUNK, CHUNK), 1)
    return jnp.where((r >= c) if lower else (r <= c), 1.0, 0.0).astype(F32)


def _dot_exact(a, b):
    return lax.dot_general(a, b, _DIMS["nn"], preferred_element_type=F32, precision=lax.Precision.HIGHEST)


def _dt_fwd(p, bias_pad, alog_pad):
    t = p.shape[0]

    def body(raw_ref, b_ref, al_ref, dt_ref, acs_ref):
        z = raw_ref[...] + b_ref[...]
        dt = jnp.maximum(z, 0.0) + jnp.log(1.0 + jnp.exp(-jnp.abs(z)))
        dt_ref[...] = dt
        a = dt * (-jnp.exp(al_ref[...]))
        tri = _tri(True)
        for k in range(DT_ROWS // CHUNK):
            acs_ref[k * CHUNK:(k + 1) * CHUNK, :] = _dot_exact(tri, a[k * CHUNK:(k + 1) * CHUNK, :])

    blk = pl.BlockSpec((DT_ROWS, DT_W), lambda i: (i, 0))
    vec = pl.BlockSpec((1, DT_W), lambda i: (0, 0))
    return pl.pallas_call(
        body, name="dt_fwd", grid=(t // DT_ROWS,),
        in_specs=[pl.BlockSpec((DT_ROWS, DT_W), lambda i: (i, OFF_DT // DT_W)), vec, vec],
        out_specs=[blk, blk], out_shape=[jax.ShapeDtypeStruct((t, DT_W), F32)] * 2,
        compiler_params=_params("parallel"))(p, bias_pad, alog_pad)


def _dt_bwd(p, bias_pad, alog_pad, dt, ddt, dacs, dp_gd):
    t = p.shape[0]

    def body(raw_ref, b_ref, al_ref, dt_ref, ddt_ref, dacs_ref, dp_in, draw_ref, db_ref, dal_ref):
        del dp_in
        i = pl.program_id(0)
        acoef = -jnp.exp(al_ref[...])
        triu = _tri(False)
        das = []
        for k in range(DT_ROWS // CHUNK):
            das.append(_dot_exact(triu, dacs_ref[k * CHUNK:(k + 1) * CHUNK, :]))
        da = jnp.concatenate(das, axis=0)
        dtv = dt_ref[...]
        ddt_tot = ddt_ref[...] + da * acoef
        lane = lax.broadcasted_iota(jnp.int32, (DT_ROWS, DT_W), 1)
        draw = jnp.where(lane < N_HEADS, ddt_tot * _sigmoid(raw_ref[...] + b_ref[...]), 0.0)
        draw_ref[...] = draw.astype(BF)
        pb = jnp.sum(draw, axis=0, keepdims=True)
        pa = jnp.sum(da * dtv * acoef, axis=0, keepdims=True)

        @pl.when(i == 0)
        def _():
            db_ref[...] = pb
            dal_ref[...] = pa

        @pl.when(i > 0)
        def _():
            db_ref[...] += pb
            dal_ref[...] += pa

    blk = pl.BlockSpec((DT_ROWS, DT_W), lambda i: (i, 0))
    vec = pl.BlockSpec((1, DT_W), lambda i: (0, 0))
    return pl.pallas_call(
        body, name="dt_bwd", grid=(t // DT_ROWS,),
        in_specs=[pl.BlockSpec((DT_ROWS, DT_W), lambda i: (i, OFF_DT // DT_W)), vec, vec, blk, blk, blk,
                  pl.BlockSpec(memory_space=pl.ANY)],
        out_specs=[pl.BlockSpec((DT_ROWS, DT_W), lambda i: (i, OFF_DT // DT_W)), vec, vec],
        out_shape=[jax.ShapeDtypeStruct(dp_gd.shape, dp_gd.dtype), jax.ShapeDtypeStruct((1, DT_W), F32),
                   jax.ShapeDtypeStruct((1, DT_W), F32)],
        input_output_aliases={6: 0},
        compiler_params=_params("arbitrary"))(p, bias_pad, alog_pad, dt, ddt, dacs, dp_gd)


def _split_dot(z, onehot, terms):
    out = None
    rest = z
    for _ in range(terms):
        piece = rest.astype(BF)
        part = _dot(piece, onehot)
        out = part if out is None else out + part
        rest = rest - piece.astype(F32)
    return out


def _spread_mat():
    row = lax.broadcasted_iota(jnp.int32, (DT_W, D_INNER), 0)
    lane = lax.broadcasted_iota(jnp.int32, (DT_W, D_INNER), 1)
    return jnp.where(row == lane // HEAD_DIM, 1.0, 0.0).astype(BF)


def _gather_mat():
    row = lax.broadcasted_iota(jnp.int32, (D_INNER, DT_W), 0)
    lane = lax.broadcasted_iota(jnp.int32, (D_INNER, DT_W), 1)
    return jnp.where(lane == row // HEAD_DIM, 1.0, 0.0).astype(BF)


def _ssd_masks():
    row = lax.broadcasted_iota(jnp.int32, (CHUNK, GROUP_W), 0)
    col = lax.broadcasted_iota(jnp.int32, (CHUNK, GROUP_W), 1) % HEAD_DIM
    brow = lax.broadcasted_iota(jnp.int32, (GROUP_W, GROUP_W), 0) // HEAD_DIM
    bcol = lax.broadcasted_iota(jnp.int32, (GROUP_W, GROUP_W), 1) // HEAD_DIM
    return row >= col, row == col, brow == bcol


def _stack4(v):
    return jnp.concatenate([v, v, v, v], axis=0)


def _fold4(v):
    return v[0:CHUNK] + v[CHUNK:2 * CHUNK] + v[2 * CHUNK:3 * CHUNK] + v[3 * CHUNK:4 * CHUNK]


def _ssd_group(xc_ref, wide_ref, g, tri, eye, blockdiag):
    gs = slice(GROUP_W * g, GROUP_W * (g + 1))
    xs_g = xc_ref[:, gs]
    b_g = xc_ref[:, D_INNER + D_STATE * g:D_INNER + D_STATE * (g + 1)].astype(BF)
    c_g = xc_ref[:, D_INNER + 1024 + D_STATE * g:D_INNER + 1024 + D_STATE * (g + 1)].astype(BF)
    acs_e, dt_e = wide_ref[0:CHUNK, gs], wide_ref[CHUNK:2 * CHUNK, gs]
    atot_e = acs_e[CHUNK - 1:CHUNK, :]
    acs_j = jnp.sum(jnp.where(eye, acs_e, 0.0), axis=0, keepdims=True)
    lmat = jnp.where(tri, jnp.exp(jnp.minimum(acs_e - acs_j, 0.0)), 0.0)
    b_t = _stack4(b_g)
    m = _dot(c_g, b_t, "nt") * lmat
    x_g = xs_g * dt_e
    xbd = jnp.where(blockdiag, _stack4(x_g), 0.0).astype(BF)
    return dict(gs=gs, xs=xs_g, b=b_g, c=c_g, b_t=b_t, dt=dt_e, e=jnp.exp(acs_e), dec=jnp.exp(atot_e - acs_e),
                eat=jnp.exp(atot_e), lmat=lmat, m=m, x=x_g, xbd=xbd)


def _ssd_fwd(xconv, dt, acs, d_exp, comm=None):
    t = xconv.shape[0]
    nc = t // CHUNK

    def body(xc_ref, dt_ref, acs_ref, d_ref, y_ref, hs_ref, state, wide):
        c = pl.program_id(0)

        @pl.when(c == 0)
        def _():
            state[...] = jnp.zeros_like(state)

        hs_ref[...] = state[...]
        tri, eye, blockdiag = _ssd_masks()
        wide[...] = _split_dot(jnp.concatenate([acs_ref[...], dt_ref[...]], axis=0), _spread_mat(), 3)
        for g in range(N_GROUPS):
            q = _ssd_group(xc_ref, wide, g, tri, eye, blockdiag)
            gs = q["gs"]
            h_t = state[:, gs]
            ydiag = _dot(q["m"].astype(BF), q["xbd"])
            yoff = _dot(q["c"], h_t.astype(BF)) * q["e"]
            y_ref[:, gs] = ydiag + yoff + d_ref[:, gs] * q["xs"]
            s_t = _dot(q["b"], (q["x"] * q["dec"]).astype(BF), "tn")
            state[:, gs] = q["eat"] * h_t + s_t

    blk = lambda w: pl.BlockSpec((CHUNK, w), lambda c: (c, 0))
    outs, couts = _pcall(
        "ssd_fwd", body, (nc,),
        [blk(D_XBC), blk(DT_W), blk(DT_W), pl.BlockSpec((1, D_INNER), lambda c: (0, 0))],
        [blk(D_INNER), pl.BlockSpec((None, D_STATE, D_INNER), lambda c: (c, 0, 0))],
        [jax.ShapeDtypeStruct((t, D_INNER), F32), jax.ShapeDtypeStruct((nc, D_STATE, D_INNER), F32)],
        (xconv, dt, acs, d_exp), [pltpu.VMEM((D_STATE, D_INNER), F32), pltpu.VMEM((2 * CHUNK, D_INNER), F32)],
        ("arbitrary",), comm)
    return outs if comm is None else (outs, couts)


def _ssd_bwd(xconv, dt, acs, d_exp, hsave, dy, comm=None):
    t = xconv.shape[0]
    nc = t // CHUNK

    def body(xc_ref, dt_ref, acs_ref, d_ref, hs_ref, dy_ref, dxc_ref, ddt_ref, dacs_ref, dd_ref, dstate, wide, per_head):
        c = pl.program_id(0)

        @pl.when(c == 0)
        def _():
            dstate[...] = jnp.zeros_like(dstate)
            dd_ref[...] = jnp.zeros_like(dd_ref)

        tri, eye, blockdiag = _ssd_masks()
        acsv = acs_ref[...]
        wide[...] = _split_dot(jnp.concatenate([acsv, dt_ref[...]], axis=0), _spread_mat(), 3)
        eat_heads = jnp.exp(acsv[CHUNK - 1:CHUNK, :])

        for g in range(N_GROUPS):
            q = _ssd_group(xc_ref, wide, g, tri, eye, blockdiag)
            gs, xs_g, b_g, c_g, m = q["gs"], q["xs"], q["b"], q["c"], q["m"]
            bs = slice(D_INNER + D_STATE * g, D_INNER + D_STATE * (g + 1))
            cs = slice(D_INNER + 1024 + D_STATE * g, D_INNER + 1024 + D_STATE * (g + 1))
            h_t = hs_ref[:, gs]
            h_b = h_t.astype(BF)
            dy_g = dy_ref[:, gs]
            dy_b = dy_g.astype(BF)
            ds_t = dstate[:, gs]
            ds_b = ds_t.astype(BF)

            yoff = _dot(c_g, h_b) * q["e"]
            edy = (q["e"] * dy_g).astype(BF)
            d_c = _dot(edy, h_b, "nt")
            d_ht = _dot(c_g, edy, "tn")
            bds = _dot(b_g, ds_b)
            xd = q["x"] * q["dec"]
            d_b = _dot(xd.astype(BF), ds_b, "nt")
            dm = _dot(dy_b, q["xbd"], "nt")
            cross = _dot(m.astype(BF), dy_b, "tn")
            dx_full = q["dec"] * bds + _fold4(jnp.where(blockdiag, cross, 0.0))
            dml = (dm * q["lmat"]).astype(BF)
            d_c = d_c + _dot(dml, q["b_t"])
            d_b = d_b + _fold4(_dot(dml, c_g, "tn"))
            w = dm * m
            q_dec = xd * bds
            z = w - jnp.where(eye, jnp.sum(w, axis=0, keepdims=True), 0.0) + dy_g * yoff - q_dec
            rows = jnp.concatenate(
                [jnp.sum(q_dec, axis=0, keepdims=True), jnp.sum(ds_t * h_t, axis=0, keepdims=True),
                 jnp.zeros((6, GROUP_W), F32)], axis=0)
            per_head[:, gs] = jnp.concatenate([z, dx_full * xs_g, rows], axis=0)
            dxc_ref[:, cs] = d_c
            dxc_ref[:, bs] = d_b
            dxc_ref[:, gs] = dx_full * q["dt"] + d_ref[:, gs] * dy_g
            dd_ref[:, gs] += jnp.sum(dy_g * xs_g, axis=0, keepdims=True)
            dstate[:, gs] = q["eat"] * ds_t + d_ht

        seg = _split_dot(per_head[...], _gather_mat(), 2)
        datot = seg[2 * CHUNK:2 * CHUNK + 1] + eat_heads * seg[2 * CHUNK + 1:2 * CHUNK + 2]
        rowi = lax.broadcasted_iota(jnp.int32, (CHUNK, DT_W), 0)
        ddt_ref[...] = seg[CHUNK:2 * CHUNK]
        dacs_ref[...] = seg[0:CHUNK] + jnp.where(rowi == CHUNK - 1, datot, 0.0)

    rev = lambda w: pl.BlockSpec((CHUNK, w), lambda c: (nc - 1 - c, 0))
    vec = pl.BlockSpec((1, D_INNER), lambda c: (0, 0))
    outs, couts = _pcall(
        "ssd_bwd", body, (nc,),
        [rev(D_XBC), rev(DT_W), rev(DT_W), vec,
         pl.BlockSpec((None, D_STATE, D_INNER), lambda c: (nc - 1 - c, 0, 0)), rev(D_INNER)],
        [rev(D_XBC), rev(DT_W), rev(DT_W), vec],
        [jax.ShapeDtypeStruct((t, D_XBC), F32), jax.ShapeDtypeStruct((t, DT_W), F32),
         jax.ShapeDtypeStruct((t, DT_W), F32), jax.ShapeDtypeStruct((1, D_INNER), F32)],
        (xconv, dt, acs, d_exp, hsave, dy),
        [pltpu.VMEM((D_STATE, D_INNER), F32), pltpu.VMEM((2 * CHUNK, D_INNER), F32),
         pltpu.VMEM((2 * CHUNK + 8, D_INNER), F32)], ("arbitrary",), comm)
    return outs if comm is None else (outs, couts)


GN_CB = 1024
GN_GROUPS = GN_CB // GROUP_W


def _gnorm_fwd(y, p, w, comm=None):
    t = y.shape[0]
    zoff = OFF_Z // GN_CB

    def body(y_ref, z_ref, w_ref, o_ref):
        for g in range(GN_GROUPS):
            gs = slice(GROUP_W * g, GROUP_W * (g + 1))
            z = z_ref[:, gs].astype(F32)
            yf = y_ref[:, gs] * (z * _sigmoid(z))
            rstd = lax.rsqrt(jnp.mean(yf * yf, axis=-1, keepdims=True) + NORM_EPS)
            o_ref[:, gs] = (yf * rstd * w_ref[:, gs]).astype(BF)

    blk = pl.BlockSpec((TE, GN_CB), lambda i, j: (i, j))
    out, couts = _pcall(
        "gnorm_fwd", body, (t // TE, D_INNER // GN_CB),
        [blk, pl.BlockSpec((TE, GN_CB), lambda i, j: (i, zoff + j)), pl.BlockSpec((1, GN_CB), lambda i, j: (0, j))],
        blk, jax.ShapeDtypeStruct((t, D_INNER), BF), (y, p, w), (), ("parallel", "parallel"), comm)
    return out if comm is None else (out, couts)


def _gnorm_bwd(y, p, w, dyn, comm=None):
    t = y.shape[0]
    zoff = OFF_Z // GN_CB

    def body(y_ref, z_ref, w_ref, dn_ref, dy_ref, dz_ref, dw_ref):
        i = pl.program_id(1)
        for g in range(GN_GROUPS):
            gs = slice(GROUP_W * g, GROUP_W * (g + 1))
            z = z_ref[:, gs].astype(F32)
            yv = y_ref[:, gs]
            s = _sigmoid(z)
            sil = z * s
            yf = yv * sil
            rstd = lax.rsqrt(jnp.mean(yf * yf, axis=-1, keepdims=True) + NORM_EPS)
            xhat = yf * rstd
            dn = dn_ref[:, gs]
            wd = dn * w_ref[:, gs]
            proj = jnp.mean(wd * xhat, axis=-1, keepdims=True)
            dyf = rstd * (wd - xhat * proj)
            dy_ref[:, gs] = dyf * sil
            dz_ref[:, gs] = (dyf * yv * (s * (1.0 + z * (1.0 - s)))).astype(BF)
            part = jnp.sum(dn * xhat, axis=0, keepdims=True)

            @pl.when(i == 0)
            def _():
                dw_ref[:, gs] = part

            @pl.when(i > 0)
            def _():
                dw_ref[:, gs] += part

    blk = pl.BlockSpec((TE, GN_CB), lambda j, i: (i, j))
    vec = pl.BlockSpec((1, GN_CB), lambda j, i: (0, j))
    outs, couts = _pcall(
        "gnorm_bwd", body, (D_INNER // GN_CB, t // TE),
        [blk, pl.BlockSpec((TE, GN_CB), lambda j, i: (i, zoff + j)), vec, blk],
        [blk, pl.BlockSpec((TE, GN_CB), lambda j, i: (i, zoff + j)), vec],
        [jax.ShapeDtypeStruct((t, D_INNER), F32), jax.ShapeDtypeStruct((t, N_MAIN), BF),
         jax.ShapeDtypeStruct((1, D_INNER), F32)],
        (y, p, w, dyn), (), ("parallel", "arbitrary"), comm)
    return outs if comm is None else (outs, couts)


MERGE_CB = 512


def _merge_fwd(p, ya, yb):
    t = ya.shape[0]

    def body(ga_ref, gb_ref, ya_ref, yb_ref, o_ref):
        o_ref[...] = (_sigmoid(ga_ref[...]) * ya_ref[...] + _sigmoid(gb_ref[...]) * yb_ref[...]).astype(BF)

    blk = pl.BlockSpec((TE, MERGE_CB), lambda i, j: (i, j))
    return pl.pallas_call(
        body, name="merge_fwd", grid=(t // TE, D_MODEL // MERGE_CB),
        in_specs=[pl.BlockSpec((TE, MERGE_CB), lambda i, j: (i, 2 * j)),
                  pl.BlockSpec((TE, MERGE_CB), lambda i, j: (i, 2 * j + 1)), blk, blk],
        out_specs=blk, out_shape=jax.ShapeDtypeStruct((t, D_MODEL), BF),
        compiler_params=_params("parallel", "parallel"))(p, p, ya, yb)


def _merge_bwd(p, ya, yb, dm):
    t = ya.shape[0]

    def body(ga_ref, gb_ref, ya_ref, yb_ref, dm_ref, dg_ref, dya_ref, dyb_ref):
        d = dm_ref[...]
        sa = _sigmoid(ga_ref[...])
        sb = _sigmoid(gb_ref[...])
        dg_ref[:, 0:MERGE_CB] = (d * ya_ref[...] * sa * (1.0 - sa)).astype(BF)
        dg_ref[:, MERGE_CB:2 * MERGE_CB] = (d * yb_ref[...] * sb * (1.0 - sb)).astype(BF)
        dya_ref[...] = (d * sa).astype(BF)
        dyb_ref[...] = (d * sb).astype(BF)

    blk = pl.BlockSpec((TE, MERGE_CB), lambda i, j: (i, j))
    return pl.pallas_call(
        body, name="merge_bwd", grid=(t // TE, D_MODEL // MERGE_CB),
        in_specs=[pl.BlockSpec((TE, MERGE_CB), lambda i, j: (i, 2 * j)),
                  pl.BlockSpec((TE, MERGE_CB), lambda i, j: (i, 2 * j + 1)), blk, blk, blk],
        out_specs=[pl.BlockSpec((TE, 2 * MERGE_CB), lambda i, j: (i, j)), blk, blk],
        out_shape=[jax.ShapeDtypeStruct((t, N_GD), BF)] + [jax.ShapeDtypeStruct((t, D_MODEL), BF)] * 2,
        compiler_params=_params("parallel", "parallel"))(p, p, ya, yb, dm)


def _adamw(name, parts, w, m, v, comm=None):
    r, c = w.shape
    tr = _row_tile(r)
    tc = ADAM_COL_TILE if (tr == r and r > 512 and c % ADAM_COL_TILE == 0) else c
    n_parts = parts.shape[0]
    bc1 = 1.0 - ADAM_B1 ** ADAM_STEP
    bc2 = 1.0 - ADAM_B2 ** ADAM_STEP

    def body(p_ref, w_ref, m_ref, v_ref, g_ref, d_ref, nm_ref, nv_ref):
        g = p_ref[0].astype(F32)
        for k in range(1, n_parts):
            g = g + p_ref[k].astype(F32)
        nm = ADAM_B1 * m_ref[...] + (1.0 - ADAM_B1) * g
        nv = ADAM_B2 * v_ref[...] + (1.0 - ADAM_B2) * (g * g)
        g_ref[...] = g
        nm_ref[...] = nm
        nv_ref[...] = nv
        d_ref[...] = -ADAM_LR * ((nm / bc1) / (jnp.sqrt(nv / bc2) + ADAM_EPS) + ADAM_WD * w_ref[...])

    blk = pl.BlockSpec((tr, tc), lambda i, j: (i, j))
    outs, couts = _pcall(
        name, body, (r // tr, c // tc),
        [pl.BlockSpec((n_parts, tr, tc), lambda i, j: (0, i, j)), blk, blk, blk], [blk] * 4,
        [jax.ShapeDtypeStruct((r, c), F32)] * 4, (parts, w, m, v), (), ("parallel", "parallel"), comm)
    return outs if comm is None else (outs, couts)


def _pad_lanes(v, width):
    return jnp.pad(v, ((0, 0), (0, width - v.shape[1])))


def _reduce_start(slots, host):
    outs, sib = host(_pair_comm([a for _, a in slots]))
    sums = [(n, _add_pairs("pairsum_" + n, a, b)) for (n, a), b in zip(slots, sib)]
    return outs, sums


def _train_step(x, target, shard, rep):
    gdt = BF
    recv = {}
    h1, (got,) = _rms_fwd("rms1_fwd", x, rep["ffn1_norm"], comm=_gather_comm([shard["ffn1_w_in"]], [True]))
    w1_in = got.reshape(2 * D_FF, D_MODEL)
    gu1, got = _mm_nt("ffn1_in", h1, w1_in, tn=FF_HALF, out_dtype=BF, comm=_gather_comm(
        [shard["ffn1_w_out"], shard["w_in"], shard["short_conv_w"], shard["ssm_conv_w"]]))
    w1_out = got[0].reshape(D_FF, D_MODEL)
    w_in_t = got[1].reshape(N_IN, D_MODEL)
    short_conv_w = got[2].transpose(1, 0, 2).reshape(3, D_MODEL)
    ssm_conv_w = got[3].transpose(1, 0, 2).reshape(4, D_XBC)
    act1 = _swiglu_fwd("swiglu1_fwd", gu1)
    x1 = _mm_nn("ffn1_out", act1, w1_out, res=x, alpha=0.5)
    ga0 = N_MAIN + N_HEADS
    gb0 = ga0 + D_MODEL
    half = D_MODEL // 2
    w_gd = jnp.concatenate(
        [w_in_t[ga0:ga0 + half], w_in_t[gb0:gb0 + half], w_in_t[ga0 + half:gb0], w_in_t[gb0 + half:],
         w_in_t[N_MAIN:N_MAIN + N_HEADS], jnp.zeros((DT_W - N_HEADS, D_MODEL), BF)], axis=0)
    w_mix_perm = w_in_t[0:3 * D_MODEL].reshape(3, 4, CONV_CB, D_MODEL).transpose(1, 0, 2, 3).reshape(3 * D_MODEL, D_MODEL)

    h2 = _rms_fwd("rms2_fwd", x1, rep["mix_norm"])
    p, got = _mm_nt("proj_main", h2, w_in_t, n=N_MAIN, tn=1024, out_dtype=BF, comm=_gather_comm(
        [shard["short_w_out"], shard["ssm_w_out"], shard["w_out"]]))
    p_gd = _mm_nt("proj_gd", h2, w_gd)
    short_w_out = got[0].reshape(D_MODEL, D_MODEL)
    ssm_w_out = got[1].reshape(D_INNER, D_MODEL)
    w_out = got[2].reshape(D_MODEL, D_MODEL)
    ya_in = _mix_a_fwd(p, short_conv_w)
    y_a = _mm_nn("short_out", ya_in, short_w_out)
    xconv, (got,) = _ssm_conv_fwd(p, ssm_conv_w, rep["ssm_conv_b"], comm=_gather_comm([shard["ffn2_w_out"]]))
    w2_out = got.reshape(D_FF, D_MODEL)
    dt, acs = _dt_fwd(p_gd, rep["dt_bias_pad"], rep["a_log_pad"])
    (y_ssm, hsave), (got,) = _ssd_fwd(xconv, dt, acs, rep["d_exp"], comm=_gather_comm([shard["ffn2_w_in"]], [True]))
    w2_in = got.reshape(2 * D_FF, D_MODEL)
    yn = _gnorm_fwd(y_ssm, p, rep["ssm_norm"])
    y_b = _mm_nn("ssm_out", yn, ssm_w_out, tk=1024)
    merged = _merge_fwd(p_gd, y_a, y_b)
    x2 = _mm_nn("mix_out", merged, w_out, res=x1)

    h3 = _rms_fwd("rms3_fwd", x2, rep["ffn2_norm"])
    gu2 = _mm_nt("ffn2_in", h3, w2_in, tn=FF_HALF, out_dtype=BF)
    act2 = _swiglu_fwd("swiglu2_fwd", gu2)
    x3 = _mm_nn("ffn2_out", act2, w2_out, res=x2, alpha=0.5)

    loss, dx3, dx3h, g_final = _final_loss(x3, rep["final_norm"], target)

    small = {"final_norm": g_final}
    dact2 = _mm_nt("ffn2_out_bwd_act", dx3h, w2_out, out_dtype=BF)
    g_w2_out = _mm_tn("ffn2_out_bwd_w", act2, dx3h, gdt, tm=FF_HALF)
    dgu2 = _swiglu_bwd("swiglu2_bwd", gu2, dact2)
    g_w2_in = _mm_tn("ffn2_in_bwd_w", dgu2, h3, gdt, tm=FF_HALF)
    dh3 = _mm_nn("ffn2_in_bwd_h", dgu2, w2_in, tk=FF_HALF)
    dx2, dx2b, small["ffn2_norm"] = _rms_bwd("rms3_bwd", x2, rep["ffn2_norm"], dh3, dx3, 1.0)

    dmerged = _mm_nt("mix_out_bwd_x", dx2b, w_out)
    g_w_out = _mm_tn("mix_out_bwd_w", merged, dx2b, gdt)
    dp_gd, dya, dyb = _merge_bwd(p_gd, y_a, y_b, dmerged)

    dya_in = _mm_nt("short_out_bwd_x", dya, short_w_out)
    g_short_w_out = _mm_tn("short_out_bwd_w", ya_in, dya, gdt)

    dyn = _mm_nt("ssm_out_bwd_x", dyb, ssm_w_out)
    g_ssm_w_out = _mm_tn("ssm_out_bwd_w", yn, dyb, gdt)
    late = [("ffn2_w_out", g_w2_out.reshape(N_DEV, FF_SHARD // 2, D_MODEL)),
            ("ffn2_w_in", g_w2_in.reshape(N_DEV, FF_SHARD, D_MODEL)),
            ("w_out", g_w_out.reshape(N_DEV, -1, D_MODEL)), ("short_w_out", g_short_w_out.reshape(N_DEV, -1, D_MODEL)),
            ("ssm_w_out", g_ssm_w_out.reshape(N_DEV, -1, D_MODEL))]
    (dy_ssm, dp, small["ssm_norm"]), sums = _reduce_start(
        late, lambda comm: _gnorm_bwd(y_ssm, p, rep["ssm_norm"], dyn, comm=comm))
    dp, g_short_conv = _mix_a_bwd(p, short_conv_w, dya_in, dp)
    first = [(n, a) for n, a in sums if n.startswith("ffn2")]
    second = [(n, a) for n, a in sums if not n.startswith("ffn2")]
    (dxconv, ddt, dacs, dd_lane), got = _ssd_bwd(
        xconv, dt, acs, rep["d_exp"], hsave, dy_ssm,
        comm=_chip_comm([a for _, a in first], [n == "ffn2_w_in" for n, _ in first]))
    recv.update({n: a for (n, _), a in zip(first, got)})
    small["ssm_D"] = dd_lane.reshape(N_HEADS, HEAD_DIM).sum(axis=1)[None, :]
    (dp, g_ssm_conv, small["ssm_conv_b"]), got = _ssm_conv_bwd(
        p, ssm_conv_w, rep["ssm_conv_b"], dxconv, dp, comm=_chip_comm([a for _, a in second]))
    recv.update({n: a for (n, _), a in zip(second, got)})
    dp_gd, dbias, dalog = _dt_bwd(p_gd, rep["dt_bias_pad"], rep["a_log_pad"], dt, ddt, dacs, dp_gd)
    small["ssm_dt_bias"] = dbias[:, :N_HEADS]
    small["ssm_A_log"] = dalog[:, :N_HEADS]

    g_main = _mm_tn("proj_main_bwd_w", dp, h2, gdt, tm=1024)
    g_gd = _mm_tn("proj_gd_bwd_w", dp_gd, h2, gdt)
    g_mix = g_main[0:3 * D_MODEL].reshape(4, 3, CONV_CB, D_MODEL).transpose(1, 0, 2, 3).reshape(3 * D_MODEL, D_MODEL)
    g_in_t = jnp.concatenate(
        [g_mix, g_main[3 * D_MODEL:], g_gd[2 * D_MODEL:2 * D_MODEL + N_HEADS],
         g_gd[0:half], g_gd[2 * half:3 * half], g_gd[half:2 * half], g_gd[3 * half:4 * half]], axis=0).reshape(
        N_DEV, IN_SHARD, D_MODEL)
    dh2, w_sums = _reduce_start(
        [("w_in", g_in_t)], lambda comm: _mm_nn("proj_mix_bwd_x", dp, w_mix_perm, tk=1024, kk=3 * D_MODEL, comm=comm))
    w_sum = w_sums[0][1]

    def w_piece(i):
        return _chip_comm([w_sum], rows=[W_GRAD_ROW_CUTS[i]])

    dh2, got0 = _mm_nn("proj_rest_bwd_x", dp, w_in_t, tk=1024, kk=N_MAIN - 3 * D_MODEL, a_off=3, b_off=3, res=dh2,
                       comm=w_piece(0))
    dh2, got1 = _mm_nn("proj_gd_bwd_x", dp_gd, w_gd, res=dh2, comm=w_piece(1))
    (dx1, dx1h, small["mix_norm"]), got2 = _rms_bwd("rms2_bwd", x1, rep["mix_norm"], dh2, dx2, 0.5, comm=w_piece(2))
    g_w1_out, got3 = _mm_tn("ffn1_out_bwd_w", act1, dx1h, gdt, tm=FF_HALF, comm=w_piece(3))
    rest = [("ffn1_w_out", g_w1_out.reshape(N_DEV, FF_SHARD // 2, D_MODEL)),
            ("short_conv_w", g_short_conv.reshape(3, N_DEV, -1).transpose(1, 0, 2)),
            ("ssm_conv_w", g_ssm_conv.reshape(4, N_DEV, -1).transpose(1, 0, 2))]
    dact1, got = _mm_nt("ffn1_out_bwd_act", dx1h, w1_out, out_dtype=BF,
                        comm=_join_comm(w_piece(4), _pair_comm([a for _, a in rest])))
    got4, sib = got[0], got[1:]
    rest_sums = [(n, _add_pairs("pairsum_" + n, a, b)) for (n, a), b in zip(rest, sib)]
    w1_out_sum = rest_sums[0][1]
    half_rows = FF_SHARD // 4
    dgu1, got = _swiglu_bwd("swiglu1_bwd", gu1, dact1, comm=_chip_comm(
        [a for _, a in rest_sums], rows=[(0, half_rows), None, None]))
    recv_w1_out_a = got[0]
    recv.update({n: a for (n, _), a in zip(rest_sums[1:], got[1:])})

    def part(tag, width, off, comm=None):
        out = _mm_tn("ffn1_in_bwd_w_" + tag, dgu1, h1, gdt, tm=FF_HALF, n=width, col_off=off, comm=comm)
        g, couts = (out, None) if comm is None else out
        return g.reshape(N_DEV, FF_SHARD, width), couts

    g_a, (got5,) = part("a", 384, 0, w_piece(5))
    g_b, (got6, sib) = part("b", 384, 1, _join_comm(w_piece(6), _pair_comm([g_a])))
    recv["w_in"] = jnp.concatenate([got0[0], got1[0], got2[0], got3[0], got4, got5, got6], axis=1)
    sum_a = _add_pairs("pairsum_ffn1_w_in_a", g_a, sib)
    g_c, (recv_a, sib_b) = part("c", 256, 3, _join_comm(_chip_comm([sum_a], [True]), _pair_comm([g_b])))
    sum_b = _add_pairs("pairsum_ffn1_w_in_b", g_b, sib_b)
    dh1, (recv_b, recv_w1_out_b, sib_c) = _mm_nn(
        "ffn1_in_bwd_h", dgu1, w1_in, tk=FF_HALF,
        comm=_join_comm(_chip_comm([sum_b, w1_out_sum], [True, False], rows=[None, (half_rows, 2 * half_rows)]),
                        _pair_comm([g_c])))
    recv["ffn1_w_out"] = jnp.concatenate([recv_w1_out_a, recv_w1_out_b], axis=1)
    sum_c = _add_pairs("pairsum_ffn1_w_in_c", g_c, sib_c)
    (dx0, _, small["ffn1_norm"]), (recv_c,) = _rms_bwd("rms1_bwd", x, rep["ffn1_norm"], dh1, dx1, 1.0,
                                                        comm=_chip_comm([sum_c], [True]))
    recv["ffn1_w_in"] = jnp.concatenate([recv_a, recv_b, recv_c], axis=2)
    return dx0, recv, _pack_small(small, loss[:, 0:1])


_SMALL = [("ffn1_norm", 1024), ("mix_norm", 1024), ("ssm_conv_b", 4096), ("ssm_dt_bias", 32), ("ssm_A_log", 32),
          ("ssm_D", 32), ("ssm_norm", 2048), ("ffn2_norm", 1024), ("final_norm", 1024)]
SMALL_W = 10368


def _pack_small(d, loss=None):
    parts = [d[n].reshape(1, -1).astype(F32) for n, _ in _SMALL]
    used = sum(sz for _, sz in _SMALL)
    tail = jnp.zeros((1, SMALL_W - used), F32)
    if loss is not None:
        tail = tail.at[:, 0:1].set(loss)
    return jnp.concatenate(parts + [tail], axis=1)


def _adamw_small(parts, w, m, v):
    n_par = len(_SMALL)
    bc1 = 1.0 - ADAM_B1 ** ADAM_STEP
    bc2 = 1.0 - ADAM_B2 ** ADAM_STEP
    used = sum(sz for _, sz in _SMALL)

    def body(*refs):
        p_ref = refs[0]
        ins = refs[1:1 + 3 * n_par]
        outs = refs[1 + 3 * n_par:]
        g_all = p_ref[0]
        for k in range(1, N_DEV):
            g_all = g_all + p_ref[k]
        off = 0
        for i, (_, sz) in enumerate(_SMALL):
            g = g_all[:, off:off + sz]
            w_ref, m_ref, v_ref = ins[3 * i:3 * i + 3]
            nm = ADAM_B1 * m_ref[...] + (1.0 - ADAM_B1) * g
            nv = ADAM_B2 * v_ref[...] + (1.0 - ADAM_B2) * (g * g)
            outs[4 * i][...] = g
            outs[4 * i + 1][...] = -ADAM_LR * ((nm / bc1) / (jnp.sqrt(nv / bc2) + ADAM_EPS) + ADAM_WD * w_ref[...])
            outs[4 * i + 2][...] = nm
            outs[4 * i + 3][...] = nv
            off += sz
        outs[4 * n_par][...] = g_all[:, used:SMALL_W]

    args = [parts]
    out_shape = []
    for name, sz in _SMALL:
        args += [w[name], m[name], v[name]]
        out_shape += [jax.ShapeDtypeStruct((1, sz), F32)] * 4
    out_shape.append(jax.ShapeDtypeStruct((1, SMALL_W - used), F32))
    res = pl.pallas_call(body, name="adamw_small", out_shape=out_shape,
                         compiler_params=pltpu.CompilerParams(vmem_limit_bytes=VMEM_LIMIT_V7X))(*args)
    return {name: tuple(res[4 * i:4 * i + 4]) for i, (name, _) in enumerate(_SMALL)}, res[-1]


_SHARDED = ["ffn1_w_in", "ffn1_w_out", "w_in", "short_conv_w", "short_w_out", "ssm_conv_w", "ssm_w_out", "w_out",
            "ffn2_w_in", "ffn2_w_out"]
_TRANSPOSED = ("ffn1_w_in", "w_in", "ffn2_w_in")
_ORDER = ["ffn1_norm", "ffn1_w_in", "ffn1_w_out", "mix_norm", "w_in", "short_conv_w", "short_w_out", "ssm_conv_w",
          "ssm_conv_b", "ssm_dt_bias", "ssm_A_log", "ssm_D", "ssm_norm", "ssm_w_out", "w_out", "ffn2_norm",
          "ffn2_w_in", "ffn2_w_out", "final_norm"]


def kernel(x, ffn1_norm, ffn1_w_in, ffn1_w_out, mix_norm, w_in, short_conv_w, short_w_out, ssm_conv_w, ssm_conv_b, ssm_dt_bias, ssm_A_log, ssm_D, ssm_norm, ssm_w_out, w_out, ffn2_norm, ffn2_w_in, ffn2_w_out, final_norm, loss_target, m_ffn1_norm, m_ffn1_w_in, m_ffn1_w_out, m_mix_norm, m_w_in, m_short_conv_w, m_short_w_out, m_ssm_conv_w, m_ssm_conv_b, m_ssm_dt_bias, m_ssm_A_log, m_ssm_D, m_ssm_norm, m_ssm_w_out, m_w_out, m_ffn2_norm, m_ffn2_w_in, m_ffn2_w_out, m_final_norm, v_ffn1_norm, v_ffn1_w_in, v_ffn1_w_out, v_mix_norm, v_w_in, v_short_conv_w, v_short_w_out, v_ssm_conv_w, v_ssm_conv_b, v_ssm_dt_bias, v_ssm_A_log, v_ssm_D, v_ssm_norm, v_ssm_w_out, v_w_out, v_ffn2_norm, v_ffn2_w_in, v_ffn2_w_out, v_final_norm):
    w = dict(ffn1_norm=ffn1_norm, ffn1_w_in=ffn1_w_in, ffn1_w_out=ffn1_w_out, mix_norm=mix_norm, w_in=w_in,
             short_conv_w=short_conv_w, short_w_out=short_w_out, ssm_conv_w=ssm_conv_w, ssm_conv_b=ssm_conv_b,
             ssm_dt_bias=ssm_dt_bias, ssm_A_log=ssm_A_log, ssm_D=ssm_D, ssm_norm=ssm_norm, ssm_w_out=ssm_w_out,
             w_out=w_out, ffn2_norm=ffn2_norm, ffn2_w_in=ffn2_w_in, ffn2_w_out=ffn2_w_out, final_norm=final_norm)
    m = dict(ffn1_norm=m_ffn1_norm, ffn1_w_in=m_ffn1_w_in, ffn1_w_out=m_ffn1_w_out, mix_norm=m_mix_norm, w_in=m_w_in,
             short_conv_w=m_short_conv_w, short_w_out=m_short_w_out, ssm_conv_w=m_ssm_conv_w,
             ssm_conv_b=m_ssm_conv_b, ssm_dt_bias=m_ssm_dt_bias, ssm_A_log=m_ssm_A_log, ssm_D=m_ssm_D,
             ssm_norm=m_ssm_norm, ssm_w_out=m_ssm_w_out, w_out=m_w_out, ffn2_norm=m_ffn2_norm,
             ffn2_w_in=m_ffn2_w_in, ffn2_w_out=m_ffn2_w_out, final_norm=m_final_norm)
    v = dict(ffn1_norm=v_ffn1_norm, ffn1_w_in=v_ffn1_w_in, ffn1_w_out=v_ffn1_w_out, mix_norm=v_mix_norm, w_in=v_w_in,
             short_conv_w=v_short_conv_w, short_w_out=v_short_w_out, ssm_conv_w=v_ssm_conv_w,
             ssm_conv_b=v_ssm_conv_b, ssm_dt_bias=v_ssm_dt_bias, ssm_A_log=v_ssm_A_log, ssm_D=v_ssm_D,
             ssm_norm=v_ssm_norm, ssm_w_out=v_ssm_w_out, w_out=v_w_out, ffn2_norm=v_ffn2_norm,
             ffn2_w_in=v_ffn2_w_in, ffn2_w_out=v_ffn2_w_out, final_norm=v_final_norm)
    shapes = {n: w[n].shape for n in _ORDER}

    def local(d, n):
        return d[n][0].T if n in _TRANSPOSED else d[n][0]

    shard = {n: local(w, n) for n in _SHARDED}

    wire = {n: (shard[n] if n in ("short_conv_w", "ssm_conv_w") else shard[n].astype(BF)) for n in _SHARDED}
    rep = {
        "ffn1_norm": ffn1_norm, "mix_norm": mix_norm, "ffn2_norm": ffn2_norm, "ssm_norm": ssm_norm,
        "ssm_conv_b": ssm_conv_b, "final_norm": final_norm.reshape(1, D_MODEL),
        "dt_bias_pad": _pad_lanes(ssm_dt_bias, DT_W), "a_log_pad": _pad_lanes(ssm_A_log, DT_W),
        "d_exp": jnp.repeat(ssm_D, HEAD_DIM, axis=1),
    }
    grad_x, parts, packed = _train_step(x[0], loss_target[0], wire, rep)

    out_g, out_d, out_m, out_v = {}, {}, {}, {}
    for n in _SHARDED:
        if n == "ssm_w_out":
            res, (small_parts,) = _adamw("adamw_" + n, parts[n], shard[n], local(m, n), local(v, n),
                                         comm=_gather_comm([packed]))
        else:
            res = _adamw("adamw_" + n, parts[n], shard[n], local(m, n), local(v, n))
        out_g[n], out_d[n], out_m[n], out_v[n] = [(r.T if n in _TRANSPOSED else r).reshape(shapes[n]) for r in res]
    row = lambda d: {n: d[n].reshape(1, -1) for n, _ in _SMALL}
    sres, loss_row = _adamw_small(small_parts, row(w), row(m), row(v))
    for n, _ in _SMALL:
        out_g[n], out_d[n], out_m[n], out_v[n] = [r.reshape(shapes[n]) for r in sres[n]]
    loss = loss_row[0, 0]
    return (loss, grad_x[None], *[out_g[n] for n in _ORDER], *[out_d[n] for n in _ORDER],
            *[out_m[n] for n in _ORDER], *[out_v[n] for n in _ORDER])
```

```python
import functools

import jax
import jax.numpy as jnp
from jax import lax
from jax.experimental import pallas as pl
from jax.experimental.pallas import tpu as pltpu

F32 = jnp.float32
BF = jnp.bfloat16

N_DEV = 8
D_MODEL = 1024
D_FF = 2816
D_INNER = 2048
D_XBC = 4096
N_HEADS = 32
HEAD_DIM = 64
N_GROUPS = 8
D_STATE = 128
CHUNK = 64
GROUP_W = D_INNER // N_GROUPS
NORM_EPS = 1e-5
N_IN = 11296
FF_SHARD = 2 * D_FF // N_DEV
FF_HALF = D_FF // 2
IN_SHARD = N_IN // N_DEV

OFF_B, OFF_C, OFF_XA, OFF_Z, OFF_XBC = 0, 1024, 2048, 3072, 5120
N_MAIN = 9216
OFF_DT = 2048
DT_W = 128
N_GD = 2048 + DT_W
W_GRAD_ROW_CUTS = [(0, 400), (400, 568), (568, 704), (704, 880), (880, 1040), (1040, 1240), (1240, 1412)]

ADAM_LR, ADAM_B1, ADAM_B2, ADAM_EPS, ADAM_WD, ADAM_STEP = 0.001, 0.9, 0.999, 1e-08, 0.01, 10

VMEM_LIMIT_V7X = 56 * 1024 * 1024
TM = 1024
TN_MAX_TOKENS = 2048
TE = 512
ADAM_COL_TILE = 256
GATHER_PIECES = 8
GATHER_PIECE_MIN_ROWS = 512


def _params(*sem):
    return pltpu.CompilerParams(dimension_semantics=sem, vmem_limit_bytes=VMEM_LIMIT_V7X)


_DIMS = {
    "nn": (((1,), (0,)), ((), ())),
    "nt": (((1,), (1,)), ((), ())),
    "tn": (((0,), (0,)), ((), ())),
}


def _dot(a, b, mode="nn"):
    return lax.dot_general(a, b, _DIMS[mode], preferred_element_type=F32)


def _sigmoid(x):
    return 1.0 / (1.0 + jnp.exp(-x))


class _Comm:
    def __init__(self, inputs, out_shapes, sems, start, finish):
        self.inputs, self.out_shapes, self.sems, self.start, self.finish = inputs, out_shapes, sems, start, finish


def _pcall(name, body, grid, in_specs, out_specs, out_shape, args, scratch=(), sem=None, comm=None, aliases=None):
    single = not isinstance(out_shape, (list, tuple))
    out_shapes = [out_shape] if single else list(out_shape)
    out_specs = [out_specs] if single else list(out_specs)
    n_in, n_out, n_scr = len(args), len(out_shapes), len(scratch)
    aliases = {} if aliases is None else aliases
    if comm is None:
        res = pl.pallas_call(
            body, name=name, grid=grid, in_specs=list(in_specs), out_specs=out_specs, out_shape=out_shapes,
            scratch_shapes=list(scratch), input_output_aliases=aliases, compiler_params=_params(*sem))(*args)
        return (res[0] if single else res), []
    nci, nco = len(comm.inputs), len(comm.out_shapes)

    def wrapped(*refs):
        a = refs[:n_in]
        ci = refs[n_in:n_in + nci]
        o0 = n_in + nci
        o = refs[o0:o0 + n_out]
        co = refs[o0 + n_out:o0 + n_out + nco]
        s0 = o0 + n_out + nco
        s = refs[s0:s0 + n_scr]
        cs = refs[s0 + n_scr:]
        pids = [pl.program_id(i) for i in range(len(grid))]
        first = functools.reduce(jnp.logical_and, [p == 0 for p in pids])
        last = functools.reduce(jnp.logical_and, [p == g - 1 for p, g in zip(pids, grid)])

        @pl.when(first)
        def _():
            comm.start(ci, co, cs)

        body(*a, *o, *s)

        @pl.when(last)
        def _():
            comm.finish(ci, co, cs)

    any_spec = pl.BlockSpec(memory_space=pl.ANY)
    res = pl.pallas_call(
        wrapped, name=name, grid=grid, in_specs=list(in_specs) + [any_spec] * nci,
        out_specs=out_specs + [any_spec] * nco, out_shape=out_shapes + list(comm.out_shapes),
        scratch_shapes=list(scratch) + list(comm.sems), input_output_aliases=aliases,
        compiler_params=_params(*(("arbitrary",) * len(grid))))(*args, *comm.inputs)
    core = res[:n_out]
    return (core[0] if single else core), list(res[n_out:])


def _comm_call(name, comm):
    nci, nco = len(comm.inputs), len(comm.out_shapes)

    def body(*refs):
        ci, co, cs = refs[:nci], refs[nci:nci + nco], refs[nci + nco:]
        comm.start(ci, co, cs)
        comm.finish(ci, co, cs)

    any_spec = pl.BlockSpec(memory_space=pl.ANY)
    return pl.pallas_call(
        body, name=name, in_specs=[any_spec] * nci, out_specs=[any_spec] * nco, out_shape=list(comm.out_shapes),
        scratch_shapes=list(comm.sems), compiler_params=pltpu.CompilerParams(has_side_effects=True))(*comm.inputs)


def _remote(src, dst, ssem, rsem, dev):
    return pltpu.make_async_remote_copy(src_ref=src, dst_ref=dst, send_sem=ssem, recv_sem=rsem, device_id=dev,
                                        device_id_type=pl.DeviceIdType.MESH)


def _place():
    x, y, c = lax.axis_index("x"), lax.axis_index("y"), lax.axis_index("c")
    other_chips = [(1 - x, y), (x, 1 - y), (1 - x, 1 - y)]
    return x, y, c, other_chips


def _slot(x, y, c, swap):
    return 4 * y + 2 * x + c if swap else 4 * x + 2 * y + c


def _chip_slot(x, y, swap):
    return 2 * y + x if swap else 2 * x + y


def _gather_comm(shards, swaps=None):
    n = len(shards)
    per = N_DEV - 1
    swaps = [False] * n if swaps is None else swaps
    pieces = []
    for i, a in enumerate(shards):
        rows = a.shape[0]
        k = GATHER_PIECES if (a.ndim == 2 and rows >= GATHER_PIECE_MIN_ROWS) else 1
        step = -(-rows // (k * 8)) * 8
        if k == 1:
            pieces.append((i, 0, None))
        else:
            pieces += [(i, r, min(step, rows - r)) for r in range(0, rows, step)]
    m = len(pieces)

    def src(ins, v):
        i, r, cnt = pieces[v]
        return ins[i] if cnt is None else ins[i].at[pl.ds(r, cnt)]

    def place(outs, v, x, y, c):
        i, r, cnt = pieces[v]
        blk = outs[i].at[_slot(x, y, c, swaps[i])]
        return blk if cnt is None else blk.at[pl.ds(r, cnt)]

    def start(ins, outs, sems):
        send, recv, loc = sems
        x, y, c, chips = _place()
        for v in range(m):
            me = place(outs, v, x, y, c)
            pltpu.make_async_copy(src(ins, v), me, loc.at[v]).start()
            _remote(src(ins, v), me, send.at[per * v], recv.at[per * v], (x, y, 1 - c)).start()
        for j, (qx, qy) in enumerate(chips):
            for v in range(m):
                _remote(src(ins, v), place(outs, v, x, y, c), send.at[per * v + 1 + j], recv.at[per * v + 1 + j],
                        (qx, qy, c)).start()

    def finish(ins, outs, sems):
        send, recv, loc = sems
        x, y, c, chips = _place()
        sib = (x, y, 1 - c)
        for v in range(m):
            for j, (qx, qy) in enumerate(chips):
                blk = place(outs, v, qx, qy, c)
                _remote(blk, blk, send.at[per * v + 1 + j], recv.at[per * v + 1 + j], (qx, qy, c)).wait_recv()
                _remote(blk, blk, send.at[per * v + 4 + j], recv.at[per * v + 4 + j], sib).start()
        for v in range(m):
            blk = place(outs, v, x, y, 1 - c)
            _remote(blk, blk, send.at[per * v], recv.at[per * v], sib).wait_recv()
            for j, (qx, qy) in enumerate(chips):
                blk = place(outs, v, qx, qy, 1 - c)
                _remote(blk, blk, send.at[per * v + 4 + j], recv.at[per * v + 4 + j], sib).wait_recv()
        for v in range(m):
            own = place(outs, v, x, y, c)
            for k in range(per):
                _remote(src(ins, v), own, send.at[per * v + k], recv.at[per * v + k], sib).wait_send()
            pltpu.make_async_copy(src(ins, v), own, loc.at[v]).wait()

    out_shapes = [jax.ShapeDtypeStruct((N_DEV,) + tuple(a.shape), a.dtype) for a in shards]
    sems = [pltpu.SemaphoreType.DMA((per * m,)), pltpu.SemaphoreType.DMA((per * m,)), pltpu.SemaphoreType.DMA((m,))]
    return _Comm(list(shards), out_shapes, sems, start, finish)


def _pair_comm(slots):
    n = len(slots)

    def copies(ins, outs, sems):
        send, recv = sems
        x, y, c, _ = _place()
        sib = (x, y, 1 - c)
        out = []
        for i in range(n):
            for q in range(4):
                out.append(_remote(ins[i].at[2 * q + 1 - c], outs[i].at[q], send.at[4 * i + q], recv.at[4 * i + q], sib))
        return out

    def start(ins, outs, sems):
        for cp in copies(ins, outs, sems):
            cp.start()

    def finish(ins, outs, sems):
        for cp in copies(ins, outs, sems):
            cp.wait_send()
            cp.wait_recv()

    out_shapes = [jax.ShapeDtypeStruct((4,) + tuple(a.shape[1:]), a.dtype) for a in slots]
    sems = [pltpu.SemaphoreType.DMA((4 * n,)), pltpu.SemaphoreType.DMA((4 * n,))]
    return _Comm(list(slots), out_shapes, sems, start, finish)


def _chip_comm(chip_sums, swaps=None, rows=None):
    n = len(chip_sums)
    swaps = [False] * n if swaps is None else swaps
    rows = [None] * n if rows is None else rows

    def src(ins, i, q):
        return ins[i].at[q] if rows[i] is None else ins[i].at[q, pl.ds(rows[i][0], rows[i][1] - rows[i][0])]

    def start(ins, outs, sems):
        send, recv, loc = sems
        x, y, c, chips = _place()
        for i in range(n):
            mine = _chip_slot(x, y, swaps[i])
            pltpu.make_async_copy(src(ins, i, mine), outs[i].at[mine], loc.at[i]).start()
            for j, (qx, qy) in enumerate(chips):
                _remote(src(ins, i, _chip_slot(qx, qy, swaps[i])), outs[i].at[mine], send.at[3 * i + j],
                        recv.at[3 * i + j], (qx, qy, c)).start()

    def finish(ins, outs, sems):
        send, recv, loc = sems
        x, y, c, chips = _place()
        for i in range(n):
            mine = _chip_slot(x, y, swaps[i])
            for j, (qx, qy) in enumerate(chips):
                theirs = _chip_slot(qx, qy, swaps[i])
                cp = _remote(src(ins, i, theirs), outs[i].at[theirs], send.at[3 * i + j], recv.at[3 * i + j], (qx, qy, c))
                cp.wait_send()
                cp.wait_recv()
            pltpu.make_async_copy(src(ins, i, mine), outs[i].at[mine], loc.at[i]).wait()

    def out_shape(a, r):
        shape = a.shape if r is None else (a.shape[0], r[1] - r[0]) + tuple(a.shape[2:])
        return jax.ShapeDtypeStruct(shape, a.dtype)

    out_shapes = [out_shape(a, r) for a, r in zip(chip_sums, rows)]
    sems = [pltpu.SemaphoreType.DMA((3 * n,)), pltpu.SemaphoreType.DMA((3 * n,)), pltpu.SemaphoreType.DMA((n,))]
    return _Comm(list(chip_sums), out_shapes, sems, start, finish)


def _join_comm(a, b):
    na_i, na_o, na_s = len(a.inputs), len(a.out_shapes), len(a.sems)

    def start(ins, outs, sems):
        a.start(ins[:na_i], outs[:na_o], sems[:na_s])
        b.start(ins[na_i:], outs[na_o:], sems[na_s:])

    def finish(ins, outs, sems):
        a.finish(ins[:na_i], outs[:na_o], sems[:na_s])
        b.finish(ins[na_i:], outs[na_o:], sems[na_s:])

    return _Comm(a.inputs + b.inputs, a.out_shapes + b.out_shapes, a.sems + b.sems, start, finish)


def _row_tile(r):
    for cand in (256, 128):
        if r > cand and r % cand == 0:
            return cand
    return r


def _add_pairs(name, slots, sib):
    r, c = slots.shape[1:]
    tr = _row_tile(r)

    def body(core_ref, s_ref, b_ref, o_ref):
        o_ref[...] = (s_ref[...].astype(F32) + b_ref[...].astype(F32)).astype(o_ref.dtype)

    core = jnp.full((1,), lax.axis_index("c"), jnp.int32)
    return pl.pallas_call(
        body, name=name,
        grid_spec=pltpu.PrefetchScalarGridSpec(
            num_scalar_prefetch=1, grid=(4, r // tr),
            in_specs=[pl.BlockSpec((None, tr, c), lambda q, i, core_ref: (2 * q + core_ref[0], i, 0)),
                      pl.BlockSpec((None, tr, c), lambda q, i, core_ref: (q, i, 0))],
            out_specs=pl.BlockSpec((None, tr, c), lambda q, i, core_ref: (q, i, 0))),
        out_shape=jax.ShapeDtypeStruct((4, r, c), slots.dtype),
        compiler_params=_params("parallel", "parallel"))(core, slots, sib)


def _matmul(name, mode, a, b, grid, a_spec, b_spec, o_spec, out_shape, acc_shape,
            res=None, res_spec=None, alpha=1.0, comm=None):
    nk = grid[-1]
    has_res = res is not None

    def body(*refs):
        if has_res:
            a_ref, b_ref, r_ref, o_ref = refs[:4]
        else:
            a_ref, b_ref, o_ref = refs[:3]
            r_ref = None
        part = _dot(a_ref[...], b_ref[...], mode)

        def finish(v):
            if alpha != 1.0:
                v = v * alpha
            if has_res:
                v = r_ref[...] + v
            o_ref[...] = v.astype(o_ref.dtype)

        if nk == 1:
            finish(part)
        else:
            acc = refs[-1]
            k = pl.program_id(len(grid) - 1)

            @pl.when(k == 0)
            def _():
                acc[...] = part

            @pl.when(k > 0)
            def _():
                acc[...] += part

            @pl.when(k == nk - 1)
            def _():
                finish(acc[...])

    in_specs = [a_spec, b_spec] + ([res_spec] if has_res else [])
    args = (a, b) + ((res,) if has_res else ())
    scratch = [] if nk == 1 else [pltpu.VMEM(acc_shape, F32)]
    sem = ("parallel",) * (len(grid) - 1) + ("arbitrary",)
    out, couts = _pcall(name, body, grid, in_specs, o_spec, out_shape, args, scratch, sem, comm)
    return out if comm is None else (out, couts)


def _mm_nn(name, a, b, out_dtype=F32, res=None, alpha=1.0, tk=None, kk=None, a_off=0, b_off=0, comm=None):
    t = a.shape[0]
    kk = a.shape[1] if kk is None else kk
    n = b.shape[1]
    tk = kk if tk is None else tk
    grid = (t // TM, 1, kk // tk)
    return _matmul(
        name, "nn", a, b, grid,
        pl.BlockSpec((TM, tk), lambda i, j, k: (i, k + a_off)),
        pl.BlockSpec((tk, n), lambda i, j, k: (k + b_off, 0)),
        pl.BlockSpec((TM, n), lambda i, j, k: (i, 0)),
        jax.ShapeDtypeStruct((t, n), out_dtype), (TM, n),
        res=res, res_spec=pl.BlockSpec((TM, n), lambda i, j, k: (i, 0)), alpha=alpha, comm=comm)


def _mm_nt(name, a, b, n=None, tn=None, tk=None, out_dtype=F32, comm=None):
    t, kk = a.shape
    n = b.shape[0] if n is None else n
    tn = n if tn is None else tn
    tk = kk if tk is None else tk
    grid = (n // tn, t // TM, kk // tk)
    return _matmul(
        name, "nt", a, b, grid,
        pl.BlockSpec((TM, tk), lambda j, i, k: (i, k)),
        pl.BlockSpec((tn, tk), lambda j, i, k: (j, k)),
        pl.BlockSpec((TM, tn), lambda j, i, k: (i, j)),
        jax.ShapeDtypeStruct((t, n), out_dtype), (TM, tn), comm=comm)


def _mm_tn(name, a, b, out_dtype, tm=None, n=None, col_off=0, comm=None):
    t, m = a.shape
    n = b.shape[1] if n is None else n
    tm = m if tm is None else tm
    tk = t if t <= TN_MAX_TOKENS else TM
    grid = (m // tm, 1, t // tk)
    return _matmul(
        name, "tn", a, b, grid,
        pl.BlockSpec((tk, tm), lambda j, i, k: (k, j)),
        pl.BlockSpec((tk, n), lambda j, i, k: (k, col_off)),
        pl.BlockSpec((tm, n), lambda j, i, k: (j, 0)),
        jax.ShapeDtypeStruct((m, n), out_dtype), (tm, n), comm=comm)


def _rms_fwd(name, x, w, comm=None):
    t, d = x.shape

    def body(x_ref, w_ref, h_ref):
        xv = x_ref[...]
        rstd = lax.rsqrt(jnp.mean(xv * xv, axis=-1, keepdims=True) + NORM_EPS)
        h_ref[...] = (xv * rstd * w_ref[...]).astype(h_ref.dtype)

    out, couts = _pcall(
        name, body, (t // TE,),
        [pl.BlockSpec((TE, d), lambda i: (i, 0)), pl.BlockSpec((1, d), lambda i: (0, 0))],
        pl.BlockSpec((TE, d), lambda i: (i, 0)), jax.ShapeDtypeStruct((t, d), BF), (x, w), (), ("parallel",), comm)
    return out if comm is None else (out, couts)


def _rms_bwd(name, x, w, dh, dres, out_scale, comm=None):
    t, d = x.shape

    def body(x_ref, w_ref, dh_ref, dres_ref, dx_ref, dxb_ref, dw_ref):
        i = pl.program_id(0)
        xv = x_ref[...]
        rstd = lax.rsqrt(jnp.mean(xv * xv, axis=-1, keepdims=True) + NORM_EPS)
        xhat = xv * rstd
        dhv = dh_ref[...]
        wd = dhv * w_ref[...]
        proj = jnp.mean(wd * xhat, axis=-1, keepdims=True)
        dx = dres_ref[...] + rstd * (wd - xhat * proj)
        dx_ref[...] = dx
        dxb_ref[...] = (dx * out_scale).astype(BF)
        part = jnp.sum(dhv * xhat, axis=0, keepdims=True)

        @pl.when(i == 0)
        def _():
            dw_ref[...] = part

        @pl.when(i > 0)
        def _():
            dw_ref[...] += part

    row = pl.BlockSpec((TE, d), lambda i: (i, 0))
    vec = pl.BlockSpec((1, d), lambda i: (0, 0))
    outs, couts = _pcall(
        name, body, (t // TE,), [row, vec, row, row], [row, row, vec],
        [jax.ShapeDtypeStruct((t, d), F32), jax.ShapeDtypeStruct((t, d), BF), jax.ShapeDtypeStruct((1, d), F32)],
        (x, w, dh, dres), (), ("arbitrary",), comm)
    return outs if comm is None else (outs, couts)


def _final_loss(x, w, target):
    t, d = x.shape

    def body(x_ref, w_ref, t_ref, loss_ref, dx_ref, dxb_ref, dw_ref):
        i = pl.program_id(0)
        xv = x_ref[...]
        rstd = lax.rsqrt(jnp.mean(xv * xv, axis=-1, keepdims=True) + NORM_EPS)
        xhat = xv * rstd
        err = xhat * w_ref[...] - t_ref[...]
        lpart = 0.5 * jnp.sum(jnp.mean(err * err, axis=-1, keepdims=True), axis=0, keepdims=True)
        dy = err * (1.0 / d)
        wd = dy * w_ref[...]
        proj = jnp.mean(wd * xhat, axis=-1, keepdims=True)
        dx = rstd * (wd - xhat * proj)
        dx_ref[...] = dx
        dxb_ref[...] = (0.5 * dx).astype(BF)
        part = jnp.sum(dy * xhat, axis=0, keepdims=True)
        lfull = jnp.broadcast_to(lpart, (1, 128))

        @pl.when(i == 0)
        def _():
            dw_ref[...] = part
            loss_ref[...] = lfull

        @pl.when(i > 0)
        def _():
            dw_ref[...] += part
            loss_ref[...] += lfull

    row = pl.BlockSpec((TE, d), lambda i: (i, 0))
    vec = pl.BlockSpec((1, d), lambda i: (0, 0))
    return pl.pallas_call(
        body, name="final_loss", grid=(t // TE,), in_specs=[row, vec, row],
        out_specs=[pl.BlockSpec((1, 128), lambda i: (0, 0)), row, row, vec],
        out_shape=[jax.ShapeDtypeStruct((1, 128), F32), jax.ShapeDtypeStruct((t, d), F32),
                   jax.ShapeDtypeStruct((t, d), BF), jax.ShapeDtypeStruct((1, d), F32)],
        compiler_params=_params("arbitrary"))(x, w, target)


def _swiglu_fwd(name, gu, comm=None):
    t = gu.shape[0]

    def body(g_ref, u_ref, a_ref):
        g = g_ref[...].astype(F32)
        a_ref[...] = (g * _sigmoid(g) * u_ref[...].astype(F32)).astype(BF)

    blk = (TE, FF_HALF)
    out, couts = _pcall(
        name, body, (t // TE, 2),
        [pl.BlockSpec(blk, lambda i, j: (i, 2 * j)), pl.BlockSpec(blk, lambda i, j: (i, 2 * j + 1))],
        pl.BlockSpec(blk, lambda i, j: (i, j)), jax.ShapeDtypeStruct((t, D_FF), BF),
        (gu, gu), (), ("parallel", "parallel"), comm)
    return out if comm is None else (out, couts)


def _swiglu_bwd(name, gu, dact, comm=None):
    t = gu.shape[0]

    def body(g_ref, u_ref, da_ref, o_ref):
        g = g_ref[...].astype(F32)
        da = da_ref[...].astype(F32)
        s = _sigmoid(g)
        o_ref[:, 0:FF_HALF] = (da * u_ref[...].astype(F32) * (s * (1.0 + g * (1.0 - s)))).astype(BF)
        o_ref[:, FF_HALF:2 * FF_HALF] = (da * g * s).astype(BF)

    blk = (TE, FF_HALF)
    out, couts = _pcall(
        name, body, (t // TE, 2),
        [pl.BlockSpec(blk, lambda i, j: (i, 2 * j)), pl.BlockSpec(blk, lambda i, j: (i, 2 * j + 1)),
         pl.BlockSpec(blk, lambda i, j: (i, j))],
        pl.BlockSpec((TE, 2 * FF_HALF), lambda i, j: (i, j)),
        jax.ShapeDtypeStruct((t, 2 * D_FF), BF), (gu, gu, dact), (), ("parallel", "parallel"), comm)
    return out if comm is None else (out, couts)


CONV_CB = 256


CONV_ROWS = 128
CONV_HALO = 16


def _taps_down(ext, w, k):
    shifted = [pltpu.roll(ext, k - 1 - j, 0)[CONV_HALO:] for j in range(k - 1)] + [ext[CONV_HALO:]]
    out = shifted[k - 1] * w[k - 1:k, :]
    for j in range(k - 1):
        out = out + shifted[j] * w[j:j + 1, :]
    return out, shifted


def _taps_up(ext, w, k):
    rows = ext.shape[0]
    n = rows - CONV_HALO
    out = ext[:n] * w[k - 1:k, :]
    for j in range(k - 1):
        out = out + pltpu.roll(ext, rows - (k - 1 - j), 0)[:n] * w[j:j + 1, :]
    return out


def _rows_before(ref, i, r0):
    start = pl.multiple_of(jnp.maximum(r0 - CONV_HALO, 0), CONV_HALO)
    return jnp.where(i > 0, ref[pl.ds(start, CONV_HALO), :].astype(F32), 0.0)


def _rows_after(ref, r0, t):
    start = pl.multiple_of(jnp.minimum(r0 + CONV_ROWS, t - CONV_HALO), CONV_HALO)
    return ref[pl.ds(start, CONV_HALO), :].astype(F32)


def _fold8(v):
    return v.reshape(v.shape[0] // 8, 8, v.shape[1]).sum(axis=0)


def _silu_grad(pre):
    s = _sigmoid(pre)
    return s * (1.0 + pre * (1.0 - s))


def _pspec(t, off):
    base = off // CONV_CB
    return pl.BlockSpec((t, CONV_CB), lambda j: (0, base + j))


def _mix_a_fwd(p, conv_w):
    t = p.shape[0]

    def body(b_ref, c_ref, xa_ref, w_ref, o_ref):
        w = w_ref[...]

        def step(i, carry):
            r0 = pl.multiple_of(i * CONV_ROWS, CONV_ROWS)
            rows = pl.ds(r0, CONV_ROWS)
            q = c_ref[rows, :].astype(F32) * xa_ref[rows, :].astype(F32)
            q_before = _rows_before(c_ref, i, r0) * _rows_before(xa_ref, i, r0)
            va, _ = _taps_down(jnp.concatenate([q_before, q], axis=0), w, 3)
            o_ref[rows, :] = (b_ref[rows, :].astype(F32) * va).astype(BF)
            return carry

        lax.fori_loop(0, t // CONV_ROWS, step, 0)

    return pl.pallas_call(
        body, name="mix_a_fwd", grid=(D_MODEL // CONV_CB,),
        in_specs=[_pspec(t, OFF_B), _pspec(t, OFF_C), _pspec(t, OFF_XA),
                  pl.BlockSpec((3, CONV_CB), lambda j: (0, j))],
        out_specs=pl.BlockSpec((t, CONV_CB), lambda j: (0, j)),
        out_shape=jax.ShapeDtypeStruct((t, D_MODEL), BF), compiler_params=_params("parallel"))(p, p, p, conv_w)


def _mix_a_bwd(p, conv_w, dya, dp):
    t = p.shape[0]

    def body(b_ref, c_ref, xa_ref, w_ref, dy_ref, dp_in, dp_ref, dw_ref):
        del dp_in
        w = w_ref[...]
        n = t // CONV_ROWS

        def step(i, acc):
            r0 = pl.multiple_of(i * CONV_ROWS, CONV_ROWS)
            rows = pl.ds(r0, CONV_ROWS)
            cv = c_ref[rows, :].astype(F32)
            xav = xa_ref[rows, :].astype(F32)
            q_before = _rows_before(c_ref, i, r0) * _rows_before(xa_ref, i, r0)
            va, shifted = _taps_down(jnp.concatenate([q_before, cv * xav], axis=0), w, 3)
            dyv = dy_ref[rows, :]
            dp_ref[rows, 0:CONV_CB] = (dyv * va).astype(BF)
            dv = dyv * b_ref[rows, :].astype(F32)
            dv_after = jnp.where(i < n - 1, _rows_after(dy_ref, r0, t) * _rows_after(b_ref, r0, t), 0.0)
            dq = _taps_up(jnp.concatenate([dv, dv_after], axis=0), w, 3)
            dp_ref[rows, CONV_CB:2 * CONV_CB] = (dq * xav).astype(BF)
            dp_ref[rows, 2 * CONV_CB:3 * CONV_CB] = (dq * cv).astype(BF)
            return tuple(a + _fold8(dv * s) for a, s in zip(acc, shifted))

        zero = jnp.zeros((8, CONV_CB), F32)
        acc = lax.fori_loop(0, n, step, (zero, zero, zero))
        for j in range(3):
            dw_ref[j:j + 1, :] = jnp.sum(acc[j], axis=0, keepdims=True)

    col = pl.BlockSpec((t, CONV_CB), lambda j: (0, j))
    wsp = pl.BlockSpec((3, CONV_CB), lambda j: (0, j))
    return pl.pallas_call(
        body, name="mix_a_bwd", grid=(D_MODEL // CONV_CB,),
        in_specs=[_pspec(t, OFF_B), _pspec(t, OFF_C), _pspec(t, OFF_XA), wsp, col, pl.BlockSpec(memory_space=pl.ANY)],
        out_specs=[pl.BlockSpec((t, 3 * CONV_CB), lambda j: (0, j)), wsp],
        out_shape=[jax.ShapeDtypeStruct(dp.shape, dp.dtype), jax.ShapeDtypeStruct((3, D_MODEL), F32)],
        input_output_aliases={5: 0},
        compiler_params=_params("parallel"))(p, p, p, conv_w, dya, dp)


def _ssm_conv_fwd(p, conv_w, conv_b, comm=None):
    t = p.shape[0]

    def body(x_ref, w_ref, b_ref, o_ref):
        w = w_ref[...]
        bias = b_ref[...]

        def step(i, carry):
            r0 = pl.multiple_of(i * CONV_ROWS, CONV_ROWS)
            rows = pl.ds(r0, CONV_ROWS)
            ext = jnp.concatenate([_rows_before(x_ref, i, r0), x_ref[rows, :].astype(F32)], axis=0)
            pre = _taps_down(ext, w, 4)[0] + bias
            o_ref[rows, :] = pre * _sigmoid(pre)
            return carry

        lax.fori_loop(0, t // CONV_ROWS, step, 0)

    out, couts = _pcall(
        "ssm_conv_fwd", body, (D_XBC // CONV_CB,),
        [_pspec(t, OFF_XBC), pl.BlockSpec((4, CONV_CB), lambda j: (0, j)), pl.BlockSpec((1, CONV_CB), lambda j: (0, j))],
        pl.BlockSpec((t, CONV_CB), lambda j: (0, j)), jax.ShapeDtypeStruct((t, D_XBC), F32),
        (p, conv_w, conv_b), (), ("parallel",), comm)
    return out if comm is None else (out, couts)


def _ssm_conv_bwd(p, conv_w, conv_b, dxc, dp, comm=None):
    t = p.shape[0]

    def body(x_ref, w_ref, b_ref, d_ref, dp_in, dx_ref, dw_ref, db_ref):
        del dp_in
        w = w_ref[...]
        bias = b_ref[...]
        n = t // CONV_ROWS

        def step(i, acc):
            r0 = pl.multiple_of(i * CONV_ROWS, CONV_ROWS)
            rows = pl.ds(r0, CONV_ROWS)
            x_cur = x_ref[rows, :].astype(F32)
            pre, shifted = _taps_down(jnp.concatenate([_rows_before(x_ref, i, r0), x_cur], axis=0), w, 4)
            pre = pre + bias
            dpre = d_ref[rows, :] * _silu_grad(pre)
            ext_after = jnp.concatenate([x_cur[CONV_ROWS - CONV_HALO:], _rows_after(x_ref, r0, t)], axis=0)
            pre_after = _taps_down(ext_after, w, 4)[0] + bias
            dpre_after = jnp.where(i < n - 1, _rows_after(d_ref, r0, t) * _silu_grad(pre_after), 0.0)
            dx_ref[rows, :] = _taps_up(jnp.concatenate([dpre, dpre_after], axis=0), w, 4).astype(BF)
            new = tuple(a + _fold8(dpre * s) for a, s in zip(acc[:4], shifted))
            return new + (acc[4] + _fold8(dpre),)

        zero = jnp.zeros((8, CONV_CB), F32)
        acc = lax.fori_loop(0, n, step, (zero,) * 5)
        for j in range(4):
            dw_ref[j:j + 1, :] = jnp.sum(acc[j], axis=0, keepdims=True)
        db_ref[...] = jnp.sum(acc[4], axis=0, keepdims=True)

    col = pl.BlockSpec((t, CONV_CB), lambda j: (0, j))
    wsp = pl.BlockSpec((4, CONV_CB), lambda j: (0, j))
    bsp = pl.BlockSpec((1, CONV_CB), lambda j: (0, j))
    outs, couts = _pcall(
        "ssm_conv_bwd", body, (D_XBC // CONV_CB,),
        [_pspec(t, OFF_XBC), wsp, bsp, col, pl.BlockSpec(memory_space=pl.ANY)], [_pspec(t, OFF_XBC), wsp, bsp],
        [jax.ShapeDtypeStruct(dp.shape, dp.dtype), jax.ShapeDtypeStruct((4, D_XBC), F32),
         jax.ShapeDtypeStruct((1, D_XBC), F32)],
        (p, conv_w, conv_b, dxc, dp), (), ("parallel",), comm, aliases={4: 0})
    return outs if comm is None else (outs, couts)


DT_ROWS = 512


def _tri(lower):
    r = lax.broadcasted_iota(jnp.int32, (CHUNK, CHUNK), 0)
    c = lax.broadcasted_iota(jnp.int32, (CHUNK, CHUNK), 1)
    return jnp.where((r >= c) if lower else (r <= c), 1.0, 0.0).astype(F32)


def _dot_exact(a, b):
    return lax.dot_general(a, b, _DIMS["nn"], preferred_element_type=F32, precision=lax.Precision.HIGHEST)


def _dt_fwd(p, bias_pad, alog_pad):
    t = p.shape[0]

    def body(raw_ref, b_ref, al_ref, dt_ref, acs_ref):
        z = raw_ref[...] + b_ref[...]
        dt = jnp.maximum(z, 0.0) + jnp.log(1.0 + jnp.exp(-jnp.abs(z)))
        dt_ref[...] = dt
        a = dt * (-jnp.exp(al_ref[...]))
        tri = _tri(True)
        for k in range(DT_ROWS // CHUNK):
            acs_ref[k * CHUNK:(k + 1) * CHUNK, :] = _dot_exact(tri, a[k * CHUNK:(k + 1) * CHUNK, :])

    blk = pl.BlockSpec((DT_ROWS, DT_W), lambda i: (i, 0))
    vec = pl.BlockSpec((1, DT_W), lambda i: (0, 0))
    return pl.pallas_call(
        body, name="dt_fwd", grid=(t // DT_ROWS,),
        in_specs=[pl.BlockSpec((DT_ROWS, DT_W), lambda i: (i, OFF_DT // DT_W)), vec, vec],
        out_specs=[blk, blk], out_shape=[jax.ShapeDtypeStruct((t, DT_W), F32)] * 2,
        compiler_params=_params("parallel"))(p, bias_pad, alog_pad)


def _dt_bwd(p, bias_pad, alog_pad, dt, ddt, dacs, dp_gd):
    t = p.shape[0]

    def body(raw_ref, b_ref, al_ref, dt_ref, ddt_ref, dacs_ref, dp_in, draw_ref, db_ref, dal_ref):
        del dp_in
        i = pl.program_id(0)
        acoef = -jnp.exp(al_ref[...])
        triu = _tri(False)
        das = []
        for k in range(DT_ROWS // CHUNK):
            das.append(_dot_exact(triu, dacs_ref[k * CHUNK:(k + 1) * CHUNK, :]))
        da = jnp.concatenate(das, axis=0)
        dtv = dt_ref[...]
        ddt_tot = ddt_ref[...] + da * acoef
        lane = lax.broadcasted_iota(jnp.int32, (DT_ROWS, DT_W), 1)
        draw = jnp.where(lane < N_HEADS, ddt_tot * _sigmoid(raw_ref[...] + b_ref[...]), 0.0)
        draw_ref[...] = draw.astype(BF)
        pb = jnp.sum(draw, axis=0, keepdims=True)
        pa = jnp.sum(da * dtv * acoef, axis=0, keepdims=True)

        @pl.when(i == 0)
        def _():
            db_ref[...] = pb
            dal_ref[...] = pa

        @pl.when(i > 0)
        def _():
            db_ref[...] += pb
            dal_ref[...] += pa

    blk = pl.BlockSpec((DT_ROWS, DT_W), lambda i: (i, 0))
    vec = pl.BlockSpec((1, DT_W), lambda i: (0, 0))
    return pl.pallas_call(
        body, name="dt_bwd", grid=(t // DT_ROWS,),
        in_specs=[pl.BlockSpec((DT_ROWS, DT_W), lambda i: (i, OFF_DT // DT_W)), vec, vec, blk, blk, blk,
                  pl.BlockSpec(memory_space=pl.ANY)],
        out_specs=[pl.BlockSpec((DT_ROWS, DT_W), lambda i: (i, OFF_DT // DT_W)), vec, vec],
        out_shape=[jax.ShapeDtypeStruct(dp_gd.shape, dp_gd.dtype), jax.ShapeDtypeStruct((1, DT_W), F32),
                   jax.ShapeDtypeStruct((1, DT_W), F32)],
        input_output_aliases={6: 0},
        compiler_params=_params("arbitrary"))(p, bias_pad, alog_pad, dt, ddt, dacs, dp_gd)


def _split_dot(z, onehot, terms):
    out = None
    rest = z
    for _ in range(terms):
        piece = rest.astype(BF)
        part = _dot(piece, onehot)
        out = part if out is None else out + part
        rest = rest - piece.astype(F32)
    return out


def _spread_mat():
    row = lax.broadcasted_iota(jnp.int32, (DT_W, D_INNER), 0)
    lane = lax.broadcasted_iota(jnp.int32, (DT_W, D_INNER), 1)
    return jnp.where(row == lane // HEAD_DIM, 1.0, 0.0).astype(BF)


def _gather_mat():
    row = lax.broadcasted_iota(jnp.int32, (D_INNER, DT_W), 0)
    lane = lax.broadcasted_iota(jnp.int32, (D_INNER, DT_W), 1)
    return jnp.where(lane == row // HEAD_DIM, 1.0, 0.0).astype(BF)


def _ssd_masks():
    row = lax.broadcasted_iota(jnp.int32, (CHUNK, GROUP_W), 0)
    col = lax.broadcasted_iota(jnp.int32, (CHUNK, GROUP_W), 1) % HEAD_DIM
    brow = lax.broadcasted_iota(jnp.int32, (GROUP_W, GROUP_W), 0) // HEAD_DIM
    bcol = lax.broadcasted_iota(jnp.int32, (GROUP_W, GROUP_W), 1) // HEAD_DIM
    return row >= col, row == col, brow == bcol


def _stack4(v):
    return jnp.concatenate([v, v, v, v], axis=0)


def _fold4(v):
    return v[0:CHUNK] + v[CHUNK:2 * CHUNK] + v[2 * CHUNK:3 * CHUNK] + v[3 * CHUNK:4 * CHUNK]


def _ssd_group(xc_ref, wide_ref, g, tri, eye, blockdiag):
    gs = slice(GROUP_W * g, GROUP_W * (g + 1))
    xs_g = xc_ref[:, gs]
    b_g = xc_ref[:, D_INNER + D_STATE * g:D_INNER + D_STATE * (g + 1)].astype(BF)
    c_g = xc_ref[:, D_INNER + 1024 + D_STATE * g:D_INNER + 1024 + D_STATE * (g + 1)].astype(BF)
    acs_e, dt_e = wide_ref[0:CHUNK, gs], wide_ref[CHUNK:2 * CHUNK, gs]
    atot_e = acs_e[CHUNK - 1:CHUNK, :]
    acs_j = jnp.sum(jnp.where(eye, acs_e, 0.0), axis=0, keepdims=True)
    lmat = jnp.where(tri, jnp.exp(jnp.minimum(acs_e - acs_j, 0.0)), 0.0)
    b_t = _stack4(b_g)
    m = _dot(c_g, b_t, "nt") * lmat
    x_g = xs_g * dt_e
    xbd = jnp.where(blockdiag, _stack4(x_g), 0.0).astype(BF)
    return dict(gs=gs, xs=xs_g, b=b_g, c=c_g, b_t=b_t, dt=dt_e, e=jnp.exp(acs_e), dec=jnp.exp(atot_e - acs_e),
                eat=jnp.exp(atot_e), lmat=lmat, m=m, x=x_g, xbd=xbd)


def _ssd_fwd(xconv, dt, acs, d_exp, comm=None):
    t = xconv.shape[0]
    nc = t // CHUNK

    def body(xc_ref, dt_ref, acs_ref, d_ref, y_ref, hs_ref, state, wide):
        c = pl.program_id(0)

        @pl.when(c == 0)
        def _():
            state[...] = jnp.zeros_like(state)

        hs_ref[...] = state[...]
        tri, eye, blockdiag = _ssd_masks()
        wide[...] = _split_dot(jnp.concatenate([acs_ref[...], dt_ref[...]], axis=0), _spread_mat(), 3)
        for g in range(N_GROUPS):
            q = _ssd_group(xc_ref, wide, g, tri, eye, blockdiag)
            gs = q["gs"]
            h_t = state[:, gs]
            ydiag = _dot(q["m"].astype(BF), q["xbd"])
            yoff = _dot(q["c"], h_t.astype(BF)) * q["e"]
            y_ref[:, gs] = ydiag + yoff + d_ref[:, gs] * q["xs"]
            s_t = _dot(q["b"], (q["x"] * q["dec"]).astype(BF), "tn")
            state[:, gs] = q["eat"] * h_t + s_t

    blk = lambda w: pl.BlockSpec((CHUNK, w), lambda c: (c, 0))
    outs, couts = _pcall(
        "ssd_fwd", body, (nc,),
        [blk(D_XBC), blk(DT_W), blk(DT_W), pl.BlockSpec((1, D_INNER), lambda c: (0, 0))],
        [blk(D_INNER), pl.BlockSpec((None, D_STATE, D_INNER), lambda c: (c, 0, 0))],
        [jax.ShapeDtypeStruct((t, D_INNER), F32), jax.ShapeDtypeStruct((nc, D_STATE, D_INNER), F32)],
        (xconv, dt, acs, d_exp), [pltpu.VMEM((D_STATE, D_INNER), F32), pltpu.VMEM((2 * CHUNK, D_INNER), F32)],
        ("arbitrary",), comm)
    return outs if comm is None else (outs, couts)


def _ssd_bwd(xconv, dt, acs, d_exp, hsave, dy, comm=None):
    t = xconv.shape[0]
    nc = t // CHUNK

    def body(xc_ref, dt_ref, acs_ref, d_ref, hs_ref, dy_ref, dxc_ref, ddt_ref, dacs_ref, dd_ref, dstate, wide, per_head):
        c = pl.program_id(0)

        @pl.when(c == 0)
        def _():
            dstate[...] = jnp.zeros_like(dstate)
            dd_ref[...] = jnp.zeros_like(dd_ref)

        tri, eye, blockdiag = _ssd_masks()
        acsv = acs_ref[...]
        wide[...] = _split_dot(jnp.concatenate([acsv, dt_ref[...]], axis=0), _spread_mat(), 3)
        eat_heads = jnp.exp(acsv[CHUNK - 1:CHUNK, :])

        for g in range(N_GROUPS):
            q = _ssd_group(xc_ref, wide, g, tri, eye, blockdiag)
            gs, xs_g, b_g, c_g, m = q["gs"], q["xs"], q["b"], q["c"], q["m"]
            bs = slice(D_INNER + D_STATE * g, D_INNER + D_STATE * (g + 1))
            cs = slice(D_INNER + 1024 + D_STATE * g, D_INNER + 1024 + D_STATE * (g + 1))
            h_t = hs_ref[:, gs]
            h_b = h_t.astype(BF)
            dy_g = dy_ref[:, gs]
            dy_b = dy_g.astype(BF)
            ds_t = dstate[:, gs]
            ds_b = ds_t.astype(BF)

            yoff = _dot(c_g, h_b) * q["e"]
            edy = (q["e"] * dy_g).astype(BF)
            d_c = _dot(edy, h_b, "nt")
            d_ht = _dot(c_g, edy, "tn")
            bds = _dot(b_g, ds_b)
            xd = q["x"] * q["dec"]
            d_b = _dot(xd.astype(BF), ds_b, "nt")
            dm = _dot(dy_b, q["xbd"], "nt")
            cross = _dot(m.astype(BF), dy_b, "tn")
            dx_full = q["dec"] * bds + _fold4(jnp.where(blockdiag, cross, 0.0))
            dml = (dm * q["lmat"]).astype(BF)
            d_c = d_c + _dot(dml, q["b_t"])
            d_b = d_b + _fold4(_dot(dml, c_g, "tn"))
            w = dm * m
            q_dec = xd * bds
            z = w - jnp.where(eye, jnp.sum(w, axis=0, keepdims=True), 0.0) + dy_g * yoff - q_dec
            rows = jnp.concatenate(
                [jnp.sum(q_dec, axis=0, keepdims=True), jnp.sum(ds_t * h_t, axis=0, keepdims=True),
                 jnp.zeros((6, GROUP_W), F32)], axis=0)
            per_head[:, gs] = jnp.concatenate([z, dx_full * xs_g, rows], axis=0)
            dxc_ref[:, cs] = d_c
            dxc_ref[:, bs] = d_b
            dxc_ref[:, gs] = dx_full * q["dt"] + d_ref[:, gs] * dy_g
            dd_ref[:, gs] += jnp.sum(dy_g * xs_g, axis=0, keepdims=True)
            dstate[:, gs] = q["eat"] * ds_t + d_ht

        seg = _split_dot(per_head[...], _gather_mat(), 2)
        datot = seg[2 * CHUNK:2 * CHUNK + 1] + eat_heads * seg[2 * CHUNK + 1:2 * CHUNK + 2]
        rowi = lax.broadcasted_iota(jnp.int32, (CHUNK, DT_W), 0)
        ddt_ref[...] = seg[CHUNK:2 * CHUNK]
        dacs_ref[...] = seg[0:CHUNK] + jnp.where(rowi == CHUNK - 1, datot, 0.0)

    rev = lambda w: pl.BlockSpec((CHUNK, w), lambda c: (nc - 1 - c, 0))
    vec = pl.BlockSpec((1, D_INNER), lambda c: (0, 0))
    outs, couts = _pcall(
        "ssd_bwd", body, (nc,),
        [rev(D_XBC), rev(DT_W), rev(DT_W), vec,
         pl.BlockSpec((None, D_STATE, D_INNER), lambda c: (nc - 1 - c, 0, 0)), rev(D_INNER)],
        [rev(D_XBC), rev(DT_W), rev(DT_W), vec],
        [jax.ShapeDtypeStruct((t, D_XBC), F32), jax.ShapeDtypeStruct((t, DT_W), F32),
         jax.ShapeDtypeStruct((t, DT_W), F32), jax.ShapeDtypeStruct((1, D_INNER), F32)],
        (xconv, dt, acs, d_exp, hsave, dy),
        [pltpu.VMEM((D_STATE, D_INNER), F32), pltpu.VMEM((2 * CHUNK, D_INNER), F32),
         pltpu.VMEM((2 * CHUNK + 8, D_INNER), F32)], ("arbitrary",), comm)
    return outs if comm is None else (outs, couts)


GN_CB = 1024
GN_GROUPS = GN_CB // GROUP_W


def _gnorm_fwd(y, p, w, comm=None):
    t = y.shape[0]
    zoff = OFF_Z // GN_CB

    def body(y_ref, z_ref, w_ref, o_ref):
        for g in range(GN_GROUPS):
            gs = slice(GROUP_W * g, GROUP_W * (g + 1))
            z = z_ref[:, gs].astype(F32)
            yf = y_ref[:, gs] * (z * _sigmoid(z))
            rstd = lax.rsqrt(jnp.mean(yf * yf, axis=-1, keepdims=True) + NORM_EPS)
            o_ref[:, gs] = (yf * rstd * w_ref[:, gs]).astype(BF)

    blk = pl.BlockSpec((TE, GN_CB), lambda i, j: (i, j))
    out, couts = _pcall(
        "gnorm_fwd", body, (t // TE, D_INNER // GN_CB),
        [blk, pl.BlockSpec((TE, GN_CB), lambda i, j: (i, zoff + j)), pl.BlockSpec((1, GN_CB), lambda i, j: (0, j))],
        blk, jax.ShapeDtypeStruct((t, D_INNER), BF), (y, p, w), (), ("parallel", "parallel"), comm)
    return out if comm is None else (out, couts)


def _gnorm_bwd(y, p, w, dyn, comm=None):
    t = y.shape[0]
    zoff = OFF_Z // GN_CB

    def body(y_ref, z_ref, w_ref, dn_ref, dy_ref, dz_ref, dw_ref):
        i = pl.program_id(1)
        for g in range(GN_GROUPS):
            gs = slice(GROUP_W * g, GROUP_W * (g + 1))
            z = z_ref[:, gs].astype(F32)
            yv = y_ref[:, gs]
            s = _sigmoid(z)
            sil = z * s
            yf = yv * sil
            rstd = lax.rsqrt(jnp.mean(yf * yf, axis=-1, keepdims=True) + NORM_EPS)
            xhat = yf * rstd
            dn = dn_ref[:, gs]
            wd = dn * w_ref[:, gs]
            proj = jnp.mean(wd * xhat, axis=-1, keepdims=True)
            dyf = rstd * (wd - xhat * proj)
            dy_ref[:, gs] = dyf * sil
            dz_ref[:, gs] = (dyf * yv * (s * (1.0 + z * (1.0 - s)))).astype(BF)
            part = jnp.sum(dn * xhat, axis=0, keepdims=True)

            @pl.when(i == 0)
            def _():
                dw_ref[:, gs] = part

            @pl.when(i > 0)
            def _():
                dw_ref[:, gs] += part

    blk = pl.BlockSpec((TE, GN_CB), lambda j, i: (i, j))
    vec = pl.BlockSpec((1, GN_CB), lambda j, i: (0, j))
    outs, couts = _pcall(
        "gnorm_bwd", body, (D_INNER // GN_CB, t // TE),
        [blk, pl.BlockSpec((TE, GN_CB), lambda j, i: (i, zoff + j)), vec, blk],
        [blk, pl.BlockSpec((TE, GN_CB), lambda j, i: (i, zoff + j)), vec],
        [jax.ShapeDtypeStruct((t, D_INNER), F32), jax.ShapeDtypeStruct((t, N_MAIN), BF),
         jax.ShapeDtypeStruct((1, D_INNER), F32)],
        (y, p, w, dyn), (), ("parallel", "arbitrary"), comm)
    return outs if comm is None else (outs, couts)


MERGE_CB = 512


def _merge_fwd(p, ya, yb):
    t = ya.shape[0]

    def body(ga_ref, gb_ref, ya_ref, yb_ref, o_ref):
        o_ref[...] = (_sigmoid(ga_ref[...]) * ya_ref[...] + _sigmoid(gb_ref[...]) * yb_ref[...]).astype(BF)

    blk = pl.BlockSpec((TE, MERGE_CB), lambda i, j: (i, j))
    return pl.pallas_call(
        body, name="merge_fwd", grid=(t // TE, D_MODEL // MERGE_CB),
        in_specs=[pl.BlockSpec((TE, MERGE_CB), lambda i, j: (i, 2 * j)),
                  pl.BlockSpec((TE, MERGE_CB), lambda i, j: (i, 2 * j + 1)), blk, blk],
        out_specs=blk, out_shape=jax.ShapeDtypeStruct((t, D_MODEL), BF),
        compiler_params=_params("parallel", "parallel"))(p, p, ya, yb)


def _merge_bwd(p, ya, yb, dm):
    t = ya.shape[0]

    def body(ga_ref, gb_ref, ya_ref, yb_ref, dm_ref, dg_ref, dya_ref, dyb_ref):
        d = dm_ref[...]
        sa = _sigmoid(ga_ref[...])
        sb = _sigmoid(gb_ref[...])
        dg_ref[:, 0:MERGE_CB] = (d * ya_ref[...] * sa * (1.0 - sa)).astype(BF)
        dg_ref[:, MERGE_CB:2 * MERGE_CB] = (d * yb_ref[...] * sb * (1.0 - sb)).astype(BF)
        dya_ref[...] = (d * sa).astype(BF)
        dyb_ref[...] = (d * sb).astype(BF)

    blk = pl.BlockSpec((TE, MERGE_CB), lambda i, j: (i, j))
    return pl.pallas_call(
        body, name="merge_bwd", grid=(t // TE, D_MODEL // MERGE_CB),
        in_specs=[pl.BlockSpec((TE, MERGE_CB), lambda i, j: (i, 2 * j)),
                  pl.BlockSpec((TE, MERGE_CB), lambda i, j: (i, 2 * j + 1)), blk, blk, blk],
        out_specs=[pl.BlockSpec((TE, 2 * MERGE_CB), lambda i, j: (i, j)), blk, blk],
        out_shape=[jax.ShapeDtypeStruct((t, N_GD), BF)] + [jax.ShapeDtypeStruct((t, D_MODEL), BF)] * 2,
        compiler_params=_params("parallel", "parallel"))(p, p, ya, yb, dm)


def _adamw(name, parts, w, m, v, comm=None):
    r, c = w.shape
    tr = _row_tile(r)
    tc = ADAM_COL_TILE if (tr == r and r > 512 and c % ADAM_COL_TILE == 0) else c
    n_parts = parts.shape[0]
    bc1 = 1.0 - ADAM_B1 ** ADAM_STEP
    bc2 = 1.0 - ADAM_B2 ** ADAM_STEP

    def body(p_ref, w_ref, m_ref, v_ref, g_ref, d_ref, nm_ref, nv_ref):
        g = p_ref[0].astype(F32)
        for k in range(1, n_parts):
            g = g + p_ref[k].astype(F32)
        nm = ADAM_B1 * m_ref[...] + (1.0 - ADAM_B1) * g
        nv = ADAM_B2 * v_ref[...] + (1.0 - ADAM_B2) * (g * g)
        g_ref[...] = g
        nm_ref[...] = nm
        nv_ref[...] = nv
        d_ref[...] = -ADAM_LR * ((nm / bc1) / (jnp.sqrt(nv / bc2) + ADAM_EPS) + ADAM_WD * w_ref[...])

    blk = pl.BlockSpec((tr, tc), lambda i, j: (i, j))
    outs, couts = _pcall(
        name, body, (r // tr, c // tc),
        [pl.BlockSpec((n_parts, tr, tc), lambda i, j: (0, i, j)), blk, blk, blk], [blk] * 4,
        [jax.ShapeDtypeStruct((r, c), F32)] * 4, (parts, w, m, v), (), ("parallel", "parallel"), comm)
    return outs if comm is None else (outs, couts)


def _pad_lanes(v, width):
    return jnp.pad(v, ((0, 0), (0, width - v.shape[1])))


def _reduce_start(slots, host):
    outs, sib = host(_pair_comm([a for _, a in slots]))
    sums = [(n, _add_pairs("pairsum_" + n, a, b)) for (n, a), b in zip(slots, sib)]
    return outs, sums


def _train_step(x, target, shard, rep):
    gdt = BF
    recv = {}
    h1, (got,) = _rms_fwd("rms1_fwd", x, rep["ffn1_norm"], comm=_gather_comm([shard["ffn1_w_in"]], [True]))
    w1_in = got.reshape(2 * D_FF, D_MODEL)
    gu1, got = _mm_nt("ffn1_in", h1, w1_in, tn=FF_HALF, out_dtype=BF, comm=_gather_comm(
        [shard["ffn1_w_out"], shard["w_in"], shard["short_conv_w"], shard["ssm_conv_w"]]))
    w1_out = got[0].reshape(D_FF, D_MODEL)
    w_in_t = got[1].reshape(N_IN, D_MODEL)
    short_conv_w = got[2].transpose(1, 0, 2).reshape(3, D_MODEL)
    ssm_conv_w = got[3].transpose(1, 0, 2).reshape(4, D_XBC)
    act1 = _swiglu_fwd("swiglu1_fwd", gu1)
    x1 = _mm_nn("ffn1_out", act1, w1_out, res=x, alpha=0.5)
    ga0 = N_MAIN + N_HEADS
    gb0 = ga0 + D_MODEL
    half = D_MODEL // 2
    w_gd = jnp.concatenate(
        [w_in_t[ga0:ga0 + half], w_in_t[gb0:gb0 + half], w_in_t[ga0 + half:gb0], w_in_t[gb0 + half:],
         w_in_t[N_MAIN:N_MAIN + N_HEADS], jnp.zeros((DT_W - N_HEADS, D_MODEL), BF)], axis=0)
    w_mix_perm = w_in_t[0:3 * D_MODEL].reshape(3, 4, CONV_CB, D_MODEL).transpose(1, 0, 2, 3).reshape(3 * D_MODEL, D_MODEL)

    h2 = _rms_fwd("rms2_fwd", x1, rep["mix_norm"])
    p, got = _mm_nt("proj_main", h2, w_in_t, n=N_MAIN, tn=1024, out_dtype=BF, comm=_gather_comm(
        [shard["short_w_out"], shard["ssm_w_out"], shard["w_out"]]))
    p_gd = _mm_nt("proj_gd", h2, w_gd)
    short_w_out = got[0].reshape(D_MODEL, D_MODEL)
    ssm_w_out = got[1].reshape(D_INNER, D_MODEL)
    w_out = got[2].reshape(D_MODEL, D_MODEL)
    ya_in = _mix_a_fwd(p, short_conv_w)
    y_a = _mm_nn("short_out", ya_in, short_w_out)
    xconv, (got,) = _ssm_conv_fwd(p, ssm_conv_w, rep["ssm_conv_b"], comm=_gather_comm([shard["ffn2_w_out"]]))
    w2_out = got.reshape(D_FF, D_MODEL)
    dt, acs = _dt_fwd(p_gd, rep["dt_bias_pad"], rep["a_log_pad"])
    (y_ssm, hsave), (got,) = _ssd_fwd(xconv, dt, acs, rep["d_exp"], comm=_gather_comm([shard["ffn2_w_in"]], [True]))
    w2_in = got.reshape(2 * D_FF, D_MODEL)
    yn = _gnorm_fwd(y_ssm, p, rep["ssm_norm"])
    y_b = _mm_nn("ssm_out", yn, ssm_w_out, tk=1024)
    merged = _merge_fwd(p_gd, y_a, y_b)
    x2 = _mm_nn("mix_out", merged, w_out, res=x1)

    h3 = _rms_fwd("rms3_fwd", x2, rep["ffn2_norm"])
    gu2 = _mm_nt("ffn2_in", h3, w2_in, tn=FF_HALF, out_dtype=BF)
    act2 = _swiglu_fwd("swiglu2_fwd", gu2)
    x3 = _mm_nn("ffn2_out", act2, w2_out, res=x2, alpha=0.5)

    loss, dx3, dx3h, g_final = _final_loss(x3, rep["final_norm"], target)

    small = {"final_norm": g_final}
    dact2 = _mm_nt("ffn2_out_bwd_act", dx3h, w2_out, out_dtype=BF)
    g_w2_out = _mm_tn("ffn2_out_bwd_w", act2, dx3h, gdt, tm=FF_HALF)
    dgu2 = _swiglu_bwd("swiglu2_bwd", gu2, dact2)
    g_w2_in = _mm_tn("ffn2_in_bwd_w", dgu2, h3, gdt, tm=FF_HALF)
    dh3 = _mm_nn("ffn2_in_bwd_h", dgu2, w2_in, tk=FF_HALF)
    dx2, dx2b, small["ffn2_norm"] = _rms_bwd("rms3_bwd", x2, rep["ffn2_norm"], dh3, dx3, 1.0)

    dmerged = _mm_nt("mix_out_bwd_x", dx2b, w_out)
    g_w_out = _mm_tn("mix_out_bwd_w", merged, dx2b, gdt)
    dp_gd, dya, dyb = _merge_bwd(p_gd, y_a, y_b, dmerged)

    dya_in = _mm_nt("short_out_bwd_x", dya, short_w_out)
    g_short_w_out = _mm_tn("short_out_bwd_w", ya_in, dya, gdt)

    dyn = _mm_nt("ssm_out_bwd_x", dyb, ssm_w_out)
    g_ssm_w_out = _mm_tn("ssm_out_bwd_w", yn, dyb, gdt)
    late = [("ffn2_w_out", g_w2_out.reshape(N_DEV, FF_SHARD // 2, D_MODEL)),
            ("ffn2_w_in", g_w2_in.reshape(N_DEV, FF_SHARD, D_MODEL)),
            ("w_out", g_w_out.reshape(N_DEV, -1, D_MODEL)), ("short_w_out", g_short_w_out.reshape(N_DEV, -1, D_MODEL)),
            ("ssm_w_out", g_ssm_w_out.reshape(N_DEV, -1, D_MODEL))]
    (dy_ssm, dp, small["ssm_norm"]), sums = _reduce_start(
        late, lambda comm: _gnorm_bwd(y_ssm, p, rep["ssm_norm"], dyn, comm=comm))
    dp, g_short_conv = _mix_a_bwd(p, short_conv_w, dya_in, dp)
    first = [(n, a) for n, a in sums if n.startswith("ffn2")]
    second = [(n, a) for n, a in sums if not n.startswith("ffn2")]
    (dxconv, ddt, dacs, dd_lane), got = _ssd_bwd(
        xconv, dt, acs, rep["d_exp"], hsave, dy_ssm,
        comm=_chip_comm([a for _, a in first], [n == "ffn2_w_in" for n, _ in first]))
    recv.update({n: a for (n, _), a in zip(first, got)})
    small["ssm_D"] = dd_lane.reshape(N_HEADS, HEAD_DIM).sum(axis=1)[None, :]
    (dp, g_ssm_conv, small["ssm_conv_b"]), got = _ssm_conv_bwd(
        p, ssm_conv_w, rep["ssm_conv_b"], dxconv, dp, comm=_chip_comm([a for _, a in second]))
    recv.update({n: a for (n, _), a in zip(second, got)})
    dp_gd, dbias, dalog = _dt_bwd(p_gd, rep["dt_bias_pad"], rep["a_log_pad"], dt, ddt, dacs, dp_gd)
    small["ssm_dt_bias"] = dbias[:, :N_HEADS]
    small["ssm_A_log"] = dalog[:, :N_HEADS]

    g_main = _mm_tn("proj_main_bwd_w", dp, h2, gdt, tm=1024)
    g_gd = _mm_tn("proj_gd_bwd_w", dp_gd, h2, gdt)
    g_mix = g_main[0:3 * D_MODEL].reshape(4, 3, CONV_CB, D_MODEL).transpose(1, 0, 2, 3).reshape(3 * D_MODEL, D_MODEL)
    g_in_t = jnp.concatenate(
        [g_mix, g_main[3 * D_MODEL:], g_gd[2 * D_MODEL:2 * D_MODEL + N_HEADS],
         g_gd[0:half], g_gd[2 * half:3 * half], g_gd[half:2 * half], g_gd[3 * half:4 * half]], axis=0).reshape(
        N_DEV, IN_SHARD, D_MODEL)
    dh2, w_sums = _reduce_start(
        [("w_in", g_in_t)], lambda comm: _mm_nn("proj_mix_bwd_x", dp, w_mix_perm, tk=1024, kk=3 * D_MODEL, comm=comm))
    w_sum = w_sums[0][1]

    def w_piece(i):
        return _chip_comm([w_sum], rows=[W_GRAD_ROW_CUTS[i]])

    dh2, got0 = _mm_nn("proj_rest_bwd_x", dp, w_in_t, tk=1024, kk=N_MAIN - 3 * D_MODEL, a_off=3, b_off=3, res=dh2,
                       comm=w_piece(0))
    dh2, got1 = _mm_nn("proj_gd_bwd_x", dp_gd, w_gd, res=dh2, comm=w_piece(1))
    (dx1, dx1h, small["mix_norm"]), got2 = _rms_bwd("rms2_bwd", x1, rep["mix_norm"], dh2, dx2, 0.5, comm=w_piece(2))
    g_w1_out, got3 = _mm_tn("ffn1_out_bwd_w", act1, dx1h, gdt, tm=FF_HALF, comm=w_piece(3))
    rest = [("ffn1_w_out", g_w1_out.reshape(N_DEV, FF_SHARD // 2, D_MODEL)),
            ("short_conv_w", g_short_conv.reshape(3, N_DEV, -1).transpose(1, 0, 2)),
            ("ssm_conv_w", g_ssm_conv.reshape(4, N_DEV, -1).transpose(1, 0, 2))]
    dact1, got = _mm_nt("ffn1_out_bwd_act", dx1h, w1_out, out_dtype=BF,
                        comm=_join_comm(w_piece(4), _pair_comm([a for _, a in rest])))
    got4, sib = got[0], got[1:]
    rest_sums = [(n, _add_pairs("pairsum_" + n, a, b)) for (n, a), b in zip(rest, sib)]
    w1_out_sum = rest_sums[0][1]
    half_rows = FF_SHARD // 4
    dgu1, got = _swiglu_bwd("swiglu1_bwd", gu1, dact1, comm=_chip_comm(
        [a for _, a in rest_sums], rows=[(0, half_rows), None, None]))
    recv_w1_out_a = got[0]
    recv.update({n: a for (n, _), a in zip(rest_sums[1:], got[1:])})

    def part(tag, width, off, comm=None):
        out = _mm_tn("ffn1_in_bwd_w_" + tag, dgu1, h1, gdt, tm=FF_HALF, n=width, col_off=off, comm=comm)
        g, couts = (out, None) if comm is None else out
        return g.reshape(N_DEV, FF_SHARD, width), couts

    g_a, (got5,) = part("a", 384, 0, w_piece(5))
    g_b, (got6, sib) = part("b", 384, 1, _join_comm(w_piece(6), _pair_comm([g_a])))
    recv["w_in"] = jnp.concatenate([got0[0], got1[0], got2[0], got3[0], got4, got5, got6], axis=1)
    sum_a = _add_pairs("pairsum_ffn1_w_in_a", g_a, sib)
    g_c, (recv_a, sib_b) = part("c", 256, 3, _join_comm(_chip_comm([sum_a], [True]), _pair_comm([g_b])))
    sum_b = _add_pairs("pairsum_ffn1_w_in_b", g_b, sib_b)
    dh1, (recv_b, recv_w1_out_b, sib_c) = _mm_nn(
        "ffn1_in_bwd_h", dgu1, w1_in, tk=FF_HALF,
        comm=_join_comm(_chip_comm([sum_b, w1_out_sum], [True, False], rows=[None, (half_rows, 2 * half_rows)]),
                        _pair_comm([g_c])))
    recv["ffn1_w_out"] = jnp.concatenate([recv_w1_out_a, recv_w1_out_b], axis=1)
    sum_c = _add_pairs("pairsum_ffn1_w_in_c", g_c, sib_c)
    (dx0, _, small["ffn1_norm"]), (recv_c,) = _rms_bwd("rms1_bwd", x, rep["ffn1_norm"], dh1, dx1, 1.0,
                                                        comm=_chip_comm([sum_c], [True]))
    recv["ffn1_w_in"] = jnp.concatenate([recv_a, recv_b, recv_c], axis=2)
    return dx0, recv, _pack_small(small, loss[:, 0:1])


_SMALL = [("ffn1_norm", 1024), ("mix_norm", 1024), ("ssm_conv_b", 4096), ("ssm_dt_bias", 32), ("ssm_A_log", 32),
          ("ssm_D", 32), ("ssm_norm", 2048), ("ffn2_norm", 1024), ("final_norm", 1024)]
SMALL_W = 10368


def _pack_small(d, loss=None):
    parts = [d[n].reshape(1, -1).astype(F32) for n, _ in _SMALL]
    used = sum(sz for _, sz in _SMALL)
    tail = jnp.zeros((1, SMALL_W - used), F32)
    if loss is not None:
        tail = tail.at[:, 0:1].set(loss)
    return jnp.concatenate(parts + [tail], axis=1)


def _adamw_small(parts, w, m, v):
    n_par = len(_SMALL)
    bc1 = 1.0 - ADAM_B1 ** ADAM_STEP
    bc2 = 1.0 - ADAM_B2 ** ADAM_STEP
    used = sum(sz for _, sz in _SMALL)

    def body(*refs):
        p_ref = refs[0]
        ins = refs[1:1 + 3 * n_par]
        outs = refs[1 + 3 * n_par:]
        g_all = p_ref[0]
        for k in range(1, N_DEV):
            g_all = g_all + p_ref[k]
        off = 0
        for i, (_, sz) in enumerate(_SMALL):
            g = g_all[:, off:off + sz]
            w_ref, m_ref, v_ref = ins[3 * i:3 * i + 3]
            nm = ADAM_B1 * m_ref[...] + (1.0 - ADAM_B1) * g
            nv = ADAM_B2 * v_ref[...] + (1.0 - ADAM_B2) * (g * g)
            outs[4 * i][...] = g
            outs[4 * i + 1][...] = -ADAM_LR * ((nm / bc1) / (jnp.sqrt(nv / bc2) + ADAM_EPS) + ADAM_WD * w_ref[...])
            outs[4 * i + 2][...] = nm
            outs[4 * i + 3][...] = nv
            off += sz
        outs[4 * n_par][...] = g_all[:, used:SMALL_W]

    args = [parts]
    out_shape = []
    for name, sz in _SMALL:
        args += [w[name], m[name], v[name]]
        out_shape += [jax.ShapeDtypeStruct((1, sz), F32)] * 4
    out_shape.append(jax.ShapeDtypeStruct((1, SMALL_W - used), F32))
    res = pl.pallas_call(body, name="adamw_small", out_shape=out_shape,
                         compiler_params=pltpu.CompilerParams(vmem_limit_bytes=VMEM_LIMIT_V7X))(*args)
    return {name: tuple(res[4 * i:4 * i + 4]) for i, (name, _) in enumerate(_SMALL)}, res[-1]


_SHARDED = ["ffn1_w_in", "ffn1_w_out", "w_in", "short_conv_w", "short_w_out", "ssm_conv_w", "ssm_w_out", "w_out",
            "ffn2_w_in", "ffn2_w_out"]
_TRANSPOSED = ("ffn1_w_in", "w_in", "ffn2_w_in")
_ORDER = ["ffn1_norm", "ffn1_w_in", "ffn1_w_out", "mix_norm", "w_in", "short_conv_w", "short_w_out", "ssm_conv_w",
          "ssm_conv_b", "ssm_dt_bias", "ssm_A_log", "ssm_D", "ssm_norm", "ssm_w_out", "w_out", "ffn2_norm",
          "ffn2_w_in", "ffn2_w_out", "final_norm"]


def kernel(x, ffn1_norm, ffn1_w_in, ffn1_w_out, mix_norm, w_in, short_conv_w, short_w_out, ssm_conv_w, ssm_conv_b, ssm_dt_bias, ssm_A_log, ssm_D, ssm_norm, ssm_w_out, w_out, ffn2_norm, ffn2_w_in, ffn2_w_out, final_norm, loss_target, m_ffn1_norm, m_ffn1_w_in, m_ffn1_w_out, m_mix_norm, m_w_in, m_short_conv_w, m_short_w_out, m_ssm_conv_w, m_ssm_conv_b, m_ssm_dt_bias, m_ssm_A_log, m_ssm_D, m_ssm_norm, m_ssm_w_out, m_w_out, m_ffn2_norm, m_ffn2_w_in, m_ffn2_w_out, m_final_norm, v_ffn1_norm, v_ffn1_w_in, v_ffn1_w_out, v_mix_norm, v_w_in, v_short_conv_w, v_short_w_out, v_ssm_conv_w, v_ssm_conv_b, v_ssm_dt_bias, v_ssm_A_log, v_ssm_D, v_ssm_norm, v_ssm_w_out, v_w_out, v_ffn2_norm, v_ffn2_w_in, v_ffn2_w_out, v_final_norm):
    w = dict(ffn1_norm=ffn1_norm, ffn1_w_in=ffn1_w_in, ffn1_w_out=ffn1_w_out, mix_norm=mix_norm, w_in=w_in,
             short_conv_w=short_conv_w, short_w_out=short_w_out, ssm_conv_w=ssm_conv_w, ssm_conv_b=ssm_conv_b,
             ssm_dt_bias=ssm_dt_bias, ssm_A_log=ssm_A_log, ssm_D=ssm_D, ssm_norm=ssm_norm, ssm_w_out=ssm_w_out,
             w_out=w_out, ffn2_norm=ffn2_norm, ffn2_w_in=ffn2_w_in, ffn2_w_out=ffn2_w_out, final_norm=final_norm)
    m = dict(ffn1_norm=m_ffn1_norm, ffn1_w_in=m_ffn1_w_in, ffn1_w_out=m_ffn1_w_out, mix_norm=m_mix_norm, w_in=m_w_in,
             short_conv_w=m_short_conv_w, short_w_out=m_short_w_out, ssm_conv_w=m_ssm_conv_w,
             ssm_conv_b=m_ssm_conv_b, ssm_dt_bias=m_ssm_dt_bias, ssm_A_log=m_ssm_A_log, ssm_D=m_ssm_D,
             ssm_norm=m_ssm_norm, ssm_w_out=m_ssm_w_out, w_out=m_w_out, ffn2_norm=m_ffn2_norm,
             ffn2_w_in=m_ffn2_w_in, ffn2_w_out=m_ffn2_w_out, final_norm=m_final_norm)
    v = dict(ffn1_norm=v_ffn1_norm, ffn1_w_in=v_ffn1_w_in, ffn1_w_out=v_ffn1_w_out, mix_norm=v_mix_norm, w_in=v_w_in,
             short_conv_w=v_short_conv_w, short_w_out=v_short_w_out, ssm_conv_w=v_ssm_conv_w,
             ssm_conv_b=v_ssm_conv_b, ssm_dt_bias=v_ssm_dt_bias, ssm_A_log=v_ssm_A_log, ssm_D=v_ssm_D,
             ssm_norm=v_ssm_norm, ssm_w_out=v_ssm_w_out, w_out=v_w_out, ffn2_norm=v_ffn2_norm,
             ffn2_w_in=v_ffn2_w_in, ffn2_w_out=v_ffn2_w_out, final_norm=v_final_norm)
    shapes = {n: w[n].shape for n in _ORDER}

    def local(d, n):
        return d[n][0].T if n in _TRANSPOSED else d[n][0]

    shard = {n: local(w, n) for n in _SHARDED}

    wire = {n: (shard[n] if n in ("short_conv_w", "ssm_conv_w") else shard[n].astype(BF)) for n in _SHARDED}
    rep = {
        "ffn1_norm": ffn1_norm, "mix_norm": mix_norm, "ffn2_norm": ffn2_norm, "ssm_norm": ssm_norm,
        "ssm_conv_b": ssm_conv_b, "final_norm": final_norm.reshape(1, D_MODEL),
        "dt_bias_pad": _pad_lanes(ssm_dt_bias, DT_W), "a_log_pad": _pad_lanes(ssm_A_log, DT_W),
        "d_exp": jnp.repeat(ssm_D, HEAD_DIM, axis=1),
    }
    grad_x, parts, packed = _train_step(x[0], loss_target[0], wire, rep)

    out_g, out_d, out_m, out_v = {}, {}, {}, {}
    for n in _SHARDED:
        if n == "ssm_w_out":
            res, (small_parts,) = _adamw("adamw_" + n, parts[n], shard[n], local(m, n), local(v, n),
                                         comm=_gather_comm([packed]))
        else:
            res = _adamw("adamw_" + n, parts[n], shard[n], local(m, n), local(v, n))
        out_g[n], out_d[n], out_m[n], out_v[n] = [(r.T if n in _TRANSPOSED else r).reshape(shapes[n]) for r in res]
    row = lambda d: {n: d[n].reshape(1, -1) for n, _ in _SMALL}
    sres, loss_row = _adamw_small(small_parts, row(w), row(m), row(v))
    for n, _ in _SMALL:
        out_g[n], out_d[n], out_m[n], out_v[n] = [r.reshape(shapes[n]) for r in sres[n]]
    loss = loss_row[0, 0]
    return (loss, grad_x[None], *[out_g[n] for n in _ORDER], *[out_d[n] for n in _ORDER],
            *[out_m[n] for n in _ORDER], *[out_v[n] for n in _ORDER])
```
